```python
import math
import jax, jax.numpy as jnp
from jax import lax
import numpy as np

D_MODEL = 1024
BATCH = 8
SEQ = 4096
DEPTH = 2

CHUNK = 64
Q_BLOCK = 128
N_HEADS = 8
QK_NOPE_DIM = 128
QK_ROPE_DIM = 64
V_HEAD_DIM = 128
Q_LORA_RANK = 512
KV_LORA_RANK = 256
ROPE_THETA = 10000.0
MAX_POS_OFFSET = 16384
CONV_WIDTH = 3
D_FF = ((8 * D_MODEL // 3 + 255) // 256) * 256
RMS_EPS = 1e-6
N_MLA_LAYERS = (DEPTH + 1) // 2
N_CONV_LAYERS = DEPTH // 2
MLA_IN_DIM = Q_LORA_RANK + KV_LORA_RANK + QK_ROPE_DIM

kernel_name = "hybrid_mla_shortconv_swiglu_trunk"


def rms_norm(x, g):
    xf = x.astype(jnp.float32)
    y = xf * lax.rsqrt(jnp.mean(xf * xf, axis=-1, keepdims=True) + RMS_EPS)
    return (y * g.astype(jnp.float32)).astype(x.dtype)


def rope_tables(positions):
    inv_freq = 1.0 / (ROPE_THETA ** (jnp.arange(0, QK_ROPE_DIM, 2, dtype=jnp.float32) / QK_ROPE_DIM))
    ang = positions.astype(jnp.float32)[..., None] * inv_freq
    return jnp.cos(ang), jnp.sin(ang)


def apply_rope(x, cos, sin):
    xf = x.astype(jnp.float32)
    x1, x2 = jnp.split(xf, 2, axis=-1)
    out = jnp.concatenate([x1 * cos - x2 * sin, x1 * sin + x2 * cos], axis=-1)
    return out.astype(x.dtype)


def chunk_causal_mla_attention(q_n, q_r, k_n, k_r, v):
    bsz, seq, h, _ = q_n.shape
    nb = seq // Q_BLOCK
    scale = 1.0 / math.sqrt(QK_NOPE_DIM + QK_ROPE_DIM)
    qn_b = q_n.reshape(bsz, nb, Q_BLOCK, h, QK_NOPE_DIM).transpose(1, 0, 2, 3, 4)
    qr_b = q_r.reshape(bsz, nb, Q_BLOCK, h, QK_ROPE_DIM).transpose(1, 0, 2, 3, 4)
    k_chunk = jnp.arange(seq) // CHUNK

    def one_block(args):
        qn, qr, blk = args
        q_chunk = (blk * Q_BLOCK + jnp.arange(Q_BLOCK)) // CHUNK
        s = (jnp.einsum('bqhd,bkhd->bhqk', qn, k_n).astype(jnp.float32)
             + jnp.einsum('bqhd,bkd->bhqk', qr, k_r).astype(jnp.float32)) * scale
        mask = k_chunk[None, :] <= q_chunk[:, None]
        s = jnp.where(mask[None, None], s, jnp.float32(-1e30))
        p = jax.nn.softmax(s, axis=-1).astype(v.dtype)
        return jnp.einsum('bhqk,bkhd->bqhd', p, v)

    out = lax.map(one_block, (qn_b, qr_b, jnp.arange(nb)))
    return out.transpose(1, 0, 2, 3, 4).reshape(bsz, seq, h, V_HEAD_DIM)


def mla_mixer(h, positions, w_in, g_cq, g_ckv, w_uq, w_ukv, w_o):
    bsz, seq, _ = h.shape
    proj = h @ w_in
    c_q = proj[..., :Q_LORA_RANK]
    c_kv = proj[..., Q_LORA_RANK:Q_LORA_RANK + KV_LORA_RANK]
    k_r = proj[..., Q_LORA_RANK + KV_LORA_RANK:]
    c_q = rms_norm(c_q, g_cq)
    c_kv = rms_norm(c_kv, g_ckv)
    q = (c_q @ w_uq).reshape(bsz, seq, N_HEADS, QK_NOPE_DIM + QK_ROPE_DIM)
    q_n, q_r = q[..., :QK_NOPE_DIM], q[..., QK_NOPE_DIM:]
    kv = (c_kv @ w_ukv).reshape(bsz, seq, N_HEADS, QK_NOPE_DIM + V_HEAD_DIM)
    k_n, v = kv[..., :QK_NOPE_DIM], kv[..., QK_NOPE_DIM:]
    cos, sin = rope_tables(positions)
    q_r = apply_rope(q_r, cos[:, :, None, :], sin[:, :, None, :])
    k_r = apply_rope(k_r, cos, sin)
    attn = chunk_causal_mla_attention(q_n, q_r, k_n, k_r, v)
    return attn.reshape(bsz, seq, N_HEADS * V_HEAD_DIM) @ w_o


def short_conv_mixer(h, w_in, conv_w, w_out):
    bcx = h @ w_in
    b_gate = bcx[..., :D_MODEL]
    c_gate = bcx[..., D_MODEL:2 * D_MODEL]
    xp = bcx[..., 2 * D_MODEL:]
    u = c_gate * xp
    u_conv = lax.conv_general_dilated(
        u, conv_w[:, None, :].astype(u.dtype), window_strides=(1,),
        padding=[(CONV_WIDTH - 1, 0)], dimension_numbers=('NWC', 'WIO', 'NWC'),
        feature_group_count=D_MODEL)
    return (b_gate * u_conv) @ w_out


def swiglu(h, w_gate, w_up, w_down):
    return (jax.nn.silu(h @ w_gate) * (h @ w_up)) @ w_down


def _fwd_setup_inputs(seed: int = 0) -> dict:
    key = jax.random.key(seed)
    ks = jax.random.split(key, 24)

    def w(k, shape, fan_in):
        return jax.random.normal(k, shape, jnp.float32) * (fan_in ** -0.5)

    def gain(k, shape):
        return 1.0 + 0.05 * jax.random.normal(k, shape, jnp.float32)

    x = jax.random.normal(ks[0], (BATCH, SEQ, D_MODEL), jnp.float32)
    offset = jax.random.randint(ks[1], (BATCH, 1), 0, MAX_POS_OFFSET, dtype=jnp.int32)
    positions = (offset + jnp.arange(SEQ, dtype=jnp.int32)[None, :]).astype(jnp.int32)
    L_a, L_b = N_MLA_LAYERS, N_CONV_LAYERS
    return {
        "x": x,
        "positions": positions,
        "mla_norm": gain(ks[2], (L_a, D_MODEL)),
        "mla_w_in": w(ks[3], (L_a, D_MODEL, MLA_IN_DIM), D_MODEL),
        "mla_g_cq": gain(ks[4], (L_a, Q_LORA_RANK)),
        "mla_g_ckv": gain(ks[5], (L_a, KV_LORA_RANK)),
        "mla_w_uq": w(ks[6], (L_a, Q_LORA_RANK, N_HEADS * (QK_NOPE_DIM + QK_ROPE_DIM)), Q_LORA_RANK),
        "mla_w_ukv": w(ks[7], (L_a, KV_LORA_RANK, N_HEADS * (QK_NOPE_DIM + V_HEAD_DIM)), KV_LORA_RANK),
        "mla_w_o": w(ks[8], (L_a, N_HEADS * V_HEAD_DIM, D_MODEL), N_HEADS * V_HEAD_DIM),
        "conv_norm": gain(ks[9], (L_b, D_MODEL)),
        "conv_w_in": w(ks[10], (L_b, D_MODEL, 3 * D_MODEL), D_MODEL),
        "conv_w": w(ks[11], (L_b, CONV_WIDTH, D_MODEL), CONV_WIDTH),
        "conv_w_out": w(ks[12], (L_b, D_MODEL, D_MODEL), D_MODEL),
        "ffn_norm": gain(ks[13], (DEPTH, D_MODEL)),
        "ffn_w_gate": w(ks[14], (DEPTH, D_MODEL, D_FF), D_MODEL),
        "ffn_w_up": w(ks[15], (DEPTH, D_MODEL, D_FF), D_MODEL),
        "ffn_w_down": w(ks[16], (DEPTH, D_FF, D_MODEL), D_FF),
        "final_norm": gain(ks[17], (D_MODEL,)),
    }


def _fwd_reference(x, positions, mla_norm, mla_w_in, mla_g_cq, mla_g_ckv, mla_w_uq,
              mla_w_ukv, mla_w_o, conv_norm, conv_w_in, conv_w, conv_w_out,
              ffn_norm, ffn_w_gate, ffn_w_up, ffn_w_down, final_norm):
    h = x
    for i in range(DEPTH):
        j = i // 2
        if i % 2 == 0:
            h = h + mla_mixer(rms_norm(h, mla_norm[j]), positions, mla_w_in[j],
                              mla_g_cq[j], mla_g_ckv[j], mla_w_uq[j], mla_w_ukv[j], mla_w_o[j])
        else:
            h = h + short_conv_mixer(rms_norm(h, conv_norm[j]), conv_w_in[j],
                                     conv_w[j], conv_w_out[j])
        h = h + swiglu(rms_norm(h, ffn_norm[i]), ffn_w_gate[i], ffn_w_up[i], ffn_w_down[i])
    return rms_norm(h, final_norm)


import jax as _jax
import jax.numpy as _jnp

TWIN_FORMAT = 'train_step'
FWD_PARAMS = ['x', 'positions', 'mla_norm', 'mla_w_in', 'mla_g_cq', 'mla_g_ckv', 'mla_w_uq', 'mla_w_ukv', 'mla_w_o', 'conv_norm', 'conv_w_in', 'conv_w', 'conv_w_out', 'ffn_norm', 'ffn_w_gate', 'ffn_w_up', 'ffn_w_down', 'final_norm']
TWIN_WEIGHTS = ['mla_norm', 'mla_w_in', 'mla_g_cq', 'mla_g_ckv', 'mla_w_uq', 'mla_w_ukv', 'mla_w_o', 'conv_norm', 'conv_w_in', 'conv_w', 'conv_w_out', 'ffn_norm', 'ffn_w_gate', 'ffn_w_up', 'ffn_w_down', 'final_norm']
TWIN_DIFF_INPUT = 'x'
TWIN_INPUTS = ['x', 'positions', 'mla_norm', 'mla_w_in', 'mla_g_cq', 'mla_g_ckv', 'mla_w_uq', 'mla_w_ukv', 'mla_w_o', 'conv_norm', 'conv_w_in', 'conv_w', 'conv_w_out', 'ffn_norm', 'ffn_w_gate', 'ffn_w_up', 'ffn_w_down', 'final_norm', 'loss_target', 'm_mla_norm', 'm_mla_w_in', 'm_mla_g_cq', 'm_mla_g_ckv', 'm_mla_w_uq', 'm_mla_w_ukv', 'm_mla_w_o', 'm_conv_norm', 'm_conv_w_in', 'm_conv_w', 'm_conv_w_out', 'm_ffn_norm', 'm_ffn_w_gate', 'm_ffn_w_up', 'm_ffn_w_down', 'm_final_norm', 'v_mla_norm', 'v_mla_w_in', 'v_mla_g_cq', 'v_mla_g_ckv', 'v_mla_w_uq', 'v_mla_w_ukv', 'v_mla_w_o', 'v_conv_norm', 'v_conv_w_in', 'v_conv_w', 'v_conv_w_out', 'v_ffn_norm', 'v_ffn_w_gate', 'v_ffn_w_up', 'v_ffn_w_down', 'v_final_norm']
TWIN_OUTPUTS = ['loss', 'grad_x', 'grad_mla_norm', 'grad_mla_w_in', 'grad_mla_g_cq', 'grad_mla_g_ckv', 'grad_mla_w_uq', 'grad_mla_w_ukv', 'grad_mla_w_o', 'grad_conv_norm', 'grad_conv_w_in', 'grad_conv_w', 'grad_conv_w_out', 'grad_ffn_norm', 'grad_ffn_w_gate', 'grad_ffn_w_up', 'grad_ffn_w_down', 'grad_final_norm', 'delta_mla_norm', 'delta_mla_w_in', 'delta_mla_g_cq', 'delta_mla_g_ckv', 'delta_mla_w_uq', 'delta_mla_w_ukv', 'delta_mla_w_o', 'delta_conv_norm', 'delta_conv_w_in', 'delta_conv_w', 'delta_conv_w_out', 'delta_ffn_norm', 'delta_ffn_w_gate', 'delta_ffn_w_up', 'delta_ffn_w_down', 'delta_final_norm', 'new_m_mla_norm', 'new_m_mla_w_in', 'new_m_mla_g_cq', 'new_m_mla_g_ckv', 'new_m_mla_w_uq', 'new_m_mla_w_ukv', 'new_m_mla_w_o', 'new_m_conv_norm', 'new_m_conv_w_in', 'new_m_conv_w', 'new_m_conv_w_out', 'new_m_ffn_norm', 'new_m_ffn_w_gate', 'new_m_ffn_w_up', 'new_m_ffn_w_down', 'new_m_final_norm', 'new_v_mla_norm', 'new_v_mla_w_in', 'new_v_mla_g_cq', 'new_v_mla_g_ckv', 'new_v_mla_w_uq', 'new_v_mla_w_ukv', 'new_v_mla_w_o', 'new_v_conv_norm', 'new_v_conv_w_in', 'new_v_conv_w', 'new_v_conv_w_out', 'new_v_ffn_norm', 'new_v_ffn_w_gate', 'new_v_ffn_w_up', 'new_v_ffn_w_down', 'new_v_final_norm']
TWIN_LEAF_KINDS = {'loss': 'loss', 'grad_x': 'grad_x', 'grad_mla_norm': 'grad_w', 'grad_mla_w_in': 'grad_w', 'grad_mla_g_cq': 'grad_w', 'grad_mla_g_ckv': 'grad_w', 'grad_mla_w_uq': 'grad_w', 'grad_mla_w_ukv': 'grad_w', 'grad_mla_w_o': 'grad_w', 'grad_conv_norm': 'grad_w', 'grad_conv_w_in': 'grad_w', 'grad_conv_w': 'grad_w', 'grad_conv_w_out': 'grad_w', 'grad_ffn_norm': 'grad_w', 'grad_ffn_w_gate': 'grad_w', 'grad_ffn_w_up': 'grad_w', 'grad_ffn_w_down': 'grad_w', 'grad_final_norm': 'grad_w', 'delta_mla_norm': 'delta_w', 'delta_mla_w_in': 'delta_w', 'delta_mla_g_cq': 'delta_w', 'delta_mla_g_ckv': 'delta_w', 'delta_mla_w_uq': 'delta_w', 'delta_mla_w_ukv': 'delta_w', 'delta_mla_w_o': 'delta_w', 'delta_conv_norm': 'delta_w', 'delta_conv_w_in': 'delta_w', 'delta_conv_w': 'delta_w', 'delta_conv_w_out': 'delta_w', 'delta_ffn_norm': 'delta_w', 'delta_ffn_w_gate': 'delta_w', 'delta_ffn_w_up': 'delta_w', 'delta_ffn_w_down': 'delta_w', 'delta_final_norm': 'delta_w', 'new_m_mla_norm': 'new_m', 'new_m_mla_w_in': 'new_m', 'new_m_mla_g_cq': 'new_m', 'new_m_mla_g_ckv': 'new_m', 'new_m_mla_w_uq': 'new_m', 'new_m_mla_w_ukv': 'new_m', 'new_m_mla_w_o': 'new_m', 'new_m_conv_norm': 'new_m', 'new_m_conv_w_in': 'new_m', 'new_m_conv_w': 'new_m', 'new_m_conv_w_out': 'new_m', 'new_m_ffn_norm': 'new_m', 'new_m_ffn_w_gate': 'new_m', 'new_m_ffn_w_up': 'new_m', 'new_m_ffn_w_down': 'new_m', 'new_m_final_norm': 'new_m', 'new_v_mla_norm': 'new_v', 'new_v_mla_w_in': 'new_v', 'new_v_mla_g_cq': 'new_v', 'new_v_mla_g_ckv': 'new_v', 'new_v_mla_w_uq': 'new_v', 'new_v_mla_w_ukv': 'new_v', 'new_v_mla_w_o': 'new_v', 'new_v_conv_norm': 'new_v', 'new_v_conv_w_in': 'new_v', 'new_v_conv_w': 'new_v', 'new_v_conv_w_out': 'new_v', 'new_v_ffn_norm': 'new_v', 'new_v_ffn_w_gate': 'new_v', 'new_v_ffn_w_up': 'new_v', 'new_v_ffn_w_down': 'new_v', 'new_v_final_norm': 'new_v'}


def _forward(args):
    return _fwd_reference(*[args[k] for k in FWD_PARAMS])


def _output_shape():
    def fwd():
        inp = _fwd_setup_inputs(0)
        return _fwd_reference(*[inp[k] for k in FWD_PARAMS])
    out = _jax.eval_shape(fwd)
    return out.shape, out.dtype

N_MICROBATCH = 1
ADAM_LR = 0.001
ADAM_B1 = 0.9
ADAM_B2 = 0.999
ADAM_EPS = 1e-08
ADAM_WD = 0.01
ADAM_STEP = 10
PER_EXAMPLE_BATCH_AXIS = {'x': 0, 'positions': 0, 'loss_target': 0}
SHARED_INPUTS = []
_WEIGHT_DTYPES = {'mla_norm': _jnp.float32, 'mla_w_in': _jnp.float32, 'mla_g_cq': _jnp.float32, 'mla_g_ckv': _jnp.float32, 'mla_w_uq': _jnp.float32, 'mla_w_ukv': _jnp.float32, 'mla_w_o': _jnp.float32, 'conv_norm': _jnp.float32, 'conv_w_in': _jnp.float32, 'conv_w': _jnp.float32, 'conv_w_out': _jnp.float32, 'ffn_norm': _jnp.float32, 'ffn_w_gate': _jnp.float32, 'ffn_w_up': _jnp.float32, 'ffn_w_down': _jnp.float32, 'final_norm': _jnp.float32}
MOMENT_SCALE = {'mla_norm': 7.195327e-02, 'mla_w_in': 8.058454e-02, 'mla_g_cq': 5.692267e-02, 'mla_g_ckv': 1.267886e-01, 'mla_w_uq': 3.499774e-02, 'mla_w_ukv': 4.085069e-02, 'mla_w_o': 4.503229e-02, 'conv_norm': 2.145008e-01, 'conv_w_in': 1.236890e-01, 'conv_w': 1.237148e-01, 'conv_w_out': 1.269383e-01, 'ffn_norm': 1.461189e-01, 'ffn_w_gate': 6.524686e-02, 'ffn_w_up': 6.333219e-02, 'ffn_w_down': 1.052340e-01, 'final_norm': 3.208496e+01}


def _to_microbatches(a, axis):
    t = _jnp.moveaxis(a, axis, 0)
    t = t.reshape((N_MICROBATCH, t.shape[0] // N_MICROBATCH) + t.shape[1:])
    return _jnp.moveaxis(t, 1, axis + 1)


def setup_inputs(seed: int = 0) -> dict:
    inp = _fwd_setup_inputs(seed)
    key = _jax.random.fold_in(_jax.random.key(seed), 7919)
    shape, _ = _output_shape()
    out = dict(inp)
    out["loss_target"] = _jax.random.normal(_jax.random.fold_in(key, 0), shape, _jnp.float32)
    for i, name in enumerate(TWIN_WEIGHTS):
        w = inp[name].astype(_jnp.float32)
        if MOMENT_SCALE is None:
            s = _jnp.sqrt(_jnp.mean(_jnp.square(w)) + 1e-30)
        else:
            s = MOMENT_SCALE[name]
        km, kv = _jax.random.split(_jax.random.fold_in(key, i + 1))
        out[name] = w
        out["m_" + name] = s * _jax.random.normal(km, w.shape, _jnp.float32)
        out["v_" + name] = (s * s) * _jax.random.uniform(kv, w.shape, _jnp.float32, 0.5, 1.5)
    if N_MICROBATCH > 1:
        for name, axis in PER_EXAMPLE_BATCH_AXIS.items():
            out[name] = _to_microbatches(out[name], axis)
    return {'x': out['x'], 'positions': out['positions'], 'mla_norm': out['mla_norm'], 'mla_w_in': out['mla_w_in'], 'mla_g_cq': out['mla_g_cq'], 'mla_g_ckv': out['mla_g_ckv'], 'mla_w_uq': out['mla_w_uq'], 'mla_w_ukv': out['mla_w_ukv'], 'mla_w_o': out['mla_w_o'], 'conv_norm': out['conv_norm'], 'conv_w_in': out['conv_w_in'], 'conv_w': out['conv_w'], 'conv_w_out': out['conv_w_out'], 'ffn_norm': out['ffn_norm'], 'ffn_w_gate': out['ffn_w_gate'], 'ffn_w_up': out['ffn_w_up'], 'ffn_w_down': out['ffn_w_down'], 'final_norm': out['final_norm'], 'loss_target': out['loss_target'], 'm_mla_norm': out['m_mla_norm'], 'm_mla_w_in': out['m_mla_w_in'], 'm_mla_g_cq': out['m_mla_g_cq'], 'm_mla_g_ckv': out['m_mla_g_ckv'], 'm_mla_w_uq': out['m_mla_w_uq'], 'm_mla_w_ukv': out['m_mla_w_ukv'], 'm_mla_w_o': out['m_mla_w_o'], 'm_conv_norm': out['m_conv_norm'], 'm_conv_w_in': out['m_conv_w_in'], 'm_conv_w': out['m_conv_w'], 'm_conv_w_out': out['m_conv_w_out'], 'm_ffn_norm': out['m_ffn_norm'], 'm_ffn_w_gate': out['m_ffn_w_gate'], 'm_ffn_w_up': out['m_ffn_w_up'], 'm_ffn_w_down': out['m_ffn_w_down'], 'm_final_norm': out['m_final_norm'], 'v_mla_norm': out['v_mla_norm'], 'v_mla_w_in': out['v_mla_w_in'], 'v_mla_g_cq': out['v_mla_g_cq'], 'v_mla_g_ckv': out['v_mla_g_ckv'], 'v_mla_w_uq': out['v_mla_w_uq'], 'v_mla_w_ukv': out['v_mla_w_ukv'], 'v_mla_w_o': out['v_mla_w_o'], 'v_conv_norm': out['v_conv_norm'], 'v_conv_w_in': out['v_conv_w_in'], 'v_conv_w': out['v_conv_w'], 'v_conv_w_out': out['v_conv_w_out'], 'v_ffn_norm': out['v_ffn_norm'], 'v_ffn_w_gate': out['v_ffn_w_gate'], 'v_ffn_w_up': out['v_ffn_w_up'], 'v_ffn_w_down': out['v_ffn_w_down'], 'v_final_norm': out['v_final_norm']}


def _loss(weights, diff, rest, loss_target):
    with _jax.named_scope("forward"):
        args = {**rest, TWIN_DIFF_INPUT: diff, **{k: w.astype(_WEIGHT_DTYPES[k]) for k, w in weights.items()}}
        y = _forward(args)
    with _jax.named_scope("loss_head"):
        err = _jnp.square(y.astype(_jnp.float32) - loss_target)
        return 0.5 * _jnp.sum(_jnp.mean(err, axis=-1)) if err.ndim else 0.5 * err


def _adamw(w, g, m, v):
    m = ADAM_B1 * m + (1.0 - ADAM_B1) * g
    v = ADAM_B2 * v + (1.0 - ADAM_B2) * _jnp.square(g)
    m_hat = m / (1.0 - ADAM_B1 ** ADAM_STEP)
    v_hat = v / (1.0 - ADAM_B2 ** ADAM_STEP)
    delta = -ADAM_LR * (m_hat / (_jnp.sqrt(v_hat) + ADAM_EPS) + ADAM_WD * w)
    return delta, m, v


def reference(x, positions, mla_norm, mla_w_in, mla_g_cq, mla_g_ckv, mla_w_uq, mla_w_ukv, mla_w_o, conv_norm, conv_w_in, conv_w, conv_w_out, ffn_norm, ffn_w_gate, ffn_w_up, ffn_w_down, final_norm, loss_target, m_mla_norm, m_mla_w_in, m_mla_g_cq, m_mla_g_ckv, m_mla_w_uq, m_mla_w_ukv, m_mla_w_o, m_conv_norm, m_conv_w_in, m_conv_w, m_conv_w_out, m_ffn_norm, m_ffn_w_gate, m_ffn_w_up, m_ffn_w_down, m_final_norm, v_mla_norm, v_mla_w_in, v_mla_g_cq, v_mla_g_ckv, v_mla_w_uq, v_mla_w_ukv, v_mla_w_o, v_conv_norm, v_conv_w_in, v_conv_w, v_conv_w_out, v_ffn_norm, v_ffn_w_gate, v_ffn_w_up, v_ffn_w_down, v_final_norm):
    given = dict(x=x, positions=positions, mla_norm=mla_norm, mla_w_in=mla_w_in, mla_g_cq=mla_g_cq, mla_g_ckv=mla_g_ckv, mla_w_uq=mla_w_uq, mla_w_ukv=mla_w_ukv, mla_w_o=mla_w_o, conv_norm=conv_norm, conv_w_in=conv_w_in, conv_w=conv_w, conv_w_out=conv_w_out, ffn_norm=ffn_norm, ffn_w_gate=ffn_w_gate, ffn_w_up=ffn_w_up, ffn_w_down=ffn_w_down, final_norm=final_norm, loss_target=loss_target, m_mla_norm=m_mla_norm, m_mla_w_in=m_mla_w_in, m_mla_g_cq=m_mla_g_cq, m_mla_g_ckv=m_mla_g_ckv, m_mla_w_uq=m_mla_w_uq, m_mla_w_ukv=m_mla_w_ukv, m_mla_w_o=m_mla_w_o, m_conv_norm=m_conv_norm, m_conv_w_in=m_conv_w_in, m_conv_w=m_conv_w, m_conv_w_out=m_conv_w_out, m_ffn_norm=m_ffn_norm, m_ffn_w_gate=m_ffn_w_gate, m_ffn_w_up=m_ffn_w_up, m_ffn_w_down=m_ffn_w_down, m_final_norm=m_final_norm, v_mla_norm=v_mla_norm, v_mla_w_in=v_mla_w_in, v_mla_g_cq=v_mla_g_cq, v_mla_g_ckv=v_mla_g_ckv, v_mla_w_uq=v_mla_w_uq, v_mla_w_ukv=v_mla_w_ukv, v_mla_w_o=v_mla_w_o, v_conv_norm=v_conv_norm, v_conv_w_in=v_conv_w_in, v_conv_w=v_conv_w, v_conv_w_out=v_conv_w_out, v_ffn_norm=v_ffn_norm, v_ffn_w_gate=v_ffn_w_gate, v_ffn_w_up=v_ffn_w_up, v_ffn_w_down=v_ffn_w_down, v_final_norm=v_final_norm)
    weights = {n: given[n] for n in TWIN_WEIGHTS}
    shared = {n: given[n] for n in SHARED_INPUTS}
    per_example = {n: given[n] for n in ['x', 'positions']}
    grad_fn = _jax.value_and_grad(_loss, argnums=(0, 1))

    def one_microbatch(ex, loss_target):
        ex = dict(ex)
        diff = ex.pop(TWIN_DIFF_INPUT)
        return grad_fn(weights, diff, {**shared, **ex}, loss_target)

    if N_MICROBATCH == 1:
        loss, (grad_w, grad_x) = one_microbatch(per_example, given["loss_target"])
    else:
        def body(carry, xs):
            loss_sum, grad_sum = carry
            l_k, (gw_k, gx_k) = one_microbatch(xs[0], xs[1])
            with _jax.named_scope("update"):
                return (loss_sum + l_k, _jax.tree.map(_jnp.add, grad_sum, gw_k)), gx_k

        init = (_jnp.zeros((), _jnp.float32), _jax.tree.map(_jnp.zeros_like, weights))
        (loss, grad_w), grad_x = _jax.lax.scan(body, init, (per_example, given["loss_target"]))
    with _jax.named_scope("update"):
        delta_w, new_m, new_v = {}, {}, {}
        for n in TWIN_WEIGHTS:
            delta_w[n], new_m[n], new_v[n] = _adamw(weights[n], grad_w[n], given["m_" + n], given["v_" + n])
    return (loss, grad_x, *[grad_w[n] for n in TWIN_WEIGHTS], *[delta_w[n] for n in TWIN_WEIGHTS],
            *[new_m[n] for n in TWIN_WEIGHTS], *[new_v[n] for n in TWIN_WEIGHTS])
```

```python
import math

import jax
import jax.numpy as jnp
from jax import lax
from jax.experimental import pallas as pl
from jax.experimental.pallas import tpu as pltpu

F32 = jnp.float32
BF16 = jnp.bfloat16
S = jax.ShapeDtypeStruct

N_HEADS = 8
NOPE = 128
ROPE = 64
HALF = ROPE // 2
VDIM = 128
QK = NOPE + ROPE
CHUNK = 64
ROPE_THETA = 10000.0
RMS_EPS = 1e-6
ADAM_LR = 0.001
ADAM_B1 = 0.9
ADAM_B2 = 0.999
ADAM_EPS = 1e-08
ADAM_WD = 0.01
ADAM_STEP = 10

N_CHIPS = 4
N_DEV = 8
MASK_VALUE = -1e30
VMEM_LIMIT = 48 * 1024 * 1024
ATT_BLOCK = 256
SMALL_ROWS = 16

_NN = (((1,), (0,)), ((), ()))
_NT = (((1,), (1,)), ((), ()))
_TN = (((0,), (0,)), ((), ()))
MESH = pl.DeviceIdType.MESH
ANY = pl.BlockSpec(memory_space=pl.ANY)


def _params(n_axes):
    return pltpu.CompilerParams(dimension_semantics=("arbitrary",) * n_axes, vmem_limit_bytes=VMEM_LIMIT)


def _tile(n, cap, mult=8):
    for t in range(min(cap, n), 0, -1):
        if n % t == 0 and t % mult == 0:
            return t
    return n


def _sigmoid(x):
    return 1.0 / (1.0 + jnp.exp(-x))


def _mm(name, a_ops, b_ops, products, dims, grid, k_axis, outs, acc_shape, epilogue, extra_ops=()):
    na, nb, ne, no = len(a_ops), len(b_ops), len(extra_ops), len(outs)
    n_acc = 1 + max(c for _, _, c in products)
    nk = 1 if k_axis is None else grid[k_axis]

    def body(*refs):
        a_refs = refs[:na]
        b_refs = refs[na:na + nb]
        e_refs = refs[na + nb:na + nb + ne]
        o_refs = refs[na + nb + ne:na + nb + ne + no]
        acc_refs = refs[na + nb + ne + no:]

        def partial_sums():
            vals = [None] * n_acc
            for ai, bi, ci in products:
                d = lax.dot_general(a_refs[ai][...].astype(BF16), b_refs[bi][...].astype(BF16), dims,
                                    preferred_element_type=F32)
                vals[ci] = d if vals[ci] is None else vals[ci] + d
            return vals

        if nk == 1:
            epilogue(partial_sums(), e_refs, o_refs)
        else:
            k = pl.program_id(k_axis)

            @pl.when(k == 0)
            def _():
                for acc in acc_refs:
                    acc[...] = jnp.zeros_like(acc)

            for acc, v in zip(acc_refs, partial_sums()):
                acc[...] += v

            @pl.when(k == nk - 1)
            def _():
                epilogue([acc[...] for acc in acc_refs], e_refs, o_refs)

    ops = list(a_ops) + list(b_ops) + list(extra_ops)
    return pl.pallas_call(
        body, name=name, grid=grid,
        in_specs=[s for _, s in ops], out_specs=[s for _, s in outs], out_shape=[o for o, _ in outs],
        scratch_shapes=[pltpu.VMEM(acc_shape, F32) for _ in range(n_acc if nk > 1 else 0)],
        compiler_params=_params(len(grid)),
    )(*[a for a, _ in ops])


def _store(accs, e_refs, o_refs):
    o_refs[0][...] = accs[0].astype(o_refs[0].dtype)


def _store_plus_residual(accs, e_refs, o_refs):
    o_refs[0][...] = (e_refs[0][...] + accs[0]).astype(o_refs[0].dtype)


def linear(name, x, w, out_dtype, resid=None):
    t, k = x.shape
    n = w.shape[1]
    tm = _tile(t, 512)
    tn = n if n <= 2048 else _tile(n, 1024, 128)
    extra = [] if resid is None else [(resid, pl.BlockSpec((tm, tn), lambda j, i: (i, j)))]
    return _mm(name, [(x, pl.BlockSpec((tm, k), lambda j, i: (i, 0)))], [(w, pl.BlockSpec((k, tn), lambda j, i: (0, j)))],
               [(0, 0, 0)], _NN, (n // tn, t // tm), None,
               [(S((t, n), out_dtype), pl.BlockSpec((tm, tn), lambda j, i: (i, j)))], None,
               _store if resid is None else _store_plus_residual, extra)[0]


def linear_nt(name, dy, w, out_dtype):
    t, n = dy.shape
    k = w.shape[0]
    tm = _tile(t, 512)
    tc = n if n <= 2048 else _tile(n, 1024, 128)
    return _mm(name, [(dy, pl.BlockSpec((tm, tc), lambda i, c: (i, c)))], [(w, pl.BlockSpec((k, tc), lambda i, c: (0, c)))],
               [(0, 0, 0)], _NT, (t // tm, n // tc), 1,
               [(S((t, k), out_dtype), pl.BlockSpec((tm, k), lambda i, c: (i, 0)))], (tm, k), _store)[0]


def wgrad(name, x, dy):
    t, k = x.shape
    n = dy.shape[1]
    tk = _tile(t, 512)
    tn = n if n <= 1024 else _tile(n, 1024, 128)
    return _mm(name, [(x, pl.BlockSpec((tk, k), lambda j, s: (s, 0)))], [(dy, pl.BlockSpec((tk, tn), lambda j, s: (s, j)))],
               [(0, 0, 0)], _TN, (n // tn, t // tk), 1,
               [(S((k, n), BF16), pl.BlockSpec((k, tn), lambda j, s: (0, j)))], (k, tn), _store)[0]


def ffn_up(name, a, wg_all, wu_all, layer):
    t, d = a.shape
    f4 = wg_all.shape[2]
    tm = _tile(t, 512)
    w_spec = pl.BlockSpec((None, d, f4), lambda k, i: (k, layer, 0))
    h_spec = pl.BlockSpec((None, tm, f4), lambda k, i: (k, i, 0))

    def epilogue(accs, e_refs, o_refs):
        g, u = accs
        o_refs[0][...] = g
        o_refs[1][...] = u
        o_refs[2][...] = (g * _sigmoid(g) * u).astype(BF16)

    return _mm(name, [(a, pl.BlockSpec((tm, d), lambda k, i: (i, 0)))], [(wg_all, w_spec), (wu_all, w_spec)],
               [(0, 0, 0), (0, 1, 1)], _NN, (N_CHIPS, t // tm), None,
               [(S((N_CHIPS, t, f4), F32), h_spec), (S((N_CHIPS, t, f4), F32), h_spec), (S((N_CHIPS, t, f4), BF16), h_spec)],
               None, epilogue)


def ffn_down(name, z, wd_all, layer, resid):
    _, t, f4 = z.shape
    d = wd_all.shape[2]
    tm = _tile(t, 512)
    return _mm(name, [(z, pl.BlockSpec((None, tm, f4), lambda i, k: (k, i, 0)))],
               [(wd_all, pl.BlockSpec((None, f4, d), lambda i, k: (k, layer, 0)))],
               [(0, 0, 0)], _NN, (t // tm, N_CHIPS), 1,
               [(S((t, d), F32), pl.BlockSpec((tm, d), lambda i, k: (i, 0)))], (tm, d), _store_plus_residual,
               [(resid, pl.BlockSpec((tm, d), lambda i, k: (i, 0)))])[0]


def ffn_bwd_hidden(name, dh, wd_all, layer, g, u):
    t, d = dh.shape
    f4 = g.shape[2]
    tm = _tile(t, 512)
    h_spec = pl.BlockSpec((None, tm, f4), lambda k, i: (k, i, 0))

    def epilogue(accs, e_refs, o_refs):
        dz = accs[0]
        gv = e_refs[0][...]
        uv = e_refs[1][...]
        sg = _sigmoid(gv)
        o_refs[0][...] = (dz * uv * (sg * (1.0 + gv * (1.0 - sg)))).astype(BF16)
        o_refs[1][...] = (dz * (gv * sg)).astype(BF16)

    return _mm(name, [(dh, pl.BlockSpec((tm, d), lambda k, i: (i, 0)))],
               [(wd_all, pl.BlockSpec((None, f4, d), lambda k, i: (k, layer, 0)))],
               [(0, 0, 0)], _NT, (N_CHIPS, t // tm), None,
               [(S((N_CHIPS, t, f4), BF16), h_spec), (S((N_CHIPS, t, f4), BF16), h_spec)], None, epilogue,
               [(g, h_spec), (u, h_spec)])


def ffn_bwd_input(name, dg, du, wg_all, wu_all, layer, d):
    _, t, f4 = dg.shape
    tm = _tile(t, 512)
    h_spec = pl.BlockSpec((None, tm, f4), lambda i, k: (k, i, 0))
    w_spec = pl.BlockSpec((None, d, f4), lambda i, k: (k, layer, 0))
    return _mm(name, [(dg, h_spec), (du, h_spec)], [(wg_all, w_spec), (wu_all, w_spec)],
               [(0, 0, 0), (1, 1, 0)], _NT, (t // tm, N_CHIPS), 1,
               [(S((t, d), F32), pl.BlockSpec((tm, d), lambda i, k: (i, 0)))], (tm, d), _store)[0]


def ffn_wgrad_up(name, a, dg, du):
    t, d = a.shape
    f4 = dg.shape[2]
    tk = _tile(t, 512)
    h_spec = pl.BlockSpec((None, tk, f4), lambda k, s: (k, s, 0))
    o_spec = pl.BlockSpec((None, d, f4), lambda k, s: (k, 0, 0))

    def epilogue(accs, e_refs, o_refs):
        o_refs[0][...] = accs[0].astype(BF16)
        o_refs[1][...] = accs[1].astype(BF16)

    return _mm(name, [(a, pl.BlockSpec((tk, d), lambda k, s: (s, 0)))], [(dg, h_spec), (du, h_spec)],
               [(0, 0, 0), (0, 1, 1)], _TN, (N_CHIPS, t // tk), 1,
               [(S((N_CHIPS, d, f4), BF16), o_spec), (S((N_CHIPS, d, f4), BF16), o_spec)], (d, f4), epilogue)


def ffn_wgrad_down(name, z, dh):
    _, t, f4 = z.shape
    d = dh.shape[1]
    tk = _tile(t, 512)
    return _mm(name, [(z, pl.BlockSpec((None, tk, f4), lambda k, s: (k, s, 0)))], [(dh, pl.BlockSpec((tk, d), lambda k, s: (s, 0)))],
               [(0, 0, 0)], _TN, (N_CHIPS, t // tk), 1,
               [(S((N_CHIPS, f4, d), BF16), pl.BlockSpec((None, f4, d), lambda k, s: (k, 0, 0)))], (f4, d), _store)[0]


def conv_in_proj(name, a, w):
    t, d = a.shape
    tm = _tile(t, 512)
    return _mm(name, [(a, pl.BlockSpec((tm, d), lambda j, i: (i, 0)))], [(w, pl.BlockSpec((d, d), lambda j, i: (0, j)))],
               [(0, 0, 0)], _NN, (3, t // tm), None,
               [(S((3, t, d), F32), pl.BlockSpec((None, tm, d), lambda j, i: (j, i, 0)))], None, _store)[0]


def conv_in_bwd_input(name, dbcx, w):
    _, t, d = dbcx.shape
    tm = _tile(t, 512)
    return _mm(name, [(dbcx, pl.BlockSpec((None, tm, d), lambda i, j: (j, i, 0)))], [(w, pl.BlockSpec((d, d), lambda i, j: (0, j)))],
               [(0, 0, 0)], _NT, (t // tm, 3), 1,
               [(S((t, d), F32), pl.BlockSpec((tm, d), lambda i, j: (i, 0)))], (tm, d), _store)[0]


def conv_in_wgrad(name, a, dbcx):
    t, d = a.shape
    tk = _tile(t, 512)
    return _mm(name, [(a, pl.BlockSpec((tk, d), lambda j, s: (s, 0)))], [(dbcx, pl.BlockSpec((None, tk, d), lambda j, s: (j, s, 0)))],
               [(0, 0, 0)], _TN, (3, t // tk), 1,
               [(S((d, 3 * d), BF16), pl.BlockSpec((d, d), lambda j, s: (0, j)))], (d, d), _store)[0]


def _rstd(x):
    return lax.rsqrt(jnp.mean(x * x, axis=-1, keepdims=True) + RMS_EPS)


def _rms_bwd(x, g, dy):
    r = _rstd(x)
    xhat = x * r
    dgain = jnp.sum(dy * xhat, axis=0, keepdims=True)
    dxh = dy * g
    dx = r * (dxh - xhat * jnp.mean(dxh * xhat, axis=-1, keepdims=True))
    return dx, dgain


def rms_fwd(name, h, g):
    t, d = h.shape
    tr = _tile(t, 512)

    def body(h_ref, g_ref, a_ref):
        x = h_ref[...]
        a_ref[...] = (x * _rstd(x) * g_ref[...]).astype(BF16)

    return pl.pallas_call(
        body, name=name, grid=(t // tr,),
        in_specs=[pl.BlockSpec((tr, d), lambda i: (i, 0)), pl.BlockSpec((1, d), lambda i: (0, 0))],
        out_specs=pl.BlockSpec((tr, d), lambda i: (i, 0)), out_shape=S((t, d), BF16), compiler_params=_params(1))(h, g)


def rms_bwd(name, h, g, da, dh_in):
    t, d = h.shape
    tr = _tile(t, 512)

    def body(h_ref, g_ref, da_ref, dhi_ref, dho_ref, dg_ref):
        dx, dgain = _rms_bwd(h_ref[...], g_ref[...], da_ref[...])
        dho_ref[...] = dhi_ref[...] + dx

        @pl.when(pl.program_id(0) == 0)
        def _():
            dg_ref[...] = jnp.zeros_like(dg_ref)

        dg_ref[...] += dgain

    row = pl.BlockSpec((tr, d), lambda i: (i, 0))
    vec = pl.BlockSpec((1, d), lambda i: (0, 0))
    return pl.pallas_call(
        body, name=name, grid=(t // tr,), in_specs=[row, vec, row, row], out_specs=[row, vec],
        out_shape=[S((t, d), F32), S((1, d), F32)], compiler_params=_params(1))(h, g, da, dh_in)


def loss_head(name, h, g, target):
    t, d = h.shape
    tr = _tile(t, 512)

    def body(h_ref, g_ref, t_ref, dh_ref, dg_ref, loss_ref):
        x = h_ref[...]
        g = g_ref[...]
        r = _rstd(x)
        xhat = x * r
        err = xhat * g - t_ref[...]
        dy = err * (1.0 / d)
        dxh = dy * g
        dh_ref[...] = r * (dxh - xhat * jnp.mean(dxh * xhat, axis=-1, keepdims=True))

        @pl.when(pl.program_id(0) == 0)
        def _():
            dg_ref[...] = jnp.zeros_like(dg_ref)
            loss_ref[...] = jnp.zeros_like(loss_ref)

        dg_ref[...] += jnp.sum(dy * xhat, axis=0, keepdims=True)
        per_token = jnp.mean(err * err, axis=-1, keepdims=True)
        loss_ref[...] += 0.5 * jnp.sum(per_token, axis=0, keepdims=True)

    row = pl.BlockSpec((tr, d), lambda i: (i, 0))
    vec = pl.BlockSpec((1, d), lambda i: (0, 0))
    one = pl.BlockSpec((1, 1), lambda i: (0, 0))
    return pl.pallas_call(
        body, name=name, grid=(t // tr,), in_specs=[row, vec, row], out_specs=[row, vec, one],
        out_shape=[S((t, d), F32), S((1, d), F32), S((1, 1), F32)], compiler_params=_params(1))(h, g, target)


def mla_mid(name, proj, g_cq, g_ckv, cos, sin):
    t, n = proj.shape
    ql, kl = g_cq.shape[1], g_ckv.shape[1]
    tr = _tile(t, 512)

    def body(p_ref, gq_ref, gk_ref, c_ref, s_ref, cq_ref, ckv_ref, kr_ref):
        xq = p_ref[:, 0:ql]
        cq_ref[...] = (xq * _rstd(xq) * gq_ref[...]).astype(BF16)
        xk = p_ref[:, ql:ql + kl]
        ckv_ref[...] = (xk * _rstd(xk) * gk_ref[...]).astype(BF16)
        k1 = p_ref[:, ql + kl:ql + kl + HALF]
        k2 = p_ref[:, ql + kl + HALF:ql + kl + ROPE]
        c = c_ref[...]
        s = s_ref[...]
        kr_ref[:, 0:HALF] = k1 * c - k2 * s
        kr_ref[:, HALF:ROPE] = k1 * s + k2 * c

    def row(w):
        return pl.BlockSpec((tr, w), lambda i: (i, 0))

    def vec(w):
        return pl.BlockSpec((1, w), lambda i: (0, 0))

    return pl.pallas_call(
        body, name=name, grid=(t // tr,), in_specs=[row(n), vec(ql), vec(kl), row(HALF), row(HALF)],
        out_specs=[row(ql), row(kl), row(ROPE)], out_shape=[S((t, ql), BF16), S((t, kl), BF16), S((t, ROPE), F32)],
        compiler_params=_params(1))(proj, g_cq, g_ckv, cos, sin)


def mla_mid_bwd(name, proj, g_cq, g_ckv, dcq, dckv, dkr, cos, sin):
    t, n = proj.shape
    ql, kl = g_cq.shape[1], g_ckv.shape[1]
    tr = _tile(t, 512)

    def body(p_ref, gq_ref, gk_ref, dcq_ref, dckv_ref, dkr_ref, c_ref, s_ref, dp_ref, dgq_ref, dgk_ref):
        dxq, dgq = _rms_bwd(p_ref[:, 0:ql], gq_ref[...], dcq_ref[...])
        dp_ref[:, 0:ql] = dxq.astype(BF16)
        dxk, dgk = _rms_bwd(p_ref[:, ql:ql + kl], gk_ref[...], dckv_ref[...])
        dp_ref[:, ql:ql + kl] = dxk.astype(BF16)
        d1 = dkr_ref[:, 0:HALF]
        d2 = dkr_ref[:, HALF:ROPE]
        c = c_ref[...]
        s = s_ref[...]
        dp_ref[:, ql + kl:ql + kl + HALF] = (d1 * c + d2 * s).astype(BF16)
        dp_ref[:, ql + kl + HALF:ql + kl + ROPE] = (d2 * c - d1 * s).astype(BF16)

        @pl.when(pl.program_id(0) == 0)
        def _():
            dgq_ref[...] = jnp.zeros_like(dgq_ref)
            dgk_ref[...] = jnp.zeros_like(dgk_ref)

        dgq_ref[...] += dgq
        dgk_ref[...] += dgk

    def row(w):
        return pl.BlockSpec((tr, w), lambda i: (i, 0))

    def vec(w):
        return pl.BlockSpec((1, w), lambda i: (0, 0))

    return pl.pallas_call(
        body, name=name, grid=(t // tr,),
        in_specs=[row(n), vec(ql), vec(kl), row(ql), row(kl), row(ROPE), row(HALF), row(HALF)],
        out_specs=[row(n), vec(ql), vec(kl)], out_shape=[S((t, n), BF16), S((1, ql), F32), S((1, kl), F32)],
        compiler_params=_params(1))(proj, g_cq, g_ckv, dcq, dckv, dkr, cos, sin)


def qkv_heads(name, q, kv, kr, cos, sin):
    t = q.shape[0]
    tr = _tile(t, 256)

    def body(q_ref, kv_ref, kr_ref, c_ref, s_ref, qo_ref, ko_ref, vo_ref):
        c = c_ref[...]
        s = s_ref[...]
        krb = kr_ref[...].astype(BF16)
        for h in range(N_HEADS):
            q0 = h * QK
            qo_ref[h, :, 0:NOPE] = q_ref[:, q0:q0 + NOPE].astype(BF16)
            q1 = q_ref[:, q0 + NOPE:q0 + NOPE + HALF]
            q2 = q_ref[:, q0 + NOPE + HALF:q0 + QK]
            qo_ref[h, :, NOPE:NOPE + HALF] = (q1 * c - q2 * s).astype(BF16)
            qo_ref[h, :, NOPE + HALF:QK] = (q1 * s + q2 * c).astype(BF16)
            k0 = h * (NOPE + VDIM)
            ko_ref[h, :, 0:NOPE] = kv_ref[:, k0:k0 + NOPE]
            ko_ref[h, :, NOPE:QK] = krb
            vo_ref[h] = kv_ref[:, k0 + NOPE:k0 + NOPE + VDIM]

    def row(w):
        return pl.BlockSpec((tr, w), lambda i: (i, 0))

    def heads(w):
        return pl.BlockSpec((N_HEADS, tr, w), lambda i: (0, i, 0))

    return pl.pallas_call(
        body, name=name, grid=(t // tr,),
        in_specs=[row(N_HEADS * QK), row(N_HEADS * (NOPE + VDIM)), row(ROPE), row(HALF), row(HALF)],
        out_specs=[heads(QK), heads(QK), heads(VDIM)],
        out_shape=[S((N_HEADS, t, QK), BF16), S((N_HEADS, t, QK), BF16), S((N_HEADS, t, VDIM), BF16)],
        compiler_params=_params(1))(q, kv, kr, cos, sin)


def qkv_heads_bwd(name, dq_h, dk_h, dv_h, cos, sin):
    t = dq_h.shape[1]
    tr = _tile(t, 256)

    def body(dq_ref, dk_ref, dv_ref, c_ref, s_ref, q_ref, kv_ref, kr_ref):
        c = c_ref[...]
        s = s_ref[...]
        dkr = jnp.zeros((tr, ROPE), F32)
        for h in range(N_HEADS):
            q0 = h * QK
            q_ref[:, q0:q0 + NOPE] = dq_ref[h, :, 0:NOPE].astype(BF16)
            d1 = dq_ref[h, :, NOPE:NOPE + HALF]
            d2 = dq_ref[h, :, NOPE + HALF:QK]
            q_ref[:, q0 + NOPE:q0 + NOPE + HALF] = (d1 * c + d2 * s).astype(BF16)
            q_ref[:, q0 + NOPE + HALF:q0 + QK] = (d2 * c - d1 * s).astype(BF16)
            k0 = h * (NOPE + VDIM)
            kv_ref[:, k0:k0 + NOPE] = dk_ref[h, :, 0:NOPE].astype(BF16)
            kv_ref[:, k0 + NOPE:k0 + NOPE + VDIM] = dv_ref[h].astype(BF16)
            dkr = dkr + dk_ref[h, :, NOPE:QK]
        kr_ref[...] = dkr

    def row(w):
        return pl.BlockSpec((tr, w), lambda i: (i, 0))

    def heads(w):
        return pl.BlockSpec((N_HEADS, tr, w), lambda i: (0, i, 0))

    return pl.pallas_call(
        body, name=name, grid=(t // tr,),
        in_specs=[heads(QK), heads(QK), heads(VDIM), row(HALF), row(HALF)],
        out_specs=[row(N_HEADS * QK), row(N_HEADS * (NOPE + VDIM)), row(ROPE)],
        out_shape=[S((t, N_HEADS * QK), BF16), S((t, N_HEADS * (NOPE + VDIM)), BF16), S((t, ROPE), F32)],
        compiler_params=_params(1))(dq_h, dk_h, dv_h, cos, sin)


def _chunk_mask(q_start, k_start, bq, bk):
    qc = (q_start + lax.broadcasted_iota(jnp.int32, (bq, bk), 0)) // CHUNK
    kc = (k_start + lax.broadcasted_iota(jnp.int32, (bq, bk), 1)) // CHUNK
    return kc <= qc


def attention_fwd(name, q, k, v):
    nh, t, _ = q.shape
    blk = ATT_BLOCK
    scale = 1.0 / math.sqrt(QK)

    def body(q_ref, k_ref, v_ref, o_ref, lse_ref, m_ref, l_ref, acc_ref):
        i = pl.program_id(1)
        qv = q_ref[...]
        m_ref[...] = jnp.full_like(m_ref, MASK_VALUE)
        l_ref[...] = jnp.zeros_like(l_ref)
        acc_ref[...] = jnp.zeros_like(acc_ref)

        def block(j, masked):
            start = pl.multiple_of(j * blk, blk)
            kb = k_ref[pl.ds(start, blk), :]
            vb = v_ref[pl.ds(start, blk), :]
            s = lax.dot_general(qv, kb, _NT, preferred_element_type=F32) * scale
            if masked:
                s = jnp.where(_chunk_mask(i * blk, j * blk, blk, blk), s, MASK_VALUE)
            m_old = m_ref[...]
            m_new = jnp.maximum(m_old, jnp.max(s, axis=-1, keepdims=True))
            p = jnp.exp(s - m_new)
            alpha = jnp.exp(m_old - m_new)
            l_ref[...] = alpha * l_ref[...] + jnp.sum(p, axis=-1, keepdims=True)
            acc_ref[...] = alpha * acc_ref[...] + jnp.dot(p.astype(BF16), vb, preferred_element_type=F32)
            m_ref[...] = m_new

        def step(j, carry):
            block(j, False)
            return carry

        lax.fori_loop(0, i, step, 0)
        block(i, True)
        l = l_ref[...]
        o_ref[...] = acc_ref[...] / l
        lse_ref[...] = m_ref[...] + jnp.log(l)

    return pl.pallas_call(
        body, name=name, grid=(nh, t // blk),
        in_specs=[pl.BlockSpec((None, blk, QK), lambda h, i: (h, i, 0)), pl.BlockSpec((None, t, QK), lambda h, i: (h, 0, 0)),
                  pl.BlockSpec((None, t, VDIM), lambda h, i: (h, 0, 0))],
        out_specs=[pl.BlockSpec((blk, VDIM), lambda h, i: (i, h)), pl.BlockSpec((None, blk, 1), lambda h, i: (h, i, 0))],
        out_shape=[S((t, nh * VDIM), F32), S((nh, t, 1), F32)],
        scratch_shapes=[pltpu.VMEM((blk, 1), F32), pltpu.VMEM((blk, 1), F32), pltpu.VMEM((blk, VDIM), F32)],
        compiler_params=_params(2))(q, k, v)


def attention_delta(name, do, o):
    t = do.shape[0]
    tr = _tile(t, 512)

    def body(do_ref, o_ref, d_ref):
        d_ref[...] = jnp.sum(do_ref[...] * o_ref[...], axis=-1, keepdims=True)

    blk = pl.BlockSpec((tr, VDIM), lambda h, i: (i, h))
    return pl.pallas_call(
        body, name=name, grid=(N_HEADS, t // tr), in_specs=[blk, blk],
        out_specs=pl.BlockSpec((None, tr, 1), lambda h, i: (h, i, 0)), out_shape=S((N_HEADS, t, 1), F32),
        compiler_params=_params(2))(do, o)


def attention_bwd(name, q, k, v, do, lse, delta):
    nh, t, _ = q.shape
    blk = ATT_BLOCK
    nq = t // blk
    scale = 1.0 / math.sqrt(QK)

    def body(q_ref, k_ref, v_ref, do_ref, lse_ref, dl_ref, dq_ref, dk_ref, dv_ref):
        j = pl.program_id(1)

        @pl.when(j == 0)
        def _():
            dq_ref[...] = jnp.zeros_like(dq_ref)

        kb = k_ref[...]
        vb = v_ref[...]
        dk_ref[...] = jnp.zeros_like(dk_ref)
        dv_ref[...] = jnp.zeros_like(dv_ref)

        def block(i, masked):
            rows = pl.ds(pl.multiple_of(i * blk, blk), blk)
            qb = q_ref[rows, :]
            dob = do_ref[rows, :].astype(BF16)
            s = lax.dot_general(qb, kb, _NT, preferred_element_type=F32) * scale
            if masked:
                s = jnp.where(_chunk_mask(i * blk, j * blk, blk, blk), s, MASK_VALUE)
            p = jnp.exp(s - lse_ref[rows, :])
            dp = lax.dot_general(dob, vb, _NT, preferred_element_type=F32)
            ds = (p * (dp - dl_ref[rows, :]) * scale).astype(BF16)
            dv_ref[...] += lax.dot_general(p.astype(BF16), dob, _TN, preferred_element_type=F32)
            dk_ref[...] += lax.dot_general(ds, qb, _TN, preferred_element_type=F32)
            dq_ref[rows, :] += jnp.dot(ds, kb, preferred_element_type=F32)

        block(j, True)

        def step(i, carry):
            block(i, False)
            return carry

        lax.fori_loop(j + 1, nq, step, 0)

    head_all = lambda w: pl.BlockSpec((None, t, w), lambda h, j: (h, 0, 0))
    head_blk = lambda w: pl.BlockSpec((None, blk, w), lambda h, j: (h, j, 0))
    return pl.pallas_call(
        body, name=name, grid=(nh, nq),
        in_specs=[head_all(QK), head_blk(QK), head_blk(VDIM), pl.BlockSpec((t, VDIM), lambda h, j: (0, h)), head_all(1), head_all(1)],
        out_specs=[head_all(QK), head_blk(QK), head_blk(VDIM)],
        out_shape=[S((nh, t, QK), F32), S((nh, t, QK), F32), S((nh, t, VDIM), F32)],
        compiler_params=_params(2))(q, k, v, do, lse, delta)


def _shift_down(u, s):
    rows = lax.broadcasted_iota(jnp.int32, u.shape, 0)
    return jnp.where(rows >= s, pltpu.roll(u, s, 0), 0.0)


def _shift_up(u, s):
    n = u.shape[0]
    rows = lax.broadcasted_iota(jnp.int32, u.shape, 0)
    return jnp.where(rows < n - s, pltpu.roll(u, n - s, 0), 0.0)


def _conv_specs(t, d, lanes):
    slab = lambda part: pl.BlockSpec((None, t, lanes), lambda j, part=part: (part, 0, j))
    return slab, pl.BlockSpec((3, lanes), lambda j: (0, j)), pl.BlockSpec((t, lanes), lambda j: (0, j))


def conv_fwd(name, bcx, w):
    _, t, d = bcx.shape
    lanes = _tile(d, 128, 128)
    slab, w_spec, col = _conv_specs(t, d, lanes)

    def body(b_ref, c_ref, x_ref, w_ref, y_ref):
        u = c_ref[...] * x_ref[...]
        uc = w_ref[0:1, :] * _shift_down(u, 2) + w_ref[1:2, :] * _shift_down(u, 1) + w_ref[2:3, :] * u
        y_ref[...] = (b_ref[...] * uc).astype(BF16)

    return pl.pallas_call(
        body, name=name, grid=(d // lanes,), in_specs=[slab(0), slab(1), slab(2), w_spec], out_specs=col,
        out_shape=S((t, d), BF16), compiler_params=_params(1))(bcx, bcx, bcx, w)


def conv_bwd(name, bcx, w, dy):
    _, t, d = bcx.shape
    lanes = _tile(d, 128, 128)
    slab, w_spec, col = _conv_specs(t, d, lanes)

    def body(b_ref, c_ref, x_ref, w_ref, dy_ref, db_ref, dc_ref, dx_ref, dw_ref):
        c = c_ref[...]
        x = x_ref[...]
        dyv = dy_ref[...]
        u = c * x
        u1 = _shift_down(u, 1)
        u2 = _shift_down(u, 2)
        w0, w1, w2 = w_ref[0:1, :], w_ref[1:2, :], w_ref[2:3, :]
        db_ref[...] = (dyv * (w0 * u2 + w1 * u1 + w2 * u)).astype(BF16)
        duc = dyv * b_ref[...]
        dw_ref[0:1, :] = jnp.sum(duc * u2, axis=0, keepdims=True)
        dw_ref[1:2, :] = jnp.sum(duc * u1, axis=0, keepdims=True)
        dw_ref[2:3, :] = jnp.sum(duc * u, axis=0, keepdims=True)
        du = w2 * duc + w1 * _shift_up(duc, 1) + w0 * _shift_up(duc, 2)
        dc_ref[...] = (du * x).astype(BF16)
        dx_ref[...] = (du * c).astype(BF16)

    return pl.pallas_call(
        body, name=name, grid=(d // lanes,), in_specs=[slab(0), slab(1), slab(2), w_spec, col],
        out_specs=[col, col, col, w_spec], out_shape=[S((t, d), BF16)] * 3 + [S((3, d), F32)],
        compiler_params=_params(1))(bcx, bcx, bcx, w, dy)


def adamw(name, w, g, m, v):
    r, c = w.shape
    tr = _tile(r, 512)

    def body(w_ref, g_ref, m_ref, v_ref, d_ref, mo_ref, vo_ref):
        gv = g_ref[...]
        m_new = ADAM_B1 * m_ref[...] + (1.0 - ADAM_B1) * gv
        v_new = ADAM_B2 * v_ref[...] + (1.0 - ADAM_B2) * (gv * gv)
        m_hat = m_new / (1.0 - ADAM_B1 ** ADAM_STEP)
        v_hat = v_new / (1.0 - ADAM_B2 ** ADAM_STEP)
        d_ref[...] = -ADAM_LR * (m_hat / (jnp.sqrt(v_hat) + ADAM_EPS) + ADAM_WD * w_ref[...])
        mo_ref[...] = m_new
        vo_ref[...] = v_new

    blk = pl.BlockSpec((tr, c), lambda i: (i, 0))
    return pl.pallas_call(
        body, name=name, grid=(r // tr,), in_specs=[blk] * 4, out_specs=[blk] * 3, out_shape=[S((r, c), F32)] * 3,
        compiler_params=_params(1))(w, g, m, v)


def _place():
    x, y, c = lax.axis_index("x"), lax.axis_index("y"), lax.axis_index("c")
    other_chips = [(1 - x, y), (x, 1 - y), (1 - x, 1 - y)]
    return x, y, c, other_chips


def _half(c, rows):
    return pl.ds(pl.multiple_of(c * (rows // 2), 16), rows // 2)


def gather_weight_shards(shards):
    n = len(shards)

    def body(*refs):
        src = refs[:n]
        dst = refs[n:2 * n]
        local_sems, send_sems, recv_sems = refs[2 * n:]
        x, y, c, chips = _place()
        me = 2 * x + y
        sibling = (x, y, 1 - c)

        def copy(i, slot, half_of, sem, to, from_input=False):
            rows = _half(half_of, src[i].shape[0])
            return pltpu.make_async_remote_copy(
                src_ref=src[i].at[rows] if from_input else dst[i].at[slot, rows], dst_ref=dst[i].at[slot, rows],
                send_sem=send_sems.at[i, sem], recv_sem=recv_sems.at[i, sem], device_id=to, device_id_type=MESH)

        local = [pltpu.make_async_copy(src[i], dst[i].at[me], local_sems.at[i]) for i in range(n)]
        for cp in local:
            cp.start()
        sent = []
        for i in range(n):
            for j, chip in enumerate(chips):
                sent.append(copy(i, me, c, j, (*chip, c), from_input=True))
                sent[-1].start()
        for i in range(n):
            for j, (px, py) in enumerate(chips):
                copy(i, 2 * px + py, c, j, sibling).wait_recv()
                sent.append(copy(i, 2 * px + py, c, 3 + j, sibling))
                sent[-1].start()
        for i in range(n):
            for j, (px, py) in enumerate(chips):
                copy(i, 2 * px + py, 1 - c, 3 + j, sibling).wait_recv()
        for cp in sent:
            cp.wait_send()
        for cp in local:
            cp.wait()

    return pl.pallas_call(
        body, name="gather_weight_shards", in_specs=[ANY] * n, out_specs=[ANY] * n,
        out_shape=[S((N_CHIPS,) + s.shape, s.dtype) for s in shards],
        scratch_shapes=[pltpu.SemaphoreType.DMA((n,)), pltpu.SemaphoreType.DMA((n, 6)), pltpu.SemaphoreType.DMA((n, 6))],
    )(*shards)


def sibling_swap_halves(grads):
    n = len(grads)

    def body(*refs):
        src = refs[:n]
        dst = refs[n:2 * n]
        send_sems, recv_sems = refs[2 * n:]
        x, y, c, _ = _place()
        copies = [pltpu.make_async_remote_copy(
            src_ref=src[i].at[:, _half(1 - c, src[i].shape[1]), :], dst_ref=dst[i], send_sem=send_sems.at[i],
            recv_sem=recv_sems.at[i], device_id=(x, y, 1 - c), device_id_type=MESH) for i in range(n)]
        for cp in copies:
            cp.start()
        for cp in copies:
            cp.wait()

    return pl.pallas_call(
        body, name="sibling_swap_halves", in_specs=[ANY] * n, out_specs=[ANY] * n,
        out_shape=[S((g.shape[0], g.shape[1] // 2, g.shape[2]), g.dtype) for g in grads],
        scratch_shapes=[pltpu.SemaphoreType.DMA((n,)), pltpu.SemaphoreType.DMA((n,))],
    )(*grads)


def add_halves(name, g, rx):
    _, r, cdim = g.shape
    r2 = r // 2
    tr = _tile(r2, 512, 16)
    nb = r2 // tr

    def body(lo_ref, hi_ref, rx_ref, o_ref):
        mine = jnp.where(lax.axis_index("c") == 0, lo_ref[...], hi_ref[...])
        o_ref[...] = (mine.astype(F32) + rx_ref[...].astype(F32)).astype(BF16)

    half = pl.BlockSpec((None, tr, cdim), lambda k, i: (k, i, 0))
    return pl.pallas_call(
        body, name=name, grid=(N_CHIPS, nb),
        in_specs=[half, pl.BlockSpec((None, tr, cdim), lambda k, i: (k, nb + i, 0)), half],
        out_specs=half, out_shape=S((N_CHIPS, r2, cdim), BF16), compiler_params=_params(2))(g, g, rx)


def scatter_to_owner_chips(parts):
    n = len(parts)

    def body(*refs):
        src = refs[:n]
        dst = refs[n:2 * n]
        local_sems, send_sems, recv_sems = refs[2 * n:]
        x, y, c, chips = _place()
        me = 2 * x + y
        local = [pltpu.make_async_copy(src[i].at[me], dst[i].at[me], local_sems.at[i]) for i in range(n)]
        for cp in local:
            cp.start()

        def copy(i, j, from_slot, to_slot, to):
            return pltpu.make_async_remote_copy(
                src_ref=src[i].at[from_slot], dst_ref=dst[i].at[to_slot], send_sem=send_sems.at[i, j],
                recv_sem=recv_sems.at[i, j], device_id=to, device_id_type=MESH)

        sent = []
        for i in range(n):
            for j, (px, py) in enumerate(chips):
                sent.append(copy(i, j, 2 * px + py, me, (px, py, c)))
                sent[-1].start()
        for i in range(n):
            for j, (px, py) in enumerate(chips):
                copy(i, j, me, 2 * px + py, (px, py, c)).wait_recv()
        for cp in sent:
            cp.wait_send()
        for cp in local:
            cp.wait()

    return pl.pallas_call(
        body, name="scatter_to_owner_chips", in_specs=[ANY] * n, out_specs=[ANY] * n,
        out_shape=[S(p.shape, p.dtype) for p in parts],
        scratch_shapes=[pltpu.SemaphoreType.DMA((n,)), pltpu.SemaphoreType.DMA((n, 3)), pltpu.SemaphoreType.DMA((n, 3))],
    )(*parts)


def sum_chips(name, parts):
    _, r2, cdim = parts.shape
    tr = _tile(r2, 512, 16)

    def body(p_ref, o_ref):
        acc = p_ref[0].astype(F32)
        for k in range(1, N_CHIPS):
            acc = acc + p_ref[k].astype(F32)
        o_ref[...] = acc

    return pl.pallas_call(
        body, name=name, grid=(r2 // tr,), in_specs=[pl.BlockSpec((N_CHIPS, tr, cdim), lambda i: (0, i, 0))],
        out_specs=pl.BlockSpec((tr, cdim), lambda i: (i, 0)), out_shape=S((r2, cdim), F32), compiler_params=_params(1))(parts)


def sibling_join_halves(halves):
    n = len(halves)

    def body(*refs):
        src = refs[:n]
        dst = refs[n:2 * n]
        local_sems, send_sems, recv_sems = refs[2 * n:]
        x, y, c, _ = _place()
        local = [pltpu.make_async_copy(src[i], dst[i].at[_half(c, dst[i].shape[0])], local_sems.at[i]) for i in range(n)]
        for cp in local:
            cp.start()
        copies = [pltpu.make_async_remote_copy(
            src_ref=src[i], dst_ref=dst[i].at[_half(c, dst[i].shape[0])], send_sem=send_sems.at[i],
            recv_sem=recv_sems.at[i], device_id=(x, y, 1 - c), device_id_type=MESH) for i in range(n)]
        for cp in copies:
            cp.start()
        for i in range(n):
            pltpu.make_async_remote_copy(
                src_ref=src[i], dst_ref=dst[i].at[_half(1 - c, dst[i].shape[0])], send_sem=send_sems.at[i],
                recv_sem=recv_sems.at[i], device_id=(x, y, 1 - c), device_id_type=MESH).wait_recv()
        for cp in copies:
            cp.wait_send()
        for cp in local:
            cp.wait()

    return pl.pallas_call(
        body, name="sibling_join_halves", in_specs=[ANY] * n, out_specs=[ANY] * n,
        out_shape=[S((2 * h.shape[0], h.shape[1]), h.dtype) for h in halves],
        scratch_shapes=[pltpu.SemaphoreType.DMA((n,)), pltpu.SemaphoreType.DMA((n,)), pltpu.SemaphoreType.DMA((n,))],
    )(*halves)


def all_reduce_small(name, packed):
    rows, width = packed.shape

    def body(x_ref, o_ref, gathered, send_sems, recv_sems):
        x, y, c, _ = _place()
        me = 4 * x + 2 * y + c
        gathered[me] = x_ref[...]
        flips = [(fx, fy, fc) for fx in (0, 1) for fy in (0, 1) for fc in (0, 1)][1:]

        def copy(r, slot, to):
            return pltpu.make_async_remote_copy(
                src_ref=x_ref, dst_ref=gathered.at[slot], send_sem=send_sems.at[r], recv_sem=recv_sems.at[r],
                device_id=to, device_id_type=MESH)

        def peer(f):
            return (x ^ f[0], y ^ f[1], c ^ f[2])

        sent = [copy(r, me, peer(f)) for r, f in enumerate(flips)]
        for cp in sent:
            cp.start()
        for r, f in enumerate(flips):
            px, py, pc = peer(f)
            copy(r, 4 * px + 2 * py + pc, peer(f)).wait_recv()
        for cp in sent:
            cp.wait_send()
        acc = gathered[0]
        for k in range(1, N_DEV):
            acc = acc + gathered[k]
        o_ref[...] = acc

    vmem = pl.BlockSpec(memory_space=pltpu.VMEM)
    return pl.pallas_call(
        body, name=name, in_specs=[vmem], out_specs=vmem, out_shape=S((rows, width), F32),
        scratch_shapes=[pltpu.VMEM((N_DEV, rows, width), F32), pltpu.SemaphoreType.DMA((N_DEV - 1,)),
                        pltpu.SemaphoreType.DMA((N_DEV - 1,))],
    )(packed)


def _rope_tables(positions):
    inv_freq = 1.0 / (ROPE_THETA ** (jnp.arange(0, ROPE, 2, dtype=F32) / ROPE))
    ang = positions.astype(F32)[:, None] * inv_freq
    return jnp.cos(ang), jnp.sin(ang)


def _unstack_cols(w):
    k4, k, n4 = w.shape
    return jnp.transpose(w, (1, 0, 2)).reshape(k, k4 * n4)


def _stack_cols(w):
    k, n = w.shape
    return jnp.transpose(w.reshape(k, N_CHIPS, n // N_CHIPS), (1, 0, 2))


def kernel(x, positions, mla_norm, mla_w_in, mla_g_cq, mla_g_ckv, mla_w_uq, mla_w_ukv, mla_w_o, conv_norm, conv_w_in, conv_w, conv_w_out, ffn_norm, ffn_w_gate, ffn_w_up, ffn_w_down, final_norm, loss_target, m_mla_norm, m_mla_w_in, m_mla_g_cq, m_mla_g_ckv, m_mla_w_uq, m_mla_w_ukv, m_mla_w_o, m_conv_norm, m_conv_w_in, m_conv_w, m_conv_w_out, m_ffn_norm, m_ffn_w_gate, m_ffn_w_up, m_ffn_w_down, m_final_norm, v_mla_norm, v_mla_w_in, v_mla_g_cq, v_mla_g_ckv, v_mla_w_uq, v_mla_w_ukv, v_mla_w_o, v_conv_norm, v_conv_w_in, v_conv_w, v_conv_w_out, v_ffn_norm, v_ffn_w_gate, v_ffn_w_up, v_ffn_w_down, v_final_norm):
    weights = dict(mla_norm=mla_norm, mla_w_in=mla_w_in, mla_g_cq=mla_g_cq, mla_g_ckv=mla_g_ckv, mla_w_uq=mla_w_uq,
                   mla_w_ukv=mla_w_ukv, mla_w_o=mla_w_o, conv_norm=conv_norm, conv_w_in=conv_w_in, conv_w=conv_w,
                   conv_w_out=conv_w_out, ffn_norm=ffn_norm, ffn_w_gate=ffn_w_gate, ffn_w_up=ffn_w_up,
                   ffn_w_down=ffn_w_down, final_norm=final_norm)
    m_in = dict(mla_norm=m_mla_norm, mla_w_in=m_mla_w_in, mla_g_cq=m_mla_g_cq, mla_g_ckv=m_mla_g_ckv, mla_w_uq=m_mla_w_uq,
                mla_w_ukv=m_mla_w_ukv, mla_w_o=m_mla_w_o, conv_norm=m_conv_norm, conv_w_in=m_conv_w_in, conv_w=m_conv_w,
                conv_w_out=m_conv_w_out, ffn_norm=m_ffn_norm, ffn_w_gate=m_ffn_w_gate, ffn_w_up=m_ffn_w_up,
                ffn_w_down=m_ffn_w_down, final_norm=m_final_norm)
    v_in = dict(mla_norm=v_mla_norm, mla_w_in=v_mla_w_in, mla_g_cq=v_mla_g_cq, mla_g_ckv=v_mla_g_ckv, mla_w_uq=v_mla_w_uq,
                mla_w_ukv=v_mla_w_ukv, mla_w_o=v_mla_w_o, conv_norm=v_conv_norm, conv_w_in=v_conv_w_in, conv_w=v_conv_w,
                conv_w_out=v_conv_w_out, ffn_norm=v_ffn_norm, ffn_w_gate=v_ffn_w_gate, ffn_w_up=v_ffn_w_up,
                ffn_w_down=v_ffn_w_down, final_norm=v_final_norm)
    big = ["mla_w_in", "mla_w_uq", "mla_w_ukv", "mla_w_o", "conv_w_in", "conv_w_out", "ffn_w_gate", "ffn_w_up", "ffn_w_down"]
    order = list(weights)

    t, d = x.shape[1], x.shape[2]
    h0 = x.reshape(t, d)
    target = loss_target.reshape(t, d)
    cos, sin = _rope_tables(positions.reshape(t))

    def rows2d(a):
        return a.reshape(-1, a.shape[-1])

    gathered = dict(zip(big, gather_weight_shards([rows2d(weights[n]).astype(BF16) for n in big])))
    w_in = gathered["mla_w_in"].reshape(-1, gathered["mla_w_in"].shape[-1])
    w_uq = _unstack_cols(gathered["mla_w_uq"])
    w_ukv = _unstack_cols(gathered["mla_w_ukv"])
    w_o = gathered["mla_w_o"].reshape(-1, d)
    cw_in = _unstack_cols(gathered["conv_w_in"])
    cw_out = gathered["conv_w_out"].reshape(-1, d)
    wg_all, wu_all, wd_all = gathered["ffn_w_gate"], gathered["ffn_w_up"], gathered["ffn_w_down"]

    chip = 2 * lax.axis_index("x") + lax.axis_index("y")
    core = lax.axis_index("c")
    d4 = d // N_CHIPS
    first_core = (core == 0).astype(F32)

    def place_shard(shard):
        full = jnp.zeros((shard.shape[0], d), F32)
        return lax.dynamic_update_slice(full, shard * first_core, (0, chip * d4))

    def pack_rows(rows):
        idx = lax.broadcasted_iota(jnp.int32, (SMALL_ROWS, d), 0)
        out = jnp.zeros((SMALL_ROWS, d), F32)
        for r, row in enumerate(rows):
            out = out + jnp.where(idx == r, row, 0.0)
        return out

    cw = place_shard(conv_w.reshape(3, d4))
    pre = all_reduce_small("all_gather_conv_small", pack_rows([place_shard(conv_norm.reshape(1, d4)), cw[0:1], cw[1:2], cw[2:3]]))
    conv_norm_full = pre[0:1]
    conv_w_full = pre[1:4]

    a0 = rms_fwd("mla_norm_fwd", h0, mla_norm)
    proj = linear("mla_in_proj", a0, w_in, F32)
    cq, ckv, kr = mla_mid("mla_mid", proj, mla_g_cq, mla_g_ckv, cos, sin)
    q = linear("mla_q_up", cq, w_uq, F32)
    kv = linear("mla_kv_up", ckv, w_ukv, BF16)
    qh, kh, vh = qkv_heads("qkv_heads", q, kv, kr, cos, sin)
    attn, lse = attention_fwd("attention_fwd", qh, kh, vh)
    h1 = linear("mla_out_proj", attn, w_o, F32, resid=h0)

    def ffn_forward(tag, h, layer):
        a = rms_fwd(f"ffn{tag}_norm_fwd", h, ffn_norm[layer:layer + 1])
        g, u, z = ffn_up(f"ffn{tag}_up", a, wg_all, wu_all, layer)
        return a, g, u, z, ffn_down(f"ffn{tag}_down", z, wd_all, layer, h)

    a1, g0, u0, z0, h2 = ffn_forward(0, h1, 0)
    a2 = rms_fwd("conv_norm_fwd", h2, conv_norm_full)
    bcx = conv_in_proj("conv_in_proj", a2, cw_in)
    yc = conv_fwd("conv_fwd", bcx, conv_w_full)
    h3 = linear("conv_out_proj", yc, cw_out, F32, resid=h2)
    a3, g1, u1, z1, h4 = ffn_forward(1, h3, 1)
    dh4, d_final_norm, loss_local = loss_head("loss_head", h4, final_norm.reshape(1, d), target)

    def ffn_backward(tag, dh, h, layer, a, g, u, z):
        dg, du = ffn_bwd_hidden(f"ffn{tag}_bwd_hidden", dh, wd_all, layer, g, u)
        d_wd = ffn_wgrad_down(f"ffn{tag}_wgrad_down", z, dh)
        da = ffn_bwd_input(f"ffn{tag}_bwd_input", dg, du, wg_all, wu_all, layer, d)
        d_wg, d_wu = ffn_wgrad_up(f"ffn{tag}_wgrad_up", a, dg, du)
        dh_prev, d_norm = rms_bwd(f"ffn{tag}_norm_bwd", h, ffn_norm[layer:layer + 1], da, dh)
        return dh_prev, d_norm, d_wg, d_wu, d_wd

    dh3, d_ffn_norm1, d_wg1, d_wu1, d_wd1 = ffn_backward(1, dh4, h3, 1, a3, g1, u1, z1)

    dyc = linear_nt("conv_out_bwd_input", dh3, cw_out, F32)
    d_cw_out = wgrad("conv_out_wgrad", yc, dh3)
    db, dc, dxp, d_conv_w = conv_bwd("conv_bwd", bcx, conv_w_full, dyc)
    dbcx = jnp.stack([db, dc, dxp])
    da2 = conv_in_bwd_input("conv_in_bwd_input", dbcx, cw_in)
    d_cw_in = conv_in_wgrad("conv_in_wgrad", a2, dbcx)
    dh2, d_conv_norm = rms_bwd("conv_norm_bwd", h2, conv_norm_full, da2, dh3)

    dh1, d_ffn_norm0, d_wg0, d_wu0, d_wd0 = ffn_backward(0, dh2, h1, 0, a1, g0, u0, z0)

    d_attn = linear_nt("mla_out_bwd_input", dh1, w_o, F32)
    d_w_o = wgrad("mla_out_wgrad", attn, dh1)
    delta = attention_delta("attention_delta", d_attn, attn)
    dqh, dkh, dvh = attention_bwd("attention_bwd", qh, kh, vh, d_attn, lse, delta)
    dq, dkv, dkr = qkv_heads_bwd("qkv_heads_bwd", dqh, dkh, dvh, cos, sin)
    dcq = linear_nt("mla_q_up_bwd_input", dq, w_uq, F32)
    d_w_uq = wgrad("mla_q_up_wgrad", cq, dq)
    dckv = linear_nt("mla_kv_up_bwd_input", dkv, w_ukv, F32)
    d_w_ukv = wgrad("mla_kv_up_wgrad", ckv, dkv)
    dproj, d_g_cq, d_g_ckv = mla_mid_bwd("mla_mid_bwd", proj, mla_g_cq, mla_g_ckv, dcq, dckv, dkr, cos, sin)
    da0 = linear_nt("mla_in_bwd_input", dproj, w_in, F32)
    d_w_in = wgrad("mla_in_wgrad", a0, dproj)
    grad_x, d_mla_norm = rms_bwd("mla_norm_bwd", h0, mla_norm, da0, dh1)

    def by_layer(g_l0, g_l1):
        s = jnp.stack([g_l0, g_l1], axis=1)
        return s.reshape(N_CHIPS, -1, s.shape[-1])

    stacked = dict(
        mla_w_in=d_w_in.reshape(N_CHIPS, -1, d_w_in.shape[-1]), mla_w_uq=_stack_cols(d_w_uq), mla_w_ukv=_stack_cols(d_w_ukv),
        mla_w_o=d_w_o.reshape(N_CHIPS, -1, d), conv_w_in=_stack_cols(d_cw_in), conv_w_out=d_cw_out.reshape(N_CHIPS, -1, d),
        ffn_w_gate=by_layer(d_wg0, d_wg1), ffn_w_up=by_layer(d_wu0, d_wu1), ffn_w_down=by_layer(d_wd0, d_wd1))
    local = [stacked[n] for n in big]
    from_sibling = sibling_swap_halves(local)
    pair_sums = [add_halves(f"pair_sum_{n}", g, r) for n, g, r in zip(big, local, from_sibling)]
    from_chips = scatter_to_owner_chips(pair_sums)
    my_halves = [sum_chips(f"chip_sum_{n}", p) for n, p in zip(big, from_chips)]
    grads = dict(zip(big, sibling_join_halves(my_halves)))

    def pad_row(v):
        return jnp.pad(v, ((0, 0), (0, d - v.shape[1])))

    small = all_reduce_small("all_reduce_small_grads", pack_rows([
        d_mla_norm, pad_row(d_g_cq), pad_row(d_g_ckv), d_ffn_norm0, d_ffn_norm1, d_final_norm, d_conv_norm,
        d_conv_w[0:1], d_conv_w[1:2], d_conv_w[2:3], jnp.broadcast_to(loss_local, (1, d))]))
    loss = small[10, 0]
    grads["mla_norm"] = small[0:1]
    grads["mla_g_cq"] = small[1:2, :mla_g_cq.shape[1]]
    grads["mla_g_ckv"] = small[2:3, :mla_g_ckv.shape[1]]
    grads["ffn_norm"] = small[3:5]
    grads["final_norm"] = small[5:6]
    grads["conv_norm"] = lax.dynamic_slice(small[6:7], (0, chip * d4), (1, d4))
    grads["conv_w"] = lax.dynamic_slice(small[7:10], (0, chip * d4), (3, d4))

    outs_g, outs_d, outs_m, outs_v = [], [], [], []
    for n in order:
        w = weights[n]
        delta_w, new_m, new_v = adamw(f"adamw_{n}", rows2d(w), grads[n].reshape(rows2d(w).shape), rows2d(m_in[n]), rows2d(v_in[n]))
        outs_g.append(grads[n].reshape(w.shape))
        outs_d.append(delta_w.reshape(w.shape))
        outs_m.append(new_m.reshape(w.shape))
        outs_v.append(new_v.reshape(w.shape))
    return (loss, grad_x.reshape(x.shape), *outs_g, *outs_d, *outs_m, *outs_v)
```

```python
import math

import jax
import jax.numpy as jnp
from jax import lax
from jax.experimental import pallas as pl
from jax.experimental.pallas import tpu as pltpu

F32 = jnp.float32
BF16 = jnp.bfloat16
S = jax.ShapeDtypeStruct

N_HEADS = 8
NOPE = 128
ROPE = 64
HALF = ROPE // 2
VDIM = 128
QK = NOPE + ROPE
CHUNK = 64
ROPE_THETA = 10000.0
RMS_EPS = 1e-6
ADAM_LR = 0.001
ADAM_B1 = 0.9
ADAM_B2 = 0.999
ADAM_EPS = 1e-08
ADAM_WD = 0.01
ADAM_STEP = 10

N_CHIPS = 4
N_DEV = 8
MASK_VALUE = -1e30
VMEM_LIMIT = 48 * 1024 * 1024
ATT_BLOCK = 512
SMALL_ROWS = 16

_NN = (((1,), (0,)), ((), ()))
_NT = (((1,), (1,)), ((), ()))
_TN = (((0,), (0,)), ((), ()))
MESH = pl.DeviceIdType.MESH
ANY = pl.BlockSpec(memory_space=pl.ANY)


def _params(n_axes):
    return pltpu.CompilerParams(dimension_semantics=("arbitrary",) * n_axes, vmem_limit_bytes=VMEM_LIMIT)


def _tile(n, cap, mult=8):
    for t in range(min(cap, n), 0, -1):
        if n % t == 0 and t % mult == 0:
            return t
    return n


def _sigmoid(x):
    return 1.0 / (1.0 + jnp.exp(-x))


def _mm(name, a_ops, b_ops, products, dims, grid, k_axis, outs, acc_shape, epilogue, extra_ops=()):
    na, nb, ne, no = len(a_ops), len(b_ops), len(extra_ops), len(outs)
    n_acc = 1 + max(c for _, _, c in products)
    nk = 1 if k_axis is None else grid[k_axis]

    def body(*refs):
        a_refs = refs[:na]
        b_refs = refs[na:na + nb]
        e_refs = refs[na + nb:na + nb + ne]
        o_refs = refs[na + nb + ne:na + nb + ne + no]
        acc_refs = refs[na + nb + ne + no:]

        def partial_sums():
            vals = [None] * n_acc
            for ai, bi, ci in products:
                d = lax.dot_general(a_refs[ai][...].astype(BF16), b_refs[bi][...].astype(BF16), dims,
                                    preferred_element_type=F32)
                vals[ci] = d if vals[ci] is None else vals[ci] + d
            return vals

        if nk == 1:
            epilogue(partial_sums(), e_refs, o_refs)
        else:
            k = pl.program_id(k_axis)

            @pl.when(k == 0)
            def _():
                for acc in acc_refs:
                    acc[...] = jnp.zeros_like(acc)

            for acc, v in zip(acc_refs, partial_sums()):
                acc[...] += v

            @pl.when(k == nk - 1)
            def _():
                epilogue([acc[...] for acc in acc_refs], e_refs, o_refs)

    ops = list(a_ops) + list(b_ops) + list(extra_ops)
    return pl.pallas_call(
        body, name=name, grid=grid,
        in_specs=[s for _, s in ops], out_specs=[s for _, s in outs], out_shape=[o for o, _ in outs],
        scratch_shapes=[pltpu.VMEM(acc_shape, F32) for _ in range(n_acc if nk > 1 else 0)],
        compiler_params=_params(len(grid)),
    )(*[a for a, _ in ops])


def _store(accs, e_refs, o_refs):
    o_refs[0][...] = accs[0].astype(o_refs[0].dtype)


def _store_plus_residual(accs, e_refs, o_refs):
    o_refs[0][...] = (e_refs[0][...] + accs[0]).astype(o_refs[0].dtype)


def linear(name, x, w, out_dtype, resid=None):
    t, k = x.shape
    n = w.shape[1]
    tm = _tile(t, 512)
    tn = n if n <= 2048 else _tile(n, 1024, 128)
    extra = [] if resid is None else [(resid, pl.BlockSpec((tm, tn), lambda j, i: (i, j)))]
    return _mm(name, [(x, pl.BlockSpec((tm, k), lambda j, i: (i, 0)))], [(w, pl.BlockSpec((k, tn), lambda j, i: (0, j)))],
               [(0, 0, 0)], _NN, (n // tn, t // tm), None,
               [(S((t, n), out_dtype), pl.BlockSpec((tm, tn), lambda j, i: (i, j)))], None,
               _store if resid is None else _store_plus_residual, extra)[0]


def linear_nt(name, dy, w, out_dtype):
    t, n = dy.shape
    k = w.shape[0]
    tm = _tile(t, 512)
    tc = n if n <= 2048 else _tile(n, 1024, 128)
    return _mm(name, [(dy, pl.BlockSpec((tm, tc), lambda i, c: (i, c)))], [(w, pl.BlockSpec((k, tc), lambda i, c: (0, c)))],
               [(0, 0, 0)], _NT, (t // tm, n // tc), 1,
               [(S((t, k), out_dtype), pl.BlockSpec((tm, k), lambda i, c: (i, 0)))], (tm, k), _store)[0]


def wgrad(name, x, dy):
    t, k = x.shape
    n = dy.shape[1]
    tk = _tile(t, 512)
    tn = n if n <= 1024 else _tile(n, 1024, 128)
    return _mm(name, [(x, pl.BlockSpec((tk, k), lambda j, s: (s, 0)))], [(dy, pl.BlockSpec((tk, tn), lambda j, s: (s, j)))],
               [(0, 0, 0)], _TN, (n // tn, t // tk), 1,
               [(S((k, n), BF16), pl.BlockSpec((k, tn), lambda j, s: (0, j)))], (k, tn), _store)[0]


def ffn_up(name, a, wg_all, wu_all, layer):
    t, d = a.shape
    f4 = wg_all.shape[2]
    tm = _tile(t, 512)
    w_spec = pl.BlockSpec((None, d, f4), lambda k, i: (k, layer, 0))
    h_spec = pl.BlockSpec((None, tm, f4), lambda k, i: (k, i, 0))

    def epilogue(accs, e_refs, o_refs):
        g, u = accs
        o_refs[0][...] = g
        o_refs[1][...] = u
        o_refs[2][...] = (g * _sigmoid(g) * u).astype(BF16)

    return _mm(name, [(a, pl.BlockSpec((tm, d), lambda k, i: (i, 0)))], [(wg_all, w_spec), (wu_all, w_spec)],
               [(0, 0, 0), (0, 1, 1)], _NN, (N_CHIPS, t // tm), None,
               [(S((N_CHIPS, t, f4), F32), h_spec), (S((N_CHIPS, t, f4), F32), h_spec), (S((N_CHIPS, t, f4), BF16), h_spec)],
               None, epilogue)


def ffn_down(name, z, wd_all, layer, resid):
    _, t, f4 = z.shape
    d = wd_all.shape[2]
    tm = _tile(t, 512)
    return _mm(name, [(z, pl.BlockSpec((None, tm, f4), lambda i, k: (k, i, 0)))],
               [(wd_all, pl.BlockSpec((None, f4, d), lambda i, k: (k, layer, 0)))],
               [(0, 0, 0)], _NN, (t // tm, N_CHIPS), 1,
               [(S((t, d), F32), pl.BlockSpec((tm, d), lambda i, k: (i, 0)))], (tm, d), _store_plus_residual,
               [(resid, pl.BlockSpec((tm, d), lambda i, k: (i, 0)))])[0]


def ffn_bwd_hidden(name, dh, wd_all, layer, g, u):
    t, d = dh.shape
    f4 = g.shape[2]
    tm = _tile(t, 512)
    h_spec = pl.BlockSpec((None, tm, f4), lambda k, i: (k, i, 0))

    def epilogue(accs, e_refs, o_refs):
        dz = accs[0]
        gv = e_refs[0][...]
        uv = e_refs[1][...]
        sg = _sigmoid(gv)
        o_refs[0][...] = (dz * uv * (sg * (1.0 + gv * (1.0 - sg)))).astype(BF16)
        o_refs[1][...] = (dz * (gv * sg)).astype(BF16)

    return _mm(name, [(dh, pl.BlockSpec((tm, d), lambda k, i: (i, 0)))],
               [(wd_all, pl.BlockSpec((None, f4, d), lambda k, i: (k, layer, 0)))],
               [(0, 0, 0)], _NT, (N_CHIPS, t // tm), None,
               [(S((N_CHIPS, t, f4), BF16), h_spec), (S((N_CHIPS, t, f4), BF16), h_spec)], None, epilogue,
               [(g, h_spec), (u, h_spec)])


def ffn_bwd_input(name, dg, du, wg_all, wu_all, layer, d):
    _, t, f4 = dg.shape
    tm = _tile(t, 512)
    h_spec = pl.BlockSpec((None, tm, f4), lambda i, k: (k, i, 0))
    w_spec = pl.BlockSpec((None, d, f4), lambda i, k: (k, layer, 0))
    return _mm(name, [(dg, h_spec), (du, h_spec)], [(wg_all, w_spec), (wu_all, w_spec)],
               [(0, 0, 0), (1, 1, 0)], _NT, (t // tm, N_CHIPS), 1,
               [(S((t, d), F32), pl.BlockSpec((tm, d), lambda i, k: (i, 0)))], (tm, d), _store)[0]


def ffn_wgrad_up(name, a, dg, du):
    t, d = a.shape
    f4 = dg.shape[2]
    tk = _tile(t, 512)
    h_spec = pl.BlockSpec((None, tk, f4), lambda k, s: (k, s, 0))
    o_spec = pl.BlockSpec((None, d, f4), lambda k, s: (k, 0, 0))

    def epilogue(accs, e_refs, o_refs):
        o_refs[0][...] = accs[0].astype(BF16)
        o_refs[1][...] = accs[1].astype(BF16)

    return _mm(name, [(a, pl.BlockSpec((tk, d), lambda k, s: (s, 0)))], [(dg, h_spec), (du, h_spec)],
               [(0, 0, 0), (0, 1, 1)], _TN, (N_CHIPS, t // tk), 1,
               [(S((N_CHIPS, d, f4), BF16), o_spec), (S((N_CHIPS, d, f4), BF16), o_spec)], (d, f4), epilogue)


def ffn_wgrad_down(name, z, dh):
    _, t, f4 = z.shape
    d = dh.shape[1]
    tk = _tile(t, 512)
    return _mm(name, [(z, pl.BlockSpec((None, tk, f4), lambda k, s: (k, s, 0)))], [(dh, pl.BlockSpec((tk, d), lambda k, s: (s, 0)))],
               [(0, 0, 0)], _TN, (N_CHIPS, t // tk), 1,
               [(S((N_CHIPS, f4, d), BF16), pl.BlockSpec((None, f4, d), lambda k, s: (k, 0, 0)))], (f4, d), _store)[0]


def conv_in_proj(name, a, w):
    t, d = a.shape
    tm = _tile(t, 512)
    return _mm(name, [(a, pl.BlockSpec((tm, d), lambda j, i: (i, 0)))], [(w, pl.BlockSpec((d, d), lambda j, i: (0, j)))],
               [(0, 0, 0)], _NN, (3, t // tm), None,
               [(S((3, t, d), F32), pl.BlockSpec((None, tm, d), lambda j, i: (j, i, 0)))], None, _store)[0]


def conv_in_bwd_input(name, dbcx, w):
    _, t, d = dbcx.shape
    tm = _tile(t, 512)
    return _mm(name, [(dbcx, pl.BlockSpec((None, tm, d), lambda i, j: (j, i, 0)))], [(w, pl.BlockSpec((d, d), lambda i, j: (0, j)))],
               [(0, 0, 0)], _NT, (t // tm, 3), 1,
               [(S((t, d), F32), pl.BlockSpec((tm, d), lambda i, j: (i, 0)))], (tm, d), _store)[0]


def conv_in_wgrad(name, a, dbcx):
    t, d = a.shape
    tk = _tile(t, 512)
    return _mm(name, [(a, pl.BlockSpec((tk, d), lambda j, s: (s, 0)))], [(dbcx, pl.BlockSpec((None, tk, d), lambda j, s: (j, s, 0)))],
               [(0, 0, 0)], _TN, (3, t // tk), 1,
               [(S((d, 3 * d), BF16), pl.BlockSpec((d, d), lambda j, s: (0, j)))], (d, d), _store)[0]


def _rstd(x):
    return lax.rsqrt(jnp.mean(x * x, axis=-1, keepdims=True) + RMS_EPS)


def _rms_bwd(x, g, dy):
    r = _rstd(x)
    xhat = x * r
    dgain = jnp.sum(dy * xhat, axis=0, keepdims=True)
    dxh = dy * g
    dx = r * (dxh - xhat * jnp.mean(dxh * xhat, axis=-1, keepdims=True))
    return dx, dgain


def rms_fwd(name, h, g):
    t, d = h.shape
    tr = _tile(t, 512)

    def body(h_ref, g_ref, a_ref):
        x = h_ref[...]
        a_ref[...] = (x * _rstd(x) * g_ref[...]).astype(BF16)

    return pl.pallas_call(
        body, name=name, grid=(t // tr,),
        in_specs=[pl.BlockSpec((tr, d), lambda i: (i, 0)), pl.BlockSpec((1, d), lambda i: (0, 0))],
        out_specs=pl.BlockSpec((tr, d), lambda i: (i, 0)), out_shape=S((t, d), BF16), compiler_params=_params(1))(h, g)


def rms_bwd(name, h, g, da, dh_in):
    t, d = h.shape
    tr = _tile(t, 512)

    def body(h_ref, g_ref, da_ref, dhi_ref, dho_ref, dg_ref):
        dx, dgain = _rms_bwd(h_ref[...], g_ref[...], da_ref[...])
        dho_ref[...] = dhi_ref[...] + dx

        @pl.when(pl.program_id(0) == 0)
        def _():
            dg_ref[...] = jnp.zeros_like(dg_ref)

        dg_ref[...] += dgain

    row = pl.BlockSpec((tr, d), lambda i: (i, 0))
    vec = pl.BlockSpec((1, d), lambda i: (0, 0))
    return pl.pallas_call(
        body, name=name, grid=(t // tr,), in_specs=[row, vec, row, row], out_specs=[row, vec],
        out_shape=[S((t, d), F32), S((1, d), F32)], compiler_params=_params(1))(h, g, da, dh_in)


def loss_head(name, h, g, target):
    t, d = h.shape
    tr = _tile(t, 512)

    def body(h_ref, g_ref, t_ref, dh_ref, dg_ref, loss_ref):
        x = h_ref[...]
        g = g_ref[...]
        r = _rstd(x)
        xhat = x * r
        err = xhat * g - t_ref[...]
        dy = err * (1.0 / d)
        dxh = dy * g
        dh_ref[...] = r * (dxh - xhat * jnp.mean(dxh * xhat, axis=-1, keepdims=True))

        @pl.when(pl.program_id(0) == 0)
        def _():
            dg_ref[...] = jnp.zeros_like(dg_ref)
            loss_ref[...] = jnp.zeros_like(loss_ref)

        dg_ref[...] += jnp.sum(dy * xhat, axis=0, keepdims=True)
        per_token = jnp.mean(err * err, axis=-1, keepdims=True)
        loss_ref[...] += 0.5 * jnp.sum(per_token, axis=0, keepdims=True)

    row = pl.BlockSpec((tr, d), lambda i: (i, 0))
    vec = pl.BlockSpec((1, d), lambda i: (0, 0))
    one = pl.BlockSpec((1, 1), lambda i: (0, 0))
    return pl.pallas_call(
        body, name=name, grid=(t // tr,), in_specs=[row, vec, row], out_specs=[row, vec, one],
        out_shape=[S((t, d), F32), S((1, d), F32), S((1, 1), F32)], compiler_params=_params(1))(h, g, target)


def mla_mid(name, proj, g_cq, g_ckv, cos, sin):
    t, n = proj.shape
    ql, kl = g_cq.shape[1], g_ckv.shape[1]
    tr = _tile(t, 512)

    def body(p_ref, gq_ref, gk_ref, c_ref, s_ref, cq_ref, ckv_ref, kr_ref):
        xq = p_ref[:, 0:ql]
        cq_ref[...] = (xq * _rstd(xq) * gq_ref[...]).astype(BF16)
        xk = p_ref[:, ql:ql + kl]
        ckv_ref[...] = (xk * _rstd(xk) * gk_ref[...]).astype(BF16)
        k1 = p_ref[:, ql + kl:ql + kl + HALF]
        k2 = p_ref[:, ql + kl + HALF:ql + kl + ROPE]
        c = c_ref[...]
        s = s_ref[...]
        kr_ref[:, 0:HALF] = k1 * c - k2 * s
        kr_ref[:, HALF:ROPE] = k1 * s + k2 * c

    def row(w):
        return pl.BlockSpec((tr, w), lambda i: (i, 0))

    def vec(w):
        return pl.BlockSpec((1, w), lambda i: (0, 0))

    return pl.pallas_call(
        body, name=name, grid=(t // tr,), in_specs=[row(n), vec(ql), vec(kl), row(HALF), row(HALF)],
        out_specs=[row(ql), row(kl), row(ROPE)], out_shape=[S((t, ql), BF16), S((t, kl), BF16), S((t, ROPE), F32)],
        compiler_params=_params(1))(proj, g_cq, g_ckv, cos, sin)


def mla_mid_bwd(name, proj, g_cq, g_ckv, dcq, dckv, dkr, cos, sin):
    t, n = proj.shape
    ql, kl = g_cq.shape[1], g_ckv.shape[1]
    tr = _tile(t, 512)

    def body(p_ref, gq_ref, gk_ref, dcq_ref, dckv_ref, dkr_ref, c_ref, s_ref, dp_ref, dgq_ref, dgk_ref):
        dxq, dgq = _rms_bwd(p_ref[:, 0:ql], gq_ref[...], dcq_ref[...])
        dp_ref[:, 0:ql] = dxq.astype(BF16)
        dxk, dgk = _rms_bwd(p_ref[:, ql:ql + kl], gk_ref[...], dckv_ref[...])
        dp_ref[:, ql:ql + kl] = dxk.astype(BF16)
        d1 = dkr_ref[:, 0:HALF]
        d2 = dkr_ref[:, HALF:ROPE]
        c = c_ref[...]
        s = s_ref[...]
        dp_ref[:, ql + kl:ql + kl + HALF] = (d1 * c + d2 * s).astype(BF16)
        dp_ref[:, ql + kl + HALF:ql + kl + ROPE] = (d2 * c - d1 * s).astype(BF16)

        @pl.when(pl.program_id(0) == 0)
        def _():
            dgq_ref[...] = jnp.zeros_like(dgq_ref)
            dgk_ref[...] = jnp.zeros_like(dgk_ref)

        dgq_ref[...] += dgq
        dgk_ref[...] += dgk

    def row(w):
        return pl.BlockSpec((tr, w), lambda i: (i, 0))

    def vec(w):
        return pl.BlockSpec((1, w), lambda i: (0, 0))

    return pl.pallas_call(
        body, name=name, grid=(t // tr,),
        in_specs=[row(n), vec(ql), vec(kl), row(ql), row(kl), row(ROPE), row(HALF), row(HALF)],
        out_specs=[row(n), vec(ql), vec(kl)], out_shape=[S((t, n), BF16), S((1, ql), F32), S((1, kl), F32)],
        compiler_params=_params(1))(proj, g_cq, g_ckv, dcq, dckv, dkr, cos, sin)


def qkv_heads(name, q, kv, kr, cos, sin):
    t = q.shape[0]
    tr = _tile(t, 256)

    def body(q_ref, kv_ref, kr_ref, c_ref, s_ref, qo_ref, ko_ref, vo_ref):
        c = c_ref[...]
        s = s_ref[...]
        krb = kr_ref[...].astype(BF16)
        for h in range(N_HEADS):
            q0 = h * QK
            qo_ref[h, :, 0:NOPE] = q_ref[:, q0:q0 + NOPE].astype(BF16)
            q1 = q_ref[:, q0 + NOPE:q0 + NOPE + HALF]
            q2 = q_ref[:, q0 + NOPE + HALF:q0 + QK]
            qo_ref[h, :, NOPE:NOPE + HALF] = (q1 * c - q2 * s).astype(BF16)
            qo_ref[h, :, NOPE + HALF:QK] = (q1 * s + q2 * c).astype(BF16)
            k0 = h * (NOPE + VDIM)
            ko_ref[h, :, 0:NOPE] = kv_ref[:, k0:k0 + NOPE]
            ko_ref[h, :, NOPE:QK] = krb
            vo_ref[h] = kv_ref[:, k0 + NOPE:k0 + NOPE + VDIM]

    def row(w):
        return pl.BlockSpec((tr, w), lambda i: (i, 0))

    def heads(w):
        return pl.BlockSpec((N_HEADS, tr, w), lambda i: (0, i, 0))

    return pl.pallas_call(
        body, name=name, grid=(t // tr,),
        in_specs=[row(N_HEADS * QK), row(N_HEADS * (NOPE + VDIM)), row(ROPE), row(HALF), row(HALF)],
        out_specs=[heads(QK), heads(QK), heads(VDIM)],
        out_shape=[S((N_HEADS, t, QK), BF16), S((N_HEADS, t, QK), BF16), S((N_HEADS, t, VDIM), BF16)],
        compiler_params=_params(1))(q, kv, kr, cos, sin)


def qkv_heads_bwd(name, dq_h, dk_h, dv_h, cos, sin):
    t = dq_h.shape[1]
    tr = _tile(t, 256)

    def body(dq_ref, dk_ref, dv_ref, c_ref, s_ref, q_ref, kv_ref, kr_ref):
        c = c_ref[...]
        s = s_ref[...]
        dkr = jnp.zeros((tr, ROPE), F32)
        for h in range(N_HEADS):
            q0 = h * QK
            q_ref[:, q0:q0 + NOPE] = dq_ref[h, :, 0:NOPE].astype(BF16)
            d1 = dq_ref[h, :, NOPE:NOPE + HALF]
            d2 = dq_ref[h, :, NOPE + HALF:QK]
            q_ref[:, q0 + NOPE:q0 + NOPE + HALF] = (d1 * c + d2 * s).astype(BF16)
            q_ref[:, q0 + NOPE + HALF:q0 + QK] = (d2 * c - d1 * s).astype(BF16)
            k0 = h * (NOPE + VDIM)
            kv_ref[:, k0:k0 + NOPE] = dk_ref[h, :, 0:NOPE].astype(BF16)
            kv_ref[:, k0 + NOPE:k0 + NOPE + VDIM] = dv_ref[h].astype(BF16)
            dkr = dkr + dk_ref[h, :, NOPE:QK]
        kr_ref[...] = dkr

    def row(w):
        return pl.BlockSpec((tr, w), lambda i: (i, 0))

    def heads(w):
        return pl.BlockSpec((N_HEADS, tr, w), lambda i: (0, i, 0))

    return pl.pallas_call(
        body, name=name, grid=(t // tr,),
        in_specs=[heads(QK), heads(QK), heads(VDIM), row(HALF), row(HALF)],
        out_specs=[row(N_HEADS * QK), row(N_HEADS * (NOPE + VDIM)), row(ROPE)],
        out_shape=[S((t, N_HEADS * QK), BF16), S((t, N_HEADS * (NOPE + VDIM)), BF16), S((t, ROPE), F32)],
        compiler_params=_params(1))(dq_h, dk_h, dv_h, cos, sin)


def _chunk_mask(q_start, k_start, bq, bk):
    qc = (q_start + lax.broadcasted_iota(jnp.int32, (bq, bk), 0)) // CHUNK
    kc = (k_start + lax.broadcasted_iota(jnp.int32, (bq, bk), 1)) // CHUNK
    return kc <= qc


def attention_fwd(name, q, k, v):
    nh, t, _ = q.shape
    blk = ATT_BLOCK
    scale = 1.0 / math.sqrt(QK)

    def body(q_ref, k_ref, v_ref, o_ref, lse_ref, m_ref, l_ref, acc_ref):
        i = pl.program_id(1)
        qv = q_ref[...]
        m_ref[...] = jnp.full_like(m_ref, MASK_VALUE)
        l_ref[...] = jnp.zeros_like(l_ref)
        acc_ref[...] = jnp.zeros_like(acc_ref)

        def block(j, masked):
            start = pl.multiple_of(j * blk, blk)
            kb = k_ref[pl.ds(start, blk), :]
            vb = v_ref[pl.ds(start, blk), :]
            s = lax.dot_general(qv, kb, _NT, preferred_element_type=F32) * scale
            if masked:
                s = jnp.where(_chunk_mask(i * blk, j * blk, blk, blk), s, MASK_VALUE)
            m_old = m_ref[...]
            m_new = jnp.maximum(m_old, jnp.max(s, axis=-1, keepdims=True))
            p = jnp.exp(s - m_new)
            alpha = jnp.exp(m_old - m_new)
            l_ref[...] = alpha * l_ref[...] + jnp.sum(p, axis=-1, keepdims=True)
            acc_ref[...] = alpha * acc_ref[...] + jnp.dot(p.astype(BF16), vb, preferred_element_type=F32)
            m_ref[...] = m_new

        def step(j, carry):
            block(j, False)
            return carry

        lax.fori_loop(0, i, step, 0)
        block(i, True)
        l = l_ref[...]
        o_ref[...] = acc_ref[...] / l
        lse_ref[...] = m_ref[...] + jnp.log(l)

    return pl.pallas_call(
        body, name=name, grid=(nh, t // blk),
        in_specs=[pl.BlockSpec((None, blk, QK), lambda h, i: (h, i, 0)), pl.BlockSpec((None, t, QK), lambda h, i: (h, 0, 0)),
                  pl.BlockSpec((None, t, VDIM), lambda h, i: (h, 0, 0))],
        out_specs=[pl.BlockSpec((blk, VDIM), lambda h, i: (i, h)), pl.BlockSpec((None, blk, 1), lambda h, i: (h, i, 0))],
        out_shape=[S((t, nh * VDIM), F32), S((nh, t, 1), F32)],
        scratch_shapes=[pltpu.VMEM((blk, 1), F32), pltpu.VMEM((blk, 1), F32), pltpu.VMEM((blk, VDIM), F32)],
        compiler_params=_params(2))(q, k, v)


def attention_delta(name, do, o):
    t = do.shape[0]
    tr = _tile(t, 512)

    def body(do_ref, o_ref, d_ref):
        d_ref[...] = jnp.sum(do_ref[...] * o_ref[...], axis=-1, keepdims=True)

    blk = pl.BlockSpec((tr, VDIM), lambda h, i: (i, h))
    return pl.pallas_call(
        body, name=name, grid=(N_HEADS, t // tr), in_specs=[blk, blk],
        out_specs=pl.BlockSpec((None, tr, 1), lambda h, i: (h, i, 0)), out_shape=S((N_HEADS, t, 1), F32),
        compiler_params=_params(2))(do, o)


def attention_bwd(name, q, k, v, do, lse, delta):
    nh, t, _ = q.shape
    blk = ATT_BLOCK
    nq = t // blk
    scale = 1.0 / math.sqrt(QK)

    def body(q_ref, k_ref, v_ref, do_ref, lse_ref, dl_ref, dq_ref, dk_ref, dv_ref):
        j = pl.program_id(1)

        @pl.when(j == 0)
        def _():
            dq_ref[...] = jnp.zeros_like(dq_ref)

        kb = k_ref[...]
        vb = v_ref[...]
        dk_ref[...] = jnp.zeros_like(dk_ref)
        dv_ref[...] = jnp.zeros_like(dv_ref)

        def block(i, masked):
            rows = pl.ds(pl.multiple_of(i * blk, blk), blk)
            qb = q_ref[rows, :]
            dob = do_ref[rows, :].astype(BF16)
            s = lax.dot_general(qb, kb, _NT, preferred_element_type=F32) * scale
            if masked:
                s = jnp.where(_chunk_mask(i * blk, j * blk, blk, blk), s, MASK_VALUE)
            p = jnp.exp(s - lse_ref[rows, :])
            dp = lax.dot_general(dob, vb, _NT, preferred_element_type=F32)
            ds = (p * (dp - dl_ref[rows, :]) * scale).astype(BF16)
            dv_ref[...] += lax.dot_general(p.astype(BF16), dob, _TN, preferred_element_type=F32)
            dk_ref[...] += lax.dot_general(ds, qb, _TN, preferred_element_type=F32)
            dq_ref[rows, :] += jnp.dot(ds, kb, preferred_element_type=F32)

        block(j, True)

        def step(i, carry):
            block(i, False)
            return carry

        lax.fori_loop(j + 1, nq, step, 0)

    head_all = lambda w: pl.BlockSpec((None, t, w), lambda h, j: (h, 0, 0))
    head_blk = lambda w: pl.BlockSpec((None, blk, w), lambda h, j: (h, j, 0))
    return pl.pallas_call(
        body, name=name, grid=(nh, nq),
        in_specs=[head_all(QK), head_blk(QK), head_blk(VDIM), pl.BlockSpec((t, VDIM), lambda h, j: (0, h)), head_all(1), head_all(1)],
        out_specs=[head_all(QK), head_blk(QK), head_blk(VDIM)],
        out_shape=[S((nh, t, QK), F32), S((nh, t, QK), F32), S((nh, t, VDIM), F32)],
        compiler_params=_params(2))(q, k, v, do, lse, delta)


def _shift_down(u, s):
    rows = lax.broadcasted_iota(jnp.int32, u.shape, 0)
    return jnp.where(rows >= s, pltpu.roll(u, s, 0), 0.0)


def _shift_up(u, s):
    n = u.shape[0]
    rows = lax.broadcasted_iota(jnp.int32, u.shape, 0)
    return jnp.where(rows < n - s, pltpu.roll(u, n - s, 0), 0.0)


def _conv_specs(t, d, lanes):
    slab = lambda part: pl.BlockSpec((None, t, lanes), lambda j, part=part: (part, 0, j))
    return slab, pl.BlockSpec((3, lanes), lambda j: (0, j)), pl.BlockSpec((t, lanes), lambda j: (0, j))


def conv_fwd(name, bcx, w):
    _, t, d = bcx.shape
    lanes = _tile(d, 128, 128)
    slab, w_spec, col = _conv_specs(t, d, lanes)

    def body(b_ref, c_ref, x_ref, w_ref, y_ref):
        u = c_ref[...] * x_ref[...]
        uc = w_ref[0:1, :] * _shift_down(u, 2) + w_ref[1:2, :] * _shift_down(u, 1) + w_ref[2:3, :] * u
        y_ref[...] = (b_ref[...] * uc).astype(BF16)

    return pl.pallas_call(
        body, name=name, grid=(d // lanes,), in_specs=[slab(0), slab(1), slab(2), w_spec], out_specs=col,
        out_shape=S((t, d), BF16), compiler_params=_params(1))(bcx, bcx, bcx, w)


def conv_bwd(name, bcx, w, dy):
    _, t, d = bcx.shape
    lanes = _tile(d, 128, 128)
    slab, w_spec, col = _conv_specs(t, d, lanes)

    def body(b_ref, c_ref, x_ref, w_ref, dy_ref, db_ref, dc_ref, dx_ref, dw_ref):
        c = c_ref[...]
        x = x_ref[...]
        dyv = dy_ref[...]
        u = c * x
        u1 = _shift_down(u, 1)
        u2 = _shift_down(u, 2)
        w0, w1, w2 = w_ref[0:1, :], w_ref[1:2, :], w_ref[2:3, :]
        db_ref[...] = (dyv * (w0 * u2 + w1 * u1 + w2 * u)).astype(BF16)
        duc = dyv * b_ref[...]
        dw_ref[0:1, :] = jnp.sum(duc * u2, axis=0, keepdims=True)
        dw_ref[1:2, :] = jnp.sum(duc * u1, axis=0, keepdims=True)
        dw_ref[2:3, :] = jnp.sum(duc * u, axis=0, keepdims=True)
        du = w2 * duc + w1 * _shift_up(duc, 1) + w0 * _shift_up(duc, 2)
        dc_ref[...] = (du * x).astype(BF16)
        dx_ref[...] = (du * c).astype(BF16)

    return pl.pallas_call(
        body, name=name, grid=(d // lanes,), in_specs=[slab(0), slab(1), slab(2), w_spec, col],
        out_specs=[col, col, col, w_spec], out_shape=[S((t, d), BF16)] * 3 + [S((3, d), F32)],
        compiler_params=_params(1))(bcx, bcx, bcx, w, dy)


def adamw(name, w, g, m, v):
    r, c = w.shape
    tr = _tile(r, 512)

    def body(w_ref, g_ref, m_ref, v_ref, d_ref, mo_ref, vo_ref):
        gv = g_ref[...]
        m_new = ADAM_B1 * m_ref[...] + (1.0 - ADAM_B1) * gv
        v_new = ADAM_B2 * v_ref[...] + (1.0 - ADAM_B2) * (gv * gv)
        m_hat = m_new / (1.0 - ADAM_B1 ** ADAM_STEP)
        v_hat = v_new / (1.0 - ADAM_B2 ** ADAM_STEP)
        d_ref[...] = -ADAM_LR * (m_hat / (jnp.sqrt(v_hat) + ADAM_EPS) + ADAM_WD * w_ref[...])
        mo_ref[...] = m_new
        vo_ref[...] = v_new

    blk = pl.BlockSpec((tr, c), lambda i: (i, 0))
    return pl.pallas_call(
        body, name=name, grid=(r // tr,), in_specs=[blk] * 4, out_specs=[blk] * 3, out_shape=[S((r, c), F32)] * 3,
        compiler_params=_params(1))(w, g, m, v)


def _place():
    x, y, c = lax.axis_index("x"), lax.axis_index("y"), lax.axis_index("c")
    other_chips = [(1 - x, y), (x, 1 - y), (1 - x, 1 - y)]
    return x, y, c, other_chips


def _half(c, rows):
    return pl.ds(pl.multiple_of(c * (rows // 2), 16), rows // 2)


def gather_weight_shards(shards):
    n = len(shards)

    def body(*refs):
        src = refs[:n]
        dst = refs[n:2 * n]
        send_sems, recv_sems = refs[2 * n:]
        x, y, c, chips = _place()
        me = 2 * x + y
        sibling = (x, y, 1 - c)

        def copy(i, slot, half_of, sem, to, from_input=False):
            rows = _half(half_of, src[i].shape[0])
            return pltpu.make_async_remote_copy(
                src_ref=src[i].at[rows] if from_input else dst[i].at[slot, rows], dst_ref=dst[i].at[slot, rows],
                send_sem=send_sems.at[i, sem], recv_sem=recv_sems.at[i, sem], device_id=to, device_id_type=MESH)

        sent = []
        for i in range(n):
            for j, chip in enumerate(chips):
                sent.append(copy(i, me, c, j, (*chip, c), from_input=True))
                sent[-1].start()
        for i in range(n):
            for j, (px, py) in enumerate(chips):
                copy(i, 2 * px + py, c, j, sibling).wait_recv()
                sent.append(copy(i, 2 * px + py, c, 3 + j, sibling))
                sent[-1].start()
        for i in range(n):
            for j, (px, py) in enumerate(chips):
                copy(i, 2 * px + py, 1 - c, 3 + j, sibling).wait_recv()
        for cp in sent:
            cp.wait_send()

    outs = pl.pallas_call(
        body, name="gather_weight_shards", in_specs=[ANY] * n, out_specs=[ANY] * n,
        out_shape=[S((N_CHIPS,) + s.shape, s.dtype) for s in shards],
        scratch_shapes=[pltpu.SemaphoreType.DMA((n, 6)), pltpu.SemaphoreType.DMA((n, 6))],
    )(*shards)
    me = 2 * lax.axis_index("x") + lax.axis_index("y")
    return [lax.dynamic_update_slice(o, s[None], (me, 0, 0)) for o, s in zip(outs, shards)]


def sibling_swap_halves(grads):
    n = len(grads)

    def body(*refs):
        src = refs[:n]
        dst = refs[n:2 * n]
        send_sems, recv_sems = refs[2 * n:]
        x, y, c, _ = _place()
        copies = [pltpu.make_async_remote_copy(
            src_ref=src[i].at[:, _half(1 - c, src[i].shape[1]), :], dst_ref=dst[i], send_sem=send_sems.at[i],
            recv_sem=recv_sems.at[i], device_id=(x, y, 1 - c), device_id_type=MESH) for i in range(n)]
        for cp in copies:
            cp.start()
        for cp in copies:
            cp.wait()

    return pl.pallas_call(
        body, name="sibling_swap_halves", in_specs=[ANY] * n, out_specs=[ANY] * n,
        out_shape=[S((g.shape[0], g.shape[1] // 2, g.shape[2]), g.dtype) for g in grads],
        scratch_shapes=[pltpu.SemaphoreType.DMA((n,)), pltpu.SemaphoreType.DMA((n,))],
    )(*grads)


def add_halves(name, g, rx):
    _, r, cdim = g.shape
    r2 = r // 2
    tr = _tile(r2, 512, 16)
    nb = r2 // tr

    def body(lo_ref, hi_ref, rx_ref, o_ref):
        mine = jnp.where(lax.axis_index("c") == 0, lo_ref[...], hi_ref[...])
        o_ref[...] = (mine.astype(F32) + rx_ref[...].astype(F32)).astype(BF16)

    half = pl.BlockSpec((None, tr, cdim), lambda k, i: (k, i, 0))
    return pl.pallas_call(
        body, name=name, grid=(N_CHIPS, nb),
        in_specs=[half, pl.BlockSpec((None, tr, cdim), lambda k, i: (k, nb + i, 0)), half],
        out_specs=half, out_shape=S((N_CHIPS, r2, cdim), BF16), compiler_params=_params(2))(g, g, rx)


def scatter_to_owner_chips(parts):
    n = len(parts)

    def body(*refs):
        src = refs[:n]
        dst = refs[n:2 * n]
        send_sems, recv_sems = refs[2 * n:]
        x, y, c, chips = _place()
        me = 2 * x + y

        def copy(i, j, from_slot, to_slot, to):
            return pltpu.make_async_remote_copy(
                src_ref=src[i].at[from_slot], dst_ref=dst[i].at[to_slot], send_sem=send_sems.at[i, j],
                recv_sem=recv_sems.at[i, j], device_id=to, device_id_type=MESH)

        sent = []
        for i in range(n):
            for j, (px, py) in enumerate(chips):
                sent.append(copy(i, j, 2 * px + py, me, (px, py, c)))
                sent[-1].start()
        for i in range(n):
            for j, (px, py) in enumerate(chips):
                copy(i, j, me, 2 * px + py, (px, py, c)).wait_recv()
        for cp in sent:
            cp.wait_send()

    outs = pl.pallas_call(
        body, name="scatter_to_owner_chips", in_specs=[ANY] * n, out_specs=[ANY] * n,
        out_shape=[S(p.shape, p.dtype) for p in parts],
        scratch_shapes=[pltpu.SemaphoreType.DMA((n, 3)), pltpu.SemaphoreType.DMA((n, 3))],
    )(*parts)
    me = 2 * lax.axis_index("x") + lax.axis_index("y")
    return [lax.dynamic_update_slice(o, lax.dynamic_slice(p, (me, 0, 0), (1,) + p.shape[1:]), (me, 0, 0))
            for o, p in zip(outs, parts)]


def sum_chips(name, parts):
    _, r2, cdim = parts.shape
    tr = _tile(r2, 512, 16)

    def body(p_ref, o_ref):
        acc = p_ref[0].astype(F32)
        for k in range(1, N_CHIPS):
            acc = acc + p_ref[k].astype(F32)
        o_ref[...] = acc

    return pl.pallas_call(
        body, name=name, grid=(r2 // tr,), in_specs=[pl.BlockSpec((N_CHIPS, tr, cdim), lambda i: (0, i, 0))],
        out_specs=pl.BlockSpec((tr, cdim), lambda i: (i, 0)), out_shape=S((r2, cdim), F32), compiler_params=_params(1))(parts)


def sibling_join_halves(halves):
    n = len(halves)

    def body(*refs):
        src = refs[:n]
        dst = refs[n:2 * n]
        send_sems, recv_sems = refs[2 * n:]
        x, y, c, _ = _place()
        copies = [pltpu.make_async_remote_copy(
            src_ref=src[i], dst_ref=dst[i].at[_half(c, dst[i].shape[0])], send_sem=send_sems.at[i],
            recv_sem=recv_sems.at[i], device_id=(x, y, 1 - c), device_id_type=MESH) for i in range(n)]
        for cp in copies:
            cp.start()
        for i in range(n):
            pltpu.make_async_remote_copy(
                src_ref=src[i], dst_ref=dst[i].at[_half(1 - c, dst[i].shape[0])], send_sem=send_sems.at[i],
                recv_sem=recv_sems.at[i], device_id=(x, y, 1 - c), device_id_type=MESH).wait_recv()
        for cp in copies:
            cp.wait_send()

    outs = pl.pallas_call(
        body, name="sibling_join_halves", in_specs=[ANY] * n, out_specs=[ANY] * n,
        out_shape=[S((2 * h.shape[0], h.shape[1]), h.dtype) for h in halves],
        scratch_shapes=[pltpu.SemaphoreType.DMA((n,)), pltpu.SemaphoreType.DMA((n,))],
    )(*halves)
    c = lax.axis_index("c")
    return [lax.dynamic_update_slice(o, h, (c * h.shape[0], 0)) for o, h in zip(outs, halves)]


def all_reduce_small(name, packed):
    rows, width = packed.shape

    def body(x_ref, o_ref, gathered, send_sems, recv_sems):
        x, y, c, _ = _place()
        me = 4 * x + 2 * y + c
        gathered[me] = x_ref[...]
        flips = [(fx, fy, fc) for fx in (0, 1) for fy in (0, 1) for fc in (0, 1)][1:]

        def copy(r, slot, to):
            return pltpu.make_async_remote_copy(
                src_ref=x_ref, dst_ref=gathered.at[slot], send_sem=send_sems.at[r], recv_sem=recv_sems.at[r],
                device_id=to, device_id_type=MESH)

        def peer(f):
            return (x ^ f[0], y ^ f[1], c ^ f[2])

        sent = [copy(r, me, peer(f)) for r, f in enumerate(flips)]
        for cp in sent:
            cp.start()
        for r, f in enumerate(flips):
            px, py, pc = peer(f)
            copy(r, 4 * px + 2 * py + pc, peer(f)).wait_recv()
        for cp in sent:
            cp.wait_send()
        acc = gathered[0]
        for k in range(1, N_DEV):
            acc = acc + gathered[k]
        o_ref[...] = acc

    vmem = pl.BlockSpec(memory_space=pltpu.VMEM)
    return pl.pallas_call(
        body, name=name, in_specs=[vmem], out_specs=vmem, out_shape=S((rows, width), F32),
        scratch_shapes=[pltpu.VMEM((N_DEV, rows, width), F32), pltpu.SemaphoreType.DMA((N_DEV - 1,)),
                        pltpu.SemaphoreType.DMA((N_DEV - 1,))],
    )(packed)


def _rope_tables(positions):
    inv_freq = 1.0 / (ROPE_THETA ** (jnp.arange(0, ROPE, 2, dtype=F32) / ROPE))
    ang = positions.astype(F32)[:, None] * inv_freq
    return jnp.cos(ang), jnp.sin(ang)


def _unstack_cols(w):
    k4, k, n4 = w.shape
    return jnp.transpose(w, (1, 0, 2)).reshape(k, k4 * n4)


def _stack_cols(w):
    k, n = w.shape
    return jnp.transpose(w.reshape(k, N_CHIPS, n // N_CHIPS), (1, 0, 2))


def kernel(x, positions, mla_norm, mla_w_in, mla_g_cq, mla_g_ckv, mla_w_uq, mla_w_ukv, mla_w_o, conv_norm, conv_w_in, conv_w, conv_w_out, ffn_norm, ffn_w_gate, ffn_w_up, ffn_w_down, final_norm, loss_target, m_mla_norm, m_mla_w_in, m_mla_g_cq, m_mla_g_ckv, m_mla_w_uq, m_mla_w_ukv, m_mla_w_o, m_conv_norm, m_conv_w_in, m_conv_w, m_conv_w_out, m_ffn_norm, m_ffn_w_gate, m_ffn_w_up, m_ffn_w_down, m_final_norm, v_mla_norm, v_mla_w_in, v_mla_g_cq, v_mla_g_ckv, v_mla_w_uq, v_mla_w_ukv, v_mla_w_o, v_conv_norm, v_conv_w_in, v_conv_w, v_conv_w_out, v_ffn_norm, v_ffn_w_gate, v_ffn_w_up, v_ffn_w_down, v_final_norm):
    weights = dict(mla_norm=mla_norm, mla_w_in=mla_w_in, mla_g_cq=mla_g_cq, mla_g_ckv=mla_g_ckv, mla_w_uq=mla_w_uq,
                   mla_w_ukv=mla_w_ukv, mla_w_o=mla_w_o, conv_norm=conv_norm, conv_w_in=conv_w_in, conv_w=conv_w,
                   conv_w_out=conv_w_out, ffn_norm=ffn_norm, ffn_w_gate=ffn_w_gate, ffn_w_up=ffn_w_up,
                   ffn_w_down=ffn_w_down, final_norm=final_norm)
    m_in = dict(mla_norm=m_mla_norm, mla_w_in=m_mla_w_in, mla_g_cq=m_mla_g_cq, mla_g_ckv=m_mla_g_ckv, mla_w_uq=m_mla_w_uq,
                mla_w_ukv=m_mla_w_ukv, mla_w_o=m_mla_w_o, conv_norm=m_conv_norm, conv_w_in=m_conv_w_in, conv_w=m_conv_w,
                conv_w_out=m_conv_w_out, ffn_norm=m_ffn_norm, ffn_w_gate=m_ffn_w_gate, ffn_w_up=m_ffn_w_up,
                ffn_w_down=m_ffn_w_down, final_norm=m_final_norm)
    v_in = dict(mla_norm=v_mla_norm, mla_w_in=v_mla_w_in, mla_g_cq=v_mla_g_cq, mla_g_ckv=v_mla_g_ckv, mla_w_uq=v_mla_w_uq,
                mla_w_ukv=v_mla_w_ukv, mla_w_o=v_mla_w_o, conv_norm=v_conv_norm, conv_w_in=v_conv_w_in, conv_w=v_conv_w,
                conv_w_out=v_conv_w_out, ffn_norm=v_ffn_norm, ffn_w_gate=v_ffn_w_gate, ffn_w_up=v_ffn_w_up,
                ffn_w_down=v_ffn_w_down, final_norm=v_final_norm)
    big = ["mla_w_in", "mla_w_uq", "mla_w_ukv", "mla_w_o", "conv_w_in", "conv_w_out", "ffn_w_gate", "ffn_w_up", "ffn_w_down"]
    order = list(weights)

    t, d = x.shape[1], x.shape[2]
    h0 = x.reshape(t, d)
    target = loss_target.reshape(t, d)
    cos, sin = _rope_tables(positions.reshape(t))

    def rows2d(a):
        return a.reshape(-1, a.shape[-1])

    gathered = dict(zip(big, gather_weight_shards([rows2d(weights[n]).astype(BF16) for n in big])))
    w_in = gathered["mla_w_in"].reshape(-1, gathered["mla_w_in"].shape[-1])
    w_uq = _unstack_cols(gathered["mla_w_uq"])
    w_ukv = _unstack_cols(gathered["mla_w_ukv"])
    w_o = gathered["mla_w_o"].reshape(-1, d)
    cw_in = _unstack_cols(gathered["conv_w_in"])
    cw_out = gathered["conv_w_out"].reshape(-1, d)
    wg_all, wu_all, wd_all = gathered["ffn_w_gate"], gathered["ffn_w_up"], gathered["ffn_w_down"]

    chip = 2 * lax.axis_index("x") + lax.axis_index("y")
    core = lax.axis_index("c")
    d4 = d // N_CHIPS
    first_core = (core == 0).astype(F32)

    def place_shard(shard):
        full = jnp.zeros((shard.shape[0], d), F32)
        return lax.dynamic_update_slice(full, shard * first_core, (0, chip * d4))

    def pack_rows(rows):
        idx = lax.broadcasted_iota(jnp.int32, (SMALL_ROWS, d), 0)
        out = jnp.zeros((SMALL_ROWS, d), F32)
        for r, row in enumerate(rows):
            out = out + jnp.where(idx == r, row, 0.0)
        return out

    cw = place_shard(conv_w.reshape(3, d4))
    pre = all_reduce_small("all_gather_conv_small", pack_rows([place_shard(conv_norm.reshape(1, d4)), cw[0:1], cw[1:2], cw[2:3]]))
    conv_norm_full = pre[0:1]
    conv_w_full = pre[1:4]

    a0 = rms_fwd("mla_norm_fwd", h0, mla_norm)
    proj = linear("mla_in_proj", a0, w_in, F32)
    cq, ckv, kr = mla_mid("mla_mid", proj, mla_g_cq, mla_g_ckv, cos, sin)
    q = linear("mla_q_up", cq, w_uq, F32)
    kv = linear("mla_kv_up", ckv, w_ukv, BF16)
    qh, kh, vh = qkv_heads("qkv_heads", q, kv, kr, cos, sin)
    attn, lse = attention_fwd("attention_fwd", qh, kh, vh)
    h1 = linear("mla_out_proj", attn, w_o, F32, resid=h0)

    def ffn_forward(tag, h, layer):
        a = rms_fwd(f"ffn{tag}_norm_fwd", h, ffn_norm[layer:layer + 1])
        g, u, z = ffn_up(f"ffn{tag}_up", a, wg_all, wu_all, layer)
        return a, g, u, z, ffn_down(f"ffn{tag}_down", z, wd_all, layer, h)

    a1, g0, u0, z0, h2 = ffn_forward(0, h1, 0)
    a2 = rms_fwd("conv_norm_fwd", h2, conv_norm_full)
    bcx = conv_in_proj("conv_in_proj", a2, cw_in)
    yc = conv_fwd("conv_fwd", bcx, conv_w_full)
    h3 = linear("conv_out_proj", yc, cw_out, F32, resid=h2)
    a3, g1, u1, z1, h4 = ffn_forward(1, h3, 1)
    dh4, d_final_norm, loss_local = loss_head("loss_head", h4, final_norm.reshape(1, d), target)

    def ffn_backward(tag, dh, h, layer, a, g, u, z):
        dg, du = ffn_bwd_hidden(f"ffn{tag}_bwd_hidden", dh, wd_all, layer, g, u)
        d_wd = ffn_wgrad_down(f"ffn{tag}_wgrad_down", z, dh)
        da = ffn_bwd_input(f"ffn{tag}_bwd_input", dg, du, wg_all, wu_all, layer, d)
        d_wg, d_wu = ffn_wgrad_up(f"ffn{tag}_wgrad_up", a, dg, du)
        dh_prev, d_norm = rms_bwd(f"ffn{tag}_norm_bwd", h, ffn_norm[layer:layer + 1], da, dh)
        return dh_prev, d_norm, d_wg, d_wu, d_wd

    dh3, d_ffn_norm1, d_wg1, d_wu1, d_wd1 = ffn_backward(1, dh4, h3, 1, a3, g1, u1, z1)

    dyc = linear_nt("conv_out_bwd_input", dh3, cw_out, F32)
    d_cw_out = wgrad("conv_out_wgrad", yc, dh3)
    db, dc, dxp, d_conv_w = conv_bwd("conv_bwd", bcx, conv_w_full, dyc)
    dbcx = jnp.stack([db, dc, dxp])
    da2 = conv_in_bwd_input("conv_in_bwd_input", dbcx, cw_in)
    d_cw_in = conv_in_wgrad("conv_in_wgrad", a2, dbcx)
    dh2, d_conv_norm = rms_bwd("conv_norm_bwd", h2, conv_norm_full, da2, dh3)

    dh1, d_ffn_norm0, d_wg0, d_wu0, d_wd0 = ffn_backward(0, dh2, h1, 0, a1, g0, u0, z0)

    d_attn = linear_nt("mla_out_bwd_input", dh1, w_o, F32)
    d_w_o = wgrad("mla_out_wgrad", attn, dh1)
    delta = attention_delta("attention_delta", d_attn, attn)
    dqh, dkh, dvh = attention_bwd("attention_bwd", qh, kh, vh, d_attn, lse, delta)
    dq, dkv, dkr = qkv_heads_bwd("qkv_heads_bwd", dqh, dkh, dvh, cos, sin)
    dcq = linear_nt("mla_q_up_bwd_input", dq, w_uq, F32)
    d_w_uq = wgrad("mla_q_up_wgrad", cq, dq)
    dckv = linear_nt("mla_kv_up_bwd_input", dkv, w_ukv, F32)
    d_w_ukv = wgrad("mla_kv_up_wgrad", ckv, dkv)
    dproj, d_g_cq, d_g_ckv = mla_mid_bwd("mla_mid_bwd", proj, mla_g_cq, mla_g_ckv, dcq, dckv, dkr, cos, sin)
    da0 = linear_nt("mla_in_bwd_input", dproj, w_in, F32)
    d_w_in = wgrad("mla_in_wgrad", a0, dproj)
    grad_x, d_mla_norm = rms_bwd("mla_norm_bwd", h0, mla_norm, da0, dh1)

    def by_layer(g_l0, g_l1):
        s = jnp.stack([g_l0, g_l1], axis=1)
        return s.reshape(N_CHIPS, -1, s.shape[-1])

    stacked = dict(
        mla_w_in=d_w_in.reshape(N_CHIPS, -1, d_w_in.shape[-1]), mla_w_uq=_stack_cols(d_w_uq), mla_w_ukv=_stack_cols(d_w_ukv),
        mla_w_o=d_w_o.reshape(N_CHIPS, -1, d), conv_w_in=_stack_cols(d_cw_in), conv_w_out=d_cw_out.reshape(N_CHIPS, -1, d),
        ffn_w_gate=by_layer(d_wg0, d_wg1), ffn_w_up=by_layer(d_wu0, d_wu1), ffn_w_down=by_layer(d_wd0, d_wd1))
    local = [stacked[n] for n in big]
    from_sibling = sibling_swap_halves(local)
    pair_sums = [add_halves(f"pair_sum_{n}", g, r) for n, g, r in zip(big, local, from_sibling)]
    from_chips = scatter_to_owner_chips(pair_sums)
    my_halves = [sum_chips(f"chip_sum_{n}", p) for n, p in zip(big, from_chips)]
    grads = dict(zip(big, sibling_join_halves(my_halves)))

    def pad_row(v):
        return jnp.pad(v, ((0, 0), (0, d - v.shape[1])))

    small = all_reduce_small("all_reduce_small_grads", pack_rows([
        d_mla_norm, pad_row(d_g_cq), pad_row(d_g_ckv), d_ffn_norm0, d_ffn_norm1, d_final_norm, d_conv_norm,
        d_conv_w[0:1], d_conv_w[1:2], d_conv_w[2:3], jnp.broadcast_to(loss_local, (1, d))]))
    loss = small[10, 0]
    grads["mla_norm"] = small[0:1]
    grads["mla_g_cq"] = small[1:2, :mla_g_cq.shape[1]]
    grads["mla_g_ckv"] = small[2:3, :mla_g_ckv.shape[1]]
    grads["ffn_norm"] = small[3:5]
    grads["final_norm"] = small[5:6]
    grads["conv_norm"] = lax.dynamic_slice(small[6:7], (0, chip * d4), (1, d4))
    grads["conv_w"] = lax.dynamic_slice(small[7:10], (0, chip * d4), (3, d4))

    outs_g, outs_d, outs_m, outs_v = [], [], [], []
    for n in order:
        w = weights[n]
        delta_w, new_m, new_v = adamw(f"adamw_{n}", rows2d(w), grads[n].reshape(rows2d(w).shape), rows2d(m_in[n]), rows2d(v_in[n]))
        outs_g.append(grads[n].reshape(w.shape))
        outs_d.append(delta_w.reshape(w.shape))
        outs_m.append(new_m.reshape(w.shape))
        outs_v.append(new_v.reshape(w.shape))
    return (loss, grad_x.reshape(x.shape), *outs_g, *outs_d, *outs_m, *outs_v)
```

```python
import math

import jax
import jax.numpy as jnp
from jax import lax
from jax.experimental import pallas as pl
from jax.experimental.pallas import tpu as pltpu

F32 = jnp.float32
BF16 = jnp.bfloat16
S = jax.ShapeDtypeStruct

N_HEADS = 8
NOPE = 128
ROPE = 64
HALF = ROPE // 2
VDIM = 128
QK = NOPE + ROPE
CHUNK = 64
ROPE_THETA = 10000.0
RMS_EPS = 1e-6
ADAM_LR = 0.001
ADAM_B1 = 0.9
ADAM_B2 = 0.999
ADAM_EPS = 1e-08
ADAM_WD = 0.01
ADAM_STEP = 10

N_CHIPS = 4
N_DEV = 8
MASK_VALUE = -1e30
VMEM_LIMIT = 48 * 1024 * 1024
ATT_BLOCK = 512
SMALL_ROWS = 16

_NN = (((1,), (0,)), ((), ()))
_NT = (((1,), (1,)), ((), ()))
_TN = (((0,), (0,)), ((), ()))
MESH = pl.DeviceIdType.MESH
ANY = pl.BlockSpec(memory_space=pl.ANY)


def _params(n_axes):
    return pltpu.CompilerParams(dimension_semantics=("arbitrary",) * n_axes, vmem_limit_bytes=VMEM_LIMIT)


def _tile(n, cap, mult=8):
    for t in range(min(cap, n), 0, -1):
        if n % t == 0 and t % mult == 0:
            return t
    return n


def _sigmoid(x):
    return 1.0 / (1.0 + jnp.exp(-x))


def _mm(name, a_ops, b_ops, products, dims, grid, k_axis, outs, acc_shape, epilogue, extra_ops=()):
    na, nb, ne, no = len(a_ops), len(b_ops), len(extra_ops), len(outs)
    n_acc = 1 + max(c for _, _, c in products)
    nk = 1 if k_axis is None else grid[k_axis]

    def body(*refs):
        a_refs = refs[:na]
        b_refs = refs[na:na + nb]
        e_refs = refs[na + nb:na + nb + ne]
        o_refs = refs[na + nb + ne:na + nb + ne + no]
        acc_refs = refs[na + nb + ne + no:]

        def partial_sums():
            vals = [None] * n_acc
            for ai, bi, ci in products:
                d = lax.dot_general(a_refs[ai][...].astype(BF16), b_refs[bi][...].astype(BF16), dims,
                                    preferred_element_type=F32)
                vals[ci] = d if vals[ci] is None else vals[ci] + d
            return vals

        if nk == 1:
            epilogue(partial_sums(), e_refs, o_refs)
        else:
            k = pl.program_id(k_axis)

            @pl.when(k == 0)
            def _():
                for acc in acc_refs:
                    acc[...] = jnp.zeros_like(acc)

            for acc, v in zip(acc_refs, partial_sums()):
                acc[...] += v

            @pl.when(k == nk - 1)
            def _():
                epilogue([acc[...] for acc in acc_refs], e_refs, o_refs)

    ops = list(a_ops) + list(b_ops) + list(extra_ops)
    return pl.pallas_call(
        body, name=name, grid=grid,
        in_specs=[s for _, s in ops], out_specs=[s for _, s in outs], out_shape=[o for o, _ in outs],
        scratch_shapes=[pltpu.VMEM(acc_shape, F32) for _ in range(n_acc if nk > 1 else 0)],
        compiler_params=_params(len(grid)),
    )(*[a for a, _ in ops])


def _store(accs, e_refs, o_refs):
    o_refs[0][...] = accs[0].astype(o_refs[0].dtype)


def _store_plus_residual(accs, e_refs, o_refs):
    o_refs[0][...] = (e_refs[0][...] + accs[0]).astype(o_refs[0].dtype)


def linear(name, x, w, out_dtype, resid=None):
    t, k = x.shape
    n = w.shape[1]
    tm = _tile(t, 512)
    tn = n if n <= 2048 else _tile(n, 1024, 128)
    extra = [] if resid is None else [(resid, pl.BlockSpec((tm, tn), lambda j, i: (i, j)))]
    return _mm(name, [(x, pl.BlockSpec((tm, k), lambda j, i: (i, 0)))], [(w, pl.BlockSpec((k, tn), lambda j, i: (0, j)))],
               [(0, 0, 0)], _NN, (n // tn, t // tm), None,
               [(S((t, n), out_dtype), pl.BlockSpec((tm, tn), lambda j, i: (i, j)))], None,
               _store if resid is None else _store_plus_residual, extra)[0]


def linear_nt(name, dy, w, out_dtype):
    t, n = dy.shape
    k = w.shape[0]
    tm = _tile(t, 512)
    tc = n if n <= 2048 else _tile(n, 1024, 128)
    return _mm(name, [(dy, pl.BlockSpec((tm, tc), lambda i, c: (i, c)))], [(w, pl.BlockSpec((k, tc), lambda i, c: (0, c)))],
               [(0, 0, 0)], _NT, (t // tm, n // tc), 1,
               [(S((t, k), out_dtype), pl.BlockSpec((tm, k), lambda i, c: (i, 0)))], (tm, k), _store)[0]


def wgrad(name, x, dy):
    t, k = x.shape
    n = dy.shape[1]
    tk = _tile(t, 512)
    tn = n if n <= 1024 else _tile(n, 1024, 128)
    return _mm(name, [(x, pl.BlockSpec((tk, k), lambda j, s: (s, 0)))], [(dy, pl.BlockSpec((tk, tn), lambda j, s: (s, j)))],
               [(0, 0, 0)], _TN, (n // tn, t // tk), 1,
               [(S((k, n), BF16), pl.BlockSpec((k, tn), lambda j, s: (0, j)))], (k, tn), _store)[0]


def _resident(shape, index_map):
    return pl.BlockSpec(shape, index_map, pipeline_mode=pl.Buffered(1))


def ffn_up(name, a, wg_all, wu_all, layer):
    t, d = a.shape
    f4 = wg_all.shape[2]
    tm = _tile(t, 512)
    w_spec = _resident((N_CHIPS, d, f4), lambda i: (0, layer, 0))
    h_spec = pl.BlockSpec((N_CHIPS, tm, f4), lambda i: (0, i, 0))

    def body(a_ref, wg_ref, wu_ref, g_ref, u_ref, z_ref):
        av = a_ref[...]
        for k in range(N_CHIPS):
            g = jnp.dot(av, wg_ref[k], preferred_element_type=F32)
            u = jnp.dot(av, wu_ref[k], preferred_element_type=F32)
            g_ref[k] = g.astype(BF16)
            u_ref[k] = u.astype(BF16)
            z_ref[k] = (g * _sigmoid(g) * u).astype(BF16)

    return pl.pallas_call(
        body, name=name, grid=(t // tm,), in_specs=[pl.BlockSpec((tm, d), lambda i: (i, 0)), w_spec, w_spec],
        out_specs=[h_spec] * 3, out_shape=[S((N_CHIPS, t, f4), BF16)] * 3, compiler_params=_params(1))(a, wg_all, wu_all)


def ffn_down(name, z, wd_all, layer, resid):
    _, t, f4 = z.shape
    d = wd_all.shape[2]
    tm = _tile(t, 512)
    row = pl.BlockSpec((tm, d), lambda i: (i, 0))

    def body(z_ref, wd_ref, r_ref, o_ref):
        acc = r_ref[...]
        for k in range(N_CHIPS):
            acc = acc + jnp.dot(z_ref[k], wd_ref[k], preferred_element_type=F32)
        o_ref[...] = acc

    return pl.pallas_call(
        body, name=name, grid=(t // tm,),
        in_specs=[pl.BlockSpec((N_CHIPS, tm, f4), lambda i: (0, i, 0)), _resident((N_CHIPS, f4, d), lambda i: (0, layer, 0)), row],
        out_specs=row, out_shape=S((t, d), F32), compiler_params=_params(1))(z, wd_all, resid)


def ffn_bwd_hidden(name, dh, wd_all, layer, g, u):
    t, d = dh.shape
    f4 = g.shape[2]
    tm = _tile(t, 512)
    h_spec = pl.BlockSpec((N_CHIPS, tm, f4), lambda i: (0, i, 0))

    def body(dh_ref, wd_ref, g_ref, u_ref, dg_ref, du_ref):
        dhb = dh_ref[...].astype(BF16)
        for k in range(N_CHIPS):
            dz = lax.dot_general(dhb, wd_ref[k], _NT, preferred_element_type=F32)
            gv = g_ref[k].astype(F32)
            uv = u_ref[k].astype(F32)
            sg = _sigmoid(gv)
            dg_ref[k] = (dz * uv * (sg * (1.0 + gv * (1.0 - sg)))).astype(BF16)
            du_ref[k] = (dz * (gv * sg)).astype(BF16)

    return pl.pallas_call(
        body, name=name, grid=(t // tm,),
        in_specs=[pl.BlockSpec((tm, d), lambda i: (i, 0)), _resident((N_CHIPS, f4, d), lambda i: (0, layer, 0)), h_spec, h_spec],
        out_specs=[h_spec] * 2, out_shape=[S((N_CHIPS, t, f4), BF16)] * 2, compiler_params=_params(1))(dh, wd_all, g, u)


def ffn_bwd_input(name, dg, du, wg_all, wu_all, layer, d):
    _, t, f4 = dg.shape
    tm = _tile(t, 512)
    h_spec = pl.BlockSpec((N_CHIPS, tm, f4), lambda i: (0, i, 0))
    w_spec = _resident((N_CHIPS, d, f4), lambda i: (0, layer, 0))

    def body(dg_ref, du_ref, wg_ref, wu_ref, da_ref):
        acc = jnp.zeros((tm, d), F32)
        for k in range(N_CHIPS):
            acc = acc + lax.dot_general(dg_ref[k], wg_ref[k], _NT, preferred_element_type=F32)
            acc = acc + lax.dot_general(du_ref[k], wu_ref[k], _NT, preferred_element_type=F32)
        da_ref[...] = acc

    return pl.pallas_call(
        body, name=name, grid=(t // tm,), in_specs=[h_spec, h_spec, w_spec, w_spec],
        out_specs=pl.BlockSpec((tm, d), lambda i: (i, 0)), out_shape=S((t, d), F32), compiler_params=_params(1))(dg, du, wg_all, wu_all)


def ffn_wgrad_up(name, a, dg, du, layer, n_layers, prev=None):
    t, d = a.shape
    f4 = dg.shape[2]
    tk = _tile(t, 512)
    nt = t // tk
    h_spec = pl.BlockSpec((None, tk, f4), lambda k, s: (k, s, 0))
    o_spec = pl.BlockSpec((None, d, f4), lambda k, s: (k, layer, 0))

    def body(a_ref, dg_ref, du_ref, *rest):
        og_ref, ou_ref, accg, accu = rest[-4:]
        s = pl.program_id(1)

        @pl.when(s == 0)
        def _():
            accg[...] = jnp.zeros_like(accg)
            accu[...] = jnp.zeros_like(accu)

        av = a_ref[...]
        accg[...] += lax.dot_general(av, dg_ref[...], _TN, preferred_element_type=F32)
        accu[...] += lax.dot_general(av, du_ref[...], _TN, preferred_element_type=F32)

        @pl.when(s == nt - 1)
        def _():
            og_ref[...] = accg[...].astype(BF16)
            ou_ref[...] = accu[...].astype(BF16)

    out = S((N_CHIPS, n_layers * d, f4), BF16)
    extra = [] if prev is None else list(prev)
    return pl.pallas_call(
        body, name=name, grid=(N_CHIPS, nt),
        in_specs=[pl.BlockSpec((tk, d), lambda k, s: (s, 0)), h_spec, h_spec] + [ANY] * len(extra),
        out_specs=[o_spec, o_spec], out_shape=[out, out],
        scratch_shapes=[pltpu.VMEM((d, f4), F32), pltpu.VMEM((d, f4), F32)],
        input_output_aliases={3 + i: i for i in range(len(extra))},
        compiler_params=_params(2))(a, dg, du, *extra)


def ffn_wgrad_down(name, z, dh, layer, n_layers, prev=None):
    _, t, f4 = z.shape
    d = dh.shape[1]
    tk = _tile(t, 512)
    nt = t // tk

    def body(z_ref, dh_ref, *rest):
        o_ref, acc = rest[-2:]
        s = pl.program_id(0)

        @pl.when(s == 0)
        def _():
            acc[...] = jnp.zeros_like(acc)

        dhb = dh_ref[...].astype(BF16)
        for k in range(N_CHIPS):
            acc[k] += lax.dot_general(z_ref[k], dhb, _TN, preferred_element_type=F32)

        @pl.when(s == nt - 1)
        def _():
            o_ref[...] = acc[...].astype(BF16)

    extra = [] if prev is None else [prev]
    return pl.pallas_call(
        body, name=name, grid=(nt,),
        in_specs=[pl.BlockSpec((N_CHIPS, tk, f4), lambda s: (0, s, 0)), pl.BlockSpec((tk, d), lambda s: (s, 0))] + [ANY] * len(extra),
        out_specs=pl.BlockSpec((N_CHIPS, f4, d), lambda s: (0, layer, 0)), out_shape=S((N_CHIPS, n_layers * f4, d), BF16),
        scratch_shapes=[pltpu.VMEM((N_CHIPS, f4, d), F32)],
        input_output_aliases={2 + i: 0 for i in range(len(extra))},
        compiler_params=_params(1))(z, dh, *extra)


def conv_in_proj(name, a, w):
    t, d = a.shape
    tm = _tile(t, 512)
    return _mm(name, [(a, pl.BlockSpec((tm, d), lambda j, i: (i, 0)))], [(w, pl.BlockSpec((d, d), lambda j, i: (0, j)))],
               [(0, 0, 0)], _NN, (3, t // tm), None,
               [(S((3, t, d), F32), pl.BlockSpec((None, tm, d), lambda j, i: (j, i, 0)))], None, _store)[0]


def conv_in_bwd_input(name, dbcx, w):
    _, t, d = dbcx.shape
    tm = _tile(t, 512)

    def body(g_ref, w_ref, o_ref):
        acc = jnp.zeros((tm, d), F32)
        for j in range(3):
            acc = acc + lax.dot_general(g_ref[j], w_ref[:, j * d:(j + 1) * d], _NT, preferred_element_type=F32)
        o_ref[...] = acc

    return pl.pallas_call(
        body, name=name, grid=(t // tm,),
        in_specs=[pl.BlockSpec((3, tm, d), lambda i: (0, i, 0)), _resident((d, 3 * d), lambda i: (0, 0))],
        out_specs=pl.BlockSpec((tm, d), lambda i: (i, 0)), out_shape=S((t, d), F32), compiler_params=_params(1))(dbcx, w)


def conv_in_wgrad(name, a, dbcx):
    t, d = a.shape
    tk = _tile(t, 512)
    return _mm(name, [(a, pl.BlockSpec((tk, d), lambda j, s: (s, 0)))], [(dbcx, pl.BlockSpec((None, tk, d), lambda j, s: (j, s, 0)))],
               [(0, 0, 0)], _TN, (3, t // tk), 1,
               [(S((d, 3 * d), BF16), pl.BlockSpec((d, d), lambda j, s: (0, j)))], (d, d), _store)[0]


def _rstd(x):
    return lax.rsqrt(jnp.mean(x * x, axis=-1, keepdims=True) + RMS_EPS)


def _rms_bwd(x, g, dy):
    r = _rstd(x)
    xhat = x * r
    dgain = jnp.sum(dy * xhat, axis=0, keepdims=True)
    dxh = dy * g
    dx = r * (dxh - xhat * jnp.mean(dxh * xhat, axis=-1, keepdims=True))
    return dx, dgain


def rms_fwd(name, h, g):
    t, d = h.shape
    tr = _tile(t, 512)

    def body(h_ref, g_ref, a_ref):
        x = h_ref[...]
        a_ref[...] = (x * _rstd(x) * g_ref[...]).astype(BF16)

    return pl.pallas_call(
        body, name=name, grid=(t // tr,),
        in_specs=[pl.BlockSpec((tr, d), lambda i: (i, 0)), pl.BlockSpec((1, d), lambda i: (0, 0))],
        out_specs=pl.BlockSpec((tr, d), lambda i: (i, 0)), out_shape=S((t, d), BF16), compiler_params=_params(1))(h, g)


def rms_bwd(name, h, g, da, dh_in):
    t, d = h.shape
    tr = _tile(t, 512)

    def body(h_ref, g_ref, da_ref, dhi_ref, dho_ref, dg_ref):
        dx, dgain = _rms_bwd(h_ref[...], g_ref[...], da_ref[...])
        dho_ref[...] = dhi_ref[...] + dx

        @pl.when(pl.program_id(0) == 0)
        def _():
            dg_ref[...] = jnp.zeros_like(dg_ref)

        dg_ref[...] += dgain

    row = pl.BlockSpec((tr, d), lambda i: (i, 0))
    vec = pl.BlockSpec((1, d), lambda i: (0, 0))
    return pl.pallas_call(
        body, name=name, grid=(t // tr,), in_specs=[row, vec, row, row], out_specs=[row, vec],
        out_shape=[S((t, d), F32), S((1, d), F32)], compiler_params=_params(1))(h, g, da, dh_in)


def loss_head(name, h, g, target):
    t, d = h.shape
    tr = _tile(t, 512)

    def body(h_ref, g_ref, t_ref, dh_ref, dg_ref, loss_ref):
        x = h_ref[...]
        g = g_ref[...]
        r = _rstd(x)
        xhat = x * r
        err = xhat * g - t_ref[...]
        dy = err * (1.0 / d)
        dxh = dy * g
        dh_ref[...] = r * (dxh - xhat * jnp.mean(dxh * xhat, axis=-1, keepdims=True))

        @pl.when(pl.program_id(0) == 0)
        def _():
            dg_ref[...] = jnp.zeros_like(dg_ref)
            loss_ref[...] = jnp.zeros_like(loss_ref)

        dg_ref[...] += jnp.sum(dy * xhat, axis=0, keepdims=True)
        per_token = jnp.mean(err * err, axis=-1, keepdims=True)
        loss_ref[...] += 0.5 * jnp.sum(per_token, axis=0, keepdims=True)

    row = pl.BlockSpec((tr, d), lambda i: (i, 0))
    vec = pl.BlockSpec((1, d), lambda i: (0, 0))
    one = pl.BlockSpec((1, 1), lambda i: (0, 0))
    return pl.pallas_call(
        body, name=name, grid=(t // tr,), in_specs=[row, vec, row], out_specs=[row, vec, one],
        out_shape=[S((t, d), F32), S((1, d), F32), S((1, 1), F32)], compiler_params=_params(1))(h, g, target)


def mla_mid(name, proj, g_cq, g_ckv, cos, sin):
    t, n = proj.shape
    ql, kl = g_cq.shape[1], g_ckv.shape[1]
    tr = _tile(t, 512)

    def body(p_ref, gq_ref, gk_ref, c_ref, s_ref, cq_ref, ckv_ref, kr_ref):
        xq = p_ref[:, 0:ql]
        cq_ref[...] = (xq * _rstd(xq) * gq_ref[...]).astype(BF16)
        xk = p_ref[:, ql:ql + kl]
        ckv_ref[...] = (xk * _rstd(xk) * gk_ref[...]).astype(BF16)
        k1 = p_ref[:, ql + kl:ql + kl + HALF]
        k2 = p_ref[:, ql + kl + HALF:ql + kl + ROPE]
        c = c_ref[...]
        s = s_ref[...]
        kr_ref[:, 0:HALF] = k1 * c - k2 * s
        kr_ref[:, HALF:ROPE] = k1 * s + k2 * c

    def row(w):
        return pl.BlockSpec((tr, w), lambda i: (i, 0))

    def vec(w):
        return pl.BlockSpec((1, w), lambda i: (0, 0))

    return pl.pallas_call(
        body, name=name, grid=(t // tr,), in_specs=[row(n), vec(ql), vec(kl), row(HALF), row(HALF)],
        out_specs=[row(ql), row(kl), row(ROPE)], out_shape=[S((t, ql), BF16), S((t, kl), BF16), S((t, ROPE), F32)],
        compiler_params=_params(1))(proj, g_cq, g_ckv, cos, sin)


def mla_mid_bwd(name, proj, g_cq, g_ckv, dcq, dckv, dkr, cos, sin):
    t, n = proj.shape
    ql, kl = g_cq.shape[1], g_ckv.shape[1]
    tr = _tile(t, 512)

    def body(p_ref, gq_ref, gk_ref, dcq_ref, dckv_ref, dkr_ref, c_ref, s_ref, dp_ref, dgq_ref, dgk_ref):
        dxq, dgq = _rms_bwd(p_ref[:, 0:ql], gq_ref[...], dcq_ref[...])
        dp_ref[:, 0:ql] = dxq.astype(BF16)
        dxk, dgk = _rms_bwd(p_ref[:, ql:ql + kl], gk_ref[...], dckv_ref[...])
        dp_ref[:, ql:ql + kl] = dxk.astype(BF16)
        d1 = dkr_ref[:, 0:HALF]
        d2 = dkr_ref[:, HALF:ROPE]
        c = c_ref[...]
        s = s_ref[...]
        dp_ref[:, ql + kl:ql + kl + HALF] = (d1 * c + d2 * s).astype(BF16)
        dp_ref[:, ql + kl + HALF:ql + kl + ROPE] = (d2 * c - d1 * s).astype(BF16)

        @pl.when(pl.program_id(0) == 0)
        def _():
            dgq_ref[...] = jnp.zeros_like(dgq_ref)
            dgk_ref[...] = jnp.zeros_like(dgk_ref)

        dgq_ref[...] += dgq
        dgk_ref[...] += dgk

    def row(w):
        return pl.BlockSpec((tr, w), lambda i: (i, 0))

    def vec(w):
        return pl.BlockSpec((1, w), lambda i: (0, 0))

    return pl.pallas_call(
        body, name=name, grid=(t // tr,),
        in_specs=[row(n), vec(ql), vec(kl), row(ql), row(kl), row(ROPE), row(HALF), row(HALF)],
        out_specs=[row(n), vec(ql), vec(kl)], out_shape=[S((t, n), BF16), S((1, ql), F32), S((1, kl), F32)],
        compiler_params=_params(1))(proj, g_cq, g_ckv, dcq, dckv, dkr, cos, sin)


def qkv_heads(name, q, kv, kr, cos, sin):
    t = q.shape[0]
    tr = _tile(t, 256)

    def body(q_ref, kv_ref, kr_ref, c_ref, s_ref, qo_ref, ko_ref, vo_ref):
        c = c_ref[...]
        s = s_ref[...]
        krb = kr_ref[...].astype(BF16)
        for h in range(N_HEADS):
            q0 = h * QK
            qo_ref[h, :, 0:NOPE] = q_ref[:, q0:q0 + NOPE].astype(BF16)
            q1 = q_ref[:, q0 + NOPE:q0 + NOPE + HALF]
            q2 = q_ref[:, q0 + NOPE + HALF:q0 + QK]
            qo_ref[h, :, NOPE:NOPE + HALF] = (q1 * c - q2 * s).astype(BF16)
            qo_ref[h, :, NOPE + HALF:QK] = (q1 * s + q2 * c).astype(BF16)
            k0 = h * (NOPE + VDIM)
            ko_ref[h, :, 0:NOPE] = kv_ref[:, k0:k0 + NOPE]
            ko_ref[h, :, NOPE:QK] = krb
            vo_ref[h] = kv_ref[:, k0 + NOPE:k0 + NOPE + VDIM]

    def row(w):
        return pl.BlockSpec((tr, w), lambda i: (i, 0))

    def heads(w):
        return pl.BlockSpec((N_HEADS, tr, w), lambda i: (0, i, 0))

    return pl.pallas_call(
        body, name=name, grid=(t // tr,),
        in_specs=[row(N_HEADS * QK), row(N_HEADS * (NOPE + VDIM)), row(ROPE), row(HALF), row(HALF)],
        out_specs=[heads(QK), heads(QK), heads(VDIM)],
        out_shape=[S((N_HEADS, t, QK), BF16), S((N_HEADS, t, QK), BF16), S((N_HEADS, t, VDIM), BF16)],
        compiler_params=_params(1))(q, kv, kr, cos, sin)


def qkv_heads_bwd(name, dq_h, dk_h, dv_h, cos, sin):
    t = dq_h.shape[1]
    tr = _tile(t, 256)

    def body(dq_ref, dk_ref, dv_ref, c_ref, s_ref, q_ref, kv_ref, kr_ref):
        c = c_ref[...]
        s = s_ref[...]
        dkr = jnp.zeros((tr, ROPE), F32)
        for h in range(N_HEADS):
            q0 = h * QK
            q_ref[:, q0:q0 + NOPE] = dq_ref[h, :, 0:NOPE].astype(BF16)
            d1 = dq_ref[h, :, NOPE:NOPE + HALF]
            d2 = dq_ref[h, :, NOPE + HALF:QK]
            q_ref[:, q0 + NOPE:q0 + NOPE + HALF] = (d1 * c + d2 * s).astype(BF16)
            q_ref[:, q0 + NOPE + HALF:q0 + QK] = (d2 * c - d1 * s).astype(BF16)
            k0 = h * (NOPE + VDIM)
            kv_ref[:, k0:k0 + NOPE] = dk_ref[h, :, 0:NOPE].astype(BF16)
            kv_ref[:, k0 + NOPE:k0 + NOPE + VDIM] = dv_ref[h].astype(BF16)
            dkr = dkr + dk_ref[h, :, NOPE:QK]
        kr_ref[...] = dkr

    def row(w):
        return pl.BlockSpec((tr, w), lambda i: (i, 0))

    def heads(w):
        return pl.BlockSpec((N_HEADS, tr, w), lambda i: (0, i, 0))

    return pl.pallas_call(
        body, name=name, grid=(t // tr,),
        in_specs=[heads(QK), heads(QK), heads(VDIM), row(HALF), row(HALF)],
        out_specs=[row(N_HEADS * QK), row(N_HEADS * (NOPE + VDIM)), row(ROPE)],
        out_shape=[S((t, N_HEADS * QK), BF16), S((t, N_HEADS * (NOPE + VDIM)), BF16), S((t, ROPE), F32)],
        compiler_params=_params(1))(dq_h, dk_h, dv_h, cos, sin)


def _chunk_mask(q_start, k_start, bq, bk):
    qc = (q_start + lax.broadcasted_iota(jnp.int32, (bq, bk), 0)) // CHUNK
    kc = (k_start + lax.broadcasted_iota(jnp.int32, (bq, bk), 1)) // CHUNK
    return kc <= qc


def attention_fwd(name, q, k, v):
    nh, t, _ = q.shape
    blk = ATT_BLOCK
    scale = 1.0 / math.sqrt(QK)

    def body(q_ref, k_ref, v_ref, o_ref, lse_ref, m_ref, l_ref, acc_ref):
        i = pl.program_id(1)
        qv = q_ref[...]
        m_ref[...] = jnp.full_like(m_ref, MASK_VALUE)
        l_ref[...] = jnp.zeros_like(l_ref)
        acc_ref[...] = jnp.zeros_like(acc_ref)

        def block(j, masked):
            start = pl.multiple_of(j * blk, blk)
            kb = k_ref[pl.ds(start, blk), :]
            vb = v_ref[pl.ds(start, blk), :]
            s = lax.dot_general(qv, kb, _NT, preferred_element_type=F32) * scale
            if masked:
                s = jnp.where(_chunk_mask(i * blk, j * blk, blk, blk), s, MASK_VALUE)
            m_old = m_ref[...]
            m_new = jnp.maximum(m_old, jnp.max(s, axis=-1, keepdims=True))
            p = jnp.exp(s - m_new)
            alpha = jnp.exp(m_old - m_new)
            l_ref[...] = alpha * l_ref[...] + jnp.sum(p, axis=-1, keepdims=True)
            acc_ref[...] = alpha * acc_ref[...] + jnp.dot(p.astype(BF16), vb, preferred_element_type=F32)
            m_ref[...] = m_new

        def step(j, carry):
            block(j, False)
            return carry

        lax.fori_loop(0, i, step, 0)
        block(i, True)
        l = l_ref[...]
        o_ref[...] = acc_ref[...] / l
        lse_ref[...] = m_ref[...] + jnp.log(l)

    return pl.pallas_call(
        body, name=name, grid=(nh, t // blk),
        in_specs=[pl.BlockSpec((None, blk, QK), lambda h, i: (h, i, 0)), pl.BlockSpec((None, t, QK), lambda h, i: (h, 0, 0)),
                  pl.BlockSpec((None, t, VDIM), lambda h, i: (h, 0, 0))],
        out_specs=[pl.BlockSpec((blk, VDIM), lambda h, i: (i, h)), pl.BlockSpec((None, blk, 1), lambda h, i: (h, i, 0))],
        out_shape=[S((t, nh * VDIM), F32), S((nh, t, 1), F32)],
        scratch_shapes=[pltpu.VMEM((blk, 1), F32), pltpu.VMEM((blk, 1), F32), pltpu.VMEM((blk, VDIM), F32)],
        compiler_params=_params(2))(q, k, v)


def attention_delta(name, do, o):
    t = do.shape[0]
    tr = _tile(t, 512)

    def body(do_ref, o_ref, d_ref):
        d_ref[...] = jnp.sum(do_ref[...] * o_ref[...], axis=-1, keepdims=True)

    blk = pl.BlockSpec((tr, VDIM), lambda h, i: (i, h))
    return pl.pallas_call(
        body, name=name, grid=(N_HEADS, t // tr), in_specs=[blk, blk],
        out_specs=pl.BlockSpec((None, tr, 1), lambda h, i: (h, i, 0)), out_shape=S((N_HEADS, t, 1), F32),
        compiler_params=_params(2))(do, o)


def attention_bwd(name, q, k, v, do, lse, delta):
    nh, t, _ = q.shape
    blk = ATT_BLOCK
    nq = t // blk
    scale = 1.0 / math.sqrt(QK)

    def body(q_ref, k_ref, v_ref, do_ref, lse_ref, dl_ref, dq_ref, dk_ref, dv_ref):
        j = pl.program_id(1)

        @pl.when(j == 0)
        def _():
            dq_ref[...] = jnp.zeros_like(dq_ref)

        kb = k_ref[...]
        vb = v_ref[...]
        dk_ref[...] = jnp.zeros_like(dk_ref)
        dv_ref[...] = jnp.zeros_like(dv_ref)

        def block(i, masked):
            rows = pl.ds(pl.multiple_of(i * blk, blk), blk)
            qb = q_ref[rows, :]
            dob = do_ref[rows, :].astype(BF16)
            s = lax.dot_general(qb, kb, _NT, preferred_element_type=F32) * scale
            if masked:
                s = jnp.where(_chunk_mask(i * blk, j * blk, blk, blk), s, MASK_VALUE)
            p = jnp.exp(s - lse_ref[rows, :])
            dp = lax.dot_general(dob, vb, _NT, preferred_element_type=F32)
            ds = (p * (dp - dl_ref[rows, :]) * scale).astype(BF16)
            dv_ref[...] += lax.dot_general(p.astype(BF16), dob, _TN, preferred_element_type=F32)
            dk_ref[...] += lax.dot_general(ds, qb, _TN, preferred_element_type=F32)
            dq_ref[rows, :] += jnp.dot(ds, kb, preferred_element_type=F32)

        block(j, True)

        def step(i, carry):
            block(i, False)
            return carry

        lax.fori_loop(j + 1, nq, step, 0)

    head_all = lambda w: pl.BlockSpec((None, t, w), lambda h, j: (h, 0, 0))
    head_blk = lambda w: pl.BlockSpec((None, blk, w), lambda h, j: (h, j, 0))
    return pl.pallas_call(
        body, name=name, grid=(nh, nq),
        in_specs=[head_all(QK), head_blk(QK), head_blk(VDIM), pl.BlockSpec((t, VDIM), lambda h, j: (0, h)), head_all(1), head_all(1)],
        out_specs=[head_all(QK), head_blk(QK), head_blk(VDIM)],
        out_shape=[S((nh, t, QK), F32), S((nh, t, QK), F32), S((nh, t, VDIM), F32)],
        compiler_params=_params(2))(q, k, v, do, lse, delta)


def _shift_down(u, s):
    rows = lax.broadcasted_iota(jnp.int32, u.shape, 0)
    return jnp.where(rows >= s, pltpu.roll(u, s, 0), 0.0)


def _shift_up(u, s):
    n = u.shape[0]
    rows = lax.broadcasted_iota(jnp.int32, u.shape, 0)
    return jnp.where(rows < n - s, pltpu.roll(u, n - s, 0), 0.0)


def _conv_specs(t, d, lanes):
    slab = lambda part: pl.BlockSpec((None, t, lanes), lambda j, part=part: (part, 0, j))
    return slab, pl.BlockSpec((3, lanes), lambda j: (0, j)), pl.BlockSpec((t, lanes), lambda j: (0, j))


def conv_fwd(name, bcx, w):
    _, t, d = bcx.shape
    lanes = _tile(d, 128, 128)
    slab, w_spec, col = _conv_specs(t, d, lanes)

    def body(b_ref, c_ref, x_ref, w_ref, y_ref):
        u = c_ref[...] * x_ref[...]
        uc = w_ref[0:1, :] * _shift_down(u, 2) + w_ref[1:2, :] * _shift_down(u, 1) + w_ref[2:3, :] * u
        y_ref[...] = (b_ref[...] * uc).astype(BF16)

    return pl.pallas_call(
        body, name=name, grid=(d // lanes,), in_specs=[slab(0), slab(1), slab(2), w_spec], out_specs=col,
        out_shape=S((t, d), BF16), compiler_params=_params(1))(bcx, bcx, bcx, w)


def conv_bwd(name, bcx, w, dy):
    _, t, d = bcx.shape
    lanes = _tile(d, 128, 128)
    slab, w_spec, col = _conv_specs(t, d, lanes)

    def body(b_ref, c_ref, x_ref, w_ref, dy_ref, d_ref, dw_ref):
        c = c_ref[...]
        x = x_ref[...]
        dyv = dy_ref[...]
        u = c * x
        u1 = _shift_down(u, 1)
        u2 = _shift_down(u, 2)
        w0, w1, w2 = w_ref[0:1, :], w_ref[1:2, :], w_ref[2:3, :]
        d_ref[0] = (dyv * (w0 * u2 + w1 * u1 + w2 * u)).astype(BF16)
        duc = dyv * b_ref[...]
        dw_ref[0:1, :] = jnp.sum(duc * u2, axis=0, keepdims=True)
        dw_ref[1:2, :] = jnp.sum(duc * u1, axis=0, keepdims=True)
        dw_ref[2:3, :] = jnp.sum(duc * u, axis=0, keepdims=True)
        du = w2 * duc + w1 * _shift_up(duc, 1) + w0 * _shift_up(duc, 2)
        d_ref[1] = (du * x).astype(BF16)
        d_ref[2] = (du * c).astype(BF16)

    return pl.pallas_call(
        body, name=name, grid=(d // lanes,), in_specs=[slab(0), slab(1), slab(2), w_spec, col],
        out_specs=[pl.BlockSpec((3, t, lanes), lambda j: (0, 0, j)), w_spec], out_shape=[S((3, t, d), BF16), S((3, d), F32)],
        compiler_params=_params(1))(bcx, bcx, bcx, w, dy)


def adamw(name, w, g, m, v):
    r, c = w.shape
    tr = _tile(r, 512)

    def body(w_ref, g_ref, m_ref, v_ref, d_ref, mo_ref, vo_ref):
        gv = g_ref[...]
        m_new = ADAM_B1 * m_ref[...] + (1.0 - ADAM_B1) * gv
        v_new = ADAM_B2 * v_ref[...] + (1.0 - ADAM_B2) * (gv * gv)
        m_hat = m_new / (1.0 - ADAM_B1 ** ADAM_STEP)
        v_hat = v_new / (1.0 - ADAM_B2 ** ADAM_STEP)
        d_ref[...] = -ADAM_LR * (m_hat / (jnp.sqrt(v_hat) + ADAM_EPS) + ADAM_WD * w_ref[...])
        mo_ref[...] = m_new
        vo_ref[...] = v_new

    blk = pl.BlockSpec((tr, c), lambda i: (i, 0))
    return pl.pallas_call(
        body, name=name, grid=(r // tr,), in_specs=[blk] * 4, out_specs=[blk] * 3, out_shape=[S((r, c), F32)] * 3,
        compiler_params=_params(1))(w, g, m, v)


def _place():
    x, y, c = lax.axis_index("x"), lax.axis_index("y"), lax.axis_index("c")
    other_chips = [(1 - x, y), (x, 1 - y), (1 - x, 1 - y)]
    return x, y, c, other_chips


def _half(c, rows):
    return pl.ds(pl.multiple_of(c * (rows // 2), 16), rows // 2)


def gather_weight_shards(shards):
    n = len(shards)

    def body(*refs):
        src = refs[:n]
        dst = refs[n:2 * n]
        send_sems, recv_sems = refs[2 * n:]
        x, y, c, chips = _place()
        me = 2 * x + y
        sibling = (x, y, 1 - c)

        def copy(i, slot, half_of, sem, to, from_input=False):
            rows = _half(half_of, src[i].shape[0])
            return pltpu.make_async_remote_copy(
                src_ref=src[i].at[rows] if from_input else dst[i].at[slot, rows], dst_ref=dst[i].at[slot, rows],
                send_sem=send_sems.at[i, sem], recv_sem=recv_sems.at[i, sem], device_id=to, device_id_type=MESH)

        sent = []
        for i in range(n):
            for j, chip in enumerate(chips):
                sent.append(copy(i, me, c, j, (*chip, c), from_input=True))
                sent[-1].start()
        for i in range(n):
            for j, (px, py) in enumerate(chips):
                copy(i, 2 * px + py, c, j, sibling).wait_recv()
                sent.append(copy(i, 2 * px + py, c, 3 + j, sibling))
                sent[-1].start()
        for i in range(n):
            for j, (px, py) in enumerate(chips):
                copy(i, 2 * px + py, 1 - c, 3 + j, sibling).wait_recv()
        for cp in sent:
            cp.wait_send()

    outs = pl.pallas_call(
        body, name="gather_weight_shards", in_specs=[ANY] * n, out_specs=[ANY] * n,
        out_shape=[S((N_CHIPS,) + s.shape, s.dtype) for s in shards],
        scratch_shapes=[pltpu.SemaphoreType.DMA((n, 6)), pltpu.SemaphoreType.DMA((n, 6))],
    )(*shards)
    me = 2 * lax.axis_index("x") + lax.axis_index("y")
    return [lax.dynamic_update_slice(o, s[None], (me, 0, 0)) for o, s in zip(outs, shards)]


def sibling_swap_halves(grads):
    n = len(grads)

    def body(*refs):
        src = refs[:n]
        dst = refs[n:2 * n]
        send_sems, recv_sems = refs[2 * n:]
        x, y, c, _ = _place()
        copies = [pltpu.make_async_remote_copy(
            src_ref=src[i].at[:, _half(1 - c, src[i].shape[1]), :], dst_ref=dst[i], send_sem=send_sems.at[i],
            recv_sem=recv_sems.at[i], device_id=(x, y, 1 - c), device_id_type=MESH) for i in range(n)]
        for cp in copies:
            cp.start()
        for cp in copies:
            cp.wait()

    return pl.pallas_call(
        body, name="sibling_swap_halves", in_specs=[ANY] * n, out_specs=[ANY] * n,
        out_shape=[S((g.shape[0], g.shape[1] // 2, g.shape[2]), g.dtype) for g in grads],
        scratch_shapes=[pltpu.SemaphoreType.DMA((n,)), pltpu.SemaphoreType.DMA((n,))],
    )(*grads)


def add_halves(name, g, rx):
    _, r, cdim = g.shape
    r2 = r // 2
    tr = _tile(r2, 512, 16)
    nb = r2 // tr

    def body(lo_ref, hi_ref, rx_ref, o_ref):
        mine = jnp.where(lax.axis_index("c") == 0, lo_ref[...], hi_ref[...])
        o_ref[...] = (mine.astype(F32) + rx_ref[...].astype(F32)).astype(BF16)

    half = pl.BlockSpec((None, tr, cdim), lambda k, i: (k, i, 0))
    return pl.pallas_call(
        body, name=name, grid=(N_CHIPS, nb),
        in_specs=[half, pl.BlockSpec((None, tr, cdim), lambda k, i: (k, nb + i, 0)), half],
        out_specs=half, out_shape=S((N_CHIPS, r2, cdim), BF16), compiler_params=_params(2))(g, g, rx)


def scatter_to_owner_chips(parts):
    n = len(parts)

    def body(*refs):
        src = refs[:n]
        dst = refs[n:2 * n]
        send_sems, recv_sems = refs[2 * n:]
        x, y, c, chips = _place()
        me = 2 * x + y

        def copy(i, j, from_slot, to_slot, to):
            return pltpu.make_async_remote_copy(
                src_ref=src[i].at[from_slot], dst_ref=dst[i].at[to_slot], send_sem=send_sems.at[i, j],
                recv_sem=recv_sems.at[i, j], device_id=to, device_id_type=MESH)

        sent = []
        for i in range(n):
            for j, (px, py) in enumerate(chips):
                sent.append(copy(i, j, 2 * px + py, me, (px, py, c)))
                sent[-1].start()
        for i in range(n):
            for j, (px, py) in enumerate(chips):
                copy(i, j, me, 2 * px + py, (px, py, c)).wait_recv()
        for cp in sent:
            cp.wait_send()

    outs = pl.pallas_call(
        body, name="scatter_to_owner_chips", in_specs=[ANY] * n, out_specs=[ANY] * n,
        out_shape=[S(p.shape, p.dtype) for p in parts],
        scratch_shapes=[pltpu.SemaphoreType.DMA((n, 3)), pltpu.SemaphoreType.DMA((n, 3))],
    )(*parts)
    me = 2 * lax.axis_index("x") + lax.axis_index("y")
    return [lax.dynamic_update_slice(o, lax.dynamic_slice(p, (me, 0, 0), (1,) + p.shape[1:]), (me, 0, 0))
            for o, p in zip(outs, parts)]


def sum_chips(name, parts):
    _, r2, cdim = parts.shape
    tr = _tile(r2, 512, 16)

    def body(p_ref, o_ref):
        acc = p_ref[0].astype(F32)
        for k in range(1, N_CHIPS):
            acc = acc + p_ref[k].astype(F32)
        o_ref[...] = acc

    return pl.pallas_call(
        body, name=name, grid=(r2 // tr,), in_specs=[pl.BlockSpec((N_CHIPS, tr, cdim), lambda i: (0, i, 0))],
        out_specs=pl.BlockSpec((tr, cdim), lambda i: (i, 0)), out_shape=S((r2, cdim), F32), compiler_params=_params(1))(parts)


def sibling_join_halves(halves):
    n = len(halves)

    def body(*refs):
        src = refs[:n]
        dst = refs[n:2 * n]
        send_sems, recv_sems = refs[2 * n:]
        x, y, c, _ = _place()
        copies = [pltpu.make_async_remote_copy(
            src_ref=src[i], dst_ref=dst[i].at[_half(c, dst[i].shape[0])], send_sem=send_sems.at[i],
            recv_sem=recv_sems.at[i], device_id=(x, y, 1 - c), device_id_type=MESH) for i in range(n)]
        for cp in copies:
            cp.start()
        for i in range(n):
            pltpu.make_async_remote_copy(
                src_ref=src[i], dst_ref=dst[i].at[_half(1 - c, dst[i].shape[0])], send_sem=send_sems.at[i],
                recv_sem=recv_sems.at[i], device_id=(x, y, 1 - c), device_id_type=MESH).wait_recv()
        for cp in copies:
            cp.wait_send()

    outs = pl.pallas_call(
        body, name="sibling_join_halves", in_specs=[ANY] * n, out_specs=[ANY] * n,
        out_shape=[S((2 * h.shape[0], h.shape[1]), h.dtype) for h in halves],
        scratch_shapes=[pltpu.SemaphoreType.DMA((n,)), pltpu.SemaphoreType.DMA((n,))],
    )(*halves)
    c = lax.axis_index("c")
    return [lax.dynamic_update_slice(o, h, (c * h.shape[0], 0)) for o, h in zip(outs, halves)]


def all_reduce_small(name, packed):
    rows, width = packed.shape

    def body(x_ref, o_ref, gathered, send_sems, recv_sems):
        x, y, c, _ = _place()
        me = 4 * x + 2 * y + c
        gathered[me] = x_ref[...]
        flips = [(fx, fy, fc) for fx in (0, 1) for fy in (0, 1) for fc in (0, 1)][1:]

        def copy(r, slot, to):
            return pltpu.make_async_remote_copy(
                src_ref=x_ref, dst_ref=gathered.at[slot], send_sem=send_sems.at[r], recv_sem=recv_sems.at[r],
                device_id=to, device_id_type=MESH)

        def peer(f):
            return (x ^ f[0], y ^ f[1], c ^ f[2])

        sent = [copy(r, me, peer(f)) for r, f in enumerate(flips)]
        for cp in sent:
            cp.start()
        for r, f in enumerate(flips):
            px, py, pc = peer(f)
            copy(r, 4 * px + 2 * py + pc, peer(f)).wait_recv()
        for cp in sent:
            cp.wait_send()
        acc = gathered[0]
        for k in range(1, N_DEV):
            acc = acc + gathered[k]
        o_ref[...] = acc

    vmem = pl.BlockSpec(memory_space=pltpu.VMEM)
    return pl.pallas_call(
        body, name=name, in_specs=[vmem], out_specs=vmem, out_shape=S((rows, width), F32),
        scratch_shapes=[pltpu.VMEM((N_DEV, rows, width), F32), pltpu.SemaphoreType.DMA((N_DEV - 1,)),
                        pltpu.SemaphoreType.DMA((N_DEV - 1,))],
    )(packed)


def _rope_tables(positions):
    inv_freq = 1.0 / (ROPE_THETA ** (jnp.arange(0, ROPE, 2, dtype=F32) / ROPE))
    ang = positions.astype(F32)[:, None] * inv_freq
    return jnp.cos(ang), jnp.sin(ang)


def _unstack_cols(w):
    k4, k, n4 = w.shape
    return jnp.transpose(w, (1, 0, 2)).reshape(k, k4 * n4)


def _stack_cols(w):
    k, n = w.shape
    return jnp.transpose(w.reshape(k, N_CHIPS, n // N_CHIPS), (1, 0, 2))


def kernel(x, positions, mla_norm, mla_w_in, mla_g_cq, mla_g_ckv, mla_w_uq, mla_w_ukv, mla_w_o, conv_norm, conv_w_in, conv_w, conv_w_out, ffn_norm, ffn_w_gate, ffn_w_up, ffn_w_down, final_norm, loss_target, m_mla_norm, m_mla_w_in, m_mla_g_cq, m_mla_g_ckv, m_mla_w_uq, m_mla_w_ukv, m_mla_w_o, m_conv_norm, m_conv_w_in, m_conv_w, m_conv_w_out, m_ffn_norm, m_ffn_w_gate, m_ffn_w_up, m_ffn_w_down, m_final_norm, v_mla_norm, v_mla_w_in, v_mla_g_cq, v_mla_g_ckv, v_mla_w_uq, v_mla_w_ukv, v_mla_w_o, v_conv_norm, v_conv_w_in, v_conv_w, v_conv_w_out, v_ffn_norm, v_ffn_w_gate, v_ffn_w_up, v_ffn_w_down, v_final_norm):
    weights = dict(mla_norm=mla_norm, mla_w_in=mla_w_in, mla_g_cq=mla_g_cq, mla_g_ckv=mla_g_ckv, mla_w_uq=mla_w_uq,
                   mla_w_ukv=mla_w_ukv, mla_w_o=mla_w_o, conv_norm=conv_norm, conv_w_in=conv_w_in, conv_w=conv_w,
                   conv_w_out=conv_w_out, ffn_norm=ffn_norm, ffn_w_gate=ffn_w_gate, ffn_w_up=ffn_w_up,
                   ffn_w_down=ffn_w_down, final_norm=final_norm)
    m_in = dict(mla_norm=m_mla_norm, mla_w_in=m_mla_w_in, mla_g_cq=m_mla_g_cq, mla_g_ckv=m_mla_g_ckv, mla_w_uq=m_mla_w_uq,
                mla_w_ukv=m_mla_w_ukv, mla_w_o=m_mla_w_o, conv_norm=m_conv_norm, conv_w_in=m_conv_w_in, conv_w=m_conv_w,
                conv_w_out=m_conv_w_out, ffn_norm=m_ffn_norm, ffn_w_gate=m_ffn_w_gate, ffn_w_up=m_ffn_w_up,
                ffn_w_down=m_ffn_w_down, final_norm=m_final_norm)
    v_in = dict(mla_norm=v_mla_norm, mla_w_in=v_mla_w_in, mla_g_cq=v_mla_g_cq, mla_g_ckv=v_mla_g_ckv, mla_w_uq=v_mla_w_uq,
                mla_w_ukv=v_mla_w_ukv, mla_w_o=v_mla_w_o, conv_norm=v_conv_norm, conv_w_in=v_conv_w_in, conv_w=v_conv_w,
                conv_w_out=v_conv_w_out, ffn_norm=v_ffn_norm, ffn_w_gate=v_ffn_w_gate, ffn_w_up=v_ffn_w_up,
                ffn_w_down=v_ffn_w_down, final_norm=v_final_norm)
    big = ["mla_w_in", "mla_w_uq", "mla_w_ukv", "mla_w_o", "conv_w_in", "conv_w_out", "ffn_w_gate", "ffn_w_up", "ffn_w_down"]
    order = list(weights)

    t, d = x.shape[1], x.shape[2]
    h0 = x.reshape(t, d)
    target = loss_target.reshape(t, d)
    cos, sin = _rope_tables(positions.reshape(t))

    def rows2d(a):
        return a.reshape(-1, a.shape[-1])

    gathered = dict(zip(big, gather_weight_shards([rows2d(weights[n]).astype(BF16) for n in big])))
    w_in = gathered["mla_w_in"].reshape(-1, gathered["mla_w_in"].shape[-1])
    w_uq = _unstack_cols(gathered["mla_w_uq"])
    w_ukv = _unstack_cols(gathered["mla_w_ukv"])
    w_o = gathered["mla_w_o"].reshape(-1, d)
    cw_in = _unstack_cols(gathered["conv_w_in"])
    cw_out = gathered["conv_w_out"].reshape(-1, d)
    wg_all, wu_all, wd_all = gathered["ffn_w_gate"], gathered["ffn_w_up"], gathered["ffn_w_down"]

    chip = 2 * lax.axis_index("x") + lax.axis_index("y")
    core = lax.axis_index("c")
    d4 = d // N_CHIPS
    first_core = (core == 0).astype(F32)

    def place_shard(shard):
        full = jnp.zeros((shard.shape[0], d), F32)
        return lax.dynamic_update_slice(full, shard * first_core, (0, chip * d4))

    def pack_rows(rows):
        idx = lax.broadcasted_iota(jnp.int32, (SMALL_ROWS, d), 0)
        out = jnp.zeros((SMALL_ROWS, d), F32)
        for r, row in enumerate(rows):
            out = out + jnp.where(idx == r, row, 0.0)
        return out

    cw = place_shard(conv_w.reshape(3, d4))
    pre = all_reduce_small("all_gather_conv_small", pack_rows([place_shard(conv_norm.reshape(1, d4)), cw[0:1], cw[1:2], cw[2:3]]))
    conv_norm_full = pre[0:1]
    conv_w_full = pre[1:4]

    a0 = rms_fwd("mla_norm_fwd", h0, mla_norm)
    proj = linear("mla_in_proj", a0, w_in, F32)
    cq, ckv, kr = mla_mid("mla_mid", proj, mla_g_cq, mla_g_ckv, cos, sin)
    q = linear("mla_q_up", cq, w_uq, F32)
    kv = linear("mla_kv_up", ckv, w_ukv, BF16)
    qh, kh, vh = qkv_heads("qkv_heads", q, kv, kr, cos, sin)
    attn, lse = attention_fwd("attention_fwd", qh, kh, vh)
    h1 = linear("mla_out_proj", attn, w_o, F32, resid=h0)

    def ffn_forward(tag, h, layer):
        a = rms_fwd(f"ffn{tag}_norm_fwd", h, ffn_norm[layer:layer + 1])
        g, u, z = ffn_up(f"ffn{tag}_up", a, wg_all, wu_all, layer)
        return a, g, u, z, ffn_down(f"ffn{tag}_down", z, wd_all, layer, h)

    a1, g0, u0, z0, h2 = ffn_forward(0, h1, 0)
    a2 = rms_fwd("conv_norm_fwd", h2, conv_norm_full)
    bcx = conv_in_proj("conv_in_proj", a2, cw_in)
    yc = conv_fwd("conv_fwd", bcx, conv_w_full)
    h3 = linear("conv_out_proj", yc, cw_out, F32, resid=h2)
    a3, g1, u1, z1, h4 = ffn_forward(1, h3, 1)
    dh4, d_final_norm, loss_local = loss_head("loss_head", h4, final_norm.reshape(1, d), target)

    n_layers = ffn_norm.shape[0]

    def ffn_backward(tag, dh, h, layer, a, g, u, z, prev):
        dg, du = ffn_bwd_hidden(f"ffn{tag}_bwd_hidden", dh, wd_all, layer, g, u)
        d_wd = ffn_wgrad_down(f"ffn{tag}_wgrad_down", z, dh, layer, n_layers, None if prev is None else prev[2])
        da = ffn_bwd_input(f"ffn{tag}_bwd_input", dg, du, wg_all, wu_all, layer, d)
        d_wg, d_wu = ffn_wgrad_up(f"ffn{tag}_wgrad_up", a, dg, du, layer, n_layers, None if prev is None else prev[:2])
        dh_prev, d_norm = rms_bwd(f"ffn{tag}_norm_bwd", h, ffn_norm[layer:layer + 1], da, dh)
        return dh_prev, d_norm, (d_wg, d_wu, d_wd)

    dh3, d_ffn_norm1, ffn_grads = ffn_backward(1, dh4, h3, 1, a3, g1, u1, z1, None)

    dyc = linear_nt("conv_out_bwd_input", dh3, cw_out, F32)
    d_cw_out = wgrad("conv_out_wgrad", yc, dh3)
    dbcx, d_conv_w = conv_bwd("conv_bwd", bcx, conv_w_full, dyc)
    da2 = conv_in_bwd_input("conv_in_bwd_input", dbcx, cw_in)
    d_cw_in = conv_in_wgrad("conv_in_wgrad", a2, dbcx)
    dh2, d_conv_norm = rms_bwd("conv_norm_bwd", h2, conv_norm_full, da2, dh3)

    dh1, d_ffn_norm0, ffn_grads = ffn_backward(0, dh2, h1, 0, a1, g0, u0, z0, ffn_grads)

    d_attn = linear_nt("mla_out_bwd_input", dh1, w_o, F32)
    d_w_o = wgrad("mla_out_wgrad", attn, dh1)
    delta = attention_delta("attention_delta", d_attn, attn)
    dqh, dkh, dvh = attention_bwd("attention_bwd", qh, kh, vh, d_attn, lse, delta)
    dq, dkv, dkr = qkv_heads_bwd("qkv_heads_bwd", dqh, dkh, dvh, cos, sin)
    dcq = linear_nt("mla_q_up_bwd_input", dq, w_uq, F32)
    d_w_uq = wgrad("mla_q_up_wgrad", cq, dq)
    dckv = linear_nt("mla_kv_up_bwd_input", dkv, w_ukv, F32)
    d_w_ukv = wgrad("mla_kv_up_wgrad", ckv, dkv)
    dproj, d_g_cq, d_g_ckv = mla_mid_bwd("mla_mid_bwd", proj, mla_g_cq, mla_g_ckv, dcq, dckv, dkr, cos, sin)
    da0 = linear_nt("mla_in_bwd_input", dproj, w_in, F32)
    d_w_in = wgrad("mla_in_wgrad", a0, dproj)
    grad_x, d_mla_norm = rms_bwd("mla_norm_bwd", h0, mla_norm, da0, dh1)

    stacked = dict(
        mla_w_in=d_w_in.reshape(N_CHIPS, -1, d_w_in.shape[-1]), mla_w_uq=_stack_cols(d_w_uq), mla_w_ukv=_stack_cols(d_w_ukv),
        mla_w_o=d_w_o.reshape(N_CHIPS, -1, d), conv_w_in=_stack_cols(d_cw_in), conv_w_out=d_cw_out.reshape(N_CHIPS, -1, d),
        ffn_w_gate=ffn_grads[0], ffn_w_up=ffn_grads[1], ffn_w_down=ffn_grads[2])
    local = [stacked[n] for n in big]
    from_sibling = sibling_swap_halves(local)
    pair_sums = [add_halves(f"pair_sum_{n}", g, r) for n, g, r in zip(big, local, from_sibling)]
    from_chips = scatter_to_owner_chips(pair_sums)
    my_halves = [sum_chips(f"chip_sum_{n}", p) for n, p in zip(big, from_chips)]
    grads = dict(zip(big, sibling_join_halves(my_halves)))

    def pad_row(v):
        return jnp.pad(v, ((0, 0), (0, d - v.shape[1])))

    small = all_reduce_small("all_reduce_small_grads", pack_rows([
        d_mla_norm, pad_row(d_g_cq), pad_row(d_g_ckv), d_ffn_norm0, d_ffn_norm1, d_final_norm, d_conv_norm,
        d_conv_w[0:1], d_conv_w[1:2], d_conv_w[2:3], jnp.broadcast_to(loss_local, (1, d))]))
    loss = small[10, 0]
    grads["mla_norm"] = small[0:1]
    grads["mla_g_cq"] = small[1:2, :mla_g_cq.shape[1]]
    grads["mla_g_ckv"] = small[2:3, :mla_g_ckv.shape[1]]
    grads["ffn_norm"] = small[3:5]
    grads["final_norm"] = small[5:6]
    grads["conv_norm"] = lax.dynamic_slice(small[6:7], (0, chip * d4), (1, d4))
    grads["conv_w"] = lax.dynamic_slice(small[7:10], (0, chip * d4), (3, d4))

    outs_g, outs_d, outs_m, outs_v = [], [], [], []
    for n in order:
        w = weights[n]
        delta_w, new_m, new_v = adamw(f"adamw_{n}", rows2d(w), grads[n].reshape(rows2d(w).shape), rows2d(m_in[n]), rows2d(v_in[n]))
        outs_g.append(grads[n].reshape(w.shape))
        outs_d.append(delta_w.reshape(w.shape))
        outs_m.append(new_m.reshape(w.shape))
        outs_v.append(new_v.reshape(w.shape))
    return (loss, grad_x.reshape(x.shape), *outs_g, *outs_d, *outs_m, *outs_v)
```

```python
import math

import jax
import jax.numpy as jnp
from jax import lax
from jax.experimental import pallas as pl
from jax.experimental.pallas import tpu as pltpu

F32 = jnp.float32
BF16 = jnp.bfloat16
S = jax.ShapeDtypeStruct

N_HEADS = 8
NOPE = 128
ROPE = 64
HALF = ROPE // 2
VDIM = 128
QK = NOPE + ROPE
CHUNK = 64
ROPE_THETA = 10000.0
RMS_EPS = 1e-6
ADAM_LR = 0.001
ADAM_B1 = 0.9
ADAM_B2 = 0.999
ADAM_EPS = 1e-08
ADAM_WD = 0.01
ADAM_STEP = 10

N_CHIPS = 4
N_DEV = 8
MASK_VALUE = -1e30
VMEM_LIMIT = 48 * 1024 * 1024
ATT_BLOCK = 512
SMALL_ROWS = 16

_NN = (((1,), (0,)), ((), ()))
_NT = (((1,), (1,)), ((), ()))
_TN = (((0,), (0,)), ((), ()))
MESH = pl.DeviceIdType.MESH
ANY = pl.BlockSpec(memory_space=pl.ANY)


def _params(n_axes):
    return pltpu.CompilerParams(dimension_semantics=("arbitrary",) * n_axes, vmem_limit_bytes=VMEM_LIMIT)


def _tile(n, cap, mult=8):
    for t in range(min(cap, n), 0, -1):
        if n % t == 0 and t % mult == 0:
            return t
    return n


def _sigmoid(x):
    return 1.0 / (1.0 + jnp.exp(-x))


def _mm(name, a_ops, b_ops, products, dims, grid, k_axis, outs, acc_shape, epilogue, extra_ops=()):
    na, nb, ne, no = len(a_ops), len(b_ops), len(extra_ops), len(outs)
    n_acc = 1 + max(c for _, _, c in products)
    nk = 1 if k_axis is None else grid[k_axis]

    def body(*refs):
        a_refs = refs[:na]
        b_refs = refs[na:na + nb]
        e_refs = refs[na + nb:na + nb + ne]
        o_refs = refs[na + nb + ne:na + nb + ne + no]
        acc_refs = refs[na + nb + ne + no:]

        def partial_sums():
            vals = [None] * n_acc
            for ai, bi, ci in products:
                d = lax.dot_general(a_refs[ai][...].astype(BF16), b_refs[bi][...].astype(BF16), dims,
                                    preferred_element_type=F32)
                vals[ci] = d if vals[ci] is None else vals[ci] + d
            return vals

        if nk == 1:
            epilogue(partial_sums(), e_refs, o_refs)
        else:
            k = pl.program_id(k_axis)

            @pl.when(k == 0)
            def _():
                for acc in acc_refs:
                    acc[...] = jnp.zeros_like(acc)

            for acc, v in zip(acc_refs, partial_sums()):
                acc[...] += v

            @pl.when(k == nk - 1)
            def _():
                epilogue([acc[...] for acc in acc_refs], e_refs, o_refs)

    ops = list(a_ops) + list(b_ops) + list(extra_ops)
    return pl.pallas_call(
        body, name=name, grid=grid,
        in_specs=[s for _, s in ops], out_specs=[s for _, s in outs], out_shape=[o for o, _ in outs],
        scratch_shapes=[pltpu.VMEM(acc_shape, F32) for _ in range(n_acc if nk > 1 else 0)],
        compiler_params=_params(len(grid)),
    )(*[a for a, _ in ops])


def _store(accs, e_refs, o_refs):
    o_refs[0][...] = accs[0].astype(o_refs[0].dtype)


def _store_plus_residual(accs, e_refs, o_refs):
    o_refs[0][...] = (e_refs[0][...] + accs[0]).astype(o_refs[0].dtype)


def linear(name, x, w, out_dtype, resid=None):
    t, k = x.shape
    n = w.shape[1]
    tm = _tile(t, 512)
    tn = n if n <= 2048 else _tile(n, 1024, 128)
    extra = [] if resid is None else [(resid, pl.BlockSpec((tm, tn), lambda j, i: (i, j)))]
    return _mm(name, [(x, pl.BlockSpec((tm, k), lambda j, i: (i, 0)))], [(w, pl.BlockSpec((k, tn), lambda j, i: (0, j)))],
               [(0, 0, 0)], _NN, (n // tn, t // tm), None,
               [(S((t, n), out_dtype), pl.BlockSpec((tm, tn), lambda j, i: (i, j)))], None,
               _store if resid is None else _store_plus_residual, extra)[0]


def linear_nt(name, dy, w, out_dtype):
    t, n = dy.shape
    k = w.shape[0]
    tm = _tile(t, 512)
    tc = n if n <= 2048 else _tile(n, 1024, 128)
    return _mm(name, [(dy, pl.BlockSpec((tm, tc), lambda i, c: (i, c)))], [(w, pl.BlockSpec((k, tc), lambda i, c: (0, c)))],
               [(0, 0, 0)], _NT, (t // tm, n // tc), 1,
               [(S((t, k), out_dtype), pl.BlockSpec((tm, k), lambda i, c: (i, 0)))], (tm, k), _store)[0]


def wgrad(name, x, dy):
    t, k = x.shape
    n = dy.shape[1]
    tk = _tile(t, 512)
    tn = n if n <= 1024 else _tile(n, 1024, 128)
    return _mm(name, [(x, pl.BlockSpec((tk, k), lambda j, s: (s, 0)))], [(dy, pl.BlockSpec((tk, tn), lambda j, s: (s, j)))],
               [(0, 0, 0)], _TN, (n // tn, t // tk), 1,
               [(S((k, n), BF16), pl.BlockSpec((k, tn), lambda j, s: (0, j)))], (k, tn), _store)[0]


def _resident(shape, index_map):
    return pl.BlockSpec(shape, index_map, pipeline_mode=pl.Buffered(1))


def ffn_up(name, a, wg_all, wu_all, layer):
    t, d = a.shape
    f4 = wg_all.shape[2]
    tm = _tile(t, 512)
    w_spec = _resident((N_CHIPS, d, f4), lambda i: (0, layer, 0))
    h_spec = pl.BlockSpec((N_CHIPS, tm, f4), lambda i: (0, i, 0))

    def body(a_ref, wg_ref, wu_ref, g_ref, u_ref, z_ref):
        av = a_ref[...]
        for k in range(N_CHIPS):
            g = jnp.dot(av, wg_ref[k], preferred_element_type=F32)
            u = jnp.dot(av, wu_ref[k], preferred_element_type=F32)
            g_ref[k] = g.astype(BF16)
            u_ref[k] = u.astype(BF16)
            z_ref[k] = (g * _sigmoid(g) * u).astype(BF16)

    return pl.pallas_call(
        body, name=name, grid=(t // tm,), in_specs=[pl.BlockSpec((tm, d), lambda i: (i, 0)), w_spec, w_spec],
        out_specs=[h_spec] * 3, out_shape=[S((N_CHIPS, t, f4), BF16)] * 3, compiler_params=_params(1))(a, wg_all, wu_all)


def ffn_down(name, z, wd_all, layer, resid):
    _, t, f4 = z.shape
    d = wd_all.shape[2]
    tm = _tile(t, 512)
    row = pl.BlockSpec((tm, d), lambda i: (i, 0))

    def body(z_ref, wd_ref, r_ref, o_ref):
        acc = r_ref[...]
        for k in range(N_CHIPS):
            acc = acc + jnp.dot(z_ref[k], wd_ref[k], preferred_element_type=F32)
        o_ref[...] = acc

    return pl.pallas_call(
        body, name=name, grid=(t // tm,),
        in_specs=[pl.BlockSpec((N_CHIPS, tm, f4), lambda i: (0, i, 0)), _resident((N_CHIPS, f4, d), lambda i: (0, layer, 0)), row],
        out_specs=row, out_shape=S((t, d), F32), compiler_params=_params(1))(z, wd_all, resid)


def ffn_bwd_hidden(name, dh, wd_all, layer, g, u):
    t, d = dh.shape
    f4 = g.shape[2]
    tm = _tile(t, 512)
    h_spec = pl.BlockSpec((N_CHIPS, tm, f4), lambda i: (0, i, 0))

    def body(dh_ref, wd_ref, g_ref, u_ref, dg_ref, du_ref):
        dhb = dh_ref[...].astype(BF16)
        for k in range(N_CHIPS):
            dz = lax.dot_general(dhb, wd_ref[k], _NT, preferred_element_type=F32)
            gv = g_ref[k].astype(F32)
            uv = u_ref[k].astype(F32)
            sg = _sigmoid(gv)
            dg_ref[k] = (dz * uv * (sg * (1.0 + gv * (1.0 - sg)))).astype(BF16)
            du_ref[k] = (dz * (gv * sg)).astype(BF16)

    return pl.pallas_call(
        body, name=name, grid=(t // tm,),
        in_specs=[pl.BlockSpec((tm, d), lambda i: (i, 0)), _resident((N_CHIPS, f4, d), lambda i: (0, layer, 0)), h_spec, h_spec],
        out_specs=[h_spec] * 2, out_shape=[S((N_CHIPS, t, f4), BF16)] * 2, compiler_params=_params(1))(dh, wd_all, g, u)


def ffn_bwd_input(name, dg, du, wg_all, wu_all, layer, d):
    _, t, f4 = dg.shape
    tm = _tile(t, 512)
    h_spec = pl.BlockSpec((N_CHIPS, tm, f4), lambda i: (0, i, 0))
    w_spec = _resident((N_CHIPS, d, f4), lambda i: (0, layer, 0))

    def body(dg_ref, du_ref, wg_ref, wu_ref, da_ref):
        acc = jnp.zeros((tm, d), F32)
        for k in range(N_CHIPS):
            acc = acc + lax.dot_general(dg_ref[k], wg_ref[k], _NT, preferred_element_type=F32)
            acc = acc + lax.dot_general(du_ref[k], wu_ref[k], _NT, preferred_element_type=F32)
        da_ref[...] = acc

    return pl.pallas_call(
        body, name=name, grid=(t // tm,), in_specs=[h_spec, h_spec, w_spec, w_spec],
        out_specs=pl.BlockSpec((tm, d), lambda i: (i, 0)), out_shape=S((t, d), F32), compiler_params=_params(1))(dg, du, wg_all, wu_all)


def ffn_wgrad_up(name, a, dg, du, layer, n_layers, prev=None):
    t, d = a.shape
    f4 = dg.shape[2]
    tk = _tile(t, 512)
    nt = t // tk
    h_spec = pl.BlockSpec((None, tk, f4), lambda k, s: (k, s, 0))
    o_spec = pl.BlockSpec((None, d, f4), lambda k, s: (k, layer, 0))

    def body(a_ref, dg_ref, du_ref, *rest):
        og_ref, ou_ref, accg, accu = rest[-4:]
        s = pl.program_id(1)

        @pl.when(s == 0)
        def _():
            accg[...] = jnp.zeros_like(accg)
            accu[...] = jnp.zeros_like(accu)

        av = a_ref[...]
        accg[...] += lax.dot_general(av, dg_ref[...], _TN, preferred_element_type=F32)
        accu[...] += lax.dot_general(av, du_ref[...], _TN, preferred_element_type=F32)

        @pl.when(s == nt - 1)
        def _():
            og_ref[...] = accg[...].astype(BF16)
            ou_ref[...] = accu[...].astype(BF16)

    out = S((N_CHIPS, n_layers * d, f4), BF16)
    extra = [] if prev is None else list(prev)
    return pl.pallas_call(
        body, name=name, grid=(N_CHIPS, nt),
        in_specs=[pl.BlockSpec((tk, d), lambda k, s: (s, 0)), h_spec, h_spec] + [ANY] * len(extra),
        out_specs=[o_spec, o_spec], out_shape=[out, out],
        scratch_shapes=[pltpu.VMEM((d, f4), F32), pltpu.VMEM((d, f4), F32)],
        input_output_aliases={3 + i: i for i in range(len(extra))},
        compiler_params=_params(2))(a, dg, du, *extra)


def ffn_wgrad_down(name, z, dh, layer, n_layers, prev=None):
    _, t, f4 = z.shape
    d = dh.shape[1]
    tk = _tile(t, 512)
    nt = t // tk

    def body(z_ref, dh_ref, *rest):
        o_ref, acc = rest[-2:]
        s = pl.program_id(0)

        @pl.when(s == 0)
        def _():
            acc[...] = jnp.zeros_like(acc)

        dhb = dh_ref[...].astype(BF16)
        for k in range(N_CHIPS):
            acc[k] += lax.dot_general(z_ref[k], dhb, _TN, preferred_element_type=F32)

        @pl.when(s == nt - 1)
        def _():
            o_ref[...] = acc[...].astype(BF16)

    extra = [] if prev is None else [prev]
    return pl.pallas_call(
        body, name=name, grid=(nt,),
        in_specs=[pl.BlockSpec((N_CHIPS, tk, f4), lambda s: (0, s, 0)), pl.BlockSpec((tk, d), lambda s: (s, 0))] + [ANY] * len(extra),
        out_specs=pl.BlockSpec((N_CHIPS, f4, d), lambda s: (0, layer, 0)), out_shape=S((N_CHIPS, n_layers * f4, d), BF16),
        scratch_shapes=[pltpu.VMEM((N_CHIPS, f4, d), F32)],
        input_output_aliases={2 + i: 0 for i in range(len(extra))},
        compiler_params=_params(1))(z, dh, *extra)


def conv_in_proj(name, a, w):
    t, d = a.shape
    tm = _tile(t, 512)
    return _mm(name, [(a, pl.BlockSpec((tm, d), lambda j, i: (i, 0)))], [(w, pl.BlockSpec((d, d), lambda j, i: (0, j)))],
               [(0, 0, 0)], _NN, (3, t // tm), None,
               [(S((3, t, d), F32), pl.BlockSpec((None, tm, d), lambda j, i: (j, i, 0)))], None, _store)[0]


def conv_in_bwd_input(name, dbcx, w):
    _, t, d = dbcx.shape
    tm = _tile(t, 512)

    def body(g_ref, w_ref, o_ref):
        acc = jnp.zeros((tm, d), F32)
        for j in range(3):
            acc = acc + lax.dot_general(g_ref[j], w_ref[:, j * d:(j + 1) * d], _NT, preferred_element_type=F32)
        o_ref[...] = acc

    return pl.pallas_call(
        body, name=name, grid=(t // tm,),
        in_specs=[pl.BlockSpec((3, tm, d), lambda i: (0, i, 0)), _resident((d, 3 * d), lambda i: (0, 0))],
        out_specs=pl.BlockSpec((tm, d), lambda i: (i, 0)), out_shape=S((t, d), F32), compiler_params=_params(1))(dbcx, w)


def conv_in_wgrad(name, a, dbcx):
    t, d = a.shape
    tk = _tile(t, 512)
    return _mm(name, [(a, pl.BlockSpec((tk, d), lambda j, s: (s, 0)))], [(dbcx, pl.BlockSpec((None, tk, d), lambda j, s: (j, s, 0)))],
               [(0, 0, 0)], _TN, (3, t // tk), 1,
               [(S((d, 3 * d), BF16), pl.BlockSpec((d, d), lambda j, s: (0, j)))], (d, d), _store)[0]


def _rstd(x):
    return lax.rsqrt(jnp.mean(x * x, axis=-1, keepdims=True) + RMS_EPS)


def _rms_bwd(x, g, dy):
    r = _rstd(x)
    xhat = x * r
    dgain = jnp.sum(dy * xhat, axis=0, keepdims=True)
    dxh = dy * g
    dx = r * (dxh - xhat * jnp.mean(dxh * xhat, axis=-1, keepdims=True))
    return dx, dgain


def rms_fwd(name, h, g):
    t, d = h.shape
    tr = _tile(t, 512)

    def body(h_ref, g_ref, a_ref):
        x = h_ref[...]
        a_ref[...] = (x * _rstd(x) * g_ref[...]).astype(BF16)

    return pl.pallas_call(
        body, name=name, grid=(t // tr,),
        in_specs=[pl.BlockSpec((tr, d), lambda i: (i, 0)), pl.BlockSpec((1, d), lambda i: (0, 0))],
        out_specs=pl.BlockSpec((tr, d), lambda i: (i, 0)), out_shape=S((t, d), BF16), compiler_params=_params(1))(h, g)


def rms_bwd(name, h, g, da, dh_in):
    t, d = h.shape
    tr = _tile(t, 512)

    def body(h_ref, g_ref, da_ref, dhi_ref, dho_ref, dg_ref):
        dx, dgain = _rms_bwd(h_ref[...], g_ref[...], da_ref[...])
        dho_ref[...] = dhi_ref[...] + dx

        @pl.when(pl.program_id(0) == 0)
        def _():
            dg_ref[...] = jnp.zeros_like(dg_ref)

        dg_ref[...] += dgain

    row = pl.BlockSpec((tr, d), lambda i: (i, 0))
    vec = pl.BlockSpec((1, d), lambda i: (0, 0))
    return pl.pallas_call(
        body, name=name, grid=(t // tr,), in_specs=[row, vec, row, row], out_specs=[row, vec],
        out_shape=[S((t, d), F32), S((1, d), F32)], compiler_params=_params(1))(h, g, da, dh_in)


def loss_head(name, h, g, target):
    t, d = h.shape
    tr = _tile(t, 512)

    def body(h_ref, g_ref, t_ref, dh_ref, dg_ref, loss_ref):
        x = h_ref[...]
        g = g_ref[...]
        r = _rstd(x)
        xhat = x * r
        err = xhat * g - t_ref[...]
        dy = err * (1.0 / d)
        dxh = dy * g
        dh_ref[...] = r * (dxh - xhat * jnp.mean(dxh * xhat, axis=-1, keepdims=True))

        @pl.when(pl.program_id(0) == 0)
        def _():
            dg_ref[...] = jnp.zeros_like(dg_ref)
            loss_ref[...] = jnp.zeros_like(loss_ref)

        dg_ref[...] += jnp.sum(dy * xhat, axis=0, keepdims=True)
        per_token = jnp.mean(err * err, axis=-1, keepdims=True)
        loss_ref[...] += 0.5 * jnp.sum(per_token, axis=0, keepdims=True)

    row = pl.BlockSpec((tr, d), lambda i: (i, 0))
    vec = pl.BlockSpec((1, d), lambda i: (0, 0))
    one = pl.BlockSpec((1, 1), lambda i: (0, 0))
    return pl.pallas_call(
        body, name=name, grid=(t // tr,), in_specs=[row, vec, row], out_specs=[row, vec, one],
        out_shape=[S((t, d), F32), S((1, d), F32), S((1, 1), F32)], compiler_params=_params(1))(h, g, target)


def mla_mid(name, proj, g_cq, g_ckv, cos, sin):
    t, n = proj.shape
    ql, kl = g_cq.shape[1], g_ckv.shape[1]
    tr = _tile(t, 512)

    def body(p_ref, gq_ref, gk_ref, c_ref, s_ref, cq_ref, ckv_ref, kr_ref):
        xq = p_ref[:, 0:ql]
        cq_ref[...] = (xq * _rstd(xq) * gq_ref[...]).astype(BF16)
        xk = p_ref[:, ql:ql + kl]
        ckv_ref[...] = (xk * _rstd(xk) * gk_ref[...]).astype(BF16)
        k1 = p_ref[:, ql + kl:ql + kl + HALF]
        k2 = p_ref[:, ql + kl + HALF:ql + kl + ROPE]
        c = c_ref[...]
        s = s_ref[...]
        kr_ref[:, 0:HALF] = k1 * c - k2 * s
        kr_ref[:, HALF:ROPE] = k1 * s + k2 * c

    def row(w):
        return pl.BlockSpec((tr, w), lambda i: (i, 0))

    def vec(w):
        return pl.BlockSpec((1, w), lambda i: (0, 0))

    return pl.pallas_call(
        body, name=name, grid=(t // tr,), in_specs=[row(n), vec(ql), vec(kl), row(HALF), row(HALF)],
        out_specs=[row(ql), row(kl), row(ROPE)], out_shape=[S((t, ql), BF16), S((t, kl), BF16), S((t, ROPE), F32)],
        compiler_params=_params(1))(proj, g_cq, g_ckv, cos, sin)


def mla_mid_bwd(name, proj, g_cq, g_ckv, dcq, dckv, dkr, cos, sin):
    t, n = proj.shape
    ql, kl = g_cq.shape[1], g_ckv.shape[1]
    tr = _tile(t, 512)

    def body(p_ref, gq_ref, gk_ref, dcq_ref, dckv_ref, dkr_ref, c_ref, s_ref, dp_ref, dgq_ref, dgk_ref):
        dxq, dgq = _rms_bwd(p_ref[:, 0:ql], gq_ref[...], dcq_ref[...])
        dp_ref[:, 0:ql] = dxq.astype(BF16)
        dxk, dgk = _rms_bwd(p_ref[:, ql:ql + kl], gk_ref[...], dckv_ref[...])
        dp_ref[:, ql:ql + kl] = dxk.astype(BF16)
        d1 = dkr_ref[:, 0:HALF]
        d2 = dkr_ref[:, HALF:ROPE]
        c = c_ref[...]
        s = s_ref[...]
        dp_ref[:, ql + kl:ql + kl + HALF] = (d1 * c + d2 * s).astype(BF16)
        dp_ref[:, ql + kl + HALF:ql + kl + ROPE] = (d2 * c - d1 * s).astype(BF16)

        @pl.when(pl.program_id(0) == 0)
        def _():
            dgq_ref[...] = jnp.zeros_like(dgq_ref)
            dgk_ref[...] = jnp.zeros_like(dgk_ref)

        dgq_ref[...] += dgq
        dgk_ref[...] += dgk

    def row(w):
        return pl.BlockSpec((tr, w), lambda i: (i, 0))

    def vec(w):
        return pl.BlockSpec((1, w), lambda i: (0, 0))

    return pl.pallas_call(
        body, name=name, grid=(t // tr,),
        in_specs=[row(n), vec(ql), vec(kl), row(ql), row(kl), row(ROPE), row(HALF), row(HALF)],
        out_specs=[row(n), vec(ql), vec(kl)], out_shape=[S((t, n), BF16), S((1, ql), F32), S((1, kl), F32)],
        compiler_params=_params(1))(proj, g_cq, g_ckv, dcq, dckv, dkr, cos, sin)


def qkv_heads(name, q, kv, kr, cos, sin):
    t = q.shape[0]
    tr = _tile(t, 256)

    def body(q_ref, kv_ref, kr_ref, c_ref, s_ref, qo_ref, ko_ref, vo_ref):
        c = c_ref[...]
        s = s_ref[...]
        krb = kr_ref[...].astype(BF16)
        for h in range(N_HEADS):
            q0 = h * QK
            qo_ref[h, :, 0:NOPE] = q_ref[:, q0:q0 + NOPE].astype(BF16)
            q1 = q_ref[:, q0 + NOPE:q0 + NOPE + HALF]
            q2 = q_ref[:, q0 + NOPE + HALF:q0 + QK]
            qo_ref[h, :, NOPE:NOPE + HALF] = (q1 * c - q2 * s).astype(BF16)
            qo_ref[h, :, NOPE + HALF:QK] = (q1 * s + q2 * c).astype(BF16)
            k0 = h * (NOPE + VDIM)
            ko_ref[h, :, 0:NOPE] = kv_ref[:, k0:k0 + NOPE]
            ko_ref[h, :, NOPE:QK] = krb
            vo_ref[h] = kv_ref[:, k0 + NOPE:k0 + NOPE + VDIM]

    def row(w):
        return pl.BlockSpec((tr, w), lambda i: (i, 0))

    def heads(w):
        return pl.BlockSpec((N_HEADS, tr, w), lambda i: (0, i, 0))

    return pl.pallas_call(
        body, name=name, grid=(t // tr,),
        in_specs=[row(N_HEADS * QK), row(N_HEADS * (NOPE + VDIM)), row(ROPE), row(HALF), row(HALF)],
        out_specs=[heads(QK), heads(QK), heads(VDIM)],
        out_shape=[S((N_HEADS, t, QK), BF16), S((N_HEADS, t, QK), BF16), S((N_HEADS, t, VDIM), BF16)],
        compiler_params=_params(1))(q, kv, kr, cos, sin)


def qkv_heads_bwd(name, dq_h, dk_h, dv_h, cos, sin):
    t = dq_h.shape[1]
    tr = _tile(t, 256)

    def body(dq_ref, dk_ref, dv_ref, c_ref, s_ref, q_ref, kv_ref, kr_ref):
        c = c_ref[...]
        s = s_ref[...]
        dkr = jnp.zeros((tr, ROPE), F32)
        for h in range(N_HEADS):
            q0 = h * QK
            q_ref[:, q0:q0 + NOPE] = dq_ref[h, :, 0:NOPE].astype(BF16)
            d1 = dq_ref[h, :, NOPE:NOPE + HALF]
            d2 = dq_ref[h, :, NOPE + HALF:QK]
            q_ref[:, q0 + NOPE:q0 + NOPE + HALF] = (d1 * c + d2 * s).astype(BF16)
            q_ref[:, q0 + NOPE + HALF:q0 + QK] = (d2 * c - d1 * s).astype(BF16)
            k0 = h * (NOPE + VDIM)
            kv_ref[:, k0:k0 + NOPE] = dk_ref[h, :, 0:NOPE].astype(BF16)
            kv_ref[:, k0 + NOPE:k0 + NOPE + VDIM] = dv_ref[h].astype(BF16)
            dkr = dkr + dk_ref[h, :, NOPE:QK]
        kr_ref[...] = dkr

    def row(w):
        return pl.BlockSpec((tr, w), lambda i: (i, 0))

    def heads(w):
        return pl.BlockSpec((N_HEADS, tr, w), lambda i: (0, i, 0))

    return pl.pallas_call(
        body, name=name, grid=(t // tr,),
        in_specs=[heads(QK), heads(QK), heads(VDIM), row(HALF), row(HALF)],
        out_specs=[row(N_HEADS * QK), row(N_HEADS * (NOPE + VDIM)), row(ROPE)],
        out_shape=[S((t, N_HEADS * QK), BF16), S((t, N_HEADS * (NOPE + VDIM)), BF16), S((t, ROPE), F32)],
        compiler_params=_params(1))(dq_h, dk_h, dv_h, cos, sin)


def _chunk_mask_t(q_start, k_start, bq, bk):
    kc = (k_start + lax.broadcasted_iota(jnp.int32, (bk, bq), 0)) // CHUNK
    qc = (q_start + lax.broadcasted_iota(jnp.int32, (bk, bq), 1)) // CHUNK
    return kc <= qc


def attention_fwd(name, q, k, v):
    nh, t, _ = q.shape
    blk = ATT_BLOCK
    scale = 1.0 / math.sqrt(QK)

    def body(q_ref, k_ref, v_ref, o_ref, lse_ref, m_ref, l_ref, acc_ref):
        i = pl.program_id(1)
        qv = q_ref[...]
        m_ref[...] = jnp.full_like(m_ref, MASK_VALUE)
        l_ref[...] = jnp.zeros_like(l_ref)
        acc_ref[...] = jnp.zeros_like(acc_ref)

        def block(j, masked):
            start = pl.multiple_of(j * blk, blk)
            kb = k_ref[pl.ds(start, blk), :]
            vb = v_ref[pl.ds(start, blk), :]
            s = lax.dot_general(kb, qv, _NT, preferred_element_type=F32) * scale
            if masked:
                s = jnp.where(_chunk_mask_t(i * blk, j * blk, blk, blk), s, MASK_VALUE)
            m_old = m_ref[...]
            m_new = jnp.maximum(m_old, jnp.max(s, axis=0, keepdims=True))
            p = jnp.exp(s - m_new)
            alpha = jnp.exp(m_old - m_new)
            l_ref[...] = alpha * l_ref[...] + jnp.sum(p, axis=0, keepdims=True)
            acc_ref[...] = alpha * acc_ref[...] + lax.dot_general(vb, p.astype(BF16), _TN, preferred_element_type=F32)
            m_ref[...] = m_new

        def step(j, carry):
            block(j, False)
            return carry

        lax.fori_loop(0, i, step, 0)
        block(i, True)
        l = l_ref[...]
        o_ref[...] = (acc_ref[...] / l).T
        lse_ref[...] = m_ref[...] + jnp.log(l)

    return pl.pallas_call(
        body, name=name, grid=(nh, t // blk),
        in_specs=[pl.BlockSpec((None, blk, QK), lambda h, i: (h, i, 0)), pl.BlockSpec((None, t, QK), lambda h, i: (h, 0, 0)),
                  pl.BlockSpec((None, t, VDIM), lambda h, i: (h, 0, 0))],
        out_specs=[pl.BlockSpec((blk, VDIM), lambda h, i: (i, h)), pl.BlockSpec((None, None, 1, blk), lambda h, i: (h, i, 0, 0))],
        out_shape=[S((t, nh * VDIM), F32), S((nh, t // blk, 1, blk), F32)],
        scratch_shapes=[pltpu.VMEM((1, blk), F32), pltpu.VMEM((1, blk), F32), pltpu.VMEM((VDIM, blk), F32)],
        compiler_params=_params(2))(q, k, v)


def attention_delta(name, do, o):
    t = do.shape[0]
    blk = ATT_BLOCK

    def body(do_ref, o_ref, d_ref):
        d_ref[...] = jnp.sum((do_ref[...] * o_ref[...]).T, axis=0, keepdims=True)

    tile = pl.BlockSpec((blk, VDIM), lambda h, i: (i, h))
    return pl.pallas_call(
        body, name=name, grid=(N_HEADS, t // blk), in_specs=[tile, tile],
        out_specs=pl.BlockSpec((None, None, 1, blk), lambda h, i: (h, i, 0, 0)), out_shape=S((N_HEADS, t // blk, 1, blk), F32),
        compiler_params=_params(2))(do, o)


def attention_bwd(name, q, k, v, do, lse, delta):
    nh, t, _ = q.shape
    blk = ATT_BLOCK
    nq = t // blk
    scale = 1.0 / math.sqrt(QK)

    def body(q_ref, k_ref, v_ref, do_ref, lse_ref, dl_ref, dq_ref, dk_ref, dv_ref):
        j = pl.program_id(1)

        @pl.when(j == 0)
        def _():
            dq_ref[...] = jnp.zeros_like(dq_ref)

        kb = k_ref[...]
        vb = v_ref[...]
        dk_ref[...] = jnp.zeros_like(dk_ref)
        dv_ref[...] = jnp.zeros_like(dv_ref)

        def block(i, masked):
            rows = pl.ds(pl.multiple_of(i * blk, blk), blk)
            qb = q_ref[rows, :]
            dob = do_ref[rows, :].astype(BF16)
            s = lax.dot_general(kb, qb, _NT, preferred_element_type=F32) * scale
            if masked:
                s = jnp.where(_chunk_mask_t(i * blk, j * blk, blk, blk), s, MASK_VALUE)
            p = jnp.exp(s - lse_ref[i])
            dp = lax.dot_general(vb, dob, _NT, preferred_element_type=F32)
            ds = (p * (dp - dl_ref[i]) * scale).astype(BF16)
            dv_ref[...] += jnp.dot(p.astype(BF16), dob, preferred_element_type=F32)
            dk_ref[...] += jnp.dot(ds, qb, preferred_element_type=F32)
            dq_ref[rows, :] += lax.dot_general(ds, kb, _TN, preferred_element_type=F32)

        block(j, True)

        def step(i, carry):
            block(i, False)
            return carry

        lax.fori_loop(j + 1, nq, step, 0)

    head_all = lambda w: pl.BlockSpec((None, t, w), lambda h, j: (h, 0, 0))
    head_blk = lambda w: pl.BlockSpec((None, blk, w), lambda h, j: (h, j, 0))
    stats = pl.BlockSpec((None, nq, 1, blk), lambda h, j: (h, 0, 0, 0))
    return pl.pallas_call(
        body, name=name, grid=(nh, nq),
        in_specs=[head_all(QK), head_blk(QK), head_blk(VDIM), pl.BlockSpec((t, VDIM), lambda h, j: (0, h)), stats, stats],
        out_specs=[head_all(QK), head_blk(QK), head_blk(VDIM)],
        out_shape=[S((nh, t, QK), F32), S((nh, t, QK), F32), S((nh, t, VDIM), F32)],
        compiler_params=_params(2))(q, k, v, do, lse, delta)


def _shift_down(u, s):
    rows = lax.broadcasted_iota(jnp.int32, u.shape, 0)
    return jnp.where(rows >= s, pltpu.roll(u, s, 0), 0.0)


def _shift_up(u, s):
    n = u.shape[0]
    rows = lax.broadcasted_iota(jnp.int32, u.shape, 0)
    return jnp.where(rows < n - s, pltpu.roll(u, n - s, 0), 0.0)


def _conv_specs(t, d, lanes):
    slab = lambda part: pl.BlockSpec((None, t, lanes), lambda j, part=part: (part, 0, j))
    return slab, pl.BlockSpec((3, lanes), lambda j: (0, j)), pl.BlockSpec((t, lanes), lambda j: (0, j))


def conv_fwd(name, bcx, w):
    _, t, d = bcx.shape
    lanes = _tile(d, 128, 128)
    slab, w_spec, col = _conv_specs(t, d, lanes)

    def body(b_ref, c_ref, x_ref, w_ref, y_ref):
        u = c_ref[...] * x_ref[...]
        uc = w_ref[0:1, :] * _shift_down(u, 2) + w_ref[1:2, :] * _shift_down(u, 1) + w_ref[2:3, :] * u
        y_ref[...] = (b_ref[...] * uc).astype(BF16)

    return pl.pallas_call(
        body, name=name, grid=(d // lanes,), in_specs=[slab(0), slab(1), slab(2), w_spec], out_specs=col,
        out_shape=S((t, d), BF16), compiler_params=_params(1))(bcx, bcx, bcx, w)


def conv_bwd(name, bcx, w, dy):
    _, t, d = bcx.shape
    lanes = _tile(d, 128, 128)
    slab, w_spec, col = _conv_specs(t, d, lanes)

    def body(b_ref, c_ref, x_ref, w_ref, dy_ref, d_ref, dw_ref):
        c = c_ref[...]
        x = x_ref[...]
        dyv = dy_ref[...]
        u = c * x
        u1 = _shift_down(u, 1)
        u2 = _shift_down(u, 2)
        w0, w1, w2 = w_ref[0:1, :], w_ref[1:2, :], w_ref[2:3, :]
        d_ref[0] = (dyv * (w0 * u2 + w1 * u1 + w2 * u)).astype(BF16)
        duc = dyv * b_ref[...]
        dw_ref[0:1, :] = jnp.sum(duc * u2, axis=0, keepdims=True)
        dw_ref[1:2, :] = jnp.sum(duc * u1, axis=0, keepdims=True)
        dw_ref[2:3, :] = jnp.sum(duc * u, axis=0, keepdims=True)
        du = w2 * duc + w1 * _shift_up(duc, 1) + w0 * _shift_up(duc, 2)
        d_ref[1] = (du * x).astype(BF16)
        d_ref[2] = (du * c).astype(BF16)

    return pl.pallas_call(
        body, name=name, grid=(d // lanes,), in_specs=[slab(0), slab(1), slab(2), w_spec, col],
        out_specs=[pl.BlockSpec((3, t, lanes), lambda j: (0, 0, j)), w_spec], out_shape=[S((3, t, d), BF16), S((3, d), F32)],
        compiler_params=_params(1))(bcx, bcx, bcx, w, dy)


def adamw(name, w, g, m, v):
    r, c = w.shape
    tr = _tile(r, 512)

    def body(w_ref, g_ref, m_ref, v_ref, d_ref, mo_ref, vo_ref):
        gv = g_ref[...]
        m_new = ADAM_B1 * m_ref[...] + (1.0 - ADAM_B1) * gv
        v_new = ADAM_B2 * v_ref[...] + (1.0 - ADAM_B2) * (gv * gv)
        m_hat = m_new / (1.0 - ADAM_B1 ** ADAM_STEP)
        v_hat = v_new / (1.0 - ADAM_B2 ** ADAM_STEP)
        d_ref[...] = -ADAM_LR * (m_hat / (jnp.sqrt(v_hat) + ADAM_EPS) + ADAM_WD * w_ref[...])
        mo_ref[...] = m_new
        vo_ref[...] = v_new

    blk = pl.BlockSpec((tr, c), lambda i: (i, 0))
    return pl.pallas_call(
        body, name=name, grid=(r // tr,), in_specs=[blk] * 4, out_specs=[blk] * 3, out_shape=[S((r, c), F32)] * 3,
        compiler_params=_params(1))(w, g, m, v)


def _place():
    x, y, c = lax.axis_index("x"), lax.axis_index("y"), lax.axis_index("c")
    other_chips = [(1 - x, y), (x, 1 - y), (1 - x, 1 - y)]
    return x, y, c, other_chips


def _half(c, rows):
    return pl.ds(pl.multiple_of(c * (rows // 2), 16), rows // 2)


def gather_weight_shards(shards):
    n = len(shards)

    def body(*refs):
        src = refs[:n]
        dst = refs[n:2 * n]
        send_sems, recv_sems = refs[2 * n:]
        x, y, c, chips = _place()
        me = 2 * x + y
        sibling = (x, y, 1 - c)

        def copy(i, slot, half_of, sem, to, from_input=False):
            rows = _half(half_of, src[i].shape[0])
            return pltpu.make_async_remote_copy(
                src_ref=src[i].at[rows] if from_input else dst[i].at[slot, rows], dst_ref=dst[i].at[slot, rows],
                send_sem=send_sems.at[i, sem], recv_sem=recv_sems.at[i, sem], device_id=to, device_id_type=MESH)

        sent = []
        for i in range(n):
            for j, chip in enumerate(chips):
                sent.append(copy(i, me, c, j, (*chip, c), from_input=True))
                sent[-1].start()
        for i in range(n):
            for j, (px, py) in enumerate(chips):
                copy(i, 2 * px + py, c, j, sibling).wait_recv()
                sent.append(copy(i, 2 * px + py, c, 3 + j, sibling))
                sent[-1].start()
        for i in range(n):
            for j, (px, py) in enumerate(chips):
                copy(i, 2 * px + py, 1 - c, 3 + j, sibling).wait_recv()
        for cp in sent:
            cp.wait_send()

    outs = pl.pallas_call(
        body, name="gather_weight_shards", in_specs=[ANY] * n, out_specs=[ANY] * n,
        out_shape=[S((N_CHIPS,) + s.shape, s.dtype) for s in shards],
        scratch_shapes=[pltpu.SemaphoreType.DMA((n, 6)), pltpu.SemaphoreType.DMA((n, 6))],
    )(*shards)
    me = 2 * lax.axis_index("x") + lax.axis_index("y")
    return [lax.dynamic_update_slice(o, s[None], (me, 0, 0)) for o, s in zip(outs, shards)]


def sibling_swap_halves(grads):
    n = len(grads)

    def body(*refs):
        src = refs[:n]
        dst = refs[n:2 * n]
        send_sems, recv_sems = refs[2 * n:]
        x, y, c, _ = _place()
        copies = [pltpu.make_async_remote_copy(
            src_ref=src[i].at[:, _half(1 - c, src[i].shape[1]), :], dst_ref=dst[i], send_sem=send_sems.at[i],
            recv_sem=recv_sems.at[i], device_id=(x, y, 1 - c), device_id_type=MESH) for i in range(n)]
        for cp in copies:
            cp.start()
        for cp in copies:
            cp.wait()

    return pl.pallas_call(
        body, name="sibling_swap_halves", in_specs=[ANY] * n, out_specs=[ANY] * n,
        out_shape=[S((g.shape[0], g.shape[1] // 2, g.shape[2]), g.dtype) for g in grads],
        scratch_shapes=[pltpu.SemaphoreType.DMA((n,)), pltpu.SemaphoreType.DMA((n,))],
    )(*grads)


def add_halves(name, g, rx):
    _, r, cdim = g.shape
    r2 = r // 2
    tr = _tile(r2, 512, 16)
    nb = r2 // tr

    def body(lo_ref, hi_ref, rx_ref, o_ref):
        mine = jnp.where(lax.axis_index("c") == 0, lo_ref[...], hi_ref[...])
        o_ref[...] = (mine.astype(F32) + rx_ref[...].astype(F32)).astype(BF16)

    half = pl.BlockSpec((None, tr, cdim), lambda k, i: (k, i, 0))
    return pl.pallas_call(
        body, name=name, grid=(N_CHIPS, nb),
        in_specs=[half, pl.BlockSpec((None, tr, cdim), lambda k, i: (k, nb + i, 0)), half],
        out_specs=half, out_shape=S((N_CHIPS, r2, cdim), BF16), compiler_params=_params(2))(g, g, rx)


def scatter_to_owner_chips(parts):
    n = len(parts)

    def body(*refs):
        src = refs[:n]
        dst = refs[n:2 * n]
        send_sems, recv_sems = refs[2 * n:]
        x, y, c, chips = _place()
        me = 2 * x + y

        def copy(i, j, from_slot, to_slot, to):
            return pltpu.make_async_remote_copy(
                src_ref=src[i].at[from_slot], dst_ref=dst[i].at[to_slot], send_sem=send_sems.at[i, j],
                recv_sem=recv_sems.at[i, j], device_id=to, device_id_type=MESH)

        sent = []
        for i in range(n):
            for j, (px, py) in enumerate(chips):
                sent.append(copy(i, j, 2 * px + py, me, (px, py, c)))
                sent[-1].start()
        for i in range(n):
            for j, (px, py) in enumerate(chips):
                copy(i, j, me, 2 * px + py, (px, py, c)).wait_recv()
        for cp in sent:
            cp.wait_send()

    outs = pl.pallas_call(
        body, name="scatter_to_owner_chips", in_specs=[ANY] * n, out_specs=[ANY] * n,
        out_shape=[S(p.shape, p.dtype) for p in parts],
        scratch_shapes=[pltpu.SemaphoreType.DMA((n, 3)), pltpu.SemaphoreType.DMA((n, 3))],
    )(*parts)
    me = 2 * lax.axis_index("x") + lax.axis_index("y")
    return [lax.dynamic_update_slice(o, lax.dynamic_slice(p, (me, 0, 0), (1,) + p.shape[1:]), (me, 0, 0))
            for o, p in zip(outs, parts)]


def sum_chips(name, parts):
    _, r2, cdim = parts.shape
    tr = _tile(r2, 512, 16)

    def body(p_ref, o_ref):
        acc = p_ref[0].astype(F32)
        for k in range(1, N_CHIPS):
            acc = acc + p_ref[k].astype(F32)
        o_ref[...] = acc

    return pl.pallas_call(
        body, name=name, grid=(r2 // tr,), in_specs=[pl.BlockSpec((N_CHIPS, tr, cdim), lambda i: (0, i, 0))],
        out_specs=pl.BlockSpec((tr, cdim), lambda i: (i, 0)), out_shape=S((r2, cdim), F32), compiler_params=_params(1))(parts)


def sibling_join_halves(halves):
    n = len(halves)

    def body(*refs):
        src = refs[:n]
        dst = refs[n:2 * n]
        send_sems, recv_sems = refs[2 * n:]
        x, y, c, _ = _place()
        copies = [pltpu.make_async_remote_copy(
            src_ref=src[i], dst_ref=dst[i].at[_half(c, dst[i].shape[0])], send_sem=send_sems.at[i],
            recv_sem=recv_sems.at[i], device_id=(x, y, 1 - c), device_id_type=MESH) for i in range(n)]
        for cp in copies:
            cp.start()
        for i in range(n):
            pltpu.make_async_remote_copy(
                src_ref=src[i], dst_ref=dst[i].at[_half(1 - c, dst[i].shape[0])], send_sem=send_sems.at[i],
                recv_sem=recv_sems.at[i], device_id=(x, y, 1 - c), device_id_type=MESH).wait_recv()
        for cp in copies:
            cp.wait_send()

    outs = pl.pallas_call(
        body, name="sibling_join_halves", in_specs=[ANY] * n, out_specs=[ANY] * n,
        out_shape=[S((2 * h.shape[0], h.shape[1]), h.dtype) for h in halves],
        scratch_shapes=[pltpu.SemaphoreType.DMA((n,)), pltpu.SemaphoreType.DMA((n,))],
    )(*halves)
    c = lax.axis_index("c")
    return [lax.dynamic_update_slice(o, h, (c * h.shape[0], 0)) for o, h in zip(outs, halves)]


def all_reduce_small(name, packed):
    rows, width = packed.shape

    def body(x_ref, o_ref, gathered, send_sems, recv_sems):
        x, y, c, _ = _place()
        me = 4 * x + 2 * y + c
        gathered[me] = x_ref[...]
        flips = [(fx, fy, fc) for fx in (0, 1) for fy in (0, 1) for fc in (0, 1)][1:]

        def copy(r, slot, to):
            return pltpu.make_async_remote_copy(
                src_ref=x_ref, dst_ref=gathered.at[slot], send_sem=send_sems.at[r], recv_sem=recv_sems.at[r],
                device_id=to, device_id_type=MESH)

        def peer(f):
            return (x ^ f[0], y ^ f[1], c ^ f[2])

        sent = [copy(r, me, peer(f)) for r, f in enumerate(flips)]
        for cp in sent:
            cp.start()
        for r, f in enumerate(flips):
            px, py, pc = peer(f)
            copy(r, 4 * px + 2 * py + pc, peer(f)).wait_recv()
        for cp in sent:
            cp.wait_send()
        acc = gathered[0]
        for k in range(1, N_DEV):
            acc = acc + gathered[k]
        o_ref[...] = acc

    vmem = pl.BlockSpec(memory_space=pltpu.VMEM)
    return pl.pallas_call(
        body, name=name, in_specs=[vmem], out_specs=vmem, out_shape=S((rows, width), F32),
        scratch_shapes=[pltpu.VMEM((N_DEV, rows, width), F32), pltpu.SemaphoreType.DMA((N_DEV - 1,)),
                        pltpu.SemaphoreType.DMA((N_DEV - 1,))],
    )(packed)


def _rope_tables(positions):
    inv_freq = 1.0 / (ROPE_THETA ** (jnp.arange(0, ROPE, 2, dtype=F32) / ROPE))
    ang = positions.astype(F32)[:, None] * inv_freq
    return jnp.cos(ang), jnp.sin(ang)


def _unstack_cols(w):
    k4, k, n4 = w.shape
    return jnp.transpose(w, (1, 0, 2)).reshape(k, k4 * n4)


def _stack_cols(w):
    k, n = w.shape
    return jnp.transpose(w.reshape(k, N_CHIPS, n // N_CHIPS), (1, 0, 2))


def kernel(x, positions, mla_norm, mla_w_in, mla_g_cq, mla_g_ckv, mla_w_uq, mla_w_ukv, mla_w_o, conv_norm, conv_w_in, conv_w, conv_w_out, ffn_norm, ffn_w_gate, ffn_w_up, ffn_w_down, final_norm, loss_target, m_mla_norm, m_mla_w_in, m_mla_g_cq, m_mla_g_ckv, m_mla_w_uq, m_mla_w_ukv, m_mla_w_o, m_conv_norm, m_conv_w_in, m_conv_w, m_conv_w_out, m_ffn_norm, m_ffn_w_gate, m_ffn_w_up, m_ffn_w_down, m_final_norm, v_mla_norm, v_mla_w_in, v_mla_g_cq, v_mla_g_ckv, v_mla_w_uq, v_mla_w_ukv, v_mla_w_o, v_conv_norm, v_conv_w_in, v_conv_w, v_conv_w_out, v_ffn_norm, v_ffn_w_gate, v_ffn_w_up, v_ffn_w_down, v_final_norm):
    weights = dict(mla_norm=mla_norm, mla_w_in=mla_w_in, mla_g_cq=mla_g_cq, mla_g_ckv=mla_g_ckv, mla_w_uq=mla_w_uq,
                   mla_w_ukv=mla_w_ukv, mla_w_o=mla_w_o, conv_norm=conv_norm, conv_w_in=conv_w_in, conv_w=conv_w,
                   conv_w_out=conv_w_out, ffn_norm=ffn_norm, ffn_w_gate=ffn_w_gate, ffn_w_up=ffn_w_up,
                   ffn_w_down=ffn_w_down, final_norm=final_norm)
    m_in = dict(mla_norm=m_mla_norm, mla_w_in=m_mla_w_in, mla_g_cq=m_mla_g_cq, mla_g_ckv=m_mla_g_ckv, mla_w_uq=m_mla_w_uq,
                mla_w_ukv=m_mla_w_ukv, mla_w_o=m_mla_w_o, conv_norm=m_conv_norm, conv_w_in=m_conv_w_in, conv_w=m_conv_w,
                conv_w_out=m_conv_w_out, ffn_norm=m_ffn_norm, ffn_w_gate=m_ffn_w_gate, ffn_w_up=m_ffn_w_up,
                ffn_w_down=m_ffn_w_down, final_norm=m_final_norm)
    v_in = dict(mla_norm=v_mla_norm, mla_w_in=v_mla_w_in, mla_g_cq=v_mla_g_cq, mla_g_ckv=v_mla_g_ckv, mla_w_uq=v_mla_w_uq,
                mla_w_ukv=v_mla_w_ukv, mla_w_o=v_mla_w_o, conv_norm=v_conv_norm, conv_w_in=v_conv_w_in, conv_w=v_conv_w,
                conv_w_out=v_conv_w_out, ffn_norm=v_ffn_norm, ffn_w_gate=v_ffn_w_gate, ffn_w_up=v_ffn_w_up,
                ffn_w_down=v_ffn_w_down, final_norm=v_final_norm)
    big = ["mla_w_in", "mla_w_uq", "mla_w_ukv", "mla_w_o", "conv_w_in", "conv_w_out", "ffn_w_gate", "ffn_w_up", "ffn_w_down"]
    order = list(weights)

    t, d = x.shape[1], x.shape[2]
    h0 = x.reshape(t, d)
    target = loss_target.reshape(t, d)
    cos, sin = _rope_tables(positions.reshape(t))

    def rows2d(a):
        return a.reshape(-1, a.shape[-1])

    gathered = dict(zip(big, gather_weight_shards([rows2d(weights[n]).astype(BF16) for n in big])))
    w_in = gathered["mla_w_in"].reshape(-1, gathered["mla_w_in"].shape[-1])
    w_uq = _unstack_cols(gathered["mla_w_uq"])
    w_ukv = _unstack_cols(gathered["mla_w_ukv"])
    w_o = gathered["mla_w_o"].reshape(-1, d)
    cw_in = _unstack_cols(gathered["conv_w_in"])
    cw_out = gathered["conv_w_out"].reshape(-1, d)
    wg_all, wu_all, wd_all = gathered["ffn_w_gate"], gathered["ffn_w_up"], gathered["ffn_w_down"]

    chip = 2 * lax.axis_index("x") + lax.axis_index("y")
    core = lax.axis_index("c")
    d4 = d // N_CHIPS
    first_core = (core == 0).astype(F32)

    def place_shard(shard):
        full = jnp.zeros((shard.shape[0], d), F32)
        return lax.dynamic_update_slice(full, shard * first_core, (0, chip * d4))

    def pack_rows(rows):
        idx = lax.broadcasted_iota(jnp.int32, (SMALL_ROWS, d), 0)
        out = jnp.zeros((SMALL_ROWS, d), F32)
        for r, row in enumerate(rows):
            out = out + jnp.where(idx == r, row, 0.0)
        return out

    cw = place_shard(conv_w.reshape(3, d4))
    pre = all_reduce_small("all_gather_conv_small", pack_rows([place_shard(conv_norm.reshape(1, d4)), cw[0:1], cw[1:2], cw[2:3]]))
    conv_norm_full = pre[0:1]
    conv_w_full = pre[1:4]

    a0 = rms_fwd("mla_norm_fwd", h0, mla_norm)
    proj = linear("mla_in_proj", a0, w_in, F32)
    cq, ckv, kr = mla_mid("mla_mid", proj, mla_g_cq, mla_g_ckv, cos, sin)
    q = linear("mla_q_up", cq, w_uq, F32)
    kv = linear("mla_kv_up", ckv, w_ukv, BF16)
    qh, kh, vh = qkv_heads("qkv_heads", q, kv, kr, cos, sin)
    attn, lse = attention_fwd("attention_fwd", qh, kh, vh)
    h1 = linear("mla_out_proj", attn, w_o, F32, resid=h0)

    def ffn_forward(tag, h, layer):
        a = rms_fwd(f"ffn{tag}_norm_fwd", h, ffn_norm[layer:layer + 1])
        g, u, z = ffn_up(f"ffn{tag}_up", a, wg_all, wu_all, layer)
        return a, g, u, z, ffn_down(f"ffn{tag}_down", z, wd_all, layer, h)

    a1, g0, u0, z0, h2 = ffn_forward(0, h1, 0)
    a2 = rms_fwd("conv_norm_fwd", h2, conv_norm_full)
    bcx = conv_in_proj("conv_in_proj", a2, cw_in)
    yc = conv_fwd("conv_fwd", bcx, conv_w_full)
    h3 = linear("conv_out_proj", yc, cw_out, F32, resid=h2)
    a3, g1, u1, z1, h4 = ffn_forward(1, h3, 1)
    dh4, d_final_norm, loss_local = loss_head("loss_head", h4, final_norm.reshape(1, d), target)

    n_layers = ffn_norm.shape[0]

    def ffn_backward(tag, dh, h, layer, a, g, u, z, prev):
        dg, du = ffn_bwd_hidden(f"ffn{tag}_bwd_hidden", dh, wd_all, layer, g, u)
        d_wd = ffn_wgrad_down(f"ffn{tag}_wgrad_down", z, dh, layer, n_layers, None if prev is None else prev[2])
        da = ffn_bwd_input(f"ffn{tag}_bwd_input", dg, du, wg_all, wu_all, layer, d)
        d_wg, d_wu = ffn_wgrad_up(f"ffn{tag}_wgrad_up", a, dg, du, layer, n_layers, None if prev is None else prev[:2])
        dh_prev, d_norm = rms_bwd(f"ffn{tag}_norm_bwd", h, ffn_norm[layer:layer + 1], da, dh)
        return dh_prev, d_norm, (d_wg, d_wu, d_wd)

    dh3, d_ffn_norm1, ffn_grads = ffn_backward(1, dh4, h3, 1, a3, g1, u1, z1, None)

    dyc = linear_nt("conv_out_bwd_input", dh3, cw_out, F32)
    d_cw_out = wgrad("conv_out_wgrad", yc, dh3)
    dbcx, d_conv_w = conv_bwd("conv_bwd", bcx, conv_w_full, dyc)
    da2 = conv_in_bwd_input("conv_in_bwd_input", dbcx, cw_in)
    d_cw_in = conv_in_wgrad("conv_in_wgrad", a2, dbcx)
    dh2, d_conv_norm = rms_bwd("conv_norm_bwd", h2, conv_norm_full, da2, dh3)

    dh1, d_ffn_norm0, ffn_grads = ffn_backward(0, dh2, h1, 0, a1, g0, u0, z0, ffn_grads)

    d_attn = linear_nt("mla_out_bwd_input", dh1, w_o, F32)
    d_w_o = wgrad("mla_out_wgrad", attn, dh1)
    delta = attention_delta("attention_delta", d_attn, attn)
    dqh, dkh, dvh = attention_bwd("attention_bwd", qh, kh, vh, d_attn, lse, delta)
    dq, dkv, dkr = qkv_heads_bwd("qkv_heads_bwd", dqh, dkh, dvh, cos, sin)
    dcq = linear_nt("mla_q_up_bwd_input", dq, w_uq, F32)
    d_w_uq = wgrad("mla_q_up_wgrad", cq, dq)
    dckv = linear_nt("mla_kv_up_bwd_input", dkv, w_ukv, F32)
    d_w_ukv = wgrad("mla_kv_up_wgrad", ckv, dkv)
    dproj, d_g_cq, d_g_ckv = mla_mid_bwd("mla_mid_bwd", proj, mla_g_cq, mla_g_ckv, dcq, dckv, dkr, cos, sin)
    da0 = linear_nt("mla_in_bwd_input", dproj, w_in, F32)
    d_w_in = wgrad("mla_in_wgrad", a0, dproj)
    grad_x, d_mla_norm = rms_bwd("mla_norm_bwd", h0, mla_norm, da0, dh1)

    stacked = dict(
        mla_w_in=d_w_in.reshape(N_CHIPS, -1, d_w_in.shape[-1]), mla_w_uq=_stack_cols(d_w_uq), mla_w_ukv=_stack_cols(d_w_ukv),
        mla_w_o=d_w_o.reshape(N_CHIPS, -1, d), conv_w_in=_stack_cols(d_cw_in), conv_w_out=d_cw_out.reshape(N_CHIPS, -1, d),
        ffn_w_gate=ffn_grads[0], ffn_w_up=ffn_grads[1], ffn_w_down=ffn_grads[2])
    local = [stacked[n] for n in big]
    from_sibling = sibling_swap_halves(local)
    pair_sums = [add_halves(f"pair_sum_{n}", g, r) for n, g, r in zip(big, local, from_sibling)]
    from_chips = scatter_to_owner_chips(pair_sums)
    my_halves = [sum_chips(f"chip_sum_{n}", p) for n, p in zip(big, from_chips)]
    grads = dict(zip(big, sibling_join_halves(my_halves)))

    def pad_row(v):
        return jnp.pad(v, ((0, 0), (0, d - v.shape[1])))

    small = all_reduce_small("all_reduce_small_grads", pack_rows([
        d_mla_norm, pad_row(d_g_cq), pad_row(d_g_ckv), d_ffn_norm0, d_ffn_norm1, d_final_norm, d_conv_norm,
        d_conv_w[0:1], d_conv_w[1:2], d_conv_w[2:3], jnp.broadcast_to(loss_local, (1, d))]))
    loss = small[10, 0]
    grads["mla_norm"] = small[0:1]
    grads["mla_g_cq"] = small[1:2, :mla_g_cq.shape[1]]
    grads["mla_g_ckv"] = small[2:3, :mla_g_ckv.shape[1]]
    grads["ffn_norm"] = small[3:5]
    grads["final_norm"] = small[5:6]
    grads["conv_norm"] = lax.dynamic_slice(small[6:7], (0, chip * d4), (1, d4))
    grads["conv_w"] = lax.dynamic_slice(small[7:10], (0, chip * d4), (3, d4))

    outs_g, outs_d, outs_m, outs_v = [], [], [], []
    for n in order:
        w = weights[n]
        delta_w, new_m, new_v = adamw(f"adamw_{n}", rows2d(w), grads[n].reshape(rows2d(w).shape), rows2d(m_in[n]), rows2d(v_in[n]))
        outs_g.append(grads[n].reshape(w.shape))
        outs_d.append(delta_w.reshape(w.shape))
        outs_m.append(new_m.reshape(w.shape))
        outs_v.append(new_v.reshape(w.shape))
    return (loss, grad_x.reshape(x.shape), *outs_g, *outs_d, *outs_m, *outs_v)
```

```python
import math

import jax
import jax.numpy as jnp
from jax import lax
from jax.experimental import pallas as pl
from jax.experimental.pallas import tpu as pltpu

F32 = jnp.float32
BF16 = jnp.bfloat16
S = jax.ShapeDtypeStruct

N_HEADS = 8
NOPE = 128
ROPE = 64
HALF = ROPE // 2
VDIM = 128
QK = NOPE + ROPE
CHUNK = 64
ROPE_THETA = 10000.0
RMS_EPS = 1e-6
ADAM_LR = 0.001
ADAM_B1 = 0.9
ADAM_B2 = 0.999
ADAM_EPS = 1e-08
ADAM_WD = 0.01
ADAM_STEP = 10

N_CHIPS = 4
N_DEV = 8
MASK_VALUE = -1e30
VMEM_LIMIT = 48 * 1024 * 1024
ATT_BLOCK = 512
SMALL_ROWS = 16

_NN = (((1,), (0,)), ((), ()))
_NT = (((1,), (1,)), ((), ()))
_TN = (((0,), (0,)), ((), ()))
MESH = pl.DeviceIdType.MESH
ANY = pl.BlockSpec(memory_space=pl.ANY)


def _params(n_axes):
    return pltpu.CompilerParams(dimension_semantics=("arbitrary",) * n_axes, vmem_limit_bytes=VMEM_LIMIT)


def _tile(n, cap, mult=8):
    for t in range(min(cap, n), 0, -1):
        if n % t == 0 and t % mult == 0:
            return t
    return n


def _sigmoid(x):
    return 1.0 / (1.0 + jnp.exp(-x))


def _mm(name, a_ops, b_ops, products, dims, grid, k_axis, outs, acc_shape, epilogue, extra_ops=()):
    na, nb, ne, no = len(a_ops), len(b_ops), len(extra_ops), len(outs)
    n_acc = 1 + max(c for _, _, c in products)
    nk = 1 if k_axis is None else grid[k_axis]

    def body(*refs):
        a_refs = refs[:na]
        b_refs = refs[na:na + nb]
        e_refs = refs[na + nb:na + nb + ne]
        o_refs = refs[na + nb + ne:na + nb + ne + no]
        acc_refs = refs[na + nb + ne + no:]

        def partial_sums():
            vals = [None] * n_acc
            for ai, bi, ci in products:
                d = lax.dot_general(a_refs[ai][...].astype(BF16), b_refs[bi][...].astype(BF16), dims,
                                    preferred_element_type=F32)
                vals[ci] = d if vals[ci] is None else vals[ci] + d
            return vals

        if nk == 1:
            epilogue(partial_sums(), e_refs, o_refs)
        else:
            k = pl.program_id(k_axis)

            @pl.when(k == 0)
            def _():
                for acc in acc_refs:
                    acc[...] = jnp.zeros_like(acc)

            for acc, v in zip(acc_refs, partial_sums()):
                acc[...] += v

            @pl.when(k == nk - 1)
            def _():
                epilogue([acc[...] for acc in acc_refs], e_refs, o_refs)

    ops = list(a_ops) + list(b_ops) + list(extra_ops)
    return pl.pallas_call(
        body, name=name, grid=grid,
        in_specs=[s for _, s in ops], out_specs=[s for _, s in outs], out_shape=[o for o, _ in outs],
        scratch_shapes=[pltpu.VMEM(acc_shape, F32) for _ in range(n_acc if nk > 1 else 0)],
        compiler_params=_params(len(grid)),
    )(*[a for a, _ in ops])


def _store(accs, e_refs, o_refs):
    o_refs[0][...] = accs[0].astype(o_refs[0].dtype)


def _store_plus_residual(accs, e_refs, o_refs):
    o_refs[0][...] = (e_refs[0][...] + accs[0]).astype(o_refs[0].dtype)


def linear(name, x, w, out_dtype, resid=None):
    t, k = x.shape
    n = w.shape[1]
    tm = _tile(t, 512)
    tn = n if n <= 2048 else _tile(n, 1024, 128)
    extra = [] if resid is None else [(resid, pl.BlockSpec((tm, tn), lambda j, i: (i, j)))]
    return _mm(name, [(x, pl.BlockSpec((tm, k), lambda j, i: (i, 0)))], [(w, pl.BlockSpec((k, tn), lambda j, i: (0, j)))],
               [(0, 0, 0)], _NN, (n // tn, t // tm), None,
               [(S((t, n), out_dtype), pl.BlockSpec((tm, tn), lambda j, i: (i, j)))], None,
               _store if resid is None else _store_plus_residual, extra)[0]


def linear_nt(name, dy, w, out_dtype):
    t, n = dy.shape
    k = w.shape[0]
    tm = _tile(t, 512)
    tc = n if n <= 2048 else _tile(n, 1024, 128)
    return _mm(name, [(dy, pl.BlockSpec((tm, tc), lambda i, c: (i, c)))], [(w, pl.BlockSpec((k, tc), lambda i, c: (0, c)))],
               [(0, 0, 0)], _NT, (t // tm, n // tc), 1,
               [(S((t, k), out_dtype), pl.BlockSpec((tm, k), lambda i, c: (i, 0)))], (tm, k), _store)[0]


def wgrad(name, x, dy):
    t, k = x.shape
    n = dy.shape[1]
    tk = _tile(t, 512)
    tn = n if n <= 1024 else _tile(n, 1024, 128)
    return _mm(name, [(x, pl.BlockSpec((tk, k), lambda j, s: (s, 0)))], [(dy, pl.BlockSpec((tk, tn), lambda j, s: (s, j)))],
               [(0, 0, 0)], _TN, (n // tn, t // tk), 1,
               [(S((k, n), BF16), pl.BlockSpec((k, tn), lambda j, s: (0, j)))], (k, tn), _store)[0]


def _resident(shape, index_map):
    return pl.BlockSpec(shape, index_map, pipeline_mode=pl.Buffered(1))


def ffn_up(name, a, wg_all, wu_all, layer):
    t, d = a.shape
    f4 = wg_all.shape[2]
    tm = _tile(t, 512)
    w_spec = _resident((N_CHIPS, d, f4), lambda i: (0, layer, 0))
    h_spec = pl.BlockSpec((N_CHIPS, tm, f4), lambda i: (0, i, 0))

    def body(a_ref, wg_ref, wu_ref, g_ref, u_ref, z_ref):
        av = a_ref[...]
        for k in range(N_CHIPS):
            g = jnp.dot(av, wg_ref[k], preferred_element_type=F32)
            u = jnp.dot(av, wu_ref[k], preferred_element_type=F32)
            g_ref[k] = g.astype(BF16)
            u_ref[k] = u.astype(BF16)
            z_ref[k] = (g * _sigmoid(g) * u).astype(BF16)

    return pl.pallas_call(
        body, name=name, grid=(t // tm,), in_specs=[pl.BlockSpec((tm, d), lambda i: (i, 0)), w_spec, w_spec],
        out_specs=[h_spec] * 3, out_shape=[S((N_CHIPS, t, f4), BF16)] * 3, compiler_params=_params(1))(a, wg_all, wu_all)


def ffn_down(name, z, wd_all, layer, resid):
    _, t, f4 = z.shape
    d = wd_all.shape[2]
    tm = _tile(t, 512)
    row = pl.BlockSpec((tm, d), lambda i: (i, 0))

    def body(z_ref, wd_ref, r_ref, o_ref):
        acc = r_ref[...]
        for k in range(N_CHIPS):
            acc = acc + jnp.dot(z_ref[k], wd_ref[k], preferred_element_type=F32)
        o_ref[...] = acc

    return pl.pallas_call(
        body, name=name, grid=(t // tm,),
        in_specs=[pl.BlockSpec((N_CHIPS, tm, f4), lambda i: (0, i, 0)), _resident((N_CHIPS, f4, d), lambda i: (0, layer, 0)), row],
        out_specs=row, out_shape=S((t, d), F32), compiler_params=_params(1))(z, wd_all, resid)


def ffn_bwd_hidden(name, dh, wd_all, layer, g, u):
    t, d = dh.shape
    f4 = g.shape[2]
    tm = _tile(t, 512)
    h_spec = pl.BlockSpec((N_CHIPS, tm, f4), lambda i: (0, i, 0))

    def body(dh_ref, wd_ref, g_ref, u_ref, dg_ref, du_ref):
        dhb = dh_ref[...].astype(BF16)
        for k in range(N_CHIPS):
            dz = lax.dot_general(dhb, wd_ref[k], _NT, preferred_element_type=F32)
            gv = g_ref[k].astype(F32)
            uv = u_ref[k].astype(F32)
            sg = _sigmoid(gv)
            dg_ref[k] = (dz * uv * (sg * (1.0 + gv * (1.0 - sg)))).astype(BF16)
            du_ref[k] = (dz * (gv * sg)).astype(BF16)

    return pl.pallas_call(
        body, name=name, grid=(t // tm,),
        in_specs=[pl.BlockSpec((tm, d), lambda i: (i, 0)), _resident((N_CHIPS, f4, d), lambda i: (0, layer, 0)), h_spec, h_spec],
        out_specs=[h_spec] * 2, out_shape=[S((N_CHIPS, t, f4), BF16)] * 2, compiler_params=_params(1))(dh, wd_all, g, u)


def ffn_bwd_input(name, dg, du, wg_all, wu_all, layer, d):
    _, t, f4 = dg.shape
    tm = _tile(t, 512)
    h_spec = pl.BlockSpec((N_CHIPS, tm, f4), lambda i: (0, i, 0))
    w_spec = _resident((N_CHIPS, d, f4), lambda i: (0, layer, 0))

    def body(dg_ref, du_ref, wg_ref, wu_ref, da_ref):
        acc = jnp.zeros((tm, d), F32)
        for k in range(N_CHIPS):
            acc = acc + lax.dot_general(dg_ref[k], wg_ref[k], _NT, preferred_element_type=F32)
            acc = acc + lax.dot_general(du_ref[k], wu_ref[k], _NT, preferred_element_type=F32)
        da_ref[...] = acc

    return pl.pallas_call(
        body, name=name, grid=(t // tm,), in_specs=[h_spec, h_spec, w_spec, w_spec],
        out_specs=pl.BlockSpec((tm, d), lambda i: (i, 0)), out_shape=S((t, d), F32), compiler_params=_params(1))(dg, du, wg_all, wu_all)


def ffn_wgrad_up(name, a, dg, du):
    t, d = a.shape
    f4 = dg.shape[2]
    tk = _tile(t, 512)
    nt = t // tk
    h_spec = pl.BlockSpec((None, tk, f4), lambda k, s: (k, s, 0))
    o_spec = pl.BlockSpec((None, d, f4), lambda k, s: (k, 0, 0))

    def body(a_ref, dg_ref, du_ref, og_ref, ou_ref, accg, accu):
        s = pl.program_id(1)

        @pl.when(s == 0)
        def _():
            accg[...] = jnp.zeros_like(accg)
            accu[...] = jnp.zeros_like(accu)

        av = a_ref[...]
        accg[...] += lax.dot_general(av, dg_ref[...], _TN, preferred_element_type=F32)
        accu[...] += lax.dot_general(av, du_ref[...], _TN, preferred_element_type=F32)

        @pl.when(s == nt - 1)
        def _():
            og_ref[...] = accg[...].astype(BF16)
            ou_ref[...] = accu[...].astype(BF16)

    out = S((N_CHIPS, d, f4), BF16)
    return pl.pallas_call(
        body, name=name, grid=(N_CHIPS, nt), in_specs=[pl.BlockSpec((tk, d), lambda k, s: (s, 0)), h_spec, h_spec],
        out_specs=[o_spec, o_spec], out_shape=[out, out],
        scratch_shapes=[pltpu.VMEM((d, f4), F32), pltpu.VMEM((d, f4), F32)], compiler_params=_params(2))(a, dg, du)


def ffn_wgrad_down(name, z, dh):
    _, t, f4 = z.shape
    d = dh.shape[1]
    tk = _tile(t, 512)
    nt = t // tk

    def body(z_ref, dh_ref, o_ref, acc):
        s = pl.program_id(0)

        @pl.when(s == 0)
        def _():
            acc[...] = jnp.zeros_like(acc)

        dhb = dh_ref[...].astype(BF16)
        for k in range(N_CHIPS):
            acc[k] += lax.dot_general(z_ref[k], dhb, _TN, preferred_element_type=F32)

        @pl.when(s == nt - 1)
        def _():
            o_ref[...] = acc[...].astype(BF16)

    return pl.pallas_call(
        body, name=name, grid=(nt,),
        in_specs=[pl.BlockSpec((N_CHIPS, tk, f4), lambda s: (0, s, 0)), pl.BlockSpec((tk, d), lambda s: (s, 0))],
        out_specs=pl.BlockSpec((N_CHIPS, f4, d), lambda s: (0, 0, 0)), out_shape=S((N_CHIPS, f4, d), BF16),
        scratch_shapes=[pltpu.VMEM((N_CHIPS, f4, d), F32)], compiler_params=_params(1))(z, dh)


def conv_in_proj(name, a, w):
    t, d = a.shape
    tm = _tile(t, 512)
    return _mm(name, [(a, pl.BlockSpec((tm, d), lambda j, i: (i, 0)))], [(w, pl.BlockSpec((d, d), lambda j, i: (0, j)))],
               [(0, 0, 0)], _NN, (3, t // tm), None,
               [(S((3, t, d), F32), pl.BlockSpec((None, tm, d), lambda j, i: (j, i, 0)))], None, _store)[0]


def conv_in_bwd_input(name, dbcx, w):
    _, t, d = dbcx.shape
    tm = _tile(t, 512)

    def body(g_ref, w_ref, o_ref):
        acc = jnp.zeros((tm, d), F32)
        for j in range(3):
            acc = acc + lax.dot_general(g_ref[j], w_ref[:, j * d:(j + 1) * d], _NT, preferred_element_type=F32)
        o_ref[...] = acc

    return pl.pallas_call(
        body, name=name, grid=(t // tm,),
        in_specs=[pl.BlockSpec((3, tm, d), lambda i: (0, i, 0)), _resident((d, 3 * d), lambda i: (0, 0))],
        out_specs=pl.BlockSpec((tm, d), lambda i: (i, 0)), out_shape=S((t, d), F32), compiler_params=_params(1))(dbcx, w)


def conv_in_wgrad(name, a, dbcx):
    t, d = a.shape
    tk = _tile(t, 512)
    return _mm(name, [(a, pl.BlockSpec((tk, d), lambda j, s: (s, 0)))], [(dbcx, pl.BlockSpec((None, tk, d), lambda j, s: (j, s, 0)))],
               [(0, 0, 0)], _TN, (3, t // tk), 1,
               [(S((d, 3 * d), BF16), pl.BlockSpec((d, d), lambda j, s: (0, j)))], (d, d), _store)[0]


def _rstd(x):
    return lax.rsqrt(jnp.mean(x * x, axis=-1, keepdims=True) + RMS_EPS)


def _rms_bwd(x, g, dy):
    r = _rstd(x)
    xhat = x * r
    dgain = jnp.sum(dy * xhat, axis=0, keepdims=True)
    dxh = dy * g
    dx = r * (dxh - xhat * jnp.mean(dxh * xhat, axis=-1, keepdims=True))
    return dx, dgain


def rms_fwd(name, h, g):
    t, d = h.shape
    tr = _tile(t, 512)

    def body(h_ref, g_ref, a_ref):
        x = h_ref[...]
        a_ref[...] = (x * _rstd(x) * g_ref[...]).astype(BF16)

    return pl.pallas_call(
        body, name=name, grid=(t // tr,),
        in_specs=[pl.BlockSpec((tr, d), lambda i: (i, 0)), pl.BlockSpec((1, d), lambda i: (0, 0))],
        out_specs=pl.BlockSpec((tr, d), lambda i: (i, 0)), out_shape=S((t, d), BF16), compiler_params=_params(1))(h, g)


def rms_bwd(name, h, g, da, dh_in):
    t, d = h.shape
    tr = _tile(t, 512)

    def body(h_ref, g_ref, da_ref, dhi_ref, dho_ref, dg_ref):
        dx, dgain = _rms_bwd(h_ref[...], g_ref[...], da_ref[...])
        dho_ref[...] = dhi_ref[...] + dx

        @pl.when(pl.program_id(0) == 0)
        def _():
            dg_ref[...] = jnp.zeros_like(dg_ref)

        dg_ref[...] += dgain

    row = pl.BlockSpec((tr, d), lambda i: (i, 0))
    vec = pl.BlockSpec((1, d), lambda i: (0, 0))
    return pl.pallas_call(
        body, name=name, grid=(t // tr,), in_specs=[row, vec, row, row], out_specs=[row, vec],
        out_shape=[S((t, d), F32), S((1, d), F32)], compiler_params=_params(1))(h, g, da, dh_in)


def loss_head(name, h, g, target):
    t, d = h.shape
    tr = _tile(t, 512)

    def body(h_ref, g_ref, t_ref, dh_ref, dg_ref, loss_ref):
        x = h_ref[...]
        g = g_ref[...]
        r = _rstd(x)
        xhat = x * r
        err = xhat * g - t_ref[...]
        dy = err * (1.0 / d)
        dxh = dy * g
        dh_ref[...] = r * (dxh - xhat * jnp.mean(dxh * xhat, axis=-1, keepdims=True))

        @pl.when(pl.program_id(0) == 0)
        def _():
            dg_ref[...] = jnp.zeros_like(dg_ref)
            loss_ref[...] = jnp.zeros_like(loss_ref)

        dg_ref[...] += jnp.sum(dy * xhat, axis=0, keepdims=True)
        per_token = jnp.mean(err * err, axis=-1, keepdims=True)
        loss_ref[...] += 0.5 * jnp.sum(per_token, axis=0, keepdims=True)

    row = pl.BlockSpec((tr, d), lambda i: (i, 0))
    vec = pl.BlockSpec((1, d), lambda i: (0, 0))
    one = pl.BlockSpec((1, 1), lambda i: (0, 0))
    return pl.pallas_call(
        body, name=name, grid=(t // tr,), in_specs=[row, vec, row], out_specs=[row, vec, one],
        out_shape=[S((t, d), F32), S((1, d), F32), S((1, 1), F32)], compiler_params=_params(1))(h, g, target)


def mla_mid(name, proj, g_cq, g_ckv, cos, sin):
    t, n = proj.shape
    ql, kl = g_cq.shape[1], g_ckv.shape[1]
    tr = _tile(t, 512)

    def body(p_ref, gq_ref, gk_ref, c_ref, s_ref, cq_ref, ckv_ref, kr_ref):
        xq = p_ref[:, 0:ql]
        cq_ref[...] = (xq * _rstd(xq) * gq_ref[...]).astype(BF16)
        xk = p_ref[:, ql:ql + kl]
        ckv_ref[...] = (xk * _rstd(xk) * gk_ref[...]).astype(BF16)
        k1 = p_ref[:, ql + kl:ql + kl + HALF]
        k2 = p_ref[:, ql + kl + HALF:ql + kl + ROPE]
        c = c_ref[...]
        s = s_ref[...]
        kr_ref[:, 0:HALF] = k1 * c - k2 * s
        kr_ref[:, HALF:ROPE] = k1 * s + k2 * c

    def row(w):
        return pl.BlockSpec((tr, w), lambda i: (i, 0))

    def vec(w):
        return pl.BlockSpec((1, w), lambda i: (0, 0))

    return pl.pallas_call(
        body, name=name, grid=(t // tr,), in_specs=[row(n), vec(ql), vec(kl), row(HALF), row(HALF)],
        out_specs=[row(ql), row(kl), row(ROPE)], out_shape=[S((t, ql), BF16), S((t, kl), BF16), S((t, ROPE), F32)],
        compiler_params=_params(1))(proj, g_cq, g_ckv, cos, sin)


def mla_mid_bwd(name, proj, g_cq, g_ckv, dcq, dckv, dkr, cos, sin):
    t, n = proj.shape
    ql, kl = g_cq.shape[1], g_ckv.shape[1]
    tr = _tile(t, 512)

    def body(p_ref, gq_ref, gk_ref, dcq_ref, dckv_ref, dkr_ref, c_ref, s_ref, dp_ref, dgq_ref, dgk_ref):
        dxq, dgq = _rms_bwd(p_ref[:, 0:ql], gq_ref[...], dcq_ref[...])
        dp_ref[:, 0:ql] = dxq.astype(BF16)
        dxk, dgk = _rms_bwd(p_ref[:, ql:ql + kl], gk_ref[...], dckv_ref[...])
        dp_ref[:, ql:ql + kl] = dxk.astype(BF16)
        d1 = dkr_ref[:, 0:HALF]
        d2 = dkr_ref[:, HALF:ROPE]
        c = c_ref[...]
        s = s_ref[...]
        dp_ref[:, ql + kl:ql + kl + HALF] = (d1 * c + d2 * s).astype(BF16)
        dp_ref[:, ql + kl + HALF:ql + kl + ROPE] = (d2 * c - d1 * s).astype(BF16)

        @pl.when(pl.program_id(0) == 0)
        def _():
            dgq_ref[...] = jnp.zeros_like(dgq_ref)
            dgk_ref[...] = jnp.zeros_like(dgk_ref)

        dgq_ref[...] += dgq
        dgk_ref[...] += dgk

    def row(w):
        return pl.BlockSpec((tr, w), lambda i: (i, 0))

    def vec(w):
        return pl.BlockSpec((1, w), lambda i: (0, 0))

    return pl.pallas_call(
        body, name=name, grid=(t // tr,),
        in_specs=[row(n), vec(ql), vec(kl), row(ql), row(kl), row(ROPE), row(HALF), row(HALF)],
        out_specs=[row(n), vec(ql), vec(kl)], out_shape=[S((t, n), BF16), S((1, ql), F32), S((1, kl), F32)],
        compiler_params=_params(1))(proj, g_cq, g_ckv, dcq, dckv, dkr, cos, sin)


def qkv_heads(name, q, kv, kr, cos, sin):
    t = q.shape[0]
    tr = _tile(t, 256)

    def body(q_ref, kv_ref, kr_ref, c_ref, s_ref, qo_ref, ko_ref, vo_ref):
        c = c_ref[...]
        s = s_ref[...]
        krb = kr_ref[...].astype(BF16)
        for h in range(N_HEADS):
            q0 = h * QK
            qo_ref[h, :, 0:NOPE] = q_ref[:, q0:q0 + NOPE].astype(BF16)
            q1 = q_ref[:, q0 + NOPE:q0 + NOPE + HALF]
            q2 = q_ref[:, q0 + NOPE + HALF:q0 + QK]
            qo_ref[h, :, NOPE:NOPE + HALF] = (q1 * c - q2 * s).astype(BF16)
            qo_ref[h, :, NOPE + HALF:QK] = (q1 * s + q2 * c).astype(BF16)
            k0 = h * (NOPE + VDIM)
            ko_ref[h, :, 0:NOPE] = kv_ref[:, k0:k0 + NOPE]
            ko_ref[h, :, NOPE:QK] = krb
            vo_ref[h] = kv_ref[:, k0 + NOPE:k0 + NOPE + VDIM]

    def row(w):
        return pl.BlockSpec((tr, w), lambda i: (i, 0))

    def heads(w):
        return pl.BlockSpec((N_HEADS, tr, w), lambda i: (0, i, 0))

    return pl.pallas_call(
        body, name=name, grid=(t // tr,),
        in_specs=[row(N_HEADS * QK), row(N_HEADS * (NOPE + VDIM)), row(ROPE), row(HALF), row(HALF)],
        out_specs=[heads(QK), heads(QK), heads(VDIM)],
        out_shape=[S((N_HEADS, t, QK), BF16), S((N_HEADS, t, QK), BF16), S((N_HEADS, t, VDIM), BF16)],
        compiler_params=_params(1))(q, kv, kr, cos, sin)


def qkv_heads_bwd(name, dq_h, dk_h, dv_h, cos, sin):
    t = dq_h.shape[1]
    tr = _tile(t, 256)

    def body(dq_ref, dk_ref, dv_ref, c_ref, s_ref, q_ref, kv_ref, kr_ref):
        c = c_ref[...]
        s = s_ref[...]
        dkr = jnp.zeros((tr, ROPE), F32)
        for h in range(N_HEADS):
            q0 = h * QK
            q_ref[:, q0:q0 + NOPE] = dq_ref[h, :, 0:NOPE].astype(BF16)
            d1 = dq_ref[h, :, NOPE:NOPE + HALF]
            d2 = dq_ref[h, :, NOPE + HALF:QK]
            q_ref[:, q0 + NOPE:q0 + NOPE + HALF] = (d1 * c + d2 * s).astype(BF16)
            q_ref[:, q0 + NOPE + HALF:q0 + QK] = (d2 * c - d1 * s).astype(BF16)
            k0 = h * (NOPE + VDIM)
            kv_ref[:, k0:k0 + NOPE] = dk_ref[h, :, 0:NOPE].astype(BF16)
            kv_ref[:, k0 + NOPE:k0 + NOPE + VDIM] = dv_ref[h].astype(BF16)
            dkr = dkr + dk_ref[h, :, NOPE:QK]
        kr_ref[...] = dkr

    def row(w):
        return pl.BlockSpec((tr, w), lambda i: (i, 0))

    def heads(w):
        return pl.BlockSpec((N_HEADS, tr, w), lambda i: (0, i, 0))

    return pl.pallas_call(
        body, name=name, grid=(t // tr,),
        in_specs=[heads(QK), heads(QK), heads(VDIM), row(HALF), row(HALF)],
        out_specs=[row(N_HEADS * QK), row(N_HEADS * (NOPE + VDIM)), row(ROPE)],
        out_shape=[S((t, N_HEADS * QK), BF16), S((t, N_HEADS * (NOPE + VDIM)), BF16), S((t, ROPE), F32)],
        compiler_params=_params(1))(dq_h, dk_h, dv_h, cos, sin)


def _chunk_mask_t(q_start, k_start, bq, bk):
    kc = (k_start + lax.broadcasted_iota(jnp.int32, (bk, bq), 0)) // CHUNK
    qc = (q_start + lax.broadcasted_iota(jnp.int32, (bk, bq), 1)) // CHUNK
    return kc <= qc


def attention_fwd(name, q, k, v, shards=()):
    nh, t, _ = q.shape
    blk = ATT_BLOCK
    nq = t // blk
    n = len(shards)
    scale = 1.0 / math.sqrt(QK)

    def body(q_ref, k_ref, v_ref, *refs):
        src = refs[:n]
        o_ref, lse_ref = refs[n:n + 2]
        dst = refs[n + 2:2 * n + 2]
        m_ref, l_ref, acc_ref = refs[2 * n + 2:2 * n + 5]
        i = pl.program_id(1)
        if n:
            send_sems, recv_sems = refs[2 * n + 5:]
            _ride_along(gather_ici_copies(src, dst, send_sems, recv_sems), (pl.program_id(0), i), (nh, nq))
        qv = q_ref[...]
        m_ref[...] = jnp.full_like(m_ref, MASK_VALUE)
        l_ref[...] = jnp.zeros_like(l_ref)
        acc_ref[...] = jnp.zeros_like(acc_ref)

        def block(j, masked):
            start = pl.multiple_of(j * blk, blk)
            kb = k_ref[pl.ds(start, blk), :]
            vb = v_ref[pl.ds(start, blk), :]
            s = lax.dot_general(kb, qv, _NT, preferred_element_type=F32) * scale
            if masked:
                s = jnp.where(_chunk_mask_t(i * blk, j * blk, blk, blk), s, MASK_VALUE)
            m_old = m_ref[...]
            m_new = jnp.maximum(m_old, jnp.max(s, axis=0, keepdims=True))
            p = jnp.exp(s - m_new)
            alpha = jnp.exp(m_old - m_new)
            l_ref[...] = alpha * l_ref[...] + jnp.sum(p, axis=0, keepdims=True)
            acc_ref[...] = alpha * acc_ref[...] + lax.dot_general(vb, p.astype(BF16), _TN, preferred_element_type=F32)
            m_ref[...] = m_new

        def step(j, carry):
            block(j, False)
            return carry

        lax.fori_loop(0, i, step, 0)
        block(i, True)
        l = l_ref[...]
        o_ref[...] = (acc_ref[...] / l).T
        lse_ref[...] = m_ref[...] + jnp.log(l)

    outs = pl.pallas_call(
        body, name=name, grid=(nh, nq),
        in_specs=[pl.BlockSpec((None, blk, QK), lambda h, i: (h, i, 0)), pl.BlockSpec((None, t, QK), lambda h, i: (h, 0, 0)),
                  pl.BlockSpec((None, t, VDIM), lambda h, i: (h, 0, 0))] + [ANY] * n,
        out_specs=[pl.BlockSpec((blk, VDIM), lambda h, i: (i, h)),
                   pl.BlockSpec((None, None, 1, blk), lambda h, i: (h, i, 0, 0))] + [ANY] * n,
        out_shape=[S((t, nh * VDIM), F32), S((nh, nq, 1, blk), F32)] + [S((N_CHIPS,) + s.shape, s.dtype) for s in shards],
        scratch_shapes=[pltpu.VMEM((1, blk), F32), pltpu.VMEM((1, blk), F32), pltpu.VMEM((VDIM, blk), F32)]
        + ([pltpu.SemaphoreType.DMA((n, 3)), pltpu.SemaphoreType.DMA((n, 3))] if n else []),
        compiler_params=_params(2))(q, k, v, *shards)
    return outs[0], outs[1], list(outs[2:])


def attention_delta(name, do, o):
    t = do.shape[0]
    blk = ATT_BLOCK

    def body(do_ref, o_ref, d_ref):
        d_ref[...] = jnp.sum((do_ref[...] * o_ref[...]).T, axis=0, keepdims=True)

    tile = pl.BlockSpec((blk, VDIM), lambda h, i: (i, h))
    return pl.pallas_call(
        body, name=name, grid=(N_HEADS, t // blk), in_specs=[tile, tile],
        out_specs=pl.BlockSpec((None, None, 1, blk), lambda h, i: (h, i, 0, 0)), out_shape=S((N_HEADS, t // blk, 1, blk), F32),
        compiler_params=_params(2))(do, o)


def attention_bwd(name, q, k, v, do, lse, delta, parts=()):
    nh, t, _ = q.shape
    blk = ATT_BLOCK
    nq = t // blk
    n = len(parts)
    scale = 1.0 / math.sqrt(QK)

    def body(q_ref, k_ref, v_ref, do_ref, lse_ref, dl_ref, *refs):
        src = refs[:n]
        dq_ref, dk_ref, dv_ref = refs[n:n + 3]
        dst = refs[n + 3:2 * n + 3]
        j = pl.program_id(1)
        if n:
            send_sems, recv_sems = refs[2 * n + 3:]
            _ride_along(scatter_ici_copies(src, dst, send_sems, recv_sems), (pl.program_id(0), j), (nh, nq))

        @pl.when(j == 0)
        def _():
            dq_ref[...] = jnp.zeros_like(dq_ref)

        kb = k_ref[...]
        vb = v_ref[...]
        dk_ref[...] = jnp.zeros_like(dk_ref)
        dv_ref[...] = jnp.zeros_like(dv_ref)

        def block(i, masked):
            rows = pl.ds(pl.multiple_of(i * blk, blk), blk)
            qb = q_ref[rows, :]
            dob = do_ref[rows, :].astype(BF16)
            s = lax.dot_general(kb, qb, _NT, preferred_element_type=F32) * scale
            if masked:
                s = jnp.where(_chunk_mask_t(i * blk, j * blk, blk, blk), s, MASK_VALUE)
            p = jnp.exp(s - lse_ref[i])
            dp = lax.dot_general(vb, dob, _NT, preferred_element_type=F32)
            ds = (p * (dp - dl_ref[i]) * scale).astype(BF16)
            dv_ref[...] += jnp.dot(p.astype(BF16), dob, preferred_element_type=F32)
            dk_ref[...] += jnp.dot(ds, qb, preferred_element_type=F32)
            dq_ref[rows, :] += lax.dot_general(ds, kb, _TN, preferred_element_type=F32)

        block(j, True)

        def step(i, carry):
            block(i, False)
            return carry

        lax.fori_loop(j + 1, nq, step, 0)

    head_all = lambda w: pl.BlockSpec((None, t, w), lambda h, j: (h, 0, 0))
    head_blk = lambda w: pl.BlockSpec((None, blk, w), lambda h, j: (h, j, 0))
    stats = pl.BlockSpec((None, nq, 1, blk), lambda h, j: (h, 0, 0, 0))
    outs = pl.pallas_call(
        body, name=name, grid=(nh, nq),
        in_specs=[head_all(QK), head_blk(QK), head_blk(VDIM), pl.BlockSpec((t, VDIM), lambda h, j: (0, h)), stats, stats] + [ANY] * n,
        out_specs=[head_all(QK), head_blk(QK), head_blk(VDIM)] + [ANY] * n,
        out_shape=[S((nh, t, QK), F32), S((nh, t, QK), F32), S((nh, t, VDIM), F32)] + [S(p.shape, p.dtype) for p in parts],
        scratch_shapes=[pltpu.SemaphoreType.DMA((n, 3)), pltpu.SemaphoreType.DMA((n, 3))] if n else [],
        compiler_params=_params(2))(q, k, v, do, lse, delta, *parts)
    return outs[0], outs[1], outs[2], list(outs[3:])


def _shift_down(u, s):
    rows = lax.broadcasted_iota(jnp.int32, u.shape, 0)
    return jnp.where(rows >= s, pltpu.roll(u, s, 0), 0.0)


def _shift_up(u, s):
    n = u.shape[0]
    rows = lax.broadcasted_iota(jnp.int32, u.shape, 0)
    return jnp.where(rows < n - s, pltpu.roll(u, n - s, 0), 0.0)


def _conv_specs(t, d, lanes):
    slab = lambda part: pl.BlockSpec((None, t, lanes), lambda j, part=part: (part, 0, j))
    return slab, pl.BlockSpec((3, lanes), lambda j: (0, j)), pl.BlockSpec((t, lanes), lambda j: (0, j))


def conv_fwd(name, bcx, w):
    _, t, d = bcx.shape
    lanes = _tile(d, 128, 128)
    slab, w_spec, col = _conv_specs(t, d, lanes)

    def body(b_ref, c_ref, x_ref, w_ref, y_ref):
        u = c_ref[...] * x_ref[...]
        uc = w_ref[0:1, :] * _shift_down(u, 2) + w_ref[1:2, :] * _shift_down(u, 1) + w_ref[2:3, :] * u
        y_ref[...] = (b_ref[...] * uc).astype(BF16)

    return pl.pallas_call(
        body, name=name, grid=(d // lanes,), in_specs=[slab(0), slab(1), slab(2), w_spec], out_specs=col,
        out_shape=S((t, d), BF16), compiler_params=_params(1))(bcx, bcx, bcx, w)


def conv_bwd(name, bcx, w, dy):
    _, t, d = bcx.shape
    lanes = _tile(d, 128, 128)
    slab, w_spec, col = _conv_specs(t, d, lanes)

    def body(b_ref, c_ref, x_ref, w_ref, dy_ref, d_ref, dw_ref):
        c = c_ref[...]
        x = x_ref[...]
        dyv = dy_ref[...]
        u = c * x
        u1 = _shift_down(u, 1)
        u2 = _shift_down(u, 2)
        w0, w1, w2 = w_ref[0:1, :], w_ref[1:2, :], w_ref[2:3, :]
        d_ref[0] = (dyv * (w0 * u2 + w1 * u1 + w2 * u)).astype(BF16)
        duc = dyv * b_ref[...]
        dw_ref[0:1, :] = jnp.sum(duc * u2, axis=0, keepdims=True)
        dw_ref[1:2, :] = jnp.sum(duc * u1, axis=0, keepdims=True)
        dw_ref[2:3, :] = jnp.sum(duc * u, axis=0, keepdims=True)
        du = w2 * duc + w1 * _shift_up(duc, 1) + w0 * _shift_up(duc, 2)
        d_ref[1] = (du * x).astype(BF16)
        d_ref[2] = (du * c).astype(BF16)

    return pl.pallas_call(
        body, name=name, grid=(d // lanes,), in_specs=[slab(0), slab(1), slab(2), w_spec, col],
        out_specs=[pl.BlockSpec((3, t, lanes), lambda j: (0, 0, j)), w_spec], out_shape=[S((3, t, d), BF16), S((3, d), F32)],
        compiler_params=_params(1))(bcx, bcx, bcx, w, dy)


def adamw(name, w, g, m, v):
    r, c = w.shape
    tr = _tile(r, 512)

    def body(w_ref, g_ref, m_ref, v_ref, d_ref, mo_ref, vo_ref):
        gv = g_ref[...]
        m_new = ADAM_B1 * m_ref[...] + (1.0 - ADAM_B1) * gv
        v_new = ADAM_B2 * v_ref[...] + (1.0 - ADAM_B2) * (gv * gv)
        m_hat = m_new / (1.0 - ADAM_B1 ** ADAM_STEP)
        v_hat = v_new / (1.0 - ADAM_B2 ** ADAM_STEP)
        d_ref[...] = -ADAM_LR * (m_hat / (jnp.sqrt(v_hat) + ADAM_EPS) + ADAM_WD * w_ref[...])
        mo_ref[...] = m_new
        vo_ref[...] = v_new

    blk = pl.BlockSpec((tr, c), lambda i: (i, 0))
    return pl.pallas_call(
        body, name=name, grid=(r // tr,), in_specs=[blk] * 4, out_specs=[blk] * 3, out_shape=[S((r, c), F32)] * 3,
        compiler_params=_params(1))(w, g, m, v)


def _place():
    x, y, c = lax.axis_index("x"), lax.axis_index("y"), lax.axis_index("c")
    other_chips = [(1 - x, y), (x, 1 - y), (1 - x, 1 - y)]
    return x, y, c, other_chips


def _half(c, rows):
    return pl.ds(pl.multiple_of(c * (rows // 2), 16), rows // 2)


def gather_weight_shards(shards):
    n = len(shards)

    def body(*refs):
        src = refs[:n]
        dst = refs[n:2 * n]
        send_sems, recv_sems = refs[2 * n:]
        x, y, c, chips = _place()
        me = 2 * x + y
        sibling = (x, y, 1 - c)

        def copy(i, slot, half_of, sem, to, from_input=False):
            rows = _half(half_of, src[i].shape[0])
            return pltpu.make_async_remote_copy(
                src_ref=src[i].at[rows] if from_input else dst[i].at[slot, rows], dst_ref=dst[i].at[slot, rows],
                send_sem=send_sems.at[i, sem], recv_sem=recv_sems.at[i, sem], device_id=to, device_id_type=MESH)

        sent = []
        for i in range(n):
            for j, chip in enumerate(chips):
                sent.append(copy(i, me, c, j, (*chip, c), from_input=True))
                sent[-1].start()
        for i in range(n):
            for j, (px, py) in enumerate(chips):
                copy(i, 2 * px + py, c, j, sibling).wait_recv()
                sent.append(copy(i, 2 * px + py, c, 3 + j, sibling))
                sent[-1].start()
        for i in range(n):
            for j, (px, py) in enumerate(chips):
                copy(i, 2 * px + py, 1 - c, 3 + j, sibling).wait_recv()
        for cp in sent:
            cp.wait_send()

    outs = pl.pallas_call(
        body, name="gather_weight_shards", in_specs=[ANY] * n, out_specs=[ANY] * n,
        out_shape=[S((N_CHIPS,) + s.shape, s.dtype) for s in shards],
        scratch_shapes=[pltpu.SemaphoreType.DMA((n, 6)), pltpu.SemaphoreType.DMA((n, 6))],
    )(*shards)
    return _fill_own_slot(outs, [s[None] for s in shards])


def gather_ici_copies(src, dst, send_sems, recv_sems):
    x, y, c, chips = _place()
    me = 2 * x + y
    pairs = []
    for i in range(len(src)):
        rows = _half(c, src[i].shape[0])
        for j, (px, py) in enumerate(chips):
            def copy(slot):
                return pltpu.make_async_remote_copy(
                    src_ref=src[i].at[rows], dst_ref=dst[i].at[slot, rows], send_sem=send_sems.at[i, j],
                    recv_sem=recv_sems.at[i, j], device_id=(px, py, c), device_id_type=MESH)
            pairs.append((copy(me), copy(2 * px + py)))
    return pairs


def scatter_ici_copies(src, dst, send_sems, recv_sems):
    x, y, c, chips = _place()
    me = 2 * x + y
    pairs = []
    for i in range(len(src)):
        for j, (px, py) in enumerate(chips):
            def copy(from_slot, to_slot):
                return pltpu.make_async_remote_copy(
                    src_ref=src[i].at[from_slot], dst_ref=dst[i].at[to_slot], send_sem=send_sems.at[i, j],
                    recv_sem=recv_sems.at[i, j], device_id=(px, py, c), device_id_type=MESH)
            pairs.append((copy(2 * px + py, me), copy(me, 2 * px + py)))
    return pairs


def _ride_along(pairs, grid_ids, grid_sizes):
    first = (grid_ids[0] == 0) & (grid_ids[1] == 0)
    last = (grid_ids[0] == grid_sizes[0] - 1) & (grid_ids[1] == grid_sizes[1] - 1)

    @pl.when(first)
    def _():
        for outgoing, _ in pairs:
            outgoing.start()

    @pl.when(last)
    def _():
        for _, incoming in pairs:
            incoming.wait_recv()
        for outgoing, _ in pairs:
            outgoing.wait_send()


def _fill_own_slot(gathered, own):
    me = 2 * lax.axis_index("x") + lax.axis_index("y")
    return [lax.dynamic_update_slice(g, o, (me,) + (0,) * (g.ndim - 1)) for g, o in zip(gathered, own)]


def forward_to_sibling(gathered):
    n = len(gathered)

    def body(*refs):
        src = refs[:n]
        dst = refs[n:2 * n]
        send_sems, recv_sems = refs[2 * n:]
        x, y, c, chips = _place()
        pairs = []
        for i in range(n):
            for j, (px, py) in enumerate(chips):
                def copy(half_of):
                    rows = _half(half_of, src[i].shape[1])
                    return pltpu.make_async_remote_copy(
                        src_ref=src[i].at[2 * px + py, rows], dst_ref=dst[i].at[2 * px + py, rows], send_sem=send_sems.at[i, j],
                        recv_sem=recv_sems.at[i, j], device_id=(x, y, 1 - c), device_id_type=MESH)
                pairs.append((copy(c), copy(1 - c)))
        for outgoing, _ in pairs:
            outgoing.start()
        for _, incoming in pairs:
            incoming.wait_recv()
        for outgoing, _ in pairs:
            outgoing.wait_send()

    return pl.pallas_call(
        body, name="forward_to_sibling", in_specs=[ANY] * n, out_specs=[ANY] * n,
        out_shape=[S(g.shape, g.dtype) for g in gathered], input_output_aliases={i: i for i in range(n)},
        scratch_shapes=[pltpu.SemaphoreType.DMA((n, 3)), pltpu.SemaphoreType.DMA((n, 3))],
    )(*gathered)


def sibling_swap_halves(name, grads):
    n = len(grads)

    def body(*refs):
        src = refs[:n]
        dst = refs[n:2 * n]
        send_sems, recv_sems = refs[2 * n:]
        x, y, c, _ = _place()
        copies = [pltpu.make_async_remote_copy(
            src_ref=src[i].at[:, _half(1 - c, src[i].shape[1]), :], dst_ref=dst[i], send_sem=send_sems.at[i],
            recv_sem=recv_sems.at[i], device_id=(x, y, 1 - c), device_id_type=MESH) for i in range(n)]
        for cp in copies:
            cp.start()
        for cp in copies:
            cp.wait()

    return pl.pallas_call(
        body, name=name, in_specs=[ANY] * n, out_specs=[ANY] * n,
        out_shape=[S((g.shape[0], g.shape[1] // 2, g.shape[2]), g.dtype) for g in grads],
        scratch_shapes=[pltpu.SemaphoreType.DMA((n,)), pltpu.SemaphoreType.DMA((n,))],
    )(*grads)


def add_halves(name, g, rx):
    _, r, cdim = g.shape
    r2 = r // 2
    tr = _tile(r2, 512, 16)
    nb = r2 // tr

    def body(lo_ref, hi_ref, rx_ref, o_ref):
        mine = jnp.where(lax.axis_index("c") == 0, lo_ref[...], hi_ref[...])
        o_ref[...] = (mine.astype(F32) + rx_ref[...].astype(F32)).astype(BF16)

    half = pl.BlockSpec((None, tr, cdim), lambda k, i: (k, i, 0))
    return pl.pallas_call(
        body, name=name, grid=(N_CHIPS, nb),
        in_specs=[half, pl.BlockSpec((None, tr, cdim), lambda k, i: (k, nb + i, 0)), half],
        out_specs=half, out_shape=S((N_CHIPS, r2, cdim), BF16), compiler_params=_params(2))(g, g, rx)


def scatter_to_owner_chips(parts):
    n = len(parts)

    def body(*refs):
        src = refs[:n]
        dst = refs[n:2 * n]
        send_sems, recv_sems = refs[2 * n:]
        pairs = scatter_ici_copies(src, dst, send_sems, recv_sems)
        for outgoing, _ in pairs:
            outgoing.start()
        for _, incoming in pairs:
            incoming.wait_recv()
        for outgoing, _ in pairs:
            outgoing.wait_send()

    return pl.pallas_call(
        body, name="scatter_to_owner_chips", in_specs=[ANY] * n, out_specs=[ANY] * n,
        out_shape=[S(p.shape, p.dtype) for p in parts],
        scratch_shapes=[pltpu.SemaphoreType.DMA((n, 3)), pltpu.SemaphoreType.DMA((n, 3))],
    )(*parts)


def _own_slots(parts):
    me = 2 * lax.axis_index("x") + lax.axis_index("y")
    return [lax.dynamic_slice(p, (me, 0, 0), (1,) + p.shape[1:]) for p in parts]


def sum_chips(name, parts):
    _, r2, cdim = parts.shape
    tr = _tile(r2, 512, 16)

    def body(p_ref, o_ref):
        acc = p_ref[0].astype(F32)
        for k in range(1, N_CHIPS):
            acc = acc + p_ref[k].astype(F32)
        o_ref[...] = acc

    return pl.pallas_call(
        body, name=name, grid=(r2 // tr,), in_specs=[pl.BlockSpec((N_CHIPS, tr, cdim), lambda i: (0, i, 0))],
        out_specs=pl.BlockSpec((tr, cdim), lambda i: (i, 0)), out_shape=S((r2, cdim), F32), compiler_params=_params(1))(parts)


def sibling_join_halves(name, halves, targets, where):
    n = len(halves)

    def rows_of(i, half_of):
        r2 = halves[i].shape[0]
        return pl.ds(pl.multiple_of(where[i][1] + half_of * r2, 8), r2)

    def body(*refs):
        src = refs[:n]
        dst = refs[n:n + len(targets)]
        send_sems, recv_sems = refs[n + len(targets):]
        x, y, c, _ = _place()

        def copy(i, half_of):
            return pltpu.make_async_remote_copy(
                src_ref=src[i], dst_ref=dst[where[i][0]].at[rows_of(i, half_of)], send_sem=send_sems.at[i],
                recv_sem=recv_sems.at[i], device_id=(x, y, 1 - c), device_id_type=MESH)

        for i in range(n):
            copy(i, c).start()
        for i in range(n):
            copy(i, 1 - c).wait_recv()
        for i in range(n):
            copy(i, c).wait_send()

    outs = list(pl.pallas_call(
        body, name=name, in_specs=[ANY] * n, out_specs=[ANY] * len(targets), out_shape=[S(tg, F32) for tg in targets],
        scratch_shapes=[pltpu.SemaphoreType.DMA((n,)), pltpu.SemaphoreType.DMA((n,))],
    )(*halves))
    c = lax.axis_index("c")
    for i, h in enumerate(halves):
        tgt, first = where[i]
        outs[tgt] = lax.dynamic_update_slice(outs[tgt], h, (first + c * h.shape[0], 0))
    return outs


def all_reduce_small(name, packed):
    rows, width = packed.shape

    def body(x_ref, o_ref, gathered, send_sems, recv_sems):
        x, y, c, _ = _place()
        me = 4 * x + 2 * y + c
        gathered[me] = x_ref[...]
        flips = [(fx, fy, fc) for fx in (0, 1) for fy in (0, 1) for fc in (0, 1)][1:]

        def copy(r, slot, to):
            return pltpu.make_async_remote_copy(
                src_ref=x_ref, dst_ref=gathered.at[slot], send_sem=send_sems.at[r], recv_sem=recv_sems.at[r],
                device_id=to, device_id_type=MESH)

        def peer(f):
            return (x ^ f[0], y ^ f[1], c ^ f[2])

        sent = [copy(r, me, peer(f)) for r, f in enumerate(flips)]
        for cp in sent:
            cp.start()
        for r, f in enumerate(flips):
            px, py, pc = peer(f)
            copy(r, 4 * px + 2 * py + pc, peer(f)).wait_recv()
        for cp in sent:
            cp.wait_send()
        acc = gathered[0]
        for k in range(1, N_DEV):
            acc = acc + gathered[k]
        o_ref[...] = acc

    vmem = pl.BlockSpec(memory_space=pltpu.VMEM)
    return pl.pallas_call(
        body, name=name, in_specs=[vmem], out_specs=vmem, out_shape=S((rows, width), F32),
        scratch_shapes=[pltpu.VMEM((N_DEV, rows, width), F32), pltpu.SemaphoreType.DMA((N_DEV - 1,)),
                        pltpu.SemaphoreType.DMA((N_DEV - 1,))],
    )(packed)


def _rope_tables(positions):
    inv_freq = 1.0 / (ROPE_THETA ** (jnp.arange(0, ROPE, 2, dtype=F32) / ROPE))
    ang = positions.astype(F32)[:, None] * inv_freq
    return jnp.cos(ang), jnp.sin(ang)


def _unstack_cols(w):
    k4, k, n4 = w.shape
    return jnp.transpose(w, (1, 0, 2)).reshape(k, k4 * n4)


def _stack_cols(w):
    k, n = w.shape
    return jnp.transpose(w.reshape(k, N_CHIPS, n // N_CHIPS), (1, 0, 2))


def kernel(x, positions, mla_norm, mla_w_in, mla_g_cq, mla_g_ckv, mla_w_uq, mla_w_ukv, mla_w_o, conv_norm, conv_w_in, conv_w, conv_w_out, ffn_norm, ffn_w_gate, ffn_w_up, ffn_w_down, final_norm, loss_target, m_mla_norm, m_mla_w_in, m_mla_g_cq, m_mla_g_ckv, m_mla_w_uq, m_mla_w_ukv, m_mla_w_o, m_conv_norm, m_conv_w_in, m_conv_w, m_conv_w_out, m_ffn_norm, m_ffn_w_gate, m_ffn_w_up, m_ffn_w_down, m_final_norm, v_mla_norm, v_mla_w_in, v_mla_g_cq, v_mla_g_ckv, v_mla_w_uq, v_mla_w_ukv, v_mla_w_o, v_conv_norm, v_conv_w_in, v_conv_w, v_conv_w_out, v_ffn_norm, v_ffn_w_gate, v_ffn_w_up, v_ffn_w_down, v_final_norm):
    weights = dict(mla_norm=mla_norm, mla_w_in=mla_w_in, mla_g_cq=mla_g_cq, mla_g_ckv=mla_g_ckv, mla_w_uq=mla_w_uq,
                   mla_w_ukv=mla_w_ukv, mla_w_o=mla_w_o, conv_norm=conv_norm, conv_w_in=conv_w_in, conv_w=conv_w,
                   conv_w_out=conv_w_out, ffn_norm=ffn_norm, ffn_w_gate=ffn_w_gate, ffn_w_up=ffn_w_up,
                   ffn_w_down=ffn_w_down, final_norm=final_norm)
    m_in = dict(mla_norm=m_mla_norm, mla_w_in=m_mla_w_in, mla_g_cq=m_mla_g_cq, mla_g_ckv=m_mla_g_ckv, mla_w_uq=m_mla_w_uq,
                mla_w_ukv=m_mla_w_ukv, mla_w_o=m_mla_w_o, conv_norm=m_conv_norm, conv_w_in=m_conv_w_in, conv_w=m_conv_w,
                conv_w_out=m_conv_w_out, ffn_norm=m_ffn_norm, ffn_w_gate=m_ffn_w_gate, ffn_w_up=m_ffn_w_up,
                ffn_w_down=m_ffn_w_down, final_norm=m_final_norm)
    v_in = dict(mla_norm=v_mla_norm, mla_w_in=v_mla_w_in, mla_g_cq=v_mla_g_cq, mla_g_ckv=v_mla_g_ckv, mla_w_uq=v_mla_w_uq,
                mla_w_ukv=v_mla_w_ukv, mla_w_o=v_mla_w_o, conv_norm=v_conv_norm, conv_w_in=v_conv_w_in, conv_w=v_conv_w,
                conv_w_out=v_conv_w_out, ffn_norm=v_ffn_norm, ffn_w_gate=v_ffn_w_gate, ffn_w_up=v_ffn_w_up,
                ffn_w_down=v_ffn_w_down, final_norm=v_final_norm)
    big = ["mla_w_in", "mla_w_uq", "mla_w_ukv", "mla_w_o", "conv_w_in", "conv_w_out", "ffn_w_gate", "ffn_w_up", "ffn_w_down"]
    order = list(weights)

    t, d = x.shape[1], x.shape[2]
    h0 = x.reshape(t, d)
    target = loss_target.reshape(t, d)
    cos, sin = _rope_tables(positions.reshape(t))

    def rows2d(a):
        return a.reshape(-1, a.shape[-1])

    first, later = big[:4], big[4:]
    shards = {n: rows2d(weights[n]).astype(BF16) for n in big}
    gathered = dict(zip(first, gather_weight_shards([shards[n] for n in first])))
    w_in = gathered["mla_w_in"].reshape(-1, gathered["mla_w_in"].shape[-1])
    w_uq = _unstack_cols(gathered["mla_w_uq"])
    w_ukv = _unstack_cols(gathered["mla_w_ukv"])
    w_o = gathered["mla_w_o"].reshape(-1, d)

    chip = 2 * lax.axis_index("x") + lax.axis_index("y")
    core = lax.axis_index("c")
    d4 = d // N_CHIPS
    first_core = (core == 0).astype(F32)

    def place_shard(shard):
        full = jnp.zeros((shard.shape[0], d), F32)
        return lax.dynamic_update_slice(full, shard * first_core, (0, chip * d4))

    def pack_rows(rows):
        idx = lax.broadcasted_iota(jnp.int32, (SMALL_ROWS, d), 0)
        out = jnp.zeros((SMALL_ROWS, d), F32)
        for r, row in enumerate(rows):
            out = out + jnp.where(idx == r, row, 0.0)
        return out

    cw = place_shard(conv_w.reshape(3, d4))
    pre = all_reduce_small("all_gather_conv_small", pack_rows([place_shard(conv_norm.reshape(1, d4)), cw[0:1], cw[1:2], cw[2:3]]))
    conv_norm_full = pre[0:1]
    conv_w_full = pre[1:4]

    a0 = rms_fwd("mla_norm_fwd", h0, mla_norm)
    proj = linear("mla_in_proj", a0, w_in, F32)
    cq, ckv, kr = mla_mid("mla_mid", proj, mla_g_cq, mla_g_ckv, cos, sin)
    q = linear("mla_q_up", cq, w_uq, F32)
    kv = linear("mla_kv_up", ckv, w_ukv, BF16)
    qh, kh, vh = qkv_heads("qkv_heads", q, kv, kr, cos, sin)
    attn, lse, arriving = attention_fwd("attention_fwd", qh, kh, vh, [shards[n] for n in later])
    gathered.update(zip(later, _fill_own_slot(forward_to_sibling(arriving), [shards[n][None] for n in later])))
    cw_in = _unstack_cols(gathered["conv_w_in"])
    cw_out = gathered["conv_w_out"].reshape(-1, d)
    wg_all, wu_all, wd_all = gathered["ffn_w_gate"], gathered["ffn_w_up"], gathered["ffn_w_down"]
    h1 = linear("mla_out_proj", attn, w_o, F32, resid=h0)

    def ffn_forward(tag, h, layer):
        a = rms_fwd(f"ffn{tag}_norm_fwd", h, ffn_norm[layer:layer + 1])
        g, u, z = ffn_up(f"ffn{tag}_up", a, wg_all, wu_all, layer)
        return a, g, u, z, ffn_down(f"ffn{tag}_down", z, wd_all, layer, h)

    a1, g0, u0, z0, h2 = ffn_forward(0, h1, 0)
    a2 = rms_fwd("conv_norm_fwd", h2, conv_norm_full)
    bcx = conv_in_proj("conv_in_proj", a2, cw_in)
    yc = conv_fwd("conv_fwd", bcx, conv_w_full)
    h3 = linear("conv_out_proj", yc, cw_out, F32, resid=h2)
    a3, g1, u1, z1, h4 = ffn_forward(1, h3, 1)
    dh4, d_final_norm, loss_local = loss_head("loss_head", h4, final_norm.reshape(1, d), target)

    def ffn_backward(tag, dh, h, layer, a, g, u, z):
        dg, du = ffn_bwd_hidden(f"ffn{tag}_bwd_hidden", dh, wd_all, layer, g, u)
        d_wd = ffn_wgrad_down(f"ffn{tag}_wgrad_down", z, dh)
        da = ffn_bwd_input(f"ffn{tag}_bwd_input", dg, du, wg_all, wu_all, layer, d)
        d_wg, d_wu = ffn_wgrad_up(f"ffn{tag}_wgrad_up", a, dg, du)
        dh_prev, d_norm = rms_bwd(f"ffn{tag}_norm_bwd", h, ffn_norm[layer:layer + 1], da, dh)
        return dh_prev, d_norm, [d_wg, d_wu, d_wd]

    def reduce_to_pair_sums(tag, local):
        from_sibling = sibling_swap_halves(f"sibling_swap_{tag}", local)
        return [add_halves(f"pair_sum_{tag}{i}", g, r) for i, (g, r) in enumerate(zip(local, from_sibling))]

    def reduce_from_chips(tag, pair_sums, arrived, targets, where):
        from_chips = _fill_own_slot(arrived, _own_slots(pair_sums))
        my_halves = [sum_chips(f"chip_sum_{tag}{i}", p) for i, p in enumerate(from_chips)]
        return sibling_join_halves(f"sibling_join_{tag}", my_halves, targets, where)

    dh3, d_ffn_norm1, ffn1_grads = ffn_backward(1, dh4, h3, 1, a3, g1, u1, z1)

    dyc = linear_nt("conv_out_bwd_input", dh3, cw_out, F32)
    d_cw_out = wgrad("conv_out_wgrad", yc, dh3)
    dbcx, d_conv_w = conv_bwd("conv_bwd", bcx, conv_w_full, dyc)
    da2 = conv_in_bwd_input("conv_in_bwd_input", dbcx, cw_in)
    d_cw_in = conv_in_wgrad("conv_in_wgrad", a2, dbcx)
    dh2, d_conv_norm = rms_bwd("conv_norm_bwd", h2, conv_norm_full, da2, dh3)

    dh1, d_ffn_norm0, ffn0_grads = ffn_backward(0, dh2, h1, 0, a1, g0, u0, z0)

    rest_pairs = reduce_to_pair_sums("rest", [_stack_cols(d_cw_in), d_cw_out.reshape(N_CHIPS, -1, d)] + ffn1_grads + ffn0_grads)

    d_attn = linear_nt("mla_out_bwd_input", dh1, w_o, F32)
    d_w_o = wgrad("mla_out_wgrad", attn, dh1)
    delta = attention_delta("attention_delta", d_attn, attn)
    dqh, dkh, dvh, rest_arrived = attention_bwd("attention_bwd", qh, kh, vh, d_attn, lse, delta, rest_pairs)
    dq, dkv, dkr = qkv_heads_bwd("qkv_heads_bwd", dqh, dkh, dvh, cos, sin)
    dcq = linear_nt("mla_q_up_bwd_input", dq, w_uq, F32)
    d_w_uq = wgrad("mla_q_up_wgrad", cq, dq)
    dckv = linear_nt("mla_kv_up_bwd_input", dkv, w_ukv, F32)
    d_w_ukv = wgrad("mla_kv_up_wgrad", ckv, dkv)
    dproj, d_g_cq, d_g_ckv = mla_mid_bwd("mla_mid_bwd", proj, mla_g_cq, mla_g_ckv, dcq, dckv, dkr, cos, sin)
    da0 = linear_nt("mla_in_bwd_input", dproj, w_in, F32)
    d_w_in = wgrad("mla_in_wgrad", a0, dproj)
    grad_x, d_mla_norm = rms_bwd("mla_norm_bwd", h0, mla_norm, da0, dh1)

    def shard_shape(n):
        return rows2d(weights[n]).shape

    rd, rf = ffn0_grads[0].shape[1], ffn0_grads[2].shape[1]
    rest_where = [(0, 0), (1, 0), (2, rd), (3, rd), (4, rf), (2, 0), (3, 0), (4, 0)]
    grads = dict(zip(later, reduce_from_chips("rest", rest_pairs, rest_arrived, [shard_shape(n) for n in later], rest_where)))

    mla_pairs = reduce_to_pair_sums("mla", [d_w_in.reshape(N_CHIPS, -1, d_w_in.shape[-1]), _stack_cols(d_w_uq),
                                            _stack_cols(d_w_ukv), d_w_o.reshape(N_CHIPS, -1, d)])
    grads.update(zip(first, reduce_from_chips("mla", mla_pairs, scatter_to_owner_chips(mla_pairs),
                                              [shard_shape(n) for n in first], [(i, 0) for i in range(4)])))

    def pad_row(v):
        return jnp.pad(v, ((0, 0), (0, d - v.shape[1])))

    small = all_reduce_small("all_reduce_small_grads", pack_rows([
        d_mla_norm, pad_row(d_g_cq), pad_row(d_g_ckv), d_ffn_norm0, d_ffn_norm1, d_final_norm, d_conv_norm,
        d_conv_w[0:1], d_conv_w[1:2], d_conv_w[2:3], jnp.broadcast_to(loss_local, (1, d))]))
    loss = small[10, 0]
    grads["mla_norm"] = small[0:1]
    grads["mla_g_cq"] = small[1:2, :mla_g_cq.shape[1]]
    grads["mla_g_ckv"] = small[2:3, :mla_g_ckv.shape[1]]
    grads["ffn_norm"] = small[3:5]
    grads["final_norm"] = small[5:6]
    grads["conv_norm"] = lax.dynamic_slice(small[6:7], (0, chip * d4), (1, d4))
    grads["conv_w"] = lax.dynamic_slice(small[7:10], (0, chip * d4), (3, d4))

    outs_g, outs_d, outs_m, outs_v = [], [], [], []
    for n in order:
        w = weights[n]
        delta_w, new_m, new_v = adamw(f"adamw_{n}", rows2d(w), grads[n].reshape(rows2d(w).shape), rows2d(m_in[n]), rows2d(v_in[n]))
        outs_g.append(grads[n].reshape(w.shape))
        outs_d.append(delta_w.reshape(w.shape))
        outs_m.append(new_m.reshape(w.shape))
        outs_v.append(new_v.reshape(w.shape))
    return (loss, grad_x.reshape(x.shape), *outs_g, *outs_d, *outs_m, *outs_v)
```

```python
import math

import jax
import jax.numpy as jnp
from jax import lax
from jax.experimental import pallas as pl
from jax.experimental.pallas import tpu as pltpu

F32 = jnp.float32
BF16 = jnp.bfloat16
S = jax.ShapeDtypeStruct

N_HEADS = 8
NOPE = 128
ROPE = 64
HALF = ROPE // 2
VDIM = 128
QK = NOPE + ROPE
CHUNK = 64
ROPE_THETA = 10000.0
RMS_EPS = 1e-6
ADAM_LR = 0.001
ADAM_B1 = 0.9
ADAM_B2 = 0.999
ADAM_EPS = 1e-08
ADAM_WD = 0.01
ADAM_STEP = 10

N_CHIPS = 4
N_DEV = 8
MASK_VALUE = -1e30
VMEM_LIMIT = 48 * 1024 * 1024
ATT_BLOCK = 512
SMALL_ROWS = 16

_NN = (((1,), (0,)), ((), ()))
_NT = (((1,), (1,)), ((), ()))
_TN = (((0,), (0,)), ((), ()))
MESH = pl.DeviceIdType.MESH
ANY = pl.BlockSpec(memory_space=pl.ANY)


def _params(n_axes):
    return pltpu.CompilerParams(dimension_semantics=("arbitrary",) * n_axes, vmem_limit_bytes=VMEM_LIMIT)


def _tile(n, cap, mult=8):
    for t in range(min(cap, n), 0, -1):
        if n % t == 0 and t % mult == 0:
            return t
    return n


def _sigmoid(x):
    return 1.0 / (1.0 + jnp.exp(-x))


def _mm(name, a_ops, b_ops, products, dims, grid, k_axis, outs, acc_shape, epilogue, extra_ops=()):
    na, nb, ne, no = len(a_ops), len(b_ops), len(extra_ops), len(outs)
    n_acc = 1 + max(c for _, _, c in products)
    nk = 1 if k_axis is None else grid[k_axis]

    def body(*refs):
        a_refs = refs[:na]
        b_refs = refs[na:na + nb]
        e_refs = refs[na + nb:na + nb + ne]
        o_refs = refs[na + nb + ne:na + nb + ne + no]
        acc_refs = refs[na + nb + ne + no:]

        def partial_sums():
            vals = [None] * n_acc
            for ai, bi, ci in products:
                d = lax.dot_general(a_refs[ai][...].astype(BF16), b_refs[bi][...].astype(BF16), dims,
                                    preferred_element_type=F32)
                vals[ci] = d if vals[ci] is None else vals[ci] + d
            return vals

        if nk == 1:
            epilogue(partial_sums(), e_refs, o_refs)
        else:
            k = pl.program_id(k_axis)

            @pl.when(k == 0)
            def _():
                for acc in acc_refs:
                    acc[...] = jnp.zeros_like(acc)

            for acc, v in zip(acc_refs, partial_sums()):
                acc[...] += v

            @pl.when(k == nk - 1)
            def _():
                epilogue([acc[...] for acc in acc_refs], e_refs, o_refs)

    ops = list(a_ops) + list(b_ops) + list(extra_ops)
    return pl.pallas_call(
        body, name=name, grid=grid,
        in_specs=[s for _, s in ops], out_specs=[s for _, s in outs], out_shape=[o for o, _ in outs],
        scratch_shapes=[pltpu.VMEM(acc_shape, F32) for _ in range(n_acc if nk > 1 else 0)],
        compiler_params=_params(len(grid)),
    )(*[a for a, _ in ops])


def _store(accs, e_refs, o_refs):
    o_refs[0][...] = accs[0].astype(o_refs[0].dtype)


def linear(name, x, w, out_dtype, resid=None, next_gain=None):
    t, k = x.shape
    n = w.shape[1]
    tm = _tile(t, 512)
    tn = n if n <= 2048 else _tile(n, 1024, 128)
    tile = pl.BlockSpec((tm, tn), lambda j, i: (i, j))
    extra = [] if resid is None else [(resid, tile)]
    outs = [(S((t, n), out_dtype), tile)]
    if next_gain is not None:
        assert tn == n
        extra.append((next_gain, pl.BlockSpec((1, n), lambda j, i: (0, 0))))
        outs.append((S((t, n), BF16), tile))

    def epilogue(accs, e_refs, o_refs):
        y = accs[0] if resid is None else e_refs[0][...] + accs[0]
        o_refs[0][...] = y.astype(out_dtype)
        if next_gain is not None:
            o_refs[1][...] = (y * _rstd(y) * e_refs[-1][...]).astype(BF16)

    res = _mm(name, [(x, pl.BlockSpec((tm, k), lambda j, i: (i, 0)))], [(w, pl.BlockSpec((k, tn), lambda j, i: (0, j)))],
              [(0, 0, 0)], _NN, (n // tn, t // tm), None, outs, None, epilogue, extra)
    return res[0] if next_gain is None else res


def linear_nt(name, dy, w, out_dtype):
    t, n = dy.shape
    k = w.shape[0]
    tm = _tile(t, 512)
    tc = n if n <= 2048 else _tile(n, 1024, 128)
    return _mm(name, [(dy, pl.BlockSpec((tm, tc), lambda i, c: (i, c)))], [(w, pl.BlockSpec((k, tc), lambda i, c: (0, c)))],
               [(0, 0, 0)], _NT, (t // tm, n // tc), 1,
               [(S((t, k), out_dtype), pl.BlockSpec((tm, k), lambda i, c: (i, 0)))], (tm, k), _store)[0]


def wgrad(name, x, dy):
    t, k = x.shape
    n = dy.shape[1]
    tk = _tile(t, 512)
    tn = n if n <= 1024 else _tile(n, 1024, 128)
    return _mm(name, [(x, pl.BlockSpec((tk, k), lambda j, s: (s, 0)))], [(dy, pl.BlockSpec((tk, tn), lambda j, s: (s, j)))],
               [(0, 0, 0)], _TN, (n // tn, t // tk), 1,
               [(S((k, n), BF16), pl.BlockSpec((k, tn), lambda j, s: (0, j)))], (k, tn), _store)[0]


def _resident(shape, index_map):
    return pl.BlockSpec(shape, index_map, pipeline_mode=pl.Buffered(1))


def ffn_up(name, a, wg_all, wu_all, layer):
    t, d = a.shape
    f4 = wg_all.shape[2]
    tm = _tile(t, 512)
    w_spec = _resident((N_CHIPS, d, f4), lambda i: (0, layer, 0))
    h_spec = pl.BlockSpec((N_CHIPS, tm, f4), lambda i: (0, i, 0))

    def body(a_ref, wg_ref, wu_ref, g_ref, u_ref, z_ref):
        av = a_ref[...]
        for k in range(N_CHIPS):
            g = jnp.dot(av, wg_ref[k], preferred_element_type=F32)
            u = jnp.dot(av, wu_ref[k], preferred_element_type=F32)
            g_ref[k] = g.astype(BF16)
            u_ref[k] = u.astype(BF16)
            z_ref[k] = (g * _sigmoid(g) * u).astype(BF16)

    return pl.pallas_call(
        body, name=name, grid=(t // tm,), in_specs=[pl.BlockSpec((tm, d), lambda i: (i, 0)), w_spec, w_spec],
        out_specs=[h_spec] * 3, out_shape=[S((N_CHIPS, t, f4), BF16)] * 3, compiler_params=_params(1))(a, wg_all, wu_all)


def ffn_down(name, z, wd_all, layer, resid, next_gain=None):
    _, t, f4 = z.shape
    d = wd_all.shape[2]
    tm = _tile(t, 512)
    row = pl.BlockSpec((tm, d), lambda i: (i, 0))
    normed = next_gain is not None

    def body(z_ref, wd_ref, r_ref, *refs):
        acc = r_ref[...]
        for k in range(N_CHIPS):
            acc = acc + jnp.dot(z_ref[k], wd_ref[k], preferred_element_type=F32)
        refs[-2 if normed else -1][...] = acc
        if normed:
            refs[-1][...] = (acc * _rstd(acc) * refs[0][...]).astype(BF16)

    res = pl.pallas_call(
        body, name=name, grid=(t // tm,),
        in_specs=[pl.BlockSpec((N_CHIPS, tm, f4), lambda i: (0, i, 0)), _resident((N_CHIPS, f4, d), lambda i: (0, layer, 0)), row]
        + ([pl.BlockSpec((1, d), lambda i: (0, 0))] if normed else []),
        out_specs=[row] * (2 if normed else 1), out_shape=[S((t, d), F32)] + ([S((t, d), BF16)] if normed else []),
        compiler_params=_params(1))(z, wd_all, resid, *([next_gain] if normed else []))
    return res if normed else res[0]


def ffn_bwd_hidden(name, dh, wd_all, layer, g, u):
    t, d = dh.shape
    f4 = g.shape[2]
    tm = _tile(t, 512)
    h_spec = pl.BlockSpec((N_CHIPS, tm, f4), lambda i: (0, i, 0))

    def body(dh_ref, wd_ref, g_ref, u_ref, dg_ref, du_ref):
        dhb = dh_ref[...].astype(BF16)
        for k in range(N_CHIPS):
            dz = lax.dot_general(dhb, wd_ref[k], _NT, preferred_element_type=F32)
            gv = g_ref[k].astype(F32)
            uv = u_ref[k].astype(F32)
            sg = 0.5 * jnp.tanh(0.5 * gv) + 0.5
            dg_ref[k] = (dz * uv * (sg * (1.0 + gv * (1.0 - sg)))).astype(BF16)
            du_ref[k] = (dz * (gv * sg)).astype(BF16)

    return pl.pallas_call(
        body, name=name, grid=(t // tm,),
        in_specs=[pl.BlockSpec((tm, d), lambda i: (i, 0)), _resident((N_CHIPS, f4, d), lambda i: (0, layer, 0)), h_spec, h_spec],
        out_specs=[h_spec] * 2, out_shape=[S((N_CHIPS, t, f4), BF16)] * 2, compiler_params=_params(1))(dh, wd_all, g, u)


def _norm_bwd_specs(tm, d):
    row = pl.BlockSpec((tm, d), lambda i: (i, 0))
    vec = pl.BlockSpec((1, d), lambda i: (0, 0))
    return [row, vec, row], [row, vec]


def _norm_bwd_tail(da, h_ref, g_ref, dhi_ref, dho_ref, dgain_ref):
    dx, dgain = _rms_bwd(h_ref[...], g_ref[...], da)
    dho_ref[...] = dhi_ref[...] + dx

    @pl.when(pl.program_id(0) == 0)
    def _():
        dgain_ref[...] = jnp.zeros_like(dgain_ref)

    dgain_ref[...] += dgain


def ffn_bwd_input(name, dg, du, wg_all, wu_all, layer, h, gain, dh_in):
    _, t, f4 = dg.shape
    d = h.shape[1]
    tm = _tile(t, 512)
    h_spec = pl.BlockSpec((N_CHIPS, tm, f4), lambda i: (0, i, 0))
    w_spec = _resident((N_CHIPS, d, f4), lambda i: (0, layer, 0))
    tail_in, tail_out = _norm_bwd_specs(tm, d)

    def body(dg_ref, du_ref, wg_ref, wu_ref, *tail):
        acc = jnp.zeros((tm, d), F32)
        for k in range(N_CHIPS):
            acc = acc + lax.dot_general(dg_ref[k], wg_ref[k], _NT, preferred_element_type=F32)
            acc = acc + lax.dot_general(du_ref[k], wu_ref[k], _NT, preferred_element_type=F32)
        _norm_bwd_tail(acc, *tail)

    return pl.pallas_call(
        body, name=name, grid=(t // tm,), in_specs=[h_spec, h_spec, w_spec, w_spec] + tail_in, out_specs=tail_out,
        out_shape=[S((t, d), F32), S((1, d), F32)], compiler_params=_params(1))(dg, du, wg_all, wu_all, h, gain, dh_in)


def ffn_wgrad_up(name, a, dg, du):
    t, d = a.shape
    f4 = dg.shape[2]
    tk = _tile(t, 512)
    nt = t // tk
    h_spec = pl.BlockSpec((None, tk, f4), lambda k, s: (k, s, 0))
    o_spec = pl.BlockSpec((None, d, f4), lambda k, s: (k, 0, 0))

    def body(a_ref, dg_ref, du_ref, og_ref, ou_ref, accg, accu):
        s = pl.program_id(1)

        @pl.when(s == 0)
        def _():
            accg[...] = jnp.zeros_like(accg)
            accu[...] = jnp.zeros_like(accu)

        av = a_ref[...]
        accg[...] += lax.dot_general(av, dg_ref[...], _TN, preferred_element_type=F32)
        accu[...] += lax.dot_general(av, du_ref[...], _TN, preferred_element_type=F32)

        @pl.when(s == nt - 1)
        def _():
            og_ref[...] = accg[...].astype(BF16)
            ou_ref[...] = accu[...].astype(BF16)

    out = S((N_CHIPS, d, f4), BF16)
    return pl.pallas_call(
        body, name=name, grid=(N_CHIPS, nt), in_specs=[pl.BlockSpec((tk, d), lambda k, s: (s, 0)), h_spec, h_spec],
        out_specs=[o_spec, o_spec], out_shape=[out, out],
        scratch_shapes=[pltpu.VMEM((d, f4), F32), pltpu.VMEM((d, f4), F32)], compiler_params=_params(2))(a, dg, du)


def ffn_wgrad_down(name, z, dh):
    _, t, f4 = z.shape
    d = dh.shape[1]
    tk = _tile(t, 512)
    nt = t // tk

    def body(z_ref, dh_ref, o_ref, acc):
        s = pl.program_id(0)

        @pl.when(s == 0)
        def _():
            acc[...] = jnp.zeros_like(acc)

        dhb = dh_ref[...].astype(BF16)
        for k in range(N_CHIPS):
            acc[k] += lax.dot_general(z_ref[k], dhb, _TN, preferred_element_type=F32)

        @pl.when(s == nt - 1)
        def _():
            o_ref[...] = acc[...].astype(BF16)

    return pl.pallas_call(
        body, name=name, grid=(nt,),
        in_specs=[pl.BlockSpec((N_CHIPS, tk, f4), lambda s: (0, s, 0)), pl.BlockSpec((tk, d), lambda s: (s, 0))],
        out_specs=pl.BlockSpec((N_CHIPS, f4, d), lambda s: (0, 0, 0)), out_shape=S((N_CHIPS, f4, d), BF16),
        scratch_shapes=[pltpu.VMEM((N_CHIPS, f4, d), F32)], compiler_params=_params(1))(z, dh)


def conv_in_proj(name, a, w):
    t, d = a.shape
    tm = _tile(t, 512)
    return _mm(name, [(a, pl.BlockSpec((tm, d), lambda j, i: (i, 0)))], [(w, pl.BlockSpec((d, d), lambda j, i: (0, j)))],
               [(0, 0, 0)], _NN, (3, t // tm), None,
               [(S((3, t, d), F32), pl.BlockSpec((None, tm, d), lambda j, i: (j, i, 0)))], None, _store)[0]


def conv_in_bwd_input(name, dbcx, w, h, gain, dh_in):
    _, t, d = dbcx.shape
    tm = _tile(t, 512)
    tail_in, tail_out = _norm_bwd_specs(tm, d)

    def body(g_ref, w_ref, *tail):
        acc = jnp.zeros((tm, d), F32)
        for j in range(3):
            acc = acc + lax.dot_general(g_ref[j], w_ref[:, j * d:(j + 1) * d], _NT, preferred_element_type=F32)
        _norm_bwd_tail(acc, *tail)

    return pl.pallas_call(
        body, name=name, grid=(t // tm,),
        in_specs=[pl.BlockSpec((3, tm, d), lambda i: (0, i, 0)), _resident((d, 3 * d), lambda i: (0, 0))] + tail_in,
        out_specs=tail_out, out_shape=[S((t, d), F32), S((1, d), F32)], compiler_params=_params(1))(dbcx, w, h, gain, dh_in)


def linear_nt_norm_bwd(name, dy, w, h, gain, dh_in):
    t, n = dy.shape
    k = w.shape[0]
    tm = _tile(t, 512)
    tail_in, tail_out = _norm_bwd_specs(tm, k)

    def body(dy_ref, w_ref, *tail):
        _norm_bwd_tail(lax.dot_general(dy_ref[...].astype(BF16), w_ref[...], _NT, preferred_element_type=F32), *tail)

    return pl.pallas_call(
        body, name=name, grid=(t // tm,),
        in_specs=[pl.BlockSpec((tm, n), lambda i: (i, 0)), _resident((k, n), lambda i: (0, 0))] + tail_in,
        out_specs=tail_out, out_shape=[S((t, k), F32), S((1, k), F32)], compiler_params=_params(1))(dy, w, h, gain, dh_in)


def conv_in_wgrad(name, a, dbcx):
    t, d = a.shape
    tk = _tile(t, 512)
    return _mm(name, [(a, pl.BlockSpec((tk, d), lambda j, s: (s, 0)))], [(dbcx, pl.BlockSpec((None, tk, d), lambda j, s: (j, s, 0)))],
               [(0, 0, 0)], _TN, (3, t // tk), 1,
               [(S((d, 3 * d), BF16), pl.BlockSpec((d, d), lambda j, s: (0, j)))], (d, d), _store)[0]


def _rstd(x):
    return lax.rsqrt(jnp.mean(x * x, axis=-1, keepdims=True) + RMS_EPS)


def _rms_bwd(x, g, dy):
    r = _rstd(x)
    xhat = x * r
    dgain = jnp.sum(dy * xhat, axis=0, keepdims=True)
    dxh = dy * g
    dx = r * (dxh - xhat * jnp.mean(dxh * xhat, axis=-1, keepdims=True))
    return dx, dgain


def rms_fwd(name, h, g):
    t, d = h.shape
    tr = _tile(t, 512)

    def body(h_ref, g_ref, a_ref):
        x = h_ref[...]
        a_ref[...] = (x * _rstd(x) * g_ref[...]).astype(BF16)

    return pl.pallas_call(
        body, name=name, grid=(t // tr,),
        in_specs=[pl.BlockSpec((tr, d), lambda i: (i, 0)), pl.BlockSpec((1, d), lambda i: (0, 0))],
        out_specs=pl.BlockSpec((tr, d), lambda i: (i, 0)), out_shape=S((t, d), BF16), compiler_params=_params(1))(h, g)


def rms_bwd(name, h, g, da, dh_in):
    t, d = h.shape
    tr = _tile(t, 512)

    def body(h_ref, g_ref, da_ref, dhi_ref, dho_ref, dg_ref):
        dx, dgain = _rms_bwd(h_ref[...], g_ref[...], da_ref[...])
        dho_ref[...] = dhi_ref[...] + dx

        @pl.when(pl.program_id(0) == 0)
        def _():
            dg_ref[...] = jnp.zeros_like(dg_ref)

        dg_ref[...] += dgain

    row = pl.BlockSpec((tr, d), lambda i: (i, 0))
    vec = pl.BlockSpec((1, d), lambda i: (0, 0))
    return pl.pallas_call(
        body, name=name, grid=(t // tr,), in_specs=[row, vec, row, row], out_specs=[row, vec],
        out_shape=[S((t, d), F32), S((1, d), F32)], compiler_params=_params(1))(h, g, da, dh_in)


def loss_head(name, h, g, target):
    t, d = h.shape
    tr = _tile(t, 512)

    def body(h_ref, g_ref, t_ref, dh_ref, dg_ref, loss_ref):
        x = h_ref[...]
        g = g_ref[...]
        r = _rstd(x)
        xhat = x * r
        err = xhat * g - t_ref[...]
        dy = err * (1.0 / d)
        dxh = dy * g
        dh_ref[...] = r * (dxh - xhat * jnp.mean(dxh * xhat, axis=-1, keepdims=True))

        @pl.when(pl.program_id(0) == 0)
        def _():
            dg_ref[...] = jnp.zeros_like(dg_ref)
            loss_ref[...] = jnp.zeros_like(loss_ref)

        dg_ref[...] += jnp.sum(dy * xhat, axis=0, keepdims=True)
        per_token = jnp.mean(err * err, axis=-1, keepdims=True)
        loss_ref[...] += 0.5 * jnp.sum(per_token, axis=0, keepdims=True)

    row = pl.BlockSpec((tr, d), lambda i: (i, 0))
    vec = pl.BlockSpec((1, d), lambda i: (0, 0))
    one = pl.BlockSpec((1, 1), lambda i: (0, 0))
    return pl.pallas_call(
        body, name=name, grid=(t // tr,), in_specs=[row, vec, row], out_specs=[row, vec, one],
        out_shape=[S((t, d), F32), S((1, d), F32), S((1, 1), F32)], compiler_params=_params(1))(h, g, target)


def mla_mid(name, proj, g_cq, g_ckv, cos, sin):
    t, n = proj.shape
    ql, kl = g_cq.shape[1], g_ckv.shape[1]
    tr = _tile(t, 512)

    def body(p_ref, gq_ref, gk_ref, c_ref, s_ref, cq_ref, ckv_ref, kr_ref):
        xq = p_ref[:, 0:ql]
        cq_ref[...] = (xq * _rstd(xq) * gq_ref[...]).astype(BF16)
        xk = p_ref[:, ql:ql + kl]
        ckv_ref[...] = (xk * _rstd(xk) * gk_ref[...]).astype(BF16)
        k1 = p_ref[:, ql + kl:ql + kl + HALF]
        k2 = p_ref[:, ql + kl + HALF:ql + kl + ROPE]
        c = c_ref[...]
        s = s_ref[...]
        kr_ref[:, 0:HALF] = k1 * c - k2 * s
        kr_ref[:, HALF:ROPE] = k1 * s + k2 * c

    def row(w):
        return pl.BlockSpec((tr, w), lambda i: (i, 0))

    def vec(w):
        return pl.BlockSpec((1, w), lambda i: (0, 0))

    return pl.pallas_call(
        body, name=name, grid=(t // tr,), in_specs=[row(n), vec(ql), vec(kl), row(HALF), row(HALF)],
        out_specs=[row(ql), row(kl), row(ROPE)], out_shape=[S((t, ql), BF16), S((t, kl), BF16), S((t, ROPE), F32)],
        compiler_params=_params(1))(proj, g_cq, g_ckv, cos, sin)


def mla_mid_bwd(name, proj, g_cq, g_ckv, dcq, dckv, dkr, cos, sin):
    t, n = proj.shape
    ql, kl = g_cq.shape[1], g_ckv.shape[1]
    tr = _tile(t, 512)

    def body(p_ref, gq_ref, gk_ref, dcq_ref, dckv_ref, dkr_ref, c_ref, s_ref, dp_ref, dgq_ref, dgk_ref):
        dxq, dgq = _rms_bwd(p_ref[:, 0:ql], gq_ref[...], dcq_ref[...])
        dp_ref[:, 0:ql] = dxq.astype(BF16)
        dxk, dgk = _rms_bwd(p_ref[:, ql:ql + kl], gk_ref[...], dckv_ref[...])
        dp_ref[:, ql:ql + kl] = dxk.astype(BF16)
        d1 = dkr_ref[:, 0:HALF]
        d2 = dkr_ref[:, HALF:ROPE]
        c = c_ref[...]
        s = s_ref[...]
        dp_ref[:, ql + kl:ql + kl + HALF] = (d1 * c + d2 * s).astype(BF16)
        dp_ref[:, ql + kl + HALF:ql + kl + ROPE] = (d2 * c - d1 * s).astype(BF16)

        @pl.when(pl.program_id(0) == 0)
        def _():
            dgq_ref[...] = jnp.zeros_like(dgq_ref)
            dgk_ref[...] = jnp.zeros_like(dgk_ref)

        dgq_ref[...] += dgq
        dgk_ref[...] += dgk

    def row(w):
        return pl.BlockSpec((tr, w), lambda i: (i, 0))

    def vec(w):
        return pl.BlockSpec((1, w), lambda i: (0, 0))

    return pl.pallas_call(
        body, name=name, grid=(t // tr,),
        in_specs=[row(n), vec(ql), vec(kl), row(ql), row(kl), row(ROPE), row(HALF), row(HALF)],
        out_specs=[row(n), vec(ql), vec(kl)], out_shape=[S((t, n), BF16), S((1, ql), F32), S((1, kl), F32)],
        compiler_params=_params(1))(proj, g_cq, g_ckv, dcq, dckv, dkr, cos, sin)


def qkv_heads(name, q, kv, kr, cos, sin):
    t = q.shape[0]
    tr = _tile(t, 256)

    def body(q_ref, kv_ref, kr_ref, c_ref, s_ref, qo_ref, ko_ref, vo_ref):
        c = c_ref[...]
        s = s_ref[...]
        krb = kr_ref[...].astype(BF16)
        for h in range(N_HEADS):
            q0 = h * QK
            qo_ref[h, :, 0:NOPE] = q_ref[:, q0:q0 + NOPE].astype(BF16)
            q1 = q_ref[:, q0 + NOPE:q0 + NOPE + HALF]
            q2 = q_ref[:, q0 + NOPE + HALF:q0 + QK]
            qo_ref[h, :, NOPE:NOPE + HALF] = (q1 * c - q2 * s).astype(BF16)
            qo_ref[h, :, NOPE + HALF:QK] = (q1 * s + q2 * c).astype(BF16)
            k0 = h * (NOPE + VDIM)
            ko_ref[h, :, 0:NOPE] = kv_ref[:, k0:k0 + NOPE]
            ko_ref[h, :, NOPE:QK] = krb
            vo_ref[h] = kv_ref[:, k0 + NOPE:k0 + NOPE + VDIM]

    def row(w):
        return pl.BlockSpec((tr, w), lambda i: (i, 0))

    def heads(w):
        return pl.BlockSpec((N_HEADS, tr, w), lambda i: (0, i, 0))

    return pl.pallas_call(
        body, name=name, grid=(t // tr,),
        in_specs=[row(N_HEADS * QK), row(N_HEADS * (NOPE + VDIM)), row(ROPE), row(HALF), row(HALF)],
        out_specs=[heads(QK), heads(QK), heads(VDIM)],
        out_shape=[S((N_HEADS, t, QK), BF16), S((N_HEADS, t, QK), BF16), S((N_HEADS, t, VDIM), BF16)],
        compiler_params=_params(1))(q, kv, kr, cos, sin)


def qkv_heads_bwd(name, dq_h, dk_h, dv_h, cos, sin):
    t = dq_h.shape[1]
    tr = _tile(t, 256)

    def body(dq_ref, dk_ref, dv_ref, c_ref, s_ref, q_ref, kv_ref, kr_ref):
        c = c_ref[...]
        s = s_ref[...]
        dkr = jnp.zeros((tr, ROPE), F32)
        for h in range(N_HEADS):
            q0 = h * QK
            q_ref[:, q0:q0 + NOPE] = dq_ref[h, :, 0:NOPE].astype(BF16)
            d1 = dq_ref[h, :, NOPE:NOPE + HALF]
            d2 = dq_ref[h, :, NOPE + HALF:QK]
            q_ref[:, q0 + NOPE:q0 + NOPE + HALF] = (d1 * c + d2 * s).astype(BF16)
            q_ref[:, q0 + NOPE + HALF:q0 + QK] = (d2 * c - d1 * s).astype(BF16)
            k0 = h * (NOPE + VDIM)
            kv_ref[:, k0:k0 + NOPE] = dk_ref[h, :, 0:NOPE].astype(BF16)
            kv_ref[:, k0 + NOPE:k0 + NOPE + VDIM] = dv_ref[h].astype(BF16)
            dkr = dkr + dk_ref[h, :, NOPE:QK]
        kr_ref[...] = dkr

    def row(w):
        return pl.BlockSpec((tr, w), lambda i: (i, 0))

    def heads(w):
        return pl.BlockSpec((N_HEADS, tr, w), lambda i: (0, i, 0))

    return pl.pallas_call(
        body, name=name, grid=(t // tr,),
        in_specs=[heads(QK), heads(QK), heads(VDIM), row(HALF), row(HALF)],
        out_specs=[row(N_HEADS * QK), row(N_HEADS * (NOPE + VDIM)), row(ROPE)],
        out_shape=[S((t, N_HEADS * QK), BF16), S((t, N_HEADS * (NOPE + VDIM)), BF16), S((t, ROPE), F32)],
        compiler_params=_params(1))(dq_h, dk_h, dv_h, cos, sin)


def _chunk_mask_t(q_start, k_start, bq, bk):
    kc = (k_start + lax.broadcasted_iota(jnp.int32, (bk, bq), 0)) // CHUNK
    qc = (q_start + lax.broadcasted_iota(jnp.int32, (bk, bq), 1)) // CHUNK
    return kc <= qc


def attention_fwd(name, q, k, v, shards=()):
    nh, t, _ = q.shape
    blk = ATT_BLOCK
    nq = t // blk
    n = len(shards)
    scale = 1.0 / math.sqrt(QK)

    def body(q_ref, k_ref, v_ref, *refs):
        src = refs[:n]
        o_ref, lse_ref = refs[n:n + 2]
        dst = refs[n + 2:2 * n + 2]
        m_ref, l_ref, acc_ref = refs[2 * n + 2:2 * n + 5]
        i = pl.program_id(1)
        if n:
            send_sems, recv_sems = refs[2 * n + 5:]
            _ride_along(gather_ici_copies(src, dst, send_sems, recv_sems), (pl.program_id(0), i), (nh, nq))
        qv = q_ref[...]
        m_ref[...] = jnp.full_like(m_ref, MASK_VALUE)
        l_ref[...] = jnp.zeros_like(l_ref)
        acc_ref[...] = jnp.zeros_like(acc_ref)

        def block(j, masked):
            start = pl.multiple_of(j * blk, blk)
            kb = k_ref[pl.ds(start, blk), :]
            vb = v_ref[pl.ds(start, blk), :]
            s = lax.dot_general(kb, qv, _NT, preferred_element_type=F32) * scale
            if masked:
                s = jnp.where(_chunk_mask_t(i * blk, j * blk, blk, blk), s, MASK_VALUE)
            m_old = m_ref[...]
            m_new = jnp.maximum(m_old, jnp.max(s, axis=0, keepdims=True))
            p = jnp.exp(s - m_new)
            alpha = jnp.exp(m_old - m_new)
            l_ref[...] = alpha * l_ref[...] + jnp.sum(p, axis=0, keepdims=True)
            acc_ref[...] = alpha * acc_ref[...] + lax.dot_general(vb, p.astype(BF16), _TN, preferred_element_type=F32)
            m_ref[...] = m_new

        def step(j, carry):
            block(j, False)
            return carry

        lax.fori_loop(0, i, step, 0)
        block(i, True)
        l = l_ref[...]
        o_ref[...] = (acc_ref[...] / l).T
        lse_ref[...] = m_ref[...] + jnp.log(l)

    outs = pl.pallas_call(
        body, name=name, grid=(nh, nq),
        in_specs=[pl.BlockSpec((None, blk, QK), lambda h, i: (h, i, 0)), pl.BlockSpec((None, t, QK), lambda h, i: (h, 0, 0)),
                  pl.BlockSpec((None, t, VDIM), lambda h, i: (h, 0, 0))] + [ANY] * n,
        out_specs=[pl.BlockSpec((blk, VDIM), lambda h, i: (i, h)),
                   pl.BlockSpec((None, None, 1, blk), lambda h, i: (h, i, 0, 0))] + [ANY] * n,
        out_shape=[S((t, nh * VDIM), F32), S((nh, nq, 1, blk), F32)] + [S((N_CHIPS,) + s.shape, s.dtype) for s in shards],
        scratch_shapes=[pltpu.VMEM((1, blk), F32), pltpu.VMEM((1, blk), F32), pltpu.VMEM((VDIM, blk), F32)]
        + ([pltpu.SemaphoreType.DMA((n, 3)), pltpu.SemaphoreType.DMA((n, 3))] if n else []),
        compiler_params=_params(2))(q, k, v, *shards)
    return outs[0], outs[1], list(outs[2:])


def attention_delta(name, do, o):
    t = do.shape[0]
    blk = ATT_BLOCK

    def body(do_ref, o_ref, d_ref):
        for h in range(N_HEADS):
            cols = slice(h * VDIM, (h + 1) * VDIM)
            d_ref[h] = jnp.sum((do_ref[:, cols] * o_ref[:, cols]).T, axis=0, keepdims=True)

    tile = pl.BlockSpec((blk, N_HEADS * VDIM), lambda i: (i, 0))
    return pl.pallas_call(
        body, name=name, grid=(t // blk,), in_specs=[tile, tile],
        out_specs=pl.BlockSpec((N_HEADS, None, 1, blk), lambda i: (0, i, 0, 0)), out_shape=S((N_HEADS, t // blk, 1, blk), F32),
        compiler_params=_params(1))(do, o)


def attention_bwd(name, q, k, v, do, lse, delta, parts=()):
    nh, t, _ = q.shape
    blk = ATT_BLOCK
    nq = t // blk
    n = len(parts)
    scale = 1.0 / math.sqrt(QK)

    def body(q_ref, k_ref, v_ref, do_ref, lse_ref, dl_ref, *refs):
        src = refs[:n]
        dq_ref, dk_ref, dv_ref = refs[n:n + 3]
        dst = refs[n + 3:2 * n + 3]
        j = pl.program_id(1)
        if n:
            send_sems, recv_sems = refs[2 * n + 3:]
            _ride_along(scatter_ici_copies(src, dst, send_sems, recv_sems), (pl.program_id(0), j), (nh, nq))

        @pl.when(j == 0)
        def _():
            dq_ref[...] = jnp.zeros_like(dq_ref)

        kb = k_ref[...]
        vb = v_ref[...]
        dk_ref[...] = jnp.zeros_like(dk_ref)
        dv_ref[...] = jnp.zeros_like(dv_ref)

        def block(i, masked):
            rows = pl.ds(pl.multiple_of(i * blk, blk), blk)
            qb = q_ref[rows, :]
            dob = do_ref[rows, :].astype(BF16)
            s = lax.dot_general(kb, qb, _NT, preferred_element_type=F32) * scale
            if masked:
                s = jnp.where(_chunk_mask_t(i * blk, j * blk, blk, blk), s, MASK_VALUE)
            p = jnp.exp(s - lse_ref[i])
            dp = lax.dot_general(vb, dob, _NT, preferred_element_type=F32)
            ds = (p * (dp - dl_ref[i]) * scale).astype(BF16)
            dv_ref[...] += jnp.dot(p.astype(BF16), dob, preferred_element_type=F32)
            dk_ref[...] += jnp.dot(ds, qb, preferred_element_type=F32)
            dq_ref[rows, :] += lax.dot_general(ds, kb, _TN, preferred_element_type=F32)

        block(j, True)

        def step(i, carry):
            block(i, False)
            return carry

        lax.fori_loop(j + 1, nq, step, 0)

    head_all = lambda w: pl.BlockSpec((None, t, w), lambda h, j: (h, 0, 0))
    head_blk = lambda w: pl.BlockSpec((None, blk, w), lambda h, j: (h, j, 0))
    stats = pl.BlockSpec((None, nq, 1, blk), lambda h, j: (h, 0, 0, 0))
    outs = pl.pallas_call(
        body, name=name, grid=(nh, nq),
        in_specs=[head_all(QK), head_blk(QK), head_blk(VDIM), pl.BlockSpec((t, VDIM), lambda h, j: (0, h)), stats, stats] + [ANY] * n,
        out_specs=[head_all(QK), head_blk(QK), head_blk(VDIM)] + [ANY] * n,
        out_shape=[S((nh, t, QK), F32), S((nh, t, QK), F32), S((nh, t, VDIM), F32)] + [S(p.shape, p.dtype) for p in parts],
        scratch_shapes=[pltpu.SemaphoreType.DMA((n, 3)), pltpu.SemaphoreType.DMA((n, 3))] if n else [],
        compiler_params=_params(2))(q, k, v, do, lse, delta, *parts)
    return outs[0], outs[1], outs[2], list(outs[3:])


def _shift_down(u, s):
    rows = lax.broadcasted_iota(jnp.int32, u.shape, 0)
    return jnp.where(rows >= s, pltpu.roll(u, s, 0), 0.0)


def _shift_up(u, s):
    n = u.shape[0]
    rows = lax.broadcasted_iota(jnp.int32, u.shape, 0)
    return jnp.where(rows < n - s, pltpu.roll(u, n - s, 0), 0.0)


def _conv_specs(t, d, lanes):
    slab = lambda part: pl.BlockSpec((None, t, lanes), lambda j, part=part: (part, 0, j))
    return slab, pl.BlockSpec((3, lanes), lambda j: (0, j)), pl.BlockSpec((t, lanes), lambda j: (0, j))


def conv_fwd(name, bcx, w):
    _, t, d = bcx.shape
    lanes = _tile(d, 128, 128)
    slab, w_spec, col = _conv_specs(t, d, lanes)

    def body(b_ref, c_ref, x_ref, w_ref, y_ref):
        u = c_ref[...] * x_ref[...]
        uc = w_ref[0:1, :] * _shift_down(u, 2) + w_ref[1:2, :] * _shift_down(u, 1) + w_ref[2:3, :] * u
        y_ref[...] = (b_ref[...] * uc).astype(BF16)

    return pl.pallas_call(
        body, name=name, grid=(d // lanes,), in_specs=[slab(0), slab(1), slab(2), w_spec], out_specs=col,
        out_shape=S((t, d), BF16), compiler_params=_params(1))(bcx, bcx, bcx, w)


def conv_bwd(name, bcx, w, dy):
    _, t, d = bcx.shape
    lanes = _tile(d, 128, 128)
    slab, w_spec, col = _conv_specs(t, d, lanes)

    def body(b_ref, c_ref, x_ref, w_ref, dy_ref, d_ref, dw_ref):
        c = c_ref[...]
        x = x_ref[...]
        dyv = dy_ref[...]
        u = c * x
        u1 = _shift_down(u, 1)
        u2 = _shift_down(u, 2)
        w0, w1, w2 = w_ref[0:1, :], w_ref[1:2, :], w_ref[2:3, :]
        d_ref[0] = (dyv * (w0 * u2 + w1 * u1 + w2 * u)).astype(BF16)
        duc = dyv * b_ref[...]
        dw_ref[0:1, :] = jnp.sum(duc * u2, axis=0, keepdims=True)
        dw_ref[1:2, :] = jnp.sum(duc * u1, axis=0, keepdims=True)
        dw_ref[2:3, :] = jnp.sum(duc * u, axis=0, keepdims=True)
        du = w2 * duc + w1 * _shift_up(duc, 1) + w0 * _shift_up(duc, 2)
        d_ref[1] = (du * x).astype(BF16)
        d_ref[2] = (du * c).astype(BF16)

    return pl.pallas_call(
        body, name=name, grid=(d // lanes,), in_specs=[slab(0), slab(1), slab(2), w_spec, col],
        out_specs=[pl.BlockSpec((3, t, lanes), lambda j: (0, 0, j)), w_spec], out_shape=[S((3, t, d), BF16), S((3, d), F32)],
        compiler_params=_params(1))(bcx, bcx, bcx, w, dy)


def adamw(name, w, g, m, v):
    r, c = w.shape
    tr = _tile(r, 512)

    def body(w_ref, g_ref, m_ref, v_ref, d_ref, mo_ref, vo_ref):
        gv = g_ref[...]
        m_new = ADAM_B1 * m_ref[...] + (1.0 - ADAM_B1) * gv
        v_new = ADAM_B2 * v_ref[...] + (1.0 - ADAM_B2) * (gv * gv)
        m_hat = m_new / (1.0 - ADAM_B1 ** ADAM_STEP)
        v_hat = v_new / (1.0 - ADAM_B2 ** ADAM_STEP)
        d_ref[...] = -ADAM_LR * (m_hat / (jnp.sqrt(v_hat) + ADAM_EPS) + ADAM_WD * w_ref[...])
        mo_ref[...] = m_new
        vo_ref[...] = v_new

    blk = pl.BlockSpec((tr, c), lambda i: (i, 0))
    return pl.pallas_call(
        body, name=name, grid=(r // tr,), in_specs=[blk] * 4, out_specs=[blk] * 3, out_shape=[S((r, c), F32)] * 3,
        compiler_params=_params(1))(w, g, m, v)


def _place():
    x, y, c = lax.axis_index("x"), lax.axis_index("y"), lax.axis_index("c")
    other_chips = [(1 - x, y), (x, 1 - y), (1 - x, 1 - y)]
    return x, y, c, other_chips


def _half(c, rows):
    return pl.ds(pl.multiple_of(c * (rows // 2), 16), rows // 2)


def gather_weight_shards(shards):
    n = len(shards)

    def body(*refs):
        src = refs[:n]
        dst = refs[n:2 * n]
        send_sems, recv_sems = refs[2 * n:]
        x, y, c, chips = _place()
        me = 2 * x + y
        sibling = (x, y, 1 - c)

        def copy(i, slot, half_of, sem, to, from_input=False):
            rows = _half(half_of, src[i].shape[0])
            return pltpu.make_async_remote_copy(
                src_ref=src[i].at[rows] if from_input else dst[i].at[slot, rows], dst_ref=dst[i].at[slot, rows],
                send_sem=send_sems.at[i, sem], recv_sem=recv_sems.at[i, sem], device_id=to, device_id_type=MESH)

        sent = []
        for i in range(n):
            for j, chip in enumerate(chips):
                sent.append(copy(i, me, c, j, (*chip, c), from_input=True))
                sent[-1].start()
        for i in range(n):
            for j, (px, py) in enumerate(chips):
                copy(i, 2 * px + py, c, j, sibling).wait_recv()
                sent.append(copy(i, 2 * px + py, c, 3 + j, sibling))
                sent[-1].start()
        for i in range(n):
            for j, (px, py) in enumerate(chips):
                copy(i, 2 * px + py, 1 - c, 3 + j, sibling).wait_recv()
        for cp in sent:
            cp.wait_send()

    outs = pl.pallas_call(
        body, name="gather_weight_shards", in_specs=[ANY] * n, out_specs=[ANY] * n,
        out_shape=[S((N_CHIPS,) + s.shape, s.dtype) for s in shards],
        scratch_shapes=[pltpu.SemaphoreType.DMA((n, 6)), pltpu.SemaphoreType.DMA((n, 6))],
    )(*shards)
    return _fill_own_slot(outs, [s[None] for s in shards])


def gather_ici_copies(src, dst, send_sems, recv_sems):
    x, y, c, chips = _place()
    me = 2 * x + y
    pairs = []
    for i in range(len(src)):
        rows = _half(c, src[i].shape[0])
        for j, (px, py) in enumerate(chips):
            def copy(slot):
                return pltpu.make_async_remote_copy(
                    src_ref=src[i].at[rows], dst_ref=dst[i].at[slot, rows], send_sem=send_sems.at[i, j],
                    recv_sem=recv_sems.at[i, j], device_id=(px, py, c), device_id_type=MESH)
            pairs.append((copy(me), copy(2 * px + py)))
    return pairs


def scatter_ici_copies(src, dst, send_sems, recv_sems):
    x, y, c, chips = _place()
    me = 2 * x + y
    pairs = []
    for i in range(len(src)):
        for j, (px, py) in enumerate(chips):
            def copy(from_slot, to_slot):
                return pltpu.make_async_remote_copy(
                    src_ref=src[i].at[from_slot], dst_ref=dst[i].at[to_slot], send_sem=send_sems.at[i, j],
                    recv_sem=recv_sems.at[i, j], device_id=(px, py, c), device_id_type=MESH)
            pairs.append((copy(2 * px + py, me), copy(me, 2 * px + py)))
    return pairs


def _ride_along(pairs, grid_ids, grid_sizes):
    first = (grid_ids[0] == 0) & (grid_ids[1] == 0)
    last = (grid_ids[0] == grid_sizes[0] - 1) & (grid_ids[1] == grid_sizes[1] - 1)

    @pl.when(first)
    def _():
        for outgoing, _ in pairs:
            outgoing.start()

    @pl.when(last)
    def _():
        for _, incoming in pairs:
            incoming.wait_recv()
        for outgoing, _ in pairs:
            outgoing.wait_send()


def _fill_own_slot(gathered, own):
    me = 2 * lax.axis_index("x") + lax.axis_index("y")
    return [lax.dynamic_update_slice(g, o, (me,) + (0,) * (g.ndim - 1)) for g, o in zip(gathered, own)]


def forward_to_sibling(gathered):
    n = len(gathered)

    def body(*refs):
        src = refs[:n]
        dst = refs[n:2 * n]
        send_sems, recv_sems = refs[2 * n:]
        x, y, c, chips = _place()
        pairs = []
        for i in range(n):
            for j, (px, py) in enumerate(chips):
                def copy(half_of):
                    rows = _half(half_of, src[i].shape[1])
                    return pltpu.make_async_remote_copy(
                        src_ref=src[i].at[2 * px + py, rows], dst_ref=dst[i].at[2 * px + py, rows], send_sem=send_sems.at[i, j],
                        recv_sem=recv_sems.at[i, j], device_id=(x, y, 1 - c), device_id_type=MESH)
                pairs.append((copy(c), copy(1 - c)))
        for outgoing, _ in pairs:
            outgoing.start()
        for _, incoming in pairs:
            incoming.wait_recv()
        for outgoing, _ in pairs:
            outgoing.wait_send()

    return pl.pallas_call(
        body, name="forward_to_sibling", in_specs=[ANY] * n, out_specs=[ANY] * n,
        out_shape=[S(g.shape, g.dtype) for g in gathered], input_output_aliases={i: i for i in range(n)},
        scratch_shapes=[pltpu.SemaphoreType.DMA((n, 3)), pltpu.SemaphoreType.DMA((n, 3))],
    )(*gathered)


def sibling_swap_halves(name, grads):
    n = len(grads)

    def body(*refs):
        src = refs[:n]
        dst = refs[n:2 * n]
        send_sems, recv_sems = refs[2 * n:]
        x, y, c, _ = _place()
        copies = [pltpu.make_async_remote_copy(
            src_ref=src[i].at[:, _half(1 - c, src[i].shape[1]), :], dst_ref=dst[i], send_sem=send_sems.at[i],
            recv_sem=recv_sems.at[i], device_id=(x, y, 1 - c), device_id_type=MESH) for i in range(n)]
        for cp in copies:
            cp.start()
        for cp in copies:
            cp.wait()

    return pl.pallas_call(
        body, name=name, in_specs=[ANY] * n, out_specs=[ANY] * n,
        out_shape=[S((g.shape[0], g.shape[1] // 2, g.shape[2]), g.dtype) for g in grads],
        scratch_shapes=[pltpu.SemaphoreType.DMA((n,)), pltpu.SemaphoreType.DMA((n,))],
    )(*grads)


def add_halves(name, g, rx):
    _, r, cdim = g.shape
    r2 = r // 2
    tr = _tile(r2, 512, 16)
    nb = r2 // tr

    def body(lo_ref, hi_ref, rx_ref, o_ref):
        mine = jnp.where(lax.axis_index("c") == 0, lo_ref[...], hi_ref[...])
        o_ref[...] = (mine.astype(F32) + rx_ref[...].astype(F32)).astype(BF16)

    half = pl.BlockSpec((None, tr, cdim), lambda k, i: (k, i, 0))
    return pl.pallas_call(
        body, name=name, grid=(N_CHIPS, nb),
        in_specs=[half, pl.BlockSpec((None, tr, cdim), lambda k, i: (k, nb + i, 0)), half],
        out_specs=half, out_shape=S((N_CHIPS, r2, cdim), BF16), compiler_params=_params(2))(g, g, rx)


def scatter_to_owner_chips(parts):
    n = len(parts)

    def body(*refs):
        src = refs[:n]
        dst = refs[n:2 * n]
        send_sems, recv_sems = refs[2 * n:]
        pairs = scatter_ici_copies(src, dst, send_sems, recv_sems)
        for outgoing, _ in pairs:
            outgoing.start()
        for _, incoming in pairs:
            incoming.wait_recv()
        for outgoing, _ in pairs:
            outgoing.wait_send()

    return pl.pallas_call(
        body, name="scatter_to_owner_chips", in_specs=[ANY] * n, out_specs=[ANY] * n,
        out_shape=[S(p.shape, p.dtype) for p in parts],
        scratch_shapes=[pltpu.SemaphoreType.DMA((n, 3)), pltpu.SemaphoreType.DMA((n, 3))],
    )(*parts)


def _own_slots(parts):
    me = 2 * lax.axis_index("x") + lax.axis_index("y")
    return [lax.dynamic_slice(p, (me, 0, 0), (1,) + p.shape[1:]) for p in parts]


def sum_chips(name, parts):
    _, r2, cdim = parts.shape
    tr = _tile(r2, 512, 16)

    def body(p_ref, o_ref):
        acc = p_ref[0].astype(F32)
        for k in range(1, N_CHIPS):
            acc = acc + p_ref[k].astype(F32)
        o_ref[...] = acc

    return pl.pallas_call(
        body, name=name, grid=(r2 // tr,), in_specs=[pl.BlockSpec((N_CHIPS, tr, cdim), lambda i: (0, i, 0))],
        out_specs=pl.BlockSpec((tr, cdim), lambda i: (i, 0)), out_shape=S((r2, cdim), F32), compiler_params=_params(1))(parts)


def sibling_join_halves(name, halves, targets, where):
    n = len(halves)

    def rows_of(i, half_of):
        r2 = halves[i].shape[0]
        return pl.ds(pl.multiple_of(where[i][1] + half_of * r2, 8), r2)

    def body(*refs):
        src = refs[:n]
        dst = refs[n:n + len(targets)]
        send_sems, recv_sems = refs[n + len(targets):]
        x, y, c, _ = _place()

        def copy(i, half_of):
            return pltpu.make_async_remote_copy(
                src_ref=src[i], dst_ref=dst[where[i][0]].at[rows_of(i, half_of)], send_sem=send_sems.at[i],
                recv_sem=recv_sems.at[i], device_id=(x, y, 1 - c), device_id_type=MESH)

        for i in range(n):
            copy(i, c).start()
        for i in range(n):
            copy(i, 1 - c).wait_recv()
        for i in range(n):
            copy(i, c).wait_send()

    outs = list(pl.pallas_call(
        body, name=name, in_specs=[ANY] * n, out_specs=[ANY] * len(targets), out_shape=[S(tg, F32) for tg in targets],
        scratch_shapes=[pltpu.SemaphoreType.DMA((n,)), pltpu.SemaphoreType.DMA((n,))],
    )(*halves))
    c = lax.axis_index("c")
    for i, h in enumerate(halves):
        tgt, first = where[i]
        outs[tgt] = lax.dynamic_update_slice(outs[tgt], h, (first + c * h.shape[0], 0))
    return outs


def all_reduce_small(name, packed):
    rows, width = packed.shape

    def body(x_ref, o_ref, gathered, send_sems, recv_sems):
        x, y, c, _ = _place()
        me = 4 * x + 2 * y + c
        gathered[me] = x_ref[...]
        flips = [(fx, fy, fc) for fx in (0, 1) for fy in (0, 1) for fc in (0, 1)][1:]

        def copy(r, slot, to):
            return pltpu.make_async_remote_copy(
                src_ref=x_ref, dst_ref=gathered.at[slot], send_sem=send_sems.at[r], recv_sem=recv_sems.at[r],
                device_id=to, device_id_type=MESH)

        def peer(f):
            return (x ^ f[0], y ^ f[1], c ^ f[2])

        sent = [copy(r, me, peer(f)) for r, f in enumerate(flips)]
        for cp in sent:
            cp.start()
        for r, f in enumerate(flips):
            px, py, pc = peer(f)
            copy(r, 4 * px + 2 * py + pc, peer(f)).wait_recv()
        for cp in sent:
            cp.wait_send()
        acc = gathered[0]
        for k in range(1, N_DEV):
            acc = acc + gathered[k]
        o_ref[...] = acc

    vmem = pl.BlockSpec(memory_space=pltpu.VMEM)
    return pl.pallas_call(
        body, name=name, in_specs=[vmem], out_specs=vmem, out_shape=S((rows, width), F32),
        scratch_shapes=[pltpu.VMEM((N_DEV, rows, width), F32), pltpu.SemaphoreType.DMA((N_DEV - 1,)),
                        pltpu.SemaphoreType.DMA((N_DEV - 1,))],
    )(packed)


def _rope_tables(positions):
    inv_freq = 1.0 / (ROPE_THETA ** (jnp.arange(0, ROPE, 2, dtype=F32) / ROPE))
    ang = positions.astype(F32)[:, None] * inv_freq
    return jnp.cos(ang), jnp.sin(ang)


def _unstack_cols(w):
    k4, k, n4 = w.shape
    return jnp.transpose(w, (1, 0, 2)).reshape(k, k4 * n4)


def _stack_cols(w):
    k, n = w.shape
    return jnp.transpose(w.reshape(k, N_CHIPS, n // N_CHIPS), (1, 0, 2))


def kernel(x, positions, mla_norm, mla_w_in, mla_g_cq, mla_g_ckv, mla_w_uq, mla_w_ukv, mla_w_o, conv_norm, conv_w_in, conv_w, conv_w_out, ffn_norm, ffn_w_gate, ffn_w_up, ffn_w_down, final_norm, loss_target, m_mla_norm, m_mla_w_in, m_mla_g_cq, m_mla_g_ckv, m_mla_w_uq, m_mla_w_ukv, m_mla_w_o, m_conv_norm, m_conv_w_in, m_conv_w, m_conv_w_out, m_ffn_norm, m_ffn_w_gate, m_ffn_w_up, m_ffn_w_down, m_final_norm, v_mla_norm, v_mla_w_in, v_mla_g_cq, v_mla_g_ckv, v_mla_w_uq, v_mla_w_ukv, v_mla_w_o, v_conv_norm, v_conv_w_in, v_conv_w, v_conv_w_out, v_ffn_norm, v_ffn_w_gate, v_ffn_w_up, v_ffn_w_down, v_final_norm):
    weights = dict(mla_norm=mla_norm, mla_w_in=mla_w_in, mla_g_cq=mla_g_cq, mla_g_ckv=mla_g_ckv, mla_w_uq=mla_w_uq,
                   mla_w_ukv=mla_w_ukv, mla_w_o=mla_w_o, conv_norm=conv_norm, conv_w_in=conv_w_in, conv_w=conv_w,
                   conv_w_out=conv_w_out, ffn_norm=ffn_norm, ffn_w_gate=ffn_w_gate, ffn_w_up=ffn_w_up,
                   ffn_w_down=ffn_w_down, final_norm=final_norm)
    m_in = dict(mla_norm=m_mla_norm, mla_w_in=m_mla_w_in, mla_g_cq=m_mla_g_cq, mla_g_ckv=m_mla_g_ckv, mla_w_uq=m_mla_w_uq,
                mla_w_ukv=m_mla_w_ukv, mla_w_o=m_mla_w_o, conv_norm=m_conv_norm, conv_w_in=m_conv_w_in, conv_w=m_conv_w,
                conv_w_out=m_conv_w_out, ffn_norm=m_ffn_norm, ffn_w_gate=m_ffn_w_gate, ffn_w_up=m_ffn_w_up,
                ffn_w_down=m_ffn_w_down, final_norm=m_final_norm)
    v_in = dict(mla_norm=v_mla_norm, mla_w_in=v_mla_w_in, mla_g_cq=v_mla_g_cq, mla_g_ckv=v_mla_g_ckv, mla_w_uq=v_mla_w_uq,
                mla_w_ukv=v_mla_w_ukv, mla_w_o=v_mla_w_o, conv_norm=v_conv_norm, conv_w_in=v_conv_w_in, conv_w=v_conv_w,
                conv_w_out=v_conv_w_out, ffn_norm=v_ffn_norm, ffn_w_gate=v_ffn_w_gate, ffn_w_up=v_ffn_w_up,
                ffn_w_down=v_ffn_w_down, final_norm=v_final_norm)
    big = ["mla_w_in", "mla_w_uq", "mla_w_ukv", "mla_w_o", "conv_w_in", "conv_w_out", "ffn_w_gate", "ffn_w_up", "ffn_w_down"]
    order = list(weights)

    t, d = x.shape[1], x.shape[2]
    h0 = x.reshape(t, d)
    target = loss_target.reshape(t, d)
    cos, sin = _rope_tables(positions.reshape(t))

    def rows2d(a):
        return a.reshape(-1, a.shape[-1])

    first, later = big[:4], big[4:]
    shards = {n: rows2d(weights[n]).astype(BF16) for n in big}
    gathered = dict(zip(first, gather_weight_shards([shards[n] for n in first])))
    w_in = gathered["mla_w_in"].reshape(-1, gathered["mla_w_in"].shape[-1])
    w_uq = _unstack_cols(gathered["mla_w_uq"])
    w_ukv = _unstack_cols(gathered["mla_w_ukv"])
    w_o = gathered["mla_w_o"].reshape(-1, d)

    chip = 2 * lax.axis_index("x") + lax.axis_index("y")
    core = lax.axis_index("c")
    d4 = d // N_CHIPS
    first_core = (core == 0).astype(F32)

    def place_shard(shard):
        full = jnp.zeros((shard.shape[0], d), F32)
        return lax.dynamic_update_slice(full, shard * first_core, (0, chip * d4))

    def pack_rows(rows):
        idx = lax.broadcasted_iota(jnp.int32, (SMALL_ROWS, d), 0)
        out = jnp.zeros((SMALL_ROWS, d), F32)
        for r, row in enumerate(rows):
            out = out + jnp.where(idx == r, row, 0.0)
        return out

    cw = place_shard(conv_w.reshape(3, d4))
    pre = all_reduce_small("all_gather_conv_small", pack_rows([place_shard(conv_norm.reshape(1, d4)), cw[0:1], cw[1:2], cw[2:3]]))
    conv_norm_full = pre[0:1]
    conv_w_full = pre[1:4]

    a0 = rms_fwd("mla_norm_fwd", h0, mla_norm)
    proj = linear("mla_in_proj", a0, w_in, F32)
    cq, ckv, kr = mla_mid("mla_mid", proj, mla_g_cq, mla_g_ckv, cos, sin)
    q = linear("mla_q_up", cq, w_uq, F32)
    kv = linear("mla_kv_up", ckv, w_ukv, BF16)
    qh, kh, vh = qkv_heads("qkv_heads", q, kv, kr, cos, sin)
    attn, lse, arriving = attention_fwd("attention_fwd", qh, kh, vh, [shards[n] for n in later])
    gathered.update(zip(later, _fill_own_slot(forward_to_sibling(arriving), [shards[n][None] for n in later])))
    cw_in = _unstack_cols(gathered["conv_w_in"])
    cw_out = gathered["conv_w_out"].reshape(-1, d)
    wg_all, wu_all, wd_all = gathered["ffn_w_gate"], gathered["ffn_w_up"], gathered["ffn_w_down"]
    h1, a1 = linear("mla_out_proj", attn, w_o, F32, resid=h0, next_gain=ffn_norm[0:1])

    def ffn_forward(tag, h, a, layer, next_gain):
        g, u, z = ffn_up(f"ffn{tag}_up", a, wg_all, wu_all, layer)
        return g, u, z, ffn_down(f"ffn{tag}_down", z, wd_all, layer, h, next_gain)

    g0, u0, z0, (h2, a2) = ffn_forward(0, h1, a1, 0, conv_norm_full)
    bcx = conv_in_proj("conv_in_proj", a2, cw_in)
    yc = conv_fwd("conv_fwd", bcx, conv_w_full)
    h3, a3 = linear("conv_out_proj", yc, cw_out, F32, resid=h2, next_gain=ffn_norm[1:2])
    g1, u1, z1, h4 = ffn_forward(1, h3, a3, 1, None)
    dh4, d_final_norm, loss_local = loss_head("loss_head", h4, final_norm.reshape(1, d), target)

    def ffn_backward(tag, dh, h, layer, a, g, u, z):
        dg, du = ffn_bwd_hidden(f"ffn{tag}_bwd_hidden", dh, wd_all, layer, g, u)
        d_wd = ffn_wgrad_down(f"ffn{tag}_wgrad_down", z, dh)
        dh_prev, d_norm = ffn_bwd_input(f"ffn{tag}_bwd_input", dg, du, wg_all, wu_all, layer, h, ffn_norm[layer:layer + 1], dh)
        d_wg, d_wu = ffn_wgrad_up(f"ffn{tag}_wgrad_up", a, dg, du)
        return dh_prev, d_norm, [d_wg, d_wu, d_wd]

    def reduce_to_pair_sums(tag, local):
        from_sibling = sibling_swap_halves(f"sibling_swap_{tag}", local)
        return [add_halves(f"pair_sum_{tag}{i}", g, r) for i, (g, r) in enumerate(zip(local, from_sibling))]

    def reduce_from_chips(tag, pair_sums, arrived, targets, where):
        from_chips = _fill_own_slot(arrived, _own_slots(pair_sums))
        my_halves = [sum_chips(f"chip_sum_{tag}{i}", p) for i, p in enumerate(from_chips)]
        return sibling_join_halves(f"sibling_join_{tag}", my_halves, targets, where)

    dh3, d_ffn_norm1, ffn1_grads = ffn_backward(1, dh4, h3, 1, a3, g1, u1, z1)

    dyc = linear_nt("conv_out_bwd_input", dh3, cw_out, F32)
    d_cw_out = wgrad("conv_out_wgrad", yc, dh3)
    dbcx, d_conv_w = conv_bwd("conv_bwd", bcx, conv_w_full, dyc)
    dh2, d_conv_norm = conv_in_bwd_input("conv_in_bwd_input", dbcx, cw_in, h2, conv_norm_full, dh3)
    d_cw_in = conv_in_wgrad("conv_in_wgrad", a2, dbcx)

    dh1, d_ffn_norm0, ffn0_grads = ffn_backward(0, dh2, h1, 0, a1, g0, u0, z0)

    d_attn = linear_nt("mla_out_bwd_input", dh1, w_o, F32)
    d_w_o = wgrad("mla_out_wgrad", attn, dh1)
    rest_pairs = reduce_to_pair_sums("rest", [_stack_cols(d_cw_in), d_cw_out.reshape(N_CHIPS, -1, d)] + ffn1_grads + ffn0_grads
                                     + [d_w_o.reshape(N_CHIPS, -1, d)])
    delta = attention_delta("attention_delta", d_attn, attn)
    dqh, dkh, dvh, rest_arrived = attention_bwd("attention_bwd", qh, kh, vh, d_attn, lse, delta, rest_pairs)
    dq, dkv, dkr = qkv_heads_bwd("qkv_heads_bwd", dqh, dkh, dvh, cos, sin)
    dcq = linear_nt("mla_q_up_bwd_input", dq, w_uq, F32)
    d_w_uq = wgrad("mla_q_up_wgrad", cq, dq)
    dckv = linear_nt("mla_kv_up_bwd_input", dkv, w_ukv, F32)
    d_w_ukv = wgrad("mla_kv_up_wgrad", ckv, dkv)
    dproj, d_g_cq, d_g_ckv = mla_mid_bwd("mla_mid_bwd", proj, mla_g_cq, mla_g_ckv, dcq, dckv, dkr, cos, sin)
    d_w_in = wgrad("mla_in_wgrad", a0, dproj)
    grad_x, d_mla_norm = linear_nt_norm_bwd("mla_in_bwd_input", dproj, w_in, h0, mla_norm, dh1)

    def shard_shape(n):
        return rows2d(weights[n]).shape

    rd, rf = ffn0_grads[0].shape[1], ffn0_grads[2].shape[1]
    rest_where = [(0, 0), (1, 0), (2, rd), (3, rd), (4, rf), (2, 0), (3, 0), (4, 0), (5, 0)]
    rest_names = later + ["mla_w_o"]
    grads = dict(zip(rest_names, reduce_from_chips("rest", rest_pairs, rest_arrived, [shard_shape(n) for n in rest_names], rest_where)))

    mla_pairs = reduce_to_pair_sums("mla", [d_w_in.reshape(N_CHIPS, -1, d_w_in.shape[-1]), _stack_cols(d_w_uq), _stack_cols(d_w_ukv)])
    grads.update(zip(first[:3], reduce_from_chips("mla", mla_pairs, scatter_to_owner_chips(mla_pairs),
                                                  [shard_shape(n) for n in first[:3]], [(i, 0) for i in range(3)])))

    def pad_row(v):
        return jnp.pad(v, ((0, 0), (0, d - v.shape[1])))

    small = all_reduce_small("all_reduce_small_grads", pack_rows([
        d_mla_norm, pad_row(d_g_cq), pad_row(d_g_ckv), d_ffn_norm0, d_ffn_norm1, d_final_norm, d_conv_norm,
        d_conv_w[0:1], d_conv_w[1:2], d_conv_w[2:3], jnp.broadcast_to(loss_local, (1, d))]))
    loss = small[10, 0]
    grads["mla_norm"] = small[0:1]
    grads["mla_g_cq"] = small[1:2, :mla_g_cq.shape[1]]
    grads["mla_g_ckv"] = small[2:3, :mla_g_ckv.shape[1]]
    grads["ffn_norm"] = small[3:5]
    grads["final_norm"] = small[5:6]
    grads["conv_norm"] = lax.dynamic_slice(small[6:7], (0, chip * d4), (1, d4))
    grads["conv_w"] = lax.dynamic_slice(small[7:10], (0, chip * d4), (3, d4))

    outs_g, outs_d, outs_m, outs_v = [], [], [], []
    for n in order:
        w = weights[n]
        delta_w, new_m, new_v = adamw(f"adamw_{n}", rows2d(w), grads[n].reshape(rows2d(w).shape), rows2d(m_in[n]), rows2d(v_in[n]))
        outs_g.append(grads[n].reshape(w.shape))
        outs_d.append(delta_w.reshape(w.shape))
        outs_m.append(new_m.reshape(w.shape))
        outs_v.append(new_v.reshape(w.shape))
    return (loss, grad_x.reshape(x.shape), *outs_g, *outs_d, *outs_m, *outs_v)
```

```python
import math

import jax
import jax.numpy as jnp
from jax import lax
from jax.experimental import pallas as pl
from jax.experimental.pallas import tpu as pltpu

F32 = jnp.float32
BF16 = jnp.bfloat16
S = jax.ShapeDtypeStruct

N_HEADS = 8
NOPE = 128
ROPE = 64
HALF = ROPE // 2
VDIM = 128
QK = NOPE + ROPE
CHUNK = 64
ROPE_THETA = 10000.0
RMS_EPS = 1e-6
ADAM_LR = 0.001
ADAM_B1 = 0.9
ADAM_B2 = 0.999
ADAM_EPS = 1e-08
ADAM_WD = 0.01
ADAM_STEP = 10

N_CHIPS = 4
N_DEV = 8
MASK_VALUE = -1e30
SCORE_SCALE = 1.0 / math.sqrt(QK)
LOG2_E = math.log2(math.e)
SCORE_SCALE_LOG2 = SCORE_SCALE * LOG2_E
VMEM_LIMIT = 48 * 1024 * 1024
ATT_BLOCK = 512
SMALL_ROWS = 16

_NN = (((1,), (0,)), ((), ()))
_NT = (((1,), (1,)), ((), ()))
_TN = (((0,), (0,)), ((), ()))
MESH = pl.DeviceIdType.MESH
ANY = pl.BlockSpec(memory_space=pl.ANY)


def _params(n_axes):
    return pltpu.CompilerParams(dimension_semantics=("arbitrary",) * n_axes, vmem_limit_bytes=VMEM_LIMIT)


def _tile(n, cap, mult=8):
    for t in range(min(cap, n), 0, -1):
        if n % t == 0 and t % mult == 0:
            return t
    return n


def _sigmoid(x):
    return 1.0 / (1.0 + jnp.exp(-x))


def _mm(name, a_ops, b_ops, products, dims, grid, k_axis, outs, acc_shape, epilogue, extra_ops=()):
    na, nb, ne, no = len(a_ops), len(b_ops), len(extra_ops), len(outs)
    n_acc = 1 + max(c for _, _, c in products)
    nk = 1 if k_axis is None else grid[k_axis]

    def body(*refs):
        a_refs = refs[:na]
        b_refs = refs[na:na + nb]
        e_refs = refs[na + nb:na + nb + ne]
        o_refs = refs[na + nb + ne:na + nb + ne + no]
        acc_refs = refs[na + nb + ne + no:]

        def partial_sums():
            vals = [None] * n_acc
            for ai, bi, ci in products:
                d = lax.dot_general(a_refs[ai][...].astype(BF16), b_refs[bi][...].astype(BF16), dims,
                                    preferred_element_type=F32)
                vals[ci] = d if vals[ci] is None else vals[ci] + d
            return vals

        if nk == 1:
            epilogue(partial_sums(), e_refs, o_refs)
        else:
            k = pl.program_id(k_axis)

            @pl.when(k == 0)
            def _():
                for acc in acc_refs:
                    acc[...] = jnp.zeros_like(acc)

            for acc, v in zip(acc_refs, partial_sums()):
                acc[...] += v

            @pl.when(k == nk - 1)
            def _():
                epilogue([acc[...] for acc in acc_refs], e_refs, o_refs)

    ops = list(a_ops) + list(b_ops) + list(extra_ops)
    return pl.pallas_call(
        body, name=name, grid=grid,
        in_specs=[s for _, s in ops], out_specs=[s for _, s in outs], out_shape=[o for o, _ in outs],
        scratch_shapes=[pltpu.VMEM(acc_shape, F32) for _ in range(n_acc if nk > 1 else 0)],
        compiler_params=_params(len(grid)),
    )(*[a for a, _ in ops])


def _store(accs, e_refs, o_refs):
    o_refs[0][...] = accs[0].astype(o_refs[0].dtype)


def linear(name, x, w, out_dtype, resid=None, next_gain=None):
    t, k = x.shape
    n = w.shape[1]
    tm = _tile(t, 512)
    tn = n if n <= 2048 else _tile(n, 1024, 128)
    tile = pl.BlockSpec((tm, tn), lambda j, i: (i, j))
    extra = [] if resid is None else [(resid, tile)]
    outs = [(S((t, n), out_dtype), tile)]
    if next_gain is not None:
        assert tn == n
        extra.append((next_gain, pl.BlockSpec((1, n), lambda j, i: (0, 0))))
        outs.append((S((t, n), BF16), tile))

    def epilogue(accs, e_refs, o_refs):
        y = accs[0] if resid is None else e_refs[0][...] + accs[0]
        o_refs[0][...] = y.astype(out_dtype)
        if next_gain is not None:
            o_refs[1][...] = (y * _rstd(y) * e_refs[-1][...]).astype(BF16)

    res = _mm(name, [(x, pl.BlockSpec((tm, k), lambda j, i: (i, 0)))], [(w, pl.BlockSpec((k, tn), lambda j, i: (0, j)))],
              [(0, 0, 0)], _NN, (n // tn, t // tm), None, outs, None, epilogue, extra)
    return res[0] if next_gain is None else res


def linear_nt(name, dy, w, out_dtype):
    t, n = dy.shape
    k = w.shape[0]
    tm = _tile(t, 512)
    tc = n if n <= 2048 else _tile(n, 1024, 128)
    return _mm(name, [(dy, pl.BlockSpec((tm, tc), lambda i, c: (i, c)))], [(w, pl.BlockSpec((k, tc), lambda i, c: (0, c)))],
               [(0, 0, 0)], _NT, (t // tm, n // tc), 1,
               [(S((t, k), out_dtype), pl.BlockSpec((tm, k), lambda i, c: (i, 0)))], (tm, k), _store)[0]


def wgrad(name, x, dy):
    t, k = x.shape
    n = dy.shape[1]
    tk = _tile(t, 512)
    tn = n if n <= 1024 else _tile(n, 1024, 128)
    return _mm(name, [(x, pl.BlockSpec((tk, k), lambda j, s: (s, 0)))], [(dy, pl.BlockSpec((tk, tn), lambda j, s: (s, j)))],
               [(0, 0, 0)], _TN, (n // tn, t // tk), 1,
               [(S((k, n), BF16), pl.BlockSpec((k, tn), lambda j, s: (0, j)))], (k, tn), _store)[0]


def _resident(shape, index_map):
    return pl.BlockSpec(shape, index_map, pipeline_mode=pl.Buffered(1))


def ffn_up(name, a, wg_all, wu_all, layer):
    t, d = a.shape
    f4 = wg_all.shape[2]
    tm = _tile(t, 512)
    w_spec = _resident((N_CHIPS, d, f4), lambda i: (0, layer, 0))
    h_spec = pl.BlockSpec((N_CHIPS, tm, f4), lambda i: (0, i, 0))

    def body(a_ref, wg_ref, wu_ref, g_ref, u_ref, z_ref):
        av = a_ref[...]
        for k in range(N_CHIPS):
            g = jnp.dot(av, wg_ref[k], preferred_element_type=F32)
            u = jnp.dot(av, wu_ref[k], preferred_element_type=F32)
            g_ref[k] = g.astype(BF16)
            u_ref[k] = u.astype(BF16)
            z_ref[k] = (g * _sigmoid(g) * u).astype(BF16)

    return pl.pallas_call(
        body, name=name, grid=(t // tm,), in_specs=[pl.BlockSpec((tm, d), lambda i: (i, 0)), w_spec, w_spec],
        out_specs=[h_spec] * 3, out_shape=[S((N_CHIPS, t, f4), BF16)] * 3, compiler_params=_params(1))(a, wg_all, wu_all)


def ffn_down(name, z, wd_all, layer, resid, next_gain=None):
    _, t, f4 = z.shape
    d = wd_all.shape[2]
    tm = _tile(t, 512)
    row = pl.BlockSpec((tm, d), lambda i: (i, 0))
    normed = next_gain is not None

    def body(z_ref, wd_ref, r_ref, *refs):
        acc = r_ref[...]
        for k in range(N_CHIPS):
            acc = acc + jnp.dot(z_ref[k], wd_ref[k], preferred_element_type=F32)
        refs[-2 if normed else -1][...] = acc
        if normed:
            refs[-1][...] = (acc * _rstd(acc) * refs[0][...]).astype(BF16)

    res = pl.pallas_call(
        body, name=name, grid=(t // tm,),
        in_specs=[pl.BlockSpec((N_CHIPS, tm, f4), lambda i: (0, i, 0)), _resident((N_CHIPS, f4, d), lambda i: (0, layer, 0)), row]
        + ([pl.BlockSpec((1, d), lambda i: (0, 0))] if normed else []),
        out_specs=[row] * (2 if normed else 1), out_shape=[S((t, d), F32)] + ([S((t, d), BF16)] if normed else []),
        compiler_params=_params(1))(z, wd_all, resid, *([next_gain] if normed else []))
    return res if normed else res[0]


def ffn_bwd_hidden(name, dh, wd_all, layer, g, u):
    t, d = dh.shape
    f4 = g.shape[2]
    tm = _tile(t, 512)
    h_spec = pl.BlockSpec((N_CHIPS, tm, f4), lambda i: (0, i, 0))

    def body(dh_ref, wd_ref, g_ref, u_ref, dg_ref, du_ref):
        dhb = dh_ref[...].astype(BF16)
        for k in range(N_CHIPS):
            dz = lax.dot_general(dhb, wd_ref[k], _NT, preferred_element_type=F32)
            gv = g_ref[k].astype(F32)
            uv = u_ref[k].astype(F32)
            sg = 0.5 * jnp.tanh(0.5 * gv) + 0.5
            dg_ref[k] = (dz * uv * (sg * (1.0 + gv * (1.0 - sg)))).astype(BF16)
            du_ref[k] = (dz * (gv * sg)).astype(BF16)

    return pl.pallas_call(
        body, name=name, grid=(t // tm,),
        in_specs=[pl.BlockSpec((tm, d), lambda i: (i, 0)), _resident((N_CHIPS, f4, d), lambda i: (0, layer, 0)), h_spec, h_spec],
        out_specs=[h_spec] * 2, out_shape=[S((N_CHIPS, t, f4), BF16)] * 2, compiler_params=_params(1))(dh, wd_all, g, u)


def _norm_bwd_specs(tm, d):
    row = pl.BlockSpec((tm, d), lambda i: (i, 0))
    vec = pl.BlockSpec((1, d), lambda i: (0, 0))
    return [row, vec, row], [row, vec]


def _norm_bwd_tail(da, h_ref, g_ref, dhi_ref, dho_ref, dgain_ref):
    dx, dgain = _rms_bwd(h_ref[...], g_ref[...], da)
    dho_ref[...] = dhi_ref[...] + dx

    @pl.when(pl.program_id(0) == 0)
    def _():
        dgain_ref[...] = jnp.zeros_like(dgain_ref)

    dgain_ref[...] += dgain


def ffn_bwd_input(name, dg, du, wg_all, wu_all, layer, h, gain, dh_in):
    _, t, f4 = dg.shape
    d = h.shape[1]
    tm = _tile(t, 512)
    h_spec = pl.BlockSpec((N_CHIPS, tm, f4), lambda i: (0, i, 0))
    w_spec = _resident((N_CHIPS, d, f4), lambda i: (0, layer, 0))
    tail_in, tail_out = _norm_bwd_specs(tm, d)

    def body(dg_ref, du_ref, wg_ref, wu_ref, *tail):
        acc = jnp.zeros((tm, d), F32)
        for k in range(N_CHIPS):
            acc = acc + lax.dot_general(dg_ref[k], wg_ref[k], _NT, preferred_element_type=F32)
            acc = acc + lax.dot_general(du_ref[k], wu_ref[k], _NT, preferred_element_type=F32)
        _norm_bwd_tail(acc, *tail)

    return pl.pallas_call(
        body, name=name, grid=(t // tm,), in_specs=[h_spec, h_spec, w_spec, w_spec] + tail_in, out_specs=tail_out,
        out_shape=[S((t, d), F32), S((1, d), F32)], compiler_params=_params(1))(dg, du, wg_all, wu_all, h, gain, dh_in)


def ffn_wgrad_up(name, a, dg, du):
    t, d = a.shape
    f4 = dg.shape[2]
    tk = _tile(t, 512)
    nt = t // tk
    h_spec = pl.BlockSpec((None, tk, f4), lambda k, s: (k, s, 0))
    o_spec = pl.BlockSpec((None, d, f4), lambda k, s: (k, 0, 0))

    def body(a_ref, dg_ref, du_ref, og_ref, ou_ref, accg, accu):
        s = pl.program_id(1)

        @pl.when(s == 0)
        def _():
            accg[...] = jnp.zeros_like(accg)
            accu[...] = jnp.zeros_like(accu)

        av = a_ref[...]
        accg[...] += lax.dot_general(av, dg_ref[...], _TN, preferred_element_type=F32)
        accu[...] += lax.dot_general(av, du_ref[...], _TN, preferred_element_type=F32)

        @pl.when(s == nt - 1)
        def _():
            og_ref[...] = accg[...].astype(BF16)
            ou_ref[...] = accu[...].astype(BF16)

    out = S((N_CHIPS, d, f4), BF16)
    return pl.pallas_call(
        body, name=name, grid=(N_CHIPS, nt), in_specs=[pl.BlockSpec((tk, d), lambda k, s: (s, 0)), h_spec, h_spec],
        out_specs=[o_spec, o_spec], out_shape=[out, out],
        scratch_shapes=[pltpu.VMEM((d, f4), F32), pltpu.VMEM((d, f4), F32)], compiler_params=_params(2))(a, dg, du)


def ffn_wgrad_down(name, z, dh):
    _, t, f4 = z.shape
    d = dh.shape[1]
    tk = _tile(t, 512)
    nt = t // tk

    def body(z_ref, dh_ref, o_ref, acc):
        s = pl.program_id(0)

        @pl.when(s == 0)
        def _():
            acc[...] = jnp.zeros_like(acc)

        dhb = dh_ref[...].astype(BF16)
        for k in range(N_CHIPS):
            acc[k] += lax.dot_general(z_ref[k], dhb, _TN, preferred_element_type=F32)

        @pl.when(s == nt - 1)
        def _():
            o_ref[...] = acc[...].astype(BF16)

    return pl.pallas_call(
        body, name=name, grid=(nt,),
        in_specs=[pl.BlockSpec((N_CHIPS, tk, f4), lambda s: (0, s, 0)), pl.BlockSpec((tk, d), lambda s: (s, 0))],
        out_specs=pl.BlockSpec((N_CHIPS, f4, d), lambda s: (0, 0, 0)), out_shape=S((N_CHIPS, f4, d), BF16),
        scratch_shapes=[pltpu.VMEM((N_CHIPS, f4, d), F32)], compiler_params=_params(1))(z, dh)


def conv_in_proj(name, a, w):
    t, d = a.shape
    tm = _tile(t, 512)
    return _mm(name, [(a, pl.BlockSpec((tm, d), lambda j, i: (i, 0)))], [(w, pl.BlockSpec((d, d), lambda j, i: (0, j)))],
               [(0, 0, 0)], _NN, (3, t // tm), None,
               [(S((3, t, d), F32), pl.BlockSpec((None, tm, d), lambda j, i: (j, i, 0)))], None, _store)[0]


def conv_in_bwd_input(name, dbcx, w, h, gain, dh_in):
    _, t, d = dbcx.shape
    tm = _tile(t, 512)
    tail_in, tail_out = _norm_bwd_specs(tm, d)

    def body(g_ref, w_ref, *tail):
        acc = jnp.zeros((tm, d), F32)
        for j in range(3):
            acc = acc + lax.dot_general(g_ref[j], w_ref[:, j * d:(j + 1) * d], _NT, preferred_element_type=F32)
        _norm_bwd_tail(acc, *tail)

    return pl.pallas_call(
        body, name=name, grid=(t // tm,),
        in_specs=[pl.BlockSpec((3, tm, d), lambda i: (0, i, 0)), _resident((d, 3 * d), lambda i: (0, 0))] + tail_in,
        out_specs=tail_out, out_shape=[S((t, d), F32), S((1, d), F32)], compiler_params=_params(1))(dbcx, w, h, gain, dh_in)


def linear_nt_norm_bwd(name, dy, w, h, gain, dh_in):
    t, n = dy.shape
    k = w.shape[0]
    tm = _tile(t, 512)
    tail_in, tail_out = _norm_bwd_specs(tm, k)

    def body(dy_ref, w_ref, *tail):
        _norm_bwd_tail(lax.dot_general(dy_ref[...].astype(BF16), w_ref[...], _NT, preferred_element_type=F32), *tail)

    return pl.pallas_call(
        body, name=name, grid=(t // tm,),
        in_specs=[pl.BlockSpec((tm, n), lambda i: (i, 0)), _resident((k, n), lambda i: (0, 0))] + tail_in,
        out_specs=tail_out, out_shape=[S((t, k), F32), S((1, k), F32)], compiler_params=_params(1))(dy, w, h, gain, dh_in)


def conv_in_wgrad(name, a, dbcx):
    t, d = a.shape
    tk = _tile(t, 512)
    return _mm(name, [(a, pl.BlockSpec((tk, d), lambda j, s: (s, 0)))], [(dbcx, pl.BlockSpec((None, tk, d), lambda j, s: (j, s, 0)))],
               [(0, 0, 0)], _TN, (3, t // tk), 1,
               [(S((d, 3 * d), BF16), pl.BlockSpec((d, d), lambda j, s: (0, j)))], (d, d), _store)[0]


def _rstd(x):
    return lax.rsqrt(jnp.mean(x * x, axis=-1, keepdims=True) + RMS_EPS)


def _rms_bwd(x, g, dy):
    r = _rstd(x)
    xhat = x * r
    dgain = jnp.sum(dy * xhat, axis=0, keepdims=True)
    dxh = dy * g
    dx = r * (dxh - xhat * jnp.mean(dxh * xhat, axis=-1, keepdims=True))
    return dx, dgain


def rms_fwd(name, h, g):
    t, d = h.shape
    tr = _tile(t, 512)

    def body(h_ref, g_ref, a_ref):
        x = h_ref[...]
        a_ref[...] = (x * _rstd(x) * g_ref[...]).astype(BF16)

    return pl.pallas_call(
        body, name=name, grid=(t // tr,),
        in_specs=[pl.BlockSpec((tr, d), lambda i: (i, 0)), pl.BlockSpec((1, d), lambda i: (0, 0))],
        out_specs=pl.BlockSpec((tr, d), lambda i: (i, 0)), out_shape=S((t, d), BF16), compiler_params=_params(1))(h, g)


def rms_bwd(name, h, g, da, dh_in):
    t, d = h.shape
    tr = _tile(t, 512)

    def body(h_ref, g_ref, da_ref, dhi_ref, dho_ref, dg_ref):
        dx, dgain = _rms_bwd(h_ref[...], g_ref[...], da_ref[...])
        dho_ref[...] = dhi_ref[...] + dx

        @pl.when(pl.program_id(0) == 0)
        def _():
            dg_ref[...] = jnp.zeros_like(dg_ref)

        dg_ref[...] += dgain

    row = pl.BlockSpec((tr, d), lambda i: (i, 0))
    vec = pl.BlockSpec((1, d), lambda i: (0, 0))
    return pl.pallas_call(
        body, name=name, grid=(t // tr,), in_specs=[row, vec, row, row], out_specs=[row, vec],
        out_shape=[S((t, d), F32), S((1, d), F32)], compiler_params=_params(1))(h, g, da, dh_in)


def loss_head(name, h, g, target):
    t, d = h.shape
    tr = _tile(t, 512)

    def body(h_ref, g_ref, t_ref, dh_ref, dg_ref, loss_ref):
        x = h_ref[...]
        g = g_ref[...]
        r = _rstd(x)
        xhat = x * r
        err = xhat * g - t_ref[...]
        dy = err * (1.0 / d)
        dxh = dy * g
        dh_ref[...] = r * (dxh - xhat * jnp.mean(dxh * xhat, axis=-1, keepdims=True))

        @pl.when(pl.program_id(0) == 0)
        def _():
            dg_ref[...] = jnp.zeros_like(dg_ref)
            loss_ref[...] = jnp.zeros_like(loss_ref)

        dg_ref[...] += jnp.sum(dy * xhat, axis=0, keepdims=True)
        per_token = jnp.mean(err * err, axis=-1, keepdims=True)
        loss_ref[...] += 0.5 * jnp.sum(per_token, axis=0, keepdims=True)

    row = pl.BlockSpec((tr, d), lambda i: (i, 0))
    vec = pl.BlockSpec((1, d), lambda i: (0, 0))
    one = pl.BlockSpec((1, 1), lambda i: (0, 0))
    return pl.pallas_call(
        body, name=name, grid=(t // tr,), in_specs=[row, vec, row], out_specs=[row, vec, one],
        out_shape=[S((t, d), F32), S((1, d), F32), S((1, 1), F32)], compiler_params=_params(1))(h, g, target)


def mla_mid(name, proj, g_cq, g_ckv, cos, sin):
    t, n = proj.shape
    ql, kl = g_cq.shape[1], g_ckv.shape[1]
    tr = _tile(t, 512)

    def body(p_ref, gq_ref, gk_ref, c_ref, s_ref, cq_ref, ckv_ref, kr_ref):
        xq = p_ref[:, 0:ql]
        cq_ref[...] = (xq * _rstd(xq) * gq_ref[...]).astype(BF16)
        xk = p_ref[:, ql:ql + kl]
        ckv_ref[...] = (xk * _rstd(xk) * gk_ref[...]).astype(BF16)
        k1 = p_ref[:, ql + kl:ql + kl + HALF]
        k2 = p_ref[:, ql + kl + HALF:ql + kl + ROPE]
        c = c_ref[...]
        s = s_ref[...]
        kr_ref[:, 0:HALF] = k1 * c - k2 * s
        kr_ref[:, HALF:ROPE] = k1 * s + k2 * c

    def row(w):
        return pl.BlockSpec((tr, w), lambda i: (i, 0))

    def vec(w):
        return pl.BlockSpec((1, w), lambda i: (0, 0))

    return pl.pallas_call(
        body, name=name, grid=(t // tr,), in_specs=[row(n), vec(ql), vec(kl), row(HALF), row(HALF)],
        out_specs=[row(ql), row(kl), row(ROPE)], out_shape=[S((t, ql), BF16), S((t, kl), BF16), S((t, ROPE), F32)],
        compiler_params=_params(1))(proj, g_cq, g_ckv, cos, sin)


def mla_mid_bwd(name, proj, g_cq, g_ckv, dcq, dckv, dkr, cos, sin):
    t, n = proj.shape
    ql, kl = g_cq.shape[1], g_ckv.shape[1]
    tr = _tile(t, 512)

    def body(p_ref, gq_ref, gk_ref, dcq_ref, dckv_ref, dkr_ref, c_ref, s_ref, dp_ref, dgq_ref, dgk_ref):
        dxq, dgq = _rms_bwd(p_ref[:, 0:ql], gq_ref[...], dcq_ref[...])
        dp_ref[:, 0:ql] = dxq.astype(BF16)
        dxk, dgk = _rms_bwd(p_ref[:, ql:ql + kl], gk_ref[...], dckv_ref[...])
        dp_ref[:, ql:ql + kl] = dxk.astype(BF16)
        d1 = dkr_ref[:, 0:HALF]
        d2 = dkr_ref[:, HALF:ROPE]
        c = c_ref[...]
        s = s_ref[...]
        dp_ref[:, ql + kl:ql + kl + HALF] = (d1 * c + d2 * s).astype(BF16)
        dp_ref[:, ql + kl + HALF:ql + kl + ROPE] = (d2 * c - d1 * s).astype(BF16)

        @pl.when(pl.program_id(0) == 0)
        def _():
            dgq_ref[...] = jnp.zeros_like(dgq_ref)
            dgk_ref[...] = jnp.zeros_like(dgk_ref)

        dgq_ref[...] += dgq
        dgk_ref[...] += dgk

    def row(w):
        return pl.BlockSpec((tr, w), lambda i: (i, 0))

    def vec(w):
        return pl.BlockSpec((1, w), lambda i: (0, 0))

    return pl.pallas_call(
        body, name=name, grid=(t // tr,),
        in_specs=[row(n), vec(ql), vec(kl), row(ql), row(kl), row(ROPE), row(HALF), row(HALF)],
        out_specs=[row(n), vec(ql), vec(kl)], out_shape=[S((t, n), BF16), S((1, ql), F32), S((1, kl), F32)],
        compiler_params=_params(1))(proj, g_cq, g_ckv, dcq, dckv, dkr, cos, sin)


def qkv_heads(name, q, kv, kr, cos, sin):
    t = q.shape[0]
    tr = _tile(t, 256)

    def body(q_ref, kv_ref, kr_ref, c_ref, s_ref, qo_ref, ko_ref, vo_ref):
        c = c_ref[...]
        s = s_ref[...]
        krb = kr_ref[...].astype(BF16)
        for h in range(N_HEADS):
            q0 = h * QK
            qo_ref[h, :, 0:NOPE] = q_ref[:, q0:q0 + NOPE].astype(BF16)
            q1 = q_ref[:, q0 + NOPE:q0 + NOPE + HALF]
            q2 = q_ref[:, q0 + NOPE + HALF:q0 + QK]
            qo_ref[h, :, NOPE:NOPE + HALF] = (q1 * c - q2 * s).astype(BF16)
            qo_ref[h, :, NOPE + HALF:QK] = (q1 * s + q2 * c).astype(BF16)
            k0 = h * (NOPE + VDIM)
            ko_ref[h, :, 0:NOPE] = kv_ref[:, k0:k0 + NOPE]
            ko_ref[h, :, NOPE:QK] = krb
            vo_ref[h] = kv_ref[:, k0 + NOPE:k0 + NOPE + VDIM]

    def row(w):
        return pl.BlockSpec((tr, w), lambda i: (i, 0))

    def heads(w):
        return pl.BlockSpec((N_HEADS, tr, w), lambda i: (0, i, 0))

    return pl.pallas_call(
        body, name=name, grid=(t // tr,),
        in_specs=[row(N_HEADS * QK), row(N_HEADS * (NOPE + VDIM)), row(ROPE), row(HALF), row(HALF)],
        out_specs=[heads(QK), heads(QK), heads(VDIM)],
        out_shape=[S((N_HEADS, t, QK), BF16), S((N_HEADS, t, QK), BF16), S((N_HEADS, t, VDIM), BF16)],
        compiler_params=_params(1))(q, kv, kr, cos, sin)


def qkv_heads_bwd(name, dq_h, dk_h, dv_h, cos, sin):
    t = dq_h.shape[1]
    tr = _tile(t, 256)

    def body(dq_ref, dk_ref, dv_ref, c_ref, s_ref, q_ref, kv_ref, kr_ref):
        c = c_ref[...]
        s = s_ref[...]
        dkr = jnp.zeros((tr, ROPE), F32)
        for h in range(N_HEADS):
            q0 = h * QK
            q_ref[:, q0:q0 + NOPE] = dq_ref[h, :, 0:NOPE].astype(BF16)
            d1 = dq_ref[h, :, NOPE:NOPE + HALF]
            d2 = dq_ref[h, :, NOPE + HALF:QK]
            q_ref[:, q0 + NOPE:q0 + NOPE + HALF] = (d1 * c + d2 * s).astype(BF16)
            q_ref[:, q0 + NOPE + HALF:q0 + QK] = (d2 * c - d1 * s).astype(BF16)
            k0 = h * (NOPE + VDIM)
            kv_ref[:, k0:k0 + NOPE] = dk_ref[h, :, 0:NOPE].astype(BF16)
            kv_ref[:, k0 + NOPE:k0 + NOPE + VDIM] = dv_ref[h].astype(BF16)
            dkr = dkr + dk_ref[h, :, NOPE:QK]
        kr_ref[...] = dkr

    def row(w):
        return pl.BlockSpec((tr, w), lambda i: (i, 0))

    def heads(w):
        return pl.BlockSpec((N_HEADS, tr, w), lambda i: (0, i, 0))

    return pl.pallas_call(
        body, name=name, grid=(t // tr,),
        in_specs=[heads(QK), heads(QK), heads(VDIM), row(HALF), row(HALF)],
        out_specs=[row(N_HEADS * QK), row(N_HEADS * (NOPE + VDIM)), row(ROPE)],
        out_shape=[S((t, N_HEADS * QK), BF16), S((t, N_HEADS * (NOPE + VDIM)), BF16), S((t, ROPE), F32)],
        compiler_params=_params(1))(dq_h, dk_h, dv_h, cos, sin)


def _chunk_mask_t(q_start, k_start, bq, bk):
    kc = (k_start + lax.broadcasted_iota(jnp.int32, (bk, bq), 0)) // CHUNK
    qc = (q_start + lax.broadcasted_iota(jnp.int32, (bk, bq), 1)) // CHUNK
    return kc <= qc


def attention_fwd(name, q, k, v, shards=()):
    nh, t, _ = q.shape
    blk = ATT_BLOCK
    nq = t // blk
    n = len(shards)

    def body(q_ref, k_ref, v_ref, *refs):
        src = refs[:n]
        o_ref, lse_ref = refs[n:n + 2]
        dst = refs[n + 2:2 * n + 2]
        m_ref, l_ref, acc_ref, s_buf, p_buf, alpha_buf, bias_ref = refs[2 * n + 2:2 * n + 9]
        i = pl.program_id(1)
        if n:
            send_sems, recv_sems = refs[2 * n + 9:]
            _ride_along(gather_ici_copies(src, dst, send_sems, recv_sems), (pl.program_id(0), i), (nh, nq))

        @pl.when((pl.program_id(0) == 0) & (i == 0))
        def _():
            bias_ref[...] = jnp.where(_chunk_mask_t(0, 0, blk, blk), 0.0, MASK_VALUE)

        m_ref[...] = jnp.full_like(m_ref, MASK_VALUE)
        l_ref[...] = jnp.zeros_like(l_ref)
        acc_ref[...] = jnp.zeros_like(acc_ref)

        def rows(b):
            return pl.ds(pl.multiple_of(b * blk, blk), blk)

        def scores(b, slot):
            s_buf[slot] = lax.dot_general(k_ref[rows(b), :], q_ref[...], _NT, preferred_element_type=F32)

        def softmax(slot, diagonal):
            s = s_buf[slot]
            if diagonal:
                s = s + bias_ref[...]
            m_old = m_ref[...]
            m_new = jnp.maximum(m_old, jnp.max(s, axis=0, keepdims=True))
            p = jnp.exp2((s - m_new) * SCORE_SCALE_LOG2)
            alpha = jnp.exp2((m_old - m_new) * SCORE_SCALE_LOG2)
            l_ref[...] = alpha * l_ref[...] + jnp.sum(p, axis=0, keepdims=True)
            m_ref[...] = m_new
            alpha_buf[slot] = alpha
            p_buf[slot] = p.astype(BF16)

        def values(b, slot):
            pv = lax.dot_general(v_ref[rows(b), :], p_buf[slot], _TN, preferred_element_type=F32)
            acc_ref[...] = alpha_buf[slot] * acc_ref[...] + pv

        def step(t, slot):
            values(t - 2, slot)
            softmax(1 - slot, False)
            scores(t, slot)

        scores(0, 0)

        @pl.when(i == 0)
        def _():
            softmax(0, True)
            values(0, 0)

        @pl.when(i > 0)
        def _():
            scores(1, 1)
            softmax(0, False)
            steady = i - 1

            def pair(u, carry):
                step(2 + 2 * u, 0)
                step(3 + 2 * u, 1)
                return carry

            lax.fori_loop(0, steady // 2, pair, 0)

            @pl.when(steady % 2 == 1)
            def _():
                step(i, 0)

            last = i % 2
            softmax(last, True)
            values(i - 1, 1 - last)
            values(i, last)

        l = l_ref[...]
        o_ref[...] = (acc_ref[...] / l).T
        lse_ref[...] = m_ref[...] * SCORE_SCALE + jnp.log(l)

    outs = pl.pallas_call(
        body, name=name, grid=(nh, nq),
        in_specs=[pl.BlockSpec((None, blk, QK), lambda h, i: (h, i, 0)), pl.BlockSpec((None, t, QK), lambda h, i: (h, 0, 0)),
                  pl.BlockSpec((None, t, VDIM), lambda h, i: (h, 0, 0))] + [ANY] * n,
        out_specs=[pl.BlockSpec((blk, VDIM), lambda h, i: (i, h)),
                   pl.BlockSpec((None, None, 1, blk), lambda h, i: (h, i, 0, 0))] + [ANY] * n,
        out_shape=[S((t, nh * VDIM), F32), S((nh, nq, 1, blk), F32)] + [S((N_CHIPS,) + s.shape, s.dtype) for s in shards],
        scratch_shapes=[pltpu.VMEM((1, blk), F32), pltpu.VMEM((1, blk), F32), pltpu.VMEM((VDIM, blk), F32),
                        pltpu.VMEM((2, blk, blk), F32), pltpu.VMEM((2, blk, blk), BF16), pltpu.VMEM((2, 1, blk), F32),
                        pltpu.VMEM((blk, blk), F32)]
        + ([pltpu.SemaphoreType.DMA((n, 3)), pltpu.SemaphoreType.DMA((n, 3))] if n else []),
        compiler_params=_params(2))(q, k, v, *shards)
    return outs[0], outs[1], list(outs[2:])


def attention_delta(name, do, o):
    t = do.shape[0]
    blk = ATT_BLOCK

    def body(do_ref, o_ref, d_ref):
        for h in range(N_HEADS):
            cols = slice(h * VDIM, (h + 1) * VDIM)
            d_ref[h] = jnp.sum((do_ref[:, cols] * o_ref[:, cols]).T, axis=0, keepdims=True)

    tile = pl.BlockSpec((blk, N_HEADS * VDIM), lambda i: (i, 0))
    return pl.pallas_call(
        body, name=name, grid=(t // blk,), in_specs=[tile, tile],
        out_specs=pl.BlockSpec((N_HEADS, None, 1, blk), lambda i: (0, i, 0, 0)), out_shape=S((N_HEADS, t // blk, 1, blk), F32),
        compiler_params=_params(1))(do, o)


def attention_bwd(name, q, k, v, do, lse, delta, parts=()):
    nh, t, _ = q.shape
    blk = ATT_BLOCK
    nq = t // blk
    n = len(parts)
    scale = 1.0 / math.sqrt(QK)

    def body(q_ref, k_ref, v_ref, do_ref, lse_ref, dl_ref, *refs):
        src = refs[:n]
        dq_ref, dk_ref, dv_ref = refs[n:n + 3]
        dst = refs[n + 3:2 * n + 3]
        j = pl.program_id(1)
        if n:
            send_sems, recv_sems = refs[2 * n + 3:]
            _ride_along(scatter_ici_copies(src, dst, send_sems, recv_sems), (pl.program_id(0), j), (nh, nq))

        @pl.when(j == 0)
        def _():
            dq_ref[...] = jnp.zeros_like(dq_ref)

        kb = k_ref[...]
        vb = v_ref[...]
        dk_ref[...] = jnp.zeros_like(dk_ref)
        dv_ref[...] = jnp.zeros_like(dv_ref)

        def block(i, masked):
            rows = pl.ds(pl.multiple_of(i * blk, blk), blk)
            qb = q_ref[rows, :]
            dob = do_ref[rows, :].astype(BF16)
            s = lax.dot_general(kb, qb, _NT, preferred_element_type=F32)
            if masked:
                s = jnp.where(_chunk_mask_t(i * blk, j * blk, blk, blk), s, MASK_VALUE)
            p = jnp.exp2(s * SCORE_SCALE_LOG2 - lse_ref[i] * LOG2_E)
            dp = lax.dot_general(vb, dob, _NT, preferred_element_type=F32)
            ds = (p * (dp - dl_ref[i]) * scale).astype(BF16)
            dv_ref[...] += jnp.dot(p.astype(BF16), dob, preferred_element_type=F32)
            dk_ref[...] += jnp.dot(ds, qb, preferred_element_type=F32)
            dq_ref[rows, :] += lax.dot_general(ds, kb, _TN, preferred_element_type=F32)

        block(j, True)

        def step(i, carry):
            block(i, False)
            return carry

        lax.fori_loop(j + 1, nq, step, 0)

    head_all = lambda w: pl.BlockSpec((None, t, w), lambda h, j: (h, 0, 0))
    head_blk = lambda w: pl.BlockSpec((None, blk, w), lambda h, j: (h, j, 0))
    stats = pl.BlockSpec((None, nq, 1, blk), lambda h, j: (h, 0, 0, 0))
    outs = pl.pallas_call(
        body, name=name, grid=(nh, nq),
        in_specs=[head_all(QK), head_blk(QK), head_blk(VDIM), pl.BlockSpec((t, VDIM), lambda h, j: (0, h)), stats, stats] + [ANY] * n,
        out_specs=[head_all(QK), head_blk(QK), head_blk(VDIM)] + [ANY] * n,
        out_shape=[S((nh, t, QK), F32), S((nh, t, QK), F32), S((nh, t, VDIM), F32)] + [S(p.shape, p.dtype) for p in parts],
        scratch_shapes=[pltpu.SemaphoreType.DMA((n, 3)), pltpu.SemaphoreType.DMA((n, 3))] if n else [],
        compiler_params=_params(2))(q, k, v, do, lse, delta, *parts)
    return outs[0], outs[1], outs[2], list(outs[3:])


def _shift_down(u, s):
    rows = lax.broadcasted_iota(jnp.int32, u.shape, 0)
    return jnp.where(rows >= s, pltpu.roll(u, s, 0), 0.0)


def _shift_up(u, s):
    n = u.shape[0]
    rows = lax.broadcasted_iota(jnp.int32, u.shape, 0)
    return jnp.where(rows < n - s, pltpu.roll(u, n - s, 0), 0.0)


def _conv_specs(t, d, lanes):
    slab = lambda part: pl.BlockSpec((None, t, lanes), lambda j, part=part: (part, 0, j))
    return slab, pl.BlockSpec((3, lanes), lambda j: (0, j)), pl.BlockSpec((t, lanes), lambda j: (0, j))


def conv_fwd(name, bcx, w):
    _, t, d = bcx.shape
    lanes = _tile(d, 128, 128)
    slab, w_spec, col = _conv_specs(t, d, lanes)

    def body(b_ref, c_ref, x_ref, w_ref, y_ref):
        u = c_ref[...] * x_ref[...]
        uc = w_ref[0:1, :] * _shift_down(u, 2) + w_ref[1:2, :] * _shift_down(u, 1) + w_ref[2:3, :] * u
        y_ref[...] = (b_ref[...] * uc).astype(BF16)

    return pl.pallas_call(
        body, name=name, grid=(d // lanes,), in_specs=[slab(0), slab(1), slab(2), w_spec], out_specs=col,
        out_shape=S((t, d), BF16), compiler_params=_params(1))(bcx, bcx, bcx, w)


def conv_bwd(name, bcx, w, dy):
    _, t, d = bcx.shape
    lanes = _tile(d, 128, 128)
    slab, w_spec, col = _conv_specs(t, d, lanes)

    def body(b_ref, c_ref, x_ref, w_ref, dy_ref, d_ref, dw_ref):
        c = c_ref[...]
        x = x_ref[...]
        dyv = dy_ref[...]
        u = c * x
        u1 = _shift_down(u, 1)
        u2 = _shift_down(u, 2)
        w0, w1, w2 = w_ref[0:1, :], w_ref[1:2, :], w_ref[2:3, :]
        d_ref[0] = (dyv * (w0 * u2 + w1 * u1 + w2 * u)).astype(BF16)
        duc = dyv * b_ref[...]
        dw_ref[0:1, :] = jnp.sum(duc * u2, axis=0, keepdims=True)
        dw_ref[1:2, :] = jnp.sum(duc * u1, axis=0, keepdims=True)
        dw_ref[2:3, :] = jnp.sum(duc * u, axis=0, keepdims=True)
        du = w2 * duc + w1 * _shift_up(duc, 1) + w0 * _shift_up(duc, 2)
        d_ref[1] = (du * x).astype(BF16)
        d_ref[2] = (du * c).astype(BF16)

    return pl.pallas_call(
        body, name=name, grid=(d // lanes,), in_specs=[slab(0), slab(1), slab(2), w_spec, col],
        out_specs=[pl.BlockSpec((3, t, lanes), lambda j: (0, 0, j)), w_spec], out_shape=[S((3, t, d), BF16), S((3, d), F32)],
        compiler_params=_params(1))(bcx, bcx, bcx, w, dy)


def adamw(name, w, g, m, v):
    r, c = w.shape
    tr = _tile(r, 512)

    def body(w_ref, g_ref, m_ref, v_ref, d_ref, mo_ref, vo_ref):
        gv = g_ref[...]
        m_new = ADAM_B1 * m_ref[...] + (1.0 - ADAM_B1) * gv
        v_new = ADAM_B2 * v_ref[...] + (1.0 - ADAM_B2) * (gv * gv)
        m_hat = m_new / (1.0 - ADAM_B1 ** ADAM_STEP)
        v_hat = v_new / (1.0 - ADAM_B2 ** ADAM_STEP)
        d_ref[...] = -ADAM_LR * (m_hat / (jnp.sqrt(v_hat) + ADAM_EPS) + ADAM_WD * w_ref[...])
        mo_ref[...] = m_new
        vo_ref[...] = v_new

    blk = pl.BlockSpec((tr, c), lambda i: (i, 0))
    return pl.pallas_call(
        body, name=name, grid=(r // tr,), in_specs=[blk] * 4, out_specs=[blk] * 3, out_shape=[S((r, c), F32)] * 3,
        compiler_params=_params(1))(w, g, m, v)


def _place():
    x, y, c = lax.axis_index("x"), lax.axis_index("y"), lax.axis_index("c")
    other_chips = [(1 - x, y), (x, 1 - y), (1 - x, 1 - y)]
    return x, y, c, other_chips


def _half(c, rows):
    return pl.ds(pl.multiple_of(c * (rows // 2), 16), rows // 2)


def gather_weight_shards(shards):
    n = len(shards)

    def body(*refs):
        src = refs[:n]
        dst = refs[n:2 * n]
        send_sems, recv_sems = refs[2 * n:]
        x, y, c, chips = _place()
        me = 2 * x + y
        sibling = (x, y, 1 - c)

        def copy(i, slot, half_of, sem, to, from_input=False):
            rows = _half(half_of, src[i].shape[0])
            return pltpu.make_async_remote_copy(
                src_ref=src[i].at[rows] if from_input else dst[i].at[slot, rows], dst_ref=dst[i].at[slot, rows],
                send_sem=send_sems.at[i, sem], recv_sem=recv_sems.at[i, sem], device_id=to, device_id_type=MESH)

        sent = []
        for i in range(n):
            for j, chip in enumerate(chips):
                sent.append(copy(i, me, c, j, (*chip, c), from_input=True))
                sent[-1].start()
        for i in range(n):
            for j, (px, py) in enumerate(chips):
                copy(i, 2 * px + py, c, j, sibling).wait_recv()
                sent.append(copy(i, 2 * px + py, c, 3 + j, sibling))
                sent[-1].start()
        for i in range(n):
            for j, (px, py) in enumerate(chips):
                copy(i, 2 * px + py, 1 - c, 3 + j, sibling).wait_recv()
        for cp in sent:
            cp.wait_send()

    outs = pl.pallas_call(
        body, name="gather_weight_shards", in_specs=[ANY] * n, out_specs=[ANY] * n,
        out_shape=[S((N_CHIPS,) + s.shape, s.dtype) for s in shards],
        scratch_shapes=[pltpu.SemaphoreType.DMA((n, 6)), pltpu.SemaphoreType.DMA((n, 6))],
    )(*shards)
    return _fill_own_slot(outs, [s[None] for s in shards])


def gather_ici_copies(src, dst, send_sems, recv_sems):
    x, y, c, chips = _place()
    me = 2 * x + y
    pairs = []
    for i in range(len(src)):
        rows = _half(c, src[i].shape[0])
        for j, (px, py) in enumerate(chips):
            def copy(slot):
                return pltpu.make_async_remote_copy(
                    src_ref=src[i].at[rows], dst_ref=dst[i].at[slot, rows], send_sem=send_sems.at[i, j],
                    recv_sem=recv_sems.at[i, j], device_id=(px, py, c), device_id_type=MESH)
            pairs.append((copy(me), copy(2 * px + py)))
    return pairs


def scatter_ici_copies(src, dst, send_sems, recv_sems):
    x, y, c, chips = _place()
    me = 2 * x + y
    pairs = []
    for i in range(len(src)):
        for j, (px, py) in enumerate(chips):
            def copy(from_slot, to_slot):
                return pltpu.make_async_remote_copy(
                    src_ref=src[i].at[from_slot], dst_ref=dst[i].at[to_slot], send_sem=send_sems.at[i, j],
                    recv_sem=recv_sems.at[i, j], device_id=(px, py, c), device_id_type=MESH)
            pairs.append((copy(2 * px + py, me), copy(me, 2 * px + py)))
    return pairs


def _ride_along(pairs, grid_ids, grid_sizes):
    first = (grid_ids[0] == 0) & (grid_ids[1] == 0)
    last = (grid_ids[0] == grid_sizes[0] - 1) & (grid_ids[1] == grid_sizes[1] - 1)

    @pl.when(first)
    def _():
        for outgoing, _ in pairs:
            outgoing.start()

    @pl.when(last)
    def _():
        for _, incoming in pairs:
            incoming.wait_recv()
        for outgoing, _ in pairs:
            outgoing.wait_send()


def _fill_own_slot(gathered, own):
    me = 2 * lax.axis_index("x") + lax.axis_index("y")
    return [lax.dynamic_update_slice(g, o, (me,) + (0,) * (g.ndim - 1)) for g, o in zip(gathered, own)]


def forward_to_sibling(gathered):
    n = len(gathered)

    def body(*refs):
        src = refs[:n]
        dst = refs[n:2 * n]
        send_sems, recv_sems = refs[2 * n:]
        x, y, c, chips = _place()
        pairs = []
        for i in range(n):
            for j, (px, py) in enumerate(chips):
                def copy(half_of):
                    rows = _half(half_of, src[i].shape[1])
                    return pltpu.make_async_remote_copy(
                        src_ref=src[i].at[2 * px + py, rows], dst_ref=dst[i].at[2 * px + py, rows], send_sem=send_sems.at[i, j],
                        recv_sem=recv_sems.at[i, j], device_id=(x, y, 1 - c), device_id_type=MESH)
                pairs.append((copy(c), copy(1 - c)))
        for outgoing, _ in pairs:
            outgoing.start()
        for _, incoming in pairs:
            incoming.wait_recv()
        for outgoing, _ in pairs:
            outgoing.wait_send()

    return pl.pallas_call(
        body, name="forward_to_sibling", in_specs=[ANY] * n, out_specs=[ANY] * n,
        out_shape=[S(g.shape, g.dtype) for g in gathered], input_output_aliases={i: i for i in range(n)},
        scratch_shapes=[pltpu.SemaphoreType.DMA((n, 3)), pltpu.SemaphoreType.DMA((n, 3))],
    )(*gathered)


def sibling_swap_halves(name, grads):
    n = len(grads)

    def body(*refs):
        src = refs[:n]
        dst = refs[n:2 * n]
        send_sems, recv_sems = refs[2 * n:]
        x, y, c, _ = _place()
        copies = [pltpu.make_async_remote_copy(
            src_ref=src[i].at[:, _half(1 - c, src[i].shape[1]), :], dst_ref=dst[i], send_sem=send_sems.at[i],
            recv_sem=recv_sems.at[i], device_id=(x, y, 1 - c), device_id_type=MESH) for i in range(n)]
        for cp in copies:
            cp.start()
        for cp in copies:
            cp.wait()

    return pl.pallas_call(
        body, name=name, in_specs=[ANY] * n, out_specs=[ANY] * n,
        out_shape=[S((g.shape[0], g.shape[1] // 2, g.shape[2]), g.dtype) for g in grads],
        scratch_shapes=[pltpu.SemaphoreType.DMA((n,)), pltpu.SemaphoreType.DMA((n,))],
    )(*grads)


def add_halves(name, g, rx):
    _, r, cdim = g.shape
    r2 = r // 2
    tr = _tile(r2, 512, 16)
    nb = r2 // tr

    def body(lo_ref, hi_ref, rx_ref, o_ref):
        mine = jnp.where(lax.axis_index("c") == 0, lo_ref[...], hi_ref[...])
        o_ref[...] = (mine.astype(F32) + rx_ref[...].astype(F32)).astype(BF16)

    half = pl.BlockSpec((None, tr, cdim), lambda k, i: (k, i, 0))
    return pl.pallas_call(
        body, name=name, grid=(N_CHIPS, nb),
        in_specs=[half, pl.BlockSpec((None, tr, cdim), lambda k, i: (k, nb + i, 0)), half],
        out_specs=half, out_shape=S((N_CHIPS, r2, cdim), BF16), compiler_params=_params(2))(g, g, rx)


def scatter_to_owner_chips(parts):
    n = len(parts)

    def body(*refs):
        src = refs[:n]
        dst = refs[n:2 * n]
        send_sems, recv_sems = refs[2 * n:]
        pairs = scatter_ici_copies(src, dst, send_sems, recv_sems)
        for outgoing, _ in pairs:
            outgoing.start()
        for _, incoming in pairs:
            incoming.wait_recv()
        for outgoing, _ in pairs:
            outgoing.wait_send()

    return pl.pallas_call(
        body, name="scatter_to_owner_chips", in_specs=[ANY] * n, out_specs=[ANY] * n,
        out_shape=[S(p.shape, p.dtype) for p in parts],
        scratch_shapes=[pltpu.SemaphoreType.DMA((n, 3)), pltpu.SemaphoreType.DMA((n, 3))],
    )(*parts)


def _own_slots(parts):
    me = 2 * lax.axis_index("x") + lax.axis_index("y")
    return [lax.dynamic_slice(p, (me, 0, 0), (1,) + p.shape[1:]) for p in parts]


def sum_chips(name, parts):
    _, r2, cdim = parts.shape
    tr = _tile(r2, 512, 16)

    def body(p_ref, o_ref):
        acc = p_ref[0].astype(F32)
        for k in range(1, N_CHIPS):
            acc = acc + p_ref[k].astype(F32)
        o_ref[...] = acc

    return pl.pallas_call(
        body, name=name, grid=(r2 // tr,), in_specs=[pl.BlockSpec((N_CHIPS, tr, cdim), lambda i: (0, i, 0))],
        out_specs=pl.BlockSpec((tr, cdim), lambda i: (i, 0)), out_shape=S((r2, cdim), F32), compiler_params=_params(1))(parts)


def sibling_join_halves(name, halves, targets, where):
    n = len(halves)

    def rows_of(i, half_of):
        r2 = halves[i].shape[0]
        return pl.ds(pl.multiple_of(where[i][1] + half_of * r2, 8), r2)

    def body(*refs):
        src = refs[:n]
        dst = refs[n:n + len(targets)]
        send_sems, recv_sems = refs[n + len(targets):]
        x, y, c, _ = _place()

        def copy(i, half_of):
            return pltpu.make_async_remote_copy(
                src_ref=src[i], dst_ref=dst[where[i][0]].at[rows_of(i, half_of)], send_sem=send_sems.at[i],
                recv_sem=recv_sems.at[i], device_id=(x, y, 1 - c), device_id_type=MESH)

        for i in range(n):
            copy(i, c).start()
        for i in range(n):
            copy(i, 1 - c).wait_recv()
        for i in range(n):
            copy(i, c).wait_send()

    outs = list(pl.pallas_call(
        body, name=name, in_specs=[ANY] * n, out_specs=[ANY] * len(targets), out_shape=[S(tg, F32) for tg in targets],
        scratch_shapes=[pltpu.SemaphoreType.DMA((n,)), pltpu.SemaphoreType.DMA((n,))],
    )(*halves))
    c = lax.axis_index("c")
    for i, h in enumerate(halves):
        tgt, first = where[i]
        outs[tgt] = lax.dynamic_update_slice(outs[tgt], h, (first + c * h.shape[0], 0))
    return outs


def all_reduce_small(name, packed):
    rows, width = packed.shape

    def body(x_ref, o_ref, gathered, send_sems, recv_sems):
        x, y, c, _ = _place()
        me = 4 * x + 2 * y + c
        gathered[me] = x_ref[...]
        flips = [(fx, fy, fc) for fx in (0, 1) for fy in (0, 1) for fc in (0, 1)][1:]

        def copy(r, slot, to):
            return pltpu.make_async_remote_copy(
                src_ref=x_ref, dst_ref=gathered.at[slot], send_sem=send_sems.at[r], recv_sem=recv_sems.at[r],
                device_id=to, device_id_type=MESH)

        def peer(f):
            return (x ^ f[0], y ^ f[1], c ^ f[2])

        sent = [copy(r, me, peer(f)) for r, f in enumerate(flips)]
        for cp in sent:
            cp.start()
        for r, f in enumerate(flips):
            px, py, pc = peer(f)
            copy(r, 4 * px + 2 * py + pc, peer(f)).wait_recv()
        for cp in sent:
            cp.wait_send()
        acc = gathered[0]
        for k in range(1, N_DEV):
            acc = acc + gathered[k]
        o_ref[...] = acc

    vmem = pl.BlockSpec(memory_space=pltpu.VMEM)
    return pl.pallas_call(
        body, name=name, in_specs=[vmem], out_specs=vmem, out_shape=S((rows, width), F32),
        scratch_shapes=[pltpu.VMEM((N_DEV, rows, width), F32), pltpu.SemaphoreType.DMA((N_DEV - 1,)),
                        pltpu.SemaphoreType.DMA((N_DEV - 1,))],
    )(packed)


def _rope_tables(positions):
    inv_freq = 1.0 / (ROPE_THETA ** (jnp.arange(0, ROPE, 2, dtype=F32) / ROPE))
    ang = positions.astype(F32)[:, None] * inv_freq
    return jnp.cos(ang), jnp.sin(ang)


def _unstack_cols(w):
    k4, k, n4 = w.shape
    return jnp.transpose(w, (1, 0, 2)).reshape(k, k4 * n4)


def _stack_cols(w):
    k, n = w.shape
    return jnp.transpose(w.reshape(k, N_CHIPS, n // N_CHIPS), (1, 0, 2))


def kernel(x, positions, mla_norm, mla_w_in, mla_g_cq, mla_g_ckv, mla_w_uq, mla_w_ukv, mla_w_o, conv_norm, conv_w_in, conv_w, conv_w_out, ffn_norm, ffn_w_gate, ffn_w_up, ffn_w_down, final_norm, loss_target, m_mla_norm, m_mla_w_in, m_mla_g_cq, m_mla_g_ckv, m_mla_w_uq, m_mla_w_ukv, m_mla_w_o, m_conv_norm, m_conv_w_in, m_conv_w, m_conv_w_out, m_ffn_norm, m_ffn_w_gate, m_ffn_w_up, m_ffn_w_down, m_final_norm, v_mla_norm, v_mla_w_in, v_mla_g_cq, v_mla_g_ckv, v_mla_w_uq, v_mla_w_ukv, v_mla_w_o, v_conv_norm, v_conv_w_in, v_conv_w, v_conv_w_out, v_ffn_norm, v_ffn_w_gate, v_ffn_w_up, v_ffn_w_down, v_final_norm):
    weights = dict(mla_norm=mla_norm, mla_w_in=mla_w_in, mla_g_cq=mla_g_cq, mla_g_ckv=mla_g_ckv, mla_w_uq=mla_w_uq,
                   mla_w_ukv=mla_w_ukv, mla_w_o=mla_w_o, conv_norm=conv_norm, conv_w_in=conv_w_in, conv_w=conv_w,
                   conv_w_out=conv_w_out, ffn_norm=ffn_norm, ffn_w_gate=ffn_w_gate, ffn_w_up=ffn_w_up,
                   ffn_w_down=ffn_w_down, final_norm=final_norm)
    m_in = dict(mla_norm=m_mla_norm, mla_w_in=m_mla_w_in, mla_g_cq=m_mla_g_cq, mla_g_ckv=m_mla_g_ckv, mla_w_uq=m_mla_w_uq,
                mla_w_ukv=m_mla_w_ukv, mla_w_o=m_mla_w_o, conv_norm=m_conv_norm, conv_w_in=m_conv_w_in, conv_w=m_conv_w,
                conv_w_out=m_conv_w_out, ffn_norm=m_ffn_norm, ffn_w_gate=m_ffn_w_gate, ffn_w_up=m_ffn_w_up,
                ffn_w_down=m_ffn_w_down, final_norm=m_final_norm)
    v_in = dict(mla_norm=v_mla_norm, mla_w_in=v_mla_w_in, mla_g_cq=v_mla_g_cq, mla_g_ckv=v_mla_g_ckv, mla_w_uq=v_mla_w_uq,
                mla_w_ukv=v_mla_w_ukv, mla_w_o=v_mla_w_o, conv_norm=v_conv_norm, conv_w_in=v_conv_w_in, conv_w=v_conv_w,
                conv_w_out=v_conv_w_out, ffn_norm=v_ffn_norm, ffn_w_gate=v_ffn_w_gate, ffn_w_up=v_ffn_w_up,
                ffn_w_down=v_ffn_w_down, final_norm=v_final_norm)
    big = ["mla_w_in", "mla_w_uq", "mla_w_ukv", "mla_w_o", "conv_w_in", "conv_w_out", "ffn_w_gate", "ffn_w_up", "ffn_w_down"]
    order = list(weights)

    t, d = x.shape[1], x.shape[2]
    h0 = x.reshape(t, d)
    target = loss_target.reshape(t, d)
    cos, sin = _rope_tables(positions.reshape(t))

    def rows2d(a):
        return a.reshape(-1, a.shape[-1])

    first, later = big[:4], big[4:]
    shards = {n: rows2d(weights[n]).astype(BF16) for n in big}
    gathered = dict(zip(first, gather_weight_shards([shards[n] for n in first])))
    w_in = gathered["mla_w_in"].reshape(-1, gathered["mla_w_in"].shape[-1])
    w_uq = _unstack_cols(gathered["mla_w_uq"])
    w_ukv = _unstack_cols(gathered["mla_w_ukv"])
    w_o = gathered["mla_w_o"].reshape(-1, d)

    chip = 2 * lax.axis_index("x") + lax.axis_index("y")
    core = lax.axis_index("c")
    d4 = d // N_CHIPS
    first_core = (core == 0).astype(F32)

    def place_shard(shard):
        full = jnp.zeros((shard.shape[0], d), F32)
        return lax.dynamic_update_slice(full, shard * first_core, (0, chip * d4))

    def pack_rows(rows):
        idx = lax.broadcasted_iota(jnp.int32, (SMALL_ROWS, d), 0)
        out = jnp.zeros((SMALL_ROWS, d), F32)
        for r, row in enumerate(rows):
            out = out + jnp.where(idx == r, row, 0.0)
        return out

    cw = place_shard(conv_w.reshape(3, d4))
    pre = all_reduce_small("all_gather_conv_small", pack_rows([place_shard(conv_norm.reshape(1, d4)), cw[0:1], cw[1:2], cw[2:3]]))
    conv_norm_full = pre[0:1]
    conv_w_full = pre[1:4]

    a0 = rms_fwd("mla_norm_fwd", h0, mla_norm)
    proj = linear("mla_in_proj", a0, w_in, F32)
    cq, ckv, kr = mla_mid("mla_mid", proj, mla_g_cq, mla_g_ckv, cos, sin)
    q = linear("mla_q_up", cq, w_uq, F32)
    kv = linear("mla_kv_up", ckv, w_ukv, BF16)
    qh, kh, vh = qkv_heads("qkv_heads", q, kv, kr, cos, sin)
    attn, lse, arriving = attention_fwd("attention_fwd", qh, kh, vh, [shards[n] for n in later])
    gathered.update(zip(later, _fill_own_slot(forward_to_sibling(arriving), [shards[n][None] for n in later])))
    cw_in = _unstack_cols(gathered["conv_w_in"])
    cw_out = gathered["conv_w_out"].reshape(-1, d)
    wg_all, wu_all, wd_all = gathered["ffn_w_gate"], gathered["ffn_w_up"], gathered["ffn_w_down"]
    h1, a1 = linear("mla_out_proj", attn, w_o, F32, resid=h0, next_gain=ffn_norm[0:1])

    def ffn_forward(tag, h, a, layer, next_gain):
        g, u, z = ffn_up(f"ffn{tag}_up", a, wg_all, wu_all, layer)
        return g, u, z, ffn_down(f"ffn{tag}_down", z, wd_all, layer, h, next_gain)

    g0, u0, z0, (h2, a2) = ffn_forward(0, h1, a1, 0, conv_norm_full)
    bcx = conv_in_proj("conv_in_proj", a2, cw_in)
    yc = conv_fwd("conv_fwd", bcx, conv_w_full)
    h3, a3 = linear("conv_out_proj", yc, cw_out, F32, resid=h2, next_gain=ffn_norm[1:2])
    g1, u1, z1, h4 = ffn_forward(1, h3, a3, 1, None)
    dh4, d_final_norm, loss_local = loss_head("loss_head", h4, final_norm.reshape(1, d), target)

    def ffn_backward(tag, dh, h, layer, a, g, u, z):
        dg, du = ffn_bwd_hidden(f"ffn{tag}_bwd_hidden", dh, wd_all, layer, g, u)
        d_wd = ffn_wgrad_down(f"ffn{tag}_wgrad_down", z, dh)
        dh_prev, d_norm = ffn_bwd_input(f"ffn{tag}_bwd_input", dg, du, wg_all, wu_all, layer, h, ffn_norm[layer:layer + 1], dh)
        d_wg, d_wu = ffn_wgrad_up(f"ffn{tag}_wgrad_up", a, dg, du)
        return dh_prev, d_norm, [d_wg, d_wu, d_wd]

    def reduce_to_pair_sums(tag, local):
        from_sibling = sibling_swap_halves(f"sibling_swap_{tag}", local)
        return [add_halves(f"pair_sum_{tag}{i}", g, r) for i, (g, r) in enumerate(zip(local, from_sibling))]

    def reduce_from_chips(tag, pair_sums, arrived, targets, where):
        from_chips = _fill_own_slot(arrived, _own_slots(pair_sums))
        my_halves = [sum_chips(f"chip_sum_{tag}{i}", p) for i, p in enumerate(from_chips)]
        return sibling_join_halves(f"sibling_join_{tag}", my_halves, targets, where)

    dh3, d_ffn_norm1, ffn1_grads = ffn_backward(1, dh4, h3, 1, a3, g1, u1, z1)

    dyc = linear_nt("conv_out_bwd_input", dh3, cw_out, F32)
    d_cw_out = wgrad("conv_out_wgrad", yc, dh3)
    dbcx, d_conv_w = conv_bwd("conv_bwd", bcx, conv_w_full, dyc)
    dh2, d_conv_norm = conv_in_bwd_input("conv_in_bwd_input", dbcx, cw_in, h2, conv_norm_full, dh3)
    d_cw_in = conv_in_wgrad("conv_in_wgrad", a2, dbcx)

    dh1, d_ffn_norm0, ffn0_grads = ffn_backward(0, dh2, h1, 0, a1, g0, u0, z0)

    d_attn = linear_nt("mla_out_bwd_input", dh1, w_o, F32)
    d_w_o = wgrad("mla_out_wgrad", attn, dh1)
    rest_pairs = reduce_to_pair_sums("rest", [_stack_cols(d_cw_in), d_cw_out.reshape(N_CHIPS, -1, d)] + ffn1_grads + ffn0_grads
                                     + [d_w_o.reshape(N_CHIPS, -1, d)])
    delta = attention_delta("attention_delta", d_attn, attn)
    dqh, dkh, dvh, rest_arrived = attention_bwd("attention_bwd", qh, kh, vh, d_attn, lse, delta, rest_pairs)
    dq, dkv, dkr = qkv_heads_bwd("qkv_heads_bwd", dqh, dkh, dvh, cos, sin)
    dcq = linear_nt("mla_q_up_bwd_input", dq, w_uq, F32)
    d_w_uq = wgrad("mla_q_up_wgrad", cq, dq)
    dckv = linear_nt("mla_kv_up_bwd_input", dkv, w_ukv, F32)
    d_w_ukv = wgrad("mla_kv_up_wgrad", ckv, dkv)
    dproj, d_g_cq, d_g_ckv = mla_mid_bwd("mla_mid_bwd", proj, mla_g_cq, mla_g_ckv, dcq, dckv, dkr, cos, sin)
    d_w_in = wgrad("mla_in_wgrad", a0, dproj)
    grad_x, d_mla_norm = linear_nt_norm_bwd("mla_in_bwd_input", dproj, w_in, h0, mla_norm, dh1)

    def shard_shape(n):
        return rows2d(weights[n]).shape

    rd, rf = ffn0_grads[0].shape[1], ffn0_grads[2].shape[1]
    rest_where = [(0, 0), (1, 0), (2, rd), (3, rd), (4, rf), (2, 0), (3, 0), (4, 0), (5, 0)]
    rest_names = later + ["mla_w_o"]
    grads = dict(zip(rest_names, reduce_from_chips("rest", rest_pairs, rest_arrived, [shard_shape(n) for n in rest_names], rest_where)))

    mla_pairs = reduce_to_pair_sums("mla", [d_w_in.reshape(N_CHIPS, -1, d_w_in.shape[-1]), _stack_cols(d_w_uq), _stack_cols(d_w_ukv)])
    grads.update(zip(first[:3], reduce_from_chips("mla", mla_pairs, scatter_to_owner_chips(mla_pairs),
                                                  [shard_shape(n) for n in first[:3]], [(i, 0) for i in range(3)])))

    def pad_row(v):
        return jnp.pad(v, ((0, 0), (0, d - v.shape[1])))

    small = all_reduce_small("all_reduce_small_grads", pack_rows([
        d_mla_norm, pad_row(d_g_cq), pad_row(d_g_ckv), d_ffn_norm0, d_ffn_norm1, d_final_norm, d_conv_norm,
        d_conv_w[0:1], d_conv_w[1:2], d_conv_w[2:3], jnp.broadcast_to(loss_local, (1, d))]))
    loss = small[10, 0]
    grads["mla_norm"] = small[0:1]
    grads["mla_g_cq"] = small[1:2, :mla_g_cq.shape[1]]
    grads["mla_g_ckv"] = small[2:3, :mla_g_ckv.shape[1]]
    grads["ffn_norm"] = small[3:5]
    grads["final_norm"] = small[5:6]
    grads["conv_norm"] = lax.dynamic_slice(small[6:7], (0, chip * d4), (1, d4))
    grads["conv_w"] = lax.dynamic_slice(small[7:10], (0, chip * d4), (3, d4))

    outs_g, outs_d, outs_m, outs_v = [], [], [], []
    for n in order:
        w = weights[n]
        delta_w, new_m, new_v = adamw(f"adamw_{n}", rows2d(w), grads[n].reshape(rows2d(w).shape), rows2d(m_in[n]), rows2d(v_in[n]))
        outs_g.append(grads[n].reshape(w.shape))
        outs_d.append(delta_w.reshape(w.shape))
        outs_m.append(new_m.reshape(w.shape))
        outs_v.append(new_v.reshape(w.shape))
    return (loss, grad_x.reshape(x.shape), *outs_g, *outs_d, *outs_m, *outs_v)
```

```python
import math

import jax
import jax.numpy as jnp
from jax import lax
from jax.experimental import pallas as pl
from jax.experimental.pallas import tpu as pltpu

F32 = jnp.float32
BF16 = jnp.bfloat16
S = jax.ShapeDtypeStruct

N_HEADS = 8
NOPE = 128
ROPE = 64
HALF = ROPE // 2
VDIM = 128
QK = NOPE + ROPE
CHUNK = 64
ROPE_THETA = 10000.0
RMS_EPS = 1e-6
ADAM_LR = 0.001
ADAM_B1 = 0.9
ADAM_B2 = 0.999
ADAM_EPS = 1e-08
ADAM_WD = 0.01
ADAM_STEP = 10

N_CHIPS = 4
N_DEV = 8
MASK_VALUE = -1e30
SCORE_SCALE = 1.0 / math.sqrt(QK)
LOG2_E = math.log2(math.e)
SCORE_SCALE_LOG2 = SCORE_SCALE * LOG2_E
VMEM_LIMIT = 48 * 1024 * 1024
ATT_BLOCK = 512
SMALL_ROWS = 16

_NN = (((1,), (0,)), ((), ()))
_NT = (((1,), (1,)), ((), ()))
_TN = (((0,), (0,)), ((), ()))
MESH = pl.DeviceIdType.MESH
ANY = pl.BlockSpec(memory_space=pl.ANY)


def _params(n_axes):
    return pltpu.CompilerParams(dimension_semantics=("arbitrary",) * n_axes, vmem_limit_bytes=VMEM_LIMIT)


def _tile(n, cap, mult=8):
    for t in range(min(cap, n), 0, -1):
        if n % t == 0 and t % mult == 0:
            return t
    return n


def _sigmoid(x):
    return 1.0 / (1.0 + jnp.exp(-x))


def _mm(name, a_ops, b_ops, products, dims, grid, k_axis, outs, acc_shape, epilogue, extra_ops=()):
    na, nb, ne, no = len(a_ops), len(b_ops), len(extra_ops), len(outs)
    n_acc = 1 + max(c for _, _, c in products)
    nk = 1 if k_axis is None else grid[k_axis]

    def body(*refs):
        a_refs = refs[:na]
        b_refs = refs[na:na + nb]
        e_refs = refs[na + nb:na + nb + ne]
        o_refs = refs[na + nb + ne:na + nb + ne + no]
        acc_refs = refs[na + nb + ne + no:]

        def partial_sums():
            vals = [None] * n_acc
            for ai, bi, ci in products:
                d = lax.dot_general(a_refs[ai][...].astype(BF16), b_refs[bi][...].astype(BF16), dims,
                                    preferred_element_type=F32)
                vals[ci] = d if vals[ci] is None else vals[ci] + d
            return vals

        if nk == 1:
            epilogue(partial_sums(), e_refs, o_refs)
        else:
            k = pl.program_id(k_axis)

            @pl.when(k == 0)
            def _():
                for acc in acc_refs:
                    acc[...] = jnp.zeros_like(acc)

            for acc, v in zip(acc_refs, partial_sums()):
                acc[...] += v

            @pl.when(k == nk - 1)
            def _():
                epilogue([acc[...] for acc in acc_refs], e_refs, o_refs)

    ops = list(a_ops) + list(b_ops) + list(extra_ops)
    return pl.pallas_call(
        body, name=name, grid=grid,
        in_specs=[s for _, s in ops], out_specs=[s for _, s in outs], out_shape=[o for o, _ in outs],
        scratch_shapes=[pltpu.VMEM(acc_shape, F32) for _ in range(n_acc if nk > 1 else 0)],
        compiler_params=_params(len(grid)),
    )(*[a for a, _ in ops])


def _store(accs, e_refs, o_refs):
    o_refs[0][...] = accs[0].astype(o_refs[0].dtype)


def linear(name, x, w, out_dtype, resid=None, next_gain=None):
    t, k = x.shape
    n = w.shape[1]
    tm = _tile(t, 512)
    tn = n if n <= 2048 else _tile(n, 1024, 128)
    tile = pl.BlockSpec((tm, tn), lambda j, i: (i, j))
    extra = [] if resid is None else [(resid, tile)]
    outs = [(S((t, n), out_dtype), tile)]
    if next_gain is not None:
        assert tn == n
        extra.append((next_gain, pl.BlockSpec((1, n), lambda j, i: (0, 0))))
        outs.append((S((t, n), BF16), tile))

    def epilogue(accs, e_refs, o_refs):
        y = accs[0] if resid is None else e_refs[0][...] + accs[0]
        o_refs[0][...] = y.astype(out_dtype)
        if next_gain is not None:
            o_refs[1][...] = (y * _rstd(y) * e_refs[-1][...]).astype(BF16)

    res = _mm(name, [(x, pl.BlockSpec((tm, k), lambda j, i: (i, 0)))], [(w, pl.BlockSpec((k, tn), lambda j, i: (0, j)))],
              [(0, 0, 0)], _NN, (n // tn, t // tm), None, outs, None, epilogue, extra)
    return res[0] if next_gain is None else res


def linear_nt(name, dy, w, out_dtype):
    t, n = dy.shape
    k = w.shape[0]
    tm = _tile(t, 512)
    tc = n if n <= 2048 else _tile(n, 1024, 128)
    return _mm(name, [(dy, pl.BlockSpec((tm, tc), lambda i, c: (i, c)))], [(w, pl.BlockSpec((k, tc), lambda i, c: (0, c)))],
               [(0, 0, 0)], _NT, (t // tm, n // tc), 1,
               [(S((t, k), out_dtype), pl.BlockSpec((tm, k), lambda i, c: (i, 0)))], (tm, k), _store)[0]


def wgrad(name, x, dy):
    t, k = x.shape
    n = dy.shape[1]
    tk = _tile(t, 512)
    tn = n if n <= 1024 else _tile(n, 1024, 128)
    return _mm(name, [(x, pl.BlockSpec((tk, k), lambda j, s: (s, 0)))], [(dy, pl.BlockSpec((tk, tn), lambda j, s: (s, j)))],
               [(0, 0, 0)], _TN, (n // tn, t // tk), 1,
               [(S((k, n), BF16), pl.BlockSpec((k, tn), lambda j, s: (0, j)))], (k, tn), _store)[0]


def _resident(shape, index_map):
    return pl.BlockSpec(shape, index_map, pipeline_mode=pl.Buffered(1))


def ffn_up(name, a, wg_all, wu_all, layer):
    t, d = a.shape
    f4 = wg_all.shape[2]
    tm = _tile(t, 512)
    w_spec = _resident((N_CHIPS, d, f4), lambda i: (0, layer, 0))
    h_spec = pl.BlockSpec((N_CHIPS, tm, f4), lambda i: (0, i, 0))

    def body(a_ref, wg_ref, wu_ref, g_ref, u_ref, z_ref):
        av = a_ref[...]
        for k in range(N_CHIPS):
            g = jnp.dot(av, wg_ref[k], preferred_element_type=F32)
            u = jnp.dot(av, wu_ref[k], preferred_element_type=F32)
            g_ref[k] = g.astype(BF16)
            u_ref[k] = u.astype(BF16)
            z_ref[k] = (g * _sigmoid(g) * u).astype(BF16)

    return pl.pallas_call(
        body, name=name, grid=(t // tm,), in_specs=[pl.BlockSpec((tm, d), lambda i: (i, 0)), w_spec, w_spec],
        out_specs=[h_spec] * 3, out_shape=[S((N_CHIPS, t, f4), BF16)] * 3, compiler_params=_params(1))(a, wg_all, wu_all)


def ffn_down(name, z, wd_all, layer, resid, next_gain=None):
    _, t, f4 = z.shape
    d = wd_all.shape[2]
    tm = _tile(t, 512)
    row = pl.BlockSpec((tm, d), lambda i: (i, 0))
    normed = next_gain is not None

    def body(z_ref, wd_ref, r_ref, *refs):
        acc = r_ref[...]
        for k in range(N_CHIPS):
            acc = acc + jnp.dot(z_ref[k], wd_ref[k], preferred_element_type=F32)
        refs[-2 if normed else -1][...] = acc
        if normed:
            refs[-1][...] = (acc * _rstd(acc) * refs[0][...]).astype(BF16)

    res = pl.pallas_call(
        body, name=name, grid=(t // tm,),
        in_specs=[pl.BlockSpec((N_CHIPS, tm, f4), lambda i: (0, i, 0)), _resident((N_CHIPS, f4, d), lambda i: (0, layer, 0)), row]
        + ([pl.BlockSpec((1, d), lambda i: (0, 0))] if normed else []),
        out_specs=[row] * (2 if normed else 1), out_shape=[S((t, d), F32)] + ([S((t, d), BF16)] if normed else []),
        compiler_params=_params(1))(z, wd_all, resid, *([next_gain] if normed else []))
    return res if normed else res[0]


def ffn_bwd_hidden(name, dh, wd_all, layer, g, u):
    t, d = dh.shape
    f4 = g.shape[2]
    tm = _tile(t, 512)
    h_spec = pl.BlockSpec((N_CHIPS, tm, f4), lambda i: (0, i, 0))

    def body(dh_ref, wd_ref, g_ref, u_ref, dg_ref, du_ref):
        dhb = dh_ref[...].astype(BF16)
        for k in range(N_CHIPS):
            dz = lax.dot_general(dhb, wd_ref[k], _NT, preferred_element_type=F32)
            gv = g_ref[k].astype(F32)
            uv = u_ref[k].astype(F32)
            sg = 0.5 * jnp.tanh(0.5 * gv) + 0.5
            dg_ref[k] = (dz * uv * (sg * (1.0 + gv * (1.0 - sg)))).astype(BF16)
            du_ref[k] = (dz * (gv * sg)).astype(BF16)

    return pl.pallas_call(
        body, name=name, grid=(t // tm,),
        in_specs=[pl.BlockSpec((tm, d), lambda i: (i, 0)), _resident((N_CHIPS, f4, d), lambda i: (0, layer, 0)), h_spec, h_spec],
        out_specs=[h_spec] * 2, out_shape=[S((N_CHIPS, t, f4), BF16)] * 2, compiler_params=_params(1))(dh, wd_all, g, u)


def _norm_bwd_specs(tm, d):
    row = pl.BlockSpec((tm, d), lambda i: (i, 0))
    vec = pl.BlockSpec((1, d), lambda i: (0, 0))
    return [row, vec, row], [row, vec]


def _norm_bwd_tail(da, h_ref, g_ref, dhi_ref, dho_ref, dgain_ref):
    dx, dgain = _rms_bwd(h_ref[...], g_ref[...], da)
    dho_ref[...] = dhi_ref[...] + dx

    @pl.when(pl.program_id(0) == 0)
    def _():
        dgain_ref[...] = jnp.zeros_like(dgain_ref)

    dgain_ref[...] += dgain


def ffn_bwd_input(name, dg, du, wg_all, wu_all, layer, h, gain, dh_in):
    _, t, f4 = dg.shape
    d = h.shape[1]
    tm = _tile(t, 512)
    h_spec = pl.BlockSpec((N_CHIPS, tm, f4), lambda i: (0, i, 0))
    w_spec = _resident((N_CHIPS, d, f4), lambda i: (0, layer, 0))
    tail_in, tail_out = _norm_bwd_specs(tm, d)

    def body(dg_ref, du_ref, wg_ref, wu_ref, *tail):
        acc = jnp.zeros((tm, d), F32)
        for k in range(N_CHIPS):
            acc = acc + lax.dot_general(dg_ref[k], wg_ref[k], _NT, preferred_element_type=F32)
            acc = acc + lax.dot_general(du_ref[k], wu_ref[k], _NT, preferred_element_type=F32)
        _norm_bwd_tail(acc, *tail)

    return pl.pallas_call(
        body, name=name, grid=(t // tm,), in_specs=[h_spec, h_spec, w_spec, w_spec] + tail_in, out_specs=tail_out,
        out_shape=[S((t, d), F32), S((1, d), F32)], compiler_params=_params(1))(dg, du, wg_all, wu_all, h, gain, dh_in)


def ffn_wgrad_up(name, a, dg, du):
    t, d = a.shape
    f4 = dg.shape[2]
    tk = _tile(t, 512)
    nt = t // tk
    h_spec = pl.BlockSpec((None, tk, f4), lambda k, s: (k, s, 0))
    o_spec = pl.BlockSpec((None, d, f4), lambda k, s: (k, 0, 0))

    def body(a_ref, dg_ref, du_ref, og_ref, ou_ref, accg, accu):
        s = pl.program_id(1)

        @pl.when(s == 0)
        def _():
            accg[...] = jnp.zeros_like(accg)
            accu[...] = jnp.zeros_like(accu)

        av = a_ref[...]
        accg[...] += lax.dot_general(av, dg_ref[...], _TN, preferred_element_type=F32)
        accu[...] += lax.dot_general(av, du_ref[...], _TN, preferred_element_type=F32)

        @pl.when(s == nt - 1)
        def _():
            og_ref[...] = accg[...].astype(BF16)
            ou_ref[...] = accu[...].astype(BF16)

    out = S((N_CHIPS, d, f4), BF16)
    return pl.pallas_call(
        body, name=name, grid=(N_CHIPS, nt), in_specs=[pl.BlockSpec((tk, d), lambda k, s: (s, 0)), h_spec, h_spec],
        out_specs=[o_spec, o_spec], out_shape=[out, out],
        scratch_shapes=[pltpu.VMEM((d, f4), F32), pltpu.VMEM((d, f4), F32)], compiler_params=_params(2))(a, dg, du)


def ffn_wgrad_down(name, z, dh):
    _, t, f4 = z.shape
    d = dh.shape[1]
    tk = _tile(t, 512)
    nt = t // tk

    def body(z_ref, dh_ref, o_ref, acc):
        s = pl.program_id(0)

        @pl.when(s == 0)
        def _():
            acc[...] = jnp.zeros_like(acc)

        dhb = dh_ref[...].astype(BF16)
        for k in range(N_CHIPS):
            acc[k] += lax.dot_general(z_ref[k], dhb, _TN, preferred_element_type=F32)

        @pl.when(s == nt - 1)
        def _():
            o_ref[...] = acc[...].astype(BF16)

    return pl.pallas_call(
        body, name=name, grid=(nt,),
        in_specs=[pl.BlockSpec((N_CHIPS, tk, f4), lambda s: (0, s, 0)), pl.BlockSpec((tk, d), lambda s: (s, 0))],
        out_specs=pl.BlockSpec((N_CHIPS, f4, d), lambda s: (0, 0, 0)), out_shape=S((N_CHIPS, f4, d), BF16),
        scratch_shapes=[pltpu.VMEM((N_CHIPS, f4, d), F32)], compiler_params=_params(1))(z, dh)


def conv_in_proj(name, a, w):
    t, d = a.shape
    tm = _tile(t, 512)
    return _mm(name, [(a, pl.BlockSpec((tm, d), lambda j, i: (i, 0)))], [(w, pl.BlockSpec((d, d), lambda j, i: (0, j)))],
               [(0, 0, 0)], _NN, (3, t // tm), None,
               [(S((3, t, d), F32), pl.BlockSpec((None, tm, d), lambda j, i: (j, i, 0)))], None, _store)[0]


def conv_in_bwd_input(name, dbcx, w, h, gain, dh_in):
    _, t, d = dbcx.shape
    tm = _tile(t, 512)
    tail_in, tail_out = _norm_bwd_specs(tm, d)

    def body(g_ref, w_ref, *tail):
        acc = jnp.zeros((tm, d), F32)
        for j in range(3):
            acc = acc + lax.dot_general(g_ref[j], w_ref[:, j * d:(j + 1) * d], _NT, preferred_element_type=F32)
        _norm_bwd_tail(acc, *tail)

    return pl.pallas_call(
        body, name=name, grid=(t // tm,),
        in_specs=[pl.BlockSpec((3, tm, d), lambda i: (0, i, 0)), _resident((d, 3 * d), lambda i: (0, 0))] + tail_in,
        out_specs=tail_out, out_shape=[S((t, d), F32), S((1, d), F32)], compiler_params=_params(1))(dbcx, w, h, gain, dh_in)


def linear_nt_norm_bwd(name, dy, w, h, gain, dh_in):
    t, n = dy.shape
    k = w.shape[0]
    tm = _tile(t, 512)
    tail_in, tail_out = _norm_bwd_specs(tm, k)

    def body(dy_ref, w_ref, *tail):
        _norm_bwd_tail(lax.dot_general(dy_ref[...].astype(BF16), w_ref[...], _NT, preferred_element_type=F32), *tail)

    return pl.pallas_call(
        body, name=name, grid=(t // tm,),
        in_specs=[pl.BlockSpec((tm, n), lambda i: (i, 0)), _resident((k, n), lambda i: (0, 0))] + tail_in,
        out_specs=tail_out, out_shape=[S((t, k), F32), S((1, k), F32)], compiler_params=_params(1))(dy, w, h, gain, dh_in)


def conv_in_wgrad(name, a, dbcx):
    t, d = a.shape
    tk = _tile(t, 512)
    return _mm(name, [(a, pl.BlockSpec((tk, d), lambda j, s: (s, 0)))], [(dbcx, pl.BlockSpec((None, tk, d), lambda j, s: (j, s, 0)))],
               [(0, 0, 0)], _TN, (3, t // tk), 1,
               [(S((d, 3 * d), BF16), pl.BlockSpec((d, d), lambda j, s: (0, j)))], (d, d), _store)[0]


def _rstd(x):
    return lax.rsqrt(jnp.mean(x * x, axis=-1, keepdims=True) + RMS_EPS)


def _rms_bwd(x, g, dy):
    r = _rstd(x)
    xhat = x * r
    dgain = jnp.sum(dy * xhat, axis=0, keepdims=True)
    dxh = dy * g
    dx = r * (dxh - xhat * jnp.mean(dxh * xhat, axis=-1, keepdims=True))
    return dx, dgain


def rms_fwd(name, h, g):
    t, d = h.shape
    tr = _tile(t, 512)

    def body(h_ref, g_ref, a_ref):
        x = h_ref[...]
        a_ref[...] = (x * _rstd(x) * g_ref[...]).astype(BF16)

    return pl.pallas_call(
        body, name=name, grid=(t // tr,),
        in_specs=[pl.BlockSpec((tr, d), lambda i: (i, 0)), pl.BlockSpec((1, d), lambda i: (0, 0))],
        out_specs=pl.BlockSpec((tr, d), lambda i: (i, 0)), out_shape=S((t, d), BF16), compiler_params=_params(1))(h, g)


def rms_bwd(name, h, g, da, dh_in):
    t, d = h.shape
    tr = _tile(t, 512)

    def body(h_ref, g_ref, da_ref, dhi_ref, dho_ref, dg_ref):
        dx, dgain = _rms_bwd(h_ref[...], g_ref[...], da_ref[...])
        dho_ref[...] = dhi_ref[...] + dx

        @pl.when(pl.program_id(0) == 0)
        def _():
            dg_ref[...] = jnp.zeros_like(dg_ref)

        dg_ref[...] += dgain

    row = pl.BlockSpec((tr, d), lambda i: (i, 0))
    vec = pl.BlockSpec((1, d), lambda i: (0, 0))
    return pl.pallas_call(
        body, name=name, grid=(t // tr,), in_specs=[row, vec, row, row], out_specs=[row, vec],
        out_shape=[S((t, d), F32), S((1, d), F32)], compiler_params=_params(1))(h, g, da, dh_in)


def loss_head(name, h, g, target):
    t, d = h.shape
    tr = _tile(t, 512)

    def body(h_ref, g_ref, t_ref, dh_ref, dg_ref, loss_ref):
        x = h_ref[...]
        g = g_ref[...]
        r = _rstd(x)
        xhat = x * r
        err = xhat * g - t_ref[...]
        dy = err * (1.0 / d)
        dxh = dy * g
        dh_ref[...] = r * (dxh - xhat * jnp.mean(dxh * xhat, axis=-1, keepdims=True))

        @pl.when(pl.program_id(0) == 0)
        def _():
            dg_ref[...] = jnp.zeros_like(dg_ref)
            loss_ref[...] = jnp.zeros_like(loss_ref)

        dg_ref[...] += jnp.sum(dy * xhat, axis=0, keepdims=True)
        per_token = jnp.mean(err * err, axis=-1, keepdims=True)
        loss_ref[...] += 0.5 * jnp.sum(per_token, axis=0, keepdims=True)

    row = pl.BlockSpec((tr, d), lambda i: (i, 0))
    vec = pl.BlockSpec((1, d), lambda i: (0, 0))
    one = pl.BlockSpec((1, 1), lambda i: (0, 0))
    return pl.pallas_call(
        body, name=name, grid=(t // tr,), in_specs=[row, vec, row], out_specs=[row, vec, one],
        out_shape=[S((t, d), F32), S((1, d), F32), S((1, 1), F32)], compiler_params=_params(1))(h, g, target)


def mla_mid(name, proj, g_cq, g_ckv, cos, sin):
    t, n = proj.shape
    ql, kl = g_cq.shape[1], g_ckv.shape[1]
    tr = _tile(t, 512)

    def body(p_ref, gq_ref, gk_ref, c_ref, s_ref, cq_ref, ckv_ref, kr_ref):
        xq = p_ref[:, 0:ql]
        cq_ref[...] = (xq * _rstd(xq) * gq_ref[...]).astype(BF16)
        xk = p_ref[:, ql:ql + kl]
        ckv_ref[...] = (xk * _rstd(xk) * gk_ref[...]).astype(BF16)
        k1 = p_ref[:, ql + kl:ql + kl + HALF]
        k2 = p_ref[:, ql + kl + HALF:ql + kl + ROPE]
        c = c_ref[...]
        s = s_ref[...]
        kr_ref[:, 0:HALF] = k1 * c - k2 * s
        kr_ref[:, HALF:ROPE] = k1 * s + k2 * c

    def row(w):
        return pl.BlockSpec((tr, w), lambda i: (i, 0))

    def vec(w):
        return pl.BlockSpec((1, w), lambda i: (0, 0))

    return pl.pallas_call(
        body, name=name, grid=(t // tr,), in_specs=[row(n), vec(ql), vec(kl), row(HALF), row(HALF)],
        out_specs=[row(ql), row(kl), row(ROPE)], out_shape=[S((t, ql), BF16), S((t, kl), BF16), S((t, ROPE), F32)],
        compiler_params=_params(1))(proj, g_cq, g_ckv, cos, sin)


def mla_mid_bwd(name, proj, g_cq, g_ckv, dcq, dckv, dkr, cos, sin):
    t, n = proj.shape
    ql, kl = g_cq.shape[1], g_ckv.shape[1]
    tr = _tile(t, 512)

    def body(p_ref, gq_ref, gk_ref, dcq_ref, dckv_ref, dkr_ref, c_ref, s_ref, dp_ref, dgq_ref, dgk_ref):
        dxq, dgq = _rms_bwd(p_ref[:, 0:ql], gq_ref[...], dcq_ref[...])
        dp_ref[:, 0:ql] = dxq.astype(BF16)
        dxk, dgk = _rms_bwd(p_ref[:, ql:ql + kl], gk_ref[...], dckv_ref[...])
        dp_ref[:, ql:ql + kl] = dxk.astype(BF16)
        d1 = dkr_ref[:, 0:HALF]
        d2 = dkr_ref[:, HALF:ROPE]
        c = c_ref[...]
        s = s_ref[...]
        dp_ref[:, ql + kl:ql + kl + HALF] = (d1 * c + d2 * s).astype(BF16)
        dp_ref[:, ql + kl + HALF:ql + kl + ROPE] = (d2 * c - d1 * s).astype(BF16)

        @pl.when(pl.program_id(0) == 0)
        def _():
            dgq_ref[...] = jnp.zeros_like(dgq_ref)
            dgk_ref[...] = jnp.zeros_like(dgk_ref)

        dgq_ref[...] += dgq
        dgk_ref[...] += dgk

    def row(w):
        return pl.BlockSpec((tr, w), lambda i: (i, 0))

    def vec(w):
        return pl.BlockSpec((1, w), lambda i: (0, 0))

    return pl.pallas_call(
        body, name=name, grid=(t // tr,),
        in_specs=[row(n), vec(ql), vec(kl), row(ql), row(kl), row(ROPE), row(HALF), row(HALF)],
        out_specs=[row(n), vec(ql), vec(kl)], out_shape=[S((t, n), BF16), S((1, ql), F32), S((1, kl), F32)],
        compiler_params=_params(1))(proj, g_cq, g_ckv, dcq, dckv, dkr, cos, sin)


def qkv_heads(name, q, kv, kr, cos, sin, shards=()):
    t = q.shape[0]
    tr = _tile(t, 256)
    n = len(shards)

    def body(q_ref, kv_ref, kr_ref, c_ref, s_ref, *refs):
        src = refs[:n]
        qo_ref, ko_ref, vo_ref = refs[n:n + 3]
        if n:
            _ride_along(gather_ici_copies(src, refs[n + 3:2 * n + 3], *refs[2 * n + 3:]), (pl.program_id(0),), (t // tr,))
        c = c_ref[...]
        s = s_ref[...]
        krb = kr_ref[...].astype(BF16)
        for h in range(N_HEADS):
            q0 = h * QK
            qo_ref[h, :, 0:NOPE] = q_ref[:, q0:q0 + NOPE].astype(BF16)
            q1 = q_ref[:, q0 + NOPE:q0 + NOPE + HALF]
            q2 = q_ref[:, q0 + NOPE + HALF:q0 + QK]
            qo_ref[h, :, NOPE:NOPE + HALF] = (q1 * c - q2 * s).astype(BF16)
            qo_ref[h, :, NOPE + HALF:QK] = (q1 * s + q2 * c).astype(BF16)
            k0 = h * (NOPE + VDIM)
            ko_ref[h, :, 0:NOPE] = kv_ref[:, k0:k0 + NOPE]
            ko_ref[h, :, NOPE:QK] = krb
            vo_ref[h] = kv_ref[:, k0 + NOPE:k0 + NOPE + VDIM]

    def row(w):
        return pl.BlockSpec((tr, w), lambda i: (i, 0))

    def heads(w):
        return pl.BlockSpec((N_HEADS, tr, w), lambda i: (0, i, 0))

    outs = pl.pallas_call(
        body, name=name, grid=(t // tr,),
        in_specs=[row(N_HEADS * QK), row(N_HEADS * (NOPE + VDIM)), row(ROPE), row(HALF), row(HALF)] + [ANY] * n,
        out_specs=[heads(QK), heads(QK), heads(VDIM)] + [ANY] * n,
        out_shape=[S((N_HEADS, t, QK), BF16), S((N_HEADS, t, QK), BF16), S((N_HEADS, t, VDIM), BF16)]
        + [S((N_CHIPS,) + s.shape, s.dtype) for s in shards],
        scratch_shapes=[pltpu.SemaphoreType.DMA((n, 3)), pltpu.SemaphoreType.DMA((n, 3))] if n else [],
        compiler_params=_params(1))(q, kv, kr, cos, sin, *shards)
    return outs[0], outs[1], outs[2], list(outs[3:])


def qkv_heads_bwd(name, dq_h, dk_h, dv_h, cos, sin):
    t = dq_h.shape[1]
    tr = _tile(t, 256)

    def body(dq_ref, dk_ref, dv_ref, c_ref, s_ref, q_ref, kv_ref, kr_ref):
        c = c_ref[...]
        s = s_ref[...]
        dkr = jnp.zeros((tr, ROPE), F32)
        for h in range(N_HEADS):
            q0 = h * QK
            q_ref[:, q0:q0 + NOPE] = dq_ref[h, :, 0:NOPE].astype(BF16)
            d1 = dq_ref[h, :, NOPE:NOPE + HALF]
            d2 = dq_ref[h, :, NOPE + HALF:QK]
            q_ref[:, q0 + NOPE:q0 + NOPE + HALF] = (d1 * c + d2 * s).astype(BF16)
            q_ref[:, q0 + NOPE + HALF:q0 + QK] = (d2 * c - d1 * s).astype(BF16)
            k0 = h * (NOPE + VDIM)
            kv_ref[:, k0:k0 + NOPE] = dk_ref[h, :, 0:NOPE].astype(BF16)
            kv_ref[:, k0 + NOPE:k0 + NOPE + VDIM] = dv_ref[h].astype(BF16)
            dkr = dkr + dk_ref[h, :, NOPE:QK]
        kr_ref[...] = dkr

    def row(w):
        return pl.BlockSpec((tr, w), lambda i: (i, 0))

    def heads(w):
        return pl.BlockSpec((N_HEADS, tr, w), lambda i: (0, i, 0))

    return pl.pallas_call(
        body, name=name, grid=(t // tr,),
        in_specs=[heads(QK), heads(QK), heads(VDIM), row(HALF), row(HALF)],
        out_specs=[row(N_HEADS * QK), row(N_HEADS * (NOPE + VDIM)), row(ROPE)],
        out_shape=[S((t, N_HEADS * QK), BF16), S((t, N_HEADS * (NOPE + VDIM)), BF16), S((t, ROPE), F32)],
        compiler_params=_params(1))(dq_h, dk_h, dv_h, cos, sin)


def _chunk_mask_t(q_start, k_start, bq, bk):
    kc = (k_start + lax.broadcasted_iota(jnp.int32, (bk, bq), 0)) // CHUNK
    qc = (q_start + lax.broadcasted_iota(jnp.int32, (bk, bq), 1)) // CHUNK
    return kc <= qc


def attention_fwd(name, q, k, v, shards=()):
    nh, t, _ = q.shape
    blk = ATT_BLOCK
    nq = t // blk
    n = len(shards)

    def body(q_ref, k_ref, v_ref, *refs):
        src = refs[:n]
        o_ref, lse_ref = refs[n:n + 2]
        dst = refs[n + 2:2 * n + 2]
        m_ref, l_ref, acc_ref, s_buf, p_buf, alpha_buf, bias_ref = refs[2 * n + 2:2 * n + 9]
        i = pl.program_id(1)
        if n:
            send_sems, recv_sems = refs[2 * n + 9:]
            _ride_along(gather_ici_copies(src, dst, send_sems, recv_sems), (pl.program_id(0), i), (nh, nq))

        @pl.when((pl.program_id(0) == 0) & (i == 0))
        def _():
            bias_ref[...] = jnp.where(_chunk_mask_t(0, 0, blk, blk), 0.0, MASK_VALUE)

        m_ref[...] = jnp.full_like(m_ref, MASK_VALUE)
        l_ref[...] = jnp.zeros_like(l_ref)
        acc_ref[...] = jnp.zeros_like(acc_ref)

        def rows(b):
            return pl.ds(pl.multiple_of(b * blk, blk), blk)

        def scores(b, slot):
            s_buf[slot] = lax.dot_general(k_ref[rows(b), :], q_ref[...], _NT, preferred_element_type=F32)

        def softmax(slot, diagonal):
            s = s_buf[slot]
            if diagonal:
                s = s + bias_ref[...]
            m_old = m_ref[...]
            m_new = jnp.maximum(m_old, jnp.max(s, axis=0, keepdims=True))
            p = jnp.exp2((s - m_new) * SCORE_SCALE_LOG2)
            alpha = jnp.exp2((m_old - m_new) * SCORE_SCALE_LOG2)
            l_ref[...] = alpha * l_ref[...] + jnp.sum(p, axis=0, keepdims=True)
            m_ref[...] = m_new
            alpha_buf[slot] = alpha
            p_buf[slot] = p.astype(BF16)

        def values(b, slot):
            pv = lax.dot_general(v_ref[rows(b), :], p_buf[slot], _TN, preferred_element_type=F32)
            acc_ref[...] = alpha_buf[slot] * acc_ref[...] + pv

        def step(t, slot):
            values(t - 2, slot)
            softmax(1 - slot, False)
            scores(t, slot)

        scores(0, 0)

        @pl.when(i == 0)
        def _():
            softmax(0, True)
            values(0, 0)

        @pl.when(i > 0)
        def _():
            scores(1, 1)
            softmax(0, False)
            steady = i - 1

            def pair(u, carry):
                step(2 + 2 * u, 0)
                step(3 + 2 * u, 1)
                return carry

            lax.fori_loop(0, steady // 2, pair, 0)

            @pl.when(steady % 2 == 1)
            def _():
                step(i, 0)

            last = i % 2
            softmax(last, True)
            values(i - 1, 1 - last)
            values(i, last)

        l = l_ref[...]
        o_ref[...] = (acc_ref[...] / l).T
        lse_ref[...] = m_ref[...] * SCORE_SCALE + jnp.log(l)

    outs = pl.pallas_call(
        body, name=name, grid=(nh, nq),
        in_specs=[pl.BlockSpec((None, blk, QK), lambda h, i: (h, i, 0)), pl.BlockSpec((None, t, QK), lambda h, i: (h, 0, 0)),
                  pl.BlockSpec((None, t, VDIM), lambda h, i: (h, 0, 0))] + [ANY] * n,
        out_specs=[pl.BlockSpec((blk, VDIM), lambda h, i: (i, h)),
                   pl.BlockSpec((None, None, 1, blk), lambda h, i: (h, i, 0, 0))] + [ANY] * n,
        out_shape=[S((t, nh * VDIM), F32), S((nh, nq, 1, blk), F32)] + [S((N_CHIPS,) + s.shape, s.dtype) for s in shards],
        scratch_shapes=[pltpu.VMEM((1, blk), F32), pltpu.VMEM((1, blk), F32), pltpu.VMEM((VDIM, blk), F32),
                        pltpu.VMEM((2, blk, blk), F32), pltpu.VMEM((2, blk, blk), BF16), pltpu.VMEM((2, 1, blk), F32),
                        pltpu.VMEM((blk, blk), F32)]
        + ([pltpu.SemaphoreType.DMA((n, 3)), pltpu.SemaphoreType.DMA((n, 3))] if n else []),
        compiler_params=_params(2))(q, k, v, *shards)
    return outs[0], outs[1], list(outs[2:])


def attention_delta(name, do, o):
    t = do.shape[0]
    blk = ATT_BLOCK

    def body(do_ref, o_ref, d_ref):
        for h in range(N_HEADS):
            cols = slice(h * VDIM, (h + 1) * VDIM)
            d_ref[h] = jnp.sum((do_ref[:, cols] * o_ref[:, cols]).T, axis=0, keepdims=True)

    tile = pl.BlockSpec((blk, N_HEADS * VDIM), lambda i: (i, 0))
    return pl.pallas_call(
        body, name=name, grid=(t // blk,), in_specs=[tile, tile],
        out_specs=pl.BlockSpec((N_HEADS, None, 1, blk), lambda i: (0, i, 0, 0)), out_shape=S((N_HEADS, t // blk, 1, blk), F32),
        compiler_params=_params(1))(do, o)


def attention_bwd(name, q, k, v, do, lse, delta, parts=()):
    nh, t, _ = q.shape
    blk = ATT_BLOCK
    nq = t // blk
    n = len(parts)
    scale = 1.0 / math.sqrt(QK)

    def body(q_ref, k_ref, v_ref, do_ref, lse_ref, dl_ref, *refs):
        src = refs[:n]
        dq_ref, dk_ref, dv_ref = refs[n:n + 3]
        dst = refs[n + 3:2 * n + 3]
        s_buf, dp_buf, p_buf, ds_buf, bias_ref = refs[2 * n + 3:2 * n + 8]
        j = pl.program_id(1)
        if n:
            send_sems, recv_sems = refs[2 * n + 8:]
            _ride_along(scatter_ici_copies(src, dst, send_sems, recv_sems), (pl.program_id(0), j), (nh, nq))

        @pl.when((pl.program_id(0) == 0) & (j == 0))
        def _():
            bias_ref[...] = jnp.where(_chunk_mask_t(0, 0, blk, blk), 0.0, MASK_VALUE)

        @pl.when(j == 0)
        def _():
            dq_ref[...] = jnp.zeros_like(dq_ref)

        dk_ref[...] = jnp.zeros_like(dk_ref)
        dv_ref[...] = jnp.zeros_like(dv_ref)

        def rows(b):
            return pl.ds(pl.multiple_of((j + b) * blk, blk), blk)

        def products(b, slot):
            s_buf[slot] = lax.dot_general(k_ref[...], q_ref[rows(b), :], _NT, preferred_element_type=F32)
            dp_buf[slot] = lax.dot_general(v_ref[...], do_ref[rows(b), :].astype(BF16), _NT, preferred_element_type=F32)

        def softmax_bwd(b, slot, diagonal):
            s = s_buf[slot]
            if diagonal:
                s = s + bias_ref[...]
            p = jnp.exp2(s * SCORE_SCALE_LOG2 - lse_ref[j + b] * LOG2_E)
            p_buf[slot] = p.astype(BF16)
            ds_buf[slot] = (p * (dp_buf[slot] - dl_ref[j + b]) * scale).astype(BF16)

        def gradients(b, slot):
            dv_ref[...] += jnp.dot(p_buf[slot], do_ref[rows(b), :].astype(BF16), preferred_element_type=F32)
            dk_ref[...] += jnp.dot(ds_buf[slot], q_ref[rows(b), :], preferred_element_type=F32)
            dq_ref[rows(b), :] += lax.dot_general(ds_buf[slot], k_ref[...], _TN, preferred_element_type=F32)

        def step(t, slot):
            gradients(t - 2, slot)
            softmax_bwd(t - 1, 1 - slot, False)
            products(t, slot)

        count = nq - j
        products(0, 0)

        @pl.when(count == 1)
        def _():
            softmax_bwd(0, 0, True)
            gradients(0, 0)

        @pl.when(count > 1)
        def _():
            products(1, 1)
            softmax_bwd(0, 0, True)
            steady = count - 2

            def pair(u, carry):
                step(2 + 2 * u, 0)
                step(3 + 2 * u, 1)
                return carry

            lax.fori_loop(0, steady // 2, pair, 0)

            @pl.when(steady % 2 == 1)
            def _():
                step(count - 1, 0)

            last = (count - 1) % 2
            softmax_bwd(count - 1, last, False)
            gradients(count - 2, 1 - last)
            gradients(count - 1, last)

    head_all = lambda w: pl.BlockSpec((None, t, w), lambda h, j: (h, 0, 0))
    head_blk = lambda w: pl.BlockSpec((None, blk, w), lambda h, j: (h, j, 0))
    stats = pl.BlockSpec((None, nq, 1, blk), lambda h, j: (h, 0, 0, 0))
    outs = pl.pallas_call(
        body, name=name, grid=(nh, nq),
        in_specs=[head_all(QK), head_blk(QK), head_blk(VDIM), pl.BlockSpec((t, VDIM), lambda h, j: (0, h)), stats, stats] + [ANY] * n,
        out_specs=[head_all(QK), head_blk(QK), head_blk(VDIM)] + [ANY] * n,
        out_shape=[S((nh, t, QK), F32), S((nh, t, QK), F32), S((nh, t, VDIM), F32)] + [S(p.shape, p.dtype) for p in parts],
        scratch_shapes=[pltpu.VMEM((2, blk, blk), F32), pltpu.VMEM((2, blk, blk), F32), pltpu.VMEM((2, blk, blk), BF16),
                        pltpu.VMEM((2, blk, blk), BF16), pltpu.VMEM((blk, blk), F32)]
        + ([pltpu.SemaphoreType.DMA((n, 3)), pltpu.SemaphoreType.DMA((n, 3))] if n else []),
        compiler_params=_params(2))(q, k, v, do, lse, delta, *parts)
    return outs[0], outs[1], outs[2], list(outs[3:])


def _shift_down(u, s):
    rows = lax.broadcasted_iota(jnp.int32, u.shape, 0)
    return jnp.where(rows >= s, pltpu.roll(u, s, 0), 0.0)


def _shift_up(u, s):
    n = u.shape[0]
    rows = lax.broadcasted_iota(jnp.int32, u.shape, 0)
    return jnp.where(rows < n - s, pltpu.roll(u, n - s, 0), 0.0)


def _conv_specs(t, d, lanes):
    slab = lambda part: pl.BlockSpec((None, t, lanes), lambda j, part=part: (part, 0, j))
    return slab, pl.BlockSpec((3, lanes), lambda j: (0, j)), pl.BlockSpec((t, lanes), lambda j: (0, j))


def conv_fwd(name, bcx, w):
    _, t, d = bcx.shape
    lanes = _tile(d, 128, 128)
    slab, w_spec, col = _conv_specs(t, d, lanes)

    def body(b_ref, c_ref, x_ref, w_ref, y_ref):
        u = c_ref[...] * x_ref[...]
        uc = w_ref[0:1, :] * _shift_down(u, 2) + w_ref[1:2, :] * _shift_down(u, 1) + w_ref[2:3, :] * u
        y_ref[...] = (b_ref[...] * uc).astype(BF16)

    return pl.pallas_call(
        body, name=name, grid=(d // lanes,), in_specs=[slab(0), slab(1), slab(2), w_spec], out_specs=col,
        out_shape=S((t, d), BF16), compiler_params=_params(1))(bcx, bcx, bcx, w)


def conv_bwd(name, bcx, w, dy):
    _, t, d = bcx.shape
    lanes = _tile(d, 128, 128)
    slab, w_spec, col = _conv_specs(t, d, lanes)

    def body(b_ref, c_ref, x_ref, w_ref, dy_ref, d_ref, dw_ref):
        c = c_ref[...]
        x = x_ref[...]
        dyv = dy_ref[...]
        u = c * x
        u1 = _shift_down(u, 1)
        u2 = _shift_down(u, 2)
        w0, w1, w2 = w_ref[0:1, :], w_ref[1:2, :], w_ref[2:3, :]
        d_ref[0] = (dyv * (w0 * u2 + w1 * u1 + w2 * u)).astype(BF16)
        duc = dyv * b_ref[...]
        dw_ref[0:1, :] = jnp.sum(duc * u2, axis=0, keepdims=True)
        dw_ref[1:2, :] = jnp.sum(duc * u1, axis=0, keepdims=True)
        dw_ref[2:3, :] = jnp.sum(duc * u, axis=0, keepdims=True)
        du = w2 * duc + w1 * _shift_up(duc, 1) + w0 * _shift_up(duc, 2)
        d_ref[1] = (du * x).astype(BF16)
        d_ref[2] = (du * c).astype(BF16)

    return pl.pallas_call(
        body, name=name, grid=(d // lanes,), in_specs=[slab(0), slab(1), slab(2), w_spec, col],
        out_specs=[pl.BlockSpec((3, t, lanes), lambda j: (0, 0, j)), w_spec], out_shape=[S((3, t, d), BF16), S((3, d), F32)],
        compiler_params=_params(1))(bcx, bcx, bcx, w, dy)


def adamw(name, w, g, m, v):
    r, c = w.shape
    tr = _tile(r, 512)

    def body(w_ref, g_ref, m_ref, v_ref, d_ref, mo_ref, vo_ref):
        gv = g_ref[...]
        m_new = ADAM_B1 * m_ref[...] + (1.0 - ADAM_B1) * gv
        v_new = ADAM_B2 * v_ref[...] + (1.0 - ADAM_B2) * (gv * gv)
        m_hat = m_new / (1.0 - ADAM_B1 ** ADAM_STEP)
        v_hat = v_new / (1.0 - ADAM_B2 ** ADAM_STEP)
        d_ref[...] = -ADAM_LR * (m_hat / (jnp.sqrt(v_hat) + ADAM_EPS) + ADAM_WD * w_ref[...])
        mo_ref[...] = m_new
        vo_ref[...] = v_new

    blk = pl.BlockSpec((tr, c), lambda i: (i, 0))
    return pl.pallas_call(
        body, name=name, grid=(r // tr,), in_specs=[blk] * 4, out_specs=[blk] * 3, out_shape=[S((r, c), F32)] * 3,
        compiler_params=_params(1))(w, g, m, v)


def _place():
    x, y, c = lax.axis_index("x"), lax.axis_index("y"), lax.axis_index("c")
    other_chips = [(1 - x, y), (x, 1 - y), (1 - x, 1 - y)]
    return x, y, c, other_chips


def _half(c, rows):
    return pl.ds(pl.multiple_of(c * (rows // 2), 16), rows // 2)


def gather_weight_shards(shards):
    n = len(shards)

    def body(*refs):
        src = refs[:n]
        dst = refs[n:2 * n]
        send_sems, recv_sems = refs[2 * n:]
        x, y, c, chips = _place()
        me = 2 * x + y
        sibling = (x, y, 1 - c)

        def copy(i, slot, half_of, sem, to, from_input=False):
            rows = _half(half_of, src[i].shape[0])
            return pltpu.make_async_remote_copy(
                src_ref=src[i].at[rows] if from_input else dst[i].at[slot, rows], dst_ref=dst[i].at[slot, rows],
                send_sem=send_sems.at[i, sem], recv_sem=recv_sems.at[i, sem], device_id=to, device_id_type=MESH)

        sent = []
        for i in range(n):
            for j, chip in enumerate(chips):
                sent.append(copy(i, me, c, j, (*chip, c), from_input=True))
                sent[-1].start()
        for i in range(n):
            for j, (px, py) in enumerate(chips):
                copy(i, 2 * px + py, c, j, sibling).wait_recv()
                sent.append(copy(i, 2 * px + py, c, 3 + j, sibling))
                sent[-1].start()
        for i in range(n):
            for j, (px, py) in enumerate(chips):
                copy(i, 2 * px + py, 1 - c, 3 + j, sibling).wait_recv()
        for cp in sent:
            cp.wait_send()

    outs = pl.pallas_call(
        body, name="gather_weight_shards", in_specs=[ANY] * n, out_specs=[ANY] * n,
        out_shape=[S((N_CHIPS,) + s.shape, s.dtype) for s in shards],
        scratch_shapes=[pltpu.SemaphoreType.DMA((n, 6)), pltpu.SemaphoreType.DMA((n, 6))],
    )(*shards)
    return _fill_own_slot(outs, [s[None] for s in shards])


def gather_ici_copies(src, dst, send_sems, recv_sems):
    x, y, c, chips = _place()
    me = 2 * x + y
    pairs = []
    for i in range(len(src)):
        rows = _half(c, src[i].shape[0])
        for j, (px, py) in enumerate(chips):
            def copy(slot):
                return pltpu.make_async_remote_copy(
                    src_ref=src[i].at[rows], dst_ref=dst[i].at[slot, rows], send_sem=send_sems.at[i, j],
                    recv_sem=recv_sems.at[i, j], device_id=(px, py, c), device_id_type=MESH)
            pairs.append((copy(me), copy(2 * px + py)))
    return pairs


def scatter_ici_copies(src, dst, send_sems, recv_sems):
    x, y, c, chips = _place()
    me = 2 * x + y
    pairs = []
    for i in range(len(src)):
        for j, (px, py) in enumerate(chips):
            def copy(from_slot, to_slot):
                return pltpu.make_async_remote_copy(
                    src_ref=src[i].at[from_slot], dst_ref=dst[i].at[to_slot], send_sem=send_sems.at[i, j],
                    recv_sem=recv_sems.at[i, j], device_id=(px, py, c), device_id_type=MESH)
            pairs.append((copy(2 * px + py, me), copy(me, 2 * px + py)))
    return pairs


def _ride_along(pairs, grid_ids, grid_sizes):
    first = grid_ids[0] == 0
    last = grid_ids[0] == grid_sizes[0] - 1
    for g, size in zip(grid_ids[1:], grid_sizes[1:]):
        first = first & (g == 0)
        last = last & (g == size - 1)

    @pl.when(first)
    def _():
        for outgoing, _ in pairs:
            outgoing.start()

    @pl.when(last)
    def _():
        for _, incoming in pairs:
            incoming.wait_recv()
        for outgoing, _ in pairs:
            outgoing.wait_send()


def _fill_own_slot(gathered, own):
    me = 2 * lax.axis_index("x") + lax.axis_index("y")
    return [lax.dynamic_update_slice(g, o, (me,) + (0,) * (g.ndim - 1)) for g, o in zip(gathered, own)]


def forward_to_sibling(gathered):
    n = len(gathered)

    def body(*refs):
        src = refs[:n]
        dst = refs[n:2 * n]
        send_sems, recv_sems = refs[2 * n:]
        x, y, c, chips = _place()
        pairs = []
        for i in range(n):
            for j, (px, py) in enumerate(chips):
                def copy(half_of):
                    rows = _half(half_of, src[i].shape[1])
                    return pltpu.make_async_remote_copy(
                        src_ref=src[i].at[2 * px + py, rows], dst_ref=dst[i].at[2 * px + py, rows], send_sem=send_sems.at[i, j],
                        recv_sem=recv_sems.at[i, j], device_id=(x, y, 1 - c), device_id_type=MESH)
                pairs.append((copy(c), copy(1 - c)))
        for outgoing, _ in pairs:
            outgoing.start()
        for _, incoming in pairs:
            incoming.wait_recv()
        for outgoing, _ in pairs:
            outgoing.wait_send()

    return pl.pallas_call(
        body, name="forward_to_sibling", in_specs=[ANY] * n, out_specs=[ANY] * n,
        out_shape=[S(g.shape, g.dtype) for g in gathered], input_output_aliases={i: i for i in range(n)},
        scratch_shapes=[pltpu.SemaphoreType.DMA((n, 3)), pltpu.SemaphoreType.DMA((n, 3))],
    )(*gathered)


def sibling_swap_halves(name, grads):
    n = len(grads)

    def body(*refs):
        src = refs[:n]
        dst = refs[n:2 * n]
        send_sems, recv_sems = refs[2 * n:]
        x, y, c, _ = _place()
        copies = [pltpu.make_async_remote_copy(
            src_ref=src[i].at[:, _half(1 - c, src[i].shape[1]), :], dst_ref=dst[i], send_sem=send_sems.at[i],
            recv_sem=recv_sems.at[i], device_id=(x, y, 1 - c), device_id_type=MESH) for i in range(n)]
        for cp in copies:
            cp.start()
        for cp in copies:
            cp.wait()

    return pl.pallas_call(
        body, name=name, in_specs=[ANY] * n, out_specs=[ANY] * n,
        out_shape=[S((g.shape[0], g.shape[1] // 2, g.shape[2]), g.dtype) for g in grads],
        scratch_shapes=[pltpu.SemaphoreType.DMA((n,)), pltpu.SemaphoreType.DMA((n,))],
    )(*grads)


def add_halves(name, g, rx):
    _, r, cdim = g.shape
    r2 = r // 2
    tr = _tile(r2, 512, 16)
    nb = r2 // tr

    def body(lo_ref, hi_ref, rx_ref, o_ref):
        mine = jnp.where(lax.axis_index("c") == 0, lo_ref[...], hi_ref[...])
        o_ref[...] = (mine.astype(F32) + rx_ref[...].astype(F32)).astype(BF16)

    half = pl.BlockSpec((None, tr, cdim), lambda k, i: (k, i, 0))
    return pl.pallas_call(
        body, name=name, grid=(N_CHIPS, nb),
        in_specs=[half, pl.BlockSpec((None, tr, cdim), lambda k, i: (k, nb + i, 0)), half],
        out_specs=half, out_shape=S((N_CHIPS, r2, cdim), BF16), compiler_params=_params(2))(g, g, rx)


def scatter_to_owner_chips(parts):
    n = len(parts)

    def body(*refs):
        src = refs[:n]
        dst = refs[n:2 * n]
        send_sems, recv_sems = refs[2 * n:]
        pairs = scatter_ici_copies(src, dst, send_sems, recv_sems)
        for outgoing, _ in pairs:
            outgoing.start()
        for _, incoming in pairs:
            incoming.wait_recv()
        for outgoing, _ in pairs:
            outgoing.wait_send()

    return pl.pallas_call(
        body, name="scatter_to_owner_chips", in_specs=[ANY] * n, out_specs=[ANY] * n,
        out_shape=[S(p.shape, p.dtype) for p in parts],
        scratch_shapes=[pltpu.SemaphoreType.DMA((n, 3)), pltpu.SemaphoreType.DMA((n, 3))],
    )(*parts)


def _own_slots(parts):
    me = 2 * lax.axis_index("x") + lax.axis_index("y")
    return [lax.dynamic_slice(p, (me, 0, 0), (1,) + p.shape[1:]) for p in parts]


def sum_chips(name, parts):
    _, r2, cdim = parts.shape
    tr = _tile(r2, 512, 16)

    def body(p_ref, o_ref):
        acc = p_ref[0].astype(F32)
        for k in range(1, N_CHIPS):
            acc = acc + p_ref[k].astype(F32)
        o_ref[...] = acc

    return pl.pallas_call(
        body, name=name, grid=(r2 // tr,), in_specs=[pl.BlockSpec((N_CHIPS, tr, cdim), lambda i: (0, i, 0))],
        out_specs=pl.BlockSpec((tr, cdim), lambda i: (i, 0)), out_shape=S((r2, cdim), F32), compiler_params=_params(1))(parts)


def sibling_join_halves(name, halves, targets, where):
    n = len(halves)

    def rows_of(i, half_of):
        r2 = halves[i].shape[0]
        return pl.ds(pl.multiple_of(where[i][1] + half_of * r2, 8), r2)

    def body(*refs):
        src = refs[:n]
        dst = refs[n:n + len(targets)]
        send_sems, recv_sems = refs[n + len(targets):]
        x, y, c, _ = _place()

        def copy(i, half_of):
            return pltpu.make_async_remote_copy(
                src_ref=src[i], dst_ref=dst[where[i][0]].at[rows_of(i, half_of)], send_sem=send_sems.at[i],
                recv_sem=recv_sems.at[i], device_id=(x, y, 1 - c), device_id_type=MESH)

        for i in range(n):
            copy(i, c).start()
        for i in range(n):
            copy(i, 1 - c).wait_recv()
        for i in range(n):
            copy(i, c).wait_send()

    outs = list(pl.pallas_call(
        body, name=name, in_specs=[ANY] * n, out_specs=[ANY] * len(targets), out_shape=[S(tg, F32) for tg in targets],
        scratch_shapes=[pltpu.SemaphoreType.DMA((n,)), pltpu.SemaphoreType.DMA((n,))],
    )(*halves))
    c = lax.axis_index("c")
    for i, h in enumerate(halves):
        tgt, first = where[i]
        outs[tgt] = lax.dynamic_update_slice(outs[tgt], h, (first + c * h.shape[0], 0))
    return outs


def all_reduce_small(name, packed):
    rows, width = packed.shape

    def body(x_ref, o_ref, gathered, send_sems, recv_sems):
        x, y, c, _ = _place()
        me = 4 * x + 2 * y + c
        gathered[me] = x_ref[...]
        flips = [(fx, fy, fc) for fx in (0, 1) for fy in (0, 1) for fc in (0, 1)][1:]

        def copy(r, slot, to):
            return pltpu.make_async_remote_copy(
                src_ref=x_ref, dst_ref=gathered.at[slot], send_sem=send_sems.at[r], recv_sem=recv_sems.at[r],
                device_id=to, device_id_type=MESH)

        def peer(f):
            return (x ^ f[0], y ^ f[1], c ^ f[2])

        sent = [copy(r, me, peer(f)) for r, f in enumerate(flips)]
        for cp in sent:
            cp.start()
        for r, f in enumerate(flips):
            px, py, pc = peer(f)
            copy(r, 4 * px + 2 * py + pc, peer(f)).wait_recv()
        for cp in sent:
            cp.wait_send()
        acc = gathered[0]
        for k in range(1, N_DEV):
            acc = acc + gathered[k]
        o_ref[...] = acc

    vmem = pl.BlockSpec(memory_space=pltpu.VMEM)
    return pl.pallas_call(
        body, name=name, in_specs=[vmem], out_specs=vmem, out_shape=S((rows, width), F32),
        scratch_shapes=[pltpu.VMEM((N_DEV, rows, width), F32), pltpu.SemaphoreType.DMA((N_DEV - 1,)),
                        pltpu.SemaphoreType.DMA((N_DEV - 1,))],
    )(packed)


def _rope_tables(positions):
    inv_freq = 1.0 / (ROPE_THETA ** (jnp.arange(0, ROPE, 2, dtype=F32) / ROPE))
    ang = positions.astype(F32)[:, None] * inv_freq
    return jnp.cos(ang), jnp.sin(ang)


def _unstack_cols(w):
    k4, k, n4 = w.shape
    return jnp.transpose(w, (1, 0, 2)).reshape(k, k4 * n4)


def _stack_cols(w):
    k, n = w.shape
    return jnp.transpose(w.reshape(k, N_CHIPS, n // N_CHIPS), (1, 0, 2))


def kernel(x, positions, mla_norm, mla_w_in, mla_g_cq, mla_g_ckv, mla_w_uq, mla_w_ukv, mla_w_o, conv_norm, conv_w_in, conv_w, conv_w_out, ffn_norm, ffn_w_gate, ffn_w_up, ffn_w_down, final_norm, loss_target, m_mla_norm, m_mla_w_in, m_mla_g_cq, m_mla_g_ckv, m_mla_w_uq, m_mla_w_ukv, m_mla_w_o, m_conv_norm, m_conv_w_in, m_conv_w, m_conv_w_out, m_ffn_norm, m_ffn_w_gate, m_ffn_w_up, m_ffn_w_down, m_final_norm, v_mla_norm, v_mla_w_in, v_mla_g_cq, v_mla_g_ckv, v_mla_w_uq, v_mla_w_ukv, v_mla_w_o, v_conv_norm, v_conv_w_in, v_conv_w, v_conv_w_out, v_ffn_norm, v_ffn_w_gate, v_ffn_w_up, v_ffn_w_down, v_final_norm):
    weights = dict(mla_norm=mla_norm, mla_w_in=mla_w_in, mla_g_cq=mla_g_cq, mla_g_ckv=mla_g_ckv, mla_w_uq=mla_w_uq,
                   mla_w_ukv=mla_w_ukv, mla_w_o=mla_w_o, conv_norm=conv_norm, conv_w_in=conv_w_in, conv_w=conv_w,
                   conv_w_out=conv_w_out, ffn_norm=ffn_norm, ffn_w_gate=ffn_w_gate, ffn_w_up=ffn_w_up,
                   ffn_w_down=ffn_w_down, final_norm=final_norm)
    m_in = dict(mla_norm=m_mla_norm, mla_w_in=m_mla_w_in, mla_g_cq=m_mla_g_cq, mla_g_ckv=m_mla_g_ckv, mla_w_uq=m_mla_w_uq,
                mla_w_ukv=m_mla_w_ukv, mla_w_o=m_mla_w_o, conv_norm=m_conv_norm, conv_w_in=m_conv_w_in, conv_w=m_conv_w,
                conv_w_out=m_conv_w_out, ffn_norm=m_ffn_norm, ffn_w_gate=m_ffn_w_gate, ffn_w_up=m_ffn_w_up,
                ffn_w_down=m_ffn_w_down, final_norm=m_final_norm)
    v_in = dict(mla_norm=v_mla_norm, mla_w_in=v_mla_w_in, mla_g_cq=v_mla_g_cq, mla_g_ckv=v_mla_g_ckv, mla_w_uq=v_mla_w_uq,
                mla_w_ukv=v_mla_w_ukv, mla_w_o=v_mla_w_o, conv_norm=v_conv_norm, conv_w_in=v_conv_w_in, conv_w=v_conv_w,
                conv_w_out=v_conv_w_out, ffn_norm=v_ffn_norm, ffn_w_gate=v_ffn_w_gate, ffn_w_up=v_ffn_w_up,
                ffn_w_down=v_ffn_w_down, final_norm=v_final_norm)
    big = ["mla_w_in", "mla_w_uq", "mla_w_ukv", "mla_w_o", "conv_w_in", "conv_w_out", "ffn_w_gate", "ffn_w_up", "ffn_w_down"]
    order = list(weights)

    t, d = x.shape[1], x.shape[2]
    h0 = x.reshape(t, d)
    target = loss_target.reshape(t, d)
    cos, sin = _rope_tables(positions.reshape(t))

    def rows2d(a):
        return a.reshape(-1, a.shape[-1])

    first, later = big[:4], big[4:]
    shards = {n: rows2d(weights[n]).astype(BF16) for n in big}
    gathered = dict(zip(first, gather_weight_shards([shards[n] for n in first])))
    w_in = gathered["mla_w_in"].reshape(-1, gathered["mla_w_in"].shape[-1])
    w_uq = _unstack_cols(gathered["mla_w_uq"])
    w_ukv = _unstack_cols(gathered["mla_w_ukv"])
    w_o = gathered["mla_w_o"].reshape(-1, d)

    chip = 2 * lax.axis_index("x") + lax.axis_index("y")
    core = lax.axis_index("c")
    d4 = d // N_CHIPS
    first_core = (core == 0).astype(F32)

    def place_shard(shard):
        full = jnp.zeros((shard.shape[0], d), F32)
        return lax.dynamic_update_slice(full, shard * first_core, (0, chip * d4))

    def pack_rows(rows):
        idx = lax.broadcasted_iota(jnp.int32, (SMALL_ROWS, d), 0)
        out = jnp.zeros((SMALL_ROWS, d), F32)
        for r, row in enumerate(rows):
            out = out + jnp.where(idx == r, row, 0.0)
        return out

    cw = place_shard(conv_w.reshape(3, d4))
    pre = all_reduce_small("all_gather_conv_small", pack_rows([place_shard(conv_norm.reshape(1, d4)), cw[0:1], cw[1:2], cw[2:3]]))
    conv_norm_full = pre[0:1]
    conv_w_full = pre[1:4]

    a0 = rms_fwd("mla_norm_fwd", h0, mla_norm)
    proj = linear("mla_in_proj", a0, w_in, F32)
    cq, ckv, kr = mla_mid("mla_mid", proj, mla_g_cq, mla_g_ckv, cos, sin)
    q = linear("mla_q_up", cq, w_uq, F32)
    kv = linear("mla_kv_up", ckv, w_ukv, BF16)
    qh, kh, vh, conv_arriving = qkv_heads("qkv_heads", q, kv, kr, cos, sin, [shards[n] for n in later[:2]])
    attn, lse, ffn_arriving = attention_fwd("attention_fwd", qh, kh, vh, [shards[n] for n in later[2:]])
    gathered.update(zip(later, _fill_own_slot(forward_to_sibling(conv_arriving + ffn_arriving), [shards[n][None] for n in later])))
    cw_in = _unstack_cols(gathered["conv_w_in"])
    cw_out = gathered["conv_w_out"].reshape(-1, d)
    wg_all, wu_all, wd_all = gathered["ffn_w_gate"], gathered["ffn_w_up"], gathered["ffn_w_down"]
    h1, a1 = linear("mla_out_proj", attn, w_o, F32, resid=h0, next_gain=ffn_norm[0:1])

    def ffn_forward(tag, h, a, layer, next_gain):
        g, u, z = ffn_up(f"ffn{tag}_up", a, wg_all, wu_all, layer)
        return g, u, z, ffn_down(f"ffn{tag}_down", z, wd_all, layer, h, next_gain)

    g0, u0, z0, (h2, a2) = ffn_forward(0, h1, a1, 0, conv_norm_full)
    bcx = conv_in_proj("conv_in_proj", a2, cw_in)
    yc = conv_fwd("conv_fwd", bcx, conv_w_full)
    h3, a3 = linear("conv_out_proj", yc, cw_out, F32, resid=h2, next_gain=ffn_norm[1:2])
    g1, u1, z1, h4 = ffn_forward(1, h3, a3, 1, None)
    dh4, d_final_norm, loss_local = loss_head("loss_head", h4, final_norm.reshape(1, d), target)

    def ffn_backward(tag, dh, h, layer, a, g, u, z):
        dg, du = ffn_bwd_hidden(f"ffn{tag}_bwd_hidden", dh, wd_all, layer, g, u)
        d_wd = ffn_wgrad_down(f"ffn{tag}_wgrad_down", z, dh)
        dh_prev, d_norm = ffn_bwd_input(f"ffn{tag}_bwd_input", dg, du, wg_all, wu_all, layer, h, ffn_norm[layer:layer + 1], dh)
        d_wg, d_wu = ffn_wgrad_up(f"ffn{tag}_wgrad_up", a, dg, du)
        return dh_prev, d_norm, [d_wg, d_wu, d_wd]

    def reduce_to_pair_sums(tag, local):
        from_sibling = sibling_swap_halves(f"sibling_swap_{tag}", local)
        return [add_halves(f"pair_sum_{tag}{i}", g, r) for i, (g, r) in enumerate(zip(local, from_sibling))]

    def reduce_from_chips(tag, pair_sums, arrived, targets, where):
        from_chips = _fill_own_slot(arrived, _own_slots(pair_sums))
        my_halves = [sum_chips(f"chip_sum_{tag}{i}", p) for i, p in enumerate(from_chips)]
        return sibling_join_halves(f"sibling_join_{tag}", my_halves, targets, where)

    dh3, d_ffn_norm1, ffn1_grads = ffn_backward(1, dh4, h3, 1, a3, g1, u1, z1)

    dyc = linear_nt("conv_out_bwd_input", dh3, cw_out, F32)
    d_cw_out = wgrad("conv_out_wgrad", yc, dh3)
    dbcx, d_conv_w = conv_bwd("conv_bwd", bcx, conv_w_full, dyc)
    dh2, d_conv_norm = conv_in_bwd_input("conv_in_bwd_input", dbcx, cw_in, h2, conv_norm_full, dh3)
    d_cw_in = conv_in_wgrad("conv_in_wgrad", a2, dbcx)

    dh1, d_ffn_norm0, ffn0_grads = ffn_backward(0, dh2, h1, 0, a1, g0, u0, z0)

    d_attn = linear_nt("mla_out_bwd_input", dh1, w_o, F32)
    d_w_o = wgrad("mla_out_wgrad", attn, dh1)
    rest_pairs = reduce_to_pair_sums("rest", [_stack_cols(d_cw_in), d_cw_out.reshape(N_CHIPS, -1, d)] + ffn1_grads + ffn0_grads
                                     + [d_w_o.reshape(N_CHIPS, -1, d)])
    delta = attention_delta("attention_delta", d_attn, attn)
    dqh, dkh, dvh, rest_arrived = attention_bwd("attention_bwd", qh, kh, vh, d_attn, lse, delta, rest_pairs)
    dq, dkv, dkr = qkv_heads_bwd("qkv_heads_bwd", dqh, dkh, dvh, cos, sin)
    dcq = linear_nt("mla_q_up_bwd_input", dq, w_uq, F32)
    d_w_uq = wgrad("mla_q_up_wgrad", cq, dq)
    dckv = linear_nt("mla_kv_up_bwd_input", dkv, w_ukv, F32)
    d_w_ukv = wgrad("mla_kv_up_wgrad", ckv, dkv)
    dproj, d_g_cq, d_g_ckv = mla_mid_bwd("mla_mid_bwd", proj, mla_g_cq, mla_g_ckv, dcq, dckv, dkr, cos, sin)
    d_w_in = wgrad("mla_in_wgrad", a0, dproj)
    grad_x, d_mla_norm = linear_nt_norm_bwd("mla_in_bwd_input", dproj, w_in, h0, mla_norm, dh1)

    def shard_shape(n):
        return rows2d(weights[n]).shape

    rd, rf = ffn0_grads[0].shape[1], ffn0_grads[2].shape[1]
    rest_where = [(0, 0), (1, 0), (2, rd), (3, rd), (4, rf), (2, 0), (3, 0), (4, 0), (5, 0)]
    rest_names = later + ["mla_w_o"]
    grads = dict(zip(rest_names, reduce_from_chips("rest", rest_pairs, rest_arrived, [shard_shape(n) for n in rest_names], rest_where)))

    mla_pairs = reduce_to_pair_sums("mla", [d_w_in.reshape(N_CHIPS, -1, d_w_in.shape[-1]), _stack_cols(d_w_uq), _stack_cols(d_w_ukv)])
    grads.update(zip(first[:3], reduce_from_chips("mla", mla_pairs, scatter_to_owner_chips(mla_pairs),
                                                  [shard_shape(n) for n in first[:3]], [(i, 0) for i in range(3)])))

    def pad_row(v):
        return jnp.pad(v, ((0, 0), (0, d - v.shape[1])))

    small = all_reduce_small("all_reduce_small_grads", pack_rows([
        d_mla_norm, pad_row(d_g_cq), pad_row(d_g_ckv), d_ffn_norm0, d_ffn_norm1, d_final_norm, d_conv_norm,
        d_conv_w[0:1], d_conv_w[1:2], d_conv_w[2:3], jnp.broadcast_to(loss_local, (1, d))]))
    loss = small[10, 0]
    grads["mla_norm"] = small[0:1]
    grads["mla_g_cq"] = small[1:2, :mla_g_cq.shape[1]]
    grads["mla_g_ckv"] = small[2:3, :mla_g_ckv.shape[1]]
    grads["ffn_norm"] = small[3:5]
    grads["final_norm"] = small[5:6]
    grads["conv_norm"] = lax.dynamic_slice(small[6:7], (0, chip * d4), (1, d4))
    grads["conv_w"] = lax.dynamic_slice(small[7:10], (0, chip * d4), (3, d4))

    outs_g, outs_d, outs_m, outs_v = [], [], [], []
    for n in order:
        w = weights[n]
        delta_w, new_m, new_v = adamw(f"adamw_{n}", rows2d(w), grads[n].reshape(rows2d(w).shape), rows2d(m_in[n]), rows2d(v_in[n]))
        outs_g.append(grads[n].reshape(w.shape))
        outs_d.append(delta_w.reshape(w.shape))
        outs_m.append(new_m.reshape(w.shape))
        outs_v.append(new_v.reshape(w.shape))
    return (loss, grad_x.reshape(x.shape), *outs_g, *outs_d, *outs_m, *outs_v)
```

```python
import math

import jax
import jax.numpy as jnp
from jax import lax
from jax.experimental import pallas as pl
from jax.experimental.pallas import tpu as pltpu

F32 = jnp.float32
BF16 = jnp.bfloat16
S = jax.ShapeDtypeStruct

N_HEADS = 8
NOPE = 128
ROPE = 64
HALF = ROPE // 2
VDIM = 128
QK = NOPE + ROPE
CHUNK = 64
ROPE_THETA = 10000.0
RMS_EPS = 1e-6
ADAM_LR = 0.001
ADAM_B1 = 0.9
ADAM_B2 = 0.999
ADAM_EPS = 1e-08
ADAM_WD = 0.01
ADAM_STEP = 10

N_CHIPS = 4
N_DEV = 8
MASK_VALUE = -1e30
SCORE_SCALE = 1.0 / math.sqrt(QK)
LOG2_E = math.log2(math.e)
SCORE_SCALE_LOG2 = SCORE_SCALE * LOG2_E
VMEM_LIMIT = 48 * 1024 * 1024
VMEM_LIMIT_WHOLE_HEAD = 58 * 1024 * 1024
ATT_BLOCK = 512
SMALL_ROWS = 16

_NN = (((1,), (0,)), ((), ()))
_NT = (((1,), (1,)), ((), ()))
_TN = (((0,), (0,)), ((), ()))
MESH = pl.DeviceIdType.MESH
ANY = pl.BlockSpec(memory_space=pl.ANY)


def _params(n_axes, vmem_limit=VMEM_LIMIT):
    return pltpu.CompilerParams(dimension_semantics=("arbitrary",) * n_axes, vmem_limit_bytes=vmem_limit)


def _tile(n, cap, mult=8):
    for t in range(min(cap, n), 0, -1):
        if n % t == 0 and t % mult == 0:
            return t
    return n


def _sigmoid(x):
    return 1.0 / (1.0 + jnp.exp(-x))


def _mm(name, a_ops, b_ops, products, dims, grid, k_axis, outs, acc_shape, epilogue, extra_ops=()):
    na, nb, ne, no = len(a_ops), len(b_ops), len(extra_ops), len(outs)
    n_acc = 1 + max(c for _, _, c in products)
    nk = 1 if k_axis is None else grid[k_axis]

    def body(*refs):
        a_refs = refs[:na]
        b_refs = refs[na:na + nb]
        e_refs = refs[na + nb:na + nb + ne]
        o_refs = refs[na + nb + ne:na + nb + ne + no]
        acc_refs = refs[na + nb + ne + no:]

        def partial_sums():
            vals = [None] * n_acc
            for ai, bi, ci in products:
                d = lax.dot_general(a_refs[ai][...].astype(BF16), b_refs[bi][...].astype(BF16), dims,
                                    preferred_element_type=F32)
                vals[ci] = d if vals[ci] is None else vals[ci] + d
            return vals

        if nk == 1:
            epilogue(partial_sums(), e_refs, o_refs)
        else:
            k = pl.program_id(k_axis)

            @pl.when(k == 0)
            def _():
                for acc in acc_refs:
                    acc[...] = jnp.zeros_like(acc)

            for acc, v in zip(acc_refs, partial_sums()):
                acc[...] += v

            @pl.when(k == nk - 1)
            def _():
                epilogue([acc[...] for acc in acc_refs], e_refs, o_refs)

    ops = list(a_ops) + list(b_ops) + list(extra_ops)
    return pl.pallas_call(
        body, name=name, grid=grid,
        in_specs=[s for _, s in ops], out_specs=[s for _, s in outs], out_shape=[o for o, _ in outs],
        scratch_shapes=[pltpu.VMEM(acc_shape, F32) for _ in range(n_acc if nk > 1 else 0)],
        compiler_params=_params(len(grid)),
    )(*[a for a, _ in ops])


def _store(accs, e_refs, o_refs):
    o_refs[0][...] = accs[0].astype(o_refs[0].dtype)


def linear(name, x, w, out_dtype, resid=None, next_gain=None):
    t, k = x.shape
    n = w.shape[1]
    tm = _tile(t, 512)
    tn = n if n <= 2048 else _tile(n, 1024, 128)
    tile = pl.BlockSpec((tm, tn), lambda j, i: (i, j))
    extra = [] if resid is None else [(resid, tile)]
    outs = [(S((t, n), out_dtype), tile)]
    if next_gain is not None:
        assert tn == n
        extra.append((next_gain, pl.BlockSpec((1, n), lambda j, i: (0, 0))))
        outs.append((S((t, n), BF16), tile))

    def epilogue(accs, e_refs, o_refs):
        y = accs[0] if resid is None else e_refs[0][...] + accs[0]
        o_refs[0][...] = y.astype(out_dtype)
        if next_gain is not None:
            o_refs[1][...] = (y * _rstd(y) * e_refs[-1][...]).astype(BF16)

    res = _mm(name, [(x, pl.BlockSpec((tm, k), lambda j, i: (i, 0)))], [(w, pl.BlockSpec((k, tn), lambda j, i: (0, j)))],
              [(0, 0, 0)], _NN, (n // tn, t // tm), None, outs, None, epilogue, extra)
    return res[0] if next_gain is None else res


def linear_nt(name, dy, w, out_dtype):
    t, n = dy.shape
    k = w.shape[0]
    tm = _tile(t, 512)
    tc = n if n <= 2048 else _tile(n, 1024, 128)
    return _mm(name, [(dy, pl.BlockSpec((tm, tc), lambda i, c: (i, c)))], [(w, pl.BlockSpec((k, tc), lambda i, c: (0, c)))],
               [(0, 0, 0)], _NT, (t // tm, n // tc), 1,
               [(S((t, k), out_dtype), pl.BlockSpec((tm, k), lambda i, c: (i, 0)))], (tm, k), _store)[0]


def wgrad(name, x, dy):
    t, k = x.shape
    n = dy.shape[1]
    tk = _tile(t, 512)
    tn = n if n <= 1024 else _tile(n, 1024, 128)
    return _mm(name, [(x, pl.BlockSpec((tk, k), lambda j, s: (s, 0)))], [(dy, pl.BlockSpec((tk, tn), lambda j, s: (s, j)))],
               [(0, 0, 0)], _TN, (n // tn, t // tk), 1,
               [(S((k, n), BF16), pl.BlockSpec((k, tn), lambda j, s: (0, j)))], (k, tn), _store)[0]


def _resident(shape, index_map):
    return pl.BlockSpec(shape, index_map, pipeline_mode=pl.Buffered(1))


def ffn_up(name, a, wg_all, wu_all, layer):
    t, d = a.shape
    f4 = wg_all.shape[2]
    tm = _tile(t, 512)
    w_spec = _resident((N_CHIPS, d, f4), lambda i: (0, layer, 0))
    h_spec = pl.BlockSpec((N_CHIPS, tm, f4), lambda i: (0, i, 0))

    def body(a_ref, wg_ref, wu_ref, g_ref, u_ref, z_ref):
        av = a_ref[...]
        for k in range(N_CHIPS):
            g = jnp.dot(av, wg_ref[k], preferred_element_type=F32)
            u = jnp.dot(av, wu_ref[k], preferred_element_type=F32)
            g_ref[k] = g.astype(BF16)
            u_ref[k] = u.astype(BF16)
            z_ref[k] = (g * _sigmoid(g) * u).astype(BF16)

    return pl.pallas_call(
        body, name=name, grid=(t // tm,), in_specs=[pl.BlockSpec((tm, d), lambda i: (i, 0)), w_spec, w_spec],
        out_specs=[h_spec] * 3, out_shape=[S((N_CHIPS, t, f4), BF16)] * 3, compiler_params=_params(1))(a, wg_all, wu_all)


def ffn_down(name, z, wd_all, layer, resid, next_gain=None):
    _, t, f4 = z.shape
    d = wd_all.shape[2]
    tm = _tile(t, 512)
    row = pl.BlockSpec((tm, d), lambda i: (i, 0))
    normed = next_gain is not None

    def body(z_ref, wd_ref, r_ref, *refs):
        acc = r_ref[...]
        for k in range(N_CHIPS):
            acc = acc + jnp.dot(z_ref[k], wd_ref[k], preferred_element_type=F32)
        refs[-2 if normed else -1][...] = acc
        if normed:
            refs[-1][...] = (acc * _rstd(acc) * refs[0][...]).astype(BF16)

    res = pl.pallas_call(
        body, name=name, grid=(t // tm,),
        in_specs=[pl.BlockSpec((N_CHIPS, tm, f4), lambda i: (0, i, 0)), _resident((N_CHIPS, f4, d), lambda i: (0, layer, 0)), row]
        + ([pl.BlockSpec((1, d), lambda i: (0, 0))] if normed else []),
        out_specs=[row] * (2 if normed else 1), out_shape=[S((t, d), F32)] + ([S((t, d), BF16)] if normed else []),
        compiler_params=_params(1))(z, wd_all, resid, *([next_gain] if normed else []))
    return res if normed else res[0]


def ffn_bwd_hidden(name, dh, wd_all, layer, g, u):
    t, d = dh.shape
    f4 = g.shape[2]
    tm = _tile(t, 512)
    h_spec = pl.BlockSpec((N_CHIPS, tm, f4), lambda i: (0, i, 0))

    def body(dh_ref, wd_ref, g_ref, u_ref, dg_ref, du_ref):
        dhb = dh_ref[...].astype(BF16)
        for k in range(N_CHIPS):
            dz = lax.dot_general(dhb, wd_ref[k], _NT, preferred_element_type=F32)
            gv = g_ref[k].astype(F32)
            uv = u_ref[k].astype(F32)
            sg = 0.5 * jnp.tanh(0.5 * gv) + 0.5
            dg_ref[k] = (dz * uv * (sg * (1.0 + gv * (1.0 - sg)))).astype(BF16)
            du_ref[k] = (dz * (gv * sg)).astype(BF16)

    return pl.pallas_call(
        body, name=name, grid=(t // tm,),
        in_specs=[pl.BlockSpec((tm, d), lambda i: (i, 0)), _resident((N_CHIPS, f4, d), lambda i: (0, layer, 0)), h_spec, h_spec],
        out_specs=[h_spec] * 2, out_shape=[S((N_CHIPS, t, f4), BF16)] * 2, compiler_params=_params(1))(dh, wd_all, g, u)


def _norm_bwd_specs(tm, d):
    row = pl.BlockSpec((tm, d), lambda i: (i, 0))
    vec = pl.BlockSpec((1, d), lambda i: (0, 0))
    return [row, vec, row], [row, vec]


def _norm_bwd_tail(da, h_ref, g_ref, dhi_ref, dho_ref, dgain_ref):
    dx, dgain = _rms_bwd(h_ref[...], g_ref[...], da)
    dho_ref[...] = dhi_ref[...] + dx

    @pl.when(pl.program_id(0) == 0)
    def _():
        dgain_ref[...] = jnp.zeros_like(dgain_ref)

    dgain_ref[...] += dgain


def ffn_bwd_input(name, dg, du, wg_all, wu_all, layer, h, gain, dh_in):
    _, t, f4 = dg.shape
    d = h.shape[1]
    tm = _tile(t, 512)
    h_spec = pl.BlockSpec((N_CHIPS, tm, f4), lambda i: (0, i, 0))
    w_spec = _resident((N_CHIPS, d, f4), lambda i: (0, layer, 0))
    tail_in, tail_out = _norm_bwd_specs(tm, d)

    def body(dg_ref, du_ref, wg_ref, wu_ref, *tail):
        acc = jnp.zeros((tm, d), F32)
        for k in range(N_CHIPS):
            acc = acc + lax.dot_general(dg_ref[k], wg_ref[k], _NT, preferred_element_type=F32)
            acc = acc + lax.dot_general(du_ref[k], wu_ref[k], _NT, preferred_element_type=F32)
        _norm_bwd_tail(acc, *tail)

    return pl.pallas_call(
        body, name=name, grid=(t // tm,), in_specs=[h_spec, h_spec, w_spec, w_spec] + tail_in, out_specs=tail_out,
        out_shape=[S((t, d), F32), S((1, d), F32)], compiler_params=_params(1))(dg, du, wg_all, wu_all, h, gain, dh_in)


def ffn_wgrad_up(name, a, dg, du):
    t, d = a.shape
    f4 = dg.shape[2]
    tk = _tile(t, 512)
    nt = t // tk
    h_spec = pl.BlockSpec((None, tk, f4), lambda k, s: (k, s, 0))
    o_spec = pl.BlockSpec((None, d, f4), lambda k, s: (k, 0, 0))

    def body(a_ref, dg_ref, du_ref, og_ref, ou_ref, accg, accu):
        s = pl.program_id(1)

        @pl.when(s == 0)
        def _():
            accg[...] = jnp.zeros_like(accg)
            accu[...] = jnp.zeros_like(accu)

        av = a_ref[...]
        accg[...] += lax.dot_general(av, dg_ref[...], _TN, preferred_element_type=F32)
        accu[...] += lax.dot_general(av, du_ref[...], _TN, preferred_element_type=F32)

        @pl.when(s == nt - 1)
        def _():
            og_ref[...] = accg[...].astype(BF16)
            ou_ref[...] = accu[...].astype(BF16)

    out = S((N_CHIPS, d, f4), BF16)
    return pl.pallas_call(
        body, name=name, grid=(N_CHIPS, nt), in_specs=[pl.BlockSpec((tk, d), lambda k, s: (s, 0)), h_spec, h_spec],
        out_specs=[o_spec, o_spec], out_shape=[out, out],
        scratch_shapes=[pltpu.VMEM((d, f4), F32), pltpu.VMEM((d, f4), F32)], compiler_params=_params(2))(a, dg, du)


def ffn_wgrad_down(name, z, dh):
    _, t, f4 = z.shape
    d = dh.shape[1]
    tk = _tile(t, 512)
    nt = t // tk

    def body(z_ref, dh_ref, o_ref, acc):
        s = pl.program_id(0)

        @pl.when(s == 0)
        def _():
            acc[...] = jnp.zeros_like(acc)

        dhb = dh_ref[...].astype(BF16)
        for k in range(N_CHIPS):
            acc[k] += lax.dot_general(z_ref[k], dhb, _TN, preferred_element_type=F32)

        @pl.when(s == nt - 1)
        def _():
            o_ref[...] = acc[...].astype(BF16)

    return pl.pallas_call(
        body, name=name, grid=(nt,),
        in_specs=[pl.BlockSpec((N_CHIPS, tk, f4), lambda s: (0, s, 0)), pl.BlockSpec((tk, d), lambda s: (s, 0))],
        out_specs=pl.BlockSpec((N_CHIPS, f4, d), lambda s: (0, 0, 0)), out_shape=S((N_CHIPS, f4, d), BF16),
        scratch_shapes=[pltpu.VMEM((N_CHIPS, f4, d), F32)], compiler_params=_params(1))(z, dh)


def conv_in_proj(name, a, w):
    t, d = a.shape
    tm = _tile(t, 512)
    return _mm(name, [(a, pl.BlockSpec((tm, d), lambda j, i: (i, 0)))], [(w, pl.BlockSpec((d, d), lambda j, i: (0, j)))],
               [(0, 0, 0)], _NN, (3, t // tm), None,
               [(S((3, t, d), F32), pl.BlockSpec((None, tm, d), lambda j, i: (j, i, 0)))], None, _store)[0]


def conv_in_bwd_input(name, dbcx, w, h, gain, dh_in):
    _, t, d = dbcx.shape
    tm = _tile(t, 512)
    tail_in, tail_out = _norm_bwd_specs(tm, d)

    def body(g_ref, w_ref, *tail):
        acc = jnp.zeros((tm, d), F32)
        for j in range(3):
            acc = acc + lax.dot_general(g_ref[j], w_ref[:, j * d:(j + 1) * d], _NT, preferred_element_type=F32)
        _norm_bwd_tail(acc, *tail)

    return pl.pallas_call(
        body, name=name, grid=(t // tm,),
        in_specs=[pl.BlockSpec((3, tm, d), lambda i: (0, i, 0)), _resident((d, 3 * d), lambda i: (0, 0))] + tail_in,
        out_specs=tail_out, out_shape=[S((t, d), F32), S((1, d), F32)], compiler_params=_params(1))(dbcx, w, h, gain, dh_in)


def linear_nt_norm_bwd(name, dy, w, h, gain, dh_in):
    t, n = dy.shape
    k = w.shape[0]
    tm = _tile(t, 512)
    tail_in, tail_out = _norm_bwd_specs(tm, k)

    def body(dy_ref, w_ref, *tail):
        _norm_bwd_tail(lax.dot_general(dy_ref[...].astype(BF16), w_ref[...], _NT, preferred_element_type=F32), *tail)

    return pl.pallas_call(
        body, name=name, grid=(t // tm,),
        in_specs=[pl.BlockSpec((tm, n), lambda i: (i, 0)), _resident((k, n), lambda i: (0, 0))] + tail_in,
        out_specs=tail_out, out_shape=[S((t, k), F32), S((1, k), F32)], compiler_params=_params(1))(dy, w, h, gain, dh_in)


def conv_in_wgrad(name, a, dbcx):
    t, d = a.shape
    tk = _tile(t, 512)
    return _mm(name, [(a, pl.BlockSpec((tk, d), lambda j, s: (s, 0)))], [(dbcx, pl.BlockSpec((None, tk, d), lambda j, s: (j, s, 0)))],
               [(0, 0, 0)], _TN, (3, t // tk), 1,
               [(S((d, 3 * d), BF16), pl.BlockSpec((d, d), lambda j, s: (0, j)))], (d, d), _store)[0]


def _rstd(x):
    return lax.rsqrt(jnp.mean(x * x, axis=-1, keepdims=True) + RMS_EPS)


def _rms_bwd(x, g, dy):
    r = _rstd(x)
    xhat = x * r
    dgain = jnp.sum(dy * xhat, axis=0, keepdims=True)
    dxh = dy * g
    dx = r * (dxh - xhat * jnp.mean(dxh * xhat, axis=-1, keepdims=True))
    return dx, dgain


def rms_fwd(name, h, g):
    t, d = h.shape
    tr = _tile(t, 512)

    def body(h_ref, g_ref, a_ref):
        x = h_ref[...]
        a_ref[...] = (x * _rstd(x) * g_ref[...]).astype(BF16)

    return pl.pallas_call(
        body, name=name, grid=(t // tr,),
        in_specs=[pl.BlockSpec((tr, d), lambda i: (i, 0)), pl.BlockSpec((1, d), lambda i: (0, 0))],
        out_specs=pl.BlockSpec((tr, d), lambda i: (i, 0)), out_shape=S((t, d), BF16), compiler_params=_params(1))(h, g)


def rms_bwd(name, h, g, da, dh_in):
    t, d = h.shape
    tr = _tile(t, 512)

    def body(h_ref, g_ref, da_ref, dhi_ref, dho_ref, dg_ref):
        dx, dgain = _rms_bwd(h_ref[...], g_ref[...], da_ref[...])
        dho_ref[...] = dhi_ref[...] + dx

        @pl.when(pl.program_id(0) == 0)
        def _():
            dg_ref[...] = jnp.zeros_like(dg_ref)

        dg_ref[...] += dgain

    row = pl.BlockSpec((tr, d), lambda i: (i, 0))
    vec = pl.BlockSpec((1, d), lambda i: (0, 0))
    return pl.pallas_call(
        body, name=name, grid=(t // tr,), in_specs=[row, vec, row, row], out_specs=[row, vec],
        out_shape=[S((t, d), F32), S((1, d), F32)], compiler_params=_params(1))(h, g, da, dh_in)


def loss_head(name, h, g, target):
    t, d = h.shape
    tr = _tile(t, 512)

    def body(h_ref, g_ref, t_ref, dh_ref, dg_ref, loss_ref):
        x = h_ref[...]
        g = g_ref[...]
        r = _rstd(x)
        xhat = x * r
        err = xhat * g - t_ref[...]
        dy = err * (1.0 / d)
        dxh = dy * g
        dh_ref[...] = r * (dxh - xhat * jnp.mean(dxh * xhat, axis=-1, keepdims=True))

        @pl.when(pl.program_id(0) == 0)
        def _():
            dg_ref[...] = jnp.zeros_like(dg_ref)
            loss_ref[...] = jnp.zeros_like(loss_ref)

        dg_ref[...] += jnp.sum(dy * xhat, axis=0, keepdims=True)
        per_token = jnp.mean(err * err, axis=-1, keepdims=True)
        loss_ref[...] += 0.5 * jnp.sum(per_token, axis=0, keepdims=True)

    row = pl.BlockSpec((tr, d), lambda i: (i, 0))
    vec = pl.BlockSpec((1, d), lambda i: (0, 0))
    one = pl.BlockSpec((1, 1), lambda i: (0, 0))
    return pl.pallas_call(
        body, name=name, grid=(t // tr,), in_specs=[row, vec, row], out_specs=[row, vec, one],
        out_shape=[S((t, d), F32), S((1, d), F32), S((1, 1), F32)], compiler_params=_params(1))(h, g, target)


def mla_mid(name, proj, g_cq, g_ckv, cos, sin):
    t, n = proj.shape
    ql, kl = g_cq.shape[1], g_ckv.shape[1]
    tr = _tile(t, 512)

    def body(p_ref, gq_ref, gk_ref, c_ref, s_ref, cq_ref, ckv_ref, kr_ref):
        xq = p_ref[:, 0:ql]
        cq_ref[...] = (xq * _rstd(xq) * gq_ref[...]).astype(BF16)
        xk = p_ref[:, ql:ql + kl]
        ckv_ref[...] = (xk * _rstd(xk) * gk_ref[...]).astype(BF16)
        k1 = p_ref[:, ql + kl:ql + kl + HALF]
        k2 = p_ref[:, ql + kl + HALF:ql + kl + ROPE]
        c = c_ref[...]
        s = s_ref[...]
        kr_ref[:, 0:HALF] = k1 * c - k2 * s
        kr_ref[:, HALF:ROPE] = k1 * s + k2 * c

    def row(w):
        return pl.BlockSpec((tr, w), lambda i: (i, 0))

    def vec(w):
        return pl.BlockSpec((1, w), lambda i: (0, 0))

    return pl.pallas_call(
        body, name=name, grid=(t // tr,), in_specs=[row(n), vec(ql), vec(kl), row(HALF), row(HALF)],
        out_specs=[row(ql), row(kl), row(ROPE)], out_shape=[S((t, ql), BF16), S((t, kl), BF16), S((t, ROPE), F32)],
        compiler_params=_params(1))(proj, g_cq, g_ckv, cos, sin)


def mla_mid_bwd(name, proj, g_cq, g_ckv, dcq, dckv, dkr, cos, sin):
    t, n = proj.shape
    ql, kl = g_cq.shape[1], g_ckv.shape[1]
    tr = _tile(t, 512)

    def body(p_ref, gq_ref, gk_ref, dcq_ref, dckv_ref, dkr_ref, c_ref, s_ref, dp_ref, dgq_ref, dgk_ref):
        dxq, dgq = _rms_bwd(p_ref[:, 0:ql], gq_ref[...], dcq_ref[...])
        dp_ref[:, 0:ql] = dxq.astype(BF16)
        dxk, dgk = _rms_bwd(p_ref[:, ql:ql + kl], gk_ref[...], dckv_ref[...])
        dp_ref[:, ql:ql + kl] = dxk.astype(BF16)
        d1 = dkr_ref[:, 0:HALF]
        d2 = dkr_ref[:, HALF:ROPE]
        c = c_ref[...]
        s = s_ref[...]
        dp_ref[:, ql + kl:ql + kl + HALF] = (d1 * c + d2 * s).astype(BF16)
        dp_ref[:, ql + kl + HALF:ql + kl + ROPE] = (d2 * c - d1 * s).astype(BF16)

        @pl.when(pl.program_id(0) == 0)
        def _():
            dgq_ref[...] = jnp.zeros_like(dgq_ref)
            dgk_ref[...] = jnp.zeros_like(dgk_ref)

        dgq_ref[...] += dgq
        dgk_ref[...] += dgk

    def row(w):
        return pl.BlockSpec((tr, w), lambda i: (i, 0))

    def vec(w):
        return pl.BlockSpec((1, w), lambda i: (0, 0))

    return pl.pallas_call(
        body, name=name, grid=(t // tr,),
        in_specs=[row(n), vec(ql), vec(kl), row(ql), row(kl), row(ROPE), row(HALF), row(HALF)],
        out_specs=[row(n), vec(ql), vec(kl)], out_shape=[S((t, n), BF16), S((1, ql), F32), S((1, kl), F32)],
        compiler_params=_params(1))(proj, g_cq, g_ckv, dcq, dckv, dkr, cos, sin)


def qkv_heads(name, q, kv, kr, cos, sin, shards=()):
    t = q.shape[0]
    tr = _tile(t, 256)
    n = len(shards)

    def body(q_ref, kv_ref, kr_ref, c_ref, s_ref, *refs):
        src = refs[:n]
        qo_ref, ko_ref, vo_ref = refs[n:n + 3]
        if n:
            _ride_along(gather_ici_copies(src, refs[n + 3:2 * n + 3], *refs[2 * n + 3:]), (pl.program_id(0),), (t // tr,))
        c = c_ref[...]
        s = s_ref[...]
        krb = kr_ref[...].astype(BF16)
        for h in range(N_HEADS):
            q0 = h * QK
            qo_ref[h, :, 0:NOPE] = q_ref[:, q0:q0 + NOPE].astype(BF16)
            q1 = q_ref[:, q0 + NOPE:q0 + NOPE + HALF]
            q2 = q_ref[:, q0 + NOPE + HALF:q0 + QK]
            qo_ref[h, :, NOPE:NOPE + HALF] = (q1 * c - q2 * s).astype(BF16)
            qo_ref[h, :, NOPE + HALF:QK] = (q1 * s + q2 * c).astype(BF16)
            k0 = h * (NOPE + VDIM)
            ko_ref[h, :, 0:NOPE] = kv_ref[:, k0:k0 + NOPE]
            ko_ref[h, :, NOPE:QK] = krb
            vo_ref[h] = kv_ref[:, k0 + NOPE:k0 + NOPE + VDIM]

    def row(w):
        return pl.BlockSpec((tr, w), lambda i: (i, 0))

    def heads(w):
        return pl.BlockSpec((N_HEADS, tr, w), lambda i: (0, i, 0))

    outs = pl.pallas_call(
        body, name=name, grid=(t // tr,),
        in_specs=[row(N_HEADS * QK), row(N_HEADS * (NOPE + VDIM)), row(ROPE), row(HALF), row(HALF)] + [ANY] * n,
        out_specs=[heads(QK), heads(QK), heads(VDIM)] + [ANY] * n,
        out_shape=[S((N_HEADS, t, QK), BF16), S((N_HEADS, t, QK), BF16), S((N_HEADS, t, VDIM), BF16)]
        + [S((N_CHIPS,) + s.shape, s.dtype) for s in shards],
        scratch_shapes=[pltpu.SemaphoreType.DMA((n, 3)), pltpu.SemaphoreType.DMA((n, 3))] if n else [],
        compiler_params=_params(1))(q, kv, kr, cos, sin, *shards)
    return outs[0], outs[1], outs[2], list(outs[3:])


def qkv_heads_bwd(name, dq_h, dk_h, dv_h, cos, sin):
    t = dq_h.shape[1]
    tr = _tile(t, 256)

    def body(dq_ref, dk_ref, dv_ref, c_ref, s_ref, q_ref, kv_ref, kr_ref):
        c = c_ref[...]
        s = s_ref[...]
        dkr = jnp.zeros((tr, ROPE), F32)
        for h in range(N_HEADS):
            q0 = h * QK
            q_ref[:, q0:q0 + NOPE] = dq_ref[h, :, 0:NOPE].astype(BF16)
            d1 = dq_ref[h, :, NOPE:NOPE + HALF]
            d2 = dq_ref[h, :, NOPE + HALF:QK]
            q_ref[:, q0 + NOPE:q0 + NOPE + HALF] = (d1 * c + d2 * s).astype(BF16)
            q_ref[:, q0 + NOPE + HALF:q0 + QK] = (d2 * c - d1 * s).astype(BF16)
            k0 = h * (NOPE + VDIM)
            kv_ref[:, k0:k0 + NOPE] = dk_ref[h, :, 0:NOPE].astype(BF16)
            kv_ref[:, k0 + NOPE:k0 + NOPE + VDIM] = dv_ref[h].astype(BF16)
            dkr = dkr + dk_ref[h, :, NOPE:QK]
        kr_ref[...] = dkr

    def row(w):
        return pl.BlockSpec((tr, w), lambda i: (i, 0))

    def heads(w):
        return pl.BlockSpec((N_HEADS, tr, w), lambda i: (0, i, 0))

    return pl.pallas_call(
        body, name=name, grid=(t // tr,),
        in_specs=[heads(QK), heads(QK), heads(VDIM), row(HALF), row(HALF)],
        out_specs=[row(N_HEADS * QK), row(N_HEADS * (NOPE + VDIM)), row(ROPE)],
        out_shape=[S((t, N_HEADS * QK), BF16), S((t, N_HEADS * (NOPE + VDIM)), BF16), S((t, ROPE), F32)],
        compiler_params=_params(1))(dq_h, dk_h, dv_h, cos, sin)


def _chunk_mask_t(q_start, k_start, bq, bk):
    kc = (k_start + lax.broadcasted_iota(jnp.int32, (bk, bq), 0)) // CHUNK
    qc = (q_start + lax.broadcasted_iota(jnp.int32, (bk, bq), 1)) // CHUNK
    return kc <= qc


def attention_fwd(name, q, k, v, shards=()):
    nh, t, _ = q.shape
    blk = ATT_BLOCK
    nq = t // blk
    n = len(shards)

    def body(q_ref, k_ref, v_ref, *refs):
        src = refs[:n]
        o_ref, lse_ref = refs[n:n + 2]
        dst = refs[n + 2:2 * n + 2]
        m_ref, l_ref, acc_ref, s_buf, p_buf, alpha_buf, bias_ref = refs[2 * n + 2:2 * n + 9]
        i = pl.program_id(1)
        if n:
            send_sems, recv_sems = refs[2 * n + 9:]
            _ride_along(gather_ici_copies(src, dst, send_sems, recv_sems), (pl.program_id(0), i), (nh, nq))

        @pl.when((pl.program_id(0) == 0) & (i == 0))
        def _():
            bias_ref[...] = jnp.where(_chunk_mask_t(0, 0, blk, blk), 0.0, MASK_VALUE)

        m_ref[...] = jnp.full_like(m_ref, MASK_VALUE)
        l_ref[...] = jnp.zeros_like(l_ref)
        acc_ref[...] = jnp.zeros_like(acc_ref)

        def rows(b):
            return pl.ds(pl.multiple_of(b * blk, blk), blk)

        def scores(b, slot):
            s_buf[slot] = lax.dot_general(k_ref[rows(b), :], q_ref[...], _NT, preferred_element_type=F32)

        def softmax(slot, diagonal):
            s = s_buf[slot]
            if diagonal:
                s = s + bias_ref[...]
            m_old = m_ref[...]
            m_new = jnp.maximum(m_old, jnp.max(s, axis=0, keepdims=True))
            p = jnp.exp2((s - m_new) * SCORE_SCALE_LOG2)
            alpha = jnp.exp2((m_old - m_new) * SCORE_SCALE_LOG2)
            l_ref[...] = alpha * l_ref[...] + jnp.sum(p, axis=0, keepdims=True)
            m_ref[...] = m_new
            alpha_buf[slot] = alpha
            p_buf[slot] = p.astype(BF16)

        def values(b, slot):
            pv = lax.dot_general(v_ref[rows(b), :], p_buf[slot], _TN, preferred_element_type=F32)
            acc_ref[...] = alpha_buf[slot] * acc_ref[...] + pv

        def step(t, slot):
            values(t - 2, slot)
            softmax(1 - slot, False)
            scores(t, slot)

        scores(0, 0)

        @pl.when(i == 0)
        def _():
            softmax(0, True)
            values(0, 0)

        @pl.when(i > 0)
        def _():
            scores(1, 1)
            softmax(0, False)
            steady = i - 1

            def pair(u, carry):
                step(2 + 2 * u, 0)
                step(3 + 2 * u, 1)
                return carry

            lax.fori_loop(0, steady // 2, pair, 0)

            @pl.when(steady % 2 == 1)
            def _():
                step(i, 0)

            last = i % 2
            softmax(last, True)
            values(i - 1, 1 - last)
            values(i, last)

        l = l_ref[...]
        o_ref[...] = (acc_ref[...] / l).T
        lse_ref[...] = m_ref[...] * SCORE_SCALE + jnp.log(l)

    outs = pl.pallas_call(
        body, name=name, grid=(nh, nq),
        in_specs=[pl.BlockSpec((None, blk, QK), lambda h, i: (h, i, 0)), pl.BlockSpec((None, t, QK), lambda h, i: (h, 0, 0)),
                  pl.BlockSpec((None, t, VDIM), lambda h, i: (h, 0, 0))] + [ANY] * n,
        out_specs=[pl.BlockSpec((blk, VDIM), lambda h, i: (i, h)),
                   pl.BlockSpec((None, None, 1, blk), lambda h, i: (h, i, 0, 0))] + [ANY] * n,
        out_shape=[S((t, nh * VDIM), F32), S((nh, nq, 1, blk), F32)] + [S((N_CHIPS,) + s.shape, s.dtype) for s in shards],
        scratch_shapes=[pltpu.VMEM((1, blk), F32), pltpu.VMEM((1, blk), F32), pltpu.VMEM((VDIM, blk), F32),
                        pltpu.VMEM((2, blk, blk), F32), pltpu.VMEM((2, blk, blk), BF16), pltpu.VMEM((2, 1, blk), F32),
                        pltpu.VMEM((blk, blk), F32)]
        + ([pltpu.SemaphoreType.DMA((n, 3)), pltpu.SemaphoreType.DMA((n, 3))] if n else []),
        compiler_params=_params(2))(q, k, v, *shards)
    return outs[0], outs[1], list(outs[2:])


def attention_delta(name, do, o):
    t = do.shape[0]
    blk = ATT_BLOCK

    def body(do_ref, o_ref, d_ref):
        for h in range(N_HEADS):
            cols = slice(h * VDIM, (h + 1) * VDIM)
            d_ref[h] = jnp.sum((do_ref[:, cols] * o_ref[:, cols]).T, axis=0, keepdims=True)

    tile = pl.BlockSpec((blk, N_HEADS * VDIM), lambda i: (i, 0))
    return pl.pallas_call(
        body, name=name, grid=(t // blk,), in_specs=[tile, tile],
        out_specs=pl.BlockSpec((N_HEADS, None, 1, blk), lambda i: (0, i, 0, 0)), out_shape=S((N_HEADS, t // blk, 1, blk), F32),
        compiler_params=_params(1))(do, o)


def attention_bwd(name, q, k, v, do, lse, delta, parts=()):
    nh, t, _ = q.shape
    blk = ATT_BLOCK
    nq = t // blk
    n_pairs = nq * (nq + 1) // 2
    n = len(parts)
    scale = SCORE_SCALE

    def body(q_ref, k_ref, v_ref, do_ref, lse_ref, dl_ref, *refs):
        src = refs[:n]
        dq_ref, dk_ref, dv_ref = refs[n:n + 3]
        dst = refs[n + 3:2 * n + 3]
        s_buf, dp_buf, p_buf, ds_buf, bias_ref = refs[2 * n + 3:2 * n + 8]
        if n:
            send_sems, recv_sems = refs[2 * n + 8:]
            _ride_along(scatter_ici_copies(src, dst, send_sems, recv_sems), (pl.program_id(0),), (nh,))

        @pl.when(pl.program_id(0) == 0)
        def _():
            bias_ref[...] = jnp.where(_chunk_mask_t(0, 0, blk, blk), 0.0, MASK_VALUE)

        dq_ref[...] = jnp.zeros_like(dq_ref)
        dk_ref[...] = jnp.zeros_like(dk_ref)
        dv_ref[...] = jnp.zeros_like(dv_ref)

        def rows(x):
            return pl.ds(pl.multiple_of(x * blk, blk), blk)

        def after(jb):
            j, b = jb
            wrap = b == nq - 1 - j
            return jnp.where(wrap, j + 1, j), jnp.where(wrap, 0, b + 1)

        def products(jb, slot):
            j, b = jb
            s_buf[slot] = lax.dot_general(k_ref[rows(j), :], q_ref[rows(j + b), :], _NT, preferred_element_type=F32)
            dp_buf[slot] = lax.dot_general(v_ref[rows(j), :], do_ref[rows(j + b), :].astype(BF16), _NT, preferred_element_type=F32)

        def softmax_bwd(jb, slot):
            j, b = jb
            s = s_buf[slot] + bias_ref[...] * (b == 0).astype(F32)
            p = jnp.exp2(s * SCORE_SCALE_LOG2 - lse_ref[j + b] * LOG2_E)
            p_buf[slot] = p.astype(BF16)
            ds_buf[slot] = (p * (dp_buf[slot] - dl_ref[j + b]) * scale).astype(BF16)

        def gradients(jb, slot):
            j, b = jb
            dv_ref[rows(j), :] += jnp.dot(p_buf[slot], do_ref[rows(j + b), :].astype(BF16), preferred_element_type=F32)
            dk_ref[rows(j), :] += jnp.dot(ds_buf[slot], q_ref[rows(j + b), :], preferred_element_type=F32)
            dq_ref[rows(j + b), :] += lax.dot_general(ds_buf[slot], k_ref[rows(j), :], _TN, preferred_element_type=F32)

        def step(state, slot):
            third, second, first = state
            gradients(third, slot)
            softmax_bwd(second, 1 - slot)
            products(first, slot)
            return second, first, after(first)

        zero = jnp.int32(0)
        pair0 = (zero, zero)
        products(pair0, 0)
        if n_pairs == 1:
            softmax_bwd(pair0, 0)
            gradients(pair0, 0)
        else:
            pair1 = after(pair0)
            products(pair1, 1)
            softmax_bwd(pair0, 0)
            steady = n_pairs - 2
            state = lax.fori_loop(0, steady // 2, lambda u, st: step(step(st, 0), 1), (pair0, pair1, after(pair1)))
            if steady % 2:
                state = step(state, 0)
            before_last, last_pair, _ = state
            last = (n_pairs - 1) % 2
            softmax_bwd(last_pair, last)
            gradients(before_last, 1 - last)
            gradients(last_pair, last)

    head = lambda w: pl.BlockSpec((None, t, w), lambda h: (h, 0, 0))
    stats = pl.BlockSpec((None, nq, 1, blk), lambda h: (h, 0, 0, 0))
    outs = pl.pallas_call(
        body, name=name, grid=(nh,),
        in_specs=[head(QK), head(QK), head(VDIM), pl.BlockSpec((t, VDIM), lambda h: (0, h)), stats, stats] + [ANY] * n,
        out_specs=[head(QK), head(QK), head(VDIM)] + [ANY] * n,
        out_shape=[S((nh, t, QK), F32), S((nh, t, QK), F32), S((nh, t, VDIM), F32)] + [S(p.shape, p.dtype) for p in parts],
        scratch_shapes=[pltpu.VMEM((2, blk, blk), F32), pltpu.VMEM((2, blk, blk), F32), pltpu.VMEM((2, blk, blk), BF16),
                        pltpu.VMEM((2, blk, blk), BF16), pltpu.VMEM((blk, blk), F32)]
        + ([pltpu.SemaphoreType.DMA((n, 3)), pltpu.SemaphoreType.DMA((n, 3))] if n else []),
        compiler_params=_params(1, VMEM_LIMIT_WHOLE_HEAD))(q, k, v, do, lse, delta, *parts)
    return outs[0], outs[1], outs[2], list(outs[3:])


def _shift_down(u, s):
    rows = lax.broadcasted_iota(jnp.int32, u.shape, 0)
    return jnp.where(rows >= s, pltpu.roll(u, s, 0), 0.0)


def _shift_up(u, s):
    n = u.shape[0]
    rows = lax.broadcasted_iota(jnp.int32, u.shape, 0)
    return jnp.where(rows < n - s, pltpu.roll(u, n - s, 0), 0.0)


def _conv_specs(t, d, lanes):
    slab = lambda part: pl.BlockSpec((None, t, lanes), lambda j, part=part: (part, 0, j))
    return slab, pl.BlockSpec((3, lanes), lambda j: (0, j)), pl.BlockSpec((t, lanes), lambda j: (0, j))


def conv_fwd(name, bcx, w):
    _, t, d = bcx.shape
    lanes = _tile(d, 128, 128)
    slab, w_spec, col = _conv_specs(t, d, lanes)

    def body(b_ref, c_ref, x_ref, w_ref, y_ref):
        u = c_ref[...] * x_ref[...]
        uc = w_ref[0:1, :] * _shift_down(u, 2) + w_ref[1:2, :] * _shift_down(u, 1) + w_ref[2:3, :] * u
        y_ref[...] = (b_ref[...] * uc).astype(BF16)

    return pl.pallas_call(
        body, name=name, grid=(d // lanes,), in_specs=[slab(0), slab(1), slab(2), w_spec], out_specs=col,
        out_shape=S((t, d), BF16), compiler_params=_params(1))(bcx, bcx, bcx, w)


def conv_bwd(name, bcx, w, dy):
    _, t, d = bcx.shape
    lanes = _tile(d, 128, 128)
    slab, w_spec, col = _conv_specs(t, d, lanes)

    def body(b_ref, c_ref, x_ref, w_ref, dy_ref, d_ref, dw_ref):
        c = c_ref[...]
        x = x_ref[...]
        dyv = dy_ref[...]
        u = c * x
        u1 = _shift_down(u, 1)
        u2 = _shift_down(u, 2)
        w0, w1, w2 = w_ref[0:1, :], w_ref[1:2, :], w_ref[2:3, :]
        d_ref[0] = (dyv * (w0 * u2 + w1 * u1 + w2 * u)).astype(BF16)
        duc = dyv * b_ref[...]
        dw_ref[0:1, :] = jnp.sum(duc * u2, axis=0, keepdims=True)
        dw_ref[1:2, :] = jnp.sum(duc * u1, axis=0, keepdims=True)
        dw_ref[2:3, :] = jnp.sum(duc * u, axis=0, keepdims=True)
        du = w2 * duc + w1 * _shift_up(duc, 1) + w0 * _shift_up(duc, 2)
        d_ref[1] = (du * x).astype(BF16)
        d_ref[2] = (du * c).astype(BF16)

    return pl.pallas_call(
        body, name=name, grid=(d // lanes,), in_specs=[slab(0), slab(1), slab(2), w_spec, col],
        out_specs=[pl.BlockSpec((3, t, lanes), lambda j: (0, 0, j)), w_spec], out_shape=[S((3, t, d), BF16), S((3, d), F32)],
        compiler_params=_params(1))(bcx, bcx, bcx, w, dy)


def adamw(name, w, g, m, v):
    r, c = w.shape
    tr = _tile(r, 512)

    def body(w_ref, g_ref, m_ref, v_ref, d_ref, mo_ref, vo_ref):
        gv = g_ref[...]
        m_new = ADAM_B1 * m_ref[...] + (1.0 - ADAM_B1) * gv
        v_new = ADAM_B2 * v_ref[...] + (1.0 - ADAM_B2) * (gv * gv)
        m_hat = m_new / (1.0 - ADAM_B1 ** ADAM_STEP)
        v_hat = v_new / (1.0 - ADAM_B2 ** ADAM_STEP)
        d_ref[...] = -ADAM_LR * (m_hat / (jnp.sqrt(v_hat) + ADAM_EPS) + ADAM_WD * w_ref[...])
        mo_ref[...] = m_new
        vo_ref[...] = v_new

    blk = pl.BlockSpec((tr, c), lambda i: (i, 0))
    return pl.pallas_call(
        body, name=name, grid=(r // tr,), in_specs=[blk] * 4, out_specs=[blk] * 3, out_shape=[S((r, c), F32)] * 3,
        compiler_params=_params(1))(w, g, m, v)


def _place():
    x, y, c = lax.axis_index("x"), lax.axis_index("y"), lax.axis_index("c")
    other_chips = [(1 - x, y), (x, 1 - y), (1 - x, 1 - y)]
    return x, y, c, other_chips


def _half(c, rows):
    return pl.ds(pl.multiple_of(c * (rows // 2), 16), rows // 2)


def gather_weight_shards(shards):
    n = len(shards)

    def body(*refs):
        src = refs[:n]
        dst = refs[n:2 * n]
        send_sems, recv_sems = refs[2 * n:]
        x, y, c, chips = _place()
        me = 2 * x + y
        sibling = (x, y, 1 - c)

        def copy(i, slot, half_of, sem, to, from_input=False):
            rows = _half(half_of, src[i].shape[0])
            return pltpu.make_async_remote_copy(
                src_ref=src[i].at[rows] if from_input else dst[i].at[slot, rows], dst_ref=dst[i].at[slot, rows],
                send_sem=send_sems.at[i, sem], recv_sem=recv_sems.at[i, sem], device_id=to, device_id_type=MESH)

        sent = []
        for i in range(n):
            for j, chip in enumerate(chips):
                sent.append(copy(i, me, c, j, (*chip, c), from_input=True))
                sent[-1].start()
        for i in range(n):
            for j, (px, py) in enumerate(chips):
                copy(i, 2 * px + py, c, j, sibling).wait_recv()
                sent.append(copy(i, 2 * px + py, c, 3 + j, sibling))
                sent[-1].start()
        for i in range(n):
            for j, (px, py) in enumerate(chips):
                copy(i, 2 * px + py, 1 - c, 3 + j, sibling).wait_recv()
        for cp in sent:
            cp.wait_send()

    outs = pl.pallas_call(
        body, name="gather_weight_shards", in_specs=[ANY] * n, out_specs=[ANY] * n,
        out_shape=[S((N_CHIPS,) + s.shape, s.dtype) for s in shards],
        scratch_shapes=[pltpu.SemaphoreType.DMA((n, 6)), pltpu.SemaphoreType.DMA((n, 6))],
    )(*shards)
    return _fill_own_slot(outs, [s[None] for s in shards])


def gather_ici_copies(src, dst, send_sems, recv_sems):
    x, y, c, chips = _place()
    me = 2 * x + y
    pairs = []
    for i in range(len(src)):
        rows = _half(c, src[i].shape[0])
        for j, (px, py) in enumerate(chips):
            def copy(slot):
                return pltpu.make_async_remote_copy(
                    src_ref=src[i].at[rows], dst_ref=dst[i].at[slot, rows], send_sem=send_sems.at[i, j],
                    recv_sem=recv_sems.at[i, j], device_id=(px, py, c), device_id_type=MESH)
            pairs.append((copy(me), copy(2 * px + py)))
    return pairs


def scatter_ici_copies(src, dst, send_sems, recv_sems):
    x, y, c, chips = _place()
    me = 2 * x + y
    pairs = []
    for i in range(len(src)):
        for j, (px, py) in enumerate(chips):
            def copy(from_slot, to_slot):
                return pltpu.make_async_remote_copy(
                    src_ref=src[i].at[from_slot], dst_ref=dst[i].at[to_slot], send_sem=send_sems.at[i, j],
                    recv_sem=recv_sems.at[i, j], device_id=(px, py, c), device_id_type=MESH)
            pairs.append((copy(2 * px + py, me), copy(me, 2 * px + py)))
    return pairs


def _ride_along(pairs, grid_ids, grid_sizes):
    first = grid_ids[0] == 0
    last = grid_ids[0] == grid_sizes[0] - 1
    for g, size in zip(grid_ids[1:], grid_sizes[1:]):
        first = first & (g == 0)
        last = last & (g == size - 1)

    @pl.when(first)
    def _():
        for outgoing, _ in pairs:
            outgoing.start()

    @pl.when(last)
    def _():
        for _, incoming in pairs:
            incoming.wait_recv()
        for outgoing, _ in pairs:
            outgoing.wait_send()


def _fill_own_slot(gathered, own):
    me = 2 * lax.axis_index("x") + lax.axis_index("y")
    return [lax.dynamic_update_slice(g, o, (me,) + (0,) * (g.ndim - 1)) for g, o in zip(gathered, own)]


def forward_to_sibling(gathered):
    n = len(gathered)

    def body(*refs):
        src = refs[:n]
        dst = refs[n:2 * n]
        send_sems, recv_sems = refs[2 * n:]
        x, y, c, chips = _place()
        pairs = []
        for i in range(n):
            for j, (px, py) in enumerate(chips):
                def copy(half_of):
                    rows = _half(half_of, src[i].shape[1])
                    return pltpu.make_async_remote_copy(
                        src_ref=src[i].at[2 * px + py, rows], dst_ref=dst[i].at[2 * px + py, rows], send_sem=send_sems.at[i, j],
                        recv_sem=recv_sems.at[i, j], device_id=(x, y, 1 - c), device_id_type=MESH)
                pairs.append((copy(c), copy(1 - c)))
        for outgoing, _ in pairs:
            outgoing.start()
        for _, incoming in pairs:
            incoming.wait_recv()
        for outgoing, _ in pairs:
            outgoing.wait_send()

    return pl.pallas_call(
        body, name="forward_to_sibling", in_specs=[ANY] * n, out_specs=[ANY] * n,
        out_shape=[S(g.shape, g.dtype) for g in gathered], input_output_aliases={i: i for i in range(n)},
        scratch_shapes=[pltpu.SemaphoreType.DMA((n, 3)), pltpu.SemaphoreType.DMA((n, 3))],
    )(*gathered)


def sibling_swap_halves(name, grads):
    n = len(grads)

    def body(*refs):
        src = refs[:n]
        dst = refs[n:2 * n]
        send_sems, recv_sems = refs[2 * n:]
        x, y, c, _ = _place()
        copies = [pltpu.make_async_remote_copy(
            src_ref=src[i].at[:, _half(1 - c, src[i].shape[1]), :], dst_ref=dst[i], send_sem=send_sems.at[i],
            recv_sem=recv_sems.at[i], device_id=(x, y, 1 - c), device_id_type=MESH) for i in range(n)]
        for cp in copies:
            cp.start()
        for cp in copies:
            cp.wait()

    return pl.pallas_call(
        body, name=name, in_specs=[ANY] * n, out_specs=[ANY] * n,
        out_shape=[S((g.shape[0], g.shape[1] // 2, g.shape[2]), g.dtype) for g in grads],
        scratch_shapes=[pltpu.SemaphoreType.DMA((n,)), pltpu.SemaphoreType.DMA((n,))],
    )(*grads)


def add_halves(name, g, rx):
    _, r, cdim = g.shape
    r2 = r // 2
    tr = _tile(r2, 512, 16)
    nb = r2 // tr

    def body(lo_ref, hi_ref, rx_ref, o_ref):
        mine = jnp.where(lax.axis_index("c") == 0, lo_ref[...], hi_ref[...])
        o_ref[...] = (mine.astype(F32) + rx_ref[...].astype(F32)).astype(BF16)

    half = pl.BlockSpec((None, tr, cdim), lambda k, i: (k, i, 0))
    return pl.pallas_call(
        body, name=name, grid=(N_CHIPS, nb),
        in_specs=[half, pl.BlockSpec((None, tr, cdim), lambda k, i: (k, nb + i, 0)), half],
        out_specs=half, out_shape=S((N_CHIPS, r2, cdim), BF16), compiler_params=_params(2))(g, g, rx)


def scatter_to_owner_chips(parts):
    n = len(parts)

    def body(*refs):
        src = refs[:n]
        dst = refs[n:2 * n]
        send_sems, recv_sems = refs[2 * n:]
        pairs = scatter_ici_copies(src, dst, send_sems, recv_sems)
        for outgoing, _ in pairs:
            outgoing.start()
        for _, incoming in pairs:
            incoming.wait_recv()
        for outgoing, _ in pairs:
            outgoing.wait_send()

    return pl.pallas_call(
        body, name="scatter_to_owner_chips", in_specs=[ANY] * n, out_specs=[ANY] * n,
        out_shape=[S(p.shape, p.dtype) for p in parts],
        scratch_shapes=[pltpu.SemaphoreType.DMA((n, 3)), pltpu.SemaphoreType.DMA((n, 3))],
    )(*parts)


def _own_slots(parts):
    me = 2 * lax.axis_index("x") + lax.axis_index("y")
    return [lax.dynamic_slice(p, (me, 0, 0), (1,) + p.shape[1:]) for p in parts]


def sum_chips(name, parts):
    _, r2, cdim = parts.shape
    tr = _tile(r2, 512, 16)

    def body(p_ref, o_ref):
        acc = p_ref[0].astype(F32)
        for k in range(1, N_CHIPS):
            acc = acc + p_ref[k].astype(F32)
        o_ref[...] = acc

    return pl.pallas_call(
        body, name=name, grid=(r2 // tr,), in_specs=[pl.BlockSpec((N_CHIPS, tr, cdim), lambda i: (0, i, 0))],
        out_specs=pl.BlockSpec((tr, cdim), lambda i: (i, 0)), out_shape=S((r2, cdim), F32), compiler_params=_params(1))(parts)


def sibling_join_halves(name, halves, targets, where):
    n = len(halves)

    def rows_of(i, half_of):
        r2 = halves[i].shape[0]
        return pl.ds(pl.multiple_of(where[i][1] + half_of * r2, 8), r2)

    def body(*refs):
        src = refs[:n]
        dst = refs[n:n + len(targets)]
        send_sems, recv_sems = refs[n + len(targets):]
        x, y, c, _ = _place()

        def copy(i, half_of):
            return pltpu.make_async_remote_copy(
                src_ref=src[i], dst_ref=dst[where[i][0]].at[rows_of(i, half_of)], send_sem=send_sems.at[i],
                recv_sem=recv_sems.at[i], device_id=(x, y, 1 - c), device_id_type=MESH)

        for i in range(n):
            copy(i, c).start()
        for i in range(n):
            copy(i, 1 - c).wait_recv()
        for i in range(n):
            copy(i, c).wait_send()

    outs = list(pl.pallas_call(
        body, name=name, in_specs=[ANY] * n, out_specs=[ANY] * len(targets), out_shape=[S(tg, F32) for tg in targets],
        scratch_shapes=[pltpu.SemaphoreType.DMA((n,)), pltpu.SemaphoreType.DMA((n,))],
    )(*halves))
    c = lax.axis_index("c")
    for i, h in enumerate(halves):
        tgt, first = where[i]
        outs[tgt] = lax.dynamic_update_slice(outs[tgt], h, (first + c * h.shape[0], 0))
    return outs


def all_reduce_small(name, packed):
    rows, width = packed.shape

    def body(x_ref, o_ref, gathered, send_sems, recv_sems):
        x, y, c, _ = _place()
        me = 4 * x + 2 * y + c
        gathered[me] = x_ref[...]
        flips = [(fx, fy, fc) for fx in (0, 1) for fy in (0, 1) for fc in (0, 1)][1:]

        def copy(r, slot, to):
            return pltpu.make_async_remote_copy(
                src_ref=x_ref, dst_ref=gathered.at[slot], send_sem=send_sems.at[r], recv_sem=recv_sems.at[r],
                device_id=to, device_id_type=MESH)

        def peer(f):
            return (x ^ f[0], y ^ f[1], c ^ f[2])

        sent = [copy(r, me, peer(f)) for r, f in enumerate(flips)]
        for cp in sent:
            cp.start()
        for r, f in enumerate(flips):
            px, py, pc = peer(f)
            copy(r, 4 * px + 2 * py + pc, peer(f)).wait_recv()
        for cp in sent:
            cp.wait_send()
        acc = gathered[0]
        for k in range(1, N_DEV):
            acc = acc + gathered[k]
        o_ref[...] = acc

    vmem = pl.BlockSpec(memory_space=pltpu.VMEM)
    return pl.pallas_call(
        body, name=name, in_specs=[vmem], out_specs=vmem, out_shape=S((rows, width), F32),
        scratch_shapes=[pltpu.VMEM((N_DEV, rows, width), F32), pltpu.SemaphoreType.DMA((N_DEV - 1,)),
                        pltpu.SemaphoreType.DMA((N_DEV - 1,))],
    )(packed)


def _rope_tables(positions):
    inv_freq = 1.0 / (ROPE_THETA ** (jnp.arange(0, ROPE, 2, dtype=F32) / ROPE))
    ang = positions.astype(F32)[:, None] * inv_freq
    return jnp.cos(ang), jnp.sin(ang)


def _unstack_cols(w):
    k4, k, n4 = w.shape
    return jnp.transpose(w, (1, 0, 2)).reshape(k, k4 * n4)


def _stack_cols(w):
    k, n = w.shape
    return jnp.transpose(w.reshape(k, N_CHIPS, n // N_CHIPS), (1, 0, 2))


def kernel(x, positions, mla_norm, mla_w_in, mla_g_cq, mla_g_ckv, mla_w_uq, mla_w_ukv, mla_w_o, conv_norm, conv_w_in, conv_w, conv_w_out, ffn_norm, ffn_w_gate, ffn_w_up, ffn_w_down, final_norm, loss_target, m_mla_norm, m_mla_w_in, m_mla_g_cq, m_mla_g_ckv, m_mla_w_uq, m_mla_w_ukv, m_mla_w_o, m_conv_norm, m_conv_w_in, m_conv_w, m_conv_w_out, m_ffn_norm, m_ffn_w_gate, m_ffn_w_up, m_ffn_w_down, m_final_norm, v_mla_norm, v_mla_w_in, v_mla_g_cq, v_mla_g_ckv, v_mla_w_uq, v_mla_w_ukv, v_mla_w_o, v_conv_norm, v_conv_w_in, v_conv_w, v_conv_w_out, v_ffn_norm, v_ffn_w_gate, v_ffn_w_up, v_ffn_w_down, v_final_norm):
    weights = dict(mla_norm=mla_norm, mla_w_in=mla_w_in, mla_g_cq=mla_g_cq, mla_g_ckv=mla_g_ckv, mla_w_uq=mla_w_uq,
                   mla_w_ukv=mla_w_ukv, mla_w_o=mla_w_o, conv_norm=conv_norm, conv_w_in=conv_w_in, conv_w=conv_w,
                   conv_w_out=conv_w_out, ffn_norm=ffn_norm, ffn_w_gate=ffn_w_gate, ffn_w_up=ffn_w_up,
                   ffn_w_down=ffn_w_down, final_norm=final_norm)
    m_in = dict(mla_norm=m_mla_norm, mla_w_in=m_mla_w_in, mla_g_cq=m_mla_g_cq, mla_g_ckv=m_mla_g_ckv, mla_w_uq=m_mla_w_uq,
                mla_w_ukv=m_mla_w_ukv, mla_w_o=m_mla_w_o, conv_norm=m_conv_norm, conv_w_in=m_conv_w_in, conv_w=m_conv_w,
                conv_w_out=m_conv_w_out, ffn_norm=m_ffn_norm, ffn_w_gate=m_ffn_w_gate, ffn_w_up=m_ffn_w_up,
                ffn_w_down=m_ffn_w_down, final_norm=m_final_norm)
    v_in = dict(mla_norm=v_mla_norm, mla_w_in=v_mla_w_in, mla_g_cq=v_mla_g_cq, mla_g_ckv=v_mla_g_ckv, mla_w_uq=v_mla_w_uq,
                mla_w_ukv=v_mla_w_ukv, mla_w_o=v_mla_w_o, conv_norm=v_conv_norm, conv_w_in=v_conv_w_in, conv_w=v_conv_w,
                conv_w_out=v_conv_w_out, ffn_norm=v_ffn_norm, ffn_w_gate=v_ffn_w_gate, ffn_w_up=v_ffn_w_up,
                ffn_w_down=v_ffn_w_down, final_norm=v_final_norm)
    big = ["mla_w_in", "mla_w_uq", "mla_w_ukv", "mla_w_o", "conv_w_in", "conv_w_out", "ffn_w_gate", "ffn_w_up", "ffn_w_down"]
    order = list(weights)

    t, d = x.shape[1], x.shape[2]
    h0 = x.reshape(t, d)
    target = loss_target.reshape(t, d)
    cos, sin = _rope_tables(positions.reshape(t))

    def rows2d(a):
        return a.reshape(-1, a.shape[-1])

    first, later = big[:4], big[4:]
    shards = {n: rows2d(weights[n]).astype(BF16) for n in big}
    gathered = dict(zip(first, gather_weight_shards([shards[n] for n in first])))
    w_in = gathered["mla_w_in"].reshape(-1, gathered["mla_w_in"].shape[-1])
    w_uq = _unstack_cols(gathered["mla_w_uq"])
    w_ukv = _unstack_cols(gathered["mla_w_ukv"])
    w_o = gathered["mla_w_o"].reshape(-1, d)

    chip = 2 * lax.axis_index("x") + lax.axis_index("y")
    core = lax.axis_index("c")
    d4 = d // N_CHIPS
    first_core = (core == 0).astype(F32)

    def place_shard(shard):
        full = jnp.zeros((shard.shape[0], d), F32)
        return lax.dynamic_update_slice(full, shard * first_core, (0, chip * d4))

    def pack_rows(rows):
        idx = lax.broadcasted_iota(jnp.int32, (SMALL_ROWS, d), 0)
        out = jnp.zeros((SMALL_ROWS, d), F32)
        for r, row in enumerate(rows):
            out = out + jnp.where(idx == r, row, 0.0)
        return out

    cw = place_shard(conv_w.reshape(3, d4))
    pre = all_reduce_small("all_gather_conv_small", pack_rows([place_shard(conv_norm.reshape(1, d4)), cw[0:1], cw[1:2], cw[2:3]]))
    conv_norm_full = pre[0:1]
    conv_w_full = pre[1:4]

    a0 = rms_fwd("mla_norm_fwd", h0, mla_norm)
    proj = linear("mla_in_proj", a0, w_in, F32)
    cq, ckv, kr = mla_mid("mla_mid", proj, mla_g_cq, mla_g_ckv, cos, sin)
    q = linear("mla_q_up", cq, w_uq, F32)
    kv = linear("mla_kv_up", ckv, w_ukv, BF16)
    qh, kh, vh, conv_arriving = qkv_heads("qkv_heads", q, kv, kr, cos, sin, [shards[n] for n in later[:2]])
    attn, lse, ffn_arriving = attention_fwd("attention_fwd", qh, kh, vh, [shards[n] for n in later[2:]])
    gathered.update(zip(later, _fill_own_slot(forward_to_sibling(conv_arriving + ffn_arriving), [shards[n][None] for n in later])))
    cw_in = _unstack_cols(gathered["conv_w_in"])
    cw_out = gathered["conv_w_out"].reshape(-1, d)
    wg_all, wu_all, wd_all = gathered["ffn_w_gate"], gathered["ffn_w_up"], gathered["ffn_w_down"]
    h1, a1 = linear("mla_out_proj", attn, w_o, F32, resid=h0, next_gain=ffn_norm[0:1])

    def ffn_forward(tag, h, a, layer, next_gain):
        g, u, z = ffn_up(f"ffn{tag}_up", a, wg_all, wu_all, layer)
        return g, u, z, ffn_down(f"ffn{tag}_down", z, wd_all, layer, h, next_gain)

    g0, u0, z0, (h2, a2) = ffn_forward(0, h1, a1, 0, conv_norm_full)
    bcx = conv_in_proj("conv_in_proj", a2, cw_in)
    yc = conv_fwd("conv_fwd", bcx, conv_w_full)
    h3, a3 = linear("conv_out_proj", yc, cw_out, F32, resid=h2, next_gain=ffn_norm[1:2])
    g1, u1, z1, h4 = ffn_forward(1, h3, a3, 1, None)
    dh4, d_final_norm, loss_local = loss_head("loss_head", h4, final_norm.reshape(1, d), target)

    def ffn_backward(tag, dh, h, layer, a, g, u, z):
        dg, du = ffn_bwd_hidden(f"ffn{tag}_bwd_hidden", dh, wd_all, layer, g, u)
        d_wd = ffn_wgrad_down(f"ffn{tag}_wgrad_down", z, dh)
        dh_prev, d_norm = ffn_bwd_input(f"ffn{tag}_bwd_input", dg, du, wg_all, wu_all, layer, h, ffn_norm[layer:layer + 1], dh)
        d_wg, d_wu = ffn_wgrad_up(f"ffn{tag}_wgrad_up", a, dg, du)
        return dh_prev, d_norm, [d_wg, d_wu, d_wd]

    def reduce_to_pair_sums(tag, local):
        from_sibling = sibling_swap_halves(f"sibling_swap_{tag}", local)
        return [add_halves(f"pair_sum_{tag}{i}", g, r) for i, (g, r) in enumerate(zip(local, from_sibling))]

    def reduce_from_chips(tag, pair_sums, arrived, targets, where):
        from_chips = _fill_own_slot(arrived, _own_slots(pair_sums))
        my_halves = [sum_chips(f"chip_sum_{tag}{i}", p) for i, p in enumerate(from_chips)]
        return sibling_join_halves(f"sibling_join_{tag}", my_halves, targets, where)

    dh3, d_ffn_norm1, ffn1_grads = ffn_backward(1, dh4, h3, 1, a3, g1, u1, z1)

    dyc = linear_nt("conv_out_bwd_input", dh3, cw_out, F32)
    d_cw_out = wgrad("conv_out_wgrad", yc, dh3)
    dbcx, d_conv_w = conv_bwd("conv_bwd", bcx, conv_w_full, dyc)
    dh2, d_conv_norm = conv_in_bwd_input("conv_in_bwd_input", dbcx, cw_in, h2, conv_norm_full, dh3)
    d_cw_in = conv_in_wgrad("conv_in_wgrad", a2, dbcx)

    dh1, d_ffn_norm0, ffn0_grads = ffn_backward(0, dh2, h1, 0, a1, g0, u0, z0)

    d_attn = linear_nt("mla_out_bwd_input", dh1, w_o, F32)
    d_w_o = wgrad("mla_out_wgrad", attn, dh1)
    rest_pairs = reduce_to_pair_sums("rest", [_stack_cols(d_cw_in), d_cw_out.reshape(N_CHIPS, -1, d)] + ffn1_grads + ffn0_grads
                                     + [d_w_o.reshape(N_CHIPS, -1, d)])
    delta = attention_delta("attention_delta", d_attn, attn)
    dqh, dkh, dvh, rest_arrived = attention_bwd("attention_bwd", qh, kh, vh, d_attn, lse, delta, rest_pairs)
    dq, dkv, dkr = qkv_heads_bwd("qkv_heads_bwd", dqh, dkh, dvh, cos, sin)
    dcq = linear_nt("mla_q_up_bwd_input", dq, w_uq, F32)
    d_w_uq = wgrad("mla_q_up_wgrad", cq, dq)
    dckv = linear_nt("mla_kv_up_bwd_input", dkv, w_ukv, F32)
    d_w_ukv = wgrad("mla_kv_up_wgrad", ckv, dkv)
    dproj, d_g_cq, d_g_ckv = mla_mid_bwd("mla_mid_bwd", proj, mla_g_cq, mla_g_ckv, dcq, dckv, dkr, cos, sin)
    d_w_in = wgrad("mla_in_wgrad", a0, dproj)
    grad_x, d_mla_norm = linear_nt_norm_bwd("mla_in_bwd_input", dproj, w_in, h0, mla_norm, dh1)

    def shard_shape(n):
        return rows2d(weights[n]).shape

    rd, rf = ffn0_grads[0].shape[1], ffn0_grads[2].shape[1]
    rest_where = [(0, 0), (1, 0), (2, rd), (3, rd), (4, rf), (2, 0), (3, 0), (4, 0), (5, 0)]
    rest_names = later + ["mla_w_o"]
    grads = dict(zip(rest_names, reduce_from_chips("rest", rest_pairs, rest_arrived, [shard_shape(n) for n in rest_names], rest_where)))

    mla_pairs = reduce_to_pair_sums("mla", [d_w_in.reshape(N_CHIPS, -1, d_w_in.shape[-1]), _stack_cols(d_w_uq), _stack_cols(d_w_ukv)])
    grads.update(zip(first[:3], reduce_from_chips("mla", mla_pairs, scatter_to_owner_chips(mla_pairs),
                                                  [shard_shape(n) for n in first[:3]], [(i, 0) for i in range(3)])))

    def pad_row(v):
        return jnp.pad(v, ((0, 0), (0, d - v.shape[1])))

    small = all_reduce_small("all_reduce_small_grads", pack_rows([
        d_mla_norm, pad_row(d_g_cq), pad_row(d_g_ckv), d_ffn_norm0, d_ffn_norm1, d_final_norm, d_conv_norm,
        d_conv_w[0:1], d_conv_w[1:2], d_conv_w[2:3], jnp.broadcast_to(loss_local, (1, d))]))
    loss = small[10, 0]
    grads["mla_norm"] = small[0:1]
    grads["mla_g_cq"] = small[1:2, :mla_g_cq.shape[1]]
    grads["mla_g_ckv"] = small[2:3, :mla_g_ckv.shape[1]]
    grads["ffn_norm"] = small[3:5]
    grads["final_norm"] = small[5:6]
    grads["conv_norm"] = lax.dynamic_slice(small[6:7], (0, chip * d4), (1, d4))
    grads["conv_w"] = lax.dynamic_slice(small[7:10], (0, chip * d4), (3, d4))

    outs_g, outs_d, outs_m, outs_v = [], [], [], []
    for n in order:
        w = weights[n]
        delta_w, new_m, new_v = adamw(f"adamw_{n}", rows2d(w), grads[n].reshape(rows2d(w).shape), rows2d(m_in[n]), rows2d(v_in[n]))
        outs_g.append(grads[n].reshape(w.shape))
        outs_d.append(delta_w.reshape(w.shape))
        outs_m.append(new_m.reshape(w.shape))
        outs_v.append(new_v.reshape(w.shape))
    return (loss, grad_x.reshape(x.shape), *outs_g, *outs_d, *outs_m, *outs_v)
```

```python
import math

import jax
import jax.numpy as jnp
from jax import lax
from jax.experimental import pallas as pl
from jax.experimental.pallas import tpu as pltpu

F32 = jnp.float32
BF16 = jnp.bfloat16
S = jax.ShapeDtypeStruct

N_HEADS = 8
NOPE = 128
ROPE = 64
HALF = ROPE // 2
VDIM = 128
QK = NOPE + ROPE
CHUNK = 64
ROPE_THETA = 10000.0
RMS_EPS = 1e-6
ADAM_LR = 0.001
ADAM_B1 = 0.9
ADAM_B2 = 0.999
ADAM_EPS = 1e-08
ADAM_WD = 0.01
ADAM_STEP = 10

N_CHIPS = 4
N_DEV = 8
MASK_VALUE = -1e30
SCORE_SCALE = 1.0 / math.sqrt(QK)
LOG2_E = math.log2(math.e)
SCORE_SCALE_LOG2 = SCORE_SCALE * LOG2_E
VMEM_LIMIT = 48 * 1024 * 1024
VMEM_LIMIT_WHOLE_HEAD = 58 * 1024 * 1024
ATT_BLOCK = 512
SMALL_ROWS = 16

_NN = (((1,), (0,)), ((), ()))
_NT = (((1,), (1,)), ((), ()))
_TN = (((0,), (0,)), ((), ()))
MESH = pl.DeviceIdType.MESH
ANY = pl.BlockSpec(memory_space=pl.ANY)


def _params(n_axes, vmem_limit=VMEM_LIMIT):
    return pltpu.CompilerParams(dimension_semantics=("arbitrary",) * n_axes, vmem_limit_bytes=vmem_limit)


def _tile(n, cap, mult=8):
    for t in range(min(cap, n), 0, -1):
        if n % t == 0 and t % mult == 0:
            return t
    return n


def _sigmoid(x):
    return 1.0 / (1.0 + jnp.exp(-x))


def _mm(name, a_ops, b_ops, products, dims, grid, k_axis, outs, acc_shape, epilogue, extra_ops=()):
    na, nb, ne, no = len(a_ops), len(b_ops), len(extra_ops), len(outs)
    n_acc = 1 + max(c for _, _, c in products)
    nk = 1 if k_axis is None else grid[k_axis]

    def body(*refs):
        a_refs = refs[:na]
        b_refs = refs[na:na + nb]
        e_refs = refs[na + nb:na + nb + ne]
        o_refs = refs[na + nb + ne:na + nb + ne + no]
        acc_refs = refs[na + nb + ne + no:]

        def partial_sums():
            vals = [None] * n_acc
            for ai, bi, ci in products:
                d = lax.dot_general(a_refs[ai][...].astype(BF16), b_refs[bi][...].astype(BF16), dims,
                                    preferred_element_type=F32)
                vals[ci] = d if vals[ci] is None else vals[ci] + d
            return vals

        if nk == 1:
            epilogue(partial_sums(), e_refs, o_refs)
        else:
            k = pl.program_id(k_axis)

            @pl.when(k == 0)
            def _():
                for acc in acc_refs:
                    acc[...] = jnp.zeros_like(acc)

            for acc, v in zip(acc_refs, partial_sums()):
                acc[...] += v

            @pl.when(k == nk - 1)
            def _():
                epilogue([acc[...] for acc in acc_refs], e_refs, o_refs)

    ops = list(a_ops) + list(b_ops) + list(extra_ops)
    return pl.pallas_call(
        body, name=name, grid=grid,
        in_specs=[s for _, s in ops], out_specs=[s for _, s in outs], out_shape=[o for o, _ in outs],
        scratch_shapes=[pltpu.VMEM(acc_shape, F32) for _ in range(n_acc if nk > 1 else 0)],
        compiler_params=_params(len(grid)),
    )(*[a for a, _ in ops])


def _store(accs, e_refs, o_refs):
    o_refs[0][...] = accs[0].astype(o_refs[0].dtype)


def linear(name, x, w, out_dtype, resid=None, next_gain=None):
    t, k = x.shape
    n = w.shape[1]
    tm = _tile(t, 512)
    tn = n if n <= 2048 else _tile(n, 1024, 128)
    tile = pl.BlockSpec((tm, tn), lambda j, i: (i, j))
    extra = [] if resid is None else [(resid, tile)]
    outs = [(S((t, n), out_dtype), tile)]
    if next_gain is not None:
        assert tn == n
        extra.append((next_gain, pl.BlockSpec((1, n), lambda j, i: (0, 0))))
        outs.append((S((t, n), BF16), tile))

    def epilogue(accs, e_refs, o_refs):
        y = accs[0] if resid is None else e_refs[0][...] + accs[0]
        o_refs[0][...] = y.astype(out_dtype)
        if next_gain is not None:
            o_refs[1][...] = (y * _rstd(y) * e_refs[-1][...]).astype(BF16)

    res = _mm(name, [(x, pl.BlockSpec((tm, k), lambda j, i: (i, 0)))], [(w, pl.BlockSpec((k, tn), lambda j, i: (0, j)))],
              [(0, 0, 0)], _NN, (n // tn, t // tm), None, outs, None, epilogue, extra)
    return res[0] if next_gain is None else res


def linear_nt(name, dy, w, out_dtype):
    t, n = dy.shape
    k = w.shape[0]
    tm = _tile(t, 512)
    tc = n if n <= 2048 else _tile(n, 1024, 128)
    return _mm(name, [(dy, pl.BlockSpec((tm, tc), lambda i, c: (i, c)))], [(w, pl.BlockSpec((k, tc), lambda i, c: (0, c)))],
               [(0, 0, 0)], _NT, (t // tm, n // tc), 1,
               [(S((t, k), out_dtype), pl.BlockSpec((tm, k), lambda i, c: (i, 0)))], (tm, k), _store)[0]


def wgrad(name, x, dy):
    t, k = x.shape
    n = dy.shape[1]
    tk = _tile(t, 512)
    tn = n if n <= 1024 else _tile(n, 1024, 128)
    return _mm(name, [(x, pl.BlockSpec((tk, k), lambda j, s: (s, 0)))], [(dy, pl.BlockSpec((tk, tn), lambda j, s: (s, j)))],
               [(0, 0, 0)], _TN, (n // tn, t // tk), 1,
               [(S((k, n), BF16), pl.BlockSpec((k, tn), lambda j, s: (0, j)))], (k, tn), _store)[0]


def _resident(shape, index_map):
    return pl.BlockSpec(shape, index_map, pipeline_mode=pl.Buffered(1))


def ffn_up(name, a, wg_all, wu_all, layer):
    t, d = a.shape
    f4 = wg_all.shape[2]
    tm = _tile(t, 512)
    w_spec = _resident((N_CHIPS, d, f4), lambda i: (0, layer, 0))
    h_spec = pl.BlockSpec((N_CHIPS, tm, f4), lambda i: (0, i, 0))

    def body(a_ref, wg_ref, wu_ref, g_ref, u_ref, z_ref):
        av = a_ref[...]
        for k in range(N_CHIPS):
            g = jnp.dot(av, wg_ref[k], preferred_element_type=F32)
            u = jnp.dot(av, wu_ref[k], preferred_element_type=F32)
            g_ref[k] = g.astype(BF16)
            u_ref[k] = u.astype(BF16)
            z_ref[k] = (g * _sigmoid(g) * u).astype(BF16)

    return pl.pallas_call(
        body, name=name, grid=(t // tm,), in_specs=[pl.BlockSpec((tm, d), lambda i: (i, 0)), w_spec, w_spec],
        out_specs=[h_spec] * 3, out_shape=[S((N_CHIPS, t, f4), BF16)] * 3, compiler_params=_params(1))(a, wg_all, wu_all)


def ffn_down(name, z, wd_all, layer, resid, next_gain=None):
    _, t, f4 = z.shape
    d = wd_all.shape[2]
    tm = _tile(t, 512)
    row = pl.BlockSpec((tm, d), lambda i: (i, 0))
    normed = next_gain is not None

    def body(z_ref, wd_ref, r_ref, *refs):
        acc = r_ref[...]
        for k in range(N_CHIPS):
            acc = acc + jnp.dot(z_ref[k], wd_ref[k], preferred_element_type=F32)
        refs[-2 if normed else -1][...] = acc
        if normed:
            refs[-1][...] = (acc * _rstd(acc) * refs[0][...]).astype(BF16)

    res = pl.pallas_call(
        body, name=name, grid=(t // tm,),
        in_specs=[pl.BlockSpec((N_CHIPS, tm, f4), lambda i: (0, i, 0)), _resident((N_CHIPS, f4, d), lambda i: (0, layer, 0)), row]
        + ([pl.BlockSpec((1, d), lambda i: (0, 0))] if normed else []),
        out_specs=[row] * (2 if normed else 1), out_shape=[S((t, d), F32)] + ([S((t, d), BF16)] if normed else []),
        compiler_params=_params(1))(z, wd_all, resid, *([next_gain] if normed else []))
    return res if normed else res[0]


def ffn_bwd_hidden(name, dh, wd_all, layer, g, u):
    t, d = dh.shape
    f4 = g.shape[2]
    tm = _tile(t, 512)
    h_spec = pl.BlockSpec((N_CHIPS, tm, f4), lambda i: (0, i, 0))

    def body(dh_ref, wd_ref, g_ref, u_ref, dg_ref, du_ref):
        dhb = dh_ref[...].astype(BF16)
        for k in range(N_CHIPS):
            dz = lax.dot_general(dhb, wd_ref[k], _NT, preferred_element_type=F32)
            gv = g_ref[k].astype(F32)
            uv = u_ref[k].astype(F32)
            sg = 0.5 * jnp.tanh(0.5 * gv) + 0.5
            dg_ref[k] = (dz * uv * (sg * (1.0 + gv * (1.0 - sg)))).astype(BF16)
            du_ref[k] = (dz * (gv * sg)).astype(BF16)

    return pl.pallas_call(
        body, name=name, grid=(t // tm,),
        in_specs=[pl.BlockSpec((tm, d), lambda i: (i, 0)), _resident((N_CHIPS, f4, d), lambda i: (0, layer, 0)), h_spec, h_spec],
        out_specs=[h_spec] * 2, out_shape=[S((N_CHIPS, t, f4), BF16)] * 2, compiler_params=_params(1))(dh, wd_all, g, u)


def _norm_bwd_specs(tm, d):
    row = pl.BlockSpec((tm, d), lambda i: (i, 0))
    vec = pl.BlockSpec((1, d), lambda i: (0, 0))
    return [row, vec, row], [row, vec]


def _norm_bwd_tail(da, h_ref, g_ref, dhi_ref, dho_ref, dgain_ref):
    dx, dgain = _rms_bwd(h_ref[...], g_ref[...], da)
    dho_ref[...] = dhi_ref[...] + dx

    @pl.when(pl.program_id(0) == 0)
    def _():
        dgain_ref[...] = jnp.zeros_like(dgain_ref)

    dgain_ref[...] += dgain


def ffn_bwd_input(name, dg, du, wg_all, wu_all, layer, h, gain, dh_in):
    _, t, f4 = dg.shape
    d = h.shape[1]
    tm = _tile(t, 512)
    h_spec = pl.BlockSpec((N_CHIPS, tm, f4), lambda i: (0, i, 0))
    w_spec = _resident((N_CHIPS, d, f4), lambda i: (0, layer, 0))
    tail_in, tail_out = _norm_bwd_specs(tm, d)

    def body(dg_ref, du_ref, wg_ref, wu_ref, *tail):
        acc = jnp.zeros((tm, d), F32)
        for k in range(N_CHIPS):
            acc = acc + lax.dot_general(dg_ref[k], wg_ref[k], _NT, preferred_element_type=F32)
            acc = acc + lax.dot_general(du_ref[k], wu_ref[k], _NT, preferred_element_type=F32)
        _norm_bwd_tail(acc, *tail)

    return pl.pallas_call(
        body, name=name, grid=(t // tm,), in_specs=[h_spec, h_spec, w_spec, w_spec] + tail_in, out_specs=tail_out,
        out_shape=[S((t, d), F32), S((1, d), F32)], compiler_params=_params(1))(dg, du, wg_all, wu_all, h, gain, dh_in)


def ffn_wgrad_up(name, a, dy):
    t, d = a.shape
    f4 = dy.shape[2]
    tk = _tile(t, 512)
    nt = t // tk

    def body(a_ref, dy_ref, o_ref, acc):
        s = pl.program_id(0)

        @pl.when(s == 0)
        def _():
            acc[...] = jnp.zeros_like(acc)

        at = a_ref[...].T
        for k in range(N_CHIPS):
            acc[k] += jnp.dot(at, dy_ref[k], preferred_element_type=F32)

        @pl.when(s == nt - 1)
        def _():
            o_ref[...] = acc[...].astype(BF16)

    return pl.pallas_call(
        body, name=name, grid=(nt,),
        in_specs=[pl.BlockSpec((tk, d), lambda s: (s, 0)), pl.BlockSpec((N_CHIPS, tk, f4), lambda s: (0, s, 0))],
        out_specs=pl.BlockSpec((N_CHIPS, d, f4), lambda s: (0, 0, 0)), out_shape=S((N_CHIPS, d, f4), BF16),
        scratch_shapes=[pltpu.VMEM((N_CHIPS, d, f4), F32)], compiler_params=_params(1))(a, dy)


def ffn_wgrad_down(name, z, dh):
    _, t, f4 = z.shape
    d = dh.shape[1]
    tk = _tile(t, 512)
    nt = t // tk

    def body(z_ref, dh_ref, o_ref, acc):
        s = pl.program_id(0)

        @pl.when(s == 0)
        def _():
            acc[...] = jnp.zeros_like(acc)

        dhb = dh_ref[...].astype(BF16)
        for k in range(N_CHIPS):
            acc[k] += lax.dot_general(z_ref[k], dhb, _TN, preferred_element_type=F32)

        @pl.when(s == nt - 1)
        def _():
            o_ref[...] = acc[...].astype(BF16)

    return pl.pallas_call(
        body, name=name, grid=(nt,),
        in_specs=[pl.BlockSpec((N_CHIPS, tk, f4), lambda s: (0, s, 0)), pl.BlockSpec((tk, d), lambda s: (s, 0))],
        out_specs=pl.BlockSpec((N_CHIPS, f4, d), lambda s: (0, 0, 0)), out_shape=S((N_CHIPS, f4, d), BF16),
        scratch_shapes=[pltpu.VMEM((N_CHIPS, f4, d), F32)], compiler_params=_params(1))(z, dh)


def conv_in_proj(name, a, w):
    t, d = a.shape
    tm = _tile(t, 512)
    return _mm(name, [(a, pl.BlockSpec((tm, d), lambda j, i: (i, 0)))], [(w, pl.BlockSpec((d, d), lambda j, i: (0, j)))],
               [(0, 0, 0)], _NN, (3, t // tm), None,
               [(S((3, t, d), F32), pl.BlockSpec((None, tm, d), lambda j, i: (j, i, 0)))], None, _store)[0]


def conv_in_bwd_input(name, dbcx, w, h, gain, dh_in):
    _, t, d = dbcx.shape
    tm = _tile(t, 512)
    tail_in, tail_out = _norm_bwd_specs(tm, d)

    def body(g_ref, w_ref, *tail):
        acc = jnp.zeros((tm, d), F32)
        for j in range(3):
            acc = acc + lax.dot_general(g_ref[j], w_ref[:, j * d:(j + 1) * d], _NT, preferred_element_type=F32)
        _norm_bwd_tail(acc, *tail)

    return pl.pallas_call(
        body, name=name, grid=(t // tm,),
        in_specs=[pl.BlockSpec((3, tm, d), lambda i: (0, i, 0)), _resident((d, 3 * d), lambda i: (0, 0))] + tail_in,
        out_specs=tail_out, out_shape=[S((t, d), F32), S((1, d), F32)], compiler_params=_params(1))(dbcx, w, h, gain, dh_in)


def linear_nt_norm_bwd(name, dy, w, h, gain, dh_in):
    t, n = dy.shape
    k = w.shape[0]
    tm = _tile(t, 512)
    tail_in, tail_out = _norm_bwd_specs(tm, k)

    def body(dy_ref, w_ref, *tail):
        _norm_bwd_tail(lax.dot_general(dy_ref[...].astype(BF16), w_ref[...], _NT, preferred_element_type=F32), *tail)

    return pl.pallas_call(
        body, name=name, grid=(t // tm,),
        in_specs=[pl.BlockSpec((tm, n), lambda i: (i, 0)), _resident((k, n), lambda i: (0, 0))] + tail_in,
        out_specs=tail_out, out_shape=[S((t, k), F32), S((1, k), F32)], compiler_params=_params(1))(dy, w, h, gain, dh_in)


def conv_in_wgrad(name, a, dbcx):
    t, d = a.shape
    tk = _tile(t, 512)
    nt = t // tk

    def body(a_ref, g_ref, o_ref, acc):
        s = pl.program_id(0)

        @pl.when(s == 0)
        def _():
            acc[...] = jnp.zeros_like(acc)

        at = a_ref[...].T
        for j in range(3):
            acc[:, j * d:(j + 1) * d] += jnp.dot(at, g_ref[j], preferred_element_type=F32)

        @pl.when(s == nt - 1)
        def _():
            o_ref[...] = acc[...].astype(BF16)

    return pl.pallas_call(
        body, name=name, grid=(nt,),
        in_specs=[pl.BlockSpec((tk, d), lambda s: (s, 0)), pl.BlockSpec((3, tk, d), lambda s: (0, s, 0))],
        out_specs=pl.BlockSpec((d, 3 * d), lambda s: (0, 0)), out_shape=S((d, 3 * d), BF16),
        scratch_shapes=[pltpu.VMEM((d, 3 * d), F32)], compiler_params=_params(1))(a, dbcx)


def _rstd(x):
    return lax.rsqrt(jnp.mean(x * x, axis=-1, keepdims=True) + RMS_EPS)


def _rms_bwd(x, g, dy):
    r = _rstd(x)
    xhat = x * r
    dgain = jnp.sum(dy * xhat, axis=0, keepdims=True)
    dxh = dy * g
    dx = r * (dxh - xhat * jnp.mean(dxh * xhat, axis=-1, keepdims=True))
    return dx, dgain


def rms_fwd(name, h, g):
    t, d = h.shape
    tr = _tile(t, 512)

    def body(h_ref, g_ref, a_ref):
        x = h_ref[...]
        a_ref[...] = (x * _rstd(x) * g_ref[...]).astype(BF16)

    return pl.pallas_call(
        body, name=name, grid=(t // tr,),
        in_specs=[pl.BlockSpec((tr, d), lambda i: (i, 0)), pl.BlockSpec((1, d), lambda i: (0, 0))],
        out_specs=pl.BlockSpec((tr, d), lambda i: (i, 0)), out_shape=S((t, d), BF16), compiler_params=_params(1))(h, g)


def ffn_down_loss(name, z, wd_all, layer, resid, gain, target):
    _, t, f4 = z.shape
    d = wd_all.shape[2]
    tm = _tile(t, 512)

    def body(z_ref, wd_ref, r_ref, g_ref, t_ref, dh_ref, dg_ref, loss_ref):
        x = r_ref[...]
        for k in range(N_CHIPS):
            x = x + jnp.dot(z_ref[k], wd_ref[k], preferred_element_type=F32)
        g = g_ref[...]
        r = _rstd(x)
        xhat = x * r
        err = xhat * g - t_ref[...]
        dy = err * (1.0 / d)
        dxh = dy * g
        dh_ref[...] = r * (dxh - xhat * jnp.mean(dxh * xhat, axis=-1, keepdims=True))

        @pl.when(pl.program_id(0) == 0)
        def _():
            dg_ref[...] = jnp.zeros_like(dg_ref)
            loss_ref[...] = jnp.zeros_like(loss_ref)

        dg_ref[...] += jnp.sum(dy * xhat, axis=0, keepdims=True)
        per_token = jnp.mean(err * err, axis=-1, keepdims=True)
        loss_ref[...] += 0.5 * jnp.sum(per_token, axis=0, keepdims=True)

    row = pl.BlockSpec((tm, d), lambda i: (i, 0))
    vec = pl.BlockSpec((1, d), lambda i: (0, 0))
    one = pl.BlockSpec((1, 1), lambda i: (0, 0))
    return pl.pallas_call(
        body, name=name, grid=(t // tm,),
        in_specs=[pl.BlockSpec((N_CHIPS, tm, f4), lambda i: (0, i, 0)), _resident((N_CHIPS, f4, d), lambda i: (0, layer, 0)), row, vec, row],
        out_specs=[row, vec, one], out_shape=[S((t, d), F32), S((1, d), F32), S((1, 1), F32)],
        compiler_params=_params(1))(z, wd_all, resid, gain, target)


def mla_in_proj(name, a, w, g_cq, g_ckv, cos, sin):
    t, d = a.shape
    n = w.shape[1]
    ql, kl = g_cq.shape[1], g_ckv.shape[1]
    tr = _tile(t, 512)

    def body(a_ref, w_ref, gq_ref, gk_ref, c_ref, s_ref, p_ref, cq_ref, ckv_ref, kr_ref):
        p_ref[...] = jnp.dot(a_ref[...], w_ref[...], preferred_element_type=F32)
        xq = p_ref[:, 0:ql]
        cq_ref[...] = (xq * _rstd(xq) * gq_ref[...]).astype(BF16)
        xk = p_ref[:, ql:ql + kl]
        ckv_ref[...] = (xk * _rstd(xk) * gk_ref[...]).astype(BF16)
        k1 = p_ref[:, ql + kl:ql + kl + HALF]
        k2 = p_ref[:, ql + kl + HALF:ql + kl + ROPE]
        c = c_ref[...]
        s = s_ref[...]
        kr_ref[:, 0:HALF] = k1 * c - k2 * s
        kr_ref[:, HALF:ROPE] = k1 * s + k2 * c

    def row(w):
        return pl.BlockSpec((tr, w), lambda i: (i, 0))

    def vec(w):
        return pl.BlockSpec((1, w), lambda i: (0, 0))

    return pl.pallas_call(
        body, name=name, grid=(t // tr,),
        in_specs=[row(d), _resident((d, n), lambda i: (0, 0)), vec(ql), vec(kl), row(HALF), row(HALF)],
        out_specs=[row(n), row(ql), row(kl), row(ROPE)],
        out_shape=[S((t, n), F32), S((t, ql), BF16), S((t, kl), BF16), S((t, ROPE), F32)],
        compiler_params=_params(1))(a, w, g_cq, g_ckv, cos, sin)


def mla_mid_bwd(name, proj, g_cq, g_ckv, dcq, dckv, dkr, cos, sin):
    t, n = proj.shape
    ql, kl = g_cq.shape[1], g_ckv.shape[1]
    tr = _tile(t, 512)

    def body(p_ref, gq_ref, gk_ref, dcq_ref, dckv_ref, dkr_ref, c_ref, s_ref, dp_ref, dgq_ref, dgk_ref):
        dxq, dgq = _rms_bwd(p_ref[:, 0:ql], gq_ref[...], dcq_ref[...])
        dp_ref[:, 0:ql] = dxq.astype(BF16)
        dxk, dgk = _rms_bwd(p_ref[:, ql:ql + kl], gk_ref[...], dckv_ref[...])
        dp_ref[:, ql:ql + kl] = dxk.astype(BF16)
        d1 = dkr_ref[:, 0:HALF]
        d2 = dkr_ref[:, HALF:ROPE]
        c = c_ref[...]
        s = s_ref[...]
        dp_ref[:, ql + kl:ql + kl + HALF] = (d1 * c + d2 * s).astype(BF16)
        dp_ref[:, ql + kl + HALF:ql + kl + ROPE] = (d2 * c - d1 * s).astype(BF16)

        @pl.when(pl.program_id(0) == 0)
        def _():
            dgq_ref[...] = jnp.zeros_like(dgq_ref)
            dgk_ref[...] = jnp.zeros_like(dgk_ref)

        dgq_ref[...] += dgq
        dgk_ref[...] += dgk

    def row(w):
        return pl.BlockSpec((tr, w), lambda i: (i, 0))

    def vec(w):
        return pl.BlockSpec((1, w), lambda i: (0, 0))

    return pl.pallas_call(
        body, name=name, grid=(t // tr,),
        in_specs=[row(n), vec(ql), vec(kl), row(ql), row(kl), row(ROPE), row(HALF), row(HALF)],
        out_specs=[row(n), vec(ql), vec(kl)], out_shape=[S((t, n), BF16), S((1, ql), F32), S((1, kl), F32)],
        compiler_params=_params(1))(proj, g_cq, g_ckv, dcq, dckv, dkr, cos, sin)


def qkv_heads(name, q, kv, kr, cos, sin, shards=()):
    t = q.shape[0]
    tr = _tile(t, 256)
    n = len(shards)

    def body(q_ref, kv_ref, kr_ref, c_ref, s_ref, *refs):
        src = refs[:n]
        qo_ref, ko_ref, vo_ref = refs[n:n + 3]
        if n:
            _ride_along(gather_ici_copies(src, refs[n + 3:2 * n + 3], *refs[2 * n + 3:]), (pl.program_id(0),), (t // tr,))
        c = c_ref[...]
        s = s_ref[...]
        krb = kr_ref[...].astype(BF16)
        for h in range(N_HEADS):
            q0 = h * QK
            qo_ref[h, :, 0:NOPE] = q_ref[:, q0:q0 + NOPE].astype(BF16)
            q1 = q_ref[:, q0 + NOPE:q0 + NOPE + HALF]
            q2 = q_ref[:, q0 + NOPE + HALF:q0 + QK]
            qo_ref[h, :, NOPE:NOPE + HALF] = (q1 * c - q2 * s).astype(BF16)
            qo_ref[h, :, NOPE + HALF:QK] = (q1 * s + q2 * c).astype(BF16)
            k0 = h * (NOPE + VDIM)
            ko_ref[h, :, 0:NOPE] = kv_ref[:, k0:k0 + NOPE]
            ko_ref[h, :, NOPE:QK] = krb
            vo_ref[h] = kv_ref[:, k0 + NOPE:k0 + NOPE + VDIM]

    def row(w):
        return pl.BlockSpec((tr, w), lambda i: (i, 0))

    def heads(w):
        return pl.BlockSpec((N_HEADS, tr, w), lambda i: (0, i, 0))

    outs = pl.pallas_call(
        body, name=name, grid=(t // tr,),
        in_specs=[row(N_HEADS * QK), row(N_HEADS * (NOPE + VDIM)), row(ROPE), row(HALF), row(HALF)] + [ANY] * n,
        out_specs=[heads(QK), heads(QK), heads(VDIM)] + [ANY] * n,
        out_shape=[S((N_HEADS, t, QK), BF16), S((N_HEADS, t, QK), BF16), S((N_HEADS, t, VDIM), BF16)]
        + [S((N_CHIPS,) + s.shape, s.dtype) for s in shards],
        scratch_shapes=[pltpu.SemaphoreType.DMA((n, 3)), pltpu.SemaphoreType.DMA((n, 3))] if n else [],
        compiler_params=_params(1))(q, kv, kr, cos, sin, *shards)
    return outs[0], outs[1], outs[2], list(outs[3:])


def qkv_heads_bwd(name, dq_h, dk_h, dv_h, cos, sin):
    t = dq_h.shape[1]
    tr = _tile(t, 256)

    def body(dq_ref, dk_ref, dv_ref, c_ref, s_ref, q_ref, kv_ref, kr_ref):
        c = c_ref[...]
        s = s_ref[...]
        dkr = jnp.zeros((tr, ROPE), F32)
        for h in range(N_HEADS):
            q0 = h * QK
            q_ref[:, q0:q0 + NOPE] = dq_ref[h, :, 0:NOPE].astype(BF16)
            d1 = dq_ref[h, :, NOPE:NOPE + HALF]
            d2 = dq_ref[h, :, NOPE + HALF:QK]
            q_ref[:, q0 + NOPE:q0 + NOPE + HALF] = (d1 * c + d2 * s).astype(BF16)
            q_ref[:, q0 + NOPE + HALF:q0 + QK] = (d2 * c - d1 * s).astype(BF16)
            k0 = h * (NOPE + VDIM)
            kv_ref[:, k0:k0 + NOPE] = dk_ref[h, :, 0:NOPE].astype(BF16)
            kv_ref[:, k0 + NOPE:k0 + NOPE + VDIM] = dv_ref[h].astype(BF16)
            dkr = dkr + dk_ref[h, :, NOPE:QK]
        kr_ref[...] = dkr

    def row(w):
        return pl.BlockSpec((tr, w), lambda i: (i, 0))

    def heads(w):
        return pl.BlockSpec((N_HEADS, tr, w), lambda i: (0, i, 0))

    return pl.pallas_call(
        body, name=name, grid=(t // tr,),
        in_specs=[heads(QK), heads(QK), heads(VDIM), row(HALF), row(HALF)],
        out_specs=[row(N_HEADS * QK), row(N_HEADS * (NOPE + VDIM)), row(ROPE)],
        out_shape=[S((t, N_HEADS * QK), BF16), S((t, N_HEADS * (NOPE + VDIM)), BF16), S((t, ROPE), F32)],
        compiler_params=_params(1))(dq_h, dk_h, dv_h, cos, sin)


def _chunk_mask_t(q_start, k_start, bq, bk):
    kc = (k_start + lax.broadcasted_iota(jnp.int32, (bk, bq), 0)) // CHUNK
    qc = (q_start + lax.broadcasted_iota(jnp.int32, (bk, bq), 1)) // CHUNK
    return kc <= qc


def attention_fwd(name, q, k, v, shards=()):
    nh, t, _ = q.shape
    blk = ATT_BLOCK
    nq = t // blk
    n = len(shards)

    def body(q_ref, k_ref, v_ref, *refs):
        src = refs[:n]
        o_ref, lse_ref = refs[n:n + 2]
        dst = refs[n + 2:2 * n + 2]
        m_ref, l_ref, acc_ref, s_buf, p_buf, alpha_buf, bias_ref = refs[2 * n + 2:2 * n + 9]
        i = pl.program_id(1)
        if n:
            send_sems, recv_sems = refs[2 * n + 9:]
            _ride_along(gather_ici_copies(src, dst, send_sems, recv_sems), (pl.program_id(0), i), (nh, nq))

        @pl.when((pl.program_id(0) == 0) & (i == 0))
        def _():
            bias_ref[...] = jnp.where(_chunk_mask_t(0, 0, blk, blk), 0.0, MASK_VALUE)

        m_ref[...] = jnp.full_like(m_ref, MASK_VALUE)
        l_ref[...] = jnp.zeros_like(l_ref)
        acc_ref[...] = jnp.zeros_like(acc_ref)

        def rows(b):
            return pl.ds(pl.multiple_of(b * blk, blk), blk)

        def scores(b, slot):
            s_buf[slot] = lax.dot_general(k_ref[rows(b), :], q_ref[...], _NT, preferred_element_type=F32)

        def softmax(slot, diagonal):
            s = s_buf[slot]
            if diagonal:
                s = s + bias_ref[...]
            m_old = m_ref[...]
            m_new = jnp.maximum(m_old, jnp.max(s, axis=0, keepdims=True))
            p = jnp.exp2((s - m_new) * SCORE_SCALE_LOG2)
            alpha = jnp.exp2((m_old - m_new) * SCORE_SCALE_LOG2)
            l_ref[...] = alpha * l_ref[...] + jnp.sum(p, axis=0, keepdims=True)
            m_ref[...] = m_new
            alpha_buf[slot] = alpha
            p_buf[slot] = p.astype(BF16)

        def values(b, slot):
            pv = lax.dot_general(v_ref[rows(b), :], p_buf[slot], _TN, preferred_element_type=F32)
            acc_ref[...] = alpha_buf[slot] * acc_ref[...] + pv

        def step(t, slot):
            values(t - 2, slot)
            softmax(1 - slot, False)
            scores(t, slot)

        scores(0, 0)

        @pl.when(i == 0)
        def _():
            softmax(0, True)
            values(0, 0)

        @pl.when(i > 0)
        def _():
            scores(1, 1)
            softmax(0, False)
            steady = i - 1

            def pair(u, carry):
                step(2 + 2 * u, 0)
                step(3 + 2 * u, 1)
                return carry

            lax.fori_loop(0, steady // 2, pair, 0)

            @pl.when(steady % 2 == 1)
            def _():
                step(i, 0)

            last = i % 2
            softmax(last, True)
            values(i - 1, 1 - last)
            values(i, last)

        l = l_ref[...]
        o_ref[...] = (acc_ref[...] / l).T
        lse_ref[...] = m_ref[...] * SCORE_SCALE + jnp.log(l)

    outs = pl.pallas_call(
        body, name=name, grid=(nh, nq),
        in_specs=[pl.BlockSpec((None, blk, QK), lambda h, i: (h, i, 0)), pl.BlockSpec((None, t, QK), lambda h, i: (h, 0, 0)),
                  pl.BlockSpec((None, t, VDIM), lambda h, i: (h, 0, 0))] + [ANY] * n,
        out_specs=[pl.BlockSpec((blk, VDIM), lambda h, i: (i, h)),
                   pl.BlockSpec((None, None, 1, blk), lambda h, i: (h, i, 0, 0))] + [ANY] * n,
        out_shape=[S((t, nh * VDIM), F32), S((nh, nq, 1, blk), F32)] + [S((N_CHIPS,) + s.shape, s.dtype) for s in shards],
        scratch_shapes=[pltpu.VMEM((1, blk), F32), pltpu.VMEM((1, blk), F32), pltpu.VMEM((VDIM, blk), F32),
                        pltpu.VMEM((2, blk, blk), F32), pltpu.VMEM((2, blk, blk), BF16), pltpu.VMEM((2, 1, blk), F32),
                        pltpu.VMEM((blk, blk), F32)]
        + ([pltpu.SemaphoreType.DMA((n, 3)), pltpu.SemaphoreType.DMA((n, 3))] if n else []),
        compiler_params=_params(2))(q, k, v, *shards)
    return outs[0], outs[1], list(outs[2:])


def attention_delta(name, do, o):
    t = do.shape[0]
    blk = ATT_BLOCK

    def body(do_ref, o_ref, d_ref):
        for h in range(N_HEADS):
            cols = slice(h * VDIM, (h + 1) * VDIM)
            d_ref[h] = jnp.sum((do_ref[:, cols] * o_ref[:, cols]).T, axis=0, keepdims=True)

    tile = pl.BlockSpec((blk, N_HEADS * VDIM), lambda i: (i, 0))
    return pl.pallas_call(
        body, name=name, grid=(t // blk,), in_specs=[tile, tile],
        out_specs=pl.BlockSpec((N_HEADS, None, 1, blk), lambda i: (0, i, 0, 0)), out_shape=S((N_HEADS, t // blk, 1, blk), F32),
        compiler_params=_params(1))(do, o)


def attention_bwd(name, q, k, v, do, lse, delta, parts=()):
    nh, t, _ = q.shape
    blk = ATT_BLOCK
    nq = t // blk
    n_pairs = nq * (nq + 1) // 2
    n = len(parts)
    scale = SCORE_SCALE

    def body(q_ref, k_ref, v_ref, do_ref, lse_ref, dl_ref, *refs):
        src = refs[:n]
        dq_ref, dk_ref, dv_ref = refs[n:n + 3]
        dst = refs[n + 3:2 * n + 3]
        s_buf, dp_buf, p_buf, ds_buf, bias_ref = refs[2 * n + 3:2 * n + 8]
        if n:
            send_sems, recv_sems = refs[2 * n + 8:]
            _ride_along(scatter_ici_copies(src, dst, send_sems, recv_sems), (pl.program_id(0),), (nh,))

        @pl.when(pl.program_id(0) == 0)
        def _():
            bias_ref[...] = jnp.where(_chunk_mask_t(0, 0, blk, blk), 0.0, MASK_VALUE)

        dq_ref[...] = jnp.zeros_like(dq_ref)
        dk_ref[...] = jnp.zeros_like(dk_ref)
        dv_ref[...] = jnp.zeros_like(dv_ref)

        def rows(x):
            return pl.ds(pl.multiple_of(x * blk, blk), blk)

        def after(jb):
            j, b = jb
            wrap = b == nq - 1 - j
            return jnp.where(wrap, j + 1, j), jnp.where(wrap, 0, b + 1)

        def products(jb, slot):
            j, b = jb
            s_buf[slot] = lax.dot_general(k_ref[rows(j), :], q_ref[rows(j + b), :], _NT, preferred_element_type=F32)
            dp_buf[slot] = lax.dot_general(v_ref[rows(j), :], do_ref[rows(j + b), :].astype(BF16), _NT, preferred_element_type=F32)

        def softmax_bwd(jb, slot):
            j, b = jb
            s = s_buf[slot] + bias_ref[...] * (b == 0).astype(F32)
            p = jnp.exp2(s * SCORE_SCALE_LOG2 - lse_ref[j + b] * LOG2_E)
            p_buf[slot] = p.astype(BF16)
            ds_buf[slot] = (p * (dp_buf[slot] - dl_ref[j + b]) * scale).astype(BF16)

        def gradients(jb, slot):
            j, b = jb
            dv_ref[rows(j), :] += jnp.dot(p_buf[slot], do_ref[rows(j + b), :].astype(BF16), preferred_element_type=F32)
            dk_ref[rows(j), :] += jnp.dot(ds_buf[slot], q_ref[rows(j + b), :], preferred_element_type=F32)
            dq_ref[rows(j + b), :] += lax.dot_general(ds_buf[slot], k_ref[rows(j), :], _TN, preferred_element_type=F32)

        def step(state, slot):
            third, second, first = state
            gradients(third, slot)
            softmax_bwd(second, 1 - slot)
            products(first, slot)
            return second, first, after(first)

        zero = jnp.int32(0)
        pair0 = (zero, zero)
        products(pair0, 0)
        if n_pairs == 1:
            softmax_bwd(pair0, 0)
            gradients(pair0, 0)
        else:
            pair1 = after(pair0)
            products(pair1, 1)
            softmax_bwd(pair0, 0)
            steady = n_pairs - 2
            state = lax.fori_loop(0, steady // 2, lambda u, st: step(step(st, 0), 1), (pair0, pair1, after(pair1)))
            if steady % 2:
                state = step(state, 0)
            before_last, last_pair, _ = state
            last = (n_pairs - 1) % 2
            softmax_bwd(last_pair, last)
            gradients(before_last, 1 - last)
            gradients(last_pair, last)

    head = lambda w: pl.BlockSpec((None, t, w), lambda h: (h, 0, 0))
    stats = pl.BlockSpec((None, nq, 1, blk), lambda h: (h, 0, 0, 0))
    outs = pl.pallas_call(
        body, name=name, grid=(nh,),
        in_specs=[head(QK), head(QK), head(VDIM), pl.BlockSpec((t, VDIM), lambda h: (0, h)), stats, stats] + [ANY] * n,
        out_specs=[head(QK), head(QK), head(VDIM)] + [ANY] * n,
        out_shape=[S((nh, t, QK), F32), S((nh, t, QK), F32), S((nh, t, VDIM), F32)] + [S(p.shape, p.dtype) for p in parts],
        scratch_shapes=[pltpu.VMEM((2, blk, blk), F32), pltpu.VMEM((2, blk, blk), F32), pltpu.VMEM((2, blk, blk), BF16),
                        pltpu.VMEM((2, blk, blk), BF16), pltpu.VMEM((blk, blk), F32)]
        + ([pltpu.SemaphoreType.DMA((n, 3)), pltpu.SemaphoreType.DMA((n, 3))] if n else []),
        compiler_params=_params(1, VMEM_LIMIT_WHOLE_HEAD))(q, k, v, do, lse, delta, *parts)
    return outs[0], outs[1], outs[2], list(outs[3:])


def _shift_down(u, s):
    rows = lax.broadcasted_iota(jnp.int32, u.shape, 0)
    return jnp.where(rows >= s, pltpu.roll(u, s, 0), 0.0)


def _shift_up(u, s):
    n = u.shape[0]
    rows = lax.broadcasted_iota(jnp.int32, u.shape, 0)
    return jnp.where(rows < n - s, pltpu.roll(u, n - s, 0), 0.0)


def _conv_specs(t, d, lanes):
    slab = lambda part: pl.BlockSpec((None, t, lanes), lambda j, part=part: (part, 0, j))
    return slab, pl.BlockSpec((3, lanes), lambda j: (0, j)), pl.BlockSpec((t, lanes), lambda j: (0, j))


def conv_fwd(name, bcx, w):
    _, t, d = bcx.shape
    lanes = _tile(d, 128, 128)
    slab, w_spec, col = _conv_specs(t, d, lanes)

    def body(b_ref, c_ref, x_ref, w_ref, y_ref):
        u = c_ref[...] * x_ref[...]
        uc = w_ref[0:1, :] * _shift_down(u, 2) + w_ref[1:2, :] * _shift_down(u, 1) + w_ref[2:3, :] * u
        y_ref[...] = (b_ref[...] * uc).astype(BF16)

    return pl.pallas_call(
        body, name=name, grid=(d // lanes,), in_specs=[slab(0), slab(1), slab(2), w_spec], out_specs=col,
        out_shape=S((t, d), BF16), compiler_params=_params(1))(bcx, bcx, bcx, w)


def conv_bwd(name, bcx, w, dy):
    _, t, d = bcx.shape
    lanes = _tile(d, 128, 128)
    slab, w_spec, col = _conv_specs(t, d, lanes)

    def body(b_ref, c_ref, x_ref, w_ref, dy_ref, d_ref, dw_ref):
        c = c_ref[...]
        x = x_ref[...]
        dyv = dy_ref[...]
        u = c * x
        u1 = _shift_down(u, 1)
        u2 = _shift_down(u, 2)
        w0, w1, w2 = w_ref[0:1, :], w_ref[1:2, :], w_ref[2:3, :]
        d_ref[0] = (dyv * (w0 * u2 + w1 * u1 + w2 * u)).astype(BF16)
        duc = dyv * b_ref[...]
        dw_ref[0:1, :] = jnp.sum(duc * u2, axis=0, keepdims=True)
        dw_ref[1:2, :] = jnp.sum(duc * u1, axis=0, keepdims=True)
        dw_ref[2:3, :] = jnp.sum(duc * u, axis=0, keepdims=True)
        du = w2 * duc + w1 * _shift_up(duc, 1) + w0 * _shift_up(duc, 2)
        d_ref[1] = (du * x).astype(BF16)
        d_ref[2] = (du * c).astype(BF16)

    return pl.pallas_call(
        body, name=name, grid=(d // lanes,), in_specs=[slab(0), slab(1), slab(2), w_spec, col],
        out_specs=[pl.BlockSpec((3, t, lanes), lambda j: (0, 0, j)), w_spec], out_shape=[S((3, t, d), BF16), S((3, d), F32)],
        compiler_params=_params(1))(bcx, bcx, bcx, w, dy)


def adamw(name, w, g, m, v):
    r, c = w.shape
    tr = _tile(r, 512)

    def body(w_ref, g_ref, m_ref, v_ref, d_ref, mo_ref, vo_ref):
        gv = g_ref[...]
        m_new = ADAM_B1 * m_ref[...] + (1.0 - ADAM_B1) * gv
        v_new = ADAM_B2 * v_ref[...] + (1.0 - ADAM_B2) * (gv * gv)
        m_hat = m_new / (1.0 - ADAM_B1 ** ADAM_STEP)
        v_hat = v_new / (1.0 - ADAM_B2 ** ADAM_STEP)
        d_ref[...] = -ADAM_LR * (m_hat / (jnp.sqrt(v_hat) + ADAM_EPS) + ADAM_WD * w_ref[...])
        mo_ref[...] = m_new
        vo_ref[...] = v_new

    blk = pl.BlockSpec((tr, c), lambda i: (i, 0))
    return pl.pallas_call(
        body, name=name, grid=(r // tr,), in_specs=[blk] * 4, out_specs=[blk] * 3, out_shape=[S((r, c), F32)] * 3,
        compiler_params=_params(1))(w, g, m, v)


def _place():
    x, y, c = lax.axis_index("x"), lax.axis_index("y"), lax.axis_index("c")
    other_chips = [(1 - x, y), (x, 1 - y), (1 - x, 1 - y)]
    return x, y, c, other_chips


def _half(c, rows):
    return pl.ds(pl.multiple_of(c * (rows // 2), 16), rows // 2)


def gather_weight_shards(shards):
    n = len(shards)

    def body(*refs):
        src = refs[:n]
        dst = refs[n:2 * n]
        send_sems, recv_sems = refs[2 * n:]
        x, y, c, chips = _place()
        me = 2 * x + y
        sibling = (x, y, 1 - c)

        def copy(i, slot, half_of, sem, to, from_input=False):
            rows = _half(half_of, src[i].shape[0])
            return pltpu.make_async_remote_copy(
                src_ref=src[i].at[rows] if from_input else dst[i].at[slot, rows], dst_ref=dst[i].at[slot, rows],
                send_sem=send_sems.at[i, sem], recv_sem=recv_sems.at[i, sem], device_id=to, device_id_type=MESH)

        sent = []
        for i in range(n):
            for j, chip in enumerate(chips):
                sent.append(copy(i, me, c, j, (*chip, c), from_input=True))
                sent[-1].start()
        for i in range(n):
            for j, (px, py) in enumerate(chips):
                copy(i, 2 * px + py, c, j, sibling).wait_recv()
                sent.append(copy(i, 2 * px + py, c, 3 + j, sibling))
                sent[-1].start()
        for i in range(n):
            for j, (px, py) in enumerate(chips):
                copy(i, 2 * px + py, 1 - c, 3 + j, sibling).wait_recv()
        for cp in sent:
            cp.wait_send()

    outs = pl.pallas_call(
        body, name="gather_weight_shards", in_specs=[ANY] * n, out_specs=[ANY] * n,
        out_shape=[S((N_CHIPS,) + s.shape, s.dtype) for s in shards],
        scratch_shapes=[pltpu.SemaphoreType.DMA((n, 6)), pltpu.SemaphoreType.DMA((n, 6))],
    )(*shards)
    return _fill_own_slot(outs, [s[None] for s in shards])


def gather_ici_copies(src, dst, send_sems, recv_sems):
    x, y, c, chips = _place()
    me = 2 * x + y
    pairs = []
    for i in range(len(src)):
        rows = _half(c, src[i].shape[0])
        for j, (px, py) in enumerate(chips):
            def copy(slot):
                return pltpu.make_async_remote_copy(
                    src_ref=src[i].at[rows], dst_ref=dst[i].at[slot, rows], send_sem=send_sems.at[i, j],
                    recv_sem=recv_sems.at[i, j], device_id=(px, py, c), device_id_type=MESH)
            pairs.append((copy(me), copy(2 * px + py)))
    return pairs


def scatter_ici_copies(src, dst, send_sems, recv_sems):
    x, y, c, chips = _place()
    me = 2 * x + y
    pairs = []
    for i in range(len(src)):
        for j, (px, py) in enumerate(chips):
            def copy(from_slot, to_slot):
                return pltpu.make_async_remote_copy(
                    src_ref=src[i].at[from_slot], dst_ref=dst[i].at[to_slot], send_sem=send_sems.at[i, j],
                    recv_sem=recv_sems.at[i, j], device_id=(px, py, c), device_id_type=MESH)
            pairs.append((copy(2 * px + py, me), copy(me, 2 * px + py)))
    return pairs


def _ride_along(pairs, grid_ids, grid_sizes):
    first = grid_ids[0] == 0
    last = grid_ids[0] == grid_sizes[0] - 1
    for g, size in zip(grid_ids[1:], grid_sizes[1:]):
        first = first & (g == 0)
        last = last & (g == size - 1)

    @pl.when(first)
    def _():
        for outgoing, _ in pairs:
            outgoing.start()

    @pl.when(last)
    def _():
        for _, incoming in pairs:
            incoming.wait_recv()
        for outgoing, _ in pairs:
            outgoing.wait_send()


def _fill_own_slot(gathered, own):
    me = 2 * lax.axis_index("x") + lax.axis_index("y")
    return [lax.dynamic_update_slice(g, o, (me,) + (0,) * (g.ndim - 1)) for g, o in zip(gathered, own)]


def forward_to_sibling(gathered):
    n = len(gathered)

    def body(*refs):
        src = refs[:n]
        dst = refs[n:2 * n]
        send_sems, recv_sems = refs[2 * n:]
        x, y, c, chips = _place()
        pairs = []
        for i in range(n):
            for j, (px, py) in enumerate(chips):
                def copy(half_of):
                    rows = _half(half_of, src[i].shape[1])
                    return pltpu.make_async_remote_copy(
                        src_ref=src[i].at[2 * px + py, rows], dst_ref=dst[i].at[2 * px + py, rows], send_sem=send_sems.at[i, j],
                        recv_sem=recv_sems.at[i, j], device_id=(x, y, 1 - c), device_id_type=MESH)
                pairs.append((copy(c), copy(1 - c)))
        for outgoing, _ in pairs:
            outgoing.start()
        for _, incoming in pairs:
            incoming.wait_recv()
        for outgoing, _ in pairs:
            outgoing.wait_send()

    return pl.pallas_call(
        body, name="forward_to_sibling", in_specs=[ANY] * n, out_specs=[ANY] * n,
        out_shape=[S(g.shape, g.dtype) for g in gathered], input_output_aliases={i: i for i in range(n)},
        scratch_shapes=[pltpu.SemaphoreType.DMA((n, 3)), pltpu.SemaphoreType.DMA((n, 3))],
    )(*gathered)


def sibling_swap_halves(name, grads):
    n = len(grads)

    def body(*refs):
        src = refs[:n]
        dst = refs[n:2 * n]
        send_sems, recv_sems = refs[2 * n:]
        x, y, c, _ = _place()
        copies = [pltpu.make_async_remote_copy(
            src_ref=src[i].at[:, _half(1 - c, src[i].shape[1]), :], dst_ref=dst[i], send_sem=send_sems.at[i],
            recv_sem=recv_sems.at[i], device_id=(x, y, 1 - c), device_id_type=MESH) for i in range(n)]
        for cp in copies:
            cp.start()
        for cp in copies:
            cp.wait()

    return pl.pallas_call(
        body, name=name, in_specs=[ANY] * n, out_specs=[ANY] * n,
        out_shape=[S((g.shape[0], g.shape[1] // 2, g.shape[2]), g.dtype) for g in grads],
        scratch_shapes=[pltpu.SemaphoreType.DMA((n,)), pltpu.SemaphoreType.DMA((n,))],
    )(*grads)


def add_halves(name, g, rx):
    _, r, cdim = g.shape
    r2 = r // 2
    tr = _tile(r2, 512, 16)
    nb = r2 // tr

    def body(lo_ref, hi_ref, rx_ref, o_ref):
        mine = jnp.where(lax.axis_index("c") == 0, lo_ref[...], hi_ref[...])
        o_ref[...] = (mine.astype(F32) + rx_ref[...].astype(F32)).astype(BF16)

    half = pl.BlockSpec((None, tr, cdim), lambda k, i: (k, i, 0))
    return pl.pallas_call(
        body, name=name, grid=(N_CHIPS, nb),
        in_specs=[half, pl.BlockSpec((None, tr, cdim), lambda k, i: (k, nb + i, 0)), half],
        out_specs=half, out_shape=S((N_CHIPS, r2, cdim), BF16), compiler_params=_params(2))(g, g, rx)


def scatter_to_owner_chips(parts):
    n = len(parts)

    def body(*refs):
        src = refs[:n]
        dst = refs[n:2 * n]
        send_sems, recv_sems = refs[2 * n:]
        pairs = scatter_ici_copies(src, dst, send_sems, recv_sems)
        for outgoing, _ in pairs:
            outgoing.start()
        for _, incoming in pairs:
            incoming.wait_recv()
        for outgoing, _ in pairs:
            outgoing.wait_send()

    return pl.pallas_call(
        body, name="scatter_to_owner_chips", in_specs=[ANY] * n, out_specs=[ANY] * n,
        out_shape=[S(p.shape, p.dtype) for p in parts],
        scratch_shapes=[pltpu.SemaphoreType.DMA((n, 3)), pltpu.SemaphoreType.DMA((n, 3))],
    )(*parts)


def _own_slots(parts):
    me = 2 * lax.axis_index("x") + lax.axis_index("y")
    return [lax.dynamic_slice(p, (me, 0, 0), (1,) + p.shape[1:]) for p in parts]


def sum_chips(name, parts):
    _, r2, cdim = parts.shape
    tr = _tile(r2, 512, 16)

    def body(p_ref, o_ref):
        acc = p_ref[0].astype(F32)
        for k in range(1, N_CHIPS):
            acc = acc + p_ref[k].astype(F32)
        o_ref[...] = acc

    return pl.pallas_call(
        body, name=name, grid=(r2 // tr,), in_specs=[pl.BlockSpec((N_CHIPS, tr, cdim), lambda i: (0, i, 0))],
        out_specs=pl.BlockSpec((tr, cdim), lambda i: (i, 0)), out_shape=S((r2, cdim), F32), compiler_params=_params(1))(parts)


def sibling_join_halves(name, halves, targets, where):
    n = len(halves)

    def rows_of(i, half_of):
        r2 = halves[i].shape[0]
        return pl.ds(pl.multiple_of(where[i][1] + half_of * r2, 8), r2)

    def body(*refs):
        src = refs[:n]
        dst = refs[n:n + len(targets)]
        send_sems, recv_sems = refs[n + len(targets):]
        x, y, c, _ = _place()

        def copy(i, half_of):
            return pltpu.make_async_remote_copy(
                src_ref=src[i], dst_ref=dst[where[i][0]].at[rows_of(i, half_of)], send_sem=send_sems.at[i],
                recv_sem=recv_sems.at[i], device_id=(x, y, 1 - c), device_id_type=MESH)

        for i in range(n):
            copy(i, c).start()
        for i in range(n):
            copy(i, 1 - c).wait_recv()
        for i in range(n):
            copy(i, c).wait_send()

    outs = list(pl.pallas_call(
        body, name=name, in_specs=[ANY] * n, out_specs=[ANY] * len(targets), out_shape=[S(tg, F32) for tg in targets],
        scratch_shapes=[pltpu.SemaphoreType.DMA((n,)), pltpu.SemaphoreType.DMA((n,))],
    )(*halves))
    c = lax.axis_index("c")
    for i, h in enumerate(halves):
        tgt, first = where[i]
        outs[tgt] = lax.dynamic_update_slice(outs[tgt], h, (first + c * h.shape[0], 0))
    return outs


def all_reduce_small(name, packed):
    rows, width = packed.shape

    def body(x_ref, o_ref, gathered, send_sems, recv_sems):
        x, y, c, _ = _place()
        me = 4 * x + 2 * y + c
        gathered[me] = x_ref[...]
        flips = [(fx, fy, fc) for fx in (0, 1) for fy in (0, 1) for fc in (0, 1)][1:]

        def copy(r, slot, to):
            return pltpu.make_async_remote_copy(
                src_ref=x_ref, dst_ref=gathered.at[slot], send_sem=send_sems.at[r], recv_sem=recv_sems.at[r],
                device_id=to, device_id_type=MESH)

        def peer(f):
            return (x ^ f[0], y ^ f[1], c ^ f[2])

        sent = [copy(r, me, peer(f)) for r, f in enumerate(flips)]
        for cp in sent:
            cp.start()
        for r, f in enumerate(flips):
            px, py, pc = peer(f)
            copy(r, 4 * px + 2 * py + pc, peer(f)).wait_recv()
        for cp in sent:
            cp.wait_send()
        acc = gathered[0]
        for k in range(1, N_DEV):
            acc = acc + gathered[k]
        o_ref[...] = acc

    vmem = pl.BlockSpec(memory_space=pltpu.VMEM)
    return pl.pallas_call(
        body, name=name, in_specs=[vmem], out_specs=vmem, out_shape=S((rows, width), F32),
        scratch_shapes=[pltpu.VMEM((N_DEV, rows, width), F32), pltpu.SemaphoreType.DMA((N_DEV - 1,)),
                        pltpu.SemaphoreType.DMA((N_DEV - 1,))],
    )(packed)


def _rope_tables(positions):
    inv_freq = 1.0 / (ROPE_THETA ** (jnp.arange(0, ROPE, 2, dtype=F32) / ROPE))
    ang = positions.astype(F32)[:, None] * inv_freq
    return jnp.cos(ang), jnp.sin(ang)


def _unstack_cols(w):
    k4, k, n4 = w.shape
    return jnp.transpose(w, (1, 0, 2)).reshape(k, k4 * n4)


def _stack_cols(w):
    k, n = w.shape
    return jnp.transpose(w.reshape(k, N_CHIPS, n // N_CHIPS), (1, 0, 2))


def kernel(x, positions, mla_norm, mla_w_in, mla_g_cq, mla_g_ckv, mla_w_uq, mla_w_ukv, mla_w_o, conv_norm, conv_w_in, conv_w, conv_w_out, ffn_norm, ffn_w_gate, ffn_w_up, ffn_w_down, final_norm, loss_target, m_mla_norm, m_mla_w_in, m_mla_g_cq, m_mla_g_ckv, m_mla_w_uq, m_mla_w_ukv, m_mla_w_o, m_conv_norm, m_conv_w_in, m_conv_w, m_conv_w_out, m_ffn_norm, m_ffn_w_gate, m_ffn_w_up, m_ffn_w_down, m_final_norm, v_mla_norm, v_mla_w_in, v_mla_g_cq, v_mla_g_ckv, v_mla_w_uq, v_mla_w_ukv, v_mla_w_o, v_conv_norm, v_conv_w_in, v_conv_w, v_conv_w_out, v_ffn_norm, v_ffn_w_gate, v_ffn_w_up, v_ffn_w_down, v_final_norm):
    weights = dict(mla_norm=mla_norm, mla_w_in=mla_w_in, mla_g_cq=mla_g_cq, mla_g_ckv=mla_g_ckv, mla_w_uq=mla_w_uq,
                   mla_w_ukv=mla_w_ukv, mla_w_o=mla_w_o, conv_norm=conv_norm, conv_w_in=conv_w_in, conv_w=conv_w,
                   conv_w_out=conv_w_out, ffn_norm=ffn_norm, ffn_w_gate=ffn_w_gate, ffn_w_up=ffn_w_up,
                   ffn_w_down=ffn_w_down, final_norm=final_norm)
    m_in = dict(mla_norm=m_mla_norm, mla_w_in=m_mla_w_in, mla_g_cq=m_mla_g_cq, mla_g_ckv=m_mla_g_ckv, mla_w_uq=m_mla_w_uq,
                mla_w_ukv=m_mla_w_ukv, mla_w_o=m_mla_w_o, conv_norm=m_conv_norm, conv_w_in=m_conv_w_in, conv_w=m_conv_w,
                conv_w_out=m_conv_w_out, ffn_norm=m_ffn_norm, ffn_w_gate=m_ffn_w_gate, ffn_w_up=m_ffn_w_up,
                ffn_w_down=m_ffn_w_down, final_norm=m_final_norm)
    v_in = dict(mla_norm=v_mla_norm, mla_w_in=v_mla_w_in, mla_g_cq=v_mla_g_cq, mla_g_ckv=v_mla_g_ckv, mla_w_uq=v_mla_w_uq,
                mla_w_ukv=v_mla_w_ukv, mla_w_o=v_mla_w_o, conv_norm=v_conv_norm, conv_w_in=v_conv_w_in, conv_w=v_conv_w,
                conv_w_out=v_conv_w_out, ffn_norm=v_ffn_norm, ffn_w_gate=v_ffn_w_gate, ffn_w_up=v_ffn_w_up,
                ffn_w_down=v_ffn_w_down, final_norm=v_final_norm)
    big = ["mla_w_in", "mla_w_uq", "mla_w_ukv", "mla_w_o", "conv_w_in", "conv_w_out", "ffn_w_gate", "ffn_w_up", "ffn_w_down"]
    order = list(weights)

    t, d = x.shape[1], x.shape[2]
    h0 = x.reshape(t, d)
    target = loss_target.reshape(t, d)
    cos, sin = _rope_tables(positions.reshape(t))

    def rows2d(a):
        return a.reshape(-1, a.shape[-1])

    first, later = big[:4], big[4:]
    shards = {n: rows2d(weights[n]).astype(BF16) for n in big}
    gathered = dict(zip(first, gather_weight_shards([shards[n] for n in first])))
    w_in = gathered["mla_w_in"].reshape(-1, gathered["mla_w_in"].shape[-1])
    w_uq = _unstack_cols(gathered["mla_w_uq"])
    w_ukv = _unstack_cols(gathered["mla_w_ukv"])
    w_o = gathered["mla_w_o"].reshape(-1, d)

    chip = 2 * lax.axis_index("x") + lax.axis_index("y")
    core = lax.axis_index("c")
    d4 = d // N_CHIPS
    first_core = (core == 0).astype(F32)

    def place_shard(shard):
        full = jnp.zeros((shard.shape[0], d), F32)
        return lax.dynamic_update_slice(full, shard * first_core, (0, chip * d4))

    def pack_rows(rows):
        idx = lax.broadcasted_iota(jnp.int32, (SMALL_ROWS, d), 0)
        out = jnp.zeros((SMALL_ROWS, d), F32)
        for r, row in enumerate(rows):
            out = out + jnp.where(idx == r, row, 0.0)
        return out

    cw = place_shard(conv_w.reshape(3, d4))
    pre = all_reduce_small("all_gather_conv_small", pack_rows([place_shard(conv_norm.reshape(1, d4)), cw[0:1], cw[1:2], cw[2:3]]))
    conv_norm_full = pre[0:1]
    conv_w_full = pre[1:4]

    a0 = rms_fwd("mla_norm_fwd", h0, mla_norm)
    proj, cq, ckv, kr = mla_in_proj("mla_in_proj", a0, w_in, mla_g_cq, mla_g_ckv, cos, sin)
    q = linear("mla_q_up", cq, w_uq, F32)
    kv = linear("mla_kv_up", ckv, w_ukv, BF16)
    qh, kh, vh, conv_arriving = qkv_heads("qkv_heads", q, kv, kr, cos, sin, [shards[n] for n in later[:2]])
    attn, lse, ffn_arriving = attention_fwd("attention_fwd", qh, kh, vh, [shards[n] for n in later[2:]])
    gathered.update(zip(later, _fill_own_slot(forward_to_sibling(conv_arriving + ffn_arriving), [shards[n][None] for n in later])))
    cw_in = _unstack_cols(gathered["conv_w_in"])
    cw_out = gathered["conv_w_out"].reshape(-1, d)
    wg_all, wu_all, wd_all = gathered["ffn_w_gate"], gathered["ffn_w_up"], gathered["ffn_w_down"]
    h1, a1 = linear("mla_out_proj", attn, w_o, F32, resid=h0, next_gain=ffn_norm[0:1])

    def ffn_forward(tag, h, a, layer, next_gain):
        g, u, z = ffn_up(f"ffn{tag}_up", a, wg_all, wu_all, layer)
        return g, u, z, ffn_down(f"ffn{tag}_down", z, wd_all, layer, h, next_gain)

    g0, u0, z0, (h2, a2) = ffn_forward(0, h1, a1, 0, conv_norm_full)
    bcx = conv_in_proj("conv_in_proj", a2, cw_in)
    yc = conv_fwd("conv_fwd", bcx, conv_w_full)
    h3, a3 = linear("conv_out_proj", yc, cw_out, F32, resid=h2, next_gain=ffn_norm[1:2])
    g1, u1, z1 = ffn_up("ffn1_up", a3, wg_all, wu_all, 1)
    dh4, d_final_norm, loss_local = ffn_down_loss("ffn1_down_loss", z1, wd_all, 1, h3, final_norm.reshape(1, d), target)

    def ffn_backward(tag, dh, h, layer, a, g, u, z):
        dg, du = ffn_bwd_hidden(f"ffn{tag}_bwd_hidden", dh, wd_all, layer, g, u)
        d_wd = ffn_wgrad_down(f"ffn{tag}_wgrad_down", z, dh)
        dh_prev, d_norm = ffn_bwd_input(f"ffn{tag}_bwd_input", dg, du, wg_all, wu_all, layer, h, ffn_norm[layer:layer + 1], dh)
        d_wg = ffn_wgrad_up(f"ffn{tag}_wgrad_gate", a, dg)
        d_wu = ffn_wgrad_up(f"ffn{tag}_wgrad_up", a, du)
        return dh_prev, d_norm, [d_wg, d_wu, d_wd]

    def reduce_to_pair_sums(tag, local):
        from_sibling = sibling_swap_halves(f"sibling_swap_{tag}", local)
        return [add_halves(f"pair_sum_{tag}{i}", g, r) for i, (g, r) in enumerate(zip(local, from_sibling))]

    def reduce_from_chips(tag, pair_sums, arrived, targets, where):
        from_chips = _fill_own_slot(arrived, _own_slots(pair_sums))
        my_halves = [sum_chips(f"chip_sum_{tag}{i}", p) for i, p in enumerate(from_chips)]
        return sibling_join_halves(f"sibling_join_{tag}", my_halves, targets, where)

    dh3, d_ffn_norm1, ffn1_grads = ffn_backward(1, dh4, h3, 1, a3, g1, u1, z1)

    dyc = linear_nt("conv_out_bwd_input", dh3, cw_out, F32)
    d_cw_out = wgrad("conv_out_wgrad", yc, dh3)
    dbcx, d_conv_w = conv_bwd("conv_bwd", bcx, conv_w_full, dyc)
    dh2, d_conv_norm = conv_in_bwd_input("conv_in_bwd_input", dbcx, cw_in, h2, conv_norm_full, dh3)
    d_cw_in = conv_in_wgrad("conv_in_wgrad", a2, dbcx)

    dh1, d_ffn_norm0, ffn0_grads = ffn_backward(0, dh2, h1, 0, a1, g0, u0, z0)

    d_attn = linear_nt("mla_out_bwd_input", dh1, w_o, F32)
    d_w_o = wgrad("mla_out_wgrad", attn, dh1)
    rest_pairs = reduce_to_pair_sums("rest", [_stack_cols(d_cw_in), d_cw_out.reshape(N_CHIPS, -1, d)] + ffn1_grads + ffn0_grads
                                     + [d_w_o.reshape(N_CHIPS, -1, d)])
    delta = attention_delta("attention_delta", d_attn, attn)
    dqh, dkh, dvh, rest_arrived = attention_bwd("attention_bwd", qh, kh, vh, d_attn, lse, delta, rest_pairs)
    dq, dkv, dkr = qkv_heads_bwd("qkv_heads_bwd", dqh, dkh, dvh, cos, sin)
    dcq = linear_nt("mla_q_up_bwd_input", dq, w_uq, F32)
    d_w_uq = wgrad("mla_q_up_wgrad", cq, dq)
    dckv = linear_nt("mla_kv_up_bwd_input", dkv, w_ukv, F32)
    d_w_ukv = wgrad("mla_kv_up_wgrad", ckv, dkv)
    dproj, d_g_cq, d_g_ckv = mla_mid_bwd("mla_mid_bwd", proj, mla_g_cq, mla_g_ckv, dcq, dckv, dkr, cos, sin)
    d_w_in = wgrad("mla_in_wgrad", a0, dproj)
    grad_x, d_mla_norm = linear_nt_norm_bwd("mla_in_bwd_input", dproj, w_in, h0, mla_norm, dh1)

    def shard_shape(n):
        return rows2d(weights[n]).shape

    rd, rf = ffn0_grads[0].shape[1], ffn0_grads[2].shape[1]
    rest_where = [(0, 0), (1, 0), (2, rd), (3, rd), (4, rf), (2, 0), (3, 0), (4, 0), (5, 0)]
    rest_names = later + ["mla_w_o"]
    grads = dict(zip(rest_names, reduce_from_chips("rest", rest_pairs, rest_arrived, [shard_shape(n) for n in rest_names], rest_where)))

    mla_pairs = reduce_to_pair_sums("mla", [d_w_in.reshape(N_CHIPS, -1, d_w_in.shape[-1]), _stack_cols(d_w_uq), _stack_cols(d_w_ukv)])
    grads.update(zip(first[:3], reduce_from_chips("mla", mla_pairs, scatter_to_owner_chips(mla_pairs),
                                                  [shard_shape(n) for n in first[:3]], [(i, 0) for i in range(3)])))

    def pad_row(v):
        return jnp.pad(v, ((0, 0), (0, d - v.shape[1])))

    small = all_reduce_small("all_reduce_small_grads", pack_rows([
        d_mla_norm, pad_row(d_g_cq), pad_row(d_g_ckv), d_ffn_norm0, d_ffn_norm1, d_final_norm, d_conv_norm,
        d_conv_w[0:1], d_conv_w[1:2], d_conv_w[2:3], jnp.broadcast_to(loss_local, (1, d))]))
    loss = small[10, 0]
    grads["mla_norm"] = small[0:1]
    grads["mla_g_cq"] = small[1:2, :mla_g_cq.shape[1]]
    grads["mla_g_ckv"] = small[2:3, :mla_g_ckv.shape[1]]
    grads["ffn_norm"] = small[3:5]
    grads["final_norm"] = small[5:6]
    grads["conv_norm"] = lax.dynamic_slice(small[6:7], (0, chip * d4), (1, d4))
    grads["conv_w"] = lax.dynamic_slice(small[7:10], (0, chip * d4), (3, d4))

    outs_g, outs_d, outs_m, outs_v = [], [], [], []
    for n in order:
        w = weights[n]
        delta_w, new_m, new_v = adamw(f"adamw_{n}", rows2d(w), grads[n].reshape(rows2d(w).shape), rows2d(m_in[n]), rows2d(v_in[n]))
        outs_g.append(grads[n].reshape(w.shape))
        outs_d.append(delta_w.reshape(w.shape))
        outs_m.append(new_m.reshape(w.shape))
        outs_v.append(new_v.reshape(w.shape))
    return (loss, grad_x.reshape(x.shape), *outs_g, *outs_d, *outs_m, *outs_v)
```

```python
import math

import jax
import jax.numpy as jnp
from jax import lax
from jax.experimental import pallas as pl
from jax.experimental.pallas import tpu as pltpu

F32 = jnp.float32
BF16 = jnp.bfloat16
S = jax.ShapeDtypeStruct

N_HEADS = 8
NOPE = 128
ROPE = 64
HALF = ROPE // 2
VDIM = 128
QK = NOPE + ROPE
CHUNK = 64
ROPE_THETA = 10000.0
RMS_EPS = 1e-6
ADAM_LR = 0.001
ADAM_B1 = 0.9
ADAM_B2 = 0.999
ADAM_EPS = 1e-08
ADAM_WD = 0.01
ADAM_STEP = 10

N_CHIPS = 4
N_DEV = 8
MASK_VALUE = -1e30
SCORE_SCALE = 1.0 / math.sqrt(QK)
LOG2_E = math.log2(math.e)
SCORE_SCALE_LOG2 = SCORE_SCALE * LOG2_E
VMEM_LIMIT = 48 * 1024 * 1024
VMEM_LIMIT_WHOLE_HEAD = 58 * 1024 * 1024
ATT_BLOCK = 512
SMALL_ROWS = 16

_NN = (((1,), (0,)), ((), ()))
_NT = (((1,), (1,)), ((), ()))
_TN = (((0,), (0,)), ((), ()))
MESH = pl.DeviceIdType.MESH
ANY = pl.BlockSpec(memory_space=pl.ANY)


def _params(n_axes, vmem_limit=VMEM_LIMIT):
    return pltpu.CompilerParams(dimension_semantics=("arbitrary",) * n_axes, vmem_limit_bytes=vmem_limit)


def _tile(n, cap, mult=8):
    for t in range(min(cap, n), 0, -1):
        if n % t == 0 and t % mult == 0:
            return t
    return n


def _sigmoid(x):
    return 1.0 / (1.0 + jnp.exp(-x))


def _mm(name, a_ops, b_ops, products, dims, grid, k_axis, outs, acc_shape, epilogue, extra_ops=()):
    na, nb, ne, no = len(a_ops), len(b_ops), len(extra_ops), len(outs)
    n_acc = 1 + max(c for _, _, c in products)
    nk = 1 if k_axis is None else grid[k_axis]

    def body(*refs):
        a_refs = refs[:na]
        b_refs = refs[na:na + nb]
        e_refs = refs[na + nb:na + nb + ne]
        o_refs = refs[na + nb + ne:na + nb + ne + no]
        acc_refs = refs[na + nb + ne + no:]

        def partial_sums():
            vals = [None] * n_acc
            for ai, bi, ci in products:
                d = lax.dot_general(a_refs[ai][...].astype(BF16), b_refs[bi][...].astype(BF16), dims,
                                    preferred_element_type=F32)
                vals[ci] = d if vals[ci] is None else vals[ci] + d
            return vals

        if nk == 1:
            epilogue(partial_sums(), e_refs, o_refs)
        else:
            k = pl.program_id(k_axis)

            @pl.when(k == 0)
            def _():
                for acc in acc_refs:
                    acc[...] = jnp.zeros_like(acc)

            for acc, v in zip(acc_refs, partial_sums()):
                acc[...] += v

            @pl.when(k == nk - 1)
            def _():
                epilogue([acc[...] for acc in acc_refs], e_refs, o_refs)

    ops = list(a_ops) + list(b_ops) + list(extra_ops)
    return pl.pallas_call(
        body, name=name, grid=grid,
        in_specs=[s for _, s in ops], out_specs=[s for _, s in outs], out_shape=[o for o, _ in outs],
        scratch_shapes=[pltpu.VMEM(acc_shape, F32) for _ in range(n_acc if nk > 1 else 0)],
        compiler_params=_params(len(grid)),
    )(*[a for a, _ in ops])


def _store(accs, e_refs, o_refs):
    o_refs[0][...] = accs[0].astype(o_refs[0].dtype)


def linear(name, x, w, out_dtype, resid=None, next_gain=None):
    t, k = x.shape
    n = w.shape[1]
    tm = _tile(t, 512)
    tn = n if n <= 2048 else _tile(n, 1024, 128)
    tile = pl.BlockSpec((tm, tn), lambda j, i: (i, j))
    extra = [] if resid is None else [(resid, tile)]
    outs = [(S((t, n), out_dtype), tile)]
    if next_gain is not None:
        assert tn == n
        extra.append((next_gain, pl.BlockSpec((1, n), lambda j, i: (0, 0))))
        outs.append((S((t, n), BF16), tile))

    def epilogue(accs, e_refs, o_refs):
        y = accs[0] if resid is None else e_refs[0][...] + accs[0]
        o_refs[0][...] = y.astype(out_dtype)
        if next_gain is not None:
            o_refs[1][...] = (y * _rstd(y) * e_refs[-1][...]).astype(BF16)

    res = _mm(name, [(x, pl.BlockSpec((tm, k), lambda j, i: (i, 0)))], [(w, pl.BlockSpec((k, tn), lambda j, i: (0, j)))],
              [(0, 0, 0)], _NN, (n // tn, t // tm), None, outs, None, epilogue, extra)
    return res[0] if next_gain is None else res


def linear_nt(name, dy, w, out_dtype):
    t, n = dy.shape
    k = w.shape[0]
    tm = _tile(t, 512)
    tc = n if n <= 2048 else _tile(n, 1024, 128)
    return _mm(name, [(dy, pl.BlockSpec((tm, tc), lambda i, c: (i, c)))], [(w, pl.BlockSpec((k, tc), lambda i, c: (0, c)))],
               [(0, 0, 0)], _NT, (t // tm, n // tc), 1,
               [(S((t, k), out_dtype), pl.BlockSpec((tm, k), lambda i, c: (i, 0)))], (tm, k), _store)[0]


def wgrad(name, x, dy):
    t, k = x.shape
    n = dy.shape[1]
    tk = _tile(t, 512)
    tn = n if n <= 1024 else _tile(n, 1024, 128)
    return _mm(name, [(x, pl.BlockSpec((tk, k), lambda j, s: (s, 0)))], [(dy, pl.BlockSpec((tk, tn), lambda j, s: (s, j)))],
               [(0, 0, 0)], _TN, (n // tn, t // tk), 1,
               [(S((k, n), BF16), pl.BlockSpec((k, tn), lambda j, s: (0, j)))], (k, tn), _store)[0]


def _resident(shape, index_map):
    return pl.BlockSpec(shape, index_map, pipeline_mode=pl.Buffered(1))


def ffn_up(name, a, wg_all, wu_all, layer):
    t, d = a.shape
    f4 = wg_all.shape[2]
    tm = _tile(t, 512)
    w_spec = _resident((N_CHIPS, d, f4), lambda i: (0, layer, 0))
    h_spec = pl.BlockSpec((N_CHIPS, tm, f4), lambda i: (0, i, 0))

    def body(a_ref, wg_ref, wu_ref, zg_ref, zu_ref, z_ref):
        av = a_ref[...]
        for k in range(N_CHIPS):
            g = jnp.dot(av, wg_ref[k], preferred_element_type=F32)
            u = jnp.dot(av, wu_ref[k], preferred_element_type=F32)
            sg = _sigmoid(g)
            silu = g * sg
            zg_ref[k] = (u * (sg * (1.0 + g * (1.0 - sg)))).astype(BF16)
            zu_ref[k] = silu.astype(BF16)
            z_ref[k] = (silu * u).astype(BF16)

    return pl.pallas_call(
        body, name=name, grid=(t // tm,), in_specs=[pl.BlockSpec((tm, d), lambda i: (i, 0)), w_spec, w_spec],
        out_specs=[h_spec] * 3, out_shape=[S((N_CHIPS, t, f4), BF16)] * 3, compiler_params=_params(1))(a, wg_all, wu_all)


def ffn_down(name, z, wd_all, layer, resid, next_gain=None):
    _, t, f4 = z.shape
    d = wd_all.shape[2]
    tm = _tile(t, 512)
    row = pl.BlockSpec((tm, d), lambda i: (i, 0))
    normed = next_gain is not None

    def body(z_ref, wd_ref, r_ref, *refs):
        acc = r_ref[...]
        for k in range(N_CHIPS):
            acc = acc + jnp.dot(z_ref[k], wd_ref[k], preferred_element_type=F32)
        refs[-2 if normed else -1][...] = acc
        if normed:
            refs[-1][...] = (acc * _rstd(acc) * refs[0][...]).astype(BF16)

    res = pl.pallas_call(
        body, name=name, grid=(t // tm,),
        in_specs=[pl.BlockSpec((N_CHIPS, tm, f4), lambda i: (0, i, 0)), _resident((N_CHIPS, f4, d), lambda i: (0, layer, 0)), row]
        + ([pl.BlockSpec((1, d), lambda i: (0, 0))] if normed else []),
        out_specs=[row] * (2 if normed else 1), out_shape=[S((t, d), F32)] + ([S((t, d), BF16)] if normed else []),
        compiler_params=_params(1))(z, wd_all, resid, *([next_gain] if normed else []))
    return res if normed else res[0]


def ffn_bwd_hidden(name, dh, wd_all, layer, zg, zu):
    t, d = dh.shape
    f4 = zg.shape[2]
    tm = _tile(t, 512)
    h_spec = pl.BlockSpec((N_CHIPS, tm, f4), lambda i: (0, i, 0))

    def body(dh_ref, wd_ref, zg_ref, zu_ref, dg_ref, du_ref):
        dhb = dh_ref[...].astype(BF16)
        for k in range(N_CHIPS):
            dz = lax.dot_general(dhb, wd_ref[k], _NT, preferred_element_type=F32)
            dg_ref[k] = (dz * zg_ref[k].astype(F32)).astype(BF16)
            du_ref[k] = (dz * zu_ref[k].astype(F32)).astype(BF16)

    return pl.pallas_call(
        body, name=name, grid=(t // tm,),
        in_specs=[pl.BlockSpec((tm, d), lambda i: (i, 0)), _resident((N_CHIPS, f4, d), lambda i: (0, layer, 0)), h_spec, h_spec],
        out_specs=[h_spec] * 2, out_shape=[S((N_CHIPS, t, f4), BF16)] * 2, compiler_params=_params(1))(dh, wd_all, zg, zu)


def _norm_bwd_specs(tm, d):
    row = pl.BlockSpec((tm, d), lambda i: (i, 0))
    vec = pl.BlockSpec((1, d), lambda i: (0, 0))
    return [row, vec, row], [row, vec]


def _norm_bwd_tail(da, h_ref, g_ref, dhi_ref, dho_ref, dgain_ref):
    dx, dgain = _rms_bwd(h_ref[...], g_ref[...], da)
    dho_ref[...] = dhi_ref[...] + dx

    @pl.when(pl.program_id(0) == 0)
    def _():
        dgain_ref[...] = jnp.zeros_like(dgain_ref)

    dgain_ref[...] += dgain


def ffn_bwd_input(name, dg, du, wg_all, wu_all, layer, h, gain, dh_in):
    _, t, f4 = dg.shape
    d = h.shape[1]
    tm = _tile(t, 512)
    h_spec = pl.BlockSpec((N_CHIPS, tm, f4), lambda i: (0, i, 0))
    w_spec = _resident((N_CHIPS, d, f4), lambda i: (0, layer, 0))
    tail_in, tail_out = _norm_bwd_specs(tm, d)

    def body(dg_ref, du_ref, wg_ref, wu_ref, *tail):
        acc = jnp.zeros((tm, d), F32)
        for k in range(N_CHIPS):
            acc = acc + lax.dot_general(dg_ref[k], wg_ref[k], _NT, preferred_element_type=F32)
            acc = acc + lax.dot_general(du_ref[k], wu_ref[k], _NT, preferred_element_type=F32)
        _norm_bwd_tail(acc, *tail)

    return pl.pallas_call(
        body, name=name, grid=(t // tm,), in_specs=[h_spec, h_spec, w_spec, w_spec] + tail_in, out_specs=tail_out,
        out_shape=[S((t, d), F32), S((1, d), F32)], compiler_params=_params(1))(dg, du, wg_all, wu_all, h, gain, dh_in)


def ffn_wgrad_up(name, a, dy):
    t, d = a.shape
    f4 = dy.shape[2]
    tk = _tile(t, 512)
    nt = t // tk

    def body(a_ref, dy_ref, o_ref, acc):
        s = pl.program_id(0)

        @pl.when(s == 0)
        def _():
            acc[...] = jnp.zeros_like(acc)

        at = a_ref[...].T
        for k in range(N_CHIPS):
            acc[k] += jnp.dot(at, dy_ref[k], preferred_element_type=F32)

        @pl.when(s == nt - 1)
        def _():
            o_ref[...] = acc[...].astype(BF16)

    return pl.pallas_call(
        body, name=name, grid=(nt,),
        in_specs=[pl.BlockSpec((tk, d), lambda s: (s, 0)), pl.BlockSpec((N_CHIPS, tk, f4), lambda s: (0, s, 0))],
        out_specs=pl.BlockSpec((N_CHIPS, d, f4), lambda s: (0, 0, 0)), out_shape=S((N_CHIPS, d, f4), BF16),
        scratch_shapes=[pltpu.VMEM((N_CHIPS, d, f4), F32)], compiler_params=_params(1))(a, dy)


def ffn_wgrad_down(name, z, dh):
    _, t, f4 = z.shape
    d = dh.shape[1]
    tk = _tile(t, 512)
    nt = t // tk

    def body(z_ref, dh_ref, o_ref, acc):
        s = pl.program_id(0)

        @pl.when(s == 0)
        def _():
            acc[...] = jnp.zeros_like(acc)

        dhb = dh_ref[...].astype(BF16)
        for k in range(N_CHIPS):
            acc[k] += lax.dot_general(z_ref[k], dhb, _TN, preferred_element_type=F32)

        @pl.when(s == nt - 1)
        def _():
            o_ref[...] = acc[...].astype(BF16)

    return pl.pallas_call(
        body, name=name, grid=(nt,),
        in_specs=[pl.BlockSpec((N_CHIPS, tk, f4), lambda s: (0, s, 0)), pl.BlockSpec((tk, d), lambda s: (s, 0))],
        out_specs=pl.BlockSpec((N_CHIPS, f4, d), lambda s: (0, 0, 0)), out_shape=S((N_CHIPS, f4, d), BF16),
        scratch_shapes=[pltpu.VMEM((N_CHIPS, f4, d), F32)], compiler_params=_params(1))(z, dh)


def conv_in_proj(name, a, w):
    t, d = a.shape
    tm = _tile(t, 512)
    return _mm(name, [(a, pl.BlockSpec((tm, d), lambda j, i: (i, 0)))], [(w, pl.BlockSpec((d, d), lambda j, i: (0, j)))],
               [(0, 0, 0)], _NN, (3, t // tm), None,
               [(S((3, t, d), F32), pl.BlockSpec((None, tm, d), lambda j, i: (j, i, 0)))], None, _store)[0]


def conv_in_bwd_input(name, dbcx, w, h, gain, dh_in):
    _, t, d = dbcx.shape
    tm = _tile(t, 512)
    tail_in, tail_out = _norm_bwd_specs(tm, d)

    def body(g_ref, w_ref, *tail):
        acc = jnp.zeros((tm, d), F32)
        for j in range(3):
            acc = acc + lax.dot_general(g_ref[j], w_ref[:, j * d:(j + 1) * d], _NT, preferred_element_type=F32)
        _norm_bwd_tail(acc, *tail)

    return pl.pallas_call(
        body, name=name, grid=(t // tm,),
        in_specs=[pl.BlockSpec((3, tm, d), lambda i: (0, i, 0)), _resident((d, 3 * d), lambda i: (0, 0))] + tail_in,
        out_specs=tail_out, out_shape=[S((t, d), F32), S((1, d), F32)], compiler_params=_params(1))(dbcx, w, h, gain, dh_in)


def linear_nt_norm_bwd(name, dy, w, h, gain, dh_in):
    t, n = dy.shape
    k = w.shape[0]
    tm = _tile(t, 512)
    tail_in, tail_out = _norm_bwd_specs(tm, k)

    def body(dy_ref, w_ref, *tail):
        _norm_bwd_tail(lax.dot_general(dy_ref[...].astype(BF16), w_ref[...], _NT, preferred_element_type=F32), *tail)

    return pl.pallas_call(
        body, name=name, grid=(t // tm,),
        in_specs=[pl.BlockSpec((tm, n), lambda i: (i, 0)), _resident((k, n), lambda i: (0, 0))] + tail_in,
        out_specs=tail_out, out_shape=[S((t, k), F32), S((1, k), F32)], compiler_params=_params(1))(dy, w, h, gain, dh_in)


def conv_in_wgrad(name, a, dbcx):
    t, d = a.shape
    tk = _tile(t, 512)
    nt = t // tk

    def body(a_ref, g_ref, o_ref, acc):
        s = pl.program_id(0)

        @pl.when(s == 0)
        def _():
            acc[...] = jnp.zeros_like(acc)

        at = a_ref[...].T
        for j in range(3):
            acc[:, j * d:(j + 1) * d] += jnp.dot(at, g_ref[j], preferred_element_type=F32)

        @pl.when(s == nt - 1)
        def _():
            o_ref[...] = acc[...].astype(BF16)

    return pl.pallas_call(
        body, name=name, grid=(nt,),
        in_specs=[pl.BlockSpec((tk, d), lambda s: (s, 0)), pl.BlockSpec((3, tk, d), lambda s: (0, s, 0))],
        out_specs=pl.BlockSpec((d, 3 * d), lambda s: (0, 0)), out_shape=S((d, 3 * d), BF16),
        scratch_shapes=[pltpu.VMEM((d, 3 * d), F32)], compiler_params=_params(1))(a, dbcx)


def _rstd(x):
    return lax.rsqrt(jnp.mean(x * x, axis=-1, keepdims=True) + RMS_EPS)


def _rms_bwd(x, g, dy):
    r = _rstd(x)
    xhat = x * r
    dgain = jnp.sum(dy * xhat, axis=0, keepdims=True)
    dxh = dy * g
    dx = r * (dxh - xhat * jnp.mean(dxh * xhat, axis=-1, keepdims=True))
    return dx, dgain


def rms_fwd(name, h, g):
    t, d = h.shape
    tr = _tile(t, 512)

    def body(h_ref, g_ref, a_ref):
        x = h_ref[...]
        a_ref[...] = (x * _rstd(x) * g_ref[...]).astype(BF16)

    return pl.pallas_call(
        body, name=name, grid=(t // tr,),
        in_specs=[pl.BlockSpec((tr, d), lambda i: (i, 0)), pl.BlockSpec((1, d), lambda i: (0, 0))],
        out_specs=pl.BlockSpec((tr, d), lambda i: (i, 0)), out_shape=S((t, d), BF16), compiler_params=_params(1))(h, g)


def ffn_down_loss(name, z, wd_all, layer, resid, gain, target):
    _, t, f4 = z.shape
    d = wd_all.shape[2]
    tm = _tile(t, 512)

    def body(z_ref, wd_ref, r_ref, g_ref, t_ref, dh_ref, dg_ref, loss_ref):
        x = r_ref[...]
        for k in range(N_CHIPS):
            x = x + jnp.dot(z_ref[k], wd_ref[k], preferred_element_type=F32)
        g = g_ref[...]
        r = _rstd(x)
        xhat = x * r
        err = xhat * g - t_ref[...]
        dy = err * (1.0 / d)
        dxh = dy * g
        dh_ref[...] = r * (dxh - xhat * jnp.mean(dxh * xhat, axis=-1, keepdims=True))

        @pl.when(pl.program_id(0) == 0)
        def _():
            dg_ref[...] = jnp.zeros_like(dg_ref)
            loss_ref[...] = jnp.zeros_like(loss_ref)

        dg_ref[...] += jnp.sum(dy * xhat, axis=0, keepdims=True)
        per_token = jnp.mean(err * err, axis=-1, keepdims=True)
        loss_ref[...] += 0.5 * jnp.sum(per_token, axis=0, keepdims=True)

    row = pl.BlockSpec((tm, d), lambda i: (i, 0))
    vec = pl.BlockSpec((1, d), lambda i: (0, 0))
    one = pl.BlockSpec((1, 1), lambda i: (0, 0))
    return pl.pallas_call(
        body, name=name, grid=(t // tm,),
        in_specs=[pl.BlockSpec((N_CHIPS, tm, f4), lambda i: (0, i, 0)), _resident((N_CHIPS, f4, d), lambda i: (0, layer, 0)), row, vec, row],
        out_specs=[row, vec, one], out_shape=[S((t, d), F32), S((1, d), F32), S((1, 1), F32)],
        compiler_params=_params(1))(z, wd_all, resid, gain, target)


def mla_in_proj(name, a, w, g_cq, g_ckv, cos, sin):
    t, d = a.shape
    n = w.shape[1]
    ql, kl = g_cq.shape[1], g_ckv.shape[1]
    tr = _tile(t, 512)

    def body(a_ref, w_ref, gq_ref, gk_ref, c_ref, s_ref, p_ref, cq_ref, ckv_ref, kr_ref):
        p_ref[...] = jnp.dot(a_ref[...], w_ref[...], preferred_element_type=F32)
        xq = p_ref[:, 0:ql]
        cq_ref[...] = (xq * _rstd(xq) * gq_ref[...]).astype(BF16)
        xk = p_ref[:, ql:ql + kl]
        ckv_ref[...] = (xk * _rstd(xk) * gk_ref[...]).astype(BF16)
        k1 = p_ref[:, ql + kl:ql + kl + HALF]
        k2 = p_ref[:, ql + kl + HALF:ql + kl + ROPE]
        c = c_ref[...]
        s = s_ref[...]
        kr_ref[:, 0:HALF] = k1 * c - k2 * s
        kr_ref[:, HALF:ROPE] = k1 * s + k2 * c

    def row(w):
        return pl.BlockSpec((tr, w), lambda i: (i, 0))

    def vec(w):
        return pl.BlockSpec((1, w), lambda i: (0, 0))

    return pl.pallas_call(
        body, name=name, grid=(t // tr,),
        in_specs=[row(d), _resident((d, n), lambda i: (0, 0)), vec(ql), vec(kl), row(HALF), row(HALF)],
        out_specs=[row(n), row(ql), row(kl), row(ROPE)],
        out_shape=[S((t, n), F32), S((t, ql), BF16), S((t, kl), BF16), S((t, ROPE), F32)],
        compiler_params=_params(1))(a, w, g_cq, g_ckv, cos, sin)


def mla_mid_bwd(name, proj, g_cq, g_ckv, dcq, dckv, dkr, cos, sin):
    t, n = proj.shape
    ql, kl = g_cq.shape[1], g_ckv.shape[1]
    tr = _tile(t, 512)

    def body(p_ref, gq_ref, gk_ref, dcq_ref, dckv_ref, dkr_ref, c_ref, s_ref, dp_ref, dgq_ref, dgk_ref):
        dxq, dgq = _rms_bwd(p_ref[:, 0:ql], gq_ref[...], dcq_ref[...])
        dp_ref[:, 0:ql] = dxq.astype(BF16)
        dxk, dgk = _rms_bwd(p_ref[:, ql:ql + kl], gk_ref[...], dckv_ref[...])
        dp_ref[:, ql:ql + kl] = dxk.astype(BF16)
        d1 = dkr_ref[:, 0:HALF]
        d2 = dkr_ref[:, HALF:ROPE]
        c = c_ref[...]
        s = s_ref[...]
        dp_ref[:, ql + kl:ql + kl + HALF] = (d1 * c + d2 * s).astype(BF16)
        dp_ref[:, ql + kl + HALF:ql + kl + ROPE] = (d2 * c - d1 * s).astype(BF16)

        @pl.when(pl.program_id(0) == 0)
        def _():
            dgq_ref[...] = jnp.zeros_like(dgq_ref)
            dgk_ref[...] = jnp.zeros_like(dgk_ref)

        dgq_ref[...] += dgq
        dgk_ref[...] += dgk

    def row(w):
        return pl.BlockSpec((tr, w), lambda i: (i, 0))

    def vec(w):
        return pl.BlockSpec((1, w), lambda i: (0, 0))

    return pl.pallas_call(
        body, name=name, grid=(t // tr,),
        in_specs=[row(n), vec(ql), vec(kl), row(ql), row(kl), row(ROPE), row(HALF), row(HALF)],
        out_specs=[row(n), vec(ql), vec(kl)], out_shape=[S((t, n), BF16), S((1, ql), F32), S((1, kl), F32)],
        compiler_params=_params(1))(proj, g_cq, g_ckv, dcq, dckv, dkr, cos, sin)


def qkv_heads(name, q, kv, kr, cos, sin, shards=()):
    t = q.shape[0]
    tr = _tile(t, 256)
    n = len(shards)

    def body(q_ref, kv_ref, kr_ref, c_ref, s_ref, *refs):
        src = refs[:n]
        qo_ref, ko_ref, vo_ref = refs[n:n + 3]
        if n:
            _ride_along(gather_ici_copies(src, refs[n + 3:2 * n + 3], *refs[2 * n + 3:]), (pl.program_id(0),), (t // tr,))
        c = c_ref[...]
        s = s_ref[...]
        krb = kr_ref[...].astype(BF16)
        for h in range(N_HEADS):
            q0 = h * QK
            qo_ref[h, :, 0:NOPE] = q_ref[:, q0:q0 + NOPE].astype(BF16)
            q1 = q_ref[:, q0 + NOPE:q0 + NOPE + HALF]
            q2 = q_ref[:, q0 + NOPE + HALF:q0 + QK]
            qo_ref[h, :, NOPE:NOPE + HALF] = (q1 * c - q2 * s).astype(BF16)
            qo_ref[h, :, NOPE + HALF:QK] = (q1 * s + q2 * c).astype(BF16)
            k0 = h * (NOPE + VDIM)
            ko_ref[h, :, 0:NOPE] = kv_ref[:, k0:k0 + NOPE]
            ko_ref[h, :, NOPE:QK] = krb
            vo_ref[h] = kv_ref[:, k0 + NOPE:k0 + NOPE + VDIM]

    def row(w):
        return pl.BlockSpec((tr, w), lambda i: (i, 0))

    def heads(w):
        return pl.BlockSpec((N_HEADS, tr, w), lambda i: (0, i, 0))

    outs = pl.pallas_call(
        body, name=name, grid=(t // tr,),
        in_specs=[row(N_HEADS * QK), row(N_HEADS * (NOPE + VDIM)), row(ROPE), row(HALF), row(HALF)] + [ANY] * n,
        out_specs=[heads(QK), heads(QK), heads(VDIM)] + [ANY] * n,
        out_shape=[S((N_HEADS, t, QK), BF16), S((N_HEADS, t, QK), BF16), S((N_HEADS, t, VDIM), BF16)]
        + [S((N_CHIPS,) + s.shape, s.dtype) for s in shards],
        scratch_shapes=[pltpu.SemaphoreType.DMA((n, 3)), pltpu.SemaphoreType.DMA((n, 3))] if n else [],
        compiler_params=_params(1))(q, kv, kr, cos, sin, *shards)
    return outs[0], outs[1], outs[2], list(outs[3:])


def qkv_heads_bwd(name, dq_h, dk_h, dv_h, cos, sin, halves=(), targets=(), where=()):
    t = dq_h.shape[1]
    tr = _tile(t, 256)
    n, nt = len(halves), len(targets)

    def body(dq_ref, dk_ref, dv_ref, c_ref, s_ref, *refs):
        q_ref, kv_ref, kr_ref = refs[n:n + 3]
        if n:
            _ride_along(join_copies(refs[:n], refs[n + 3:n + 3 + nt], where, *refs[n + 3 + nt:]), (pl.program_id(0),), (t // tr,))
        c = c_ref[...]
        s = s_ref[...]
        dkr = jnp.zeros((tr, ROPE), F32)
        for h in range(N_HEADS):
            q0 = h * QK
            q_ref[:, q0:q0 + NOPE] = dq_ref[h, :, 0:NOPE].astype(BF16)
            d1 = dq_ref[h, :, NOPE:NOPE + HALF]
            d2 = dq_ref[h, :, NOPE + HALF:QK]
            q_ref[:, q0 + NOPE:q0 + NOPE + HALF] = (d1 * c + d2 * s).astype(BF16)
            q_ref[:, q0 + NOPE + HALF:q0 + QK] = (d2 * c - d1 * s).astype(BF16)
            k0 = h * (NOPE + VDIM)
            kv_ref[:, k0:k0 + NOPE] = dk_ref[h, :, 0:NOPE].astype(BF16)
            kv_ref[:, k0 + NOPE:k0 + NOPE + VDIM] = dv_ref[h].astype(BF16)
            dkr = dkr + dk_ref[h, :, NOPE:QK]
        kr_ref[...] = dkr

    def row(w):
        return pl.BlockSpec((tr, w), lambda i: (i, 0))

    def heads(w):
        return pl.BlockSpec((N_HEADS, tr, w), lambda i: (0, i, 0))

    outs = pl.pallas_call(
        body, name=name, grid=(t // tr,),
        in_specs=[heads(QK), heads(QK), heads(VDIM), row(HALF), row(HALF)] + [ANY] * n,
        out_specs=[row(N_HEADS * QK), row(N_HEADS * (NOPE + VDIM)), row(ROPE)] + [ANY] * nt,
        out_shape=[S((t, N_HEADS * QK), BF16), S((t, N_HEADS * (NOPE + VDIM)), BF16), S((t, ROPE), F32)]
        + [S(tg, F32) for tg in targets],
        scratch_shapes=[pltpu.SemaphoreType.DMA((n,)), pltpu.SemaphoreType.DMA((n,))] if n else [],
        compiler_params=_params(1))(dq_h, dk_h, dv_h, cos, sin, *halves)
    return outs[0], outs[1], outs[2], _fill_own_halves(outs[3:], halves, where)


def _chunk_mask_t(q_start, k_start, bq, bk):
    kc = (k_start + lax.broadcasted_iota(jnp.int32, (bk, bq), 0)) // CHUNK
    qc = (q_start + lax.broadcasted_iota(jnp.int32, (bk, bq), 1)) // CHUNK
    return kc <= qc


def attention_fwd(name, q, k, v, shards=()):
    nh, t, _ = q.shape
    blk = ATT_BLOCK
    nq = t // blk
    n = len(shards)

    def body(q_ref, k_ref, v_ref, *refs):
        src = refs[:n]
        o_ref, lse_ref = refs[n:n + 2]
        dst = refs[n + 2:2 * n + 2]
        m_ref, l_ref, acc_ref, s_buf, p_buf, alpha_buf, bias_ref = refs[2 * n + 2:2 * n + 9]
        i = pl.program_id(1)
        if n:
            send_sems, recv_sems = refs[2 * n + 9:]
            _ride_along(gather_ici_copies(src, dst, send_sems, recv_sems), (pl.program_id(0), i), (nh, nq))

        @pl.when((pl.program_id(0) == 0) & (i == 0))
        def _():
            bias_ref[...] = jnp.where(_chunk_mask_t(0, 0, blk, blk), 0.0, MASK_VALUE)

        m_ref[...] = jnp.full_like(m_ref, MASK_VALUE)
        l_ref[...] = jnp.zeros_like(l_ref)
        acc_ref[...] = jnp.zeros_like(acc_ref)

        def rows(b):
            return pl.ds(pl.multiple_of(b * blk, blk), blk)

        def scores(b, slot):
            s_buf[slot] = lax.dot_general(k_ref[rows(b), :], q_ref[...], _NT, preferred_element_type=F32)

        def softmax(slot, diagonal):
            s = s_buf[slot]
            if diagonal:
                s = s + bias_ref[...]
            m_old = m_ref[...]
            m_new = jnp.maximum(m_old, jnp.max(s, axis=0, keepdims=True))
            p = jnp.exp2((s - m_new) * SCORE_SCALE_LOG2)
            alpha = jnp.exp2((m_old - m_new) * SCORE_SCALE_LOG2)
            l_ref[...] = alpha * l_ref[...] + jnp.sum(p, axis=0, keepdims=True)
            m_ref[...] = m_new
            alpha_buf[slot] = alpha
            p_buf[slot] = p.astype(BF16)

        def values(b, slot):
            pv = lax.dot_general(v_ref[rows(b), :], p_buf[slot], _TN, preferred_element_type=F32)
            acc_ref[...] = alpha_buf[slot] * acc_ref[...] + pv

        def step(t, slot):
            values(t - 2, slot)
            softmax(1 - slot, False)
            scores(t, slot)

        scores(0, 0)

        @pl.when(i == 0)
        def _():
            softmax(0, True)
            values(0, 0)

        @pl.when(i > 0)
        def _():
            scores(1, 1)
            softmax(0, False)
            steady = i - 1

            def pair(u, carry):
                step(2 + 2 * u, 0)
                step(3 + 2 * u, 1)
                return carry

            lax.fori_loop(0, steady // 2, pair, 0)

            @pl.when(steady % 2 == 1)
            def _():
                step(i, 0)

            last = i % 2
            softmax(last, True)
            values(i - 1, 1 - last)
            values(i, last)

        l = l_ref[...]
        o_ref[...] = (acc_ref[...] / l).T
        lse_ref[...] = m_ref[...] * SCORE_SCALE + jnp.log(l)

    outs = pl.pallas_call(
        body, name=name, grid=(nh, nq),
        in_specs=[pl.BlockSpec((None, blk, QK), lambda h, i: (h, i, 0)), pl.BlockSpec((None, t, QK), lambda h, i: (h, 0, 0)),
                  pl.BlockSpec((None, t, VDIM), lambda h, i: (h, 0, 0))] + [ANY] * n,
        out_specs=[pl.BlockSpec((blk, VDIM), lambda h, i: (i, h)),
                   pl.BlockSpec((None, None, 1, blk), lambda h, i: (h, i, 0, 0))] + [ANY] * n,
        out_shape=[S((t, nh * VDIM), F32), S((nh, nq, 1, blk), F32)] + [S((N_CHIPS,) + s.shape, s.dtype) for s in shards],
        scratch_shapes=[pltpu.VMEM((1, blk), F32), pltpu.VMEM((1, blk), F32), pltpu.VMEM((VDIM, blk), F32),
                        pltpu.VMEM((2, blk, blk), F32), pltpu.VMEM((2, blk, blk), BF16), pltpu.VMEM((2, 1, blk), F32),
                        pltpu.VMEM((blk, blk), F32)]
        + ([pltpu.SemaphoreType.DMA((n, 3)), pltpu.SemaphoreType.DMA((n, 3))] if n else []),
        compiler_params=_params(2))(q, k, v, *shards)
    return outs[0], outs[1], list(outs[2:])


def attention_out_bwd(name, dh, w_o, o):
    t, d = dh.shape
    n = w_o.shape[0]
    blk = ATT_BLOCK

    def body(dh_ref, w_ref, o_ref, do_ref, d_ref):
        do_ref[...] = lax.dot_general(dh_ref[...].astype(BF16), w_ref[...], _NT, preferred_element_type=F32)
        for h in range(N_HEADS):
            cols = slice(h * VDIM, (h + 1) * VDIM)
            d_ref[h] = jnp.sum((do_ref[:, cols] * o_ref[:, cols]).T, axis=0, keepdims=True)

    tile = pl.BlockSpec((blk, n), lambda i: (i, 0))
    return pl.pallas_call(
        body, name=name, grid=(t // blk,),
        in_specs=[pl.BlockSpec((blk, d), lambda i: (i, 0)), _resident((n, d), lambda i: (0, 0)), tile],
        out_specs=[tile, pl.BlockSpec((N_HEADS, None, 1, blk), lambda i: (0, i, 0, 0))],
        out_shape=[S((t, n), F32), S((N_HEADS, t // blk, 1, blk), F32)], compiler_params=_params(1))(dh, w_o, o)


def attention_bwd(name, q, k, v, do, lse, delta, parts=()):
    nh, t, _ = q.shape
    blk = ATT_BLOCK
    nq = t // blk
    n_pairs = nq * (nq + 1) // 2
    n = len(parts)
    scale = SCORE_SCALE

    def body(q_ref, k_ref, v_ref, do_ref, lse_ref, dl_ref, *refs):
        src = refs[:n]
        dq_ref, dk_ref, dv_ref = refs[n:n + 3]
        dst = refs[n + 3:2 * n + 3]
        s_buf, dp_buf, p_buf, ds_buf, bias_ref = refs[2 * n + 3:2 * n + 8]
        if n:
            send_sems, recv_sems = refs[2 * n + 8:]
            _ride_along(scatter_ici_copies(src, dst, send_sems, recv_sems), (pl.program_id(0),), (nh,))

        @pl.when(pl.program_id(0) == 0)
        def _():
            bias_ref[...] = jnp.where(_chunk_mask_t(0, 0, blk, blk), 0.0, MASK_VALUE)

        dq_ref[...] = jnp.zeros_like(dq_ref)
        dk_ref[...] = jnp.zeros_like(dk_ref)
        dv_ref[...] = jnp.zeros_like(dv_ref)

        def rows(x):
            return pl.ds(pl.multiple_of(x * blk, blk), blk)

        def after(jb):
            j, b = jb
            wrap = b == nq - 1 - j
            return jnp.where(wrap, j + 1, j), jnp.where(wrap, 0, b + 1)

        def products(jb, slot):
            j, b = jb
            s_buf[slot] = lax.dot_general(k_ref[rows(j), :], q_ref[rows(j + b), :], _NT, preferred_element_type=F32)
            dp_buf[slot] = lax.dot_general(v_ref[rows(j), :], do_ref[rows(j + b), :].astype(BF16), _NT, preferred_element_type=F32)

        def softmax_bwd(jb, slot):
            j, b = jb
            s = s_buf[slot] + bias_ref[...] * (b == 0).astype(F32)
            p = jnp.exp2(s * SCORE_SCALE_LOG2 - lse_ref[j + b] * LOG2_E)
            p_buf[slot] = p.astype(BF16)
            ds_buf[slot] = (p * (dp_buf[slot] - dl_ref[j + b]) * scale).astype(BF16)

        def gradients(jb, slot):
            j, b = jb
            dv_ref[rows(j), :] += jnp.dot(p_buf[slot], do_ref[rows(j + b), :].astype(BF16), preferred_element_type=F32)
            dk_ref[rows(j), :] += jnp.dot(ds_buf[slot], q_ref[rows(j + b), :], preferred_element_type=F32)
            dq_ref[rows(j + b), :] += lax.dot_general(ds_buf[slot], k_ref[rows(j), :], _TN, preferred_element_type=F32)

        def step(state, slot):
            third, second, first = state
            gradients(third, slot)
            softmax_bwd(second, 1 - slot)
            products(first, slot)
            return second, first, after(first)

        zero = jnp.int32(0)
        pair0 = (zero, zero)
        products(pair0, 0)
        if n_pairs == 1:
            softmax_bwd(pair0, 0)
            gradients(pair0, 0)
        else:
            pair1 = after(pair0)
            products(pair1, 1)
            softmax_bwd(pair0, 0)
            steady = n_pairs - 2
            state = lax.fori_loop(0, steady // 2, lambda u, st: step(step(st, 0), 1), (pair0, pair1, after(pair1)))
            if steady % 2:
                state = step(state, 0)
            before_last, last_pair, _ = state
            last = (n_pairs - 1) % 2
            softmax_bwd(last_pair, last)
            gradients(before_last, 1 - last)
            gradients(last_pair, last)

    head = lambda w: pl.BlockSpec((None, t, w), lambda h: (h, 0, 0))
    stats = pl.BlockSpec((None, nq, 1, blk), lambda h: (h, 0, 0, 0))
    outs = pl.pallas_call(
        body, name=name, grid=(nh,),
        in_specs=[head(QK), head(QK), head(VDIM), pl.BlockSpec((t, VDIM), lambda h: (0, h)), stats, stats] + [ANY] * n,
        out_specs=[head(QK), head(QK), head(VDIM)] + [ANY] * n,
        out_shape=[S((nh, t, QK), F32), S((nh, t, QK), F32), S((nh, t, VDIM), F32)] + [S(p.shape, p.dtype) for p in parts],
        scratch_shapes=[pltpu.VMEM((2, blk, blk), F32), pltpu.VMEM((2, blk, blk), F32), pltpu.VMEM((2, blk, blk), BF16),
                        pltpu.VMEM((2, blk, blk), BF16), pltpu.VMEM((blk, blk), F32)]
        + ([pltpu.SemaphoreType.DMA((n, 3)), pltpu.SemaphoreType.DMA((n, 3))] if n else []),
        compiler_params=_params(1, VMEM_LIMIT_WHOLE_HEAD))(q, k, v, do, lse, delta, *parts)
    return outs[0], outs[1], outs[2], list(outs[3:])


def _shift_down(u, s):
    rows = lax.broadcasted_iota(jnp.int32, u.shape, 0)
    return jnp.where(rows >= s, pltpu.roll(u, s, 0), 0.0)


def _shift_up(u, s):
    n = u.shape[0]
    rows = lax.broadcasted_iota(jnp.int32, u.shape, 0)
    return jnp.where(rows < n - s, pltpu.roll(u, n - s, 0), 0.0)


def _conv_specs(t, d, lanes):
    slab = lambda part: pl.BlockSpec((None, t, lanes), lambda j, part=part: (part, 0, j))
    return slab, pl.BlockSpec((3, lanes), lambda j: (0, j)), pl.BlockSpec((t, lanes), lambda j: (0, j))


def conv_fwd(name, bcx, w):
    _, t, d = bcx.shape
    lanes = _tile(d, 128, 128)
    slab, w_spec, col = _conv_specs(t, d, lanes)

    def body(b_ref, c_ref, x_ref, w_ref, y_ref):
        u = c_ref[...] * x_ref[...]
        uc = w_ref[0:1, :] * _shift_down(u, 2) + w_ref[1:2, :] * _shift_down(u, 1) + w_ref[2:3, :] * u
        y_ref[...] = (b_ref[...] * uc).astype(BF16)

    return pl.pallas_call(
        body, name=name, grid=(d // lanes,), in_specs=[slab(0), slab(1), slab(2), w_spec], out_specs=col,
        out_shape=S((t, d), BF16), compiler_params=_params(1))(bcx, bcx, bcx, w)


def conv_bwd(name, bcx, w, dy):
    _, t, d = bcx.shape
    lanes = _tile(d, 128, 128)
    slab, w_spec, col = _conv_specs(t, d, lanes)

    def body(b_ref, c_ref, x_ref, w_ref, dy_ref, d_ref, dw_ref):
        c = c_ref[...]
        x = x_ref[...]
        dyv = dy_ref[...]
        u = c * x
        u1 = _shift_down(u, 1)
        u2 = _shift_down(u, 2)
        w0, w1, w2 = w_ref[0:1, :], w_ref[1:2, :], w_ref[2:3, :]
        d_ref[0] = (dyv * (w0 * u2 + w1 * u1 + w2 * u)).astype(BF16)
        duc = dyv * b_ref[...]
        dw_ref[0:1, :] = jnp.sum(duc * u2, axis=0, keepdims=True)
        dw_ref[1:2, :] = jnp.sum(duc * u1, axis=0, keepdims=True)
        dw_ref[2:3, :] = jnp.sum(duc * u, axis=0, keepdims=True)
        du = w2 * duc + w1 * _shift_up(duc, 1) + w0 * _shift_up(duc, 2)
        d_ref[1] = (du * x).astype(BF16)
        d_ref[2] = (du * c).astype(BF16)

    return pl.pallas_call(
        body, name=name, grid=(d // lanes,), in_specs=[slab(0), slab(1), slab(2), w_spec, col],
        out_specs=[pl.BlockSpec((3, t, lanes), lambda j: (0, 0, j)), w_spec], out_shape=[S((3, t, d), BF16), S((3, d), F32)],
        compiler_params=_params(1))(bcx, bcx, bcx, w, dy)


def adamw(name, w, g, m, v):
    r, c = w.shape
    tr = _tile(r, 512)

    def body(w_ref, g_ref, m_ref, v_ref, d_ref, mo_ref, vo_ref):
        gv = g_ref[...]
        m_new = ADAM_B1 * m_ref[...] + (1.0 - ADAM_B1) * gv
        v_new = ADAM_B2 * v_ref[...] + (1.0 - ADAM_B2) * (gv * gv)
        m_hat = m_new / (1.0 - ADAM_B1 ** ADAM_STEP)
        v_hat = v_new / (1.0 - ADAM_B2 ** ADAM_STEP)
        d_ref[...] = -ADAM_LR * (m_hat / (jnp.sqrt(v_hat) + ADAM_EPS) + ADAM_WD * w_ref[...])
        mo_ref[...] = m_new
        vo_ref[...] = v_new

    blk = pl.BlockSpec((tr, c), lambda i: (i, 0))
    return pl.pallas_call(
        body, name=name, grid=(r // tr,), in_specs=[blk] * 4, out_specs=[blk] * 3, out_shape=[S((r, c), F32)] * 3,
        compiler_params=_params(1))(w, g, m, v)


def _place():
    x, y, c = lax.axis_index("x"), lax.axis_index("y"), lax.axis_index("c")
    other_chips = [(1 - x, y), (x, 1 - y), (1 - x, 1 - y)]
    return x, y, c, other_chips


def _half(c, rows):
    return pl.ds(pl.multiple_of(c * (rows // 2), 16), rows // 2)


def gather_weight_shards(shards):
    n = len(shards)

    def body(*refs):
        src = refs[:n]
        dst = refs[n:2 * n]
        send_sems, recv_sems = refs[2 * n:]
        x, y, c, chips = _place()
        me = 2 * x + y
        sibling = (x, y, 1 - c)

        def copy(i, slot, half_of, sem, to, from_input=False):
            rows = _half(half_of, src[i].shape[0])
            return pltpu.make_async_remote_copy(
                src_ref=src[i].at[rows] if from_input else dst[i].at[slot, rows], dst_ref=dst[i].at[slot, rows],
                send_sem=send_sems.at[i, sem], recv_sem=recv_sems.at[i, sem], device_id=to, device_id_type=MESH)

        sent = []
        for i in range(n):
            for j, chip in enumerate(chips):
                sent.append(copy(i, me, c, j, (*chip, c), from_input=True))
                sent[-1].start()
        for i in range(n):
            for j, (px, py) in enumerate(chips):
                copy(i, 2 * px + py, c, j, sibling).wait_recv()
                sent.append(copy(i, 2 * px + py, c, 3 + j, sibling))
                sent[-1].start()
        for i in range(n):
            for j, (px, py) in enumerate(chips):
                copy(i, 2 * px + py, 1 - c, 3 + j, sibling).wait_recv()
        for cp in sent:
            cp.wait_send()

    outs = pl.pallas_call(
        body, name="gather_weight_shards", in_specs=[ANY] * n, out_specs=[ANY] * n,
        out_shape=[S((N_CHIPS,) + s.shape, s.dtype) for s in shards],
        scratch_shapes=[pltpu.SemaphoreType.DMA((n, 6)), pltpu.SemaphoreType.DMA((n, 6))],
    )(*shards)
    return _fill_own_slot(outs, [s[None] for s in shards])


def gather_ici_copies(src, dst, send_sems, recv_sems):
    x, y, c, chips = _place()
    me = 2 * x + y
    pairs = []
    for i in range(len(src)):
        rows = _half(c, src[i].shape[0])
        for j, (px, py) in enumerate(chips):
            def copy(slot):
                return pltpu.make_async_remote_copy(
                    src_ref=src[i].at[rows], dst_ref=dst[i].at[slot, rows], send_sem=send_sems.at[i, j],
                    recv_sem=recv_sems.at[i, j], device_id=(px, py, c), device_id_type=MESH)
            pairs.append((copy(me), copy(2 * px + py)))
    return pairs


def scatter_ici_copies(src, dst, send_sems, recv_sems):
    x, y, c, chips = _place()
    me = 2 * x + y
    pairs = []
    for i in range(len(src)):
        for j, (px, py) in enumerate(chips):
            def copy(from_slot, to_slot):
                return pltpu.make_async_remote_copy(
                    src_ref=src[i].at[from_slot], dst_ref=dst[i].at[to_slot], send_sem=send_sems.at[i, j],
                    recv_sem=recv_sems.at[i, j], device_id=(px, py, c), device_id_type=MESH)
            pairs.append((copy(2 * px + py, me), copy(me, 2 * px + py)))
    return pairs


def _ride_along(pairs, grid_ids, grid_sizes):
    first = grid_ids[0] == 0
    last = grid_ids[0] == grid_sizes[0] - 1
    for g, size in zip(grid_ids[1:], grid_sizes[1:]):
        first = first & (g == 0)
        last = last & (g == size - 1)

    @pl.when(first)
    def _():
        for outgoing, _ in pairs:
            outgoing.start()

    @pl.when(last)
    def _():
        for _, incoming in pairs:
            incoming.wait_recv()
        for outgoing, _ in pairs:
            outgoing.wait_send()


def _fill_own_slot(gathered, own):
    me = 2 * lax.axis_index("x") + lax.axis_index("y")
    return [lax.dynamic_update_slice(g, o, (me,) + (0,) * (g.ndim - 1)) for g, o in zip(gathered, own)]


def forward_to_sibling(gathered):
    n = len(gathered)

    def body(*refs):
        src = refs[:n]
        dst = refs[n:2 * n]
        send_sems, recv_sems = refs[2 * n:]
        x, y, c, chips = _place()
        pairs = []
        for i in range(n):
            for j, (px, py) in enumerate(chips):
                def copy(half_of):
                    rows = _half(half_of, src[i].shape[1])
                    return pltpu.make_async_remote_copy(
                        src_ref=src[i].at[2 * px + py, rows], dst_ref=dst[i].at[2 * px + py, rows], send_sem=send_sems.at[i, j],
                        recv_sem=recv_sems.at[i, j], device_id=(x, y, 1 - c), device_id_type=MESH)
                pairs.append((copy(c), copy(1 - c)))
        for outgoing, _ in pairs:
            outgoing.start()
        for _, incoming in pairs:
            incoming.wait_recv()
        for outgoing, _ in pairs:
            outgoing.wait_send()

    return pl.pallas_call(
        body, name="forward_to_sibling", in_specs=[ANY] * n, out_specs=[ANY] * n,
        out_shape=[S(g.shape, g.dtype) for g in gathered], input_output_aliases={i: i for i in range(n)},
        scratch_shapes=[pltpu.SemaphoreType.DMA((n, 3)), pltpu.SemaphoreType.DMA((n, 3))],
    )(*gathered)


def sibling_swap_halves(name, grads):
    n = len(grads)

    def body(*refs):
        src = refs[:n]
        dst = refs[n:2 * n]
        send_sems, recv_sems = refs[2 * n:]
        x, y, c, _ = _place()
        copies = [pltpu.make_async_remote_copy(
            src_ref=src[i].at[:, _half(1 - c, src[i].shape[1]), :], dst_ref=dst[i], send_sem=send_sems.at[i],
            recv_sem=recv_sems.at[i], device_id=(x, y, 1 - c), device_id_type=MESH) for i in range(n)]
        for cp in copies:
            cp.start()
        for cp in copies:
            cp.wait()

    return pl.pallas_call(
        body, name=name, in_specs=[ANY] * n, out_specs=[ANY] * n,
        out_shape=[S((g.shape[0], g.shape[1] // 2, g.shape[2]), g.dtype) for g in grads],
        scratch_shapes=[pltpu.SemaphoreType.DMA((n,)), pltpu.SemaphoreType.DMA((n,))],
    )(*grads)


def add_halves(name, g, rx):
    _, r, cdim = g.shape
    r2 = r // 2
    tr = _tile(r2, 512, 16)
    nb = r2 // tr

    def body(lo_ref, hi_ref, rx_ref, o_ref):
        mine = jnp.where(lax.axis_index("c") == 0, lo_ref[...], hi_ref[...])
        o_ref[...] = (mine.astype(F32) + rx_ref[...].astype(F32)).astype(BF16)

    half = pl.BlockSpec((None, tr, cdim), lambda k, i: (k, i, 0))
    return pl.pallas_call(
        body, name=name, grid=(N_CHIPS, nb),
        in_specs=[half, pl.BlockSpec((None, tr, cdim), lambda k, i: (k, nb + i, 0)), half],
        out_specs=half, out_shape=S((N_CHIPS, r2, cdim), BF16), compiler_params=_params(2))(g, g, rx)


def scatter_to_owner_chips(parts):
    n = len(parts)

    def body(*refs):
        src = refs[:n]
        dst = refs[n:2 * n]
        send_sems, recv_sems = refs[2 * n:]
        pairs = scatter_ici_copies(src, dst, send_sems, recv_sems)
        for outgoing, _ in pairs:
            outgoing.start()
        for _, incoming in pairs:
            incoming.wait_recv()
        for outgoing, _ in pairs:
            outgoing.wait_send()

    return pl.pallas_call(
        body, name="scatter_to_owner_chips", in_specs=[ANY] * n, out_specs=[ANY] * n,
        out_shape=[S(p.shape, p.dtype) for p in parts],
        scratch_shapes=[pltpu.SemaphoreType.DMA((n, 3)), pltpu.SemaphoreType.DMA((n, 3))],
    )(*parts)


def _own_slots(parts):
    me = 2 * lax.axis_index("x") + lax.axis_index("y")
    return [lax.dynamic_slice(p, (me, 0, 0), (1,) + p.shape[1:]) for p in parts]


def sum_chips(name, parts):
    _, r2, cdim = parts.shape
    tr = _tile(r2, 512, 16)

    def body(p_ref, o_ref):
        acc = p_ref[0].astype(F32)
        for k in range(1, N_CHIPS):
            acc = acc + p_ref[k].astype(F32)
        o_ref[...] = acc

    return pl.pallas_call(
        body, name=name, grid=(r2 // tr,), in_specs=[pl.BlockSpec((N_CHIPS, tr, cdim), lambda i: (0, i, 0))],
        out_specs=pl.BlockSpec((tr, cdim), lambda i: (i, 0)), out_shape=S((r2, cdim), F32), compiler_params=_params(1))(parts)


def join_copies(src, dst, where, send_sems, recv_sems):
    x, y, c, _ = _place()
    pairs = []
    for i in range(len(src)):
        def copy(half_of):
            r2 = src[i].shape[0]
            rows = pl.ds(pl.multiple_of(where[i][1] + half_of * r2, 8), r2)
            return pltpu.make_async_remote_copy(
                src_ref=src[i], dst_ref=dst[where[i][0]].at[rows], send_sem=send_sems.at[i],
                recv_sem=recv_sems.at[i], device_id=(x, y, 1 - c), device_id_type=MESH)
        pairs.append((copy(c), copy(1 - c)))
    return pairs


def _fill_own_halves(targets, halves, where):
    targets = list(targets)
    c = lax.axis_index("c")
    for h, (tgt, first) in zip(halves, where):
        targets[tgt] = lax.dynamic_update_slice(targets[tgt], h, (first + c * h.shape[0], 0))
    return targets


def sibling_join_halves(name, halves, targets, where):
    n = len(halves)

    def body(*refs):
        pairs = join_copies(refs[:n], refs[n:n + len(targets)], where, *refs[n + len(targets):])
        for outgoing, _ in pairs:
            outgoing.start()
        for _, incoming in pairs:
            incoming.wait_recv()
        for outgoing, _ in pairs:
            outgoing.wait_send()

    outs = pl.pallas_call(
        body, name=name, in_specs=[ANY] * n, out_specs=[ANY] * len(targets), out_shape=[S(tg, F32) for tg in targets],
        scratch_shapes=[pltpu.SemaphoreType.DMA((n,)), pltpu.SemaphoreType.DMA((n,))],
    )(*halves)
    return _fill_own_halves(outs, halves, where)


def all_reduce_small(name, packed):
    rows, width = packed.shape

    def body(x_ref, o_ref, gathered, send_sems, recv_sems):
        x, y, c, _ = _place()
        me = 4 * x + 2 * y + c
        gathered[me] = x_ref[...]
        flips = [(fx, fy, fc) for fx in (0, 1) for fy in (0, 1) for fc in (0, 1)][1:]

        def copy(r, slot, to):
            return pltpu.make_async_remote_copy(
                src_ref=x_ref, dst_ref=gathered.at[slot], send_sem=send_sems.at[r], recv_sem=recv_sems.at[r],
                device_id=to, device_id_type=MESH)

        def peer(f):
            return (x ^ f[0], y ^ f[1], c ^ f[2])

        sent = [copy(r, me, peer(f)) for r, f in enumerate(flips)]
        for cp in sent:
            cp.start()
        for r, f in enumerate(flips):
            px, py, pc = peer(f)
            copy(r, 4 * px + 2 * py + pc, peer(f)).wait_recv()
        for cp in sent:
            cp.wait_send()
        acc = gathered[0]
        for k in range(1, N_DEV):
            acc = acc + gathered[k]
        o_ref[...] = acc

    vmem = pl.BlockSpec(memory_space=pltpu.VMEM)
    return pl.pallas_call(
        body, name=name, in_specs=[vmem], out_specs=vmem, out_shape=S((rows, width), F32),
        scratch_shapes=[pltpu.VMEM((N_DEV, rows, width), F32), pltpu.SemaphoreType.DMA((N_DEV - 1,)),
                        pltpu.SemaphoreType.DMA((N_DEV - 1,))],
    )(packed)


def _rope_tables(positions):
    inv_freq = 1.0 / (ROPE_THETA ** (jnp.arange(0, ROPE, 2, dtype=F32) / ROPE))
    ang = positions.astype(F32)[:, None] * inv_freq
    return jnp.cos(ang), jnp.sin(ang)


def _unstack_cols(w):
    k4, k, n4 = w.shape
    return jnp.transpose(w, (1, 0, 2)).reshape(k, k4 * n4)


def _stack_cols(w):
    k, n = w.shape
    return jnp.transpose(w.reshape(k, N_CHIPS, n // N_CHIPS), (1, 0, 2))


def kernel(x, positions, mla_norm, mla_w_in, mla_g_cq, mla_g_ckv, mla_w_uq, mla_w_ukv, mla_w_o, conv_norm, conv_w_in, conv_w, conv_w_out, ffn_norm, ffn_w_gate, ffn_w_up, ffn_w_down, final_norm, loss_target, m_mla_norm, m_mla_w_in, m_mla_g_cq, m_mla_g_ckv, m_mla_w_uq, m_mla_w_ukv, m_mla_w_o, m_conv_norm, m_conv_w_in, m_conv_w, m_conv_w_out, m_ffn_norm, m_ffn_w_gate, m_ffn_w_up, m_ffn_w_down, m_final_norm, v_mla_norm, v_mla_w_in, v_mla_g_cq, v_mla_g_ckv, v_mla_w_uq, v_mla_w_ukv, v_mla_w_o, v_conv_norm, v_conv_w_in, v_conv_w, v_conv_w_out, v_ffn_norm, v_ffn_w_gate, v_ffn_w_up, v_ffn_w_down, v_final_norm):
    weights = dict(mla_norm=mla_norm, mla_w_in=mla_w_in, mla_g_cq=mla_g_cq, mla_g_ckv=mla_g_ckv, mla_w_uq=mla_w_uq,
                   mla_w_ukv=mla_w_ukv, mla_w_o=mla_w_o, conv_norm=conv_norm, conv_w_in=conv_w_in, conv_w=conv_w,
                   conv_w_out=conv_w_out, ffn_norm=ffn_norm, ffn_w_gate=ffn_w_gate, ffn_w_up=ffn_w_up,
                   ffn_w_down=ffn_w_down, final_norm=final_norm)
    m_in = dict(mla_norm=m_mla_norm, mla_w_in=m_mla_w_in, mla_g_cq=m_mla_g_cq, mla_g_ckv=m_mla_g_ckv, mla_w_uq=m_mla_w_uq,
                mla_w_ukv=m_mla_w_ukv, mla_w_o=m_mla_w_o, conv_norm=m_conv_norm, conv_w_in=m_conv_w_in, conv_w=m_conv_w,
                conv_w_out=m_conv_w_out, ffn_norm=m_ffn_norm, ffn_w_gate=m_ffn_w_gate, ffn_w_up=m_ffn_w_up,
                ffn_w_down=m_ffn_w_down, final_norm=m_final_norm)
    v_in = dict(mla_norm=v_mla_norm, mla_w_in=v_mla_w_in, mla_g_cq=v_mla_g_cq, mla_g_ckv=v_mla_g_ckv, mla_w_uq=v_mla_w_uq,
                mla_w_ukv=v_mla_w_ukv, mla_w_o=v_mla_w_o, conv_norm=v_conv_norm, conv_w_in=v_conv_w_in, conv_w=v_conv_w,
                conv_w_out=v_conv_w_out, ffn_norm=v_ffn_norm, ffn_w_gate=v_ffn_w_gate, ffn_w_up=v_ffn_w_up,
                ffn_w_down=v_ffn_w_down, final_norm=v_final_norm)
    big = ["mla_w_in", "mla_w_uq", "mla_w_ukv", "mla_w_o", "conv_w_in", "conv_w_out", "ffn_w_gate", "ffn_w_up", "ffn_w_down"]
    order = list(weights)

    t, d = x.shape[1], x.shape[2]
    h0 = x.reshape(t, d)
    target = loss_target.reshape(t, d)
    cos, sin = _rope_tables(positions.reshape(t))

    def rows2d(a):
        return a.reshape(-1, a.shape[-1])

    first, later = big[:4], big[4:]
    shards = {n: rows2d(weights[n]).astype(BF16) for n in big}
    gathered = dict(zip(first, gather_weight_shards([shards[n] for n in first])))
    w_in = gathered["mla_w_in"].reshape(-1, gathered["mla_w_in"].shape[-1])
    w_uq = _unstack_cols(gathered["mla_w_uq"])
    w_ukv = _unstack_cols(gathered["mla_w_ukv"])
    w_o = gathered["mla_w_o"].reshape(-1, d)

    chip = 2 * lax.axis_index("x") + lax.axis_index("y")
    core = lax.axis_index("c")
    d4 = d // N_CHIPS
    first_core = (core == 0).astype(F32)

    def place_shard(shard):
        full = jnp.zeros((shard.shape[0], d), F32)
        return lax.dynamic_update_slice(full, shard * first_core, (0, chip * d4))

    def pack_rows(rows):
        idx = lax.broadcasted_iota(jnp.int32, (SMALL_ROWS, d), 0)
        out = jnp.zeros((SMALL_ROWS, d), F32)
        for r, row in enumerate(rows):
            out = out + jnp.where(idx == r, row, 0.0)
        return out

    cw = place_shard(conv_w.reshape(3, d4))
    pre = all_reduce_small("all_gather_conv_small", pack_rows([place_shard(conv_norm.reshape(1, d4)), cw[0:1], cw[1:2], cw[2:3]]))
    conv_norm_full = pre[0:1]
    conv_w_full = pre[1:4]

    a0 = rms_fwd("mla_norm_fwd", h0, mla_norm)
    proj, cq, ckv, kr = mla_in_proj("mla_in_proj", a0, w_in, mla_g_cq, mla_g_ckv, cos, sin)
    q = linear("mla_q_up", cq, w_uq, F32)
    kv = linear("mla_kv_up", ckv, w_ukv, BF16)
    qh, kh, vh, conv_arriving = qkv_heads("qkv_heads", q, kv, kr, cos, sin, [shards[n] for n in later[:2]])
    attn, lse, ffn_arriving = attention_fwd("attention_fwd", qh, kh, vh, [shards[n] for n in later[2:]])
    gathered.update(zip(later, _fill_own_slot(forward_to_sibling(conv_arriving + ffn_arriving), [shards[n][None] for n in later])))
    cw_in = _unstack_cols(gathered["conv_w_in"])
    cw_out = gathered["conv_w_out"].reshape(-1, d)
    wg_all, wu_all, wd_all = gathered["ffn_w_gate"], gathered["ffn_w_up"], gathered["ffn_w_down"]
    h1, a1 = linear("mla_out_proj", attn, w_o, F32, resid=h0, next_gain=ffn_norm[0:1])

    def ffn_forward(tag, h, a, layer, next_gain):
        g, u, z = ffn_up(f"ffn{tag}_up", a, wg_all, wu_all, layer)
        return g, u, z, ffn_down(f"ffn{tag}_down", z, wd_all, layer, h, next_gain)

    g0, u0, z0, (h2, a2) = ffn_forward(0, h1, a1, 0, conv_norm_full)
    bcx = conv_in_proj("conv_in_proj", a2, cw_in)
    yc = conv_fwd("conv_fwd", bcx, conv_w_full)
    h3, a3 = linear("conv_out_proj", yc, cw_out, F32, resid=h2, next_gain=ffn_norm[1:2])
    g1, u1, z1 = ffn_up("ffn1_up", a3, wg_all, wu_all, 1)
    dh4, d_final_norm, loss_local = ffn_down_loss("ffn1_down_loss", z1, wd_all, 1, h3, final_norm.reshape(1, d), target)

    def ffn_backward(tag, dh, h, layer, a, g, u, z):
        dg, du = ffn_bwd_hidden(f"ffn{tag}_bwd_hidden", dh, wd_all, layer, g, u)
        d_wd = ffn_wgrad_down(f"ffn{tag}_wgrad_down", z, dh)
        dh_prev, d_norm = ffn_bwd_input(f"ffn{tag}_bwd_input", dg, du, wg_all, wu_all, layer, h, ffn_norm[layer:layer + 1], dh)
        d_wg = ffn_wgrad_up(f"ffn{tag}_wgrad_gate", a, dg)
        d_wu = ffn_wgrad_up(f"ffn{tag}_wgrad_up", a, du)
        return dh_prev, d_norm, [d_wg, d_wu, d_wd]

    def reduce_to_pair_sums(tag, local):
        from_sibling = sibling_swap_halves(f"sibling_swap_{tag}", local)
        return [add_halves(f"pair_sum_{tag}{i}", g, r) for i, (g, r) in enumerate(zip(local, from_sibling))]

    def sum_from_chips(tag, pair_sums, arrived):
        from_chips = _fill_own_slot(arrived, _own_slots(pair_sums))
        return [sum_chips(f"chip_sum_{tag}{i}", p) for i, p in enumerate(from_chips)]

    def shard_shape(n):
        return rows2d(weights[n]).shape

    dh3, d_ffn_norm1, ffn1_grads = ffn_backward(1, dh4, h3, 1, a3, g1, u1, z1)

    dyc = linear_nt("conv_out_bwd_input", dh3, cw_out, F32)
    d_cw_out = wgrad("conv_out_wgrad", yc, dh3)
    dbcx, d_conv_w = conv_bwd("conv_bwd", bcx, conv_w_full, dyc)
    dh2, d_conv_norm = conv_in_bwd_input("conv_in_bwd_input", dbcx, cw_in, h2, conv_norm_full, dh3)
    d_cw_in = conv_in_wgrad("conv_in_wgrad", a2, dbcx)

    dh1, d_ffn_norm0, ffn0_grads = ffn_backward(0, dh2, h1, 0, a1, g0, u0, z0)

    d_attn, delta = attention_out_bwd("mla_out_bwd_input", dh1, w_o, attn)
    d_w_o = wgrad("mla_out_wgrad", attn, dh1)
    rest_pairs = reduce_to_pair_sums("rest", [_stack_cols(d_cw_in), d_cw_out.reshape(N_CHIPS, -1, d)] + ffn1_grads + ffn0_grads
                                     + [d_w_o.reshape(N_CHIPS, -1, d)])
    dqh, dkh, dvh, rest_arrived = attention_bwd("attention_bwd", qh, kh, vh, d_attn, lse, delta, rest_pairs)
    rd, rf = ffn0_grads[0].shape[1], ffn0_grads[2].shape[1]
    rest_where = [(0, 0), (1, 0), (2, rd), (3, rd), (4, rf), (2, 0), (3, 0), (4, 0), (5, 0)]
    rest_names = later + ["mla_w_o"]
    dq, dkv, dkr, rest_grads = qkv_heads_bwd("qkv_heads_bwd", dqh, dkh, dvh, cos, sin, sum_from_chips("rest", rest_pairs, rest_arrived),
                                              [shard_shape(n) for n in rest_names], rest_where)
    grads = dict(zip(rest_names, rest_grads))
    dcq = linear_nt("mla_q_up_bwd_input", dq, w_uq, F32)
    d_w_uq = wgrad("mla_q_up_wgrad", cq, dq)
    dckv = linear_nt("mla_kv_up_bwd_input", dkv, w_ukv, F32)
    d_w_ukv = wgrad("mla_kv_up_wgrad", ckv, dkv)
    dproj, d_g_cq, d_g_ckv = mla_mid_bwd("mla_mid_bwd", proj, mla_g_cq, mla_g_ckv, dcq, dckv, dkr, cos, sin)
    d_w_in = wgrad("mla_in_wgrad", a0, dproj)
    grad_x, d_mla_norm = linear_nt_norm_bwd("mla_in_bwd_input", dproj, w_in, h0, mla_norm, dh1)

    mla_pairs = reduce_to_pair_sums("mla", [d_w_in.reshape(N_CHIPS, -1, d_w_in.shape[-1]), _stack_cols(d_w_uq), _stack_cols(d_w_ukv)])
    grads.update(zip(first[:3], sibling_join_halves("sibling_join_mla", sum_from_chips("mla", mla_pairs, scatter_to_owner_chips(mla_pairs)),
                                                    [shard_shape(n) for n in first[:3]], [(i, 0) for i in range(3)])))

    def pad_row(v):
        return jnp.pad(v, ((0, 0), (0, d - v.shape[1])))

    small = all_reduce_small("all_reduce_small_grads", pack_rows([
        d_mla_norm, pad_row(d_g_cq), pad_row(d_g_ckv), d_ffn_norm0, d_ffn_norm1, d_final_norm, d_conv_norm,
        d_conv_w[0:1], d_conv_w[1:2], d_conv_w[2:3], jnp.broadcast_to(loss_local, (1, d))]))
    loss = small[10, 0]
    grads["mla_norm"] = small[0:1]
    grads["mla_g_cq"] = small[1:2, :mla_g_cq.shape[1]]
    grads["mla_g_ckv"] = small[2:3, :mla_g_ckv.shape[1]]
    grads["ffn_norm"] = small[3:5]
    grads["final_norm"] = small[5:6]
    grads["conv_norm"] = lax.dynamic_slice(small[6:7], (0, chip * d4), (1, d4))
    grads["conv_w"] = lax.dynamic_slice(small[7:10], (0, chip * d4), (3, d4))

    outs_g, outs_d, outs_m, outs_v = [], [], [], []
    for n in order:
        w = weights[n]
        delta_w, new_m, new_v = adamw(f"adamw_{n}", rows2d(w), grads[n].reshape(rows2d(w).shape), rows2d(m_in[n]), rows2d(v_in[n]))
        outs_g.append(grads[n].reshape(w.shape))
        outs_d.append(delta_w.reshape(w.shape))
        outs_m.append(new_m.reshape(w.shape))
        outs_v.append(new_v.reshape(w.shape))
    return (loss, grad_x.reshape(x.shape), *outs_g, *outs_d, *outs_m, *outs_v)
```

```python
import math

import jax
import jax.numpy as jnp
from jax import lax
from jax.experimental import pallas as pl
from jax.experimental.pallas import tpu as pltpu

F32 = jnp.float32
BF16 = jnp.bfloat16
S = jax.ShapeDtypeStruct

N_HEADS = 8
NOPE = 128
ROPE = 64
HALF = ROPE // 2
VDIM = 128
QK = NOPE + ROPE
CHUNK = 64
ROPE_THETA = 10000.0
RMS_EPS = 1e-6
ADAM_LR = 0.001
ADAM_B1 = 0.9
ADAM_B2 = 0.999
ADAM_EPS = 1e-08
ADAM_WD = 0.01
ADAM_STEP = 10

N_CHIPS = 4
N_DEV = 8
MASK_VALUE = -1e30
SCORE_SCALE = 1.0 / math.sqrt(QK)
LOG2_E = math.log2(math.e)
SCORE_SCALE_LOG2 = SCORE_SCALE * LOG2_E
VMEM_LIMIT = 48 * 1024 * 1024
VMEM_LIMIT_WHOLE_HEAD = 58 * 1024 * 1024
ATT_BLOCK = 512
SMALL_ROWS = 16

_NN = (((1,), (0,)), ((), ()))
_NT = (((1,), (1,)), ((), ()))
_TN = (((0,), (0,)), ((), ()))
MESH = pl.DeviceIdType.MESH
ANY = pl.BlockSpec(memory_space=pl.ANY)


def _params(n_axes, vmem_limit=VMEM_LIMIT):
    return pltpu.CompilerParams(dimension_semantics=("arbitrary",) * n_axes, vmem_limit_bytes=vmem_limit)


def _tile(n, cap, mult=8):
    for t in range(min(cap, n), 0, -1):
        if n % t == 0 and t % mult == 0:
            return t
    return n


def _sigmoid(x):
    return 0.5 * jnp.tanh(0.5 * x) + 0.5


def _mm(name, a_ops, b_ops, products, dims, grid, k_axis, outs, acc_shape, epilogue, extra_ops=()):
    na, nb, ne, no = len(a_ops), len(b_ops), len(extra_ops), len(outs)
    n_acc = 1 + max(c for _, _, c in products)
    nk = 1 if k_axis is None else grid[k_axis]

    def body(*refs):
        a_refs = refs[:na]
        b_refs = refs[na:na + nb]
        e_refs = refs[na + nb:na + nb + ne]
        o_refs = refs[na + nb + ne:na + nb + ne + no]
        acc_refs = refs[na + nb + ne + no:]

        def partial_sums():
            vals = [None] * n_acc
            for ai, bi, ci in products:
                d = lax.dot_general(a_refs[ai][...].astype(BF16), b_refs[bi][...].astype(BF16), dims,
                                    preferred_element_type=F32)
                vals[ci] = d if vals[ci] is None else vals[ci] + d
            return vals

        if nk == 1:
            epilogue(partial_sums(), e_refs, o_refs)
        else:
            k = pl.program_id(k_axis)

            @pl.when(k == 0)
            def _():
                for acc in acc_refs:
                    acc[...] = jnp.zeros_like(acc)

            for acc, v in zip(acc_refs, partial_sums()):
                acc[...] += v

            @pl.when(k == nk - 1)
            def _():
                epilogue([acc[...] for acc in acc_refs], e_refs, o_refs)

    ops = list(a_ops) + list(b_ops) + list(extra_ops)
    return pl.pallas_call(
        body, name=name, grid=grid,
        in_specs=[s for _, s in ops], out_specs=[s for _, s in outs], out_shape=[o for o, _ in outs],
        scratch_shapes=[pltpu.VMEM(acc_shape, F32) for _ in range(n_acc if nk > 1 else 0)],
        compiler_params=_params(len(grid)),
    )(*[a for a, _ in ops])


def _store(accs, e_refs, o_refs):
    o_refs[0][...] = accs[0].astype(o_refs[0].dtype)


def linear(name, x, w, out_dtype, resid=None, next_gain=None):
    t, k = x.shape
    n = w.shape[1]
    tm = _tile(t, 512)
    tn = n if n <= 2048 else _tile(n, 1024, 128)
    tile = pl.BlockSpec((tm, tn), lambda j, i: (i, j))
    extra = [] if resid is None else [(resid, tile)]
    outs = [(S((t, n), out_dtype), tile)]
    if next_gain is not None:
        assert tn == n
        extra.append((next_gain, pl.BlockSpec((1, n), lambda j, i: (0, 0))))
        outs.append((S((t, n), BF16), tile))

    def epilogue(accs, e_refs, o_refs):
        y = accs[0] if resid is None else e_refs[0][...] + accs[0]
        o_refs[0][...] = y.astype(out_dtype)
        if next_gain is not None:
            o_refs[1][...] = (y * _rstd(y) * e_refs[-1][...]).astype(BF16)

    res = _mm(name, [(x, pl.BlockSpec((tm, k), lambda j, i: (i, 0)))], [(w, pl.BlockSpec((k, tn), lambda j, i: (0, j)))],
              [(0, 0, 0)], _NN, (n // tn, t // tm), None, outs, None, epilogue, extra)
    return res[0] if next_gain is None else res


def linear_nt(name, dy, w, out_dtype):
    t, n = dy.shape
    k = w.shape[0]
    tm = _tile(t, 512)
    tc = n if n <= 2048 else _tile(n, 1024, 128)
    return _mm(name, [(dy, pl.BlockSpec((tm, tc), lambda i, c: (i, c)))], [(w, pl.BlockSpec((k, tc), lambda i, c: (0, c)))],
               [(0, 0, 0)], _NT, (t // tm, n // tc), 1,
               [(S((t, k), out_dtype), pl.BlockSpec((tm, k), lambda i, c: (i, 0)))], (tm, k), _store)[0]


def wgrad(name, x, dy):
    t, k = x.shape
    n = dy.shape[1]
    tk = _tile(t, 512)
    tn = n if n <= 1024 else _tile(n, 1024, 128)
    return _mm(name, [(x, pl.BlockSpec((tk, k), lambda j, s: (s, 0)))], [(dy, pl.BlockSpec((tk, tn), lambda j, s: (s, j)))],
               [(0, 0, 0)], _TN, (n // tn, t // tk), 1,
               [(S((k, n), BF16), pl.BlockSpec((k, tn), lambda j, s: (0, j)))], (k, tn), _store)[0]


def _resident(shape, index_map):
    return pl.BlockSpec(shape, index_map, pipeline_mode=pl.Buffered(1))


def ffn_up(name, a, wg_all, wu_all, layer):
    t, d = a.shape
    f4 = wg_all.shape[2]
    tm = _tile(t, 512)
    w_spec = _resident((N_CHIPS, d, f4), lambda i: (0, layer, 0))
    h_spec = pl.BlockSpec((N_CHIPS, tm, f4), lambda i: (0, i, 0))

    def body(a_ref, wg_ref, wu_ref, zg_ref, zu_ref, z_ref):
        av = a_ref[...]
        for k in range(N_CHIPS):
            g = jnp.dot(av, wg_ref[k], preferred_element_type=F32)
            u = jnp.dot(av, wu_ref[k], preferred_element_type=F32)
            sg = _sigmoid(g)
            silu = g * sg
            zg_ref[k] = (u * (sg * (1.0 + g * (1.0 - sg)))).astype(BF16)
            zu_ref[k] = silu.astype(BF16)
            z_ref[k] = (silu * u).astype(BF16)

    return pl.pallas_call(
        body, name=name, grid=(t // tm,), in_specs=[pl.BlockSpec((tm, d), lambda i: (i, 0)), w_spec, w_spec],
        out_specs=[h_spec] * 3, out_shape=[S((N_CHIPS, t, f4), BF16)] * 3, compiler_params=_params(1))(a, wg_all, wu_all)


def ffn_down(name, z, wd_all, layer, resid, next_gain=None):
    _, t, f4 = z.shape
    d = wd_all.shape[2]
    tm = _tile(t, 512)
    row = pl.BlockSpec((tm, d), lambda i: (i, 0))
    normed = next_gain is not None

    def body(z_ref, wd_ref, r_ref, *refs):
        acc = r_ref[...]
        for k in range(N_CHIPS):
            acc = acc + jnp.dot(z_ref[k], wd_ref[k], preferred_element_type=F32)
        refs[-2 if normed else -1][...] = acc
        if normed:
            refs[-1][...] = (acc * _rstd(acc) * refs[0][...]).astype(BF16)

    res = pl.pallas_call(
        body, name=name, grid=(t // tm,),
        in_specs=[pl.BlockSpec((N_CHIPS, tm, f4), lambda i: (0, i, 0)), _resident((N_CHIPS, f4, d), lambda i: (0, layer, 0)), row]
        + ([pl.BlockSpec((1, d), lambda i: (0, 0))] if normed else []),
        out_specs=[row] * (2 if normed else 1), out_shape=[S((t, d), F32)] + ([S((t, d), BF16)] if normed else []),
        compiler_params=_params(1))(z, wd_all, resid, *([next_gain] if normed else []))
    return res if normed else res[0]


def ffn_bwd_hidden(name, dh, wd_all, layer, zg, zu):
    t, d = dh.shape
    f4 = zg.shape[2]
    tm = _tile(t, 512)
    h_spec = pl.BlockSpec((N_CHIPS, tm, f4), lambda i: (0, i, 0))

    def body(dh_ref, wd_ref, zg_ref, zu_ref, dg_ref, du_ref):
        dhb = dh_ref[...].astype(BF16)
        for k in range(N_CHIPS):
            dz = lax.dot_general(dhb, wd_ref[k], _NT, preferred_element_type=F32)
            dg_ref[k] = (dz * zg_ref[k].astype(F32)).astype(BF16)
            du_ref[k] = (dz * zu_ref[k].astype(F32)).astype(BF16)

    return pl.pallas_call(
        body, name=name, grid=(t // tm,),
        in_specs=[pl.BlockSpec((tm, d), lambda i: (i, 0)), _resident((N_CHIPS, f4, d), lambda i: (0, layer, 0)), h_spec, h_spec],
        out_specs=[h_spec] * 2, out_shape=[S((N_CHIPS, t, f4), BF16)] * 2, compiler_params=_params(1))(dh, wd_all, zg, zu)


def _norm_bwd_specs(tm, d):
    row = pl.BlockSpec((tm, d), lambda i: (i, 0))
    vec = pl.BlockSpec((1, d), lambda i: (0, 0))
    return [row, vec, row], [row, vec]


def _norm_bwd_tail(da, h_ref, g_ref, dhi_ref, dho_ref, dgain_ref):
    dx, dgain = _rms_bwd(h_ref[...], g_ref[...], da)
    dho_ref[...] = dhi_ref[...] + dx

    @pl.when(pl.program_id(0) == 0)
    def _():
        dgain_ref[...] = jnp.zeros_like(dgain_ref)

    dgain_ref[...] += dgain


def ffn_bwd_input(name, dg, du, wg_all, wu_all, layer, h, gain, dh_in):
    _, t, f4 = dg.shape
    d = h.shape[1]
    tm = _tile(t, 512)
    h_spec = pl.BlockSpec((N_CHIPS, tm, f4), lambda i: (0, i, 0))
    w_spec = _resident((N_CHIPS, d, f4), lambda i: (0, layer, 0))
    tail_in, tail_out = _norm_bwd_specs(tm, d)

    def body(dg_ref, du_ref, wg_ref, wu_ref, *tail):
        acc = jnp.zeros((tm, d), F32)
        for k in range(N_CHIPS):
            acc = acc + lax.dot_general(dg_ref[k], wg_ref[k], _NT, preferred_element_type=F32)
            acc = acc + lax.dot_general(du_ref[k], wu_ref[k], _NT, preferred_element_type=F32)
        _norm_bwd_tail(acc, *tail)

    return pl.pallas_call(
        body, name=name, grid=(t // tm,), in_specs=[h_spec, h_spec, w_spec, w_spec] + tail_in, out_specs=tail_out,
        out_shape=[S((t, d), F32), S((1, d), F32)], compiler_params=_params(1))(dg, du, wg_all, wu_all, h, gain, dh_in)


def ffn_wgrad_up(name, a, dy):
    t, d = a.shape
    f4 = dy.shape[2]
    tk = _tile(t, 512)
    nt = t // tk

    def body(a_ref, dy_ref, o_ref, acc):
        s = pl.program_id(0)

        @pl.when(s == 0)
        def _():
            acc[...] = jnp.zeros_like(acc)

        at = a_ref[...].T
        for k in range(N_CHIPS):
            acc[k] += jnp.dot(at, dy_ref[k], preferred_element_type=F32)

        @pl.when(s == nt - 1)
        def _():
            o_ref[...] = acc[...].astype(BF16)

    return pl.pallas_call(
        body, name=name, grid=(nt,),
        in_specs=[pl.BlockSpec((tk, d), lambda s: (s, 0)), pl.BlockSpec((N_CHIPS, tk, f4), lambda s: (0, s, 0))],
        out_specs=pl.BlockSpec((N_CHIPS, d, f4), lambda s: (0, 0, 0)), out_shape=S((N_CHIPS, d, f4), BF16),
        scratch_shapes=[pltpu.VMEM((N_CHIPS, d, f4), F32)], compiler_params=_params(1))(a, dy)


def ffn_wgrad_down(name, z, dh):
    _, t, f4 = z.shape
    d = dh.shape[1]
    tk = _tile(t, 512)
    nt = t // tk

    def body(z_ref, dh_ref, o_ref, acc):
        s = pl.program_id(0)

        @pl.when(s == 0)
        def _():
            acc[...] = jnp.zeros_like(acc)

        dhb = dh_ref[...].astype(BF16)
        for k in range(N_CHIPS):
            acc[k] += lax.dot_general(z_ref[k], dhb, _TN, preferred_element_type=F32)

        @pl.when(s == nt - 1)
        def _():
            o_ref[...] = acc[...].astype(BF16)

    return pl.pallas_call(
        body, name=name, grid=(nt,),
        in_specs=[pl.BlockSpec((N_CHIPS, tk, f4), lambda s: (0, s, 0)), pl.BlockSpec((tk, d), lambda s: (s, 0))],
        out_specs=pl.BlockSpec((N_CHIPS, f4, d), lambda s: (0, 0, 0)), out_shape=S((N_CHIPS, f4, d), BF16),
        scratch_shapes=[pltpu.VMEM((N_CHIPS, f4, d), F32)], compiler_params=_params(1))(z, dh)


def conv_in_proj(name, a, w):
    t, d = a.shape
    tm = _tile(t, 512)
    return _mm(name, [(a, pl.BlockSpec((tm, d), lambda j, i: (i, 0)))], [(w, pl.BlockSpec((d, d), lambda j, i: (0, j)))],
               [(0, 0, 0)], _NN, (3, t // tm), None,
               [(S((3, t, d), F32), pl.BlockSpec((None, tm, d), lambda j, i: (j, i, 0)))], None, _store)[0]


def conv_in_bwd_input(name, dbcx, w, h, gain, dh_in):
    _, t, d = dbcx.shape
    tm = _tile(t, 512)
    tail_in, tail_out = _norm_bwd_specs(tm, d)

    def body(g_ref, w_ref, *tail):
        acc = jnp.zeros((tm, d), F32)
        for j in range(3):
            acc = acc + lax.dot_general(g_ref[j], w_ref[:, j * d:(j + 1) * d], _NT, preferred_element_type=F32)
        _norm_bwd_tail(acc, *tail)

    return pl.pallas_call(
        body, name=name, grid=(t // tm,),
        in_specs=[pl.BlockSpec((3, tm, d), lambda i: (0, i, 0)), _resident((d, 3 * d), lambda i: (0, 0))] + tail_in,
        out_specs=tail_out, out_shape=[S((t, d), F32), S((1, d), F32)], compiler_params=_params(1))(dbcx, w, h, gain, dh_in)


def linear_nt_norm_bwd(name, dy, w, h, gain, dh_in, parts=()):
    t, n = dy.shape
    k = w.shape[0]
    tm = _tile(t, 512)
    tail_in, tail_out = _norm_bwd_specs(tm, k)
    m = len(parts)

    def body(dy_ref, w_ref, h_ref, g_ref, dhi_ref, *refs):
        if m:
            _ride_along(scatter_ici_copies(refs[:m], refs[m + 2:2 * m + 2], *refs[2 * m + 2:]), (pl.program_id(0),), (t // tm,))
        da = lax.dot_general(dy_ref[...].astype(BF16), w_ref[...], _NT, preferred_element_type=F32)
        _norm_bwd_tail(da, h_ref, g_ref, dhi_ref, *refs[m:m + 2])

    outs = pl.pallas_call(
        body, name=name, grid=(t // tm,),
        in_specs=[pl.BlockSpec((tm, n), lambda i: (i, 0)), _resident((k, n), lambda i: (0, 0))] + tail_in + [ANY] * m,
        out_specs=tail_out + [ANY] * m, out_shape=[S((t, k), F32), S((1, k), F32)] + [S(p.shape, p.dtype) for p in parts],
        scratch_shapes=[pltpu.SemaphoreType.DMA((m, 3)), pltpu.SemaphoreType.DMA((m, 3))] if m else [],
        compiler_params=_params(1))(dy, w, h, gain, dh_in, *parts)
    return outs[0], outs[1], list(outs[2:])


def conv_in_wgrad(name, a, dbcx):
    t, d = a.shape
    tk = _tile(t, 512)
    nt = t // tk

    def body(a_ref, g_ref, o_ref, acc):
        s = pl.program_id(0)

        @pl.when(s == 0)
        def _():
            acc[...] = jnp.zeros_like(acc)

        at = a_ref[...].T
        for j in range(3):
            acc[:, j * d:(j + 1) * d] += jnp.dot(at, g_ref[j], preferred_element_type=F32)

        @pl.when(s == nt - 1)
        def _():
            o_ref[...] = acc[...].astype(BF16)

    return pl.pallas_call(
        body, name=name, grid=(nt,),
        in_specs=[pl.BlockSpec((tk, d), lambda s: (s, 0)), pl.BlockSpec((3, tk, d), lambda s: (0, s, 0))],
        out_specs=pl.BlockSpec((d, 3 * d), lambda s: (0, 0)), out_shape=S((d, 3 * d), BF16),
        scratch_shapes=[pltpu.VMEM((d, 3 * d), F32)], compiler_params=_params(1))(a, dbcx)


def _rstd(x):
    return lax.rsqrt(jnp.mean(x * x, axis=-1, keepdims=True) + RMS_EPS)


def _rms_bwd(x, g, dy):
    r = _rstd(x)
    xhat = x * r
    dgain = jnp.sum(dy * xhat, axis=0, keepdims=True)
    dxh = dy * g
    dx = r * (dxh - xhat * jnp.mean(dxh * xhat, axis=-1, keepdims=True))
    return dx, dgain


def rms_fwd(name, h, g):
    t, d = h.shape
    tr = _tile(t, 512)

    def body(h_ref, g_ref, a_ref):
        x = h_ref[...]
        a_ref[...] = (x * _rstd(x) * g_ref[...]).astype(BF16)

    return pl.pallas_call(
        body, name=name, grid=(t // tr,),
        in_specs=[pl.BlockSpec((tr, d), lambda i: (i, 0)), pl.BlockSpec((1, d), lambda i: (0, 0))],
        out_specs=pl.BlockSpec((tr, d), lambda i: (i, 0)), out_shape=S((t, d), BF16), compiler_params=_params(1))(h, g)


def ffn_down_loss(name, z, wd_all, layer, resid, gain, target):
    _, t, f4 = z.shape
    d = wd_all.shape[2]
    tm = _tile(t, 512)

    def body(z_ref, wd_ref, r_ref, g_ref, t_ref, dh_ref, dg_ref, loss_ref):
        x = r_ref[...]
        for k in range(N_CHIPS):
            x = x + jnp.dot(z_ref[k], wd_ref[k], preferred_element_type=F32)
        g = g_ref[...]
        r = _rstd(x)
        xhat = x * r
        err = xhat * g - t_ref[...]
        dy = err * (1.0 / d)
        dxh = dy * g
        dh_ref[...] = r * (dxh - xhat * jnp.mean(dxh * xhat, axis=-1, keepdims=True))

        @pl.when(pl.program_id(0) == 0)
        def _():
            dg_ref[...] = jnp.zeros_like(dg_ref)
            loss_ref[...] = jnp.zeros_like(loss_ref)

        dg_ref[...] += jnp.sum(dy * xhat, axis=0, keepdims=True)
        per_token = jnp.mean(err * err, axis=-1, keepdims=True)
        loss_ref[...] += 0.5 * jnp.sum(per_token, axis=0, keepdims=True)

    row = pl.BlockSpec((tm, d), lambda i: (i, 0))
    vec = pl.BlockSpec((1, d), lambda i: (0, 0))
    one = pl.BlockSpec((1, 1), lambda i: (0, 0))
    return pl.pallas_call(
        body, name=name, grid=(t // tm,),
        in_specs=[pl.BlockSpec((N_CHIPS, tm, f4), lambda i: (0, i, 0)), _resident((N_CHIPS, f4, d), lambda i: (0, layer, 0)), row, vec, row],
        out_specs=[row, vec, one], out_shape=[S((t, d), F32), S((1, d), F32), S((1, 1), F32)],
        compiler_params=_params(1))(z, wd_all, resid, gain, target)


def mla_in_proj(name, a, w, g_cq, g_ckv, cos, sin):
    t, d = a.shape
    n = w.shape[1]
    ql, kl = g_cq.shape[1], g_ckv.shape[1]
    tr = _tile(t, 512)

    def body(a_ref, w_ref, gq_ref, gk_ref, c_ref, s_ref, p_ref, cq_ref, ckv_ref, kr_ref):
        p_ref[...] = jnp.dot(a_ref[...], w_ref[...], preferred_element_type=F32)
        xq = p_ref[:, 0:ql]
        cq_ref[...] = (xq * _rstd(xq) * gq_ref[...]).astype(BF16)
        xk = p_ref[:, ql:ql + kl]
        ckv_ref[...] = (xk * _rstd(xk) * gk_ref[...]).astype(BF16)
        k1 = p_ref[:, ql + kl:ql + kl + HALF]
        k2 = p_ref[:, ql + kl + HALF:ql + kl + ROPE]
        c = c_ref[...]
        s = s_ref[...]
        kr_ref[:, 0:HALF] = k1 * c - k2 * s
        kr_ref[:, HALF:ROPE] = k1 * s + k2 * c

    def row(w):
        return pl.BlockSpec((tr, w), lambda i: (i, 0))

    def vec(w):
        return pl.BlockSpec((1, w), lambda i: (0, 0))

    return pl.pallas_call(
        body, name=name, grid=(t // tr,),
        in_specs=[row(d), _resident((d, n), lambda i: (0, 0)), vec(ql), vec(kl), row(HALF), row(HALF)],
        out_specs=[row(n), row(ql), row(kl), row(ROPE)],
        out_shape=[S((t, n), F32), S((t, ql), BF16), S((t, kl), BF16), S((t, ROPE), F32)],
        compiler_params=_params(1))(a, w, g_cq, g_ckv, cos, sin)


def mla_mid_bwd(name, proj, g_cq, g_ckv, dcq, dckv, dkr, cos, sin):
    t, n = proj.shape
    ql, kl = g_cq.shape[1], g_ckv.shape[1]
    tr = _tile(t, 512)

    def body(p_ref, gq_ref, gk_ref, dcq_ref, dckv_ref, dkr_ref, c_ref, s_ref, dp_ref, dgq_ref, dgk_ref):
        dxq, dgq = _rms_bwd(p_ref[:, 0:ql], gq_ref[...], dcq_ref[...])
        dp_ref[:, 0:ql] = dxq.astype(BF16)
        dxk, dgk = _rms_bwd(p_ref[:, ql:ql + kl], gk_ref[...], dckv_ref[...])
        dp_ref[:, ql:ql + kl] = dxk.astype(BF16)
        d1 = dkr_ref[:, 0:HALF]
        d2 = dkr_ref[:, HALF:ROPE]
        c = c_ref[...]
        s = s_ref[...]
        dp_ref[:, ql + kl:ql + kl + HALF] = (d1 * c + d2 * s).astype(BF16)
        dp_ref[:, ql + kl + HALF:ql + kl + ROPE] = (d2 * c - d1 * s).astype(BF16)

        @pl.when(pl.program_id(0) == 0)
        def _():
            dgq_ref[...] = jnp.zeros_like(dgq_ref)
            dgk_ref[...] = jnp.zeros_like(dgk_ref)

        dgq_ref[...] += dgq
        dgk_ref[...] += dgk

    def row(w):
        return pl.BlockSpec((tr, w), lambda i: (i, 0))

    def vec(w):
        return pl.BlockSpec((1, w), lambda i: (0, 0))

    return pl.pallas_call(
        body, name=name, grid=(t // tr,),
        in_specs=[row(n), vec(ql), vec(kl), row(ql), row(kl), row(ROPE), row(HALF), row(HALF)],
        out_specs=[row(n), vec(ql), vec(kl)], out_shape=[S((t, n), BF16), S((1, ql), F32), S((1, kl), F32)],
        compiler_params=_params(1))(proj, g_cq, g_ckv, dcq, dckv, dkr, cos, sin)


def qkv_heads(name, q, kv, kr, cos, sin, shards=()):
    t = q.shape[0]
    tr = _tile(t, 256)
    n = len(shards)

    def body(q_ref, kv_ref, kr_ref, c_ref, s_ref, *refs):
        src = refs[:n]
        qo_ref, ko_ref, vo_ref = refs[n:n + 3]
        if n:
            _ride_along(gather_ici_copies(src, refs[n + 3:2 * n + 3], *refs[2 * n + 3:]), (pl.program_id(0),), (t // tr,))
        c = c_ref[...]
        s = s_ref[...]
        krb = kr_ref[...].astype(BF16)
        for h in range(N_HEADS):
            q0 = h * QK
            qo_ref[h, :, 0:NOPE] = q_ref[:, q0:q0 + NOPE].astype(BF16)
            q1 = q_ref[:, q0 + NOPE:q0 + NOPE + HALF]
            q2 = q_ref[:, q0 + NOPE + HALF:q0 + QK]
            qo_ref[h, :, NOPE:NOPE + HALF] = (q1 * c - q2 * s).astype(BF16)
            qo_ref[h, :, NOPE + HALF:QK] = (q1 * s + q2 * c).astype(BF16)
            k0 = h * (NOPE + VDIM)
            ko_ref[h, :, 0:NOPE] = kv_ref[:, k0:k0 + NOPE]
            ko_ref[h, :, NOPE:QK] = krb
            vo_ref[h] = kv_ref[:, k0 + NOPE:k0 + NOPE + VDIM]

    def row(w):
        return pl.BlockSpec((tr, w), lambda i: (i, 0))

    def heads(w):
        return pl.BlockSpec((N_HEADS, tr, w), lambda i: (0, i, 0))

    outs = pl.pallas_call(
        body, name=name, grid=(t // tr,),
        in_specs=[row(N_HEADS * QK), row(N_HEADS * (NOPE + VDIM)), row(ROPE), row(HALF), row(HALF)] + [ANY] * n,
        out_specs=[heads(QK), heads(QK), heads(VDIM)] + [ANY] * n,
        out_shape=[S((N_HEADS, t, QK), BF16), S((N_HEADS, t, QK), BF16), S((N_HEADS, t, VDIM), BF16)]
        + [S((N_CHIPS,) + s.shape, s.dtype) for s in shards],
        scratch_shapes=[pltpu.SemaphoreType.DMA((n, 3)), pltpu.SemaphoreType.DMA((n, 3))] if n else [],
        compiler_params=_params(1))(q, kv, kr, cos, sin, *shards)
    return outs[0], outs[1], outs[2], list(outs[3:])


def qkv_heads_bwd(name, dq_h, dk_h, dv_h, cos, sin, halves=(), targets=(), where=()):
    t = dq_h.shape[1]
    tr = _tile(t, 256)
    n, nt = len(halves), len(targets)

    def body(dq_ref, dk_ref, dv_ref, c_ref, s_ref, *refs):
        q_ref, kv_ref, kr_ref = refs[n:n + 3]
        if n:
            _ride_along(join_copies(refs[:n], refs[n + 3:n + 3 + nt], where, *refs[n + 3 + nt:]), (pl.program_id(0),), (t // tr,))
        c = c_ref[...]
        s = s_ref[...]
        dkr = jnp.zeros((tr, ROPE), F32)
        for h in range(N_HEADS):
            q0 = h * QK
            q_ref[:, q0:q0 + NOPE] = dq_ref[h, :, 0:NOPE].astype(BF16)
            d1 = dq_ref[h, :, NOPE:NOPE + HALF]
            d2 = dq_ref[h, :, NOPE + HALF:QK]
            q_ref[:, q0 + NOPE:q0 + NOPE + HALF] = (d1 * c + d2 * s).astype(BF16)
            q_ref[:, q0 + NOPE + HALF:q0 + QK] = (d2 * c - d1 * s).astype(BF16)
            k0 = h * (NOPE + VDIM)
            kv_ref[:, k0:k0 + NOPE] = dk_ref[h, :, 0:NOPE].astype(BF16)
            kv_ref[:, k0 + NOPE:k0 + NOPE + VDIM] = dv_ref[h].astype(BF16)
            dkr = dkr + dk_ref[h, :, NOPE:QK]
        kr_ref[...] = dkr

    def row(w):
        return pl.BlockSpec((tr, w), lambda i: (i, 0))

    def heads(w):
        return pl.BlockSpec((N_HEADS, tr, w), lambda i: (0, i, 0))

    outs = pl.pallas_call(
        body, name=name, grid=(t // tr,),
        in_specs=[heads(QK), heads(QK), heads(VDIM), row(HALF), row(HALF)] + [ANY] * n,
        out_specs=[row(N_HEADS * QK), row(N_HEADS * (NOPE + VDIM)), row(ROPE)] + [ANY] * nt,
        out_shape=[S((t, N_HEADS * QK), BF16), S((t, N_HEADS * (NOPE + VDIM)), BF16), S((t, ROPE), F32)]
        + [S(tg, F32) for tg in targets],
        scratch_shapes=[pltpu.SemaphoreType.DMA((n,)), pltpu.SemaphoreType.DMA((n,))] if n else [],
        compiler_params=_params(1))(dq_h, dk_h, dv_h, cos, sin, *halves)
    return outs[0], outs[1], outs[2], _fill_own_halves(outs[3:], halves, where)


def _chunk_mask_t(q_start, k_start, bq, bk):
    kc = (k_start + lax.broadcasted_iota(jnp.int32, (bk, bq), 0)) // CHUNK
    qc = (q_start + lax.broadcasted_iota(jnp.int32, (bk, bq), 1)) // CHUNK
    return kc <= qc


def attention_fwd(name, q, k, v, shards=()):
    nh, t, _ = q.shape
    blk = ATT_BLOCK
    nq = t // blk
    n = len(shards)

    def body(q_ref, k_ref, v_ref, *refs):
        src = refs[:n]
        o_ref, lse_ref = refs[n:n + 2]
        dst = refs[n + 2:2 * n + 2]
        m_ref, l_ref, acc_ref, s_buf, p_buf, alpha_buf, bias_ref = refs[2 * n + 2:2 * n + 9]
        i = pl.program_id(1)
        if n:
            send_sems, recv_sems = refs[2 * n + 9:]
            _ride_along(gather_ici_copies(src, dst, send_sems, recv_sems), (pl.program_id(0), i), (nh, nq))

        @pl.when((pl.program_id(0) == 0) & (i == 0))
        def _():
            bias_ref[...] = jnp.where(_chunk_mask_t(0, 0, blk, blk), 0.0, MASK_VALUE)

        m_ref[...] = jnp.full_like(m_ref, MASK_VALUE)
        l_ref[...] = jnp.zeros_like(l_ref)
        acc_ref[...] = jnp.zeros_like(acc_ref)

        def rows(b):
            return pl.ds(pl.multiple_of(b * blk, blk), blk)

        def scores(b, slot):
            s_buf[slot] = lax.dot_general(k_ref[rows(b), :], q_ref[...], _NT, preferred_element_type=F32)

        def softmax(slot, diagonal):
            s = s_buf[slot]
            if diagonal:
                s = s + bias_ref[...]
            m_old = m_ref[...]
            m_new = jnp.maximum(m_old, jnp.max(s, axis=0, keepdims=True))
            p = jnp.exp2((s - m_new) * SCORE_SCALE_LOG2)
            alpha = jnp.exp2((m_old - m_new) * SCORE_SCALE_LOG2)
            l_ref[...] = alpha * l_ref[...] + jnp.sum(p, axis=0, keepdims=True)
            m_ref[...] = m_new
            alpha_buf[slot] = alpha
            p_buf[slot] = p.astype(BF16)

        def values(b, slot):
            pv = lax.dot_general(v_ref[rows(b), :], p_buf[slot], _TN, preferred_element_type=F32)
            acc_ref[...] = alpha_buf[slot] * acc_ref[...] + pv

        def step(t, slot):
            values(t - 2, slot)
            softmax(1 - slot, False)
            scores(t, slot)

        scores(0, 0)

        @pl.when(i == 0)
        def _():
            softmax(0, True)
            values(0, 0)

        @pl.when(i > 0)
        def _():
            scores(1, 1)
            softmax(0, False)
            steady = i - 1

            def pair(u, carry):
                step(2 + 2 * u, 0)
                step(3 + 2 * u, 1)
                return carry

            lax.fori_loop(0, steady // 2, pair, 0)

            @pl.when(steady % 2 == 1)
            def _():
                step(i, 0)

            last = i % 2
            softmax(last, True)
            values(i - 1, 1 - last)
            values(i, last)

        l = l_ref[...]
        o_ref[...] = (acc_ref[...] / l).T
        lse_ref[...] = m_ref[...] * SCORE_SCALE + jnp.log(l)

    outs = pl.pallas_call(
        body, name=name, grid=(nh, nq),
        in_specs=[pl.BlockSpec((None, blk, QK), lambda h, i: (h, i, 0)), pl.BlockSpec((None, t, QK), lambda h, i: (h, 0, 0)),
                  pl.BlockSpec((None, t, VDIM), lambda h, i: (h, 0, 0))] + [ANY] * n,
        out_specs=[pl.BlockSpec((blk, VDIM), lambda h, i: (i, h)),
                   pl.BlockSpec((None, None, 1, blk), lambda h, i: (h, i, 0, 0))] + [ANY] * n,
        out_shape=[S((t, nh * VDIM), F32), S((nh, nq, 1, blk), F32)] + [S((N_CHIPS,) + s.shape, s.dtype) for s in shards],
        scratch_shapes=[pltpu.VMEM((1, blk), F32), pltpu.VMEM((1, blk), F32), pltpu.VMEM((VDIM, blk), F32),
                        pltpu.VMEM((2, blk, blk), F32), pltpu.VMEM((2, blk, blk), BF16), pltpu.VMEM((2, 1, blk), F32),
                        pltpu.VMEM((blk, blk), F32)]
        + ([pltpu.SemaphoreType.DMA((n, 3)), pltpu.SemaphoreType.DMA((n, 3))] if n else []),
        compiler_params=_params(2))(q, k, v, *shards)
    return outs[0], outs[1], list(outs[2:])


def attention_out_bwd(name, dh, w_o, o):
    t, d = dh.shape
    n = w_o.shape[0]
    blk = ATT_BLOCK

    def body(dh_ref, w_ref, o_ref, do_ref, d_ref):
        do_ref[...] = lax.dot_general(dh_ref[...].astype(BF16), w_ref[...], _NT, preferred_element_type=F32)
        for h in range(N_HEADS):
            cols = slice(h * VDIM, (h + 1) * VDIM)
            d_ref[h] = jnp.sum((do_ref[:, cols] * o_ref[:, cols]).T, axis=0, keepdims=True)

    tile = pl.BlockSpec((blk, n), lambda i: (i, 0))
    return pl.pallas_call(
        body, name=name, grid=(t // blk,),
        in_specs=[pl.BlockSpec((blk, d), lambda i: (i, 0)), _resident((n, d), lambda i: (0, 0)), tile],
        out_specs=[tile, pl.BlockSpec((N_HEADS, None, 1, blk), lambda i: (0, i, 0, 0))],
        out_shape=[S((t, n), F32), S((N_HEADS, t // blk, 1, blk), F32)], compiler_params=_params(1))(dh, w_o, o)


def attention_bwd(name, q, k, v, do, lse, delta, parts=()):
    nh, t, _ = q.shape
    blk = ATT_BLOCK
    nq = t // blk
    n_pairs = nq * (nq + 1) // 2
    n = len(parts)
    scale = SCORE_SCALE

    def body(q_ref, k_ref, v_ref, do_ref, lse_ref, dl_ref, *refs):
        src = refs[:n]
        dq_ref, dk_ref, dv_ref = refs[n:n + 3]
        dst = refs[n + 3:2 * n + 3]
        s_buf, dp_buf, p_buf, ds_buf, bias_ref = refs[2 * n + 3:2 * n + 8]
        if n:
            send_sems, recv_sems = refs[2 * n + 8:]
            _ride_along(scatter_ici_copies(src, dst, send_sems, recv_sems), (pl.program_id(0),), (nh,))

        @pl.when(pl.program_id(0) == 0)
        def _():
            bias_ref[...] = jnp.where(_chunk_mask_t(0, 0, blk, blk), 0.0, MASK_VALUE)

        dq_ref[...] = jnp.zeros_like(dq_ref)
        dk_ref[...] = jnp.zeros_like(dk_ref)
        dv_ref[...] = jnp.zeros_like(dv_ref)

        def rows(x):
            return pl.ds(pl.multiple_of(x * blk, blk), blk)

        def after(jb):
            j, b = jb
            wrap = b == nq - 1 - j
            return jnp.where(wrap, j + 1, j), jnp.where(wrap, 0, b + 1)

        def products(jb, slot):
            j, b = jb
            s_buf[slot] = lax.dot_general(k_ref[rows(j), :], q_ref[rows(j + b), :], _NT, preferred_element_type=F32)
            dp_buf[slot] = lax.dot_general(v_ref[rows(j), :], do_ref[rows(j + b), :].astype(BF16), _NT, preferred_element_type=F32)

        def softmax_bwd(jb, slot):
            j, b = jb
            s = s_buf[slot] + bias_ref[...] * (b == 0).astype(F32)
            p = jnp.exp2(s * SCORE_SCALE_LOG2 - lse_ref[j + b] * LOG2_E)
            p_buf[slot] = p.astype(BF16)
            ds_buf[slot] = (p * (dp_buf[slot] - dl_ref[j + b]) * scale).astype(BF16)

        def gradients(jb, slot):
            j, b = jb
            dv_ref[rows(j), :] += jnp.dot(p_buf[slot], do_ref[rows(j + b), :].astype(BF16), preferred_element_type=F32)
            dk_ref[rows(j), :] += jnp.dot(ds_buf[slot], q_ref[rows(j + b), :], preferred_element_type=F32)
            dq_ref[rows(j + b), :] += lax.dot_general(ds_buf[slot], k_ref[rows(j), :], _TN, preferred_element_type=F32)

        def step(state, slot):
            third, second, first = state
            gradients(third, slot)
            softmax_bwd(second, 1 - slot)
            products(first, slot)
            return second, first, after(first)

        zero = jnp.int32(0)
        pair0 = (zero, zero)
        products(pair0, 0)
        if n_pairs == 1:
            softmax_bwd(pair0, 0)
            gradients(pair0, 0)
        else:
            pair1 = after(pair0)
            products(pair1, 1)
            softmax_bwd(pair0, 0)
            steady = n_pairs - 2
            state = lax.fori_loop(0, steady // 2, lambda u, st: step(step(st, 0), 1), (pair0, pair1, after(pair1)))
            if steady % 2:
                state = step(state, 0)
            before_last, last_pair, _ = state
            last = (n_pairs - 1) % 2
            softmax_bwd(last_pair, last)
            gradients(before_last, 1 - last)
            gradients(last_pair, last)

    head = lambda w: pl.BlockSpec((None, t, w), lambda h: (h, 0, 0))
    stats = pl.BlockSpec((None, nq, 1, blk), lambda h: (h, 0, 0, 0))
    outs = pl.pallas_call(
        body, name=name, grid=(nh,),
        in_specs=[head(QK), head(QK), head(VDIM), pl.BlockSpec((t, VDIM), lambda h: (0, h)), stats, stats] + [ANY] * n,
        out_specs=[head(QK), head(QK), head(VDIM)] + [ANY] * n,
        out_shape=[S((nh, t, QK), F32), S((nh, t, QK), F32), S((nh, t, VDIM), F32)] + [S(p.shape, p.dtype) for p in parts],
        scratch_shapes=[pltpu.VMEM((2, blk, blk), F32), pltpu.VMEM((2, blk, blk), F32), pltpu.VMEM((2, blk, blk), BF16),
                        pltpu.VMEM((2, blk, blk), BF16), pltpu.VMEM((blk, blk), F32)]
        + ([pltpu.SemaphoreType.DMA((n, 3)), pltpu.SemaphoreType.DMA((n, 3))] if n else []),
        compiler_params=_params(1, VMEM_LIMIT_WHOLE_HEAD))(q, k, v, do, lse, delta, *parts)
    return outs[0], outs[1], outs[2], list(outs[3:])


def _shift_down(u, s):
    rows = lax.broadcasted_iota(jnp.int32, u.shape, 0)
    return jnp.where(rows >= s, pltpu.roll(u, s, 0), 0.0)


def _shift_up(u, s):
    n = u.shape[0]
    rows = lax.broadcasted_iota(jnp.int32, u.shape, 0)
    return jnp.where(rows < n - s, pltpu.roll(u, n - s, 0), 0.0)


def _conv_specs(t, d, lanes):
    slab = lambda part: pl.BlockSpec((None, t, lanes), lambda j, part=part: (part, 0, j))
    return slab, pl.BlockSpec((3, lanes), lambda j: (0, j)), pl.BlockSpec((t, lanes), lambda j: (0, j))


def conv_fwd(name, bcx, w):
    _, t, d = bcx.shape
    lanes = _tile(d, 128, 128)
    slab, w_spec, col = _conv_specs(t, d, lanes)

    def body(b_ref, c_ref, x_ref, w_ref, y_ref):
        u = c_ref[...] * x_ref[...]
        uc = w_ref[0:1, :] * _shift_down(u, 2) + w_ref[1:2, :] * _shift_down(u, 1) + w_ref[2:3, :] * u
        y_ref[...] = (b_ref[...] * uc).astype(BF16)

    return pl.pallas_call(
        body, name=name, grid=(d // lanes,), in_specs=[slab(0), slab(1), slab(2), w_spec], out_specs=col,
        out_shape=S((t, d), BF16), compiler_params=_params(1))(bcx, bcx, bcx, w)


def conv_bwd(name, bcx, w, dy):
    _, t, d = bcx.shape
    lanes = _tile(d, 128, 128)
    slab, w_spec, col = _conv_specs(t, d, lanes)

    def body(b_ref, c_ref, x_ref, w_ref, dy_ref, d_ref, dw_ref):
        c = c_ref[...]
        x = x_ref[...]
        dyv = dy_ref[...]
        u = c * x
        u1 = _shift_down(u, 1)
        u2 = _shift_down(u, 2)
        w0, w1, w2 = w_ref[0:1, :], w_ref[1:2, :], w_ref[2:3, :]
        d_ref[0] = (dyv * (w0 * u2 + w1 * u1 + w2 * u)).astype(BF16)
        duc = dyv * b_ref[...]
        dw_ref[0:1, :] = jnp.sum(duc * u2, axis=0, keepdims=True)
        dw_ref[1:2, :] = jnp.sum(duc * u1, axis=0, keepdims=True)
        dw_ref[2:3, :] = jnp.sum(duc * u, axis=0, keepdims=True)
        du = w2 * duc + w1 * _shift_up(duc, 1) + w0 * _shift_up(duc, 2)
        d_ref[1] = (du * x).astype(BF16)
        d_ref[2] = (du * c).astype(BF16)

    return pl.pallas_call(
        body, name=name, grid=(d // lanes,), in_specs=[slab(0), slab(1), slab(2), w_spec, col],
        out_specs=[pl.BlockSpec((3, t, lanes), lambda j: (0, 0, j)), w_spec], out_shape=[S((3, t, d), BF16), S((3, d), F32)],
        compiler_params=_params(1))(bcx, bcx, bcx, w, dy)


def adamw(name, w, g, m, v):
    r, c = w.shape
    tr = _tile(r, 512)

    def body(w_ref, g_ref, m_ref, v_ref, d_ref, mo_ref, vo_ref):
        gv = g_ref[...]
        m_new = ADAM_B1 * m_ref[...] + (1.0 - ADAM_B1) * gv
        v_new = ADAM_B2 * v_ref[...] + (1.0 - ADAM_B2) * (gv * gv)
        m_hat = m_new / (1.0 - ADAM_B1 ** ADAM_STEP)
        v_hat = v_new / (1.0 - ADAM_B2 ** ADAM_STEP)
        d_ref[...] = -ADAM_LR * (m_hat / (jnp.sqrt(v_hat) + ADAM_EPS) + ADAM_WD * w_ref[...])
        mo_ref[...] = m_new
        vo_ref[...] = v_new

    blk = pl.BlockSpec((tr, c), lambda i: (i, 0))
    return pl.pallas_call(
        body, name=name, grid=(r // tr,), in_specs=[blk] * 4, out_specs=[blk] * 3, out_shape=[S((r, c), F32)] * 3,
        compiler_params=_params(1))(w, g, m, v)


def _place():
    x, y, c = lax.axis_index("x"), lax.axis_index("y"), lax.axis_index("c")
    other_chips = [(1 - x, y), (x, 1 - y), (1 - x, 1 - y)]
    return x, y, c, other_chips


def _half(c, rows):
    return pl.ds(pl.multiple_of(c * (rows // 2), 16), rows // 2)


def gather_weight_shards(shards):
    n = len(shards)

    def body(*refs):
        src = refs[:n]
        dst = refs[n:2 * n]
        send_sems, recv_sems = refs[2 * n:]
        x, y, c, chips = _place()
        me = 2 * x + y
        sibling = (x, y, 1 - c)

        def copy(i, slot, half_of, sem, to, from_input=False):
            rows = _half(half_of, src[i].shape[0])
            return pltpu.make_async_remote_copy(
                src_ref=src[i].at[rows] if from_input else dst[i].at[slot, rows], dst_ref=dst[i].at[slot, rows],
                send_sem=send_sems.at[i, sem], recv_sem=recv_sems.at[i, sem], device_id=to, device_id_type=MESH)

        sent = []
        for i in range(n):
            for j, chip in enumerate(chips):
                sent.append(copy(i, me, c, j, (*chip, c), from_input=True))
                sent[-1].start()
        for i in range(n):
            for j, (px, py) in enumerate(chips):
                copy(i, 2 * px + py, c, j, sibling).wait_recv()
                sent.append(copy(i, 2 * px + py, c, 3 + j, sibling))
                sent[-1].start()
        for i in range(n):
            for j, (px, py) in enumerate(chips):
                copy(i, 2 * px + py, 1 - c, 3 + j, sibling).wait_recv()
        for cp in sent:
            cp.wait_send()

    outs = pl.pallas_call(
        body, name="gather_weight_shards", in_specs=[ANY] * n, out_specs=[ANY] * n,
        out_shape=[S((N_CHIPS,) + s.shape, s.dtype) for s in shards],
        scratch_shapes=[pltpu.SemaphoreType.DMA((n, 6)), pltpu.SemaphoreType.DMA((n, 6))],
    )(*shards)
    return _fill_own_slot(outs, [s[None] for s in shards])


def gather_ici_copies(src, dst, send_sems, recv_sems):
    x, y, c, chips = _place()
    me = 2 * x + y
    pairs = []
    for i in range(len(src)):
        rows = _half(c, src[i].shape[0])
        for j, (px, py) in enumerate(chips):
            def copy(slot):
                return pltpu.make_async_remote_copy(
                    src_ref=src[i].at[rows], dst_ref=dst[i].at[slot, rows], send_sem=send_sems.at[i, j],
                    recv_sem=recv_sems.at[i, j], device_id=(px, py, c), device_id_type=MESH)
            pairs.append((copy(me), copy(2 * px + py)))
    return pairs


def scatter_ici_copies(src, dst, send_sems, recv_sems):
    x, y, c, chips = _place()
    me = 2 * x + y
    pairs = []
    for i in range(len(src)):
        for j, (px, py) in enumerate(chips):
            def copy(from_slot, to_slot):
                return pltpu.make_async_remote_copy(
                    src_ref=src[i].at[from_slot], dst_ref=dst[i].at[to_slot], send_sem=send_sems.at[i, j],
                    recv_sem=recv_sems.at[i, j], device_id=(px, py, c), device_id_type=MESH)
            pairs.append((copy(2 * px + py, me), copy(me, 2 * px + py)))
    return pairs


def _ride_along(pairs, grid_ids, grid_sizes):
    first = grid_ids[0] == 0
    last = grid_ids[0] == grid_sizes[0] - 1
    for g, size in zip(grid_ids[1:], grid_sizes[1:]):
        first = first & (g == 0)
        last = last & (g == size - 1)

    @pl.when(first)
    def _():
        for outgoing, _ in pairs:
            outgoing.start()

    @pl.when(last)
    def _():
        for _, incoming in pairs:
            incoming.wait_recv()
        for outgoing, _ in pairs:
            outgoing.wait_send()


def _fill_own_slot(gathered, own):
    me = 2 * lax.axis_index("x") + lax.axis_index("y")
    return [lax.dynamic_update_slice(g, o, (me,) + (0,) * (g.ndim - 1)) for g, o in zip(gathered, own)]


def forward_copies(src, dst, send_sems, recv_sems):
    x, y, c, chips = _place()
    pairs = []
    for i in range(len(src)):
        for j, (px, py) in enumerate(chips):
            def copy(half_of):
                rows = _half(half_of, src[i].shape[1])
                return pltpu.make_async_remote_copy(
                    src_ref=src[i].at[2 * px + py, rows], dst_ref=dst[i].at[2 * px + py, rows], send_sem=send_sems.at[i, j],
                    recv_sem=recv_sems.at[i, j], device_id=(x, y, 1 - c), device_id_type=MESH)
            pairs.append((copy(c), copy(1 - c)))
    return pairs


def attention_out_proj(name, attn, w_o, resid, next_gain, arriving):
    t, kdim = attn.shape
    n = w_o.shape[1]
    tm = _tile(t, 512)
    m = len(arriving)

    def body(x_ref, w_ref, r_ref, g_ref, *refs):
        h_ref, a_ref = refs[m:m + 2]
        _ride_along(forward_copies(refs[:m], refs[m + 2:2 * m + 2], *refs[2 * m + 2:]), (pl.program_id(0),), (t // tm,))
        y = r_ref[...] + jnp.dot(x_ref[...].astype(BF16), w_ref[...], preferred_element_type=F32)
        h_ref[...] = y
        a_ref[...] = (y * _rstd(y) * g_ref[...]).astype(BF16)

    row = pl.BlockSpec((tm, n), lambda i: (i, 0))
    outs = pl.pallas_call(
        body, name=name, grid=(t // tm,),
        in_specs=[pl.BlockSpec((tm, kdim), lambda i: (i, 0)), _resident((kdim, n), lambda i: (0, 0)), row,
                  pl.BlockSpec((1, n), lambda i: (0, 0))] + [ANY] * m,
        out_specs=[row, row] + [ANY] * m, out_shape=[S((t, n), F32), S((t, n), BF16)] + [S(g.shape, g.dtype) for g in arriving],
        input_output_aliases={4 + i: 2 + i for i in range(m)},
        scratch_shapes=[pltpu.SemaphoreType.DMA((m, 3)), pltpu.SemaphoreType.DMA((m, 3))],
        compiler_params=_params(1))(attn, w_o, resid, next_gain, *arriving)
    return outs[0], outs[1], list(outs[2:])


def sibling_swap_halves(name, grads):
    n = len(grads)

    def body(*refs):
        src = refs[:n]
        dst = refs[n:2 * n]
        send_sems, recv_sems = refs[2 * n:]
        x, y, c, _ = _place()
        copies = [pltpu.make_async_remote_copy(
            src_ref=src[i].at[:, _half(1 - c, src[i].shape[1]), :], dst_ref=dst[i], send_sem=send_sems.at[i],
            recv_sem=recv_sems.at[i], device_id=(x, y, 1 - c), device_id_type=MESH) for i in range(n)]
        for cp in copies:
            cp.start()
        for cp in copies:
            cp.wait()

    return pl.pallas_call(
        body, name=name, in_specs=[ANY] * n, out_specs=[ANY] * n,
        out_shape=[S((g.shape[0], g.shape[1] // 2, g.shape[2]), g.dtype) for g in grads],
        scratch_shapes=[pltpu.SemaphoreType.DMA((n,)), pltpu.SemaphoreType.DMA((n,))],
    )(*grads)


def add_halves(name, g, rx):
    _, r, cdim = g.shape
    r2 = r // 2
    tr = _tile(r2, 512, 16)
    nb = r2 // tr

    def body(lo_ref, hi_ref, rx_ref, o_ref):
        mine = jnp.where(lax.axis_index("c") == 0, lo_ref[...], hi_ref[...])
        o_ref[...] = (mine.astype(F32) + rx_ref[...].astype(F32)).astype(BF16)

    half = pl.BlockSpec((None, tr, cdim), lambda k, i: (k, i, 0))
    return pl.pallas_call(
        body, name=name, grid=(N_CHIPS, nb),
        in_specs=[half, pl.BlockSpec((None, tr, cdim), lambda k, i: (k, nb + i, 0)), half],
        out_specs=half, out_shape=S((N_CHIPS, r2, cdim), BF16), compiler_params=_params(2))(g, g, rx)


def _own_slots(parts):
    me = 2 * lax.axis_index("x") + lax.axis_index("y")
    return [lax.dynamic_slice(p, (me, 0, 0), (1,) + p.shape[1:]) for p in parts]


def sum_chips(name, parts):
    _, r2, cdim = parts.shape
    tr = _tile(r2, 512, 16)

    def body(p_ref, o_ref):
        acc = p_ref[0].astype(F32)
        for k in range(1, N_CHIPS):
            acc = acc + p_ref[k].astype(F32)
        o_ref[...] = acc

    return pl.pallas_call(
        body, name=name, grid=(r2 // tr,), in_specs=[pl.BlockSpec((N_CHIPS, tr, cdim), lambda i: (0, i, 0))],
        out_specs=pl.BlockSpec((tr, cdim), lambda i: (i, 0)), out_shape=S((r2, cdim), F32), compiler_params=_params(1))(parts)


def join_copies(src, dst, where, send_sems, recv_sems):
    x, y, c, _ = _place()
    pairs = []
    for i in range(len(src)):
        def copy(half_of):
            r2 = src[i].shape[0]
            rows = pl.ds(pl.multiple_of(where[i][1] + half_of * r2, 8), r2)
            return pltpu.make_async_remote_copy(
                src_ref=src[i], dst_ref=dst[where[i][0]].at[rows], send_sem=send_sems.at[i],
                recv_sem=recv_sems.at[i], device_id=(x, y, 1 - c), device_id_type=MESH)
        pairs.append((copy(c), copy(1 - c)))
    return pairs


def _fill_own_halves(targets, halves, where):
    targets = list(targets)
    c = lax.axis_index("c")
    for h, (tgt, first) in zip(halves, where):
        targets[tgt] = lax.dynamic_update_slice(targets[tgt], h, (first + c * h.shape[0], 0))
    return targets


def sibling_join_halves(name, halves, targets, where):
    n = len(halves)

    def body(*refs):
        pairs = join_copies(refs[:n], refs[n:n + len(targets)], where, *refs[n + len(targets):])
        for outgoing, _ in pairs:
            outgoing.start()
        for _, incoming in pairs:
            incoming.wait_recv()
        for outgoing, _ in pairs:
            outgoing.wait_send()

    outs = pl.pallas_call(
        body, name=name, in_specs=[ANY] * n, out_specs=[ANY] * len(targets), out_shape=[S(tg, F32) for tg in targets],
        scratch_shapes=[pltpu.SemaphoreType.DMA((n,)), pltpu.SemaphoreType.DMA((n,))],
    )(*halves)
    return _fill_own_halves(outs, halves, where)


def all_reduce_small(name, packed):
    rows, width = packed.shape

    def body(x_ref, o_ref, gathered, send_sems, recv_sems):
        x, y, c, _ = _place()
        me = 4 * x + 2 * y + c
        gathered[me] = x_ref[...]
        flips = [(fx, fy, fc) for fx in (0, 1) for fy in (0, 1) for fc in (0, 1)][1:]

        def copy(r, slot, to):
            return pltpu.make_async_remote_copy(
                src_ref=x_ref, dst_ref=gathered.at[slot], send_sem=send_sems.at[r], recv_sem=recv_sems.at[r],
                device_id=to, device_id_type=MESH)

        def peer(f):
            return (x ^ f[0], y ^ f[1], c ^ f[2])

        sent = [copy(r, me, peer(f)) for r, f in enumerate(flips)]
        for cp in sent:
            cp.start()
        for r, f in enumerate(flips):
            px, py, pc = peer(f)
            copy(r, 4 * px + 2 * py + pc, peer(f)).wait_recv()
        for cp in sent:
            cp.wait_send()
        acc = gathered[0]
        for k in range(1, N_DEV):
            acc = acc + gathered[k]
        o_ref[...] = acc

    vmem = pl.BlockSpec(memory_space=pltpu.VMEM)
    return pl.pallas_call(
        body, name=name, in_specs=[vmem], out_specs=vmem, out_shape=S((rows, width), F32),
        scratch_shapes=[pltpu.VMEM((N_DEV, rows, width), F32), pltpu.SemaphoreType.DMA((N_DEV - 1,)),
                        pltpu.SemaphoreType.DMA((N_DEV - 1,))],
    )(packed)


def _rope_tables(positions):
    inv_freq = 1.0 / (ROPE_THETA ** (jnp.arange(0, ROPE, 2, dtype=F32) / ROPE))
    ang = positions.astype(F32)[:, None] * inv_freq
    return jnp.cos(ang), jnp.sin(ang)


def _unstack_cols(w):
    k4, k, n4 = w.shape
    return jnp.transpose(w, (1, 0, 2)).reshape(k, k4 * n4)


def _stack_cols(w):
    k, n = w.shape
    return jnp.transpose(w.reshape(k, N_CHIPS, n // N_CHIPS), (1, 0, 2))


def kernel(x, positions, mla_norm, mla_w_in, mla_g_cq, mla_g_ckv, mla_w_uq, mla_w_ukv, mla_w_o, conv_norm, conv_w_in, conv_w, conv_w_out, ffn_norm, ffn_w_gate, ffn_w_up, ffn_w_down, final_norm, loss_target, m_mla_norm, m_mla_w_in, m_mla_g_cq, m_mla_g_ckv, m_mla_w_uq, m_mla_w_ukv, m_mla_w_o, m_conv_norm, m_conv_w_in, m_conv_w, m_conv_w_out, m_ffn_norm, m_ffn_w_gate, m_ffn_w_up, m_ffn_w_down, m_final_norm, v_mla_norm, v_mla_w_in, v_mla_g_cq, v_mla_g_ckv, v_mla_w_uq, v_mla_w_ukv, v_mla_w_o, v_conv_norm, v_conv_w_in, v_conv_w, v_conv_w_out, v_ffn_norm, v_ffn_w_gate, v_ffn_w_up, v_ffn_w_down, v_final_norm):
    weights = dict(mla_norm=mla_norm, mla_w_in=mla_w_in, mla_g_cq=mla_g_cq, mla_g_ckv=mla_g_ckv, mla_w_uq=mla_w_uq,
                   mla_w_ukv=mla_w_ukv, mla_w_o=mla_w_o, conv_norm=conv_norm, conv_w_in=conv_w_in, conv_w=conv_w,
                   conv_w_out=conv_w_out, ffn_norm=ffn_norm, ffn_w_gate=ffn_w_gate, ffn_w_up=ffn_w_up,
                   ffn_w_down=ffn_w_down, final_norm=final_norm)
    m_in = dict(mla_norm=m_mla_norm, mla_w_in=m_mla_w_in, mla_g_cq=m_mla_g_cq, mla_g_ckv=m_mla_g_ckv, mla_w_uq=m_mla_w_uq,
                mla_w_ukv=m_mla_w_ukv, mla_w_o=m_mla_w_o, conv_norm=m_conv_norm, conv_w_in=m_conv_w_in, conv_w=m_conv_w,
                conv_w_out=m_conv_w_out, ffn_norm=m_ffn_norm, ffn_w_gate=m_ffn_w_gate, ffn_w_up=m_ffn_w_up,
                ffn_w_down=m_ffn_w_down, final_norm=m_final_norm)
    v_in = dict(mla_norm=v_mla_norm, mla_w_in=v_mla_w_in, mla_g_cq=v_mla_g_cq, mla_g_ckv=v_mla_g_ckv, mla_w_uq=v_mla_w_uq,
                mla_w_ukv=v_mla_w_ukv, mla_w_o=v_mla_w_o, conv_norm=v_conv_norm, conv_w_in=v_conv_w_in, conv_w=v_conv_w,
                conv_w_out=v_conv_w_out, ffn_norm=v_ffn_norm, ffn_w_gate=v_ffn_w_gate, ffn_w_up=v_ffn_w_up,
                ffn_w_down=v_ffn_w_down, final_norm=v_final_norm)
    big = ["mla_w_in", "mla_w_uq", "mla_w_ukv", "mla_w_o", "conv_w_in", "conv_w_out", "ffn_w_gate", "ffn_w_up", "ffn_w_down"]
    order = list(weights)

    t, d = x.shape[1], x.shape[2]
    h0 = x.reshape(t, d)
    target = loss_target.reshape(t, d)
    cos, sin = _rope_tables(positions.reshape(t))

    def rows2d(a):
        return a.reshape(-1, a.shape[-1])

    first, later = big[:4], big[4:]
    shards = {n: rows2d(weights[n]).astype(BF16) for n in big}
    gathered = dict(zip(first, gather_weight_shards([shards[n] for n in first])))
    w_in = gathered["mla_w_in"].reshape(-1, gathered["mla_w_in"].shape[-1])
    w_uq = _unstack_cols(gathered["mla_w_uq"])
    w_ukv = _unstack_cols(gathered["mla_w_ukv"])
    w_o = gathered["mla_w_o"].reshape(-1, d)

    chip = 2 * lax.axis_index("x") + lax.axis_index("y")
    core = lax.axis_index("c")
    d4 = d // N_CHIPS
    first_core = (core == 0).astype(F32)

    def place_shard(shard):
        full = jnp.zeros((shard.shape[0], d), F32)
        return lax.dynamic_update_slice(full, shard * first_core, (0, chip * d4))

    def pack_rows(rows):
        idx = lax.broadcasted_iota(jnp.int32, (SMALL_ROWS, d), 0)
        out = jnp.zeros((SMALL_ROWS, d), F32)
        for r, row in enumerate(rows):
            out = out + jnp.where(idx == r, row, 0.0)
        return out

    cw = place_shard(conv_w.reshape(3, d4))
    pre = all_reduce_small("all_gather_conv_small", pack_rows([place_shard(conv_norm.reshape(1, d4)), cw[0:1], cw[1:2], cw[2:3]]))
    conv_norm_full = pre[0:1]
    conv_w_full = pre[1:4]

    a0 = rms_fwd("mla_norm_fwd", h0, mla_norm)
    proj, cq, ckv, kr = mla_in_proj("mla_in_proj", a0, w_in, mla_g_cq, mla_g_ckv, cos, sin)
    q = linear("mla_q_up", cq, w_uq, F32)
    kv = linear("mla_kv_up", ckv, w_ukv, BF16)
    qh, kh, vh, conv_arriving = qkv_heads("qkv_heads", q, kv, kr, cos, sin, [shards[n] for n in later[:2]])
    attn, lse, ffn_arriving = attention_fwd("attention_fwd", qh, kh, vh, [shards[n] for n in later[2:]])
    h1, a1, handed = attention_out_proj("mla_out_proj", attn, w_o, h0, ffn_norm[0:1], conv_arriving + ffn_arriving)
    gathered.update(zip(later, _fill_own_slot(handed, [shards[n][None] for n in later])))
    cw_in = _unstack_cols(gathered["conv_w_in"])
    cw_out = gathered["conv_w_out"].reshape(-1, d)
    wg_all, wu_all, wd_all = gathered["ffn_w_gate"], gathered["ffn_w_up"], gathered["ffn_w_down"]

    def ffn_forward(tag, h, a, layer, next_gain):
        g, u, z = ffn_up(f"ffn{tag}_up", a, wg_all, wu_all, layer)
        return g, u, z, ffn_down(f"ffn{tag}_down", z, wd_all, layer, h, next_gain)

    g0, u0, z0, (h2, a2) = ffn_forward(0, h1, a1, 0, conv_norm_full)
    bcx = conv_in_proj("conv_in_proj", a2, cw_in)
    yc = conv_fwd("conv_fwd", bcx, conv_w_full)
    h3, a3 = linear("conv_out_proj", yc, cw_out, F32, resid=h2, next_gain=ffn_norm[1:2])
    g1, u1, z1 = ffn_up("ffn1_up", a3, wg_all, wu_all, 1)
    dh4, d_final_norm, loss_local = ffn_down_loss("ffn1_down_loss", z1, wd_all, 1, h3, final_norm.reshape(1, d), target)

    def ffn_backward(tag, dh, h, layer, a, g, u, z):
        dg, du = ffn_bwd_hidden(f"ffn{tag}_bwd_hidden", dh, wd_all, layer, g, u)
        d_wd = ffn_wgrad_down(f"ffn{tag}_wgrad_down", z, dh)
        dh_prev, d_norm = ffn_bwd_input(f"ffn{tag}_bwd_input", dg, du, wg_all, wu_all, layer, h, ffn_norm[layer:layer + 1], dh)
        d_wg = ffn_wgrad_up(f"ffn{tag}_wgrad_gate", a, dg)
        d_wu = ffn_wgrad_up(f"ffn{tag}_wgrad_up", a, du)
        return dh_prev, d_norm, [d_wg, d_wu, d_wd]

    def reduce_to_pair_sums(tag, local):
        from_sibling = sibling_swap_halves(f"sibling_swap_{tag}", local)
        return [add_halves(f"pair_sum_{tag}{i}", g, r) for i, (g, r) in enumerate(zip(local, from_sibling))]

    def sum_from_chips(tag, pair_sums, arrived):
        from_chips = _fill_own_slot(arrived, _own_slots(pair_sums))
        return [sum_chips(f"chip_sum_{tag}{i}", p) for i, p in enumerate(from_chips)]

    def shard_shape(n):
        return rows2d(weights[n]).shape

    dh3, d_ffn_norm1, ffn1_grads = ffn_backward(1, dh4, h3, 1, a3, g1, u1, z1)

    dyc = linear_nt("conv_out_bwd_input", dh3, cw_out, F32)
    d_cw_out = wgrad("conv_out_wgrad", yc, dh3)
    dbcx, d_conv_w = conv_bwd("conv_bwd", bcx, conv_w_full, dyc)
    dh2, d_conv_norm = conv_in_bwd_input("conv_in_bwd_input", dbcx, cw_in, h2, conv_norm_full, dh3)
    d_cw_in = conv_in_wgrad("conv_in_wgrad", a2, dbcx)

    dh1, d_ffn_norm0, ffn0_grads = ffn_backward(0, dh2, h1, 0, a1, g0, u0, z0)

    d_attn, delta = attention_out_bwd("mla_out_bwd_input", dh1, w_o, attn)
    d_w_o = wgrad("mla_out_wgrad", attn, dh1)
    rest_pairs = reduce_to_pair_sums("rest", [_stack_cols(d_cw_in), d_cw_out.reshape(N_CHIPS, -1, d)] + ffn1_grads + ffn0_grads
                                     + [d_w_o.reshape(N_CHIPS, -1, d)])
    dqh, dkh, dvh, rest_arrived = attention_bwd("attention_bwd", qh, kh, vh, d_attn, lse, delta, rest_pairs)
    rd, rf = ffn0_grads[0].shape[1], ffn0_grads[2].shape[1]
    rest_where = [(0, 0), (1, 0), (2, rd), (3, rd), (4, rf), (2, 0), (3, 0), (4, 0), (5, 0)]
    rest_names = later + ["mla_w_o"]
    dq, dkv, dkr, rest_grads = qkv_heads_bwd("qkv_heads_bwd", dqh, dkh, dvh, cos, sin, sum_from_chips("rest", rest_pairs, rest_arrived),
                                              [shard_shape(n) for n in rest_names], rest_where)
    grads = dict(zip(rest_names, rest_grads))
    dcq = linear_nt("mla_q_up_bwd_input", dq, w_uq, F32)
    d_w_uq = wgrad("mla_q_up_wgrad", cq, dq)
    dckv = linear_nt("mla_kv_up_bwd_input", dkv, w_ukv, F32)
    d_w_ukv = wgrad("mla_kv_up_wgrad", ckv, dkv)
    dproj, d_g_cq, d_g_ckv = mla_mid_bwd("mla_mid_bwd", proj, mla_g_cq, mla_g_ckv, dcq, dckv, dkr, cos, sin)
    d_w_in = wgrad("mla_in_wgrad", a0, dproj)
    mla_pairs = reduce_to_pair_sums("mla", [d_w_in.reshape(N_CHIPS, -1, d_w_in.shape[-1]), _stack_cols(d_w_uq), _stack_cols(d_w_ukv)])
    grad_x, d_mla_norm, mla_arrived = linear_nt_norm_bwd("mla_in_bwd_input", dproj, w_in, h0, mla_norm, dh1, mla_pairs)

    grads.update(zip(first[:3], sibling_join_halves("sibling_join_mla", sum_from_chips("mla", mla_pairs, mla_arrived),
                                                    [shard_shape(n) for n in first[:3]], [(i, 0) for i in range(3)])))

    def pad_row(v):
        return jnp.pad(v, ((0, 0), (0, d - v.shape[1])))

    small = all_reduce_small("all_reduce_small_grads", pack_rows([
        d_mla_norm, pad_row(d_g_cq), pad_row(d_g_ckv), d_ffn_norm0, d_ffn_norm1, d_final_norm, d_conv_norm,
        d_conv_w[0:1], d_conv_w[1:2], d_conv_w[2:3], jnp.broadcast_to(loss_local, (1, d))]))
    loss = small[10, 0]
    grads["mla_norm"] = small[0:1]
    grads["mla_g_cq"] = small[1:2, :mla_g_cq.shape[1]]
    grads["mla_g_ckv"] = small[2:3, :mla_g_ckv.shape[1]]
    grads["ffn_norm"] = small[3:5]
    grads["final_norm"] = small[5:6]
    grads["conv_norm"] = lax.dynamic_slice(small[6:7], (0, chip * d4), (1, d4))
    grads["conv_w"] = lax.dynamic_slice(small[7:10], (0, chip * d4), (3, d4))

    outs_g, outs_d, outs_m, outs_v = [], [], [], []
    for n in order:
        w = weights[n]
        delta_w, new_m, new_v = adamw(f"adamw_{n}", rows2d(w), grads[n].reshape(rows2d(w).shape), rows2d(m_in[n]), rows2d(v_in[n]))
        outs_g.append(grads[n].reshape(w.shape))
        outs_d.append(delta_w.reshape(w.shape))
        outs_m.append(new_m.reshape(w.shape))
        outs_v.append(new_v.reshape(w.shape))
    return (loss, grad_x.reshape(x.shape), *outs_g, *outs_d, *outs_m, *outs_v)
```

```python
import math

import jax
import jax.numpy as jnp
from jax import lax
from jax.experimental import pallas as pl
from jax.experimental.pallas import tpu as pltpu

F32 = jnp.float32
BF16 = jnp.bfloat16
S = jax.ShapeDtypeStruct

N_HEADS = 8
NOPE = 128
ROPE = 64
HALF = ROPE // 2
VDIM = 128
QK = NOPE + ROPE
CHUNK = 64
ROPE_THETA = 10000.0
RMS_EPS = 1e-6
ADAM_LR = 0.001
ADAM_B1 = 0.9
ADAM_B2 = 0.999
ADAM_EPS = 1e-08
ADAM_WD = 0.01
ADAM_STEP = 10

N_CHIPS = 4
N_DEV = 8
MASK_VALUE = -1e30
SCORE_SCALE = 1.0 / math.sqrt(QK)
LOG2_E = math.log2(math.e)
SCORE_SCALE_LOG2 = SCORE_SCALE * LOG2_E
VMEM_LIMIT = 48 * 1024 * 1024
VMEM_LIMIT_WHOLE_HEAD = 58 * 1024 * 1024
ATT_BLOCK = 512
SMALL_ROWS = 16

_NN = (((1,), (0,)), ((), ()))
_NT = (((1,), (1,)), ((), ()))
_TN = (((0,), (0,)), ((), ()))
MESH = pl.DeviceIdType.MESH
ANY = pl.BlockSpec(memory_space=pl.ANY)


def _pallas_call(body, operands_in_hbm=True, **kwargs):
    if not operands_in_hbm:
        return pl.pallas_call(body, **kwargs)
    out_shape = kwargs.pop("out_shape")
    single = not isinstance(out_shape, (list, tuple))
    on_hbm = [pltpu.HBM(s.shape, s.dtype) for s in ([out_shape] if single else out_shape)]
    call = pl.pallas_call(body, out_shape=on_hbm[0] if single else on_hbm, **kwargs)
    return lambda *operands: call(*[pltpu.with_memory_space_constraint(x, pltpu.HBM) for x in operands])


def _params(n_axes, vmem_limit=VMEM_LIMIT):
    return pltpu.CompilerParams(dimension_semantics=("arbitrary",) * n_axes, vmem_limit_bytes=vmem_limit)


def _tile(n, cap, mult=8):
    for t in range(min(cap, n), 0, -1):
        if n % t == 0 and t % mult == 0:
            return t
    return n


def _sigmoid(x):
    return 0.5 * jnp.tanh(0.5 * x) + 0.5


def _mm(name, a_ops, b_ops, products, dims, grid, k_axis, outs, acc_shape, epilogue, extra_ops=()):
    na, nb, ne, no = len(a_ops), len(b_ops), len(extra_ops), len(outs)
    n_acc = 1 + max(c for _, _, c in products)
    nk = 1 if k_axis is None else grid[k_axis]

    def body(*refs):
        a_refs = refs[:na]
        b_refs = refs[na:na + nb]
        e_refs = refs[na + nb:na + nb + ne]
        o_refs = refs[na + nb + ne:na + nb + ne + no]
        acc_refs = refs[na + nb + ne + no:]

        def partial_sums():
            vals = [None] * n_acc
            for ai, bi, ci in products:
                d = lax.dot_general(a_refs[ai][...].astype(BF16), b_refs[bi][...].astype(BF16), dims,
                                    preferred_element_type=F32)
                vals[ci] = d if vals[ci] is None else vals[ci] + d
            return vals

        if nk == 1:
            epilogue(partial_sums(), e_refs, o_refs)
        else:
            k = pl.program_id(k_axis)

            @pl.when(k == 0)
            def _():
                for acc in acc_refs:
                    acc[...] = jnp.zeros_like(acc)

            for acc, v in zip(acc_refs, partial_sums()):
                acc[...] += v

            @pl.when(k == nk - 1)
            def _():
                epilogue([acc[...] for acc in acc_refs], e_refs, o_refs)

    ops = list(a_ops) + list(b_ops) + list(extra_ops)
    return _pallas_call(
        body, name=name, grid=grid,
        in_specs=[s for _, s in ops], out_specs=[s for _, s in outs], out_shape=[o for o, _ in outs],
        scratch_shapes=[pltpu.VMEM(acc_shape, F32) for _ in range(n_acc if nk > 1 else 0)],
        compiler_params=_params(len(grid)),
    )(*[a for a, _ in ops])


def _store(accs, e_refs, o_refs):
    o_refs[0][...] = accs[0].astype(o_refs[0].dtype)


def linear(name, x, w, out_dtype, resid=None, next_gain=None):
    t, k = x.shape
    n = w.shape[1]
    tm = _tile(t, 512)
    tn = n if n <= 2048 else _tile(n, 1024, 128)
    tile = pl.BlockSpec((tm, tn), lambda j, i: (i, j))
    extra = [] if resid is None else [(resid, tile)]
    outs = [(S((t, n), out_dtype), tile)]
    if next_gain is not None:
        assert tn == n
        extra.append((next_gain, pl.BlockSpec((1, n), lambda j, i: (0, 0))))
        outs.append((S((t, n), BF16), tile))

    def epilogue(accs, e_refs, o_refs):
        y = accs[0] if resid is None else e_refs[0][...] + accs[0]
        o_refs[0][...] = y.astype(out_dtype)
        if next_gain is not None:
            o_refs[1][...] = (y * _rstd(y) * e_refs[-1][...]).astype(BF16)

    res = _mm(name, [(x, pl.BlockSpec((tm, k), lambda j, i: (i, 0)))], [(w, pl.BlockSpec((k, tn), lambda j, i: (0, j)))],
              [(0, 0, 0)], _NN, (n // tn, t // tm), None, outs, None, epilogue, extra)
    return res[0] if next_gain is None else res


def linear_nt(name, dy, w, out_dtype):
    t, n = dy.shape
    k = w.shape[0]
    tm = _tile(t, 512)
    tc = n if n <= 2048 else _tile(n, 1024, 128)
    return _mm(name, [(dy, pl.BlockSpec((tm, tc), lambda i, c: (i, c)))], [(w, pl.BlockSpec((k, tc), lambda i, c: (0, c)))],
               [(0, 0, 0)], _NT, (t // tm, n // tc), 1,
               [(S((t, k), out_dtype), pl.BlockSpec((tm, k), lambda i, c: (i, 0)))], (tm, k), _store)[0]


def wgrad(name, x, dy):
    t, k = x.shape
    n = dy.shape[1]
    tk = _tile(t, 512)
    tn = n if n <= 1024 else _tile(n, 1024, 128)
    return _mm(name, [(x, pl.BlockSpec((tk, k), lambda j, s: (s, 0)))], [(dy, pl.BlockSpec((tk, tn), lambda j, s: (s, j)))],
               [(0, 0, 0)], _TN, (n // tn, t // tk), 1,
               [(S((k, n), BF16), pl.BlockSpec((k, tn), lambda j, s: (0, j)))], (k, tn), _store)[0]


def _resident(shape, index_map):
    return pl.BlockSpec(shape, index_map, pipeline_mode=pl.Buffered(1))


def ffn_up(name, a, wg_all, wu_all, layer):
    t, d = a.shape
    f4 = wg_all.shape[2]
    tm = _tile(t, 512)
    w_spec = _resident((N_CHIPS, d, f4), lambda i: (0, layer, 0))
    h_spec = pl.BlockSpec((N_CHIPS, tm, f4), lambda i: (0, i, 0))

    def body(a_ref, wg_ref, wu_ref, zg_ref, zu_ref, z_ref):
        av = a_ref[...]
        for k in range(N_CHIPS):
            g = jnp.dot(av, wg_ref[k], preferred_element_type=F32)
            u = jnp.dot(av, wu_ref[k], preferred_element_type=F32)
            sg = _sigmoid(g)
            silu = g * sg
            zg_ref[k] = (u * (sg * (1.0 + g * (1.0 - sg)))).astype(BF16)
            zu_ref[k] = silu.astype(BF16)
            z_ref[k] = (silu * u).astype(BF16)

    return _pallas_call(
        body, name=name, grid=(t // tm,), in_specs=[pl.BlockSpec((tm, d), lambda i: (i, 0)), w_spec, w_spec],
        out_specs=[h_spec] * 3, out_shape=[S((N_CHIPS, t, f4), BF16)] * 3, compiler_params=_params(1))(a, wg_all, wu_all)


def ffn_down(name, z, wd_all, layer, resid, next_gain=None):
    _, t, f4 = z.shape
    d = wd_all.shape[2]
    tm = _tile(t, 512)
    row = pl.BlockSpec((tm, d), lambda i: (i, 0))
    normed = next_gain is not None

    def body(z_ref, wd_ref, r_ref, *refs):
        acc = r_ref[...]
        for k in range(N_CHIPS):
            acc = acc + jnp.dot(z_ref[k], wd_ref[k], preferred_element_type=F32)
        refs[-2 if normed else -1][...] = acc
        if normed:
            refs[-1][...] = (acc * _rstd(acc) * refs[0][...]).astype(BF16)

    res = _pallas_call(
        body, name=name, grid=(t // tm,),
        in_specs=[pl.BlockSpec((N_CHIPS, tm, f4), lambda i: (0, i, 0)), _resident((N_CHIPS, f4, d), lambda i: (0, layer, 0)), row]
        + ([pl.BlockSpec((1, d), lambda i: (0, 0))] if normed else []),
        out_specs=[row] * (2 if normed else 1), out_shape=[S((t, d), F32)] + ([S((t, d), BF16)] if normed else []),
        compiler_params=_params(1))(z, wd_all, resid, *([next_gain] if normed else []))
    return res if normed else res[0]


def ffn_bwd_hidden(name, dh, wd_all, layer, zg, zu):
    t, d = dh.shape
    f4 = zg.shape[2]
    tm = _tile(t, 512)
    h_spec = pl.BlockSpec((N_CHIPS, tm, f4), lambda i: (0, i, 0))

    def body(dh_ref, wd_ref, zg_ref, zu_ref, dg_ref, du_ref):
        dhb = dh_ref[...].astype(BF16)
        for k in range(N_CHIPS):
            dz = lax.dot_general(dhb, wd_ref[k], _NT, preferred_element_type=F32)
            dg_ref[k] = (dz * zg_ref[k].astype(F32)).astype(BF16)
            du_ref[k] = (dz * zu_ref[k].astype(F32)).astype(BF16)

    return _pallas_call(
        body, name=name, grid=(t // tm,),
        in_specs=[pl.BlockSpec((tm, d), lambda i: (i, 0)), _resident((N_CHIPS, f4, d), lambda i: (0, layer, 0)), h_spec, h_spec],
        out_specs=[h_spec] * 2, out_shape=[S((N_CHIPS, t, f4), BF16)] * 2, compiler_params=_params(1))(dh, wd_all, zg, zu)


def _norm_bwd_specs(tm, d):
    row = pl.BlockSpec((tm, d), lambda i: (i, 0))
    vec = pl.BlockSpec((1, d), lambda i: (0, 0))
    return [row, vec, row], [row, vec]


def _norm_bwd_tail(da, h_ref, g_ref, dhi_ref, dho_ref, dgain_ref):
    dx, dgain = _rms_bwd(h_ref[...], g_ref[...], da)
    dho_ref[...] = dhi_ref[...] + dx

    @pl.when(pl.program_id(0) == 0)
    def _():
        dgain_ref[...] = jnp.zeros_like(dgain_ref)

    dgain_ref[...] += dgain


def ffn_bwd_input(name, dg, du, wg_all, wu_all, layer, h, gain, dh_in):
    _, t, f4 = dg.shape
    d = h.shape[1]
    tm = _tile(t, 512)
    h_spec = pl.BlockSpec((N_CHIPS, tm, f4), lambda i: (0, i, 0))
    w_spec = _resident((N_CHIPS, d, f4), lambda i: (0, layer, 0))
    tail_in, tail_out = _norm_bwd_specs(tm, d)

    def body(dg_ref, du_ref, wg_ref, wu_ref, *tail):
        acc = jnp.zeros((tm, d), F32)
        for k in range(N_CHIPS):
            acc = acc + lax.dot_general(dg_ref[k], wg_ref[k], _NT, preferred_element_type=F32)
            acc = acc + lax.dot_general(du_ref[k], wu_ref[k], _NT, preferred_element_type=F32)
        _norm_bwd_tail(acc, *tail)

    return _pallas_call(
        body, name=name, grid=(t // tm,), in_specs=[h_spec, h_spec, w_spec, w_spec] + tail_in, out_specs=tail_out,
        out_shape=[S((t, d), F32), S((1, d), F32)], compiler_params=_params(1))(dg, du, wg_all, wu_all, h, gain, dh_in)


def ffn_wgrad_up(name, a, dy):
    t, d = a.shape
    f4 = dy.shape[2]
    tk = _tile(t, 512)
    nt = t // tk

    def body(a_ref, dy_ref, o_ref, acc):
        s = pl.program_id(0)

        @pl.when(s == 0)
        def _():
            acc[...] = jnp.zeros_like(acc)

        at = a_ref[...].T
        for k in range(N_CHIPS):
            acc[k] += jnp.dot(at, dy_ref[k], preferred_element_type=F32)

        @pl.when(s == nt - 1)
        def _():
            o_ref[...] = acc[...].astype(BF16)

    return _pallas_call(
        body, name=name, grid=(nt,),
        in_specs=[pl.BlockSpec((tk, d), lambda s: (s, 0)), pl.BlockSpec((N_CHIPS, tk, f4), lambda s: (0, s, 0))],
        out_specs=pl.BlockSpec((N_CHIPS, d, f4), lambda s: (0, 0, 0)), out_shape=S((N_CHIPS, d, f4), BF16),
        scratch_shapes=[pltpu.VMEM((N_CHIPS, d, f4), F32)], compiler_params=_params(1))(a, dy)


def ffn_wgrad_down(name, z, dh):
    _, t, f4 = z.shape
    d = dh.shape[1]
    tk = _tile(t, 512)
    nt = t // tk

    def body(z_ref, dh_ref, o_ref, acc):
        s = pl.program_id(0)

        @pl.when(s == 0)
        def _():
            acc[...] = jnp.zeros_like(acc)

        dhb = dh_ref[...].astype(BF16)
        for k in range(N_CHIPS):
            acc[k] += lax.dot_general(z_ref[k], dhb, _TN, preferred_element_type=F32)

        @pl.when(s == nt - 1)
        def _():
            o_ref[...] = acc[...].astype(BF16)

    return _pallas_call(
        body, name=name, grid=(nt,),
        in_specs=[pl.BlockSpec((N_CHIPS, tk, f4), lambda s: (0, s, 0)), pl.BlockSpec((tk, d), lambda s: (s, 0))],
        out_specs=pl.BlockSpec((N_CHIPS, f4, d), lambda s: (0, 0, 0)), out_shape=S((N_CHIPS, f4, d), BF16),
        scratch_shapes=[pltpu.VMEM((N_CHIPS, f4, d), F32)], compiler_params=_params(1))(z, dh)


def conv_in_proj(name, a, w):
    t, d = a.shape
    tm = _tile(t, 512)
    return _mm(name, [(a, pl.BlockSpec((tm, d), lambda j, i: (i, 0)))], [(w, pl.BlockSpec((d, d), lambda j, i: (0, j)))],
               [(0, 0, 0)], _NN, (3, t // tm), None,
               [(S((3, t, d), F32), pl.BlockSpec((None, tm, d), lambda j, i: (j, i, 0)))], None, _store)[0]


def conv_in_bwd_input(name, dbcx, w, h, gain, dh_in):
    _, t, d = dbcx.shape
    tm = _tile(t, 512)
    tail_in, tail_out = _norm_bwd_specs(tm, d)

    def body(g_ref, w_ref, *tail):
        acc = jnp.zeros((tm, d), F32)
        for j in range(3):
            acc = acc + lax.dot_general(g_ref[j], w_ref[:, j * d:(j + 1) * d], _NT, preferred_element_type=F32)
        _norm_bwd_tail(acc, *tail)

    return _pallas_call(
        body, name=name, grid=(t // tm,),
        in_specs=[pl.BlockSpec((3, tm, d), lambda i: (0, i, 0)), _resident((d, 3 * d), lambda i: (0, 0))] + tail_in,
        out_specs=tail_out, out_shape=[S((t, d), F32), S((1, d), F32)], compiler_params=_params(1))(dbcx, w, h, gain, dh_in)


def linear_nt_norm_bwd(name, dy, w, h, gain, dh_in, parts=()):
    t, n = dy.shape
    k = w.shape[0]
    tm = _tile(t, 512)
    tail_in, tail_out = _norm_bwd_specs(tm, k)
    m = len(parts)

    def body(dy_ref, w_ref, h_ref, g_ref, dhi_ref, *refs):
        if m:
            _ride_along(scatter_ici_copies(refs[:m], refs[m + 2:2 * m + 2], *refs[2 * m + 2:]), (pl.program_id(0),), (t // tm,))
        da = lax.dot_general(dy_ref[...].astype(BF16), w_ref[...], _NT, preferred_element_type=F32)
        _norm_bwd_tail(da, h_ref, g_ref, dhi_ref, *refs[m:m + 2])

    outs = _pallas_call(
        body, name=name, grid=(t // tm,),
        in_specs=[pl.BlockSpec((tm, n), lambda i: (i, 0)), _resident((k, n), lambda i: (0, 0))] + tail_in + [ANY] * m,
        out_specs=tail_out + [ANY] * m, out_shape=[S((t, k), F32), S((1, k), F32)] + [S(p.shape, p.dtype) for p in parts],
        scratch_shapes=[pltpu.SemaphoreType.DMA((m, 3)), pltpu.SemaphoreType.DMA((m, 3))] if m else [],
        compiler_params=_params(1))(dy, w, h, gain, dh_in, *parts)
    return outs[0], outs[1], list(outs[2:])


def conv_in_wgrad(name, a, dbcx):
    t, d = a.shape
    tk = _tile(t, 512)
    nt = t // tk

    def body(a_ref, g_ref, o_ref, acc):
        s = pl.program_id(0)

        @pl.when(s == 0)
        def _():
            acc[...] = jnp.zeros_like(acc)

        at = a_ref[...].T
        for j in range(3):
            acc[:, j * d:(j + 1) * d] += jnp.dot(at, g_ref[j], preferred_element_type=F32)

        @pl.when(s == nt - 1)
        def _():
            o_ref[...] = acc[...].astype(BF16)

    return _pallas_call(
        body, name=name, grid=(nt,),
        in_specs=[pl.BlockSpec((tk, d), lambda s: (s, 0)), pl.BlockSpec((3, tk, d), lambda s: (0, s, 0))],
        out_specs=pl.BlockSpec((d, 3 * d), lambda s: (0, 0)), out_shape=S((d, 3 * d), BF16),
        scratch_shapes=[pltpu.VMEM((d, 3 * d), F32)], compiler_params=_params(1))(a, dbcx)


def _rstd(x):
    return lax.rsqrt(jnp.mean(x * x, axis=-1, keepdims=True) + RMS_EPS)


def _rms_bwd(x, g, dy):
    r = _rstd(x)
    xhat = x * r
    dgain = jnp.sum(dy * xhat, axis=0, keepdims=True)
    dxh = dy * g
    dx = r * (dxh - xhat * jnp.mean(dxh * xhat, axis=-1, keepdims=True))
    return dx, dgain


def rms_fwd(name, h, g):
    t, d = h.shape
    tr = _tile(t, 512)

    def body(h_ref, g_ref, a_ref):
        x = h_ref[...]
        a_ref[...] = (x * _rstd(x) * g_ref[...]).astype(BF16)

    return _pallas_call(
        body, name=name, grid=(t // tr,),
        in_specs=[pl.BlockSpec((tr, d), lambda i: (i, 0)), pl.BlockSpec((1, d), lambda i: (0, 0))],
        out_specs=pl.BlockSpec((tr, d), lambda i: (i, 0)), out_shape=S((t, d), BF16), compiler_params=_params(1))(h, g)


def ffn_down_loss(name, z, wd_all, layer, resid, gain, target):
    _, t, f4 = z.shape
    d = wd_all.shape[2]
    tm = _tile(t, 512)

    def body(z_ref, wd_ref, r_ref, g_ref, t_ref, dh_ref, dg_ref, loss_ref):
        x = r_ref[...]
        for k in range(N_CHIPS):
            x = x + jnp.dot(z_ref[k], wd_ref[k], preferred_element_type=F32)
        g = g_ref[...]
        r = _rstd(x)
        xhat = x * r
        err = xhat * g - t_ref[...]
        dy = err * (1.0 / d)
        dxh = dy * g
        dh_ref[...] = r * (dxh - xhat * jnp.mean(dxh * xhat, axis=-1, keepdims=True))

        @pl.when(pl.program_id(0) == 0)
        def _():
            dg_ref[...] = jnp.zeros_like(dg_ref)
            loss_ref[...] = jnp.zeros_like(loss_ref)

        dg_ref[...] += jnp.sum(dy * xhat, axis=0, keepdims=True)
        per_token = jnp.mean(err * err, axis=-1, keepdims=True)
        loss_ref[...] += 0.5 * jnp.sum(per_token, axis=0, keepdims=True)

    row = pl.BlockSpec((tm, d), lambda i: (i, 0))
    vec = pl.BlockSpec((1, d), lambda i: (0, 0))
    one = pl.BlockSpec((1, 1), lambda i: (0, 0))
    return _pallas_call(
        body, name=name, grid=(t // tm,),
        in_specs=[pl.BlockSpec((N_CHIPS, tm, f4), lambda i: (0, i, 0)), _resident((N_CHIPS, f4, d), lambda i: (0, layer, 0)), row, vec, row],
        out_specs=[row, vec, one], out_shape=[S((t, d), F32), S((1, d), F32), S((1, 1), F32)],
        compiler_params=_params(1))(z, wd_all, resid, gain, target)


def mla_in_proj(name, a, w, g_cq, g_ckv, cos, sin):
    t, d = a.shape
    n = w.shape[1]
    ql, kl = g_cq.shape[1], g_ckv.shape[1]
    tr = _tile(t, 512)

    def body(a_ref, w_ref, gq_ref, gk_ref, c_ref, s_ref, p_ref, cq_ref, ckv_ref, kr_ref):
        p_ref[...] = jnp.dot(a_ref[...], w_ref[...], preferred_element_type=F32)
        xq = p_ref[:, 0:ql]
        cq_ref[...] = (xq * _rstd(xq) * gq_ref[...]).astype(BF16)
        xk = p_ref[:, ql:ql + kl]
        ckv_ref[...] = (xk * _rstd(xk) * gk_ref[...]).astype(BF16)
        k1 = p_ref[:, ql + kl:ql + kl + HALF]
        k2 = p_ref[:, ql + kl + HALF:ql + kl + ROPE]
        c = c_ref[...]
        s = s_ref[...]
        kr_ref[:, 0:HALF] = k1 * c - k2 * s
        kr_ref[:, HALF:ROPE] = k1 * s + k2 * c

    def row(w):
        return pl.BlockSpec((tr, w), lambda i: (i, 0))

    def vec(w):
        return pl.BlockSpec((1, w), lambda i: (0, 0))

    return _pallas_call(
        body, name=name, grid=(t // tr,),
        in_specs=[row(d), _resident((d, n), lambda i: (0, 0)), vec(ql), vec(kl), row(HALF), row(HALF)],
        out_specs=[row(n), row(ql), row(kl), row(ROPE)],
        out_shape=[S((t, n), F32), S((t, ql), BF16), S((t, kl), BF16), S((t, ROPE), F32)],
        compiler_params=_params(1))(a, w, g_cq, g_ckv, cos, sin)


def mla_mid_bwd(name, proj, g_cq, g_ckv, dcq, dckv, dkr, cos, sin):
    t, n = proj.shape
    ql, kl = g_cq.shape[1], g_ckv.shape[1]
    tr = _tile(t, 512)

    def body(p_ref, gq_ref, gk_ref, dcq_ref, dckv_ref, dkr_ref, c_ref, s_ref, dp_ref, dgq_ref, dgk_ref):
        dxq, dgq = _rms_bwd(p_ref[:, 0:ql], gq_ref[...], dcq_ref[...])
        dp_ref[:, 0:ql] = dxq.astype(BF16)
        dxk, dgk = _rms_bwd(p_ref[:, ql:ql + kl], gk_ref[...], dckv_ref[...])
        dp_ref[:, ql:ql + kl] = dxk.astype(BF16)
        d1 = dkr_ref[:, 0:HALF]
        d2 = dkr_ref[:, HALF:ROPE]
        c = c_ref[...]
        s = s_ref[...]
        dp_ref[:, ql + kl:ql + kl + HALF] = (d1 * c + d2 * s).astype(BF16)
        dp_ref[:, ql + kl + HALF:ql + kl + ROPE] = (d2 * c - d1 * s).astype(BF16)

        @pl.when(pl.program_id(0) == 0)
        def _():
            dgq_ref[...] = jnp.zeros_like(dgq_ref)
            dgk_ref[...] = jnp.zeros_like(dgk_ref)

        dgq_ref[...] += dgq
        dgk_ref[...] += dgk

    def row(w):
        return pl.BlockSpec((tr, w), lambda i: (i, 0))

    def vec(w):
        return pl.BlockSpec((1, w), lambda i: (0, 0))

    return _pallas_call(
        body, name=name, grid=(t // tr,),
        in_specs=[row(n), vec(ql), vec(kl), row(ql), row(kl), row(ROPE), row(HALF), row(HALF)],
        out_specs=[row(n), vec(ql), vec(kl)], out_shape=[S((t, n), BF16), S((1, ql), F32), S((1, kl), F32)],
        compiler_params=_params(1))(proj, g_cq, g_ckv, dcq, dckv, dkr, cos, sin)


def qkv_heads(name, q, kv, kr, cos, sin, shards=()):
    t = q.shape[0]
    tr = _tile(t, 256)
    n = len(shards)

    def body(q_ref, kv_ref, kr_ref, c_ref, s_ref, *refs):
        src = refs[:n]
        qo_ref, ko_ref, vo_ref = refs[n:n + 3]
        if n:
            _ride_along(gather_ici_copies(src, refs[n + 3:2 * n + 3], *refs[2 * n + 3:]), (pl.program_id(0),), (t // tr,))
        c = c_ref[...]
        s = s_ref[...]
        krb = kr_ref[...].astype(BF16)
        for h in range(N_HEADS):
            q0 = h * QK
            qo_ref[h, :, 0:NOPE] = q_ref[:, q0:q0 + NOPE].astype(BF16)
            q1 = q_ref[:, q0 + NOPE:q0 + NOPE + HALF]
            q2 = q_ref[:, q0 + NOPE + HALF:q0 + QK]
            qo_ref[h, :, NOPE:NOPE + HALF] = (q1 * c - q2 * s).astype(BF16)
            qo_ref[h, :, NOPE + HALF:QK] = (q1 * s + q2 * c).astype(BF16)
            k0 = h * (NOPE + VDIM)
            ko_ref[h, :, 0:NOPE] = kv_ref[:, k0:k0 + NOPE]
            ko_ref[h, :, NOPE:QK] = krb
            vo_ref[h] = kv_ref[:, k0 + NOPE:k0 + NOPE + VDIM]

    def row(w):
        return pl.BlockSpec((tr, w), lambda i: (i, 0))

    def heads(w):
        return pl.BlockSpec((N_HEADS, tr, w), lambda i: (0, i, 0))

    outs = _pallas_call(
        body, name=name, grid=(t // tr,),
        in_specs=[row(N_HEADS * QK), row(N_HEADS * (NOPE + VDIM)), row(ROPE), row(HALF), row(HALF)] + [ANY] * n,
        out_specs=[heads(QK), heads(QK), heads(VDIM)] + [ANY] * n,
        out_shape=[S((N_HEADS, t, QK), BF16), S((N_HEADS, t, QK), BF16), S((N_HEADS, t, VDIM), BF16)]
        + [S((N_CHIPS,) + s.shape, s.dtype) for s in shards],
        scratch_shapes=[pltpu.SemaphoreType.DMA((n, 3)), pltpu.SemaphoreType.DMA((n, 3))] if n else [],
        compiler_params=_params(1))(q, kv, kr, cos, sin, *shards)
    return outs[0], outs[1], outs[2], list(outs[3:])


def qkv_heads_bwd(name, dq_h, dk_h, dv_h, cos, sin, halves=(), targets=(), where=()):
    t = dq_h.shape[1]
    tr = _tile(t, 256)
    n, nt = len(halves), len(targets)

    def body(dq_ref, dk_ref, dv_ref, c_ref, s_ref, *refs):
        q_ref, kv_ref, kr_ref = refs[n:n + 3]
        if n:
            _ride_along(join_copies(refs[:n], refs[n + 3:n + 3 + nt], where, *refs[n + 3 + nt:]), (pl.program_id(0),), (t // tr,))
        c = c_ref[...]
        s = s_ref[...]
        dkr = jnp.zeros((tr, ROPE), F32)
        for h in range(N_HEADS):
            q0 = h * QK
            q_ref[:, q0:q0 + NOPE] = dq_ref[h, :, 0:NOPE].astype(BF16)
            d1 = dq_ref[h, :, NOPE:NOPE + HALF]
            d2 = dq_ref[h, :, NOPE + HALF:QK]
            q_ref[:, q0 + NOPE:q0 + NOPE + HALF] = (d1 * c + d2 * s).astype(BF16)
            q_ref[:, q0 + NOPE + HALF:q0 + QK] = (d2 * c - d1 * s).astype(BF16)
            k0 = h * (NOPE + VDIM)
            kv_ref[:, k0:k0 + NOPE] = dk_ref[h, :, 0:NOPE].astype(BF16)
            kv_ref[:, k0 + NOPE:k0 + NOPE + VDIM] = dv_ref[h].astype(BF16)
            dkr = dkr + dk_ref[h, :, NOPE:QK]
        kr_ref[...] = dkr

    def row(w):
        return pl.BlockSpec((tr, w), lambda i: (i, 0))

    def heads(w):
        return pl.BlockSpec((N_HEADS, tr, w), lambda i: (0, i, 0))

    outs = _pallas_call(
        body, name=name, grid=(t // tr,),
        in_specs=[heads(QK), heads(QK), heads(VDIM), row(HALF), row(HALF)] + [ANY] * n,
        out_specs=[row(N_HEADS * QK), row(N_HEADS * (NOPE + VDIM)), row(ROPE)] + [ANY] * nt,
        out_shape=[S((t, N_HEADS * QK), BF16), S((t, N_HEADS * (NOPE + VDIM)), BF16), S((t, ROPE), F32)]
        + [S(tg, F32) for tg in targets],
        scratch_shapes=[pltpu.SemaphoreType.DMA((n,)), pltpu.SemaphoreType.DMA((n,))] if n else [],
        compiler_params=_params(1))(dq_h, dk_h, dv_h, cos, sin, *halves)
    return outs[0], outs[1], outs[2], _fill_own_halves(outs[3:], halves, where)


def _chunk_mask_t(q_start, k_start, bq, bk):
    kc = (k_start + lax.broadcasted_iota(jnp.int32, (bk, bq), 0)) // CHUNK
    qc = (q_start + lax.broadcasted_iota(jnp.int32, (bk, bq), 1)) // CHUNK
    return kc <= qc


def attention_fwd(name, q, k, v, shards=()):
    nh, t, _ = q.shape
    blk = ATT_BLOCK
    nq = t // blk
    n = len(shards)

    def body(q_ref, k_ref, v_ref, *refs):
        src = refs[:n]
        o_ref, lse_ref = refs[n:n + 2]
        dst = refs[n + 2:2 * n + 2]
        m_ref, l_ref, acc_ref, s_buf, p_buf, alpha_buf, bias_ref = refs[2 * n + 2:2 * n + 9]
        i = pl.program_id(1)
        if n:
            send_sems, recv_sems = refs[2 * n + 9:]
            _ride_along(gather_ici_copies(src, dst, send_sems, recv_sems), (pl.program_id(0), i), (nh, nq))

        @pl.when((pl.program_id(0) == 0) & (i == 0))
        def _():
            bias_ref[...] = jnp.where(_chunk_mask_t(0, 0, blk, blk), 0.0, MASK_VALUE)

        m_ref[...] = jnp.full_like(m_ref, MASK_VALUE)
        l_ref[...] = jnp.zeros_like(l_ref)
        acc_ref[...] = jnp.zeros_like(acc_ref)

        def rows(b):
            return pl.ds(pl.multiple_of(b * blk, blk), blk)

        def scores(b, slot):
            s_buf[slot] = lax.dot_general(k_ref[rows(b), :], q_ref[...], _NT, preferred_element_type=F32)

        def softmax(slot, diagonal):
            s = s_buf[slot]
            if diagonal:
                s = s + bias_ref[...]
            m_old = m_ref[...]
            m_new = jnp.maximum(m_old, jnp.max(s, axis=0, keepdims=True))
            p = jnp.exp2((s - m_new) * SCORE_SCALE_LOG2)
            alpha = jnp.exp2((m_old - m_new) * SCORE_SCALE_LOG2)
            l_ref[...] = alpha * l_ref[...] + jnp.sum(p, axis=0, keepdims=True)
            m_ref[...] = m_new
            alpha_buf[slot] = alpha
            p_buf[slot] = p.astype(BF16)

        def values(b, slot):
            pv = lax.dot_general(v_ref[rows(b), :], p_buf[slot], _TN, preferred_element_type=F32)
            acc_ref[...] = alpha_buf[slot] * acc_ref[...] + pv

        def step(t, slot):
            values(t - 2, slot)
            softmax(1 - slot, False)
            scores(t, slot)

        scores(0, 0)

        @pl.when(i == 0)
        def _():
            softmax(0, True)
            values(0, 0)

        @pl.when(i > 0)
        def _():
            scores(1, 1)
            softmax(0, False)
            steady = i - 1

            def pair(u, carry):
                step(2 + 2 * u, 0)
                step(3 + 2 * u, 1)
                return carry

            lax.fori_loop(0, steady // 2, pair, 0)

            @pl.when(steady % 2 == 1)
            def _():
                step(i, 0)

            last = i % 2
            softmax(last, True)
            values(i - 1, 1 - last)
            values(i, last)

        l = l_ref[...]
        o_ref[...] = (acc_ref[...] / l).T
        lse_ref[...] = m_ref[...] * SCORE_SCALE + jnp.log(l)

    outs = _pallas_call(
        body, name=name, grid=(nh, nq),
        in_specs=[pl.BlockSpec((None, blk, QK), lambda h, i: (h, i, 0)), pl.BlockSpec((None, t, QK), lambda h, i: (h, 0, 0)),
                  pl.BlockSpec((None, t, VDIM), lambda h, i: (h, 0, 0))] + [ANY] * n,
        out_specs=[pl.BlockSpec((blk, VDIM), lambda h, i: (i, h)),
                   pl.BlockSpec((None, None, 1, blk), lambda h, i: (h, i, 0, 0))] + [ANY] * n,
        out_shape=[S((t, nh * VDIM), F32), S((nh, nq, 1, blk), F32)] + [S((N_CHIPS,) + s.shape, s.dtype) for s in shards],
        scratch_shapes=[pltpu.VMEM((1, blk), F32), pltpu.VMEM((1, blk), F32), pltpu.VMEM((VDIM, blk), F32),
                        pltpu.VMEM((2, blk, blk), F32), pltpu.VMEM((2, blk, blk), BF16), pltpu.VMEM((2, 1, blk), F32),
                        pltpu.VMEM((blk, blk), F32)]
        + ([pltpu.SemaphoreType.DMA((n, 3)), pltpu.SemaphoreType.DMA((n, 3))] if n else []),
        compiler_params=_params(2))(q, k, v, *shards)
    return outs[0], outs[1], list(outs[2:])


def attention_out_bwd(name, dh, w_o, o):
    t, d = dh.shape
    n = w_o.shape[0]
    blk = ATT_BLOCK

    def body(dh_ref, w_ref, o_ref, do_ref, d_ref):
        do_ref[...] = lax.dot_general(dh_ref[...].astype(BF16), w_ref[...], _NT, preferred_element_type=F32)
        for h in range(N_HEADS):
            cols = slice(h * VDIM, (h + 1) * VDIM)
            d_ref[h] = jnp.sum((do_ref[:, cols] * o_ref[:, cols]).T, axis=0, keepdims=True)

    tile = pl.BlockSpec((blk, n), lambda i: (i, 0))
    return _pallas_call(
        body, name=name, grid=(t // blk,),
        in_specs=[pl.BlockSpec((blk, d), lambda i: (i, 0)), _resident((n, d), lambda i: (0, 0)), tile],
        out_specs=[tile, pl.BlockSpec((N_HEADS, None, 1, blk), lambda i: (0, i, 0, 0))],
        out_shape=[S((t, n), F32), S((N_HEADS, t // blk, 1, blk), F32)], compiler_params=_params(1))(dh, w_o, o)


def attention_bwd(name, q, k, v, do, lse, delta, parts=()):
    nh, t, _ = q.shape
    blk = ATT_BLOCK
    nq = t // blk
    n_pairs = nq * (nq + 1) // 2
    n = len(parts)
    scale = SCORE_SCALE

    def body(q_ref, k_ref, v_ref, do_ref, lse_ref, dl_ref, *refs):
        src = refs[:n]
        dq_ref, dk_ref, dv_ref = refs[n:n + 3]
        dst = refs[n + 3:2 * n + 3]
        s_buf, dp_buf, p_buf, ds_buf, bias_ref = refs[2 * n + 3:2 * n + 8]
        if n:
            send_sems, recv_sems = refs[2 * n + 8:]
            _ride_along(scatter_ici_copies(src, dst, send_sems, recv_sems), (pl.program_id(0),), (nh,))

        @pl.when(pl.program_id(0) == 0)
        def _():
            bias_ref[...] = jnp.where(_chunk_mask_t(0, 0, blk, blk), 0.0, MASK_VALUE)

        dq_ref[...] = jnp.zeros_like(dq_ref)
        dk_ref[...] = jnp.zeros_like(dk_ref)
        dv_ref[...] = jnp.zeros_like(dv_ref)

        def rows(x):
            return pl.ds(pl.multiple_of(x * blk, blk), blk)

        def after(jb):
            j, b = jb
            wrap = b == nq - 1 - j
            return jnp.where(wrap, j + 1, j), jnp.where(wrap, 0, b + 1)

        def products(jb, slot):
            j, b = jb
            s_buf[slot] = lax.dot_general(k_ref[rows(j), :], q_ref[rows(j + b), :], _NT, preferred_element_type=F32)
            dp_buf[slot] = lax.dot_general(v_ref[rows(j), :], do_ref[rows(j + b), :].astype(BF16), _NT, preferred_element_type=F32)

        def softmax_bwd(jb, slot):
            j, b = jb
            s = s_buf[slot] + bias_ref[...] * (b == 0).astype(F32)
            p = jnp.exp2(s * SCORE_SCALE_LOG2 - lse_ref[j + b] * LOG2_E)
            p_buf[slot] = p.astype(BF16)
            ds_buf[slot] = (p * (dp_buf[slot] - dl_ref[j + b]) * scale).astype(BF16)

        def gradients(jb, slot):
            j, b = jb
            dv_ref[rows(j), :] += jnp.dot(p_buf[slot], do_ref[rows(j + b), :].astype(BF16), preferred_element_type=F32)
            dk_ref[rows(j), :] += jnp.dot(ds_buf[slot], q_ref[rows(j + b), :], preferred_element_type=F32)
            dq_ref[rows(j + b), :] += lax.dot_general(ds_buf[slot], k_ref[rows(j), :], _TN, preferred_element_type=F32)

        def step(state, slot):
            third, second, first = state
            gradients(third, slot)
            softmax_bwd(second, 1 - slot)
            products(first, slot)
            return second, first, after(first)

        zero = jnp.int32(0)
        pair0 = (zero, zero)
        products(pair0, 0)
        if n_pairs == 1:
            softmax_bwd(pair0, 0)
            gradients(pair0, 0)
        else:
            pair1 = after(pair0)
            products(pair1, 1)
            softmax_bwd(pair0, 0)
            steady = n_pairs - 2
            state = lax.fori_loop(0, steady // 2, lambda u, st: step(step(st, 0), 1), (pair0, pair1, after(pair1)))
            if steady % 2:
                state = step(state, 0)
            before_last, last_pair, _ = state
            last = (n_pairs - 1) % 2
            softmax_bwd(last_pair, last)
            gradients(before_last, 1 - last)
            gradients(last_pair, last)

    head = lambda w: pl.BlockSpec((None, t, w), lambda h: (h, 0, 0))
    stats = pl.BlockSpec((None, nq, 1, blk), lambda h: (h, 0, 0, 0))
    outs = _pallas_call(
        body, name=name, grid=(nh,),
        in_specs=[head(QK), head(QK), head(VDIM), pl.BlockSpec((t, VDIM), lambda h: (0, h)), stats, stats] + [ANY] * n,
        out_specs=[head(QK), head(QK), head(VDIM)] + [ANY] * n,
        out_shape=[S((nh, t, QK), F32), S((nh, t, QK), F32), S((nh, t, VDIM), F32)] + [S(p.shape, p.dtype) for p in parts],
        scratch_shapes=[pltpu.VMEM((2, blk, blk), F32), pltpu.VMEM((2, blk, blk), F32), pltpu.VMEM((2, blk, blk), BF16),
                        pltpu.VMEM((2, blk, blk), BF16), pltpu.VMEM((blk, blk), F32)]
        + ([pltpu.SemaphoreType.DMA((n, 3)), pltpu.SemaphoreType.DMA((n, 3))] if n else []),
        compiler_params=_params(1, VMEM_LIMIT_WHOLE_HEAD))(q, k, v, do, lse, delta, *parts)
    return outs[0], outs[1], outs[2], list(outs[3:])


def _shift_down(u, s):
    rows = lax.broadcasted_iota(jnp.int32, u.shape, 0)
    return jnp.where(rows >= s, pltpu.roll(u, s, 0), 0.0)


def _shift_up(u, s):
    n = u.shape[0]
    rows = lax.broadcasted_iota(jnp.int32, u.shape, 0)
    return jnp.where(rows < n - s, pltpu.roll(u, n - s, 0), 0.0)


def _conv_specs(t, d, lanes):
    slab = lambda part: pl.BlockSpec((None, t, lanes), lambda j, part=part: (part, 0, j))
    return slab, pl.BlockSpec((3, lanes), lambda j: (0, j)), pl.BlockSpec((t, lanes), lambda j: (0, j))


def conv_fwd(name, bcx, w):
    _, t, d = bcx.shape
    lanes = _tile(d, 128, 128)
    slab, w_spec, col = _conv_specs(t, d, lanes)

    def body(b_ref, c_ref, x_ref, w_ref, y_ref):
        u = c_ref[...] * x_ref[...]
        uc = w_ref[0:1, :] * _shift_down(u, 2) + w_ref[1:2, :] * _shift_down(u, 1) + w_ref[2:3, :] * u
        y_ref[...] = (b_ref[...] * uc).astype(BF16)

    return _pallas_call(
        body, name=name, grid=(d // lanes,), in_specs=[slab(0), slab(1), slab(2), w_spec], out_specs=col,
        out_shape=S((t, d), BF16), compiler_params=_params(1))(bcx, bcx, bcx, w)


def conv_bwd(name, bcx, w, dy):
    _, t, d = bcx.shape
    lanes = _tile(d, 128, 128)
    slab, w_spec, col = _conv_specs(t, d, lanes)

    def body(b_ref, c_ref, x_ref, w_ref, dy_ref, d_ref, dw_ref):
        c = c_ref[...]
        x = x_ref[...]
        dyv = dy_ref[...]
        u = c * x
        u1 = _shift_down(u, 1)
        u2 = _shift_down(u, 2)
        w0, w1, w2 = w_ref[0:1, :], w_ref[1:2, :], w_ref[2:3, :]
        d_ref[0] = (dyv * (w0 * u2 + w1 * u1 + w2 * u)).astype(BF16)
        duc = dyv * b_ref[...]
        dw_ref[0:1, :] = jnp.sum(duc * u2, axis=0, keepdims=True)
        dw_ref[1:2, :] = jnp.sum(duc * u1, axis=0, keepdims=True)
        dw_ref[2:3, :] = jnp.sum(duc * u, axis=0, keepdims=True)
        du = w2 * duc + w1 * _shift_up(duc, 1) + w0 * _shift_up(duc, 2)
        d_ref[1] = (du * x).astype(BF16)
        d_ref[2] = (du * c).astype(BF16)

    return _pallas_call(
        body, name=name, grid=(d // lanes,), in_specs=[slab(0), slab(1), slab(2), w_spec, col],
        out_specs=[pl.BlockSpec((3, t, lanes), lambda j: (0, 0, j)), w_spec], out_shape=[S((3, t, d), BF16), S((3, d), F32)],
        compiler_params=_params(1))(bcx, bcx, bcx, w, dy)


def adamw(name, w, g, m, v):
    r, c = w.shape
    tr = _tile(r, 512)

    def body(w_ref, g_ref, m_ref, v_ref, d_ref, mo_ref, vo_ref):
        gv = g_ref[...]
        m_new = ADAM_B1 * m_ref[...] + (1.0 - ADAM_B1) * gv
        v_new = ADAM_B2 * v_ref[...] + (1.0 - ADAM_B2) * (gv * gv)
        m_hat = m_new / (1.0 - ADAM_B1 ** ADAM_STEP)
        v_hat = v_new / (1.0 - ADAM_B2 ** ADAM_STEP)
        d_ref[...] = -ADAM_LR * (m_hat / (jnp.sqrt(v_hat) + ADAM_EPS) + ADAM_WD * w_ref[...])
        mo_ref[...] = m_new
        vo_ref[...] = v_new

    blk = pl.BlockSpec((tr, c), lambda i: (i, 0))
    return _pallas_call(
        body, name=name, grid=(r // tr,), in_specs=[blk] * 4, out_specs=[blk] * 3, out_shape=[S((r, c), F32)] * 3,
        compiler_params=_params(1))(w, g, m, v)


def _place():
    x, y, c = lax.axis_index("x"), lax.axis_index("y"), lax.axis_index("c")
    other_chips = [(1 - x, y), (x, 1 - y), (1 - x, 1 - y)]
    return x, y, c, other_chips


def _half(c, rows):
    return pl.ds(pl.multiple_of(c * (rows // 2), 16), rows // 2)


def gather_weight_shards(shards):
    n = len(shards)

    def body(*refs):
        src = refs[:n]
        dst = refs[n:2 * n]
        send_sems, recv_sems = refs[2 * n:]
        x, y, c, chips = _place()
        me = 2 * x + y
        sibling = (x, y, 1 - c)

        def copy(i, slot, half_of, sem, to, from_input=False):
            rows = _half(half_of, src[i].shape[0])
            return pltpu.make_async_remote_copy(
                src_ref=src[i].at[rows] if from_input else dst[i].at[slot, rows], dst_ref=dst[i].at[slot, rows],
                send_sem=send_sems.at[i, sem], recv_sem=recv_sems.at[i, sem], device_id=to, device_id_type=MESH)

        sent = []
        for i in range(n):
            for j, chip in enumerate(chips):
                sent.append(copy(i, me, c, j, (*chip, c), from_input=True))
                sent[-1].start()
        for i in range(n):
            for j, (px, py) in enumerate(chips):
                copy(i, 2 * px + py, c, j, sibling).wait_recv()
                sent.append(copy(i, 2 * px + py, c, 3 + j, sibling))
                sent[-1].start()
        for i in range(n):
            for j, (px, py) in enumerate(chips):
                copy(i, 2 * px + py, 1 - c, 3 + j, sibling).wait_recv()
        for cp in sent:
            cp.wait_send()

    outs = _pallas_call(
        body, name="gather_weight_shards", in_specs=[ANY] * n, out_specs=[ANY] * n,
        out_shape=[S((N_CHIPS,) + s.shape, s.dtype) for s in shards],
        scratch_shapes=[pltpu.SemaphoreType.DMA((n, 6)), pltpu.SemaphoreType.DMA((n, 6))],
    )(*shards)
    return _fill_own_slot(outs, [s[None] for s in shards])


def gather_ici_copies(src, dst, send_sems, recv_sems):
    x, y, c, chips = _place()
    me = 2 * x + y
    pairs = []
    for i in range(len(src)):
        rows = _half(c, src[i].shape[0])
        for j, (px, py) in enumerate(chips):
            def copy(slot):
                return pltpu.make_async_remote_copy(
                    src_ref=src[i].at[rows], dst_ref=dst[i].at[slot, rows], send_sem=send_sems.at[i, j],
                    recv_sem=recv_sems.at[i, j], device_id=(px, py, c), device_id_type=MESH)
            pairs.append((copy(me), copy(2 * px + py)))
    return pairs


def scatter_ici_copies(src, dst, send_sems, recv_sems):
    x, y, c, chips = _place()
    me = 2 * x + y
    pairs = []
    for i in range(len(src)):
        for j, (px, py) in enumerate(chips):
            def copy(from_slot, to_slot):
                return pltpu.make_async_remote_copy(
                    src_ref=src[i].at[from_slot], dst_ref=dst[i].at[to_slot], send_sem=send_sems.at[i, j],
                    recv_sem=recv_sems.at[i, j], device_id=(px, py, c), device_id_type=MESH)
            pairs.append((copy(2 * px + py, me), copy(me, 2 * px + py)))
    return pairs


def _ride_along(pairs, grid_ids, grid_sizes):
    first = grid_ids[0] == 0
    last = grid_ids[0] == grid_sizes[0] - 1
    for g, size in zip(grid_ids[1:], grid_sizes[1:]):
        first = first & (g == 0)
        last = last & (g == size - 1)

    @pl.when(first)
    def _():
        for outgoing, _ in pairs:
            outgoing.start()

    @pl.when(last)
    def _():
        for _, incoming in pairs:
            incoming.wait_recv()
        for outgoing, _ in pairs:
            outgoing.wait_send()


def _fill_own_slot(gathered, own):
    me = 2 * lax.axis_index("x") + lax.axis_index("y")
    return [lax.dynamic_update_slice(g, o, (me,) + (0,) * (g.ndim - 1)) for g, o in zip(gathered, own)]


def forward_copies(src, dst, send_sems, recv_sems):
    x, y, c, chips = _place()
    pairs = []
    for i in range(len(src)):
        for j, (px, py) in enumerate(chips):
            def copy(half_of):
                rows = _half(half_of, src[i].shape[1])
                return pltpu.make_async_remote_copy(
                    src_ref=src[i].at[2 * px + py, rows], dst_ref=dst[i].at[2 * px + py, rows], send_sem=send_sems.at[i, j],
                    recv_sem=recv_sems.at[i, j], device_id=(x, y, 1 - c), device_id_type=MESH)
            pairs.append((copy(c), copy(1 - c)))
    return pairs


def attention_out_proj(name, attn, w_o, resid, next_gain, arriving):
    t, kdim = attn.shape
    n = w_o.shape[1]
    tm = _tile(t, 512)
    m = len(arriving)

    def body(x_ref, w_ref, r_ref, g_ref, *refs):
        h_ref, a_ref = refs[m:m + 2]
        _ride_along(forward_copies(refs[:m], refs[m + 2:2 * m + 2], *refs[2 * m + 2:]), (pl.program_id(0),), (t // tm,))
        y = r_ref[...] + jnp.dot(x_ref[...].astype(BF16), w_ref[...], preferred_element_type=F32)
        h_ref[...] = y
        a_ref[...] = (y * _rstd(y) * g_ref[...]).astype(BF16)

    row = pl.BlockSpec((tm, n), lambda i: (i, 0))
    outs = _pallas_call(
        body, name=name, grid=(t // tm,),
        in_specs=[pl.BlockSpec((tm, kdim), lambda i: (i, 0)), _resident((kdim, n), lambda i: (0, 0)), row,
                  pl.BlockSpec((1, n), lambda i: (0, 0))] + [ANY] * m,
        out_specs=[row, row] + [ANY] * m, out_shape=[S((t, n), F32), S((t, n), BF16)] + [S(g.shape, g.dtype) for g in arriving],
        input_output_aliases={4 + i: 2 + i for i in range(m)},
        scratch_shapes=[pltpu.SemaphoreType.DMA((m, 3)), pltpu.SemaphoreType.DMA((m, 3))],
        compiler_params=_params(1))(attn, w_o, resid, next_gain, *arriving)
    return outs[0], outs[1], list(outs[2:])


def sibling_swap_halves(name, grads):
    n = len(grads)

    def body(*refs):
        src = refs[:n]
        dst = refs[n:2 * n]
        send_sems, recv_sems = refs[2 * n:]
        x, y, c, _ = _place()
        copies = [pltpu.make_async_remote_copy(
            src_ref=src[i].at[:, _half(1 - c, src[i].shape[1]), :], dst_ref=dst[i], send_sem=send_sems.at[i],
            recv_sem=recv_sems.at[i], device_id=(x, y, 1 - c), device_id_type=MESH) for i in range(n)]
        for cp in copies:
            cp.start()
        for cp in copies:
            cp.wait()

    return _pallas_call(
        body, name=name, in_specs=[ANY] * n, out_specs=[ANY] * n,
        out_shape=[S((g.shape[0], g.shape[1] // 2, g.shape[2]), g.dtype) for g in grads],
        scratch_shapes=[pltpu.SemaphoreType.DMA((n,)), pltpu.SemaphoreType.DMA((n,))],
    )(*grads)


def add_halves(name, g, rx):
    _, r, cdim = g.shape
    r2 = r // 2
    tr = _tile(r2, 512, 16)
    nb = r2 // tr

    def body(lo_ref, hi_ref, rx_ref, o_ref):
        mine = jnp.where(lax.axis_index("c") == 0, lo_ref[...], hi_ref[...])
        o_ref[...] = (mine.astype(F32) + rx_ref[...].astype(F32)).astype(BF16)

    half = pl.BlockSpec((None, tr, cdim), lambda k, i: (k, i, 0))
    return _pallas_call(
        body, name=name, grid=(N_CHIPS, nb),
        in_specs=[half, pl.BlockSpec((None, tr, cdim), lambda k, i: (k, nb + i, 0)), half],
        out_specs=half, out_shape=S((N_CHIPS, r2, cdim), BF16), compiler_params=_params(2))(g, g, rx)


def _own_slots(parts):
    me = 2 * lax.axis_index("x") + lax.axis_index("y")
    return [lax.dynamic_slice(p, (me, 0, 0), (1,) + p.shape[1:]) for p in parts]


def sum_chips(name, parts):
    _, r2, cdim = parts.shape
    tr = _tile(r2, 512, 16)

    def body(p_ref, o_ref):
        acc = p_ref[0].astype(F32)
        for k in range(1, N_CHIPS):
            acc = acc + p_ref[k].astype(F32)
        o_ref[...] = acc

    return _pallas_call(
        body, name=name, grid=(r2 // tr,), in_specs=[pl.BlockSpec((N_CHIPS, tr, cdim), lambda i: (0, i, 0))],
        out_specs=pl.BlockSpec((tr, cdim), lambda i: (i, 0)), out_shape=S((r2, cdim), F32), compiler_params=_params(1))(parts)


def join_copies(src, dst, where, send_sems, recv_sems):
    x, y, c, _ = _place()
    pairs = []
    for i in range(len(src)):
        def copy(half_of):
            r2 = src[i].shape[0]
            rows = pl.ds(pl.multiple_of(where[i][1] + half_of * r2, 8), r2)
            return pltpu.make_async_remote_copy(
                src_ref=src[i], dst_ref=dst[where[i][0]].at[rows], send_sem=send_sems.at[i],
                recv_sem=recv_sems.at[i], device_id=(x, y, 1 - c), device_id_type=MESH)
        pairs.append((copy(c), copy(1 - c)))
    return pairs


def _fill_own_halves(targets, halves, where):
    targets = list(targets)
    c = lax.axis_index("c")
    for h, (tgt, first) in zip(halves, where):
        targets[tgt] = lax.dynamic_update_slice(targets[tgt], h, (first + c * h.shape[0], 0))
    return targets


def sibling_join_halves(name, halves, targets, where):
    n = len(halves)

    def body(*refs):
        pairs = join_copies(refs[:n], refs[n:n + len(targets)], where, *refs[n + len(targets):])
        for outgoing, _ in pairs:
            outgoing.start()
        for _, incoming in pairs:
            incoming.wait_recv()
        for outgoing, _ in pairs:
            outgoing.wait_send()

    outs = _pallas_call(
        body, name=name, in_specs=[ANY] * n, out_specs=[ANY] * len(targets), out_shape=[S(tg, F32) for tg in targets],
        scratch_shapes=[pltpu.SemaphoreType.DMA((n,)), pltpu.SemaphoreType.DMA((n,))],
    )(*halves)
    return _fill_own_halves(outs, halves, where)


def all_reduce_small(name, packed):
    rows, width = packed.shape

    def body(x_ref, o_ref, gathered, send_sems, recv_sems):
        x, y, c, _ = _place()
        me = 4 * x + 2 * y + c
        gathered[me] = x_ref[...]
        flips = [(fx, fy, fc) for fx in (0, 1) for fy in (0, 1) for fc in (0, 1)][1:]

        def copy(r, slot, to):
            return pltpu.make_async_remote_copy(
                src_ref=x_ref, dst_ref=gathered.at[slot], send_sem=send_sems.at[r], recv_sem=recv_sems.at[r],
                device_id=to, device_id_type=MESH)

        def peer(f):
            return (x ^ f[0], y ^ f[1], c ^ f[2])

        sent = [copy(r, me, peer(f)) for r, f in enumerate(flips)]
        for cp in sent:
            cp.start()
        for r, f in enumerate(flips):
            px, py, pc = peer(f)
            copy(r, 4 * px + 2 * py + pc, peer(f)).wait_recv()
        for cp in sent:
            cp.wait_send()
        acc = gathered[0]
        for k in range(1, N_DEV):
            acc = acc + gathered[k]
        o_ref[...] = acc

    vmem = pl.BlockSpec(memory_space=pltpu.VMEM)
    return _pallas_call(
        body, operands_in_hbm=False, name=name, in_specs=[vmem], out_specs=vmem, out_shape=S((rows, width), F32),
        scratch_shapes=[pltpu.VMEM((N_DEV, rows, width), F32), pltpu.SemaphoreType.DMA((N_DEV - 1,)),
                        pltpu.SemaphoreType.DMA((N_DEV - 1,))],
    )(packed)


def _rope_tables(positions):
    inv_freq = 1.0 / (ROPE_THETA ** (jnp.arange(0, ROPE, 2, dtype=F32) / ROPE))
    ang = positions.astype(F32)[:, None] * inv_freq
    return jnp.cos(ang), jnp.sin(ang)


def _unstack_cols(w):
    k4, k, n4 = w.shape
    return jnp.transpose(w, (1, 0, 2)).reshape(k, k4 * n4)


def _stack_cols(w):
    k, n = w.shape
    return jnp.transpose(w.reshape(k, N_CHIPS, n // N_CHIPS), (1, 0, 2))


def kernel(x, positions, mla_norm, mla_w_in, mla_g_cq, mla_g_ckv, mla_w_uq, mla_w_ukv, mla_w_o, conv_norm, conv_w_in, conv_w, conv_w_out, ffn_norm, ffn_w_gate, ffn_w_up, ffn_w_down, final_norm, loss_target, m_mla_norm, m_mla_w_in, m_mla_g_cq, m_mla_g_ckv, m_mla_w_uq, m_mla_w_ukv, m_mla_w_o, m_conv_norm, m_conv_w_in, m_conv_w, m_conv_w_out, m_ffn_norm, m_ffn_w_gate, m_ffn_w_up, m_ffn_w_down, m_final_norm, v_mla_norm, v_mla_w_in, v_mla_g_cq, v_mla_g_ckv, v_mla_w_uq, v_mla_w_ukv, v_mla_w_o, v_conv_norm, v_conv_w_in, v_conv_w, v_conv_w_out, v_ffn_norm, v_ffn_w_gate, v_ffn_w_up, v_ffn_w_down, v_final_norm):
    weights = dict(mla_norm=mla_norm, mla_w_in=mla_w_in, mla_g_cq=mla_g_cq, mla_g_ckv=mla_g_ckv, mla_w_uq=mla_w_uq,
                   mla_w_ukv=mla_w_ukv, mla_w_o=mla_w_o, conv_norm=conv_norm, conv_w_in=conv_w_in, conv_w=conv_w,
                   conv_w_out=conv_w_out, ffn_norm=ffn_norm, ffn_w_gate=ffn_w_gate, ffn_w_up=ffn_w_up,
                   ffn_w_down=ffn_w_down, final_norm=final_norm)
    m_in = dict(mla_norm=m_mla_norm, mla_w_in=m_mla_w_in, mla_g_cq=m_mla_g_cq, mla_g_ckv=m_mla_g_ckv, mla_w_uq=m_mla_w_uq,
                mla_w_ukv=m_mla_w_ukv, mla_w_o=m_mla_w_o, conv_norm=m_conv_norm, conv_w_in=m_conv_w_in, conv_w=m_conv_w,
                conv_w_out=m_conv_w_out, ffn_norm=m_ffn_norm, ffn_w_gate=m_ffn_w_gate, ffn_w_up=m_ffn_w_up,
                ffn_w_down=m_ffn_w_down, final_norm=m_final_norm)
    v_in = dict(mla_norm=v_mla_norm, mla_w_in=v_mla_w_in, mla_g_cq=v_mla_g_cq, mla_g_ckv=v_mla_g_ckv, mla_w_uq=v_mla_w_uq,
                mla_w_ukv=v_mla_w_ukv, mla_w_o=v_mla_w_o, conv_norm=v_conv_norm, conv_w_in=v_conv_w_in, conv_w=v_conv_w,
                conv_w_out=v_conv_w_out, ffn_norm=v_ffn_norm, ffn_w_gate=v_ffn_w_gate, ffn_w_up=v_ffn_w_up,
                ffn_w_down=v_ffn_w_down, final_norm=v_final_norm)
    big = ["mla_w_in", "mla_w_uq", "mla_w_ukv", "mla_w_o", "conv_w_in", "conv_w_out", "ffn_w_gate", "ffn_w_up", "ffn_w_down"]
    order = list(weights)

    t, d = x.shape[1], x.shape[2]
    h0 = x.reshape(t, d)
    target = loss_target.reshape(t, d)
    cos, sin = _rope_tables(positions.reshape(t))

    def rows2d(a):
        return a.reshape(-1, a.shape[-1])

    first, later = big[:4], big[4:]
    shards = {n: rows2d(weights[n]).astype(BF16) for n in big}
    gathered = dict(zip(first, gather_weight_shards([shards[n] for n in first])))
    w_in = gathered["mla_w_in"].reshape(-1, gathered["mla_w_in"].shape[-1])
    w_uq = _unstack_cols(gathered["mla_w_uq"])
    w_ukv = _unstack_cols(gathered["mla_w_ukv"])
    w_o = gathered["mla_w_o"].reshape(-1, d)

    chip = 2 * lax.axis_index("x") + lax.axis_index("y")
    core = lax.axis_index("c")
    d4 = d // N_CHIPS
    first_core = (core == 0).astype(F32)

    def place_shard(shard):
        full = jnp.zeros((shard.shape[0], d), F32)
        return lax.dynamic_update_slice(full, shard * first_core, (0, chip * d4))

    def pack_rows(rows):
        idx = lax.broadcasted_iota(jnp.int32, (SMALL_ROWS, d), 0)
        out = jnp.zeros((SMALL_ROWS, d), F32)
        for r, row in enumerate(rows):
            out = out + jnp.where(idx == r, row, 0.0)
        return out

    cw = place_shard(conv_w.reshape(3, d4))
    pre = all_reduce_small("all_gather_conv_small", pack_rows([place_shard(conv_norm.reshape(1, d4)), cw[0:1], cw[1:2], cw[2:3]]))
    conv_norm_full = pre[0:1]
    conv_w_full = pre[1:4]

    a0 = rms_fwd("mla_norm_fwd", h0, mla_norm)
    proj, cq, ckv, kr = mla_in_proj("mla_in_proj", a0, w_in, mla_g_cq, mla_g_ckv, cos, sin)
    q = linear("mla_q_up", cq, w_uq, F32)
    kv = linear("mla_kv_up", ckv, w_ukv, BF16)
    qh, kh, vh, conv_arriving = qkv_heads("qkv_heads", q, kv, kr, cos, sin, [shards[n] for n in later[:2]])
    attn, lse, ffn_arriving = attention_fwd("attention_fwd", qh, kh, vh, [shards[n] for n in later[2:]])
    h1, a1, handed = attention_out_proj("mla_out_proj", attn, w_o, h0, ffn_norm[0:1], conv_arriving + ffn_arriving)
    gathered.update(zip(later, _fill_own_slot(handed, [shards[n][None] for n in later])))
    cw_in = _unstack_cols(gathered["conv_w_in"])
    cw_out = gathered["conv_w_out"].reshape(-1, d)
    wg_all, wu_all, wd_all = gathered["ffn_w_gate"], gathered["ffn_w_up"], gathered["ffn_w_down"]

    def ffn_forward(tag, h, a, layer, next_gain):
        g, u, z = ffn_up(f"ffn{tag}_up", a, wg_all, wu_all, layer)
        return g, u, z, ffn_down(f"ffn{tag}_down", z, wd_all, layer, h, next_gain)

    g0, u0, z0, (h2, a2) = ffn_forward(0, h1, a1, 0, conv_norm_full)
    bcx = conv_in_proj("conv_in_proj", a2, cw_in)
    yc = conv_fwd("conv_fwd", bcx, conv_w_full)
    h3, a3 = linear("conv_out_proj", yc, cw_out, F32, resid=h2, next_gain=ffn_norm[1:2])
    g1, u1, z1 = ffn_up("ffn1_up", a3, wg_all, wu_all, 1)
    dh4, d_final_norm, loss_local = ffn_down_loss("ffn1_down_loss", z1, wd_all, 1, h3, final_norm.reshape(1, d), target)

    def ffn_backward(tag, dh, h, layer, a, g, u, z):
        dg, du = ffn_bwd_hidden(f"ffn{tag}_bwd_hidden", dh, wd_all, layer, g, u)
        d_wd = ffn_wgrad_down(f"ffn{tag}_wgrad_down", z, dh)
        dh_prev, d_norm = ffn_bwd_input(f"ffn{tag}_bwd_input", dg, du, wg_all, wu_all, layer, h, ffn_norm[layer:layer + 1], dh)
        d_wg = ffn_wgrad_up(f"ffn{tag}_wgrad_gate", a, dg)
        d_wu = ffn_wgrad_up(f"ffn{tag}_wgrad_up", a, du)
        return dh_prev, d_norm, [d_wg, d_wu, d_wd]

    def reduce_to_pair_sums(tag, local):
        from_sibling = sibling_swap_halves(f"sibling_swap_{tag}", local)
        return [add_halves(f"pair_sum_{tag}{i}", g, r) for i, (g, r) in enumerate(zip(local, from_sibling))]

    def sum_from_chips(tag, pair_sums, arrived):
        from_chips = _fill_own_slot(arrived, _own_slots(pair_sums))
        return [sum_chips(f"chip_sum_{tag}{i}", p) for i, p in enumerate(from_chips)]

    def shard_shape(n):
        return rows2d(weights[n]).shape

    dh3, d_ffn_norm1, ffn1_grads = ffn_backward(1, dh4, h3, 1, a3, g1, u1, z1)

    dyc = linear_nt("conv_out_bwd_input", dh3, cw_out, F32)
    d_cw_out = wgrad("conv_out_wgrad", yc, dh3)
    dbcx, d_conv_w = conv_bwd("conv_bwd", bcx, conv_w_full, dyc)
    dh2, d_conv_norm = conv_in_bwd_input("conv_in_bwd_input", dbcx, cw_in, h2, conv_norm_full, dh3)
    d_cw_in = conv_in_wgrad("conv_in_wgrad", a2, dbcx)

    dh1, d_ffn_norm0, ffn0_grads = ffn_backward(0, dh2, h1, 0, a1, g0, u0, z0)

    d_attn, delta = attention_out_bwd("mla_out_bwd_input", dh1, w_o, attn)
    d_w_o = wgrad("mla_out_wgrad", attn, dh1)
    rest_pairs = reduce_to_pair_sums("rest", [_stack_cols(d_cw_in), d_cw_out.reshape(N_CHIPS, -1, d)] + ffn1_grads + ffn0_grads
                                     + [d_w_o.reshape(N_CHIPS, -1, d)])
    dqh, dkh, dvh, rest_arrived = attention_bwd("attention_bwd", qh, kh, vh, d_attn, lse, delta, rest_pairs)
    rd, rf = ffn0_grads[0].shape[1], ffn0_grads[2].shape[1]
    rest_where = [(0, 0), (1, 0), (2, rd), (3, rd), (4, rf), (2, 0), (3, 0), (4, 0), (5, 0)]
    rest_names = later + ["mla_w_o"]
    dq, dkv, dkr, rest_grads = qkv_heads_bwd("qkv_heads_bwd", dqh, dkh, dvh, cos, sin, sum_from_chips("rest", rest_pairs, rest_arrived),
                                              [shard_shape(n) for n in rest_names], rest_where)
    grads = dict(zip(rest_names, rest_grads))
    dcq = linear_nt("mla_q_up_bwd_input", dq, w_uq, F32)
    d_w_uq = wgrad("mla_q_up_wgrad", cq, dq)
    dckv = linear_nt("mla_kv_up_bwd_input", dkv, w_ukv, F32)
    d_w_ukv = wgrad("mla_kv_up_wgrad", ckv, dkv)
    dproj, d_g_cq, d_g_ckv = mla_mid_bwd("mla_mid_bwd", proj, mla_g_cq, mla_g_ckv, dcq, dckv, dkr, cos, sin)
    d_w_in = wgrad("mla_in_wgrad", a0, dproj)
    mla_pairs = reduce_to_pair_sums("mla", [d_w_in.reshape(N_CHIPS, -1, d_w_in.shape[-1]), _stack_cols(d_w_uq), _stack_cols(d_w_ukv)])
    grad_x, d_mla_norm, mla_arrived = linear_nt_norm_bwd("mla_in_bwd_input", dproj, w_in, h0, mla_norm, dh1, mla_pairs)

    grads.update(zip(first[:3], sibling_join_halves("sibling_join_mla", sum_from_chips("mla", mla_pairs, mla_arrived),
                                                    [shard_shape(n) for n in first[:3]], [(i, 0) for i in range(3)])))

    def pad_row(v):
        return jnp.pad(v, ((0, 0), (0, d - v.shape[1])))

    small = all_reduce_small("all_reduce_small_grads", pack_rows([
        d_mla_norm, pad_row(d_g_cq), pad_row(d_g_ckv), d_ffn_norm0, d_ffn_norm1, d_final_norm, d_conv_norm,
        d_conv_w[0:1], d_conv_w[1:2], d_conv_w[2:3], jnp.broadcast_to(loss_local, (1, d))]))
    loss = small[10, 0]
    grads["mla_norm"] = small[0:1]
    grads["mla_g_cq"] = small[1:2, :mla_g_cq.shape[1]]
    grads["mla_g_ckv"] = small[2:3, :mla_g_ckv.shape[1]]
    grads["ffn_norm"] = small[3:5]
    grads["final_norm"] = small[5:6]
    grads["conv_norm"] = lax.dynamic_slice(small[6:7], (0, chip * d4), (1, d4))
    grads["conv_w"] = lax.dynamic_slice(small[7:10], (0, chip * d4), (3, d4))

    outs_g, outs_d, outs_m, outs_v = [], [], [], []
    for n in order:
        w = weights[n]
        delta_w, new_m, new_v = adamw(f"adamw_{n}", rows2d(w), grads[n].reshape(rows2d(w).shape), rows2d(m_in[n]), rows2d(v_in[n]))
        outs_g.append(grads[n].reshape(w.shape))
        outs_d.append(delta_w.reshape(w.shape))
        outs_m.append(new_m.reshape(w.shape))
        outs_v.append(new_v.reshape(w.shape))
    return (loss, grad_x.reshape(x.shape), *outs_g, *outs_d, *outs_m, *outs_v)
```

```python
import math

import jax
import jax.numpy as jnp
from jax import lax
from jax.experimental import pallas as pl
from jax.experimental.pallas import tpu as pltpu

F32 = jnp.float32
BF16 = jnp.bfloat16
S = jax.ShapeDtypeStruct

N_HEADS = 8
NOPE = 128
ROPE = 64
HALF = ROPE // 2
VDIM = 128
QK = NOPE + ROPE
CHUNK = 64
ROPE_THETA = 10000.0
RMS_EPS = 1e-6
ADAM_LR = 0.001
ADAM_B1 = 0.9
ADAM_B2 = 0.999
ADAM_EPS = 1e-08
ADAM_WD = 0.01
ADAM_STEP = 10

N_CHIPS = 4
N_DEV = 8
MASK_VALUE = -1e30
SCORE_SCALE = 1.0 / math.sqrt(QK)
LOG2_E = math.log2(math.e)
SCORE_SCALE_LOG2 = SCORE_SCALE * LOG2_E
VMEM_LIMIT = 48 * 1024 * 1024
VMEM_LIMIT_WHOLE_HEAD = 58 * 1024 * 1024
ATT_BLOCK = 512
SMALL_ROWS = 16

_NN = (((1,), (0,)), ((), ()))
_NT = (((1,), (1,)), ((), ()))
_TN = (((0,), (0,)), ((), ()))
MESH = pl.DeviceIdType.MESH
ANY = pl.BlockSpec(memory_space=pl.ANY)


def _params(n_axes, vmem_limit=VMEM_LIMIT):
    return pltpu.CompilerParams(dimension_semantics=("arbitrary",) * n_axes, vmem_limit_bytes=vmem_limit)


def _tile(n, cap, mult=8):
    for t in range(min(cap, n), 0, -1):
        if n % t == 0 and t % mult == 0:
            return t
    return n


def _sigmoid(x):
    return 0.5 * jnp.tanh(0.5 * x) + 0.5


def _mm(name, a_ops, b_ops, products, dims, grid, k_axis, outs, acc_shape, epilogue, extra_ops=()):
    na, nb, ne, no = len(a_ops), len(b_ops), len(extra_ops), len(outs)
    n_acc = 1 + max(c for _, _, c in products)
    nk = 1 if k_axis is None else grid[k_axis]

    def body(*refs):
        a_refs = refs[:na]
        b_refs = refs[na:na + nb]
        e_refs = refs[na + nb:na + nb + ne]
        o_refs = refs[na + nb + ne:na + nb + ne + no]
        acc_refs = refs[na + nb + ne + no:]

        def partial_sums():
            vals = [None] * n_acc
            for ai, bi, ci in products:
                d = lax.dot_general(a_refs[ai][...].astype(BF16), b_refs[bi][...].astype(BF16), dims,
                                    preferred_element_type=F32)
                vals[ci] = d if vals[ci] is None else vals[ci] + d
            return vals

        if nk == 1:
            epilogue(partial_sums(), e_refs, o_refs)
        else:
            k = pl.program_id(k_axis)

            @pl.when(k == 0)
            def _():
                for acc in acc_refs:
                    acc[...] = jnp.zeros_like(acc)

            for acc, v in zip(acc_refs, partial_sums()):
                acc[...] += v

            @pl.when(k == nk - 1)
            def _():
                epilogue([acc[...] for acc in acc_refs], e_refs, o_refs)

    ops = list(a_ops) + list(b_ops) + list(extra_ops)
    return pl.pallas_call(
        body, name=name, grid=grid,
        in_specs=[s for _, s in ops], out_specs=[s for _, s in outs], out_shape=[o for o, _ in outs],
        scratch_shapes=[pltpu.VMEM(acc_shape, F32) for _ in range(n_acc if nk > 1 else 0)],
        compiler_params=_params(len(grid)),
    )(*[a for a, _ in ops])


def _store(accs, e_refs, o_refs):
    o_refs[0][...] = accs[0].astype(o_refs[0].dtype)


def linear(name, x, w, out_dtype, resid=None, next_gain=None):
    t, k = x.shape
    n = w.shape[1]
    tm = _tile(t, 512)
    tn = n if n <= 2048 else _tile(n, 1024, 128)
    tile = pl.BlockSpec((tm, tn), lambda j, i: (i, j))
    extra = [] if resid is None else [(resid, tile)]
    outs = [(S((t, n), out_dtype), tile)]
    if next_gain is not None:
        assert tn == n
        extra.append((next_gain, pl.BlockSpec((1, n), lambda j, i: (0, 0))))
        outs.append((S((t, n), BF16), tile))

    def epilogue(accs, e_refs, o_refs):
        y = accs[0] if resid is None else e_refs[0][...] + accs[0]
        o_refs[0][...] = y.astype(out_dtype)
        if next_gain is not None:
            o_refs[1][...] = (y * _rstd(y) * e_refs[-1][...]).astype(BF16)

    res = _mm(name, [(x, pl.BlockSpec((tm, k), lambda j, i: (i, 0)))], [(w, pl.BlockSpec((k, tn), lambda j, i: (0, j)))],
              [(0, 0, 0)], _NN, (n // tn, t // tm), None, outs, None, epilogue, extra)
    return res[0] if next_gain is None else res


def linear_nt(name, dy, w, out_dtype):
    t, n = dy.shape
    k = w.shape[0]
    tm = _tile(t, 512)
    tc = n if n <= 2048 else _tile(n, 1024, 128)
    return _mm(name, [(dy, pl.BlockSpec((tm, tc), lambda i, c: (i, c)))], [(w, pl.BlockSpec((k, tc), lambda i, c: (0, c)))],
               [(0, 0, 0)], _NT, (t // tm, n // tc), 1,
               [(S((t, k), out_dtype), pl.BlockSpec((tm, k), lambda i, c: (i, 0)))], (tm, k), _store)[0]


def wgrad(name, x, dy):
    t, k = x.shape
    n = dy.shape[1]
    tk = _tile(t, 512)
    tn = n if n <= 1024 else _tile(n, 1024, 128)
    return _mm(name, [(x, pl.BlockSpec((tk, k), lambda j, s: (s, 0)))], [(dy, pl.BlockSpec((tk, tn), lambda j, s: (s, j)))],
               [(0, 0, 0)], _TN, (n // tn, t // tk), 1,
               [(S((k, n), BF16), pl.BlockSpec((k, tn), lambda j, s: (0, j)))], (k, tn), _store)[0]


def _resident(shape, index_map):
    return pl.BlockSpec(shape, index_map, pipeline_mode=pl.Buffered(1))


def ffn_up(name, a, wg_all, wu_all, layer):
    t, d = a.shape
    f4 = wg_all.shape[2]
    tm = _tile(t, 512)
    w_spec = _resident((N_CHIPS, d, f4), lambda i: (0, layer, 0))
    h_spec = pl.BlockSpec((N_CHIPS, tm, f4), lambda i: (0, i, 0))

    def body(a_ref, wg_ref, wu_ref, zg_ref, zu_ref, z_ref):
        av = a_ref[...]
        for k in range(N_CHIPS):
            g = jnp.dot(av, wg_ref[k], preferred_element_type=F32)
            u = jnp.dot(av, wu_ref[k], preferred_element_type=F32)
            sg = _sigmoid(g)
            silu = g * sg
            zg_ref[k] = (u * (sg * (1.0 + g * (1.0 - sg)))).astype(BF16)
            zu_ref[k] = silu.astype(BF16)
            z_ref[k] = (silu * u).astype(BF16)

    return pl.pallas_call(
        body, name=name, grid=(t // tm,), in_specs=[pl.BlockSpec((tm, d), lambda i: (i, 0)), w_spec, w_spec],
        out_specs=[h_spec] * 3, out_shape=[S((N_CHIPS, t, f4), BF16)] * 3, compiler_params=_params(1))(a, wg_all, wu_all)


def ffn_down(name, z, wd_all, layer, resid, next_gain=None):
    _, t, f4 = z.shape
    d = wd_all.shape[2]
    tm = _tile(t, 512)
    row = pl.BlockSpec((tm, d), lambda i: (i, 0))
    normed = next_gain is not None

    def body(z_ref, wd_ref, r_ref, *refs):
        acc = r_ref[...]
        for k in range(N_CHIPS):
            acc = acc + jnp.dot(z_ref[k], wd_ref[k], preferred_element_type=F32)
        refs[-2 if normed else -1][...] = acc
        if normed:
            refs[-1][...] = (acc * _rstd(acc) * refs[0][...]).astype(BF16)

    res = pl.pallas_call(
        body, name=name, grid=(t // tm,),
        in_specs=[pl.BlockSpec((N_CHIPS, tm, f4), lambda i: (0, i, 0)), _resident((N_CHIPS, f4, d), lambda i: (0, layer, 0)), row]
        + ([pl.BlockSpec((1, d), lambda i: (0, 0))] if normed else []),
        out_specs=[row] * (2 if normed else 1), out_shape=[S((t, d), F32)] + ([S((t, d), BF16)] if normed else []),
        compiler_params=_params(1))(z, wd_all, resid, *([next_gain] if normed else []))
    return res if normed else res[0]


def ffn_bwd_hidden(name, dh, wd_all, layer, zg, zu):
    t, d = dh.shape
    f4 = zg.shape[2]
    tm = _tile(t, 512)
    h_spec = pl.BlockSpec((N_CHIPS, tm, f4), lambda i: (0, i, 0))

    def body(dh_ref, wd_ref, zg_ref, zu_ref, dg_ref, du_ref):
        dhb = dh_ref[...].astype(BF16)
        for k in range(N_CHIPS):
            dz = lax.dot_general(dhb, wd_ref[k], _NT, preferred_element_type=F32)
            dg_ref[k] = (dz * zg_ref[k].astype(F32)).astype(BF16)
            du_ref[k] = (dz * zu_ref[k].astype(F32)).astype(BF16)

    return pl.pallas_call(
        body, name=name, grid=(t // tm,),
        in_specs=[pl.BlockSpec((tm, d), lambda i: (i, 0)), _resident((N_CHIPS, f4, d), lambda i: (0, layer, 0)), h_spec, h_spec],
        out_specs=[h_spec] * 2, out_shape=[S((N_CHIPS, t, f4), BF16)] * 2, compiler_params=_params(1))(dh, wd_all, zg, zu)


def _norm_bwd_specs(tm, d):
    row = pl.BlockSpec((tm, d), lambda i: (i, 0))
    vec = pl.BlockSpec((1, d), lambda i: (0, 0))
    return [row, vec, row], [row, vec]


def _norm_bwd_tail(da, h_ref, g_ref, dhi_ref, dho_ref, dgain_ref):
    dx, dgain = _rms_bwd(h_ref[...], g_ref[...], da)
    dho_ref[...] = dhi_ref[...] + dx

    @pl.when(pl.program_id(0) == 0)
    def _():
        dgain_ref[...] = jnp.zeros_like(dgain_ref)

    dgain_ref[...] += dgain


def ffn_bwd_input(name, dg, du, wg_all, wu_all, layer, h, gain, dh_in):
    _, t, f4 = dg.shape
    d = h.shape[1]
    tm = _tile(t, 512)
    h_spec = pl.BlockSpec((N_CHIPS, tm, f4), lambda i: (0, i, 0))
    w_spec = _resident((N_CHIPS, d, f4), lambda i: (0, layer, 0))
    tail_in, tail_out = _norm_bwd_specs(tm, d)

    def body(dg_ref, du_ref, wg_ref, wu_ref, *tail):
        acc = jnp.zeros((tm, d), F32)
        for k in range(N_CHIPS):
            acc = acc + lax.dot_general(dg_ref[k], wg_ref[k], _NT, preferred_element_type=F32)
            acc = acc + lax.dot_general(du_ref[k], wu_ref[k], _NT, preferred_element_type=F32)
        _norm_bwd_tail(acc, *tail)

    return pl.pallas_call(
        body, name=name, grid=(t // tm,), in_specs=[h_spec, h_spec, w_spec, w_spec] + tail_in, out_specs=tail_out,
        out_shape=[S((t, d), F32), S((1, d), F32)], compiler_params=_params(1))(dg, du, wg_all, wu_all, h, gain, dh_in)


def ffn_wgrad_up(name, a, dy):
    t, d = a.shape
    f4 = dy.shape[2]
    tk = _tile(t, 512)
    nt = t // tk

    def body(a_ref, dy_ref, o_ref, acc):
        s = pl.program_id(0)

        @pl.when(s == 0)
        def _():
            acc[...] = jnp.zeros_like(acc)

        at = a_ref[...].T
        for k in range(N_CHIPS):
            acc[k] += jnp.dot(at, dy_ref[k], preferred_element_type=F32)

        @pl.when(s == nt - 1)
        def _():
            o_ref[...] = acc[...].astype(BF16)

    return pl.pallas_call(
        body, name=name, grid=(nt,),
        in_specs=[pl.BlockSpec((tk, d), lambda s: (s, 0)), pl.BlockSpec((N_CHIPS, tk, f4), lambda s: (0, s, 0))],
        out_specs=pl.BlockSpec((N_CHIPS, d, f4), lambda s: (0, 0, 0)), out_shape=S((N_CHIPS, d, f4), BF16),
        scratch_shapes=[pltpu.VMEM((N_CHIPS, d, f4), F32)], compiler_params=_params(1))(a, dy)


def ffn_wgrad_down(name, z, dh):
    _, t, f4 = z.shape
    d = dh.shape[1]
    tk = _tile(t, 512)
    nt = t // tk

    def body(z_ref, dh_ref, o_ref, acc):
        s = pl.program_id(0)

        @pl.when(s == 0)
        def _():
            acc[...] = jnp.zeros_like(acc)

        dhb = dh_ref[...].astype(BF16)
        for k in range(N_CHIPS):
            acc[k] += lax.dot_general(z_ref[k], dhb, _TN, preferred_element_type=F32)

        @pl.when(s == nt - 1)
        def _():
            o_ref[...] = acc[...].astype(BF16)

    return pl.pallas_call(
        body, name=name, grid=(nt,),
        in_specs=[pl.BlockSpec((N_CHIPS, tk, f4), lambda s: (0, s, 0)), pl.BlockSpec((tk, d), lambda s: (s, 0))],
        out_specs=pl.BlockSpec((N_CHIPS, f4, d), lambda s: (0, 0, 0)), out_shape=S((N_CHIPS, f4, d), BF16),
        scratch_shapes=[pltpu.VMEM((N_CHIPS, f4, d), F32)], compiler_params=_params(1))(z, dh)


def conv_in_proj(name, a, w):
    t, d = a.shape
    tm = _tile(t, 512)
    return _mm(name, [(a, pl.BlockSpec((tm, d), lambda j, i: (i, 0)))], [(w, pl.BlockSpec((d, d), lambda j, i: (0, j)))],
               [(0, 0, 0)], _NN, (3, t // tm), None,
               [(S((3, t, d), F32), pl.BlockSpec((None, tm, d), lambda j, i: (j, i, 0)))], None, _store)[0]


def conv_in_bwd_input(name, dbcx, w, h, gain, dh_in):
    _, t, d = dbcx.shape
    tm = _tile(t, 512)
    tail_in, tail_out = _norm_bwd_specs(tm, d)

    def body(g_ref, w_ref, *tail):
        acc = jnp.zeros((tm, d), F32)
        for j in range(3):
            acc = acc + lax.dot_general(g_ref[j], w_ref[:, j * d:(j + 1) * d], _NT, preferred_element_type=F32)
        _norm_bwd_tail(acc, *tail)

    return pl.pallas_call(
        body, name=name, grid=(t // tm,),
        in_specs=[pl.BlockSpec((3, tm, d), lambda i: (0, i, 0)), _resident((d, 3 * d), lambda i: (0, 0))] + tail_in,
        out_specs=tail_out, out_shape=[S((t, d), F32), S((1, d), F32)], compiler_params=_params(1))(dbcx, w, h, gain, dh_in)


def linear_nt_norm_bwd(name, dy, w, h, gain, dh_in, parts=()):
    t, n = dy.shape
    k = w.shape[0]
    tm = _tile(t, 512)
    tail_in, tail_out = _norm_bwd_specs(tm, k)
    m = len(parts)

    def body(dy_ref, w_ref, h_ref, g_ref, dhi_ref, *refs):
        if m:
            _ride_along(scatter_ici_copies(refs[:m], refs[m + 2:2 * m + 2], *refs[2 * m + 2:]), (pl.program_id(0),), (t // tm,))
        da = lax.dot_general(dy_ref[...].astype(BF16), w_ref[...], _NT, preferred_element_type=F32)
        _norm_bwd_tail(da, h_ref, g_ref, dhi_ref, *refs[m:m + 2])

    outs = pl.pallas_call(
        body, name=name, grid=(t // tm,),
        in_specs=[pl.BlockSpec((tm, n), lambda i: (i, 0)), _resident((k, n), lambda i: (0, 0))] + tail_in + [ANY] * m,
        out_specs=tail_out + [ANY] * m, out_shape=[S((t, k), F32), S((1, k), F32)] + [S(p.shape, p.dtype) for p in parts],
        scratch_shapes=[pltpu.SemaphoreType.DMA((m, 3)), pltpu.SemaphoreType.DMA((m, 3))] if m else [],
        compiler_params=_params(1))(dy, w, h, gain, dh_in, *parts)
    return outs[0], outs[1], list(outs[2:])


def conv_in_wgrad(name, a, dbcx):
    t, d = a.shape
    tk = _tile(t, 512)
    nt = t // tk

    def body(a_ref, g_ref, o_ref, acc):
        s = pl.program_id(0)

        @pl.when(s == 0)
        def _():
            acc[...] = jnp.zeros_like(acc)

        at = a_ref[...].T
        for j in range(3):
            acc[:, j * d:(j + 1) * d] += jnp.dot(at, g_ref[j], preferred_element_type=F32)

        @pl.when(s == nt - 1)
        def _():
            o_ref[...] = acc[...].astype(BF16)

    return pl.pallas_call(
        body, name=name, grid=(nt,),
        in_specs=[pl.BlockSpec((tk, d), lambda s: (s, 0)), pl.BlockSpec((3, tk, d), lambda s: (0, s, 0))],
        out_specs=pl.BlockSpec((d, 3 * d), lambda s: (0, 0)), out_shape=S((d, 3 * d), BF16),
        scratch_shapes=[pltpu.VMEM((d, 3 * d), F32)], compiler_params=_params(1))(a, dbcx)


def _rstd(x):
    return lax.rsqrt(jnp.mean(x * x, axis=-1, keepdims=True) + RMS_EPS)


def _rms_bwd(x, g, dy):
    r = _rstd(x)
    xhat = x * r
    dgain = jnp.sum(dy * xhat, axis=0, keepdims=True)
    dxh = dy * g
    dx = r * (dxh - xhat * jnp.mean(dxh * xhat, axis=-1, keepdims=True))
    return dx, dgain


def rms_fwd(name, h, g):
    t, d = h.shape
    tr = _tile(t, 512)

    def body(h_ref, g_ref, a_ref):
        x = h_ref[...]
        a_ref[...] = (x * _rstd(x) * g_ref[...]).astype(BF16)

    return pl.pallas_call(
        body, name=name, grid=(t // tr,),
        in_specs=[pl.BlockSpec((tr, d), lambda i: (i, 0)), pl.BlockSpec((1, d), lambda i: (0, 0))],
        out_specs=pl.BlockSpec((tr, d), lambda i: (i, 0)), out_shape=S((t, d), BF16), compiler_params=_params(1))(h, g)


def ffn_down_loss(name, z, wd_all, layer, resid, gain, target):
    _, t, f4 = z.shape
    d = wd_all.shape[2]
    tm = _tile(t, 512)

    def body(z_ref, wd_ref, r_ref, g_ref, t_ref, dh_ref, dg_ref, loss_ref):
        x = r_ref[...]
        for k in range(N_CHIPS):
            x = x + jnp.dot(z_ref[k], wd_ref[k], preferred_element_type=F32)
        g = g_ref[...]
        r = _rstd(x)
        xhat = x * r
        err = xhat * g - t_ref[...]
        dy = err * (1.0 / d)
        dxh = dy * g
        dh_ref[...] = r * (dxh - xhat * jnp.mean(dxh * xhat, axis=-1, keepdims=True))

        @pl.when(pl.program_id(0) == 0)
        def _():
            dg_ref[...] = jnp.zeros_like(dg_ref)
            loss_ref[...] = jnp.zeros_like(loss_ref)

        dg_ref[...] += jnp.sum(dy * xhat, axis=0, keepdims=True)
        per_token = jnp.mean(err * err, axis=-1, keepdims=True)
        loss_ref[...] += 0.5 * jnp.sum(per_token, axis=0, keepdims=True)

    row = pl.BlockSpec((tm, d), lambda i: (i, 0))
    vec = pl.BlockSpec((1, d), lambda i: (0, 0))
    one = pl.BlockSpec((1, 1), lambda i: (0, 0))
    return pl.pallas_call(
        body, name=name, grid=(t // tm,),
        in_specs=[pl.BlockSpec((N_CHIPS, tm, f4), lambda i: (0, i, 0)), _resident((N_CHIPS, f4, d), lambda i: (0, layer, 0)), row, vec, row],
        out_specs=[row, vec, one], out_shape=[S((t, d), F32), S((1, d), F32), S((1, 1), F32)],
        compiler_params=_params(1))(z, wd_all, resid, gain, target)


def mla_in_proj(name, a, w, g_cq, g_ckv, cos, sin):
    t, d = a.shape
    n = w.shape[1]
    ql, kl = g_cq.shape[1], g_ckv.shape[1]
    tr = _tile(t, 512)

    def body(a_ref, w_ref, gq_ref, gk_ref, c_ref, s_ref, p_ref, cq_ref, ckv_ref, kr_ref):
        p_ref[...] = jnp.dot(a_ref[...], w_ref[...], preferred_element_type=F32)
        xq = p_ref[:, 0:ql]
        cq_ref[...] = (xq * _rstd(xq) * gq_ref[...]).astype(BF16)
        xk = p_ref[:, ql:ql + kl]
        ckv_ref[...] = (xk * _rstd(xk) * gk_ref[...]).astype(BF16)
        k1 = p_ref[:, ql + kl:ql + kl + HALF]
        k2 = p_ref[:, ql + kl + HALF:ql + kl + ROPE]
        c = c_ref[...]
        s = s_ref[...]
        kr_ref[:, 0:HALF] = k1 * c - k2 * s
        kr_ref[:, HALF:ROPE] = k1 * s + k2 * c

    def row(w):
        return pl.BlockSpec((tr, w), lambda i: (i, 0))

    def vec(w):
        return pl.BlockSpec((1, w), lambda i: (0, 0))

    return pl.pallas_call(
        body, name=name, grid=(t // tr,),
        in_specs=[row(d), _resident((d, n), lambda i: (0, 0)), vec(ql), vec(kl), row(HALF), row(HALF)],
        out_specs=[row(n), row(ql), row(kl), row(ROPE)],
        out_shape=[S((t, n), F32), S((t, ql), BF16), S((t, kl), BF16), S((t, ROPE), F32)],
        compiler_params=_params(1))(a, w, g_cq, g_ckv, cos, sin)


def mla_mid_bwd(name, proj, g_cq, g_ckv, dcq, dckv, dkr, cos, sin):
    t, n = proj.shape
    ql, kl = g_cq.shape[1], g_ckv.shape[1]
    tr = _tile(t, 512)

    def body(p_ref, gq_ref, gk_ref, dcq_ref, dckv_ref, dkr_ref, c_ref, s_ref, dp_ref, dgq_ref, dgk_ref):
        dxq, dgq = _rms_bwd(p_ref[:, 0:ql], gq_ref[...], dcq_ref[...])
        dp_ref[:, 0:ql] = dxq.astype(BF16)
        dxk, dgk = _rms_bwd(p_ref[:, ql:ql + kl], gk_ref[...], dckv_ref[...])
        dp_ref[:, ql:ql + kl] = dxk.astype(BF16)
        d1 = dkr_ref[:, 0:HALF]
        d2 = dkr_ref[:, HALF:ROPE]
        c = c_ref[...]
        s = s_ref[...]
        dp_ref[:, ql + kl:ql + kl + HALF] = (d1 * c + d2 * s).astype(BF16)
        dp_ref[:, ql + kl + HALF:ql + kl + ROPE] = (d2 * c - d1 * s).astype(BF16)

        @pl.when(pl.program_id(0) == 0)
        def _():
            dgq_ref[...] = jnp.zeros_like(dgq_ref)
            dgk_ref[...] = jnp.zeros_like(dgk_ref)

        dgq_ref[...] += dgq
        dgk_ref[...] += dgk

    def row(w):
        return pl.BlockSpec((tr, w), lambda i: (i, 0))

    def vec(w):
        return pl.BlockSpec((1, w), lambda i: (0, 0))

    return pl.pallas_call(
        body, name=name, grid=(t // tr,),
        in_specs=[row(n), vec(ql), vec(kl), row(ql), row(kl), row(ROPE), row(HALF), row(HALF)],
        out_specs=[row(n), vec(ql), vec(kl)], out_shape=[S((t, n), BF16), S((1, ql), F32), S((1, kl), F32)],
        compiler_params=_params(1))(proj, g_cq, g_ckv, dcq, dckv, dkr, cos, sin)


def qkv_heads(name, q, kv, kr, cos, sin, shards=()):
    t = q.shape[0]
    tr = _tile(t, 256)
    n = len(shards)

    def body(q_ref, kv_ref, kr_ref, c_ref, s_ref, *refs):
        src = refs[:n]
        qo_ref, ko_ref, vo_ref = refs[n:n + 3]
        if n:
            _ride_along(gather_ici_copies(src, refs[n + 3:2 * n + 3], *refs[2 * n + 3:]), (pl.program_id(0),), (t // tr,))
        c = c_ref[...]
        s = s_ref[...]
        krb = kr_ref[...].astype(BF16)
        for h in range(N_HEADS):
            q0 = h * QK
            qo_ref[h, :, 0:NOPE] = q_ref[:, q0:q0 + NOPE].astype(BF16)
            q1 = q_ref[:, q0 + NOPE:q0 + NOPE + HALF]
            q2 = q_ref[:, q0 + NOPE + HALF:q0 + QK]
            qo_ref[h, :, NOPE:NOPE + HALF] = (q1 * c - q2 * s).astype(BF16)
            qo_ref[h, :, NOPE + HALF:QK] = (q1 * s + q2 * c).astype(BF16)
            k0 = h * (NOPE + VDIM)
            ko_ref[h, :, 0:NOPE] = kv_ref[:, k0:k0 + NOPE]
            ko_ref[h, :, NOPE:QK] = krb
            vo_ref[h] = kv_ref[:, k0 + NOPE:k0 + NOPE + VDIM]

    def row(w):
        return pl.BlockSpec((tr, w), lambda i: (i, 0))

    def heads(w):
        return pl.BlockSpec((N_HEADS, tr, w), lambda i: (0, i, 0))

    outs = pl.pallas_call(
        body, name=name, grid=(t // tr,),
        in_specs=[row(N_HEADS * QK), row(N_HEADS * (NOPE + VDIM)), row(ROPE), row(HALF), row(HALF)] + [ANY] * n,
        out_specs=[heads(QK), heads(QK), heads(VDIM)] + [ANY] * n,
        out_shape=[S((N_HEADS, t, QK), BF16), S((N_HEADS, t, QK), BF16), S((N_HEADS, t, VDIM), BF16)]
        + [S((N_CHIPS,) + s.shape, s.dtype) for s in shards],
        scratch_shapes=[pltpu.SemaphoreType.DMA((n, 3)), pltpu.SemaphoreType.DMA((n, 3))] if n else [],
        compiler_params=_params(1))(q, kv, kr, cos, sin, *shards)
    return outs[0], outs[1], outs[2], list(outs[3:])


def qkv_heads_bwd(name, dq_h, dk_h, dv_h, cos, sin, halves=(), targets=(), where=()):
    t = dq_h.shape[1]
    tr = _tile(t, 256)
    n, nt = len(halves), len(targets)

    def body(dq_ref, dk_ref, dv_ref, c_ref, s_ref, *refs):
        q_ref, kv_ref, kr_ref = refs[n:n + 3]
        if n:
            _ride_along(join_copies(refs[:n], refs[n + 3:n + 3 + nt], where, *refs[n + 3 + nt:]), (pl.program_id(0),), (t // tr,))
        c = c_ref[...]
        s = s_ref[...]
        dkr = jnp.zeros((tr, ROPE), F32)
        for h in range(N_HEADS):
            q0 = h * QK
            q_ref[:, q0:q0 + NOPE] = dq_ref[h, :, 0:NOPE].astype(BF16)
            d1 = dq_ref[h, :, NOPE:NOPE + HALF]
            d2 = dq_ref[h, :, NOPE + HALF:QK]
            q_ref[:, q0 + NOPE:q0 + NOPE + HALF] = (d1 * c + d2 * s).astype(BF16)
            q_ref[:, q0 + NOPE + HALF:q0 + QK] = (d2 * c - d1 * s).astype(BF16)
            k0 = h * (NOPE + VDIM)
            kv_ref[:, k0:k0 + NOPE] = dk_ref[h, :, 0:NOPE].astype(BF16)
            kv_ref[:, k0 + NOPE:k0 + NOPE + VDIM] = dv_ref[h].astype(BF16)
            dkr = dkr + dk_ref[h, :, NOPE:QK]
        kr_ref[...] = dkr

    def row(w):
        return pl.BlockSpec((tr, w), lambda i: (i, 0))

    def heads(w):
        return pl.BlockSpec((N_HEADS, tr, w), lambda i: (0, i, 0))

    outs = pl.pallas_call(
        body, name=name, grid=(t // tr,),
        in_specs=[heads(QK), heads(QK), heads(VDIM), row(HALF), row(HALF)] + [ANY] * n,
        out_specs=[row(N_HEADS * QK), row(N_HEADS * (NOPE + VDIM)), row(ROPE)] + [ANY] * nt,
        out_shape=[S((t, N_HEADS * QK), BF16), S((t, N_HEADS * (NOPE + VDIM)), BF16), S((t, ROPE), F32)]
        + [S(tg, F32) for tg in targets],
        scratch_shapes=[pltpu.SemaphoreType.DMA((n,)), pltpu.SemaphoreType.DMA((n,))] if n else [],
        compiler_params=_params(1))(dq_h, dk_h, dv_h, cos, sin, *halves)
    return outs[0], outs[1], outs[2], _fill_own_halves(outs[3:], halves, where)


def _chunk_mask_t(q_start, k_start, bq, bk):
    kc = (k_start + lax.broadcasted_iota(jnp.int32, (bk, bq), 0)) // CHUNK
    qc = (q_start + lax.broadcasted_iota(jnp.int32, (bk, bq), 1)) // CHUNK
    return kc <= qc


def attention_fwd(name, q, k, v, shards=()):
    nh, t, _ = q.shape
    blk = ATT_BLOCK
    nq = t // blk
    n = len(shards)

    def body(q_ref, k_ref, v_ref, *refs):
        src = refs[:n]
        o_ref, lse_ref = refs[n:n + 2]
        dst = refs[n + 2:2 * n + 2]
        m_ref, l_ref, acc_ref, s_buf, p_buf, alpha_buf, bias_ref = refs[2 * n + 2:2 * n + 9]
        i = pl.program_id(1)
        if n:
            send_sems, recv_sems = refs[2 * n + 9:]
            _ride_along(gather_ici_copies(src, dst, send_sems, recv_sems), (pl.program_id(0), i), (nh, nq))

        @pl.when((pl.program_id(0) == 0) & (i == 0))
        def _():
            bias_ref[...] = jnp.where(_chunk_mask_t(0, 0, blk, blk), 0.0, MASK_VALUE)

        m_ref[...] = jnp.full_like(m_ref, MASK_VALUE)
        l_ref[...] = jnp.zeros_like(l_ref)
        acc_ref[...] = jnp.zeros_like(acc_ref)

        def rows(b):
            return pl.ds(pl.multiple_of(b * blk, blk), blk)

        def scores(b, slot):
            s_buf[slot] = lax.dot_general(k_ref[rows(b), :], q_ref[...], _NT, preferred_element_type=F32)

        def softmax(slot, diagonal):
            s = s_buf[slot]
            if diagonal:
                s = s + bias_ref[...]
            m_old = m_ref[...]
            m_new = jnp.maximum(m_old, jnp.max(s, axis=0, keepdims=True))
            p = jnp.exp2((s - m_new) * SCORE_SCALE_LOG2)
            alpha = jnp.exp2((m_old - m_new) * SCORE_SCALE_LOG2)
            l_ref[...] = alpha * l_ref[...] + jnp.sum(p, axis=0, keepdims=True)
            m_ref[...] = m_new
            alpha_buf[slot] = alpha
            p_buf[slot] = p.astype(BF16)

        def values(b, slot):
            pv = lax.dot_general(v_ref[rows(b), :], p_buf[slot], _TN, preferred_element_type=F32)
            acc_ref[...] = alpha_buf[slot] * acc_ref[...] + pv

        def step(t, slot):
            values(t - 2, slot)
            softmax(1 - slot, False)
            scores(t, slot)

        scores(0, 0)

        @pl.when(i == 0)
        def _():
            softmax(0, True)
            values(0, 0)

        @pl.when(i > 0)
        def _():
            scores(1, 1)
            softmax(0, False)
            steady = i - 1

            def pair(u, carry):
                step(2 + 2 * u, 0)
                step(3 + 2 * u, 1)
                return carry

            lax.fori_loop(0, steady // 2, pair, 0)

            @pl.when(steady % 2 == 1)
            def _():
                step(i, 0)

            last = i % 2
            softmax(last, True)
            values(i - 1, 1 - last)
            values(i, last)

        l = l_ref[...]
        o_ref[...] = (acc_ref[...] / l).T
        lse_ref[...] = m_ref[...] * SCORE_SCALE + jnp.log(l)

    outs = pl.pallas_call(
        body, name=name, grid=(nh, nq),
        in_specs=[pl.BlockSpec((None, blk, QK), lambda h, i: (h, i, 0)), pl.BlockSpec((None, t, QK), lambda h, i: (h, 0, 0)),
                  pl.BlockSpec((None, t, VDIM), lambda h, i: (h, 0, 0))] + [ANY] * n,
        out_specs=[pl.BlockSpec((blk, VDIM), lambda h, i: (i, h)),
                   pl.BlockSpec((None, None, 1, blk), lambda h, i: (h, i, 0, 0))] + [ANY] * n,
        out_shape=[S((t, nh * VDIM), F32), S((nh, nq, 1, blk), F32)] + [S((N_CHIPS,) + s.shape, s.dtype) for s in shards],
        scratch_shapes=[pltpu.VMEM((1, blk), F32), pltpu.VMEM((1, blk), F32), pltpu.VMEM((VDIM, blk), F32),
                        pltpu.VMEM((2, blk, blk), F32), pltpu.VMEM((2, blk, blk), BF16), pltpu.VMEM((2, 1, blk), F32),
                        pltpu.VMEM((blk, blk), F32)]
        + ([pltpu.SemaphoreType.DMA((n, 3)), pltpu.SemaphoreType.DMA((n, 3))] if n else []),
        compiler_params=_params(2))(q, k, v, *shards)
    return outs[0], outs[1], list(outs[2:])


def attention_out_bwd(name, dh, w_o, o):
    t, d = dh.shape
    n = w_o.shape[0]
    blk = ATT_BLOCK

    def body(dh_ref, w_ref, o_ref, do_ref, d_ref):
        do_ref[...] = lax.dot_general(dh_ref[...].astype(BF16), w_ref[...], _NT, preferred_element_type=F32)
        for h in range(N_HEADS):
            cols = slice(h * VDIM, (h + 1) * VDIM)
            d_ref[h] = jnp.sum((do_ref[:, cols] * o_ref[:, cols]).T, axis=0, keepdims=True)

    tile = pl.BlockSpec((blk, n), lambda i: (i, 0))
    return pl.pallas_call(
        body, name=name, grid=(t // blk,),
        in_specs=[pl.BlockSpec((blk, d), lambda i: (i, 0)), _resident((n, d), lambda i: (0, 0)), tile],
        out_specs=[tile, pl.BlockSpec((N_HEADS, None, 1, blk), lambda i: (0, i, 0, 0))],
        out_shape=[S((t, n), F32), S((N_HEADS, t // blk, 1, blk), F32)], compiler_params=_params(1))(dh, w_o, o)


def attention_bwd(name, q, k, v, do, lse, delta, parts=()):
    nh, t, _ = q.shape
    blk = ATT_BLOCK
    nq = t // blk
    n_pairs = nq * (nq + 1) // 2
    n = len(parts)
    scale = SCORE_SCALE

    def body(q_ref, k_ref, v_ref, do_ref, lse_ref, dl_ref, *refs):
        src = refs[:n]
        dq_ref, dk_ref, dv_ref = refs[n:n + 3]
        dst = refs[n + 3:2 * n + 3]
        s_buf, dp_buf, p_buf, ds_buf, bias_ref = refs[2 * n + 3:2 * n + 8]
        if n:
            send_sems, recv_sems = refs[2 * n + 8:]
            _ride_along(scatter_ici_copies(src, dst, send_sems, recv_sems), (pl.program_id(0),), (nh,))

        @pl.when(pl.program_id(0) == 0)
        def _():
            bias_ref[...] = jnp.where(_chunk_mask_t(0, 0, blk, blk), 0.0, MASK_VALUE)

        dq_ref[...] = jnp.zeros_like(dq_ref)
        dk_ref[...] = jnp.zeros_like(dk_ref)
        dv_ref[...] = jnp.zeros_like(dv_ref)

        def rows(x):
            return pl.ds(pl.multiple_of(x * blk, blk), blk)

        def after(jb):
            j, b = jb
            wrap = b == nq - 1 - j
            return jnp.where(wrap, j + 1, j), jnp.where(wrap, 0, b + 1)

        def products(jb, slot):
            j, b = jb
            s_buf[slot] = lax.dot_general(k_ref[rows(j), :], q_ref[rows(j + b), :], _NT, preferred_element_type=F32)
            dp_buf[slot] = lax.dot_general(v_ref[rows(j), :], do_ref[rows(j + b), :].astype(BF16), _NT, preferred_element_type=F32)

        def softmax_bwd(jb, slot):
            j, b = jb
            s = s_buf[slot] + bias_ref[...] * (b == 0).astype(F32)
            p = jnp.exp2(s * SCORE_SCALE_LOG2 - lse_ref[j + b] * LOG2_E)
            p_buf[slot] = p.astype(BF16)
            ds_buf[slot] = (p * (dp_buf[slot] - dl_ref[j + b]) * scale).astype(BF16)

        def gradients(jb, slot):
            j, b = jb
            dv_ref[rows(j), :] += jnp.dot(p_buf[slot], do_ref[rows(j + b), :].astype(BF16), preferred_element_type=F32)
            dk_ref[rows(j), :] += jnp.dot(ds_buf[slot], q_ref[rows(j + b), :], preferred_element_type=F32)
            dq_ref[rows(j + b), :] += lax.dot_general(ds_buf[slot], k_ref[rows(j), :], _TN, preferred_element_type=F32)

        def step(state, slot):
            third, second, first = state
            gradients(third, slot)
            softmax_bwd(second, 1 - slot)
            products(first, slot)
            return second, first, after(first)

        zero = jnp.int32(0)
        pair0 = (zero, zero)
        products(pair0, 0)
        if n_pairs == 1:
            softmax_bwd(pair0, 0)
            gradients(pair0, 0)
        else:
            pair1 = after(pair0)
            products(pair1, 1)
            softmax_bwd(pair0, 0)
            steady = n_pairs - 2
            state = lax.fori_loop(0, steady // 2, lambda u, st: step(step(st, 0), 1), (pair0, pair1, after(pair1)))
            if steady % 2:
                state = step(state, 0)
            before_last, last_pair, _ = state
            last = (n_pairs - 1) % 2
            softmax_bwd(last_pair, last)
            gradients(before_last, 1 - last)
            gradients(last_pair, last)

    head = lambda w: pl.BlockSpec((None, t, w), lambda h: (h, 0, 0))
    stats = pl.BlockSpec((None, nq, 1, blk), lambda h: (h, 0, 0, 0))
    outs = pl.pallas_call(
        body, name=name, grid=(nh,),
        in_specs=[head(QK), head(QK), head(VDIM), pl.BlockSpec((t, VDIM), lambda h: (0, h)), stats, stats] + [ANY] * n,
        out_specs=[head(QK), head(QK), head(VDIM)] + [ANY] * n,
        out_shape=[S((nh, t, QK), F32), S((nh, t, QK), F32), S((nh, t, VDIM), F32)] + [S(p.shape, p.dtype) for p in parts],
        scratch_shapes=[pltpu.VMEM((2, blk, blk), F32), pltpu.VMEM((2, blk, blk), F32), pltpu.VMEM((2, blk, blk), BF16),
                        pltpu.VMEM((2, blk, blk), BF16), pltpu.VMEM((blk, blk), F32)]
        + ([pltpu.SemaphoreType.DMA((n, 3)), pltpu.SemaphoreType.DMA((n, 3))] if n else []),
        compiler_params=_params(1, VMEM_LIMIT_WHOLE_HEAD))(q, k, v, do, lse, delta, *parts)
    return outs[0], outs[1], outs[2], list(outs[3:])


def _shift_down(u, s):
    rows = lax.broadcasted_iota(jnp.int32, u.shape, 0)
    return jnp.where(rows >= s, pltpu.roll(u, s, 0), 0.0)


def _shift_up(u, s):
    n = u.shape[0]
    rows = lax.broadcasted_iota(jnp.int32, u.shape, 0)
    return jnp.where(rows < n - s, pltpu.roll(u, n - s, 0), 0.0)


def _conv_specs(t, d, lanes):
    slab = lambda part: pl.BlockSpec((None, t, lanes), lambda j, part=part: (part, 0, j))
    return slab, pl.BlockSpec((3, lanes), lambda j: (0, j)), pl.BlockSpec((t, lanes), lambda j: (0, j))


def conv_fwd(name, bcx, w):
    _, t, d = bcx.shape
    lanes = _tile(d, 128, 128)
    slab, w_spec, col = _conv_specs(t, d, lanes)

    def body(b_ref, c_ref, x_ref, w_ref, y_ref):
        u = c_ref[...] * x_ref[...]
        uc = w_ref[0:1, :] * _shift_down(u, 2) + w_ref[1:2, :] * _shift_down(u, 1) + w_ref[2:3, :] * u
        y_ref[...] = (b_ref[...] * uc).astype(BF16)

    return pl.pallas_call(
        body, name=name, grid=(d // lanes,), in_specs=[slab(0), slab(1), slab(2), w_spec], out_specs=col,
        out_shape=S((t, d), BF16), compiler_params=_params(1))(bcx, bcx, bcx, w)


def conv_bwd(name, bcx, w, dy):
    _, t, d = bcx.shape
    lanes = _tile(d, 128, 128)
    slab, w_spec, col = _conv_specs(t, d, lanes)

    def body(b_ref, c_ref, x_ref, w_ref, dy_ref, d_ref, dw_ref):
        c = c_ref[...]
        x = x_ref[...]
        dyv = dy_ref[...]
        u = c * x
        u1 = _shift_down(u, 1)
        u2 = _shift_down(u, 2)
        w0, w1, w2 = w_ref[0:1, :], w_ref[1:2, :], w_ref[2:3, :]
        d_ref[0] = (dyv * (w0 * u2 + w1 * u1 + w2 * u)).astype(BF16)
        duc = dyv * b_ref[...]
        dw_ref[0:1, :] = jnp.sum(duc * u2, axis=0, keepdims=True)
        dw_ref[1:2, :] = jnp.sum(duc * u1, axis=0, keepdims=True)
        dw_ref[2:3, :] = jnp.sum(duc * u, axis=0, keepdims=True)
        du = w2 * duc + w1 * _shift_up(duc, 1) + w0 * _shift_up(duc, 2)
        d_ref[1] = (du * x).astype(BF16)
        d_ref[2] = (du * c).astype(BF16)

    return pl.pallas_call(
        body, name=name, grid=(d // lanes,), in_specs=[slab(0), slab(1), slab(2), w_spec, col],
        out_specs=[pl.BlockSpec((3, t, lanes), lambda j: (0, 0, j)), w_spec], out_shape=[S((3, t, d), BF16), S((3, d), F32)],
        compiler_params=_params(1))(bcx, bcx, bcx, w, dy)


def _adamw_update(w, g, m, v):
    m_new = ADAM_B1 * m + (1.0 - ADAM_B1) * g
    v_new = ADAM_B2 * v + (1.0 - ADAM_B2) * (g * g)
    m_hat = m_new / (1.0 - ADAM_B1 ** ADAM_STEP)
    v_hat = v_new / (1.0 - ADAM_B2 ** ADAM_STEP)
    return -ADAM_LR * (m_hat / (jnp.sqrt(v_hat) + ADAM_EPS) + ADAM_WD * w), m_new, v_new


def adamw(name, w, g, m, v):
    r, c = w.shape
    tr = _tile(r, 512)

    def body(w_ref, g_ref, m_ref, v_ref, d_ref, mo_ref, vo_ref):
        d_ref[...], mo_ref[...], vo_ref[...] = _adamw_update(w_ref[...], g_ref[...], m_ref[...], v_ref[...])

    blk = pl.BlockSpec((tr, c), lambda i: (i, 0))
    return pl.pallas_call(
        body, name=name, grid=(r // tr,), in_specs=[blk] * 4, out_specs=[blk] * 3, out_shape=[S((r, c), F32)] * 3,
        compiler_params=_params(1))(w, g, m, v)


def adamw_swapped(name, wt, g, mt, vt):
    nl, c, r = wt.shape
    tr = _tile(r, 512, 128)
    nr = r // tr

    def body(w_ref, g_ref, m_ref, v_ref, go_ref, d_ref, mo_ref, vo_ref):
        gt = g_ref[...].T
        go_ref[...] = gt
        d_ref[...], mo_ref[...], vo_ref[...] = _adamw_update(w_ref[...], gt, m_ref[...], v_ref[...])

    swapped = pl.BlockSpec((None, c, tr), lambda l, i: (l, 0, i))
    return pl.pallas_call(
        body, name=name, grid=(nl, nr),
        in_specs=[swapped, pl.BlockSpec((tr, c), lambda l, i: (l * nr + i, 0)), swapped, swapped],
        out_specs=[swapped] * 4, out_shape=[S((nl, c, r), F32)] * 4, compiler_params=_params(2))(wt, g, mt, vt)


def _place():
    x, y, c = lax.axis_index("x"), lax.axis_index("y"), lax.axis_index("c")
    other_chips = [(1 - x, y), (x, 1 - y), (1 - x, 1 - y)]
    return x, y, c, other_chips


def _half(c, rows):
    return pl.ds(pl.multiple_of(c * (rows // 2), 16), rows // 2)


def gather_weight_shards(shards):
    n = len(shards)

    def body(*refs):
        src = refs[:n]
        dst = refs[n:2 * n]
        send_sems, recv_sems = refs[2 * n:]
        x, y, c, chips = _place()
        me = 2 * x + y
        sibling = (x, y, 1 - c)

        def copy(i, slot, half_of, sem, to, from_input=False):
            rows = _half(half_of, src[i].shape[0])
            return pltpu.make_async_remote_copy(
                src_ref=src[i].at[rows] if from_input else dst[i].at[slot, rows], dst_ref=dst[i].at[slot, rows],
                send_sem=send_sems.at[i, sem], recv_sem=recv_sems.at[i, sem], device_id=to, device_id_type=MESH)

        sent = []
        for i in range(n):
            for j, chip in enumerate(chips):
                sent.append(copy(i, me, c, j, (*chip, c), from_input=True))
                sent[-1].start()
        for i in range(n):
            for j, (px, py) in enumerate(chips):
                copy(i, 2 * px + py, c, j, sibling).wait_recv()
                sent.append(copy(i, 2 * px + py, c, 3 + j, sibling))
                sent[-1].start()
        for i in range(n):
            for j, (px, py) in enumerate(chips):
                copy(i, 2 * px + py, 1 - c, 3 + j, sibling).wait_recv()
        for cp in sent:
            cp.wait_send()

    outs = pl.pallas_call(
        body, name="gather_weight_shards", in_specs=[ANY] * n, out_specs=[ANY] * n,
        out_shape=[S((N_CHIPS,) + s.shape, s.dtype) for s in shards],
        scratch_shapes=[pltpu.SemaphoreType.DMA((n, 6)), pltpu.SemaphoreType.DMA((n, 6))],
    )(*shards)
    return _fill_own_slot(outs, [s[None] for s in shards])


def gather_ici_copies(src, dst, send_sems, recv_sems):
    x, y, c, chips = _place()
    me = 2 * x + y
    pairs = []
    for i in range(len(src)):
        rows = _half(c, src[i].shape[0])
        for j, (px, py) in enumerate(chips):
            def copy(slot):
                return pltpu.make_async_remote_copy(
                    src_ref=src[i].at[rows], dst_ref=dst[i].at[slot, rows], send_sem=send_sems.at[i, j],
                    recv_sem=recv_sems.at[i, j], device_id=(px, py, c), device_id_type=MESH)
            pairs.append((copy(me), copy(2 * px + py)))
    return pairs


def scatter_ici_copies(src, dst, send_sems, recv_sems):
    x, y, c, chips = _place()
    me = 2 * x + y
    pairs = []
    for i in range(len(src)):
        for j, (px, py) in enumerate(chips):
            def copy(from_slot, to_slot):
                return pltpu.make_async_remote_copy(
                    src_ref=src[i].at[from_slot], dst_ref=dst[i].at[to_slot], send_sem=send_sems.at[i, j],
                    recv_sem=recv_sems.at[i, j], device_id=(px, py, c), device_id_type=MESH)
            pairs.append((copy(2 * px + py, me), copy(me, 2 * px + py)))
    return pairs


def _ride_along(pairs, grid_ids, grid_sizes):
    first = grid_ids[0] == 0
    last = grid_ids[0] == grid_sizes[0] - 1
    for g, size in zip(grid_ids[1:], grid_sizes[1:]):
        first = first & (g == 0)
        last = last & (g == size - 1)

    @pl.when(first)
    def _():
        for outgoing, _ in pairs:
            outgoing.start()

    @pl.when(last)
    def _():
        for _, incoming in pairs:
            incoming.wait_recv()
        for outgoing, _ in pairs:
            outgoing.wait_send()


def _fill_own_slot(gathered, own):
    me = 2 * lax.axis_index("x") + lax.axis_index("y")
    return [lax.dynamic_update_slice(g, o, (me,) + (0,) * (g.ndim - 1)) for g, o in zip(gathered, own)]


def forward_copies(src, dst, send_sems, recv_sems):
    x, y, c, chips = _place()
    pairs = []
    for i in range(len(src)):
        for j, (px, py) in enumerate(chips):
            def copy(half_of):
                rows = _half(half_of, src[i].shape[1])
                return pltpu.make_async_remote_copy(
                    src_ref=src[i].at[2 * px + py, rows], dst_ref=dst[i].at[2 * px + py, rows], send_sem=send_sems.at[i, j],
                    recv_sem=recv_sems.at[i, j], device_id=(x, y, 1 - c), device_id_type=MESH)
            pairs.append((copy(c), copy(1 - c)))
    return pairs


def attention_out_proj(name, attn, w_o, resid, next_gain, arriving):
    t, kdim = attn.shape
    n = w_o.shape[1]
    tm = _tile(t, 512)
    m = len(arriving)

    def body(x_ref, w_ref, r_ref, g_ref, *refs):
        h_ref, a_ref = refs[m:m + 2]
        _ride_along(forward_copies(refs[:m], refs[m + 2:2 * m + 2], *refs[2 * m + 2:]), (pl.program_id(0),), (t // tm,))
        y = r_ref[...] + jnp.dot(x_ref[...].astype(BF16), w_ref[...], preferred_element_type=F32)
        h_ref[...] = y
        a_ref[...] = (y * _rstd(y) * g_ref[...]).astype(BF16)

    row = pl.BlockSpec((tm, n), lambda i: (i, 0))
    outs = pl.pallas_call(
        body, name=name, grid=(t // tm,),
        in_specs=[pl.BlockSpec((tm, kdim), lambda i: (i, 0)), _resident((kdim, n), lambda i: (0, 0)), row,
                  pl.BlockSpec((1, n), lambda i: (0, 0))] + [ANY] * m,
        out_specs=[row, row] + [ANY] * m, out_shape=[S((t, n), F32), S((t, n), BF16)] + [S(g.shape, g.dtype) for g in arriving],
        input_output_aliases={4 + i: 2 + i for i in range(m)},
        scratch_shapes=[pltpu.SemaphoreType.DMA((m, 3)), pltpu.SemaphoreType.DMA((m, 3))],
        compiler_params=_params(1))(attn, w_o, resid, next_gain, *arriving)
    return outs[0], outs[1], list(outs[2:])


def sibling_swap_halves(name, grads):
    n = len(grads)

    def body(*refs):
        src = refs[:n]
        dst = refs[n:2 * n]
        send_sems, recv_sems = refs[2 * n:]
        x, y, c, _ = _place()
        copies = [pltpu.make_async_remote_copy(
            src_ref=src[i].at[:, _half(1 - c, src[i].shape[1]), :], dst_ref=dst[i], send_sem=send_sems.at[i],
            recv_sem=recv_sems.at[i], device_id=(x, y, 1 - c), device_id_type=MESH) for i in range(n)]
        for cp in copies:
            cp.start()
        for cp in copies:
            cp.wait()

    return pl.pallas_call(
        body, name=name, in_specs=[ANY] * n, out_specs=[ANY] * n,
        out_shape=[S((g.shape[0], g.shape[1] // 2, g.shape[2]), g.dtype) for g in grads],
        scratch_shapes=[pltpu.SemaphoreType.DMA((n,)), pltpu.SemaphoreType.DMA((n,))],
    )(*grads)


def add_halves(name, g, rx):
    _, r, cdim = g.shape
    r2 = r // 2
    tr = _tile(r2, 512, 16)
    nb = r2 // tr

    def body(lo_ref, hi_ref, rx_ref, o_ref):
        mine = jnp.where(lax.axis_index("c") == 0, lo_ref[...], hi_ref[...])
        o_ref[...] = (mine.astype(F32) + rx_ref[...].astype(F32)).astype(BF16)

    half = pl.BlockSpec((None, tr, cdim), lambda k, i: (k, i, 0))
    return pl.pallas_call(
        body, name=name, grid=(N_CHIPS, nb),
        in_specs=[half, pl.BlockSpec((None, tr, cdim), lambda k, i: (k, nb + i, 0)), half],
        out_specs=half, out_shape=S((N_CHIPS, r2, cdim), BF16), compiler_params=_params(2))(g, g, rx)


def _own_slots(parts):
    me = 2 * lax.axis_index("x") + lax.axis_index("y")
    return [lax.dynamic_slice(p, (me, 0, 0), (1,) + p.shape[1:]) for p in parts]


def sum_chips(name, parts):
    _, r2, cdim = parts.shape
    tr = _tile(r2, 512, 16)

    def body(p_ref, o_ref):
        acc = p_ref[0].astype(F32)
        for k in range(1, N_CHIPS):
            acc = acc + p_ref[k].astype(F32)
        o_ref[...] = acc

    return pl.pallas_call(
        body, name=name, grid=(r2 // tr,), in_specs=[pl.BlockSpec((N_CHIPS, tr, cdim), lambda i: (0, i, 0))],
        out_specs=pl.BlockSpec((tr, cdim), lambda i: (i, 0)), out_shape=S((r2, cdim), F32), compiler_params=_params(1))(parts)


def join_copies(src, dst, where, send_sems, recv_sems):
    x, y, c, _ = _place()
    pairs = []
    for i in range(len(src)):
        def copy(half_of):
            r2 = src[i].shape[0]
            rows = pl.ds(pl.multiple_of(where[i][1] + half_of * r2, 8), r2)
            return pltpu.make_async_remote_copy(
                src_ref=src[i], dst_ref=dst[where[i][0]].at[rows], send_sem=send_sems.at[i],
                recv_sem=recv_sems.at[i], device_id=(x, y, 1 - c), device_id_type=MESH)
        pairs.append((copy(c), copy(1 - c)))
    return pairs


def _fill_own_halves(targets, halves, where):
    targets = list(targets)
    c = lax.axis_index("c")
    for h, (tgt, first) in zip(halves, where):
        targets[tgt] = lax.dynamic_update_slice(targets[tgt], h, (first + c * h.shape[0], 0))
    return targets


def sibling_join_halves(name, halves, targets, where):
    n = len(halves)

    def body(*refs):
        pairs = join_copies(refs[:n], refs[n:n + len(targets)], where, *refs[n + len(targets):])
        for outgoing, _ in pairs:
            outgoing.start()
        for _, incoming in pairs:
            incoming.wait_recv()
        for outgoing, _ in pairs:
            outgoing.wait_send()

    outs = pl.pallas_call(
        body, name=name, in_specs=[ANY] * n, out_specs=[ANY] * len(targets), out_shape=[S(tg, F32) for tg in targets],
        scratch_shapes=[pltpu.SemaphoreType.DMA((n,)), pltpu.SemaphoreType.DMA((n,))],
    )(*halves)
    return _fill_own_halves(outs, halves, where)


def all_reduce_small(name, packed):
    rows, width = packed.shape

    def body(x_ref, o_ref, gathered, send_sems, recv_sems):
        x, y, c, _ = _place()
        me = 4 * x + 2 * y + c
        gathered[me] = x_ref[...]
        flips = [(fx, fy, fc) for fx in (0, 1) for fy in (0, 1) for fc in (0, 1)][1:]

        def copy(r, slot, to):
            return pltpu.make_async_remote_copy(
                src_ref=x_ref, dst_ref=gathered.at[slot], send_sem=send_sems.at[r], recv_sem=recv_sems.at[r],
                device_id=to, device_id_type=MESH)

        def peer(f):
            return (x ^ f[0], y ^ f[1], c ^ f[2])

        sent = [copy(r, me, peer(f)) for r, f in enumerate(flips)]
        for cp in sent:
            cp.start()
        for r, f in enumerate(flips):
            px, py, pc = peer(f)
            copy(r, 4 * px + 2 * py + pc, peer(f)).wait_recv()
        for cp in sent:
            cp.wait_send()
        acc = gathered[0]
        for k in range(1, N_DEV):
            acc = acc + gathered[k]
        o_ref[...] = acc

    vmem = pl.BlockSpec(memory_space=pltpu.VMEM)
    return pl.pallas_call(
        body, name=name, in_specs=[vmem], out_specs=vmem, out_shape=S((rows, width), F32),
        scratch_shapes=[pltpu.VMEM((N_DEV, rows, width), F32), pltpu.SemaphoreType.DMA((N_DEV - 1,)),
                        pltpu.SemaphoreType.DMA((N_DEV - 1,))],
    )(packed)


def _rope_tables(positions):
    inv_freq = 1.0 / (ROPE_THETA ** (jnp.arange(0, ROPE, 2, dtype=F32) / ROPE))
    ang = positions.astype(F32)[:, None] * inv_freq
    return jnp.cos(ang), jnp.sin(ang)


def _unstack_cols(w):
    k4, k, n4 = w.shape
    return jnp.transpose(w, (1, 0, 2)).reshape(k, k4 * n4)


def _stack_cols(w):
    k, n = w.shape
    return jnp.transpose(w.reshape(k, N_CHIPS, n // N_CHIPS), (1, 0, 2))


def kernel(x, positions, mla_norm, mla_w_in, mla_g_cq, mla_g_ckv, mla_w_uq, mla_w_ukv, mla_w_o, conv_norm, conv_w_in, conv_w, conv_w_out, ffn_norm, ffn_w_gate, ffn_w_up, ffn_w_down, final_norm, loss_target, m_mla_norm, m_mla_w_in, m_mla_g_cq, m_mla_g_ckv, m_mla_w_uq, m_mla_w_ukv, m_mla_w_o, m_conv_norm, m_conv_w_in, m_conv_w, m_conv_w_out, m_ffn_norm, m_ffn_w_gate, m_ffn_w_up, m_ffn_w_down, m_final_norm, v_mla_norm, v_mla_w_in, v_mla_g_cq, v_mla_g_ckv, v_mla_w_uq, v_mla_w_ukv, v_mla_w_o, v_conv_norm, v_conv_w_in, v_conv_w, v_conv_w_out, v_ffn_norm, v_ffn_w_gate, v_ffn_w_up, v_ffn_w_down, v_final_norm):
    weights = dict(mla_norm=mla_norm, mla_w_in=mla_w_in, mla_g_cq=mla_g_cq, mla_g_ckv=mla_g_ckv, mla_w_uq=mla_w_uq,
                   mla_w_ukv=mla_w_ukv, mla_w_o=mla_w_o, conv_norm=conv_norm, conv_w_in=conv_w_in, conv_w=conv_w,
                   conv_w_out=conv_w_out, ffn_norm=ffn_norm, ffn_w_gate=ffn_w_gate, ffn_w_up=ffn_w_up,
                   ffn_w_down=ffn_w_down, final_norm=final_norm)
    m_in = dict(mla_norm=m_mla_norm, mla_w_in=m_mla_w_in, mla_g_cq=m_mla_g_cq, mla_g_ckv=m_mla_g_ckv, mla_w_uq=m_mla_w_uq,
                mla_w_ukv=m_mla_w_ukv, mla_w_o=m_mla_w_o, conv_norm=m_conv_norm, conv_w_in=m_conv_w_in, conv_w=m_conv_w,
                conv_w_out=m_conv_w_out, ffn_norm=m_ffn_norm, ffn_w_gate=m_ffn_w_gate, ffn_w_up=m_ffn_w_up,
                ffn_w_down=m_ffn_w_down, final_norm=m_final_norm)
    v_in = dict(mla_norm=v_mla_norm, mla_w_in=v_mla_w_in, mla_g_cq=v_mla_g_cq, mla_g_ckv=v_mla_g_ckv, mla_w_uq=v_mla_w_uq,
                mla_w_ukv=v_mla_w_ukv, mla_w_o=v_mla_w_o, conv_norm=v_conv_norm, conv_w_in=v_conv_w_in, conv_w=v_conv_w,
                conv_w_out=v_conv_w_out, ffn_norm=v_ffn_norm, ffn_w_gate=v_ffn_w_gate, ffn_w_up=v_ffn_w_up,
                ffn_w_down=v_ffn_w_down, final_norm=v_final_norm)
    big = ["mla_w_in", "mla_w_uq", "mla_w_ukv", "mla_w_o", "conv_w_in", "conv_w_out", "ffn_w_gate", "ffn_w_up", "ffn_w_down"]
    order = list(weights)

    t, d = x.shape[1], x.shape[2]
    h0 = x.reshape(t, d)
    target = loss_target.reshape(t, d)
    cos, sin = _rope_tables(positions.reshape(t))

    def rows2d(a):
        return a.reshape(-1, a.shape[-1])

    first, later = big[:4], big[4:]
    shards = {n: rows2d(weights[n]).astype(BF16) for n in big}
    gathered = dict(zip(first, gather_weight_shards([shards[n] for n in first])))
    w_in = gathered["mla_w_in"].reshape(-1, gathered["mla_w_in"].shape[-1])
    w_uq = _unstack_cols(gathered["mla_w_uq"])
    w_ukv = _unstack_cols(gathered["mla_w_ukv"])
    w_o = gathered["mla_w_o"].reshape(-1, d)

    chip = 2 * lax.axis_index("x") + lax.axis_index("y")
    core = lax.axis_index("c")
    d4 = d // N_CHIPS
    first_core = (core == 0).astype(F32)

    def place_shard(shard):
        full = jnp.zeros((shard.shape[0], d), F32)
        return lax.dynamic_update_slice(full, shard * first_core, (0, chip * d4))

    def pack_rows(rows):
        idx = lax.broadcasted_iota(jnp.int32, (SMALL_ROWS, d), 0)
        out = jnp.zeros((SMALL_ROWS, d), F32)
        for r, row in enumerate(rows):
            out = out + jnp.where(idx == r, row, 0.0)
        return out

    cw = place_shard(conv_w.reshape(3, d4))
    pre = all_reduce_small("all_gather_conv_small", pack_rows([place_shard(conv_norm.reshape(1, d4)), cw[0:1], cw[1:2], cw[2:3]]))
    conv_norm_full = pre[0:1]
    conv_w_full = pre[1:4]

    a0 = rms_fwd("mla_norm_fwd", h0, mla_norm)
    proj, cq, ckv, kr = mla_in_proj("mla_in_proj", a0, w_in, mla_g_cq, mla_g_ckv, cos, sin)
    q = linear("mla_q_up", cq, w_uq, F32)
    kv = linear("mla_kv_up", ckv, w_ukv, BF16)
    qh, kh, vh, conv_arriving = qkv_heads("qkv_heads", q, kv, kr, cos, sin, [shards[n] for n in later[:2]])
    attn, lse, ffn_arriving = attention_fwd("attention_fwd", qh, kh, vh, [shards[n] for n in later[2:]])
    h1, a1, handed = attention_out_proj("mla_out_proj", attn, w_o, h0, ffn_norm[0:1], conv_arriving + ffn_arriving)
    gathered.update(zip(later, _fill_own_slot(handed, [shards[n][None] for n in later])))
    cw_in = _unstack_cols(gathered["conv_w_in"])
    cw_out = gathered["conv_w_out"].reshape(-1, d)
    wg_all, wu_all, wd_all = gathered["ffn_w_gate"], gathered["ffn_w_up"], gathered["ffn_w_down"]

    def ffn_forward(tag, h, a, layer, next_gain):
        g, u, z = ffn_up(f"ffn{tag}_up", a, wg_all, wu_all, layer)
        return g, u, z, ffn_down(f"ffn{tag}_down", z, wd_all, layer, h, next_gain)

    g0, u0, z0, (h2, a2) = ffn_forward(0, h1, a1, 0, conv_norm_full)
    bcx = conv_in_proj("conv_in_proj", a2, cw_in)
    yc = conv_fwd("conv_fwd", bcx, conv_w_full)
    h3, a3 = linear("conv_out_proj", yc, cw_out, F32, resid=h2, next_gain=ffn_norm[1:2])
    g1, u1, z1 = ffn_up("ffn1_up", a3, wg_all, wu_all, 1)
    dh4, d_final_norm, loss_local = ffn_down_loss("ffn1_down_loss", z1, wd_all, 1, h3, final_norm.reshape(1, d), target)

    def ffn_backward(tag, dh, h, layer, a, g, u, z):
        dg, du = ffn_bwd_hidden(f"ffn{tag}_bwd_hidden", dh, wd_all, layer, g, u)
        d_wd = ffn_wgrad_down(f"ffn{tag}_wgrad_down", z, dh)
        dh_prev, d_norm = ffn_bwd_input(f"ffn{tag}_bwd_input", dg, du, wg_all, wu_all, layer, h, ffn_norm[layer:layer + 1], dh)
        d_wg = ffn_wgrad_up(f"ffn{tag}_wgrad_gate", a, dg)
        d_wu = ffn_wgrad_up(f"ffn{tag}_wgrad_up", a, du)
        return dh_prev, d_norm, [d_wg, d_wu, d_wd]

    def reduce_to_pair_sums(tag, local):
        from_sibling = sibling_swap_halves(f"sibling_swap_{tag}", local)
        return [add_halves(f"pair_sum_{tag}{i}", g, r) for i, (g, r) in enumerate(zip(local, from_sibling))]

    def sum_from_chips(tag, pair_sums, arrived):
        from_chips = _fill_own_slot(arrived, _own_slots(pair_sums))
        return [sum_chips(f"chip_sum_{tag}{i}", p) for i, p in enumerate(from_chips)]

    def shard_shape(n):
        return rows2d(weights[n]).shape

    dh3, d_ffn_norm1, ffn1_grads = ffn_backward(1, dh4, h3, 1, a3, g1, u1, z1)

    dyc = linear_nt("conv_out_bwd_input", dh3, cw_out, F32)
    d_cw_out = wgrad("conv_out_wgrad", yc, dh3)
    dbcx, d_conv_w = conv_bwd("conv_bwd", bcx, conv_w_full, dyc)
    dh2, d_conv_norm = conv_in_bwd_input("conv_in_bwd_input", dbcx, cw_in, h2, conv_norm_full, dh3)
    d_cw_in = conv_in_wgrad("conv_in_wgrad", a2, dbcx)

    dh1, d_ffn_norm0, ffn0_grads = ffn_backward(0, dh2, h1, 0, a1, g0, u0, z0)

    d_attn, delta = attention_out_bwd("mla_out_bwd_input", dh1, w_o, attn)
    d_w_o = wgrad("mla_out_wgrad", attn, dh1)
    rest_pairs = reduce_to_pair_sums("rest", [_stack_cols(d_cw_in), d_cw_out.reshape(N_CHIPS, -1, d)] + ffn1_grads + ffn0_grads
                                     + [d_w_o.reshape(N_CHIPS, -1, d)])
    dqh, dkh, dvh, rest_arrived = attention_bwd("attention_bwd", qh, kh, vh, d_attn, lse, delta, rest_pairs)
    rd, rf = ffn0_grads[0].shape[1], ffn0_grads[2].shape[1]
    rest_where = [(0, 0), (1, 0), (2, rd), (3, rd), (4, rf), (2, 0), (3, 0), (4, 0), (5, 0)]
    rest_names = later + ["mla_w_o"]
    dq, dkv, dkr, rest_grads = qkv_heads_bwd("qkv_heads_bwd", dqh, dkh, dvh, cos, sin, sum_from_chips("rest", rest_pairs, rest_arrived),
                                              [shard_shape(n) for n in rest_names], rest_where)
    grads = dict(zip(rest_names, rest_grads))
    dcq = linear_nt("mla_q_up_bwd_input", dq, w_uq, F32)
    d_w_uq = wgrad("mla_q_up_wgrad", cq, dq)
    dckv = linear_nt("mla_kv_up_bwd_input", dkv, w_ukv, F32)
    d_w_ukv = wgrad("mla_kv_up_wgrad", ckv, dkv)
    dproj, d_g_cq, d_g_ckv = mla_mid_bwd("mla_mid_bwd", proj, mla_g_cq, mla_g_ckv, dcq, dckv, dkr, cos, sin)
    d_w_in = wgrad("mla_in_wgrad", a0, dproj)
    mla_pairs = reduce_to_pair_sums("mla", [d_w_in.reshape(N_CHIPS, -1, d_w_in.shape[-1]), _stack_cols(d_w_uq), _stack_cols(d_w_ukv)])
    grad_x, d_mla_norm, mla_arrived = linear_nt_norm_bwd("mla_in_bwd_input", dproj, w_in, h0, mla_norm, dh1, mla_pairs)

    grads.update(zip(first[:3], sibling_join_halves("sibling_join_mla", sum_from_chips("mla", mla_pairs, mla_arrived),
                                                    [shard_shape(n) for n in first[:3]], [(i, 0) for i in range(3)])))

    def pad_row(v):
        return jnp.pad(v, ((0, 0), (0, d - v.shape[1])))

    small = all_reduce_small("all_reduce_small_grads", pack_rows([
        d_mla_norm, pad_row(d_g_cq), pad_row(d_g_ckv), d_ffn_norm0, d_ffn_norm1, d_final_norm, d_conv_norm,
        d_conv_w[0:1], d_conv_w[1:2], d_conv_w[2:3], jnp.broadcast_to(loss_local, (1, d))]))
    loss = small[10, 0]
    grads["mla_norm"] = small[0:1]
    grads["mla_g_cq"] = small[1:2, :mla_g_cq.shape[1]]
    grads["mla_g_ckv"] = small[2:3, :mla_g_ckv.shape[1]]
    grads["ffn_norm"] = small[3:5]
    grads["final_norm"] = small[5:6]
    grads["conv_norm"] = lax.dynamic_slice(small[6:7], (0, chip * d4), (1, d4))
    grads["conv_w"] = lax.dynamic_slice(small[7:10], (0, chip * d4), (3, d4))

    outs_g, outs_d, outs_m, outs_v = [], [], [], []
    for n in order:
        w = weights[n]
        if w.ndim == 3 and w.shape[2] % 128 and w.shape[1] % 128 == 0:
            results = adamw_swapped(f"adamw_{n}", jnp.swapaxes(w, 1, 2), grads[n].reshape(-1, w.shape[2]),
                                    jnp.swapaxes(m_in[n], 1, 2), jnp.swapaxes(v_in[n], 1, 2))
            grad_w, delta_w, new_m, new_v = [jnp.swapaxes(o, 1, 2) for o in results]
        else:
            delta_w, new_m, new_v = adamw(f"adamw_{n}", rows2d(w), grads[n].reshape(rows2d(w).shape), rows2d(m_in[n]), rows2d(v_in[n]))
            grad_w = grads[n]
        outs_g.append(grad_w.reshape(w.shape))
        outs_d.append(delta_w.reshape(w.shape))
        outs_m.append(new_m.reshape(w.shape))
        outs_v.append(new_v.reshape(w.shape))
    return (loss, grad_x.reshape(x.shape), *outs_g, *outs_d, *outs_m, *outs_v)
```

```python
import math

import jax
import jax.numpy as jnp
from jax import lax
from jax.experimental import pallas as pl
from jax.experimental.pallas import tpu as pltpu

F32 = jnp.float32
BF16 = jnp.bfloat16
S = jax.ShapeDtypeStruct

N_HEADS = 8
NOPE = 128
ROPE = 64
HALF = ROPE // 2
VDIM = 128
QK = NOPE + ROPE
CHUNK = 64
ROPE_THETA = 10000.0
RMS_EPS = 1e-6
ADAM_LR = 0.001
ADAM_B1 = 0.9
ADAM_B2 = 0.999
ADAM_EPS = 1e-08
ADAM_WD = 0.01
ADAM_STEP = 10

N_CHIPS = 4
N_DEV = 8
MASK_VALUE = -1e30
SCORE_SCALE = 1.0 / math.sqrt(QK)
LOG2_E = math.log2(math.e)
SCORE_SCALE_LOG2 = SCORE_SCALE * LOG2_E
VMEM_LIMIT = 48 * 1024 * 1024
VMEM_LIMIT_WHOLE_HEAD = 58 * 1024 * 1024
ATT_BLOCK = 512
SMALL_ROWS = 16

_NN = (((1,), (0,)), ((), ()))
_NT = (((1,), (1,)), ((), ()))
_TN = (((0,), (0,)), ((), ()))
MESH = pl.DeviceIdType.MESH
ANY = pl.BlockSpec(memory_space=pl.ANY)


def _hbm_call(body, **kwargs):
    out_shape = kwargs.pop("out_shape")
    single = not isinstance(out_shape, (list, tuple))
    on_hbm = [pltpu.HBM(s.shape, s.dtype) for s in ([out_shape] if single else out_shape)]
    call = pl.pallas_call(body, out_shape=on_hbm[0] if single else on_hbm, **kwargs)
    return lambda *operands: call(*[pltpu.with_memory_space_constraint(x, pltpu.HBM) for x in operands])


def _params(n_axes, vmem_limit=VMEM_LIMIT):
    return pltpu.CompilerParams(dimension_semantics=("arbitrary",) * n_axes, vmem_limit_bytes=vmem_limit)


def _tile(n, cap, mult=8):
    for t in range(min(cap, n), 0, -1):
        if n % t == 0 and t % mult == 0:
            return t
    return n


def _sigmoid(x):
    return 0.5 * jnp.tanh(0.5 * x) + 0.5


def _mm(name, a_ops, b_ops, products, dims, grid, k_axis, outs, acc_shape, epilogue, extra_ops=()):
    na, nb, ne, no = len(a_ops), len(b_ops), len(extra_ops), len(outs)
    n_acc = 1 + max(c for _, _, c in products)
    nk = 1 if k_axis is None else grid[k_axis]

    def body(*refs):
        a_refs = refs[:na]
        b_refs = refs[na:na + nb]
        e_refs = refs[na + nb:na + nb + ne]
        o_refs = refs[na + nb + ne:na + nb + ne + no]
        acc_refs = refs[na + nb + ne + no:]

        def partial_sums():
            vals = [None] * n_acc
            for ai, bi, ci in products:
                d = lax.dot_general(a_refs[ai][...].astype(BF16), b_refs[bi][...].astype(BF16), dims,
                                    preferred_element_type=F32)
                vals[ci] = d if vals[ci] is None else vals[ci] + d
            return vals

        if nk == 1:
            epilogue(partial_sums(), e_refs, o_refs)
        else:
            k = pl.program_id(k_axis)

            @pl.when(k == 0)
            def _():
                for acc in acc_refs:
                    acc[...] = jnp.zeros_like(acc)

            for acc, v in zip(acc_refs, partial_sums()):
                acc[...] += v

            @pl.when(k == nk - 1)
            def _():
                epilogue([acc[...] for acc in acc_refs], e_refs, o_refs)

    ops = list(a_ops) + list(b_ops) + list(extra_ops)
    return _hbm_call(
        body, name=name, grid=grid,
        in_specs=[s for _, s in ops], out_specs=[s for _, s in outs], out_shape=[o for o, _ in outs],
        scratch_shapes=[pltpu.VMEM(acc_shape, F32) for _ in range(n_acc if nk > 1 else 0)],
        compiler_params=_params(len(grid)),
    )(*[a for a, _ in ops])


def _store(accs, e_refs, o_refs):
    o_refs[0][...] = accs[0].astype(o_refs[0].dtype)


def linear(name, x, w, out_dtype, resid=None, next_gain=None):
    t, k = x.shape
    n = w.shape[1]
    tm = _tile(t, 512)
    tn = n if n <= 2048 else _tile(n, 1024, 128)
    tile = pl.BlockSpec((tm, tn), lambda j, i: (i, j))
    extra = [] if resid is None else [(resid, tile)]
    outs = [(S((t, n), out_dtype), tile)]
    if next_gain is not None:
        assert tn == n
        extra.append((next_gain, pl.BlockSpec((1, n), lambda j, i: (0, 0))))
        outs.append((S((t, n), BF16), tile))

    def epilogue(accs, e_refs, o_refs):
        y = accs[0] if resid is None else e_refs[0][...] + accs[0]
        o_refs[0][...] = y.astype(out_dtype)
        if next_gain is not None:
            o_refs[1][...] = (y * _rstd(y) * e_refs[-1][...]).astype(BF16)

    res = _mm(name, [(x, pl.BlockSpec((tm, k), lambda j, i: (i, 0)))], [(w, pl.BlockSpec((k, tn), lambda j, i: (0, j)))],
              [(0, 0, 0)], _NN, (n // tn, t // tm), None, outs, None, epilogue, extra)
    return res[0] if next_gain is None else res


def linear_nt(name, dy, w, out_dtype):
    t, n = dy.shape
    k = w.shape[0]
    tm = _tile(t, 512)
    tc = n if n <= 2048 else _tile(n, 1024, 128)
    return _mm(name, [(dy, pl.BlockSpec((tm, tc), lambda i, c: (i, c)))], [(w, pl.BlockSpec((k, tc), lambda i, c: (0, c)))],
               [(0, 0, 0)], _NT, (t // tm, n // tc), 1,
               [(S((t, k), out_dtype), pl.BlockSpec((tm, k), lambda i, c: (i, 0)))], (tm, k), _store)[0]


def wgrad(name, x, dy):
    t, k = x.shape
    n = dy.shape[1]
    tk = _tile(t, 512)
    tn = n if n <= 1024 else _tile(n, 1024, 128)
    return _mm(name, [(x, pl.BlockSpec((tk, k), lambda j, s: (s, 0)))], [(dy, pl.BlockSpec((tk, tn), lambda j, s: (s, j)))],
               [(0, 0, 0)], _TN, (n // tn, t // tk), 1,
               [(S((k, n), BF16), pl.BlockSpec((k, tn), lambda j, s: (0, j)))], (k, tn), _store)[0]


def _resident(shape, index_map):
    return pl.BlockSpec(shape, index_map, pipeline_mode=pl.Buffered(1))


def ffn_up(name, a, wg_all, wu_all, layer):
    t, d = a.shape
    f4 = wg_all.shape[2]
    tm = _tile(t, 512)
    w_spec = _resident((N_CHIPS, d, f4), lambda i: (0, layer, 0))
    h_spec = pl.BlockSpec((N_CHIPS, tm, f4), lambda i: (0, i, 0))

    def body(a_ref, wg_ref, wu_ref, zg_ref, zu_ref, z_ref):
        av = a_ref[...]
        for k in range(N_CHIPS):
            g = jnp.dot(av, wg_ref[k], preferred_element_type=F32)
            u = jnp.dot(av, wu_ref[k], preferred_element_type=F32)
            sg = _sigmoid(g)
            silu = g * sg
            zg_ref[k] = (u * (sg * (1.0 + g * (1.0 - sg)))).astype(BF16)
            zu_ref[k] = silu.astype(BF16)
            z_ref[k] = (silu * u).astype(BF16)

    return _hbm_call(
        body, name=name, grid=(t // tm,), in_specs=[pl.BlockSpec((tm, d), lambda i: (i, 0)), w_spec, w_spec],
        out_specs=[h_spec] * 3, out_shape=[S((N_CHIPS, t, f4), BF16)] * 3, compiler_params=_params(1))(a, wg_all, wu_all)


def ffn_down(name, z, wd_all, layer, resid, next_gain=None):
    _, t, f4 = z.shape
    d = wd_all.shape[2]
    tm = _tile(t, 512)
    row = pl.BlockSpec((tm, d), lambda i: (i, 0))
    normed = next_gain is not None

    def body(z_ref, wd_ref, r_ref, *refs):
        acc = r_ref[...]
        for k in range(N_CHIPS):
            acc = acc + jnp.dot(z_ref[k], wd_ref[k], preferred_element_type=F32)
        refs[-2 if normed else -1][...] = acc
        if normed:
            refs[-1][...] = (acc * _rstd(acc) * refs[0][...]).astype(BF16)

    res = _hbm_call(
        body, name=name, grid=(t // tm,),
        in_specs=[pl.BlockSpec((N_CHIPS, tm, f4), lambda i: (0, i, 0)), _resident((N_CHIPS, f4, d), lambda i: (0, layer, 0)), row]
        + ([pl.BlockSpec((1, d), lambda i: (0, 0))] if normed else []),
        out_specs=[row] * (2 if normed else 1), out_shape=[S((t, d), F32)] + ([S((t, d), BF16)] if normed else []),
        compiler_params=_params(1))(z, wd_all, resid, *([next_gain] if normed else []))
    return res if normed else res[0]


def ffn_bwd_hidden(name, dh, wd_all, layer, zg, zu):
    t, d = dh.shape
    f4 = zg.shape[2]
    tm = _tile(t, 512)
    h_spec = pl.BlockSpec((N_CHIPS, tm, f4), lambda i: (0, i, 0))

    def body(dh_ref, wd_ref, zg_ref, zu_ref, dg_ref, du_ref):
        dhb = dh_ref[...].astype(BF16)
        for k in range(N_CHIPS):
            dz = lax.dot_general(dhb, wd_ref[k], _NT, preferred_element_type=F32)
            dg_ref[k] = (dz * zg_ref[k].astype(F32)).astype(BF16)
            du_ref[k] = (dz * zu_ref[k].astype(F32)).astype(BF16)

    return _hbm_call(
        body, name=name, grid=(t // tm,),
        in_specs=[pl.BlockSpec((tm, d), lambda i: (i, 0)), _resident((N_CHIPS, f4, d), lambda i: (0, layer, 0)), h_spec, h_spec],
        out_specs=[h_spec] * 2, out_shape=[S((N_CHIPS, t, f4), BF16)] * 2, compiler_params=_params(1))(dh, wd_all, zg, zu)


def _norm_bwd_specs(tm, d):
    row = pl.BlockSpec((tm, d), lambda i: (i, 0))
    vec = pl.BlockSpec((1, d), lambda i: (0, 0))
    return [row, vec, row], [row, vec]


def _norm_bwd_tail(da, h_ref, g_ref, dhi_ref, dho_ref, dgain_ref):
    dx, dgain = _rms_bwd(h_ref[...], g_ref[...], da)
    dho_ref[...] = dhi_ref[...] + dx

    @pl.when(pl.program_id(0) == 0)
    def _():
        dgain_ref[...] = jnp.zeros_like(dgain_ref)

    dgain_ref[...] += dgain


def ffn_bwd_input(name, dg, du, wg_all, wu_all, layer, h, gain, dh_in):
    _, t, f4 = dg.shape
    d = h.shape[1]
    tm = _tile(t, 512)
    h_spec = pl.BlockSpec((N_CHIPS, tm, f4), lambda i: (0, i, 0))
    w_spec = _resident((N_CHIPS, d, f4), lambda i: (0, layer, 0))
    tail_in, tail_out = _norm_bwd_specs(tm, d)

    def body(dg_ref, du_ref, wg_ref, wu_ref, *tail):
        acc = jnp.zeros((tm, d), F32)
        for k in range(N_CHIPS):
            acc = acc + lax.dot_general(dg_ref[k], wg_ref[k], _NT, preferred_element_type=F32)
            acc = acc + lax.dot_general(du_ref[k], wu_ref[k], _NT, preferred_element_type=F32)
        _norm_bwd_tail(acc, *tail)

    return _hbm_call(
        body, name=name, grid=(t // tm,), in_specs=[h_spec, h_spec, w_spec, w_spec] + tail_in, out_specs=tail_out,
        out_shape=[S((t, d), F32), S((1, d), F32)], compiler_params=_params(1))(dg, du, wg_all, wu_all, h, gain, dh_in)


def ffn_wgrad_up(name, a, dy):
    t, d = a.shape
    f4 = dy.shape[2]
    tk = _tile(t, 512)
    nt = t // tk

    def body(a_ref, dy_ref, o_ref, acc):
        s = pl.program_id(0)

        @pl.when(s == 0)
        def _():
            acc[...] = jnp.zeros_like(acc)

        at = a_ref[...].T
        for k in range(N_CHIPS):
            acc[k] += jnp.dot(at, dy_ref[k], preferred_element_type=F32)

        @pl.when(s == nt - 1)
        def _():
            o_ref[...] = acc[...].astype(BF16)

    return _hbm_call(
        body, name=name, grid=(nt,),
        in_specs=[pl.BlockSpec((tk, d), lambda s: (s, 0)), pl.BlockSpec((N_CHIPS, tk, f4), lambda s: (0, s, 0))],
        out_specs=pl.BlockSpec((N_CHIPS, d, f4), lambda s: (0, 0, 0)), out_shape=S((N_CHIPS, d, f4), BF16),
        scratch_shapes=[pltpu.VMEM((N_CHIPS, d, f4), F32)], compiler_params=_params(1))(a, dy)


def ffn_wgrad_down(name, z, dh):
    _, t, f4 = z.shape
    d = dh.shape[1]
    tk = _tile(t, 512)
    nt = t // tk

    def body(z_ref, dh_ref, o_ref, acc):
        s = pl.program_id(0)

        @pl.when(s == 0)
        def _():
            acc[...] = jnp.zeros_like(acc)

        dhb = dh_ref[...].astype(BF16)
        for k in range(N_CHIPS):
            acc[k] += lax.dot_general(z_ref[k], dhb, _TN, preferred_element_type=F32)

        @pl.when(s == nt - 1)
        def _():
            o_ref[...] = acc[...].astype(BF16)

    return _hbm_call(
        body, name=name, grid=(nt,),
        in_specs=[pl.BlockSpec((N_CHIPS, tk, f4), lambda s: (0, s, 0)), pl.BlockSpec((tk, d), lambda s: (s, 0))],
        out_specs=pl.BlockSpec((N_CHIPS, f4, d), lambda s: (0, 0, 0)), out_shape=S((N_CHIPS, f4, d), BF16),
        scratch_shapes=[pltpu.VMEM((N_CHIPS, f4, d), F32)], compiler_params=_params(1))(z, dh)


def conv_in_proj(name, a, w):
    t, d = a.shape
    tm = _tile(t, 512)
    return _mm(name, [(a, pl.BlockSpec((tm, d), lambda j, i: (i, 0)))], [(w, pl.BlockSpec((d, d), lambda j, i: (0, j)))],
               [(0, 0, 0)], _NN, (3, t // tm), None,
               [(S((3, t, d), F32), pl.BlockSpec((None, tm, d), lambda j, i: (j, i, 0)))], None, _store)[0]


def conv_in_bwd_input(name, dbcx, w, h, gain, dh_in):
    _, t, d = dbcx.shape
    tm = _tile(t, 512)
    tail_in, tail_out = _norm_bwd_specs(tm, d)

    def body(g_ref, w_ref, *tail):
        acc = jnp.zeros((tm, d), F32)
        for j in range(3):
            acc = acc + lax.dot_general(g_ref[j], w_ref[:, j * d:(j + 1) * d], _NT, preferred_element_type=F32)
        _norm_bwd_tail(acc, *tail)

    return _hbm_call(
        body, name=name, grid=(t // tm,),
        in_specs=[pl.BlockSpec((3, tm, d), lambda i: (0, i, 0)), _resident((d, 3 * d), lambda i: (0, 0))] + tail_in,
        out_specs=tail_out, out_shape=[S((t, d), F32), S((1, d), F32)], compiler_params=_params(1))(dbcx, w, h, gain, dh_in)


def linear_nt_norm_bwd(name, dy, w, h, gain, dh_in, parts=()):
    t, n = dy.shape
    k = w.shape[0]
    tm = _tile(t, 512)
    tail_in, tail_out = _norm_bwd_specs(tm, k)
    m = len(parts)

    def body(dy_ref, w_ref, h_ref, g_ref, dhi_ref, *refs):
        if m:
            _ride_along(scatter_ici_copies(refs[:m], refs[m + 2:2 * m + 2], *refs[2 * m + 2:]), (pl.program_id(0),), (t // tm,))
        da = lax.dot_general(dy_ref[...].astype(BF16), w_ref[...], _NT, preferred_element_type=F32)
        _norm_bwd_tail(da, h_ref, g_ref, dhi_ref, *refs[m:m + 2])

    outs = _hbm_call(
        body, name=name, grid=(t // tm,),
        in_specs=[pl.BlockSpec((tm, n), lambda i: (i, 0)), _resident((k, n), lambda i: (0, 0))] + tail_in + [ANY] * m,
        out_specs=tail_out + [ANY] * m, out_shape=[S((t, k), F32), S((1, k), F32)] + [S(p.shape, p.dtype) for p in parts],
        scratch_shapes=[pltpu.SemaphoreType.DMA((m, 3)), pltpu.SemaphoreType.DMA((m, 3))] if m else [],
        compiler_params=_params(1))(dy, w, h, gain, dh_in, *parts)
    return outs[0], outs[1], list(outs[2:])


def conv_in_wgrad(name, a, dbcx):
    t, d = a.shape
    tk = _tile(t, 512)
    nt = t // tk
    n4 = 3 * d // N_CHIPS

    def body(a_ref, g_ref, o_ref, acc):
        s = pl.program_id(0)

        @pl.when(s == 0)
        def _():
            acc[...] = jnp.zeros_like(acc)

        at = a_ref[...].T
        for j in range(3):
            acc[:, j * d:(j + 1) * d] += jnp.dot(at, g_ref[j], preferred_element_type=F32)

        @pl.when(s == nt - 1)
        def _():
            for k in range(N_CHIPS):
                o_ref[k] = acc[:, k * n4:(k + 1) * n4].astype(BF16)

    return _hbm_call(
        body, name=name, grid=(nt,),
        in_specs=[pl.BlockSpec((tk, d), lambda s: (s, 0)), pl.BlockSpec((3, tk, d), lambda s: (0, s, 0))],
        out_specs=pl.BlockSpec((N_CHIPS, d, n4), lambda s: (0, 0, 0)), out_shape=S((N_CHIPS, d, n4), BF16),
        scratch_shapes=[pltpu.VMEM((d, 3 * d), F32)], compiler_params=_params(1))(a, dbcx)


def _rstd(x):
    return lax.rsqrt(jnp.mean(x * x, axis=-1, keepdims=True) + RMS_EPS)


def _rms_bwd(x, g, dy):
    r = _rstd(x)
    xhat = x * r
    dgain = jnp.sum(dy * xhat, axis=0, keepdims=True)
    dxh = dy * g
    dx = r * (dxh - xhat * jnp.mean(dxh * xhat, axis=-1, keepdims=True))
    return dx, dgain


def rms_fwd(name, h, g):
    t, d = h.shape
    tr = _tile(t, 512)

    def body(h_ref, g_ref, a_ref):
        x = h_ref[...]
        a_ref[...] = (x * _rstd(x) * g_ref[...]).astype(BF16)

    return _hbm_call(
        body, name=name, grid=(t // tr,),
        in_specs=[pl.BlockSpec((tr, d), lambda i: (i, 0)), pl.BlockSpec((1, d), lambda i: (0, 0))],
        out_specs=pl.BlockSpec((tr, d), lambda i: (i, 0)), out_shape=S((t, d), BF16), compiler_params=_params(1))(h, g)


def ffn_down_loss(name, z, wd_all, layer, resid, gain, target):
    _, t, f4 = z.shape
    d = wd_all.shape[2]
    tm = _tile(t, 512)

    def body(z_ref, wd_ref, r_ref, g_ref, t_ref, dh_ref, dg_ref, loss_ref):
        x = r_ref[...]
        for k in range(N_CHIPS):
            x = x + jnp.dot(z_ref[k], wd_ref[k], preferred_element_type=F32)
        g = g_ref[...]
        r = _rstd(x)
        xhat = x * r
        err = xhat * g - t_ref[...]
        dy = err * (1.0 / d)
        dxh = dy * g
        dh_ref[...] = r * (dxh - xhat * jnp.mean(dxh * xhat, axis=-1, keepdims=True))

        @pl.when(pl.program_id(0) == 0)
        def _():
            dg_ref[...] = jnp.zeros_like(dg_ref)
            loss_ref[...] = jnp.zeros_like(loss_ref)

        dg_ref[...] += jnp.sum(dy * xhat, axis=0, keepdims=True)
        per_token = jnp.mean(err * err, axis=-1, keepdims=True)
        loss_ref[...] += 0.5 * jnp.sum(per_token, axis=0, keepdims=True)

    row = pl.BlockSpec((tm, d), lambda i: (i, 0))
    vec = pl.BlockSpec((1, d), lambda i: (0, 0))
    one = pl.BlockSpec((1, 1), lambda i: (0, 0))
    return _hbm_call(
        body, name=name, grid=(t // tm,),
        in_specs=[pl.BlockSpec((N_CHIPS, tm, f4), lambda i: (0, i, 0)), _resident((N_CHIPS, f4, d), lambda i: (0, layer, 0)), row, vec, row],
        out_specs=[row, vec, one], out_shape=[S((t, d), F32), S((1, d), F32), S((1, 1), F32)],
        compiler_params=_params(1))(z, wd_all, resid, gain, target)


def mla_in_proj(name, a, w, g_cq, g_ckv, cos, sin):
    t, d = a.shape
    n = w.shape[1]
    ql, kl = g_cq.shape[1], g_ckv.shape[1]
    tr = _tile(t, 512)

    def body(a_ref, w_ref, gq_ref, gk_ref, c_ref, s_ref, p_ref, cq_ref, ckv_ref, kr_ref):
        p_ref[...] = jnp.dot(a_ref[...], w_ref[...], preferred_element_type=F32)
        xq = p_ref[:, 0:ql]
        cq_ref[...] = (xq * _rstd(xq) * gq_ref[...]).astype(BF16)
        xk = p_ref[:, ql:ql + kl]
        ckv_ref[...] = (xk * _rstd(xk) * gk_ref[...]).astype(BF16)
        k1 = p_ref[:, ql + kl:ql + kl + HALF]
        k2 = p_ref[:, ql + kl + HALF:ql + kl + ROPE]
        c = c_ref[...]
        s = s_ref[...]
        kr_ref[:, 0:HALF] = k1 * c - k2 * s
        kr_ref[:, HALF:ROPE] = k1 * s + k2 * c

    def row(w):
        return pl.BlockSpec((tr, w), lambda i: (i, 0))

    def vec(w):
        return pl.BlockSpec((1, w), lambda i: (0, 0))

    return _hbm_call(
        body, name=name, grid=(t // tr,),
        in_specs=[row(d), _resident((d, n), lambda i: (0, 0)), vec(ql), vec(kl), row(HALF), row(HALF)],
        out_specs=[row(n), row(ql), row(kl), row(ROPE)],
        out_shape=[S((t, n), F32), S((t, ql), BF16), S((t, kl), BF16), S((t, ROPE), F32)],
        compiler_params=_params(1))(a, w, g_cq, g_ckv, cos, sin)


def mla_mid_bwd(name, proj, g_cq, g_ckv, dcq, dckv, dkr, cos, sin):
    t, n = proj.shape
    ql, kl = g_cq.shape[1], g_ckv.shape[1]
    tr = _tile(t, 512)

    def body(p_ref, gq_ref, gk_ref, dcq_ref, dckv_ref, dkr_ref, c_ref, s_ref, dp_ref, dgq_ref, dgk_ref):
        dxq, dgq = _rms_bwd(p_ref[:, 0:ql], gq_ref[...], dcq_ref[...])
        dp_ref[:, 0:ql] = dxq.astype(BF16)
        dxk, dgk = _rms_bwd(p_ref[:, ql:ql + kl], gk_ref[...], dckv_ref[...])
        dp_ref[:, ql:ql + kl] = dxk.astype(BF16)
        d1 = dkr_ref[:, 0:HALF]
        d2 = dkr_ref[:, HALF:ROPE]
        c = c_ref[...]
        s = s_ref[...]
        dp_ref[:, ql + kl:ql + kl + HALF] = (d1 * c + d2 * s).astype(BF16)
        dp_ref[:, ql + kl + HALF:ql + kl + ROPE] = (d2 * c - d1 * s).astype(BF16)

        @pl.when(pl.program_id(0) == 0)
        def _():
            dgq_ref[...] = jnp.zeros_like(dgq_ref)
            dgk_ref[...] = jnp.zeros_like(dgk_ref)

        dgq_ref[...] += dgq
        dgk_ref[...] += dgk

    def row(w):
        return pl.BlockSpec((tr, w), lambda i: (i, 0))

    def vec(w):
        return pl.BlockSpec((1, w), lambda i: (0, 0))

    return _hbm_call(
        body, name=name, grid=(t // tr,),
        in_specs=[row(n), vec(ql), vec(kl), row(ql), row(kl), row(ROPE), row(HALF), row(HALF)],
        out_specs=[row(n), vec(ql), vec(kl)], out_shape=[S((t, n), BF16), S((1, ql), F32), S((1, kl), F32)],
        compiler_params=_params(1))(proj, g_cq, g_ckv, dcq, dckv, dkr, cos, sin)


def qkv_heads(name, q, kv, kr, cos, sin, shards=()):
    t = q.shape[0]
    tr = _tile(t, 256)
    n = len(shards)

    def body(q_ref, kv_ref, kr_ref, c_ref, s_ref, *refs):
        src = refs[:n]
        qo_ref, ko_ref, vo_ref = refs[n:n + 3]
        if n:
            _ride_along(gather_ici_copies(src, refs[n + 3:2 * n + 3], *refs[2 * n + 3:]), (pl.program_id(0),), (t // tr,))
        c = c_ref[...]
        s = s_ref[...]
        krb = kr_ref[...].astype(BF16)
        for h in range(N_HEADS):
            q0 = h * QK
            qo_ref[h, :, 0:NOPE] = q_ref[:, q0:q0 + NOPE].astype(BF16)
            q1 = q_ref[:, q0 + NOPE:q0 + NOPE + HALF]
            q2 = q_ref[:, q0 + NOPE + HALF:q0 + QK]
            qo_ref[h, :, NOPE:NOPE + HALF] = (q1 * c - q2 * s).astype(BF16)
            qo_ref[h, :, NOPE + HALF:QK] = (q1 * s + q2 * c).astype(BF16)
            k0 = h * (NOPE + VDIM)
            ko_ref[h, :, 0:NOPE] = kv_ref[:, k0:k0 + NOPE]
            ko_ref[h, :, NOPE:QK] = krb
            vo_ref[h] = kv_ref[:, k0 + NOPE:k0 + NOPE + VDIM]

    def row(w):
        return pl.BlockSpec((tr, w), lambda i: (i, 0))

    def heads(w):
        return pl.BlockSpec((N_HEADS, tr, w), lambda i: (0, i, 0))

    outs = _hbm_call(
        body, name=name, grid=(t // tr,),
        in_specs=[row(N_HEADS * QK), row(N_HEADS * (NOPE + VDIM)), row(ROPE), row(HALF), row(HALF)] + [ANY] * n,
        out_specs=[heads(QK), heads(QK), heads(VDIM)] + [ANY] * n,
        out_shape=[S((N_HEADS, t, QK), BF16), S((N_HEADS, t, QK), BF16), S((N_HEADS, t, VDIM), BF16)]
        + [S((N_CHIPS,) + s.shape, s.dtype) for s in shards],
        scratch_shapes=[pltpu.SemaphoreType.DMA((n, 3)), pltpu.SemaphoreType.DMA((n, 3))] if n else [],
        compiler_params=_params(1))(q, kv, kr, cos, sin, *shards)
    return outs[0], outs[1], outs[2], list(outs[3:])


def qkv_heads_bwd(name, dq_h, dk_h, dv_h, cos, sin, halves=(), targets=(), where=()):
    t = dq_h.shape[1]
    tr = _tile(t, 256)
    n, nt = len(halves), len(targets)

    def body(dq_ref, dk_ref, dv_ref, c_ref, s_ref, *refs):
        q_ref, kv_ref, kr_ref = refs[n:n + 3]
        if n:
            _ride_along(join_copies(refs[:n], refs[n + 3:n + 3 + nt], where, *refs[n + 3 + nt:]), (pl.program_id(0),), (t // tr,))
        c = c_ref[...]
        s = s_ref[...]
        dkr = jnp.zeros((tr, ROPE), F32)
        for h in range(N_HEADS):
            q0 = h * QK
            q_ref[:, q0:q0 + NOPE] = dq_ref[h, :, 0:NOPE].astype(BF16)
            d1 = dq_ref[h, :, NOPE:NOPE + HALF]
            d2 = dq_ref[h, :, NOPE + HALF:QK]
            q_ref[:, q0 + NOPE:q0 + NOPE + HALF] = (d1 * c + d2 * s).astype(BF16)
            q_ref[:, q0 + NOPE + HALF:q0 + QK] = (d2 * c - d1 * s).astype(BF16)
            k0 = h * (NOPE + VDIM)
            kv_ref[:, k0:k0 + NOPE] = dk_ref[h, :, 0:NOPE].astype(BF16)
            kv_ref[:, k0 + NOPE:k0 + NOPE + VDIM] = dv_ref[h].astype(BF16)
            dkr = dkr + dk_ref[h, :, NOPE:QK]
        kr_ref[...] = dkr

    def row(w):
        return pl.BlockSpec((tr, w), lambda i: (i, 0))

    def heads(w):
        return pl.BlockSpec((N_HEADS, tr, w), lambda i: (0, i, 0))

    outs = _hbm_call(
        body, name=name, grid=(t // tr,),
        in_specs=[heads(QK), heads(QK), heads(VDIM), row(HALF), row(HALF)] + [ANY] * n,
        out_specs=[row(N_HEADS * QK), row(N_HEADS * (NOPE + VDIM)), row(ROPE)] + [ANY] * nt,
        out_shape=[S((t, N_HEADS * QK), BF16), S((t, N_HEADS * (NOPE + VDIM)), BF16), S((t, ROPE), F32)]
        + [S(tg, F32) for tg in targets],
        scratch_shapes=[pltpu.SemaphoreType.DMA((n,)), pltpu.SemaphoreType.DMA((n,))] if n else [],
        compiler_params=_params(1))(dq_h, dk_h, dv_h, cos, sin, *halves)
    return outs[0], outs[1], outs[2], _fill_own_halves(outs[3:], halves, where)


def _chunk_mask_t(q_start, k_start, bq, bk):
    kc = (k_start + lax.broadcasted_iota(jnp.int32, (bk, bq), 0)) // CHUNK
    qc = (q_start + lax.broadcasted_iota(jnp.int32, (bk, bq), 1)) // CHUNK
    return kc <= qc


def attention_fwd(name, q, k, v, shards=()):
    nh, t, _ = q.shape
    blk = ATT_BLOCK
    nq = t // blk
    n = len(shards)

    def body(q_ref, k_ref, v_ref, *refs):
        src = refs[:n]
        o_ref, lse_ref = refs[n:n + 2]
        dst = refs[n + 2:2 * n + 2]
        m_ref, l_ref, acc_ref, s_buf, p_buf, alpha_buf, bias_ref = refs[2 * n + 2:2 * n + 9]
        i = pl.program_id(1)
        if n:
            send_sems, recv_sems = refs[2 * n + 9:]
            _ride_along(gather_ici_copies(src, dst, send_sems, recv_sems), (pl.program_id(0), i), (nh, nq))

        @pl.when((pl.program_id(0) == 0) & (i == 0))
        def _():
            bias_ref[...] = jnp.where(_chunk_mask_t(0, 0, blk, blk), 0.0, MASK_VALUE)

        m_ref[...] = jnp.full_like(m_ref, MASK_VALUE)
        l_ref[...] = jnp.zeros_like(l_ref)
        acc_ref[...] = jnp.zeros_like(acc_ref)

        def rows(b):
            return pl.ds(pl.multiple_of(b * blk, blk), blk)

        def scores(b, slot):
            s_buf[slot] = lax.dot_general(k_ref[rows(b), :], q_ref[...], _NT, preferred_element_type=F32)

        def softmax(slot, diagonal):
            s = s_buf[slot]
            if diagonal:
                s = s + bias_ref[...]
            m_old = m_ref[...]
            m_new = jnp.maximum(m_old, jnp.max(s, axis=0, keepdims=True))
            p = jnp.exp2((s - m_new) * SCORE_SCALE_LOG2)
            alpha = jnp.exp2((m_old - m_new) * SCORE_SCALE_LOG2)
            l_ref[...] = alpha * l_ref[...] + jnp.sum(p, axis=0, keepdims=True)
            m_ref[...] = m_new
            alpha_buf[slot] = alpha
            p_buf[slot] = p.astype(BF16)

        def values(b, slot):
            pv = lax.dot_general(v_ref[rows(b), :], p_buf[slot], _TN, preferred_element_type=F32)
            acc_ref[...] = alpha_buf[slot] * acc_ref[...] + pv

        def step(t, slot):
            values(t - 2, slot)
            softmax(1 - slot, False)
            scores(t, slot)

        scores(0, 0)

        @pl.when(i == 0)
        def _():
            softmax(0, True)
            values(0, 0)

        @pl.when(i > 0)
        def _():
            scores(1, 1)
            softmax(0, False)
            steady = i - 1

            def pair(u, carry):
                step(2 + 2 * u, 0)
                step(3 + 2 * u, 1)
                return carry

            lax.fori_loop(0, steady // 2, pair, 0)

            @pl.when(steady % 2 == 1)
            def _():
                step(i, 0)

            last = i % 2
            softmax(last, True)
            values(i - 1, 1 - last)
            values(i, last)

        l = l_ref[...]
        o_ref[...] = (acc_ref[...] / l).T
        lse_ref[...] = m_ref[...] * SCORE_SCALE + jnp.log(l)

    outs = _hbm_call(
        body, name=name, grid=(nh, nq),
        in_specs=[pl.BlockSpec((None, blk, QK), lambda h, i: (h, i, 0)), pl.BlockSpec((None, t, QK), lambda h, i: (h, 0, 0)),
                  pl.BlockSpec((None, t, VDIM), lambda h, i: (h, 0, 0))] + [ANY] * n,
        out_specs=[pl.BlockSpec((blk, VDIM), lambda h, i: (i, h)),
                   pl.BlockSpec((None, None, 1, blk), lambda h, i: (h, i, 0, 0))] + [ANY] * n,
        out_shape=[S((t, nh * VDIM), F32), S((nh, nq, 1, blk), F32)] + [S((N_CHIPS,) + s.shape, s.dtype) for s in shards],
        scratch_shapes=[pltpu.VMEM((1, blk), F32), pltpu.VMEM((1, blk), F32), pltpu.VMEM((VDIM, blk), F32),
                        pltpu.VMEM((2, blk, blk), F32), pltpu.VMEM((2, blk, blk), BF16), pltpu.VMEM((2, 1, blk), F32),
                        pltpu.VMEM((blk, blk), F32)]
        + ([pltpu.SemaphoreType.DMA((n, 3)), pltpu.SemaphoreType.DMA((n, 3))] if n else []),
        compiler_params=_params(2))(q, k, v, *shards)
    return outs[0], outs[1], list(outs[2:])


def attention_out_bwd(name, dh, w_o, o):
    t, d = dh.shape
    n = w_o.shape[0]
    blk = ATT_BLOCK

    def body(dh_ref, w_ref, o_ref, do_ref, d_ref):
        do_ref[...] = lax.dot_general(dh_ref[...].astype(BF16), w_ref[...], _NT, preferred_element_type=F32)
        for h in range(N_HEADS):
            cols = slice(h * VDIM, (h + 1) * VDIM)
            d_ref[h] = jnp.sum((do_ref[:, cols] * o_ref[:, cols]).T, axis=0, keepdims=True)

    tile = pl.BlockSpec((blk, n), lambda i: (i, 0))
    return _hbm_call(
        body, name=name, grid=(t // blk,),
        in_specs=[pl.BlockSpec((blk, d), lambda i: (i, 0)), _resident((n, d), lambda i: (0, 0)), tile],
        out_specs=[tile, pl.BlockSpec((N_HEADS, None, 1, blk), lambda i: (0, i, 0, 0))],
        out_shape=[S((t, n), F32), S((N_HEADS, t // blk, 1, blk), F32)], compiler_params=_params(1))(dh, w_o, o)


def attention_bwd(name, q, k, v, do, lse, delta, parts=()):
    nh, t, _ = q.shape
    blk = ATT_BLOCK
    nq = t // blk
    n_pairs = nq * (nq + 1) // 2
    n = len(parts)
    scale = SCORE_SCALE

    def body(q_ref, k_ref, v_ref, do_ref, lse_ref, dl_ref, *refs):
        src = refs[:n]
        dq_ref, dk_ref, dv_ref = refs[n:n + 3]
        dst = refs[n + 3:2 * n + 3]
        s_buf, dp_buf, p_buf, ds_buf, bias_ref = refs[2 * n + 3:2 * n + 8]
        if n:
            send_sems, recv_sems = refs[2 * n + 8:]
            _ride_along(scatter_ici_copies(src, dst, send_sems, recv_sems), (pl.program_id(0),), (nh,))

        @pl.when(pl.program_id(0) == 0)
        def _():
            bias_ref[...] = jnp.where(_chunk_mask_t(0, 0, blk, blk), 0.0, MASK_VALUE)

        dq_ref[...] = jnp.zeros_like(dq_ref)
        dk_ref[...] = jnp.zeros_like(dk_ref)
        dv_ref[...] = jnp.zeros_like(dv_ref)

        def rows(x):
            return pl.ds(pl.multiple_of(x * blk, blk), blk)

        def after(jb):
            j, b = jb
            wrap = b == nq - 1 - j
            return jnp.where(wrap, j + 1, j), jnp.where(wrap, 0, b + 1)

        def products(jb, slot):
            j, b = jb
            s_buf[slot] = lax.dot_general(k_ref[rows(j), :], q_ref[rows(j + b), :], _NT, preferred_element_type=F32)
            dp_buf[slot] = lax.dot_general(v_ref[rows(j), :], do_ref[rows(j + b), :].astype(BF16), _NT, preferred_element_type=F32)

        def softmax_bwd(jb, slot):
            j, b = jb
            s = s_buf[slot] + bias_ref[...] * (b == 0).astype(F32)
            p = jnp.exp2(s * SCORE_SCALE_LOG2 - lse_ref[j + b] * LOG2_E)
            p_buf[slot] = p.astype(BF16)
            ds_buf[slot] = (p * (dp_buf[slot] - dl_ref[j + b]) * scale).astype(BF16)

        def gradients(jb, slot):
            j, b = jb
            dv_ref[rows(j), :] += jnp.dot(p_buf[slot], do_ref[rows(j + b), :].astype(BF16), preferred_element_type=F32)
            dk_ref[rows(j), :] += jnp.dot(ds_buf[slot], q_ref[rows(j + b), :], preferred_element_type=F32)
            dq_ref[rows(j + b), :] += lax.dot_general(ds_buf[slot], k_ref[rows(j), :], _TN, preferred_element_type=F32)

        def step(state, slot):
            third, second, first = state
            gradients(third, slot)
            softmax_bwd(second, 1 - slot)
            products(first, slot)
            return second, first, after(first)

        zero = jnp.int32(0)
        pair0 = (zero, zero)
        products(pair0, 0)
        if n_pairs == 1:
            softmax_bwd(pair0, 0)
            gradients(pair0, 0)
        else:
            pair1 = after(pair0)
            products(pair1, 1)
            softmax_bwd(pair0, 0)
            steady = n_pairs - 2
            state = lax.fori_loop(0, steady // 2, lambda u, st: step(step(st, 0), 1), (pair0, pair1, after(pair1)))
            if steady % 2:
                state = step(state, 0)
            before_last, last_pair, _ = state
            last = (n_pairs - 1) % 2
            softmax_bwd(last_pair, last)
            gradients(before_last, 1 - last)
            gradients(last_pair, last)

    head = lambda w: pl.BlockSpec((None, t, w), lambda h: (h, 0, 0))
    stats = pl.BlockSpec((None, nq, 1, blk), lambda h: (h, 0, 0, 0))
    outs = _hbm_call(
        body, name=name, grid=(nh,),
        in_specs=[head(QK), head(QK), head(VDIM), pl.BlockSpec((t, VDIM), lambda h: (0, h)), stats, stats] + [ANY] * n,
        out_specs=[head(QK), head(QK), head(VDIM)] + [ANY] * n,
        out_shape=[S((nh, t, QK), F32), S((nh, t, QK), F32), S((nh, t, VDIM), F32)] + [S(p.shape, p.dtype) for p in parts],
        scratch_shapes=[pltpu.VMEM((2, blk, blk), F32), pltpu.VMEM((2, blk, blk), F32), pltpu.VMEM((2, blk, blk), BF16),
                        pltpu.VMEM((2, blk, blk), BF16), pltpu.VMEM((blk, blk), F32)]
        + ([pltpu.SemaphoreType.DMA((n, 3)), pltpu.SemaphoreType.DMA((n, 3))] if n else []),
        compiler_params=_params(1, VMEM_LIMIT_WHOLE_HEAD))(q, k, v, do, lse, delta, *parts)
    return outs[0], outs[1], outs[2], list(outs[3:])


def _shift_down(u, s):
    rows = lax.broadcasted_iota(jnp.int32, u.shape, 0)
    return jnp.where(rows >= s, pltpu.roll(u, s, 0), 0.0)


def _shift_up(u, s):
    n = u.shape[0]
    rows = lax.broadcasted_iota(jnp.int32, u.shape, 0)
    return jnp.where(rows < n - s, pltpu.roll(u, n - s, 0), 0.0)


def _conv_specs(t, d, lanes):
    slab = lambda part: pl.BlockSpec((None, t, lanes), lambda j, part=part: (part, 0, j))
    return slab, pl.BlockSpec((3, lanes), lambda j: (0, j)), pl.BlockSpec((t, lanes), lambda j: (0, j))


def conv_fwd(name, bcx, w):
    _, t, d = bcx.shape
    lanes = _tile(d, 128, 128)
    slab, w_spec, col = _conv_specs(t, d, lanes)

    def body(b_ref, c_ref, x_ref, w_ref, y_ref):
        u = c_ref[...] * x_ref[...]
        uc = w_ref[0:1, :] * _shift_down(u, 2) + w_ref[1:2, :] * _shift_down(u, 1) + w_ref[2:3, :] * u
        y_ref[...] = (b_ref[...] * uc).astype(BF16)

    return _hbm_call(
        body, name=name, grid=(d // lanes,), in_specs=[slab(0), slab(1), slab(2), w_spec], out_specs=col,
        out_shape=S((t, d), BF16), compiler_params=_params(1))(bcx, bcx, bcx, w)


def conv_bwd(name, bcx, w, dy):
    _, t, d = bcx.shape
    lanes = _tile(d, 128, 128)
    slab, w_spec, col = _conv_specs(t, d, lanes)

    def body(b_ref, c_ref, x_ref, w_ref, dy_ref, d_ref, dw_ref):
        c = c_ref[...]
        x = x_ref[...]
        dyv = dy_ref[...]
        u = c * x
        u1 = _shift_down(u, 1)
        u2 = _shift_down(u, 2)
        w0, w1, w2 = w_ref[0:1, :], w_ref[1:2, :], w_ref[2:3, :]
        d_ref[0] = (dyv * (w0 * u2 + w1 * u1 + w2 * u)).astype(BF16)
        duc = dyv * b_ref[...]
        dw_ref[0:1, :] = jnp.sum(duc * u2, axis=0, keepdims=True)
        dw_ref[1:2, :] = jnp.sum(duc * u1, axis=0, keepdims=True)
        dw_ref[2:3, :] = jnp.sum(duc * u, axis=0, keepdims=True)
        du = w2 * duc + w1 * _shift_up(duc, 1) + w0 * _shift_up(duc, 2)
        d_ref[1] = (du * x).astype(BF16)
        d_ref[2] = (du * c).astype(BF16)

    return _hbm_call(
        body, name=name, grid=(d // lanes,), in_specs=[slab(0), slab(1), slab(2), w_spec, col],
        out_specs=[pl.BlockSpec((3, t, lanes), lambda j: (0, 0, j)), w_spec], out_shape=[S((3, t, d), BF16), S((3, d), F32)],
        compiler_params=_params(1))(bcx, bcx, bcx, w, dy)


def _adamw_update(w, g, m, v):
    m_new = ADAM_B1 * m + (1.0 - ADAM_B1) * g
    v_new = ADAM_B2 * v + (1.0 - ADAM_B2) * (g * g)
    m_hat = m_new / (1.0 - ADAM_B1 ** ADAM_STEP)
    v_hat = v_new / (1.0 - ADAM_B2 ** ADAM_STEP)
    return -ADAM_LR * (m_hat / (jnp.sqrt(v_hat) + ADAM_EPS) + ADAM_WD * w), m_new, v_new


def adamw(name, w, g, m, v):
    r, c = w.shape
    tr = _tile(r, 512)

    def body(w_ref, g_ref, m_ref, v_ref, d_ref, mo_ref, vo_ref):
        d_ref[...], mo_ref[...], vo_ref[...] = _adamw_update(w_ref[...], g_ref[...], m_ref[...], v_ref[...])

    blk = pl.BlockSpec((tr, c), lambda i: (i, 0))
    return pl.pallas_call(
        body, name=name, grid=(r // tr,), in_specs=[blk] * 4, out_specs=[blk] * 3, out_shape=[S((r, c), F32)] * 3,
        compiler_params=_params(1))(w, g, m, v)


def adamw_swapped(name, wt, g, mt, vt):
    nl, c, r = wt.shape
    tr = _tile(r, 512, 128)
    nr = r // tr

    def body(w_ref, g_ref, m_ref, v_ref, go_ref, d_ref, mo_ref, vo_ref):
        gt = g_ref[...].T
        go_ref[...] = gt
        d_ref[...], mo_ref[...], vo_ref[...] = _adamw_update(w_ref[...], gt, m_ref[...], v_ref[...])

    swapped = pl.BlockSpec((None, c, tr), lambda l, i: (l, 0, i))
    return pl.pallas_call(
        body, name=name, grid=(nl, nr),
        in_specs=[swapped, pl.BlockSpec((tr, c), lambda l, i: (l * nr + i, 0)), swapped, swapped],
        out_specs=[swapped] * 4, out_shape=[S((nl, c, r), F32)] * 4, compiler_params=_params(2))(wt, g, mt, vt)


def _place():
    x, y, c = lax.axis_index("x"), lax.axis_index("y"), lax.axis_index("c")
    other_chips = [(1 - x, y), (x, 1 - y), (1 - x, 1 - y)]
    return x, y, c, other_chips


def _half(c, rows):
    return pl.ds(pl.multiple_of(c * (rows // 2), 16), rows // 2)


def gather_weight_shards(shards):
    n = len(shards)

    def body(*refs):
        src = refs[:n]
        dst = refs[n:2 * n]
        send_sems, recv_sems = refs[2 * n:]
        x, y, c, chips = _place()
        me = 2 * x + y
        sibling = (x, y, 1 - c)

        def copy(i, slot, half_of, sem, to, from_input=False):
            rows = _half(half_of, src[i].shape[0])
            return pltpu.make_async_remote_copy(
                src_ref=src[i].at[rows] if from_input else dst[i].at[slot, rows], dst_ref=dst[i].at[slot, rows],
                send_sem=send_sems.at[i, sem], recv_sem=recv_sems.at[i, sem], device_id=to, device_id_type=MESH)

        sent = []
        for i in range(n):
            for j, chip in enumerate(chips):
                sent.append(copy(i, me, c, j, (*chip, c), from_input=True))
                sent[-1].start()
        for i in range(n):
            for j, (px, py) in enumerate(chips):
                copy(i, 2 * px + py, c, j, sibling).wait_recv()
                sent.append(copy(i, 2 * px + py, c, 3 + j, sibling))
                sent[-1].start()
        for i in range(n):
            for j, (px, py) in enumerate(chips):
                copy(i, 2 * px + py, 1 - c, 3 + j, sibling).wait_recv()
        for cp in sent:
            cp.wait_send()

    outs = _hbm_call(
        body, name="gather_weight_shards", in_specs=[ANY] * n, out_specs=[ANY] * n,
        out_shape=[S((N_CHIPS,) + s.shape, s.dtype) for s in shards],
        scratch_shapes=[pltpu.SemaphoreType.DMA((n, 6)), pltpu.SemaphoreType.DMA((n, 6))],
    )(*shards)
    return _fill_own_slot(outs, [s[None] for s in shards])


def gather_ici_copies(src, dst, send_sems, recv_sems):
    x, y, c, chips = _place()
    me = 2 * x + y
    pairs = []
    for i in range(len(src)):
        rows = _half(c, src[i].shape[0])
        for j, (px, py) in enumerate(chips):
            def copy(slot):
                return pltpu.make_async_remote_copy(
                    src_ref=src[i].at[rows], dst_ref=dst[i].at[slot, rows], send_sem=send_sems.at[i, j],
                    recv_sem=recv_sems.at[i, j], device_id=(px, py, c), device_id_type=MESH)
            pairs.append((copy(me), copy(2 * px + py)))
    return pairs


def scatter_ici_copies(src, dst, send_sems, recv_sems):
    x, y, c, chips = _place()
    me = 2 * x + y
    pairs = []
    for i in range(len(src)):
        for j, (px, py) in enumerate(chips):
            def copy(from_slot, to_slot):
                return pltpu.make_async_remote_copy(
                    src_ref=src[i].at[from_slot], dst_ref=dst[i].at[to_slot], send_sem=send_sems.at[i, j],
                    recv_sem=recv_sems.at[i, j], device_id=(px, py, c), device_id_type=MESH)
            pairs.append((copy(2 * px + py, me), copy(me, 2 * px + py)))
    return pairs


def _ride_along(pairs, grid_ids, grid_sizes):
    first = grid_ids[0] == 0
    last = grid_ids[0] == grid_sizes[0] - 1
    for g, size in zip(grid_ids[1:], grid_sizes[1:]):
        first = first & (g == 0)
        last = last & (g == size - 1)

    @pl.when(first)
    def _():
        for outgoing, _ in pairs:
            outgoing.start()

    @pl.when(last)
    def _():
        for _, incoming in pairs:
            incoming.wait_recv()
        for outgoing, _ in pairs:
            outgoing.wait_send()


def _fill_own_slot(gathered, own):
    me = 2 * lax.axis_index("x") + lax.axis_index("y")
    return [lax.dynamic_update_slice(g, o, (me,) + (0,) * (g.ndim - 1)) for g, o in zip(gathered, own)]


def forward_copies(src, dst, send_sems, recv_sems):
    x, y, c, chips = _place()
    pairs = []
    for i in range(len(src)):
        for j, (px, py) in enumerate(chips):
            def copy(half_of):
                rows = _half(half_of, src[i].shape[1])
                return pltpu.make_async_remote_copy(
                    src_ref=src[i].at[2 * px + py, rows], dst_ref=dst[i].at[2 * px + py, rows], send_sem=send_sems.at[i, j],
                    recv_sem=recv_sems.at[i, j], device_id=(x, y, 1 - c), device_id_type=MESH)
            pairs.append((copy(c), copy(1 - c)))
    return pairs


def attention_out_proj(name, attn, w_o, resid, next_gain, arriving):
    t, kdim = attn.shape
    n = w_o.shape[1]
    tm = _tile(t, 512)
    m = len(arriving)

    def body(x_ref, w_ref, r_ref, g_ref, *refs):
        h_ref, a_ref = refs[m:m + 2]
        _ride_along(forward_copies(refs[:m], refs[m + 2:2 * m + 2], *refs[2 * m + 2:]), (pl.program_id(0),), (t // tm,))
        y = r_ref[...] + jnp.dot(x_ref[...].astype(BF16), w_ref[...], preferred_element_type=F32)
        h_ref[...] = y
        a_ref[...] = (y * _rstd(y) * g_ref[...]).astype(BF16)

    row = pl.BlockSpec((tm, n), lambda i: (i, 0))
    outs = _hbm_call(
        body, name=name, grid=(t // tm,),
        in_specs=[pl.BlockSpec((tm, kdim), lambda i: (i, 0)), _resident((kdim, n), lambda i: (0, 0)), row,
                  pl.BlockSpec((1, n), lambda i: (0, 0))] + [ANY] * m,
        out_specs=[row, row] + [ANY] * m, out_shape=[S((t, n), F32), S((t, n), BF16)] + [S(g.shape, g.dtype) for g in arriving],
        input_output_aliases={4 + i: 2 + i for i in range(m)},
        scratch_shapes=[pltpu.SemaphoreType.DMA((m, 3)), pltpu.SemaphoreType.DMA((m, 3))],
        compiler_params=_params(1))(attn, w_o, resid, next_gain, *arriving)
    return outs[0], outs[1], list(outs[2:])


def sibling_swap_halves(name, grads):
    n = len(grads)

    def body(*refs):
        src = refs[:n]
        dst = refs[n:2 * n]
        send_sems, recv_sems = refs[2 * n:]
        x, y, c, _ = _place()
        copies = [pltpu.make_async_remote_copy(
            src_ref=src[i].at[:, _half(1 - c, src[i].shape[1]), :], dst_ref=dst[i], send_sem=send_sems.at[i],
            recv_sem=recv_sems.at[i], device_id=(x, y, 1 - c), device_id_type=MESH) for i in range(n)]
        for cp in copies:
            cp.start()
        for cp in copies:
            cp.wait()

    return _hbm_call(
        body, name=name, in_specs=[ANY] * n, out_specs=[ANY] * n,
        out_shape=[S((g.shape[0], g.shape[1] // 2, g.shape[2]), g.dtype) for g in grads],
        scratch_shapes=[pltpu.SemaphoreType.DMA((n,)), pltpu.SemaphoreType.DMA((n,))],
    )(*grads)


def add_halves(name, g, rx):
    _, r, cdim = g.shape
    r2 = r // 2
    tr = _tile(r2, 512, 16)
    nb = r2 // tr

    def body(lo_ref, hi_ref, rx_ref, o_ref):
        mine = jnp.where(lax.axis_index("c") == 0, lo_ref[...], hi_ref[...])
        o_ref[...] = (mine.astype(F32) + rx_ref[...].astype(F32)).astype(BF16)

    half = pl.BlockSpec((None, tr, cdim), lambda k, i: (k, i, 0))
    return pl.pallas_call(
        body, name=name, grid=(N_CHIPS, nb),
        in_specs=[half, pl.BlockSpec((None, tr, cdim), lambda k, i: (k, nb + i, 0)), half],
        out_specs=half, out_shape=S((N_CHIPS, r2, cdim), BF16), compiler_params=_params(2))(g, g, rx)


def _own_slots(parts):
    me = 2 * lax.axis_index("x") + lax.axis_index("y")
    return [lax.dynamic_slice(p, (me, 0, 0), (1,) + p.shape[1:]) for p in parts]


def sum_chips(name, parts):
    _, r2, cdim = parts.shape
    tr = _tile(r2, 512, 16)

    def body(p_ref, o_ref):
        acc = p_ref[0].astype(F32)
        for k in range(1, N_CHIPS):
            acc = acc + p_ref[k].astype(F32)
        o_ref[...] = acc

    return pl.pallas_call(
        body, name=name, grid=(r2 // tr,), in_specs=[pl.BlockSpec((N_CHIPS, tr, cdim), lambda i: (0, i, 0))],
        out_specs=pl.BlockSpec((tr, cdim), lambda i: (i, 0)), out_shape=S((r2, cdim), F32), compiler_params=_params(1))(parts)


def join_copies(src, dst, where, send_sems, recv_sems):
    x, y, c, _ = _place()
    pairs = []
    for i in range(len(src)):
        def copy(half_of):
            r2 = src[i].shape[0]
            rows = pl.ds(pl.multiple_of(where[i][1] + half_of * r2, 8), r2)
            return pltpu.make_async_remote_copy(
                src_ref=src[i], dst_ref=dst[where[i][0]].at[rows], send_sem=send_sems.at[i],
                recv_sem=recv_sems.at[i], device_id=(x, y, 1 - c), device_id_type=MESH)
        pairs.append((copy(c), copy(1 - c)))
    return pairs


def _fill_own_halves(targets, halves, where):
    targets = list(targets)
    c = lax.axis_index("c")
    for h, (tgt, first) in zip(halves, where):
        targets[tgt] = lax.dynamic_update_slice(targets[tgt], h, (first + c * h.shape[0], 0))
    return targets


def sibling_join_halves(name, halves, targets, where):
    n = len(halves)

    def body(*refs):
        pairs = join_copies(refs[:n], refs[n:n + len(targets)], where, *refs[n + len(targets):])
        for outgoing, _ in pairs:
            outgoing.start()
        for _, incoming in pairs:
            incoming.wait_recv()
        for outgoing, _ in pairs:
            outgoing.wait_send()

    outs = _hbm_call(
        body, name=name, in_specs=[ANY] * n, out_specs=[ANY] * len(targets), out_shape=[S(tg, F32) for tg in targets],
        scratch_shapes=[pltpu.SemaphoreType.DMA((n,)), pltpu.SemaphoreType.DMA((n,))],
    )(*halves)
    return _fill_own_halves(outs, halves, where)


def all_reduce_small(name, packed):
    rows, width = packed.shape

    def body(x_ref, o_ref, gathered, send_sems, recv_sems):
        x, y, c, _ = _place()
        me = 4 * x + 2 * y + c
        gathered[me] = x_ref[...]
        flips = [(fx, fy, fc) for fx in (0, 1) for fy in (0, 1) for fc in (0, 1)][1:]

        def copy(r, slot, to):
            return pltpu.make_async_remote_copy(
                src_ref=x_ref, dst_ref=gathered.at[slot], send_sem=send_sems.at[r], recv_sem=recv_sems.at[r],
                device_id=to, device_id_type=MESH)

        def peer(f):
            return (x ^ f[0], y ^ f[1], c ^ f[2])

        sent = [copy(r, me, peer(f)) for r, f in enumerate(flips)]
        for cp in sent:
            cp.start()
        for r, f in enumerate(flips):
            px, py, pc = peer(f)
            copy(r, 4 * px + 2 * py + pc, peer(f)).wait_recv()
        for cp in sent:
            cp.wait_send()
        acc = gathered[0]
        for k in range(1, N_DEV):
            acc = acc + gathered[k]
        o_ref[...] = acc

    vmem = pl.BlockSpec(memory_space=pltpu.VMEM)
    return pl.pallas_call(
        body, name=name, in_specs=[vmem], out_specs=vmem, out_shape=S((rows, width), F32),
        scratch_shapes=[pltpu.VMEM((N_DEV, rows, width), F32), pltpu.SemaphoreType.DMA((N_DEV - 1,)),
                        pltpu.SemaphoreType.DMA((N_DEV - 1,))],
    )(packed)


def _rope_tables(positions):
    inv_freq = 1.0 / (ROPE_THETA ** (jnp.arange(0, ROPE, 2, dtype=F32) / ROPE))
    ang = positions.astype(F32)[:, None] * inv_freq
    return jnp.cos(ang), jnp.sin(ang)


def _unstack_cols(w):
    k4, k, n4 = w.shape
    return jnp.transpose(w, (1, 0, 2)).reshape(k, k4 * n4)


def _stack_cols(w):
    k, n = w.shape
    return jnp.transpose(w.reshape(k, N_CHIPS, n // N_CHIPS), (1, 0, 2))


def kernel(x, positions, mla_norm, mla_w_in, mla_g_cq, mla_g_ckv, mla_w_uq, mla_w_ukv, mla_w_o, conv_norm, conv_w_in, conv_w, conv_w_out, ffn_norm, ffn_w_gate, ffn_w_up, ffn_w_down, final_norm, loss_target, m_mla_norm, m_mla_w_in, m_mla_g_cq, m_mla_g_ckv, m_mla_w_uq, m_mla_w_ukv, m_mla_w_o, m_conv_norm, m_conv_w_in, m_conv_w, m_conv_w_out, m_ffn_norm, m_ffn_w_gate, m_ffn_w_up, m_ffn_w_down, m_final_norm, v_mla_norm, v_mla_w_in, v_mla_g_cq, v_mla_g_ckv, v_mla_w_uq, v_mla_w_ukv, v_mla_w_o, v_conv_norm, v_conv_w_in, v_conv_w, v_conv_w_out, v_ffn_norm, v_ffn_w_gate, v_ffn_w_up, v_ffn_w_down, v_final_norm):
    weights = dict(mla_norm=mla_norm, mla_w_in=mla_w_in, mla_g_cq=mla_g_cq, mla_g_ckv=mla_g_ckv, mla_w_uq=mla_w_uq,
                   mla_w_ukv=mla_w_ukv, mla_w_o=mla_w_o, conv_norm=conv_norm, conv_w_in=conv_w_in, conv_w=conv_w,
                   conv_w_out=conv_w_out, ffn_norm=ffn_norm, ffn_w_gate=ffn_w_gate, ffn_w_up=ffn_w_up,
                   ffn_w_down=ffn_w_down, final_norm=final_norm)
    m_in = dict(mla_norm=m_mla_norm, mla_w_in=m_mla_w_in, mla_g_cq=m_mla_g_cq, mla_g_ckv=m_mla_g_ckv, mla_w_uq=m_mla_w_uq,
                mla_w_ukv=m_mla_w_ukv, mla_w_o=m_mla_w_o, conv_norm=m_conv_norm, conv_w_in=m_conv_w_in, conv_w=m_conv_w,
                conv_w_out=m_conv_w_out, ffn_norm=m_ffn_norm, ffn_w_gate=m_ffn_w_gate, ffn_w_up=m_ffn_w_up,
                ffn_w_down=m_ffn_w_down, final_norm=m_final_norm)
    v_in = dict(mla_norm=v_mla_norm, mla_w_in=v_mla_w_in, mla_g_cq=v_mla_g_cq, mla_g_ckv=v_mla_g_ckv, mla_w_uq=v_mla_w_uq,
                mla_w_ukv=v_mla_w_ukv, mla_w_o=v_mla_w_o, conv_norm=v_conv_norm, conv_w_in=v_conv_w_in, conv_w=v_conv_w,
                conv_w_out=v_conv_w_out, ffn_norm=v_ffn_norm, ffn_w_gate=v_ffn_w_gate, ffn_w_up=v_ffn_w_up,
                ffn_w_down=v_ffn_w_down, final_norm=v_final_norm)
    big = ["mla_w_in", "mla_w_uq", "mla_w_ukv", "mla_w_o", "conv_w_in", "conv_w_out", "ffn_w_gate", "ffn_w_up", "ffn_w_down"]
    order = list(weights)

    t, d = x.shape[1], x.shape[2]
    h0 = x.reshape(t, d)
    target = loss_target.reshape(t, d)
    cos, sin = _rope_tables(positions.reshape(t))

    def rows2d(a):
        return a.reshape(-1, a.shape[-1])

    first, later = big[:4], big[4:]
    shards = {n: rows2d(weights[n]).astype(BF16) for n in big}
    gathered = dict(zip(first, gather_weight_shards([shards[n] for n in first])))
    w_in = gathered["mla_w_in"].reshape(-1, gathered["mla_w_in"].shape[-1])
    w_uq = _unstack_cols(gathered["mla_w_uq"])
    w_ukv = _unstack_cols(gathered["mla_w_ukv"])
    w_o = gathered["mla_w_o"].reshape(-1, d)

    chip = 2 * lax.axis_index("x") + lax.axis_index("y")
    core = lax.axis_index("c")
    d4 = d // N_CHIPS
    first_core = (core == 0).astype(F32)

    def place_shard(shard):
        full = jnp.zeros((shard.shape[0], d), F32)
        return lax.dynamic_update_slice(full, shard * first_core, (0, chip * d4))

    def pack_rows(rows):
        idx = lax.broadcasted_iota(jnp.int32, (SMALL_ROWS, d), 0)
        out = jnp.zeros((SMALL_ROWS, d), F32)
        for r, row in enumerate(rows):
            out = out + jnp.where(idx == r, row, 0.0)
        return out

    cw = place_shard(conv_w.reshape(3, d4))
    pre = all_reduce_small("all_gather_conv_small", pack_rows([place_shard(conv_norm.reshape(1, d4)), cw[0:1], cw[1:2], cw[2:3]]))
    conv_norm_full = pre[0:1]
    conv_w_full = pre[1:4]

    a0 = rms_fwd("mla_norm_fwd", h0, mla_norm)
    proj, cq, ckv, kr = mla_in_proj("mla_in_proj", a0, w_in, mla_g_cq, mla_g_ckv, cos, sin)
    q = linear("mla_q_up", cq, w_uq, F32)
    kv = linear("mla_kv_up", ckv, w_ukv, BF16)
    qh, kh, vh, conv_arriving = qkv_heads("qkv_heads", q, kv, kr, cos, sin, [shards[n] for n in later[:2]])
    attn, lse, ffn_arriving = attention_fwd("attention_fwd", qh, kh, vh, [shards[n] for n in later[2:]])
    h1, a1, handed = attention_out_proj("mla_out_proj", attn, w_o, h0, ffn_norm[0:1], conv_arriving + ffn_arriving)
    gathered.update(zip(later, _fill_own_slot(handed, [shards[n][None] for n in later])))
    cw_in = _unstack_cols(gathered["conv_w_in"])
    cw_out = gathered["conv_w_out"].reshape(-1, d)
    wg_all, wu_all, wd_all = gathered["ffn_w_gate"], gathered["ffn_w_up"], gathered["ffn_w_down"]

    def ffn_forward(tag, h, a, layer, next_gain):
        g, u, z = ffn_up(f"ffn{tag}_up", a, wg_all, wu_all, layer)
        return g, u, z, ffn_down(f"ffn{tag}_down", z, wd_all, layer, h, next_gain)

    g0, u0, z0, (h2, a2) = ffn_forward(0, h1, a1, 0, conv_norm_full)
    bcx = conv_in_proj("conv_in_proj", a2, cw_in)
    yc = conv_fwd("conv_fwd", bcx, conv_w_full)
    h3, a3 = linear("conv_out_proj", yc, cw_out, F32, resid=h2, next_gain=ffn_norm[1:2])
    g1, u1, z1 = ffn_up("ffn1_up", a3, wg_all, wu_all, 1)
    dh4, d_final_norm, loss_local = ffn_down_loss("ffn1_down_loss", z1, wd_all, 1, h3, final_norm.reshape(1, d), target)

    def ffn_backward(tag, dh, h, layer, a, g, u, z):
        dg, du = ffn_bwd_hidden(f"ffn{tag}_bwd_hidden", dh, wd_all, layer, g, u)
        d_wd = ffn_wgrad_down(f"ffn{tag}_wgrad_down", z, dh)
        dh_prev, d_norm = ffn_bwd_input(f"ffn{tag}_bwd_input", dg, du, wg_all, wu_all, layer, h, ffn_norm[layer:layer + 1], dh)
        d_wg = ffn_wgrad_up(f"ffn{tag}_wgrad_gate", a, dg)
        d_wu = ffn_wgrad_up(f"ffn{tag}_wgrad_up", a, du)
        return dh_prev, d_norm, [d_wg, d_wu, d_wd]

    def reduce_to_pair_sums(tag, local):
        from_sibling = sibling_swap_halves(f"sibling_swap_{tag}", local)
        return [add_halves(f"pair_sum_{tag}{i}", g, r) for i, (g, r) in enumerate(zip(local, from_sibling))]

    def sum_from_chips(tag, pair_sums, arrived):
        from_chips = _fill_own_slot(arrived, _own_slots(pair_sums))
        return [sum_chips(f"chip_sum_{tag}{i}", p) for i, p in enumerate(from_chips)]

    def shard_shape(n):
        return rows2d(weights[n]).shape

    dh3, d_ffn_norm1, ffn1_grads = ffn_backward(1, dh4, h3, 1, a3, g1, u1, z1)

    dyc = linear_nt("conv_out_bwd_input", dh3, cw_out, F32)
    d_cw_out = wgrad("conv_out_wgrad", yc, dh3)
    dbcx, d_conv_w = conv_bwd("conv_bwd", bcx, conv_w_full, dyc)
    dh2, d_conv_norm = conv_in_bwd_input("conv_in_bwd_input", dbcx, cw_in, h2, conv_norm_full, dh3)
    d_cw_in = conv_in_wgrad("conv_in_wgrad", a2, dbcx)

    dh1, d_ffn_norm0, ffn0_grads = ffn_backward(0, dh2, h1, 0, a1, g0, u0, z0)

    d_attn, delta = attention_out_bwd("mla_out_bwd_input", dh1, w_o, attn)
    d_w_o = wgrad("mla_out_wgrad", attn, dh1)
    rest_pairs = reduce_to_pair_sums("rest", [d_cw_in, d_cw_out.reshape(N_CHIPS, -1, d)] + ffn1_grads + ffn0_grads
                                     + [d_w_o.reshape(N_CHIPS, -1, d)])
    dqh, dkh, dvh, rest_arrived = attention_bwd("attention_bwd", qh, kh, vh, d_attn, lse, delta, rest_pairs)
    rd, rf = ffn0_grads[0].shape[1], ffn0_grads[2].shape[1]
    rest_where = [(0, 0), (1, 0), (2, rd), (3, rd), (4, rf), (2, 0), (3, 0), (4, 0), (5, 0)]
    rest_names = later + ["mla_w_o"]
    dq, dkv, dkr, rest_grads = qkv_heads_bwd("qkv_heads_bwd", dqh, dkh, dvh, cos, sin, sum_from_chips("rest", rest_pairs, rest_arrived),
                                              [shard_shape(n) for n in rest_names], rest_where)
    grads = dict(zip(rest_names, rest_grads))
    dcq = linear_nt("mla_q_up_bwd_input", dq, w_uq, F32)
    d_w_uq = wgrad("mla_q_up_wgrad", cq, dq)
    dckv = linear_nt("mla_kv_up_bwd_input", dkv, w_ukv, F32)
    d_w_ukv = wgrad("mla_kv_up_wgrad", ckv, dkv)
    dproj, d_g_cq, d_g_ckv = mla_mid_bwd("mla_mid_bwd", proj, mla_g_cq, mla_g_ckv, dcq, dckv, dkr, cos, sin)
    d_w_in = wgrad("mla_in_wgrad", a0, dproj)
    mla_pairs = reduce_to_pair_sums("mla", [d_w_in.reshape(N_CHIPS, -1, d_w_in.shape[-1]), _stack_cols(d_w_uq), _stack_cols(d_w_ukv)])
    grad_x, d_mla_norm, mla_arrived = linear_nt_norm_bwd("mla_in_bwd_input", dproj, w_in, h0, mla_norm, dh1, mla_pairs)

    grads.update(zip(first[:3], sibling_join_halves("sibling_join_mla", sum_from_chips("mla", mla_pairs, mla_arrived),
                                                    [shard_shape(n) for n in first[:3]], [(i, 0) for i in range(3)])))

    def pad_row(v):
        return jnp.pad(v, ((0, 0), (0, d - v.shape[1])))

    small = all_reduce_small("all_reduce_small_grads", pack_rows([
        d_mla_norm, pad_row(d_g_cq), pad_row(d_g_ckv), d_ffn_norm0, d_ffn_norm1, d_final_norm, d_conv_norm,
        d_conv_w[0:1], d_conv_w[1:2], d_conv_w[2:3], jnp.broadcast_to(loss_local, (1, d))]))
    loss = small[10, 0]
    grads["mla_norm"] = small[0:1]
    grads["mla_g_cq"] = small[1:2, :mla_g_cq.shape[1]]
    grads["mla_g_ckv"] = small[2:3, :mla_g_ckv.shape[1]]
    grads["ffn_norm"] = small[3:5]
    grads["final_norm"] = small[5:6]
    grads["conv_norm"] = lax.dynamic_slice(small[6:7], (0, chip * d4), (1, d4))
    grads["conv_w"] = lax.dynamic_slice(small[7:10], (0, chip * d4), (3, d4))

    outs_g, outs_d, outs_m, outs_v = [], [], [], []
    for n in order:
        w = weights[n]
        if w.ndim == 3 and w.shape[2] % 128 and w.shape[1] % 128 == 0:
            results = adamw_swapped(f"adamw_{n}", jnp.swapaxes(w, 1, 2), grads[n].reshape(-1, w.shape[2]),
                                    jnp.swapaxes(m_in[n], 1, 2), jnp.swapaxes(v_in[n], 1, 2))
            grad_w, delta_w, new_m, new_v = [jnp.swapaxes(o, 1, 2) for o in results]
        else:
            delta_w, new_m, new_v = adamw(f"adamw_{n}", rows2d(w), grads[n].reshape(rows2d(w).shape), rows2d(m_in[n]), rows2d(v_in[n]))
            grad_w = grads[n]
        outs_g.append(grad_w.reshape(w.shape))
        outs_d.append(delta_w.reshape(w.shape))
        outs_m.append(new_m.reshape(w.shape))
        outs_v.append(new_v.reshape(w.shape))
    return (loss, grad_x.reshape(x.shape), *outs_g, *outs_d, *outs_m, *outs_v)
```

```python
import math

import jax
import jax.numpy as jnp
from jax import lax
from jax.experimental import pallas as pl
from jax.experimental.pallas import tpu as pltpu

F32 = jnp.float32
BF16 = jnp.bfloat16
S = jax.ShapeDtypeStruct

N_HEADS = 8
NOPE = 128
ROPE = 64
HALF = ROPE // 2
VDIM = 128
QK = NOPE + ROPE
CHUNK = 64
ROPE_THETA = 10000.0
RMS_EPS = 1e-6
ADAM_LR = 0.001
ADAM_B1 = 0.9
ADAM_B2 = 0.999
ADAM_EPS = 1e-08
ADAM_WD = 0.01
ADAM_STEP = 10

N_CHIPS = 4
N_DEV = 8
MASK_VALUE = -1e30
SCORE_SCALE = 1.0 / math.sqrt(QK)
LOG2_E = math.log2(math.e)
SCORE_SCALE_LOG2 = SCORE_SCALE * LOG2_E
VMEM_LIMIT = 48 * 1024 * 1024
VMEM_LIMIT_WHOLE_HEAD = 58 * 1024 * 1024
ATT_BLOCK = 512
SMALL_ROWS = 16

_NN = (((1,), (0,)), ((), ()))
_NT = (((1,), (1,)), ((), ()))
_TN = (((0,), (0,)), ((), ()))
MESH = pl.DeviceIdType.MESH
ANY = pl.BlockSpec(memory_space=pl.ANY)


def _params(n_axes, vmem_limit=VMEM_LIMIT):
    return pltpu.CompilerParams(dimension_semantics=("arbitrary",) * n_axes, vmem_limit_bytes=vmem_limit)


def _tile(n, cap, mult=8):
    for t in range(min(cap, n), 0, -1):
        if n % t == 0 and t % mult == 0:
            return t
    return n


def _sigmoid(x):
    return 0.5 * jnp.tanh(0.5 * x) + 0.5


def _mm(name, a_ops, b_ops, products, dims, grid, k_axis, outs, acc_shape, epilogue, extra_ops=()):
    na, nb, ne, no = len(a_ops), len(b_ops), len(extra_ops), len(outs)
    n_acc = 1 + max(c for _, _, c in products)
    nk = 1 if k_axis is None else grid[k_axis]

    def body(*refs):
        a_refs = refs[:na]
        b_refs = refs[na:na + nb]
        e_refs = refs[na + nb:na + nb + ne]
        o_refs = refs[na + nb + ne:na + nb + ne + no]
        acc_refs = refs[na + nb + ne + no:]

        def partial_sums():
            vals = [None] * n_acc
            for ai, bi, ci in products:
                d = lax.dot_general(a_refs[ai][...].astype(BF16), b_refs[bi][...].astype(BF16), dims,
                                    preferred_element_type=F32)
                vals[ci] = d if vals[ci] is None else vals[ci] + d
            return vals

        if nk == 1:
            epilogue(partial_sums(), e_refs, o_refs)
        else:
            k = pl.program_id(k_axis)

            @pl.when(k == 0)
            def _():
                for acc in acc_refs:
                    acc[...] = jnp.zeros_like(acc)

            for acc, v in zip(acc_refs, partial_sums()):
                acc[...] += v

            @pl.when(k == nk - 1)
            def _():
                epilogue([acc[...] for acc in acc_refs], e_refs, o_refs)

    ops = list(a_ops) + list(b_ops) + list(extra_ops)
    return pl.pallas_call(
        body, name=name, grid=grid,
        in_specs=[s for _, s in ops], out_specs=[s for _, s in outs], out_shape=[o for o, _ in outs],
        scratch_shapes=[pltpu.VMEM(acc_shape, F32) for _ in range(n_acc if nk > 1 else 0)],
        compiler_params=_params(len(grid)),
    )(*[a for a, _ in ops])


def _store(accs, e_refs, o_refs):
    o_refs[0][...] = accs[0].astype(o_refs[0].dtype)


def linear(name, x, w, out_dtype, resid=None, next_gain=None):
    t, k = x.shape
    n = w.shape[1]
    tm = _tile(t, 512)
    tn = n if n <= 2048 else _tile(n, 1024, 128)
    tile = pl.BlockSpec((tm, tn), lambda j, i: (i, j))
    extra = [] if resid is None else [(resid, tile)]
    outs = [(S((t, n), out_dtype), tile)]
    if next_gain is not None:
        assert tn == n
        extra.append((next_gain, pl.BlockSpec((1, n), lambda j, i: (0, 0))))
        outs.append((S((t, n), BF16), tile))

    def epilogue(accs, e_refs, o_refs):
        y = accs[0] if resid is None else e_refs[0][...] + accs[0]
        o_refs[0][...] = y.astype(out_dtype)
        if next_gain is not None:
            o_refs[1][...] = (y * _rstd(y) * e_refs[-1][...]).astype(BF16)

    res = _mm(name, [(x, pl.BlockSpec((tm, k), lambda j, i: (i, 0)))], [(w, pl.BlockSpec((k, tn), lambda j, i: (0, j)))],
              [(0, 0, 0)], _NN, (n // tn, t // tm), None, outs, None, epilogue, extra)
    return res[0] if next_gain is None else res


def linear_nt(name, dy, w, out_dtype):
    t, n = dy.shape
    k = w.shape[0]
    tm = _tile(t, 512)
    tc = n if n <= 2048 else _tile(n, 1024, 128)
    return _mm(name, [(dy, pl.BlockSpec((tm, tc), lambda i, c: (i, c)))], [(w, pl.BlockSpec((k, tc), lambda i, c: (0, c)))],
               [(0, 0, 0)], _NT, (t // tm, n // tc), 1,
               [(S((t, k), out_dtype), pl.BlockSpec((tm, k), lambda i, c: (i, 0)))], (tm, k), _store)[0]


def wgrad(name, x, dy):
    t, k = x.shape
    n = dy.shape[1]
    tk = _tile(t, 512)
    tn = n if n <= 1024 else _tile(n, 1024, 128)
    return _mm(name, [(x, pl.BlockSpec((tk, k), lambda j, s: (s, 0)))], [(dy, pl.BlockSpec((tk, tn), lambda j, s: (s, j)))],
               [(0, 0, 0)], _TN, (n // tn, t // tk), 1,
               [(S((k, n), BF16), pl.BlockSpec((k, tn), lambda j, s: (0, j)))], (k, tn), _store)[0]


def _resident(shape, index_map):
    return pl.BlockSpec(shape, index_map, pipeline_mode=pl.Buffered(1))


def ffn_up(name, a, wg_all, wu_all, layer):
    t, d = a.shape
    f4 = wg_all.shape[2]
    tm = _tile(t, 512)
    w_spec = _resident((N_CHIPS, d, f4), lambda i: (0, layer, 0))
    h_spec = pl.BlockSpec((N_CHIPS, tm, f4), lambda i: (0, i, 0))

    def body(a_ref, wg_ref, wu_ref, zg_ref, zu_ref, z_ref):
        av = a_ref[...]
        for k in range(N_CHIPS):
            g = jnp.dot(av, wg_ref[k], preferred_element_type=F32)
            u = jnp.dot(av, wu_ref[k], preferred_element_type=F32)
            sg = _sigmoid(g)
            silu = g * sg
            zg_ref[k] = (u * (sg * (1.0 + g * (1.0 - sg)))).astype(BF16)
            zu_ref[k] = silu.astype(BF16)
            z_ref[k] = (silu * u).astype(BF16)

    return pl.pallas_call(
        body, name=name, grid=(t // tm,), in_specs=[pl.BlockSpec((tm, d), lambda i: (i, 0)), w_spec, w_spec],
        out_specs=[h_spec] * 3, out_shape=[S((N_CHIPS, t, f4), BF16)] * 3, compiler_params=_params(1))(a, wg_all, wu_all)


def ffn_down(name, z, wd_all, layer, resid, next_gain=None):
    _, t, f4 = z.shape
    d = wd_all.shape[2]
    tm = _tile(t, 512)
    row = pl.BlockSpec((tm, d), lambda i: (i, 0))
    normed = next_gain is not None

    def body(z_ref, wd_ref, r_ref, *refs):
        acc = r_ref[...]
        for k in range(N_CHIPS):
            acc = acc + jnp.dot(z_ref[k], wd_ref[k], preferred_element_type=F32)
        refs[-2 if normed else -1][...] = acc
        if normed:
            refs[-1][...] = (acc * _rstd(acc) * refs[0][...]).astype(BF16)

    res = pl.pallas_call(
        body, name=name, grid=(t // tm,),
        in_specs=[pl.BlockSpec((N_CHIPS, tm, f4), lambda i: (0, i, 0)), _resident((N_CHIPS, f4, d), lambda i: (0, layer, 0)), row]
        + ([pl.BlockSpec((1, d), lambda i: (0, 0))] if normed else []),
        out_specs=[row] * (2 if normed else 1), out_shape=[S((t, d), F32)] + ([S((t, d), BF16)] if normed else []),
        compiler_params=_params(1))(z, wd_all, resid, *([next_gain] if normed else []))
    return res if normed else res[0]


def ffn_bwd_hidden(name, dh, wd_all, layer, zg, zu, swap=()):
    t, d = dh.shape
    f4 = zg.shape[2]
    tm = _tile(t, 512)
    h_spec = pl.BlockSpec((N_CHIPS, tm, f4), lambda i: (0, i, 0))
    n = len(swap)

    def body(dh_ref, wd_ref, zg_ref, zu_ref, *refs):
        dg_ref, du_ref = refs[n:n + 2]
        if n:
            _ride_along(swap_copies(refs[:n], refs[n + 2:2 * n + 2], *refs[2 * n + 2:]), (pl.program_id(0),), (t // tm,))
        dhb = dh_ref[...].astype(BF16)
        for k in range(N_CHIPS):
            dz = lax.dot_general(dhb, wd_ref[k], _NT, preferred_element_type=F32)
            dg_ref[k] = (dz * zg_ref[k].astype(F32)).astype(BF16)
            du_ref[k] = (dz * zu_ref[k].astype(F32)).astype(BF16)

    outs = pl.pallas_call(
        body, name=name, grid=(t // tm,),
        in_specs=[pl.BlockSpec((tm, d), lambda i: (i, 0)), _resident((N_CHIPS, f4, d), lambda i: (0, layer, 0)), h_spec, h_spec]
        + [ANY] * n,
        out_specs=[h_spec] * 2 + [ANY] * n, out_shape=[S((N_CHIPS, t, f4), BF16)] * 2 + _swap_shapes(swap),
        scratch_shapes=[pltpu.SemaphoreType.DMA((n,)), pltpu.SemaphoreType.DMA((n,))] if n else [],
        compiler_params=_params(1))(dh, wd_all, zg, zu, *swap)
    return outs[0], outs[1], list(outs[2:])


def _norm_bwd_specs(tm, d):
    row = pl.BlockSpec((tm, d), lambda i: (i, 0))
    vec = pl.BlockSpec((1, d), lambda i: (0, 0))
    return [row, vec, row], [row, vec]


def _norm_bwd_tail(da, h_ref, g_ref, dhi_ref, dho_ref, dgain_ref):
    dx, dgain = _rms_bwd(h_ref[...], g_ref[...], da)
    dho_ref[...] = dhi_ref[...] + dx

    @pl.when(pl.program_id(0) == 0)
    def _():
        dgain_ref[...] = jnp.zeros_like(dgain_ref)

    dgain_ref[...] += dgain


def ffn_bwd_input(name, dg, du, wg_all, wu_all, layer, h, gain, dh_in):
    _, t, f4 = dg.shape
    d = h.shape[1]
    tm = _tile(t, 512)
    h_spec = pl.BlockSpec((N_CHIPS, tm, f4), lambda i: (0, i, 0))
    w_spec = _resident((N_CHIPS, d, f4), lambda i: (0, layer, 0))
    tail_in, tail_out = _norm_bwd_specs(tm, d)

    def body(dg_ref, du_ref, wg_ref, wu_ref, *tail):
        acc = jnp.zeros((tm, d), F32)
        for k in range(N_CHIPS):
            acc = acc + lax.dot_general(dg_ref[k], wg_ref[k], _NT, preferred_element_type=F32)
            acc = acc + lax.dot_general(du_ref[k], wu_ref[k], _NT, preferred_element_type=F32)
        _norm_bwd_tail(acc, *tail)

    return pl.pallas_call(
        body, name=name, grid=(t // tm,), in_specs=[h_spec, h_spec, w_spec, w_spec] + tail_in, out_specs=tail_out,
        out_shape=[S((t, d), F32), S((1, d), F32)], compiler_params=_params(1))(dg, du, wg_all, wu_all, h, gain, dh_in)


def ffn_wgrad_up(name, a, dy):
    t, d = a.shape
    f4 = dy.shape[2]
    tk = _tile(t, 512)
    nt = t // tk

    def body(a_ref, dy_ref, o_ref, acc):
        s = pl.program_id(0)

        @pl.when(s == 0)
        def _():
            acc[...] = jnp.zeros_like(acc)

        at = a_ref[...].T
        for k in range(N_CHIPS):
            acc[k] += jnp.dot(at, dy_ref[k], preferred_element_type=F32)

        @pl.when(s == nt - 1)
        def _():
            o_ref[...] = acc[...].astype(BF16)

    return pl.pallas_call(
        body, name=name, grid=(nt,),
        in_specs=[pl.BlockSpec((tk, d), lambda s: (s, 0)), pl.BlockSpec((N_CHIPS, tk, f4), lambda s: (0, s, 0))],
        out_specs=pl.BlockSpec((N_CHIPS, d, f4), lambda s: (0, 0, 0)), out_shape=S((N_CHIPS, d, f4), BF16),
        scratch_shapes=[pltpu.VMEM((N_CHIPS, d, f4), F32)], compiler_params=_params(1))(a, dy)


def ffn_wgrad_down(name, z, dh):
    _, t, f4 = z.shape
    d = dh.shape[1]
    tk = _tile(t, 512)
    nt = t // tk

    def body(z_ref, dh_ref, o_ref, acc):
        s = pl.program_id(0)

        @pl.when(s == 0)
        def _():
            acc[...] = jnp.zeros_like(acc)

        dhb = dh_ref[...].astype(BF16)
        for k in range(N_CHIPS):
            acc[k] += lax.dot_general(z_ref[k], dhb, _TN, preferred_element_type=F32)

        @pl.when(s == nt - 1)
        def _():
            o_ref[...] = acc[...].astype(BF16)

    return pl.pallas_call(
        body, name=name, grid=(nt,),
        in_specs=[pl.BlockSpec((N_CHIPS, tk, f4), lambda s: (0, s, 0)), pl.BlockSpec((tk, d), lambda s: (s, 0))],
        out_specs=pl.BlockSpec((N_CHIPS, f4, d), lambda s: (0, 0, 0)), out_shape=S((N_CHIPS, f4, d), BF16),
        scratch_shapes=[pltpu.VMEM((N_CHIPS, f4, d), F32)], compiler_params=_params(1))(z, dh)


def conv_in_proj(name, a, w):
    t, d = a.shape
    tm = _tile(t, 512)
    return _mm(name, [(a, pl.BlockSpec((tm, d), lambda j, i: (i, 0)))], [(w, pl.BlockSpec((d, d), lambda j, i: (0, j)))],
               [(0, 0, 0)], _NN, (3, t // tm), None,
               [(S((3, t, d), F32), pl.BlockSpec((None, tm, d), lambda j, i: (j, i, 0)))], None, _store)[0]


def conv_in_bwd_input(name, dbcx, w, h, gain, dh_in):
    _, t, d = dbcx.shape
    tm = _tile(t, 512)
    tail_in, tail_out = _norm_bwd_specs(tm, d)

    def body(g_ref, w_ref, *tail):
        acc = jnp.zeros((tm, d), F32)
        for j in range(3):
            acc = acc + lax.dot_general(g_ref[j], w_ref[:, j * d:(j + 1) * d], _NT, preferred_element_type=F32)
        _norm_bwd_tail(acc, *tail)

    return pl.pallas_call(
        body, name=name, grid=(t // tm,),
        in_specs=[pl.BlockSpec((3, tm, d), lambda i: (0, i, 0)), _resident((d, 3 * d), lambda i: (0, 0))] + tail_in,
        out_specs=tail_out, out_shape=[S((t, d), F32), S((1, d), F32)], compiler_params=_params(1))(dbcx, w, h, gain, dh_in)


def linear_nt_norm_bwd(name, dy, w, h, gain, dh_in, parts=()):
    t, n = dy.shape
    k = w.shape[0]
    tm = _tile(t, 512)
    tail_in, tail_out = _norm_bwd_specs(tm, k)
    m = len(parts)

    def body(dy_ref, w_ref, h_ref, g_ref, dhi_ref, *refs):
        if m:
            _ride_along(scatter_ici_copies(refs[:m], refs[m + 2:2 * m + 2], *refs[2 * m + 2:]), (pl.program_id(0),), (t // tm,))
        da = lax.dot_general(dy_ref[...].astype(BF16), w_ref[...], _NT, preferred_element_type=F32)
        _norm_bwd_tail(da, h_ref, g_ref, dhi_ref, *refs[m:m + 2])

    outs = pl.pallas_call(
        body, name=name, grid=(t // tm,),
        in_specs=[pl.BlockSpec((tm, n), lambda i: (i, 0)), _resident((k, n), lambda i: (0, 0))] + tail_in + [ANY] * m,
        out_specs=tail_out + [ANY] * m, out_shape=[S((t, k), F32), S((1, k), F32)] + [S(p.shape, p.dtype) for p in parts],
        scratch_shapes=[pltpu.SemaphoreType.DMA((m, 3)), pltpu.SemaphoreType.DMA((m, 3))] if m else [],
        compiler_params=_params(1))(dy, w, h, gain, dh_in, *parts)
    return outs[0], outs[1], list(outs[2:])


def conv_in_wgrad(name, a, dbcx):
    t, d = a.shape
    tk = _tile(t, 512)
    nt = t // tk
    n4 = 3 * d // N_CHIPS

    def body(a_ref, g_ref, o_ref, acc):
        s = pl.program_id(0)

        @pl.when(s == 0)
        def _():
            acc[...] = jnp.zeros_like(acc)

        at = a_ref[...].T
        for j in range(3):
            acc[:, j * d:(j + 1) * d] += jnp.dot(at, g_ref[j], preferred_element_type=F32)

        @pl.when(s == nt - 1)
        def _():
            for k in range(N_CHIPS):
                o_ref[k] = acc[:, k * n4:(k + 1) * n4].astype(BF16)

    return pl.pallas_call(
        body, name=name, grid=(nt,),
        in_specs=[pl.BlockSpec((tk, d), lambda s: (s, 0)), pl.BlockSpec((3, tk, d), lambda s: (0, s, 0))],
        out_specs=pl.BlockSpec((N_CHIPS, d, n4), lambda s: (0, 0, 0)), out_shape=S((N_CHIPS, d, n4), BF16),
        scratch_shapes=[pltpu.VMEM((d, 3 * d), F32)], compiler_params=_params(1))(a, dbcx)


def _rstd(x):
    return lax.rsqrt(jnp.mean(x * x, axis=-1, keepdims=True) + RMS_EPS)


def _rms_bwd(x, g, dy):
    r = _rstd(x)
    xhat = x * r
    dgain = jnp.sum(dy * xhat, axis=0, keepdims=True)
    dxh = dy * g
    dx = r * (dxh - xhat * jnp.mean(dxh * xhat, axis=-1, keepdims=True))
    return dx, dgain


def rms_fwd(name, h, g):
    t, d = h.shape
    tr = _tile(t, 512)

    def body(h_ref, g_ref, a_ref):
        x = h_ref[...]
        a_ref[...] = (x * _rstd(x) * g_ref[...]).astype(BF16)

    return pl.pallas_call(
        body, name=name, grid=(t // tr,),
        in_specs=[pl.BlockSpec((tr, d), lambda i: (i, 0)), pl.BlockSpec((1, d), lambda i: (0, 0))],
        out_specs=pl.BlockSpec((tr, d), lambda i: (i, 0)), out_shape=S((t, d), BF16), compiler_params=_params(1))(h, g)


def ffn_down_loss(name, z, wd_all, layer, resid, gain, target):
    _, t, f4 = z.shape
    d = wd_all.shape[2]
    tm = _tile(t, 512)

    def body(z_ref, wd_ref, r_ref, g_ref, t_ref, dh_ref, dg_ref, loss_ref):
        x = r_ref[...]
        for k in range(N_CHIPS):
            x = x + jnp.dot(z_ref[k], wd_ref[k], preferred_element_type=F32)
        g = g_ref[...]
        r = _rstd(x)
        xhat = x * r
        err = xhat * g - t_ref[...]
        dy = err * (1.0 / d)
        dxh = dy * g
        dh_ref[...] = r * (dxh - xhat * jnp.mean(dxh * xhat, axis=-1, keepdims=True))

        @pl.when(pl.program_id(0) == 0)
        def _():
            dg_ref[...] = jnp.zeros_like(dg_ref)
            loss_ref[...] = jnp.zeros_like(loss_ref)

        dg_ref[...] += jnp.sum(dy * xhat, axis=0, keepdims=True)
        per_token = jnp.mean(err * err, axis=-1, keepdims=True)
        loss_ref[...] += 0.5 * jnp.sum(per_token, axis=0, keepdims=True)

    row = pl.BlockSpec((tm, d), lambda i: (i, 0))
    vec = pl.BlockSpec((1, d), lambda i: (0, 0))
    one = pl.BlockSpec((1, 1), lambda i: (0, 0))
    return pl.pallas_call(
        body, name=name, grid=(t // tm,),
        in_specs=[pl.BlockSpec((N_CHIPS, tm, f4), lambda i: (0, i, 0)), _resident((N_CHIPS, f4, d), lambda i: (0, layer, 0)), row, vec, row],
        out_specs=[row, vec, one], out_shape=[S((t, d), F32), S((1, d), F32), S((1, 1), F32)],
        compiler_params=_params(1))(z, wd_all, resid, gain, target)


def mla_in_proj(name, a, w, g_cq, g_ckv, cos, sin):
    t, d = a.shape
    n = w.shape[1]
    ql, kl = g_cq.shape[1], g_ckv.shape[1]
    tr = _tile(t, 512)

    def body(a_ref, w_ref, gq_ref, gk_ref, c_ref, s_ref, p_ref, cq_ref, ckv_ref, kr_ref):
        p_ref[...] = jnp.dot(a_ref[...], w_ref[...], preferred_element_type=F32)
        xq = p_ref[:, 0:ql]
        cq_ref[...] = (xq * _rstd(xq) * gq_ref[...]).astype(BF16)
        xk = p_ref[:, ql:ql + kl]
        ckv_ref[...] = (xk * _rstd(xk) * gk_ref[...]).astype(BF16)
        k1 = p_ref[:, ql + kl:ql + kl + HALF]
        k2 = p_ref[:, ql + kl + HALF:ql + kl + ROPE]
        c = c_ref[...]
        s = s_ref[...]
        kr_ref[:, 0:HALF] = k1 * c - k2 * s
        kr_ref[:, HALF:ROPE] = k1 * s + k2 * c

    def row(w):
        return pl.BlockSpec((tr, w), lambda i: (i, 0))

    def vec(w):
        return pl.BlockSpec((1, w), lambda i: (0, 0))

    return pl.pallas_call(
        body, name=name, grid=(t // tr,),
        in_specs=[row(d), _resident((d, n), lambda i: (0, 0)), vec(ql), vec(kl), row(HALF), row(HALF)],
        out_specs=[row(n), row(ql), row(kl), row(ROPE)],
        out_shape=[S((t, n), F32), S((t, ql), BF16), S((t, kl), BF16), S((t, ROPE), F32)],
        compiler_params=_params(1))(a, w, g_cq, g_ckv, cos, sin)


def mla_mid_bwd(name, proj, g_cq, g_ckv, dcq, dckv, dkr, cos, sin):
    t, n = proj.shape
    ql, kl = g_cq.shape[1], g_ckv.shape[1]
    tr = _tile(t, 512)

    def body(p_ref, gq_ref, gk_ref, dcq_ref, dckv_ref, dkr_ref, c_ref, s_ref, dp_ref, dgq_ref, dgk_ref):
        dxq, dgq = _rms_bwd(p_ref[:, 0:ql], gq_ref[...], dcq_ref[...])
        dp_ref[:, 0:ql] = dxq.astype(BF16)
        dxk, dgk = _rms_bwd(p_ref[:, ql:ql + kl], gk_ref[...], dckv_ref[...])
        dp_ref[:, ql:ql + kl] = dxk.astype(BF16)
        d1 = dkr_ref[:, 0:HALF]
        d2 = dkr_ref[:, HALF:ROPE]
        c = c_ref[...]
        s = s_ref[...]
        dp_ref[:, ql + kl:ql + kl + HALF] = (d1 * c + d2 * s).astype(BF16)
        dp_ref[:, ql + kl + HALF:ql + kl + ROPE] = (d2 * c - d1 * s).astype(BF16)

        @pl.when(pl.program_id(0) == 0)
        def _():
            dgq_ref[...] = jnp.zeros_like(dgq_ref)
            dgk_ref[...] = jnp.zeros_like(dgk_ref)

        dgq_ref[...] += dgq
        dgk_ref[...] += dgk

    def row(w):
        return pl.BlockSpec((tr, w), lambda i: (i, 0))

    def vec(w):
        return pl.BlockSpec((1, w), lambda i: (0, 0))

    return pl.pallas_call(
        body, name=name, grid=(t // tr,),
        in_specs=[row(n), vec(ql), vec(kl), row(ql), row(kl), row(ROPE), row(HALF), row(HALF)],
        out_specs=[row(n), vec(ql), vec(kl)], out_shape=[S((t, n), BF16), S((1, ql), F32), S((1, kl), F32)],
        compiler_params=_params(1))(proj, g_cq, g_ckv, dcq, dckv, dkr, cos, sin)


def qkv_heads(name, q, kv, kr, cos, sin, shards=()):
    t = q.shape[0]
    tr = _tile(t, 256)
    n = len(shards)

    def body(q_ref, kv_ref, kr_ref, c_ref, s_ref, *refs):
        src = refs[:n]
        qo_ref, ko_ref, vo_ref = refs[n:n + 3]
        if n:
            _ride_along(gather_ici_copies(src, refs[n + 3:2 * n + 3], *refs[2 * n + 3:]), (pl.program_id(0),), (t // tr,))
        c = c_ref[...]
        s = s_ref[...]
        krb = kr_ref[...].astype(BF16)
        for h in range(N_HEADS):
            q0 = h * QK
            qo_ref[h, :, 0:NOPE] = q_ref[:, q0:q0 + NOPE].astype(BF16)
            q1 = q_ref[:, q0 + NOPE:q0 + NOPE + HALF]
            q2 = q_ref[:, q0 + NOPE + HALF:q0 + QK]
            qo_ref[h, :, NOPE:NOPE + HALF] = (q1 * c - q2 * s).astype(BF16)
            qo_ref[h, :, NOPE + HALF:QK] = (q1 * s + q2 * c).astype(BF16)
            k0 = h * (NOPE + VDIM)
            ko_ref[h, :, 0:NOPE] = kv_ref[:, k0:k0 + NOPE]
            ko_ref[h, :, NOPE:QK] = krb
            vo_ref[h] = kv_ref[:, k0 + NOPE:k0 + NOPE + VDIM]

    def row(w):
        return pl.BlockSpec((tr, w), lambda i: (i, 0))

    def heads(w):
        return pl.BlockSpec((N_HEADS, tr, w), lambda i: (0, i, 0))

    outs = pl.pallas_call(
        body, name=name, grid=(t // tr,),
        in_specs=[row(N_HEADS * QK), row(N_HEADS * (NOPE + VDIM)), row(ROPE), row(HALF), row(HALF)] + [ANY] * n,
        out_specs=[heads(QK), heads(QK), heads(VDIM)] + [ANY] * n,
        out_shape=[S((N_HEADS, t, QK), BF16), S((N_HEADS, t, QK), BF16), S((N_HEADS, t, VDIM), BF16)]
        + [S((N_CHIPS,) + s.shape, s.dtype) for s in shards],
        scratch_shapes=[pltpu.SemaphoreType.DMA((n, 3)), pltpu.SemaphoreType.DMA((n, 3))] if n else [],
        compiler_params=_params(1))(q, kv, kr, cos, sin, *shards)
    return outs[0], outs[1], outs[2], list(outs[3:])


def qkv_heads_bwd(name, dq_h, dk_h, dv_h, cos, sin, halves=(), targets=(), where=()):
    t = dq_h.shape[1]
    tr = _tile(t, 256)
    n, nt = len(halves), len(targets)

    def body(dq_ref, dk_ref, dv_ref, c_ref, s_ref, *refs):
        q_ref, kv_ref, kr_ref = refs[n:n + 3]
        if n:
            _ride_along(join_copies(refs[:n], refs[n + 3:n + 3 + nt], where, *refs[n + 3 + nt:]), (pl.program_id(0),), (t // tr,))
        c = c_ref[...]
        s = s_ref[...]
        dkr = jnp.zeros((tr, ROPE), F32)
        for h in range(N_HEADS):
            q0 = h * QK
            q_ref[:, q0:q0 + NOPE] = dq_ref[h, :, 0:NOPE].astype(BF16)
            d1 = dq_ref[h, :, NOPE:NOPE + HALF]
            d2 = dq_ref[h, :, NOPE + HALF:QK]
            q_ref[:, q0 + NOPE:q0 + NOPE + HALF] = (d1 * c + d2 * s).astype(BF16)
            q_ref[:, q0 + NOPE + HALF:q0 + QK] = (d2 * c - d1 * s).astype(BF16)
            k0 = h * (NOPE + VDIM)
            kv_ref[:, k0:k0 + NOPE] = dk_ref[h, :, 0:NOPE].astype(BF16)
            kv_ref[:, k0 + NOPE:k0 + NOPE + VDIM] = dv_ref[h].astype(BF16)
            dkr = dkr + dk_ref[h, :, NOPE:QK]
        kr_ref[...] = dkr

    def row(w):
        return pl.BlockSpec((tr, w), lambda i: (i, 0))

    def heads(w):
        return pl.BlockSpec((N_HEADS, tr, w), lambda i: (0, i, 0))

    outs = pl.pallas_call(
        body, name=name, grid=(t // tr,),
        in_specs=[heads(QK), heads(QK), heads(VDIM), row(HALF), row(HALF)] + [ANY] * n,
        out_specs=[row(N_HEADS * QK), row(N_HEADS * (NOPE + VDIM)), row(ROPE)] + [ANY] * nt,
        out_shape=[S((t, N_HEADS * QK), BF16), S((t, N_HEADS * (NOPE + VDIM)), BF16), S((t, ROPE), F32)]
        + [S(tg, F32) for tg in targets],
        scratch_shapes=[pltpu.SemaphoreType.DMA((n,)), pltpu.SemaphoreType.DMA((n,))] if n else [],
        compiler_params=_params(1))(dq_h, dk_h, dv_h, cos, sin, *halves)
    return outs[0], outs[1], outs[2], _fill_own_halves(outs[3:], halves, where)


def _chunk_mask_t(q_start, k_start, bq, bk):
    kc = (k_start + lax.broadcasted_iota(jnp.int32, (bk, bq), 0)) // CHUNK
    qc = (q_start + lax.broadcasted_iota(jnp.int32, (bk, bq), 1)) // CHUNK
    return kc <= qc


def attention_fwd(name, q, k, v, shards=()):
    nh, t, _ = q.shape
    blk = ATT_BLOCK
    nq = t // blk
    n = len(shards)

    def body(q_ref, k_ref, v_ref, *refs):
        src = refs[:n]
        o_ref, lse_ref = refs[n:n + 2]
        dst = refs[n + 2:2 * n + 2]
        m_ref, l_ref, acc_ref, s_buf, p_buf, alpha_buf, bias_ref = refs[2 * n + 2:2 * n + 9]
        i = pl.program_id(1)
        if n:
            send_sems, recv_sems = refs[2 * n + 9:]
            _ride_along(gather_ici_copies(src, dst, send_sems, recv_sems), (pl.program_id(0), i), (nh, nq))

        @pl.when((pl.program_id(0) == 0) & (i == 0))
        def _():
            bias_ref[...] = jnp.where(_chunk_mask_t(0, 0, blk, blk), 0.0, MASK_VALUE)

        m_ref[...] = jnp.full_like(m_ref, MASK_VALUE)
        l_ref[...] = jnp.zeros_like(l_ref)
        acc_ref[...] = jnp.zeros_like(acc_ref)

        def rows(b):
            return pl.ds(pl.multiple_of(b * blk, blk), blk)

        def scores(b, slot):
            s_buf[slot] = lax.dot_general(k_ref[rows(b), :], q_ref[...], _NT, preferred_element_type=F32)

        def softmax(slot, diagonal):
            s = s_buf[slot]
            if diagonal:
                s = s + bias_ref[...]
            m_old = m_ref[...]
            m_new = jnp.maximum(m_old, jnp.max(s, axis=0, keepdims=True))
            p = jnp.exp2((s - m_new) * SCORE_SCALE_LOG2)
            alpha = jnp.exp2((m_old - m_new) * SCORE_SCALE_LOG2)
            l_ref[...] = alpha * l_ref[...] + jnp.sum(p, axis=0, keepdims=True)
            m_ref[...] = m_new
            alpha_buf[slot] = alpha
            p_buf[slot] = p.astype(BF16)

        def values(b, slot):
            pv = lax.dot_general(v_ref[rows(b), :], p_buf[slot], _TN, preferred_element_type=F32)
            acc_ref[...] = alpha_buf[slot] * acc_ref[...] + pv

        def step(t, slot):
            values(t - 2, slot)
            softmax(1 - slot, False)
            scores(t, slot)

        scores(0, 0)

        @pl.when(i == 0)
        def _():
            softmax(0, True)
            values(0, 0)

        @pl.when(i > 0)
        def _():
            scores(1, 1)
            softmax(0, False)
            steady = i - 1

            def pair(u, carry):
                step(2 + 2 * u, 0)
                step(3 + 2 * u, 1)
                return carry

            lax.fori_loop(0, steady // 2, pair, 0)

            @pl.when(steady % 2 == 1)
            def _():
                step(i, 0)

            last = i % 2
            softmax(last, True)
            values(i - 1, 1 - last)
            values(i, last)

        l = l_ref[...]
        o_ref[...] = (acc_ref[...] / l).T
        lse_ref[...] = m_ref[...] * SCORE_SCALE + jnp.log(l)

    outs = pl.pallas_call(
        body, name=name, grid=(nh, nq),
        in_specs=[pl.BlockSpec((None, blk, QK), lambda h, i: (h, i, 0)), pl.BlockSpec((None, t, QK), lambda h, i: (h, 0, 0)),
                  pl.BlockSpec((None, t, VDIM), lambda h, i: (h, 0, 0))] + [ANY] * n,
        out_specs=[pl.BlockSpec((blk, VDIM), lambda h, i: (i, h)),
                   pl.BlockSpec((None, None, 1, blk), lambda h, i: (h, i, 0, 0))] + [ANY] * n,
        out_shape=[S((t, nh * VDIM), F32), S((nh, nq, 1, blk), F32)] + [S((N_CHIPS,) + s.shape, s.dtype) for s in shards],
        scratch_shapes=[pltpu.VMEM((1, blk), F32), pltpu.VMEM((1, blk), F32), pltpu.VMEM((VDIM, blk), F32),
                        pltpu.VMEM((2, blk, blk), F32), pltpu.VMEM((2, blk, blk), BF16), pltpu.VMEM((2, 1, blk), F32),
                        pltpu.VMEM((blk, blk), F32)]
        + ([pltpu.SemaphoreType.DMA((n, 3)), pltpu.SemaphoreType.DMA((n, 3))] if n else []),
        compiler_params=_params(2))(q, k, v, *shards)
    return outs[0], outs[1], list(outs[2:])


def attention_out_bwd(name, dh, w_o, o, swap=()):
    t, d = dh.shape
    n = w_o.shape[0]
    blk = ATT_BLOCK
    m = len(swap)

    def body(dh_ref, w_ref, o_ref, *refs):
        do_ref, d_ref = refs[m:m + 2]
        if m:
            _ride_along(swap_copies(refs[:m], refs[m + 2:2 * m + 2], *refs[2 * m + 2:]), (pl.program_id(0),), (t // blk,))
        do_ref[...] = lax.dot_general(dh_ref[...].astype(BF16), w_ref[...], _NT, preferred_element_type=F32)
        for h in range(N_HEADS):
            cols = slice(h * VDIM, (h + 1) * VDIM)
            d_ref[h] = jnp.sum((do_ref[:, cols] * o_ref[:, cols]).T, axis=0, keepdims=True)

    tile = pl.BlockSpec((blk, n), lambda i: (i, 0))
    outs = pl.pallas_call(
        body, name=name, grid=(t // blk,),
        in_specs=[pl.BlockSpec((blk, d), lambda i: (i, 0)), _resident((n, d), lambda i: (0, 0)), tile] + [ANY] * m,
        out_specs=[tile, pl.BlockSpec((N_HEADS, None, 1, blk), lambda i: (0, i, 0, 0))] + [ANY] * m,
        out_shape=[S((t, n), F32), S((N_HEADS, t // blk, 1, blk), F32)] + _swap_shapes(swap),
        scratch_shapes=[pltpu.SemaphoreType.DMA((m,)), pltpu.SemaphoreType.DMA((m,))] if m else [],
        compiler_params=_params(1))(dh, w_o, o, *swap)
    return outs[0], outs[1], list(outs[2:])


def attention_bwd(name, q, k, v, do, lse, delta, parts=()):
    nh, t, _ = q.shape
    blk = ATT_BLOCK
    nq = t // blk
    n_pairs = nq * (nq + 1) // 2
    n = len(parts)
    scale = SCORE_SCALE

    def body(q_ref, k_ref, v_ref, do_ref, lse_ref, dl_ref, *refs):
        src = refs[:n]
        dq_ref, dk_ref, dv_ref = refs[n:n + 3]
        dst = refs[n + 3:2 * n + 3]
        s_buf, dp_buf, p_buf, ds_buf, bias_ref = refs[2 * n + 3:2 * n + 8]
        if n:
            send_sems, recv_sems = refs[2 * n + 8:]
            _ride_along(scatter_ici_copies(src, dst, send_sems, recv_sems), (pl.program_id(0),), (nh,))

        @pl.when(pl.program_id(0) == 0)
        def _():
            bias_ref[...] = jnp.where(_chunk_mask_t(0, 0, blk, blk), 0.0, MASK_VALUE)

        dq_ref[...] = jnp.zeros_like(dq_ref)
        dk_ref[...] = jnp.zeros_like(dk_ref)
        dv_ref[...] = jnp.zeros_like(dv_ref)

        def rows(x):
            return pl.ds(pl.multiple_of(x * blk, blk), blk)

        def after(jb):
            j, b = jb
            wrap = b == nq - 1 - j
            return jnp.where(wrap, j + 1, j), jnp.where(wrap, 0, b + 1)

        def products(jb, slot):
            j, b = jb
            s_buf[slot] = lax.dot_general(k_ref[rows(j), :], q_ref[rows(j + b), :], _NT, preferred_element_type=F32)
            dp_buf[slot] = lax.dot_general(v_ref[rows(j), :], do_ref[rows(j + b), :].astype(BF16), _NT, preferred_element_type=F32)

        def softmax_bwd(jb, slot):
            j, b = jb
            s = s_buf[slot] + bias_ref[...] * (b == 0).astype(F32)
            p = jnp.exp2(s * SCORE_SCALE_LOG2 - lse_ref[j + b] * LOG2_E)
            p_buf[slot] = p.astype(BF16)
            ds_buf[slot] = (p * (dp_buf[slot] - dl_ref[j + b]) * scale).astype(BF16)

        def gradients(jb, slot):
            j, b = jb
            dv_ref[rows(j), :] += jnp.dot(p_buf[slot], do_ref[rows(j + b), :].astype(BF16), preferred_element_type=F32)
            dk_ref[rows(j), :] += jnp.dot(ds_buf[slot], q_ref[rows(j + b), :], preferred_element_type=F32)
            dq_ref[rows(j + b), :] += lax.dot_general(ds_buf[slot], k_ref[rows(j), :], _TN, preferred_element_type=F32)

        def step(state, slot):
            third, second, first = state
            gradients(third, slot)
            softmax_bwd(second, 1 - slot)
            products(first, slot)
            return second, first, after(first)

        zero = jnp.int32(0)
        pair0 = (zero, zero)
        products(pair0, 0)
        if n_pairs == 1:
            softmax_bwd(pair0, 0)
            gradients(pair0, 0)
        else:
            pair1 = after(pair0)
            products(pair1, 1)
            softmax_bwd(pair0, 0)
            steady = n_pairs - 2
            state = lax.fori_loop(0, steady // 2, lambda u, st: step(step(st, 0), 1), (pair0, pair1, after(pair1)))
            if steady % 2:
                state = step(state, 0)
            before_last, last_pair, _ = state
            last = (n_pairs - 1) % 2
            softmax_bwd(last_pair, last)
            gradients(before_last, 1 - last)
            gradients(last_pair, last)

    head = lambda w: pl.BlockSpec((None, t, w), lambda h: (h, 0, 0))
    stats = pl.BlockSpec((None, nq, 1, blk), lambda h: (h, 0, 0, 0))
    outs = pl.pallas_call(
        body, name=name, grid=(nh,),
        in_specs=[head(QK), head(QK), head(VDIM), pl.BlockSpec((t, VDIM), lambda h: (0, h)), stats, stats] + [ANY] * n,
        out_specs=[head(QK), head(QK), head(VDIM)] + [ANY] * n,
        out_shape=[S((nh, t, QK), F32), S((nh, t, QK), F32), S((nh, t, VDIM), F32)] + [S(p.shape, p.dtype) for p in parts],
        scratch_shapes=[pltpu.VMEM((2, blk, blk), F32), pltpu.VMEM((2, blk, blk), F32), pltpu.VMEM((2, blk, blk), BF16),
                        pltpu.VMEM((2, blk, blk), BF16), pltpu.VMEM((blk, blk), F32)]
        + ([pltpu.SemaphoreType.DMA((n, 3)), pltpu.SemaphoreType.DMA((n, 3))] if n else []),
        compiler_params=_params(1, VMEM_LIMIT_WHOLE_HEAD))(q, k, v, do, lse, delta, *parts)
    return outs[0], outs[1], outs[2], list(outs[3:])


def _shift_down(u, s):
    rows = lax.broadcasted_iota(jnp.int32, u.shape, 0)
    return jnp.where(rows >= s, pltpu.roll(u, s, 0), 0.0)


def _shift_up(u, s):
    n = u.shape[0]
    rows = lax.broadcasted_iota(jnp.int32, u.shape, 0)
    return jnp.where(rows < n - s, pltpu.roll(u, n - s, 0), 0.0)


def _conv_specs(t, d, lanes):
    slab = lambda part: pl.BlockSpec((None, t, lanes), lambda j, part=part: (part, 0, j))
    return slab, pl.BlockSpec((3, lanes), lambda j: (0, j)), pl.BlockSpec((t, lanes), lambda j: (0, j))


def conv_fwd(name, bcx, w):
    _, t, d = bcx.shape
    lanes = _tile(d, 128, 128)
    slab, w_spec, col = _conv_specs(t, d, lanes)

    def body(b_ref, c_ref, x_ref, w_ref, y_ref):
        u = c_ref[...] * x_ref[...]
        uc = w_ref[0:1, :] * _shift_down(u, 2) + w_ref[1:2, :] * _shift_down(u, 1) + w_ref[2:3, :] * u
        y_ref[...] = (b_ref[...] * uc).astype(BF16)

    return pl.pallas_call(
        body, name=name, grid=(d // lanes,), in_specs=[slab(0), slab(1), slab(2), w_spec], out_specs=col,
        out_shape=S((t, d), BF16), compiler_params=_params(1))(bcx, bcx, bcx, w)


def conv_bwd(name, bcx, w, dy):
    _, t, d = bcx.shape
    lanes = _tile(d, 128, 128)
    slab, w_spec, col = _conv_specs(t, d, lanes)

    def body(b_ref, c_ref, x_ref, w_ref, dy_ref, d_ref, dw_ref):
        c = c_ref[...]
        x = x_ref[...]
        dyv = dy_ref[...]
        u = c * x
        u1 = _shift_down(u, 1)
        u2 = _shift_down(u, 2)
        w0, w1, w2 = w_ref[0:1, :], w_ref[1:2, :], w_ref[2:3, :]
        d_ref[0] = (dyv * (w0 * u2 + w1 * u1 + w2 * u)).astype(BF16)
        duc = dyv * b_ref[...]
        dw_ref[0:1, :] = jnp.sum(duc * u2, axis=0, keepdims=True)
        dw_ref[1:2, :] = jnp.sum(duc * u1, axis=0, keepdims=True)
        dw_ref[2:3, :] = jnp.sum(duc * u, axis=0, keepdims=True)
        du = w2 * duc + w1 * _shift_up(duc, 1) + w0 * _shift_up(duc, 2)
        d_ref[1] = (du * x).astype(BF16)
        d_ref[2] = (du * c).astype(BF16)

    return pl.pallas_call(
        body, name=name, grid=(d // lanes,), in_specs=[slab(0), slab(1), slab(2), w_spec, col],
        out_specs=[pl.BlockSpec((3, t, lanes), lambda j: (0, 0, j)), w_spec], out_shape=[S((3, t, d), BF16), S((3, d), F32)],
        compiler_params=_params(1))(bcx, bcx, bcx, w, dy)


def _adamw_update(w, g, m, v):
    m_new = ADAM_B1 * m + (1.0 - ADAM_B1) * g
    v_new = ADAM_B2 * v + (1.0 - ADAM_B2) * (g * g)
    m_hat = m_new / (1.0 - ADAM_B1 ** ADAM_STEP)
    v_hat = v_new / (1.0 - ADAM_B2 ** ADAM_STEP)
    return -ADAM_LR * (m_hat / (jnp.sqrt(v_hat) + ADAM_EPS) + ADAM_WD * w), m_new, v_new


def adamw(name, w, g, m, v):
    r, c = w.shape
    tr = _tile(r, 512)

    def body(w_ref, g_ref, m_ref, v_ref, d_ref, mo_ref, vo_ref):
        d_ref[...], mo_ref[...], vo_ref[...] = _adamw_update(w_ref[...], g_ref[...], m_ref[...], v_ref[...])

    blk = pl.BlockSpec((tr, c), lambda i: (i, 0))
    return pl.pallas_call(
        body, name=name, grid=(r // tr,), in_specs=[blk] * 4, out_specs=[blk] * 3, out_shape=[S((r, c), F32)] * 3,
        compiler_params=_params(1))(w, g, m, v)


def adamw_swapped(name, wt, g, mt, vt):
    nl, c, r = wt.shape
    tr = _tile(r, 512, 128)
    nr = r // tr

    def body(w_ref, g_ref, m_ref, v_ref, go_ref, d_ref, mo_ref, vo_ref):
        gt = g_ref[...].T
        go_ref[...] = gt
        d_ref[...], mo_ref[...], vo_ref[...] = _adamw_update(w_ref[...], gt, m_ref[...], v_ref[...])

    swapped = pl.BlockSpec((None, c, tr), lambda l, i: (l, 0, i))
    return pl.pallas_call(
        body, name=name, grid=(nl, nr),
        in_specs=[swapped, pl.BlockSpec((tr, c), lambda l, i: (l * nr + i, 0)), swapped, swapped],
        out_specs=[swapped] * 4, out_shape=[S((nl, c, r), F32)] * 4, compiler_params=_params(2))(wt, g, mt, vt)


def _place():
    x, y, c = lax.axis_index("x"), lax.axis_index("y"), lax.axis_index("c")
    other_chips = [(1 - x, y), (x, 1 - y), (1 - x, 1 - y)]
    return x, y, c, other_chips


def _half(c, rows):
    return pl.ds(pl.multiple_of(c * (rows // 2), 16), rows // 2)


def gather_weight_shards(shards):
    n = len(shards)

    def body(*refs):
        src = refs[:n]
        dst = refs[n:2 * n]
        send_sems, recv_sems = refs[2 * n:]
        x, y, c, chips = _place()
        me = 2 * x + y
        sibling = (x, y, 1 - c)

        def copy(i, slot, half_of, sem, to, from_input=False):
            rows = _half(half_of, src[i].shape[0])
            return pltpu.make_async_remote_copy(
                src_ref=src[i].at[rows] if from_input else dst[i].at[slot, rows], dst_ref=dst[i].at[slot, rows],
                send_sem=send_sems.at[i, sem], recv_sem=recv_sems.at[i, sem], device_id=to, device_id_type=MESH)

        sent = []
        for i in range(n):
            for j, chip in enumerate(chips):
                sent.append(copy(i, me, c, j, (*chip, c), from_input=True))
                sent[-1].start()
        for i in range(n):
            for j, (px, py) in enumerate(chips):
                copy(i, 2 * px + py, c, j, sibling).wait_recv()
                sent.append(copy(i, 2 * px + py, c, 3 + j, sibling))
                sent[-1].start()
        for i in range(n):
            for j, (px, py) in enumerate(chips):
                copy(i, 2 * px + py, 1 - c, 3 + j, sibling).wait_recv()
        for cp in sent:
            cp.wait_send()

    outs = pl.pallas_call(
        body, name="gather_weight_shards", in_specs=[ANY] * n, out_specs=[ANY] * n,
        out_shape=[S((N_CHIPS,) + s.shape, s.dtype) for s in shards],
        scratch_shapes=[pltpu.SemaphoreType.DMA((n, 6)), pltpu.SemaphoreType.DMA((n, 6))],
    )(*shards)
    return _fill_own_slot(outs, [s[None] for s in shards])


def gather_ici_copies(src, dst, send_sems, recv_sems):
    x, y, c, chips = _place()
    me = 2 * x + y
    pairs = []
    for i in range(len(src)):
        rows = _half(c, src[i].shape[0])
        for j, (px, py) in enumerate(chips):
            def copy(slot):
                return pltpu.make_async_remote_copy(
                    src_ref=src[i].at[rows], dst_ref=dst[i].at[slot, rows], send_sem=send_sems.at[i, j],
                    recv_sem=recv_sems.at[i, j], device_id=(px, py, c), device_id_type=MESH)
            pairs.append((copy(me), copy(2 * px + py)))
    return pairs


def scatter_ici_copies(src, dst, send_sems, recv_sems):
    x, y, c, chips = _place()
    me = 2 * x + y
    pairs = []
    for i in range(len(src)):
        for j, (px, py) in enumerate(chips):
            def copy(from_slot, to_slot):
                return pltpu.make_async_remote_copy(
                    src_ref=src[i].at[from_slot], dst_ref=dst[i].at[to_slot], send_sem=send_sems.at[i, j],
                    recv_sem=recv_sems.at[i, j], device_id=(px, py, c), device_id_type=MESH)
            pairs.append((copy(2 * px + py, me), copy(me, 2 * px + py)))
    return pairs


def _ride_along(pairs, grid_ids, grid_sizes):
    first = grid_ids[0] == 0
    last = grid_ids[0] == grid_sizes[0] - 1
    for g, size in zip(grid_ids[1:], grid_sizes[1:]):
        first = first & (g == 0)
        last = last & (g == size - 1)

    @pl.when(first)
    def _():
        for outgoing, _ in pairs:
            outgoing.start()

    @pl.when(last)
    def _():
        for _, incoming in pairs:
            incoming.wait_recv()
        for outgoing, _ in pairs:
            outgoing.wait_send()


def _fill_own_slot(gathered, own):
    me = 2 * lax.axis_index("x") + lax.axis_index("y")
    return [lax.dynamic_update_slice(g, o, (me,) + (0,) * (g.ndim - 1)) for g, o in zip(gathered, own)]


def forward_copies(src, dst, send_sems, recv_sems):
    x, y, c, chips = _place()
    pairs = []
    for i in range(len(src)):
        for j, (px, py) in enumerate(chips):
            def copy(half_of):
                rows = _half(half_of, src[i].shape[1])
                return pltpu.make_async_remote_copy(
                    src_ref=src[i].at[2 * px + py, rows], dst_ref=dst[i].at[2 * px + py, rows], send_sem=send_sems.at[i, j],
                    recv_sem=recv_sems.at[i, j], device_id=(x, y, 1 - c), device_id_type=MESH)
            pairs.append((copy(c), copy(1 - c)))
    return pairs


def attention_out_proj(name, attn, w_o, resid, next_gain, arriving):
    t, kdim = attn.shape
    n = w_o.shape[1]
    tm = _tile(t, 512)
    m = len(arriving)

    def body(x_ref, w_ref, r_ref, g_ref, *refs):
        h_ref, a_ref = refs[m:m + 2]
        _ride_along(forward_copies(refs[:m], refs[m + 2:2 * m + 2], *refs[2 * m + 2:]), (pl.program_id(0),), (t // tm,))
        y = r_ref[...] + jnp.dot(x_ref[...].astype(BF16), w_ref[...], preferred_element_type=F32)
        h_ref[...] = y
        a_ref[...] = (y * _rstd(y) * g_ref[...]).astype(BF16)

    row = pl.BlockSpec((tm, n), lambda i: (i, 0))
    outs = pl.pallas_call(
        body, name=name, grid=(t // tm,),
        in_specs=[pl.BlockSpec((tm, kdim), lambda i: (i, 0)), _resident((kdim, n), lambda i: (0, 0)), row,
                  pl.BlockSpec((1, n), lambda i: (0, 0))] + [ANY] * m,
        out_specs=[row, row] + [ANY] * m, out_shape=[S((t, n), F32), S((t, n), BF16)] + [S(g.shape, g.dtype) for g in arriving],
        input_output_aliases={4 + i: 2 + i for i in range(m)},
        scratch_shapes=[pltpu.SemaphoreType.DMA((m, 3)), pltpu.SemaphoreType.DMA((m, 3))],
        compiler_params=_params(1))(attn, w_o, resid, next_gain, *arriving)
    return outs[0], outs[1], list(outs[2:])


def swap_copies(src, dst, send_sems, recv_sems):
    x, y, c, _ = _place()
    pairs = []
    for i in range(len(src)):
        cp = pltpu.make_async_remote_copy(
            src_ref=src[i].at[:, _half(1 - c, src[i].shape[1]), :], dst_ref=dst[i], send_sem=send_sems.at[i],
            recv_sem=recv_sems.at[i], device_id=(x, y, 1 - c), device_id_type=MESH)
        pairs.append((cp, cp))
    return pairs


def _swap_shapes(grads):
    return [S((g.shape[0], g.shape[1] // 2, g.shape[2]), g.dtype) for g in grads]


def sibling_swap_halves(name, grads):
    n = len(grads)

    def body(*refs):
        pairs = swap_copies(refs[:n], refs[n:2 * n], *refs[2 * n:])
        for outgoing, _ in pairs:
            outgoing.start()
        for _, incoming in pairs:
            incoming.wait_recv()
        for outgoing, _ in pairs:
            outgoing.wait_send()

    return pl.pallas_call(
        body, name=name, in_specs=[ANY] * n, out_specs=[ANY] * n, out_shape=_swap_shapes(grads),
        scratch_shapes=[pltpu.SemaphoreType.DMA((n,)), pltpu.SemaphoreType.DMA((n,))],
    )(*grads)


def add_halves(name, g, rx):
    _, r, cdim = g.shape
    r2 = r // 2
    tr = _tile(r2, 512, 16)
    nb = r2 // tr

    def body(lo_ref, hi_ref, rx_ref, o_ref):
        mine = jnp.where(lax.axis_index("c") == 0, lo_ref[...], hi_ref[...])
        o_ref[...] = (mine.astype(F32) + rx_ref[...].astype(F32)).astype(BF16)

    half = pl.BlockSpec((None, tr, cdim), lambda k, i: (k, i, 0))
    return pl.pallas_call(
        body, name=name, grid=(N_CHIPS, nb),
        in_specs=[half, pl.BlockSpec((None, tr, cdim), lambda k, i: (k, nb + i, 0)), half],
        out_specs=half, out_shape=S((N_CHIPS, r2, cdim), BF16), compiler_params=_params(2))(g, g, rx)


def _own_slots(parts):
    me = 2 * lax.axis_index("x") + lax.axis_index("y")
    return [lax.dynamic_slice(p, (me, 0, 0), (1,) + p.shape[1:]) for p in parts]


def sum_chips(name, arrived, mine):
    _, r2, cdim = arrived.shape
    tr = _tile(r2, 512, 16)

    def body(a_ref, m_ref, o_ref):
        me = 2 * lax.axis_index("x") + lax.axis_index("y")
        acc = jnp.zeros((tr, cdim), F32)
        for k in range(N_CHIPS):
            acc = acc + jnp.where(me == k, m_ref[k], a_ref[k]).astype(F32)
        o_ref[...] = acc

    slots = pl.BlockSpec((N_CHIPS, tr, cdim), lambda i: (0, i, 0))
    return pl.pallas_call(
        body, name=name, grid=(r2 // tr,), in_specs=[slots, slots],
        out_specs=pl.BlockSpec((tr, cdim), lambda i: (i, 0)), out_shape=S((r2, cdim), F32), compiler_params=_params(1))(arrived, mine)


def join_copies(src, dst, where, send_sems, recv_sems):
    x, y, c, _ = _place()
    pairs = []
    for i in range(len(src)):
        def copy(half_of):
            r2 = src[i].shape[0]
            rows = pl.ds(pl.multiple_of(where[i][1] + half_of * r2, 8), r2)
            return pltpu.make_async_remote_copy(
                src_ref=src[i], dst_ref=dst[where[i][0]].at[rows], send_sem=send_sems.at[i],
                recv_sem=recv_sems.at[i], device_id=(x, y, 1 - c), device_id_type=MESH)
        pairs.append((copy(c), copy(1 - c)))
    return pairs


def _fill_own_halves(targets, halves, where):
    targets = list(targets)
    c = lax.axis_index("c")
    for h, (tgt, first) in zip(halves, where):
        targets[tgt] = lax.dynamic_update_slice(targets[tgt], h, (first + c * h.shape[0], 0))
    return targets


def sibling_join_halves(name, halves, targets, where):
    n = len(halves)

    def body(*refs):
        pairs = join_copies(refs[:n], refs[n:n + len(targets)], where, *refs[n + len(targets):])
        for outgoing, _ in pairs:
            outgoing.start()
        for _, incoming in pairs:
            incoming.wait_recv()
        for outgoing, _ in pairs:
            outgoing.wait_send()

    outs = pl.pallas_call(
        body, name=name, in_specs=[ANY] * n, out_specs=[ANY] * len(targets), out_shape=[S(tg, F32) for tg in targets],
        scratch_shapes=[pltpu.SemaphoreType.DMA((n,)), pltpu.SemaphoreType.DMA((n,))],
    )(*halves)
    return _fill_own_halves(outs, halves, where)


def all_reduce_small(name, packed):
    rows, width = packed.shape

    def body(x_ref, o_ref, gathered, send_sems, recv_sems):
        x, y, c, _ = _place()
        me = 4 * x + 2 * y + c
        gathered[me] = x_ref[...]
        flips = [(fx, fy, fc) for fx in (0, 1) for fy in (0, 1) for fc in (0, 1)][1:]

        def copy(r, slot, to):
            return pltpu.make_async_remote_copy(
                src_ref=x_ref, dst_ref=gathered.at[slot], send_sem=send_sems.at[r], recv_sem=recv_sems.at[r],
                device_id=to, device_id_type=MESH)

        def peer(f):
            return (x ^ f[0], y ^ f[1], c ^ f[2])

        sent = [copy(r, me, peer(f)) for r, f in enumerate(flips)]
        for cp in sent:
            cp.start()
        for r, f in enumerate(flips):
            px, py, pc = peer(f)
            copy(r, 4 * px + 2 * py + pc, peer(f)).wait_recv()
        for cp in sent:
            cp.wait_send()
        acc = gathered[0]
        for k in range(1, N_DEV):
            acc = acc + gathered[k]
        o_ref[...] = acc

    vmem = pl.BlockSpec(memory_space=pltpu.VMEM)
    return pl.pallas_call(
        body, name=name, in_specs=[vmem], out_specs=vmem, out_shape=S((rows, width), F32),
        scratch_shapes=[pltpu.VMEM((N_DEV, rows, width), F32), pltpu.SemaphoreType.DMA((N_DEV - 1,)),
                        pltpu.SemaphoreType.DMA((N_DEV - 1,))],
    )(packed)


def _rope_tables(positions):
    inv_freq = 1.0 / (ROPE_THETA ** (jnp.arange(0, ROPE, 2, dtype=F32) / ROPE))
    ang = positions.astype(F32)[:, None] * inv_freq
    return jnp.cos(ang), jnp.sin(ang)


def _unstack_cols(w):
    k4, k, n4 = w.shape
    return jnp.transpose(w, (1, 0, 2)).reshape(k, k4 * n4)


def _stack_cols(w):
    k, n = w.shape
    return jnp.transpose(w.reshape(k, N_CHIPS, n // N_CHIPS), (1, 0, 2))


def kernel(x, positions, mla_norm, mla_w_in, mla_g_cq, mla_g_ckv, mla_w_uq, mla_w_ukv, mla_w_o, conv_norm, conv_w_in, conv_w, conv_w_out, ffn_norm, ffn_w_gate, ffn_w_up, ffn_w_down, final_norm, loss_target, m_mla_norm, m_mla_w_in, m_mla_g_cq, m_mla_g_ckv, m_mla_w_uq, m_mla_w_ukv, m_mla_w_o, m_conv_norm, m_conv_w_in, m_conv_w, m_conv_w_out, m_ffn_norm, m_ffn_w_gate, m_ffn_w_up, m_ffn_w_down, m_final_norm, v_mla_norm, v_mla_w_in, v_mla_g_cq, v_mla_g_ckv, v_mla_w_uq, v_mla_w_ukv, v_mla_w_o, v_conv_norm, v_conv_w_in, v_conv_w, v_conv_w_out, v_ffn_norm, v_ffn_w_gate, v_ffn_w_up, v_ffn_w_down, v_final_norm):
    weights = dict(mla_norm=mla_norm, mla_w_in=mla_w_in, mla_g_cq=mla_g_cq, mla_g_ckv=mla_g_ckv, mla_w_uq=mla_w_uq,
                   mla_w_ukv=mla_w_ukv, mla_w_o=mla_w_o, conv_norm=conv_norm, conv_w_in=conv_w_in, conv_w=conv_w,
                   conv_w_out=conv_w_out, ffn_norm=ffn_norm, ffn_w_gate=ffn_w_gate, ffn_w_up=ffn_w_up,
                   ffn_w_down=ffn_w_down, final_norm=final_norm)
    m_in = dict(mla_norm=m_mla_norm, mla_w_in=m_mla_w_in, mla_g_cq=m_mla_g_cq, mla_g_ckv=m_mla_g_ckv, mla_w_uq=m_mla_w_uq,
                mla_w_ukv=m_mla_w_ukv, mla_w_o=m_mla_w_o, conv_norm=m_conv_norm, conv_w_in=m_conv_w_in, conv_w=m_conv_w,
                conv_w_out=m_conv_w_out, ffn_norm=m_ffn_norm, ffn_w_gate=m_ffn_w_gate, ffn_w_up=m_ffn_w_up,
                ffn_w_down=m_ffn_w_down, final_norm=m_final_norm)
    v_in = dict(mla_norm=v_mla_norm, mla_w_in=v_mla_w_in, mla_g_cq=v_mla_g_cq, mla_g_ckv=v_mla_g_ckv, mla_w_uq=v_mla_w_uq,
                mla_w_ukv=v_mla_w_ukv, mla_w_o=v_mla_w_o, conv_norm=v_conv_norm, conv_w_in=v_conv_w_in, conv_w=v_conv_w,
                conv_w_out=v_conv_w_out, ffn_norm=v_ffn_norm, ffn_w_gate=v_ffn_w_gate, ffn_w_up=v_ffn_w_up,
                ffn_w_down=v_ffn_w_down, final_norm=v_final_norm)
    big = ["mla_w_in", "mla_w_uq", "mla_w_ukv", "mla_w_o", "conv_w_in", "conv_w_out", "ffn_w_gate", "ffn_w_up", "ffn_w_down"]
    order = list(weights)

    t, d = x.shape[1], x.shape[2]
    h0 = x.reshape(t, d)
    target = loss_target.reshape(t, d)
    cos, sin = _rope_tables(positions.reshape(t))

    def rows2d(a):
        return a.reshape(-1, a.shape[-1])

    first, later = big[:4], big[4:]
    shards = {n: rows2d(weights[n]).astype(BF16) for n in big}
    gathered = dict(zip(first, gather_weight_shards([shards[n] for n in first])))
    w_in = gathered["mla_w_in"].reshape(-1, gathered["mla_w_in"].shape[-1])
    w_uq = _unstack_cols(gathered["mla_w_uq"])
    w_ukv = _unstack_cols(gathered["mla_w_ukv"])
    w_o = gathered["mla_w_o"].reshape(-1, d)

    chip = 2 * lax.axis_index("x") + lax.axis_index("y")
    core = lax.axis_index("c")
    d4 = d // N_CHIPS
    first_core = (core == 0).astype(F32)

    def place_shard(shard):
        full = jnp.zeros((shard.shape[0], d), F32)
        return lax.dynamic_update_slice(full, shard * first_core, (0, chip * d4))

    def pack_rows(rows):
        idx = lax.broadcasted_iota(jnp.int32, (SMALL_ROWS, d), 0)
        out = jnp.zeros((SMALL_ROWS, d), F32)
        for r, row in enumerate(rows):
            out = out + jnp.where(idx == r, row, 0.0)
        return out

    cw = place_shard(conv_w.reshape(3, d4))
    pre = all_reduce_small("all_gather_conv_small", pack_rows([place_shard(conv_norm.reshape(1, d4)), cw[0:1], cw[1:2], cw[2:3]]))
    conv_norm_full = pre[0:1]
    conv_w_full = pre[1:4]

    a0 = rms_fwd("mla_norm_fwd", h0, mla_norm)
    proj, cq, ckv, kr = mla_in_proj("mla_in_proj", a0, w_in, mla_g_cq, mla_g_ckv, cos, sin)
    q = linear("mla_q_up", cq, w_uq, F32)
    kv = linear("mla_kv_up", ckv, w_ukv, BF16)
    qh, kh, vh, conv_arriving = qkv_heads("qkv_heads", q, kv, kr, cos, sin, [shards[n] for n in later[:2]])
    attn, lse, ffn_arriving = attention_fwd("attention_fwd", qh, kh, vh, [shards[n] for n in later[2:]])
    h1, a1, handed = attention_out_proj("mla_out_proj", attn, w_o, h0, ffn_norm[0:1], conv_arriving + ffn_arriving)
    gathered.update(zip(later, _fill_own_slot(handed, [shards[n][None] for n in later])))
    cw_in = _unstack_cols(gathered["conv_w_in"])
    cw_out = gathered["conv_w_out"].reshape(-1, d)
    wg_all, wu_all, wd_all = gathered["ffn_w_gate"], gathered["ffn_w_up"], gathered["ffn_w_down"]

    def ffn_forward(tag, h, a, layer, next_gain):
        g, u, z = ffn_up(f"ffn{tag}_up", a, wg_all, wu_all, layer)
        return g, u, z, ffn_down(f"ffn{tag}_down", z, wd_all, layer, h, next_gain)

    g0, u0, z0, (h2, a2) = ffn_forward(0, h1, a1, 0, conv_norm_full)
    bcx = conv_in_proj("conv_in_proj", a2, cw_in)
    yc = conv_fwd("conv_fwd", bcx, conv_w_full)
    h3, a3 = linear("conv_out_proj", yc, cw_out, F32, resid=h2, next_gain=ffn_norm[1:2])
    g1, u1, z1 = ffn_up("ffn1_up", a3, wg_all, wu_all, 1)
    dh4, d_final_norm, loss_local = ffn_down_loss("ffn1_down_loss", z1, wd_all, 1, h3, final_norm.reshape(1, d), target)

    def ffn_backward(tag, dh, h, layer, a, g, u, z, swap=()):
        dg, du, swapped = ffn_bwd_hidden(f"ffn{tag}_bwd_hidden", dh, wd_all, layer, g, u, swap)
        d_wd = ffn_wgrad_down(f"ffn{tag}_wgrad_down", z, dh)
        dh_prev, d_norm = ffn_bwd_input(f"ffn{tag}_bwd_input", dg, du, wg_all, wu_all, layer, h, ffn_norm[layer:layer + 1], dh)
        d_wg = ffn_wgrad_up(f"ffn{tag}_wgrad_gate", a, dg)
        d_wu = ffn_wgrad_up(f"ffn{tag}_wgrad_up", a, du)
        return dh_prev, d_norm, [d_wg, d_wu, d_wd], swapped

    def pair_sums(tag, local, from_sibling):
        return [add_halves(f"pair_sum_{tag}{i}", g, r) for i, (g, r) in enumerate(zip(local, from_sibling))]

    def sum_from_chips(tag, pairs, arrived):
        return [sum_chips(f"chip_sum_{tag}{i}", a, p) for i, (a, p) in enumerate(zip(arrived, pairs))]

    def shard_shape(n):
        return rows2d(weights[n]).shape

    dh3, d_ffn_norm1, ffn1_grads, _ = ffn_backward(1, dh4, h3, 1, a3, g1, u1, z1)

    dyc = linear_nt("conv_out_bwd_input", dh3, cw_out, F32)
    d_cw_out = wgrad("conv_out_wgrad", yc, dh3)
    dbcx, d_conv_w = conv_bwd("conv_bwd", bcx, conv_w_full, dyc)
    dh2, d_conv_norm = conv_in_bwd_input("conv_in_bwd_input", dbcx, cw_in, h2, conv_norm_full, dh3)
    d_cw_in = conv_in_wgrad("conv_in_wgrad", a2, dbcx)

    second = [d_cw_in, d_cw_out.reshape(N_CHIPS, -1, d)] + ffn1_grads
    dh1, d_ffn_norm0, ffn0_grads, second_swapped = ffn_backward(0, dh2, h1, 0, a1, g0, u0, z0, second)
    d_w_o = wgrad("mla_out_wgrad", attn, dh1)
    first_part = ffn0_grads + [d_w_o.reshape(N_CHIPS, -1, d)]
    d_attn, delta, first_swapped = attention_out_bwd("mla_out_bwd_input", dh1, w_o, attn, first_part)
    rest_pairs = pair_sums("rest", second + first_part, second_swapped + first_swapped)
    dqh, dkh, dvh, rest_arrived = attention_bwd("attention_bwd", qh, kh, vh, d_attn, lse, delta, rest_pairs)
    rd, rf = ffn0_grads[0].shape[1], ffn0_grads[2].shape[1]
    rest_where = [(0, 0), (1, 0), (2, rd), (3, rd), (4, rf), (2, 0), (3, 0), (4, 0), (5, 0)]
    rest_names = later + ["mla_w_o"]
    dq, dkv, dkr, rest_grads = qkv_heads_bwd("qkv_heads_bwd", dqh, dkh, dvh, cos, sin, sum_from_chips("rest", rest_pairs, rest_arrived),
                                              [shard_shape(n) for n in rest_names], rest_where)
    grads = dict(zip(rest_names, rest_grads))
    dcq = linear_nt("mla_q_up_bwd_input", dq, w_uq, F32)
    d_w_uq = wgrad("mla_q_up_wgrad", cq, dq)
    dckv = linear_nt("mla_kv_up_bwd_input", dkv, w_ukv, F32)
    d_w_ukv = wgrad("mla_kv_up_wgrad", ckv, dkv)
    dproj, d_g_cq, d_g_ckv = mla_mid_bwd("mla_mid_bwd", proj, mla_g_cq, mla_g_ckv, dcq, dckv, dkr, cos, sin)
    d_w_in = wgrad("mla_in_wgrad", a0, dproj)
    mla_local = [d_w_in.reshape(N_CHIPS, -1, d_w_in.shape[-1]), _stack_cols(d_w_uq), _stack_cols(d_w_ukv)]
    mla_pairs = pair_sums("mla", mla_local, sibling_swap_halves("sibling_swap_mla", mla_local))
    grad_x, d_mla_norm, mla_arrived = linear_nt_norm_bwd("mla_in_bwd_input", dproj, w_in, h0, mla_norm, dh1, mla_pairs)

    grads.update(zip(first[:3], sibling_join_halves("sibling_join_mla", sum_from_chips("mla", mla_pairs, mla_arrived),
                                                    [shard_shape(n) for n in first[:3]], [(i, 0) for i in range(3)])))

    def pad_row(v):
        return jnp.pad(v, ((0, 0), (0, d - v.shape[1])))

    small = all_reduce_small("all_reduce_small_grads", pack_rows([
        d_mla_norm, pad_row(d_g_cq), pad_row(d_g_ckv), d_ffn_norm0, d_ffn_norm1, d_final_norm, d_conv_norm,
        d_conv_w[0:1], d_conv_w[1:2], d_conv_w[2:3], jnp.broadcast_to(loss_local, (1, d))]))
    loss = small[10, 0]
    grads["mla_norm"] = small[0:1]
    grads["mla_g_cq"] = small[1:2, :mla_g_cq.shape[1]]
    grads["mla_g_ckv"] = small[2:3, :mla_g_ckv.shape[1]]
    grads["ffn_norm"] = small[3:5]
    grads["final_norm"] = small[5:6]
    grads["conv_norm"] = lax.dynamic_slice(small[6:7], (0, chip * d4), (1, d4))
    grads["conv_w"] = lax.dynamic_slice(small[7:10], (0, chip * d4), (3, d4))

    outs_g, outs_d, outs_m, outs_v = [], [], [], []
    for n in order:
        w = weights[n]
        if w.ndim == 3 and w.shape[2] % 128 and w.shape[1] % 128 == 0:
            results = adamw_swapped(f"adamw_{n}", jnp.swapaxes(w, 1, 2), grads[n].reshape(-1, w.shape[2]),
                                    jnp.swapaxes(m_in[n], 1, 2), jnp.swapaxes(v_in[n], 1, 2))
            grad_w, delta_w, new_m, new_v = [jnp.swapaxes(o, 1, 2) for o in results]
        else:
            delta_w, new_m, new_v = adamw(f"adamw_{n}", rows2d(w), grads[n].reshape(rows2d(w).shape), rows2d(m_in[n]), rows2d(v_in[n]))
            grad_w = grads[n]
        outs_g.append(grad_w.reshape(w.shape))
        outs_d.append(delta_w.reshape(w.shape))
        outs_m.append(new_m.reshape(w.shape))
        outs_v.append(new_v.reshape(w.shape))
    return (loss, grad_x.reshape(x.shape), *outs_g, *outs_d, *outs_m, *outs_v)
```

```python
import math

import jax
import jax.numpy as jnp
from jax import lax
from jax.experimental import pallas as pl
from jax.experimental.pallas import tpu as pltpu

F32 = jnp.float32
BF16 = jnp.bfloat16
S = jax.ShapeDtypeStruct

N_HEADS = 8
NOPE = 128
ROPE = 64
HALF = ROPE // 2
VDIM = 128
QK = NOPE + ROPE
CHUNK = 64
ROPE_THETA = 10000.0
RMS_EPS = 1e-6
ADAM_LR = 0.001
ADAM_B1 = 0.9
ADAM_B2 = 0.999
ADAM_EPS = 1e-08
ADAM_WD = 0.01
ADAM_STEP = 10

N_CHIPS = 4
N_DEV = 8
MASK_VALUE = -1e30
SCORE_SCALE = 1.0 / math.sqrt(QK)
LOG2_E = math.log2(math.e)
SCORE_SCALE_LOG2 = SCORE_SCALE * LOG2_E
VMEM_LIMIT = 48 * 1024 * 1024
VMEM_LIMIT_WHOLE_HEAD = 58 * 1024 * 1024
ATT_BLOCK = 512
CONV_SAVED_DTYPE = jnp.bfloat16
SMALL_ROWS = 16

_NN = (((1,), (0,)), ((), ()))
_NT = (((1,), (1,)), ((), ()))
_TN = (((0,), (0,)), ((), ()))
MESH = pl.DeviceIdType.MESH
ANY = pl.BlockSpec(memory_space=pl.ANY)


def _params(n_axes, vmem_limit=VMEM_LIMIT):
    return pltpu.CompilerParams(dimension_semantics=("arbitrary",) * n_axes, vmem_limit_bytes=vmem_limit)


def _tile(n, cap, mult=8):
    for t in range(min(cap, n), 0, -1):
        if n % t == 0 and t % mult == 0:
            return t
    return n


def _sigmoid(x):
    return 0.5 * jnp.tanh(0.5 * x) + 0.5


def _mm(name, a_ops, b_ops, products, dims, grid, k_axis, outs, acc_shape, epilogue, extra_ops=()):
    na, nb, ne, no = len(a_ops), len(b_ops), len(extra_ops), len(outs)
    n_acc = 1 + max(c for _, _, c in products)
    nk = 1 if k_axis is None else grid[k_axis]

    def body(*refs):
        a_refs = refs[:na]
        b_refs = refs[na:na + nb]
        e_refs = refs[na + nb:na + nb + ne]
        o_refs = refs[na + nb + ne:na + nb + ne + no]
        acc_refs = refs[na + nb + ne + no:]

        def partial_sums():
            vals = [None] * n_acc
            for ai, bi, ci in products:
                d = lax.dot_general(a_refs[ai][...].astype(BF16), b_refs[bi][...].astype(BF16), dims,
                                    preferred_element_type=F32)
                vals[ci] = d if vals[ci] is None else vals[ci] + d
            return vals

        if nk == 1:
            epilogue(partial_sums(), e_refs, o_refs)
        else:
            k = pl.program_id(k_axis)

            @pl.when(k == 0)
            def _():
                for acc in acc_refs:
                    acc[...] = jnp.zeros_like(acc)

            for acc, v in zip(acc_refs, partial_sums()):
                acc[...] += v

            @pl.when(k == nk - 1)
            def _():
                epilogue([acc[...] for acc in acc_refs], e_refs, o_refs)

    ops = list(a_ops) + list(b_ops) + list(extra_ops)
    return pl.pallas_call(
        body, name=name, grid=grid,
        in_specs=[s for _, s in ops], out_specs=[s for _, s in outs], out_shape=[o for o, _ in outs],
        scratch_shapes=[pltpu.VMEM(acc_shape, F32) for _ in range(n_acc if nk > 1 else 0)],
        compiler_params=_params(len(grid)),
    )(*[a for a, _ in ops])


def _store(accs, e_refs, o_refs):
    o_refs[0][...] = accs[0].astype(o_refs[0].dtype)


def linear(name, x, w, out_dtype, resid=None, next_gain=None):
    t, k = x.shape
    n = w.shape[1]
    tm = _tile(t, 512)
    tn = n if n <= 2048 else _tile(n, 1024, 128)
    tile = pl.BlockSpec((tm, tn), lambda j, i: (i, j))
    extra = [] if resid is None else [(resid, tile)]
    outs = [(S((t, n), out_dtype), tile)]
    if next_gain is not None:
        assert tn == n
        extra.append((next_gain, pl.BlockSpec((1, n), lambda j, i: (0, 0))))
        outs.append((S((t, n), BF16), tile))

    def epilogue(accs, e_refs, o_refs):
        y = accs[0] if resid is None else e_refs[0][...] + accs[0]
        o_refs[0][...] = y.astype(out_dtype)
        if next_gain is not None:
            o_refs[1][...] = (y * _rstd(y) * e_refs[-1][...]).astype(BF16)

    res = _mm(name, [(x, pl.BlockSpec((tm, k), lambda j, i: (i, 0)))], [(w, pl.BlockSpec((k, tn), lambda j, i: (0, j)))],
              [(0, 0, 0)], _NN, (n // tn, t // tm), None, outs, None, epilogue, extra)
    return res[0] if next_gain is None else res


def linear_nt(name, dy, w, out_dtype):
    t, n = dy.shape
    k = w.shape[0]
    tm = _tile(t, 512)
    tc = n if n <= 2048 else _tile(n, 1024, 128)
    return _mm(name, [(dy, pl.BlockSpec((tm, tc), lambda i, c: (i, c)))], [(w, pl.BlockSpec((k, tc), lambda i, c: (0, c)))],
               [(0, 0, 0)], _NT, (t // tm, n // tc), 1,
               [(S((t, k), out_dtype), pl.BlockSpec((tm, k), lambda i, c: (i, 0)))], (tm, k), _store)[0]


def wgrad(name, x, dy):
    t, k = x.shape
    n = dy.shape[1]
    tk = _tile(t, 512)
    tn = n if n <= 1024 else _tile(n, 1024, 128)
    return _mm(name, [(x, pl.BlockSpec((tk, k), lambda j, s: (s, 0)))], [(dy, pl.BlockSpec((tk, tn), lambda j, s: (s, j)))],
               [(0, 0, 0)], _TN, (n // tn, t // tk), 1,
               [(S((k, n), BF16), pl.BlockSpec((k, tn), lambda j, s: (0, j)))], (k, tn), _store)[0]


def _resident(shape, index_map):
    return pl.BlockSpec(shape, index_map, pipeline_mode=pl.Buffered(1))


def ffn_up(name, a, wg_all, wu_all, layer):
    t, d = a.shape
    f4 = wg_all.shape[2]
    tm = _tile(t, 512)
    w_spec = _resident((N_CHIPS, d, f4), lambda i: (0, layer, 0))
    h_spec = pl.BlockSpec((N_CHIPS, tm, f4), lambda i: (0, i, 0))

    def body(a_ref, wg_ref, wu_ref, zg_ref, zu_ref, z_ref):
        av = a_ref[...]
        for k in range(N_CHIPS):
            g = jnp.dot(av, wg_ref[k], preferred_element_type=F32)
            u = jnp.dot(av, wu_ref[k], preferred_element_type=F32)
            sg = _sigmoid(g)
            silu = g * sg
            zg_ref[k] = (u * (sg * (1.0 + g * (1.0 - sg)))).astype(BF16)
            zu_ref[k] = silu.astype(BF16)
            z_ref[k] = (silu * u).astype(BF16)

    return pl.pallas_call(
        body, name=name, grid=(t // tm,), in_specs=[pl.BlockSpec((tm, d), lambda i: (i, 0)), w_spec, w_spec],
        out_specs=[h_spec] * 3, out_shape=[S((N_CHIPS, t, f4), BF16)] * 3, compiler_params=_params(1))(a, wg_all, wu_all)


def ffn_down(name, z, wd_all, layer, resid, next_gain=None):
    _, t, f4 = z.shape
    d = wd_all.shape[2]
    tm = _tile(t, 512)
    row = pl.BlockSpec((tm, d), lambda i: (i, 0))
    normed = next_gain is not None

    def body(z_ref, wd_ref, r_ref, *refs):
        acc = r_ref[...]
        for k in range(N_CHIPS):
            acc = acc + jnp.dot(z_ref[k], wd_ref[k], preferred_element_type=F32)
        refs[-2 if normed else -1][...] = acc
        if normed:
            refs[-1][...] = (acc * _rstd(acc) * refs[0][...]).astype(BF16)

    res = pl.pallas_call(
        body, name=name, grid=(t // tm,),
        in_specs=[pl.BlockSpec((N_CHIPS, tm, f4), lambda i: (0, i, 0)), _resident((N_CHIPS, f4, d), lambda i: (0, layer, 0)), row]
        + ([pl.BlockSpec((1, d), lambda i: (0, 0))] if normed else []),
        out_specs=[row] * (2 if normed else 1), out_shape=[S((t, d), F32)] + ([S((t, d), BF16)] if normed else []),
        compiler_params=_params(1))(z, wd_all, resid, *([next_gain] if normed else []))
    return res if normed else res[0]


def ffn_bwd_hidden(name, dh, wd_all, layer, zg, zu, swap=()):
    t, d = dh.shape
    f4 = zg.shape[2]
    tm = _tile(t, 512)
    h_spec = pl.BlockSpec((N_CHIPS, tm, f4), lambda i: (0, i, 0))
    n = len(swap)

    def body(dh_ref, wd_ref, zg_ref, zu_ref, *refs):
        dg_ref, du_ref = refs[n:n + 2]
        if n:
            _ride_along(swap_copies(refs[:n], refs[n + 2:2 * n + 2], *refs[2 * n + 2:]), (pl.program_id(0),), (t // tm,))
        dhb = dh_ref[...].astype(BF16)
        for k in range(N_CHIPS):
            dz = lax.dot_general(dhb, wd_ref[k], _NT, preferred_element_type=F32)
            dg_ref[k] = (dz * zg_ref[k].astype(F32)).astype(BF16)
            du_ref[k] = (dz * zu_ref[k].astype(F32)).astype(BF16)

    outs = pl.pallas_call(
        body, name=name, grid=(t // tm,),
        in_specs=[pl.BlockSpec((tm, d), lambda i: (i, 0)), _resident((N_CHIPS, f4, d), lambda i: (0, layer, 0)), h_spec, h_spec]
        + [ANY] * n,
        out_specs=[h_spec] * 2 + [ANY] * n, out_shape=[S((N_CHIPS, t, f4), BF16)] * 2 + _swap_shapes(swap),
        scratch_shapes=[pltpu.SemaphoreType.DMA((n,)), pltpu.SemaphoreType.DMA((n,))] if n else [],
        compiler_params=_params(1))(dh, wd_all, zg, zu, *swap)
    return outs[0], outs[1], list(outs[2:])


def _norm_bwd_specs(tm, d):
    row = pl.BlockSpec((tm, d), lambda i: (i, 0))
    vec = pl.BlockSpec((1, d), lambda i: (0, 0))
    return [row, vec, row], [row, vec]


def _norm_bwd_tail(da, h_ref, g_ref, dhi_ref, dho_ref, dgain_ref):
    dx, dgain = _rms_bwd(h_ref[...], g_ref[...], da)
    dho_ref[...] = dhi_ref[...] + dx

    @pl.when(pl.program_id(0) == 0)
    def _():
        dgain_ref[...] = jnp.zeros_like(dgain_ref)

    dgain_ref[...] += dgain


def ffn_bwd_input(name, dg, du, wg_all, wu_all, layer, h, gain, dh_in):
    _, t, f4 = dg.shape
    d = h.shape[1]
    tm = _tile(t, 512)
    h_spec = pl.BlockSpec((N_CHIPS, tm, f4), lambda i: (0, i, 0))
    w_spec = _resident((N_CHIPS, d, f4), lambda i: (0, layer, 0))
    tail_in, tail_out = _norm_bwd_specs(tm, d)

    def body(dg_ref, du_ref, wg_ref, wu_ref, *tail):
        acc = jnp.zeros((tm, d), F32)
        for k in range(N_CHIPS):
            acc = acc + lax.dot_general(dg_ref[k], wg_ref[k], _NT, preferred_element_type=F32)
            acc = acc + lax.dot_general(du_ref[k], wu_ref[k], _NT, preferred_element_type=F32)
        _norm_bwd_tail(acc, *tail)

    return pl.pallas_call(
        body, name=name, grid=(t // tm,), in_specs=[h_spec, h_spec, w_spec, w_spec] + tail_in, out_specs=tail_out,
        out_shape=[S((t, d), F32), S((1, d), F32)], compiler_params=_params(1))(dg, du, wg_all, wu_all, h, gain, dh_in)


def ffn_wgrad_up(name, a, dy):
    t, d = a.shape
    f4 = dy.shape[2]
    tk = _tile(t, 512)
    nt = t // tk

    def body(a_ref, dy_ref, o_ref, acc):
        s = pl.program_id(0)

        @pl.when(s == 0)
        def _():
            acc[...] = jnp.zeros_like(acc)

        at = a_ref[...].T
        for k in range(N_CHIPS):
            acc[k] += jnp.dot(at, dy_ref[k], preferred_element_type=F32)

        @pl.when(s == nt - 1)
        def _():
            o_ref[...] = acc[...].astype(BF16)

    return pl.pallas_call(
        body, name=name, grid=(nt,),
        in_specs=[pl.BlockSpec((tk, d), lambda s: (s, 0)), pl.BlockSpec((N_CHIPS, tk, f4), lambda s: (0, s, 0))],
        out_specs=pl.BlockSpec((N_CHIPS, d, f4), lambda s: (0, 0, 0)), out_shape=S((N_CHIPS, d, f4), BF16),
        scratch_shapes=[pltpu.VMEM((N_CHIPS, d, f4), F32)], compiler_params=_params(1))(a, dy)


def ffn_wgrad_down(name, z, dh):
    _, t, f4 = z.shape
    d = dh.shape[1]
    tk = _tile(t, 512)
    nt = t // tk

    def body(z_ref, dh_ref, o_ref, acc):
        s = pl.program_id(0)

        @pl.when(s == 0)
        def _():
            acc[...] = jnp.zeros_like(acc)

        dhb = dh_ref[...].astype(BF16)
        for k in range(N_CHIPS):
            acc[k] += lax.dot_general(z_ref[k], dhb, _TN, preferred_element_type=F32)

        @pl.when(s == nt - 1)
        def _():
            o_ref[...] = acc[...].astype(BF16)

    return pl.pallas_call(
        body, name=name, grid=(nt,),
        in_specs=[pl.BlockSpec((N_CHIPS, tk, f4), lambda s: (0, s, 0)), pl.BlockSpec((tk, d), lambda s: (s, 0))],
        out_specs=pl.BlockSpec((N_CHIPS, f4, d), lambda s: (0, 0, 0)), out_shape=S((N_CHIPS, f4, d), BF16),
        scratch_shapes=[pltpu.VMEM((N_CHIPS, f4, d), F32)], compiler_params=_params(1))(z, dh)


def conv_in_proj(name, a, w, conv_w):
    t, d = a.shape
    tm = _tile(t, 256)
    keep = 8

    def body(a_ref, w_ref, cw_ref, bcx_ref, y_ref, u_ref):
        @pl.when(pl.program_id(0) == 0)
        def _():
            u_ref[0:keep, :] = jnp.zeros((keep, d), F32)

        av = a_ref[...]
        b, c, x = [jnp.dot(av, w_ref[:, j * d:(j + 1) * d], preferred_element_type=F32) for j in range(3)]
        for j, part in enumerate((b, c, x)):
            bcx_ref[j] = part.astype(bcx_ref.dtype)
        u_ref[keep:keep + tm, :] = c * x
        uc = (cw_ref[0:1, :] * u_ref[keep - 2:keep - 2 + tm, :] + cw_ref[1:2, :] * u_ref[keep - 1:keep - 1 + tm, :]
              + cw_ref[2:3, :] * u_ref[keep:keep + tm, :])
        y_ref[...] = (b * uc).astype(BF16)
        u_ref[0:keep, :] = u_ref[tm:tm + keep, :]

    return pl.pallas_call(
        body, name=name, grid=(t // tm,),
        in_specs=[pl.BlockSpec((tm, d), lambda i: (i, 0)), _resident((d, 3 * d), lambda i: (0, 0)), pl.BlockSpec((3, d), lambda i: (0, 0))],
        out_specs=[pl.BlockSpec((3, tm, d), lambda i: (0, i, 0)), pl.BlockSpec((tm, d), lambda i: (i, 0))],
        out_shape=[S((3, t, d), CONV_SAVED_DTYPE), S((t, d), BF16)], scratch_shapes=[pltpu.VMEM((tm + keep, d), F32)],
        compiler_params=_params(1))(a, w, conv_w)


def conv_in_bwd_input(name, dbcx, w, h, gain, dh_in):
    _, t, d = dbcx.shape
    tm = _tile(t, 512)
    tail_in, tail_out = _norm_bwd_specs(tm, d)

    def body(g_ref, w_ref, *tail):
        acc = jnp.zeros((tm, d), F32)
        for j in range(3):
            acc = acc + lax.dot_general(g_ref[j], w_ref[:, j * d:(j + 1) * d], _NT, preferred_element_type=F32)
        _norm_bwd_tail(acc, *tail)

    return pl.pallas_call(
        body, name=name, grid=(t // tm,),
        in_specs=[pl.BlockSpec((3, tm, d), lambda i: (0, i, 0)), _resident((d, 3 * d), lambda i: (0, 0))] + tail_in,
        out_specs=tail_out, out_shape=[S((t, d), F32), S((1, d), F32)], compiler_params=_params(1))(dbcx, w, h, gain, dh_in)


def linear_nt_norm_bwd(name, dy, w, h, gain, dh_in, parts=()):
    t, n = dy.shape
    k = w.shape[0]
    tm = _tile(t, 512)
    tail_in, tail_out = _norm_bwd_specs(tm, k)
    m = len(parts)

    def body(dy_ref, w_ref, h_ref, g_ref, dhi_ref, *refs):
        if m:
            _ride_along(scatter_ici_copies(refs[:m], refs[m + 2:2 * m + 2], *refs[2 * m + 2:]), (pl.program_id(0),), (t // tm,))
        da = lax.dot_general(dy_ref[...].astype(BF16), w_ref[...], _NT, preferred_element_type=F32)
        _norm_bwd_tail(da, h_ref, g_ref, dhi_ref, *refs[m:m + 2])

    outs = pl.pallas_call(
        body, name=name, grid=(t // tm,),
        in_specs=[pl.BlockSpec((tm, n), lambda i: (i, 0)), _resident((k, n), lambda i: (0, 0))] + tail_in + [ANY] * m,
        out_specs=tail_out + [ANY] * m, out_shape=[S((t, k), F32), S((1, k), F32)] + [S(p.shape, p.dtype) for p in parts],
        scratch_shapes=[pltpu.SemaphoreType.DMA((m, 3)), pltpu.SemaphoreType.DMA((m, 3))] if m else [],
        compiler_params=_params(1))(dy, w, h, gain, dh_in, *parts)
    return outs[0], outs[1], list(outs[2:])


def conv_in_wgrad(name, a, dbcx):
    t, d = a.shape
    tk = _tile(t, 512)
    nt = t // tk
    n4 = 3 * d // N_CHIPS

    def body(a_ref, g_ref, o_ref, acc):
        s = pl.program_id(0)

        @pl.when(s == 0)
        def _():
            acc[...] = jnp.zeros_like(acc)

        at = a_ref[...].T
        for j in range(3):
            acc[:, j * d:(j + 1) * d] += jnp.dot(at, g_ref[j], preferred_element_type=F32)

        @pl.when(s == nt - 1)
        def _():
            for k in range(N_CHIPS):
                o_ref[k] = acc[:, k * n4:(k + 1) * n4].astype(BF16)

    return pl.pallas_call(
        body, name=name, grid=(nt,),
        in_specs=[pl.BlockSpec((tk, d), lambda s: (s, 0)), pl.BlockSpec((3, tk, d), lambda s: (0, s, 0))],
        out_specs=pl.BlockSpec((N_CHIPS, d, n4), lambda s: (0, 0, 0)), out_shape=S((N_CHIPS, d, n4), BF16),
        scratch_shapes=[pltpu.VMEM((d, 3 * d), F32)], compiler_params=_params(1))(a, dbcx)


def _rstd(x):
    return lax.rsqrt(jnp.mean(x * x, axis=-1, keepdims=True) + RMS_EPS)


def _rms_bwd(x, g, dy):
    r = _rstd(x)
    xhat = x * r
    dgain = jnp.sum(dy * xhat, axis=0, keepdims=True)
    dxh = dy * g
    dx = r * (dxh - xhat * jnp.mean(dxh * xhat, axis=-1, keepdims=True))
    return dx, dgain


def rms_fwd(name, h, g):
    t, d = h.shape
    tr = _tile(t, 512)

    def body(h_ref, g_ref, a_ref):
        x = h_ref[...]
        a_ref[...] = (x * _rstd(x) * g_ref[...]).astype(BF16)

    return pl.pallas_call(
        body, name=name, grid=(t // tr,),
        in_specs=[pl.BlockSpec((tr, d), lambda i: (i, 0)), pl.BlockSpec((1, d), lambda i: (0, 0))],
        out_specs=pl.BlockSpec((tr, d), lambda i: (i, 0)), out_shape=S((t, d), BF16), compiler_params=_params(1))(h, g)


def ffn_down_loss(name, z, wd_all, layer, resid, gain, target):
    _, t, f4 = z.shape
    d = wd_all.shape[2]
    tm = _tile(t, 512)

    def body(z_ref, wd_ref, r_ref, g_ref, t_ref, dh_ref, dg_ref, loss_ref):
        x = r_ref[...]
        for k in range(N_CHIPS):
            x = x + jnp.dot(z_ref[k], wd_ref[k], preferred_element_type=F32)
        g = g_ref[...]
        r = _rstd(x)
        xhat = x * r
        err = xhat * g - t_ref[...]
        dy = err * (1.0 / d)
        dxh = dy * g
        dh_ref[...] = r * (dxh - xhat * jnp.mean(dxh * xhat, axis=-1, keepdims=True))

        @pl.when(pl.program_id(0) == 0)
        def _():
            dg_ref[...] = jnp.zeros_like(dg_ref)
            loss_ref[...] = jnp.zeros_like(loss_ref)

        dg_ref[...] += jnp.sum(dy * xhat, axis=0, keepdims=True)
        per_token = jnp.mean(err * err, axis=-1, keepdims=True)
        loss_ref[...] += 0.5 * jnp.sum(per_token, axis=0, keepdims=True)

    row = pl.BlockSpec((tm, d), lambda i: (i, 0))
    vec = pl.BlockSpec((1, d), lambda i: (0, 0))
    one = pl.BlockSpec((1, 1), lambda i: (0, 0))
    return pl.pallas_call(
        body, name=name, grid=(t // tm,),
        in_specs=[pl.BlockSpec((N_CHIPS, tm, f4), lambda i: (0, i, 0)), _resident((N_CHIPS, f4, d), lambda i: (0, layer, 0)), row, vec, row],
        out_specs=[row, vec, one], out_shape=[S((t, d), F32), S((1, d), F32), S((1, 1), F32)],
        compiler_params=_params(1))(z, wd_all, resid, gain, target)


def mla_in_proj(name, a, w, g_cq, g_ckv, cos, sin):
    t, d = a.shape
    n = w.shape[1]
    ql, kl = g_cq.shape[1], g_ckv.shape[1]
    tr = _tile(t, 512)

    def body(a_ref, w_ref, gq_ref, gk_ref, c_ref, s_ref, p_ref, cq_ref, ckv_ref, kr_ref):
        p_ref[...] = jnp.dot(a_ref[...], w_ref[...], preferred_element_type=F32)
        xq = p_ref[:, 0:ql]
        cq_ref[...] = (xq * _rstd(xq) * gq_ref[...]).astype(BF16)
        xk = p_ref[:, ql:ql + kl]
        ckv_ref[...] = (xk * _rstd(xk) * gk_ref[...]).astype(BF16)
        k1 = p_ref[:, ql + kl:ql + kl + HALF]
        k2 = p_ref[:, ql + kl + HALF:ql + kl + ROPE]
        c = c_ref[...]
        s = s_ref[...]
        kr_ref[:, 0:HALF] = k1 * c - k2 * s
        kr_ref[:, HALF:ROPE] = k1 * s + k2 * c

    def row(w):
        return pl.BlockSpec((tr, w), lambda i: (i, 0))

    def vec(w):
        return pl.BlockSpec((1, w), lambda i: (0, 0))

    return pl.pallas_call(
        body, name=name, grid=(t // tr,),
        in_specs=[row(d), _resident((d, n), lambda i: (0, 0)), vec(ql), vec(kl), row(HALF), row(HALF)],
        out_specs=[row(n), row(ql), row(kl), row(ROPE)],
        out_shape=[S((t, n), F32), S((t, ql), BF16), S((t, kl), BF16), S((t, ROPE), F32)],
        compiler_params=_params(1))(a, w, g_cq, g_ckv, cos, sin)


def mla_mid_bwd(name, proj, g_cq, g_ckv, dcq, dckv, dkr, cos, sin):
    t, n = proj.shape
    ql, kl = g_cq.shape[1], g_ckv.shape[1]
    tr = _tile(t, 512)

    def body(p_ref, gq_ref, gk_ref, dcq_ref, dckv_ref, dkr_ref, c_ref, s_ref, dp_ref, dgq_ref, dgk_ref):
        dxq, dgq = _rms_bwd(p_ref[:, 0:ql], gq_ref[...], dcq_ref[...])
        dp_ref[:, 0:ql] = dxq.astype(BF16)
        dxk, dgk = _rms_bwd(p_ref[:, ql:ql + kl], gk_ref[...], dckv_ref[...])
        dp_ref[:, ql:ql + kl] = dxk.astype(BF16)
        d1 = dkr_ref[:, 0:HALF]
        d2 = dkr_ref[:, HALF:ROPE]
        c = c_ref[...]
        s = s_ref[...]
        dp_ref[:, ql + kl:ql + kl + HALF] = (d1 * c + d2 * s).astype(BF16)
        dp_ref[:, ql + kl + HALF:ql + kl + ROPE] = (d2 * c - d1 * s).astype(BF16)

        @pl.when(pl.program_id(0) == 0)
        def _():
            dgq_ref[...] = jnp.zeros_like(dgq_ref)
            dgk_ref[...] = jnp.zeros_like(dgk_ref)

        dgq_ref[...] += dgq
        dgk_ref[...] += dgk

    def row(w):
        return pl.BlockSpec((tr, w), lambda i: (i, 0))

    def vec(w):
        return pl.BlockSpec((1, w), lambda i: (0, 0))

    return pl.pallas_call(
        body, name=name, grid=(t // tr,),
        in_specs=[row(n), vec(ql), vec(kl), row(ql), row(kl), row(ROPE), row(HALF), row(HALF)],
        out_specs=[row(n), vec(ql), vec(kl)], out_shape=[S((t, n), BF16), S((1, ql), F32), S((1, kl), F32)],
        compiler_params=_params(1))(proj, g_cq, g_ckv, dcq, dckv, dkr, cos, sin)


def qkv_heads(name, q, kv, kr, cos, sin, shards=()):
    t = q.shape[0]
    tr = _tile(t, 256)
    n = len(shards)

    def body(q_ref, kv_ref, kr_ref, c_ref, s_ref, *refs):
        src = refs[:n]
        qo_ref, ko_ref, vo_ref = refs[n:n + 3]
        if n:
            _ride_along(gather_ici_copies(src, refs[n + 3:2 * n + 3], *refs[2 * n + 3:]), (pl.program_id(0),), (t // tr,))
        c = c_ref[...]
        s = s_ref[...]
        krb = kr_ref[...].astype(BF16)
        for h in range(N_HEADS):
            q0 = h * QK
            qo_ref[h, :, 0:NOPE] = q_ref[:, q0:q0 + NOPE].astype(BF16)
            q1 = q_ref[:, q0 + NOPE:q0 + NOPE + HALF]
            q2 = q_ref[:, q0 + NOPE + HALF:q0 + QK]
            qo_ref[h, :, NOPE:NOPE + HALF] = (q1 * c - q2 * s).astype(BF16)
            qo_ref[h, :, NOPE + HALF:QK] = (q1 * s + q2 * c).astype(BF16)
            k0 = h * (NOPE + VDIM)
            ko_ref[h, :, 0:NOPE] = kv_ref[:, k0:k0 + NOPE]
            ko_ref[h, :, NOPE:QK] = krb
            vo_ref[h] = kv_ref[:, k0 + NOPE:k0 + NOPE + VDIM]

    def row(w):
        return pl.BlockSpec((tr, w), lambda i: (i, 0))

    def heads(w):
        return pl.BlockSpec((N_HEADS, tr, w), lambda i: (0, i, 0))

    outs = pl.pallas_call(
        body, name=name, grid=(t // tr,),
        in_specs=[row(N_HEADS * QK), row(N_HEADS * (NOPE + VDIM)), row(ROPE), row(HALF), row(HALF)] + [ANY] * n,
        out_specs=[heads(QK), heads(QK), heads(VDIM)] + [ANY] * n,
        out_shape=[S((N_HEADS, t, QK), BF16), S((N_HEADS, t, QK), BF16), S((N_HEADS, t, VDIM), BF16)]
        + [S((N_CHIPS,) + s.shape, s.dtype) for s in shards],
        scratch_shapes=[pltpu.SemaphoreType.DMA((n, 3)), pltpu.SemaphoreType.DMA((n, 3))] if n else [],
        compiler_params=_params(1))(q, kv, kr, cos, sin, *shards)
    return outs[0], outs[1], outs[2], list(outs[3:])


def qkv_heads_bwd(name, dq_h, dk_h, dv_h, cos, sin, halves=(), targets=(), where=()):
    t = dq_h.shape[1]
    tr = _tile(t, 256)
    n, nt = len(halves), len(targets)

    def body(dq_ref, dk_ref, dv_ref, c_ref, s_ref, *refs):
        q_ref, kv_ref, kr_ref = refs[n:n + 3]
        if n:
            _ride_along(join_copies(refs[:n], refs[n + 3:n + 3 + nt], where, *refs[n + 3 + nt:]), (pl.program_id(0),), (t // tr,))
        c = c_ref[...]
        s = s_ref[...]
        dkr = jnp.zeros((tr, ROPE), F32)
        for h in range(N_HEADS):
            q0 = h * QK
            q_ref[:, q0:q0 + NOPE] = dq_ref[h, :, 0:NOPE].astype(BF16)
            d1 = dq_ref[h, :, NOPE:NOPE + HALF]
            d2 = dq_ref[h, :, NOPE + HALF:QK]
            q_ref[:, q0 + NOPE:q0 + NOPE + HALF] = (d1 * c + d2 * s).astype(BF16)
            q_ref[:, q0 + NOPE + HALF:q0 + QK] = (d2 * c - d1 * s).astype(BF16)
            k0 = h * (NOPE + VDIM)
            kv_ref[:, k0:k0 + NOPE] = dk_ref[h, :, 0:NOPE].astype(BF16)
            kv_ref[:, k0 + NOPE:k0 + NOPE + VDIM] = dv_ref[h].astype(BF16)
            dkr = dkr + dk_ref[h, :, NOPE:QK]
        kr_ref[...] = dkr

    def row(w):
        return pl.BlockSpec((tr, w), lambda i: (i, 0))

    def heads(w):
        return pl.BlockSpec((N_HEADS, tr, w), lambda i: (0, i, 0))

    outs = pl.pallas_call(
        body, name=name, grid=(t // tr,),
        in_specs=[heads(QK), heads(QK), heads(VDIM), row(HALF), row(HALF)] + [ANY] * n,
        out_specs=[row(N_HEADS * QK), row(N_HEADS * (NOPE + VDIM)), row(ROPE)] + [ANY] * nt,
        out_shape=[S((t, N_HEADS * QK), BF16), S((t, N_HEADS * (NOPE + VDIM)), BF16), S((t, ROPE), F32)]
        + [S(tg, F32) for tg in targets],
        scratch_shapes=[pltpu.SemaphoreType.DMA((n,)), pltpu.SemaphoreType.DMA((n,))] if n else [],
        compiler_params=_params(1))(dq_h, dk_h, dv_h, cos, sin, *halves)
    return outs[0], outs[1], outs[2], _fill_own_halves(outs[3:], halves, where)


def _chunk_mask_t(q_start, k_start, bq, bk):
    kc = (k_start + lax.broadcasted_iota(jnp.int32, (bk, bq), 0)) // CHUNK
    qc = (q_start + lax.broadcasted_iota(jnp.int32, (bk, bq), 1)) // CHUNK
    return kc <= qc


def attention_fwd(name, q, k, v, shards=()):
    nh, t, _ = q.shape
    blk = ATT_BLOCK
    nq = t // blk
    n = len(shards)

    def body(q_ref, k_ref, v_ref, *refs):
        src = refs[:n]
        o_ref, lse_ref = refs[n:n + 2]
        dst = refs[n + 2:2 * n + 2]
        m_ref, l_ref, acc_ref, s_buf, p_buf, alpha_buf, bias_ref = refs[2 * n + 2:2 * n + 9]
        i = pl.program_id(1)
        if n:
            send_sems, recv_sems = refs[2 * n + 9:]
            _ride_along(gather_ici_copies(src, dst, send_sems, recv_sems), (pl.program_id(0), i), (nh, nq))

        @pl.when((pl.program_id(0) == 0) & (i == 0))
        def _():
            bias_ref[...] = jnp.where(_chunk_mask_t(0, 0, blk, blk), 0.0, MASK_VALUE)

        m_ref[...] = jnp.full_like(m_ref, MASK_VALUE)
        l_ref[...] = jnp.zeros_like(l_ref)
        acc_ref[...] = jnp.zeros_like(acc_ref)

        def rows(b):
            return pl.ds(pl.multiple_of(b * blk, blk), blk)

        def scores(b, slot):
            s_buf[slot] = lax.dot_general(k_ref[rows(b), :], q_ref[...], _NT, preferred_element_type=F32)

        def softmax(slot, diagonal):
            s = s_buf[slot]
            if diagonal:
                s = s + bias_ref[...]
            m_old = m_ref[...]
            m_new = jnp.maximum(m_old, jnp.max(s, axis=0, keepdims=True))
            p = jnp.exp2((s - m_new) * SCORE_SCALE_LOG2)
            alpha = jnp.exp2((m_old - m_new) * SCORE_SCALE_LOG2)
            l_ref[...] = alpha * l_ref[...] + jnp.sum(p, axis=0, keepdims=True)
            m_ref[...] = m_new
            alpha_buf[slot] = alpha
            p_buf[slot] = p.astype(BF16)

        def values(b, slot):
            pv = lax.dot_general(v_ref[rows(b), :], p_buf[slot], _TN, preferred_element_type=F32)
            acc_ref[...] = alpha_buf[slot] * acc_ref[...] + pv

        def step(t, slot):
            values(t - 2, slot)
            softmax(1 - slot, False)
            scores(t, slot)

        scores(0, 0)

        @pl.when(i == 0)
        def _():
            softmax(0, True)
            values(0, 0)

        @pl.when(i > 0)
        def _():
            scores(1, 1)
            softmax(0, False)
            steady = i - 1

            def pair(u, carry):
                step(2 + 2 * u, 0)
                step(3 + 2 * u, 1)
                return carry

            lax.fori_loop(0, steady // 2, pair, 0)

            @pl.when(steady % 2 == 1)
            def _():
                step(i, 0)

            last = i % 2
            softmax(last, True)
            values(i - 1, 1 - last)
            values(i, last)

        l = l_ref[...]
        o_ref[...] = (acc_ref[...] / l).T
        lse_ref[...] = m_ref[...] * SCORE_SCALE + jnp.log(l)

    outs = pl.pallas_call(
        body, name=name, grid=(nh, nq),
        in_specs=[pl.BlockSpec((None, blk, QK), lambda h, i: (h, i, 0)), pl.BlockSpec((None, t, QK), lambda h, i: (h, 0, 0)),
                  pl.BlockSpec((None, t, VDIM), lambda h, i: (h, 0, 0))] + [ANY] * n,
        out_specs=[pl.BlockSpec((blk, VDIM), lambda h, i: (i, h)),
                   pl.BlockSpec((None, None, 1, blk), lambda h, i: (h, i, 0, 0))] + [ANY] * n,
        out_shape=[S((t, nh * VDIM), F32), S((nh, nq, 1, blk), F32)] + [S((N_CHIPS,) + s.shape, s.dtype) for s in shards],
        scratch_shapes=[pltpu.VMEM((1, blk), F32), pltpu.VMEM((1, blk), F32), pltpu.VMEM((VDIM, blk), F32),
                        pltpu.VMEM((2, blk, blk), F32), pltpu.VMEM((2, blk, blk), BF16), pltpu.VMEM((2, 1, blk), F32),
                        pltpu.VMEM((blk, blk), F32)]
        + ([pltpu.SemaphoreType.DMA((n, 3)), pltpu.SemaphoreType.DMA((n, 3))] if n else []),
        compiler_params=_params(2))(q, k, v, *shards)
    return outs[0], outs[1], list(outs[2:])


def attention_out_bwd(name, dh, w_o, o, swap=()):
    t, d = dh.shape
    n = w_o.shape[0]
    blk = ATT_BLOCK
    m = len(swap)

    def body(dh_ref, w_ref, o_ref, *refs):
        do_ref, d_ref = refs[m:m + 2]
        if m:
            _ride_along(swap_copies(refs[:m], refs[m + 2:2 * m + 2], *refs[2 * m + 2:]), (pl.program_id(0),), (t // blk,))
        do_ref[...] = lax.dot_general(dh_ref[...].astype(BF16), w_ref[...], _NT, preferred_element_type=F32)
        for h in range(N_HEADS):
            cols = slice(h * VDIM, (h + 1) * VDIM)
            d_ref[h] = jnp.sum((do_ref[:, cols] * o_ref[:, cols]).T, axis=0, keepdims=True)

    tile = pl.BlockSpec((blk, n), lambda i: (i, 0))
    outs = pl.pallas_call(
        body, name=name, grid=(t // blk,),
        in_specs=[pl.BlockSpec((blk, d), lambda i: (i, 0)), _resident((n, d), lambda i: (0, 0)), tile] + [ANY] * m,
        out_specs=[tile, pl.BlockSpec((N_HEADS, None, 1, blk), lambda i: (0, i, 0, 0))] + [ANY] * m,
        out_shape=[S((t, n), F32), S((N_HEADS, t // blk, 1, blk), F32)] + _swap_shapes(swap),
        scratch_shapes=[pltpu.SemaphoreType.DMA((m,)), pltpu.SemaphoreType.DMA((m,))] if m else [],
        compiler_params=_params(1))(dh, w_o, o, *swap)
    return outs[0], outs[1], list(outs[2:])


def attention_bwd(name, q, k, v, do, lse, delta, parts=()):
    nh, t, _ = q.shape
    blk = ATT_BLOCK
    nq = t // blk
    n_pairs = nq * (nq + 1) // 2
    n = len(parts)
    scale = SCORE_SCALE

    def body(q_ref, k_ref, v_ref, do_ref, lse_ref, dl_ref, *refs):
        src = refs[:n]
        dq_ref, dk_ref, dv_ref = refs[n:n + 3]
        dst = refs[n + 3:2 * n + 3]
        s_buf, dp_buf, p_buf, ds_buf, bias_ref = refs[2 * n + 3:2 * n + 8]
        if n:
            send_sems, recv_sems = refs[2 * n + 8:]
            _ride_along(scatter_ici_copies(src, dst, send_sems, recv_sems), (pl.program_id(0),), (nh,))

        @pl.when(pl.program_id(0) == 0)
        def _():
            bias_ref[...] = jnp.where(_chunk_mask_t(0, 0, blk, blk), 0.0, MASK_VALUE)

        dq_ref[...] = jnp.zeros_like(dq_ref)
        dk_ref[...] = jnp.zeros_like(dk_ref)
        dv_ref[...] = jnp.zeros_like(dv_ref)

        def rows(x):
            return pl.ds(pl.multiple_of(x * blk, blk), blk)

        def after(jb):
            j, b = jb
            wrap = b == nq - 1 - j
            return jnp.where(wrap, j + 1, j), jnp.where(wrap, 0, b + 1)

        def products(jb, slot):
            j, b = jb
            s_buf[slot] = lax.dot_general(k_ref[rows(j), :], q_ref[rows(j + b), :], _NT, preferred_element_type=F32)
            dp_buf[slot] = lax.dot_general(v_ref[rows(j), :], do_ref[rows(j + b), :].astype(BF16), _NT, preferred_element_type=F32)

        def softmax_bwd(jb, slot):
            j, b = jb
            s = s_buf[slot] + bias_ref[...] * (b == 0).astype(F32)
            p = jnp.exp2(s * SCORE_SCALE_LOG2 - lse_ref[j + b] * LOG2_E)
            p_buf[slot] = p.astype(BF16)
            ds_buf[slot] = (p * (dp_buf[slot] - dl_ref[j + b]) * scale).astype(BF16)

        def gradients(jb, slot):
            j, b = jb
            dv_ref[rows(j), :] += jnp.dot(p_buf[slot], do_ref[rows(j + b), :].astype(BF16), preferred_element_type=F32)
            dk_ref[rows(j), :] += jnp.dot(ds_buf[slot], q_ref[rows(j + b), :], preferred_element_type=F32)
            dq_ref[rows(j + b), :] += lax.dot_general(ds_buf[slot], k_ref[rows(j), :], _TN, preferred_element_type=F32)

        def step(state, slot):
            third, second, first = state
            gradients(third, slot)
            softmax_bwd(second, 1 - slot)
            products(first, slot)
            return second, first, after(first)

        zero = jnp.int32(0)
        pair0 = (zero, zero)
        products(pair0, 0)
        if n_pairs == 1:
            softmax_bwd(pair0, 0)
            gradients(pair0, 0)
        else:
            pair1 = after(pair0)
            products(pair1, 1)
            softmax_bwd(pair0, 0)
            steady = n_pairs - 2
            state = lax.fori_loop(0, steady // 2, lambda u, st: step(step(st, 0), 1), (pair0, pair1, after(pair1)))
            if steady % 2:
                state = step(state, 0)
            before_last, last_pair, _ = state
            last = (n_pairs - 1) % 2
            softmax_bwd(last_pair, last)
            gradients(before_last, 1 - last)
            gradients(last_pair, last)

    head = lambda w: pl.BlockSpec((None, t, w), lambda h: (h, 0, 0))
    stats = pl.BlockSpec((None, nq, 1, blk), lambda h: (h, 0, 0, 0))
    outs = pl.pallas_call(
        body, name=name, grid=(nh,),
        in_specs=[head(QK), head(QK), head(VDIM), pl.BlockSpec((t, VDIM), lambda h: (0, h)), stats, stats] + [ANY] * n,
        out_specs=[head(QK), head(QK), head(VDIM)] + [ANY] * n,
        out_shape=[S((nh, t, QK), F32), S((nh, t, QK), F32), S((nh, t, VDIM), F32)] + [S(p.shape, p.dtype) for p in parts],
        scratch_shapes=[pltpu.VMEM((2, blk, blk), F32), pltpu.VMEM((2, blk, blk), F32), pltpu.VMEM((2, blk, blk), BF16),
                        pltpu.VMEM((2, blk, blk), BF16), pltpu.VMEM((blk, blk), F32)]
        + ([pltpu.SemaphoreType.DMA((n, 3)), pltpu.SemaphoreType.DMA((n, 3))] if n else []),
        compiler_params=_params(1, VMEM_LIMIT_WHOLE_HEAD))(q, k, v, do, lse, delta, *parts)
    return outs[0], outs[1], outs[2], list(outs[3:])


def _shift_down(u, s):
    rows = lax.broadcasted_iota(jnp.int32, u.shape, 0)
    return jnp.where(rows >= s, pltpu.roll(u, s, 0), 0.0)


def _shift_up(u, s):
    n = u.shape[0]
    rows = lax.broadcasted_iota(jnp.int32, u.shape, 0)
    return jnp.where(rows < n - s, pltpu.roll(u, n - s, 0), 0.0)


def _conv_specs(t, d, lanes):
    slab = lambda part: pl.BlockSpec((None, t, lanes), lambda j, part=part: (part, 0, j))
    return slab, pl.BlockSpec((3, lanes), lambda j: (0, j)), pl.BlockSpec((t, lanes), lambda j: (0, j))


def conv_bwd(name, bcx, w, dy):
    _, t, d = bcx.shape
    lanes = _tile(d, 128, 128)
    slab, w_spec, col = _conv_specs(t, d, lanes)

    def body(b_ref, c_ref, x_ref, w_ref, dy_ref, d_ref, dw_ref):
        c = c_ref[...].astype(F32)
        x = x_ref[...].astype(F32)
        dyv = dy_ref[...]
        u = c * x
        u1 = _shift_down(u, 1)
        u2 = _shift_down(u, 2)
        w0, w1, w2 = w_ref[0:1, :], w_ref[1:2, :], w_ref[2:3, :]
        d_ref[0] = (dyv * (w0 * u2 + w1 * u1 + w2 * u)).astype(BF16)
        duc = dyv * b_ref[...].astype(F32)
        dw_ref[0:1, :] = jnp.sum(duc * u2, axis=0, keepdims=True)
        dw_ref[1:2, :] = jnp.sum(duc * u1, axis=0, keepdims=True)
        dw_ref[2:3, :] = jnp.sum(duc * u, axis=0, keepdims=True)
        du = w2 * duc + w1 * _shift_up(duc, 1) + w0 * _shift_up(duc, 2)
        d_ref[1] = (du * x).astype(BF16)
        d_ref[2] = (du * c).astype(BF16)

    return pl.pallas_call(
        body, name=name, grid=(d // lanes,), in_specs=[slab(0), slab(1), slab(2), w_spec, col],
        out_specs=[pl.BlockSpec((3, t, lanes), lambda j: (0, 0, j)), w_spec], out_shape=[S((3, t, d), BF16), S((3, d), F32)],
        compiler_params=_params(1))(bcx, bcx, bcx, w, dy)


def _adamw_update(w, g, m, v):
    m_new = ADAM_B1 * m + (1.0 - ADAM_B1) * g
    v_new = ADAM_B2 * v + (1.0 - ADAM_B2) * (g * g)
    m_hat = m_new / (1.0 - ADAM_B1 ** ADAM_STEP)
    v_hat = v_new / (1.0 - ADAM_B2 ** ADAM_STEP)
    return -ADAM_LR * (m_hat / (jnp.sqrt(v_hat) + ADAM_EPS) + ADAM_WD * w), m_new, v_new


def adamw(name, w, g, m, v):
    r, c = w.shape
    tr = _tile(r, 512)

    def body(w_ref, g_ref, m_ref, v_ref, d_ref, mo_ref, vo_ref):
        d_ref[...], mo_ref[...], vo_ref[...] = _adamw_update(w_ref[...], g_ref[...], m_ref[...], v_ref[...])

    blk = pl.BlockSpec((tr, c), lambda i: (i, 0))
    return pl.pallas_call(
        body, name=name, grid=(r // tr,), in_specs=[blk] * 4, out_specs=[blk] * 3, out_shape=[S((r, c), F32)] * 3,
        compiler_params=_params(1))(w, g, m, v)


def adamw_swapped(name, wt, g, mt, vt):
    nl, c, r = wt.shape
    tr = _tile(r, 512, 128)
    nr = r // tr

    def body(w_ref, g_ref, m_ref, v_ref, go_ref, d_ref, mo_ref, vo_ref):
        gt = g_ref[...].T
        go_ref[...] = gt
        d_ref[...], mo_ref[...], vo_ref[...] = _adamw_update(w_ref[...], gt, m_ref[...], v_ref[...])

    swapped = pl.BlockSpec((None, c, tr), lambda l, i: (l, 0, i))
    return pl.pallas_call(
        body, name=name, grid=(nl, nr),
        in_specs=[swapped, pl.BlockSpec((tr, c), lambda l, i: (l * nr + i, 0)), swapped, swapped],
        out_specs=[swapped] * 4, out_shape=[S((nl, c, r), F32)] * 4, compiler_params=_params(2))(wt, g, mt, vt)


def _place():
    x, y, c = lax.axis_index("x"), lax.axis_index("y"), lax.axis_index("c")
    other_chips = [(1 - x, y), (x, 1 - y), (1 - x, 1 - y)]
    return x, y, c, other_chips


def _half(c, rows):
    return pl.ds(pl.multiple_of(c * (rows // 2), 16), rows // 2)


def gather_weight_shards(shards):
    n = len(shards)

    def body(*refs):
        src = refs[:n]
        dst = refs[n:2 * n]
        send_sems, recv_sems = refs[2 * n:]
        x, y, c, chips = _place()
        me = 2 * x + y
        sibling = (x, y, 1 - c)

        def copy(i, slot, half_of, sem, to, from_input=False):
            rows = _half(half_of, src[i].shape[0])
            return pltpu.make_async_remote_copy(
                src_ref=src[i].at[rows] if from_input else dst[i].at[slot, rows], dst_ref=dst[i].at[slot, rows],
                send_sem=send_sems.at[i, sem], recv_sem=recv_sems.at[i, sem], device_id=to, device_id_type=MESH)

        sent = []
        for i in range(n):
            for j, chip in enumerate(chips):
                sent.append(copy(i, me, c, j, (*chip, c), from_input=True))
                sent[-1].start()
        for i in range(n):
            for j, (px, py) in enumerate(chips):
                copy(i, 2 * px + py, c, j, sibling).wait_recv()
                sent.append(copy(i, 2 * px + py, c, 3 + j, sibling))
                sent[-1].start()
        for i in range(n):
            for j, (px, py) in enumerate(chips):
                copy(i, 2 * px + py, 1 - c, 3 + j, sibling).wait_recv()
        for cp in sent:
            cp.wait_send()

    outs = pl.pallas_call(
        body, name="gather_weight_shards", in_specs=[ANY] * n, out_specs=[ANY] * n,
        out_shape=[S((N_CHIPS,) + s.shape, s.dtype) for s in shards],
        scratch_shapes=[pltpu.SemaphoreType.DMA((n, 6)), pltpu.SemaphoreType.DMA((n, 6))],
    )(*shards)
    return _fill_own_slot(outs, [s[None] for s in shards])


def gather_ici_copies(src, dst, send_sems, recv_sems):
    x, y, c, chips = _place()
    me = 2 * x + y
    pairs = []
    for i in range(len(src)):
        rows = _half(c, src[i].shape[0])
        for j, (px, py) in enumerate(chips):
            def copy(slot):
                return pltpu.make_async_remote_copy(
                    src_ref=src[i].at[rows], dst_ref=dst[i].at[slot, rows], send_sem=send_sems.at[i, j],
                    recv_sem=recv_sems.at[i, j], device_id=(px, py, c), device_id_type=MESH)
            pairs.append((copy(me), copy(2 * px + py)))
    return pairs


def scatter_ici_copies(src, dst, send_sems, recv_sems):
    x, y, c, chips = _place()
    me = 2 * x + y
    pairs = []
    for i in range(len(src)):
        for j, (px, py) in enumerate(chips):
            def copy(from_slot, to_slot):
                return pltpu.make_async_remote_copy(
                    src_ref=src[i].at[from_slot], dst_ref=dst[i].at[to_slot], send_sem=send_sems.at[i, j],
                    recv_sem=recv_sems.at[i, j], device_id=(px, py, c), device_id_type=MESH)
            pairs.append((copy(2 * px + py, me), copy(me, 2 * px + py)))
    return pairs


def _ride_along(pairs, grid_ids, grid_sizes):
    first = grid_ids[0] == 0
    last = grid_ids[0] == grid_sizes[0] - 1
    for g, size in zip(grid_ids[1:], grid_sizes[1:]):
        first = first & (g == 0)
        last = last & (g == size - 1)

    @pl.when(first)
    def _():
        for outgoing, _ in pairs:
            outgoing.start()

    @pl.when(last)
    def _():
        for _, incoming in pairs:
            incoming.wait_recv()
        for outgoing, _ in pairs:
            outgoing.wait_send()


def _fill_own_slot(gathered, own):
    me = 2 * lax.axis_index("x") + lax.axis_index("y")
    return [lax.dynamic_update_slice(g, o, (me,) + (0,) * (g.ndim - 1)) for g, o in zip(gathered, own)]


def forward_copies(src, dst, send_sems, recv_sems):
    x, y, c, chips = _place()
    pairs = []
    for i in range(len(src)):
        for j, (px, py) in enumerate(chips):
            def copy(half_of):
                rows = _half(half_of, src[i].shape[1])
                return pltpu.make_async_remote_copy(
                    src_ref=src[i].at[2 * px + py, rows], dst_ref=dst[i].at[2 * px + py, rows], send_sem=send_sems.at[i, j],
                    recv_sem=recv_sems.at[i, j], device_id=(x, y, 1 - c), device_id_type=MESH)
            pairs.append((copy(c), copy(1 - c)))
    return pairs


def attention_out_proj(name, attn, w_o, resid, next_gain, arriving):
    t, kdim = attn.shape
    n = w_o.shape[1]
    tm = _tile(t, 512)
    m = len(arriving)

    def body(x_ref, w_ref, r_ref, g_ref, *refs):
        h_ref, a_ref = refs[m:m + 2]
        _ride_along(forward_copies(refs[:m], refs[m + 2:2 * m + 2], *refs[2 * m + 2:]), (pl.program_id(0),), (t // tm,))
        y = r_ref[...] + jnp.dot(x_ref[...].astype(BF16), w_ref[...], preferred_element_type=F32)
        h_ref[...] = y
        a_ref[...] = (y * _rstd(y) * g_ref[...]).astype(BF16)

    row = pl.BlockSpec((tm, n), lambda i: (i, 0))
    outs = pl.pallas_call(
        body, name=name, grid=(t // tm,),
        in_specs=[pl.BlockSpec((tm, kdim), lambda i: (i, 0)), _resident((kdim, n), lambda i: (0, 0)), row,
                  pl.BlockSpec((1, n), lambda i: (0, 0))] + [ANY] * m,
        out_specs=[row, row] + [ANY] * m, out_shape=[S((t, n), F32), S((t, n), BF16)] + [S(g.shape, g.dtype) for g in arriving],
        input_output_aliases={4 + i: 2 + i for i in range(m)},
        scratch_shapes=[pltpu.SemaphoreType.DMA((m, 3)), pltpu.SemaphoreType.DMA((m, 3))],
        compiler_params=_params(1))(attn, w_o, resid, next_gain, *arriving)
    return outs[0], outs[1], list(outs[2:])


def swap_copies(src, dst, send_sems, recv_sems):
    x, y, c, _ = _place()
    pairs = []
    for i in range(len(src)):
        cp = pltpu.make_async_remote_copy(
            src_ref=src[i].at[:, _half(1 - c, src[i].shape[1]), :], dst_ref=dst[i], send_sem=send_sems.at[i],
            recv_sem=recv_sems.at[i], device_id=(x, y, 1 - c), device_id_type=MESH)
        pairs.append((cp, cp))
    return pairs


def _swap_shapes(grads):
    return [S((g.shape[0], g.shape[1] // 2, g.shape[2]), g.dtype) for g in grads]


def sibling_swap_halves(name, grads):
    n = len(grads)

    def body(*refs):
        pairs = swap_copies(refs[:n], refs[n:2 * n], *refs[2 * n:])
        for outgoing, _ in pairs:
            outgoing.start()
        for _, incoming in pairs:
            incoming.wait_recv()
        for outgoing, _ in pairs:
            outgoing.wait_send()

    return pl.pallas_call(
        body, name=name, in_specs=[ANY] * n, out_specs=[ANY] * n, out_shape=_swap_shapes(grads),
        scratch_shapes=[pltpu.SemaphoreType.DMA((n,)), pltpu.SemaphoreType.DMA((n,))],
    )(*grads)


def add_halves(name, g, rx):
    _, r, cdim = g.shape
    r2 = r // 2
    tr = _tile(r2, 512, 16)
    nb = r2 // tr

    def body(lo_ref, hi_ref, rx_ref, o_ref):
        mine = jnp.where(lax.axis_index("c") == 0, lo_ref[...], hi_ref[...])
        o_ref[...] = (mine.astype(F32) + rx_ref[...].astype(F32)).astype(BF16)

    half = pl.BlockSpec((None, tr, cdim), lambda k, i: (k, i, 0))
    return pl.pallas_call(
        body, name=name, grid=(N_CHIPS, nb),
        in_specs=[half, pl.BlockSpec((None, tr, cdim), lambda k, i: (k, nb + i, 0)), half],
        out_specs=half, out_shape=S((N_CHIPS, r2, cdim), BF16), compiler_params=_params(2))(g, g, rx)


def sum_chips(name, arrived, mine):
    _, r2, cdim = arrived.shape
    tr = _tile(r2, 512, 16)

    def body(a_ref, m_ref, o_ref):
        me = 2 * lax.axis_index("x") + lax.axis_index("y")
        acc = jnp.zeros((tr, cdim), F32)
        for k in range(N_CHIPS):
            acc = acc + jnp.where(me == k, m_ref[k], a_ref[k]).astype(F32)
        o_ref[...] = acc

    slots = pl.BlockSpec((N_CHIPS, tr, cdim), lambda i: (0, i, 0))
    return pl.pallas_call(
        body, name=name, grid=(r2 // tr,), in_specs=[slots, slots],
        out_specs=pl.BlockSpec((tr, cdim), lambda i: (i, 0)), out_shape=S((r2, cdim), F32), compiler_params=_params(1))(arrived, mine)


def join_copies(src, dst, where, send_sems, recv_sems):
    x, y, c, _ = _place()
    pairs = []
    for i in range(len(src)):
        def copy(half_of):
            r2 = src[i].shape[0]
            rows = pl.ds(pl.multiple_of(where[i][1] + half_of * r2, 8), r2)
            return pltpu.make_async_remote_copy(
                src_ref=src[i], dst_ref=dst[where[i][0]].at[rows], send_sem=send_sems.at[i],
                recv_sem=recv_sems.at[i], device_id=(x, y, 1 - c), device_id_type=MESH)
        pairs.append((copy(c), copy(1 - c)))
    return pairs


def _fill_own_halves(targets, halves, where):
    targets = list(targets)
    c = lax.axis_index("c")
    for h, (tgt, first) in zip(halves, where):
        targets[tgt] = lax.dynamic_update_slice(targets[tgt], h, (first + c * h.shape[0], 0))
    return targets


def sibling_join_halves(name, halves, targets, where):
    n = len(halves)

    def body(*refs):
        pairs = join_copies(refs[:n], refs[n:n + len(targets)], where, *refs[n + len(targets):])
        for outgoing, _ in pairs:
            outgoing.start()
        for _, incoming in pairs:
            incoming.wait_recv()
        for outgoing, _ in pairs:
            outgoing.wait_send()

    outs = pl.pallas_call(
        body, name=name, in_specs=[ANY] * n, out_specs=[ANY] * len(targets), out_shape=[S(tg, F32) for tg in targets],
        scratch_shapes=[pltpu.SemaphoreType.DMA((n,)), pltpu.SemaphoreType.DMA((n,))],
    )(*halves)
    return _fill_own_halves(outs, halves, where)


def all_reduce_small(name, packed):
    rows, width = packed.shape

    def body(x_ref, o_ref, gathered, send_sems, recv_sems):
        x, y, c, _ = _place()
        me = 4 * x + 2 * y + c
        gathered[me] = x_ref[...]
        flips = [(fx, fy, fc) for fx in (0, 1) for fy in (0, 1) for fc in (0, 1)][1:]

        def copy(r, slot, to):
            return pltpu.make_async_remote_copy(
                src_ref=x_ref, dst_ref=gathered.at[slot], send_sem=send_sems.at[r], recv_sem=recv_sems.at[r],
                device_id=to, device_id_type=MESH)

        def peer(f):
            return (x ^ f[0], y ^ f[1], c ^ f[2])

        sent = [copy(r, me, peer(f)) for r, f in enumerate(flips)]
        for cp in sent:
            cp.start()
        for r, f in enumerate(flips):
            px, py, pc = peer(f)
            copy(r, 4 * px + 2 * py + pc, peer(f)).wait_recv()
        for cp in sent:
            cp.wait_send()
        acc = gathered[0]
        for k in range(1, N_DEV):
            acc = acc + gathered[k]
        o_ref[...] = acc

    vmem = pl.BlockSpec(memory_space=pltpu.VMEM)
    return pl.pallas_call(
        body, name=name, in_specs=[vmem], out_specs=vmem, out_shape=S((rows, width), F32),
        scratch_shapes=[pltpu.VMEM((N_DEV, rows, width), F32), pltpu.SemaphoreType.DMA((N_DEV - 1,)),
                        pltpu.SemaphoreType.DMA((N_DEV - 1,))],
    )(packed)


def _rope_tables(positions):
    inv_freq = 1.0 / (ROPE_THETA ** (jnp.arange(0, ROPE, 2, dtype=F32) / ROPE))
    ang = positions.astype(F32)[:, None] * inv_freq
    return jnp.cos(ang), jnp.sin(ang)


def _unstack_cols(w):
    k4, k, n4 = w.shape
    return jnp.transpose(w, (1, 0, 2)).reshape(k, k4 * n4)


def _stack_cols(w):
    k, n = w.shape
    return jnp.transpose(w.reshape(k, N_CHIPS, n // N_CHIPS), (1, 0, 2))


def kernel(x, positions, mla_norm, mla_w_in, mla_g_cq, mla_g_ckv, mla_w_uq, mla_w_ukv, mla_w_o, conv_norm, conv_w_in, conv_w, conv_w_out, ffn_norm, ffn_w_gate, ffn_w_up, ffn_w_down, final_norm, loss_target, m_mla_norm, m_mla_w_in, m_mla_g_cq, m_mla_g_ckv, m_mla_w_uq, m_mla_w_ukv, m_mla_w_o, m_conv_norm, m_conv_w_in, m_conv_w, m_conv_w_out, m_ffn_norm, m_ffn_w_gate, m_ffn_w_up, m_ffn_w_down, m_final_norm, v_mla_norm, v_mla_w_in, v_mla_g_cq, v_mla_g_ckv, v_mla_w_uq, v_mla_w_ukv, v_mla_w_o, v_conv_norm, v_conv_w_in, v_conv_w, v_conv_w_out, v_ffn_norm, v_ffn_w_gate, v_ffn_w_up, v_ffn_w_down, v_final_norm):
    weights = dict(mla_norm=mla_norm, mla_w_in=mla_w_in, mla_g_cq=mla_g_cq, mla_g_ckv=mla_g_ckv, mla_w_uq=mla_w_uq,
                   mla_w_ukv=mla_w_ukv, mla_w_o=mla_w_o, conv_norm=conv_norm, conv_w_in=conv_w_in, conv_w=conv_w,
                   conv_w_out=conv_w_out, ffn_norm=ffn_norm, ffn_w_gate=ffn_w_gate, ffn_w_up=ffn_w_up,
                   ffn_w_down=ffn_w_down, final_norm=final_norm)
    m_in = dict(mla_norm=m_mla_norm, mla_w_in=m_mla_w_in, mla_g_cq=m_mla_g_cq, mla_g_ckv=m_mla_g_ckv, mla_w_uq=m_mla_w_uq,
                mla_w_ukv=m_mla_w_ukv, mla_w_o=m_mla_w_o, conv_norm=m_conv_norm, conv_w_in=m_conv_w_in, conv_w=m_conv_w,
                conv_w_out=m_conv_w_out, ffn_norm=m_ffn_norm, ffn_w_gate=m_ffn_w_gate, ffn_w_up=m_ffn_w_up,
                ffn_w_down=m_ffn_w_down, final_norm=m_final_norm)
    v_in = dict(mla_norm=v_mla_norm, mla_w_in=v_mla_w_in, mla_g_cq=v_mla_g_cq, mla_g_ckv=v_mla_g_ckv, mla_w_uq=v_mla_w_uq,
                mla_w_ukv=v_mla_w_ukv, mla_w_o=v_mla_w_o, conv_norm=v_conv_norm, conv_w_in=v_conv_w_in, conv_w=v_conv_w,
                conv_w_out=v_conv_w_out, ffn_norm=v_ffn_norm, ffn_w_gate=v_ffn_w_gate, ffn_w_up=v_ffn_w_up,
                ffn_w_down=v_ffn_w_down, final_norm=v_final_norm)
    big = ["mla_w_in", "mla_w_uq", "mla_w_ukv", "mla_w_o", "conv_w_in", "conv_w_out", "ffn_w_gate", "ffn_w_up", "ffn_w_down"]
    order = list(weights)

    t, d = x.shape[1], x.shape[2]
    h0 = x.reshape(t, d)
    target = loss_target.reshape(t, d)
    cos, sin = _rope_tables(positions.reshape(t))

    def rows2d(a):
        return a.reshape(-1, a.shape[-1])

    first, later = big[:4], big[4:]
    shards = {n: rows2d(weights[n]).astype(BF16) for n in big}
    gathered = dict(zip(first, gather_weight_shards([shards[n] for n in first])))
    w_in = gathered["mla_w_in"].reshape(-1, gathered["mla_w_in"].shape[-1])
    w_uq = _unstack_cols(gathered["mla_w_uq"])
    w_ukv = _unstack_cols(gathered["mla_w_ukv"])
    w_o = gathered["mla_w_o"].reshape(-1, d)

    chip = 2 * lax.axis_index("x") + lax.axis_index("y")
    core = lax.axis_index("c")
    d4 = d // N_CHIPS
    first_core = (core == 0).astype(F32)

    def place_shard(shard):
        full = jnp.zeros((shard.shape[0], d), F32)
        return lax.dynamic_update_slice(full, shard * first_core, (0, chip * d4))

    def pack_rows(rows):
        idx = lax.broadcasted_iota(jnp.int32, (SMALL_ROWS, d), 0)
        out = jnp.zeros((SMALL_ROWS, d), F32)
        for r, row in enumerate(rows):
            out = out + jnp.where(idx == r, row, 0.0)
        return out

    cw = place_shard(conv_w.reshape(3, d4))
    pre = all_reduce_small("all_gather_conv_small", pack_rows([place_shard(conv_norm.reshape(1, d4)), cw[0:1], cw[1:2], cw[2:3]]))
    conv_norm_full = pre[0:1]
    conv_w_full = pre[1:4]

    a0 = rms_fwd("mla_norm_fwd", h0, mla_norm)
    proj, cq, ckv, kr = mla_in_proj("mla_in_proj", a0, w_in, mla_g_cq, mla_g_ckv, cos, sin)
    q = linear("mla_q_up", cq, w_uq, F32)
    kv = linear("mla_kv_up", ckv, w_ukv, BF16)
    qh, kh, vh, conv_arriving = qkv_heads("qkv_heads", q, kv, kr, cos, sin, [shards[n] for n in later[:2]])
    attn, lse, ffn_arriving = attention_fwd("attention_fwd", qh, kh, vh, [shards[n] for n in later[2:]])
    h1, a1, handed = attention_out_proj("mla_out_proj", attn, w_o, h0, ffn_norm[0:1], conv_arriving + ffn_arriving)
    gathered.update(zip(later, _fill_own_slot(handed, [shards[n][None] for n in later])))
    cw_in = _unstack_cols(gathered["conv_w_in"])
    cw_out = gathered["conv_w_out"].reshape(-1, d)
    wg_all, wu_all, wd_all = gathered["ffn_w_gate"], gathered["ffn_w_up"], gathered["ffn_w_down"]

    def ffn_forward(tag, h, a, layer, next_gain):
        g, u, z = ffn_up(f"ffn{tag}_up", a, wg_all, wu_all, layer)
        return g, u, z, ffn_down(f"ffn{tag}_down", z, wd_all, layer, h, next_gain)

    g0, u0, z0, (h2, a2) = ffn_forward(0, h1, a1, 0, conv_norm_full)
    bcx, yc = conv_in_proj("conv_in_proj", a2, cw_in, conv_w_full)
    h3, a3 = linear("conv_out_proj", yc, cw_out, F32, resid=h2, next_gain=ffn_norm[1:2])
    g1, u1, z1 = ffn_up("ffn1_up", a3, wg_all, wu_all, 1)
    dh4, d_final_norm, loss_local = ffn_down_loss("ffn1_down_loss", z1, wd_all, 1, h3, final_norm.reshape(1, d), target)

    def ffn_backward(tag, dh, h, layer, a, g, u, z, swap=()):
        dg, du, swapped = ffn_bwd_hidden(f"ffn{tag}_bwd_hidden", dh, wd_all, layer, g, u, swap)
        d_wd = ffn_wgrad_down(f"ffn{tag}_wgrad_down", z, dh)
        dh_prev, d_norm = ffn_bwd_input(f"ffn{tag}_bwd_input", dg, du, wg_all, wu_all, layer, h, ffn_norm[layer:layer + 1], dh)
        d_wg = ffn_wgrad_up(f"ffn{tag}_wgrad_gate", a, dg)
        d_wu = ffn_wgrad_up(f"ffn{tag}_wgrad_up", a, du)
        return dh_prev, d_norm, [d_wg, d_wu, d_wd], swapped

    def pair_sums(tag, local, from_sibling):
        return [add_halves(f"pair_sum_{tag}{i}", g, r) for i, (g, r) in enumerate(zip(local, from_sibling))]

    def sum_from_chips(tag, pairs, arrived):
        return [sum_chips(f"chip_sum_{tag}{i}", a, p) for i, (a, p) in enumerate(zip(arrived, pairs))]

    def shard_shape(n):
        return rows2d(weights[n]).shape

    dh3, d_ffn_norm1, ffn1_grads, _ = ffn_backward(1, dh4, h3, 1, a3, g1, u1, z1)

    dyc = linear_nt("conv_out_bwd_input", dh3, cw_out, F32)
    d_cw_out = wgrad("conv_out_wgrad", yc, dh3)
    dbcx, d_conv_w = conv_bwd("conv_bwd", bcx, conv_w_full, dyc)
    dh2, d_conv_norm = conv_in_bwd_input("conv_in_bwd_input", dbcx, cw_in, h2, conv_norm_full, dh3)
    d_cw_in = conv_in_wgrad("conv_in_wgrad", a2, dbcx)

    second = [d_cw_in, d_cw_out.reshape(N_CHIPS, -1, d)] + ffn1_grads
    dh1, d_ffn_norm0, ffn0_grads, second_swapped = ffn_backward(0, dh2, h1, 0, a1, g0, u0, z0, second)
    d_w_o = wgrad("mla_out_wgrad", attn, dh1)
    first_part = ffn0_grads + [d_w_o.reshape(N_CHIPS, -1, d)]
    d_attn, delta, first_swapped = attention_out_bwd("mla_out_bwd_input", dh1, w_o, attn, first_part)
    rest_pairs = pair_sums("rest", second + first_part, second_swapped + first_swapped)
    dqh, dkh, dvh, rest_arrived = attention_bwd("attention_bwd", qh, kh, vh, d_attn, lse, delta, rest_pairs)
    rd, rf = ffn0_grads[0].shape[1], ffn0_grads[2].shape[1]
    rest_where = [(0, 0), (1, 0), (2, rd), (3, rd), (4, rf), (2, 0), (3, 0), (4, 0), (5, 0)]
    rest_names = later + ["mla_w_o"]
    dq, dkv, dkr, rest_grads = qkv_heads_bwd("qkv_heads_bwd", dqh, dkh, dvh, cos, sin, sum_from_chips("rest", rest_pairs, rest_arrived),
                                              [shard_shape(n) for n in rest_names], rest_where)
    grads = dict(zip(rest_names, rest_grads))
    dcq = linear_nt("mla_q_up_bwd_input", dq, w_uq, F32)
    d_w_uq = wgrad("mla_q_up_wgrad", cq, dq)
    dckv = linear_nt("mla_kv_up_bwd_input", dkv, w_ukv, F32)
    d_w_ukv = wgrad("mla_kv_up_wgrad", ckv, dkv)
    dproj, d_g_cq, d_g_ckv = mla_mid_bwd("mla_mid_bwd", proj, mla_g_cq, mla_g_ckv, dcq, dckv, dkr, cos, sin)
    d_w_in = wgrad("mla_in_wgrad", a0, dproj)
    mla_local = [d_w_in.reshape(N_CHIPS, -1, d_w_in.shape[-1]), _stack_cols(d_w_uq), _stack_cols(d_w_ukv)]
    mla_pairs = pair_sums("mla", mla_local, sibling_swap_halves("sibling_swap_mla", mla_local))
    grad_x, d_mla_norm, mla_arrived = linear_nt_norm_bwd("mla_in_bwd_input", dproj, w_in, h0, mla_norm, dh1, mla_pairs)

    grads.update(zip(first[:3], sibling_join_halves("sibling_join_mla", sum_from_chips("mla", mla_pairs, mla_arrived),
                                                    [shard_shape(n) for n in first[:3]], [(i, 0) for i in range(3)])))

    def pad_row(v):
        return jnp.pad(v, ((0, 0), (0, d - v.shape[1])))

    small = all_reduce_small("all_reduce_small_grads", pack_rows([
        d_mla_norm, pad_row(d_g_cq), pad_row(d_g_ckv), d_ffn_norm0, d_ffn_norm1, d_final_norm, d_conv_norm,
        d_conv_w[0:1], d_conv_w[1:2], d_conv_w[2:3], jnp.broadcast_to(loss_local, (1, d))]))
    loss = small[10, 0]
    grads["mla_norm"] = small[0:1]
    grads["mla_g_cq"] = small[1:2, :mla_g_cq.shape[1]]
    grads["mla_g_ckv"] = small[2:3, :mla_g_ckv.shape[1]]
    grads["ffn_norm"] = small[3:5]
    grads["final_norm"] = small[5:6]
    grads["conv_norm"] = lax.dynamic_slice(small[6:7], (0, chip * d4), (1, d4))
    grads["conv_w"] = lax.dynamic_slice(small[7:10], (0, chip * d4), (3, d4))

    outs_g, outs_d, outs_m, outs_v = [], [], [], []
    for n in order:
        w = weights[n]
        if w.ndim == 3 and w.shape[2] % 128 and w.shape[1] % 128 == 0:
            results = adamw_swapped(f"adamw_{n}", jnp.swapaxes(w, 1, 2), grads[n].reshape(-1, w.shape[2]),
                                    jnp.swapaxes(m_in[n], 1, 2), jnp.swapaxes(v_in[n], 1, 2))
            grad_w, delta_w, new_m, new_v = [jnp.swapaxes(o, 1, 2) for o in results]
        else:
            delta_w, new_m, new_v = adamw(f"adamw_{n}", rows2d(w), grads[n].reshape(rows2d(w).shape), rows2d(m_in[n]), rows2d(v_in[n]))
            grad_w = grads[n]
        outs_g.append(grad_w.reshape(w.shape))
        outs_d.append(delta_w.reshape(w.shape))
        outs_m.append(new_m.reshape(w.shape))
        outs_v.append(new_v.reshape(w.shape))
    return (loss, grad_x.reshape(x.shape), *outs_g, *outs_d, *outs_m, *outs_v)
```

```python
import math

import jax
import jax.numpy as jnp
from jax import lax
from jax.experimental import pallas as pl
from jax.experimental.pallas import tpu as pltpu

F32 = jnp.float32
BF16 = jnp.bfloat16
S = jax.ShapeDtypeStruct

N_HEADS = 8
NOPE = 128
ROPE = 64
HALF = ROPE // 2
VDIM = 128
QK = NOPE + ROPE
CHUNK = 64
ROPE_THETA = 10000.0
RMS_EPS = 1e-6
ADAM_LR = 0.001
ADAM_B1 = 0.9
ADAM_B2 = 0.999
ADAM_EPS = 1e-08
ADAM_WD = 0.01
ADAM_STEP = 10

N_CHIPS = 4
N_DEV = 8
MASK_VALUE = -1e30
SCORE_SCALE = 1.0 / math.sqrt(QK)
LOG2_E = math.log2(math.e)
SCORE_SCALE_LOG2 = SCORE_SCALE * LOG2_E
VMEM_LIMIT = 48 * 1024 * 1024
VMEM_LIMIT_WHOLE_HEAD = 58 * 1024 * 1024
ATT_BLOCK = 512
CONV_SAVED_DTYPE = jnp.bfloat16
SMALL_ROWS = 16

_NN = (((1,), (0,)), ((), ()))
_NT = (((1,), (1,)), ((), ()))
_TN = (((0,), (0,)), ((), ()))
MESH = pl.DeviceIdType.MESH
ANY = pl.BlockSpec(memory_space=pl.ANY)


def _params(n_axes, vmem_limit=VMEM_LIMIT):
    return pltpu.CompilerParams(dimension_semantics=("arbitrary",) * n_axes, vmem_limit_bytes=vmem_limit)


def _tile(n, cap, mult=8):
    for t in range(min(cap, n), 0, -1):
        if n % t == 0 and t % mult == 0:
            return t
    return n


def _sigmoid(x):
    return 0.5 * jnp.tanh(0.5 * x) + 0.5


def _mm(name, a_ops, b_ops, products, dims, grid, k_axis, outs, acc_shape, epilogue, extra_ops=()):
    na, nb, ne, no = len(a_ops), len(b_ops), len(extra_ops), len(outs)
    n_acc = 1 + max(c for _, _, c in products)
    nk = 1 if k_axis is None else grid[k_axis]

    def body(*refs):
        a_refs = refs[:na]
        b_refs = refs[na:na + nb]
        e_refs = refs[na + nb:na + nb + ne]
        o_refs = refs[na + nb + ne:na + nb + ne + no]
        acc_refs = refs[na + nb + ne + no:]

        def partial_sums():
            vals = [None] * n_acc
            for ai, bi, ci in products:
                d = lax.dot_general(a_refs[ai][...].astype(BF16), b_refs[bi][...].astype(BF16), dims,
                                    preferred_element_type=F32)
                vals[ci] = d if vals[ci] is None else vals[ci] + d
            return vals

        if nk == 1:
            epilogue(partial_sums(), e_refs, o_refs)
        else:
            k = pl.program_id(k_axis)

            @pl.when(k == 0)
            def _():
                for acc in acc_refs:
                    acc[...] = jnp.zeros_like(acc)

            for acc, v in zip(acc_refs, partial_sums()):
                acc[...] += v

            @pl.when(k == nk - 1)
            def _():
                epilogue([acc[...] for acc in acc_refs], e_refs, o_refs)

    ops = list(a_ops) + list(b_ops) + list(extra_ops)
    return pl.pallas_call(
        body, name=name, grid=grid,
        in_specs=[s for _, s in ops], out_specs=[s for _, s in outs], out_shape=[o for o, _ in outs],
        scratch_shapes=[pltpu.VMEM(acc_shape, F32) for _ in range(n_acc if nk > 1 else 0)],
        compiler_params=_params(len(grid)),
    )(*[a for a, _ in ops])


def _store(accs, e_refs, o_refs):
    o_refs[0][...] = accs[0].astype(o_refs[0].dtype)


def linear(name, x, w, out_dtype, resid=None, next_gain=None):
    t, k = x.shape
    n = w.shape[1]
    tm = _tile(t, 512)
    tn = n if n <= 2048 else _tile(n, 1024, 128)
    tile = pl.BlockSpec((tm, tn), lambda j, i: (i, j))
    extra = [] if resid is None else [(resid, tile)]
    outs = [(S((t, n), out_dtype), tile)]
    if next_gain is not None:
        assert tn == n
        extra.append((next_gain, pl.BlockSpec((1, n), lambda j, i: (0, 0))))
        outs.append((S((t, n), BF16), tile))

    def epilogue(accs, e_refs, o_refs):
        y = accs[0] if resid is None else e_refs[0][...] + accs[0]
        o_refs[0][...] = y.astype(out_dtype)
        if next_gain is not None:
            o_refs[1][...] = (y * _rstd(y) * e_refs[-1][...]).astype(BF16)

    res = _mm(name, [(x, pl.BlockSpec((tm, k), lambda j, i: (i, 0)))], [(w, pl.BlockSpec((k, tn), lambda j, i: (0, j)))],
              [(0, 0, 0)], _NN, (n // tn, t // tm), None, outs, None, epilogue, extra)
    return res[0] if next_gain is None else res


def linear_nt(name, dy, w, out_dtype):
    t, n = dy.shape
    k = w.shape[0]
    tm = _tile(t, 512)
    tc = n if n <= 2048 else _tile(n, 1024, 128)
    return _mm(name, [(dy, pl.BlockSpec((tm, tc), lambda i, c: (i, c)))], [(w, pl.BlockSpec((k, tc), lambda i, c: (0, c)))],
               [(0, 0, 0)], _NT, (t // tm, n // tc), 1,
               [(S((t, k), out_dtype), pl.BlockSpec((tm, k), lambda i, c: (i, 0)))], (tm, k), _store)[0]


def wgrad(name, x, dy):
    t, k = x.shape
    n = dy.shape[1]
    tk = _tile(t, 512)
    tn = n if n <= 1024 else _tile(n, 1024, 128)
    return _mm(name, [(x, pl.BlockSpec((tk, k), lambda j, s: (s, 0)))], [(dy, pl.BlockSpec((tk, tn), lambda j, s: (s, j)))],
               [(0, 0, 0)], _TN, (n // tn, t // tk), 1,
               [(S((k, n), BF16), pl.BlockSpec((k, tn), lambda j, s: (0, j)))], (k, tn), _store)[0]


def _resident(shape, index_map):
    return pl.BlockSpec(shape, index_map, pipeline_mode=pl.Buffered(1))


def ffn_up(name, a, wg_all, wu_all, layer):
    t, d = a.shape
    f4 = wg_all.shape[2]
    tm = _tile(t, 512)
    w_spec = _resident((N_CHIPS, d, f4), lambda i: (0, layer, 0))
    h_spec = pl.BlockSpec((N_CHIPS, tm, f4), lambda i: (0, i, 0))

    def body(a_ref, wg_ref, wu_ref, zg_ref, zu_ref, z_ref):
        av = a_ref[...]
        for k in range(N_CHIPS):
            g = jnp.dot(av, wg_ref[k], preferred_element_type=F32)
            u = jnp.dot(av, wu_ref[k], preferred_element_type=F32)
            sg = _sigmoid(g)
            silu = g * sg
            zg_ref[k] = (u * (sg * (1.0 + g * (1.0 - sg)))).astype(BF16)
            zu_ref[k] = silu.astype(BF16)
            z_ref[k] = (silu * u).astype(BF16)

    return pl.pallas_call(
        body, name=name, grid=(t // tm,), in_specs=[pl.BlockSpec((tm, d), lambda i: (i, 0)), w_spec, w_spec],
        out_specs=[h_spec] * 3, out_shape=[S((N_CHIPS, t, f4), BF16)] * 3, compiler_params=_params(1))(a, wg_all, wu_all)


def ffn_down(name, z, wd_all, layer, resid, next_gain=None):
    _, t, f4 = z.shape
    d = wd_all.shape[2]
    tm = _tile(t, 512)
    row = pl.BlockSpec((tm, d), lambda i: (i, 0))
    normed = next_gain is not None

    def body(z_ref, wd_ref, r_ref, *refs):
        acc = r_ref[...]
        for k in range(N_CHIPS):
            acc = acc + jnp.dot(z_ref[k], wd_ref[k], preferred_element_type=F32)
        refs[-2 if normed else -1][...] = acc
        if normed:
            refs[-1][...] = (acc * _rstd(acc) * refs[0][...]).astype(BF16)

    res = pl.pallas_call(
        body, name=name, grid=(t // tm,),
        in_specs=[pl.BlockSpec((N_CHIPS, tm, f4), lambda i: (0, i, 0)), _resident((N_CHIPS, f4, d), lambda i: (0, layer, 0)), row]
        + ([pl.BlockSpec((1, d), lambda i: (0, 0))] if normed else []),
        out_specs=[row] * (2 if normed else 1), out_shape=[S((t, d), F32)] + ([S((t, d), BF16)] if normed else []),
        compiler_params=_params(1))(z, wd_all, resid, *([next_gain] if normed else []))
    return res if normed else res[0]


def ffn_bwd_hidden(name, dh, wd_all, layer, zg, zu, swap=()):
    t, d = dh.shape
    f4 = zg.shape[2]
    tm = _tile(t, 512)
    h_spec = pl.BlockSpec((N_CHIPS, tm, f4), lambda i: (0, i, 0))
    n = len(swap)

    def body(dh_ref, wd_ref, zg_ref, zu_ref, *refs):
        dg_ref, du_ref = refs[n:n + 2]
        if n:
            _ride_along(swap_copies(refs[:n], refs[n + 2:2 * n + 2], *refs[2 * n + 2:]), (pl.program_id(0),), (t // tm,))
        dhb = dh_ref[...].astype(BF16)
        for k in range(N_CHIPS):
            dz = lax.dot_general(dhb, wd_ref[k], _NT, preferred_element_type=F32)
            dg_ref[k] = (dz * zg_ref[k].astype(F32)).astype(BF16)
            du_ref[k] = (dz * zu_ref[k].astype(F32)).astype(BF16)

    outs = pl.pallas_call(
        body, name=name, grid=(t // tm,),
        in_specs=[pl.BlockSpec((tm, d), lambda i: (i, 0)), _resident((N_CHIPS, f4, d), lambda i: (0, layer, 0)), h_spec, h_spec]
        + [ANY] * n,
        out_specs=[h_spec] * 2 + [ANY] * n, out_shape=[S((N_CHIPS, t, f4), BF16)] * 2 + _swap_shapes(swap),
        scratch_shapes=[pltpu.SemaphoreType.DMA((n,)), pltpu.SemaphoreType.DMA((n,))] if n else [],
        compiler_params=_params(1))(dh, wd_all, zg, zu, *swap)
    return outs[0], outs[1], list(outs[2:])


def _norm_bwd_specs(tm, d):
    row = pl.BlockSpec((tm, d), lambda i: (i, 0))
    vec = pl.BlockSpec((1, d), lambda i: (0, 0))
    return [row, vec, row], [row, vec]


def _norm_bwd_tail(da, h_ref, g_ref, dhi_ref, dho_ref, dgain_ref):
    dx, dgain = _rms_bwd(h_ref[...], g_ref[...], da)
    dho_ref[...] = dhi_ref[...] + dx

    @pl.when(pl.program_id(0) == 0)
    def _():
        dgain_ref[...] = jnp.zeros_like(dgain_ref)

    dgain_ref[...] += dgain


def ffn_bwd_input(name, dg, du, wg_all, wu_all, layer, h, gain, dh_in):
    _, t, f4 = dg.shape
    d = h.shape[1]
    tm = _tile(t, 512)
    h_spec = pl.BlockSpec((N_CHIPS, tm, f4), lambda i: (0, i, 0))
    w_spec = _resident((N_CHIPS, d, f4), lambda i: (0, layer, 0))
    tail_in, tail_out = _norm_bwd_specs(tm, d)

    def body(dg_ref, du_ref, wg_ref, wu_ref, *tail):
        acc = jnp.zeros((tm, d), F32)
        for k in range(N_CHIPS):
            acc = acc + lax.dot_general(dg_ref[k], wg_ref[k], _NT, preferred_element_type=F32)
            acc = acc + lax.dot_general(du_ref[k], wu_ref[k], _NT, preferred_element_type=F32)
        _norm_bwd_tail(acc, *tail)

    return pl.pallas_call(
        body, name=name, grid=(t // tm,), in_specs=[h_spec, h_spec, w_spec, w_spec] + tail_in, out_specs=tail_out,
        out_shape=[S((t, d), F32), S((1, d), F32)], compiler_params=_params(1))(dg, du, wg_all, wu_all, h, gain, dh_in)


def ffn_wgrad_up(name, a, dy):
    t, d = a.shape
    f4 = dy.shape[2]
    tk = _tile(t, 512)
    nt = t // tk

    def body(a_ref, dy_ref, o_ref, acc):
        s = pl.program_id(0)

        @pl.when(s == 0)
        def _():
            acc[...] = jnp.zeros_like(acc)

        at = a_ref[...].T
        for k in range(N_CHIPS):
            acc[k] += jnp.dot(at, dy_ref[k], preferred_element_type=F32)

        @pl.when(s == nt - 1)
        def _():
            o_ref[...] = acc[...].astype(BF16)

    return pl.pallas_call(
        body, name=name, grid=(nt,),
        in_specs=[pl.BlockSpec((tk, d), lambda s: (s, 0)), pl.BlockSpec((N_CHIPS, tk, f4), lambda s: (0, s, 0))],
        out_specs=pl.BlockSpec((N_CHIPS, d, f4), lambda s: (0, 0, 0)), out_shape=S((N_CHIPS, d, f4), BF16),
        scratch_shapes=[pltpu.VMEM((N_CHIPS, d, f4), F32)], compiler_params=_params(1))(a, dy)


def ffn_wgrad_down(name, z, dh):
    _, t, f4 = z.shape
    d = dh.shape[1]
    tk = _tile(t, 512)
    nt = t // tk

    def body(z_ref, dh_ref, o_ref, acc):
        s = pl.program_id(0)

        @pl.when(s == 0)
        def _():
            acc[...] = jnp.zeros_like(acc)

        dhb = dh_ref[...].astype(BF16)
        for k in range(N_CHIPS):
            acc[k] += lax.dot_general(z_ref[k], dhb, _TN, preferred_element_type=F32)

        @pl.when(s == nt - 1)
        def _():
            o_ref[...] = acc[...].astype(BF16)

    return pl.pallas_call(
        body, name=name, grid=(nt,),
        in_specs=[pl.BlockSpec((N_CHIPS, tk, f4), lambda s: (0, s, 0)), pl.BlockSpec((tk, d), lambda s: (s, 0))],
        out_specs=pl.BlockSpec((N_CHIPS, f4, d), lambda s: (0, 0, 0)), out_shape=S((N_CHIPS, f4, d), BF16),
        scratch_shapes=[pltpu.VMEM((N_CHIPS, f4, d), F32)], compiler_params=_params(1))(z, dh)


def conv_in_proj(name, a, w, conv_w):
    t, d = a.shape
    tm = _tile(t, 256)
    keep = 8

    def body(a_ref, w_ref, cw_ref, bcx_ref, y_ref, u_ref):
        @pl.when(pl.program_id(0) == 0)
        def _():
            u_ref[0:keep, :] = jnp.zeros((keep, d), F32)

        av = a_ref[...]
        b, c, x = [jnp.dot(av, w_ref[:, j * d:(j + 1) * d], preferred_element_type=F32) for j in range(3)]
        for j, part in enumerate((b, c, x)):
            bcx_ref[j] = part.astype(bcx_ref.dtype)
        u_ref[keep:keep + tm, :] = c * x
        uc = (cw_ref[0:1, :] * u_ref[keep - 2:keep - 2 + tm, :] + cw_ref[1:2, :] * u_ref[keep - 1:keep - 1 + tm, :]
              + cw_ref[2:3, :] * u_ref[keep:keep + tm, :])
        y_ref[...] = (b * uc).astype(BF16)
        u_ref[0:keep, :] = u_ref[tm:tm + keep, :]

    return pl.pallas_call(
        body, name=name, grid=(t // tm,),
        in_specs=[pl.BlockSpec((tm, d), lambda i: (i, 0)), _resident((d, 3 * d), lambda i: (0, 0)), pl.BlockSpec((3, d), lambda i: (0, 0))],
        out_specs=[pl.BlockSpec((3, tm, d), lambda i: (0, i, 0)), pl.BlockSpec((tm, d), lambda i: (i, 0))],
        out_shape=[S((3, t, d), CONV_SAVED_DTYPE), S((t, d), BF16)], scratch_shapes=[pltpu.VMEM((tm + keep, d), F32)],
        compiler_params=_params(1))(a, w, conv_w)


def conv_in_bwd_input(name, dbcx, w, h, gain, dh_in):
    _, t, d = dbcx.shape
    tm = _tile(t, 512)
    tail_in, tail_out = _norm_bwd_specs(tm, d)

    def body(g_ref, w_ref, *tail):
        acc = jnp.zeros((tm, d), F32)
        for j in range(3):
            acc = acc + lax.dot_general(g_ref[j], w_ref[:, j * d:(j + 1) * d], _NT, preferred_element_type=F32)
        _norm_bwd_tail(acc, *tail)

    return pl.pallas_call(
        body, name=name, grid=(t // tm,),
        in_specs=[pl.BlockSpec((3, tm, d), lambda i: (0, i, 0)), _resident((d, 3 * d), lambda i: (0, 0))] + tail_in,
        out_specs=tail_out, out_shape=[S((t, d), F32), S((1, d), F32)], compiler_params=_params(1))(dbcx, w, h, gain, dh_in)


def linear_nt_norm_bwd(name, dy, w, h, gain, dh_in, parts=()):
    t, n = dy.shape
    k = w.shape[0]
    tm = _tile(t, 512)
    tail_in, tail_out = _norm_bwd_specs(tm, k)
    m = len(parts)

    def body(dy_ref, w_ref, h_ref, g_ref, dhi_ref, *refs):
        if m:
            _ride_along(scatter_ici_copies(refs[:m], refs[m + 2:2 * m + 2], *refs[2 * m + 2:]), (pl.program_id(0),), (t // tm,))
        da = lax.dot_general(dy_ref[...].astype(BF16), w_ref[...], _NT, preferred_element_type=F32)
        _norm_bwd_tail(da, h_ref, g_ref, dhi_ref, *refs[m:m + 2])

    outs = pl.pallas_call(
        body, name=name, grid=(t // tm,),
        in_specs=[pl.BlockSpec((tm, n), lambda i: (i, 0)), _resident((k, n), lambda i: (0, 0))] + tail_in + [ANY] * m,
        out_specs=tail_out + [ANY] * m, out_shape=[S((t, k), F32), S((1, k), F32)] + [S(p.shape, p.dtype) for p in parts],
        scratch_shapes=[pltpu.SemaphoreType.DMA((m, 3)), pltpu.SemaphoreType.DMA((m, 3))] if m else [],
        compiler_params=_params(1))(dy, w, h, gain, dh_in, *parts)
    return outs[0], outs[1], list(outs[2:])


def conv_in_wgrad(name, a, dbcx):
    t, d = a.shape
    tk = _tile(t, 512)
    nt = t // tk
    n4 = 3 * d // N_CHIPS

    def body(a_ref, g_ref, o_ref, acc):
        s = pl.program_id(0)

        @pl.when(s == 0)
        def _():
            acc[...] = jnp.zeros_like(acc)

        at = a_ref[...].T
        for j in range(3):
            acc[:, j * d:(j + 1) * d] += jnp.dot(at, g_ref[j], preferred_element_type=F32)

        @pl.when(s == nt - 1)
        def _():
            for k in range(N_CHIPS):
                o_ref[k] = acc[:, k * n4:(k + 1) * n4].astype(BF16)

    return pl.pallas_call(
        body, name=name, grid=(nt,),
        in_specs=[pl.BlockSpec((tk, d), lambda s: (s, 0)), pl.BlockSpec((3, tk, d), lambda s: (0, s, 0))],
        out_specs=pl.BlockSpec((N_CHIPS, d, n4), lambda s: (0, 0, 0)), out_shape=S((N_CHIPS, d, n4), BF16),
        scratch_shapes=[pltpu.VMEM((d, 3 * d), F32)], compiler_params=_params(1))(a, dbcx)


def _rstd(x):
    return lax.rsqrt(jnp.mean(x * x, axis=-1, keepdims=True) + RMS_EPS)


def _rms_bwd(x, g, dy):
    r = _rstd(x)
    xhat = x * r
    dgain = jnp.sum(dy * xhat, axis=0, keepdims=True)
    dxh = dy * g
    dx = r * (dxh - xhat * jnp.mean(dxh * xhat, axis=-1, keepdims=True))
    return dx, dgain


def rms_fwd(name, h, g):
    t, d = h.shape
    tr = _tile(t, 512)

    def body(h_ref, g_ref, a_ref):
        x = h_ref[...]
        a_ref[...] = (x * _rstd(x) * g_ref[...]).astype(BF16)

    return pl.pallas_call(
        body, name=name, grid=(t // tr,),
        in_specs=[pl.BlockSpec((tr, d), lambda i: (i, 0)), pl.BlockSpec((1, d), lambda i: (0, 0))],
        out_specs=pl.BlockSpec((tr, d), lambda i: (i, 0)), out_shape=S((t, d), BF16), compiler_params=_params(1))(h, g)


def ffn_down_loss(name, z, wd_all, layer, resid, gain, target):
    _, t, f4 = z.shape
    d = wd_all.shape[2]
    tm = _tile(t, 512)

    def body(z_ref, wd_ref, r_ref, g_ref, t_ref, dh_ref, dg_ref, loss_ref):
        x = r_ref[...]
        for k in range(N_CHIPS):
            x = x + jnp.dot(z_ref[k], wd_ref[k], preferred_element_type=F32)
        g = g_ref[...]
        r = _rstd(x)
        xhat = x * r
        err = xhat * g - t_ref[...]
        dy = err * (1.0 / d)
        dxh = dy * g
        dh_ref[...] = r * (dxh - xhat * jnp.mean(dxh * xhat, axis=-1, keepdims=True))

        @pl.when(pl.program_id(0) == 0)
        def _():
            dg_ref[...] = jnp.zeros_like(dg_ref)
            loss_ref[...] = jnp.zeros_like(loss_ref)

        dg_ref[...] += jnp.sum(dy * xhat, axis=0, keepdims=True)
        per_token = jnp.mean(err * err, axis=-1, keepdims=True)
        loss_ref[...] += 0.5 * jnp.sum(per_token, axis=0, keepdims=True)

    row = pl.BlockSpec((tm, d), lambda i: (i, 0))
    vec = pl.BlockSpec((1, d), lambda i: (0, 0))
    one = pl.BlockSpec((1, 1), lambda i: (0, 0))
    return pl.pallas_call(
        body, name=name, grid=(t // tm,),
        in_specs=[pl.BlockSpec((N_CHIPS, tm, f4), lambda i: (0, i, 0)), _resident((N_CHIPS, f4, d), lambda i: (0, layer, 0)), row, vec, row],
        out_specs=[row, vec, one], out_shape=[S((t, d), F32), S((1, d), F32), S((1, 1), F32)],
        compiler_params=_params(1))(z, wd_all, resid, gain, target)


def mla_in_proj(name, a, w, g_cq, g_ckv, cos, sin):
    t, d = a.shape
    n = w.shape[1]
    ql, kl = g_cq.shape[1], g_ckv.shape[1]
    tr = _tile(t, 512)

    def body(a_ref, w_ref, gq_ref, gk_ref, c_ref, s_ref, p_ref, cq_ref, ckv_ref, kr_ref):
        p_ref[...] = jnp.dot(a_ref[...], w_ref[...], preferred_element_type=F32)
        xq = p_ref[:, 0:ql]
        cq_ref[...] = (xq * _rstd(xq) * gq_ref[...]).astype(BF16)
        xk = p_ref[:, ql:ql + kl]
        ckv_ref[...] = (xk * _rstd(xk) * gk_ref[...]).astype(BF16)
        k1 = p_ref[:, ql + kl:ql + kl + HALF]
        k2 = p_ref[:, ql + kl + HALF:ql + kl + ROPE]
        c = c_ref[...]
        s = s_ref[...]
        kr_ref[:, 0:HALF] = k1 * c - k2 * s
        kr_ref[:, HALF:ROPE] = k1 * s + k2 * c

    def row(w):
        return pl.BlockSpec((tr, w), lambda i: (i, 0))

    def vec(w):
        return pl.BlockSpec((1, w), lambda i: (0, 0))

    return pl.pallas_call(
        body, name=name, grid=(t // tr,),
        in_specs=[row(d), _resident((d, n), lambda i: (0, 0)), vec(ql), vec(kl), row(HALF), row(HALF)],
        out_specs=[row(n), row(ql), row(kl), row(ROPE)],
        out_shape=[S((t, n), F32), S((t, ql), BF16), S((t, kl), BF16), S((t, ROPE), F32)],
        compiler_params=_params(1))(a, w, g_cq, g_ckv, cos, sin)


def mla_mid_bwd(name, proj, g_cq, g_ckv, dcq, dckv, dkr, cos, sin):
    t, n = proj.shape
    ql, kl = g_cq.shape[1], g_ckv.shape[1]
    tr = _tile(t, 512)

    def body(p_ref, gq_ref, gk_ref, dcq_ref, dckv_ref, dkr_ref, c_ref, s_ref, dp_ref, dgq_ref, dgk_ref):
        dxq, dgq = _rms_bwd(p_ref[:, 0:ql], gq_ref[...], dcq_ref[...])
        dp_ref[:, 0:ql] = dxq.astype(BF16)
        dxk, dgk = _rms_bwd(p_ref[:, ql:ql + kl], gk_ref[...], dckv_ref[...])
        dp_ref[:, ql:ql + kl] = dxk.astype(BF16)
        d1 = dkr_ref[:, 0:HALF]
        d2 = dkr_ref[:, HALF:ROPE]
        c = c_ref[...]
        s = s_ref[...]
        dp_ref[:, ql + kl:ql + kl + HALF] = (d1 * c + d2 * s).astype(BF16)
        dp_ref[:, ql + kl + HALF:ql + kl + ROPE] = (d2 * c - d1 * s).astype(BF16)

        @pl.when(pl.program_id(0) == 0)
        def _():
            dgq_ref[...] = jnp.zeros_like(dgq_ref)
            dgk_ref[...] = jnp.zeros_like(dgk_ref)

        dgq_ref[...] += dgq
        dgk_ref[...] += dgk

    def row(w):
        return pl.BlockSpec((tr, w), lambda i: (i, 0))

    def vec(w):
        return pl.BlockSpec((1, w), lambda i: (0, 0))

    return pl.pallas_call(
        body, name=name, grid=(t // tr,),
        in_specs=[row(n), vec(ql), vec(kl), row(ql), row(kl), row(ROPE), row(HALF), row(HALF)],
        out_specs=[row(n), vec(ql), vec(kl)], out_shape=[S((t, n), BF16), S((1, ql), F32), S((1, kl), F32)],
        compiler_params=_params(1))(proj, g_cq, g_ckv, dcq, dckv, dkr, cos, sin)


def qkv_heads(name, q, kv, kr, cos, sin, shards=()):
    t = q.shape[0]
    tr = _tile(t, 256)
    n = len(shards)

    def body(q_ref, kv_ref, kr_ref, c_ref, s_ref, *refs):
        src = refs[:n]
        qo_ref, ko_ref, vo_ref = refs[n:n + 3]
        if n:
            _ride_along(gather_ici_copies(src, refs[n + 3:2 * n + 3], *refs[2 * n + 3:]), (pl.program_id(0),), (t // tr,))
        c = c_ref[...]
        s = s_ref[...]
        krb = kr_ref[...].astype(BF16)
        for h in range(N_HEADS):
            q0 = h * QK
            qo_ref[h, :, 0:NOPE] = q_ref[:, q0:q0 + NOPE].astype(BF16)
            q1 = q_ref[:, q0 + NOPE:q0 + NOPE + HALF]
            q2 = q_ref[:, q0 + NOPE + HALF:q0 + QK]
            qo_ref[h, :, NOPE:NOPE + HALF] = (q1 * c - q2 * s).astype(BF16)
            qo_ref[h, :, NOPE + HALF:QK] = (q1 * s + q2 * c).astype(BF16)
            k0 = h * (NOPE + VDIM)
            ko_ref[h, :, 0:NOPE] = kv_ref[:, k0:k0 + NOPE]
            ko_ref[h, :, NOPE:QK] = krb
            vo_ref[h] = kv_ref[:, k0 + NOPE:k0 + NOPE + VDIM]

    def row(w):
        return pl.BlockSpec((tr, w), lambda i: (i, 0))

    def heads(w):
        return pl.BlockSpec((N_HEADS, tr, w), lambda i: (0, i, 0))

    outs = pl.pallas_call(
        body, name=name, grid=(t // tr,),
        in_specs=[row(N_HEADS * QK), row(N_HEADS * (NOPE + VDIM)), row(ROPE), row(HALF), row(HALF)] + [ANY] * n,
        out_specs=[heads(QK), heads(QK), heads(VDIM)] + [ANY] * n,
        out_shape=[S((N_HEADS, t, QK), BF16), S((N_HEADS, t, QK), BF16), S((N_HEADS, t, VDIM), BF16)]
        + [S((N_CHIPS,) + s.shape, s.dtype) for s in shards],
        scratch_shapes=[pltpu.SemaphoreType.DMA((n, 3)), pltpu.SemaphoreType.DMA((n, 3))] if n else [],
        compiler_params=_params(1))(q, kv, kr, cos, sin, *shards)
    return outs[0], outs[1], outs[2], list(outs[3:])


def qkv_heads_bwd(name, dq_h, dk_h, dv_h, cos, sin, halves=(), targets=(), where=()):
    t = dq_h.shape[1]
    tr = _tile(t, 256)
    n, nt = len(halves), len(targets)

    def body(dq_ref, dk_ref, dv_ref, c_ref, s_ref, *refs):
        q_ref, kv_ref, kr_ref = refs[n:n + 3]
        if n:
            _ride_along(join_copies(refs[:n], refs[n + 3:n + 3 + nt], where, *refs[n + 3 + nt:]), (pl.program_id(0),), (t // tr,))
        c = c_ref[...]
        s = s_ref[...]
        dkr = jnp.zeros((tr, ROPE), F32)
        for h in range(N_HEADS):
            q0 = h * QK
            q_ref[:, q0:q0 + NOPE] = dq_ref[h, :, 0:NOPE].astype(BF16)
            d1 = dq_ref[h, :, NOPE:NOPE + HALF]
            d2 = dq_ref[h, :, NOPE + HALF:QK]
            q_ref[:, q0 + NOPE:q0 + NOPE + HALF] = (d1 * c + d2 * s).astype(BF16)
            q_ref[:, q0 + NOPE + HALF:q0 + QK] = (d2 * c - d1 * s).astype(BF16)
            k0 = h * (NOPE + VDIM)
            kv_ref[:, k0:k0 + NOPE] = dk_ref[h, :, 0:NOPE].astype(BF16)
            kv_ref[:, k0 + NOPE:k0 + NOPE + VDIM] = dv_ref[h].astype(BF16)
            dkr = dkr + dk_ref[h, :, NOPE:QK]
        kr_ref[...] = dkr

    def row(w):
        return pl.BlockSpec((tr, w), lambda i: (i, 0))

    def heads(w):
        return pl.BlockSpec((N_HEADS, tr, w), lambda i: (0, i, 0))

    outs = pl.pallas_call(
        body, name=name, grid=(t // tr,),
        in_specs=[heads(QK), heads(QK), heads(VDIM), row(HALF), row(HALF)] + [ANY] * n,
        out_specs=[row(N_HEADS * QK), row(N_HEADS * (NOPE + VDIM)), row(ROPE)] + [ANY] * nt,
        out_shape=[S((t, N_HEADS * QK), BF16), S((t, N_HEADS * (NOPE + VDIM)), BF16), S((t, ROPE), F32)]
        + [S(tg, F32) for tg in targets],
        scratch_shapes=[pltpu.SemaphoreType.DMA((n,)), pltpu.SemaphoreType.DMA((n,))] if n else [],
        compiler_params=_params(1))(dq_h, dk_h, dv_h, cos, sin, *halves)
    return outs[0], outs[1], outs[2], _fill_own_halves(outs[3:], halves, where)


def _chunk_mask_t(q_start, k_start, bq, bk):
    kc = (k_start + lax.broadcasted_iota(jnp.int32, (bk, bq), 0)) // CHUNK
    qc = (q_start + lax.broadcasted_iota(jnp.int32, (bk, bq), 1)) // CHUNK
    return kc <= qc


def attention_fwd(name, q, k, v, shards=()):
    nh, t, _ = q.shape
    blk = ATT_BLOCK
    nq = t // blk
    n = len(shards)

    def body(q_ref, k_ref, v_ref, *refs):
        src = refs[:n]
        o_ref, lse_ref = refs[n:n + 2]
        dst = refs[n + 2:2 * n + 2]
        m_ref, l_ref, acc_ref, s_buf, p_buf, alpha_buf, bias_ref = refs[2 * n + 2:2 * n + 9]
        i = pl.program_id(1)
        if n:
            send_sems, recv_sems = refs[2 * n + 9:]
            _ride_along(gather_ici_copies(src, dst, send_sems, recv_sems), (pl.program_id(0), i), (nh, nq))

        @pl.when((pl.program_id(0) == 0) & (i == 0))
        def _():
            bias_ref[...] = jnp.where(_chunk_mask_t(0, 0, blk, blk), 0.0, MASK_VALUE)

        m_ref[...] = jnp.full_like(m_ref, MASK_VALUE)
        l_ref[...] = jnp.zeros_like(l_ref)
        acc_ref[...] = jnp.zeros_like(acc_ref)

        def rows(b):
            return pl.ds(pl.multiple_of(b * blk, blk), blk)

        def scores(b, slot):
            s_buf[slot] = lax.dot_general(k_ref[rows(b), :], q_ref[...], _NT, preferred_element_type=F32)

        def softmax(slot, diagonal):
            s = s_buf[slot]
            if diagonal:
                s = s + bias_ref[...]
            m_old = m_ref[...]
            m_new = jnp.maximum(m_old, jnp.max(s, axis=0, keepdims=True))
            p = jnp.exp2((s - m_new) * SCORE_SCALE_LOG2)
            alpha = jnp.exp2((m_old - m_new) * SCORE_SCALE_LOG2)
            l_ref[...] = alpha * l_ref[...] + jnp.sum(p, axis=0, keepdims=True)
            m_ref[...] = m_new
            alpha_buf[slot] = alpha
            p_buf[slot] = p.astype(BF16)

        def values(b, slot):
            pv = lax.dot_general(v_ref[rows(b), :], p_buf[slot], _TN, preferred_element_type=F32)
            acc_ref[...] = alpha_buf[slot] * acc_ref[...] + pv

        def step(t, slot):
            values(t - 2, slot)
            softmax(1 - slot, False)
            scores(t, slot)

        scores(0, 0)

        @pl.when(i == 0)
        def _():
            softmax(0, True)
            values(0, 0)

        @pl.when(i > 0)
        def _():
            scores(1, 1)
            softmax(0, False)
            steady = i - 1

            def pair(u, carry):
                step(2 + 2 * u, 0)
                step(3 + 2 * u, 1)
                return carry

            lax.fori_loop(0, steady // 2, pair, 0)

            @pl.when(steady % 2 == 1)
            def _():
                step(i, 0)

            last = i % 2
            softmax(last, True)
            values(i - 1, 1 - last)
            values(i, last)

        l = l_ref[...]
        o_ref[...] = (acc_ref[...] / l).T
        lse_ref[...] = m_ref[...] * SCORE_SCALE + jnp.log(l)

    outs = pl.pallas_call(
        body, name=name, grid=(nh, nq),
        in_specs=[pl.BlockSpec((None, blk, QK), lambda h, i: (h, i, 0)), pl.BlockSpec((None, t, QK), lambda h, i: (h, 0, 0)),
                  pl.BlockSpec((None, t, VDIM), lambda h, i: (h, 0, 0))] + [ANY] * n,
        out_specs=[pl.BlockSpec((blk, VDIM), lambda h, i: (i, h)),
                   pl.BlockSpec((None, None, 1, blk), lambda h, i: (h, i, 0, 0))] + [ANY] * n,
        out_shape=[S((t, nh * VDIM), F32), S((nh, nq, 1, blk), F32)] + [S((N_CHIPS,) + s.shape, s.dtype) for s in shards],
        scratch_shapes=[pltpu.VMEM((1, blk), F32), pltpu.VMEM((1, blk), F32), pltpu.VMEM((VDIM, blk), F32),
                        pltpu.VMEM((2, blk, blk), F32), pltpu.VMEM((2, blk, blk), BF16), pltpu.VMEM((2, 1, blk), F32),
                        pltpu.VMEM((blk, blk), F32)]
        + ([pltpu.SemaphoreType.DMA((n, 3)), pltpu.SemaphoreType.DMA((n, 3))] if n else []),
        compiler_params=_params(2))(q, k, v, *shards)
    return outs[0], outs[1], list(outs[2:])


def attention_out_bwd(name, dh, w_o, o, swap=()):
    t, d = dh.shape
    n = w_o.shape[0]
    blk = ATT_BLOCK
    m = len(swap)

    def body(dh_ref, w_ref, o_ref, *refs):
        do_ref, d_ref = refs[m:m + 2]
        if m:
            _ride_along(swap_copies(refs[:m], refs[m + 2:2 * m + 2], *refs[2 * m + 2:]), (pl.program_id(0),), (t // blk,))
        do_ref[...] = lax.dot_general(dh_ref[...].astype(BF16), w_ref[...], _NT, preferred_element_type=F32)
        for h in range(N_HEADS):
            cols = slice(h * VDIM, (h + 1) * VDIM)
            d_ref[h] = jnp.sum((do_ref[:, cols] * o_ref[:, cols]).T, axis=0, keepdims=True)

    tile = pl.BlockSpec((blk, n), lambda i: (i, 0))
    outs = pl.pallas_call(
        body, name=name, grid=(t // blk,),
        in_specs=[pl.BlockSpec((blk, d), lambda i: (i, 0)), _resident((n, d), lambda i: (0, 0)), tile] + [ANY] * m,
        out_specs=[tile, pl.BlockSpec((N_HEADS, None, 1, blk), lambda i: (0, i, 0, 0))] + [ANY] * m,
        out_shape=[S((t, n), F32), S((N_HEADS, t // blk, 1, blk), F32)] + _swap_shapes(swap),
        scratch_shapes=[pltpu.SemaphoreType.DMA((m,)), pltpu.SemaphoreType.DMA((m,))] if m else [],
        compiler_params=_params(1))(dh, w_o, o, *swap)
    return outs[0], outs[1], list(outs[2:])


def attention_bwd(name, q, k, v, do, lse, delta, parts=()):
    nh, t, _ = q.shape
    blk = ATT_BLOCK
    nq = t // blk
    n_pairs = nq * (nq + 1) // 2
    n = len(parts)
    scale = SCORE_SCALE

    def body(q_ref, k_ref, v_ref, do_ref, lse_ref, dl_ref, *refs):
        src = refs[:n]
        dq_out, dk_out, dv_out = refs[n:n + 3]
        dst = refs[n + 3:2 * n + 3]
        s_buf, dp_buf, p_buf, ds_buf, bias_ref, dq_ref, dk_ref, dv_ref = refs[2 * n + 3:2 * n + 11]
        if n:
            send_sems, recv_sems = refs[2 * n + 11:]
            _ride_along(scatter_ici_copies(src, dst, send_sems, recv_sems), (pl.program_id(0),), (nh,))

        @pl.when(pl.program_id(0) == 0)
        def _():
            bias_ref[...] = jnp.where(_chunk_mask_t(0, 0, blk, blk), 0.0, MASK_VALUE)

        dq_ref[...] = jnp.zeros_like(dq_ref)
        dk_ref[...] = jnp.zeros_like(dk_ref)
        dv_ref[...] = jnp.zeros_like(dv_ref)

        def rows(x):
            return pl.ds(pl.multiple_of(x * blk, blk), blk)

        def after(jb):
            j, b = jb
            wrap = b == nq - 1 - j
            return jnp.where(wrap, j + 1, j), jnp.where(wrap, 0, b + 1)

        def products(jb, slot):
            j, b = jb
            s_buf[slot] = lax.dot_general(k_ref[rows(j), :], q_ref[rows(j + b), :], _NT, preferred_element_type=F32)
            dp_buf[slot] = lax.dot_general(v_ref[rows(j), :], do_ref[rows(j + b), :].astype(BF16), _NT, preferred_element_type=F32)

        def softmax_bwd(jb, slot):
            j, b = jb
            s = s_buf[slot] + bias_ref[...] * (b == 0).astype(F32)
            p = jnp.exp2(s * SCORE_SCALE_LOG2 - lse_ref[j + b] * LOG2_E)
            p_buf[slot] = p.astype(BF16)
            ds_buf[slot] = (p * (dp_buf[slot] - dl_ref[j + b]) * scale).astype(BF16)

        def gradients(jb, slot):
            j, b = jb
            dv_ref[rows(j), :] += jnp.dot(p_buf[slot], do_ref[rows(j + b), :].astype(BF16), preferred_element_type=F32)
            dk_ref[rows(j), :] += jnp.dot(ds_buf[slot], q_ref[rows(j + b), :], preferred_element_type=F32)
            dq_ref[rows(j + b), :] += lax.dot_general(ds_buf[slot], k_ref[rows(j), :], _TN, preferred_element_type=F32)

        def step(state, slot):
            third, second, first = state
            gradients(third, slot)
            softmax_bwd(second, 1 - slot)
            products(first, slot)
            return second, first, after(first)

        zero = jnp.int32(0)
        pair0 = (zero, zero)
        products(pair0, 0)
        if n_pairs == 1:
            softmax_bwd(pair0, 0)
            gradients(pair0, 0)
        else:
            pair1 = after(pair0)
            products(pair1, 1)
            softmax_bwd(pair0, 0)
            steady = n_pairs - 2
            state = lax.fori_loop(0, steady // 2, lambda u, st: step(step(st, 0), 1), (pair0, pair1, after(pair1)))
            if steady % 2:
                state = step(state, 0)
            before_last, last_pair, _ = state
            last = (n_pairs - 1) % 2
            softmax_bwd(last_pair, last)
            gradients(before_last, 1 - last)
            gradients(last_pair, last)
        dq_out[...] = dq_ref[...].astype(BF16)
        dk_out[...] = dk_ref[...].astype(BF16)
        dv_out[...] = dv_ref[...].astype(BF16)

    head = lambda w: pl.BlockSpec((None, t, w), lambda h: (h, 0, 0))
    stats = pl.BlockSpec((None, nq, 1, blk), lambda h: (h, 0, 0, 0))
    outs = pl.pallas_call(
        body, name=name, grid=(nh,),
        in_specs=[head(QK), head(QK), head(VDIM), pl.BlockSpec((t, VDIM), lambda h: (0, h)), stats, stats] + [ANY] * n,
        out_specs=[head(QK), head(QK), head(VDIM)] + [ANY] * n,
        out_shape=[S((nh, t, QK), BF16), S((nh, t, QK), BF16), S((nh, t, VDIM), BF16)] + [S(p.shape, p.dtype) for p in parts],
        scratch_shapes=[pltpu.VMEM((2, blk, blk), F32), pltpu.VMEM((2, blk, blk), F32), pltpu.VMEM((2, blk, blk), BF16),
                        pltpu.VMEM((2, blk, blk), BF16), pltpu.VMEM((blk, blk), F32),
                        pltpu.VMEM((t, QK), F32), pltpu.VMEM((t, QK), F32), pltpu.VMEM((t, VDIM), F32)]
        + ([pltpu.SemaphoreType.DMA((n, 3)), pltpu.SemaphoreType.DMA((n, 3))] if n else []),
        compiler_params=_params(1, VMEM_LIMIT_WHOLE_HEAD))(q, k, v, do, lse, delta, *parts)
    return outs[0], outs[1], outs[2], list(outs[3:])


def _shift_down(u, s):
    rows = lax.broadcasted_iota(jnp.int32, u.shape, 0)
    return jnp.where(rows >= s, pltpu.roll(u, s, 0), 0.0)


def _shift_up(u, s):
    n = u.shape[0]
    rows = lax.broadcasted_iota(jnp.int32, u.shape, 0)
    return jnp.where(rows < n - s, pltpu.roll(u, n - s, 0), 0.0)


def _conv_specs(t, d, lanes):
    slab = lambda part: pl.BlockSpec((None, t, lanes), lambda j, part=part: (part, 0, j))
    return slab, pl.BlockSpec((3, lanes), lambda j: (0, j)), pl.BlockSpec((t, lanes), lambda j: (0, j))


def conv_bwd(name, bcx, w, dy):
    _, t, d = bcx.shape
    lanes = _tile(d, 128, 128)
    slab, w_spec, col = _conv_specs(t, d, lanes)

    def body(b_ref, c_ref, x_ref, w_ref, dy_ref, d_ref, dw_ref):
        c = c_ref[...].astype(F32)
        x = x_ref[...].astype(F32)
        dyv = dy_ref[...]
        u = c * x
        u1 = _shift_down(u, 1)
        u2 = _shift_down(u, 2)
        w0, w1, w2 = w_ref[0:1, :], w_ref[1:2, :], w_ref[2:3, :]
        d_ref[0] = (dyv * (w0 * u2 + w1 * u1 + w2 * u)).astype(BF16)
        duc = dyv * b_ref[...].astype(F32)
        dw_ref[0:1, :] = jnp.sum(duc * u2, axis=0, keepdims=True)
        dw_ref[1:2, :] = jnp.sum(duc * u1, axis=0, keepdims=True)
        dw_ref[2:3, :] = jnp.sum(duc * u, axis=0, keepdims=True)
        du = w2 * duc + w1 * _shift_up(duc, 1) + w0 * _shift_up(duc, 2)
        d_ref[1] = (du * x).astype(BF16)
        d_ref[2] = (du * c).astype(BF16)

    return pl.pallas_call(
        body, name=name, grid=(d // lanes,), in_specs=[slab(0), slab(1), slab(2), w_spec, col],
        out_specs=[pl.BlockSpec((3, t, lanes), lambda j: (0, 0, j)), w_spec], out_shape=[S((3, t, d), BF16), S((3, d), F32)],
        compiler_params=_params(1))(bcx, bcx, bcx, w, dy)


def _adamw_update(w, g, m, v):
    m_new = ADAM_B1 * m + (1.0 - ADAM_B1) * g
    v_new = ADAM_B2 * v + (1.0 - ADAM_B2) * (g * g)
    m_hat = m_new / (1.0 - ADAM_B1 ** ADAM_STEP)
    v_hat = v_new / (1.0 - ADAM_B2 ** ADAM_STEP)
    return -ADAM_LR * (m_hat / (jnp.sqrt(v_hat) + ADAM_EPS) + ADAM_WD * w), m_new, v_new


def adamw(name, w, g, m, v):
    r, c = w.shape
    tr = _tile(r, 512)

    def body(w_ref, g_ref, m_ref, v_ref, d_ref, mo_ref, vo_ref):
        d_ref[...], mo_ref[...], vo_ref[...] = _adamw_update(w_ref[...], g_ref[...], m_ref[...], v_ref[...])

    blk = pl.BlockSpec((tr, c), lambda i: (i, 0))
    return pl.pallas_call(
        body, name=name, grid=(r // tr,), in_specs=[blk] * 4, out_specs=[blk] * 3, out_shape=[S((r, c), F32)] * 3,
        compiler_params=_params(1))(w, g, m, v)


def adamw_swapped(name, wt, g, mt, vt):
    nl, c, r = wt.shape
    tr = _tile(r, 512, 128)
    nr = r // tr

    def body(w_ref, g_ref, m_ref, v_ref, go_ref, d_ref, mo_ref, vo_ref):
        gt = g_ref[...].T
        go_ref[...] = gt
        d_ref[...], mo_ref[...], vo_ref[...] = _adamw_update(w_ref[...], gt, m_ref[...], v_ref[...])

    swapped = pl.BlockSpec((None, c, tr), lambda l, i: (l, 0, i))
    return pl.pallas_call(
        body, name=name, grid=(nl, nr),
        in_specs=[swapped, pl.BlockSpec((tr, c), lambda l, i: (l * nr + i, 0)), swapped, swapped],
        out_specs=[swapped] * 4, out_shape=[S((nl, c, r), F32)] * 4, compiler_params=_params(2))(wt, g, mt, vt)


def _place():
    x, y, c = lax.axis_index("x"), lax.axis_index("y"), lax.axis_index("c")
    other_chips = [(1 - x, y), (x, 1 - y), (1 - x, 1 - y)]
    return x, y, c, other_chips


def _half(c, rows):
    return pl.ds(pl.multiple_of(c * (rows // 2), 16), rows // 2)


def gather_weight_shards(shards):
    n = len(shards)

    def body(*refs):
        src = refs[:n]
        dst = refs[n:2 * n]
        send_sems, recv_sems = refs[2 * n:]
        x, y, c, chips = _place()
        me = 2 * x + y
        sibling = (x, y, 1 - c)

        def copy(i, slot, half_of, sem, to, from_input=False):
            rows = _half(half_of, src[i].shape[0])
            return pltpu.make_async_remote_copy(
                src_ref=src[i].at[rows] if from_input else dst[i].at[slot, rows], dst_ref=dst[i].at[slot, rows],
                send_sem=send_sems.at[i, sem], recv_sem=recv_sems.at[i, sem], device_id=to, device_id_type=MESH)

        sent = []
        for i in range(n):
            for j, chip in enumerate(chips):
                sent.append(copy(i, me, c, j, (*chip, c), from_input=True))
                sent[-1].start()
        for i in range(n):
            for j, (px, py) in enumerate(chips):
                copy(i, 2 * px + py, c, j, sibling).wait_recv()
                sent.append(copy(i, 2 * px + py, c, 3 + j, sibling))
                sent[-1].start()
        for i in range(n):
            for j, (px, py) in enumerate(chips):
                copy(i, 2 * px + py, 1 - c, 3 + j, sibling).wait_recv()
        for cp in sent:
            cp.wait_send()

    outs = pl.pallas_call(
        body, name="gather_weight_shards", in_specs=[ANY] * n, out_specs=[ANY] * n,
        out_shape=[S((N_CHIPS,) + s.shape, s.dtype) for s in shards],
        scratch_shapes=[pltpu.SemaphoreType.DMA((n, 6)), pltpu.SemaphoreType.DMA((n, 6))],
    )(*shards)
    return _fill_own_slot(outs, [s[None] for s in shards])


def gather_ici_copies(src, dst, send_sems, recv_sems):
    x, y, c, chips = _place()
    me = 2 * x + y
    pairs = []
    for i in range(len(src)):
        rows = _half(c, src[i].shape[0])
        for j, (px, py) in enumerate(chips):
            def copy(slot):
                return pltpu.make_async_remote_copy(
                    src_ref=src[i].at[rows], dst_ref=dst[i].at[slot, rows], send_sem=send_sems.at[i, j],
                    recv_sem=recv_sems.at[i, j], device_id=(px, py, c), device_id_type=MESH)
            pairs.append((copy(me), copy(2 * px + py)))
    return pairs


def scatter_ici_copies(src, dst, send_sems, recv_sems):
    x, y, c, chips = _place()
    me = 2 * x + y
    pairs = []
    for i in range(len(src)):
        for j, (px, py) in enumerate(chips):
            def copy(from_slot, to_slot):
                return pltpu.make_async_remote_copy(
                    src_ref=src[i].at[from_slot], dst_ref=dst[i].at[to_slot], send_sem=send_sems.at[i, j],
                    recv_sem=recv_sems.at[i, j], device_id=(px, py, c), device_id_type=MESH)
            pairs.append((copy(2 * px + py, me), copy(me, 2 * px + py)))
    return pairs


def _ride_along(pairs, grid_ids, grid_sizes):
    first = grid_ids[0] == 0
    last = grid_ids[0] == grid_sizes[0] - 1
    for g, size in zip(grid_ids[1:], grid_sizes[1:]):
        first = first & (g == 0)
        last = last & (g == size - 1)

    @pl.when(first)
    def _():
        for outgoing, _ in pairs:
            outgoing.start()

    @pl.when(last)
    def _():
        for _, incoming in pairs:
            incoming.wait_recv()
        for outgoing, _ in pairs:
            outgoing.wait_send()


def _fill_own_slot(gathered, own):
    me = 2 * lax.axis_index("x") + lax.axis_index("y")
    return [lax.dynamic_update_slice(g, o, (me,) + (0,) * (g.ndim - 1)) for g, o in zip(gathered, own)]


def forward_copies(src, dst, send_sems, recv_sems):
    x, y, c, chips = _place()
    pairs = []
    for i in range(len(src)):
        for j, (px, py) in enumerate(chips):
            def copy(half_of):
                rows = _half(half_of, src[i].shape[1])
                return pltpu.make_async_remote_copy(
                    src_ref=src[i].at[2 * px + py, rows], dst_ref=dst[i].at[2 * px + py, rows], send_sem=send_sems.at[i, j],
                    recv_sem=recv_sems.at[i, j], device_id=(x, y, 1 - c), device_id_type=MESH)
            pairs.append((copy(c), copy(1 - c)))
    return pairs


def attention_out_proj(name, attn, w_o, resid, next_gain, arriving):
    t, kdim = attn.shape
    n = w_o.shape[1]
    tm = _tile(t, 512)
    m = len(arriving)

    def body(x_ref, w_ref, r_ref, g_ref, *refs):
        h_ref, a_ref = refs[m:m + 2]
        _ride_along(forward_copies(refs[:m], refs[m + 2:2 * m + 2], *refs[2 * m + 2:]), (pl.program_id(0),), (t // tm,))
        y = r_ref[...] + jnp.dot(x_ref[...].astype(BF16), w_ref[...], preferred_element_type=F32)
        h_ref[...] = y
        a_ref[...] = (y * _rstd(y) * g_ref[...]).astype(BF16)

    row = pl.BlockSpec((tm, n), lambda i: (i, 0))
    outs = pl.pallas_call(
        body, name=name, grid=(t // tm,),
        in_specs=[pl.BlockSpec((tm, kdim), lambda i: (i, 0)), _resident((kdim, n), lambda i: (0, 0)), row,
                  pl.BlockSpec((1, n), lambda i: (0, 0))] + [ANY] * m,
        out_specs=[row, row] + [ANY] * m, out_shape=[S((t, n), F32), S((t, n), BF16)] + [S(g.shape, g.dtype) for g in arriving],
        input_output_aliases={4 + i: 2 + i for i in range(m)},
        scratch_shapes=[pltpu.SemaphoreType.DMA((m, 3)), pltpu.SemaphoreType.DMA((m, 3))],
        compiler_params=_params(1))(attn, w_o, resid, next_gain, *arriving)
    return outs[0], outs[1], list(outs[2:])


def swap_copies(src, dst, send_sems, recv_sems):
    x, y, c, _ = _place()
    pairs = []
    for i in range(len(src)):
        cp = pltpu.make_async_remote_copy(
            src_ref=src[i].at[:, _half(1 - c, src[i].shape[1]), :], dst_ref=dst[i], send_sem=send_sems.at[i],
            recv_sem=recv_sems.at[i], device_id=(x, y, 1 - c), device_id_type=MESH)
        pairs.append((cp, cp))
    return pairs


def _swap_shapes(grads):
    return [S((g.shape[0], g.shape[1] // 2, g.shape[2]), g.dtype) for g in grads]


def sibling_swap_halves(name, grads):
    n = len(grads)

    def body(*refs):
        pairs = swap_copies(refs[:n], refs[n:2 * n], *refs[2 * n:])
        for outgoing, _ in pairs:
            outgoing.start()
        for _, incoming in pairs:
            incoming.wait_recv()
        for outgoing, _ in pairs:
            outgoing.wait_send()

    return pl.pallas_call(
        body, name=name, in_specs=[ANY] * n, out_specs=[ANY] * n, out_shape=_swap_shapes(grads),
        scratch_shapes=[pltpu.SemaphoreType.DMA((n,)), pltpu.SemaphoreType.DMA((n,))],
    )(*grads)


def add_halves(name, g, rx):
    _, r, cdim = g.shape
    r2 = r // 2
    tr = _tile(r2, 512, 16)
    nb = r2 // tr

    def body(lo_ref, hi_ref, rx_ref, o_ref):
        mine = jnp.where(lax.axis_index("c") == 0, lo_ref[...], hi_ref[...])
        o_ref[...] = (mine.astype(F32) + rx_ref[...].astype(F32)).astype(BF16)

    half = pl.BlockSpec((None, tr, cdim), lambda k, i: (k, i, 0))
    return pl.pallas_call(
        body, name=name, grid=(N_CHIPS, nb),
        in_specs=[half, pl.BlockSpec((None, tr, cdim), lambda k, i: (k, nb + i, 0)), half],
        out_specs=half, out_shape=S((N_CHIPS, r2, cdim), BF16), compiler_params=_params(2))(g, g, rx)


def sum_chips(name, arrived, mine):
    _, r2, cdim = arrived.shape
    tr = _tile(r2, 512, 16)

    def body(a_ref, m_ref, o_ref):
        me = 2 * lax.axis_index("x") + lax.axis_index("y")
        acc = jnp.zeros((tr, cdim), F32)
        for k in range(N_CHIPS):
            acc = acc + jnp.where(me == k, m_ref[k], a_ref[k]).astype(F32)
        o_ref[...] = acc

    slots = pl.BlockSpec((N_CHIPS, tr, cdim), lambda i: (0, i, 0))
    return pl.pallas_call(
        body, name=name, grid=(r2 // tr,), in_specs=[slots, slots],
        out_specs=pl.BlockSpec((tr, cdim), lambda i: (i, 0)), out_shape=S((r2, cdim), F32), compiler_params=_params(1))(arrived, mine)


def join_copies(src, dst, where, send_sems, recv_sems):
    x, y, c, _ = _place()
    pairs = []
    for i in range(len(src)):
        def copy(half_of):
            r2 = src[i].shape[0]
            rows = pl.ds(pl.multiple_of(where[i][1] + half_of * r2, 8), r2)
            return pltpu.make_async_remote_copy(
                src_ref=src[i], dst_ref=dst[where[i][0]].at[rows], send_sem=send_sems.at[i],
                recv_sem=recv_sems.at[i], device_id=(x, y, 1 - c), device_id_type=MESH)
        pairs.append((copy(c), copy(1 - c)))
    return pairs


def _fill_own_halves(targets, halves, where):
    targets = list(targets)
    c = lax.axis_index("c")
    for h, (tgt, first) in zip(halves, where):
        targets[tgt] = lax.dynamic_update_slice(targets[tgt], h, (first + c * h.shape[0], 0))
    return targets


def sibling_join_halves(name, halves, targets, where):
    n = len(halves)

    def body(*refs):
        pairs = join_copies(refs[:n], refs[n:n + len(targets)], where, *refs[n + len(targets):])
        for outgoing, _ in pairs:
            outgoing.start()
        for _, incoming in pairs:
            incoming.wait_recv()
        for outgoing, _ in pairs:
            outgoing.wait_send()

    outs = pl.pallas_call(
        body, name=name, in_specs=[ANY] * n, out_specs=[ANY] * len(targets), out_shape=[S(tg, F32) for tg in targets],
        scratch_shapes=[pltpu.SemaphoreType.DMA((n,)), pltpu.SemaphoreType.DMA((n,))],
    )(*halves)
    return _fill_own_halves(outs, halves, where)


def all_reduce_small(name, packed):
    rows, width = packed.shape

    def body(x_ref, o_ref, gathered, send_sems, recv_sems):
        x, y, c, _ = _place()
        me = 4 * x + 2 * y + c
        gathered[me] = x_ref[...]
        flips = [(fx, fy, fc) for fx in (0, 1) for fy in (0, 1) for fc in (0, 1)][1:]

        def copy(r, slot, to):
            return pltpu.make_async_remote_copy(
                src_ref=x_ref, dst_ref=gathered.at[slot], send_sem=send_sems.at[r], recv_sem=recv_sems.at[r],
                device_id=to, device_id_type=MESH)

        def peer(f):
            return (x ^ f[0], y ^ f[1], c ^ f[2])

        sent = [copy(r, me, peer(f)) for r, f in enumerate(flips)]
        for cp in sent:
            cp.start()
        for r, f in enumerate(flips):
            px, py, pc = peer(f)
            copy(r, 4 * px + 2 * py + pc, peer(f)).wait_recv()
        for cp in sent:
            cp.wait_send()
        acc = gathered[0]
        for k in range(1, N_DEV):
            acc = acc + gathered[k]
        o_ref[...] = acc

    vmem = pl.BlockSpec(memory_space=pltpu.VMEM)
    return pl.pallas_call(
        body, name=name, in_specs=[vmem], out_specs=vmem, out_shape=S((rows, width), F32),
        scratch_shapes=[pltpu.VMEM((N_DEV, rows, width), F32), pltpu.SemaphoreType.DMA((N_DEV - 1,)),
                        pltpu.SemaphoreType.DMA((N_DEV - 1,))],
    )(packed)


def _rope_tables(positions):
    inv_freq = 1.0 / (ROPE_THETA ** (jnp.arange(0, ROPE, 2, dtype=F32) / ROPE))
    ang = positions.astype(F32)[:, None] * inv_freq
    return jnp.cos(ang), jnp.sin(ang)


def _unstack_cols(w):
    k4, k, n4 = w.shape
    return jnp.transpose(w, (1, 0, 2)).reshape(k, k4 * n4)


def _stack_cols(w):
    k, n = w.shape
    return jnp.transpose(w.reshape(k, N_CHIPS, n // N_CHIPS), (1, 0, 2))


def kernel(x, positions, mla_norm, mla_w_in, mla_g_cq, mla_g_ckv, mla_w_uq, mla_w_ukv, mla_w_o, conv_norm, conv_w_in, conv_w, conv_w_out, ffn_norm, ffn_w_gate, ffn_w_up, ffn_w_down, final_norm, loss_target, m_mla_norm, m_mla_w_in, m_mla_g_cq, m_mla_g_ckv, m_mla_w_uq, m_mla_w_ukv, m_mla_w_o, m_conv_norm, m_conv_w_in, m_conv_w, m_conv_w_out, m_ffn_norm, m_ffn_w_gate, m_ffn_w_up, m_ffn_w_down, m_final_norm, v_mla_norm, v_mla_w_in, v_mla_g_cq, v_mla_g_ckv, v_mla_w_uq, v_mla_w_ukv, v_mla_w_o, v_conv_norm, v_conv_w_in, v_conv_w, v_conv_w_out, v_ffn_norm, v_ffn_w_gate, v_ffn_w_up, v_ffn_w_down, v_final_norm):
    weights = dict(mla_norm=mla_norm, mla_w_in=mla_w_in, mla_g_cq=mla_g_cq, mla_g_ckv=mla_g_ckv, mla_w_uq=mla_w_uq,
                   mla_w_ukv=mla_w_ukv, mla_w_o=mla_w_o, conv_norm=conv_norm, conv_w_in=conv_w_in, conv_w=conv_w,
                   conv_w_out=conv_w_out, ffn_norm=ffn_norm, ffn_w_gate=ffn_w_gate, ffn_w_up=ffn_w_up,
                   ffn_w_down=ffn_w_down, final_norm=final_norm)
    m_in = dict(mla_norm=m_mla_norm, mla_w_in=m_mla_w_in, mla_g_cq=m_mla_g_cq, mla_g_ckv=m_mla_g_ckv, mla_w_uq=m_mla_w_uq,
                mla_w_ukv=m_mla_w_ukv, mla_w_o=m_mla_w_o, conv_norm=m_conv_norm, conv_w_in=m_conv_w_in, conv_w=m_conv_w,
                conv_w_out=m_conv_w_out, ffn_norm=m_ffn_norm, ffn_w_gate=m_ffn_w_gate, ffn_w_up=m_ffn_w_up,
                ffn_w_down=m_ffn_w_down, final_norm=m_final_norm)
    v_in = dict(mla_norm=v_mla_norm, mla_w_in=v_mla_w_in, mla_g_cq=v_mla_g_cq, mla_g_ckv=v_mla_g_ckv, mla_w_uq=v_mla_w_uq,
                mla_w_ukv=v_mla_w_ukv, mla_w_o=v_mla_w_o, conv_norm=v_conv_norm, conv_w_in=v_conv_w_in, conv_w=v_conv_w,
                conv_w_out=v_conv_w_out, ffn_norm=v_ffn_norm, ffn_w_gate=v_ffn_w_gate, ffn_w_up=v_ffn_w_up,
                ffn_w_down=v_ffn_w_down, final_norm=v_final_norm)
    big = ["mla_w_in", "mla_w_uq", "mla_w_ukv", "mla_w_o", "conv_w_in", "conv_w_out", "ffn_w_gate", "ffn_w_up", "ffn_w_down"]
    order = list(weights)

    t, d = x.shape[1], x.shape[2]
    h0 = x.reshape(t, d)
    target = loss_target.reshape(t, d)
    cos, sin = _rope_tables(positions.reshape(t))

    def rows2d(a):
        return a.reshape(-1, a.shape[-1])

    first, later = big[:4], big[4:]
    shards = {n: rows2d(weights[n]).astype(BF16) for n in big}
    gathered = dict(zip(first, gather_weight_shards([shards[n] for n in first])))
    w_in = gathered["mla_w_in"].reshape(-1, gathered["mla_w_in"].shape[-1])
    w_uq = _unstack_cols(gathered["mla_w_uq"])
    w_ukv = _unstack_cols(gathered["mla_w_ukv"])
    w_o = gathered["mla_w_o"].reshape(-1, d)

    chip = 2 * lax.axis_index("x") + lax.axis_index("y")
    core = lax.axis_index("c")
    d4 = d // N_CHIPS
    first_core = (core == 0).astype(F32)

    def place_shard(shard):
        full = jnp.zeros((shard.shape[0], d), F32)
        return lax.dynamic_update_slice(full, shard * first_core, (0, chip * d4))

    def pack_rows(rows):
        idx = lax.broadcasted_iota(jnp.int32, (SMALL_ROWS, d), 0)
        out = jnp.zeros((SMALL_ROWS, d), F32)
        for r, row in enumerate(rows):
            out = out + jnp.where(idx == r, row, 0.0)
        return out

    cw = place_shard(conv_w.reshape(3, d4))
    pre = all_reduce_small("all_gather_conv_small", pack_rows([place_shard(conv_norm.reshape(1, d4)), cw[0:1], cw[1:2], cw[2:3]]))
    conv_norm_full = pre[0:1]
    conv_w_full = pre[1:4]

    a0 = rms_fwd("mla_norm_fwd", h0, mla_norm)
    proj, cq, ckv, kr = mla_in_proj("mla_in_proj", a0, w_in, mla_g_cq, mla_g_ckv, cos, sin)
    q = linear("mla_q_up", cq, w_uq, F32)
    kv = linear("mla_kv_up", ckv, w_ukv, BF16)
    qh, kh, vh, conv_arriving = qkv_heads("qkv_heads", q, kv, kr, cos, sin, [shards[n] for n in later[:2]])
    attn, lse, ffn_arriving = attention_fwd("attention_fwd", qh, kh, vh, [shards[n] for n in later[2:]])
    h1, a1, handed = attention_out_proj("mla_out_proj", attn, w_o, h0, ffn_norm[0:1], conv_arriving + ffn_arriving)
    gathered.update(zip(later, _fill_own_slot(handed, [shards[n][None] for n in later])))
    cw_in = _unstack_cols(gathered["conv_w_in"])
    cw_out = gathered["conv_w_out"].reshape(-1, d)
    wg_all, wu_all, wd_all = gathered["ffn_w_gate"], gathered["ffn_w_up"], gathered["ffn_w_down"]

    def ffn_forward(tag, h, a, layer, next_gain):
        g, u, z = ffn_up(f"ffn{tag}_up", a, wg_all, wu_all, layer)
        return g, u, z, ffn_down(f"ffn{tag}_down", z, wd_all, layer, h, next_gain)

    g0, u0, z0, (h2, a2) = ffn_forward(0, h1, a1, 0, conv_norm_full)
    bcx, yc = conv_in_proj("conv_in_proj", a2, cw_in, conv_w_full)
    h3, a3 = linear("conv_out_proj", yc, cw_out, F32, resid=h2, next_gain=ffn_norm[1:2])
    g1, u1, z1 = ffn_up("ffn1_up", a3, wg_all, wu_all, 1)
    dh4, d_final_norm, loss_local = ffn_down_loss("ffn1_down_loss", z1, wd_all, 1, h3, final_norm.reshape(1, d), target)

    def ffn_backward(tag, dh, h, layer, a, g, u, z, swap=()):
        dg, du, swapped = ffn_bwd_hidden(f"ffn{tag}_bwd_hidden", dh, wd_all, layer, g, u, swap)
        d_wd = ffn_wgrad_down(f"ffn{tag}_wgrad_down", z, dh)
        dh_prev, d_norm = ffn_bwd_input(f"ffn{tag}_bwd_input", dg, du, wg_all, wu_all, layer, h, ffn_norm[layer:layer + 1], dh)
        d_wg = ffn_wgrad_up(f"ffn{tag}_wgrad_gate", a, dg)
        d_wu = ffn_wgrad_up(f"ffn{tag}_wgrad_up", a, du)
        return dh_prev, d_norm, [d_wg, d_wu, d_wd], swapped

    def pair_sums(tag, local, from_sibling):
        return [add_halves(f"pair_sum_{tag}{i}", g, r) for i, (g, r) in enumerate(zip(local, from_sibling))]

    def sum_from_chips(tag, pairs, arrived):
        return [sum_chips(f"chip_sum_{tag}{i}", a, p) for i, (a, p) in enumerate(zip(arrived, pairs))]

    def shard_shape(n):
        return rows2d(weights[n]).shape

    dh3, d_ffn_norm1, ffn1_grads, _ = ffn_backward(1, dh4, h3, 1, a3, g1, u1, z1)

    dyc = linear_nt("conv_out_bwd_input", dh3, cw_out, F32)
    d_cw_out = wgrad("conv_out_wgrad", yc, dh3)
    dbcx, d_conv_w = conv_bwd("conv_bwd", bcx, conv_w_full, dyc)
    dh2, d_conv_norm = conv_in_bwd_input("conv_in_bwd_input", dbcx, cw_in, h2, conv_norm_full, dh3)
    d_cw_in = conv_in_wgrad("conv_in_wgrad", a2, dbcx)

    second = [d_cw_in, d_cw_out.reshape(N_CHIPS, -1, d)] + ffn1_grads
    dh1, d_ffn_norm0, ffn0_grads, second_swapped = ffn_backward(0, dh2, h1, 0, a1, g0, u0, z0, second)
    d_w_o = wgrad("mla_out_wgrad", attn, dh1)
    first_part = ffn0_grads + [d_w_o.reshape(N_CHIPS, -1, d)]
    d_attn, delta, first_swapped = attention_out_bwd("mla_out_bwd_input", dh1, w_o, attn, first_part)
    rest_pairs = pair_sums("rest", second + first_part, second_swapped + first_swapped)
    dqh, dkh, dvh, rest_arrived = attention_bwd("attention_bwd", qh, kh, vh, d_attn, lse, delta, rest_pairs)
    rd, rf = ffn0_grads[0].shape[1], ffn0_grads[2].shape[1]
    rest_where = [(0, 0), (1, 0), (2, rd), (3, rd), (4, rf), (2, 0), (3, 0), (4, 0), (5, 0)]
    rest_names = later + ["mla_w_o"]
    dq, dkv, dkr, rest_grads = qkv_heads_bwd("qkv_heads_bwd", dqh, dkh, dvh, cos, sin, sum_from_chips("rest", rest_pairs, rest_arrived),
                                              [shard_shape(n) for n in rest_names], rest_where)
    grads = dict(zip(rest_names, rest_grads))
    dcq = linear_nt("mla_q_up_bwd_input", dq, w_uq, F32)
    d_w_uq = wgrad("mla_q_up_wgrad", cq, dq)
    dckv = linear_nt("mla_kv_up_bwd_input", dkv, w_ukv, F32)
    d_w_ukv = wgrad("mla_kv_up_wgrad", ckv, dkv)
    dproj, d_g_cq, d_g_ckv = mla_mid_bwd("mla_mid_bwd", proj, mla_g_cq, mla_g_ckv, dcq, dckv, dkr, cos, sin)
    d_w_in = wgrad("mla_in_wgrad", a0, dproj)
    mla_local = [d_w_in.reshape(N_CHIPS, -1, d_w_in.shape[-1]), _stack_cols(d_w_uq), _stack_cols(d_w_ukv)]
    mla_pairs = pair_sums("mla", mla_local, sibling_swap_halves("sibling_swap_mla", mla_local))
    grad_x, d_mla_norm, mla_arrived = linear_nt_norm_bwd("mla_in_bwd_input", dproj, w_in, h0, mla_norm, dh1, mla_pairs)

    grads.update(zip(first[:3], sibling_join_halves("sibling_join_mla", sum_from_chips("mla", mla_pairs, mla_arrived),
                                                    [shard_shape(n) for n in first[:3]], [(i, 0) for i in range(3)])))

    def pad_row(v):
        return jnp.pad(v, ((0, 0), (0, d - v.shape[1])))

    small = all_reduce_small("all_reduce_small_grads", pack_rows([
        d_mla_norm, pad_row(d_g_cq), pad_row(d_g_ckv), d_ffn_norm0, d_ffn_norm1, d_final_norm, d_conv_norm,
        d_conv_w[0:1], d_conv_w[1:2], d_conv_w[2:3], jnp.broadcast_to(loss_local, (1, d))]))
    loss = small[10, 0]
    grads["mla_norm"] = small[0:1]
    grads["mla_g_cq"] = small[1:2, :mla_g_cq.shape[1]]
    grads["mla_g_ckv"] = small[2:3, :mla_g_ckv.shape[1]]
    grads["ffn_norm"] = small[3:5]
    grads["final_norm"] = small[5:6]
    grads["conv_norm"] = lax.dynamic_slice(small[6:7], (0, chip * d4), (1, d4))
    grads["conv_w"] = lax.dynamic_slice(small[7:10], (0, chip * d4), (3, d4))

    outs_g, outs_d, outs_m, outs_v = [], [], [], []
    for n in order:
        w = weights[n]
        if w.ndim == 3 and w.shape[2] % 128 and w.shape[1] % 128 == 0:
            results = adamw_swapped(f"adamw_{n}", jnp.swapaxes(w, 1, 2), grads[n].reshape(-1, w.shape[2]),
                                    jnp.swapaxes(m_in[n], 1, 2), jnp.swapaxes(v_in[n], 1, 2))
            grad_w, delta_w, new_m, new_v = [jnp.swapaxes(o, 1, 2) for o in results]
        else:
            delta_w, new_m, new_v = adamw(f"adamw_{n}", rows2d(w), grads[n].reshape(rows2d(w).shape), rows2d(m_in[n]), rows2d(v_in[n]))
            grad_w = grads[n]
        outs_g.append(grad_w.reshape(w.shape))
        outs_d.append(delta_w.reshape(w.shape))
        outs_m.append(new_m.reshape(w.shape))
        outs_v.append(new_v.reshape(w.shape))
    return (loss, grad_x.reshape(x.shape), *outs_g, *outs_d, *outs_m, *outs_v)
```

```python
import math

import jax
import jax.numpy as jnp
from jax import lax
from jax.experimental import pallas as pl
from jax.experimental.pallas import tpu as pltpu

F32 = jnp.float32
BF16 = jnp.bfloat16
S = jax.ShapeDtypeStruct

N_HEADS = 8
NOPE = 128
ROPE = 64
HALF = ROPE // 2
VDIM = 128
QK = NOPE + ROPE
CHUNK = 64
ROPE_THETA = 10000.0
RMS_EPS = 1e-6
ADAM_LR = 0.001
ADAM_B1 = 0.9
ADAM_B2 = 0.999
ADAM_EPS = 1e-08
ADAM_WD = 0.01
ADAM_STEP = 10

N_CHIPS = 4
N_DEV = 8
MASK_VALUE = -1e30
SCORE_SCALE = 1.0 / math.sqrt(QK)
LOG2_E = math.log2(math.e)
SCORE_SCALE_LOG2 = SCORE_SCALE * LOG2_E
VMEM_LIMIT = 48 * 1024 * 1024
VMEM_LIMIT_WHOLE_HEAD = 58 * 1024 * 1024
ATT_BLOCK = 512
CONV_SAVED_DTYPE = jnp.bfloat16
SMALL_ROWS = 16

_NN = (((1,), (0,)), ((), ()))
_NT = (((1,), (1,)), ((), ()))
_TN = (((0,), (0,)), ((), ()))
MESH = pl.DeviceIdType.MESH
ANY = pl.BlockSpec(memory_space=pl.ANY)


def _params(n_axes, vmem_limit=VMEM_LIMIT):
    return pltpu.CompilerParams(dimension_semantics=("arbitrary",) * n_axes, vmem_limit_bytes=vmem_limit)


def _tile(n, cap, mult=8):
    for t in range(min(cap, n), 0, -1):
        if n % t == 0 and t % mult == 0:
            return t
    return n


def _sigmoid(x):
    return 0.5 * jnp.tanh(0.5 * x) + 0.5


def _mm(name, a_ops, b_ops, products, dims, grid, k_axis, outs, acc_shape, epilogue, extra_ops=()):
    na, nb, ne, no = len(a_ops), len(b_ops), len(extra_ops), len(outs)
    n_acc = 1 + max(c for _, _, c in products)
    nk = 1 if k_axis is None else grid[k_axis]

    def body(*refs):
        a_refs = refs[:na]
        b_refs = refs[na:na + nb]
        e_refs = refs[na + nb:na + nb + ne]
        o_refs = refs[na + nb + ne:na + nb + ne + no]
        acc_refs = refs[na + nb + ne + no:]

        def partial_sums():
            vals = [None] * n_acc
            for ai, bi, ci in products:
                d = lax.dot_general(a_refs[ai][...].astype(BF16), b_refs[bi][...].astype(BF16), dims,
                                    preferred_element_type=F32)
                vals[ci] = d if vals[ci] is None else vals[ci] + d
            return vals

        if nk == 1:
            epilogue(partial_sums(), e_refs, o_refs)
        else:
            k = pl.program_id(k_axis)

            @pl.when(k == 0)
            def _():
                for acc in acc_refs:
                    acc[...] = jnp.zeros_like(acc)

            for acc, v in zip(acc_refs, partial_sums()):
                acc[...] += v

            @pl.when(k == nk - 1)
            def _():
                epilogue([acc[...] for acc in acc_refs], e_refs, o_refs)

    ops = list(a_ops) + list(b_ops) + list(extra_ops)
    return pl.pallas_call(
        body, name=name, grid=grid,
        in_specs=[s for _, s in ops], out_specs=[s for _, s in outs], out_shape=[o for o, _ in outs],
        scratch_shapes=[pltpu.VMEM(acc_shape, F32) for _ in range(n_acc if nk > 1 else 0)],
        compiler_params=_params(len(grid)),
    )(*[a for a, _ in ops])


def _store(accs, e_refs, o_refs):
    o_refs[0][...] = accs[0].astype(o_refs[0].dtype)


def linear(name, x, w, out_dtype, resid=None, next_gain=None):
    t, k = x.shape
    n = w.shape[1]
    tm = _tile(t, 512)
    tn = n if n <= 2048 else _tile(n, 1024, 128)
    tile = pl.BlockSpec((tm, tn), lambda j, i: (i, j))
    extra = [] if resid is None else [(resid, tile)]
    outs = [(S((t, n), out_dtype), tile)]
    if next_gain is not None:
        assert tn == n
        extra.append((next_gain, pl.BlockSpec((1, n), lambda j, i: (0, 0))))
        outs.append((S((t, n), BF16), tile))

    def epilogue(accs, e_refs, o_refs):
        y = accs[0] if resid is None else e_refs[0][...] + accs[0]
        o_refs[0][...] = y.astype(out_dtype)
        if next_gain is not None:
            o_refs[1][...] = (y * _rstd(y) * e_refs[-1][...]).astype(BF16)

    res = _mm(name, [(x, pl.BlockSpec((tm, k), lambda j, i: (i, 0)))], [(w, pl.BlockSpec((k, tn), lambda j, i: (0, j)))],
              [(0, 0, 0)], _NN, (n // tn, t // tm), None, outs, None, epilogue, extra)
    return res[0] if next_gain is None else res


def linear_nt(name, dy, w, out_dtype):
    t, n = dy.shape
    k = w.shape[0]
    tm = _tile(t, 512)
    tc = n if n <= 2048 else _tile(n, 1024, 128)
    return _mm(name, [(dy, pl.BlockSpec((tm, tc), lambda i, c: (i, c)))], [(w, pl.BlockSpec((k, tc), lambda i, c: (0, c)))],
               [(0, 0, 0)], _NT, (t // tm, n // tc), 1,
               [(S((t, k), out_dtype), pl.BlockSpec((tm, k), lambda i, c: (i, 0)))], (tm, k), _store)[0]


def wgrad(name, x, dy):
    t, k = x.shape
    n = dy.shape[1]
    tk = _tile(t, 512)
    tn = n if n <= 1024 else _tile(n, 1024, 128)
    return _mm(name, [(x, pl.BlockSpec((tk, k), lambda j, s: (s, 0)))], [(dy, pl.BlockSpec((tk, tn), lambda j, s: (s, j)))],
               [(0, 0, 0)], _TN, (n // tn, t // tk), 1,
               [(S((k, n), BF16), pl.BlockSpec((k, tn), lambda j, s: (0, j)))], (k, tn), _store)[0]


def _resident(shape, index_map):
    return pl.BlockSpec(shape, index_map, pipeline_mode=pl.Buffered(1))


def ffn_up(name, a, wg_all, wu_all, layer):
    t, d = a.shape
    f4 = wg_all.shape[2]
    tm = _tile(t, 512)
    w_spec = _resident((N_CHIPS, d, f4), lambda i: (0, layer, 0))
    h_spec = pl.BlockSpec((N_CHIPS, tm, f4), lambda i: (0, i, 0))

    def body(a_ref, wg_ref, wu_ref, zg_ref, zu_ref, z_ref):
        av = a_ref[...]
        for k in range(N_CHIPS):
            g = jnp.dot(av, wg_ref[k], preferred_element_type=F32)
            u = jnp.dot(av, wu_ref[k], preferred_element_type=F32)
            sg = _sigmoid(g)
            silu = g * sg
            zg_ref[k] = (u * (sg * (1.0 + g * (1.0 - sg)))).astype(BF16)
            zu_ref[k] = silu.astype(BF16)
            z_ref[k] = (silu * u).astype(BF16)

    return pl.pallas_call(
        body, name=name, grid=(t // tm,), in_specs=[pl.BlockSpec((tm, d), lambda i: (i, 0)), w_spec, w_spec],
        out_specs=[h_spec] * 3, out_shape=[S((N_CHIPS, t, f4), BF16)] * 3, compiler_params=_params(1))(a, wg_all, wu_all)


def ffn_down(name, z, wd_all, layer, resid, next_gain=None):
    _, t, f4 = z.shape
    d = wd_all.shape[2]
    tm = _tile(t, 512)
    row = pl.BlockSpec((tm, d), lambda i: (i, 0))
    normed = next_gain is not None

    def body(z_ref, wd_ref, r_ref, *refs):
        acc = r_ref[...]
        for k in range(N_CHIPS):
            acc = acc + jnp.dot(z_ref[k], wd_ref[k], preferred_element_type=F32)
        refs[-2 if normed else -1][...] = acc
        if normed:
            refs[-1][...] = (acc * _rstd(acc) * refs[0][...]).astype(BF16)

    res = pl.pallas_call(
        body, name=name, grid=(t // tm,),
        in_specs=[pl.BlockSpec((N_CHIPS, tm, f4), lambda i: (0, i, 0)), _resident((N_CHIPS, f4, d), lambda i: (0, layer, 0)), row]
        + ([pl.BlockSpec((1, d), lambda i: (0, 0))] if normed else []),
        out_specs=[row] * (2 if normed else 1), out_shape=[S((t, d), F32)] + ([S((t, d), BF16)] if normed else []),
        compiler_params=_params(1))(z, wd_all, resid, *([next_gain] if normed else []))
    return res if normed else res[0]


def ffn_bwd_hidden(name, dh, wd_all, layer, zg, zu, swap=()):
    t, d = dh.shape
    f4 = zg.shape[2]
    tm = _tile(t, 512)
    h_spec = pl.BlockSpec((N_CHIPS, tm, f4), lambda i: (0, i, 0))
    n = len(swap)

    def body(dh_ref, wd_ref, zg_ref, zu_ref, *refs):
        dg_ref, du_ref = refs[n:n + 2]
        if n:
            _ride_along(swap_copies(refs[:n], refs[n + 2:2 * n + 2], *refs[2 * n + 2:]), (pl.program_id(0),), (t // tm,))
        dhb = dh_ref[...].astype(BF16)
        for k in range(N_CHIPS):
            dz = lax.dot_general(dhb, wd_ref[k], _NT, preferred_element_type=F32)
            dg_ref[k] = (dz * zg_ref[k].astype(F32)).astype(BF16)
            du_ref[k] = (dz * zu_ref[k].astype(F32)).astype(BF16)

    outs = pl.pallas_call(
        body, name=name, grid=(t // tm,),
        in_specs=[pl.BlockSpec((tm, d), lambda i: (i, 0)), _resident((N_CHIPS, f4, d), lambda i: (0, layer, 0)), h_spec, h_spec]
        + [ANY] * n,
        out_specs=[h_spec] * 2 + [ANY] * n, out_shape=[S((N_CHIPS, t, f4), BF16)] * 2 + _swap_shapes(swap),
        scratch_shapes=[pltpu.SemaphoreType.DMA((n,)), pltpu.SemaphoreType.DMA((n,))] if n else [],
        compiler_params=_params(1))(dh, wd_all, zg, zu, *swap)
    return outs[0], outs[1], list(outs[2:])


def _norm_bwd_specs(tm, d):
    row = pl.BlockSpec((tm, d), lambda i: (i, 0))
    vec = pl.BlockSpec((1, d), lambda i: (0, 0))
    return [row, vec, row], [row, vec]


def _norm_bwd_tail(da, h_ref, g_ref, dhi_ref, dho_ref, dgain_ref):
    dx, dgain = _rms_bwd(h_ref[...], g_ref[...], da)
    dho_ref[...] = dhi_ref[...] + dx

    @pl.when(pl.program_id(0) == 0)
    def _():
        dgain_ref[...] = jnp.zeros_like(dgain_ref)

    dgain_ref[...] += dgain


def ffn_bwd_input(name, dg, du, wg_all, wu_all, layer, h, gain, dh_in):
    _, t, f4 = dg.shape
    d = h.shape[1]
    tm = _tile(t, 512)
    h_spec = pl.BlockSpec((N_CHIPS, tm, f4), lambda i: (0, i, 0))
    w_spec = _resident((N_CHIPS, d, f4), lambda i: (0, layer, 0))
    tail_in, tail_out = _norm_bwd_specs(tm, d)

    def body(dg_ref, du_ref, wg_ref, wu_ref, *tail):
        acc = jnp.zeros((tm, d), F32)
        for k in range(N_CHIPS):
            acc = acc + lax.dot_general(dg_ref[k], wg_ref[k], _NT, preferred_element_type=F32)
            acc = acc + lax.dot_general(du_ref[k], wu_ref[k], _NT, preferred_element_type=F32)
        _norm_bwd_tail(acc, *tail)

    return pl.pallas_call(
        body, name=name, grid=(t // tm,), in_specs=[h_spec, h_spec, w_spec, w_spec] + tail_in, out_specs=tail_out,
        out_shape=[S((t, d), F32), S((1, d), F32)], compiler_params=_params(1))(dg, du, wg_all, wu_all, h, gain, dh_in)


def ffn_wgrad_up(name, a, dy):
    t, d = a.shape
    f4 = dy.shape[2]
    tk = _tile(t, 512)
    nt = t // tk

    def body(a_ref, dy_ref, o_ref, acc):
        s = pl.program_id(0)

        @pl.when(s == 0)
        def _():
            acc[...] = jnp.zeros_like(acc)

        at = a_ref[...].T
        for k in range(N_CHIPS):
            acc[k] += jnp.dot(at, dy_ref[k], preferred_element_type=F32)

        @pl.when(s == nt - 1)
        def _():
            o_ref[...] = acc[...].astype(BF16)

    return pl.pallas_call(
        body, name=name, grid=(nt,),
        in_specs=[pl.BlockSpec((tk, d), lambda s: (s, 0)), pl.BlockSpec((N_CHIPS, tk, f4), lambda s: (0, s, 0))],
        out_specs=pl.BlockSpec((N_CHIPS, d, f4), lambda s: (0, 0, 0)), out_shape=S((N_CHIPS, d, f4), BF16),
        scratch_shapes=[pltpu.VMEM((N_CHIPS, d, f4), F32)], compiler_params=_params(1))(a, dy)


def ffn_wgrad_down(name, z, dh):
    _, t, f4 = z.shape
    d = dh.shape[1]
    tk = _tile(t, 512)
    nt = t // tk

    def body(z_ref, dh_ref, o_ref, acc):
        s = pl.program_id(0)

        @pl.when(s == 0)
        def _():
            acc[...] = jnp.zeros_like(acc)

        dhb = dh_ref[...].astype(BF16)
        for k in range(N_CHIPS):
            acc[k] += lax.dot_general(z_ref[k], dhb, _TN, preferred_element_type=F32)

        @pl.when(s == nt - 1)
        def _():
            o_ref[...] = acc[...].astype(BF16)

    return pl.pallas_call(
        body, name=name, grid=(nt,),
        in_specs=[pl.BlockSpec((N_CHIPS, tk, f4), lambda s: (0, s, 0)), pl.BlockSpec((tk, d), lambda s: (s, 0))],
        out_specs=pl.BlockSpec((N_CHIPS, f4, d), lambda s: (0, 0, 0)), out_shape=S((N_CHIPS, f4, d), BF16),
        scratch_shapes=[pltpu.VMEM((N_CHIPS, f4, d), F32)], compiler_params=_params(1))(z, dh)


def conv_in_proj(name, a, w, conv_w):
    t, d = a.shape
    tm = _tile(t, 256)
    keep = 8

    def body(a_ref, w_ref, cw_ref, bcx_ref, y_ref, u_ref):
        @pl.when(pl.program_id(0) == 0)
        def _():
            u_ref[0:keep, :] = jnp.zeros((keep, d), F32)

        av = a_ref[...]
        b, c, x = [jnp.dot(av, w_ref[:, j * d:(j + 1) * d], preferred_element_type=F32) for j in range(3)]
        for j, part in enumerate((b, c, x)):
            bcx_ref[j] = part.astype(bcx_ref.dtype)
        u_ref[keep:keep + tm, :] = c * x
        uc = (cw_ref[0:1, :] * u_ref[keep - 2:keep - 2 + tm, :] + cw_ref[1:2, :] * u_ref[keep - 1:keep - 1 + tm, :]
              + cw_ref[2:3, :] * u_ref[keep:keep + tm, :])
        y_ref[...] = (b * uc).astype(BF16)
        u_ref[0:keep, :] = u_ref[tm:tm + keep, :]

    return pl.pallas_call(
        body, name=name, grid=(t // tm,),
        in_specs=[pl.BlockSpec((tm, d), lambda i: (i, 0)), _resident((d, 3 * d), lambda i: (0, 0)), pl.BlockSpec((3, d), lambda i: (0, 0))],
        out_specs=[pl.BlockSpec((3, tm, d), lambda i: (0, i, 0)), pl.BlockSpec((tm, d), lambda i: (i, 0))],
        out_shape=[S((3, t, d), CONV_SAVED_DTYPE), S((t, d), BF16)], scratch_shapes=[pltpu.VMEM((tm + keep, d), F32)],
        compiler_params=_params(1))(a, w, conv_w)


def conv_in_bwd_input(name, dbcx, w, h, gain, dh_in):
    _, t, d = dbcx.shape
    tm = _tile(t, 512)
    tail_in, tail_out = _norm_bwd_specs(tm, d)

    def body(g_ref, w_ref, *tail):
        acc = jnp.zeros((tm, d), F32)
        for j in range(3):
            acc = acc + lax.dot_general(g_ref[j], w_ref[:, j * d:(j + 1) * d], _NT, preferred_element_type=F32)
        _norm_bwd_tail(acc, *tail)

    return pl.pallas_call(
        body, name=name, grid=(t // tm,),
        in_specs=[pl.BlockSpec((3, tm, d), lambda i: (0, i, 0)), _resident((d, 3 * d), lambda i: (0, 0))] + tail_in,
        out_specs=tail_out, out_shape=[S((t, d), F32), S((1, d), F32)], compiler_params=_params(1))(dbcx, w, h, gain, dh_in)


def linear_nt_norm_bwd(name, dy, w, h, gain, dh_in, parts=()):
    t, n = dy.shape
    k = w.shape[0]
    tm = _tile(t, 512)
    tail_in, tail_out = _norm_bwd_specs(tm, k)
    m = len(parts)

    def body(dy_ref, w_ref, h_ref, g_ref, dhi_ref, *refs):
        if m:
            _ride_along(scatter_ici_copies(refs[:m], refs[m + 2:2 * m + 2], *refs[2 * m + 2:]), (pl.program_id(0),), (t // tm,))
        da = lax.dot_general(dy_ref[...].astype(BF16), w_ref[...], _NT, preferred_element_type=F32)
        _norm_bwd_tail(da, h_ref, g_ref, dhi_ref, *refs[m:m + 2])

    outs = pl.pallas_call(
        body, name=name, grid=(t // tm,),
        in_specs=[pl.BlockSpec((tm, n), lambda i: (i, 0)), _resident((k, n), lambda i: (0, 0))] + tail_in + [ANY] * m,
        out_specs=tail_out + [ANY] * m, out_shape=[S((t, k), F32), S((1, k), F32)] + [S(p.shape, p.dtype) for p in parts],
        scratch_shapes=[pltpu.SemaphoreType.DMA((m, 3)), pltpu.SemaphoreType.DMA((m, 3))] if m else [],
        compiler_params=_params(1))(dy, w, h, gain, dh_in, *parts)
    return outs[0], outs[1], list(outs[2:])


def conv_in_wgrad(name, a, dbcx):
    t, d = a.shape
    tk = _tile(t, 512)
    nt = t // tk
    n4 = 3 * d // N_CHIPS

    def body(a_ref, g_ref, o_ref, acc):
        s = pl.program_id(0)

        @pl.when(s == 0)
        def _():
            acc[...] = jnp.zeros_like(acc)

        at = a_ref[...].T
        for j in range(3):
            acc[:, j * d:(j + 1) * d] += jnp.dot(at, g_ref[j], preferred_element_type=F32)

        @pl.when(s == nt - 1)
        def _():
            for k in range(N_CHIPS):
                o_ref[k] = acc[:, k * n4:(k + 1) * n4].astype(BF16)

    return pl.pallas_call(
        body, name=name, grid=(nt,),
        in_specs=[pl.BlockSpec((tk, d), lambda s: (s, 0)), pl.BlockSpec((3, tk, d), lambda s: (0, s, 0))],
        out_specs=pl.BlockSpec((N_CHIPS, d, n4), lambda s: (0, 0, 0)), out_shape=S((N_CHIPS, d, n4), BF16),
        scratch_shapes=[pltpu.VMEM((d, 3 * d), F32)], compiler_params=_params(1))(a, dbcx)


def _rstd(x):
    return lax.rsqrt(jnp.mean(x * x, axis=-1, keepdims=True) + RMS_EPS)


def _rms_bwd(x, g, dy):
    r = _rstd(x)
    xhat = x * r
    dgain = jnp.sum(dy * xhat, axis=0, keepdims=True)
    dxh = dy * g
    dx = r * (dxh - xhat * jnp.mean(dxh * xhat, axis=-1, keepdims=True))
    return dx, dgain


def rms_fwd(name, h, g):
    t, d = h.shape
    tr = _tile(t, 512)

    def body(h_ref, g_ref, a_ref):
        x = h_ref[...]
        a_ref[...] = (x * _rstd(x) * g_ref[...]).astype(BF16)

    return pl.pallas_call(
        body, name=name, grid=(t // tr,),
        in_specs=[pl.BlockSpec((tr, d), lambda i: (i, 0)), pl.BlockSpec((1, d), lambda i: (0, 0))],
        out_specs=pl.BlockSpec((tr, d), lambda i: (i, 0)), out_shape=S((t, d), BF16), compiler_params=_params(1))(h, g)


def ffn_down_loss(name, z, wd_all, layer, resid, gain, target):
    _, t, f4 = z.shape
    d = wd_all.shape[2]
    tm = _tile(t, 512)

    def body(z_ref, wd_ref, r_ref, g_ref, t_ref, dh_ref, dg_ref, loss_ref):
        x = r_ref[...]
        for k in range(N_CHIPS):
            x = x + jnp.dot(z_ref[k], wd_ref[k], preferred_element_type=F32)
        g = g_ref[...]
        r = _rstd(x)
        xhat = x * r
        err = xhat * g - t_ref[...]
        dy = err * (1.0 / d)
        dxh = dy * g
        dh_ref[...] = r * (dxh - xhat * jnp.mean(dxh * xhat, axis=-1, keepdims=True))

        @pl.when(pl.program_id(0) == 0)
        def _():
            dg_ref[...] = jnp.zeros_like(dg_ref)
            loss_ref[...] = jnp.zeros_like(loss_ref)

        dg_ref[...] += jnp.sum(dy * xhat, axis=0, keepdims=True)
        per_token = jnp.mean(err * err, axis=-1, keepdims=True)
        loss_ref[...] += 0.5 * jnp.sum(per_token, axis=0, keepdims=True)

    row = pl.BlockSpec((tm, d), lambda i: (i, 0))
    vec = pl.BlockSpec((1, d), lambda i: (0, 0))
    one = pl.BlockSpec((1, 1), lambda i: (0, 0))
    return pl.pallas_call(
        body, name=name, grid=(t // tm,),
        in_specs=[pl.BlockSpec((N_CHIPS, tm, f4), lambda i: (0, i, 0)), _resident((N_CHIPS, f4, d), lambda i: (0, layer, 0)), row, vec, row],
        out_specs=[row, vec, one], out_shape=[S((t, d), F32), S((1, d), F32), S((1, 1), F32)],
        compiler_params=_params(1))(z, wd_all, resid, gain, target)


def mla_in_proj(name, a, w, g_cq, g_ckv, cos, sin):
    t, d = a.shape
    n = w.shape[1]
    ql, kl = g_cq.shape[1], g_ckv.shape[1]
    tr = _tile(t, 512)

    def body(a_ref, w_ref, gq_ref, gk_ref, c_ref, s_ref, p_ref, cq_ref, ckv_ref, kr_ref):
        p_ref[...] = jnp.dot(a_ref[...], w_ref[...], preferred_element_type=F32)
        xq = p_ref[:, 0:ql]
        cq_ref[...] = (xq * _rstd(xq) * gq_ref[...]).astype(BF16)
        xk = p_ref[:, ql:ql + kl]
        ckv_ref[...] = (xk * _rstd(xk) * gk_ref[...]).astype(BF16)
        k1 = p_ref[:, ql + kl:ql + kl + HALF]
        k2 = p_ref[:, ql + kl + HALF:ql + kl + ROPE]
        c = c_ref[...]
        s = s_ref[...]
        kr_ref[:, 0:HALF] = k1 * c - k2 * s
        kr_ref[:, HALF:ROPE] = k1 * s + k2 * c

    def row(w):
        return pl.BlockSpec((tr, w), lambda i: (i, 0))

    def vec(w):
        return pl.BlockSpec((1, w), lambda i: (0, 0))

    return pl.pallas_call(
        body, name=name, grid=(t // tr,),
        in_specs=[row(d), _resident((d, n), lambda i: (0, 0)), vec(ql), vec(kl), row(HALF), row(HALF)],
        out_specs=[row(n), row(ql), row(kl), row(ROPE)],
        out_shape=[S((t, n), F32), S((t, ql), BF16), S((t, kl), BF16), S((t, ROPE), F32)],
        compiler_params=_params(1))(a, w, g_cq, g_ckv, cos, sin)


def mla_mid_bwd(name, proj, g_cq, g_ckv, dcq, dckv, dkr, cos, sin):
    t, n = proj.shape
    ql, kl = g_cq.shape[1], g_ckv.shape[1]
    tr = _tile(t, 512)

    def body(p_ref, gq_ref, gk_ref, dcq_ref, dckv_ref, dkr_ref, c_ref, s_ref, dp_ref, dgq_ref, dgk_ref):
        dxq, dgq = _rms_bwd(p_ref[:, 0:ql], gq_ref[...], dcq_ref[...])
        dp_ref[:, 0:ql] = dxq.astype(BF16)
        dxk, dgk = _rms_bwd(p_ref[:, ql:ql + kl], gk_ref[...], dckv_ref[...])
        dp_ref[:, ql:ql + kl] = dxk.astype(BF16)
        d1 = dkr_ref[:, 0:HALF]
        d2 = dkr_ref[:, HALF:ROPE]
        c = c_ref[...]
        s = s_ref[...]
        dp_ref[:, ql + kl:ql + kl + HALF] = (d1 * c + d2 * s).astype(BF16)
        dp_ref[:, ql + kl + HALF:ql + kl + ROPE] = (d2 * c - d1 * s).astype(BF16)

        @pl.when(pl.program_id(0) == 0)
        def _():
            dgq_ref[...] = jnp.zeros_like(dgq_ref)
            dgk_ref[...] = jnp.zeros_like(dgk_ref)

        dgq_ref[...] += dgq
        dgk_ref[...] += dgk

    def row(w):
        return pl.BlockSpec((tr, w), lambda i: (i, 0))

    def vec(w):
        return pl.BlockSpec((1, w), lambda i: (0, 0))

    return pl.pallas_call(
        body, name=name, grid=(t // tr,),
        in_specs=[row(n), vec(ql), vec(kl), row(ql), row(kl), row(ROPE), row(HALF), row(HALF)],
        out_specs=[row(n), vec(ql), vec(kl)], out_shape=[S((t, n), BF16), S((1, ql), F32), S((1, kl), F32)],
        compiler_params=_params(1))(proj, g_cq, g_ckv, dcq, dckv, dkr, cos, sin)


def qkv_heads(name, q, kv, kr, cos, sin, shards=()):
    t = q.shape[0]
    tr = _tile(t, 256)
    n = len(shards)

    def body(q_ref, kv_ref, kr_ref, c_ref, s_ref, *refs):
        src = refs[:n]
        qo_ref, ko_ref, vo_ref = refs[n:n + 3]
        if n:
            _ride_along(gather_ici_copies(src, refs[n + 3:2 * n + 3], *refs[2 * n + 3:]), (pl.program_id(0),), (t // tr,))
        c = c_ref[...]
        s = s_ref[...]
        krb = kr_ref[...].astype(BF16)
        for h in range(N_HEADS):
            q0 = h * QK
            qo_ref[h, :, 0:NOPE] = q_ref[:, q0:q0 + NOPE].astype(BF16)
            q1 = q_ref[:, q0 + NOPE:q0 + NOPE + HALF]
            q2 = q_ref[:, q0 + NOPE + HALF:q0 + QK]
            qo_ref[h, :, NOPE:NOPE + HALF] = (q1 * c - q2 * s).astype(BF16)
            qo_ref[h, :, NOPE + HALF:QK] = (q1 * s + q2 * c).astype(BF16)
            k0 = h * (NOPE + VDIM)
            ko_ref[h, :, 0:NOPE] = kv_ref[:, k0:k0 + NOPE]
            ko_ref[h, :, NOPE:QK] = krb
            vo_ref[h] = kv_ref[:, k0 + NOPE:k0 + NOPE + VDIM]

    def row(w):
        return pl.BlockSpec((tr, w), lambda i: (i, 0))

    def heads(w):
        return pl.BlockSpec((N_HEADS, tr, w), lambda i: (0, i, 0))

    outs = pl.pallas_call(
        body, name=name, grid=(t // tr,),
        in_specs=[row(N_HEADS * QK), row(N_HEADS * (NOPE + VDIM)), row(ROPE), row(HALF), row(HALF)] + [ANY] * n,
        out_specs=[heads(QK), heads(QK), heads(VDIM)] + [ANY] * n,
        out_shape=[S((N_HEADS, t, QK), BF16), S((N_HEADS, t, QK), BF16), S((N_HEADS, t, VDIM), BF16)]
        + [S((N_CHIPS,) + s.shape, s.dtype) for s in shards],
        scratch_shapes=[pltpu.SemaphoreType.DMA((n, 3)), pltpu.SemaphoreType.DMA((n, 3))] if n else [],
        compiler_params=_params(1))(q, kv, kr, cos, sin, *shards)
    return outs[0], outs[1], outs[2], list(outs[3:])


def qkv_heads_bwd(name, dq_h, dk_h, dv_h, cos, sin, halves=(), targets=(), where=()):
    t = dq_h.shape[1]
    tr = _tile(t, 256)
    n, nt = len(halves), len(targets)

    def body(dq_ref, dk_ref, dv_ref, c_ref, s_ref, *refs):
        q_ref, kv_ref, kr_ref = refs[n:n + 3]
        if n:
            _ride_along(join_copies(refs[:n], refs[n + 3:n + 3 + nt], where, *refs[n + 3 + nt:]), (pl.program_id(0),), (t // tr,))
        c = c_ref[...]
        s = s_ref[...]
        dkr = jnp.zeros((tr, ROPE), F32)
        for h in range(N_HEADS):
            q0 = h * QK
            q_ref[:, q0:q0 + NOPE] = dq_ref[h, :, 0:NOPE].astype(BF16)
            d1 = dq_ref[h, :, NOPE:NOPE + HALF]
            d2 = dq_ref[h, :, NOPE + HALF:QK]
            q_ref[:, q0 + NOPE:q0 + NOPE + HALF] = (d1 * c + d2 * s).astype(BF16)
            q_ref[:, q0 + NOPE + HALF:q0 + QK] = (d2 * c - d1 * s).astype(BF16)
            k0 = h * (NOPE + VDIM)
            kv_ref[:, k0:k0 + NOPE] = dk_ref[h, :, 0:NOPE].astype(BF16)
            kv_ref[:, k0 + NOPE:k0 + NOPE + VDIM] = dv_ref[h].astype(BF16)
            dkr = dkr + dk_ref[h, :, NOPE:QK]
        kr_ref[...] = dkr

    def row(w):
        return pl.BlockSpec((tr, w), lambda i: (i, 0))

    def heads(w):
        return pl.BlockSpec((N_HEADS, tr, w), lambda i: (0, i, 0))

    outs = pl.pallas_call(
        body, name=name, grid=(t // tr,),
        in_specs=[heads(QK), heads(QK), heads(VDIM), row(HALF), row(HALF)] + [ANY] * n,
        out_specs=[row(N_HEADS * QK), row(N_HEADS * (NOPE + VDIM)), row(ROPE)] + [ANY] * nt,
        out_shape=[S((t, N_HEADS * QK), BF16), S((t, N_HEADS * (NOPE + VDIM)), BF16), S((t, ROPE), F32)]
        + [S(tg, F32) for tg in targets],
        scratch_shapes=[pltpu.SemaphoreType.DMA((n,)), pltpu.SemaphoreType.DMA((n,))] if n else [],
        compiler_params=_params(1))(dq_h, dk_h, dv_h, cos, sin, *halves)
    return outs[0], outs[1], outs[2], _fill_own_halves(outs[3:], halves, where)


def _chunk_mask_t(q_start, k_start, bq, bk):
    kc = (k_start + lax.broadcasted_iota(jnp.int32, (bk, bq), 0)) // CHUNK
    qc = (q_start + lax.broadcasted_iota(jnp.int32, (bk, bq), 1)) // CHUNK
    return kc <= qc


def attention_fwd(name, q, k, v, shards=()):
    nh, t, _ = q.shape
    blk = ATT_BLOCK
    nq = t // blk
    n = len(shards)

    def body(q_ref, k_ref, v_ref, *refs):
        src = refs[:n]
        o_ref, lse_ref = refs[n:n + 2]
        dst = refs[n + 2:2 * n + 2]
        m_ref, l_ref, acc_ref, s_buf, p_buf, alpha_buf, bias_ref = refs[2 * n + 2:2 * n + 9]
        i = pl.program_id(1)
        if n:
            send_sems, recv_sems = refs[2 * n + 9:]
            _ride_along(gather_ici_copies(src, dst, send_sems, recv_sems), (pl.program_id(0), i), (nh, nq))

        @pl.when((pl.program_id(0) == 0) & (i == 0))
        def _():
            bias_ref[...] = jnp.where(_chunk_mask_t(0, 0, blk, blk), 0.0, MASK_VALUE)

        m_ref[...] = jnp.full_like(m_ref, MASK_VALUE)
        l_ref[...] = jnp.zeros_like(l_ref)
        acc_ref[...] = jnp.zeros_like(acc_ref)

        def rows(b):
            return pl.ds(pl.multiple_of(b * blk, blk), blk)

        def scores(b, slot):
            s_buf[slot] = lax.dot_general(k_ref[rows(b), :], q_ref[...], _NT, preferred_element_type=F32)

        def softmax(slot, diagonal):
            s = s_buf[slot]
            if diagonal:
                s = s + bias_ref[...]
            m_old = m_ref[...]
            m_new = jnp.maximum(m_old, jnp.max(s, axis=0, keepdims=True))
            p = jnp.exp2((s - m_new) * SCORE_SCALE_LOG2)
            alpha = jnp.exp2((m_old - m_new) * SCORE_SCALE_LOG2)
            l_ref[...] = alpha * l_ref[...] + jnp.sum(p, axis=0, keepdims=True)
            m_ref[...] = m_new
            alpha_buf[slot] = alpha
            p_buf[slot] = p.astype(BF16)

        def values(b, slot):
            pv = lax.dot_general(v_ref[rows(b), :], p_buf[slot], _TN, preferred_element_type=F32)
            acc_ref[...] = alpha_buf[slot] * acc_ref[...] + pv

        def step(t, slot):
            values(t - 2, slot)
            softmax(1 - slot, False)
            scores(t, slot)

        scores(0, 0)

        @pl.when(i == 0)
        def _():
            softmax(0, True)
            values(0, 0)

        @pl.when(i > 0)
        def _():
            scores(1, 1)
            softmax(0, False)
            steady = i - 1

            def pair(u, carry):
                step(2 + 2 * u, 0)
                step(3 + 2 * u, 1)
                return carry

            lax.fori_loop(0, steady // 2, pair, 0)

            @pl.when(steady % 2 == 1)
            def _():
                step(i, 0)

            last = i % 2
            softmax(last, True)
            values(i - 1, 1 - last)
            values(i, last)

        l = l_ref[...]
        o_ref[...] = (acc_ref[...] / l).T
        lse_ref[...] = m_ref[...] * SCORE_SCALE + jnp.log(l)

    outs = pl.pallas_call(
        body, name=name, grid=(nh, nq),
        in_specs=[pl.BlockSpec((None, blk, QK), lambda h, i: (h, i, 0)), pl.BlockSpec((None, t, QK), lambda h, i: (h, 0, 0)),
                  pl.BlockSpec((None, t, VDIM), lambda h, i: (h, 0, 0))] + [ANY] * n,
        out_specs=[pl.BlockSpec((blk, VDIM), lambda h, i: (i, h)),
                   pl.BlockSpec((None, None, 1, blk), lambda h, i: (h, i, 0, 0))] + [ANY] * n,
        out_shape=[S((t, nh * VDIM), F32), S((nh, nq, 1, blk), F32)] + [S((N_CHIPS,) + s.shape, s.dtype) for s in shards],
        scratch_shapes=[pltpu.VMEM((1, blk), F32), pltpu.VMEM((1, blk), F32), pltpu.VMEM((VDIM, blk), F32),
                        pltpu.VMEM((2, blk, blk), F32), pltpu.VMEM((2, blk, blk), BF16), pltpu.VMEM((2, 1, blk), F32),
                        pltpu.VMEM((blk, blk), F32)]
        + ([pltpu.SemaphoreType.DMA((n, 3)), pltpu.SemaphoreType.DMA((n, 3))] if n else []),
        compiler_params=_params(2))(q, k, v, *shards)
    return outs[0], outs[1], list(outs[2:])


def attention_out_bwd(name, dh, w_o, o, swap=()):
    t, d = dh.shape
    n = w_o.shape[0]
    blk = ATT_BLOCK
    m = len(swap)

    def body(dh_ref, w_ref, o_ref, *refs):
        do_ref, d_ref = refs[m:m + 2]
        if m:
            _ride_along(swap_copies(refs[:m], refs[m + 2:2 * m + 2], *refs[2 * m + 2:]), (pl.program_id(0),), (t // blk,))
        do_ref[...] = lax.dot_general(dh_ref[...].astype(BF16), w_ref[...], _NT, preferred_element_type=F32)
        for h in range(N_HEADS):
            cols = slice(h * VDIM, (h + 1) * VDIM)
            d_ref[h] = jnp.sum((do_ref[:, cols] * o_ref[:, cols]).T, axis=0, keepdims=True)

    tile = pl.BlockSpec((blk, n), lambda i: (i, 0))
    outs = pl.pallas_call(
        body, name=name, grid=(t // blk,),
        in_specs=[pl.BlockSpec((blk, d), lambda i: (i, 0)), _resident((n, d), lambda i: (0, 0)), tile] + [ANY] * m,
        out_specs=[tile, pl.BlockSpec((N_HEADS, None, 1, blk), lambda i: (0, i, 0, 0))] + [ANY] * m,
        out_shape=[S((t, n), F32), S((N_HEADS, t // blk, 1, blk), F32)] + _swap_shapes(swap),
        scratch_shapes=[pltpu.SemaphoreType.DMA((m,)), pltpu.SemaphoreType.DMA((m,))] if m else [],
        compiler_params=_params(1))(dh, w_o, o, *swap)
    return outs[0], outs[1], list(outs[2:])


def attention_bwd(name, q, k, v, do, lse, delta, parts=()):
    nh, t, _ = q.shape
    blk = ATT_BLOCK
    nq = t // blk
    n_pairs = nq * (nq + 1) // 2
    n = len(parts)
    scale = SCORE_SCALE

    def body(q_ref, k_ref, v_ref, do_ref, lse_ref, dl_ref, *refs):
        src = refs[:n]
        dq_out, dk_out, dv_out = refs[n:n + 3]
        dst = refs[n + 3:2 * n + 3]
        s_buf, dp_buf, p_buf, ds_buf, bias_ref, dq_ref, dk_ref, dv_ref = refs[2 * n + 3:2 * n + 11]
        if n:
            send_sems, recv_sems = refs[2 * n + 11:]
            _ride_along(scatter_ici_copies(src, dst, send_sems, recv_sems), (pl.program_id(0),), (nh,))

        @pl.when(pl.program_id(0) == 0)
        def _():
            bias_ref[...] = jnp.where(_chunk_mask_t(0, 0, blk, blk), 0.0, MASK_VALUE)

        dq_ref[...] = jnp.zeros_like(dq_ref)
        dk_ref[...] = jnp.zeros_like(dk_ref)
        dv_ref[...] = jnp.zeros_like(dv_ref)

        def rows(x):
            return pl.ds(pl.multiple_of(x * blk, blk), blk)

        def after(jb):
            j, b = jb
            wrap = b == nq - 1 - j
            return jnp.where(wrap, j + 1, j), jnp.where(wrap, 0, b + 1)

        def products(jb, slot):
            j, b = jb
            s_buf[slot] = lax.dot_general(k_ref[rows(j), :], q_ref[rows(j + b), :], _NT, preferred_element_type=F32)
            dp_buf[slot] = lax.dot_general(v_ref[rows(j), :], do_ref[rows(j + b), :].astype(BF16), _NT, preferred_element_type=F32)

        def softmax_bwd(jb, slot):
            j, b = jb
            s = s_buf[slot] + bias_ref[...] * (b == 0).astype(F32)
            p = jnp.exp2(s * SCORE_SCALE_LOG2 - lse_ref[j + b] * LOG2_E)
            p_buf[slot] = p.astype(BF16)
            ds_buf[slot] = (p * (dp_buf[slot] - dl_ref[j + b]) * scale).astype(BF16)

        def gradients(jb, slot):
            j, b = jb
            dv_ref[rows(j), :] += jnp.dot(p_buf[slot], do_ref[rows(j + b), :].astype(BF16), preferred_element_type=F32)
            dk_ref[rows(j), :] += jnp.dot(ds_buf[slot], q_ref[rows(j + b), :], preferred_element_type=F32)
            dq_ref[rows(j + b), :] += lax.dot_general(ds_buf[slot], k_ref[rows(j), :], _TN, preferred_element_type=F32)

        def step(state, slot):
            third, second, first = state
            gradients(third, slot)
            softmax_bwd(second, 1 - slot)
            products(first, slot)
            return second, first, after(first)

        zero = jnp.int32(0)
        pair0 = (zero, zero)
        products(pair0, 0)
        if n_pairs == 1:
            softmax_bwd(pair0, 0)
            gradients(pair0, 0)
        else:
            pair1 = after(pair0)
            products(pair1, 1)
            softmax_bwd(pair0, 0)
            steady = n_pairs - 2
            state = lax.fori_loop(0, steady // 2, lambda u, st: step(step(st, 0), 1), (pair0, pair1, after(pair1)))
            if steady % 2:
                state = step(state, 0)
            before_last, last_pair, _ = state
            last = (n_pairs - 1) % 2
            softmax_bwd(last_pair, last)
            gradients(before_last, 1 - last)
            gradients(last_pair, last)
        dq_out[...] = dq_ref[...].astype(BF16)
        dk_out[...] = dk_ref[...].astype(BF16)
        dv_out[...] = dv_ref[...].astype(BF16)

    head = lambda w: pl.BlockSpec((None, t, w), lambda h: (h, 0, 0))
    stats = pl.BlockSpec((None, nq, 1, blk), lambda h: (h, 0, 0, 0))
    outs = pl.pallas_call(
        body, name=name, grid=(nh,),
        in_specs=[head(QK), head(QK), head(VDIM), pl.BlockSpec((t, VDIM), lambda h: (0, h)), stats, stats] + [ANY] * n,
        out_specs=[head(QK), head(QK), head(VDIM)] + [ANY] * n,
        out_shape=[S((nh, t, QK), BF16), S((nh, t, QK), BF16), S((nh, t, VDIM), BF16)] + [S(p.shape, p.dtype) for p in parts],
        scratch_shapes=[pltpu.VMEM((2, blk, blk), F32), pltpu.VMEM((2, blk, blk), F32), pltpu.VMEM((2, blk, blk), BF16),
                        pltpu.VMEM((2, blk, blk), BF16), pltpu.VMEM((blk, blk), F32),
                        pltpu.VMEM((t, QK), F32), pltpu.VMEM((t, QK), F32), pltpu.VMEM((t, VDIM), F32)]
        + ([pltpu.SemaphoreType.DMA((n, 3)), pltpu.SemaphoreType.DMA((n, 3))] if n else []),
        compiler_params=_params(1, VMEM_LIMIT_WHOLE_HEAD))(q, k, v, do, lse, delta, *parts)
    return outs[0], outs[1], outs[2], list(outs[3:])


def _shift_down(u, s):
    rows = lax.broadcasted_iota(jnp.int32, u.shape, 0)
    return jnp.where(rows >= s, pltpu.roll(u, s, 0), 0.0)


def _shift_up(u, s):
    n = u.shape[0]
    rows = lax.broadcasted_iota(jnp.int32, u.shape, 0)
    return jnp.where(rows < n - s, pltpu.roll(u, n - s, 0), 0.0)


def _conv_specs(t, d, lanes):
    slab = lambda part: pl.BlockSpec((None, t, lanes), lambda j, part=part: (part, 0, j))
    return slab, pl.BlockSpec((3, lanes), lambda j: (0, j)), pl.BlockSpec((t, lanes), lambda j: (0, j))


def conv_bwd(name, bcx, w, dy):
    _, t, d = bcx.shape
    lanes = _tile(d, 128, 128)
    slab, w_spec, col = _conv_specs(t, d, lanes)

    def body(b_ref, c_ref, x_ref, w_ref, dy_ref, d_ref, dw_ref):
        c = c_ref[...].astype(F32)
        x = x_ref[...].astype(F32)
        dyv = dy_ref[...]
        u = c * x
        u1 = _shift_down(u, 1)
        u2 = _shift_down(u, 2)
        w0, w1, w2 = w_ref[0:1, :], w_ref[1:2, :], w_ref[2:3, :]
        d_ref[0] = (dyv * (w0 * u2 + w1 * u1 + w2 * u)).astype(BF16)
        duc = dyv * b_ref[...].astype(F32)
        dw_ref[0:1, :] = jnp.sum(duc * u2, axis=0, keepdims=True)
        dw_ref[1:2, :] = jnp.sum(duc * u1, axis=0, keepdims=True)
        dw_ref[2:3, :] = jnp.sum(duc * u, axis=0, keepdims=True)
        du = w2 * duc + w1 * _shift_up(duc, 1) + w0 * _shift_up(duc, 2)
        d_ref[1] = (du * x).astype(BF16)
        d_ref[2] = (du * c).astype(BF16)

    return pl.pallas_call(
        body, name=name, grid=(d // lanes,), in_specs=[slab(0), slab(1), slab(2), w_spec, col],
        out_specs=[pl.BlockSpec((3, t, lanes), lambda j: (0, 0, j)), w_spec], out_shape=[S((3, t, d), BF16), S((3, d), F32)],
        compiler_params=_params(1))(bcx, bcx, bcx, w, dy)


def _adamw_update(w, g, m, v):
    m_new = ADAM_B1 * m + (1.0 - ADAM_B1) * g
    v_new = ADAM_B2 * v + (1.0 - ADAM_B2) * (g * g)
    m_hat = m_new / (1.0 - ADAM_B1 ** ADAM_STEP)
    v_hat = v_new / (1.0 - ADAM_B2 ** ADAM_STEP)
    return -ADAM_LR * (m_hat / (jnp.sqrt(v_hat) + ADAM_EPS) + ADAM_WD * w), m_new, v_new


def adamw(name, w, g, m, v):
    r, c = w.shape
    tr = _tile(r, 512)

    def body(w_ref, g_ref, m_ref, v_ref, d_ref, mo_ref, vo_ref):
        d_ref[...], mo_ref[...], vo_ref[...] = _adamw_update(w_ref[...], g_ref[...], m_ref[...], v_ref[...])

    blk = pl.BlockSpec((tr, c), lambda i: (i, 0))
    return pl.pallas_call(
        body, name=name, grid=(r // tr,), in_specs=[blk] * 4, out_specs=[blk] * 3, out_shape=[S((r, c), F32)] * 3,
        compiler_params=_params(1))(w, g, m, v)


def adamw_swapped(name, wt, g, mt, vt):
    nl, c, r = wt.shape
    tr = _tile(r, 512, 128)
    nr = r // tr

    def body(w_ref, g_ref, m_ref, v_ref, go_ref, d_ref, mo_ref, vo_ref):
        gt = g_ref[...].T
        go_ref[...] = gt
        d_ref[...], mo_ref[...], vo_ref[...] = _adamw_update(w_ref[...], gt, m_ref[...], v_ref[...])

    swapped = pl.BlockSpec((None, c, tr), lambda l, i: (l, 0, i))
    return pl.pallas_call(
        body, name=name, grid=(nl, nr),
        in_specs=[swapped, pl.BlockSpec((tr, c), lambda l, i: (l * nr + i, 0)), swapped, swapped],
        out_specs=[swapped] * 4, out_shape=[S((nl, c, r), F32)] * 4, compiler_params=_params(2))(wt, g, mt, vt)


def _place():
    x, y, c = lax.axis_index("x"), lax.axis_index("y"), lax.axis_index("c")
    other_chips = [(1 - x, y), (x, 1 - y), (1 - x, 1 - y)]
    return x, y, c, other_chips


def _half(c, rows):
    return pl.ds(pl.multiple_of(c * (rows // 2), 16), rows // 2)


def gather_weight_shards(shards, small=()):
    n, ns = len(shards), len(small)

    def body(*refs):
        src = refs[:n]
        small_src = refs[n:n + ns]
        dst = refs[n + ns:2 * n + ns]
        small_dst = refs[2 * n + ns:2 * (n + ns)]
        send_sems, recv_sems, small_send, small_recv = refs[2 * (n + ns):]
        x, y, c, chips = _place()
        me = 2 * x + y
        sibling = (x, y, 1 - c)
        small_pairs = []
        for i in range(ns):
            for j, (px, py) in enumerate(chips):
                def whole(slot):
                    return pltpu.make_async_remote_copy(
                        src_ref=small_src[i], dst_ref=small_dst[i].at[slot], send_sem=small_send.at[i, j],
                        recv_sem=small_recv.at[i, j], device_id=(px, py, c), device_id_type=MESH)
                small_pairs.append((whole(me), whole(2 * px + py)))
        for outgoing, _ in small_pairs:
            outgoing.start()

        def copy(i, slot, half_of, sem, to, from_input=False):
            rows = _half(half_of, src[i].shape[0])
            return pltpu.make_async_remote_copy(
                src_ref=src[i].at[rows] if from_input else dst[i].at[slot, rows], dst_ref=dst[i].at[slot, rows],
                send_sem=send_sems.at[i, sem], recv_sem=recv_sems.at[i, sem], device_id=to, device_id_type=MESH)

        sent = []
        for i in range(n):
            for j, chip in enumerate(chips):
                sent.append(copy(i, me, c, j, (*chip, c), from_input=True))
                sent[-1].start()
        for i in range(n):
            for j, (px, py) in enumerate(chips):
                copy(i, 2 * px + py, c, j, sibling).wait_recv()
                sent.append(copy(i, 2 * px + py, c, 3 + j, sibling))
                sent[-1].start()
        for i in range(n):
            for j, (px, py) in enumerate(chips):
                copy(i, 2 * px + py, 1 - c, 3 + j, sibling).wait_recv()
        for cp in sent:
            cp.wait_send()
        for _, incoming in small_pairs:
            incoming.wait_recv()
        for outgoing, _ in small_pairs:
            outgoing.wait_send()

    outs = pl.pallas_call(
        body, name="gather_weight_shards", in_specs=[ANY] * (n + ns), out_specs=[ANY] * (n + ns),
        out_shape=[S((N_CHIPS,) + s.shape, s.dtype) for s in list(shards) + list(small)],
        scratch_shapes=[pltpu.SemaphoreType.DMA((n, 6)), pltpu.SemaphoreType.DMA((n, 6)),
                        pltpu.SemaphoreType.DMA((max(ns, 1), 3)), pltpu.SemaphoreType.DMA((max(ns, 1), 3))],
    )(*shards, *small)
    filled = _fill_own_slot(outs, [s[None] for s in list(shards) + list(small)])
    return filled[:n], filled[n:]


def gather_ici_copies(src, dst, send_sems, recv_sems):
    x, y, c, chips = _place()
    me = 2 * x + y
    pairs = []
    for i in range(len(src)):
        rows = _half(c, src[i].shape[0])
        for j, (px, py) in enumerate(chips):
            def copy(slot):
                return pltpu.make_async_remote_copy(
                    src_ref=src[i].at[rows], dst_ref=dst[i].at[slot, rows], send_sem=send_sems.at[i, j],
                    recv_sem=recv_sems.at[i, j], device_id=(px, py, c), device_id_type=MESH)
            pairs.append((copy(me), copy(2 * px + py)))
    return pairs


def scatter_ici_copies(src, dst, send_sems, recv_sems):
    x, y, c, chips = _place()
    me = 2 * x + y
    pairs = []
    for i in range(len(src)):
        for j, (px, py) in enumerate(chips):
            def copy(from_slot, to_slot):
                return pltpu.make_async_remote_copy(
                    src_ref=src[i].at[from_slot], dst_ref=dst[i].at[to_slot], send_sem=send_sems.at[i, j],
                    recv_sem=recv_sems.at[i, j], device_id=(px, py, c), device_id_type=MESH)
            pairs.append((copy(2 * px + py, me), copy(me, 2 * px + py)))
    return pairs


def _ride_along(pairs, grid_ids, grid_sizes):
    first = grid_ids[0] == 0
    last = grid_ids[0] == grid_sizes[0] - 1
    for g, size in zip(grid_ids[1:], grid_sizes[1:]):
        first = first & (g == 0)
        last = last & (g == size - 1)

    @pl.when(first)
    def _():
        for outgoing, _ in pairs:
            outgoing.start()

    @pl.when(last)
    def _():
        for _, incoming in pairs:
            incoming.wait_recv()
        for outgoing, _ in pairs:
            outgoing.wait_send()


def _fill_own_slot(gathered, own):
    me = 2 * lax.axis_index("x") + lax.axis_index("y")
    return [lax.dynamic_update_slice(g, o, (me,) + (0,) * (g.ndim - 1)) for g, o in zip(gathered, own)]


def forward_copies(src, dst, send_sems, recv_sems):
    x, y, c, chips = _place()
    pairs = []
    for i in range(len(src)):
        for j, (px, py) in enumerate(chips):
            def copy(half_of):
                rows = _half(half_of, src[i].shape[1])
                return pltpu.make_async_remote_copy(
                    src_ref=src[i].at[2 * px + py, rows], dst_ref=dst[i].at[2 * px + py, rows], send_sem=send_sems.at[i, j],
                    recv_sem=recv_sems.at[i, j], device_id=(x, y, 1 - c), device_id_type=MESH)
            pairs.append((copy(c), copy(1 - c)))
    return pairs


def attention_out_proj(name, attn, w_o, resid, next_gain, arriving):
    t, kdim = attn.shape
    n = w_o.shape[1]
    tm = _tile(t, 512)
    m = len(arriving)

    def body(x_ref, w_ref, r_ref, g_ref, *refs):
        h_ref, a_ref = refs[m:m + 2]
        _ride_along(forward_copies(refs[:m], refs[m + 2:2 * m + 2], *refs[2 * m + 2:]), (pl.program_id(0),), (t // tm,))
        y = r_ref[...] + jnp.dot(x_ref[...].astype(BF16), w_ref[...], preferred_element_type=F32)
        h_ref[...] = y
        a_ref[...] = (y * _rstd(y) * g_ref[...]).astype(BF16)

    row = pl.BlockSpec((tm, n), lambda i: (i, 0))
    outs = pl.pallas_call(
        body, name=name, grid=(t // tm,),
        in_specs=[pl.BlockSpec((tm, kdim), lambda i: (i, 0)), _resident((kdim, n), lambda i: (0, 0)), row,
                  pl.BlockSpec((1, n), lambda i: (0, 0))] + [ANY] * m,
        out_specs=[row, row] + [ANY] * m, out_shape=[S((t, n), F32), S((t, n), BF16)] + [S(g.shape, g.dtype) for g in arriving],
        input_output_aliases={4 + i: 2 + i for i in range(m)},
        scratch_shapes=[pltpu.SemaphoreType.DMA((m, 3)), pltpu.SemaphoreType.DMA((m, 3))],
        compiler_params=_params(1))(attn, w_o, resid, next_gain, *arriving)
    return outs[0], outs[1], list(outs[2:])


def swap_copies(src, dst, send_sems, recv_sems):
    x, y, c, _ = _place()
    pairs = []
    for i in range(len(src)):
        cp = pltpu.make_async_remote_copy(
            src_ref=src[i].at[:, _half(1 - c, src[i].shape[1]), :], dst_ref=dst[i], send_sem=send_sems.at[i],
            recv_sem=recv_sems.at[i], device_id=(x, y, 1 - c), device_id_type=MESH)
        pairs.append((cp, cp))
    return pairs


def _swap_shapes(grads):
    return [S((g.shape[0], g.shape[1] // 2, g.shape[2]), g.dtype) for g in grads]


def sibling_swap_halves(name, grads):
    n = len(grads)

    def body(*refs):
        pairs = swap_copies(refs[:n], refs[n:2 * n], *refs[2 * n:])
        for outgoing, _ in pairs:
            outgoing.start()
        for _, incoming in pairs:
            incoming.wait_recv()
        for outgoing, _ in pairs:
            outgoing.wait_send()

    return pl.pallas_call(
        body, name=name, in_specs=[ANY] * n, out_specs=[ANY] * n, out_shape=_swap_shapes(grads),
        scratch_shapes=[pltpu.SemaphoreType.DMA((n,)), pltpu.SemaphoreType.DMA((n,))],
    )(*grads)


def add_halves(name, g, rx):
    _, r, cdim = g.shape
    r2 = r // 2
    tr = _tile(r2, 512, 16)
    nb = r2 // tr

    def body(lo_ref, hi_ref, rx_ref, o_ref):
        mine = jnp.where(lax.axis_index("c") == 0, lo_ref[...], hi_ref[...])
        o_ref[...] = (mine.astype(F32) + rx_ref[...].astype(F32)).astype(BF16)

    half = pl.BlockSpec((None, tr, cdim), lambda k, i: (k, i, 0))
    return pl.pallas_call(
        body, name=name, grid=(N_CHIPS, nb),
        in_specs=[half, pl.BlockSpec((None, tr, cdim), lambda k, i: (k, nb + i, 0)), half],
        out_specs=half, out_shape=S((N_CHIPS, r2, cdim), BF16), compiler_params=_params(2))(g, g, rx)


def sum_chips(name, arrived, mine):
    _, r2, cdim = arrived.shape
    tr = _tile(r2, 512, 16)

    def body(a_ref, m_ref, o_ref):
        me = 2 * lax.axis_index("x") + lax.axis_index("y")
        acc = jnp.zeros((tr, cdim), F32)
        for k in range(N_CHIPS):
            acc = acc + jnp.where(me == k, m_ref[k], a_ref[k]).astype(F32)
        o_ref[...] = acc

    slots = pl.BlockSpec((N_CHIPS, tr, cdim), lambda i: (0, i, 0))
    return pl.pallas_call(
        body, name=name, grid=(r2 // tr,), in_specs=[slots, slots],
        out_specs=pl.BlockSpec((tr, cdim), lambda i: (i, 0)), out_shape=S((r2, cdim), F32), compiler_params=_params(1))(arrived, mine)


def join_copies(src, dst, where, send_sems, recv_sems):
    x, y, c, _ = _place()
    pairs = []
    for i in range(len(src)):
        def copy(half_of):
            r2 = src[i].shape[0]
            rows = pl.ds(pl.multiple_of(where[i][1] + half_of * r2, 8), r2)
            return pltpu.make_async_remote_copy(
                src_ref=src[i], dst_ref=dst[where[i][0]].at[rows], send_sem=send_sems.at[i],
                recv_sem=recv_sems.at[i], device_id=(x, y, 1 - c), device_id_type=MESH)
        pairs.append((copy(c), copy(1 - c)))
    return pairs


def _fill_own_halves(targets, halves, where):
    targets = list(targets)
    c = lax.axis_index("c")
    for h, (tgt, first) in zip(halves, where):
        targets[tgt] = lax.dynamic_update_slice(targets[tgt], h, (first + c * h.shape[0], 0))
    return targets


def sibling_join_halves(name, halves, targets, where):
    n = len(halves)

    def body(*refs):
        pairs = join_copies(refs[:n], refs[n:n + len(targets)], where, *refs[n + len(targets):])
        for outgoing, _ in pairs:
            outgoing.start()
        for _, incoming in pairs:
            incoming.wait_recv()
        for outgoing, _ in pairs:
            outgoing.wait_send()

    outs = pl.pallas_call(
        body, name=name, in_specs=[ANY] * n, out_specs=[ANY] * len(targets), out_shape=[S(tg, F32) for tg in targets],
        scratch_shapes=[pltpu.SemaphoreType.DMA((n,)), pltpu.SemaphoreType.DMA((n,))],
    )(*halves)
    return _fill_own_halves(outs, halves, where)


def all_reduce_small(name, packed):
    rows, width = packed.shape

    def body(x_ref, o_ref, gathered, send_sems, recv_sems):
        x, y, c, _ = _place()
        me = 4 * x + 2 * y + c
        gathered[me] = x_ref[...]
        flips = [(fx, fy, fc) for fx in (0, 1) for fy in (0, 1) for fc in (0, 1)][1:]

        def copy(r, slot, to):
            return pltpu.make_async_remote_copy(
                src_ref=x_ref, dst_ref=gathered.at[slot], send_sem=send_sems.at[r], recv_sem=recv_sems.at[r],
                device_id=to, device_id_type=MESH)

        def peer(f):
            return (x ^ f[0], y ^ f[1], c ^ f[2])

        sent = [copy(r, me, peer(f)) for r, f in enumerate(flips)]
        for cp in sent:
            cp.start()
        for r, f in enumerate(flips):
            px, py, pc = peer(f)
            copy(r, 4 * px + 2 * py + pc, peer(f)).wait_recv()
        for cp in sent:
            cp.wait_send()
        acc = gathered[0]
        for k in range(1, N_DEV):
            acc = acc + gathered[k]
        o_ref[...] = acc

    vmem = pl.BlockSpec(memory_space=pltpu.VMEM)
    return pl.pallas_call(
        body, name=name, in_specs=[vmem], out_specs=vmem, out_shape=S((rows, width), F32),
        scratch_shapes=[pltpu.VMEM((N_DEV, rows, width), F32), pltpu.SemaphoreType.DMA((N_DEV - 1,)),
                        pltpu.SemaphoreType.DMA((N_DEV - 1,))],
    )(packed)


def _rope_tables(positions):
    inv_freq = 1.0 / (ROPE_THETA ** (jnp.arange(0, ROPE, 2, dtype=F32) / ROPE))
    ang = positions.astype(F32)[:, None] * inv_freq
    return jnp.cos(ang), jnp.sin(ang)


def _unstack_cols(w):
    k4, k, n4 = w.shape
    return jnp.transpose(w, (1, 0, 2)).reshape(k, k4 * n4)


def _stack_cols(w):
    k, n = w.shape
    return jnp.transpose(w.reshape(k, N_CHIPS, n // N_CHIPS), (1, 0, 2))


def kernel(x, positions, mla_norm, mla_w_in, mla_g_cq, mla_g_ckv, mla_w_uq, mla_w_ukv, mla_w_o, conv_norm, conv_w_in, conv_w, conv_w_out, ffn_norm, ffn_w_gate, ffn_w_up, ffn_w_down, final_norm, loss_target, m_mla_norm, m_mla_w_in, m_mla_g_cq, m_mla_g_ckv, m_mla_w_uq, m_mla_w_ukv, m_mla_w_o, m_conv_norm, m_conv_w_in, m_conv_w, m_conv_w_out, m_ffn_norm, m_ffn_w_gate, m_ffn_w_up, m_ffn_w_down, m_final_norm, v_mla_norm, v_mla_w_in, v_mla_g_cq, v_mla_g_ckv, v_mla_w_uq, v_mla_w_ukv, v_mla_w_o, v_conv_norm, v_conv_w_in, v_conv_w, v_conv_w_out, v_ffn_norm, v_ffn_w_gate, v_ffn_w_up, v_ffn_w_down, v_final_norm):
    weights = dict(mla_norm=mla_norm, mla_w_in=mla_w_in, mla_g_cq=mla_g_cq, mla_g_ckv=mla_g_ckv, mla_w_uq=mla_w_uq,
                   mla_w_ukv=mla_w_ukv, mla_w_o=mla_w_o, conv_norm=conv_norm, conv_w_in=conv_w_in, conv_w=conv_w,
                   conv_w_out=conv_w_out, ffn_norm=ffn_norm, ffn_w_gate=ffn_w_gate, ffn_w_up=ffn_w_up,
                   ffn_w_down=ffn_w_down, final_norm=final_norm)
    m_in = dict(mla_norm=m_mla_norm, mla_w_in=m_mla_w_in, mla_g_cq=m_mla_g_cq, mla_g_ckv=m_mla_g_ckv, mla_w_uq=m_mla_w_uq,
                mla_w_ukv=m_mla_w_ukv, mla_w_o=m_mla_w_o, conv_norm=m_conv_norm, conv_w_in=m_conv_w_in, conv_w=m_conv_w,
                conv_w_out=m_conv_w_out, ffn_norm=m_ffn_norm, ffn_w_gate=m_ffn_w_gate, ffn_w_up=m_ffn_w_up,
                ffn_w_down=m_ffn_w_down, final_norm=m_final_norm)
    v_in = dict(mla_norm=v_mla_norm, mla_w_in=v_mla_w_in, mla_g_cq=v_mla_g_cq, mla_g_ckv=v_mla_g_ckv, mla_w_uq=v_mla_w_uq,
                mla_w_ukv=v_mla_w_ukv, mla_w_o=v_mla_w_o, conv_norm=v_conv_norm, conv_w_in=v_conv_w_in, conv_w=v_conv_w,
                conv_w_out=v_conv_w_out, ffn_norm=v_ffn_norm, ffn_w_gate=v_ffn_w_gate, ffn_w_up=v_ffn_w_up,
                ffn_w_down=v_ffn_w_down, final_norm=v_final_norm)
    big = ["mla_w_in", "mla_w_uq", "mla_w_ukv", "mla_w_o", "conv_w_in", "conv_w_out", "ffn_w_gate", "ffn_w_up", "ffn_w_down"]
    order = list(weights)

    t, d = x.shape[1], x.shape[2]
    h0 = x.reshape(t, d)
    target = loss_target.reshape(t, d)
    cos, sin = _rope_tables(positions.reshape(t))

    def rows2d(a):
        return a.reshape(-1, a.shape[-1])

    first, later = big[:4], big[4:]
    shards = {n: rows2d(weights[n]).astype(BF16) for n in big}
    d4 = d // N_CHIPS
    first_gathered, (conv_norm_slots, conv_w_slots) = gather_weight_shards(
        [shards[n] for n in first], [conv_norm.reshape(1, d4), conv_w.reshape(3, d4)])
    gathered = dict(zip(first, first_gathered))
    conv_norm_full = conv_norm_slots.reshape(1, d)
    conv_w_full = jnp.transpose(conv_w_slots, (1, 0, 2)).reshape(3, d)
    w_in = gathered["mla_w_in"].reshape(-1, gathered["mla_w_in"].shape[-1])
    w_uq = _unstack_cols(gathered["mla_w_uq"])
    w_ukv = _unstack_cols(gathered["mla_w_ukv"])
    w_o = gathered["mla_w_o"].reshape(-1, d)

    chip = 2 * lax.axis_index("x") + lax.axis_index("y")

    def pack_rows(rows):
        idx = lax.broadcasted_iota(jnp.int32, (SMALL_ROWS, d), 0)
        out = jnp.zeros((SMALL_ROWS, d), F32)
        for r, row in enumerate(rows):
            out = out + jnp.where(idx == r, row, 0.0)
        return out


    a0 = rms_fwd("mla_norm_fwd", h0, mla_norm)
    proj, cq, ckv, kr = mla_in_proj("mla_in_proj", a0, w_in, mla_g_cq, mla_g_ckv, cos, sin)
    q = linear("mla_q_up", cq, w_uq, F32)
    kv = linear("mla_kv_up", ckv, w_ukv, BF16)
    qh, kh, vh, conv_arriving = qkv_heads("qkv_heads", q, kv, kr, cos, sin, [shards[n] for n in later[:2]])
    attn, lse, ffn_arriving = attention_fwd("attention_fwd", qh, kh, vh, [shards[n] for n in later[2:]])
    h1, a1, handed = attention_out_proj("mla_out_proj", attn, w_o, h0, ffn_norm[0:1], conv_arriving + ffn_arriving)
    gathered.update(zip(later, _fill_own_slot(handed, [shards[n][None] for n in later])))
    cw_in = _unstack_cols(gathered["conv_w_in"])
    cw_out = gathered["conv_w_out"].reshape(-1, d)
    wg_all, wu_all, wd_all = gathered["ffn_w_gate"], gathered["ffn_w_up"], gathered["ffn_w_down"]

    def ffn_forward(tag, h, a, layer, next_gain):
        g, u, z = ffn_up(f"ffn{tag}_up", a, wg_all, wu_all, layer)
        return g, u, z, ffn_down(f"ffn{tag}_down", z, wd_all, layer, h, next_gain)

    g0, u0, z0, (h2, a2) = ffn_forward(0, h1, a1, 0, conv_norm_full)
    bcx, yc = conv_in_proj("conv_in_proj", a2, cw_in, conv_w_full)
    h3, a3 = linear("conv_out_proj", yc, cw_out, F32, resid=h2, next_gain=ffn_norm[1:2])
    g1, u1, z1 = ffn_up("ffn1_up", a3, wg_all, wu_all, 1)
    dh4, d_final_norm, loss_local = ffn_down_loss("ffn1_down_loss", z1, wd_all, 1, h3, final_norm.reshape(1, d), target)

    def ffn_backward(tag, dh, h, layer, a, g, u, z, swap=()):
        dg, du, swapped = ffn_bwd_hidden(f"ffn{tag}_bwd_hidden", dh, wd_all, layer, g, u, swap)
        d_wd = ffn_wgrad_down(f"ffn{tag}_wgrad_down", z, dh)
        dh_prev, d_norm = ffn_bwd_input(f"ffn{tag}_bwd_input", dg, du, wg_all, wu_all, layer, h, ffn_norm[layer:layer + 1], dh)
        d_wg = ffn_wgrad_up(f"ffn{tag}_wgrad_gate", a, dg)
        d_wu = ffn_wgrad_up(f"ffn{tag}_wgrad_up", a, du)
        return dh_prev, d_norm, [d_wg, d_wu, d_wd], swapped

    def pair_sums(tag, local, from_sibling):
        return [add_halves(f"pair_sum_{tag}{i}", g, r) for i, (g, r) in enumerate(zip(local, from_sibling))]

    def sum_from_chips(tag, pairs, arrived):
        return [sum_chips(f"chip_sum_{tag}{i}", a, p) for i, (a, p) in enumerate(zip(arrived, pairs))]

    def shard_shape(n):
        return rows2d(weights[n]).shape

    dh3, d_ffn_norm1, ffn1_grads, _ = ffn_backward(1, dh4, h3, 1, a3, g1, u1, z1)

    dyc = linear_nt("conv_out_bwd_input", dh3, cw_out, F32)
    d_cw_out = wgrad("conv_out_wgrad", yc, dh3)
    dbcx, d_conv_w = conv_bwd("conv_bwd", bcx, conv_w_full, dyc)
    dh2, d_conv_norm = conv_in_bwd_input("conv_in_bwd_input", dbcx, cw_in, h2, conv_norm_full, dh3)
    d_cw_in = conv_in_wgrad("conv_in_wgrad", a2, dbcx)

    second = [d_cw_in, d_cw_out.reshape(N_CHIPS, -1, d)] + ffn1_grads
    dh1, d_ffn_norm0, ffn0_grads, second_swapped = ffn_backward(0, dh2, h1, 0, a1, g0, u0, z0, second)
    d_w_o = wgrad("mla_out_wgrad", attn, dh1)
    first_part = ffn0_grads + [d_w_o.reshape(N_CHIPS, -1, d)]
    d_attn, delta, first_swapped = attention_out_bwd("mla_out_bwd_input", dh1, w_o, attn, first_part)
    rest_pairs = pair_sums("rest", second + first_part, second_swapped + first_swapped)
    dqh, dkh, dvh, rest_arrived = attention_bwd("attention_bwd", qh, kh, vh, d_attn, lse, delta, rest_pairs)
    rd, rf = ffn0_grads[0].shape[1], ffn0_grads[2].shape[1]
    rest_where = [(0, 0), (1, 0), (2, rd), (3, rd), (4, rf), (2, 0), (3, 0), (4, 0), (5, 0)]
    rest_names = later + ["mla_w_o"]
    dq, dkv, dkr, rest_grads = qkv_heads_bwd("qkv_heads_bwd", dqh, dkh, dvh, cos, sin, sum_from_chips("rest", rest_pairs, rest_arrived),
                                              [shard_shape(n) for n in rest_names], rest_where)
    grads = dict(zip(rest_names, rest_grads))
    dcq = linear_nt("mla_q_up_bwd_input", dq, w_uq, F32)
    d_w_uq = wgrad("mla_q_up_wgrad", cq, dq)
    dckv = linear_nt("mla_kv_up_bwd_input", dkv, w_ukv, F32)
    d_w_ukv = wgrad("mla_kv_up_wgrad", ckv, dkv)
    dproj, d_g_cq, d_g_ckv = mla_mid_bwd("mla_mid_bwd", proj, mla_g_cq, mla_g_ckv, dcq, dckv, dkr, cos, sin)
    d_w_in = wgrad("mla_in_wgrad", a0, dproj)
    mla_local = [d_w_in.reshape(N_CHIPS, -1, d_w_in.shape[-1]), _stack_cols(d_w_uq), _stack_cols(d_w_ukv)]
    mla_pairs = pair_sums("mla", mla_local, sibling_swap_halves("sibling_swap_mla", mla_local))
    grad_x, d_mla_norm, mla_arrived = linear_nt_norm_bwd("mla_in_bwd_input", dproj, w_in, h0, mla_norm, dh1, mla_pairs)

    grads.update(zip(first[:3], sibling_join_halves("sibling_join_mla", sum_from_chips("mla", mla_pairs, mla_arrived),
                                                    [shard_shape(n) for n in first[:3]], [(i, 0) for i in range(3)])))

    def pad_row(v):
        return jnp.pad(v, ((0, 0), (0, d - v.shape[1])))

    small = all_reduce_small("all_reduce_small_grads", pack_rows([
        d_mla_norm, pad_row(d_g_cq), pad_row(d_g_ckv), d_ffn_norm0, d_ffn_norm1, d_final_norm, d_conv_norm,
        d_conv_w[0:1], d_conv_w[1:2], d_conv_w[2:3], jnp.broadcast_to(loss_local, (1, d))]))
    loss = small[10, 0]
    grads["mla_norm"] = small[0:1]
    grads["mla_g_cq"] = small[1:2, :mla_g_cq.shape[1]]
    grads["mla_g_ckv"] = small[2:3, :mla_g_ckv.shape[1]]
    grads["ffn_norm"] = small[3:5]
    grads["final_norm"] = small[5:6]
    grads["conv_norm"] = lax.dynamic_slice(small[6:7], (0, chip * d4), (1, d4))
    grads["conv_w"] = lax.dynamic_slice(small[7:10], (0, chip * d4), (3, d4))

    outs_g, outs_d, outs_m, outs_v = [], [], [], []
    for n in order:
        w = weights[n]
        if w.ndim == 3 and w.shape[2] % 128 and w.shape[1] % 128 == 0:
            results = adamw_swapped(f"adamw_{n}", jnp.swapaxes(w, 1, 2), grads[n].reshape(-1, w.shape[2]),
                                    jnp.swapaxes(m_in[n], 1, 2), jnp.swapaxes(v_in[n], 1, 2))
            grad_w, delta_w, new_m, new_v = [jnp.swapaxes(o, 1, 2) for o in results]
        else:
            delta_w, new_m, new_v = adamw(f"adamw_{n}", rows2d(w), grads[n].reshape(rows2d(w).shape), rows2d(m_in[n]), rows2d(v_in[n]))
            grad_w = grads[n]
        outs_g.append(grad_w.reshape(w.shape))
        outs_d.append(delta_w.reshape(w.shape))
        outs_m.append(new_m.reshape(w.shape))
        outs_v.append(new_v.reshape(w.shape))
    return (loss, grad_x.reshape(x.shape), *outs_g, *outs_d, *outs_m, *outs_v)
```

```python
import math

import jax
import jax.numpy as jnp
from jax import lax
from jax.experimental import pallas as pl
from jax.experimental.pallas import tpu as pltpu

F32 = jnp.float32
BF16 = jnp.bfloat16
S = jax.ShapeDtypeStruct

N_HEADS = 8
NOPE = 128
ROPE = 64
HALF = ROPE // 2
VDIM = 128
QK = NOPE + ROPE
CHUNK = 64
ROPE_THETA = 10000.0
RMS_EPS = 1e-6
ADAM_LR = 0.001
ADAM_B1 = 0.9
ADAM_B2 = 0.999
ADAM_EPS = 1e-08
ADAM_WD = 0.01
ADAM_STEP = 10

N_CHIPS = 4
N_DEV = 8
MASK_VALUE = -1e30
SCORE_SCALE = 1.0 / math.sqrt(QK)
LOG2_E = math.log2(math.e)
SCORE_SCALE_LOG2 = SCORE_SCALE * LOG2_E
VMEM_LIMIT = 48 * 1024 * 1024
VMEM_LIMIT_WHOLE_HEAD = 58 * 1024 * 1024
ATT_BLOCK = 512
CONV_SAVED_DTYPE = jnp.bfloat16
SMALL_ROWS = 16

_NN = (((1,), (0,)), ((), ()))
_NT = (((1,), (1,)), ((), ()))
_TN = (((0,), (0,)), ((), ()))
MESH = pl.DeviceIdType.MESH
ANY = pl.BlockSpec(memory_space=pl.ANY)


def _params(n_axes, vmem_limit=VMEM_LIMIT):
    return pltpu.CompilerParams(dimension_semantics=("arbitrary",) * n_axes, vmem_limit_bytes=vmem_limit)


def _tile(n, cap, mult=8):
    for t in range(min(cap, n), 0, -1):
        if n % t == 0 and t % mult == 0:
            return t
    return n


def _sigmoid(x):
    return 0.5 * jnp.tanh(0.5 * x) + 0.5


def _mm(name, a_ops, b_ops, products, dims, grid, k_axis, outs, acc_shape, epilogue, extra_ops=()):
    na, nb, ne, no = len(a_ops), len(b_ops), len(extra_ops), len(outs)
    n_acc = 1 + max(c for _, _, c in products)
    nk = 1 if k_axis is None else grid[k_axis]

    def body(*refs):
        a_refs = refs[:na]
        b_refs = refs[na:na + nb]
        e_refs = refs[na + nb:na + nb + ne]
        o_refs = refs[na + nb + ne:na + nb + ne + no]
        acc_refs = refs[na + nb + ne + no:]

        def partial_sums():
            vals = [None] * n_acc
            for ai, bi, ci in products:
                d = lax.dot_general(a_refs[ai][...].astype(BF16), b_refs[bi][...].astype(BF16), dims,
                                    preferred_element_type=F32)
                vals[ci] = d if vals[ci] is None else vals[ci] + d
            return vals

        if nk == 1:
            epilogue(partial_sums(), e_refs, o_refs)
        else:
            k = pl.program_id(k_axis)

            @pl.when(k == 0)
            def _():
                for acc in acc_refs:
                    acc[...] = jnp.zeros_like(acc)

            for acc, v in zip(acc_refs, partial_sums()):
                acc[...] += v

            @pl.when(k == nk - 1)
            def _():
                epilogue([acc[...] for acc in acc_refs], e_refs, o_refs)

    ops = list(a_ops) + list(b_ops) + list(extra_ops)
    return pl.pallas_call(
        body, name=name, grid=grid,
        in_specs=[s for _, s in ops], out_specs=[s for _, s in outs], out_shape=[o for o, _ in outs],
        scratch_shapes=[pltpu.VMEM(acc_shape, F32) for _ in range(n_acc if nk > 1 else 0)],
        compiler_params=_params(len(grid)),
    )(*[a for a, _ in ops])


def _store(accs, e_refs, o_refs):
    o_refs[0][...] = accs[0].astype(o_refs[0].dtype)


def linear(name, x, w, out_dtype, resid=None, next_gain=None):
    t, k = x.shape
    n = w.shape[1]
    tm = _tile(t, 512)
    tn = n if n <= 2048 else _tile(n, 1024, 128)
    tile = pl.BlockSpec((tm, tn), lambda j, i: (i, j))
    extra = [] if resid is None else [(resid, tile)]
    outs = [(S((t, n), out_dtype), tile)]
    if next_gain is not None:
        assert tn == n
        extra.append((next_gain, pl.BlockSpec((1, n), lambda j, i: (0, 0))))
        outs.append((S((t, n), BF16), tile))

    def epilogue(accs, e_refs, o_refs):
        y = accs[0] if resid is None else e_refs[0][...] + accs[0]
        o_refs[0][...] = y.astype(out_dtype)
        if next_gain is not None:
            o_refs[1][...] = (y * _rstd(y) * e_refs[-1][...]).astype(BF16)

    res = _mm(name, [(x, pl.BlockSpec((tm, k), lambda j, i: (i, 0)))], [(w, pl.BlockSpec((k, tn), lambda j, i: (0, j)))],
              [(0, 0, 0)], _NN, (n // tn, t // tm), None, outs, None, epilogue, extra)
    return res[0] if next_gain is None else res


def linear_nt(name, dy, w, out_dtype):
    t, n = dy.shape
    k = w.shape[0]
    tm = _tile(t, 512)
    tc = n if n <= 2048 else _tile(n, 1024, 128)
    return _mm(name, [(dy, pl.BlockSpec((tm, tc), lambda i, c: (i, c)))], [(w, pl.BlockSpec((k, tc), lambda i, c: (0, c)))],
               [(0, 0, 0)], _NT, (t // tm, n // tc), 1,
               [(S((t, k), out_dtype), pl.BlockSpec((tm, k), lambda i, c: (i, 0)))], (tm, k), _store)[0]


def wgrad(name, x, dy):
    t, k = x.shape
    n = dy.shape[1]
    tk = _tile(t, 512)
    tn = n if n <= 1024 else _tile(n, 1024, 128)
    return _mm(name, [(x, pl.BlockSpec((tk, k), lambda j, s: (s, 0)))], [(dy, pl.BlockSpec((tk, tn), lambda j, s: (s, j)))],
               [(0, 0, 0)], _TN, (n // tn, t // tk), 1,
               [(S((k, n), BF16), pl.BlockSpec((k, tn), lambda j, s: (0, j)))], (k, tn), _store)[0]


def _resident(shape, index_map):
    return pl.BlockSpec(shape, index_map, pipeline_mode=pl.Buffered(1))


def ffn_up(name, a, wg_all, wu_all, layer):
    t, d = a.shape
    f4 = wg_all.shape[2]
    tm = _tile(t, 512)
    w_spec = _resident((N_CHIPS, d, f4), lambda i: (0, layer, 0))
    h_spec = pl.BlockSpec((N_CHIPS, tm, f4), lambda i: (0, i, 0))

    def body(a_ref, wg_ref, wu_ref, zg_ref, zu_ref, z_ref):
        av = a_ref[...]
        for k in range(N_CHIPS):
            g = jnp.dot(av, wg_ref[k], preferred_element_type=F32)
            u = jnp.dot(av, wu_ref[k], preferred_element_type=F32)
            sg = _sigmoid(g)
            silu = g * sg
            zg_ref[k] = (u * (sg * (1.0 + g * (1.0 - sg)))).astype(BF16)
            zu_ref[k] = silu.astype(BF16)
            z_ref[k] = (silu * u).astype(BF16)

    return pl.pallas_call(
        body, name=name, grid=(t // tm,), in_specs=[pl.BlockSpec((tm, d), lambda i: (i, 0)), w_spec, w_spec],
        out_specs=[h_spec] * 3, out_shape=[S((N_CHIPS, t, f4), BF16)] * 3, compiler_params=_params(1))(a, wg_all, wu_all)


def ffn_down(name, z, wd_all, layer, resid, next_gain=None):
    _, t, f4 = z.shape
    d = wd_all.shape[2]
    tm = _tile(t, 512)
    row = pl.BlockSpec((tm, d), lambda i: (i, 0))
    normed = next_gain is not None

    def body(z_ref, wd_ref, r_ref, *refs):
        acc = r_ref[...]
        for k in range(N_CHIPS):
            acc = acc + jnp.dot(z_ref[k], wd_ref[k], preferred_element_type=F32)
        refs[-2 if normed else -1][...] = acc
        if normed:
            refs[-1][...] = (acc * _rstd(acc) * refs[0][...]).astype(BF16)

    res = pl.pallas_call(
        body, name=name, grid=(t // tm,),
        in_specs=[pl.BlockSpec((N_CHIPS, tm, f4), lambda i: (0, i, 0)), _resident((N_CHIPS, f4, d), lambda i: (0, layer, 0)), row]
        + ([pl.BlockSpec((1, d), lambda i: (0, 0))] if normed else []),
        out_specs=[row] * (2 if normed else 1), out_shape=[S((t, d), F32)] + ([S((t, d), BF16)] if normed else []),
        compiler_params=_params(1))(z, wd_all, resid, *([next_gain] if normed else []))
    return res if normed else res[0]


def ffn_bwd_hidden(name, dh, wd_all, layer, zg, zu, swap=()):
    t, d = dh.shape
    f4 = zg.shape[2]
    tm = _tile(t, 512)
    h_spec = pl.BlockSpec((N_CHIPS, tm, f4), lambda i: (0, i, 0))
    n = len(swap)

    def body(dh_ref, wd_ref, zg_ref, zu_ref, *refs):
        dg_ref, du_ref = refs[n:n + 2]
        if n:
            _ride_along(swap_copies(refs[:n], refs[n + 2:2 * n + 2], *refs[2 * n + 2:]), (pl.program_id(0),), (t // tm,))
        dhb = dh_ref[...].astype(BF16)
        for k in range(N_CHIPS):
            dz = lax.dot_general(dhb, wd_ref[k], _NT, preferred_element_type=F32)
            dg_ref[k] = (dz * zg_ref[k].astype(F32)).astype(BF16)
            du_ref[k] = (dz * zu_ref[k].astype(F32)).astype(BF16)

    outs = pl.pallas_call(
        body, name=name, grid=(t // tm,),
        in_specs=[pl.BlockSpec((tm, d), lambda i: (i, 0)), _resident((N_CHIPS, f4, d), lambda i: (0, layer, 0)), h_spec, h_spec]
        + [ANY] * n,
        out_specs=[h_spec] * 2 + [ANY] * n, out_shape=[S((N_CHIPS, t, f4), BF16)] * 2 + _swap_shapes(swap),
        scratch_shapes=[pltpu.SemaphoreType.DMA((n,)), pltpu.SemaphoreType.DMA((n,))] if n else [],
        compiler_params=_params(1))(dh, wd_all, zg, zu, *swap)
    return outs[0], outs[1], list(outs[2:])


def _norm_bwd_specs(tm, d):
    row = pl.BlockSpec((tm, d), lambda i: (i, 0))
    vec = pl.BlockSpec((1, d), lambda i: (0, 0))
    return [row, vec, row], [row, vec]


def _norm_bwd_tail(da, h_ref, g_ref, dhi_ref, dho_ref, dgain_ref):
    dx, dgain = _rms_bwd(h_ref[...], g_ref[...], da)
    dho_ref[...] = dhi_ref[...] + dx

    @pl.when(pl.program_id(0) == 0)
    def _():
        dgain_ref[...] = jnp.zeros_like(dgain_ref)

    dgain_ref[...] += dgain


def ffn_bwd_input(name, dg, du, wg_all, wu_all, layer, h, gain, dh_in):
    _, t, f4 = dg.shape
    d = h.shape[1]
    tm = _tile(t, 512)
    h_spec = pl.BlockSpec((N_CHIPS, tm, f4), lambda i: (0, i, 0))
    w_spec = _resident((N_CHIPS, d, f4), lambda i: (0, layer, 0))
    tail_in, tail_out = _norm_bwd_specs(tm, d)

    def body(dg_ref, du_ref, wg_ref, wu_ref, *tail):
        acc = jnp.zeros((tm, d), F32)
        for k in range(N_CHIPS):
            acc = acc + lax.dot_general(dg_ref[k], wg_ref[k], _NT, preferred_element_type=F32)
            acc = acc + lax.dot_general(du_ref[k], wu_ref[k], _NT, preferred_element_type=F32)
        _norm_bwd_tail(acc, *tail)

    return pl.pallas_call(
        body, name=name, grid=(t // tm,), in_specs=[h_spec, h_spec, w_spec, w_spec] + tail_in, out_specs=tail_out,
        out_shape=[S((t, d), F32), S((1, d), F32)], compiler_params=_params(1))(dg, du, wg_all, wu_all, h, gain, dh_in)


def ffn_wgrad_up(name, a, dy):
    t, d = a.shape
    f4 = dy.shape[2]
    tk = _tile(t, 512)
    nt = t // tk

    def body(a_ref, dy_ref, o_ref, acc):
        s = pl.program_id(0)

        @pl.when(s == 0)
        def _():
            acc[...] = jnp.zeros_like(acc)

        at = a_ref[...].T
        for k in range(N_CHIPS):
            acc[k] += jnp.dot(at, dy_ref[k], preferred_element_type=F32)

        @pl.when(s == nt - 1)
        def _():
            o_ref[...] = acc[...].astype(BF16)

    return pl.pallas_call(
        body, name=name, grid=(nt,),
        in_specs=[pl.BlockSpec((tk, d), lambda s: (s, 0)), pl.BlockSpec((N_CHIPS, tk, f4), lambda s: (0, s, 0))],
        out_specs=pl.BlockSpec((N_CHIPS, d, f4), lambda s: (0, 0, 0)), out_shape=S((N_CHIPS, d, f4), BF16),
        scratch_shapes=[pltpu.VMEM((N_CHIPS, d, f4), F32)], compiler_params=_params(1))(a, dy)


def ffn_wgrad_down(name, z, dh):
    _, t, f4 = z.shape
    d = dh.shape[1]
    tk = _tile(t, 512)
    nt = t // tk

    def body(z_ref, dh_ref, o_ref, acc):
        s = pl.program_id(0)

        @pl.when(s == 0)
        def _():
            acc[...] = jnp.zeros_like(acc)

        dhb = dh_ref[...].astype(BF16)
        for k in range(N_CHIPS):
            acc[k] += lax.dot_general(z_ref[k], dhb, _TN, preferred_element_type=F32)

        @pl.when(s == nt - 1)
        def _():
            o_ref[...] = acc[...].astype(BF16)

    return pl.pallas_call(
        body, name=name, grid=(nt,),
        in_specs=[pl.BlockSpec((N_CHIPS, tk, f4), lambda s: (0, s, 0)), pl.BlockSpec((tk, d), lambda s: (s, 0))],
        out_specs=pl.BlockSpec((N_CHIPS, f4, d), lambda s: (0, 0, 0)), out_shape=S((N_CHIPS, f4, d), BF16),
        scratch_shapes=[pltpu.VMEM((N_CHIPS, f4, d), F32)], compiler_params=_params(1))(z, dh)


def conv_in_proj(name, a, w, conv_w):
    t, d = a.shape
    tm = _tile(t, 256)
    keep = 8

    def body(a_ref, w_ref, cw_ref, bcx_ref, y_ref, u_ref):
        @pl.when(pl.program_id(0) == 0)
        def _():
            u_ref[0:keep, :] = jnp.zeros((keep, d), F32)

        av = a_ref[...]
        b, c, x = [jnp.dot(av, w_ref[:, j * d:(j + 1) * d], preferred_element_type=F32) for j in range(3)]
        for j, part in enumerate((b, c, x)):
            bcx_ref[j] = part.astype(bcx_ref.dtype)
        u_ref[keep:keep + tm, :] = c * x
        uc = (cw_ref[0:1, :] * u_ref[keep - 2:keep - 2 + tm, :] + cw_ref[1:2, :] * u_ref[keep - 1:keep - 1 + tm, :]
              + cw_ref[2:3, :] * u_ref[keep:keep + tm, :])
        y_ref[...] = (b * uc).astype(BF16)
        u_ref[0:keep, :] = u_ref[tm:tm + keep, :]

    return pl.pallas_call(
        body, name=name, grid=(t // tm,),
        in_specs=[pl.BlockSpec((tm, d), lambda i: (i, 0)), _resident((d, 3 * d), lambda i: (0, 0)), pl.BlockSpec((3, d), lambda i: (0, 0))],
        out_specs=[pl.BlockSpec((3, tm, d), lambda i: (0, i, 0)), pl.BlockSpec((tm, d), lambda i: (i, 0))],
        out_shape=[S((3, t, d), CONV_SAVED_DTYPE), S((t, d), BF16)], scratch_shapes=[pltpu.VMEM((tm + keep, d), F32)],
        compiler_params=_params(1))(a, w, conv_w)


def conv_in_bwd_input(name, dbcx, w, h, gain, dh_in):
    _, t, d = dbcx.shape
    tm = _tile(t, 512)
    tail_in, tail_out = _norm_bwd_specs(tm, d)

    def body(g_ref, w_ref, *tail):
        acc = jnp.zeros((tm, d), F32)
        for j in range(3):
            acc = acc + lax.dot_general(g_ref[j], w_ref[:, j * d:(j + 1) * d], _NT, preferred_element_type=F32)
        _norm_bwd_tail(acc, *tail)

    return pl.pallas_call(
        body, name=name, grid=(t // tm,),
        in_specs=[pl.BlockSpec((3, tm, d), lambda i: (0, i, 0)), _resident((d, 3 * d), lambda i: (0, 0))] + tail_in,
        out_specs=tail_out, out_shape=[S((t, d), F32), S((1, d), F32)], compiler_params=_params(1))(dbcx, w, h, gain, dh_in)


def linear_nt_norm_bwd(name, dy, w, h, gain, dh_in, parts=()):
    t, n = dy.shape
    k = w.shape[0]
    tm = _tile(t, 512)
    tail_in, tail_out = _norm_bwd_specs(tm, k)
    m = len(parts)

    def body(dy_ref, w_ref, h_ref, g_ref, dhi_ref, *refs):
        if m:
            _ride_along(scatter_ici_copies(refs[:m], refs[m + 2:2 * m + 2], *refs[2 * m + 2:]), (pl.program_id(0),), (t // tm,))
        da = lax.dot_general(dy_ref[...].astype(BF16), w_ref[...], _NT, preferred_element_type=F32)
        _norm_bwd_tail(da, h_ref, g_ref, dhi_ref, *refs[m:m + 2])

    outs = pl.pallas_call(
        body, name=name, grid=(t // tm,),
        in_specs=[pl.BlockSpec((tm, n), lambda i: (i, 0)), _resident((k, n), lambda i: (0, 0))] + tail_in + [ANY] * m,
        out_specs=tail_out + [ANY] * m, out_shape=[S((t, k), F32), S((1, k), F32)] + [S(p.shape, p.dtype) for p in parts],
        scratch_shapes=[pltpu.SemaphoreType.DMA((m, 3)), pltpu.SemaphoreType.DMA((m, 3))] if m else [],
        compiler_params=_params(1))(dy, w, h, gain, dh_in, *parts)
    return outs[0], outs[1], list(outs[2:])


def conv_in_wgrad(name, a, dbcx):
    t, d = a.shape
    tk = _tile(t, 512)
    nt = t // tk
    n4 = 3 * d // N_CHIPS

    def body(a_ref, g_ref, o_ref, acc):
        s = pl.program_id(0)

        @pl.when(s == 0)
        def _():
            acc[...] = jnp.zeros_like(acc)

        at = a_ref[...].T
        for j in range(3):
            acc[:, j * d:(j + 1) * d] += jnp.dot(at, g_ref[j], preferred_element_type=F32)

        @pl.when(s == nt - 1)
        def _():
            for k in range(N_CHIPS):
                o_ref[k] = acc[:, k * n4:(k + 1) * n4].astype(BF16)

    return pl.pallas_call(
        body, name=name, grid=(nt,),
        in_specs=[pl.BlockSpec((tk, d), lambda s: (s, 0)), pl.BlockSpec((3, tk, d), lambda s: (0, s, 0))],
        out_specs=pl.BlockSpec((N_CHIPS, d, n4), lambda s: (0, 0, 0)), out_shape=S((N_CHIPS, d, n4), BF16),
        scratch_shapes=[pltpu.VMEM((d, 3 * d), F32)], compiler_params=_params(1))(a, dbcx)


def _rstd(x):
    return lax.rsqrt(jnp.mean(x * x, axis=-1, keepdims=True) + RMS_EPS)


def _rms_bwd(x, g, dy):
    r = _rstd(x)
    xhat = x * r
    dgain = jnp.sum(dy * xhat, axis=0, keepdims=True)
    dxh = dy * g
    dx = r * (dxh - xhat * jnp.mean(dxh * xhat, axis=-1, keepdims=True))
    return dx, dgain


def rms_fwd(name, h, g):
    t, d = h.shape
    tr = _tile(t, 512)

    def body(h_ref, g_ref, a_ref):
        x = h_ref[...]
        a_ref[...] = (x * _rstd(x) * g_ref[...]).astype(BF16)

    return pl.pallas_call(
        body, name=name, grid=(t // tr,),
        in_specs=[pl.BlockSpec((tr, d), lambda i: (i, 0)), pl.BlockSpec((1, d), lambda i: (0, 0))],
        out_specs=pl.BlockSpec((tr, d), lambda i: (i, 0)), out_shape=S((t, d), BF16), compiler_params=_params(1))(h, g)


def ffn_down_loss(name, z, wd_all, layer, resid, gain, target):
    _, t, f4 = z.shape
    d = wd_all.shape[2]
    tm = _tile(t, 512)

    def body(z_ref, wd_ref, r_ref, g_ref, t_ref, dh_ref, dg_ref, loss_ref):
        x = r_ref[...]
        for k in range(N_CHIPS):
            x = x + jnp.dot(z_ref[k], wd_ref[k], preferred_element_type=F32)
        g = g_ref[...]
        r = _rstd(x)
        xhat = x * r
        err = xhat * g - t_ref[...]
        dy = err * (1.0 / d)
        dxh = dy * g
        dh_ref[...] = r * (dxh - xhat * jnp.mean(dxh * xhat, axis=-1, keepdims=True))

        @pl.when(pl.program_id(0) == 0)
        def _():
            dg_ref[...] = jnp.zeros_like(dg_ref)
            loss_ref[...] = jnp.zeros_like(loss_ref)

        dg_ref[...] += jnp.sum(dy * xhat, axis=0, keepdims=True)
        per_token = jnp.mean(err * err, axis=-1, keepdims=True)
        loss_ref[...] += 0.5 * jnp.sum(per_token, axis=0, keepdims=True)

    row = pl.BlockSpec((tm, d), lambda i: (i, 0))
    vec = pl.BlockSpec((1, d), lambda i: (0, 0))
    one = pl.BlockSpec((1, 1), lambda i: (0, 0))
    return pl.pallas_call(
        body, name=name, grid=(t // tm,),
        in_specs=[pl.BlockSpec((N_CHIPS, tm, f4), lambda i: (0, i, 0)), _resident((N_CHIPS, f4, d), lambda i: (0, layer, 0)), row, vec, row],
        out_specs=[row, vec, one], out_shape=[S((t, d), F32), S((1, d), F32), S((1, 1), F32)],
        compiler_params=_params(1))(z, wd_all, resid, gain, target)


def mla_in_proj(name, a, w, g_cq, g_ckv, cos, sin):
    t, d = a.shape
    n = w.shape[1]
    ql, kl = g_cq.shape[1], g_ckv.shape[1]
    tr = _tile(t, 512)

    def body(a_ref, w_ref, gq_ref, gk_ref, c_ref, s_ref, p_ref, cq_ref, ckv_ref, kr_ref):
        p_ref[...] = jnp.dot(a_ref[...], w_ref[...], preferred_element_type=F32)
        xq = p_ref[:, 0:ql]
        cq_ref[...] = (xq * _rstd(xq) * gq_ref[...]).astype(BF16)
        xk = p_ref[:, ql:ql + kl]
        ckv_ref[...] = (xk * _rstd(xk) * gk_ref[...]).astype(BF16)
        k1 = p_ref[:, ql + kl:ql + kl + HALF]
        k2 = p_ref[:, ql + kl + HALF:ql + kl + ROPE]
        c = c_ref[...]
        s = s_ref[...]
        kr_ref[:, 0:HALF] = k1 * c - k2 * s
        kr_ref[:, HALF:ROPE] = k1 * s + k2 * c

    def row(w):
        return pl.BlockSpec((tr, w), lambda i: (i, 0))

    def vec(w):
        return pl.BlockSpec((1, w), lambda i: (0, 0))

    return pl.pallas_call(
        body, name=name, grid=(t // tr,),
        in_specs=[row(d), _resident((d, n), lambda i: (0, 0)), vec(ql), vec(kl), row(HALF), row(HALF)],
        out_specs=[row(n), row(ql), row(kl), row(ROPE)],
        out_shape=[S((t, n), F32), S((t, ql), BF16), S((t, kl), BF16), S((t, ROPE), F32)],
        compiler_params=_params(1))(a, w, g_cq, g_ckv, cos, sin)


def mla_mid_bwd(name, proj, g_cq, g_ckv, dcq, dckv, dkr, cos, sin):
    t, n = proj.shape
    ql, kl = g_cq.shape[1], g_ckv.shape[1]
    tr = _tile(t, 512)

    def body(p_ref, gq_ref, gk_ref, dcq_ref, dckv_ref, dkr_ref, c_ref, s_ref, dp_ref, dgq_ref, dgk_ref):
        dxq, dgq = _rms_bwd(p_ref[:, 0:ql], gq_ref[...], dcq_ref[...])
        dp_ref[:, 0:ql] = dxq.astype(BF16)
        dxk, dgk = _rms_bwd(p_ref[:, ql:ql + kl], gk_ref[...], dckv_ref[...])
        dp_ref[:, ql:ql + kl] = dxk.astype(BF16)
        d1 = dkr_ref[:, 0:HALF]
        d2 = dkr_ref[:, HALF:ROPE]
        c = c_ref[...]
        s = s_ref[...]
        dp_ref[:, ql + kl:ql + kl + HALF] = (d1 * c + d2 * s).astype(BF16)
        dp_ref[:, ql + kl + HALF:ql + kl + ROPE] = (d2 * c - d1 * s).astype(BF16)

        @pl.when(pl.program_id(0) == 0)
        def _():
            dgq_ref[...] = jnp.zeros_like(dgq_ref)
            dgk_ref[...] = jnp.zeros_like(dgk_ref)

        dgq_ref[...] += dgq
        dgk_ref[...] += dgk

    def row(w):
        return pl.BlockSpec((tr, w), lambda i: (i, 0))

    def vec(w):
        return pl.BlockSpec((1, w), lambda i: (0, 0))

    return pl.pallas_call(
        body, name=name, grid=(t // tr,),
        in_specs=[row(n), vec(ql), vec(kl), row(ql), row(kl), row(ROPE), row(HALF), row(HALF)],
        out_specs=[row(n), vec(ql), vec(kl)], out_shape=[S((t, n), BF16), S((1, ql), F32), S((1, kl), F32)],
        compiler_params=_params(1))(proj, g_cq, g_ckv, dcq, dckv, dkr, cos, sin)


def qkv_heads(name, q, kv, kr, cos, sin, shards=()):
    t = q.shape[0]
    tr = _tile(t, 256)
    n = len(shards)

    def body(q_ref, kv_ref, kr_ref, c_ref, s_ref, *refs):
        src = refs[:n]
        qo_ref, ko_ref, vo_ref = refs[n:n + 3]
        if n:
            _ride_along(gather_ici_copies(src, refs[n + 3:2 * n + 3], *refs[2 * n + 3:]), (pl.program_id(0),), (t // tr,))
        c = c_ref[...]
        s = s_ref[...]
        krb = kr_ref[...].astype(BF16)
        for h in range(N_HEADS):
            q0 = h * QK
            qo_ref[h, :, 0:NOPE] = q_ref[:, q0:q0 + NOPE].astype(BF16)
            q1 = q_ref[:, q0 + NOPE:q0 + NOPE + HALF]
            q2 = q_ref[:, q0 + NOPE + HALF:q0 + QK]
            qo_ref[h, :, NOPE:NOPE + HALF] = (q1 * c - q2 * s).astype(BF16)
            qo_ref[h, :, NOPE + HALF:QK] = (q1 * s + q2 * c).astype(BF16)
            k0 = h * (NOPE + VDIM)
            ko_ref[h, :, 0:NOPE] = kv_ref[:, k0:k0 + NOPE]
            ko_ref[h, :, NOPE:QK] = krb
            vo_ref[h] = kv_ref[:, k0 + NOPE:k0 + NOPE + VDIM]

    def row(w):
        return pl.BlockSpec((tr, w), lambda i: (i, 0))

    def heads(w):
        return pl.BlockSpec((N_HEADS, tr, w), lambda i: (0, i, 0))

    outs = pl.pallas_call(
        body, name=name, grid=(t // tr,),
        in_specs=[row(N_HEADS * QK), row(N_HEADS * (NOPE + VDIM)), row(ROPE), row(HALF), row(HALF)] + [ANY] * n,
        out_specs=[heads(QK), heads(QK), heads(VDIM)] + [ANY] * n,
        out_shape=[S((N_HEADS, t, QK), BF16), S((N_HEADS, t, QK), BF16), S((N_HEADS, t, VDIM), BF16)]
        + [S((N_CHIPS,) + s.shape, s.dtype) for s in shards],
        scratch_shapes=[pltpu.SemaphoreType.DMA((n, 3)), pltpu.SemaphoreType.DMA((n, 3))] if n else [],
        compiler_params=_params(1))(q, kv, kr, cos, sin, *shards)
    return outs[0], outs[1], outs[2], list(outs[3:])


def qkv_heads_bwd(name, dq_h, dk_h, dv_h, cos, sin, halves=(), targets=(), where=()):
    t = dq_h.shape[1]
    tr = _tile(t, 256)
    n, nt = len(halves), len(targets)

    def body(dq_ref, dk_ref, dv_ref, c_ref, s_ref, *refs):
        q_ref, kv_ref, kr_ref = refs[n:n + 3]
        if n:
            src, dst = refs[:n], refs[n + 3:n + 3 + nt]
            stages = refs[n + 3 + nt:2 * n + 3 + nt]
            send_sems, recv_sems, local_sems = refs[2 * n + 3 + nt:]
            c = lax.axis_index("c")
            own = [(src[i], stages[i],
                    dst[where[i][0]].at[pl.ds(pl.multiple_of(where[i][1] + c * src[i].shape[0], 8), src[i].shape[0])],
                    local_sems.at[i]) for i in range(n)]
            _ride_along(join_copies(src, dst, where, send_sems, recv_sems), (pl.program_id(0),), (t // tr,), own)
        c = c_ref[...]
        s = s_ref[...]
        dkr = jnp.zeros((tr, ROPE), F32)
        for h in range(N_HEADS):
            q0 = h * QK
            q_ref[:, q0:q0 + NOPE] = dq_ref[h, :, 0:NOPE].astype(BF16)
            d1 = dq_ref[h, :, NOPE:NOPE + HALF]
            d2 = dq_ref[h, :, NOPE + HALF:QK]
            q_ref[:, q0 + NOPE:q0 + NOPE + HALF] = (d1 * c + d2 * s).astype(BF16)
            q_ref[:, q0 + NOPE + HALF:q0 + QK] = (d2 * c - d1 * s).astype(BF16)
            k0 = h * (NOPE + VDIM)
            kv_ref[:, k0:k0 + NOPE] = dk_ref[h, :, 0:NOPE].astype(BF16)
            kv_ref[:, k0 + NOPE:k0 + NOPE + VDIM] = dv_ref[h].astype(BF16)
            dkr = dkr + dk_ref[h, :, NOPE:QK]
        kr_ref[...] = dkr

    def row(w):
        return pl.BlockSpec((tr, w), lambda i: (i, 0))

    def heads(w):
        return pl.BlockSpec((N_HEADS, tr, w), lambda i: (0, i, 0))

    outs = pl.pallas_call(
        body, name=name, grid=(t // tr,),
        in_specs=[heads(QK), heads(QK), heads(VDIM), row(HALF), row(HALF)] + [ANY] * n,
        out_specs=[row(N_HEADS * QK), row(N_HEADS * (NOPE + VDIM)), row(ROPE)] + [ANY] * nt,
        out_shape=[S((t, N_HEADS * QK), BF16), S((t, N_HEADS * (NOPE + VDIM)), BF16), S((t, ROPE), F32)]
        + [S(tg, F32) for tg in targets],
        scratch_shapes=[pltpu.VMEM(h.shape, h.dtype) for h in halves]
        + ([pltpu.SemaphoreType.DMA((n,)), pltpu.SemaphoreType.DMA((n,)), pltpu.SemaphoreType.DMA((n, 2))] if n else []),
        compiler_params=_params(1))(dq_h, dk_h, dv_h, cos, sin, *halves)
    return outs[0], outs[1], outs[2], list(outs[3:])


def _chunk_mask_t(q_start, k_start, bq, bk):
    kc = (k_start + lax.broadcasted_iota(jnp.int32, (bk, bq), 0)) // CHUNK
    qc = (q_start + lax.broadcasted_iota(jnp.int32, (bk, bq), 1)) // CHUNK
    return kc <= qc


def attention_fwd(name, q, k, v, shards=()):
    nh, t, _ = q.shape
    blk = ATT_BLOCK
    nq = t // blk
    n = len(shards)

    def body(q_ref, k_ref, v_ref, *refs):
        src = refs[:n]
        o_ref, lse_ref = refs[n:n + 2]
        dst = refs[n + 2:2 * n + 2]
        m_ref, l_ref, acc_ref, s_buf, p_buf, alpha_buf, bias_ref = refs[2 * n + 2:2 * n + 9]
        i = pl.program_id(1)
        if n:
            send_sems, recv_sems = refs[2 * n + 9:]
            _ride_along(gather_ici_copies(src, dst, send_sems, recv_sems), (pl.program_id(0), i), (nh, nq))

        @pl.when((pl.program_id(0) == 0) & (i == 0))
        def _():
            bias_ref[...] = jnp.where(_chunk_mask_t(0, 0, blk, blk), 0.0, MASK_VALUE)

        m_ref[...] = jnp.full_like(m_ref, MASK_VALUE)
        l_ref[...] = jnp.zeros_like(l_ref)
        acc_ref[...] = jnp.zeros_like(acc_ref)

        def rows(b):
            return pl.ds(pl.multiple_of(b * blk, blk), blk)

        def scores(b, slot):
            s_buf[slot] = lax.dot_general(k_ref[rows(b), :], q_ref[...], _NT, preferred_element_type=F32)

        def softmax(slot, diagonal):
            s = s_buf[slot]
            if diagonal:
                s = s + bias_ref[...]
            m_old = m_ref[...]
            m_new = jnp.maximum(m_old, jnp.max(s, axis=0, keepdims=True))
            p = jnp.exp2((s - m_new) * SCORE_SCALE_LOG2)
            alpha = jnp.exp2((m_old - m_new) * SCORE_SCALE_LOG2)
            l_ref[...] = alpha * l_ref[...] + jnp.sum(p, axis=0, keepdims=True)
            m_ref[...] = m_new
            alpha_buf[slot] = alpha
            p_buf[slot] = p.astype(BF16)

        def values(b, slot):
            pv = lax.dot_general(v_ref[rows(b), :], p_buf[slot], _TN, preferred_element_type=F32)
            acc_ref[...] = alpha_buf[slot] * acc_ref[...] + pv

        def step(t, slot):
            values(t - 2, slot)
            softmax(1 - slot, False)
            scores(t, slot)

        scores(0, 0)

        @pl.when(i == 0)
        def _():
            softmax(0, True)
            values(0, 0)

        @pl.when(i > 0)
        def _():
            scores(1, 1)
            softmax(0, False)
            steady = i - 1

            def pair(u, carry):
                step(2 + 2 * u, 0)
                step(3 + 2 * u, 1)
                return carry

            lax.fori_loop(0, steady // 2, pair, 0)

            @pl.when(steady % 2 == 1)
            def _():
                step(i, 0)

            last = i % 2
            softmax(last, True)
            values(i - 1, 1 - last)
            values(i, last)

        l = l_ref[...]
        o_ref[...] = (acc_ref[...] / l).T
        lse_ref[...] = m_ref[...] * SCORE_SCALE + jnp.log(l)

    outs = pl.pallas_call(
        body, name=name, grid=(nh, nq),
        in_specs=[pl.BlockSpec((None, blk, QK), lambda h, i: (h, i, 0)), pl.BlockSpec((None, t, QK), lambda h, i: (h, 0, 0)),
                  pl.BlockSpec((None, t, VDIM), lambda h, i: (h, 0, 0))] + [ANY] * n,
        out_specs=[pl.BlockSpec((blk, VDIM), lambda h, i: (i, h)),
                   pl.BlockSpec((None, None, 1, blk), lambda h, i: (h, i, 0, 0))] + [ANY] * n,
        out_shape=[S((t, nh * VDIM), F32), S((nh, nq, 1, blk), F32)] + [S((N_CHIPS,) + s.shape, s.dtype) for s in shards],
        scratch_shapes=[pltpu.VMEM((1, blk), F32), pltpu.VMEM((1, blk), F32), pltpu.VMEM((VDIM, blk), F32),
                        pltpu.VMEM((2, blk, blk), F32), pltpu.VMEM((2, blk, blk), BF16), pltpu.VMEM((2, 1, blk), F32),
                        pltpu.VMEM((blk, blk), F32)]
        + ([pltpu.SemaphoreType.DMA((n, 3)), pltpu.SemaphoreType.DMA((n, 3))] if n else []),
        compiler_params=_params(2))(q, k, v, *shards)
    return outs[0], outs[1], list(outs[2:])


def attention_out_bwd(name, dh, w_o, o, swap=()):
    t, d = dh.shape
    n = w_o.shape[0]
    blk = ATT_BLOCK
    m = len(swap)

    def body(dh_ref, w_ref, o_ref, *refs):
        do_ref, d_ref = refs[m:m + 2]
        if m:
            _ride_along(swap_copies(refs[:m], refs[m + 2:2 * m + 2], *refs[2 * m + 2:]), (pl.program_id(0),), (t // blk,))
        do_ref[...] = lax.dot_general(dh_ref[...].astype(BF16), w_ref[...], _NT, preferred_element_type=F32)
        for h in range(N_HEADS):
            cols = slice(h * VDIM, (h + 1) * VDIM)
            d_ref[h] = jnp.sum((do_ref[:, cols] * o_ref[:, cols]).T, axis=0, keepdims=True)

    tile = pl.BlockSpec((blk, n), lambda i: (i, 0))
    outs = pl.pallas_call(
        body, name=name, grid=(t // blk,),
        in_specs=[pl.BlockSpec((blk, d), lambda i: (i, 0)), _resident((n, d), lambda i: (0, 0)), tile] + [ANY] * m,
        out_specs=[tile, pl.BlockSpec((N_HEADS, None, 1, blk), lambda i: (0, i, 0, 0))] + [ANY] * m,
        out_shape=[S((t, n), F32), S((N_HEADS, t // blk, 1, blk), F32)] + _swap_shapes(swap),
        scratch_shapes=[pltpu.SemaphoreType.DMA((m,)), pltpu.SemaphoreType.DMA((m,))] if m else [],
        compiler_params=_params(1))(dh, w_o, o, *swap)
    return outs[0], outs[1], list(outs[2:])


def attention_bwd(name, q, k, v, do, lse, delta, parts=()):
    nh, t, _ = q.shape
    blk = ATT_BLOCK
    nq = t // blk
    n_pairs = nq * (nq + 1) // 2
    n = len(parts)
    scale = SCORE_SCALE

    def body(q_ref, k_ref, v_ref, do_ref, lse_ref, dl_ref, *refs):
        src = refs[:n]
        dq_out, dk_out, dv_out = refs[n:n + 3]
        dst = refs[n + 3:2 * n + 3]
        s_buf, dp_buf, p_buf, ds_buf, bias_ref, dq_ref, dk_ref, dv_ref = refs[2 * n + 3:2 * n + 11]
        if n:
            send_sems, recv_sems = refs[2 * n + 11:]
            _ride_along(scatter_ici_copies(src, dst, send_sems, recv_sems), (pl.program_id(0),), (nh,))

        @pl.when(pl.program_id(0) == 0)
        def _():
            bias_ref[...] = jnp.where(_chunk_mask_t(0, 0, blk, blk), 0.0, MASK_VALUE)

        dq_ref[...] = jnp.zeros_like(dq_ref)
        dk_ref[...] = jnp.zeros_like(dk_ref)
        dv_ref[...] = jnp.zeros_like(dv_ref)

        def rows(x):
            return pl.ds(pl.multiple_of(x * blk, blk), blk)

        def after(jb):
            j, b = jb
            wrap = b == nq - 1 - j
            return jnp.where(wrap, j + 1, j), jnp.where(wrap, 0, b + 1)

        def products(jb, slot):
            j, b = jb
            s_buf[slot] = lax.dot_general(k_ref[rows(j), :], q_ref[rows(j + b), :], _NT, preferred_element_type=F32)
            dp_buf[slot] = lax.dot_general(v_ref[rows(j), :], do_ref[rows(j + b), :].astype(BF16), _NT, preferred_element_type=F32)

        def softmax_bwd(jb, slot):
            j, b = jb
            s = s_buf[slot] + bias_ref[...] * (b == 0).astype(F32)
            p = jnp.exp2(s * SCORE_SCALE_LOG2 - lse_ref[j + b] * LOG2_E)
            p_buf[slot] = p.astype(BF16)
            ds_buf[slot] = (p * (dp_buf[slot] - dl_ref[j + b]) * scale).astype(BF16)

        def gradients(jb, slot):
            j, b = jb
            dv_ref[rows(j), :] += jnp.dot(p_buf[slot], do_ref[rows(j + b), :].astype(BF16), preferred_element_type=F32)
            dk_ref[rows(j), :] += jnp.dot(ds_buf[slot], q_ref[rows(j + b), :], preferred_element_type=F32)
            dq_ref[rows(j + b), :] += lax.dot_general(ds_buf[slot], k_ref[rows(j), :], _TN, preferred_element_type=F32)

        def step(state, slot):
            third, second, first = state
            gradients(third, slot)
            softmax_bwd(second, 1 - slot)
            products(first, slot)
            return second, first, after(first)

        zero = jnp.int32(0)
        pair0 = (zero, zero)
        products(pair0, 0)
        if n_pairs == 1:
            softmax_bwd(pair0, 0)
            gradients(pair0, 0)
        else:
            pair1 = after(pair0)
            products(pair1, 1)
            softmax_bwd(pair0, 0)
            steady = n_pairs - 2
            state = lax.fori_loop(0, steady // 2, lambda u, st: step(step(st, 0), 1), (pair0, pair1, after(pair1)))
            if steady % 2:
                state = step(state, 0)
            before_last, last_pair, _ = state
            last = (n_pairs - 1) % 2
            softmax_bwd(last_pair, last)
            gradients(before_last, 1 - last)
            gradients(last_pair, last)
        dq_out[...] = dq_ref[...].astype(BF16)
        dk_out[...] = dk_ref[...].astype(BF16)
        dv_out[...] = dv_ref[...].astype(BF16)

    head = lambda w: pl.BlockSpec((None, t, w), lambda h: (h, 0, 0))
    stats = pl.BlockSpec((None, nq, 1, blk), lambda h: (h, 0, 0, 0))
    outs = pl.pallas_call(
        body, name=name, grid=(nh,),
        in_specs=[head(QK), head(QK), head(VDIM), pl.BlockSpec((t, VDIM), lambda h: (0, h)), stats, stats] + [ANY] * n,
        out_specs=[head(QK), head(QK), head(VDIM)] + [ANY] * n,
        out_shape=[S((nh, t, QK), BF16), S((nh, t, QK), BF16), S((nh, t, VDIM), BF16)] + [S(p.shape, p.dtype) for p in parts],
        scratch_shapes=[pltpu.VMEM((2, blk, blk), F32), pltpu.VMEM((2, blk, blk), F32), pltpu.VMEM((2, blk, blk), BF16),
                        pltpu.VMEM((2, blk, blk), BF16), pltpu.VMEM((blk, blk), F32),
                        pltpu.VMEM((t, QK), F32), pltpu.VMEM((t, QK), F32), pltpu.VMEM((t, VDIM), F32)]
        + ([pltpu.SemaphoreType.DMA((n, 3)), pltpu.SemaphoreType.DMA((n, 3))] if n else []),
        compiler_params=_params(1, VMEM_LIMIT_WHOLE_HEAD))(q, k, v, do, lse, delta, *parts)
    return outs[0], outs[1], outs[2], list(outs[3:])


def _shift_down(u, s):
    rows = lax.broadcasted_iota(jnp.int32, u.shape, 0)
    return jnp.where(rows >= s, pltpu.roll(u, s, 0), 0.0)


def _shift_up(u, s):
    n = u.shape[0]
    rows = lax.broadcasted_iota(jnp.int32, u.shape, 0)
    return jnp.where(rows < n - s, pltpu.roll(u, n - s, 0), 0.0)


def _conv_specs(t, d, lanes):
    slab = lambda part: pl.BlockSpec((None, t, lanes), lambda j, part=part: (part, 0, j))
    return slab, pl.BlockSpec((3, lanes), lambda j: (0, j)), pl.BlockSpec((t, lanes), lambda j: (0, j))


def conv_bwd(name, bcx, w, dy):
    _, t, d = bcx.shape
    lanes = _tile(d, 128, 128)
    slab, w_spec, col = _conv_specs(t, d, lanes)

    def body(b_ref, c_ref, x_ref, w_ref, dy_ref, d_ref, dw_ref):
        c = c_ref[...].astype(F32)
        x = x_ref[...].astype(F32)
        dyv = dy_ref[...]
        u = c * x
        u1 = _shift_down(u, 1)
        u2 = _shift_down(u, 2)
        w0, w1, w2 = w_ref[0:1, :], w_ref[1:2, :], w_ref[2:3, :]
        d_ref[0] = (dyv * (w0 * u2 + w1 * u1 + w2 * u)).astype(BF16)
        duc = dyv * b_ref[...].astype(F32)
        dw_ref[0:1, :] = jnp.sum(duc * u2, axis=0, keepdims=True)
        dw_ref[1:2, :] = jnp.sum(duc * u1, axis=0, keepdims=True)
        dw_ref[2:3, :] = jnp.sum(duc * u, axis=0, keepdims=True)
        du = w2 * duc + w1 * _shift_up(duc, 1) + w0 * _shift_up(duc, 2)
        d_ref[1] = (du * x).astype(BF16)
        d_ref[2] = (du * c).astype(BF16)

    return pl.pallas_call(
        body, name=name, grid=(d // lanes,), in_specs=[slab(0), slab(1), slab(2), w_spec, col],
        out_specs=[pl.BlockSpec((3, t, lanes), lambda j: (0, 0, j)), w_spec], out_shape=[S((3, t, d), BF16), S((3, d), F32)],
        compiler_params=_params(1))(bcx, bcx, bcx, w, dy)


def _adamw_update(w, g, m, v):
    m_new = ADAM_B1 * m + (1.0 - ADAM_B1) * g
    v_new = ADAM_B2 * v + (1.0 - ADAM_B2) * (g * g)
    m_hat = m_new / (1.0 - ADAM_B1 ** ADAM_STEP)
    v_hat = v_new / (1.0 - ADAM_B2 ** ADAM_STEP)
    return -ADAM_LR * (m_hat / (jnp.sqrt(v_hat) + ADAM_EPS) + ADAM_WD * w), m_new, v_new


def adamw(name, w, g, m, v):
    r, c = w.shape
    tr = _tile(r, 512)

    def body(w_ref, g_ref, m_ref, v_ref, d_ref, mo_ref, vo_ref):
        d_ref[...], mo_ref[...], vo_ref[...] = _adamw_update(w_ref[...], g_ref[...], m_ref[...], v_ref[...])

    blk = pl.BlockSpec((tr, c), lambda i: (i, 0))
    return pl.pallas_call(
        body, name=name, grid=(r // tr,), in_specs=[blk] * 4, out_specs=[blk] * 3, out_shape=[S((r, c), F32)] * 3,
        compiler_params=_params(1))(w, g, m, v)


def adamw_swapped(name, wt, g, mt, vt):
    nl, c, r = wt.shape
    tr = _tile(r, 512, 128)
    nr = r // tr

    def body(w_ref, g_ref, m_ref, v_ref, go_ref, d_ref, mo_ref, vo_ref):
        gt = g_ref[...].T
        go_ref[...] = gt
        d_ref[...], mo_ref[...], vo_ref[...] = _adamw_update(w_ref[...], gt, m_ref[...], v_ref[...])

    swapped = pl.BlockSpec((None, c, tr), lambda l, i: (l, 0, i))
    return pl.pallas_call(
        body, name=name, grid=(nl, nr),
        in_specs=[swapped, pl.BlockSpec((tr, c), lambda l, i: (l * nr + i, 0)), swapped, swapped],
        out_specs=[swapped] * 4, out_shape=[S((nl, c, r), F32)] * 4, compiler_params=_params(2))(wt, g, mt, vt)


def _place():
    x, y, c = lax.axis_index("x"), lax.axis_index("y"), lax.axis_index("c")
    other_chips = [(1 - x, y), (x, 1 - y), (1 - x, 1 - y)]
    return x, y, c, other_chips


def _half(c, rows):
    return pl.ds(pl.multiple_of(c * (rows // 2), 16), rows // 2)


def gather_weight_shards(shards, small=()):
    n, ns = len(shards), len(small)

    def body(*refs):
        src = refs[:n]
        small_src = refs[n:n + ns]
        dst = refs[n + ns:2 * n + ns]
        small_dst = refs[2 * n + ns:2 * (n + ns)]
        send_sems, recv_sems, small_send, small_recv = refs[2 * (n + ns):]
        x, y, c, chips = _place()
        me = 2 * x + y
        sibling = (x, y, 1 - c)
        small_pairs = []
        for i in range(ns):
            for j, (px, py) in enumerate(chips):
                def whole(slot):
                    return pltpu.make_async_remote_copy(
                        src_ref=small_src[i], dst_ref=small_dst[i].at[slot], send_sem=small_send.at[i, j],
                        recv_sem=small_recv.at[i, j], device_id=(px, py, c), device_id_type=MESH)
                small_pairs.append((whole(me), whole(2 * px + py)))
        for outgoing, _ in small_pairs:
            outgoing.start()

        def copy(i, slot, half_of, sem, to, from_input=False):
            rows = _half(half_of, src[i].shape[0])
            return pltpu.make_async_remote_copy(
                src_ref=src[i].at[rows] if from_input else dst[i].at[slot, rows], dst_ref=dst[i].at[slot, rows],
                send_sem=send_sems.at[i, sem], recv_sem=recv_sems.at[i, sem], device_id=to, device_id_type=MESH)

        sent = []
        for i in range(n):
            for j, chip in enumerate(chips):
                sent.append(copy(i, me, c, j, (*chip, c), from_input=True))
                sent[-1].start()
        for i in range(n):
            for j, (px, py) in enumerate(chips):
                copy(i, 2 * px + py, c, j, sibling).wait_recv()
                sent.append(copy(i, 2 * px + py, c, 3 + j, sibling))
                sent[-1].start()
        for i in range(n):
            for j, (px, py) in enumerate(chips):
                copy(i, 2 * px + py, 1 - c, 3 + j, sibling).wait_recv()
        for cp in sent:
            cp.wait_send()
        for _, incoming in small_pairs:
            incoming.wait_recv()
        for outgoing, _ in small_pairs:
            outgoing.wait_send()

    outs = pl.pallas_call(
        body, name="gather_weight_shards", in_specs=[ANY] * (n + ns), out_specs=[ANY] * (n + ns),
        out_shape=[S((N_CHIPS,) + s.shape, s.dtype) for s in list(shards) + list(small)],
        scratch_shapes=[pltpu.SemaphoreType.DMA((n, 6)), pltpu.SemaphoreType.DMA((n, 6)),
                        pltpu.SemaphoreType.DMA((max(ns, 1), 3)), pltpu.SemaphoreType.DMA((max(ns, 1), 3))],
    )(*shards, *small)
    filled = _fill_own_slot(outs, [s[None] for s in list(shards) + list(small)])
    return filled[:n], filled[n:]


def gather_ici_copies(src, dst, send_sems, recv_sems):
    x, y, c, chips = _place()
    me = 2 * x + y
    pairs = []
    for i in range(len(src)):
        rows = _half(c, src[i].shape[0])
        for j, (px, py) in enumerate(chips):
            def copy(slot):
                return pltpu.make_async_remote_copy(
                    src_ref=src[i].at[rows], dst_ref=dst[i].at[slot, rows], send_sem=send_sems.at[i, j],
                    recv_sem=recv_sems.at[i, j], device_id=(px, py, c), device_id_type=MESH)
            pairs.append((copy(me), copy(2 * px + py)))
    return pairs


def scatter_ici_copies(src, dst, send_sems, recv_sems):
    x, y, c, chips = _place()
    me = 2 * x + y
    pairs = []
    for i in range(len(src)):
        for j, (px, py) in enumerate(chips):
            def copy(from_slot, to_slot):
                return pltpu.make_async_remote_copy(
                    src_ref=src[i].at[from_slot], dst_ref=dst[i].at[to_slot], send_sem=send_sems.at[i, j],
                    recv_sem=recv_sems.at[i, j], device_id=(px, py, c), device_id_type=MESH)
            pairs.append((copy(2 * px + py, me), copy(me, 2 * px + py)))
    return pairs


def _ride_along(pairs, grid_ids, grid_sizes, local=()):
    first = grid_ids[0] == 0
    last = grid_ids[0] == grid_sizes[0] - 1
    for g, size in zip(grid_ids[1:], grid_sizes[1:]):
        first = first & (g == 0)
        last = last & (g == size - 1)

    @pl.when(first)
    def _():
        for outgoing, _ in pairs:
            outgoing.start()
        for src, stage, _, sems in local:
            pltpu.make_async_copy(src, stage, sems.at[0]).start()

    @pl.when(last)
    def _():
        for _, incoming in pairs:
            incoming.wait_recv()
        for outgoing, _ in pairs:
            outgoing.wait_send()
        for src, stage, _, sems in local:
            pltpu.make_async_copy(src, stage, sems.at[0]).wait()
        placed = [pltpu.make_async_copy(stage, dst, sems.at[1]) for _, stage, dst, sems in local]
        for cp in placed:
            cp.start()
        for cp in placed:
            cp.wait()


def _fill_own_slot(gathered, own):
    me = 2 * lax.axis_index("x") + lax.axis_index("y")
    return [lax.dynamic_update_slice(g, o, (me,) + (0,) * (g.ndim - 1)) for g, o in zip(gathered, own)]


def forward_copies(src, dst, send_sems, recv_sems):
    x, y, c, chips = _place()
    pairs = []
    for i in range(len(src)):
        for j, (px, py) in enumerate(chips):
            def copy(half_of):
                rows = _half(half_of, src[i].shape[1])
                return pltpu.make_async_remote_copy(
                    src_ref=src[i].at[2 * px + py, rows], dst_ref=dst[i].at[2 * px + py, rows], send_sem=send_sems.at[i, j],
                    recv_sem=recv_sems.at[i, j], device_id=(x, y, 1 - c), device_id_type=MESH)
            pairs.append((copy(c), copy(1 - c)))
    return pairs


def attention_out_proj(name, attn, w_o, resid, next_gain, arriving, own):
    t, kdim = attn.shape
    n = w_o.shape[1]
    tm = _tile(t, 512)
    m = len(arriving)

    def body(x_ref, w_ref, r_ref, g_ref, *refs):
        src, own_refs = refs[:m], refs[m:2 * m]
        h_ref, a_ref = refs[2 * m:2 * m + 2]
        dst = refs[2 * m + 2:3 * m + 2]
        stages = refs[3 * m + 2:4 * m + 2]
        send_sems, recv_sems, local_sems = refs[4 * m + 2:]
        me = 2 * lax.axis_index("x") + lax.axis_index("y")
        placed = [(own_refs[i], stages[i], dst[i].at[me], local_sems.at[i]) for i in range(m)]
        _ride_along(forward_copies(src, dst, send_sems, recv_sems), (pl.program_id(0),), (t // tm,), placed)
        y = r_ref[...] + jnp.dot(x_ref[...].astype(BF16), w_ref[...], preferred_element_type=F32)
        h_ref[...] = y
        a_ref[...] = (y * _rstd(y) * g_ref[...]).astype(BF16)

    row = pl.BlockSpec((tm, n), lambda i: (i, 0))
    outs = pl.pallas_call(
        body, name=name, grid=(t // tm,),
        in_specs=[pl.BlockSpec((tm, kdim), lambda i: (i, 0)), _resident((kdim, n), lambda i: (0, 0)), row,
                  pl.BlockSpec((1, n), lambda i: (0, 0))] + [ANY] * (2 * m),
        out_specs=[row, row] + [ANY] * m, out_shape=[S((t, n), F32), S((t, n), BF16)] + [S(g.shape, g.dtype) for g in arriving],
        input_output_aliases={4 + i: 2 + i for i in range(m)},
        scratch_shapes=[pltpu.VMEM(o.shape, o.dtype) for o in own]
        + [pltpu.SemaphoreType.DMA((m, 3)), pltpu.SemaphoreType.DMA((m, 3)), pltpu.SemaphoreType.DMA((m, 2))],
        compiler_params=_params(1))(attn, w_o, resid, next_gain, *arriving, *own)
    return outs[0], outs[1], list(outs[2:])


def swap_copies(src, dst, send_sems, recv_sems):
    x, y, c, _ = _place()
    pairs = []
    for i in range(len(src)):
        cp = pltpu.make_async_remote_copy(
            src_ref=src[i].at[:, _half(1 - c, src[i].shape[1]), :], dst_ref=dst[i], send_sem=send_sems.at[i],
            recv_sem=recv_sems.at[i], device_id=(x, y, 1 - c), device_id_type=MESH)
        pairs.append((cp, cp))
    return pairs


def _swap_shapes(grads):
    return [S((g.shape[0], g.shape[1] // 2, g.shape[2]), g.dtype) for g in grads]


def sibling_swap_halves(name, grads):
    n = len(grads)

    def body(*refs):
        pairs = swap_copies(refs[:n], refs[n:2 * n], *refs[2 * n:])
        for outgoing, _ in pairs:
            outgoing.start()
        for _, incoming in pairs:
            incoming.wait_recv()
        for outgoing, _ in pairs:
            outgoing.wait_send()

    return pl.pallas_call(
        body, name=name, in_specs=[ANY] * n, out_specs=[ANY] * n, out_shape=_swap_shapes(grads),
        scratch_shapes=[pltpu.SemaphoreType.DMA((n,)), pltpu.SemaphoreType.DMA((n,))],
    )(*grads)


def add_halves(name, g, rx):
    _, r, cdim = g.shape
    r2 = r // 2
    tr = _tile(r2, 512, 16)
    nb = r2 // tr

    def body(lo_ref, hi_ref, rx_ref, o_ref):
        mine = jnp.where(lax.axis_index("c") == 0, lo_ref[...], hi_ref[...])
        o_ref[...] = (mine.astype(F32) + rx_ref[...].astype(F32)).astype(BF16)

    half = pl.BlockSpec((None, tr, cdim), lambda k, i: (k, i, 0))
    return pl.pallas_call(
        body, name=name, grid=(N_CHIPS, nb),
        in_specs=[half, pl.BlockSpec((None, tr, cdim), lambda k, i: (k, nb + i, 0)), half],
        out_specs=half, out_shape=S((N_CHIPS, r2, cdim), BF16), compiler_params=_params(2))(g, g, rx)


def sum_chips(name, arrived, mine):
    _, r2, cdim = arrived.shape
    tr = _tile(r2, 512, 16)

    def body(a_ref, m_ref, o_ref):
        me = 2 * lax.axis_index("x") + lax.axis_index("y")
        acc = jnp.zeros((tr, cdim), F32)
        for k in range(N_CHIPS):
            acc = acc + jnp.where(me == k, m_ref[k], a_ref[k]).astype(F32)
        o_ref[...] = acc

    slots = pl.BlockSpec((N_CHIPS, tr, cdim), lambda i: (0, i, 0))
    return pl.pallas_call(
        body, name=name, grid=(r2 // tr,), in_specs=[slots, slots],
        out_specs=pl.BlockSpec((tr, cdim), lambda i: (i, 0)), out_shape=S((r2, cdim), F32), compiler_params=_params(1))(arrived, mine)


def join_copies(src, dst, where, send_sems, recv_sems):
    x, y, c, _ = _place()
    pairs = []
    for i in range(len(src)):
        def copy(half_of):
            r2 = src[i].shape[0]
            rows = pl.ds(pl.multiple_of(where[i][1] + half_of * r2, 8), r2)
            return pltpu.make_async_remote_copy(
                src_ref=src[i], dst_ref=dst[where[i][0]].at[rows], send_sem=send_sems.at[i],
                recv_sem=recv_sems.at[i], device_id=(x, y, 1 - c), device_id_type=MESH)
        pairs.append((copy(c), copy(1 - c)))
    return pairs


def _fill_own_halves(targets, halves, where):
    targets = list(targets)
    c = lax.axis_index("c")
    for h, (tgt, first) in zip(halves, where):
        targets[tgt] = lax.dynamic_update_slice(targets[tgt], h, (first + c * h.shape[0], 0))
    return targets


def sibling_join_halves(name, halves, targets, where):
    n = len(halves)

    def body(*refs):
        pairs = join_copies(refs[:n], refs[n:n + len(targets)], where, *refs[n + len(targets):])
        for outgoing, _ in pairs:
            outgoing.start()
        for _, incoming in pairs:
            incoming.wait_recv()
        for outgoing, _ in pairs:
            outgoing.wait_send()

    outs = pl.pallas_call(
        body, name=name, in_specs=[ANY] * n, out_specs=[ANY] * len(targets), out_shape=[S(tg, F32) for tg in targets],
        scratch_shapes=[pltpu.SemaphoreType.DMA((n,)), pltpu.SemaphoreType.DMA((n,))],
    )(*halves)
    return _fill_own_halves(outs, halves, where)


def all_reduce_small(name, packed):
    rows, width = packed.shape

    def body(x_ref, o_ref, gathered, send_sems, recv_sems):
        x, y, c, _ = _place()
        me = 4 * x + 2 * y + c
        gathered[me] = x_ref[...]
        flips = [(fx, fy, fc) for fx in (0, 1) for fy in (0, 1) for fc in (0, 1)][1:]

        def copy(r, slot, to):
            return pltpu.make_async_remote_copy(
                src_ref=x_ref, dst_ref=gathered.at[slot], send_sem=send_sems.at[r], recv_sem=recv_sems.at[r],
                device_id=to, device_id_type=MESH)

        def peer(f):
            return (x ^ f[0], y ^ f[1], c ^ f[2])

        sent = [copy(r, me, peer(f)) for r, f in enumerate(flips)]
        for cp in sent:
            cp.start()
        for r, f in enumerate(flips):
            px, py, pc = peer(f)
            copy(r, 4 * px + 2 * py + pc, peer(f)).wait_recv()
        for cp in sent:
            cp.wait_send()
        acc = gathered[0]
        for k in range(1, N_DEV):
            acc = acc + gathered[k]
        o_ref[...] = acc

    vmem = pl.BlockSpec(memory_space=pltpu.VMEM)
    return pl.pallas_call(
        body, name=name, in_specs=[vmem], out_specs=vmem, out_shape=S((rows, width), F32),
        scratch_shapes=[pltpu.VMEM((N_DEV, rows, width), F32), pltpu.SemaphoreType.DMA((N_DEV - 1,)),
                        pltpu.SemaphoreType.DMA((N_DEV - 1,))],
    )(packed)


def _rope_tables(positions):
    inv_freq = 1.0 / (ROPE_THETA ** (jnp.arange(0, ROPE, 2, dtype=F32) / ROPE))
    ang = positions.astype(F32)[:, None] * inv_freq
    return jnp.cos(ang), jnp.sin(ang)


def _unstack_cols(w):
    k4, k, n4 = w.shape
    return jnp.transpose(w, (1, 0, 2)).reshape(k, k4 * n4)


def _stack_cols(w):
    k, n = w.shape
    return jnp.transpose(w.reshape(k, N_CHIPS, n // N_CHIPS), (1, 0, 2))


def kernel(x, positions, mla_norm, mla_w_in, mla_g_cq, mla_g_ckv, mla_w_uq, mla_w_ukv, mla_w_o, conv_norm, conv_w_in, conv_w, conv_w_out, ffn_norm, ffn_w_gate, ffn_w_up, ffn_w_down, final_norm, loss_target, m_mla_norm, m_mla_w_in, m_mla_g_cq, m_mla_g_ckv, m_mla_w_uq, m_mla_w_ukv, m_mla_w_o, m_conv_norm, m_conv_w_in, m_conv_w, m_conv_w_out, m_ffn_norm, m_ffn_w_gate, m_ffn_w_up, m_ffn_w_down, m_final_norm, v_mla_norm, v_mla_w_in, v_mla_g_cq, v_mla_g_ckv, v_mla_w_uq, v_mla_w_ukv, v_mla_w_o, v_conv_norm, v_conv_w_in, v_conv_w, v_conv_w_out, v_ffn_norm, v_ffn_w_gate, v_ffn_w_up, v_ffn_w_down, v_final_norm):
    weights = dict(mla_norm=mla_norm, mla_w_in=mla_w_in, mla_g_cq=mla_g_cq, mla_g_ckv=mla_g_ckv, mla_w_uq=mla_w_uq,
                   mla_w_ukv=mla_w_ukv, mla_w_o=mla_w_o, conv_norm=conv_norm, conv_w_in=conv_w_in, conv_w=conv_w,
                   conv_w_out=conv_w_out, ffn_norm=ffn_norm, ffn_w_gate=ffn_w_gate, ffn_w_up=ffn_w_up,
                   ffn_w_down=ffn_w_down, final_norm=final_norm)
    m_in = dict(mla_norm=m_mla_norm, mla_w_in=m_mla_w_in, mla_g_cq=m_mla_g_cq, mla_g_ckv=m_mla_g_ckv, mla_w_uq=m_mla_w_uq,
                mla_w_ukv=m_mla_w_ukv, mla_w_o=m_mla_w_o, conv_norm=m_conv_norm, conv_w_in=m_conv_w_in, conv_w=m_conv_w,
                conv_w_out=m_conv_w_out, ffn_norm=m_ffn_norm, ffn_w_gate=m_ffn_w_gate, ffn_w_up=m_ffn_w_up,
                ffn_w_down=m_ffn_w_down, final_norm=m_final_norm)
    v_in = dict(mla_norm=v_mla_norm, mla_w_in=v_mla_w_in, mla_g_cq=v_mla_g_cq, mla_g_ckv=v_mla_g_ckv, mla_w_uq=v_mla_w_uq,
                mla_w_ukv=v_mla_w_ukv, mla_w_o=v_mla_w_o, conv_norm=v_conv_norm, conv_w_in=v_conv_w_in, conv_w=v_conv_w,
                conv_w_out=v_conv_w_out, ffn_norm=v_ffn_norm, ffn_w_gate=v_ffn_w_gate, ffn_w_up=v_ffn_w_up,
                ffn_w_down=v_ffn_w_down, final_norm=v_final_norm)
    big = ["mla_w_in", "mla_w_uq", "mla_w_ukv", "mla_w_o", "conv_w_in", "conv_w_out", "ffn_w_gate", "ffn_w_up", "ffn_w_down"]
    order = list(weights)

    t, d = x.shape[1], x.shape[2]
    h0 = x.reshape(t, d)
    target = loss_target.reshape(t, d)
    cos, sin = _rope_tables(positions.reshape(t))

    def rows2d(a):
        return a.reshape(-1, a.shape[-1])

    first, later = big[:4], big[4:]
    shards = {n: rows2d(weights[n]).astype(BF16) for n in big}
    d4 = d // N_CHIPS
    first_gathered, (conv_norm_slots, conv_w_slots) = gather_weight_shards(
        [shards[n] for n in first], [conv_norm.reshape(1, d4), conv_w.reshape(3, d4)])
    gathered = dict(zip(first, first_gathered))
    conv_norm_full = conv_norm_slots.reshape(1, d)
    conv_w_full = jnp.transpose(conv_w_slots, (1, 0, 2)).reshape(3, d)
    w_in = gathered["mla_w_in"].reshape(-1, gathered["mla_w_in"].shape[-1])
    w_uq = _unstack_cols(gathered["mla_w_uq"])
    w_ukv = _unstack_cols(gathered["mla_w_ukv"])
    w_o = gathered["mla_w_o"].reshape(-1, d)

    chip = 2 * lax.axis_index("x") + lax.axis_index("y")

    def pack_rows(rows):
        idx = lax.broadcasted_iota(jnp.int32, (SMALL_ROWS, d), 0)
        out = jnp.zeros((SMALL_ROWS, d), F32)
        for r, row in enumerate(rows):
            out = out + jnp.where(idx == r, row, 0.0)
        return out


    a0 = rms_fwd("mla_norm_fwd", h0, mla_norm)
    proj, cq, ckv, kr = mla_in_proj("mla_in_proj", a0, w_in, mla_g_cq, mla_g_ckv, cos, sin)
    q = linear("mla_q_up", cq, w_uq, F32)
    kv = linear("mla_kv_up", ckv, w_ukv, BF16)
    qh, kh, vh, conv_arriving = qkv_heads("qkv_heads", q, kv, kr, cos, sin, [shards[n] for n in later[:2]])
    attn, lse, ffn_arriving = attention_fwd("attention_fwd", qh, kh, vh, [shards[n] for n in later[2:]])
    h1, a1, handed = attention_out_proj("mla_out_proj", attn, w_o, h0, ffn_norm[0:1], conv_arriving + ffn_arriving,
                                        [shards[n] for n in later])
    gathered.update(zip(later, handed))
    cw_in = _unstack_cols(gathered["conv_w_in"])
    cw_out = gathered["conv_w_out"].reshape(-1, d)
    wg_all, wu_all, wd_all = gathered["ffn_w_gate"], gathered["ffn_w_up"], gathered["ffn_w_down"]

    def ffn_forward(tag, h, a, layer, next_gain):
        g, u, z = ffn_up(f"ffn{tag}_up", a, wg_all, wu_all, layer)
        return g, u, z, ffn_down(f"ffn{tag}_down", z, wd_all, layer, h, next_gain)

    g0, u0, z0, (h2, a2) = ffn_forward(0, h1, a1, 0, conv_norm_full)
    bcx, yc = conv_in_proj("conv_in_proj", a2, cw_in, conv_w_full)
    h3, a3 = linear("conv_out_proj", yc, cw_out, F32, resid=h2, next_gain=ffn_norm[1:2])
    g1, u1, z1 = ffn_up("ffn1_up", a3, wg_all, wu_all, 1)
    dh4, d_final_norm, loss_local = ffn_down_loss("ffn1_down_loss", z1, wd_all, 1, h3, final_norm.reshape(1, d), target)

    def ffn_backward(tag, dh, h, layer, a, g, u, z, swap=()):
        dg, du, swapped = ffn_bwd_hidden(f"ffn{tag}_bwd_hidden", dh, wd_all, layer, g, u, swap)
        d_wd = ffn_wgrad_down(f"ffn{tag}_wgrad_down", z, dh)
        dh_prev, d_norm = ffn_bwd_input(f"ffn{tag}_bwd_input", dg, du, wg_all, wu_all, layer, h, ffn_norm[layer:layer + 1], dh)
        d_wg = ffn_wgrad_up(f"ffn{tag}_wgrad_gate", a, dg)
        d_wu = ffn_wgrad_up(f"ffn{tag}_wgrad_up", a, du)
        return dh_prev, d_norm, [d_wg, d_wu, d_wd], swapped

    def pair_sums(tag, local, from_sibling):
        return [add_halves(f"pair_sum_{tag}{i}", g, r) for i, (g, r) in enumerate(zip(local, from_sibling))]

    def sum_from_chips(tag, pairs, arrived):
        return [sum_chips(f"chip_sum_{tag}{i}", a, p) for i, (a, p) in enumerate(zip(arrived, pairs))]

    def shard_shape(n):
        return rows2d(weights[n]).shape

    dh3, d_ffn_norm1, ffn1_grads, _ = ffn_backward(1, dh4, h3, 1, a3, g1, u1, z1)

    dyc = linear_nt("conv_out_bwd_input", dh3, cw_out, F32)
    d_cw_out = wgrad("conv_out_wgrad", yc, dh3)
    dbcx, d_conv_w = conv_bwd("conv_bwd", bcx, conv_w_full, dyc)
    dh2, d_conv_norm = conv_in_bwd_input("conv_in_bwd_input", dbcx, cw_in, h2, conv_norm_full, dh3)
    d_cw_in = conv_in_wgrad("conv_in_wgrad", a2, dbcx)

    second = [d_cw_in, d_cw_out.reshape(N_CHIPS, -1, d)] + ffn1_grads
    dh1, d_ffn_norm0, ffn0_grads, second_swapped = ffn_backward(0, dh2, h1, 0, a1, g0, u0, z0, second)
    d_w_o = wgrad("mla_out_wgrad", attn, dh1)
    first_part = ffn0_grads + [d_w_o.reshape(N_CHIPS, -1, d)]
    d_attn, delta, first_swapped = attention_out_bwd("mla_out_bwd_input", dh1, w_o, attn, first_part)
    rest_pairs = pair_sums("rest", second + first_part, second_swapped + first_swapped)
    dqh, dkh, dvh, rest_arrived = attention_bwd("attention_bwd", qh, kh, vh, d_attn, lse, delta, rest_pairs)
    rd, rf = ffn0_grads[0].shape[1], ffn0_grads[2].shape[1]
    rest_where = [(0, 0), (1, 0), (2, rd), (3, rd), (4, rf), (2, 0), (3, 0), (4, 0), (5, 0)]
    rest_names = later + ["mla_w_o"]
    dq, dkv, dkr, rest_grads = qkv_heads_bwd("qkv_heads_bwd", dqh, dkh, dvh, cos, sin, sum_from_chips("rest", rest_pairs, rest_arrived),
                                              [shard_shape(n) for n in rest_names], rest_where)
    grads = dict(zip(rest_names, rest_grads))
    dcq = linear_nt("mla_q_up_bwd_input", dq, w_uq, F32)
    d_w_uq = wgrad("mla_q_up_wgrad", cq, dq)
    dckv = linear_nt("mla_kv_up_bwd_input", dkv, w_ukv, F32)
    d_w_ukv = wgrad("mla_kv_up_wgrad", ckv, dkv)
    dproj, d_g_cq, d_g_ckv = mla_mid_bwd("mla_mid_bwd", proj, mla_g_cq, mla_g_ckv, dcq, dckv, dkr, cos, sin)
    d_w_in = wgrad("mla_in_wgrad", a0, dproj)
    mla_local = [d_w_in.reshape(N_CHIPS, -1, d_w_in.shape[-1]), _stack_cols(d_w_uq), _stack_cols(d_w_ukv)]
    mla_pairs = pair_sums("mla", mla_local, sibling_swap_halves("sibling_swap_mla", mla_local))
    grad_x, d_mla_norm, mla_arrived = linear_nt_norm_bwd("mla_in_bwd_input", dproj, w_in, h0, mla_norm, dh1, mla_pairs)

    grads.update(zip(first[:3], sibling_join_halves("sibling_join_mla", sum_from_chips("mla", mla_pairs, mla_arrived),
                                                    [shard_shape(n) for n in first[:3]], [(i, 0) for i in range(3)])))

    def pad_row(v):
        return jnp.pad(v, ((0, 0), (0, d - v.shape[1])))

    small = all_reduce_small("all_reduce_small_grads", pack_rows([
        d_mla_norm, pad_row(d_g_cq), pad_row(d_g_ckv), d_ffn_norm0, d_ffn_norm1, d_final_norm, d_conv_norm,
        d_conv_w[0:1], d_conv_w[1:2], d_conv_w[2:3], jnp.broadcast_to(loss_local, (1, d))]))
    loss = small[10, 0]
    grads["mla_norm"] = small[0:1]
    grads["mla_g_cq"] = small[1:2, :mla_g_cq.shape[1]]
    grads["mla_g_ckv"] = small[2:3, :mla_g_ckv.shape[1]]
    grads["ffn_norm"] = small[3:5]
    grads["final_norm"] = small[5:6]
    grads["conv_norm"] = lax.dynamic_slice(small[6:7], (0, chip * d4), (1, d4))
    grads["conv_w"] = lax.dynamic_slice(small[7:10], (0, chip * d4), (3, d4))

    outs_g, outs_d, outs_m, outs_v = [], [], [], []
    for n in order:
        w = weights[n]
        if w.ndim == 3 and w.shape[2] % 128 and w.shape[1] % 128 == 0:
            results = adamw_swapped(f"adamw_{n}", jnp.swapaxes(w, 1, 2), grads[n].reshape(-1, w.shape[2]),
                                    jnp.swapaxes(m_in[n], 1, 2), jnp.swapaxes(v_in[n], 1, 2))
            grad_w, delta_w, new_m, new_v = [jnp.swapaxes(o, 1, 2) for o in results]
        else:
            delta_w, new_m, new_v = adamw(f"adamw_{n}", rows2d(w), grads[n].reshape(rows2d(w).shape), rows2d(m_in[n]), rows2d(v_in[n]))
            grad_w = grads[n]
        outs_g.append(grad_w.reshape(w.shape))
        outs_d.append(delta_w.reshape(w.shape))
        outs_m.append(new_m.reshape(w.shape))
        outs_v.append(new_v.reshape(w.shape))
    return (loss, grad_x.reshape(x.shape), *outs_g, *outs_d, *outs_m, *outs_v)
```

```python
import math

import jax
import jax.numpy as jnp
from jax import lax
from jax.experimental import pallas as pl
from jax.experimental.pallas import tpu as pltpu

F32 = jnp.float32
BF16 = jnp.bfloat16
S = jax.ShapeDtypeStruct

N_HEADS = 8
NOPE = 128
ROPE = 64
HALF = ROPE // 2
VDIM = 128
QK = NOPE + ROPE
CHUNK = 64
ROPE_THETA = 10000.0
RMS_EPS = 1e-6
ADAM_LR = 0.001
ADAM_B1 = 0.9
ADAM_B2 = 0.999
ADAM_EPS = 1e-08
ADAM_WD = 0.01
ADAM_STEP = 10

N_CHIPS = 4
N_DEV = 8
MASK_VALUE = -1e30
SCORE_SCALE = 1.0 / math.sqrt(QK)
LOG2_E = math.log2(math.e)
SCORE_SCALE_LOG2 = SCORE_SCALE * LOG2_E
VMEM_LIMIT = 48 * 1024 * 1024
VMEM_LIMIT_WHOLE_HEAD = 58 * 1024 * 1024
ATT_BLOCK = 512
CONV_SAVED_DTYPE = jnp.bfloat16
SMALL_ROWS = 16

_NN = (((1,), (0,)), ((), ()))
_NT = (((1,), (1,)), ((), ()))
_TN = (((0,), (0,)), ((), ()))
MESH = pl.DeviceIdType.MESH
ANY = pl.BlockSpec(memory_space=pl.ANY)


def _params(n_axes, vmem_limit=VMEM_LIMIT):
    return pltpu.CompilerParams(dimension_semantics=("arbitrary",) * n_axes, vmem_limit_bytes=vmem_limit)


def _tile(n, cap, mult=8):
    for t in range(min(cap, n), 0, -1):
        if n % t == 0 and t % mult == 0:
            return t
    return n


def _sigmoid(x):
    return 0.5 * jnp.tanh(0.5 * x) + 0.5


def _mm(name, a_ops, b_ops, products, dims, grid, k_axis, outs, acc_shape, epilogue, extra_ops=()):
    na, nb, ne, no = len(a_ops), len(b_ops), len(extra_ops), len(outs)
    n_acc = 1 + max(c for _, _, c in products)
    nk = 1 if k_axis is None else grid[k_axis]

    def body(*refs):
        a_refs = refs[:na]
        b_refs = refs[na:na + nb]
        e_refs = refs[na + nb:na + nb + ne]
        o_refs = refs[na + nb + ne:na + nb + ne + no]
        acc_refs = refs[na + nb + ne + no:]

        def partial_sums():
            vals = [None] * n_acc
            for ai, bi, ci in products:
                d = lax.dot_general(a_refs[ai][...].astype(BF16), b_refs[bi][...].astype(BF16), dims,
                                    preferred_element_type=F32)
                vals[ci] = d if vals[ci] is None else vals[ci] + d
            return vals

        if nk == 1:
            epilogue(partial_sums(), e_refs, o_refs)
        else:
            k = pl.program_id(k_axis)

            @pl.when(k == 0)
            def _():
                for acc in acc_refs:
                    acc[...] = jnp.zeros_like(acc)

            for acc, v in zip(acc_refs, partial_sums()):
                acc[...] += v

            @pl.when(k == nk - 1)
            def _():
                epilogue([acc[...] for acc in acc_refs], e_refs, o_refs)

    ops = list(a_ops) + list(b_ops) + list(extra_ops)
    return pl.pallas_call(
        body, name=name, grid=grid,
        in_specs=[s for _, s in ops], out_specs=[s for _, s in outs], out_shape=[o for o, _ in outs],
        scratch_shapes=[pltpu.VMEM(acc_shape, F32) for _ in range(n_acc if nk > 1 else 0)],
        compiler_params=_params(len(grid)),
    )(*[a for a, _ in ops])


def _store(accs, e_refs, o_refs):
    o_refs[0][...] = accs[0].astype(o_refs[0].dtype)


def linear(name, x, w, out_dtype, resid=None, next_gain=None):
    t, k = x.shape
    n = w.shape[1]
    tm = _tile(t, 512)
    tn = n if n <= 2048 else _tile(n, 1024, 128)
    tile = pl.BlockSpec((tm, tn), lambda j, i: (i, j))
    extra = [] if resid is None else [(resid, tile)]
    outs = [(S((t, n), out_dtype), tile)]
    if next_gain is not None:
        assert tn == n
        extra.append((next_gain, pl.BlockSpec((1, n), lambda j, i: (0, 0))))
        outs.append((S((t, n), BF16), tile))

    def epilogue(accs, e_refs, o_refs):
        y = accs[0] if resid is None else e_refs[0][...] + accs[0]
        o_refs[0][...] = y.astype(out_dtype)
        if next_gain is not None:
            o_refs[1][...] = (y * _rstd(y) * e_refs[-1][...]).astype(BF16)

    res = _mm(name, [(x, pl.BlockSpec((tm, k), lambda j, i: (i, 0)))], [(w, pl.BlockSpec((k, tn), lambda j, i: (0, j)))],
              [(0, 0, 0)], _NN, (n // tn, t // tm), None, outs, None, epilogue, extra)
    return res[0] if next_gain is None else res


def linear_nt(name, dy, w, out_dtype):
    t, n = dy.shape
    k = w.shape[0]
    tm = _tile(t, 512)
    tc = n if n <= 2048 else _tile(n, 1024, 128)
    return _mm(name, [(dy, pl.BlockSpec((tm, tc), lambda i, c: (i, c)))], [(w, pl.BlockSpec((k, tc), lambda i, c: (0, c)))],
               [(0, 0, 0)], _NT, (t // tm, n // tc), 1,
               [(S((t, k), out_dtype), pl.BlockSpec((tm, k), lambda i, c: (i, 0)))], (tm, k), _store)[0]


def wgrad(name, x, dy):
    t, k = x.shape
    n = dy.shape[1]
    tk = _tile(t, 512)
    tn = n if n <= 1024 else _tile(n, 1024, 128)
    return _mm(name, [(x, pl.BlockSpec((tk, k), lambda j, s: (s, 0)))], [(dy, pl.BlockSpec((tk, tn), lambda j, s: (s, j)))],
               [(0, 0, 0)], _TN, (n // tn, t // tk), 1,
               [(S((k, n), BF16), pl.BlockSpec((k, tn), lambda j, s: (0, j)))], (k, tn), _store)[0]


def _resident(shape, index_map):
    return pl.BlockSpec(shape, index_map, pipeline_mode=pl.Buffered(1))


def ffn_up(name, a, wg_all, wu_all, layer):
    t, d = a.shape
    f4 = wg_all.shape[2]
    tm = _tile(t, 512)
    w_spec = _resident((N_CHIPS, d, f4), lambda i: (0, layer, 0))
    h_spec = pl.BlockSpec((N_CHIPS, tm, f4), lambda i: (0, i, 0))

    def body(a_ref, wg_ref, wu_ref, zg_ref, zu_ref, z_ref):
        av = a_ref[...]
        for k in range(N_CHIPS):
            g = jnp.dot(av, wg_ref[k], preferred_element_type=F32)
            u = jnp.dot(av, wu_ref[k], preferred_element_type=F32)
            sg = _sigmoid(g)
            silu = g * sg
            zg_ref[k] = (u * (sg * (1.0 + g * (1.0 - sg)))).astype(BF16)
            zu_ref[k] = silu.astype(BF16)
            z_ref[k] = (silu * u).astype(BF16)

    return pl.pallas_call(
        body, name=name, grid=(t // tm,), in_specs=[pl.BlockSpec((tm, d), lambda i: (i, 0)), w_spec, w_spec],
        out_specs=[h_spec] * 3, out_shape=[S((N_CHIPS, t, f4), BF16)] * 3, compiler_params=_params(1))(a, wg_all, wu_all)


def ffn_down(name, z, wd_all, layer, resid, next_gain=None):
    _, t, f4 = z.shape
    d = wd_all.shape[2]
    tm = _tile(t, 512)
    row = pl.BlockSpec((tm, d), lambda i: (i, 0))
    normed = next_gain is not None

    def body(z_ref, wd_ref, r_ref, *refs):
        acc = r_ref[...]
        for k in range(N_CHIPS):
            acc = acc + jnp.dot(z_ref[k], wd_ref[k], preferred_element_type=F32)
        refs[-2 if normed else -1][...] = acc
        if normed:
            refs[-1][...] = (acc * _rstd(acc) * refs[0][...]).astype(BF16)

    res = pl.pallas_call(
        body, name=name, grid=(t // tm,),
        in_specs=[pl.BlockSpec((N_CHIPS, tm, f4), lambda i: (0, i, 0)), _resident((N_CHIPS, f4, d), lambda i: (0, layer, 0)), row]
        + ([pl.BlockSpec((1, d), lambda i: (0, 0))] if normed else []),
        out_specs=[row] * (2 if normed else 1), out_shape=[S((t, d), F32)] + ([S((t, d), BF16)] if normed else []),
        compiler_params=_params(1))(z, wd_all, resid, *([next_gain] if normed else []))
    return res if normed else res[0]


def ffn_bwd_hidden(name, dh, wd_all, layer, zg, zu, swap=()):
    t, d = dh.shape
    f4 = zg.shape[2]
    tm = _tile(t, 512)
    h_spec = pl.BlockSpec((N_CHIPS, tm, f4), lambda i: (0, i, 0))
    n = len(swap)

    def body(dh_ref, wd_ref, zg_ref, zu_ref, *refs):
        dg_ref, du_ref = refs[n:n + 2]
        if n:
            _ride_along(swap_copies(refs[:n], refs[n + 2:2 * n + 2], *refs[2 * n + 2:]), (pl.program_id(0),), (t // tm,))
        dhb = dh_ref[...].astype(BF16)
        for k in range(N_CHIPS):
            dz = lax.dot_general(dhb, wd_ref[k], _NT, preferred_element_type=F32)
            dg_ref[k] = (dz * zg_ref[k].astype(F32)).astype(BF16)
            du_ref[k] = (dz * zu_ref[k].astype(F32)).astype(BF16)

    outs = pl.pallas_call(
        body, name=name, grid=(t // tm,),
        in_specs=[pl.BlockSpec((tm, d), lambda i: (i, 0)), _resident((N_CHIPS, f4, d), lambda i: (0, layer, 0)), h_spec, h_spec]
        + [ANY] * n,
        out_specs=[h_spec] * 2 + [ANY] * n, out_shape=[S((N_CHIPS, t, f4), BF16)] * 2 + _swap_shapes(swap),
        scratch_shapes=[pltpu.SemaphoreType.DMA((n,)), pltpu.SemaphoreType.DMA((n,))] if n else [],
        compiler_params=_params(1))(dh, wd_all, zg, zu, *swap)
    return outs[0], outs[1], list(outs[2:])


def _norm_bwd_specs(tm, d):
    row = pl.BlockSpec((tm, d), lambda i: (i, 0))
    vec = pl.BlockSpec((1, d), lambda i: (0, 0))
    return [row, vec, row], [row, vec]


def _norm_bwd_tail(da, h_ref, g_ref, dhi_ref, dho_ref, dgain_ref):
    dx, dgain = _rms_bwd(h_ref[...], g_ref[...], da)
    dho_ref[...] = dhi_ref[...] + dx

    @pl.when(pl.program_id(0) == 0)
    def _():
        dgain_ref[...] = jnp.zeros_like(dgain_ref)

    dgain_ref[...] += dgain


def ffn_bwd_input(name, dg, du, wg_all, wu_all, layer, h, gain, dh_in):
    _, t, f4 = dg.shape
    d = h.shape[1]
    tm = _tile(t, 512)
    h_spec = pl.BlockSpec((N_CHIPS, tm, f4), lambda i: (0, i, 0))
    w_spec = _resident((N_CHIPS, d, f4), lambda i: (0, layer, 0))
    tail_in, tail_out = _norm_bwd_specs(tm, d)

    def body(dg_ref, du_ref, wg_ref, wu_ref, *tail):
        acc = jnp.zeros((tm, d), F32)
        for k in range(N_CHIPS):
            acc = acc + lax.dot_general(dg_ref[k], wg_ref[k], _NT, preferred_element_type=F32)
            acc = acc + lax.dot_general(du_ref[k], wu_ref[k], _NT, preferred_element_type=F32)
        _norm_bwd_tail(acc, *tail)

    return pl.pallas_call(
        body, name=name, grid=(t // tm,), in_specs=[h_spec, h_spec, w_spec, w_spec] + tail_in, out_specs=tail_out,
        out_shape=[S((t, d), F32), S((1, d), F32)], compiler_params=_params(1))(dg, du, wg_all, wu_all, h, gain, dh_in)


def ffn_wgrad_up(name, a, dy):
    t, d = a.shape
    f4 = dy.shape[2]
    tk = _tile(t, 512)
    nt = t // tk

    def body(a_ref, dy_ref, o_ref, acc):
        s = pl.program_id(0)

        @pl.when(s == 0)
        def _():
            acc[...] = jnp.zeros_like(acc)

        at = a_ref[...].T
        for k in range(N_CHIPS):
            acc[k] += jnp.dot(at, dy_ref[k], preferred_element_type=F32)

        @pl.when(s == nt - 1)
        def _():
            o_ref[...] = acc[...].astype(BF16)

    return pl.pallas_call(
        body, name=name, grid=(nt,),
        in_specs=[pl.BlockSpec((tk, d), lambda s: (s, 0)), pl.BlockSpec((N_CHIPS, tk, f4), lambda s: (0, s, 0))],
        out_specs=pl.BlockSpec((N_CHIPS, d, f4), lambda s: (0, 0, 0)), out_shape=S((N_CHIPS, d, f4), BF16),
        scratch_shapes=[pltpu.VMEM((N_CHIPS, d, f4), F32)], compiler_params=_params(1))(a, dy)


def ffn_wgrad_down(name, z, dh):
    _, t, f4 = z.shape
    d = dh.shape[1]
    tk = _tile(t, 512)
    nt = t // tk

    def body(z_ref, dh_ref, o_ref, acc):
        s = pl.program_id(0)

        @pl.when(s == 0)
        def _():
            acc[...] = jnp.zeros_like(acc)

        dhb = dh_ref[...].astype(BF16)
        for k in range(N_CHIPS):
            acc[k] += lax.dot_general(z_ref[k], dhb, _TN, preferred_element_type=F32)

        @pl.when(s == nt - 1)
        def _():
            o_ref[...] = acc[...].astype(BF16)

    return pl.pallas_call(
        body, name=name, grid=(nt,),
        in_specs=[pl.BlockSpec((N_CHIPS, tk, f4), lambda s: (0, s, 0)), pl.BlockSpec((tk, d), lambda s: (s, 0))],
        out_specs=pl.BlockSpec((N_CHIPS, f4, d), lambda s: (0, 0, 0)), out_shape=S((N_CHIPS, f4, d), BF16),
        scratch_shapes=[pltpu.VMEM((N_CHIPS, f4, d), F32)], compiler_params=_params(1))(z, dh)


def conv_in_proj(name, a, w, conv_w):
    t, d = a.shape
    tm = _tile(t, 256)
    keep = 8

    def body(a_ref, w_ref, cw_ref, bcx_ref, y_ref, u_ref):
        @pl.when(pl.program_id(0) == 0)
        def _():
            u_ref[0:keep, :] = jnp.zeros((keep, d), F32)

        av = a_ref[...]
        b, c, x = [jnp.dot(av, w_ref[:, j * d:(j + 1) * d], preferred_element_type=F32) for j in range(3)]
        for j, part in enumerate((b, c, x)):
            bcx_ref[j] = part.astype(bcx_ref.dtype)
        u_ref[keep:keep + tm, :] = c * x
        uc = (cw_ref[0:1, :] * u_ref[keep - 2:keep - 2 + tm, :] + cw_ref[1:2, :] * u_ref[keep - 1:keep - 1 + tm, :]
              + cw_ref[2:3, :] * u_ref[keep:keep + tm, :])
        y_ref[...] = (b * uc).astype(BF16)
        u_ref[0:keep, :] = u_ref[tm:tm + keep, :]

    return pl.pallas_call(
        body, name=name, grid=(t // tm,),
        in_specs=[pl.BlockSpec((tm, d), lambda i: (i, 0)), _resident((d, 3 * d), lambda i: (0, 0)), pl.BlockSpec((3, d), lambda i: (0, 0))],
        out_specs=[pl.BlockSpec((3, tm, d), lambda i: (0, i, 0)), pl.BlockSpec((tm, d), lambda i: (i, 0))],
        out_shape=[S((3, t, d), CONV_SAVED_DTYPE), S((t, d), BF16)], scratch_shapes=[pltpu.VMEM((tm + keep, d), F32)],
        compiler_params=_params(1))(a, w, conv_w)


def conv_in_bwd_input(name, dbcx, w, h, gain, dh_in):
    _, t, d = dbcx.shape
    tm = _tile(t, 512)
    tail_in, tail_out = _norm_bwd_specs(tm, d)

    def body(g_ref, w_ref, *tail):
        acc = jnp.zeros((tm, d), F32)
        for j in range(3):
            acc = acc + lax.dot_general(g_ref[j], w_ref[:, j * d:(j + 1) * d], _NT, preferred_element_type=F32)
        _norm_bwd_tail(acc, *tail)

    return pl.pallas_call(
        body, name=name, grid=(t // tm,),
        in_specs=[pl.BlockSpec((3, tm, d), lambda i: (0, i, 0)), _resident((d, 3 * d), lambda i: (0, 0))] + tail_in,
        out_specs=tail_out, out_shape=[S((t, d), F32), S((1, d), F32)], compiler_params=_params(1))(dbcx, w, h, gain, dh_in)


def linear_nt_norm_bwd(name, dy, w, h, gain, dh_in, parts=()):
    t, n = dy.shape
    k = w.shape[0]
    tm = _tile(t, 512)
    tail_in, tail_out = _norm_bwd_specs(tm, k)
    m = len(parts)

    def body(dy_ref, w_ref, h_ref, g_ref, dhi_ref, *refs):
        if m:
            _ride_along(scatter_ici_copies(refs[:m], refs[m + 2:2 * m + 2], *refs[2 * m + 2:]), (pl.program_id(0),), (t // tm,))
        da = lax.dot_general(dy_ref[...].astype(BF16), w_ref[...], _NT, preferred_element_type=F32)
        _norm_bwd_tail(da, h_ref, g_ref, dhi_ref, *refs[m:m + 2])

    outs = pl.pallas_call(
        body, name=name, grid=(t // tm,),
        in_specs=[pl.BlockSpec((tm, n), lambda i: (i, 0)), _resident((k, n), lambda i: (0, 0))] + tail_in + [ANY] * m,
        out_specs=tail_out + [ANY] * m, out_shape=[S((t, k), F32), S((1, k), F32)] + [S(p.shape, p.dtype) for p in parts],
        scratch_shapes=[pltpu.SemaphoreType.DMA((m, 3)), pltpu.SemaphoreType.DMA((m, 3))] if m else [],
        compiler_params=_params(1))(dy, w, h, gain, dh_in, *parts)
    return outs[0], outs[1], list(outs[2:])


def conv_in_wgrad(name, a, dbcx):
    t, d = a.shape
    tk = _tile(t, 512)
    nt = t // tk
    n4 = 3 * d // N_CHIPS

    def body(a_ref, g_ref, o_ref, acc):
        s = pl.program_id(0)

        @pl.when(s == 0)
        def _():
            acc[...] = jnp.zeros_like(acc)

        at = a_ref[...].T
        for j in range(3):
            acc[:, j * d:(j + 1) * d] += jnp.dot(at, g_ref[j], preferred_element_type=F32)

        @pl.when(s == nt - 1)
        def _():
            for k in range(N_CHIPS):
                o_ref[k] = acc[:, k * n4:(k + 1) * n4].astype(BF16)

    return pl.pallas_call(
        body, name=name, grid=(nt,),
        in_specs=[pl.BlockSpec((tk, d), lambda s: (s, 0)), pl.BlockSpec((3, tk, d), lambda s: (0, s, 0))],
        out_specs=pl.BlockSpec((N_CHIPS, d, n4), lambda s: (0, 0, 0)), out_shape=S((N_CHIPS, d, n4), BF16),
        scratch_shapes=[pltpu.VMEM((d, 3 * d), F32)], compiler_params=_params(1))(a, dbcx)


def _rstd(x):
    return lax.rsqrt(jnp.mean(x * x, axis=-1, keepdims=True) + RMS_EPS)


def _rms_bwd(x, g, dy):
    r = _rstd(x)
    xhat = x * r
    dgain = jnp.sum(dy * xhat, axis=0, keepdims=True)
    dxh = dy * g
    dx = r * (dxh - xhat * jnp.mean(dxh * xhat, axis=-1, keepdims=True))
    return dx, dgain


def rms_fwd(name, h, g):
    t, d = h.shape
    tr = _tile(t, 512)

    def body(h_ref, g_ref, a_ref):
        x = h_ref[...]
        a_ref[...] = (x * _rstd(x) * g_ref[...]).astype(BF16)

    return pl.pallas_call(
        body, name=name, grid=(t // tr,),
        in_specs=[pl.BlockSpec((tr, d), lambda i: (i, 0)), pl.BlockSpec((1, d), lambda i: (0, 0))],
        out_specs=pl.BlockSpec((tr, d), lambda i: (i, 0)), out_shape=S((t, d), BF16), compiler_params=_params(1))(h, g)


def ffn_down_loss(name, z, wd_all, layer, resid, gain, target):
    _, t, f4 = z.shape
    d = wd_all.shape[2]
    tm = _tile(t, 512)

    def body(z_ref, wd_ref, r_ref, g_ref, t_ref, dh_ref, dg_ref, loss_ref):
        x = r_ref[...]
        for k in range(N_CHIPS):
            x = x + jnp.dot(z_ref[k], wd_ref[k], preferred_element_type=F32)
        g = g_ref[...]
        r = _rstd(x)
        xhat = x * r
        err = xhat * g - t_ref[...]
        dy = err * (1.0 / d)
        dxh = dy * g
        dh_ref[...] = r * (dxh - xhat * jnp.mean(dxh * xhat, axis=-1, keepdims=True))

        @pl.when(pl.program_id(0) == 0)
        def _():
            dg_ref[...] = jnp.zeros_like(dg_ref)
            loss_ref[...] = jnp.zeros_like(loss_ref)

        dg_ref[...] += jnp.sum(dy * xhat, axis=0, keepdims=True)
        per_token = jnp.mean(err * err, axis=-1, keepdims=True)
        loss_ref[...] += 0.5 * jnp.sum(per_token, axis=0, keepdims=True)

    row = pl.BlockSpec((tm, d), lambda i: (i, 0))
    vec = pl.BlockSpec((1, d), lambda i: (0, 0))
    one = pl.BlockSpec((1, 1), lambda i: (0, 0))
    return pl.pallas_call(
        body, name=name, grid=(t // tm,),
        in_specs=[pl.BlockSpec((N_CHIPS, tm, f4), lambda i: (0, i, 0)), _resident((N_CHIPS, f4, d), lambda i: (0, layer, 0)), row, vec, row],
        out_specs=[row, vec, one], out_shape=[S((t, d), F32), S((1, d), F32), S((1, 1), F32)],
        compiler_params=_params(1))(z, wd_all, resid, gain, target)


def mla_in_proj(name, a, w, g_cq, g_ckv, cos, sin):
    t, d = a.shape
    n = w.shape[1]
    ql, kl = g_cq.shape[1], g_ckv.shape[1]
    tr = _tile(t, 512)

    def body(a_ref, w_ref, gq_ref, gk_ref, c_ref, s_ref, p_ref, cq_ref, ckv_ref, kr_ref):
        p_ref[...] = jnp.dot(a_ref[...], w_ref[...], preferred_element_type=F32)
        xq = p_ref[:, 0:ql]
        cq_ref[...] = (xq * _rstd(xq) * gq_ref[...]).astype(BF16)
        xk = p_ref[:, ql:ql + kl]
        ckv_ref[...] = (xk * _rstd(xk) * gk_ref[...]).astype(BF16)
        k1 = p_ref[:, ql + kl:ql + kl + HALF]
        k2 = p_ref[:, ql + kl + HALF:ql + kl + ROPE]
        c = c_ref[...]
        s = s_ref[...]
        kr_ref[:, 0:HALF] = k1 * c - k2 * s
        kr_ref[:, HALF:ROPE] = k1 * s + k2 * c

    def row(w):
        return pl.BlockSpec((tr, w), lambda i: (i, 0))

    def vec(w):
        return pl.BlockSpec((1, w), lambda i: (0, 0))

    return pl.pallas_call(
        body, name=name, grid=(t // tr,),
        in_specs=[row(d), _resident((d, n), lambda i: (0, 0)), vec(ql), vec(kl), row(HALF), row(HALF)],
        out_specs=[row(n), row(ql), row(kl), row(ROPE)],
        out_shape=[S((t, n), F32), S((t, ql), BF16), S((t, kl), BF16), S((t, ROPE), F32)],
        compiler_params=_params(1))(a, w, g_cq, g_ckv, cos, sin)


def mla_mid_bwd(name, proj, g_cq, g_ckv, dcq, dckv, dkr, cos, sin):
    t, n = proj.shape
    ql, kl = g_cq.shape[1], g_ckv.shape[1]
    tr = _tile(t, 512)

    def body(p_ref, gq_ref, gk_ref, dcq_ref, dckv_ref, dkr_ref, c_ref, s_ref, dp_ref, dgq_ref, dgk_ref):
        dxq, dgq = _rms_bwd(p_ref[:, 0:ql], gq_ref[...], dcq_ref[...])
        dp_ref[:, 0:ql] = dxq.astype(BF16)
        dxk, dgk = _rms_bwd(p_ref[:, ql:ql + kl], gk_ref[...], dckv_ref[...])
        dp_ref[:, ql:ql + kl] = dxk.astype(BF16)
        d1 = dkr_ref[:, 0:HALF]
        d2 = dkr_ref[:, HALF:ROPE]
        c = c_ref[...]
        s = s_ref[...]
        dp_ref[:, ql + kl:ql + kl + HALF] = (d1 * c + d2 * s).astype(BF16)
        dp_ref[:, ql + kl + HALF:ql + kl + ROPE] = (d2 * c - d1 * s).astype(BF16)

        @pl.when(pl.program_id(0) == 0)
        def _():
            dgq_ref[...] = jnp.zeros_like(dgq_ref)
            dgk_ref[...] = jnp.zeros_like(dgk_ref)

        dgq_ref[...] += dgq
        dgk_ref[...] += dgk

    def row(w):
        return pl.BlockSpec((tr, w), lambda i: (i, 0))

    def vec(w):
        return pl.BlockSpec((1, w), lambda i: (0, 0))

    return pl.pallas_call(
        body, name=name, grid=(t // tr,),
        in_specs=[row(n), vec(ql), vec(kl), row(ql), row(kl), row(ROPE), row(HALF), row(HALF)],
        out_specs=[row(n), vec(ql), vec(kl)], out_shape=[S((t, n), BF16), S((1, ql), F32), S((1, kl), F32)],
        compiler_params=_params(1))(proj, g_cq, g_ckv, dcq, dckv, dkr, cos, sin)


def qkv_heads(name, q, kv, kr, cos, sin, shards=()):
    t = q.shape[0]
    tr = _tile(t, 256)
    n = len(shards)

    def body(q_ref, kv_ref, kr_ref, c_ref, s_ref, *refs):
        src = refs[:n]
        qo_ref, ko_ref, vo_ref = refs[n:n + 3]
        if n:
            _ride_along(gather_ici_copies(src, refs[n + 3:2 * n + 3], *refs[2 * n + 3:]), (pl.program_id(0),), (t // tr,))
        c = c_ref[...]
        s = s_ref[...]
        krb = kr_ref[...].astype(BF16)
        for h in range(N_HEADS):
            q0 = h * QK
            qo_ref[h, :, 0:NOPE] = q_ref[:, q0:q0 + NOPE].astype(BF16)
            q1 = q_ref[:, q0 + NOPE:q0 + NOPE + HALF]
            q2 = q_ref[:, q0 + NOPE + HALF:q0 + QK]
            qo_ref[h, :, NOPE:NOPE + HALF] = (q1 * c - q2 * s).astype(BF16)
            qo_ref[h, :, NOPE + HALF:QK] = (q1 * s + q2 * c).astype(BF16)
            k0 = h * (NOPE + VDIM)
            ko_ref[h, :, 0:NOPE] = kv_ref[:, k0:k0 + NOPE]
            ko_ref[h, :, NOPE:QK] = krb
            vo_ref[h] = kv_ref[:, k0 + NOPE:k0 + NOPE + VDIM]

    def row(w):
        return pl.BlockSpec((tr, w), lambda i: (i, 0))

    def heads(w):
        return pl.BlockSpec((N_HEADS, tr, w), lambda i: (0, i, 0))

    outs = pl.pallas_call(
        body, name=name, grid=(t // tr,),
        in_specs=[row(N_HEADS * QK), row(N_HEADS * (NOPE + VDIM)), row(ROPE), row(HALF), row(HALF)] + [ANY] * n,
        out_specs=[heads(QK), heads(QK), heads(VDIM)] + [ANY] * n,
        out_shape=[S((N_HEADS, t, QK), BF16), S((N_HEADS, t, QK), BF16), S((N_HEADS, t, VDIM), BF16)]
        + [S((N_CHIPS,) + s.shape, s.dtype) for s in shards],
        scratch_shapes=[pltpu.SemaphoreType.DMA((n, 3)), pltpu.SemaphoreType.DMA((n, 3))] if n else [],
        compiler_params=_params(1))(q, kv, kr, cos, sin, *shards)
    return outs[0], outs[1], outs[2], list(outs[3:])


def qkv_heads_bwd(name, dq_h, dk_h, dv_h, cos, sin, halves=(), targets=(), where=()):
    t = dq_h.shape[1]
    tr = _tile(t, 256)
    n, nt = len(halves), len(targets)

    def body(dq_ref, dk_ref, dv_ref, c_ref, s_ref, *refs):
        q_ref, kv_ref, kr_ref = refs[n:n + 3]
        if n:
            src, dst = refs[:n], refs[n + 3:n + 3 + nt]
            stages = refs[n + 3 + nt:2 * n + 3 + nt]
            send_sems, recv_sems, local_sems = refs[2 * n + 3 + nt:]
            c = lax.axis_index("c")
            own = [(src[i], stages[i],
                    dst[where[i][0]].at[pl.ds(pl.multiple_of(where[i][1] + c * src[i].shape[0], 8), src[i].shape[0])],
                    local_sems.at[i]) for i in range(n)]
            _ride_along(join_copies(src, dst, where, send_sems, recv_sems), (pl.program_id(0),), (t // tr,), own)
        c = c_ref[...]
        s = s_ref[...]
        dkr = jnp.zeros((tr, ROPE), F32)
        for h in range(N_HEADS):
            q0 = h * QK
            q_ref[:, q0:q0 + NOPE] = dq_ref[h, :, 0:NOPE].astype(BF16)
            d1 = dq_ref[h, :, NOPE:NOPE + HALF]
            d2 = dq_ref[h, :, NOPE + HALF:QK]
            q_ref[:, q0 + NOPE:q0 + NOPE + HALF] = (d1 * c + d2 * s).astype(BF16)
            q_ref[:, q0 + NOPE + HALF:q0 + QK] = (d2 * c - d1 * s).astype(BF16)
            k0 = h * (NOPE + VDIM)
            kv_ref[:, k0:k0 + NOPE] = dk_ref[h, :, 0:NOPE].astype(BF16)
            kv_ref[:, k0 + NOPE:k0 + NOPE + VDIM] = dv_ref[h].astype(BF16)
            dkr = dkr + dk_ref[h, :, NOPE:QK]
        kr_ref[...] = dkr

    def row(w):
        return pl.BlockSpec((tr, w), lambda i: (i, 0))

    def heads(w):
        return pl.BlockSpec((N_HEADS, tr, w), lambda i: (0, i, 0))

    outs = pl.pallas_call(
        body, name=name, grid=(t // tr,),
        in_specs=[heads(QK), heads(QK), heads(VDIM), row(HALF), row(HALF)] + [ANY] * n,
        out_specs=[row(N_HEADS * QK), row(N_HEADS * (NOPE + VDIM)), row(ROPE)] + [ANY] * nt,
        out_shape=[S((t, N_HEADS * QK), BF16), S((t, N_HEADS * (NOPE + VDIM)), BF16), S((t, ROPE), F32)]
        + [S(tg, F32) for tg in targets],
        scratch_shapes=[pltpu.VMEM(h.shape, h.dtype) for h in halves]
        + ([pltpu.SemaphoreType.DMA((n,)), pltpu.SemaphoreType.DMA((n,)), pltpu.SemaphoreType.DMA((n, 2))] if n else []),
        compiler_params=_params(1))(dq_h, dk_h, dv_h, cos, sin, *halves)
    return outs[0], outs[1], outs[2], list(outs[3:])


def _chunk_mask_t(q_start, k_start, bq, bk):
    kc = (k_start + lax.broadcasted_iota(jnp.int32, (bk, bq), 0)) // CHUNK
    qc = (q_start + lax.broadcasted_iota(jnp.int32, (bk, bq), 1)) // CHUNK
    return kc <= qc


def attention_fwd(name, q, k, v, shards=()):
    nh, t, _ = q.shape
    blk = ATT_BLOCK
    nq = t // blk
    n = len(shards)

    def body(q_ref, k_ref, v_ref, *refs):
        src = refs[:n]
        o_ref, lse_ref = refs[n:n + 2]
        dst = refs[n + 2:2 * n + 2]
        m_ref, l_ref, acc_ref, s_buf, p_buf, alpha_buf, bias_ref = refs[2 * n + 2:2 * n + 9]
        i = pl.program_id(1)
        if n:
            send_sems, recv_sems = refs[2 * n + 9:]
            _ride_along(gather_ici_copies(src, dst, send_sems, recv_sems), (pl.program_id(0), i), (nh, nq))

        @pl.when((pl.program_id(0) == 0) & (i == 0))
        def _():
            bias_ref[...] = jnp.where(_chunk_mask_t(0, 0, blk, blk), 0.0, MASK_VALUE)

        m_ref[...] = jnp.full_like(m_ref, MASK_VALUE)
        l_ref[...] = jnp.zeros_like(l_ref)
        acc_ref[...] = jnp.zeros_like(acc_ref)

        def rows(b):
            return pl.ds(pl.multiple_of(b * blk, blk), blk)

        def scores(b, slot):
            s_buf[slot] = lax.dot_general(k_ref[rows(b), :], q_ref[...], _NT, preferred_element_type=F32)

        def softmax(slot, diagonal):
            s = s_buf[slot]
            if diagonal:
                s = s + bias_ref[...]
            m_old = m_ref[...]
            m_new = jnp.maximum(m_old, jnp.max(s, axis=0, keepdims=True))
            p = jnp.exp2((s - m_new) * SCORE_SCALE_LOG2)
            alpha = jnp.exp2((m_old - m_new) * SCORE_SCALE_LOG2)
            l_ref[...] = alpha * l_ref[...] + jnp.sum(p, axis=0, keepdims=True)
            m_ref[...] = m_new
            alpha_buf[slot] = alpha
            p_buf[slot] = p.astype(BF16)

        def values(b, slot):
            pv = lax.dot_general(v_ref[rows(b), :], p_buf[slot], _TN, preferred_element_type=F32)
            acc_ref[...] = alpha_buf[slot] * acc_ref[...] + pv

        def step(t, slot):
            values(t - 2, slot)
            softmax(1 - slot, False)
            scores(t, slot)

        scores(0, 0)

        @pl.when(i == 0)
        def _():
            softmax(0, True)
            values(0, 0)

        @pl.when(i > 0)
        def _():
            scores(1, 1)
            softmax(0, False)
            steady = i - 1

            def pair(u, carry):
                step(2 + 2 * u, 0)
                step(3 + 2 * u, 1)
                return carry

            lax.fori_loop(0, steady // 2, pair, 0)

            @pl.when(steady % 2 == 1)
            def _():
                step(i, 0)

            last = i % 2
            softmax(last, True)
            values(i - 1, 1 - last)
            values(i, last)

        l = l_ref[...]
        o_ref[...] = (acc_ref[...] / l).T
        lse_ref[...] = m_ref[...] * SCORE_SCALE + jnp.log(l)

    outs = pl.pallas_call(
        body, name=name, grid=(nh, nq),
        in_specs=[pl.BlockSpec((None, blk, QK), lambda h, i: (h, i, 0)), pl.BlockSpec((None, t, QK), lambda h, i: (h, 0, 0)),
                  pl.BlockSpec((None, t, VDIM), lambda h, i: (h, 0, 0))] + [ANY] * n,
        out_specs=[pl.BlockSpec((blk, VDIM), lambda h, i: (i, h)),
                   pl.BlockSpec((None, None, 1, blk), lambda h, i: (h, i, 0, 0))] + [ANY] * n,
        out_shape=[S((t, nh * VDIM), F32), S((nh, nq, 1, blk), F32)] + [S((N_CHIPS,) + s.shape, s.dtype) for s in shards],
        scratch_shapes=[pltpu.VMEM((1, blk), F32), pltpu.VMEM((1, blk), F32), pltpu.VMEM((VDIM, blk), F32),
                        pltpu.VMEM((2, blk, blk), F32), pltpu.VMEM((2, blk, blk), BF16), pltpu.VMEM((2, 1, blk), F32),
                        pltpu.VMEM((blk, blk), F32)]
        + ([pltpu.SemaphoreType.DMA((n, 3)), pltpu.SemaphoreType.DMA((n, 3))] if n else []),
        compiler_params=_params(2))(q, k, v, *shards)
    return outs[0], outs[1], list(outs[2:])


def attention_out_bwd(name, dh, w_o, o, swap=()):
    t, d = dh.shape
    n = w_o.shape[0]
    blk = ATT_BLOCK
    m = len(swap)

    def body(dh_ref, w_ref, o_ref, *refs):
        do_ref, d_ref = refs[m:m + 2]
        if m:
            _ride_along(swap_copies(refs[:m], refs[m + 2:2 * m + 2], *refs[2 * m + 2:]), (pl.program_id(0),), (t // blk,))
        do_ref[...] = lax.dot_general(dh_ref[...].astype(BF16), w_ref[...], _NT, preferred_element_type=F32)
        for h in range(N_HEADS):
            cols = slice(h * VDIM, (h + 1) * VDIM)
            d_ref[h] = jnp.sum((do_ref[:, cols] * o_ref[:, cols]).T, axis=0, keepdims=True)

    tile = pl.BlockSpec((blk, n), lambda i: (i, 0))
    outs = pl.pallas_call(
        body, name=name, grid=(t // blk,),
        in_specs=[pl.BlockSpec((blk, d), lambda i: (i, 0)), _resident((n, d), lambda i: (0, 0)), tile] + [ANY] * m,
        out_specs=[tile, pl.BlockSpec((N_HEADS, None, 1, blk), lambda i: (0, i, 0, 0))] + [ANY] * m,
        out_shape=[S((t, n), F32), S((N_HEADS, t // blk, 1, blk), F32)] + _swap_shapes(swap),
        scratch_shapes=[pltpu.SemaphoreType.DMA((m,)), pltpu.SemaphoreType.DMA((m,))] if m else [],
        compiler_params=_params(1))(dh, w_o, o, *swap)
    return outs[0], outs[1], list(outs[2:])


def attention_bwd(name, q, k, v, do, lse, delta, parts=()):
    nh, t, _ = q.shape
    blk = ATT_BLOCK
    nq = t // blk
    n_pairs = nq * (nq + 1) // 2
    n = len(parts)
    scale = SCORE_SCALE

    def body(q_ref, k_ref, v_ref, do_ref, lse_ref, dl_ref, *refs):
        src = refs[:n]
        dq_out, dk_out, dv_out = refs[n:n + 3]
        dst = refs[n + 3:2 * n + 3]
        s_buf, dp_buf, p_buf, ds_buf, bias_ref, dq_ref, dk_ref, dv_ref = refs[2 * n + 3:2 * n + 11]
        if n:
            send_sems, recv_sems = refs[2 * n + 11:]
            _ride_along(scatter_ici_copies(src, dst, send_sems, recv_sems), (pl.program_id(0),), (nh,))

        @pl.when(pl.program_id(0) == 0)
        def _():
            bias_ref[...] = jnp.where(_chunk_mask_t(0, 0, blk, blk), 0.0, MASK_VALUE)

        dq_ref[...] = jnp.zeros_like(dq_ref)
        dk_ref[...] = jnp.zeros_like(dk_ref)
        dv_ref[...] = jnp.zeros_like(dv_ref)

        def rows(x):
            return pl.ds(pl.multiple_of(x * blk, blk), blk)

        def after(jb):
            j, b = jb
            wrap = b == nq - 1 - j
            return jnp.where(wrap, j + 1, j), jnp.where(wrap, 0, b + 1)

        def products(jb, slot):
            j, b = jb
            s_buf[slot] = lax.dot_general(k_ref[rows(j), :], q_ref[rows(j + b), :], _NT, preferred_element_type=F32)
            dp_buf[slot] = lax.dot_general(v_ref[rows(j), :], do_ref[rows(j + b), :].astype(BF16), _NT, preferred_element_type=F32)

        def softmax_bwd(jb, slot):
            j, b = jb
            s = s_buf[slot] + bias_ref[...] * (b == 0).astype(F32)
            p = jnp.exp2(s * SCORE_SCALE_LOG2 - lse_ref[j + b] * LOG2_E)
            p_buf[slot] = p.astype(BF16)
            ds_buf[slot] = (p * (dp_buf[slot] - dl_ref[j + b]) * scale).astype(BF16)

        def gradients(jb, slot):
            j, b = jb
            dv_ref[rows(j), :] += jnp.dot(p_buf[slot], do_ref[rows(j + b), :].astype(BF16), preferred_element_type=F32)
            dk_ref[rows(j), :] += jnp.dot(ds_buf[slot], q_ref[rows(j + b), :], preferred_element_type=F32)
            dq_ref[rows(j + b), :] += lax.dot_general(ds_buf[slot], k_ref[rows(j), :], _TN, preferred_element_type=F32)

        def step(state, slot):
            third, second, first = state
            gradients(third, slot)
            softmax_bwd(second, 1 - slot)
            products(first, slot)
            return second, first, after(first)

        zero = jnp.int32(0)
        pair0 = (zero, zero)
        products(pair0, 0)
        if n_pairs == 1:
            softmax_bwd(pair0, 0)
            gradients(pair0, 0)
        else:
            pair1 = after(pair0)
            products(pair1, 1)
            softmax_bwd(pair0, 0)
            steady = n_pairs - 2
            state = lax.fori_loop(0, steady // 2, lambda u, st: step(step(st, 0), 1), (pair0, pair1, after(pair1)))
            if steady % 2:
                state = step(state, 0)
            before_last, last_pair, _ = state
            last = (n_pairs - 1) % 2
            softmax_bwd(last_pair, last)
            gradients(before_last, 1 - last)
            gradients(last_pair, last)
        dq_out[...] = dq_ref[...].astype(BF16)
        dk_out[...] = dk_ref[...].astype(BF16)
        dv_out[...] = dv_ref[...].astype(BF16)

    head = lambda w: pl.BlockSpec((None, t, w), lambda h: (h, 0, 0))
    stats = pl.BlockSpec((None, nq, 1, blk), lambda h: (h, 0, 0, 0))
    outs = pl.pallas_call(
        body, name=name, grid=(nh,),
        in_specs=[head(QK), head(QK), head(VDIM), pl.BlockSpec((t, VDIM), lambda h: (0, h)), stats, stats] + [ANY] * n,
        out_specs=[head(QK), head(QK), head(VDIM)] + [ANY] * n,
        out_shape=[S((nh, t, QK), BF16), S((nh, t, QK), BF16), S((nh, t, VDIM), BF16)] + [S(p.shape, p.dtype) for p in parts],
        scratch_shapes=[pltpu.VMEM((2, blk, blk), F32), pltpu.VMEM((2, blk, blk), F32), pltpu.VMEM((2, blk, blk), BF16),
                        pltpu.VMEM((2, blk, blk), BF16), pltpu.VMEM((blk, blk), F32),
                        pltpu.VMEM((t, QK), F32), pltpu.VMEM((t, QK), F32), pltpu.VMEM((t, VDIM), F32)]
        + ([pltpu.SemaphoreType.DMA((n, 3)), pltpu.SemaphoreType.DMA((n, 3))] if n else []),
        compiler_params=_params(1, VMEM_LIMIT_WHOLE_HEAD))(q, k, v, do, lse, delta, *parts)
    return outs[0], outs[1], outs[2], list(outs[3:])


def _shift_down(u, s):
    rows = lax.broadcasted_iota(jnp.int32, u.shape, 0)
    return jnp.where(rows >= s, pltpu.roll(u, s, 0), 0.0)


def _shift_up(u, s):
    n = u.shape[0]
    rows = lax.broadcasted_iota(jnp.int32, u.shape, 0)
    return jnp.where(rows < n - s, pltpu.roll(u, n - s, 0), 0.0)


def _conv_specs(t, d, lanes):
    slab = lambda part: pl.BlockSpec((None, t, lanes), lambda j, part=part: (part, 0, j))
    return slab, pl.BlockSpec((3, lanes), lambda j: (0, j)), pl.BlockSpec((t, lanes), lambda j: (0, j))


def conv_bwd(name, bcx, w, dy):
    _, t, d = bcx.shape
    lanes = _tile(d, 128, 128)
    slab, w_spec, col = _conv_specs(t, d, lanes)

    def body(b_ref, c_ref, x_ref, w_ref, dy_ref, d_ref, dw_ref):
        c = c_ref[...].astype(F32)
        x = x_ref[...].astype(F32)
        dyv = dy_ref[...]
        u = c * x
        u1 = _shift_down(u, 1)
        u2 = _shift_down(u, 2)
        w0, w1, w2 = w_ref[0:1, :], w_ref[1:2, :], w_ref[2:3, :]
        d_ref[0] = (dyv * (w0 * u2 + w1 * u1 + w2 * u)).astype(BF16)
        duc = dyv * b_ref[...].astype(F32)
        dw_ref[0:1, :] = jnp.sum(duc * u2, axis=0, keepdims=True)
        dw_ref[1:2, :] = jnp.sum(duc * u1, axis=0, keepdims=True)
        dw_ref[2:3, :] = jnp.sum(duc * u, axis=0, keepdims=True)
        du = w2 * duc + w1 * _shift_up(duc, 1) + w0 * _shift_up(duc, 2)
        d_ref[1] = (du * x).astype(BF16)
        d_ref[2] = (du * c).astype(BF16)

    return pl.pallas_call(
        body, name=name, grid=(d // lanes,), in_specs=[slab(0), slab(1), slab(2), w_spec, col],
        out_specs=[pl.BlockSpec((3, t, lanes), lambda j: (0, 0, j)), w_spec], out_shape=[S((3, t, d), BF16), S((3, d), F32)],
        compiler_params=_params(1))(bcx, bcx, bcx, w, dy)


def _adamw_update(w, g, m, v):
    m_new = ADAM_B1 * m + (1.0 - ADAM_B1) * g
    v_new = ADAM_B2 * v + (1.0 - ADAM_B2) * (g * g)
    m_hat = m_new / (1.0 - ADAM_B1 ** ADAM_STEP)
    v_hat = v_new / (1.0 - ADAM_B2 ** ADAM_STEP)
    return -ADAM_LR * (m_hat / (jnp.sqrt(v_hat) + ADAM_EPS) + ADAM_WD * w), m_new, v_new


def adamw(name, w, g, m, v):
    r, c = w.shape
    tr = _tile(r, 512)

    def body(w_ref, g_ref, m_ref, v_ref, d_ref, mo_ref, vo_ref):
        d_ref[...], mo_ref[...], vo_ref[...] = _adamw_update(w_ref[...], g_ref[...], m_ref[...], v_ref[...])

    blk = pl.BlockSpec((tr, c), lambda i: (i, 0))
    return pl.pallas_call(
        body, name=name, grid=(r // tr,), in_specs=[blk] * 4, out_specs=[blk] * 3, out_shape=[S((r, c), F32)] * 3,
        compiler_params=_params(1))(w, g, m, v)


def adamw_swapped(name, wt, g, mt, vt):
    nl, c, r = wt.shape
    tr = _tile(r, 512, 128)
    nr = r // tr

    def body(w_ref, g_ref, m_ref, v_ref, go_ref, d_ref, mo_ref, vo_ref):
        gt = g_ref[...].T
        go_ref[...] = gt
        d_ref[...], mo_ref[...], vo_ref[...] = _adamw_update(w_ref[...], gt, m_ref[...], v_ref[...])

    swapped = pl.BlockSpec((None, c, tr), lambda l, i: (l, 0, i))
    return pl.pallas_call(
        body, name=name, grid=(nl, nr),
        in_specs=[swapped, pl.BlockSpec((tr, c), lambda l, i: (l * nr + i, 0)), swapped, swapped],
        out_specs=[swapped] * 4, out_shape=[S((nl, c, r), F32)] * 4, compiler_params=_params(2))(wt, g, mt, vt)


def _place():
    x, y, c = lax.axis_index("x"), lax.axis_index("y"), lax.axis_index("c")
    other_chips = [(1 - x, y), (x, 1 - y), (1 - x, 1 - y)]
    return x, y, c, other_chips


def _half(c, rows):
    return pl.ds(pl.multiple_of(c * (rows // 2), 16), rows // 2)


def gather_weight_shards(shards, small=()):
    n, ns = len(shards), len(small)

    def body(*refs):
        src = refs[:n]
        small_src = refs[n:n + ns]
        dst = refs[n + ns:2 * n + ns]
        small_dst = refs[2 * n + ns:2 * (n + ns)]
        send_sems, recv_sems, small_send, small_recv, local_sems = refs[2 * (n + ns):2 * (n + ns) + 5]
        stages = refs[2 * (n + ns) + 5:]
        x, y, c, chips = _place()
        me = 2 * x + y
        sibling = (x, y, 1 - c)
        own = [(s_ref, stages[i], d_ref.at[me], local_sems.at[i])
               for i, (s_ref, d_ref) in enumerate(zip(list(src) + list(small_src), list(dst) + list(small_dst)))]
        for s_ref, stage, _, sems in own:
            pltpu.make_async_copy(s_ref, stage, sems.at[0]).start()
        small_pairs = []
        for i in range(ns):
            for j, (px, py) in enumerate(chips):
                def whole(slot):
                    return pltpu.make_async_remote_copy(
                        src_ref=small_src[i], dst_ref=small_dst[i].at[slot], send_sem=small_send.at[i, j],
                        recv_sem=small_recv.at[i, j], device_id=(px, py, c), device_id_type=MESH)
                small_pairs.append((whole(me), whole(2 * px + py)))
        for outgoing, _ in small_pairs:
            outgoing.start()

        def copy(i, slot, half_of, sem, to, from_input=False):
            rows = _half(half_of, src[i].shape[0])
            return pltpu.make_async_remote_copy(
                src_ref=src[i].at[rows] if from_input else dst[i].at[slot, rows], dst_ref=dst[i].at[slot, rows],
                send_sem=send_sems.at[i, sem], recv_sem=recv_sems.at[i, sem], device_id=to, device_id_type=MESH)

        sent = []
        for i in range(n):
            for j, chip in enumerate(chips):
                sent.append(copy(i, me, c, j, (*chip, c), from_input=True))
                sent[-1].start()
        for i in range(n):
            for j, (px, py) in enumerate(chips):
                copy(i, 2 * px + py, c, j, sibling).wait_recv()
                sent.append(copy(i, 2 * px + py, c, 3 + j, sibling))
                sent[-1].start()
        for i in range(n):
            for j, (px, py) in enumerate(chips):
                copy(i, 2 * px + py, 1 - c, 3 + j, sibling).wait_recv()
        for cp in sent:
            cp.wait_send()
        for _, incoming in small_pairs:
            incoming.wait_recv()
        for outgoing, _ in small_pairs:
            outgoing.wait_send()
        _place_locally(own)

    everything = list(shards) + list(small)
    outs = pl.pallas_call(
        body, name="gather_weight_shards", in_specs=[ANY] * (n + ns), out_specs=[ANY] * (n + ns),
        out_shape=[S((N_CHIPS,) + s.shape, s.dtype) for s in everything],
        scratch_shapes=[pltpu.SemaphoreType.DMA((n, 6)), pltpu.SemaphoreType.DMA((n, 6)),
                        pltpu.SemaphoreType.DMA((max(ns, 1), 3)), pltpu.SemaphoreType.DMA((max(ns, 1), 3)),
                        pltpu.SemaphoreType.DMA((n + ns, 2))] + [pltpu.VMEM(s.shape, s.dtype) for s in everything],
    )(*shards, *small)
    return list(outs[:n]), list(outs[n:])


def gather_ici_copies(src, dst, send_sems, recv_sems):
    x, y, c, chips = _place()
    me = 2 * x + y
    pairs = []
    for i in range(len(src)):
        rows = _half(c, src[i].shape[0])
        for j, (px, py) in enumerate(chips):
            def copy(slot):
                return pltpu.make_async_remote_copy(
                    src_ref=src[i].at[rows], dst_ref=dst[i].at[slot, rows], send_sem=send_sems.at[i, j],
                    recv_sem=recv_sems.at[i, j], device_id=(px, py, c), device_id_type=MESH)
            pairs.append((copy(me), copy(2 * px + py)))
    return pairs


def scatter_ici_copies(src, dst, send_sems, recv_sems):
    x, y, c, chips = _place()
    me = 2 * x + y
    pairs = []
    for i in range(len(src)):
        for j, (px, py) in enumerate(chips):
            def copy(from_slot, to_slot):
                return pltpu.make_async_remote_copy(
                    src_ref=src[i].at[from_slot], dst_ref=dst[i].at[to_slot], send_sem=send_sems.at[i, j],
                    recv_sem=recv_sems.at[i, j], device_id=(px, py, c), device_id_type=MESH)
            pairs.append((copy(2 * px + py, me), copy(me, 2 * px + py)))
    return pairs


def _ride_along(pairs, grid_ids, grid_sizes, local=()):
    first = grid_ids[0] == 0
    last = grid_ids[0] == grid_sizes[0] - 1
    for g, size in zip(grid_ids[1:], grid_sizes[1:]):
        first = first & (g == 0)
        last = last & (g == size - 1)

    @pl.when(first)
    def _():
        for outgoing, _ in pairs:
            outgoing.start()
        for src, stage, _, sems in local:
            pltpu.make_async_copy(src, stage, sems.at[0]).start()

    @pl.when(last)
    def _():
        for _, incoming in pairs:
            incoming.wait_recv()
        for outgoing, _ in pairs:
            outgoing.wait_send()
        _place_locally(local)


def _place_locally(local):
    for src, stage, _, sems in local:
        pltpu.make_async_copy(src, stage, sems.at[0]).wait()
    placed = [pltpu.make_async_copy(stage, dst, sems.at[1]) for _, stage, dst, sems in local]
    for cp in placed:
        cp.start()
    for cp in placed:
        cp.wait()


def forward_copies(src, dst, send_sems, recv_sems):
    x, y, c, chips = _place()
    pairs = []
    for i in range(len(src)):
        for j, (px, py) in enumerate(chips):
            def copy(half_of):
                rows = _half(half_of, src[i].shape[1])
                return pltpu.make_async_remote_copy(
                    src_ref=src[i].at[2 * px + py, rows], dst_ref=dst[i].at[2 * px + py, rows], send_sem=send_sems.at[i, j],
                    recv_sem=recv_sems.at[i, j], device_id=(x, y, 1 - c), device_id_type=MESH)
            pairs.append((copy(c), copy(1 - c)))
    return pairs


def attention_out_proj(name, attn, w_o, resid, next_gain, arriving, own):
    t, kdim = attn.shape
    n = w_o.shape[1]
    tm = _tile(t, 512)
    m = len(arriving)

    def body(x_ref, w_ref, r_ref, g_ref, *refs):
        src, own_refs = refs[:m], refs[m:2 * m]
        h_ref, a_ref = refs[2 * m:2 * m + 2]
        dst = refs[2 * m + 2:3 * m + 2]
        stages = refs[3 * m + 2:4 * m + 2]
        send_sems, recv_sems, local_sems = refs[4 * m + 2:]
        me = 2 * lax.axis_index("x") + lax.axis_index("y")
        placed = [(own_refs[i], stages[i], dst[i].at[me], local_sems.at[i]) for i in range(m)]
        _ride_along(forward_copies(src, dst, send_sems, recv_sems), (pl.program_id(0),), (t // tm,), placed)
        y = r_ref[...] + jnp.dot(x_ref[...].astype(BF16), w_ref[...], preferred_element_type=F32)
        h_ref[...] = y
        a_ref[...] = (y * _rstd(y) * g_ref[...]).astype(BF16)

    row = pl.BlockSpec((tm, n), lambda i: (i, 0))
    outs = pl.pallas_call(
        body, name=name, grid=(t // tm,),
        in_specs=[pl.BlockSpec((tm, kdim), lambda i: (i, 0)), _resident((kdim, n), lambda i: (0, 0)), row,
                  pl.BlockSpec((1, n), lambda i: (0, 0))] + [ANY] * (2 * m),
        out_specs=[row, row] + [ANY] * m, out_shape=[S((t, n), F32), S((t, n), BF16)] + [S(g.shape, g.dtype) for g in arriving],
        input_output_aliases={4 + i: 2 + i for i in range(m)},
        scratch_shapes=[pltpu.VMEM(o.shape, o.dtype) for o in own]
        + [pltpu.SemaphoreType.DMA((m, 3)), pltpu.SemaphoreType.DMA((m, 3)), pltpu.SemaphoreType.DMA((m, 2))],
        compiler_params=_params(1))(attn, w_o, resid, next_gain, *arriving, *own)
    return outs[0], outs[1], list(outs[2:])


def swap_copies(src, dst, send_sems, recv_sems):
    x, y, c, _ = _place()
    pairs = []
    for i in range(len(src)):
        cp = pltpu.make_async_remote_copy(
            src_ref=src[i].at[:, _half(1 - c, src[i].shape[1]), :], dst_ref=dst[i], send_sem=send_sems.at[i],
            recv_sem=recv_sems.at[i], device_id=(x, y, 1 - c), device_id_type=MESH)
        pairs.append((cp, cp))
    return pairs


def _swap_shapes(grads):
    return [S((g.shape[0], g.shape[1] // 2, g.shape[2]), g.dtype) for g in grads]


def sibling_swap_halves(name, grads):
    n = len(grads)

    def body(*refs):
        pairs = swap_copies(refs[:n], refs[n:2 * n], *refs[2 * n:])
        for outgoing, _ in pairs:
            outgoing.start()
        for _, incoming in pairs:
            incoming.wait_recv()
        for outgoing, _ in pairs:
            outgoing.wait_send()

    return pl.pallas_call(
        body, name=name, in_specs=[ANY] * n, out_specs=[ANY] * n, out_shape=_swap_shapes(grads),
        scratch_shapes=[pltpu.SemaphoreType.DMA((n,)), pltpu.SemaphoreType.DMA((n,))],
    )(*grads)


def add_halves(name, g, rx):
    _, r, cdim = g.shape
    r2 = r // 2
    tr = _tile(r2, 512, 16)
    nb = r2 // tr

    def body(lo_ref, hi_ref, rx_ref, o_ref):
        mine = jnp.where(lax.axis_index("c") == 0, lo_ref[...], hi_ref[...])
        o_ref[...] = (mine.astype(F32) + rx_ref[...].astype(F32)).astype(BF16)

    half = pl.BlockSpec((None, tr, cdim), lambda k, i: (k, i, 0))
    return pl.pallas_call(
        body, name=name, grid=(N_CHIPS, nb),
        in_specs=[half, pl.BlockSpec((None, tr, cdim), lambda k, i: (k, nb + i, 0)), half],
        out_specs=half, out_shape=S((N_CHIPS, r2, cdim), BF16), compiler_params=_params(2))(g, g, rx)


def sum_chips(name, arrived, mine):
    _, r2, cdim = arrived.shape
    tr = _tile(r2, 512, 16)

    def body(a_ref, m_ref, o_ref):
        me = 2 * lax.axis_index("x") + lax.axis_index("y")
        acc = jnp.zeros((tr, cdim), F32)
        for k in range(N_CHIPS):
            acc = acc + jnp.where(me == k, m_ref[k], a_ref[k]).astype(F32)
        o_ref[...] = acc

    slots = pl.BlockSpec((N_CHIPS, tr, cdim), lambda i: (0, i, 0))
    return pl.pallas_call(
        body, name=name, grid=(r2 // tr,), in_specs=[slots, slots],
        out_specs=pl.BlockSpec((tr, cdim), lambda i: (i, 0)), out_shape=S((r2, cdim), F32), compiler_params=_params(1))(arrived, mine)


def join_copies(src, dst, where, send_sems, recv_sems):
    x, y, c, _ = _place()
    pairs = []
    for i in range(len(src)):
        def copy(half_of):
            r2 = src[i].shape[0]
            rows = pl.ds(pl.multiple_of(where[i][1] + half_of * r2, 8), r2)
            return pltpu.make_async_remote_copy(
                src_ref=src[i], dst_ref=dst[where[i][0]].at[rows], send_sem=send_sems.at[i],
                recv_sem=recv_sems.at[i], device_id=(x, y, 1 - c), device_id_type=MESH)
        pairs.append((copy(c), copy(1 - c)))
    return pairs


def sibling_join_halves(name, halves, targets, where):
    n = len(halves)

    def body(*refs):
        src, dst = refs[:n], refs[n:n + len(targets)]
        send_sems, recv_sems, local_sems = refs[n + len(targets):n + len(targets) + 3]
        stages = refs[n + len(targets) + 3:]
        c = lax.axis_index("c")
        own = [(src[i], stages[i],
                dst[where[i][0]].at[pl.ds(pl.multiple_of(where[i][1] + c * src[i].shape[0], 8), src[i].shape[0])],
                local_sems.at[i]) for i in range(n)]
        for s_ref, stage, _, sems in own:
            pltpu.make_async_copy(s_ref, stage, sems.at[0]).start()
        pairs = join_copies(src, dst, where, send_sems, recv_sems)
        for outgoing, _ in pairs:
            outgoing.start()
        for _, incoming in pairs:
            incoming.wait_recv()
        for outgoing, _ in pairs:
            outgoing.wait_send()
        _place_locally(own)

    return list(pl.pallas_call(
        body, name=name, in_specs=[ANY] * n, out_specs=[ANY] * len(targets), out_shape=[S(tg, F32) for tg in targets],
        scratch_shapes=[pltpu.SemaphoreType.DMA((n,)), pltpu.SemaphoreType.DMA((n,)), pltpu.SemaphoreType.DMA((n, 2))]
        + [pltpu.VMEM(h.shape, h.dtype) for h in halves],
    )(*halves))


def all_reduce_small(name, packed):
    rows, width = packed.shape

    def body(x_ref, o_ref, gathered, send_sems, recv_sems):
        x, y, c, _ = _place()
        me = 4 * x + 2 * y + c
        gathered[me] = x_ref[...]
        flips = [(fx, fy, fc) for fx in (0, 1) for fy in (0, 1) for fc in (0, 1)][1:]

        def copy(r, slot, to):
            return pltpu.make_async_remote_copy(
                src_ref=x_ref, dst_ref=gathered.at[slot], send_sem=send_sems.at[r], recv_sem=recv_sems.at[r],
                device_id=to, device_id_type=MESH)

        def peer(f):
            return (x ^ f[0], y ^ f[1], c ^ f[2])

        sent = [copy(r, me, peer(f)) for r, f in enumerate(flips)]
        for cp in sent:
            cp.start()
        for r, f in enumerate(flips):
            px, py, pc = peer(f)
            copy(r, 4 * px + 2 * py + pc, peer(f)).wait_recv()
        for cp in sent:
            cp.wait_send()
        acc = gathered[0]
        for k in range(1, N_DEV):
            acc = acc + gathered[k]
        o_ref[...] = acc

    vmem = pl.BlockSpec(memory_space=pltpu.VMEM)
    return pl.pallas_call(
        body, name=name, in_specs=[vmem], out_specs=vmem, out_shape=S((rows, width), F32),
        scratch_shapes=[pltpu.VMEM((N_DEV, rows, width), F32), pltpu.SemaphoreType.DMA((N_DEV - 1,)),
                        pltpu.SemaphoreType.DMA((N_DEV - 1,))],
    )(packed)


def _rope_tables(positions):
    inv_freq = 1.0 / (ROPE_THETA ** (jnp.arange(0, ROPE, 2, dtype=F32) / ROPE))
    ang = positions.astype(F32)[:, None] * inv_freq
    return jnp.cos(ang), jnp.sin(ang)


def _unstack_cols(w):
    k4, k, n4 = w.shape
    return jnp.transpose(w, (1, 0, 2)).reshape(k, k4 * n4)


def _stack_cols(w):
    k, n = w.shape
    return jnp.transpose(w.reshape(k, N_CHIPS, n // N_CHIPS), (1, 0, 2))


def kernel(x, positions, mla_norm, mla_w_in, mla_g_cq, mla_g_ckv, mla_w_uq, mla_w_ukv, mla_w_o, conv_norm, conv_w_in, conv_w, conv_w_out, ffn_norm, ffn_w_gate, ffn_w_up, ffn_w_down, final_norm, loss_target, m_mla_norm, m_mla_w_in, m_mla_g_cq, m_mla_g_ckv, m_mla_w_uq, m_mla_w_ukv, m_mla_w_o, m_conv_norm, m_conv_w_in, m_conv_w, m_conv_w_out, m_ffn_norm, m_ffn_w_gate, m_ffn_w_up, m_ffn_w_down, m_final_norm, v_mla_norm, v_mla_w_in, v_mla_g_cq, v_mla_g_ckv, v_mla_w_uq, v_mla_w_ukv, v_mla_w_o, v_conv_norm, v_conv_w_in, v_conv_w, v_conv_w_out, v_ffn_norm, v_ffn_w_gate, v_ffn_w_up, v_ffn_w_down, v_final_norm):
    weights = dict(mla_norm=mla_norm, mla_w_in=mla_w_in, mla_g_cq=mla_g_cq, mla_g_ckv=mla_g_ckv, mla_w_uq=mla_w_uq,
                   mla_w_ukv=mla_w_ukv, mla_w_o=mla_w_o, conv_norm=conv_norm, conv_w_in=conv_w_in, conv_w=conv_w,
                   conv_w_out=conv_w_out, ffn_norm=ffn_norm, ffn_w_gate=ffn_w_gate, ffn_w_up=ffn_w_up,
                   ffn_w_down=ffn_w_down, final_norm=final_norm)
    m_in = dict(mla_norm=m_mla_norm, mla_w_in=m_mla_w_in, mla_g_cq=m_mla_g_cq, mla_g_ckv=m_mla_g_ckv, mla_w_uq=m_mla_w_uq,
                mla_w_ukv=m_mla_w_ukv, mla_w_o=m_mla_w_o, conv_norm=m_conv_norm, conv_w_in=m_conv_w_in, conv_w=m_conv_w,
                conv_w_out=m_conv_w_out, ffn_norm=m_ffn_norm, ffn_w_gate=m_ffn_w_gate, ffn_w_up=m_ffn_w_up,
                ffn_w_down=m_ffn_w_down, final_norm=m_final_norm)
    v_in = dict(mla_norm=v_mla_norm, mla_w_in=v_mla_w_in, mla_g_cq=v_mla_g_cq, mla_g_ckv=v_mla_g_ckv, mla_w_uq=v_mla_w_uq,
                mla_w_ukv=v_mla_w_ukv, mla_w_o=v_mla_w_o, conv_norm=v_conv_norm, conv_w_in=v_conv_w_in, conv_w=v_conv_w,
                conv_w_out=v_conv_w_out, ffn_norm=v_ffn_norm, ffn_w_gate=v_ffn_w_gate, ffn_w_up=v_ffn_w_up,
                ffn_w_down=v_ffn_w_down, final_norm=v_final_norm)
    big = ["mla_w_in", "mla_w_uq", "mla_w_ukv", "mla_w_o", "conv_w_in", "conv_w_out", "ffn_w_gate", "ffn_w_up", "ffn_w_down"]
    order = list(weights)

    t, d = x.shape[1], x.shape[2]
    h0 = x.reshape(t, d)
    target = loss_target.reshape(t, d)
    cos, sin = _rope_tables(positions.reshape(t))

    def rows2d(a):
        return a.reshape(-1, a.shape[-1])

    first, later = big[:4], big[4:]
    shards = {n: rows2d(weights[n]).astype(BF16) for n in big}
    d4 = d // N_CHIPS
    first_gathered, (conv_norm_slots, conv_w_slots) = gather_weight_shards(
        [shards[n] for n in first], [conv_norm.reshape(1, d4), conv_w.reshape(3, d4)])
    gathered = dict(zip(first, first_gathered))
    conv_norm_full = conv_norm_slots.reshape(1, d)
    conv_w_full = jnp.transpose(conv_w_slots, (1, 0, 2)).reshape(3, d)
    w_in = gathered["mla_w_in"].reshape(-1, gathered["mla_w_in"].shape[-1])
    w_uq = _unstack_cols(gathered["mla_w_uq"])
    w_ukv = _unstack_cols(gathered["mla_w_ukv"])
    w_o = gathered["mla_w_o"].reshape(-1, d)

    chip = 2 * lax.axis_index("x") + lax.axis_index("y")

    def pack_rows(rows):
        idx = lax.broadcasted_iota(jnp.int32, (SMALL_ROWS, d), 0)
        out = jnp.zeros((SMALL_ROWS, d), F32)
        for r, row in enumerate(rows):
            out = out + jnp.where(idx == r, row, 0.0)
        return out


    a0 = rms_fwd("mla_norm_fwd", h0, mla_norm)
    proj, cq, ckv, kr = mla_in_proj("mla_in_proj", a0, w_in, mla_g_cq, mla_g_ckv, cos, sin)
    q = linear("mla_q_up", cq, w_uq, F32)
    kv = linear("mla_kv_up", ckv, w_ukv, BF16)
    qh, kh, vh, conv_arriving = qkv_heads("qkv_heads", q, kv, kr, cos, sin, [shards[n] for n in later[:2]])
    attn, lse, ffn_arriving = attention_fwd("attention_fwd", qh, kh, vh, [shards[n] for n in later[2:]])
    h1, a1, handed = attention_out_proj("mla_out_proj", attn, w_o, h0, ffn_norm[0:1], conv_arriving + ffn_arriving,
                                        [shards[n] for n in later])
    gathered.update(zip(later, handed))
    cw_in = _unstack_cols(gathered["conv_w_in"])
    cw_out = gathered["conv_w_out"].reshape(-1, d)
    wg_all, wu_all, wd_all = gathered["ffn_w_gate"], gathered["ffn_w_up"], gathered["ffn_w_down"]

    def ffn_forward(tag, h, a, layer, next_gain):
        g, u, z = ffn_up(f"ffn{tag}_up", a, wg_all, wu_all, layer)
        return g, u, z, ffn_down(f"ffn{tag}_down", z, wd_all, layer, h, next_gain)

    g0, u0, z0, (h2, a2) = ffn_forward(0, h1, a1, 0, conv_norm_full)
    bcx, yc = conv_in_proj("conv_in_proj", a2, cw_in, conv_w_full)
    h3, a3 = linear("conv_out_proj", yc, cw_out, F32, resid=h2, next_gain=ffn_norm[1:2])
    g1, u1, z1 = ffn_up("ffn1_up", a3, wg_all, wu_all, 1)
    dh4, d_final_norm, loss_local = ffn_down_loss("ffn1_down_loss", z1, wd_all, 1, h3, final_norm.reshape(1, d), target)

    def ffn_backward(tag, dh, h, layer, a, g, u, z, swap=()):
        dg, du, swapped = ffn_bwd_hidden(f"ffn{tag}_bwd_hidden", dh, wd_all, layer, g, u, swap)
        d_wd = ffn_wgrad_down(f"ffn{tag}_wgrad_down", z, dh)
        dh_prev, d_norm = ffn_bwd_input(f"ffn{tag}_bwd_input", dg, du, wg_all, wu_all, layer, h, ffn_norm[layer:layer + 1], dh)
        d_wg = ffn_wgrad_up(f"ffn{tag}_wgrad_gate", a, dg)
        d_wu = ffn_wgrad_up(f"ffn{tag}_wgrad_up", a, du)
        return dh_prev, d_norm, [d_wg, d_wu, d_wd], swapped

    def pair_sums(tag, local, from_sibling):
        return [add_halves(f"pair_sum_{tag}{i}", g, r) for i, (g, r) in enumerate(zip(local, from_sibling))]

    def sum_from_chips(tag, pairs, arrived):
        return [sum_chips(f"chip_sum_{tag}{i}", a, p) for i, (a, p) in enumerate(zip(arrived, pairs))]

    def shard_shape(n):
        return rows2d(weights[n]).shape

    dh3, d_ffn_norm1, ffn1_grads, _ = ffn_backward(1, dh4, h3, 1, a3, g1, u1, z1)

    dyc = linear_nt("conv_out_bwd_input", dh3, cw_out, F32)
    d_cw_out = wgrad("conv_out_wgrad", yc, dh3)
    dbcx, d_conv_w = conv_bwd("conv_bwd", bcx, conv_w_full, dyc)
    dh2, d_conv_norm = conv_in_bwd_input("conv_in_bwd_input", dbcx, cw_in, h2, conv_norm_full, dh3)
    d_cw_in = conv_in_wgrad("conv_in_wgrad", a2, dbcx)

    second = [d_cw_in, d_cw_out.reshape(N_CHIPS, -1, d)] + ffn1_grads
    dh1, d_ffn_norm0, ffn0_grads, second_swapped = ffn_backward(0, dh2, h1, 0, a1, g0, u0, z0, second)
    d_w_o = wgrad("mla_out_wgrad", attn, dh1)
    first_part = ffn0_grads + [d_w_o.reshape(N_CHIPS, -1, d)]
    d_attn, delta, first_swapped = attention_out_bwd("mla_out_bwd_input", dh1, w_o, attn, first_part)
    rest_pairs = pair_sums("rest", second + first_part, second_swapped + first_swapped)
    dqh, dkh, dvh, rest_arrived = attention_bwd("attention_bwd", qh, kh, vh, d_attn, lse, delta, rest_pairs)
    rd, rf = ffn0_grads[0].shape[1], ffn0_grads[2].shape[1]
    rest_where = [(0, 0), (1, 0), (2, rd), (3, rd), (4, rf), (2, 0), (3, 0), (4, 0), (5, 0)]
    rest_names = later + ["mla_w_o"]
    dq, dkv, dkr, rest_grads = qkv_heads_bwd("qkv_heads_bwd", dqh, dkh, dvh, cos, sin, sum_from_chips("rest", rest_pairs, rest_arrived),
                                              [shard_shape(n) for n in rest_names], rest_where)
    grads = dict(zip(rest_names, rest_grads))
    dcq = linear_nt("mla_q_up_bwd_input", dq, w_uq, F32)
    d_w_uq = wgrad("mla_q_up_wgrad", cq, dq)
    dckv = linear_nt("mla_kv_up_bwd_input", dkv, w_ukv, F32)
    d_w_ukv = wgrad("mla_kv_up_wgrad", ckv, dkv)
    dproj, d_g_cq, d_g_ckv = mla_mid_bwd("mla_mid_bwd", proj, mla_g_cq, mla_g_ckv, dcq, dckv, dkr, cos, sin)
    d_w_in = wgrad("mla_in_wgrad", a0, dproj)
    mla_local = [d_w_in.reshape(N_CHIPS, -1, d_w_in.shape[-1]), _stack_cols(d_w_uq), _stack_cols(d_w_ukv)]
    mla_pairs = pair_sums("mla", mla_local, sibling_swap_halves("sibling_swap_mla", mla_local))
    grad_x, d_mla_norm, mla_arrived = linear_nt_norm_bwd("mla_in_bwd_input", dproj, w_in, h0, mla_norm, dh1, mla_pairs)

    grads.update(zip(first[:3], sibling_join_halves("sibling_join_mla", sum_from_chips("mla", mla_pairs, mla_arrived),
                                                    [shard_shape(n) for n in first[:3]], [(i, 0) for i in range(3)])))

    def pad_row(v):
        return jnp.pad(v, ((0, 0), (0, d - v.shape[1])))

    small = all_reduce_small("all_reduce_small_grads", pack_rows([
        d_mla_norm, pad_row(d_g_cq), pad_row(d_g_ckv), d_ffn_norm0, d_ffn_norm1, d_final_norm, d_conv_norm,
        d_conv_w[0:1], d_conv_w[1:2], d_conv_w[2:3], jnp.broadcast_to(loss_local, (1, d))]))
    loss = small[10, 0]
    grads["mla_norm"] = small[0:1]
    grads["mla_g_cq"] = small[1:2, :mla_g_cq.shape[1]]
    grads["mla_g_ckv"] = small[2:3, :mla_g_ckv.shape[1]]
    grads["ffn_norm"] = small[3:5]
    grads["final_norm"] = small[5:6]
    grads["conv_norm"] = lax.dynamic_slice(small[6:7], (0, chip * d4), (1, d4))
    grads["conv_w"] = lax.dynamic_slice(small[7:10], (0, chip * d4), (3, d4))

    outs_g, outs_d, outs_m, outs_v = [], [], [], []
    for n in order:
        w = weights[n]
        if w.ndim == 3 and w.shape[2] % 128 and w.shape[1] % 128 == 0:
            results = adamw_swapped(f"adamw_{n}", jnp.swapaxes(w, 1, 2), grads[n].reshape(-1, w.shape[2]),
                                    jnp.swapaxes(m_in[n], 1, 2), jnp.swapaxes(v_in[n], 1, 2))
            grad_w, delta_w, new_m, new_v = [jnp.swapaxes(o, 1, 2) for o in results]
        else:
            delta_w, new_m, new_v = adamw(f"adamw_{n}", rows2d(w), grads[n].reshape(rows2d(w).shape), rows2d(m_in[n]), rows2d(v_in[n]))
            grad_w = grads[n]
        outs_g.append(grad_w.reshape(w.shape))
        outs_d.append(delta_w.reshape(w.shape))
        outs_m.append(new_m.reshape(w.shape))
        outs_v.append(new_v.reshape(w.shape))
    return (loss, grad_x.reshape(x.shape), *outs_g, *outs_d, *outs_m, *outs_v)
```

```python
import math

import jax
import jax.numpy as jnp
from jax import lax
from jax.experimental import pallas as pl
from jax.experimental.pallas import tpu as pltpu

F32 = jnp.float32
BF16 = jnp.bfloat16
S = jax.ShapeDtypeStruct

N_HEADS = 8
NOPE = 128
ROPE = 64
HALF = ROPE // 2
VDIM = 128
QK = NOPE + ROPE
CHUNK = 64
ROPE_THETA = 10000.0
RMS_EPS = 1e-6
ADAM_LR = 0.001
ADAM_B1 = 0.9
ADAM_B2 = 0.999
ADAM_EPS = 1e-08
ADAM_WD = 0.01
ADAM_STEP = 10

N_CHIPS = 4
N_DEV = 8
MASK_VALUE = -1e30
SCORE_SCALE = 1.0 / math.sqrt(QK)
LOG2_E = math.log2(math.e)
SCORE_SCALE_LOG2 = SCORE_SCALE * LOG2_E
VMEM_LIMIT = 48 * 1024 * 1024
VMEM_LIMIT_WHOLE_HEAD = 58 * 1024 * 1024
ATT_BLOCK = 512
CONV_SAVED_DTYPE = jnp.bfloat16
SMALL_ROWS = 16

_NN = (((1,), (0,)), ((), ()))
_NT = (((1,), (1,)), ((), ()))
_TN = (((0,), (0,)), ((), ()))
MESH = pl.DeviceIdType.MESH
ANY = pl.BlockSpec(memory_space=pl.ANY)


def _params(n_axes, vmem_limit=VMEM_LIMIT):
    return pltpu.CompilerParams(dimension_semantics=("arbitrary",) * n_axes, vmem_limit_bytes=vmem_limit)


def _tile(n, cap, mult=8):
    for t in range(min(cap, n), 0, -1):
        if n % t == 0 and t % mult == 0:
            return t
    return n


def _sigmoid(x):
    return 0.5 * jnp.tanh(0.5 * x) + 0.5


def _mm(name, a_ops, b_ops, products, dims, grid, k_axis, outs, acc_shape, epilogue, extra_ops=()):
    na, nb, ne, no = len(a_ops), len(b_ops), len(extra_ops), len(outs)
    n_acc = 1 + max(c for _, _, c in products)
    nk = 1 if k_axis is None else grid[k_axis]

    def body(*refs):
        a_refs = refs[:na]
        b_refs = refs[na:na + nb]
        e_refs = refs[na + nb:na + nb + ne]
        o_refs = refs[na + nb + ne:na + nb + ne + no]
        acc_refs = refs[na + nb + ne + no:]

        def partial_sums():
            vals = [None] * n_acc
            for ai, bi, ci in products:
                d = lax.dot_general(a_refs[ai][...].astype(BF16), b_refs[bi][...].astype(BF16), dims,
                                    preferred_element_type=F32)
                vals[ci] = d if vals[ci] is None else vals[ci] + d
            return vals

        if nk == 1:
            epilogue(partial_sums(), e_refs, o_refs)
        else:
            k = pl.program_id(k_axis)

            @pl.when(k == 0)
            def _():
                for acc in acc_refs:
                    acc[...] = jnp.zeros_like(acc)

            for acc, v in zip(acc_refs, partial_sums()):
                acc[...] += v

            @pl.when(k == nk - 1)
            def _():
                epilogue([acc[...] for acc in acc_refs], e_refs, o_refs)

    ops = list(a_ops) + list(b_ops) + list(extra_ops)
    return pl.pallas_call(
        body, name=name, grid=grid,
        in_specs=[s for _, s in ops], out_specs=[s for _, s in outs], out_shape=[o for o, _ in outs],
        scratch_shapes=[pltpu.VMEM(acc_shape, F32) for _ in range(n_acc if nk > 1 else 0)],
        compiler_params=_params(len(grid)),
    )(*[a for a, _ in ops])


def _store(accs, e_refs, o_refs):
    o_refs[0][...] = accs[0].astype(o_refs[0].dtype)


def linear(name, x, w, out_dtype, resid=None, next_gain=None):
    t, k = x.shape
    n = w.shape[1]
    tm = _tile(t, 512)
    tn = n if n <= 2048 else _tile(n, 1024, 128)
    tile = pl.BlockSpec((tm, tn), lambda j, i: (i, j))
    extra = [] if resid is None else [(resid, tile)]
    outs = [(S((t, n), out_dtype), tile)]
    if next_gain is not None:
        assert tn == n
        extra.append((next_gain, pl.BlockSpec((1, n), lambda j, i: (0, 0))))
        outs.append((S((t, n), BF16), tile))

    def epilogue(accs, e_refs, o_refs):
        y = accs[0] if resid is None else e_refs[0][...] + accs[0]
        o_refs[0][...] = y.astype(out_dtype)
        if next_gain is not None:
            o_refs[1][...] = (y * _rstd(y) * e_refs[-1][...]).astype(BF16)

    res = _mm(name, [(x, pl.BlockSpec((tm, k), lambda j, i: (i, 0)))], [(w, pl.BlockSpec((k, tn), lambda j, i: (0, j)))],
              [(0, 0, 0)], _NN, (n // tn, t // tm), None, outs, None, epilogue, extra)
    return res[0] if next_gain is None else res


def linear_nt(name, dy, w, out_dtype):
    t, n = dy.shape
    k = w.shape[0]
    tm = _tile(t, 512)
    tc = n if n <= 2048 else _tile(n, 1024, 128)
    return _mm(name, [(dy, pl.BlockSpec((tm, tc), lambda i, c: (i, c)))], [(w, pl.BlockSpec((k, tc), lambda i, c: (0, c)))],
               [(0, 0, 0)], _NT, (t // tm, n // tc), 1,
               [(S((t, k), out_dtype), pl.BlockSpec((tm, k), lambda i, c: (i, 0)))], (tm, k), _store)[0]


def wgrad(name, x, dy):
    t, k = x.shape
    n = dy.shape[1]
    tk = _tile(t, 512)
    tn = n if n <= 1024 else _tile(n, 1024, 128)
    return _mm(name, [(x, pl.BlockSpec((tk, k), lambda j, s: (s, 0)))], [(dy, pl.BlockSpec((tk, tn), lambda j, s: (s, j)))],
               [(0, 0, 0)], _TN, (n // tn, t // tk), 1,
               [(S((k, n), BF16), pl.BlockSpec((k, tn), lambda j, s: (0, j)))], (k, tn), _store)[0]


def _resident(shape, index_map):
    return pl.BlockSpec(shape, index_map, pipeline_mode=pl.Buffered(1))


def ffn_up(name, a, wg_all, wu_all, layer):
    t, d = a.shape
    f4 = wg_all.shape[2]
    tm = _tile(t, 512)
    w_spec = _resident((N_CHIPS, d, f4), lambda i: (0, layer, 0))
    h_spec = pl.BlockSpec((N_CHIPS, tm, f4), lambda i: (0, i, 0))

    def body(a_ref, wg_ref, wu_ref, zg_ref, zu_ref, z_ref):
        av = a_ref[...]
        for k in range(N_CHIPS):
            g = jnp.dot(av, wg_ref[k], preferred_element_type=F32)
            u = jnp.dot(av, wu_ref[k], preferred_element_type=F32)
            sg = _sigmoid(g)
            silu = g * sg
            zg_ref[k] = (u * (sg * (1.0 + g * (1.0 - sg)))).astype(BF16)
            zu_ref[k] = silu.astype(BF16)
            z_ref[k] = (silu * u).astype(BF16)

    return pl.pallas_call(
        body, name=name, grid=(t // tm,), in_specs=[pl.BlockSpec((tm, d), lambda i: (i, 0)), w_spec, w_spec],
        out_specs=[h_spec] * 3, out_shape=[S((N_CHIPS, t, f4), BF16)] * 3, compiler_params=_params(1))(a, wg_all, wu_all)


def ffn_down(name, z, wd_all, layer, resid, next_gain=None):
    _, t, f4 = z.shape
    d = wd_all.shape[2]
    tm = _tile(t, 512)
    row = pl.BlockSpec((tm, d), lambda i: (i, 0))
    normed = next_gain is not None

    def body(z_ref, wd_ref, r_ref, *refs):
        acc = r_ref[...]
        for k in range(N_CHIPS):
            acc = acc + jnp.dot(z_ref[k], wd_ref[k], preferred_element_type=F32)
        refs[-2 if normed else -1][...] = acc
        if normed:
            refs[-1][...] = (acc * _rstd(acc) * refs[0][...]).astype(BF16)

    res = pl.pallas_call(
        body, name=name, grid=(t // tm,),
        in_specs=[pl.BlockSpec((N_CHIPS, tm, f4), lambda i: (0, i, 0)), _resident((N_CHIPS, f4, d), lambda i: (0, layer, 0)), row]
        + ([pl.BlockSpec((1, d), lambda i: (0, 0))] if normed else []),
        out_specs=[row] * (2 if normed else 1), out_shape=[S((t, d), F32)] + ([S((t, d), BF16)] if normed else []),
        compiler_params=_params(1))(z, wd_all, resid, *([next_gain] if normed else []))
    return res if normed else res[0]


def ffn_bwd_hidden(name, dh, wd_all, layer, zg, zu, swap=()):
    t, d = dh.shape
    f4 = zg.shape[2]
    tm = _tile(t, 512)
    h_spec = pl.BlockSpec((N_CHIPS, tm, f4), lambda i: (0, i, 0))
    n = len(swap)

    def body(dh_ref, wd_ref, zg_ref, zu_ref, *refs):
        dg_ref, du_ref = refs[n:n + 2]
        if n:
            _ride_along(swap_copies(refs[:n], refs[n + 2:2 * n + 2], *refs[2 * n + 2:]), (pl.program_id(0),), (t // tm,))
        dhb = dh_ref[...].astype(BF16)
        for k in range(N_CHIPS):
            dz = lax.dot_general(dhb, wd_ref[k], _NT, preferred_element_type=F32)
            dg_ref[k] = (dz * zg_ref[k].astype(F32)).astype(BF16)
            du_ref[k] = (dz * zu_ref[k].astype(F32)).astype(BF16)

    outs = pl.pallas_call(
        body, name=name, grid=(t // tm,),
        in_specs=[pl.BlockSpec((tm, d), lambda i: (i, 0)), _resident((N_CHIPS, f4, d), lambda i: (0, layer, 0)), h_spec, h_spec]
        + [ANY] * n,
        out_specs=[h_spec] * 2 + [ANY] * n, out_shape=[S((N_CHIPS, t, f4), BF16)] * 2 + _swap_shapes(swap),
        scratch_shapes=[pltpu.SemaphoreType.DMA((n,)), pltpu.SemaphoreType.DMA((n,))] if n else [],
        compiler_params=_params(1))(dh, wd_all, zg, zu, *swap)
    return outs[0], outs[1], list(outs[2:])


def _norm_bwd_specs(tm, d):
    row = pl.BlockSpec((tm, d), lambda i: (i, 0))
    vec = pl.BlockSpec((1, d), lambda i: (0, 0))
    return [row, vec, row], [row, vec]


def _norm_bwd_tail(da, h_ref, g_ref, dhi_ref, dho_ref, dgain_ref):
    dx, dgain = _rms_bwd(h_ref[...], g_ref[...], da)
    dho_ref[...] = dhi_ref[...] + dx

    @pl.when(pl.program_id(0) == 0)
    def _():
        dgain_ref[...] = jnp.zeros_like(dgain_ref)

    dgain_ref[...] += dgain


def ffn_bwd_input(name, dg, du, wg_all, wu_all, layer, h, gain, dh_in):
    _, t, f4 = dg.shape
    d = h.shape[1]
    tm = _tile(t, 512)
    h_spec = pl.BlockSpec((N_CHIPS, tm, f4), lambda i: (0, i, 0))
    w_spec = _resident((N_CHIPS, d, f4), lambda i: (0, layer, 0))
    tail_in, tail_out = _norm_bwd_specs(tm, d)

    def body(dg_ref, du_ref, wg_ref, wu_ref, *tail):
        acc = jnp.zeros((tm, d), F32)
        for k in range(N_CHIPS):
            acc = acc + lax.dot_general(dg_ref[k], wg_ref[k], _NT, preferred_element_type=F32)
            acc = acc + lax.dot_general(du_ref[k], wu_ref[k], _NT, preferred_element_type=F32)
        _norm_bwd_tail(acc, *tail)

    return pl.pallas_call(
        body, name=name, grid=(t // tm,), in_specs=[h_spec, h_spec, w_spec, w_spec] + tail_in, out_specs=tail_out,
        out_shape=[S((t, d), F32), S((1, d), F32)], compiler_params=_params(1))(dg, du, wg_all, wu_all, h, gain, dh_in)


def ffn_wgrad_up(name, a, dy):
    t, d = a.shape
    f4 = dy.shape[2]
    tk = _tile(t, 512)
    nt = t // tk

    def body(a_ref, dy_ref, o_ref, acc):
        s = pl.program_id(0)

        @pl.when(s == 0)
        def _():
            acc[...] = jnp.zeros_like(acc)

        at = a_ref[...].T
        for k in range(N_CHIPS):
            acc[k] += jnp.dot(at, dy_ref[k], preferred_element_type=F32)

        @pl.when(s == nt - 1)
        def _():
            o_ref[...] = acc[...].astype(BF16)

    return pl.pallas_call(
        body, name=name, grid=(nt,),
        in_specs=[pl.BlockSpec((tk, d), lambda s: (s, 0)), pl.BlockSpec((N_CHIPS, tk, f4), lambda s: (0, s, 0))],
        out_specs=pl.BlockSpec((N_CHIPS, d, f4), lambda s: (0, 0, 0)), out_shape=S((N_CHIPS, d, f4), BF16),
        scratch_shapes=[pltpu.VMEM((N_CHIPS, d, f4), F32)], compiler_params=_params(1))(a, dy)


def ffn_wgrad_down(name, z, dh):
    _, t, f4 = z.shape
    d = dh.shape[1]
    tk = _tile(t, 512)
    nt = t // tk

    def body(z_ref, dh_ref, o_ref, acc):
        s = pl.program_id(0)

        @pl.when(s == 0)
        def _():
            acc[...] = jnp.zeros_like(acc)

        dhb = dh_ref[...].astype(BF16)
        for k in range(N_CHIPS):
            acc[k] += lax.dot_general(z_ref[k], dhb, _TN, preferred_element_type=F32)

        @pl.when(s == nt - 1)
        def _():
            o_ref[...] = acc[...].astype(BF16)

    return pl.pallas_call(
        body, name=name, grid=(nt,),
        in_specs=[pl.BlockSpec((N_CHIPS, tk, f4), lambda s: (0, s, 0)), pl.BlockSpec((tk, d), lambda s: (s, 0))],
        out_specs=pl.BlockSpec((N_CHIPS, f4, d), lambda s: (0, 0, 0)), out_shape=S((N_CHIPS, f4, d), BF16),
        scratch_shapes=[pltpu.VMEM((N_CHIPS, f4, d), F32)], compiler_params=_params(1))(z, dh)


def conv_in_proj(name, a, w, conv_w):
    t, d = a.shape
    tm = _tile(t, 256)
    keep = 8

    def body(a_ref, w_ref, cw_ref, bcx_ref, y_ref, u_ref):
        @pl.when(pl.program_id(0) == 0)
        def _():
            u_ref[0:keep, :] = jnp.zeros((keep, d), F32)

        av = a_ref[...]
        b, c, x = [jnp.dot(av, w_ref[:, j * d:(j + 1) * d], preferred_element_type=F32) for j in range(3)]
        for j, part in enumerate((b, c, x)):
            bcx_ref[j] = part.astype(bcx_ref.dtype)
        u_ref[keep:keep + tm, :] = c * x
        uc = (cw_ref[0:1, :] * u_ref[keep - 2:keep - 2 + tm, :] + cw_ref[1:2, :] * u_ref[keep - 1:keep - 1 + tm, :]
              + cw_ref[2:3, :] * u_ref[keep:keep + tm, :])
        y_ref[...] = (b * uc).astype(BF16)
        u_ref[0:keep, :] = u_ref[tm:tm + keep, :]

    return pl.pallas_call(
        body, name=name, grid=(t // tm,),
        in_specs=[pl.BlockSpec((tm, d), lambda i: (i, 0)), _resident((d, 3 * d), lambda i: (0, 0)), pl.BlockSpec((3, d), lambda i: (0, 0))],
        out_specs=[pl.BlockSpec((3, tm, d), lambda i: (0, i, 0)), pl.BlockSpec((tm, d), lambda i: (i, 0))],
        out_shape=[S((3, t, d), CONV_SAVED_DTYPE), S((t, d), BF16)], scratch_shapes=[pltpu.VMEM((tm + keep, d), F32)],
        compiler_params=_params(1))(a, w, conv_w)


def conv_in_bwd_input(name, dbcx, w, h, gain, dh_in):
    _, t, d = dbcx.shape
    tm = _tile(t, 512)
    tail_in, tail_out = _norm_bwd_specs(tm, d)

    def body(g_ref, w_ref, *tail):
        acc = jnp.zeros((tm, d), F32)
        for j in range(3):
            acc = acc + lax.dot_general(g_ref[j], w_ref[:, j * d:(j + 1) * d], _NT, preferred_element_type=F32)
        _norm_bwd_tail(acc, *tail)

    return pl.pallas_call(
        body, name=name, grid=(t // tm,),
        in_specs=[pl.BlockSpec((3, tm, d), lambda i: (0, i, 0)), _resident((d, 3 * d), lambda i: (0, 0))] + tail_in,
        out_specs=tail_out, out_shape=[S((t, d), F32), S((1, d), F32)], compiler_params=_params(1))(dbcx, w, h, gain, dh_in)


def linear_nt_norm_bwd(name, dy, w, h, gain, dh_in, parts=()):
    t, n = dy.shape
    k = w.shape[0]
    tm = _tile(t, 512)
    tail_in, tail_out = _norm_bwd_specs(tm, k)
    m = len(parts)

    def body(dy_ref, w_ref, h_ref, g_ref, dhi_ref, *refs):
        if m:
            _ride_along(scatter_ici_copies(refs[:m], refs[m + 2:2 * m + 2], *refs[2 * m + 2:]), (pl.program_id(0),), (t // tm,))
        da = lax.dot_general(dy_ref[...].astype(BF16), w_ref[...], _NT, preferred_element_type=F32)
        _norm_bwd_tail(da, h_ref, g_ref, dhi_ref, *refs[m:m + 2])

    outs = pl.pallas_call(
        body, name=name, grid=(t // tm,),
        in_specs=[pl.BlockSpec((tm, n), lambda i: (i, 0)), _resident((k, n), lambda i: (0, 0))] + tail_in + [ANY] * m,
        out_specs=tail_out + [ANY] * m, out_shape=[S((t, k), F32), S((1, k), F32)] + [S(p.shape, p.dtype) for p in parts],
        scratch_shapes=[pltpu.SemaphoreType.DMA((m, 3)), pltpu.SemaphoreType.DMA((m, 3))] if m else [],
        compiler_params=_params(1))(dy, w, h, gain, dh_in, *parts)
    return outs[0], outs[1], list(outs[2:])


def conv_in_wgrad(name, a, dbcx):
    t, d = a.shape
    tk = _tile(t, 512)
    nt = t // tk
    n4 = 3 * d // N_CHIPS

    def body(a_ref, g_ref, o_ref, acc):
        s = pl.program_id(0)

        @pl.when(s == 0)
        def _():
            acc[...] = jnp.zeros_like(acc)

        at = a_ref[...].T
        for j in range(3):
            acc[:, j * d:(j + 1) * d] += jnp.dot(at, g_ref[j], preferred_element_type=F32)

        @pl.when(s == nt - 1)
        def _():
            for k in range(N_CHIPS):
                o_ref[k] = acc[:, k * n4:(k + 1) * n4].astype(BF16)

    return pl.pallas_call(
        body, name=name, grid=(nt,),
        in_specs=[pl.BlockSpec((tk, d), lambda s: (s, 0)), pl.BlockSpec((3, tk, d), lambda s: (0, s, 0))],
        out_specs=pl.BlockSpec((N_CHIPS, d, n4), lambda s: (0, 0, 0)), out_shape=S((N_CHIPS, d, n4), BF16),
        scratch_shapes=[pltpu.VMEM((d, 3 * d), F32)], compiler_params=_params(1))(a, dbcx)


def _rstd(x):
    return lax.rsqrt(jnp.mean(x * x, axis=-1, keepdims=True) + RMS_EPS)


def _rms_bwd(x, g, dy):
    r = _rstd(x)
    xhat = x * r
    dgain = jnp.sum(dy * xhat, axis=0, keepdims=True)
    dxh = dy * g
    dx = r * (dxh - xhat * jnp.mean(dxh * xhat, axis=-1, keepdims=True))
    return dx, dgain


def rms_fwd(name, h, g):
    t, d = h.shape
    tr = _tile(t, 512)

    def body(h_ref, g_ref, a_ref):
        x = h_ref[...]
        a_ref[...] = (x * _rstd(x) * g_ref[...]).astype(BF16)

    return pl.pallas_call(
        body, name=name, grid=(t // tr,),
        in_specs=[pl.BlockSpec((tr, d), lambda i: (i, 0)), pl.BlockSpec((1, d), lambda i: (0, 0))],
        out_specs=pl.BlockSpec((tr, d), lambda i: (i, 0)), out_shape=S((t, d), BF16), compiler_params=_params(1))(h, g)


def ffn_down_loss(name, z, wd_all, layer, resid, gain, target):
    _, t, f4 = z.shape
    d = wd_all.shape[2]
    tm = _tile(t, 512)

    def body(z_ref, wd_ref, r_ref, g_ref, t_ref, dh_ref, dg_ref, loss_ref):
        x = r_ref[...]
        for k in range(N_CHIPS):
            x = x + jnp.dot(z_ref[k], wd_ref[k], preferred_element_type=F32)
        g = g_ref[...]
        r = _rstd(x)
        xhat = x * r
        err = xhat * g - t_ref[...]
        dy = err * (1.0 / d)
        dxh = dy * g
        dh_ref[...] = r * (dxh - xhat * jnp.mean(dxh * xhat, axis=-1, keepdims=True))

        @pl.when(pl.program_id(0) == 0)
        def _():
            dg_ref[...] = jnp.zeros_like(dg_ref)
            loss_ref[...] = jnp.zeros_like(loss_ref)

        dg_ref[...] += jnp.sum(dy * xhat, axis=0, keepdims=True)
        per_token = jnp.mean(err * err, axis=-1, keepdims=True)
        loss_ref[...] += 0.5 * jnp.sum(per_token, axis=0, keepdims=True)

    row = pl.BlockSpec((tm, d), lambda i: (i, 0))
    vec = pl.BlockSpec((1, d), lambda i: (0, 0))
    one = pl.BlockSpec((1, 1), lambda i: (0, 0))
    return pl.pallas_call(
        body, name=name, grid=(t // tm,),
        in_specs=[pl.BlockSpec((N_CHIPS, tm, f4), lambda i: (0, i, 0)), _resident((N_CHIPS, f4, d), lambda i: (0, layer, 0)), row, vec, row],
        out_specs=[row, vec, one], out_shape=[S((t, d), F32), S((1, d), F32), S((1, 1), F32)],
        compiler_params=_params(1))(z, wd_all, resid, gain, target)


def mla_in_proj(name, a, w, g_cq, g_ckv, cos, sin):
    t, d = a.shape
    n = w.shape[1]
    ql, kl = g_cq.shape[1], g_ckv.shape[1]
    tr = _tile(t, 512)

    def body(a_ref, w_ref, gq_ref, gk_ref, c_ref, s_ref, p_ref, cq_ref, ckv_ref, kr_ref):
        p_ref[...] = jnp.dot(a_ref[...], w_ref[...], preferred_element_type=F32)
        xq = p_ref[:, 0:ql]
        cq_ref[...] = (xq * _rstd(xq) * gq_ref[...]).astype(BF16)
        xk = p_ref[:, ql:ql + kl]
        ckv_ref[...] = (xk * _rstd(xk) * gk_ref[...]).astype(BF16)
        k1 = p_ref[:, ql + kl:ql + kl + HALF]
        k2 = p_ref[:, ql + kl + HALF:ql + kl + ROPE]
        c = c_ref[...]
        s = s_ref[...]
        kr_ref[:, 0:HALF] = k1 * c - k2 * s
        kr_ref[:, HALF:ROPE] = k1 * s + k2 * c

    def row(w):
        return pl.BlockSpec((tr, w), lambda i: (i, 0))

    def vec(w):
        return pl.BlockSpec((1, w), lambda i: (0, 0))

    return pl.pallas_call(
        body, name=name, grid=(t // tr,),
        in_specs=[row(d), _resident((d, n), lambda i: (0, 0)), vec(ql), vec(kl), row(HALF), row(HALF)],
        out_specs=[row(n), row(ql), row(kl), row(ROPE)],
        out_shape=[S((t, n), F32), S((t, ql), BF16), S((t, kl), BF16), S((t, ROPE), F32)],
        compiler_params=_params(1))(a, w, g_cq, g_ckv, cos, sin)


def mla_mid_bwd(name, proj, g_cq, g_ckv, dcq, dckv, dkr, cos, sin):
    t, n = proj.shape
    ql, kl = g_cq.shape[1], g_ckv.shape[1]
    tr = _tile(t, 512)

    def body(p_ref, gq_ref, gk_ref, dcq_ref, dckv_ref, dkr_ref, c_ref, s_ref, dp_ref, dgq_ref, dgk_ref):
        dxq, dgq = _rms_bwd(p_ref[:, 0:ql], gq_ref[...], dcq_ref[...])
        dp_ref[:, 0:ql] = dxq.astype(BF16)
        dxk, dgk = _rms_bwd(p_ref[:, ql:ql + kl], gk_ref[...], dckv_ref[...])
        dp_ref[:, ql:ql + kl] = dxk.astype(BF16)
        d1 = dkr_ref[:, 0:HALF]
        d2 = dkr_ref[:, HALF:ROPE]
        c = c_ref[...]
        s = s_ref[...]
        dp_ref[:, ql + kl:ql + kl + HALF] = (d1 * c + d2 * s).astype(BF16)
        dp_ref[:, ql + kl + HALF:ql + kl + ROPE] = (d2 * c - d1 * s).astype(BF16)

        @pl.when(pl.program_id(0) == 0)
        def _():
            dgq_ref[...] = jnp.zeros_like(dgq_ref)
            dgk_ref[...] = jnp.zeros_like(dgk_ref)

        dgq_ref[...] += dgq
        dgk_ref[...] += dgk

    def row(w):
        return pl.BlockSpec((tr, w), lambda i: (i, 0))

    def vec(w):
        return pl.BlockSpec((1, w), lambda i: (0, 0))

    return pl.pallas_call(
        body, name=name, grid=(t // tr,),
        in_specs=[row(n), vec(ql), vec(kl), row(ql), row(kl), row(ROPE), row(HALF), row(HALF)],
        out_specs=[row(n), vec(ql), vec(kl)], out_shape=[S((t, n), BF16), S((1, ql), F32), S((1, kl), F32)],
        compiler_params=_params(1))(proj, g_cq, g_ckv, dcq, dckv, dkr, cos, sin)


def qkv_heads(name, cq, ckv, w_uq, w_ukv, kr, cos, sin, shards=()):
    t = cq.shape[0]
    tr = _tile(t, 256)
    n = len(shards)

    def body(cq_ref, ckv_ref, wq_ref, wkv_ref, kr_ref, c_ref, s_ref, *refs):
        src = refs[:n]
        qo_ref, ko_ref, vo_ref = refs[n:n + 3]
        q_ref, kv_ref = refs[2 * n + 3:2 * n + 5]
        if n:
            _ride_along(gather_ici_copies(src, refs[n + 3:2 * n + 3], *refs[2 * n + 5:]), (pl.program_id(0),), (t // tr,))
        q_ref[...] = jnp.dot(cq_ref[...], wq_ref[...], preferred_element_type=F32)
        kv_ref[...] = jnp.dot(ckv_ref[...], wkv_ref[...], preferred_element_type=F32).astype(BF16)
        c = c_ref[...]
        s = s_ref[...]
        krb = kr_ref[...].astype(BF16)
        for h in range(N_HEADS):
            q0 = h * QK
            qo_ref[h, :, 0:NOPE] = q_ref[:, q0:q0 + NOPE].astype(BF16)
            q1 = q_ref[:, q0 + NOPE:q0 + NOPE + HALF]
            q2 = q_ref[:, q0 + NOPE + HALF:q0 + QK]
            qo_ref[h, :, NOPE:NOPE + HALF] = (q1 * c - q2 * s).astype(BF16)
            qo_ref[h, :, NOPE + HALF:QK] = (q1 * s + q2 * c).astype(BF16)
            k0 = h * (NOPE + VDIM)
            ko_ref[h, :, 0:NOPE] = kv_ref[:, k0:k0 + NOPE]
            ko_ref[h, :, NOPE:QK] = krb
            vo_ref[h] = kv_ref[:, k0 + NOPE:k0 + NOPE + VDIM]

    def row(w):
        return pl.BlockSpec((tr, w), lambda i: (i, 0))

    def heads(w):
        return pl.BlockSpec((N_HEADS, tr, w), lambda i: (0, i, 0))

    outs = pl.pallas_call(
        body, name=name, grid=(t // tr,),
        in_specs=[row(cq.shape[1]), row(ckv.shape[1]), _resident(w_uq.shape, lambda i: (0, 0)), _resident(w_ukv.shape, lambda i: (0, 0)),
                  row(ROPE), row(HALF), row(HALF)] + [ANY] * n,
        out_specs=[heads(QK), heads(QK), heads(VDIM)] + [ANY] * n,
        out_shape=[S((N_HEADS, t, QK), BF16), S((N_HEADS, t, QK), BF16), S((N_HEADS, t, VDIM), BF16)]
        + [S((N_CHIPS,) + s.shape, s.dtype) for s in shards],
        scratch_shapes=[pltpu.VMEM((tr, N_HEADS * QK), F32), pltpu.VMEM((tr, N_HEADS * (NOPE + VDIM)), BF16)]
        + ([pltpu.SemaphoreType.DMA((n, 3)), pltpu.SemaphoreType.DMA((n, 3))] if n else []),
        compiler_params=_params(1))(cq, ckv, w_uq, w_ukv, kr, cos, sin, *shards)
    return outs[0], outs[1], outs[2], list(outs[3:])


def qkv_heads_bwd(name, dq_h, dk_h, dv_h, cos, sin, halves=(), targets=(), where=()):
    t = dq_h.shape[1]
    tr = _tile(t, 256)
    n, nt = len(halves), len(targets)

    def body(dq_ref, dk_ref, dv_ref, c_ref, s_ref, *refs):
        q_ref, kv_ref, kr_ref = refs[n:n + 3]
        if n:
            src, dst = refs[:n], refs[n + 3:n + 3 + nt]
            stages = refs[n + 3 + nt:2 * n + 3 + nt]
            send_sems, recv_sems, local_sems = refs[2 * n + 3 + nt:]
            c = lax.axis_index("c")
            own = [(src[i], stages[i],
                    dst[where[i][0]].at[pl.ds(pl.multiple_of(where[i][1] + c * src[i].shape[0], 8), src[i].shape[0])],
                    local_sems.at[i]) for i in range(n)]
            _ride_along(join_copies(src, dst, where, send_sems, recv_sems), (pl.program_id(0),), (t // tr,), own)
        c = c_ref[...]
        s = s_ref[...]
        dkr = jnp.zeros((tr, ROPE), F32)
        for h in range(N_HEADS):
            q0 = h * QK
            q_ref[:, q0:q0 + NOPE] = dq_ref[h, :, 0:NOPE].astype(BF16)
            d1 = dq_ref[h, :, NOPE:NOPE + HALF]
            d2 = dq_ref[h, :, NOPE + HALF:QK]
            q_ref[:, q0 + NOPE:q0 + NOPE + HALF] = (d1 * c + d2 * s).astype(BF16)
            q_ref[:, q0 + NOPE + HALF:q0 + QK] = (d2 * c - d1 * s).astype(BF16)
            k0 = h * (NOPE + VDIM)
            kv_ref[:, k0:k0 + NOPE] = dk_ref[h, :, 0:NOPE].astype(BF16)
            kv_ref[:, k0 + NOPE:k0 + NOPE + VDIM] = dv_ref[h].astype(BF16)
            dkr = dkr + dk_ref[h, :, NOPE:QK]
        kr_ref[...] = dkr

    def row(w):
        return pl.BlockSpec((tr, w), lambda i: (i, 0))

    def heads(w):
        return pl.BlockSpec((N_HEADS, tr, w), lambda i: (0, i, 0))

    outs = pl.pallas_call(
        body, name=name, grid=(t // tr,),
        in_specs=[heads(QK), heads(QK), heads(VDIM), row(HALF), row(HALF)] + [ANY] * n,
        out_specs=[row(N_HEADS * QK), row(N_HEADS * (NOPE + VDIM)), row(ROPE)] + [ANY] * nt,
        out_shape=[S((t, N_HEADS * QK), BF16), S((t, N_HEADS * (NOPE + VDIM)), BF16), S((t, ROPE), F32)]
        + [S(tg, F32) for tg in targets],
        scratch_shapes=[pltpu.VMEM(h.shape, h.dtype) for h in halves]
        + ([pltpu.SemaphoreType.DMA((n,)), pltpu.SemaphoreType.DMA((n,)), pltpu.SemaphoreType.DMA((n, 2))] if n else []),
        compiler_params=_params(1))(dq_h, dk_h, dv_h, cos, sin, *halves)
    return outs[0], outs[1], outs[2], list(outs[3:])


def _chunk_mask_t(q_start, k_start, bq, bk):
    kc = (k_start + lax.broadcasted_iota(jnp.int32, (bk, bq), 0)) // CHUNK
    qc = (q_start + lax.broadcasted_iota(jnp.int32, (bk, bq), 1)) // CHUNK
    return kc <= qc


def attention_fwd(name, q, k, v, shards=()):
    nh, t, _ = q.shape
    blk = ATT_BLOCK
    nq = t // blk
    n = len(shards)

    def body(q_ref, k_ref, v_ref, *refs):
        src = refs[:n]
        o_ref, lse_ref = refs[n:n + 2]
        dst = refs[n + 2:2 * n + 2]
        m_ref, l_ref, acc_ref, s_buf, p_buf, alpha_buf, bias_ref = refs[2 * n + 2:2 * n + 9]
        i = pl.program_id(1)
        if n:
            send_sems, recv_sems = refs[2 * n + 9:]
            _ride_along(gather_ici_copies(src, dst, send_sems, recv_sems), (pl.program_id(0), i), (nh, nq))

        @pl.when((pl.program_id(0) == 0) & (i == 0))
        def _():
            bias_ref[...] = jnp.where(_chunk_mask_t(0, 0, blk, blk), 0.0, MASK_VALUE)

        m_ref[...] = jnp.full_like(m_ref, MASK_VALUE)
        l_ref[...] = jnp.zeros_like(l_ref)
        acc_ref[...] = jnp.zeros_like(acc_ref)

        def rows(b):
            return pl.ds(pl.multiple_of(b * blk, blk), blk)

        def scores(b, slot):
            s_buf[slot] = lax.dot_general(k_ref[rows(b), :], q_ref[...], _NT, preferred_element_type=F32)

        def softmax(slot, diagonal):
            s = s_buf[slot]
            if diagonal:
                s = s + bias_ref[...]
            m_old = m_ref[...]
            m_new = jnp.maximum(m_old, jnp.max(s, axis=0, keepdims=True))
            p = jnp.exp2((s - m_new) * SCORE_SCALE_LOG2)
            alpha = jnp.exp2((m_old - m_new) * SCORE_SCALE_LOG2)
            l_ref[...] = alpha * l_ref[...] + jnp.sum(p, axis=0, keepdims=True)
            m_ref[...] = m_new
            alpha_buf[slot] = alpha
            p_buf[slot] = p.astype(BF16)

        def values(b, slot):
            pv = lax.dot_general(v_ref[rows(b), :], p_buf[slot], _TN, preferred_element_type=F32)
            acc_ref[...] = alpha_buf[slot] * acc_ref[...] + pv

        def step(t, slot):
            values(t - 2, slot)
            softmax(1 - slot, False)
            scores(t, slot)

        scores(0, 0)

        @pl.when(i == 0)
        def _():
            softmax(0, True)
            values(0, 0)

        @pl.when(i > 0)
        def _():
            scores(1, 1)
            softmax(0, False)
            steady = i - 1

            def pair(u, carry):
                step(2 + 2 * u, 0)
                step(3 + 2 * u, 1)
                return carry

            lax.fori_loop(0, steady // 2, pair, 0)

            @pl.when(steady % 2 == 1)
            def _():
                step(i, 0)

            last = i % 2
            softmax(last, True)
            values(i - 1, 1 - last)
            values(i, last)

        l = l_ref[...]
        o_ref[...] = (acc_ref[...] / l).T
        lse_ref[...] = m_ref[...] * SCORE_SCALE + jnp.log(l)

    outs = pl.pallas_call(
        body, name=name, grid=(nh, nq),
        in_specs=[pl.BlockSpec((None, blk, QK), lambda h, i: (h, i, 0)), pl.BlockSpec((None, t, QK), lambda h, i: (h, 0, 0)),
                  pl.BlockSpec((None, t, VDIM), lambda h, i: (h, 0, 0))] + [ANY] * n,
        out_specs=[pl.BlockSpec((blk, VDIM), lambda h, i: (i, h)),
                   pl.BlockSpec((None, None, 1, blk), lambda h, i: (h, i, 0, 0))] + [ANY] * n,
        out_shape=[S((t, nh * VDIM), F32), S((nh, nq, 1, blk), F32)] + [S((N_CHIPS,) + s.shape, s.dtype) for s in shards],
        scratch_shapes=[pltpu.VMEM((1, blk), F32), pltpu.VMEM((1, blk), F32), pltpu.VMEM((VDIM, blk), F32),
                        pltpu.VMEM((2, blk, blk), F32), pltpu.VMEM((2, blk, blk), BF16), pltpu.VMEM((2, 1, blk), F32),
                        pltpu.VMEM((blk, blk), F32)]
        + ([pltpu.SemaphoreType.DMA((n, 3)), pltpu.SemaphoreType.DMA((n, 3))] if n else []),
        compiler_params=_params(2))(q, k, v, *shards)
    return outs[0], outs[1], list(outs[2:])


def attention_out_bwd(name, dh, w_o, o, swap=()):
    t, d = dh.shape
    n = w_o.shape[0]
    blk = ATT_BLOCK
    m = len(swap)

    def body(dh_ref, w_ref, o_ref, *refs):
        do_ref, d_ref = refs[m:m + 2]
        if m:
            _ride_along(swap_copies(refs[:m], refs[m + 2:2 * m + 2], *refs[2 * m + 2:]), (pl.program_id(0),), (t // blk,))
        do_ref[...] = lax.dot_general(dh_ref[...].astype(BF16), w_ref[...], _NT, preferred_element_type=F32)
        for h in range(N_HEADS):
            cols = slice(h * VDIM, (h + 1) * VDIM)
            d_ref[h] = jnp.sum((do_ref[:, cols] * o_ref[:, cols]).T, axis=0, keepdims=True)

    tile = pl.BlockSpec((blk, n), lambda i: (i, 0))
    outs = pl.pallas_call(
        body, name=name, grid=(t // blk,),
        in_specs=[pl.BlockSpec((blk, d), lambda i: (i, 0)), _resident((n, d), lambda i: (0, 0)), tile] + [ANY] * m,
        out_specs=[tile, pl.BlockSpec((N_HEADS, None, 1, blk), lambda i: (0, i, 0, 0))] + [ANY] * m,
        out_shape=[S((t, n), F32), S((N_HEADS, t // blk, 1, blk), F32)] + _swap_shapes(swap),
        scratch_shapes=[pltpu.SemaphoreType.DMA((m,)), pltpu.SemaphoreType.DMA((m,))] if m else [],
        compiler_params=_params(1))(dh, w_o, o, *swap)
    return outs[0], outs[1], list(outs[2:])


def attention_bwd(name, q, k, v, do, lse, delta, parts=()):
    nh, t, _ = q.shape
    blk = ATT_BLOCK
    nq = t // blk
    n_pairs = nq * (nq + 1) // 2
    n = len(parts)
    scale = SCORE_SCALE

    def body(q_ref, k_ref, v_ref, do_ref, lse_ref, dl_ref, *refs):
        src = refs[:n]
        dq_out, dk_out, dv_out = refs[n:n + 3]
        dst = refs[n + 3:2 * n + 3]
        s_buf, dp_buf, p_buf, ds_buf, bias_ref, dq_ref, dk_ref, dv_ref = refs[2 * n + 3:2 * n + 11]
        if n:
            send_sems, recv_sems = refs[2 * n + 11:]
            _ride_along(scatter_ici_copies(src, dst, send_sems, recv_sems), (pl.program_id(0),), (nh,))

        @pl.when(pl.program_id(0) == 0)
        def _():
            bias_ref[...] = jnp.where(_chunk_mask_t(0, 0, blk, blk), 0.0, MASK_VALUE)

        dq_ref[...] = jnp.zeros_like(dq_ref)
        dk_ref[...] = jnp.zeros_like(dk_ref)
        dv_ref[...] = jnp.zeros_like(dv_ref)

        def rows(x):
            return pl.ds(pl.multiple_of(x * blk, blk), blk)

        def after(jb):
            j, b = jb
            wrap = b == nq - 1 - j
            return jnp.where(wrap, j + 1, j), jnp.where(wrap, 0, b + 1)

        def products(jb, slot):
            j, b = jb
            s_buf[slot] = lax.dot_general(k_ref[rows(j), :], q_ref[rows(j + b), :], _NT, preferred_element_type=F32)
            dp_buf[slot] = lax.dot_general(v_ref[rows(j), :], do_ref[rows(j + b), :].astype(BF16), _NT, preferred_element_type=F32)

        def softmax_bwd(jb, slot):
            j, b = jb
            s = s_buf[slot] + bias_ref[...] * (b == 0).astype(F32)
            p = jnp.exp2(s * SCORE_SCALE_LOG2 - lse_ref[j + b] * LOG2_E)
            p_buf[slot] = p.astype(BF16)
            ds_buf[slot] = (p * (dp_buf[slot] - dl_ref[j + b]) * scale).astype(BF16)

        def gradients(jb, slot):
            j, b = jb
            dv_ref[rows(j), :] += jnp.dot(p_buf[slot], do_ref[rows(j + b), :].astype(BF16), preferred_element_type=F32)
            dk_ref[rows(j), :] += jnp.dot(ds_buf[slot], q_ref[rows(j + b), :], preferred_element_type=F32)
            dq_ref[rows(j + b), :] += lax.dot_general(ds_buf[slot], k_ref[rows(j), :], _TN, preferred_element_type=F32)

        def step(state, slot):
            third, second, first = state
            gradients(third, slot)
            softmax_bwd(second, 1 - slot)
            products(first, slot)
            return second, first, after(first)

        zero = jnp.int32(0)
        pair0 = (zero, zero)
        products(pair0, 0)
        if n_pairs == 1:
            softmax_bwd(pair0, 0)
            gradients(pair0, 0)
        else:
            pair1 = after(pair0)
            products(pair1, 1)
            softmax_bwd(pair0, 0)
            steady = n_pairs - 2
            state = lax.fori_loop(0, steady // 2, lambda u, st: step(step(st, 0), 1), (pair0, pair1, after(pair1)))
            if steady % 2:
                state = step(state, 0)
            before_last, last_pair, _ = state
            last = (n_pairs - 1) % 2
            softmax_bwd(last_pair, last)
            gradients(before_last, 1 - last)
            gradients(last_pair, last)
        dq_out[...] = dq_ref[...].astype(BF16)
        dk_out[...] = dk_ref[...].astype(BF16)
        dv_out[...] = dv_ref[...].astype(BF16)

    head = lambda w: pl.BlockSpec((None, t, w), lambda h: (h, 0, 0))
    stats = pl.BlockSpec((None, nq, 1, blk), lambda h: (h, 0, 0, 0))
    outs = pl.pallas_call(
        body, name=name, grid=(nh,),
        in_specs=[head(QK), head(QK), head(VDIM), pl.BlockSpec((t, VDIM), lambda h: (0, h)), stats, stats] + [ANY] * n,
        out_specs=[head(QK), head(QK), head(VDIM)] + [ANY] * n,
        out_shape=[S((nh, t, QK), BF16), S((nh, t, QK), BF16), S((nh, t, VDIM), BF16)] + [S(p.shape, p.dtype) for p in parts],
        scratch_shapes=[pltpu.VMEM((2, blk, blk), F32), pltpu.VMEM((2, blk, blk), F32), pltpu.VMEM((2, blk, blk), BF16),
                        pltpu.VMEM((2, blk, blk), BF16), pltpu.VMEM((blk, blk), F32),
                        pltpu.VMEM((t, QK), F32), pltpu.VMEM((t, QK), F32), pltpu.VMEM((t, VDIM), F32)]
        + ([pltpu.SemaphoreType.DMA((n, 3)), pltpu.SemaphoreType.DMA((n, 3))] if n else []),
        compiler_params=_params(1, VMEM_LIMIT_WHOLE_HEAD))(q, k, v, do, lse, delta, *parts)
    return outs[0], outs[1], outs[2], list(outs[3:])


def _shift_down(u, s):
    rows = lax.broadcasted_iota(jnp.int32, u.shape, 0)
    return jnp.where(rows >= s, pltpu.roll(u, s, 0), 0.0)


def _shift_up(u, s):
    n = u.shape[0]
    rows = lax.broadcasted_iota(jnp.int32, u.shape, 0)
    return jnp.where(rows < n - s, pltpu.roll(u, n - s, 0), 0.0)


def _conv_specs(t, d, lanes):
    slab = lambda part: pl.BlockSpec((None, t, lanes), lambda j, part=part: (part, 0, j))
    return slab, pl.BlockSpec((3, lanes), lambda j: (0, j)), pl.BlockSpec((t, lanes), lambda j: (0, j))


def conv_bwd(name, bcx, w, dy):
    _, t, d = bcx.shape
    lanes = _tile(d, 128, 128)
    slab, w_spec, col = _conv_specs(t, d, lanes)

    def body(b_ref, c_ref, x_ref, w_ref, dy_ref, d_ref, dw_ref):
        c = c_ref[...].astype(F32)
        x = x_ref[...].astype(F32)
        dyv = dy_ref[...]
        u = c * x
        u1 = _shift_down(u, 1)
        u2 = _shift_down(u, 2)
        w0, w1, w2 = w_ref[0:1, :], w_ref[1:2, :], w_ref[2:3, :]
        d_ref[0] = (dyv * (w0 * u2 + w1 * u1 + w2 * u)).astype(BF16)
        duc = dyv * b_ref[...].astype(F32)
        dw_ref[0:1, :] = jnp.sum(duc * u2, axis=0, keepdims=True)
        dw_ref[1:2, :] = jnp.sum(duc * u1, axis=0, keepdims=True)
        dw_ref[2:3, :] = jnp.sum(duc * u, axis=0, keepdims=True)
        du = w2 * duc + w1 * _shift_up(duc, 1) + w0 * _shift_up(duc, 2)
        d_ref[1] = (du * x).astype(BF16)
        d_ref[2] = (du * c).astype(BF16)

    return pl.pallas_call(
        body, name=name, grid=(d // lanes,), in_specs=[slab(0), slab(1), slab(2), w_spec, col],
        out_specs=[pl.BlockSpec((3, t, lanes), lambda j: (0, 0, j)), w_spec], out_shape=[S((3, t, d), BF16), S((3, d), F32)],
        compiler_params=_params(1))(bcx, bcx, bcx, w, dy)


def _adamw_update(w, g, m, v):
    m_new = ADAM_B1 * m + (1.0 - ADAM_B1) * g
    v_new = ADAM_B2 * v + (1.0 - ADAM_B2) * (g * g)
    m_hat = m_new / (1.0 - ADAM_B1 ** ADAM_STEP)
    v_hat = v_new / (1.0 - ADAM_B2 ** ADAM_STEP)
    return -ADAM_LR * (m_hat / (jnp.sqrt(v_hat) + ADAM_EPS) + ADAM_WD * w), m_new, v_new


def adamw(name, w, g, m, v):
    r, c = w.shape
    tr = _tile(r, 512)

    def body(w_ref, g_ref, m_ref, v_ref, d_ref, mo_ref, vo_ref):
        d_ref[...], mo_ref[...], vo_ref[...] = _adamw_update(w_ref[...], g_ref[...], m_ref[...], v_ref[...])

    blk = pl.BlockSpec((tr, c), lambda i: (i, 0))
    return pl.pallas_call(
        body, name=name, grid=(r // tr,), in_specs=[blk] * 4, out_specs=[blk] * 3, out_shape=[S((r, c), F32)] * 3,
        compiler_params=_params(1))(w, g, m, v)


def adamw_swapped(name, wt, g, mt, vt):
    nl, c, r = wt.shape
    tr = _tile(r, 512, 128)
    nr = r // tr

    def body(w_ref, g_ref, m_ref, v_ref, go_ref, d_ref, mo_ref, vo_ref):
        gt = g_ref[...].T
        go_ref[...] = gt
        d_ref[...], mo_ref[...], vo_ref[...] = _adamw_update(w_ref[...], gt, m_ref[...], v_ref[...])

    swapped = pl.BlockSpec((None, c, tr), lambda l, i: (l, 0, i))
    return pl.pallas_call(
        body, name=name, grid=(nl, nr),
        in_specs=[swapped, pl.BlockSpec((tr, c), lambda l, i: (l * nr + i, 0)), swapped, swapped],
        out_specs=[swapped] * 4, out_shape=[S((nl, c, r), F32)] * 4, compiler_params=_params(2))(wt, g, mt, vt)


def _place():
    x, y, c = lax.axis_index("x"), lax.axis_index("y"), lax.axis_index("c")
    other_chips = [(1 - x, y), (x, 1 - y), (1 - x, 1 - y)]
    return x, y, c, other_chips


def _half(c, rows):
    return pl.ds(pl.multiple_of(c * (rows // 2), 16), rows // 2)


def gather_weight_shards(shards, small=()):
    n, ns = len(shards), len(small)

    def body(*refs):
        src = refs[:n]
        small_src = refs[n:n + ns]
        dst = refs[n + ns:2 * n + ns]
        small_dst = refs[2 * n + ns:2 * (n + ns)]
        send_sems, recv_sems, small_send, small_recv, local_sems = refs[2 * (n + ns):2 * (n + ns) + 5]
        stages = refs[2 * (n + ns) + 5:]
        x, y, c, chips = _place()
        me = 2 * x + y
        sibling = (x, y, 1 - c)
        own = [(s_ref, stages[i], d_ref.at[me], local_sems.at[i])
               for i, (s_ref, d_ref) in enumerate(zip(list(src) + list(small_src), list(dst) + list(small_dst)))]
        for s_ref, stage, _, sems in own:
            pltpu.make_async_copy(s_ref, stage, sems.at[0]).start()
        small_pairs = []
        for i in range(ns):
            for j, (px, py) in enumerate(chips):
                def whole(slot):
                    return pltpu.make_async_remote_copy(
                        src_ref=small_src[i], dst_ref=small_dst[i].at[slot], send_sem=small_send.at[i, j],
                        recv_sem=small_recv.at[i, j], device_id=(px, py, c), device_id_type=MESH)
                small_pairs.append((whole(me), whole(2 * px + py)))
        for outgoing, _ in small_pairs:
            outgoing.start()

        def copy(i, slot, half_of, sem, to, from_input=False):
            rows = _half(half_of, src[i].shape[0])
            return pltpu.make_async_remote_copy(
                src_ref=src[i].at[rows] if from_input else dst[i].at[slot, rows], dst_ref=dst[i].at[slot, rows],
                send_sem=send_sems.at[i, sem], recv_sem=recv_sems.at[i, sem], device_id=to, device_id_type=MESH)

        sent = []
        for i in range(n):
            for j, chip in enumerate(chips):
                sent.append(copy(i, me, c, j, (*chip, c), from_input=True))
                sent[-1].start()
        for i in range(n):
            for j, (px, py) in enumerate(chips):
                copy(i, 2 * px + py, c, j, sibling).wait_recv()
                sent.append(copy(i, 2 * px + py, c, 3 + j, sibling))
                sent[-1].start()
        for i in range(n):
            for j, (px, py) in enumerate(chips):
                copy(i, 2 * px + py, 1 - c, 3 + j, sibling).wait_recv()
        for cp in sent:
            cp.wait_send()
        for _, incoming in small_pairs:
            incoming.wait_recv()
        for outgoing, _ in small_pairs:
            outgoing.wait_send()
        _place_locally(own)

    everything = list(shards) + list(small)
    outs = pl.pallas_call(
        body, name="gather_weight_shards", in_specs=[ANY] * (n + ns), out_specs=[ANY] * (n + ns),
        out_shape=[S((N_CHIPS,) + s.shape, s.dtype) for s in everything],
        scratch_shapes=[pltpu.SemaphoreType.DMA((n, 6)), pltpu.SemaphoreType.DMA((n, 6)),
                        pltpu.SemaphoreType.DMA((max(ns, 1), 3)), pltpu.SemaphoreType.DMA((max(ns, 1), 3)),
                        pltpu.SemaphoreType.DMA((n + ns, 2))] + [pltpu.VMEM(s.shape, s.dtype) for s in everything],
    )(*shards, *small)
    return list(outs[:n]), list(outs[n:])


def gather_ici_copies(src, dst, send_sems, recv_sems):
    x, y, c, chips = _place()
    me = 2 * x + y
    pairs = []
    for i in range(len(src)):
        rows = _half(c, src[i].shape[0])
        for j, (px, py) in enumerate(chips):
            def copy(slot):
                return pltpu.make_async_remote_copy(
                    src_ref=src[i].at[rows], dst_ref=dst[i].at[slot, rows], send_sem=send_sems.at[i, j],
                    recv_sem=recv_sems.at[i, j], device_id=(px, py, c), device_id_type=MESH)
            pairs.append((copy(me), copy(2 * px + py)))
    return pairs


def scatter_ici_copies(src, dst, send_sems, recv_sems):
    x, y, c, chips = _place()
    me = 2 * x + y
    pairs = []
    for i in range(len(src)):
        for j, (px, py) in enumerate(chips):
            def copy(from_slot, to_slot):
                return pltpu.make_async_remote_copy(
                    src_ref=src[i].at[from_slot], dst_ref=dst[i].at[to_slot], send_sem=send_sems.at[i, j],
                    recv_sem=recv_sems.at[i, j], device_id=(px, py, c), device_id_type=MESH)
            pairs.append((copy(2 * px + py, me), copy(me, 2 * px + py)))
    return pairs


def _ride_along(pairs, grid_ids, grid_sizes, local=()):
    first = grid_ids[0] == 0
    last = grid_ids[0] == grid_sizes[0] - 1
    for g, size in zip(grid_ids[1:], grid_sizes[1:]):
        first = first & (g == 0)
        last = last & (g == size - 1)

    @pl.when(first)
    def _():
        for outgoing, _ in pairs:
            outgoing.start()
        for src, stage, _, sems in local:
            pltpu.make_async_copy(src, stage, sems.at[0]).start()

    @pl.when(last)
    def _():
        for _, incoming in pairs:
            incoming.wait_recv()
        for outgoing, _ in pairs:
            outgoing.wait_send()
        _place_locally(local)


def _place_locally(local):
    for src, stage, _, sems in local:
        pltpu.make_async_copy(src, stage, sems.at[0]).wait()
    placed = [pltpu.make_async_copy(stage, dst, sems.at[1]) for _, stage, dst, sems in local]
    for cp in placed:
        cp.start()
    for cp in placed:
        cp.wait()


def forward_copies(src, dst, send_sems, recv_sems):
    x, y, c, chips = _place()
    pairs = []
    for i in range(len(src)):
        for j, (px, py) in enumerate(chips):
            def copy(half_of):
                rows = _half(half_of, src[i].shape[1])
                return pltpu.make_async_remote_copy(
                    src_ref=src[i].at[2 * px + py, rows], dst_ref=dst[i].at[2 * px + py, rows], send_sem=send_sems.at[i, j],
                    recv_sem=recv_sems.at[i, j], device_id=(x, y, 1 - c), device_id_type=MESH)
            pairs.append((copy(c), copy(1 - c)))
    return pairs


def attention_out_proj(name, attn, w_o, resid, next_gain, arriving, own):
    t, kdim = attn.shape
    n = w_o.shape[1]
    tm = _tile(t, 512)
    m = len(arriving)

    def body(x_ref, w_ref, r_ref, g_ref, *refs):
        src, own_refs = refs[:m], refs[m:2 * m]
        h_ref, a_ref = refs[2 * m:2 * m + 2]
        dst = refs[2 * m + 2:3 * m + 2]
        stages = refs[3 * m + 2:4 * m + 2]
        send_sems, recv_sems, local_sems = refs[4 * m + 2:]
        me = 2 * lax.axis_index("x") + lax.axis_index("y")
        placed = [(own_refs[i], stages[i], dst[i].at[me], local_sems.at[i]) for i in range(m)]
        _ride_along(forward_copies(src, dst, send_sems, recv_sems), (pl.program_id(0),), (t // tm,), placed)
        y = r_ref[...] + jnp.dot(x_ref[...].astype(BF16), w_ref[...], preferred_element_type=F32)
        h_ref[...] = y
        a_ref[...] = (y * _rstd(y) * g_ref[...]).astype(BF16)

    row = pl.BlockSpec((tm, n), lambda i: (i, 0))
    outs = pl.pallas_call(
        body, name=name, grid=(t // tm,),
        in_specs=[pl.BlockSpec((tm, kdim), lambda i: (i, 0)), _resident((kdim, n), lambda i: (0, 0)), row,
                  pl.BlockSpec((1, n), lambda i: (0, 0))] + [ANY] * (2 * m),
        out_specs=[row, row] + [ANY] * m, out_shape=[S((t, n), F32), S((t, n), BF16)] + [S(g.shape, g.dtype) for g in arriving],
        input_output_aliases={4 + i: 2 + i for i in range(m)},
        scratch_shapes=[pltpu.VMEM(o.shape, o.dtype) for o in own]
        + [pltpu.SemaphoreType.DMA((m, 3)), pltpu.SemaphoreType.DMA((m, 3)), pltpu.SemaphoreType.DMA((m, 2))],
        compiler_params=_params(1))(attn, w_o, resid, next_gain, *arriving, *own)
    return outs[0], outs[1], list(outs[2:])


def swap_copies(src, dst, send_sems, recv_sems):
    x, y, c, _ = _place()
    pairs = []
    for i in range(len(src)):
        cp = pltpu.make_async_remote_copy(
            src_ref=src[i].at[:, _half(1 - c, src[i].shape[1]), :], dst_ref=dst[i], send_sem=send_sems.at[i],
            recv_sem=recv_sems.at[i], device_id=(x, y, 1 - c), device_id_type=MESH)
        pairs.append((cp, cp))
    return pairs


def _swap_shapes(grads):
    return [S((g.shape[0], g.shape[1] // 2, g.shape[2]), g.dtype) for g in grads]


def sibling_swap_halves(name, grads):
    n = len(grads)

    def body(*refs):
        pairs = swap_copies(refs[:n], refs[n:2 * n], *refs[2 * n:])
        for outgoing, _ in pairs:
            outgoing.start()
        for _, incoming in pairs:
            incoming.wait_recv()
        for outgoing, _ in pairs:
            outgoing.wait_send()

    return pl.pallas_call(
        body, name=name, in_specs=[ANY] * n, out_specs=[ANY] * n, out_shape=_swap_shapes(grads),
        scratch_shapes=[pltpu.SemaphoreType.DMA((n,)), pltpu.SemaphoreType.DMA((n,))],
    )(*grads)


def add_halves(name, g, rx):
    _, r, cdim = g.shape
    r2 = r // 2
    tr = _tile(r2, 512, 16)
    nb = r2 // tr

    def body(lo_ref, hi_ref, rx_ref, o_ref):
        mine = jnp.where(lax.axis_index("c") == 0, lo_ref[...], hi_ref[...])
        o_ref[...] = (mine.astype(F32) + rx_ref[...].astype(F32)).astype(BF16)

    half = pl.BlockSpec((None, tr, cdim), lambda k, i: (k, i, 0))
    return pl.pallas_call(
        body, name=name, grid=(N_CHIPS, nb),
        in_specs=[half, pl.BlockSpec((None, tr, cdim), lambda k, i: (k, nb + i, 0)), half],
        out_specs=half, out_shape=S((N_CHIPS, r2, cdim), BF16), compiler_params=_params(2))(g, g, rx)


def sum_chips(name, arrived, mine):
    _, r2, cdim = arrived.shape
    tr = _tile(r2, 512, 16)

    def body(a_ref, m_ref, o_ref):
        me = 2 * lax.axis_index("x") + lax.axis_index("y")
        acc = jnp.zeros((tr, cdim), F32)
        for k in range(N_CHIPS):
            acc = acc + jnp.where(me == k, m_ref[k], a_ref[k]).astype(F32)
        o_ref[...] = acc

    slots = pl.BlockSpec((N_CHIPS, tr, cdim), lambda i: (0, i, 0))
    return pl.pallas_call(
        body, name=name, grid=(r2 // tr,), in_specs=[slots, slots],
        out_specs=pl.BlockSpec((tr, cdim), lambda i: (i, 0)), out_shape=S((r2, cdim), F32), compiler_params=_params(1))(arrived, mine)


def join_copies(src, dst, where, send_sems, recv_sems):
    x, y, c, _ = _place()
    pairs = []
    for i in range(len(src)):
        def copy(half_of):
            r2 = src[i].shape[0]
            rows = pl.ds(pl.multiple_of(where[i][1] + half_of * r2, 8), r2)
            return pltpu.make_async_remote_copy(
                src_ref=src[i], dst_ref=dst[where[i][0]].at[rows], send_sem=send_sems.at[i],
                recv_sem=recv_sems.at[i], device_id=(x, y, 1 - c), device_id_type=MESH)
        pairs.append((copy(c), copy(1 - c)))
    return pairs


def sibling_join_halves(name, halves, targets, where):
    n = len(halves)

    def body(*refs):
        src, dst = refs[:n], refs[n:n + len(targets)]
        send_sems, recv_sems, local_sems = refs[n + len(targets):n + len(targets) + 3]
        stages = refs[n + len(targets) + 3:]
        c = lax.axis_index("c")
        own = [(src[i], stages[i],
                dst[where[i][0]].at[pl.ds(pl.multiple_of(where[i][1] + c * src[i].shape[0], 8), src[i].shape[0])],
                local_sems.at[i]) for i in range(n)]
        for s_ref, stage, _, sems in own:
            pltpu.make_async_copy(s_ref, stage, sems.at[0]).start()
        pairs = join_copies(src, dst, where, send_sems, recv_sems)
        for outgoing, _ in pairs:
            outgoing.start()
        for _, incoming in pairs:
            incoming.wait_recv()
        for outgoing, _ in pairs:
            outgoing.wait_send()
        _place_locally(own)

    return list(pl.pallas_call(
        body, name=name, in_specs=[ANY] * n, out_specs=[ANY] * len(targets), out_shape=[S(tg, F32) for tg in targets],
        scratch_shapes=[pltpu.SemaphoreType.DMA((n,)), pltpu.SemaphoreType.DMA((n,)), pltpu.SemaphoreType.DMA((n, 2))]
        + [pltpu.VMEM(h.shape, h.dtype) for h in halves],
    )(*halves))


def all_reduce_small(name, packed):
    rows, width = packed.shape

    def body(x_ref, o_ref, gathered, send_sems, recv_sems):
        x, y, c, _ = _place()
        me = 4 * x + 2 * y + c
        gathered[me] = x_ref[...]
        flips = [(fx, fy, fc) for fx in (0, 1) for fy in (0, 1) for fc in (0, 1)][1:]

        def copy(r, slot, to):
            return pltpu.make_async_remote_copy(
                src_ref=x_ref, dst_ref=gathered.at[slot], send_sem=send_sems.at[r], recv_sem=recv_sems.at[r],
                device_id=to, device_id_type=MESH)

        def peer(f):
            return (x ^ f[0], y ^ f[1], c ^ f[2])

        sent = [copy(r, me, peer(f)) for r, f in enumerate(flips)]
        for cp in sent:
            cp.start()
        for r, f in enumerate(flips):
            px, py, pc = peer(f)
            copy(r, 4 * px + 2 * py + pc, peer(f)).wait_recv()
        for cp in sent:
            cp.wait_send()
        acc = gathered[0]
        for k in range(1, N_DEV):
            acc = acc + gathered[k]
        o_ref[...] = acc

    vmem = pl.BlockSpec(memory_space=pltpu.VMEM)
    return pl.pallas_call(
        body, name=name, in_specs=[vmem], out_specs=vmem, out_shape=S((rows, width), F32),
        scratch_shapes=[pltpu.VMEM((N_DEV, rows, width), F32), pltpu.SemaphoreType.DMA((N_DEV - 1,)),
                        pltpu.SemaphoreType.DMA((N_DEV - 1,))],
    )(packed)


def _rope_tables(positions):
    inv_freq = 1.0 / (ROPE_THETA ** (jnp.arange(0, ROPE, 2, dtype=F32) / ROPE))
    ang = positions.astype(F32)[:, None] * inv_freq
    return jnp.cos(ang), jnp.sin(ang)


def _unstack_cols(w):
    k4, k, n4 = w.shape
    return jnp.transpose(w, (1, 0, 2)).reshape(k, k4 * n4)


def _stack_cols(w):
    k, n = w.shape
    return jnp.transpose(w.reshape(k, N_CHIPS, n // N_CHIPS), (1, 0, 2))


def kernel(x, positions, mla_norm, mla_w_in, mla_g_cq, mla_g_ckv, mla_w_uq, mla_w_ukv, mla_w_o, conv_norm, conv_w_in, conv_w, conv_w_out, ffn_norm, ffn_w_gate, ffn_w_up, ffn_w_down, final_norm, loss_target, m_mla_norm, m_mla_w_in, m_mla_g_cq, m_mla_g_ckv, m_mla_w_uq, m_mla_w_ukv, m_mla_w_o, m_conv_norm, m_conv_w_in, m_conv_w, m_conv_w_out, m_ffn_norm, m_ffn_w_gate, m_ffn_w_up, m_ffn_w_down, m_final_norm, v_mla_norm, v_mla_w_in, v_mla_g_cq, v_mla_g_ckv, v_mla_w_uq, v_mla_w_ukv, v_mla_w_o, v_conv_norm, v_conv_w_in, v_conv_w, v_conv_w_out, v_ffn_norm, v_ffn_w_gate, v_ffn_w_up, v_ffn_w_down, v_final_norm):
    weights = dict(mla_norm=mla_norm, mla_w_in=mla_w_in, mla_g_cq=mla_g_cq, mla_g_ckv=mla_g_ckv, mla_w_uq=mla_w_uq,
                   mla_w_ukv=mla_w_ukv, mla_w_o=mla_w_o, conv_norm=conv_norm, conv_w_in=conv_w_in, conv_w=conv_w,
                   conv_w_out=conv_w_out, ffn_norm=ffn_norm, ffn_w_gate=ffn_w_gate, ffn_w_up=ffn_w_up,
                   ffn_w_down=ffn_w_down, final_norm=final_norm)
    m_in = dict(mla_norm=m_mla_norm, mla_w_in=m_mla_w_in, mla_g_cq=m_mla_g_cq, mla_g_ckv=m_mla_g_ckv, mla_w_uq=m_mla_w_uq,
                mla_w_ukv=m_mla_w_ukv, mla_w_o=m_mla_w_o, conv_norm=m_conv_norm, conv_w_in=m_conv_w_in, conv_w=m_conv_w,
                conv_w_out=m_conv_w_out, ffn_norm=m_ffn_norm, ffn_w_gate=m_ffn_w_gate, ffn_w_up=m_ffn_w_up,
                ffn_w_down=m_ffn_w_down, final_norm=m_final_norm)
    v_in = dict(mla_norm=v_mla_norm, mla_w_in=v_mla_w_in, mla_g_cq=v_mla_g_cq, mla_g_ckv=v_mla_g_ckv, mla_w_uq=v_mla_w_uq,
                mla_w_ukv=v_mla_w_ukv, mla_w_o=v_mla_w_o, conv_norm=v_conv_norm, conv_w_in=v_conv_w_in, conv_w=v_conv_w,
                conv_w_out=v_conv_w_out, ffn_norm=v_ffn_norm, ffn_w_gate=v_ffn_w_gate, ffn_w_up=v_ffn_w_up,
                ffn_w_down=v_ffn_w_down, final_norm=v_final_norm)
    big = ["mla_w_in", "mla_w_uq", "mla_w_ukv", "mla_w_o", "conv_w_in", "conv_w_out", "ffn_w_gate", "ffn_w_up", "ffn_w_down"]
    order = list(weights)

    t, d = x.shape[1], x.shape[2]
    h0 = x.reshape(t, d)
    target = loss_target.reshape(t, d)
    cos, sin = _rope_tables(positions.reshape(t))

    def rows2d(a):
        return a.reshape(-1, a.shape[-1])

    first, later = big[:4], big[4:]
    shards = {n: rows2d(weights[n]).astype(BF16) for n in big}
    d4 = d // N_CHIPS
    first_gathered, (conv_norm_slots, conv_w_slots) = gather_weight_shards(
        [shards[n] for n in first], [conv_norm.reshape(1, d4), conv_w.reshape(3, d4)])
    gathered = dict(zip(first, first_gathered))
    conv_norm_full = conv_norm_slots.reshape(1, d)
    conv_w_full = jnp.transpose(conv_w_slots, (1, 0, 2)).reshape(3, d)
    w_in = gathered["mla_w_in"].reshape(-1, gathered["mla_w_in"].shape[-1])
    w_uq = _unstack_cols(gathered["mla_w_uq"])
    w_ukv = _unstack_cols(gathered["mla_w_ukv"])
    w_o = gathered["mla_w_o"].reshape(-1, d)

    chip = 2 * lax.axis_index("x") + lax.axis_index("y")

    def pack_rows(rows):
        idx = lax.broadcasted_iota(jnp.int32, (SMALL_ROWS, d), 0)
        out = jnp.zeros((SMALL_ROWS, d), F32)
        for r, row in enumerate(rows):
            out = out + jnp.where(idx == r, row, 0.0)
        return out


    a0 = rms_fwd("mla_norm_fwd", h0, mla_norm)
    proj, cq, ckv, kr = mla_in_proj("mla_in_proj", a0, w_in, mla_g_cq, mla_g_ckv, cos, sin)
    qh, kh, vh, conv_arriving = qkv_heads("qkv_heads", cq, ckv, w_uq, w_ukv, kr, cos, sin, [shards[n] for n in later[:2]])
    attn, lse, ffn_arriving = attention_fwd("attention_fwd", qh, kh, vh, [shards[n] for n in later[2:]])
    h1, a1, handed = attention_out_proj("mla_out_proj", attn, w_o, h0, ffn_norm[0:1], conv_arriving + ffn_arriving,
                                        [shards[n] for n in later])
    gathered.update(zip(later, handed))
    cw_in = _unstack_cols(gathered["conv_w_in"])
    cw_out = gathered["conv_w_out"].reshape(-1, d)
    wg_all, wu_all, wd_all = gathered["ffn_w_gate"], gathered["ffn_w_up"], gathered["ffn_w_down"]

    def ffn_forward(tag, h, a, layer, next_gain):
        g, u, z = ffn_up(f"ffn{tag}_up", a, wg_all, wu_all, layer)
        return g, u, z, ffn_down(f"ffn{tag}_down", z, wd_all, layer, h, next_gain)

    g0, u0, z0, (h2, a2) = ffn_forward(0, h1, a1, 0, conv_norm_full)
    bcx, yc = conv_in_proj("conv_in_proj", a2, cw_in, conv_w_full)
    h3, a3 = linear("conv_out_proj", yc, cw_out, F32, resid=h2, next_gain=ffn_norm[1:2])
    g1, u1, z1 = ffn_up("ffn1_up", a3, wg_all, wu_all, 1)
    dh4, d_final_norm, loss_local = ffn_down_loss("ffn1_down_loss", z1, wd_all, 1, h3, final_norm.reshape(1, d), target)

    def ffn_backward(tag, dh, h, layer, a, g, u, z, swap=()):
        dg, du, swapped = ffn_bwd_hidden(f"ffn{tag}_bwd_hidden", dh, wd_all, layer, g, u, swap)
        d_wd = ffn_wgrad_down(f"ffn{tag}_wgrad_down", z, dh)
        dh_prev, d_norm = ffn_bwd_input(f"ffn{tag}_bwd_input", dg, du, wg_all, wu_all, layer, h, ffn_norm[layer:layer + 1], dh)
        d_wg = ffn_wgrad_up(f"ffn{tag}_wgrad_gate", a, dg)
        d_wu = ffn_wgrad_up(f"ffn{tag}_wgrad_up", a, du)
        return dh_prev, d_norm, [d_wg, d_wu, d_wd], swapped

    def pair_sums(tag, local, from_sibling):
        return [add_halves(f"pair_sum_{tag}{i}", g, r) for i, (g, r) in enumerate(zip(local, from_sibling))]

    def sum_from_chips(tag, pairs, arrived):
        return [sum_chips(f"chip_sum_{tag}{i}", a, p) for i, (a, p) in enumerate(zip(arrived, pairs))]

    def shard_shape(n):
        return rows2d(weights[n]).shape

    dh3, d_ffn_norm1, ffn1_grads, _ = ffn_backward(1, dh4, h3, 1, a3, g1, u1, z1)

    dyc = linear_nt("conv_out_bwd_input", dh3, cw_out, F32)
    d_cw_out = wgrad("conv_out_wgrad", yc, dh3)
    dbcx, d_conv_w = conv_bwd("conv_bwd", bcx, conv_w_full, dyc)
    dh2, d_conv_norm = conv_in_bwd_input("conv_in_bwd_input", dbcx, cw_in, h2, conv_norm_full, dh3)
    d_cw_in = conv_in_wgrad("conv_in_wgrad", a2, dbcx)

    second = [d_cw_in, d_cw_out.reshape(N_CHIPS, -1, d)] + ffn1_grads
    dh1, d_ffn_norm0, ffn0_grads, second_swapped = ffn_backward(0, dh2, h1, 0, a1, g0, u0, z0, second)
    d_w_o = wgrad("mla_out_wgrad", attn, dh1)
    first_part = ffn0_grads + [d_w_o.reshape(N_CHIPS, -1, d)]
    d_attn, delta, first_swapped = attention_out_bwd("mla_out_bwd_input", dh1, w_o, attn, first_part)
    rest_pairs = pair_sums("rest", second + first_part, second_swapped + first_swapped)
    dqh, dkh, dvh, rest_arrived = attention_bwd("attention_bwd", qh, kh, vh, d_attn, lse, delta, rest_pairs)
    rd, rf = ffn0_grads[0].shape[1], ffn0_grads[2].shape[1]
    rest_where = [(0, 0), (1, 0), (2, rd), (3, rd), (4, rf), (2, 0), (3, 0), (4, 0), (5, 0)]
    rest_names = later + ["mla_w_o"]
    dq, dkv, dkr, rest_grads = qkv_heads_bwd("qkv_heads_bwd", dqh, dkh, dvh, cos, sin, sum_from_chips("rest", rest_pairs, rest_arrived),
                                              [shard_shape(n) for n in rest_names], rest_where)
    grads = dict(zip(rest_names, rest_grads))
    dcq = linear_nt("mla_q_up_bwd_input", dq, w_uq, F32)
    d_w_uq = wgrad("mla_q_up_wgrad", cq, dq)
    dckv = linear_nt("mla_kv_up_bwd_input", dkv, w_ukv, F32)
    d_w_ukv = wgrad("mla_kv_up_wgrad", ckv, dkv)
    dproj, d_g_cq, d_g_ckv = mla_mid_bwd("mla_mid_bwd", proj, mla_g_cq, mla_g_ckv, dcq, dckv, dkr, cos, sin)
    d_w_in = wgrad("mla_in_wgrad", a0, dproj)
    mla_local = [d_w_in.reshape(N_CHIPS, -1, d_w_in.shape[-1]), _stack_cols(d_w_uq), _stack_cols(d_w_ukv)]
    mla_pairs = pair_sums("mla", mla_local, sibling_swap_halves("sibling_swap_mla", mla_local))
    grad_x, d_mla_norm, mla_arrived = linear_nt_norm_bwd("mla_in_bwd_input", dproj, w_in, h0, mla_norm, dh1, mla_pairs)

    grads.update(zip(first[:3], sibling_join_halves("sibling_join_mla", sum_from_chips("mla", mla_pairs, mla_arrived),
                                                    [shard_shape(n) for n in first[:3]], [(i, 0) for i in range(3)])))

    def pad_row(v):
        return jnp.pad(v, ((0, 0), (0, d - v.shape[1])))

    small = all_reduce_small("all_reduce_small_grads", pack_rows([
        d_mla_norm, pad_row(d_g_cq), pad_row(d_g_ckv), d_ffn_norm0, d_ffn_norm1, d_final_norm, d_conv_norm,
        d_conv_w[0:1], d_conv_w[1:2], d_conv_w[2:3], jnp.broadcast_to(loss_local, (1, d))]))
    loss = small[10, 0]
    grads["mla_norm"] = small[0:1]
    grads["mla_g_cq"] = small[1:2, :mla_g_cq.shape[1]]
    grads["mla_g_ckv"] = small[2:3, :mla_g_ckv.shape[1]]
    grads["ffn_norm"] = small[3:5]
    grads["final_norm"] = small[5:6]
    grads["conv_norm"] = lax.dynamic_slice(small[6:7], (0, chip * d4), (1, d4))
    grads["conv_w"] = lax.dynamic_slice(small[7:10], (0, chip * d4), (3, d4))

    outs_g, outs_d, outs_m, outs_v = [], [], [], []
    for n in order:
        w = weights[n]
        if w.ndim == 3 and w.shape[2] % 128 and w.shape[1] % 128 == 0:
            results = adamw_swapped(f"adamw_{n}", jnp.swapaxes(w, 1, 2), grads[n].reshape(-1, w.shape[2]),
                                    jnp.swapaxes(m_in[n], 1, 2), jnp.swapaxes(v_in[n], 1, 2))
            grad_w, delta_w, new_m, new_v = [jnp.swapaxes(o, 1, 2) for o in results]
        else:
            delta_w, new_m, new_v = adamw(f"adamw_{n}", rows2d(w), grads[n].reshape(rows2d(w).shape), rows2d(m_in[n]), rows2d(v_in[n]))
            grad_w = grads[n]
        outs_g.append(grad_w.reshape(w.shape))
        outs_d.append(delta_w.reshape(w.shape))
        outs_m.append(new_m.reshape(w.shape))
        outs_v.append(new_v.reshape(w.shape))
    return (loss, grad_x.reshape(x.shape), *outs_g, *outs_d, *outs_m, *outs_v)
```

```python
import math

import jax
import jax.numpy as jnp
from jax import lax
from jax.experimental import pallas as pl
from jax.experimental.pallas import tpu as pltpu

F32 = jnp.float32
BF16 = jnp.bfloat16
S = jax.ShapeDtypeStruct

N_HEADS = 8
NOPE = 128
ROPE = 64
HALF = ROPE // 2
VDIM = 128
QK = NOPE + ROPE
CHUNK = 64
ROPE_THETA = 10000.0
RMS_EPS = 1e-6
ADAM_LR = 0.001
ADAM_B1 = 0.9
ADAM_B2 = 0.999
ADAM_EPS = 1e-08
ADAM_WD = 0.01
ADAM_STEP = 10

N_CHIPS = 4
N_DEV = 8
MASK_VALUE = -1e30
SCORE_SCALE = 1.0 / math.sqrt(QK)
LOG2_E = math.log2(math.e)
SCORE_SCALE_LOG2 = SCORE_SCALE * LOG2_E
VMEM_LIMIT = 48 * 1024 * 1024
VMEM_LIMIT_WHOLE_HEAD = 58 * 1024 * 1024
ATT_BLOCK = 512
CONV_SAVED_DTYPE = jnp.bfloat16
SMALL_ROWS = 16

_NN = (((1,), (0,)), ((), ()))
_NT = (((1,), (1,)), ((), ()))
_TN = (((0,), (0,)), ((), ()))
MESH = pl.DeviceIdType.MESH
ANY = pl.BlockSpec(memory_space=pl.ANY)


def _params(n_axes, vmem_limit=VMEM_LIMIT):
    return pltpu.CompilerParams(dimension_semantics=("arbitrary",) * n_axes, vmem_limit_bytes=vmem_limit)


def _tile(n, cap, mult=8):
    for t in range(min(cap, n), 0, -1):
        if n % t == 0 and t % mult == 0:
            return t
    return n


def _sigmoid(x):
    return 0.5 * jnp.tanh(0.5 * x) + 0.5


def _mm(name, a_ops, b_ops, products, dims, grid, k_axis, outs, acc_shape, epilogue, extra_ops=()):
    na, nb, ne, no = len(a_ops), len(b_ops), len(extra_ops), len(outs)
    n_acc = 1 + max(c for _, _, c in products)
    nk = 1 if k_axis is None else grid[k_axis]

    def body(*refs):
        a_refs = refs[:na]
        b_refs = refs[na:na + nb]
        e_refs = refs[na + nb:na + nb + ne]
        o_refs = refs[na + nb + ne:na + nb + ne + no]
        acc_refs = refs[na + nb + ne + no:]

        def partial_sums():
            vals = [None] * n_acc
            for ai, bi, ci in products:
                d = lax.dot_general(a_refs[ai][...].astype(BF16), b_refs[bi][...].astype(BF16), dims,
                                    preferred_element_type=F32)
                vals[ci] = d if vals[ci] is None else vals[ci] + d
            return vals

        if nk == 1:
            epilogue(partial_sums(), e_refs, o_refs)
        else:
            k = pl.program_id(k_axis)

            @pl.when(k == 0)
            def _():
                for acc in acc_refs:
                    acc[...] = jnp.zeros_like(acc)

            for acc, v in zip(acc_refs, partial_sums()):
                acc[...] += v

            @pl.when(k == nk - 1)
            def _():
                epilogue([acc[...] for acc in acc_refs], e_refs, o_refs)

    ops = list(a_ops) + list(b_ops) + list(extra_ops)
    return pl.pallas_call(
        body, name=name, grid=grid,
        in_specs=[s for _, s in ops], out_specs=[s for _, s in outs], out_shape=[o for o, _ in outs],
        scratch_shapes=[pltpu.VMEM(acc_shape, F32) for _ in range(n_acc if nk > 1 else 0)],
        compiler_params=_params(len(grid)),
    )(*[a for a, _ in ops])


def _store(accs, e_refs, o_refs):
    o_refs[0][...] = accs[0].astype(o_refs[0].dtype)


def linear(name, x, w, out_dtype, resid=None, next_gain=None):
    t, k = x.shape
    n = w.shape[1]
    tm = _tile(t, 512)
    tn = n if n <= 2048 else _tile(n, 1024, 128)
    tile = pl.BlockSpec((tm, tn), lambda j, i: (i, j))
    extra = [] if resid is None else [(resid, tile)]
    outs = [(S((t, n), out_dtype), tile)]
    if next_gain is not None:
        assert tn == n
        extra.append((next_gain, pl.BlockSpec((1, n), lambda j, i: (0, 0))))
        outs.append((S((t, n), BF16), tile))

    def epilogue(accs, e_refs, o_refs):
        y = accs[0] if resid is None else e_refs[0][...] + accs[0]
        o_refs[0][...] = y.astype(out_dtype)
        if next_gain is not None:
            o_refs[1][...] = (y * _rstd(y) * e_refs[-1][...]).astype(BF16)

    res = _mm(name, [(x, pl.BlockSpec((tm, k), lambda j, i: (i, 0)))], [(w, pl.BlockSpec((k, tn), lambda j, i: (0, j)))],
              [(0, 0, 0)], _NN, (n // tn, t // tm), None, outs, None, epilogue, extra)
    return res[0] if next_gain is None else res


def linear_nt(name, dy, w, out_dtype):
    t, n = dy.shape
    k = w.shape[0]
    tm = _tile(t, 512)
    tc = n if n <= 2048 else _tile(n, 1024, 128)
    return _mm(name, [(dy, pl.BlockSpec((tm, tc), lambda i, c: (i, c)))], [(w, pl.BlockSpec((k, tc), lambda i, c: (0, c)))],
               [(0, 0, 0)], _NT, (t // tm, n // tc), 1,
               [(S((t, k), out_dtype), pl.BlockSpec((tm, k), lambda i, c: (i, 0)))], (tm, k), _store)[0]


def wgrad(name, x, dy):
    t, k = x.shape
    n = dy.shape[1]
    tk = _tile(t, 512)
    tn = n if n <= 1024 else _tile(n, 1024, 128)
    return _mm(name, [(x, pl.BlockSpec((tk, k), lambda j, s: (s, 0)))], [(dy, pl.BlockSpec((tk, tn), lambda j, s: (s, j)))],
               [(0, 0, 0)], _TN, (n // tn, t // tk), 1,
               [(S((k, n), BF16), pl.BlockSpec((k, tn), lambda j, s: (0, j)))], (k, tn), _store)[0]


def _resident(shape, index_map):
    return pl.BlockSpec(shape, index_map, pipeline_mode=pl.Buffered(1))


def ffn_up(name, a, wg_all, wu_all, layer):
    t, d = a.shape
    f4 = wg_all.shape[2]
    tm = _tile(t, 512)
    w_spec = _resident((N_CHIPS, d, f4), lambda i: (0, layer, 0))
    h_spec = pl.BlockSpec((N_CHIPS, tm, f4), lambda i: (0, i, 0))

    def body(a_ref, wg_ref, wu_ref, zg_ref, zu_ref, z_ref):
        av = a_ref[...]
        for k in range(N_CHIPS):
            g = jnp.dot(av, wg_ref[k], preferred_element_type=F32)
            u = jnp.dot(av, wu_ref[k], preferred_element_type=F32)
            sg = _sigmoid(g)
            silu = g * sg
            zg_ref[k] = (u * (sg * (1.0 + g * (1.0 - sg)))).astype(BF16)
            zu_ref[k] = silu.astype(BF16)
            z_ref[k] = (silu * u).astype(BF16)

    return pl.pallas_call(
        body, name=name, grid=(t // tm,), in_specs=[pl.BlockSpec((tm, d), lambda i: (i, 0)), w_spec, w_spec],
        out_specs=[h_spec] * 3, out_shape=[S((N_CHIPS, t, f4), BF16)] * 3, compiler_params=_params(1))(a, wg_all, wu_all)


def ffn_down(name, z, wd_all, layer, resid, next_gain=None):
    _, t, f4 = z.shape
    d = wd_all.shape[2]
    tm = _tile(t, 512)
    row = pl.BlockSpec((tm, d), lambda i: (i, 0))
    normed = next_gain is not None

    def body(z_ref, wd_ref, r_ref, *refs):
        acc = r_ref[...]
        for k in range(N_CHIPS):
            acc = acc + jnp.dot(z_ref[k], wd_ref[k], preferred_element_type=F32)
        refs[-2 if normed else -1][...] = acc
        if normed:
            refs[-1][...] = (acc * _rstd(acc) * refs[0][...]).astype(BF16)

    res = pl.pallas_call(
        body, name=name, grid=(t // tm,),
        in_specs=[pl.BlockSpec((N_CHIPS, tm, f4), lambda i: (0, i, 0)), _resident((N_CHIPS, f4, d), lambda i: (0, layer, 0)), row]
        + ([pl.BlockSpec((1, d), lambda i: (0, 0))] if normed else []),
        out_specs=[row] * (2 if normed else 1), out_shape=[S((t, d), F32)] + ([S((t, d), BF16)] if normed else []),
        compiler_params=_params(1))(z, wd_all, resid, *([next_gain] if normed else []))
    return res if normed else res[0]


def ffn_bwd_hidden(name, dh, wd_all, layer, zg, zu, swap=()):
    t, d = dh.shape
    f4 = zg.shape[2]
    tm = _tile(t, 512)
    h_spec = pl.BlockSpec((N_CHIPS, tm, f4), lambda i: (0, i, 0))
    n = len(swap)

    def body(dh_ref, wd_ref, zg_ref, zu_ref, *refs):
        dg_ref, du_ref = refs[n:n + 2]
        if n:
            _ride_along(swap_copies(refs[:n], refs[n + 2:2 * n + 2], *refs[2 * n + 2:]), (pl.program_id(0),), (t // tm,))
        dhb = dh_ref[...].astype(BF16)
        for k in range(N_CHIPS):
            dz = lax.dot_general(dhb, wd_ref[k], _NT, preferred_element_type=F32)
            dg_ref[k] = (dz * zg_ref[k].astype(F32)).astype(BF16)
            du_ref[k] = (dz * zu_ref[k].astype(F32)).astype(BF16)

    outs = pl.pallas_call(
        body, name=name, grid=(t // tm,),
        in_specs=[pl.BlockSpec((tm, d), lambda i: (i, 0)), _resident((N_CHIPS, f4, d), lambda i: (0, layer, 0)), h_spec, h_spec]
        + [ANY] * n,
        out_specs=[h_spec] * 2 + [ANY] * n, out_shape=[S((N_CHIPS, t, f4), BF16)] * 2 + _swap_shapes(swap),
        scratch_shapes=[pltpu.SemaphoreType.DMA((n,)), pltpu.SemaphoreType.DMA((n,))] if n else [],
        compiler_params=_params(1))(dh, wd_all, zg, zu, *swap)
    return outs[0], outs[1], list(outs[2:])


def _norm_bwd_specs(tm, d):
    row = pl.BlockSpec((tm, d), lambda i: (i, 0))
    vec = pl.BlockSpec((1, d), lambda i: (0, 0))
    return [row, vec, row], [row, vec]


def _norm_bwd_tail(da, h_ref, g_ref, dhi_ref, dho_ref, dgain_ref):
    dx, dgain = _rms_bwd(h_ref[...], g_ref[...], da)
    dho_ref[...] = dhi_ref[...] + dx

    @pl.when(pl.program_id(0) == 0)
    def _():
        dgain_ref[...] = jnp.zeros_like(dgain_ref)

    dgain_ref[...] += dgain


def ffn_bwd_input(name, dg, du, wg_all, wu_all, layer, h, gain, dh_in):
    _, t, f4 = dg.shape
    d = h.shape[1]
    tm = _tile(t, 512)
    h_spec = pl.BlockSpec((N_CHIPS, tm, f4), lambda i: (0, i, 0))
    w_spec = _resident((N_CHIPS, d, f4), lambda i: (0, layer, 0))
    tail_in, tail_out = _norm_bwd_specs(tm, d)

    def body(dg_ref, du_ref, wg_ref, wu_ref, *tail):
        acc = jnp.zeros((tm, d), F32)
        for k in range(N_CHIPS):
            acc = acc + lax.dot_general(dg_ref[k], wg_ref[k], _NT, preferred_element_type=F32)
            acc = acc + lax.dot_general(du_ref[k], wu_ref[k], _NT, preferred_element_type=F32)
        _norm_bwd_tail(acc, *tail)

    return pl.pallas_call(
        body, name=name, grid=(t // tm,), in_specs=[h_spec, h_spec, w_spec, w_spec] + tail_in, out_specs=tail_out,
        out_shape=[S((t, d), F32), S((1, d), F32)], compiler_params=_params(1))(dg, du, wg_all, wu_all, h, gain, dh_in)


def ffn_wgrad_up(name, a, dy):
    t, d = a.shape
    f4 = dy.shape[2]
    tk = _tile(t, 512)
    nt = t // tk

    def body(a_ref, dy_ref, o_ref, acc):
        s = pl.program_id(0)

        @pl.when(s == 0)
        def _():
            acc[...] = jnp.zeros_like(acc)

        at = a_ref[...].T
        for k in range(N_CHIPS):
            acc[k] += jnp.dot(at, dy_ref[k], preferred_element_type=F32)

        @pl.when(s == nt - 1)
        def _():
            o_ref[...] = acc[...].astype(BF16)

    return pl.pallas_call(
        body, name=name, grid=(nt,),
        in_specs=[pl.BlockSpec((tk, d), lambda s: (s, 0)), pl.BlockSpec((N_CHIPS, tk, f4), lambda s: (0, s, 0))],
        out_specs=pl.BlockSpec((N_CHIPS, d, f4), lambda s: (0, 0, 0)), out_shape=S((N_CHIPS, d, f4), BF16),
        scratch_shapes=[pltpu.VMEM((N_CHIPS, d, f4), F32)], compiler_params=_params(1))(a, dy)


def ffn_wgrad_down(name, z, dh):
    _, t, f4 = z.shape
    d = dh.shape[1]
    tk = _tile(t, 512)
    nt = t // tk

    def body(z_ref, dh_ref, o_ref, acc):
        s = pl.program_id(0)

        @pl.when(s == 0)
        def _():
            acc[...] = jnp.zeros_like(acc)

        dhb = dh_ref[...].astype(BF16)
        for k in range(N_CHIPS):
            acc[k] += lax.dot_general(z_ref[k], dhb, _TN, preferred_element_type=F32)

        @pl.when(s == nt - 1)
        def _():
            o_ref[...] = acc[...].astype(BF16)

    return pl.pallas_call(
        body, name=name, grid=(nt,),
        in_specs=[pl.BlockSpec((N_CHIPS, tk, f4), lambda s: (0, s, 0)), pl.BlockSpec((tk, d), lambda s: (s, 0))],
        out_specs=pl.BlockSpec((N_CHIPS, f4, d), lambda s: (0, 0, 0)), out_shape=S((N_CHIPS, f4, d), BF16),
        scratch_shapes=[pltpu.VMEM((N_CHIPS, f4, d), F32)], compiler_params=_params(1))(z, dh)


def conv_in_proj(name, a, w, conv_w):
    t, d = a.shape
    tm = _tile(t, 256)
    keep = 8

    def body(a_ref, w_ref, cw_ref, bcx_ref, y_ref, u_ref):
        @pl.when(pl.program_id(0) == 0)
        def _():
            u_ref[0:keep, :] = jnp.zeros((keep, d), F32)

        av = a_ref[...]
        b, c, x = [jnp.dot(av, w_ref[:, j * d:(j + 1) * d], preferred_element_type=F32) for j in range(3)]
        for j, part in enumerate((b, c, x)):
            bcx_ref[j] = part.astype(bcx_ref.dtype)
        u_ref[keep:keep + tm, :] = c * x
        uc = (cw_ref[0:1, :] * u_ref[keep - 2:keep - 2 + tm, :] + cw_ref[1:2, :] * u_ref[keep - 1:keep - 1 + tm, :]
              + cw_ref[2:3, :] * u_ref[keep:keep + tm, :])
        y_ref[...] = (b * uc).astype(BF16)
        u_ref[0:keep, :] = u_ref[tm:tm + keep, :]

    return pl.pallas_call(
        body, name=name, grid=(t // tm,),
        in_specs=[pl.BlockSpec((tm, d), lambda i: (i, 0)), _resident((d, 3 * d), lambda i: (0, 0)), pl.BlockSpec((3, d), lambda i: (0, 0))],
        out_specs=[pl.BlockSpec((3, tm, d), lambda i: (0, i, 0)), pl.BlockSpec((tm, d), lambda i: (i, 0))],
        out_shape=[S((3, t, d), CONV_SAVED_DTYPE), S((t, d), BF16)], scratch_shapes=[pltpu.VMEM((tm + keep, d), F32)],
        compiler_params=_params(1))(a, w, conv_w)


def conv_in_bwd_input(name, dbcx, w, h, gain, dh_in):
    _, t, d = dbcx.shape
    tm = _tile(t, 512)
    tail_in, tail_out = _norm_bwd_specs(tm, d)

    def body(g_ref, w_ref, *tail):
        acc = jnp.zeros((tm, d), F32)
        for j in range(3):
            acc = acc + lax.dot_general(g_ref[j], w_ref[:, j * d:(j + 1) * d], _NT, preferred_element_type=F32)
        _norm_bwd_tail(acc, *tail)

    return pl.pallas_call(
        body, name=name, grid=(t // tm,),
        in_specs=[pl.BlockSpec((3, tm, d), lambda i: (0, i, 0)), _resident((d, 3 * d), lambda i: (0, 0))] + tail_in,
        out_specs=tail_out, out_shape=[S((t, d), F32), S((1, d), F32)], compiler_params=_params(1))(dbcx, w, h, gain, dh_in)


def linear_nt_norm_bwd(name, dy, w, h, gain, dh_in, parts=()):
    t, n = dy.shape
    k = w.shape[0]
    tm = _tile(t, 512)
    tail_in, tail_out = _norm_bwd_specs(tm, k)
    m = len(parts)

    def body(dy_ref, w_ref, h_ref, g_ref, dhi_ref, *refs):
        if m:
            _ride_along(scatter_ici_copies(refs[:m], refs[m + 2:2 * m + 2], *refs[2 * m + 2:]), (pl.program_id(0),), (t // tm,))
        da = lax.dot_general(dy_ref[...].astype(BF16), w_ref[...], _NT, preferred_element_type=F32)
        _norm_bwd_tail(da, h_ref, g_ref, dhi_ref, *refs[m:m + 2])

    outs = pl.pallas_call(
        body, name=name, grid=(t // tm,),
        in_specs=[pl.BlockSpec((tm, n), lambda i: (i, 0)), _resident((k, n), lambda i: (0, 0))] + tail_in + [ANY] * m,
        out_specs=tail_out + [ANY] * m, out_shape=[S((t, k), F32), S((1, k), F32)] + [S(p.shape, p.dtype) for p in parts],
        scratch_shapes=[pltpu.SemaphoreType.DMA((m, 3)), pltpu.SemaphoreType.DMA((m, 3))] if m else [],
        compiler_params=_params(1))(dy, w, h, gain, dh_in, *parts)
    return outs[0], outs[1], list(outs[2:])


def conv_in_wgrad(name, a, dbcx):
    t, d = a.shape
    tk = _tile(t, 512)
    nt = t // tk
    n4 = 3 * d // N_CHIPS

    def body(a_ref, g_ref, o_ref, acc):
        s = pl.program_id(0)

        @pl.when(s == 0)
        def _():
            acc[...] = jnp.zeros_like(acc)

        at = a_ref[...].T
        for j in range(3):
            acc[:, j * d:(j + 1) * d] += jnp.dot(at, g_ref[j], preferred_element_type=F32)

        @pl.when(s == nt - 1)
        def _():
            for k in range(N_CHIPS):
                o_ref[k] = acc[:, k * n4:(k + 1) * n4].astype(BF16)

    return pl.pallas_call(
        body, name=name, grid=(nt,),
        in_specs=[pl.BlockSpec((tk, d), lambda s: (s, 0)), pl.BlockSpec((3, tk, d), lambda s: (0, s, 0))],
        out_specs=pl.BlockSpec((N_CHIPS, d, n4), lambda s: (0, 0, 0)), out_shape=S((N_CHIPS, d, n4), BF16),
        scratch_shapes=[pltpu.VMEM((d, 3 * d), F32)], compiler_params=_params(1))(a, dbcx)


def _rstd(x):
    return lax.rsqrt(jnp.mean(x * x, axis=-1, keepdims=True) + RMS_EPS)


def _rms_bwd(x, g, dy):
    r = _rstd(x)
    xhat = x * r
    dgain = jnp.sum(dy * xhat, axis=0, keepdims=True)
    dxh = dy * g
    dx = r * (dxh - xhat * jnp.mean(dxh * xhat, axis=-1, keepdims=True))
    return dx, dgain


def rms_fwd(name, h, g):
    t, d = h.shape
    tr = _tile(t, 512)

    def body(h_ref, g_ref, a_ref):
        x = h_ref[...]
        a_ref[...] = (x * _rstd(x) * g_ref[...]).astype(BF16)

    return pl.pallas_call(
        body, name=name, grid=(t // tr,),
        in_specs=[pl.BlockSpec((tr, d), lambda i: (i, 0)), pl.BlockSpec((1, d), lambda i: (0, 0))],
        out_specs=pl.BlockSpec((tr, d), lambda i: (i, 0)), out_shape=S((t, d), BF16), compiler_params=_params(1))(h, g)


def ffn_down_loss(name, z, wd_all, layer, resid, gain, target):
    _, t, f4 = z.shape
    d = wd_all.shape[2]
    tm = _tile(t, 512)

    def body(z_ref, wd_ref, r_ref, g_ref, t_ref, dh_ref, dg_ref, loss_ref):
        x = r_ref[...]
        for k in range(N_CHIPS):
            x = x + jnp.dot(z_ref[k], wd_ref[k], preferred_element_type=F32)
        g = g_ref[...]
        r = _rstd(x)
        xhat = x * r
        err = xhat * g - t_ref[...]
        dy = err * (1.0 / d)
        dxh = dy * g
        dh_ref[...] = r * (dxh - xhat * jnp.mean(dxh * xhat, axis=-1, keepdims=True))

        @pl.when(pl.program_id(0) == 0)
        def _():
            dg_ref[...] = jnp.zeros_like(dg_ref)
            loss_ref[...] = jnp.zeros_like(loss_ref)

        dg_ref[...] += jnp.sum(dy * xhat, axis=0, keepdims=True)
        per_token = jnp.mean(err * err, axis=-1, keepdims=True)
        loss_ref[...] += 0.5 * jnp.sum(per_token, axis=0, keepdims=True)

    row = pl.BlockSpec((tm, d), lambda i: (i, 0))
    vec = pl.BlockSpec((1, d), lambda i: (0, 0))
    one = pl.BlockSpec((1, 1), lambda i: (0, 0))
    return pl.pallas_call(
        body, name=name, grid=(t // tm,),
        in_specs=[pl.BlockSpec((N_CHIPS, tm, f4), lambda i: (0, i, 0)), _resident((N_CHIPS, f4, d), lambda i: (0, layer, 0)), row, vec, row],
        out_specs=[row, vec, one], out_shape=[S((t, d), F32), S((1, d), F32), S((1, 1), F32)],
        compiler_params=_params(1))(z, wd_all, resid, gain, target)


def mla_in_proj(name, a, w, g_cq, g_ckv, cos, sin):
    t, d = a.shape
    n = w.shape[1]
    ql, kl = g_cq.shape[1], g_ckv.shape[1]
    tr = _tile(t, 512)

    def body(a_ref, w_ref, gq_ref, gk_ref, c_ref, s_ref, p_ref, cq_ref, ckv_ref, kr_ref):
        p_ref[...] = jnp.dot(a_ref[...], w_ref[...], preferred_element_type=F32)
        xq = p_ref[:, 0:ql]
        cq_ref[...] = (xq * _rstd(xq) * gq_ref[...]).astype(BF16)
        xk = p_ref[:, ql:ql + kl]
        ckv_ref[...] = (xk * _rstd(xk) * gk_ref[...]).astype(BF16)
        k1 = p_ref[:, ql + kl:ql + kl + HALF]
        k2 = p_ref[:, ql + kl + HALF:ql + kl + ROPE]
        c = c_ref[...]
        s = s_ref[...]
        kr_ref[:, 0:HALF] = k1 * c - k2 * s
        kr_ref[:, HALF:ROPE] = k1 * s + k2 * c

    def row(w):
        return pl.BlockSpec((tr, w), lambda i: (i, 0))

    def vec(w):
        return pl.BlockSpec((1, w), lambda i: (0, 0))

    return pl.pallas_call(
        body, name=name, grid=(t // tr,),
        in_specs=[row(d), _resident((d, n), lambda i: (0, 0)), vec(ql), vec(kl), row(HALF), row(HALF)],
        out_specs=[row(n), row(ql), row(kl), row(ROPE)],
        out_shape=[S((t, n), F32), S((t, ql), BF16), S((t, kl), BF16), S((t, ROPE), F32)],
        compiler_params=_params(1))(a, w, g_cq, g_ckv, cos, sin)


def qkv_heads(name, cq, ckv, w_uq, w_ukv, kr, cos, sin, shards=()):
    t = cq.shape[0]
    tr = _tile(t, 256)
    n = len(shards)

    def body(cq_ref, ckv_ref, wq_ref, wkv_ref, kr_ref, c_ref, s_ref, *refs):
        src = refs[:n]
        qo_ref, ko_ref, vo_ref = refs[n:n + 3]
        q_ref, kv_ref = refs[2 * n + 3:2 * n + 5]
        if n:
            _ride_along(gather_ici_copies(src, refs[n + 3:2 * n + 3], *refs[2 * n + 5:]), (pl.program_id(0),), (t // tr,))
        q_ref[...] = jnp.dot(cq_ref[...], wq_ref[...], preferred_element_type=F32)
        kv_ref[...] = jnp.dot(ckv_ref[...], wkv_ref[...], preferred_element_type=F32).astype(BF16)
        c = c_ref[...]
        s = s_ref[...]
        krb = kr_ref[...].astype(BF16)
        for h in range(N_HEADS):
            q0 = h * QK
            qo_ref[h, :, 0:NOPE] = q_ref[:, q0:q0 + NOPE].astype(BF16)
            q1 = q_ref[:, q0 + NOPE:q0 + NOPE + HALF]
            q2 = q_ref[:, q0 + NOPE + HALF:q0 + QK]
            qo_ref[h, :, NOPE:NOPE + HALF] = (q1 * c - q2 * s).astype(BF16)
            qo_ref[h, :, NOPE + HALF:QK] = (q1 * s + q2 * c).astype(BF16)
            k0 = h * (NOPE + VDIM)
            ko_ref[h, :, 0:NOPE] = kv_ref[:, k0:k0 + NOPE]
            ko_ref[h, :, NOPE:QK] = krb
            vo_ref[h] = kv_ref[:, k0 + NOPE:k0 + NOPE + VDIM]

    def row(w):
        return pl.BlockSpec((tr, w), lambda i: (i, 0))

    def heads(w):
        return pl.BlockSpec((N_HEADS, tr, w), lambda i: (0, i, 0))

    outs = pl.pallas_call(
        body, name=name, grid=(t // tr,),
        in_specs=[row(cq.shape[1]), row(ckv.shape[1]), _resident(w_uq.shape, lambda i: (0, 0)), _resident(w_ukv.shape, lambda i: (0, 0)),
                  row(ROPE), row(HALF), row(HALF)] + [ANY] * n,
        out_specs=[heads(QK), heads(QK), heads(VDIM)] + [ANY] * n,
        out_shape=[S((N_HEADS, t, QK), BF16), S((N_HEADS, t, QK), BF16), S((N_HEADS, t, VDIM), BF16)]
        + [S((N_CHIPS,) + s.shape, s.dtype) for s in shards],
        scratch_shapes=[pltpu.VMEM((tr, N_HEADS * QK), F32), pltpu.VMEM((tr, N_HEADS * (NOPE + VDIM)), BF16)]
        + ([pltpu.SemaphoreType.DMA((n, 3)), pltpu.SemaphoreType.DMA((n, 3))] if n else []),
        compiler_params=_params(1))(cq, ckv, w_uq, w_ukv, kr, cos, sin, *shards)
    return outs[0], outs[1], outs[2], list(outs[3:])


def qkv_heads_bwd(name, dq_h, dk_h, dv_h, cos, sin, w_uq, w_ukv, proj, g_cq, g_ckv, halves=(), targets=(), where=()):
    t = dq_h.shape[1]
    tr = _tile(t, 256)
    n, nt = len(halves), len(targets)
    ql, kl = g_cq.shape[1], g_ckv.shape[1]
    width = proj.shape[1]

    def body(dq_ref, dk_ref, dv_ref, c_ref, s_ref, wq_ref, wkv_ref, p_ref, gq_ref, gk_ref, *refs):
        q_ref, kv_ref, dp_ref, dgq_ref, dgk_ref = refs[n:n + 5]
        kr_ref = refs[n + 5 + nt]
        if n:
            src, dst = refs[:n], refs[n + 5:n + 5 + nt]
            stages = refs[n + 6 + nt:2 * n + 6 + nt]
            send_sems, recv_sems, local_sems = refs[2 * n + 6 + nt:]
            core = lax.axis_index("c")
            own = [(src[i], stages[i],
                    dst[where[i][0]].at[pl.ds(pl.multiple_of(where[i][1] + core * src[i].shape[0], 8), src[i].shape[0])],
                    local_sems.at[i]) for i in range(n)]
            _ride_along(join_copies(src, dst, where, send_sems, recv_sems), (pl.program_id(0),), (t // tr,), own)
        c = c_ref[...]
        s = s_ref[...]
        dkr = jnp.zeros((tr, ROPE), F32)
        for h in range(N_HEADS):
            q0 = h * QK
            q_ref[:, q0:q0 + NOPE] = dq_ref[h, :, 0:NOPE].astype(BF16)
            d1 = dq_ref[h, :, NOPE:NOPE + HALF]
            d2 = dq_ref[h, :, NOPE + HALF:QK]
            q_ref[:, q0 + NOPE:q0 + NOPE + HALF] = (d1 * c + d2 * s).astype(BF16)
            q_ref[:, q0 + NOPE + HALF:q0 + QK] = (d2 * c - d1 * s).astype(BF16)
            k0 = h * (NOPE + VDIM)
            kv_ref[:, k0:k0 + NOPE] = dk_ref[h, :, 0:NOPE].astype(BF16)
            kv_ref[:, k0 + NOPE:k0 + NOPE + VDIM] = dv_ref[h].astype(BF16)
            dkr = dkr + dk_ref[h, :, NOPE:QK]
        kr_ref[...] = dkr
        dcq = lax.dot_general(q_ref[...], wq_ref[...], _NT, preferred_element_type=F32)
        dckv = lax.dot_general(kv_ref[...], wkv_ref[...], _NT, preferred_element_type=F32)
        dxq, dgq = _rms_bwd(p_ref[:, 0:ql], gq_ref[...], dcq)
        dp_ref[:, 0:ql] = dxq.astype(BF16)
        dxk, dgk = _rms_bwd(p_ref[:, ql:ql + kl], gk_ref[...], dckv)
        dp_ref[:, ql:ql + kl] = dxk.astype(BF16)
        r1 = kr_ref[:, 0:HALF]
        r2 = kr_ref[:, HALF:ROPE]
        dp_ref[:, ql + kl:ql + kl + HALF] = (r1 * c + r2 * s).astype(BF16)
        dp_ref[:, ql + kl + HALF:ql + kl + ROPE] = (r2 * c - r1 * s).astype(BF16)

        @pl.when(pl.program_id(0) == 0)
        def _():
            dgq_ref[...] = jnp.zeros_like(dgq_ref)
            dgk_ref[...] = jnp.zeros_like(dgk_ref)

        dgq_ref[...] += dgq
        dgk_ref[...] += dgk

    def row(w):
        return pl.BlockSpec((tr, w), lambda i: (i, 0))

    def vec(w):
        return pl.BlockSpec((1, w), lambda i: (0, 0))

    def heads(w):
        return pl.BlockSpec((N_HEADS, tr, w), lambda i: (0, i, 0))

    outs = pl.pallas_call(
        body, name=name, grid=(t // tr,),
        in_specs=[heads(QK), heads(QK), heads(VDIM), row(HALF), row(HALF), _resident(w_uq.shape, lambda i: (0, 0)),
                  _resident(w_ukv.shape, lambda i: (0, 0)), row(width), vec(ql), vec(kl)] + [ANY] * n,
        out_specs=[row(N_HEADS * QK), row(N_HEADS * (NOPE + VDIM)), row(width), vec(ql), vec(kl)] + [ANY] * nt,
        out_shape=[S((t, N_HEADS * QK), BF16), S((t, N_HEADS * (NOPE + VDIM)), BF16), S((t, width), BF16),
                   S((1, ql), F32), S((1, kl), F32)] + [S(tg, F32) for tg in targets],
        scratch_shapes=[pltpu.VMEM((tr, ROPE), F32)] + [pltpu.VMEM(h.shape, h.dtype) for h in halves]
        + ([pltpu.SemaphoreType.DMA((n,)), pltpu.SemaphoreType.DMA((n,)), pltpu.SemaphoreType.DMA((n, 2))] if n else []),
        compiler_params=_params(1))(dq_h, dk_h, dv_h, cos, sin, w_uq, w_ukv, proj, g_cq, g_ckv, *halves)
    return outs[0], outs[1], outs[2], outs[3], outs[4], list(outs[5:])


def _chunk_mask_t(q_start, k_start, bq, bk):
    kc = (k_start + lax.broadcasted_iota(jnp.int32, (bk, bq), 0)) // CHUNK
    qc = (q_start + lax.broadcasted_iota(jnp.int32, (bk, bq), 1)) // CHUNK
    return kc <= qc


def attention_fwd(name, q, k, v, shards=()):
    nh, t, _ = q.shape
    blk = ATT_BLOCK
    nq = t // blk
    n = len(shards)

    def body(q_ref, k_ref, v_ref, *refs):
        src = refs[:n]
        o_ref, lse_ref = refs[n:n + 2]
        dst = refs[n + 2:2 * n + 2]
        m_ref, l_ref, acc_ref, s_buf, p_buf, alpha_buf, bias_ref = refs[2 * n + 2:2 * n + 9]
        i = pl.program_id(1)
        if n:
            send_sems, recv_sems = refs[2 * n + 9:]
            _ride_along(gather_ici_copies(src, dst, send_sems, recv_sems), (pl.program_id(0), i), (nh, nq))

        @pl.when((pl.program_id(0) == 0) & (i == 0))
        def _():
            bias_ref[...] = jnp.where(_chunk_mask_t(0, 0, blk, blk), 0.0, MASK_VALUE)

        m_ref[...] = jnp.full_like(m_ref, MASK_VALUE)
        l_ref[...] = jnp.zeros_like(l_ref)
        acc_ref[...] = jnp.zeros_like(acc_ref)

        def rows(b):
            return pl.ds(pl.multiple_of(b * blk, blk), blk)

        def scores(b, slot):
            s_buf[slot] = lax.dot_general(k_ref[rows(b), :], q_ref[...], _NT, preferred_element_type=F32)

        def softmax(slot, diagonal):
            s = s_buf[slot]
            if diagonal:
                s = s + bias_ref[...]
            m_old = m_ref[...]
            m_new = jnp.maximum(m_old, jnp.max(s, axis=0, keepdims=True))
            p = jnp.exp2((s - m_new) * SCORE_SCALE_LOG2)
            alpha = jnp.exp2((m_old - m_new) * SCORE_SCALE_LOG2)
            l_ref[...] = alpha * l_ref[...] + jnp.sum(p, axis=0, keepdims=True)
            m_ref[...] = m_new
            alpha_buf[slot] = alpha
            p_buf[slot] = p.astype(BF16)

        def values(b, slot):
            pv = lax.dot_general(v_ref[rows(b), :], p_buf[slot], _TN, preferred_element_type=F32)
            acc_ref[...] = alpha_buf[slot] * acc_ref[...] + pv

        def step(t, slot):
            values(t - 2, slot)
            softmax(1 - slot, False)
            scores(t, slot)

        scores(0, 0)

        @pl.when(i == 0)
        def _():
            softmax(0, True)
            values(0, 0)

        @pl.when(i > 0)
        def _():
            scores(1, 1)
            softmax(0, False)
            steady = i - 1

            def pair(u, carry):
                step(2 + 2 * u, 0)
                step(3 + 2 * u, 1)
                return carry

            lax.fori_loop(0, steady // 2, pair, 0)

            @pl.when(steady % 2 == 1)
            def _():
                step(i, 0)

            last = i % 2
            softmax(last, True)
            values(i - 1, 1 - last)
            values(i, last)

        l = l_ref[...]
        o_ref[...] = (acc_ref[...] / l).T
        lse_ref[...] = m_ref[...] * SCORE_SCALE + jnp.log(l)

    outs = pl.pallas_call(
        body, name=name, grid=(nh, nq),
        in_specs=[pl.BlockSpec((None, blk, QK), lambda h, i: (h, i, 0)), pl.BlockSpec((None, t, QK), lambda h, i: (h, 0, 0)),
                  pl.BlockSpec((None, t, VDIM), lambda h, i: (h, 0, 0))] + [ANY] * n,
        out_specs=[pl.BlockSpec((blk, VDIM), lambda h, i: (i, h)),
                   pl.BlockSpec((None, None, 1, blk), lambda h, i: (h, i, 0, 0))] + [ANY] * n,
        out_shape=[S((t, nh * VDIM), F32), S((nh, nq, 1, blk), F32)] + [S((N_CHIPS,) + s.shape, s.dtype) for s in shards],
        scratch_shapes=[pltpu.VMEM((1, blk), F32), pltpu.VMEM((1, blk), F32), pltpu.VMEM((VDIM, blk), F32),
                        pltpu.VMEM((2, blk, blk), F32), pltpu.VMEM((2, blk, blk), BF16), pltpu.VMEM((2, 1, blk), F32),
                        pltpu.VMEM((blk, blk), F32)]
        + ([pltpu.SemaphoreType.DMA((n, 3)), pltpu.SemaphoreType.DMA((n, 3))] if n else []),
        compiler_params=_params(2))(q, k, v, *shards)
    return outs[0], outs[1], list(outs[2:])


def attention_out_bwd(name, dh, w_o, o, swap=()):
    t, d = dh.shape
    n = w_o.shape[0]
    blk = ATT_BLOCK
    m = len(swap)

    def body(dh_ref, w_ref, o_ref, *refs):
        do_ref, d_ref = refs[m:m + 2]
        if m:
            _ride_along(swap_copies(refs[:m], refs[m + 2:2 * m + 2], *refs[2 * m + 2:]), (pl.program_id(0),), (t // blk,))
        do_ref[...] = lax.dot_general(dh_ref[...].astype(BF16), w_ref[...], _NT, preferred_element_type=F32)
        for h in range(N_HEADS):
            cols = slice(h * VDIM, (h + 1) * VDIM)
            d_ref[h] = jnp.sum((do_ref[:, cols] * o_ref[:, cols]).T, axis=0, keepdims=True)

    tile = pl.BlockSpec((blk, n), lambda i: (i, 0))
    outs = pl.pallas_call(
        body, name=name, grid=(t // blk,),
        in_specs=[pl.BlockSpec((blk, d), lambda i: (i, 0)), _resident((n, d), lambda i: (0, 0)), tile] + [ANY] * m,
        out_specs=[tile, pl.BlockSpec((N_HEADS, None, 1, blk), lambda i: (0, i, 0, 0))] + [ANY] * m,
        out_shape=[S((t, n), F32), S((N_HEADS, t // blk, 1, blk), F32)] + _swap_shapes(swap),
        scratch_shapes=[pltpu.SemaphoreType.DMA((m,)), pltpu.SemaphoreType.DMA((m,))] if m else [],
        compiler_params=_params(1))(dh, w_o, o, *swap)
    return outs[0], outs[1], list(outs[2:])


def attention_bwd(name, q, k, v, do, lse, delta, parts=()):
    nh, t, _ = q.shape
    blk = ATT_BLOCK
    nq = t // blk
    n_pairs = nq * (nq + 1) // 2
    n = len(parts)
    scale = SCORE_SCALE

    def body(q_ref, k_ref, v_ref, do_ref, lse_ref, dl_ref, *refs):
        src = refs[:n]
        dq_out, dk_out, dv_out = refs[n:n + 3]
        dst = refs[n + 3:2 * n + 3]
        s_buf, dp_buf, p_buf, ds_buf, bias_ref, dq_ref, dk_ref, dv_ref = refs[2 * n + 3:2 * n + 11]
        if n:
            send_sems, recv_sems = refs[2 * n + 11:]
            _ride_along(scatter_ici_copies(src, dst, send_sems, recv_sems), (pl.program_id(0),), (nh,))

        @pl.when(pl.program_id(0) == 0)
        def _():
            bias_ref[...] = jnp.where(_chunk_mask_t(0, 0, blk, blk), 0.0, MASK_VALUE)

        dq_ref[...] = jnp.zeros_like(dq_ref)
        dk_ref[...] = jnp.zeros_like(dk_ref)
        dv_ref[...] = jnp.zeros_like(dv_ref)

        def rows(x):
            return pl.ds(pl.multiple_of(x * blk, blk), blk)

        def after(jb):
            j, b = jb
            wrap = b == nq - 1 - j
            return jnp.where(wrap, j + 1, j), jnp.where(wrap, 0, b + 1)

        def products(jb, slot):
            j, b = jb
            s_buf[slot] = lax.dot_general(k_ref[rows(j), :], q_ref[rows(j + b), :], _NT, preferred_element_type=F32)
            dp_buf[slot] = lax.dot_general(v_ref[rows(j), :], do_ref[rows(j + b), :].astype(BF16), _NT, preferred_element_type=F32)

        def softmax_bwd(jb, slot):
            j, b = jb
            s = s_buf[slot] + bias_ref[...] * (b == 0).astype(F32)
            p = jnp.exp2(s * SCORE_SCALE_LOG2 - lse_ref[j + b] * LOG2_E)
            p_buf[slot] = p.astype(BF16)
            ds_buf[slot] = (p * (dp_buf[slot] - dl_ref[j + b]) * scale).astype(BF16)

        def gradients(jb, slot):
            j, b = jb
            dv_ref[rows(j), :] += jnp.dot(p_buf[slot], do_ref[rows(j + b), :].astype(BF16), preferred_element_type=F32)
            dk_ref[rows(j), :] += jnp.dot(ds_buf[slot], q_ref[rows(j + b), :], preferred_element_type=F32)
            dq_ref[rows(j + b), :] += lax.dot_general(ds_buf[slot], k_ref[rows(j), :], _TN, preferred_element_type=F32)

        def step(state, slot):
            third, second, first = state
            gradients(third, slot)
            softmax_bwd(second, 1 - slot)
            products(first, slot)
            return second, first, after(first)

        zero = jnp.int32(0)
        pair0 = (zero, zero)
        products(pair0, 0)
        if n_pairs == 1:
            softmax_bwd(pair0, 0)
            gradients(pair0, 0)
        else:
            pair1 = after(pair0)
            products(pair1, 1)
            softmax_bwd(pair0, 0)
            steady = n_pairs - 2
            state = lax.fori_loop(0, steady // 2, lambda u, st: step(step(st, 0), 1), (pair0, pair1, after(pair1)))
            if steady % 2:
                state = step(state, 0)
            before_last, last_pair, _ = state
            last = (n_pairs - 1) % 2
            softmax_bwd(last_pair, last)
            gradients(before_last, 1 - last)
            gradients(last_pair, last)
        dq_out[...] = dq_ref[...].astype(BF16)
        dk_out[...] = dk_ref[...].astype(BF16)
        dv_out[...] = dv_ref[...].astype(BF16)

    head = lambda w: pl.BlockSpec((None, t, w), lambda h: (h, 0, 0))
    stats = pl.BlockSpec((None, nq, 1, blk), lambda h: (h, 0, 0, 0))
    outs = pl.pallas_call(
        body, name=name, grid=(nh,),
        in_specs=[head(QK), head(QK), head(VDIM), pl.BlockSpec((t, VDIM), lambda h: (0, h)), stats, stats] + [ANY] * n,
        out_specs=[head(QK), head(QK), head(VDIM)] + [ANY] * n,
        out_shape=[S((nh, t, QK), BF16), S((nh, t, QK), BF16), S((nh, t, VDIM), BF16)] + [S(p.shape, p.dtype) for p in parts],
        scratch_shapes=[pltpu.VMEM((2, blk, blk), F32), pltpu.VMEM((2, blk, blk), F32), pltpu.VMEM((2, blk, blk), BF16),
                        pltpu.VMEM((2, blk, blk), BF16), pltpu.VMEM((blk, blk), F32),
                        pltpu.VMEM((t, QK), F32), pltpu.VMEM((t, QK), F32), pltpu.VMEM((t, VDIM), F32)]
        + ([pltpu.SemaphoreType.DMA((n, 3)), pltpu.SemaphoreType.DMA((n, 3))] if n else []),
        compiler_params=_params(1, VMEM_LIMIT_WHOLE_HEAD))(q, k, v, do, lse, delta, *parts)
    return outs[0], outs[1], outs[2], list(outs[3:])


def _shift_down(u, s):
    rows = lax.broadcasted_iota(jnp.int32, u.shape, 0)
    return jnp.where(rows >= s, pltpu.roll(u, s, 0), 0.0)


def _shift_up(u, s):
    n = u.shape[0]
    rows = lax.broadcasted_iota(jnp.int32, u.shape, 0)
    return jnp.where(rows < n - s, pltpu.roll(u, n - s, 0), 0.0)


def _conv_specs(t, d, lanes):
    slab = lambda part: pl.BlockSpec((None, t, lanes), lambda j, part=part: (part, 0, j))
    return slab, pl.BlockSpec((3, lanes), lambda j: (0, j)), pl.BlockSpec((t, lanes), lambda j: (0, j))


def conv_bwd(name, bcx, w, dy):
    _, t, d = bcx.shape
    lanes = _tile(d, 128, 128)
    slab, w_spec, col = _conv_specs(t, d, lanes)

    def body(b_ref, c_ref, x_ref, w_ref, dy_ref, d_ref, dw_ref):
        c = c_ref[...].astype(F32)
        x = x_ref[...].astype(F32)
        dyv = dy_ref[...]
        u = c * x
        u1 = _shift_down(u, 1)
        u2 = _shift_down(u, 2)
        w0, w1, w2 = w_ref[0:1, :], w_ref[1:2, :], w_ref[2:3, :]
        d_ref[0] = (dyv * (w0 * u2 + w1 * u1 + w2 * u)).astype(BF16)
        duc = dyv * b_ref[...].astype(F32)
        dw_ref[0:1, :] = jnp.sum(duc * u2, axis=0, keepdims=True)
        dw_ref[1:2, :] = jnp.sum(duc * u1, axis=0, keepdims=True)
        dw_ref[2:3, :] = jnp.sum(duc * u, axis=0, keepdims=True)
        du = w2 * duc + w1 * _shift_up(duc, 1) + w0 * _shift_up(duc, 2)
        d_ref[1] = (du * x).astype(BF16)
        d_ref[2] = (du * c).astype(BF16)

    return pl.pallas_call(
        body, name=name, grid=(d // lanes,), in_specs=[slab(0), slab(1), slab(2), w_spec, col],
        out_specs=[pl.BlockSpec((3, t, lanes), lambda j: (0, 0, j)), w_spec], out_shape=[S((3, t, d), BF16), S((3, d), F32)],
        compiler_params=_params(1))(bcx, bcx, bcx, w, dy)


def _adamw_update(w, g, m, v):
    m_new = ADAM_B1 * m + (1.0 - ADAM_B1) * g
    v_new = ADAM_B2 * v + (1.0 - ADAM_B2) * (g * g)
    m_hat = m_new / (1.0 - ADAM_B1 ** ADAM_STEP)
    v_hat = v_new / (1.0 - ADAM_B2 ** ADAM_STEP)
    return -ADAM_LR * (m_hat / (jnp.sqrt(v_hat) + ADAM_EPS) + ADAM_WD * w), m_new, v_new


def adamw(name, w, g, m, v):
    r, c = w.shape
    tr = _tile(r, 512)

    def body(w_ref, g_ref, m_ref, v_ref, d_ref, mo_ref, vo_ref):
        d_ref[...], mo_ref[...], vo_ref[...] = _adamw_update(w_ref[...], g_ref[...], m_ref[...], v_ref[...])

    blk = pl.BlockSpec((tr, c), lambda i: (i, 0))
    return pl.pallas_call(
        body, name=name, grid=(r // tr,), in_specs=[blk] * 4, out_specs=[blk] * 3, out_shape=[S((r, c), F32)] * 3,
        compiler_params=_params(1))(w, g, m, v)


def adamw_swapped(name, wt, g, mt, vt):
    nl, c, r = wt.shape
    tr = _tile(r, 512, 128)
    nr = r // tr

    def body(w_ref, g_ref, m_ref, v_ref, go_ref, d_ref, mo_ref, vo_ref):
        gt = g_ref[...].T
        go_ref[...] = gt
        d_ref[...], mo_ref[...], vo_ref[...] = _adamw_update(w_ref[...], gt, m_ref[...], v_ref[...])

    swapped = pl.BlockSpec((None, c, tr), lambda l, i: (l, 0, i))
    return pl.pallas_call(
        body, name=name, grid=(nl, nr),
        in_specs=[swapped, pl.BlockSpec((tr, c), lambda l, i: (l * nr + i, 0)), swapped, swapped],
        out_specs=[swapped] * 4, out_shape=[S((nl, c, r), F32)] * 4, compiler_params=_params(2))(wt, g, mt, vt)


def _place():
    x, y, c = lax.axis_index("x"), lax.axis_index("y"), lax.axis_index("c")
    other_chips = [(1 - x, y), (x, 1 - y), (1 - x, 1 - y)]
    return x, y, c, other_chips


def _half(c, rows):
    return pl.ds(pl.multiple_of(c * (rows // 2), 16), rows // 2)


def gather_weight_shards(shards, small=()):
    n, ns = len(shards), len(small)

    def body(*refs):
        src = refs[:n]
        small_src = refs[n:n + ns]
        dst = refs[n + ns:2 * n + ns]
        small_dst = refs[2 * n + ns:2 * (n + ns)]
        send_sems, recv_sems, small_send, small_recv, local_sems = refs[2 * (n + ns):2 * (n + ns) + 5]
        stages = refs[2 * (n + ns) + 5:]
        x, y, c, chips = _place()
        me = 2 * x + y
        sibling = (x, y, 1 - c)
        own = [(s_ref, stages[i], d_ref.at[me], local_sems.at[i])
               for i, (s_ref, d_ref) in enumerate(zip(list(src) + list(small_src), list(dst) + list(small_dst)))]
        for s_ref, stage, _, sems in own:
            pltpu.make_async_copy(s_ref, stage, sems.at[0]).start()
        small_pairs = []
        for i in range(ns):
            for j, (px, py) in enumerate(chips):
                def whole(slot):
                    return pltpu.make_async_remote_copy(
                        src_ref=small_src[i], dst_ref=small_dst[i].at[slot], send_sem=small_send.at[i, j],
                        recv_sem=small_recv.at[i, j], device_id=(px, py, c), device_id_type=MESH)
                small_pairs.append((whole(me), whole(2 * px + py)))
        for outgoing, _ in small_pairs:
            outgoing.start()

        def copy(i, slot, half_of, sem, to, from_input=False):
            rows = _half(half_of, src[i].shape[0])
            return pltpu.make_async_remote_copy(
                src_ref=src[i].at[rows] if from_input else dst[i].at[slot, rows], dst_ref=dst[i].at[slot, rows],
                send_sem=send_sems.at[i, sem], recv_sem=recv_sems.at[i, sem], device_id=to, device_id_type=MESH)

        sent = []
        for i in range(n):
            for j, chip in enumerate(chips):
                sent.append(copy(i, me, c, j, (*chip, c), from_input=True))
                sent[-1].start()
        for i in range(n):
            for j, (px, py) in enumerate(chips):
                copy(i, 2 * px + py, c, j, sibling).wait_recv()
                sent.append(copy(i, 2 * px + py, c, 3 + j, sibling))
                sent[-1].start()
        for i in range(n):
            for j, (px, py) in enumerate(chips):
                copy(i, 2 * px + py, 1 - c, 3 + j, sibling).wait_recv()
        for cp in sent:
            cp.wait_send()
        for _, incoming in small_pairs:
            incoming.wait_recv()
        for outgoing, _ in small_pairs:
            outgoing.wait_send()
        _place_locally(own)

    everything = list(shards) + list(small)
    outs = pl.pallas_call(
        body, name="gather_weight_shards", in_specs=[ANY] * (n + ns), out_specs=[ANY] * (n + ns),
        out_shape=[S((N_CHIPS,) + s.shape, s.dtype) for s in everything],
        scratch_shapes=[pltpu.SemaphoreType.DMA((n, 6)), pltpu.SemaphoreType.DMA((n, 6)),
                        pltpu.SemaphoreType.DMA((max(ns, 1), 3)), pltpu.SemaphoreType.DMA((max(ns, 1), 3)),
                        pltpu.SemaphoreType.DMA((n + ns, 2))] + [pltpu.VMEM(s.shape, s.dtype) for s in everything],
    )(*shards, *small)
    return list(outs[:n]), list(outs[n:])


def gather_ici_copies(src, dst, send_sems, recv_sems):
    x, y, c, chips = _place()
    me = 2 * x + y
    pairs = []
    for i in range(len(src)):
        rows = _half(c, src[i].shape[0])
        for j, (px, py) in enumerate(chips):
            def copy(slot):
                return pltpu.make_async_remote_copy(
                    src_ref=src[i].at[rows], dst_ref=dst[i].at[slot, rows], send_sem=send_sems.at[i, j],
                    recv_sem=recv_sems.at[i, j], device_id=(px, py, c), device_id_type=MESH)
            pairs.append((copy(me), copy(2 * px + py)))
    return pairs


def scatter_ici_copies(src, dst, send_sems, recv_sems):
    x, y, c, chips = _place()
    me = 2 * x + y
    pairs = []
    for i in range(len(src)):
        for j, (px, py) in enumerate(chips):
            def copy(from_slot, to_slot):
                return pltpu.make_async_remote_copy(
                    src_ref=src[i].at[from_slot], dst_ref=dst[i].at[to_slot], send_sem=send_sems.at[i, j],
                    recv_sem=recv_sems.at[i, j], device_id=(px, py, c), device_id_type=MESH)
            pairs.append((copy(2 * px + py, me), copy(me, 2 * px + py)))
    return pairs


def _ride_along(pairs, grid_ids, grid_sizes, local=()):
    first = grid_ids[0] == 0
    last = grid_ids[0] == grid_sizes[0] - 1
    for g, size in zip(grid_ids[1:], grid_sizes[1:]):
        first = first & (g == 0)
        last = last & (g == size - 1)

    @pl.when(first)
    def _():
        for outgoing, _ in pairs:
            outgoing.start()
        for src, stage, _, sems in local:
            pltpu.make_async_copy(src, stage, sems.at[0]).start()

    @pl.when(last)
    def _():
        for _, incoming in pairs:
            incoming.wait_recv()
        for outgoing, _ in pairs:
            outgoing.wait_send()
        _place_locally(local)


def _place_locally(local):
    for src, stage, _, sems in local:
        pltpu.make_async_copy(src, stage, sems.at[0]).wait()
    placed = [pltpu.make_async_copy(stage, dst, sems.at[1]) for _, stage, dst, sems in local]
    for cp in placed:
        cp.start()
    for cp in placed:
        cp.wait()


def forward_copies(src, dst, send_sems, recv_sems):
    x, y, c, chips = _place()
    pairs = []
    for i in range(len(src)):
        for j, (px, py) in enumerate(chips):
            def copy(half_of):
                rows = _half(half_of, src[i].shape[1])
                return pltpu.make_async_remote_copy(
                    src_ref=src[i].at[2 * px + py, rows], dst_ref=dst[i].at[2 * px + py, rows], send_sem=send_sems.at[i, j],
                    recv_sem=recv_sems.at[i, j], device_id=(x, y, 1 - c), device_id_type=MESH)
            pairs.append((copy(c), copy(1 - c)))
    return pairs


def attention_out_proj(name, attn, w_o, resid, next_gain, arriving, own):
    t, kdim = attn.shape
    n = w_o.shape[1]
    tm = _tile(t, 512)
    m = len(arriving)

    def body(x_ref, w_ref, r_ref, g_ref, *refs):
        src, own_refs = refs[:m], refs[m:2 * m]
        h_ref, a_ref = refs[2 * m:2 * m + 2]
        dst = refs[2 * m + 2:3 * m + 2]
        stages = refs[3 * m + 2:4 * m + 2]
        send_sems, recv_sems, local_sems = refs[4 * m + 2:]
        me = 2 * lax.axis_index("x") + lax.axis_index("y")
        placed = [(own_refs[i], stages[i], dst[i].at[me], local_sems.at[i]) for i in range(m)]
        _ride_along(forward_copies(src, dst, send_sems, recv_sems), (pl.program_id(0),), (t // tm,), placed)
        y = r_ref[...] + jnp.dot(x_ref[...].astype(BF16), w_ref[...], preferred_element_type=F32)
        h_ref[...] = y
        a_ref[...] = (y * _rstd(y) * g_ref[...]).astype(BF16)

    row = pl.BlockSpec((tm, n), lambda i: (i, 0))
    outs = pl.pallas_call(
        body, name=name, grid=(t // tm,),
        in_specs=[pl.BlockSpec((tm, kdim), lambda i: (i, 0)), _resident((kdim, n), lambda i: (0, 0)), row,
                  pl.BlockSpec((1, n), lambda i: (0, 0))] + [ANY] * (2 * m),
        out_specs=[row, row] + [ANY] * m, out_shape=[S((t, n), F32), S((t, n), BF16)] + [S(g.shape, g.dtype) for g in arriving],
        input_output_aliases={4 + i: 2 + i for i in range(m)},
        scratch_shapes=[pltpu.VMEM(o.shape, o.dtype) for o in own]
        + [pltpu.SemaphoreType.DMA((m, 3)), pltpu.SemaphoreType.DMA((m, 3)), pltpu.SemaphoreType.DMA((m, 2))],
        compiler_params=_params(1))(attn, w_o, resid, next_gain, *arriving, *own)
    return outs[0], outs[1], list(outs[2:])


def swap_copies(src, dst, send_sems, recv_sems):
    x, y, c, _ = _place()
    pairs = []
    for i in range(len(src)):
        cp = pltpu.make_async_remote_copy(
            src_ref=src[i].at[:, _half(1 - c, src[i].shape[1]), :], dst_ref=dst[i], send_sem=send_sems.at[i],
            recv_sem=recv_sems.at[i], device_id=(x, y, 1 - c), device_id_type=MESH)
        pairs.append((cp, cp))
    return pairs


def _swap_shapes(grads):
    return [S((g.shape[0], g.shape[1] // 2, g.shape[2]), g.dtype) for g in grads]


def sibling_swap_halves(name, grads):
    n = len(grads)

    def body(*refs):
        pairs = swap_copies(refs[:n], refs[n:2 * n], *refs[2 * n:])
        for outgoing, _ in pairs:
            outgoing.start()
        for _, incoming in pairs:
            incoming.wait_recv()
        for outgoing, _ in pairs:
            outgoing.wait_send()

    return pl.pallas_call(
        body, name=name, in_specs=[ANY] * n, out_specs=[ANY] * n, out_shape=_swap_shapes(grads),
        scratch_shapes=[pltpu.SemaphoreType.DMA((n,)), pltpu.SemaphoreType.DMA((n,))],
    )(*grads)


def add_halves(name, g, rx):
    _, r, cdim = g.shape
    r2 = r // 2
    tr = _tile(r2, 512, 16)
    nb = r2 // tr

    def body(lo_ref, hi_ref, rx_ref, o_ref):
        mine = jnp.where(lax.axis_index("c") == 0, lo_ref[...], hi_ref[...])
        o_ref[...] = (mine.astype(F32) + rx_ref[...].astype(F32)).astype(BF16)

    half = pl.BlockSpec((None, tr, cdim), lambda k, i: (k, i, 0))
    return pl.pallas_call(
        body, name=name, grid=(N_CHIPS, nb),
        in_specs=[half, pl.BlockSpec((None, tr, cdim), lambda k, i: (k, nb + i, 0)), half],
        out_specs=half, out_shape=S((N_CHIPS, r2, cdim), BF16), compiler_params=_params(2))(g, g, rx)


def sum_chips(name, arrived, mine):
    _, r2, cdim = arrived.shape
    tr = _tile(r2, 512, 16)

    def body(a_ref, m_ref, o_ref):
        me = 2 * lax.axis_index("x") + lax.axis_index("y")
        acc = jnp.zeros((tr, cdim), F32)
        for k in range(N_CHIPS):
            acc = acc + jnp.where(me == k, m_ref[k], a_ref[k]).astype(F32)
        o_ref[...] = acc

    slots = pl.BlockSpec((N_CHIPS, tr, cdim), lambda i: (0, i, 0))
    return pl.pallas_call(
        body, name=name, grid=(r2 // tr,), in_specs=[slots, slots],
        out_specs=pl.BlockSpec((tr, cdim), lambda i: (i, 0)), out_shape=S((r2, cdim), F32), compiler_params=_params(1))(arrived, mine)


def join_copies(src, dst, where, send_sems, recv_sems):
    x, y, c, _ = _place()
    pairs = []
    for i in range(len(src)):
        def copy(half_of):
            r2 = src[i].shape[0]
            rows = pl.ds(pl.multiple_of(where[i][1] + half_of * r2, 8), r2)
            return pltpu.make_async_remote_copy(
                src_ref=src[i], dst_ref=dst[where[i][0]].at[rows], send_sem=send_sems.at[i],
                recv_sem=recv_sems.at[i], device_id=(x, y, 1 - c), device_id_type=MESH)
        pairs.append((copy(c), copy(1 - c)))
    return pairs


def sibling_join_halves(name, halves, targets, where):
    n = len(halves)

    def body(*refs):
        src, dst = refs[:n], refs[n:n + len(targets)]
        send_sems, recv_sems, local_sems = refs[n + len(targets):n + len(targets) + 3]
        stages = refs[n + len(targets) + 3:]
        c = lax.axis_index("c")
        own = [(src[i], stages[i],
                dst[where[i][0]].at[pl.ds(pl.multiple_of(where[i][1] + c * src[i].shape[0], 8), src[i].shape[0])],
                local_sems.at[i]) for i in range(n)]
        for s_ref, stage, _, sems in own:
            pltpu.make_async_copy(s_ref, stage, sems.at[0]).start()
        pairs = join_copies(src, dst, where, send_sems, recv_sems)
        for outgoing, _ in pairs:
            outgoing.start()
        for _, incoming in pairs:
            incoming.wait_recv()
        for outgoing, _ in pairs:
            outgoing.wait_send()
        _place_locally(own)

    return list(pl.pallas_call(
        body, name=name, in_specs=[ANY] * n, out_specs=[ANY] * len(targets), out_shape=[S(tg, F32) for tg in targets],
        scratch_shapes=[pltpu.SemaphoreType.DMA((n,)), pltpu.SemaphoreType.DMA((n,)), pltpu.SemaphoreType.DMA((n, 2))]
        + [pltpu.VMEM(h.shape, h.dtype) for h in halves],
    )(*halves))


def all_reduce_small(name, packed):
    rows, width = packed.shape

    def body(x_ref, o_ref, gathered, send_sems, recv_sems):
        x, y, c, _ = _place()
        me = 4 * x + 2 * y + c
        gathered[me] = x_ref[...]
        flips = [(fx, fy, fc) for fx in (0, 1) for fy in (0, 1) for fc in (0, 1)][1:]

        def copy(r, slot, to):
            return pltpu.make_async_remote_copy(
                src_ref=x_ref, dst_ref=gathered.at[slot], send_sem=send_sems.at[r], recv_sem=recv_sems.at[r],
                device_id=to, device_id_type=MESH)

        def peer(f):
            return (x ^ f[0], y ^ f[1], c ^ f[2])

        sent = [copy(r, me, peer(f)) for r, f in enumerate(flips)]
        for cp in sent:
            cp.start()
        for r, f in enumerate(flips):
            px, py, pc = peer(f)
            copy(r, 4 * px + 2 * py + pc, peer(f)).wait_recv()
        for cp in sent:
            cp.wait_send()
        acc = gathered[0]
        for k in range(1, N_DEV):
            acc = acc + gathered[k]
        o_ref[...] = acc

    vmem = pl.BlockSpec(memory_space=pltpu.VMEM)
    return pl.pallas_call(
        body, name=name, in_specs=[vmem], out_specs=vmem, out_shape=S((rows, width), F32),
        scratch_shapes=[pltpu.VMEM((N_DEV, rows, width), F32), pltpu.SemaphoreType.DMA((N_DEV - 1,)),
                        pltpu.SemaphoreType.DMA((N_DEV - 1,))],
    )(packed)


def _rope_tables(positions):
    inv_freq = 1.0 / (ROPE_THETA ** (jnp.arange(0, ROPE, 2, dtype=F32) / ROPE))
    ang = positions.astype(F32)[:, None] * inv_freq
    return jnp.cos(ang), jnp.sin(ang)


def _unstack_cols(w):
    k4, k, n4 = w.shape
    return jnp.transpose(w, (1, 0, 2)).reshape(k, k4 * n4)


def _stack_cols(w):
    k, n = w.shape
    return jnp.transpose(w.reshape(k, N_CHIPS, n // N_CHIPS), (1, 0, 2))


def kernel(x, positions, mla_norm, mla_w_in, mla_g_cq, mla_g_ckv, mla_w_uq, mla_w_ukv, mla_w_o, conv_norm, conv_w_in, conv_w, conv_w_out, ffn_norm, ffn_w_gate, ffn_w_up, ffn_w_down, final_norm, loss_target, m_mla_norm, m_mla_w_in, m_mla_g_cq, m_mla_g_ckv, m_mla_w_uq, m_mla_w_ukv, m_mla_w_o, m_conv_norm, m_conv_w_in, m_conv_w, m_conv_w_out, m_ffn_norm, m_ffn_w_gate, m_ffn_w_up, m_ffn_w_down, m_final_norm, v_mla_norm, v_mla_w_in, v_mla_g_cq, v_mla_g_ckv, v_mla_w_uq, v_mla_w_ukv, v_mla_w_o, v_conv_norm, v_conv_w_in, v_conv_w, v_conv_w_out, v_ffn_norm, v_ffn_w_gate, v_ffn_w_up, v_ffn_w_down, v_final_norm):
    weights = dict(mla_norm=mla_norm, mla_w_in=mla_w_in, mla_g_cq=mla_g_cq, mla_g_ckv=mla_g_ckv, mla_w_uq=mla_w_uq,
                   mla_w_ukv=mla_w_ukv, mla_w_o=mla_w_o, conv_norm=conv_norm, conv_w_in=conv_w_in, conv_w=conv_w,
                   conv_w_out=conv_w_out, ffn_norm=ffn_norm, ffn_w_gate=ffn_w_gate, ffn_w_up=ffn_w_up,
                   ffn_w_down=ffn_w_down, final_norm=final_norm)
    m_in = dict(mla_norm=m_mla_norm, mla_w_in=m_mla_w_in, mla_g_cq=m_mla_g_cq, mla_g_ckv=m_mla_g_ckv, mla_w_uq=m_mla_w_uq,
                mla_w_ukv=m_mla_w_ukv, mla_w_o=m_mla_w_o, conv_norm=m_conv_norm, conv_w_in=m_conv_w_in, conv_w=m_conv_w,
                conv_w_out=m_conv_w_out, ffn_norm=m_ffn_norm, ffn_w_gate=m_ffn_w_gate, ffn_w_up=m_ffn_w_up,
                ffn_w_down=m_ffn_w_down, final_norm=m_final_norm)
    v_in = dict(mla_norm=v_mla_norm, mla_w_in=v_mla_w_in, mla_g_cq=v_mla_g_cq, mla_g_ckv=v_mla_g_ckv, mla_w_uq=v_mla_w_uq,
                mla_w_ukv=v_mla_w_ukv, mla_w_o=v_mla_w_o, conv_norm=v_conv_norm, conv_w_in=v_conv_w_in, conv_w=v_conv_w,
                conv_w_out=v_conv_w_out, ffn_norm=v_ffn_norm, ffn_w_gate=v_ffn_w_gate, ffn_w_up=v_ffn_w_up,
                ffn_w_down=v_ffn_w_down, final_norm=v_final_norm)
    big = ["mla_w_in", "mla_w_uq", "mla_w_ukv", "mla_w_o", "conv_w_in", "conv_w_out", "ffn_w_gate", "ffn_w_up", "ffn_w_down"]
    order = list(weights)

    t, d = x.shape[1], x.shape[2]
    h0 = x.reshape(t, d)
    target = loss_target.reshape(t, d)
    cos, sin = _rope_tables(positions.reshape(t))

    def rows2d(a):
        return a.reshape(-1, a.shape[-1])

    first, later = big[:4], big[4:]
    shards = {n: rows2d(weights[n]).astype(BF16) for n in big}
    d4 = d // N_CHIPS
    first_gathered, (conv_norm_slots, conv_w_slots) = gather_weight_shards(
        [shards[n] for n in first], [conv_norm.reshape(1, d4), conv_w.reshape(3, d4)])
    gathered = dict(zip(first, first_gathered))
    conv_norm_full = conv_norm_slots.reshape(1, d)
    conv_w_full = jnp.transpose(conv_w_slots, (1, 0, 2)).reshape(3, d)
    w_in = gathered["mla_w_in"].reshape(-1, gathered["mla_w_in"].shape[-1])
    w_uq = _unstack_cols(gathered["mla_w_uq"])
    w_ukv = _unstack_cols(gathered["mla_w_ukv"])
    w_o = gathered["mla_w_o"].reshape(-1, d)

    chip = 2 * lax.axis_index("x") + lax.axis_index("y")

    def pack_rows(rows):
        idx = lax.broadcasted_iota(jnp.int32, (SMALL_ROWS, d), 0)
        out = jnp.zeros((SMALL_ROWS, d), F32)
        for r, row in enumerate(rows):
            out = out + jnp.where(idx == r, row, 0.0)
        return out


    a0 = rms_fwd("mla_norm_fwd", h0, mla_norm)
    proj, cq, ckv, kr = mla_in_proj("mla_in_proj", a0, w_in, mla_g_cq, mla_g_ckv, cos, sin)
    qh, kh, vh, conv_arriving = qkv_heads("qkv_heads", cq, ckv, w_uq, w_ukv, kr, cos, sin, [shards[n] for n in later[:2]])
    attn, lse, ffn_arriving = attention_fwd("attention_fwd", qh, kh, vh, [shards[n] for n in later[2:]])
    h1, a1, handed = attention_out_proj("mla_out_proj", attn, w_o, h0, ffn_norm[0:1], conv_arriving + ffn_arriving,
                                        [shards[n] for n in later])
    gathered.update(zip(later, handed))
    cw_in = _unstack_cols(gathered["conv_w_in"])
    cw_out = gathered["conv_w_out"].reshape(-1, d)
    wg_all, wu_all, wd_all = gathered["ffn_w_gate"], gathered["ffn_w_up"], gathered["ffn_w_down"]

    def ffn_forward(tag, h, a, layer, next_gain):
        g, u, z = ffn_up(f"ffn{tag}_up", a, wg_all, wu_all, layer)
        return g, u, z, ffn_down(f"ffn{tag}_down", z, wd_all, layer, h, next_gain)

    g0, u0, z0, (h2, a2) = ffn_forward(0, h1, a1, 0, conv_norm_full)
    bcx, yc = conv_in_proj("conv_in_proj", a2, cw_in, conv_w_full)
    h3, a3 = linear("conv_out_proj", yc, cw_out, F32, resid=h2, next_gain=ffn_norm[1:2])
    g1, u1, z1 = ffn_up("ffn1_up", a3, wg_all, wu_all, 1)
    dh4, d_final_norm, loss_local = ffn_down_loss("ffn1_down_loss", z1, wd_all, 1, h3, final_norm.reshape(1, d), target)

    def ffn_backward(tag, dh, h, layer, a, g, u, z, swap=()):
        dg, du, swapped = ffn_bwd_hidden(f"ffn{tag}_bwd_hidden", dh, wd_all, layer, g, u, swap)
        d_wd = ffn_wgrad_down(f"ffn{tag}_wgrad_down", z, dh)
        dh_prev, d_norm = ffn_bwd_input(f"ffn{tag}_bwd_input", dg, du, wg_all, wu_all, layer, h, ffn_norm[layer:layer + 1], dh)
        d_wg = ffn_wgrad_up(f"ffn{tag}_wgrad_gate", a, dg)
        d_wu = ffn_wgrad_up(f"ffn{tag}_wgrad_up", a, du)
        return dh_prev, d_norm, [d_wg, d_wu, d_wd], swapped

    def pair_sums(tag, local, from_sibling):
        return [add_halves(f"pair_sum_{tag}{i}", g, r) for i, (g, r) in enumerate(zip(local, from_sibling))]

    def sum_from_chips(tag, pairs, arrived):
        return [sum_chips(f"chip_sum_{tag}{i}", a, p) for i, (a, p) in enumerate(zip(arrived, pairs))]

    def shard_shape(n):
        return rows2d(weights[n]).shape

    dh3, d_ffn_norm1, ffn1_grads, _ = ffn_backward(1, dh4, h3, 1, a3, g1, u1, z1)

    dyc = linear_nt("conv_out_bwd_input", dh3, cw_out, F32)
    d_cw_out = wgrad("conv_out_wgrad", yc, dh3)
    dbcx, d_conv_w = conv_bwd("conv_bwd", bcx, conv_w_full, dyc)
    dh2, d_conv_norm = conv_in_bwd_input("conv_in_bwd_input", dbcx, cw_in, h2, conv_norm_full, dh3)
    d_cw_in = conv_in_wgrad("conv_in_wgrad", a2, dbcx)

    second = [d_cw_in, d_cw_out.reshape(N_CHIPS, -1, d)] + ffn1_grads
    dh1, d_ffn_norm0, ffn0_grads, second_swapped = ffn_backward(0, dh2, h1, 0, a1, g0, u0, z0, second)
    d_w_o = wgrad("mla_out_wgrad", attn, dh1)
    first_part = ffn0_grads + [d_w_o.reshape(N_CHIPS, -1, d)]
    d_attn, delta, first_swapped = attention_out_bwd("mla_out_bwd_input", dh1, w_o, attn, first_part)
    rest_pairs = pair_sums("rest", second + first_part, second_swapped + first_swapped)
    dqh, dkh, dvh, rest_arrived = attention_bwd("attention_bwd", qh, kh, vh, d_attn, lse, delta, rest_pairs)
    rd, rf = ffn0_grads[0].shape[1], ffn0_grads[2].shape[1]
    rest_where = [(0, 0), (1, 0), (2, rd), (3, rd), (4, rf), (2, 0), (3, 0), (4, 0), (5, 0)]
    rest_names = later + ["mla_w_o"]
    dq, dkv, dproj, d_g_cq, d_g_ckv, rest_grads = qkv_heads_bwd(
        "qkv_heads_bwd", dqh, dkh, dvh, cos, sin, w_uq, w_ukv, proj, mla_g_cq, mla_g_ckv,
        sum_from_chips("rest", rest_pairs, rest_arrived), [shard_shape(n) for n in rest_names], rest_where)
    grads = dict(zip(rest_names, rest_grads))
    d_w_uq = wgrad("mla_q_up_wgrad", cq, dq)
    d_w_ukv = wgrad("mla_kv_up_wgrad", ckv, dkv)
    d_w_in = wgrad("mla_in_wgrad", a0, dproj)
    mla_local = [d_w_in.reshape(N_CHIPS, -1, d_w_in.shape[-1]), _stack_cols(d_w_uq), _stack_cols(d_w_ukv)]
    mla_pairs = pair_sums("mla", mla_local, sibling_swap_halves("sibling_swap_mla", mla_local))
    grad_x, d_mla_norm, mla_arrived = linear_nt_norm_bwd("mla_in_bwd_input", dproj, w_in, h0, mla_norm, dh1, mla_pairs)

    grads.update(zip(first[:3], sibling_join_halves("sibling_join_mla", sum_from_chips("mla", mla_pairs, mla_arrived),
                                                    [shard_shape(n) for n in first[:3]], [(i, 0) for i in range(3)])))

    def pad_row(v):
        return jnp.pad(v, ((0, 0), (0, d - v.shape[1])))

    small = all_reduce_small("all_reduce_small_grads", pack_rows([
        d_mla_norm, pad_row(d_g_cq), pad_row(d_g_ckv), d_ffn_norm0, d_ffn_norm1, d_final_norm, d_conv_norm,
        d_conv_w[0:1], d_conv_w[1:2], d_conv_w[2:3], jnp.broadcast_to(loss_local, (1, d))]))
    loss = small[10, 0]
    grads["mla_norm"] = small[0:1]
    grads["mla_g_cq"] = small[1:2, :mla_g_cq.shape[1]]
    grads["mla_g_ckv"] = small[2:3, :mla_g_ckv.shape[1]]
    grads["ffn_norm"] = small[3:5]
    grads["final_norm"] = small[5:6]
    grads["conv_norm"] = lax.dynamic_slice(small[6:7], (0, chip * d4), (1, d4))
    grads["conv_w"] = lax.dynamic_slice(small[7:10], (0, chip * d4), (3, d4))

    outs_g, outs_d, outs_m, outs_v = [], [], [], []
    for n in order:
        w = weights[n]
        if w.ndim == 3 and w.shape[2] % 128 and w.shape[1] % 128 == 0:
            results = adamw_swapped(f"adamw_{n}", jnp.swapaxes(w, 1, 2), grads[n].reshape(-1, w.shape[2]),
                                    jnp.swapaxes(m_in[n], 1, 2), jnp.swapaxes(v_in[n], 1, 2))
            grad_w, delta_w, new_m, new_v = [jnp.swapaxes(o, 1, 2) for o in results]
        else:
            delta_w, new_m, new_v = adamw(f"adamw_{n}", rows2d(w), grads[n].reshape(rows2d(w).shape), rows2d(m_in[n]), rows2d(v_in[n]))
            grad_w = grads[n]
        outs_g.append(grad_w.reshape(w.shape))
        outs_d.append(delta_w.reshape(w.shape))
        outs_m.append(new_m.reshape(w.shape))
        outs_v.append(new_v.reshape(w.shape))
    return (loss, grad_x.reshape(x.shape), *outs_g, *outs_d, *outs_m, *outs_v)
```

```python
import math

import jax
import jax.numpy as jnp
from jax import lax
from jax.experimental import pallas as pl
from jax.experimental.pallas import tpu as pltpu

F32 = jnp.float32
BF16 = jnp.bfloat16
S = jax.ShapeDtypeStruct

N_HEADS = 8
NOPE = 128
ROPE = 64
HALF = ROPE // 2
VDIM = 128
QK = NOPE + ROPE
CHUNK = 64
ROPE_THETA = 10000.0
RMS_EPS = 1e-6
ADAM_LR = 0.001
ADAM_B1 = 0.9
ADAM_B2 = 0.999
ADAM_EPS = 1e-08
ADAM_WD = 0.01
ADAM_STEP = 10

N_CHIPS = 4
N_DEV = 8
MASK_VALUE = -1e30
SCORE_SCALE = 1.0 / math.sqrt(QK)
LOG2_E = math.log2(math.e)
SCORE_SCALE_LOG2 = SCORE_SCALE * LOG2_E
VMEM_LIMIT = 48 * 1024 * 1024
VMEM_LIMIT_WHOLE_HEAD = 58 * 1024 * 1024
ATT_BLOCK = 512
CONV_SAVED_DTYPE = jnp.bfloat16
SMALL_ROWS = 16

_NN = (((1,), (0,)), ((), ()))
_NT = (((1,), (1,)), ((), ()))
_TN = (((0,), (0,)), ((), ()))
MESH = pl.DeviceIdType.MESH
ANY = pl.BlockSpec(memory_space=pl.ANY)


def _params(n_axes, vmem_limit=VMEM_LIMIT):
    return pltpu.CompilerParams(dimension_semantics=("arbitrary",) * n_axes, vmem_limit_bytes=vmem_limit)


def _tile(n, cap, mult=8):
    for t in range(min(cap, n), 0, -1):
        if n % t == 0 and t % mult == 0:
            return t
    return n


def _sigmoid(x):
    return 0.5 * jnp.tanh(0.5 * x) + 0.5


def _mm(name, a_ops, b_ops, products, dims, grid, k_axis, outs, acc_shape, epilogue, extra_ops=()):
    na, nb, ne, no = len(a_ops), len(b_ops), len(extra_ops), len(outs)
    n_acc = 1 + max(c for _, _, c in products)
    nk = 1 if k_axis is None else grid[k_axis]

    def body(*refs):
        a_refs = refs[:na]
        b_refs = refs[na:na + nb]
        e_refs = refs[na + nb:na + nb + ne]
        o_refs = refs[na + nb + ne:na + nb + ne + no]
        acc_refs = refs[na + nb + ne + no:]

        def partial_sums():
            vals = [None] * n_acc
            for ai, bi, ci in products:
                d = lax.dot_general(a_refs[ai][...].astype(BF16), b_refs[bi][...].astype(BF16), dims,
                                    preferred_element_type=F32)
                vals[ci] = d if vals[ci] is None else vals[ci] + d
            return vals

        if nk == 1:
            epilogue(partial_sums(), e_refs, o_refs)
        else:
            k = pl.program_id(k_axis)

            @pl.when(k == 0)
            def _():
                for acc in acc_refs:
                    acc[...] = jnp.zeros_like(acc)

            for acc, v in zip(acc_refs, partial_sums()):
                acc[...] += v

            @pl.when(k == nk - 1)
            def _():
                epilogue([acc[...] for acc in acc_refs], e_refs, o_refs)

    ops = list(a_ops) + list(b_ops) + list(extra_ops)
    return pl.pallas_call(
        body, name=name, grid=grid,
        in_specs=[s for _, s in ops], out_specs=[s for _, s in outs], out_shape=[o for o, _ in outs],
        scratch_shapes=[pltpu.VMEM(acc_shape, F32) for _ in range(n_acc if nk > 1 else 0)],
        compiler_params=_params(len(grid)),
    )(*[a for a, _ in ops])


def _store(accs, e_refs, o_refs):
    o_refs[0][...] = accs[0].astype(o_refs[0].dtype)


def linear(name, x, w, out_dtype, resid=None, next_gain=None):
    t, k = x.shape
    n = w.shape[1]
    tm = _tile(t, 512)
    tn = n if n <= 2048 else _tile(n, 1024, 128)
    tile = pl.BlockSpec((tm, tn), lambda j, i: (i, j))
    extra = [] if resid is None else [(resid, tile)]
    outs = [(S((t, n), out_dtype), tile)]
    if next_gain is not None:
        assert tn == n
        extra.append((next_gain, pl.BlockSpec((1, n), lambda j, i: (0, 0))))
        outs.append((S((t, n), BF16), tile))

    def epilogue(accs, e_refs, o_refs):
        y = accs[0] if resid is None else e_refs[0][...] + accs[0]
        o_refs[0][...] = y.astype(out_dtype)
        if next_gain is not None:
            o_refs[1][...] = (y * _rstd(y) * e_refs[-1][...]).astype(BF16)

    res = _mm(name, [(x, pl.BlockSpec((tm, k), lambda j, i: (i, 0)))], [(w, pl.BlockSpec((k, tn), lambda j, i: (0, j)))],
              [(0, 0, 0)], _NN, (n // tn, t // tm), None, outs, None, epilogue, extra)
    return res[0] if next_gain is None else res


def linear_nt(name, dy, w, out_dtype):
    t, n = dy.shape
    k = w.shape[0]
    tm = _tile(t, 512)
    tc = n if n <= 2048 else _tile(n, 1024, 128)
    return _mm(name, [(dy, pl.BlockSpec((tm, tc), lambda i, c: (i, c)))], [(w, pl.BlockSpec((k, tc), lambda i, c: (0, c)))],
               [(0, 0, 0)], _NT, (t // tm, n // tc), 1,
               [(S((t, k), out_dtype), pl.BlockSpec((tm, k), lambda i, c: (i, 0)))], (tm, k), _store)[0]


def wgrad(name, x, dy):
    t, k = x.shape
    n = dy.shape[1]
    tk = _tile(t, 512)
    tn = n if n <= 1024 else _tile(n, 1024, 128)
    return _mm(name, [(x, pl.BlockSpec((tk, k), lambda j, s: (s, 0)))], [(dy, pl.BlockSpec((tk, tn), lambda j, s: (s, j)))],
               [(0, 0, 0)], _TN, (n // tn, t // tk), 1,
               [(S((k, n), BF16), pl.BlockSpec((k, tn), lambda j, s: (0, j)))], (k, tn), _store)[0]


def _resident(shape, index_map):
    return pl.BlockSpec(shape, index_map, pipeline_mode=pl.Buffered(1))


def ffn_up(name, a, wg_all, wu_all, layer):
    t, d = a.shape
    f4 = wg_all.shape[2]
    tm = _tile(t, 512)
    w_spec = _resident((N_CHIPS, d, f4), lambda i: (0, layer, 0))
    h_spec = pl.BlockSpec((N_CHIPS, tm, f4), lambda i: (0, i, 0))

    def body(a_ref, wg_ref, wu_ref, zg_ref, zu_ref, z_ref):
        av = a_ref[...]
        for k in range(N_CHIPS):
            g = jnp.dot(av, wg_ref[k], preferred_element_type=F32)
            u = jnp.dot(av, wu_ref[k], preferred_element_type=F32)
            sg = _sigmoid(g)
            silu = g * sg
            zg_ref[k] = (u * (sg * (1.0 + g * (1.0 - sg)))).astype(BF16)
            zu_ref[k] = silu.astype(BF16)
            z_ref[k] = (silu * u).astype(BF16)

    return pl.pallas_call(
        body, name=name, grid=(t // tm,), in_specs=[pl.BlockSpec((tm, d), lambda i: (i, 0)), w_spec, w_spec],
        out_specs=[h_spec] * 3, out_shape=[S((N_CHIPS, t, f4), BF16)] * 3, compiler_params=_params(1))(a, wg_all, wu_all)


def ffn_down(name, z, wd_all, layer, resid, next_gain=None):
    _, t, f4 = z.shape
    d = wd_all.shape[2]
    tm = _tile(t, 512)
    row = pl.BlockSpec((tm, d), lambda i: (i, 0))
    normed = next_gain is not None

    def body(z_ref, wd_ref, r_ref, *refs):
        acc = r_ref[...]
        for k in range(N_CHIPS):
            acc = acc + jnp.dot(z_ref[k], wd_ref[k], preferred_element_type=F32)
        refs[-2 if normed else -1][...] = acc
        if normed:
            refs[-1][...] = (acc * _rstd(acc) * refs[0][...]).astype(BF16)

    res = pl.pallas_call(
        body, name=name, grid=(t // tm,),
        in_specs=[pl.BlockSpec((N_CHIPS, tm, f4), lambda i: (0, i, 0)), _resident((N_CHIPS, f4, d), lambda i: (0, layer, 0)), row]
        + ([pl.BlockSpec((1, d), lambda i: (0, 0))] if normed else []),
        out_specs=[row] * (2 if normed else 1), out_shape=[S((t, d), F32)] + ([S((t, d), BF16)] if normed else []),
        compiler_params=_params(1))(z, wd_all, resid, *([next_gain] if normed else []))
    return res if normed else res[0]


def ffn_bwd_hidden(name, dh, wd_all, layer, zg, zu, swap=()):
    t, d = dh.shape
    f4 = zg.shape[2]
    tm = _tile(t, 512)
    h_spec = pl.BlockSpec((N_CHIPS, tm, f4), lambda i: (0, i, 0))
    n = len(swap)

    def body(dh_ref, wd_ref, zg_ref, zu_ref, *refs):
        dg_ref, du_ref = refs[n:n + 2]
        if n:
            _ride_along(swap_copies(refs[:n], refs[n + 2:2 * n + 2], *refs[2 * n + 2:]), (pl.program_id(0),), (t // tm,))
        dhb = dh_ref[...].astype(BF16)
        for k in range(N_CHIPS):
            dz = lax.dot_general(dhb, wd_ref[k], _NT, preferred_element_type=F32)
            dg_ref[k] = (dz * zg_ref[k].astype(F32)).astype(BF16)
            du_ref[k] = (dz * zu_ref[k].astype(F32)).astype(BF16)

    outs = pl.pallas_call(
        body, name=name, grid=(t // tm,),
        in_specs=[pl.BlockSpec((tm, d), lambda i: (i, 0)), _resident((N_CHIPS, f4, d), lambda i: (0, layer, 0)), h_spec, h_spec]
        + [ANY] * n,
        out_specs=[h_spec] * 2 + [ANY] * n, out_shape=[S((N_CHIPS, t, f4), BF16)] * 2 + _swap_shapes(swap),
        scratch_shapes=[pltpu.SemaphoreType.DMA((n,)), pltpu.SemaphoreType.DMA((n,))] if n else [],
        compiler_params=_params(1))(dh, wd_all, zg, zu, *swap)
    return outs[0], outs[1], list(outs[2:])


def _norm_bwd_specs(tm, d):
    row = pl.BlockSpec((tm, d), lambda i: (i, 0))
    vec = pl.BlockSpec((1, d), lambda i: (0, 0))
    return [row, vec, row], [row, vec]


def _norm_bwd_tail(da, h_ref, g_ref, dhi_ref, dho_ref, dgain_ref):
    dx, dgain = _rms_bwd(h_ref[...], g_ref[...], da)
    dho_ref[...] = dhi_ref[...] + dx

    @pl.when(pl.program_id(0) == 0)
    def _():
        dgain_ref[...] = jnp.zeros_like(dgain_ref)

    dgain_ref[...] += dgain


def ffn_bwd_input(name, dg, du, wg_all, wu_all, layer, h, gain, dh_in):
    _, t, f4 = dg.shape
    d = h.shape[1]
    tm = _tile(t, 512)
    h_spec = pl.BlockSpec((N_CHIPS, tm, f4), lambda i: (0, i, 0))
    w_spec = _resident((N_CHIPS, d, f4), lambda i: (0, layer, 0))
    tail_in, tail_out = _norm_bwd_specs(tm, d)

    def body(dg_ref, du_ref, wg_ref, wu_ref, *tail):
        acc = jnp.zeros((tm, d), F32)
        for k in range(N_CHIPS):
            acc = acc + lax.dot_general(dg_ref[k], wg_ref[k], _NT, preferred_element_type=F32)
            acc = acc + lax.dot_general(du_ref[k], wu_ref[k], _NT, preferred_element_type=F32)
        _norm_bwd_tail(acc, *tail)

    return pl.pallas_call(
        body, name=name, grid=(t // tm,), in_specs=[h_spec, h_spec, w_spec, w_spec] + tail_in, out_specs=tail_out,
        out_shape=[S((t, d), F32), S((1, d), F32)], compiler_params=_params(1))(dg, du, wg_all, wu_all, h, gain, dh_in)


def ffn_wgrad_up(name, a, dy):
    t, d = a.shape
    f4 = dy.shape[2]
    tk = _tile(t, 512)
    nt = t // tk

    def body(a_ref, dy_ref, o_ref, acc):
        s = pl.program_id(0)

        @pl.when(s == 0)
        def _():
            acc[...] = jnp.zeros_like(acc)

        at = a_ref[...].T
        for k in range(N_CHIPS):
            acc[k] += jnp.dot(at, dy_ref[k], preferred_element_type=F32)

        @pl.when(s == nt - 1)
        def _():
            o_ref[...] = acc[...].astype(BF16)

    return pl.pallas_call(
        body, name=name, grid=(nt,),
        in_specs=[pl.BlockSpec((tk, d), lambda s: (s, 0)), pl.BlockSpec((N_CHIPS, tk, f4), lambda s: (0, s, 0))],
        out_specs=pl.BlockSpec((N_CHIPS, d, f4), lambda s: (0, 0, 0)), out_shape=S((N_CHIPS, d, f4), BF16),
        scratch_shapes=[pltpu.VMEM((N_CHIPS, d, f4), F32)], compiler_params=_params(1))(a, dy)


def ffn_wgrad_down(name, z, dh):
    _, t, f4 = z.shape
    d = dh.shape[1]
    tk = _tile(t, 512)
    nt = t // tk

    def body(z_ref, dh_ref, o_ref, acc):
        s = pl.program_id(0)

        @pl.when(s == 0)
        def _():
            acc[...] = jnp.zeros_like(acc)

        dhb = dh_ref[...].astype(BF16)
        for k in range(N_CHIPS):
            acc[k] += lax.dot_general(z_ref[k], dhb, _TN, preferred_element_type=F32)

        @pl.when(s == nt - 1)
        def _():
            o_ref[...] = acc[...].astype(BF16)

    return pl.pallas_call(
        body, name=name, grid=(nt,),
        in_specs=[pl.BlockSpec((N_CHIPS, tk, f4), lambda s: (0, s, 0)), pl.BlockSpec((tk, d), lambda s: (s, 0))],
        out_specs=pl.BlockSpec((N_CHIPS, f4, d), lambda s: (0, 0, 0)), out_shape=S((N_CHIPS, f4, d), BF16),
        scratch_shapes=[pltpu.VMEM((N_CHIPS, f4, d), F32)], compiler_params=_params(1))(z, dh)


def conv_in_proj(name, a, w, conv_w):
    t, d = a.shape
    tm = _tile(t, 256)
    keep = 8

    def body(a_ref, w_ref, cw_ref, bcx_ref, y_ref, u_ref):
        @pl.when(pl.program_id(0) == 0)
        def _():
            u_ref[0:keep, :] = jnp.zeros((keep, d), F32)

        av = a_ref[...]
        b, c, x = [jnp.dot(av, w_ref[:, j * d:(j + 1) * d], preferred_element_type=F32) for j in range(3)]
        for j, part in enumerate((b, c, x)):
            bcx_ref[j] = part.astype(bcx_ref.dtype)
        u_ref[keep:keep + tm, :] = c * x
        uc = (cw_ref[0:1, :] * u_ref[keep - 2:keep - 2 + tm, :] + cw_ref[1:2, :] * u_ref[keep - 1:keep - 1 + tm, :]
              + cw_ref[2:3, :] * u_ref[keep:keep + tm, :])
        y_ref[...] = (b * uc).astype(BF16)
        u_ref[0:keep, :] = u_ref[tm:tm + keep, :]

    return pl.pallas_call(
        body, name=name, grid=(t // tm,),
        in_specs=[pl.BlockSpec((tm, d), lambda i: (i, 0)), _resident((d, 3 * d), lambda i: (0, 0)), pl.BlockSpec((3, d), lambda i: (0, 0))],
        out_specs=[pl.BlockSpec((3, tm, d), lambda i: (0, i, 0)), pl.BlockSpec((tm, d), lambda i: (i, 0))],
        out_shape=[S((3, t, d), CONV_SAVED_DTYPE), S((t, d), BF16)], scratch_shapes=[pltpu.VMEM((tm + keep, d), F32)],
        compiler_params=_params(1))(a, w, conv_w)


def conv_in_bwd_input(name, dbcx, w, h, gain, dh_in):
    _, t, d = dbcx.shape
    tm = _tile(t, 512)
    tail_in, tail_out = _norm_bwd_specs(tm, d)

    def body(g_ref, w_ref, *tail):
        acc = jnp.zeros((tm, d), F32)
        for j in range(3):
            acc = acc + lax.dot_general(g_ref[j], w_ref[:, j * d:(j + 1) * d], _NT, preferred_element_type=F32)
        _norm_bwd_tail(acc, *tail)

    return pl.pallas_call(
        body, name=name, grid=(t // tm,),
        in_specs=[pl.BlockSpec((3, tm, d), lambda i: (0, i, 0)), _resident((d, 3 * d), lambda i: (0, 0))] + tail_in,
        out_specs=tail_out, out_shape=[S((t, d), F32), S((1, d), F32)], compiler_params=_params(1))(dbcx, w, h, gain, dh_in)


def linear_nt_norm_bwd(name, dy, w, h, gain, dh_in, parts=()):
    t, n = dy.shape
    k = w.shape[0]
    tm = _tile(t, 512)
    tail_in, tail_out = _norm_bwd_specs(tm, k)
    m = len(parts)

    def body(dy_ref, w_ref, h_ref, g_ref, dhi_ref, *refs):
        if m:
            _ride_along(scatter_ici_copies(refs[:m], refs[m + 2:2 * m + 2], *refs[2 * m + 2:]), (pl.program_id(0),), (t // tm,))
        da = lax.dot_general(dy_ref[...].astype(BF16), w_ref[...], _NT, preferred_element_type=F32)
        _norm_bwd_tail(da, h_ref, g_ref, dhi_ref, *refs[m:m + 2])

    outs = pl.pallas_call(
        body, name=name, grid=(t // tm,),
        in_specs=[pl.BlockSpec((tm, n), lambda i: (i, 0)), _resident((k, n), lambda i: (0, 0))] + tail_in + [ANY] * m,
        out_specs=tail_out + [ANY] * m, out_shape=[S((t, k), F32), S((1, k), F32)] + [S(p.shape, p.dtype) for p in parts],
        scratch_shapes=[pltpu.SemaphoreType.DMA((m, 3)), pltpu.SemaphoreType.DMA((m, 3))] if m else [],
        compiler_params=_params(1))(dy, w, h, gain, dh_in, *parts)
    return outs[0], outs[1], list(outs[2:])


def conv_in_wgrad(name, a, dbcx):
    t, d = a.shape
    tk = _tile(t, 512)
    nt = t // tk
    n4 = 3 * d // N_CHIPS

    def body(a_ref, g_ref, o_ref, acc):
        s = pl.program_id(0)

        @pl.when(s == 0)
        def _():
            acc[...] = jnp.zeros_like(acc)

        at = a_ref[...].T
        for j in range(3):
            acc[:, j * d:(j + 1) * d] += jnp.dot(at, g_ref[j], preferred_element_type=F32)

        @pl.when(s == nt - 1)
        def _():
            for k in range(N_CHIPS):
                o_ref[k] = acc[:, k * n4:(k + 1) * n4].astype(BF16)

    return pl.pallas_call(
        body, name=name, grid=(nt,),
        in_specs=[pl.BlockSpec((tk, d), lambda s: (s, 0)), pl.BlockSpec((3, tk, d), lambda s: (0, s, 0))],
        out_specs=pl.BlockSpec((N_CHIPS, d, n4), lambda s: (0, 0, 0)), out_shape=S((N_CHIPS, d, n4), BF16),
        scratch_shapes=[pltpu.VMEM((d, 3 * d), F32)], compiler_params=_params(1))(a, dbcx)


def _rstd(x):
    return lax.rsqrt(jnp.mean(x * x, axis=-1, keepdims=True) + RMS_EPS)


def _rms_bwd(x, g, dy):
    r = _rstd(x)
    xhat = x * r
    dgain = jnp.sum(dy * xhat, axis=0, keepdims=True)
    dxh = dy * g
    dx = r * (dxh - xhat * jnp.mean(dxh * xhat, axis=-1, keepdims=True))
    return dx, dgain


def ffn_down_loss(name, z, wd_all, layer, resid, gain, target):
    _, t, f4 = z.shape
    d = wd_all.shape[2]
    tm = _tile(t, 512)

    def body(z_ref, wd_ref, r_ref, g_ref, t_ref, dh_ref, dg_ref, loss_ref):
        x = r_ref[...]
        for k in range(N_CHIPS):
            x = x + jnp.dot(z_ref[k], wd_ref[k], preferred_element_type=F32)
        g = g_ref[...]
        r = _rstd(x)
        xhat = x * r
        err = xhat * g - t_ref[...]
        dy = err * (1.0 / d)
        dxh = dy * g
        dh_ref[...] = r * (dxh - xhat * jnp.mean(dxh * xhat, axis=-1, keepdims=True))

        @pl.when(pl.program_id(0) == 0)
        def _():
            dg_ref[...] = jnp.zeros_like(dg_ref)
            loss_ref[...] = jnp.zeros_like(loss_ref)

        dg_ref[...] += jnp.sum(dy * xhat, axis=0, keepdims=True)
        per_token = jnp.mean(err * err, axis=-1, keepdims=True)
        loss_ref[...] += 0.5 * jnp.sum(per_token, axis=0, keepdims=True)

    row = pl.BlockSpec((tm, d), lambda i: (i, 0))
    vec = pl.BlockSpec((1, d), lambda i: (0, 0))
    one = pl.BlockSpec((1, 1), lambda i: (0, 0))
    return pl.pallas_call(
        body, name=name, grid=(t // tm,),
        in_specs=[pl.BlockSpec((N_CHIPS, tm, f4), lambda i: (0, i, 0)), _resident((N_CHIPS, f4, d), lambda i: (0, layer, 0)), row, vec, row],
        out_specs=[row, vec, one], out_shape=[S((t, d), F32), S((1, d), F32), S((1, 1), F32)],
        compiler_params=_params(1))(z, wd_all, resid, gain, target)


def mla_in_proj(name, a, w, g_cq, g_ckv, cos, sin):
    t, d = a.shape
    n = w.shape[1]
    ql, kl = g_cq.shape[1], g_ckv.shape[1]
    tr = _tile(t, 512)

    def body(a_ref, w_ref, gq_ref, gk_ref, c_ref, s_ref, p_ref, cq_ref, ckv_ref, kr_ref):
        p_ref[...] = jnp.dot(a_ref[...], w_ref[...], preferred_element_type=F32)
        xq = p_ref[:, 0:ql]
        cq_ref[...] = (xq * _rstd(xq) * gq_ref[...]).astype(BF16)
        xk = p_ref[:, ql:ql + kl]
        ckv_ref[...] = (xk * _rstd(xk) * gk_ref[...]).astype(BF16)
        k1 = p_ref[:, ql + kl:ql + kl + HALF]
        k2 = p_ref[:, ql + kl + HALF:ql + kl + ROPE]
        c = c_ref[...]
        s = s_ref[...]
        kr_ref[:, 0:HALF] = k1 * c - k2 * s
        kr_ref[:, HALF:ROPE] = k1 * s + k2 * c

    def row(w):
        return pl.BlockSpec((tr, w), lambda i: (i, 0))

    def vec(w):
        return pl.BlockSpec((1, w), lambda i: (0, 0))

    return pl.pallas_call(
        body, name=name, grid=(t // tr,),
        in_specs=[row(d), _resident((d, n), lambda i: (0, 0)), vec(ql), vec(kl), row(HALF), row(HALF)],
        out_specs=[row(n), row(ql), row(kl), row(ROPE)],
        out_shape=[S((t, n), F32), S((t, ql), BF16), S((t, kl), BF16), S((t, ROPE), F32)],
        compiler_params=_params(1))(a, w, g_cq, g_ckv, cos, sin)


def qkv_heads(name, cq, ckv, w_uq, w_ukv, kr, cos, sin, shards=()):
    t = cq.shape[0]
    tr = _tile(t, 256)
    n = len(shards)

    def body(cq_ref, ckv_ref, wq_ref, wkv_ref, kr_ref, c_ref, s_ref, *refs):
        src = refs[:n]
        qo_ref, ko_ref, vo_ref = refs[n:n + 3]
        q_ref, kv_ref = refs[2 * n + 3:2 * n + 5]
        if n:
            _ride_along(gather_ici_copies(src, refs[n + 3:2 * n + 3], *refs[2 * n + 5:]), (pl.program_id(0),), (t // tr,))
        q_ref[...] = jnp.dot(cq_ref[...], wq_ref[...], preferred_element_type=F32)
        kv_ref[...] = jnp.dot(ckv_ref[...], wkv_ref[...], preferred_element_type=F32).astype(BF16)
        c = c_ref[...]
        s = s_ref[...]
        krb = kr_ref[...].astype(BF16)
        for h in range(N_HEADS):
            q0 = h * QK
            qo_ref[h, :, 0:NOPE] = q_ref[:, q0:q0 + NOPE].astype(BF16)
            q1 = q_ref[:, q0 + NOPE:q0 + NOPE + HALF]
            q2 = q_ref[:, q0 + NOPE + HALF:q0 + QK]
            qo_ref[h, :, NOPE:NOPE + HALF] = (q1 * c - q2 * s).astype(BF16)
            qo_ref[h, :, NOPE + HALF:QK] = (q1 * s + q2 * c).astype(BF16)
            k0 = h * (NOPE + VDIM)
            ko_ref[h, :, 0:NOPE] = kv_ref[:, k0:k0 + NOPE]
            ko_ref[h, :, NOPE:QK] = krb
            vo_ref[h] = kv_ref[:, k0 + NOPE:k0 + NOPE + VDIM]

    def row(w):
        return pl.BlockSpec((tr, w), lambda i: (i, 0))

    def heads(w):
        return pl.BlockSpec((N_HEADS, tr, w), lambda i: (0, i, 0))

    outs = pl.pallas_call(
        body, name=name, grid=(t // tr,),
        in_specs=[row(cq.shape[1]), row(ckv.shape[1]), _resident(w_uq.shape, lambda i: (0, 0)), _resident(w_ukv.shape, lambda i: (0, 0)),
                  row(ROPE), row(HALF), row(HALF)] + [ANY] * n,
        out_specs=[heads(QK), heads(QK), heads(VDIM)] + [ANY] * n,
        out_shape=[S((N_HEADS, t, QK), BF16), S((N_HEADS, t, QK), BF16), S((N_HEADS, t, VDIM), BF16)]
        + [S((N_CHIPS,) + s.shape, s.dtype) for s in shards],
        scratch_shapes=[pltpu.VMEM((tr, N_HEADS * QK), F32), pltpu.VMEM((tr, N_HEADS * (NOPE + VDIM)), BF16)]
        + ([pltpu.SemaphoreType.DMA((n, 3)), pltpu.SemaphoreType.DMA((n, 3))] if n else []),
        compiler_params=_params(1))(cq, ckv, w_uq, w_ukv, kr, cos, sin, *shards)
    return outs[0], outs[1], outs[2], list(outs[3:])


def qkv_heads_bwd(name, dq_h, dk_h, dv_h, cos, sin, w_uq, w_ukv, proj, g_cq, g_ckv, halves=(), targets=(), where=()):
    t = dq_h.shape[1]
    tr = _tile(t, 256)
    n, nt = len(halves), len(targets)
    ql, kl = g_cq.shape[1], g_ckv.shape[1]
    width = proj.shape[1]

    def body(dq_ref, dk_ref, dv_ref, c_ref, s_ref, wq_ref, wkv_ref, p_ref, gq_ref, gk_ref, *refs):
        q_ref, kv_ref, dp_ref, dgq_ref, dgk_ref = refs[n:n + 5]
        kr_ref = refs[n + 5 + nt]
        if n:
            src, dst = refs[:n], refs[n + 5:n + 5 + nt]
            stages = refs[n + 6 + nt:2 * n + 6 + nt]
            send_sems, recv_sems, local_sems = refs[2 * n + 6 + nt:]
            core = lax.axis_index("c")
            own = [(src[i], stages[i],
                    dst[where[i][0]].at[pl.ds(pl.multiple_of(where[i][1] + core * src[i].shape[0], 8), src[i].shape[0])],
                    local_sems.at[i]) for i in range(n)]
            _ride_along(join_copies(src, dst, where, send_sems, recv_sems), (pl.program_id(0),), (t // tr,), own)
        c = c_ref[...]
        s = s_ref[...]
        dkr = jnp.zeros((tr, ROPE), F32)
        for h in range(N_HEADS):
            q0 = h * QK
            q_ref[:, q0:q0 + NOPE] = dq_ref[h, :, 0:NOPE].astype(BF16)
            d1 = dq_ref[h, :, NOPE:NOPE + HALF]
            d2 = dq_ref[h, :, NOPE + HALF:QK]
            q_ref[:, q0 + NOPE:q0 + NOPE + HALF] = (d1 * c + d2 * s).astype(BF16)
            q_ref[:, q0 + NOPE + HALF:q0 + QK] = (d2 * c - d1 * s).astype(BF16)
            k0 = h * (NOPE + VDIM)
            kv_ref[:, k0:k0 + NOPE] = dk_ref[h, :, 0:NOPE].astype(BF16)
            kv_ref[:, k0 + NOPE:k0 + NOPE + VDIM] = dv_ref[h].astype(BF16)
            dkr = dkr + dk_ref[h, :, NOPE:QK]
        kr_ref[...] = dkr
        dcq = lax.dot_general(q_ref[...], wq_ref[...], _NT, preferred_element_type=F32)
        dckv = lax.dot_general(kv_ref[...], wkv_ref[...], _NT, preferred_element_type=F32)
        dxq, dgq = _rms_bwd(p_ref[:, 0:ql], gq_ref[...], dcq)
        dp_ref[:, 0:ql] = dxq.astype(BF16)
        dxk, dgk = _rms_bwd(p_ref[:, ql:ql + kl], gk_ref[...], dckv)
        dp_ref[:, ql:ql + kl] = dxk.astype(BF16)
        r1 = kr_ref[:, 0:HALF]
        r2 = kr_ref[:, HALF:ROPE]
        dp_ref[:, ql + kl:ql + kl + HALF] = (r1 * c + r2 * s).astype(BF16)
        dp_ref[:, ql + kl + HALF:ql + kl + ROPE] = (r2 * c - r1 * s).astype(BF16)

        @pl.when(pl.program_id(0) == 0)
        def _():
            dgq_ref[...] = jnp.zeros_like(dgq_ref)
            dgk_ref[...] = jnp.zeros_like(dgk_ref)

        dgq_ref[...] += dgq
        dgk_ref[...] += dgk

    def row(w):
        return pl.BlockSpec((tr, w), lambda i: (i, 0))

    def vec(w):
        return pl.BlockSpec((1, w), lambda i: (0, 0))

    def heads(w):
        return pl.BlockSpec((N_HEADS, tr, w), lambda i: (0, i, 0))

    outs = pl.pallas_call(
        body, name=name, grid=(t // tr,),
        in_specs=[heads(QK), heads(QK), heads(VDIM), row(HALF), row(HALF), _resident(w_uq.shape, lambda i: (0, 0)),
                  _resident(w_ukv.shape, lambda i: (0, 0)), row(width), vec(ql), vec(kl)] + [ANY] * n,
        out_specs=[row(N_HEADS * QK), row(N_HEADS * (NOPE + VDIM)), row(width), vec(ql), vec(kl)] + [ANY] * nt,
        out_shape=[S((t, N_HEADS * QK), BF16), S((t, N_HEADS * (NOPE + VDIM)), BF16), S((t, width), BF16),
                   S((1, ql), F32), S((1, kl), F32)] + [S(tg, F32) for tg in targets],
        scratch_shapes=[pltpu.VMEM((tr, ROPE), F32)] + [pltpu.VMEM(h.shape, h.dtype) for h in halves]
        + ([pltpu.SemaphoreType.DMA((n,)), pltpu.SemaphoreType.DMA((n,)), pltpu.SemaphoreType.DMA((n, 2))] if n else []),
        compiler_params=_params(1))(dq_h, dk_h, dv_h, cos, sin, w_uq, w_ukv, proj, g_cq, g_ckv, *halves)
    return outs[0], outs[1], outs[2], outs[3], outs[4], list(outs[5:])


def _chunk_mask_t(q_start, k_start, bq, bk):
    kc = (k_start + lax.broadcasted_iota(jnp.int32, (bk, bq), 0)) // CHUNK
    qc = (q_start + lax.broadcasted_iota(jnp.int32, (bk, bq), 1)) // CHUNK
    return kc <= qc


def attention_fwd(name, q, k, v, shards=()):
    nh, t, _ = q.shape
    blk = ATT_BLOCK
    nq = t // blk
    n = len(shards)

    def body(q_ref, k_ref, v_ref, *refs):
        src = refs[:n]
        o_ref, lse_ref = refs[n:n + 2]
        dst = refs[n + 2:2 * n + 2]
        m_ref, l_ref, acc_ref, s_buf, p_buf, alpha_buf, bias_ref = refs[2 * n + 2:2 * n + 9]
        i = pl.program_id(1)
        if n:
            send_sems, recv_sems = refs[2 * n + 9:]
            _ride_along(gather_ici_copies(src, dst, send_sems, recv_sems), (pl.program_id(0), i), (nh, nq))

        @pl.when((pl.program_id(0) == 0) & (i == 0))
        def _():
            bias_ref[...] = jnp.where(_chunk_mask_t(0, 0, blk, blk), 0.0, MASK_VALUE)

        m_ref[...] = jnp.full_like(m_ref, MASK_VALUE)
        l_ref[...] = jnp.zeros_like(l_ref)
        acc_ref[...] = jnp.zeros_like(acc_ref)

        def rows(b):
            return pl.ds(pl.multiple_of(b * blk, blk), blk)

        def scores(b, slot):
            s_buf[slot] = lax.dot_general(k_ref[rows(b), :], q_ref[...], _NT, preferred_element_type=F32)

        def softmax(slot, diagonal):
            s = s_buf[slot]
            if diagonal:
                s = s + bias_ref[...]
            m_old = m_ref[...]
            m_new = jnp.maximum(m_old, jnp.max(s, axis=0, keepdims=True))
            p = jnp.exp2((s - m_new) * SCORE_SCALE_LOG2)
            alpha = jnp.exp2((m_old - m_new) * SCORE_SCALE_LOG2)
            l_ref[...] = alpha * l_ref[...] + jnp.sum(p, axis=0, keepdims=True)
            m_ref[...] = m_new
            alpha_buf[slot] = alpha
            p_buf[slot] = p.astype(BF16)

        def values(b, slot):
            pv = lax.dot_general(v_ref[rows(b), :], p_buf[slot], _TN, preferred_element_type=F32)
            acc_ref[...] = alpha_buf[slot] * acc_ref[...] + pv

        def step(t, slot):
            values(t - 2, slot)
            softmax(1 - slot, False)
            scores(t, slot)

        scores(0, 0)

        @pl.when(i == 0)
        def _():
            softmax(0, True)
            values(0, 0)

        @pl.when(i > 0)
        def _():
            scores(1, 1)
            softmax(0, False)
            steady = i - 1

            def pair(u, carry):
                step(2 + 2 * u, 0)
                step(3 + 2 * u, 1)
                return carry

            lax.fori_loop(0, steady // 2, pair, 0)

            @pl.when(steady % 2 == 1)
            def _():
                step(i, 0)

            last = i % 2
            softmax(last, True)
            values(i - 1, 1 - last)
            values(i, last)

        l = l_ref[...]
        o_ref[...] = (acc_ref[...] / l).T
        lse_ref[...] = m_ref[...] * SCORE_SCALE + jnp.log(l)

    outs = pl.pallas_call(
        body, name=name, grid=(nh, nq),
        in_specs=[pl.BlockSpec((None, blk, QK), lambda h, i: (h, i, 0)), pl.BlockSpec((None, t, QK), lambda h, i: (h, 0, 0)),
                  pl.BlockSpec((None, t, VDIM), lambda h, i: (h, 0, 0))] + [ANY] * n,
        out_specs=[pl.BlockSpec((blk, VDIM), lambda h, i: (i, h)),
                   pl.BlockSpec((None, None, 1, blk), lambda h, i: (h, i, 0, 0))] + [ANY] * n,
        out_shape=[S((t, nh * VDIM), F32), S((nh, nq, 1, blk), F32)] + [S((N_CHIPS,) + s.shape, s.dtype) for s in shards],
        scratch_shapes=[pltpu.VMEM((1, blk), F32), pltpu.VMEM((1, blk), F32), pltpu.VMEM((VDIM, blk), F32),
                        pltpu.VMEM((2, blk, blk), F32), pltpu.VMEM((2, blk, blk), BF16), pltpu.VMEM((2, 1, blk), F32),
                        pltpu.VMEM((blk, blk), F32)]
        + ([pltpu.SemaphoreType.DMA((n, 3)), pltpu.SemaphoreType.DMA((n, 3))] if n else []),
        compiler_params=_params(2))(q, k, v, *shards)
    return outs[0], outs[1], list(outs[2:])


def attention_out_bwd(name, dh, w_o, o, swap=()):
    t, d = dh.shape
    n = w_o.shape[0]
    blk = ATT_BLOCK
    m = len(swap)

    def body(dh_ref, w_ref, o_ref, *refs):
        do_ref, d_ref = refs[m:m + 2]
        if m:
            _ride_along(swap_copies(refs[:m], refs[m + 2:2 * m + 2], *refs[2 * m + 2:]), (pl.program_id(0),), (t // blk,))
        do_ref[...] = lax.dot_general(dh_ref[...].astype(BF16), w_ref[...], _NT, preferred_element_type=F32)
        for h in range(N_HEADS):
            cols = slice(h * VDIM, (h + 1) * VDIM)
            d_ref[h] = jnp.sum((do_ref[:, cols] * o_ref[:, cols]).T, axis=0, keepdims=True)

    tile = pl.BlockSpec((blk, n), lambda i: (i, 0))
    outs = pl.pallas_call(
        body, name=name, grid=(t // blk,),
        in_specs=[pl.BlockSpec((blk, d), lambda i: (i, 0)), _resident((n, d), lambda i: (0, 0)), tile] + [ANY] * m,
        out_specs=[tile, pl.BlockSpec((N_HEADS, None, 1, blk), lambda i: (0, i, 0, 0))] + [ANY] * m,
        out_shape=[S((t, n), F32), S((N_HEADS, t // blk, 1, blk), F32)] + _swap_shapes(swap),
        scratch_shapes=[pltpu.SemaphoreType.DMA((m,)), pltpu.SemaphoreType.DMA((m,))] if m else [],
        compiler_params=_params(1))(dh, w_o, o, *swap)
    return outs[0], outs[1], list(outs[2:])


def attention_bwd(name, q, k, v, do, lse, delta, parts=()):
    nh, t, _ = q.shape
    blk = ATT_BLOCK
    nq = t // blk
    n_pairs = nq * (nq + 1) // 2
    n = len(parts)
    scale = SCORE_SCALE

    def body(q_ref, k_ref, v_ref, do_ref, lse_ref, dl_ref, *refs):
        src = refs[:n]
        dq_out, dk_out, dv_out = refs[n:n + 3]
        dst = refs[n + 3:2 * n + 3]
        s_buf, dp_buf, p_buf, ds_buf, bias_ref, dq_ref, dk_ref, dv_ref = refs[2 * n + 3:2 * n + 11]
        if n:
            send_sems, recv_sems = refs[2 * n + 11:]
            _ride_along(scatter_ici_copies(src, dst, send_sems, recv_sems), (pl.program_id(0),), (nh,))

        @pl.when(pl.program_id(0) == 0)
        def _():
            bias_ref[...] = jnp.where(_chunk_mask_t(0, 0, blk, blk), 0.0, MASK_VALUE)

        dq_ref[...] = jnp.zeros_like(dq_ref)
        dk_ref[...] = jnp.zeros_like(dk_ref)
        dv_ref[...] = jnp.zeros_like(dv_ref)

        def rows(x):
            return pl.ds(pl.multiple_of(x * blk, blk), blk)

        def after(jb):
            j, b = jb
            wrap = b == nq - 1 - j
            return jnp.where(wrap, j + 1, j), jnp.where(wrap, 0, b + 1)

        def products(jb, slot):
            j, b = jb
            s_buf[slot] = lax.dot_general(k_ref[rows(j), :], q_ref[rows(j + b), :], _NT, preferred_element_type=F32)
            dp_buf[slot] = lax.dot_general(v_ref[rows(j), :], do_ref[rows(j + b), :].astype(BF16), _NT, preferred_element_type=F32)

        def softmax_bwd(jb, slot):
            j, b = jb
            s = s_buf[slot] + bias_ref[...] * (b == 0).astype(F32)
            p = jnp.exp2(s * SCORE_SCALE_LOG2 - lse_ref[j + b] * LOG2_E)
            p_buf[slot] = p.astype(BF16)
            ds_buf[slot] = (p * (dp_buf[slot] - dl_ref[j + b]) * scale).astype(BF16)

        def gradients(jb, slot):
            j, b = jb
            dv_ref[rows(j), :] += jnp.dot(p_buf[slot], do_ref[rows(j + b), :].astype(BF16), preferred_element_type=F32)
            dk_ref[rows(j), :] += jnp.dot(ds_buf[slot], q_ref[rows(j + b), :], preferred_element_type=F32)
            dq_ref[rows(j + b), :] += lax.dot_general(ds_buf[slot], k_ref[rows(j), :], _TN, preferred_element_type=F32)

        def step(state, slot):
            third, second, first = state
            gradients(third, slot)
            softmax_bwd(second, 1 - slot)
            products(first, slot)
            return second, first, after(first)

        zero = jnp.int32(0)
        pair0 = (zero, zero)
        products(pair0, 0)
        if n_pairs == 1:
            softmax_bwd(pair0, 0)
            gradients(pair0, 0)
        else:
            pair1 = after(pair0)
            products(pair1, 1)
            softmax_bwd(pair0, 0)
            steady = n_pairs - 2
            state = lax.fori_loop(0, steady // 2, lambda u, st: step(step(st, 0), 1), (pair0, pair1, after(pair1)))
            if steady % 2:
                state = step(state, 0)
            before_last, last_pair, _ = state
            last = (n_pairs - 1) % 2
            softmax_bwd(last_pair, last)
            gradients(before_last, 1 - last)
            gradients(last_pair, last)
        dq_out[...] = dq_ref[...].astype(BF16)
        dk_out[...] = dk_ref[...].astype(BF16)
        dv_out[...] = dv_ref[...].astype(BF16)

    head = lambda w: pl.BlockSpec((None, t, w), lambda h: (h, 0, 0))
    stats = pl.BlockSpec((None, nq, 1, blk), lambda h: (h, 0, 0, 0))
    outs = pl.pallas_call(
        body, name=name, grid=(nh,),
        in_specs=[head(QK), head(QK), head(VDIM), pl.BlockSpec((t, VDIM), lambda h: (0, h)), stats, stats] + [ANY] * n,
        out_specs=[head(QK), head(QK), head(VDIM)] + [ANY] * n,
        out_shape=[S((nh, t, QK), BF16), S((nh, t, QK), BF16), S((nh, t, VDIM), BF16)] + [S(p.shape, p.dtype) for p in parts],
        scratch_shapes=[pltpu.VMEM((2, blk, blk), F32), pltpu.VMEM((2, blk, blk), F32), pltpu.VMEM((2, blk, blk), BF16),
                        pltpu.VMEM((2, blk, blk), BF16), pltpu.VMEM((blk, blk), F32),
                        pltpu.VMEM((t, QK), F32), pltpu.VMEM((t, QK), F32), pltpu.VMEM((t, VDIM), F32)]
        + ([pltpu.SemaphoreType.DMA((n, 3)), pltpu.SemaphoreType.DMA((n, 3))] if n else []),
        compiler_params=_params(1, VMEM_LIMIT_WHOLE_HEAD))(q, k, v, do, lse, delta, *parts)
    return outs[0], outs[1], outs[2], list(outs[3:])


def _shift_down(u, s):
    rows = lax.broadcasted_iota(jnp.int32, u.shape, 0)
    return jnp.where(rows >= s, pltpu.roll(u, s, 0), 0.0)


def _shift_up(u, s):
    n = u.shape[0]
    rows = lax.broadcasted_iota(jnp.int32, u.shape, 0)
    return jnp.where(rows < n - s, pltpu.roll(u, n - s, 0), 0.0)


def _conv_specs(t, d, lanes):
    slab = lambda part: pl.BlockSpec((None, t, lanes), lambda j, part=part: (part, 0, j))
    return slab, pl.BlockSpec((3, lanes), lambda j: (0, j)), pl.BlockSpec((t, lanes), lambda j: (0, j))


def conv_bwd(name, bcx, w, dy):
    _, t, d = bcx.shape
    lanes = _tile(d, 128, 128)
    slab, w_spec, col = _conv_specs(t, d, lanes)

    def body(b_ref, c_ref, x_ref, w_ref, dy_ref, d_ref, dw_ref):
        c = c_ref[...].astype(F32)
        x = x_ref[...].astype(F32)
        dyv = dy_ref[...]
        u = c * x
        u1 = _shift_down(u, 1)
        u2 = _shift_down(u, 2)
        w0, w1, w2 = w_ref[0:1, :], w_ref[1:2, :], w_ref[2:3, :]
        d_ref[0] = (dyv * (w0 * u2 + w1 * u1 + w2 * u)).astype(BF16)
        duc = dyv * b_ref[...].astype(F32)
        dw_ref[0:1, :] = jnp.sum(duc * u2, axis=0, keepdims=True)
        dw_ref[1:2, :] = jnp.sum(duc * u1, axis=0, keepdims=True)
        dw_ref[2:3, :] = jnp.sum(duc * u, axis=0, keepdims=True)
        du = w2 * duc + w1 * _shift_up(duc, 1) + w0 * _shift_up(duc, 2)
        d_ref[1] = (du * x).astype(BF16)
        d_ref[2] = (du * c).astype(BF16)

    return pl.pallas_call(
        body, name=name, grid=(d // lanes,), in_specs=[slab(0), slab(1), slab(2), w_spec, col],
        out_specs=[pl.BlockSpec((3, t, lanes), lambda j: (0, 0, j)), w_spec], out_shape=[S((3, t, d), BF16), S((3, d), F32)],
        compiler_params=_params(1))(bcx, bcx, bcx, w, dy)


def _adamw_update(w, g, m, v):
    m_new = ADAM_B1 * m + (1.0 - ADAM_B1) * g
    v_new = ADAM_B2 * v + (1.0 - ADAM_B2) * (g * g)
    m_hat = m_new / (1.0 - ADAM_B1 ** ADAM_STEP)
    v_hat = v_new / (1.0 - ADAM_B2 ** ADAM_STEP)
    return -ADAM_LR * (m_hat / (jnp.sqrt(v_hat) + ADAM_EPS) + ADAM_WD * w), m_new, v_new


def adamw(name, w, g, m, v):
    r, c = w.shape
    tr = _tile(r, 512)

    def body(w_ref, g_ref, m_ref, v_ref, d_ref, mo_ref, vo_ref):
        d_ref[...], mo_ref[...], vo_ref[...] = _adamw_update(w_ref[...], g_ref[...], m_ref[...], v_ref[...])

    blk = pl.BlockSpec((tr, c), lambda i: (i, 0))
    return pl.pallas_call(
        body, name=name, grid=(r // tr,), in_specs=[blk] * 4, out_specs=[blk] * 3, out_shape=[S((r, c), F32)] * 3,
        compiler_params=_params(1))(w, g, m, v)


def adamw_swapped(name, wt, g, mt, vt):
    nl, c, r = wt.shape
    tr = _tile(r, 512, 128)
    nr = r // tr

    def body(w_ref, g_ref, m_ref, v_ref, go_ref, d_ref, mo_ref, vo_ref):
        gt = g_ref[...].T
        go_ref[...] = gt
        d_ref[...], mo_ref[...], vo_ref[...] = _adamw_update(w_ref[...], gt, m_ref[...], v_ref[...])

    swapped = pl.BlockSpec((None, c, tr), lambda l, i: (l, 0, i))
    return pl.pallas_call(
        body, name=name, grid=(nl, nr),
        in_specs=[swapped, pl.BlockSpec((tr, c), lambda l, i: (l * nr + i, 0)), swapped, swapped],
        out_specs=[swapped] * 4, out_shape=[S((nl, c, r), F32)] * 4, compiler_params=_params(2))(wt, g, mt, vt)


def _place():
    x, y, c = lax.axis_index("x"), lax.axis_index("y"), lax.axis_index("c")
    other_chips = [(1 - x, y), (x, 1 - y), (1 - x, 1 - y)]
    return x, y, c, other_chips


def _half(c, rows):
    return pl.ds(pl.multiple_of(c * (rows // 2), 16), rows // 2)


def gather_weight_shards(shards, small, h, gain):
    n, ns = len(shards), len(small)
    t, d = h.shape
    tr = _tile(t, 512)
    steps = t // tr

    def body(h_ref, g_ref, *refs):
        src = refs[:n]
        small_src = refs[n:n + ns]
        a_ref = refs[n + ns]
        dst = refs[n + ns + 1:2 * n + ns + 1]
        small_dst = refs[2 * n + ns + 1:2 * (n + ns) + 1]
        send_sems, recv_sems, small_send, small_recv, local_sems = refs[2 * (n + ns) + 1:2 * (n + ns) + 6]
        stages = refs[2 * (n + ns) + 6:]
        x, y, c, chips = _place()
        me = 2 * x + y
        sibling = (x, y, 1 - c)
        own = [(s_ref, stages[i], d_ref.at[me], local_sems.at[i])
               for i, (s_ref, d_ref) in enumerate(zip(list(src) + list(small_src), list(dst) + list(small_dst)))]
        small_pairs = []
        for i in range(ns):
            for j, (px, py) in enumerate(chips):
                def whole(slot):
                    return pltpu.make_async_remote_copy(
                        src_ref=small_src[i], dst_ref=small_dst[i].at[slot], send_sem=small_send.at[i, j],
                        recv_sem=small_recv.at[i, j], device_id=(px, py, c), device_id_type=MESH)
                small_pairs.append((whole(me), whole(2 * px + py)))

        def copy(i, slot, half_of, sem, to, from_input=False):
            rows = _half(half_of, src[i].shape[0])
            return pltpu.make_async_remote_copy(
                src_ref=src[i].at[rows] if from_input else dst[i].at[slot, rows], dst_ref=dst[i].at[slot, rows],
                send_sem=send_sems.at[i, sem], recv_sem=recv_sems.at[i, sem], device_id=to, device_id_type=MESH)

        over_ici = [copy(i, me, c, j, (*chip, c), from_input=True) for i in range(n) for j, chip in enumerate(chips)]
        handed_on = [copy(i, 2 * px + py, c, 3 + j, sibling) for i in range(n) for j, (px, py) in enumerate(chips)]

        @pl.when(pl.program_id(0) == 0)
        def _():
            for s_ref, stage, _, sems in own:
                pltpu.make_async_copy(s_ref, stage, sems.at[0]).start()
            for outgoing, _ in small_pairs:
                outgoing.start()
            for cp in over_ici:
                cp.start()

        @pl.when(pl.program_id(0) == steps - 1)
        def _():
            k = 0
            for i in range(n):
                for j, (px, py) in enumerate(chips):
                    copy(i, 2 * px + py, c, j, sibling).wait_recv()
                    handed_on[k].start()
                    k += 1
            for i in range(n):
                for j, (px, py) in enumerate(chips):
                    copy(i, 2 * px + py, 1 - c, 3 + j, sibling).wait_recv()
            for cp in over_ici + handed_on:
                cp.wait_send()
            for _, incoming in small_pairs:
                incoming.wait_recv()
            for outgoing, _ in small_pairs:
                outgoing.wait_send()
            _place_locally(own)

        xv = h_ref[...]
        a_ref[...] = (xv * _rstd(xv) * g_ref[...]).astype(BF16)

    everything = list(shards) + list(small)
    outs = pl.pallas_call(
        body, name="gather_weight_shards", grid=(steps,),
        in_specs=[pl.BlockSpec((tr, d), lambda i: (i, 0)), pl.BlockSpec((1, d), lambda i: (0, 0))] + [ANY] * (n + ns),
        out_specs=[pl.BlockSpec((tr, d), lambda i: (i, 0))] + [ANY] * (n + ns),
        out_shape=[S((t, d), BF16)] + [S((N_CHIPS,) + s.shape, s.dtype) for s in everything],
        scratch_shapes=[pltpu.SemaphoreType.DMA((n, 6)), pltpu.SemaphoreType.DMA((n, 6)),
                        pltpu.SemaphoreType.DMA((max(ns, 1), 3)), pltpu.SemaphoreType.DMA((max(ns, 1), 3)),
                        pltpu.SemaphoreType.DMA((n + ns, 2))] + [pltpu.VMEM(s.shape, s.dtype) for s in everything],
        compiler_params=_params(1))(h, gain, *shards, *small)
    return outs[0], list(outs[1:n + 1]), list(outs[n + 1:])


def gather_ici_copies(src, dst, send_sems, recv_sems):
    x, y, c, chips = _place()
    me = 2 * x + y
    pairs = []
    for i in range(len(src)):
        rows = _half(c, src[i].shape[0])
        for j, (px, py) in enumerate(chips):
            def copy(slot):
                return pltpu.make_async_remote_copy(
                    src_ref=src[i].at[rows], dst_ref=dst[i].at[slot, rows], send_sem=send_sems.at[i, j],
                    recv_sem=recv_sems.at[i, j], device_id=(px, py, c), device_id_type=MESH)
            pairs.append((copy(me), copy(2 * px + py)))
    return pairs


def scatter_ici_copies(src, dst, send_sems, recv_sems):
    x, y, c, chips = _place()
    me = 2 * x + y
    pairs = []
    for i in range(len(src)):
        for j, (px, py) in enumerate(chips):
            def copy(from_slot, to_slot):
                return pltpu.make_async_remote_copy(
                    src_ref=src[i].at[from_slot], dst_ref=dst[i].at[to_slot], send_sem=send_sems.at[i, j],
                    recv_sem=recv_sems.at[i, j], device_id=(px, py, c), device_id_type=MESH)
            pairs.append((copy(2 * px + py, me), copy(me, 2 * px + py)))
    return pairs


def _ride_along(pairs, grid_ids, grid_sizes, local=()):
    first = grid_ids[0] == 0
    last = grid_ids[0] == grid_sizes[0] - 1
    for g, size in zip(grid_ids[1:], grid_sizes[1:]):
        first = first & (g == 0)
        last = last & (g == size - 1)

    @pl.when(first)
    def _():
        for outgoing, _ in pairs:
            outgoing.start()
        for src, stage, _, sems in local:
            pltpu.make_async_copy(src, stage, sems.at[0]).start()

    @pl.when(last)
    def _():
        for _, incoming in pairs:
            incoming.wait_recv()
        for outgoing, _ in pairs:
            outgoing.wait_send()
        _place_locally(local)


def _place_locally(local):
    for src, stage, _, sems in local:
        pltpu.make_async_copy(src, stage, sems.at[0]).wait()
    placed = [pltpu.make_async_copy(stage, dst, sems.at[1]) for _, stage, dst, sems in local]
    for cp in placed:
        cp.start()
    for cp in placed:
        cp.wait()


def forward_copies(src, dst, send_sems, recv_sems):
    x, y, c, chips = _place()
    pairs = []
    for i in range(len(src)):
        for j, (px, py) in enumerate(chips):
            def copy(half_of):
                rows = _half(half_of, src[i].shape[1])
                return pltpu.make_async_remote_copy(
                    src_ref=src[i].at[2 * px + py, rows], dst_ref=dst[i].at[2 * px + py, rows], send_sem=send_sems.at[i, j],
                    recv_sem=recv_sems.at[i, j], device_id=(x, y, 1 - c), device_id_type=MESH)
            pairs.append((copy(c), copy(1 - c)))
    return pairs


def attention_out_proj(name, attn, w_o, resid, next_gain, arriving, own):
    t, kdim = attn.shape
    n = w_o.shape[1]
    tm = _tile(t, 512)
    m = len(arriving)

    def body(x_ref, w_ref, r_ref, g_ref, *refs):
        src, own_refs = refs[:m], refs[m:2 * m]
        h_ref, a_ref = refs[2 * m:2 * m + 2]
        dst = refs[2 * m + 2:3 * m + 2]
        stages = refs[3 * m + 2:4 * m + 2]
        send_sems, recv_sems, local_sems = refs[4 * m + 2:]
        me = 2 * lax.axis_index("x") + lax.axis_index("y")
        placed = [(own_refs[i], stages[i], dst[i].at[me], local_sems.at[i]) for i in range(m)]
        _ride_along(forward_copies(src, dst, send_sems, recv_sems), (pl.program_id(0),), (t // tm,), placed)
        y = r_ref[...] + jnp.dot(x_ref[...].astype(BF16), w_ref[...], preferred_element_type=F32)
        h_ref[...] = y
        a_ref[...] = (y * _rstd(y) * g_ref[...]).astype(BF16)

    row = pl.BlockSpec((tm, n), lambda i: (i, 0))
    outs = pl.pallas_call(
        body, name=name, grid=(t // tm,),
        in_specs=[pl.BlockSpec((tm, kdim), lambda i: (i, 0)), _resident((kdim, n), lambda i: (0, 0)), row,
                  pl.BlockSpec((1, n), lambda i: (0, 0))] + [ANY] * (2 * m),
        out_specs=[row, row] + [ANY] * m, out_shape=[S((t, n), F32), S((t, n), BF16)] + [S(g.shape, g.dtype) for g in arriving],
        input_output_aliases={4 + i: 2 + i for i in range(m)},
        scratch_shapes=[pltpu.VMEM(o.shape, o.dtype) for o in own]
        + [pltpu.SemaphoreType.DMA((m, 3)), pltpu.SemaphoreType.DMA((m, 3)), pltpu.SemaphoreType.DMA((m, 2))],
        compiler_params=_params(1))(attn, w_o, resid, next_gain, *arriving, *own)
    return outs[0], outs[1], list(outs[2:])


def swap_copies(src, dst, send_sems, recv_sems):
    x, y, c, _ = _place()
    pairs = []
    for i in range(len(src)):
        cp = pltpu.make_async_remote_copy(
            src_ref=src[i].at[:, _half(1 - c, src[i].shape[1]), :], dst_ref=dst[i], send_sem=send_sems.at[i],
            recv_sem=recv_sems.at[i], device_id=(x, y, 1 - c), device_id_type=MESH)
        pairs.append((cp, cp))
    return pairs


def _swap_shapes(grads):
    return [S((g.shape[0], g.shape[1] // 2, g.shape[2]), g.dtype) for g in grads]


def sibling_swap_halves(name, grads):
    n = len(grads)

    def body(*refs):
        pairs = swap_copies(refs[:n], refs[n:2 * n], *refs[2 * n:])
        for outgoing, _ in pairs:
            outgoing.start()
        for _, incoming in pairs:
            incoming.wait_recv()
        for outgoing, _ in pairs:
            outgoing.wait_send()

    return pl.pallas_call(
        body, name=name, in_specs=[ANY] * n, out_specs=[ANY] * n, out_shape=_swap_shapes(grads),
        scratch_shapes=[pltpu.SemaphoreType.DMA((n,)), pltpu.SemaphoreType.DMA((n,))],
    )(*grads)


def add_halves(name, g, rx):
    _, r, cdim = g.shape
    r2 = r // 2
    tr = _tile(r2, 512, 16)
    nb = r2 // tr

    def body(lo_ref, hi_ref, rx_ref, o_ref):
        mine = jnp.where(lax.axis_index("c") == 0, lo_ref[...], hi_ref[...])
        o_ref[...] = (mine.astype(F32) + rx_ref[...].astype(F32)).astype(BF16)

    half = pl.BlockSpec((None, tr, cdim), lambda k, i: (k, i, 0))
    return pl.pallas_call(
        body, name=name, grid=(N_CHIPS, nb),
        in_specs=[half, pl.BlockSpec((None, tr, cdim), lambda k, i: (k, nb + i, 0)), half],
        out_specs=half, out_shape=S((N_CHIPS, r2, cdim), BF16), compiler_params=_params(2))(g, g, rx)


def sum_chips(name, arrived, mine):
    _, r2, cdim = arrived.shape
    tr = _tile(r2, 512, 16)

    def body(a_ref, m_ref, o_ref):
        me = 2 * lax.axis_index("x") + lax.axis_index("y")
        acc = jnp.zeros((tr, cdim), F32)
        for k in range(N_CHIPS):
            acc = acc + jnp.where(me == k, m_ref[k], a_ref[k]).astype(F32)
        o_ref[...] = acc

    slots = pl.BlockSpec((N_CHIPS, tr, cdim), lambda i: (0, i, 0))
    return pl.pallas_call(
        body, name=name, grid=(r2 // tr,), in_specs=[slots, slots],
        out_specs=pl.BlockSpec((tr, cdim), lambda i: (i, 0)), out_shape=S((r2, cdim), F32), compiler_params=_params(1))(arrived, mine)


def join_copies(src, dst, where, send_sems, recv_sems):
    x, y, c, _ = _place()
    pairs = []
    for i in range(len(src)):
        def copy(half_of):
            r2 = src[i].shape[0]
            rows = pl.ds(pl.multiple_of(where[i][1] + half_of * r2, 8), r2)
            return pltpu.make_async_remote_copy(
                src_ref=src[i], dst_ref=dst[where[i][0]].at[rows], send_sem=send_sems.at[i],
                recv_sem=recv_sems.at[i], device_id=(x, y, 1 - c), device_id_type=MESH)
        pairs.append((copy(c), copy(1 - c)))
    return pairs


def sibling_join_halves(name, halves, targets, where):
    n = len(halves)

    def body(*refs):
        src, dst = refs[:n], refs[n:n + len(targets)]
        send_sems, recv_sems, local_sems = refs[n + len(targets):n + len(targets) + 3]
        stages = refs[n + len(targets) + 3:]
        c = lax.axis_index("c")
        own = [(src[i], stages[i],
                dst[where[i][0]].at[pl.ds(pl.multiple_of(where[i][1] + c * src[i].shape[0], 8), src[i].shape[0])],
                local_sems.at[i]) for i in range(n)]
        for s_ref, stage, _, sems in own:
            pltpu.make_async_copy(s_ref, stage, sems.at[0]).start()
        pairs = join_copies(src, dst, where, send_sems, recv_sems)
        for outgoing, _ in pairs:
            outgoing.start()
        for _, incoming in pairs:
            incoming.wait_recv()
        for outgoing, _ in pairs:
            outgoing.wait_send()
        _place_locally(own)

    return list(pl.pallas_call(
        body, name=name, in_specs=[ANY] * n, out_specs=[ANY] * len(targets), out_shape=[S(tg, F32) for tg in targets],
        scratch_shapes=[pltpu.SemaphoreType.DMA((n,)), pltpu.SemaphoreType.DMA((n,)), pltpu.SemaphoreType.DMA((n, 2))]
        + [pltpu.VMEM(h.shape, h.dtype) for h in halves],
    )(*halves))


def all_reduce_small(name, packed):
    rows, width = packed.shape

    def body(x_ref, o_ref, gathered, send_sems, recv_sems):
        x, y, c, _ = _place()
        me = 4 * x + 2 * y + c
        gathered[me] = x_ref[...]
        flips = [(fx, fy, fc) for fx in (0, 1) for fy in (0, 1) for fc in (0, 1)][1:]

        def copy(r, slot, to):
            return pltpu.make_async_remote_copy(
                src_ref=x_ref, dst_ref=gathered.at[slot], send_sem=send_sems.at[r], recv_sem=recv_sems.at[r],
                device_id=to, device_id_type=MESH)

        def peer(f):
            return (x ^ f[0], y ^ f[1], c ^ f[2])

        sent = [copy(r, me, peer(f)) for r, f in enumerate(flips)]
        for cp in sent:
            cp.start()
        for r, f in enumerate(flips):
            px, py, pc = peer(f)
            copy(r, 4 * px + 2 * py + pc, peer(f)).wait_recv()
        for cp in sent:
            cp.wait_send()
        acc = gathered[0]
        for k in range(1, N_DEV):
            acc = acc + gathered[k]
        o_ref[...] = acc

    vmem = pl.BlockSpec(memory_space=pltpu.VMEM)
    return pl.pallas_call(
        body, name=name, in_specs=[vmem], out_specs=vmem, out_shape=S((rows, width), F32),
        scratch_shapes=[pltpu.VMEM((N_DEV, rows, width), F32), pltpu.SemaphoreType.DMA((N_DEV - 1,)),
                        pltpu.SemaphoreType.DMA((N_DEV - 1,))],
    )(packed)


def _rope_tables(positions):
    inv_freq = 1.0 / (ROPE_THETA ** (jnp.arange(0, ROPE, 2, dtype=F32) / ROPE))
    ang = positions.astype(F32)[:, None] * inv_freq
    return jnp.cos(ang), jnp.sin(ang)


def _unstack_cols(w):
    k4, k, n4 = w.shape
    return jnp.transpose(w, (1, 0, 2)).reshape(k, k4 * n4)


def _stack_cols(w):
    k, n = w.shape
    return jnp.transpose(w.reshape(k, N_CHIPS, n // N_CHIPS), (1, 0, 2))


def kernel(x, positions, mla_norm, mla_w_in, mla_g_cq, mla_g_ckv, mla_w_uq, mla_w_ukv, mla_w_o, conv_norm, conv_w_in, conv_w, conv_w_out, ffn_norm, ffn_w_gate, ffn_w_up, ffn_w_down, final_norm, loss_target, m_mla_norm, m_mla_w_in, m_mla_g_cq, m_mla_g_ckv, m_mla_w_uq, m_mla_w_ukv, m_mla_w_o, m_conv_norm, m_conv_w_in, m_conv_w, m_conv_w_out, m_ffn_norm, m_ffn_w_gate, m_ffn_w_up, m_ffn_w_down, m_final_norm, v_mla_norm, v_mla_w_in, v_mla_g_cq, v_mla_g_ckv, v_mla_w_uq, v_mla_w_ukv, v_mla_w_o, v_conv_norm, v_conv_w_in, v_conv_w, v_conv_w_out, v_ffn_norm, v_ffn_w_gate, v_ffn_w_up, v_ffn_w_down, v_final_norm):
    weights = dict(mla_norm=mla_norm, mla_w_in=mla_w_in, mla_g_cq=mla_g_cq, mla_g_ckv=mla_g_ckv, mla_w_uq=mla_w_uq,
                   mla_w_ukv=mla_w_ukv, mla_w_o=mla_w_o, conv_norm=conv_norm, conv_w_in=conv_w_in, conv_w=conv_w,
                   conv_w_out=conv_w_out, ffn_norm=ffn_norm, ffn_w_gate=ffn_w_gate, ffn_w_up=ffn_w_up,
                   ffn_w_down=ffn_w_down, final_norm=final_norm)
    m_in = dict(mla_norm=m_mla_norm, mla_w_in=m_mla_w_in, mla_g_cq=m_mla_g_cq, mla_g_ckv=m_mla_g_ckv, mla_w_uq=m_mla_w_uq,
                mla_w_ukv=m_mla_w_ukv, mla_w_o=m_mla_w_o, conv_norm=m_conv_norm, conv_w_in=m_conv_w_in, conv_w=m_conv_w,
                conv_w_out=m_conv_w_out, ffn_norm=m_ffn_norm, ffn_w_gate=m_ffn_w_gate, ffn_w_up=m_ffn_w_up,
                ffn_w_down=m_ffn_w_down, final_norm=m_final_norm)
    v_in = dict(mla_norm=v_mla_norm, mla_w_in=v_mla_w_in, mla_g_cq=v_mla_g_cq, mla_g_ckv=v_mla_g_ckv, mla_w_uq=v_mla_w_uq,
                mla_w_ukv=v_mla_w_ukv, mla_w_o=v_mla_w_o, conv_norm=v_conv_norm, conv_w_in=v_conv_w_in, conv_w=v_conv_w,
                conv_w_out=v_conv_w_out, ffn_norm=v_ffn_norm, ffn_w_gate=v_ffn_w_gate, ffn_w_up=v_ffn_w_up,
                ffn_w_down=v_ffn_w_down, final_norm=v_final_norm)
    big = ["mla_w_in", "mla_w_uq", "mla_w_ukv", "mla_w_o", "conv_w_in", "conv_w_out", "ffn_w_gate", "ffn_w_up", "ffn_w_down"]
    order = list(weights)

    t, d = x.shape[1], x.shape[2]
    h0 = x.reshape(t, d)
    target = loss_target.reshape(t, d)
    cos, sin = _rope_tables(positions.reshape(t))

    def rows2d(a):
        return a.reshape(-1, a.shape[-1])

    first, later = big[:4], big[4:]
    shards = {n: rows2d(weights[n]).astype(BF16) for n in big}
    d4 = d // N_CHIPS
    a0, first_gathered, (conv_norm_slots, conv_w_slots) = gather_weight_shards(
        [shards[n] for n in first], [conv_norm.reshape(1, d4), conv_w.reshape(3, d4)], h0, mla_norm)
    gathered = dict(zip(first, first_gathered))
    conv_norm_full = conv_norm_slots.reshape(1, d)
    conv_w_full = jnp.transpose(conv_w_slots, (1, 0, 2)).reshape(3, d)
    w_in = gathered["mla_w_in"].reshape(-1, gathered["mla_w_in"].shape[-1])
    w_uq = _unstack_cols(gathered["mla_w_uq"])
    w_ukv = _unstack_cols(gathered["mla_w_ukv"])
    w_o = gathered["mla_w_o"].reshape(-1, d)

    chip = 2 * lax.axis_index("x") + lax.axis_index("y")

    def pack_rows(rows):
        idx = lax.broadcasted_iota(jnp.int32, (SMALL_ROWS, d), 0)
        out = jnp.zeros((SMALL_ROWS, d), F32)
        for r, row in enumerate(rows):
            out = out + jnp.where(idx == r, row, 0.0)
        return out


    proj, cq, ckv, kr = mla_in_proj("mla_in_proj", a0, w_in, mla_g_cq, mla_g_ckv, cos, sin)
    qh, kh, vh, conv_arriving = qkv_heads("qkv_heads", cq, ckv, w_uq, w_ukv, kr, cos, sin, [shards[n] for n in later[:2]])
    attn, lse, ffn_arriving = attention_fwd("attention_fwd", qh, kh, vh, [shards[n] for n in later[2:]])
    h1, a1, handed = attention_out_proj("mla_out_proj", attn, w_o, h0, ffn_norm[0:1], conv_arriving + ffn_arriving,
                                        [shards[n] for n in later])
    gathered.update(zip(later, handed))
    cw_in = _unstack_cols(gathered["conv_w_in"])
    cw_out = gathered["conv_w_out"].reshape(-1, d)
    wg_all, wu_all, wd_all = gathered["ffn_w_gate"], gathered["ffn_w_up"], gathered["ffn_w_down"]

    def ffn_forward(tag, h, a, layer, next_gain):
        g, u, z = ffn_up(f"ffn{tag}_up", a, wg_all, wu_all, layer)
        return g, u, z, ffn_down(f"ffn{tag}_down", z, wd_all, layer, h, next_gain)

    g0, u0, z0, (h2, a2) = ffn_forward(0, h1, a1, 0, conv_norm_full)
    bcx, yc = conv_in_proj("conv_in_proj", a2, cw_in, conv_w_full)
    h3, a3 = linear("conv_out_proj", yc, cw_out, F32, resid=h2, next_gain=ffn_norm[1:2])
    g1, u1, z1 = ffn_up("ffn1_up", a3, wg_all, wu_all, 1)
    dh4, d_final_norm, loss_local = ffn_down_loss("ffn1_down_loss", z1, wd_all, 1, h3, final_norm.reshape(1, d), target)

    def ffn_backward(tag, dh, h, layer, a, g, u, z, swap=()):
        dg, du, swapped = ffn_bwd_hidden(f"ffn{tag}_bwd_hidden", dh, wd_all, layer, g, u, swap)
        d_wd = ffn_wgrad_down(f"ffn{tag}_wgrad_down", z, dh)
        dh_prev, d_norm = ffn_bwd_input(f"ffn{tag}_bwd_input", dg, du, wg_all, wu_all, layer, h, ffn_norm[layer:layer + 1], dh)
        d_wg = ffn_wgrad_up(f"ffn{tag}_wgrad_gate", a, dg)
        d_wu = ffn_wgrad_up(f"ffn{tag}_wgrad_up", a, du)
        return dh_prev, d_norm, [d_wg, d_wu, d_wd], swapped

    def pair_sums(tag, local, from_sibling):
        return [add_halves(f"pair_sum_{tag}{i}", g, r) for i, (g, r) in enumerate(zip(local, from_sibling))]

    def sum_from_chips(tag, pairs, arrived):
        return [sum_chips(f"chip_sum_{tag}{i}", a, p) for i, (a, p) in enumerate(zip(arrived, pairs))]

    def shard_shape(n):
        return rows2d(weights[n]).shape

    dh3, d_ffn_norm1, ffn1_grads, _ = ffn_backward(1, dh4, h3, 1, a3, g1, u1, z1)

    dyc = linear_nt("conv_out_bwd_input", dh3, cw_out, F32)
    d_cw_out = wgrad("conv_out_wgrad", yc, dh3)
    dbcx, d_conv_w = conv_bwd("conv_bwd", bcx, conv_w_full, dyc)
    dh2, d_conv_norm = conv_in_bwd_input("conv_in_bwd_input", dbcx, cw_in, h2, conv_norm_full, dh3)
    d_cw_in = conv_in_wgrad("conv_in_wgrad", a2, dbcx)

    second = [d_cw_in, d_cw_out.reshape(N_CHIPS, -1, d)] + ffn1_grads
    dh1, d_ffn_norm0, ffn0_grads, second_swapped = ffn_backward(0, dh2, h1, 0, a1, g0, u0, z0, second)
    d_w_o = wgrad("mla_out_wgrad", attn, dh1)
    first_part = ffn0_grads + [d_w_o.reshape(N_CHIPS, -1, d)]
    d_attn, delta, first_swapped = attention_out_bwd("mla_out_bwd_input", dh1, w_o, attn, first_part)
    rest_pairs = pair_sums("rest", second + first_part, second_swapped + first_swapped)
    dqh, dkh, dvh, rest_arrived = attention_bwd("attention_bwd", qh, kh, vh, d_attn, lse, delta, rest_pairs)
    rd, rf = ffn0_grads[0].shape[1], ffn0_grads[2].shape[1]
    rest_where = [(0, 0), (1, 0), (2, rd), (3, rd), (4, rf), (2, 0), (3, 0), (4, 0), (5, 0)]
    rest_names = later + ["mla_w_o"]
    dq, dkv, dproj, d_g_cq, d_g_ckv, rest_grads = qkv_heads_bwd(
        "qkv_heads_bwd", dqh, dkh, dvh, cos, sin, w_uq, w_ukv, proj, mla_g_cq, mla_g_ckv,
        sum_from_chips("rest", rest_pairs, rest_arrived), [shard_shape(n) for n in rest_names], rest_where)
    grads = dict(zip(rest_names, rest_grads))
    d_w_uq = wgrad("mla_q_up_wgrad", cq, dq)
    d_w_ukv = wgrad("mla_kv_up_wgrad", ckv, dkv)
    d_w_in = wgrad("mla_in_wgrad", a0, dproj)
    mla_local = [d_w_in.reshape(N_CHIPS, -1, d_w_in.shape[-1]), _stack_cols(d_w_uq), _stack_cols(d_w_ukv)]
    mla_pairs = pair_sums("mla", mla_local, sibling_swap_halves("sibling_swap_mla", mla_local))
    grad_x, d_mla_norm, mla_arrived = linear_nt_norm_bwd("mla_in_bwd_input", dproj, w_in, h0, mla_norm, dh1, mla_pairs)

    grads.update(zip(first[:3], sibling_join_halves("sibling_join_mla", sum_from_chips("mla", mla_pairs, mla_arrived),
                                                    [shard_shape(n) for n in first[:3]], [(i, 0) for i in range(3)])))

    def pad_row(v):
        return jnp.pad(v, ((0, 0), (0, d - v.shape[1])))

    small = all_reduce_small("all_reduce_small_grads", pack_rows([
        d_mla_norm, pad_row(d_g_cq), pad_row(d_g_ckv), d_ffn_norm0, d_ffn_norm1, d_final_norm, d_conv_norm,
        d_conv_w[0:1], d_conv_w[1:2], d_conv_w[2:3], jnp.broadcast_to(loss_local, (1, d))]))
    loss = small[10, 0]
    grads["mla_norm"] = small[0:1]
    grads["mla_g_cq"] = small[1:2, :mla_g_cq.shape[1]]
    grads["mla_g_ckv"] = small[2:3, :mla_g_ckv.shape[1]]
    grads["ffn_norm"] = small[3:5]
    grads["final_norm"] = small[5:6]
    grads["conv_norm"] = lax.dynamic_slice(small[6:7], (0, chip * d4), (1, d4))
    grads["conv_w"] = lax.dynamic_slice(small[7:10], (0, chip * d4), (3, d4))

    outs_g, outs_d, outs_m, outs_v = [], [], [], []
    for n in order:
        w = weights[n]
        if w.ndim == 3 and w.shape[2] % 128 and w.shape[1] % 128 == 0:
            results = adamw_swapped(f"adamw_{n}", jnp.swapaxes(w, 1, 2), grads[n].reshape(-1, w.shape[2]),
                                    jnp.swapaxes(m_in[n], 1, 2), jnp.swapaxes(v_in[n], 1, 2))
            grad_w, delta_w, new_m, new_v = [jnp.swapaxes(o, 1, 2) for o in results]
        else:
            delta_w, new_m, new_v = adamw(f"adamw_{n}", rows2d(w), grads[n].reshape(rows2d(w).shape), rows2d(m_in[n]), rows2d(v_in[n]))
            grad_w = grads[n]
        outs_g.append(grad_w.reshape(w.shape))
        outs_d.append(delta_w.reshape(w.shape))
        outs_m.append(new_m.reshape(w.shape))
        outs_v.append(new_v.reshape(w.shape))
    return (loss, grad_x.reshape(x.shape), *outs_g, *outs_d, *outs_m, *outs_v)
```

```python
import math

import jax
import jax.numpy as jnp
from jax import lax
from jax.experimental import pallas as pl
from jax.experimental.pallas import tpu as pltpu

F32 = jnp.float32
BF16 = jnp.bfloat16
S = jax.ShapeDtypeStruct

N_HEADS = 8
NOPE = 128
ROPE = 64
HALF = ROPE // 2
VDIM = 128
QK = NOPE + ROPE
CHUNK = 64
ROPE_THETA = 10000.0
RMS_EPS = 1e-6
ADAM_LR = 0.001
ADAM_B1 = 0.9
ADAM_B2 = 0.999
ADAM_EPS = 1e-08
ADAM_WD = 0.01
ADAM_STEP = 10

N_CHIPS = 4
N_DEV = 8
MASK_VALUE = -1e30
SCORE_SCALE = 1.0 / math.sqrt(QK)
LOG2_E = math.log2(math.e)
SCORE_SCALE_LOG2 = SCORE_SCALE * LOG2_E
VMEM_LIMIT = 48 * 1024 * 1024
VMEM_LIMIT_WHOLE_HEAD = 58 * 1024 * 1024
ATT_BLOCK = 512
CONV_SAVED_DTYPE = jnp.bfloat16
SMALL_ROWS = 16

_NN = (((1,), (0,)), ((), ()))
_NT = (((1,), (1,)), ((), ()))
_TN = (((0,), (0,)), ((), ()))
MESH = pl.DeviceIdType.MESH
ANY = pl.BlockSpec(memory_space=pl.ANY)


def _params(n_axes, vmem_limit=VMEM_LIMIT):
    return pltpu.CompilerParams(dimension_semantics=("arbitrary",) * n_axes, vmem_limit_bytes=vmem_limit)


def _tile(n, cap, mult=8):
    for t in range(min(cap, n), 0, -1):
        if n % t == 0 and t % mult == 0:
            return t
    return n


def _sigmoid(x):
    return 0.5 * jnp.tanh(0.5 * x) + 0.5


def _mm(name, a_ops, b_ops, products, dims, grid, k_axis, outs, acc_shape, epilogue, extra_ops=()):
    na, nb, ne, no = len(a_ops), len(b_ops), len(extra_ops), len(outs)
    n_acc = 1 + max(c for _, _, c in products)
    nk = 1 if k_axis is None else grid[k_axis]

    def body(*refs):
        a_refs = refs[:na]
        b_refs = refs[na:na + nb]
        e_refs = refs[na + nb:na + nb + ne]
        o_refs = refs[na + nb + ne:na + nb + ne + no]
        acc_refs = refs[na + nb + ne + no:]

        def partial_sums():
            vals = [None] * n_acc
            for ai, bi, ci in products:
                d = lax.dot_general(a_refs[ai][...].astype(BF16), b_refs[bi][...].astype(BF16), dims,
                                    preferred_element_type=F32)
                vals[ci] = d if vals[ci] is None else vals[ci] + d
            return vals

        if nk == 1:
            epilogue(partial_sums(), e_refs, o_refs)
        else:
            k = pl.program_id(k_axis)

            @pl.when(k == 0)
            def _():
                for acc in acc_refs:
                    acc[...] = jnp.zeros_like(acc)

            for acc, v in zip(acc_refs, partial_sums()):
                acc[...] += v

            @pl.when(k == nk - 1)
            def _():
                epilogue([acc[...] for acc in acc_refs], e_refs, o_refs)

    ops = list(a_ops) + list(b_ops) + list(extra_ops)
    return pl.pallas_call(
        body, name=name, grid=grid,
        in_specs=[s for _, s in ops], out_specs=[s for _, s in outs], out_shape=[o for o, _ in outs],
        scratch_shapes=[pltpu.VMEM(acc_shape, F32) for _ in range(n_acc if nk > 1 else 0)],
        compiler_params=_params(len(grid)),
    )(*[a for a, _ in ops])


def _store(accs, e_refs, o_refs):
    o_refs[0][...] = accs[0].astype(o_refs[0].dtype)


def linear(name, x, w, out_dtype, resid=None, next_gain=None):
    t, k = x.shape
    n = w.shape[1]
    tm = _tile(t, 512)
    tn = n if n <= 2048 else _tile(n, 1024, 128)
    tile = pl.BlockSpec((tm, tn), lambda j, i: (i, j))
    extra = [] if resid is None else [(resid, tile)]
    outs = [(S((t, n), out_dtype), tile)]
    if next_gain is not None:
        assert tn == n
        extra.append((next_gain, pl.BlockSpec((1, n), lambda j, i: (0, 0))))
        outs.append((S((t, n), BF16), tile))

    def epilogue(accs, e_refs, o_refs):
        y = accs[0] if resid is None else e_refs[0][...] + accs[0]
        o_refs[0][...] = y.astype(out_dtype)
        if next_gain is not None:
            o_refs[1][...] = (y * _rstd(y) * e_refs[-1][...]).astype(BF16)

    res = _mm(name, [(x, pl.BlockSpec((tm, k), lambda j, i: (i, 0)))], [(w, pl.BlockSpec((k, tn), lambda j, i: (0, j)))],
              [(0, 0, 0)], _NN, (n // tn, t // tm), None, outs, None, epilogue, extra)
    return res[0] if next_gain is None else res


def linear_nt(name, dy, w, out_dtype):
    t, n = dy.shape
    k = w.shape[0]
    tm = _tile(t, 512)
    tc = n if n <= 2048 else _tile(n, 1024, 128)
    return _mm(name, [(dy, pl.BlockSpec((tm, tc), lambda i, c: (i, c)))], [(w, pl.BlockSpec((k, tc), lambda i, c: (0, c)))],
               [(0, 0, 0)], _NT, (t // tm, n // tc), 1,
               [(S((t, k), out_dtype), pl.BlockSpec((tm, k), lambda i, c: (i, 0)))], (tm, k), _store)[0]


def wgrad(name, x, dy):
    t, k = x.shape
    n = dy.shape[1]
    tk = _tile(t, 512)
    tn = n if n <= 1024 else _tile(n, 1024, 128)
    return _mm(name, [(x, pl.BlockSpec((tk, k), lambda j, s: (s, 0)))], [(dy, pl.BlockSpec((tk, tn), lambda j, s: (s, j)))],
               [(0, 0, 0)], _TN, (n // tn, t // tk), 1,
               [(S((k, n), BF16), pl.BlockSpec((k, tn), lambda j, s: (0, j)))], (k, tn), _store)[0]


def _resident(shape, index_map):
    return pl.BlockSpec(shape, index_map, pipeline_mode=pl.Buffered(1))


def ffn_up(name, a, wg_all, wu_all, layer):
    t, d = a.shape
    f4 = wg_all.shape[2]
    tm = _tile(t, 512)
    w_spec = _resident((N_CHIPS, d, f4), lambda i: (0, layer, 0))
    h_spec = pl.BlockSpec((N_CHIPS, tm, f4), lambda i: (0, i, 0))

    def body(a_ref, wg_ref, wu_ref, zg_ref, zu_ref, z_ref):
        av = a_ref[...]
        for k in range(N_CHIPS):
            g = jnp.dot(av, wg_ref[k], preferred_element_type=F32)
            u = jnp.dot(av, wu_ref[k], preferred_element_type=F32)
            sg = _sigmoid(g)
            silu = g * sg
            zg_ref[k] = (u * (sg * (1.0 + g * (1.0 - sg)))).astype(BF16)
            zu_ref[k] = silu.astype(BF16)
            z_ref[k] = (silu * u).astype(BF16)

    return pl.pallas_call(
        body, name=name, grid=(t // tm,), in_specs=[pl.BlockSpec((tm, d), lambda i: (i, 0)), w_spec, w_spec],
        out_specs=[h_spec] * 3, out_shape=[S((N_CHIPS, t, f4), BF16)] * 3, compiler_params=_params(1))(a, wg_all, wu_all)


def ffn_down(name, z, wd_all, layer, resid, next_gain=None):
    _, t, f4 = z.shape
    d = wd_all.shape[2]
    tm = _tile(t, 512)
    row = pl.BlockSpec((tm, d), lambda i: (i, 0))
    normed = next_gain is not None

    def body(z_ref, wd_ref, r_ref, *refs):
        acc = r_ref[...]
        for k in range(N_CHIPS):
            acc = acc + jnp.dot(z_ref[k], wd_ref[k], preferred_element_type=F32)
        refs[-2 if normed else -1][...] = acc
        if normed:
            refs[-1][...] = (acc * _rstd(acc) * refs[0][...]).astype(BF16)

    res = pl.pallas_call(
        body, name=name, grid=(t // tm,),
        in_specs=[pl.BlockSpec((N_CHIPS, tm, f4), lambda i: (0, i, 0)), _resident((N_CHIPS, f4, d), lambda i: (0, layer, 0)), row]
        + ([pl.BlockSpec((1, d), lambda i: (0, 0))] if normed else []),
        out_specs=[row] * (2 if normed else 1), out_shape=[S((t, d), F32)] + ([S((t, d), BF16)] if normed else []),
        compiler_params=_params(1))(z, wd_all, resid, *([next_gain] if normed else []))
    return res if normed else res[0]


def ffn_bwd_hidden(name, dh, wd_all, layer, zg, zu, swap=()):
    t, d = dh.shape
    f4 = zg.shape[2]
    tm = _tile(t, 512)
    h_spec = pl.BlockSpec((N_CHIPS, tm, f4), lambda i: (0, i, 0))
    n = len(swap)

    def body(dh_ref, wd_ref, zg_ref, zu_ref, *refs):
        dg_ref, du_ref = refs[n:n + 2]
        if n:
            _ride_along(swap_copies(refs[:n], refs[n + 2:2 * n + 2], *refs[2 * n + 2:]), (pl.program_id(0),), (t // tm,))
        dhb = dh_ref[...].astype(BF16)
        for k in range(N_CHIPS):
            dz = lax.dot_general(dhb, wd_ref[k], _NT, preferred_element_type=F32)
            dg_ref[k] = (dz * zg_ref[k].astype(F32)).astype(BF16)
            du_ref[k] = (dz * zu_ref[k].astype(F32)).astype(BF16)

    outs = pl.pallas_call(
        body, name=name, grid=(t // tm,),
        in_specs=[pl.BlockSpec((tm, d), lambda i: (i, 0)), _resident((N_CHIPS, f4, d), lambda i: (0, layer, 0)), h_spec, h_spec]
        + [ANY] * n,
        out_specs=[h_spec] * 2 + [ANY] * n, out_shape=[S((N_CHIPS, t, f4), BF16)] * 2 + _swap_shapes(swap),
        scratch_shapes=[pltpu.SemaphoreType.DMA((n,)), pltpu.SemaphoreType.DMA((n,))] if n else [],
        compiler_params=_params(1))(dh, wd_all, zg, zu, *swap)
    return outs[0], outs[1], list(outs[2:])


def _norm_bwd_specs(tm, d):
    row = pl.BlockSpec((tm, d), lambda i: (i, 0))
    vec = pl.BlockSpec((1, d), lambda i: (0, 0))
    return [row, vec, row], [row, vec]


def _norm_bwd_tail(da, h_ref, g_ref, dhi_ref, dho_ref, dgain_ref):
    dx, dgain = _rms_bwd(h_ref[...], g_ref[...], da)
    dho_ref[...] = dhi_ref[...] + dx

    @pl.when(pl.program_id(0) == 0)
    def _():
        dgain_ref[...] = jnp.zeros_like(dgain_ref)

    dgain_ref[...] += dgain


def ffn_bwd_input(name, dg, du, wg_all, wu_all, layer, h, gain, dh_in):
    _, t, f4 = dg.shape
    d = h.shape[1]
    tm = _tile(t, 512)
    h_spec = pl.BlockSpec((N_CHIPS, tm, f4), lambda i: (0, i, 0))
    w_spec = _resident((N_CHIPS, d, f4), lambda i: (0, layer, 0))
    tail_in, tail_out = _norm_bwd_specs(tm, d)

    def body(dg_ref, du_ref, wg_ref, wu_ref, *tail):
        acc = jnp.zeros((tm, d), F32)
        for k in range(N_CHIPS):
            acc = acc + lax.dot_general(dg_ref[k], wg_ref[k], _NT, preferred_element_type=F32)
            acc = acc + lax.dot_general(du_ref[k], wu_ref[k], _NT, preferred_element_type=F32)
        _norm_bwd_tail(acc, *tail)

    return pl.pallas_call(
        body, name=name, grid=(t // tm,), in_specs=[h_spec, h_spec, w_spec, w_spec] + tail_in, out_specs=tail_out,
        out_shape=[S((t, d), F32), S((1, d), F32)], compiler_params=_params(1))(dg, du, wg_all, wu_all, h, gain, dh_in)


def ffn_wgrad_up(name, a, dy):
    t, d = a.shape
    f4 = dy.shape[2]
    tk = _tile(t, 512)
    nt = t // tk

    def body(a_ref, dy_ref, o_ref, acc):
        s = pl.program_id(0)

        @pl.when(s == 0)
        def _():
            acc[...] = jnp.zeros_like(acc)

        at = a_ref[...].T
        for k in range(N_CHIPS):
            acc[k] += jnp.dot(at, dy_ref[k], preferred_element_type=F32)

        @pl.when(s == nt - 1)
        def _():
            o_ref[...] = acc[...].astype(BF16)

    return pl.pallas_call(
        body, name=name, grid=(nt,),
        in_specs=[pl.BlockSpec((tk, d), lambda s: (s, 0)), pl.BlockSpec((N_CHIPS, tk, f4), lambda s: (0, s, 0))],
        out_specs=pl.BlockSpec((N_CHIPS, d, f4), lambda s: (0, 0, 0)), out_shape=S((N_CHIPS, d, f4), BF16),
        scratch_shapes=[pltpu.VMEM((N_CHIPS, d, f4), F32)], compiler_params=_params(1))(a, dy)


def ffn_wgrad_down(name, z, dh):
    _, t, f4 = z.shape
    d = dh.shape[1]
    tk = _tile(t, 512)
    nt = t // tk

    def body(z_ref, dh_ref, o_ref, acc):
        s = pl.program_id(0)

        @pl.when(s == 0)
        def _():
            acc[...] = jnp.zeros_like(acc)

        dhb = dh_ref[...].astype(BF16)
        for k in range(N_CHIPS):
            acc[k] += lax.dot_general(z_ref[k], dhb, _TN, preferred_element_type=F32)

        @pl.when(s == nt - 1)
        def _():
            o_ref[...] = acc[...].astype(BF16)

    return pl.pallas_call(
        body, name=name, grid=(nt,),
        in_specs=[pl.BlockSpec((N_CHIPS, tk, f4), lambda s: (0, s, 0)), pl.BlockSpec((tk, d), lambda s: (s, 0))],
        out_specs=pl.BlockSpec((N_CHIPS, f4, d), lambda s: (0, 0, 0)), out_shape=S((N_CHIPS, f4, d), BF16),
        scratch_shapes=[pltpu.VMEM((N_CHIPS, f4, d), F32)], compiler_params=_params(1))(z, dh)


def conv_in_proj(name, a, w, conv_w):
    t, d = a.shape
    tm = _tile(t, 256)
    keep = 8

    def body(a_ref, w_ref, cw_ref, bcx_ref, y_ref, u_ref):
        @pl.when(pl.program_id(0) == 0)
        def _():
            u_ref[0:keep, :] = jnp.zeros((keep, d), F32)

        av = a_ref[...]
        b, c, x = [jnp.dot(av, w_ref[:, j * d:(j + 1) * d], preferred_element_type=F32) for j in range(3)]
        for j, part in enumerate((b, c, x)):
            bcx_ref[j] = part.astype(bcx_ref.dtype)
        u_ref[keep:keep + tm, :] = c * x
        uc = (cw_ref[0:1, :] * u_ref[keep - 2:keep - 2 + tm, :] + cw_ref[1:2, :] * u_ref[keep - 1:keep - 1 + tm, :]
              + cw_ref[2:3, :] * u_ref[keep:keep + tm, :])
        y_ref[...] = (b * uc).astype(BF16)
        u_ref[0:keep, :] = u_ref[tm:tm + keep, :]

    return pl.pallas_call(
        body, name=name, grid=(t // tm,),
        in_specs=[pl.BlockSpec((tm, d), lambda i: (i, 0)), _resident((d, 3 * d), lambda i: (0, 0)), pl.BlockSpec((3, d), lambda i: (0, 0))],
        out_specs=[pl.BlockSpec((3, tm, d), lambda i: (0, i, 0)), pl.BlockSpec((tm, d), lambda i: (i, 0))],
        out_shape=[S((3, t, d), CONV_SAVED_DTYPE), S((t, d), BF16)], scratch_shapes=[pltpu.VMEM((tm + keep, d), F32)],
        compiler_params=_params(1))(a, w, conv_w)


def conv_in_bwd_input(name, dbcx, w, h, gain, dh_in):
    _, t, d = dbcx.shape
    tm = _tile(t, 512)
    tail_in, tail_out = _norm_bwd_specs(tm, d)

    def body(g_ref, w_ref, *tail):
        acc = jnp.zeros((tm, d), F32)
        for j in range(3):
            acc = acc + lax.dot_general(g_ref[j], w_ref[:, j * d:(j + 1) * d], _NT, preferred_element_type=F32)
        _norm_bwd_tail(acc, *tail)

    return pl.pallas_call(
        body, name=name, grid=(t // tm,),
        in_specs=[pl.BlockSpec((3, tm, d), lambda i: (0, i, 0)), _resident((d, 3 * d), lambda i: (0, 0))] + tail_in,
        out_specs=tail_out, out_shape=[S((t, d), F32), S((1, d), F32)], compiler_params=_params(1))(dbcx, w, h, gain, dh_in)


def linear_nt_norm_bwd(name, dy, w, h, gain, dh_in, parts=()):
    t, n = dy.shape
    k = w.shape[0]
    tm = _tile(t, 512)
    tail_in, tail_out = _norm_bwd_specs(tm, k)
    m = len(parts)

    def body(dy_ref, w_ref, h_ref, g_ref, dhi_ref, *refs):
        if m:
            _ride_along(scatter_ici_copies(refs[:m], refs[m + 2:2 * m + 2], *refs[2 * m + 2:]), (pl.program_id(0),), (t // tm,))
        da = lax.dot_general(dy_ref[...].astype(BF16), w_ref[...], _NT, preferred_element_type=F32)
        _norm_bwd_tail(da, h_ref, g_ref, dhi_ref, *refs[m:m + 2])

    outs = pl.pallas_call(
        body, name=name, grid=(t // tm,),
        in_specs=[pl.BlockSpec((tm, n), lambda i: (i, 0)), _resident((k, n), lambda i: (0, 0))] + tail_in + [ANY] * m,
        out_specs=tail_out + [ANY] * m, out_shape=[S((t, k), F32), S((1, k), F32)] + [S(p.shape, p.dtype) for p in parts],
        scratch_shapes=[pltpu.SemaphoreType.DMA((m, 3)), pltpu.SemaphoreType.DMA((m, 3))] if m else [],
        compiler_params=_params(1))(dy, w, h, gain, dh_in, *parts)
    return outs[0], outs[1], list(outs[2:])


def conv_in_wgrad(name, a, dbcx):
    t, d = a.shape
    tk = _tile(t, 512)
    nt = t // tk
    n4 = 3 * d // N_CHIPS

    def body(a_ref, g_ref, o_ref, acc):
        s = pl.program_id(0)

        @pl.when(s == 0)
        def _():
            acc[...] = jnp.zeros_like(acc)

        at = a_ref[...].T
        for j in range(3):
            acc[:, j * d:(j + 1) * d] += jnp.dot(at, g_ref[j], preferred_element_type=F32)

        @pl.when(s == nt - 1)
        def _():
            for k in range(N_CHIPS):
                o_ref[k] = acc[:, k * n4:(k + 1) * n4].astype(BF16)

    return pl.pallas_call(
        body, name=name, grid=(nt,),
        in_specs=[pl.BlockSpec((tk, d), lambda s: (s, 0)), pl.BlockSpec((3, tk, d), lambda s: (0, s, 0))],
        out_specs=pl.BlockSpec((N_CHIPS, d, n4), lambda s: (0, 0, 0)), out_shape=S((N_CHIPS, d, n4), BF16),
        scratch_shapes=[pltpu.VMEM((d, 3 * d), F32)], compiler_params=_params(1))(a, dbcx)


def _rstd(x):
    return lax.rsqrt(jnp.mean(x * x, axis=-1, keepdims=True) + RMS_EPS)


def _rms_bwd(x, g, dy):
    r = _rstd(x)
    xhat = x * r
    dgain = jnp.sum(dy * xhat, axis=0, keepdims=True)
    dxh = dy * g
    dx = r * (dxh - xhat * jnp.mean(dxh * xhat, axis=-1, keepdims=True))
    return dx, dgain


def ffn_down_loss(name, z, wd_all, layer, resid, gain, target):
    _, t, f4 = z.shape
    d = wd_all.shape[2]
    tm = _tile(t, 512)

    def body(z_ref, wd_ref, r_ref, g_ref, t_ref, dh_ref, dg_ref, loss_ref):
        x = r_ref[...]
        for k in range(N_CHIPS):
            x = x + jnp.dot(z_ref[k], wd_ref[k], preferred_element_type=F32)
        g = g_ref[...]
        r = _rstd(x)
        xhat = x * r
        err = xhat * g - t_ref[...]
        dy = err * (1.0 / d)
        dxh = dy * g
        dh_ref[...] = r * (dxh - xhat * jnp.mean(dxh * xhat, axis=-1, keepdims=True))

        @pl.when(pl.program_id(0) == 0)
        def _():
            dg_ref[...] = jnp.zeros_like(dg_ref)
            loss_ref[...] = jnp.zeros_like(loss_ref)

        dg_ref[...] += jnp.sum(dy * xhat, axis=0, keepdims=True)
        per_token = jnp.mean(err * err, axis=-1, keepdims=True)
        loss_ref[...] += 0.5 * jnp.sum(per_token, axis=0, keepdims=True)

    row = pl.BlockSpec((tm, d), lambda i: (i, 0))
    vec = pl.BlockSpec((1, d), lambda i: (0, 0))
    one = pl.BlockSpec((1, 1), lambda i: (0, 0))
    return pl.pallas_call(
        body, name=name, grid=(t // tm,),
        in_specs=[pl.BlockSpec((N_CHIPS, tm, f4), lambda i: (0, i, 0)), _resident((N_CHIPS, f4, d), lambda i: (0, layer, 0)), row, vec, row],
        out_specs=[row, vec, one], out_shape=[S((t, d), F32), S((1, d), F32), S((1, 1), F32)],
        compiler_params=_params(1))(z, wd_all, resid, gain, target)


def mla_in_proj(name, a, w, g_cq, g_ckv, cos, sin):
    t, d = a.shape
    n = w.shape[1]
    ql, kl = g_cq.shape[1], g_ckv.shape[1]
    tr = _tile(t, 512)

    def body(a_ref, w_ref, gq_ref, gk_ref, c_ref, s_ref, p_ref, cq_ref, ckv_ref, kr_ref):
        p_ref[...] = jnp.dot(a_ref[...], w_ref[...], preferred_element_type=F32)
        xq = p_ref[:, 0:ql]
        cq_ref[...] = (xq * _rstd(xq) * gq_ref[...]).astype(BF16)
        xk = p_ref[:, ql:ql + kl]
        ckv_ref[...] = (xk * _rstd(xk) * gk_ref[...]).astype(BF16)
        k1 = p_ref[:, ql + kl:ql + kl + HALF]
        k2 = p_ref[:, ql + kl + HALF:ql + kl + ROPE]
        c = c_ref[...]
        s = s_ref[...]
        kr_ref[:, 0:HALF] = k1 * c - k2 * s
        kr_ref[:, HALF:ROPE] = k1 * s + k2 * c

    def row(w):
        return pl.BlockSpec((tr, w), lambda i: (i, 0))

    def vec(w):
        return pl.BlockSpec((1, w), lambda i: (0, 0))

    return pl.pallas_call(
        body, name=name, grid=(t // tr,),
        in_specs=[row(d), _resident((d, n), lambda i: (0, 0)), vec(ql), vec(kl), row(HALF), row(HALF)],
        out_specs=[row(n), row(ql), row(kl), row(ROPE)],
        out_shape=[S((t, n), F32), S((t, ql), BF16), S((t, kl), BF16), S((t, ROPE), F32)],
        compiler_params=_params(1))(a, w, g_cq, g_ckv, cos, sin)


def qkv_heads(name, cq, ckv, w_uq, w_ukv, kr, cos, sin, shards=()):
    t = cq.shape[0]
    tr = _tile(t, 256)
    n = len(shards)

    def body(cq_ref, ckv_ref, wq_ref, wkv_ref, kr_ref, c_ref, s_ref, *refs):
        src = refs[:n]
        qo_ref, ko_ref, vo_ref = refs[n:n + 3]
        q_ref, kv_ref = refs[2 * n + 3:2 * n + 5]
        if n:
            _ride_along(gather_ici_copies(src, refs[n + 3:2 * n + 3], *refs[2 * n + 5:]), (pl.program_id(0),), (t // tr,))
        q_ref[...] = jnp.dot(cq_ref[...], wq_ref[...], preferred_element_type=F32)
        kv_ref[...] = jnp.dot(ckv_ref[...], wkv_ref[...], preferred_element_type=F32).astype(BF16)
        c = c_ref[...]
        s = s_ref[...]
        krb = kr_ref[...].astype(BF16)
        for h in range(N_HEADS):
            q0 = h * QK
            qo_ref[h, :, 0:NOPE] = q_ref[:, q0:q0 + NOPE].astype(BF16)
            q1 = q_ref[:, q0 + NOPE:q0 + NOPE + HALF]
            q2 = q_ref[:, q0 + NOPE + HALF:q0 + QK]
            qo_ref[h, :, NOPE:NOPE + HALF] = (q1 * c - q2 * s).astype(BF16)
            qo_ref[h, :, NOPE + HALF:QK] = (q1 * s + q2 * c).astype(BF16)
            k0 = h * (NOPE + VDIM)
            ko_ref[h, :, 0:NOPE] = kv_ref[:, k0:k0 + NOPE]
            ko_ref[h, :, NOPE:QK] = krb
            vo_ref[h] = kv_ref[:, k0 + NOPE:k0 + NOPE + VDIM]

    def row(w):
        return pl.BlockSpec((tr, w), lambda i: (i, 0))

    def heads(w):
        return pl.BlockSpec((N_HEADS, tr, w), lambda i: (0, i, 0))

    outs = pl.pallas_call(
        body, name=name, grid=(t // tr,),
        in_specs=[row(cq.shape[1]), row(ckv.shape[1]), _resident(w_uq.shape, lambda i: (0, 0)), _resident(w_ukv.shape, lambda i: (0, 0)),
                  row(ROPE), row(HALF), row(HALF)] + [ANY] * n,
        out_specs=[heads(QK), heads(QK), heads(VDIM)] + [ANY] * n,
        out_shape=[S((N_HEADS, t, QK), BF16), S((N_HEADS, t, QK), BF16), S((N_HEADS, t, VDIM), BF16)]
        + [S((N_CHIPS,) + s.shape, s.dtype) for s in shards],
        scratch_shapes=[pltpu.VMEM((tr, N_HEADS * QK), F32), pltpu.VMEM((tr, N_HEADS * (NOPE + VDIM)), BF16)]
        + ([pltpu.SemaphoreType.DMA((n, 3)), pltpu.SemaphoreType.DMA((n, 3))] if n else []),
        compiler_params=_params(1))(cq, ckv, w_uq, w_ukv, kr, cos, sin, *shards)
    return outs[0], outs[1], outs[2], list(outs[3:])


def qkv_heads_bwd(name, dq_h, dk_h, dv_h, cos, sin, w_uq, w_ukv, proj, g_cq, g_ckv, halves=(), targets=(), where=()):
    t = dq_h.shape[1]
    tr = _tile(t, 256)
    n, nt = len(halves), len(targets)
    ql, kl = g_cq.shape[1], g_ckv.shape[1]
    width = proj.shape[1]

    def body(dq_ref, dk_ref, dv_ref, c_ref, s_ref, wq_ref, wkv_ref, p_ref, gq_ref, gk_ref, *refs):
        q_ref, kv_ref, dp_ref, dgq_ref, dgk_ref = refs[n:n + 5]
        kr_ref = refs[n + 5 + nt]
        if n:
            src, dst = refs[:n], refs[n + 5:n + 5 + nt]
            stages = refs[n + 6 + nt:2 * n + 6 + nt]
            send_sems, recv_sems, local_sems = refs[2 * n + 6 + nt:]
            core = lax.axis_index("c")
            own = [(src[i], stages[i],
                    dst[where[i][0]].at[pl.ds(pl.multiple_of(where[i][1] + core * src[i].shape[0], 8), src[i].shape[0])],
                    local_sems.at[i]) for i in range(n)]
            _ride_along(join_copies(src, dst, where, send_sems, recv_sems), (pl.program_id(0),), (t // tr,), own)
        c = c_ref[...]
        s = s_ref[...]
        dkr = jnp.zeros((tr, ROPE), F32)
        for h in range(N_HEADS):
            q0 = h * QK
            q_ref[:, q0:q0 + NOPE] = dq_ref[h, :, 0:NOPE].astype(BF16)
            d1 = dq_ref[h, :, NOPE:NOPE + HALF]
            d2 = dq_ref[h, :, NOPE + HALF:QK]
            q_ref[:, q0 + NOPE:q0 + NOPE + HALF] = (d1 * c + d2 * s).astype(BF16)
            q_ref[:, q0 + NOPE + HALF:q0 + QK] = (d2 * c - d1 * s).astype(BF16)
            k0 = h * (NOPE + VDIM)
            kv_ref[:, k0:k0 + NOPE] = dk_ref[h, :, 0:NOPE].astype(BF16)
            kv_ref[:, k0 + NOPE:k0 + NOPE + VDIM] = dv_ref[h].astype(BF16)
            dkr = dkr + dk_ref[h, :, NOPE:QK]
        kr_ref[...] = dkr
        dcq = lax.dot_general(q_ref[...], wq_ref[...], _NT, preferred_element_type=F32)
        dckv = lax.dot_general(kv_ref[...], wkv_ref[...], _NT, preferred_element_type=F32)
        dxq, dgq = _rms_bwd(p_ref[:, 0:ql], gq_ref[...], dcq)
        dp_ref[:, 0:ql] = dxq.astype(BF16)
        dxk, dgk = _rms_bwd(p_ref[:, ql:ql + kl], gk_ref[...], dckv)
        dp_ref[:, ql:ql + kl] = dxk.astype(BF16)
        r1 = kr_ref[:, 0:HALF]
        r2 = kr_ref[:, HALF:ROPE]
        dp_ref[:, ql + kl:ql + kl + HALF] = (r1 * c + r2 * s).astype(BF16)
        dp_ref[:, ql + kl + HALF:ql + kl + ROPE] = (r2 * c - r1 * s).astype(BF16)

        @pl.when(pl.program_id(0) == 0)
        def _():
            dgq_ref[...] = jnp.zeros_like(dgq_ref)
            dgk_ref[...] = jnp.zeros_like(dgk_ref)

        dgq_ref[...] += dgq
        dgk_ref[...] += dgk

    def row(w):
        return pl.BlockSpec((tr, w), lambda i: (i, 0))

    def vec(w):
        return pl.BlockSpec((1, w), lambda i: (0, 0))

    def heads(w):
        return pl.BlockSpec((N_HEADS, tr, w), lambda i: (0, i, 0))

    outs = pl.pallas_call(
        body, name=name, grid=(t // tr,),
        in_specs=[heads(QK), heads(QK), heads(VDIM), row(HALF), row(HALF), _resident(w_uq.shape, lambda i: (0, 0)),
                  _resident(w_ukv.shape, lambda i: (0, 0)), row(width), vec(ql), vec(kl)] + [ANY] * n,
        out_specs=[row(N_HEADS * QK), row(N_HEADS * (NOPE + VDIM)), row(width), vec(ql), vec(kl)] + [ANY] * nt,
        out_shape=[S((t, N_HEADS * QK), BF16), S((t, N_HEADS * (NOPE + VDIM)), BF16), S((t, width), BF16),
                   S((1, ql), F32), S((1, kl), F32)] + [S(tg, F32) for tg in targets],
        scratch_shapes=[pltpu.VMEM((tr, ROPE), F32)] + [pltpu.VMEM(h.shape, h.dtype) for h in halves]
        + ([pltpu.SemaphoreType.DMA((n,)), pltpu.SemaphoreType.DMA((n,)), pltpu.SemaphoreType.DMA((n, 2))] if n else []),
        compiler_params=_params(1))(dq_h, dk_h, dv_h, cos, sin, w_uq, w_ukv, proj, g_cq, g_ckv, *halves)
    return outs[0], outs[1], outs[2], outs[3], outs[4], list(outs[5:])


def _chunk_mask_t(q_start, k_start, bq, bk):
    kc = (k_start + lax.broadcasted_iota(jnp.int32, (bk, bq), 0)) // CHUNK
    qc = (q_start + lax.broadcasted_iota(jnp.int32, (bk, bq), 1)) // CHUNK
    return kc <= qc


def attention_fwd(name, q, k, v, shards=()):
    nh, t, _ = q.shape
    blk = ATT_BLOCK
    nq = t // blk
    n = len(shards)

    def body(q_ref, k_ref, v_ref, *refs):
        src = refs[:n]
        o_ref, lse_ref = refs[n:n + 2]
        dst = refs[n + 2:2 * n + 2]
        m_ref, l_ref, acc_ref, s_buf, p_buf, alpha_buf, bias_ref = refs[2 * n + 2:2 * n + 9]
        i = pl.program_id(1)
        if n:
            send_sems, recv_sems = refs[2 * n + 9:]
            _ride_along(gather_ici_copies(src, dst, send_sems, recv_sems), (pl.program_id(0), i), (nh, nq))

        @pl.when((pl.program_id(0) == 0) & (i == 0))
        def _():
            bias_ref[...] = jnp.where(_chunk_mask_t(0, 0, blk, blk), 0.0, MASK_VALUE)

        m_ref[...] = jnp.full_like(m_ref, MASK_VALUE)
        l_ref[...] = jnp.zeros_like(l_ref)
        acc_ref[...] = jnp.zeros_like(acc_ref)

        def rows(b):
            return pl.ds(pl.multiple_of(b * blk, blk), blk)

        def scores(b, slot):
            s_buf[slot] = lax.dot_general(k_ref[rows(b), :], q_ref[...], _NT, preferred_element_type=F32)

        def softmax(slot, diagonal):
            s = s_buf[slot]
            if diagonal:
                s = s + bias_ref[...]
            m_old = m_ref[...]
            m_new = jnp.maximum(m_old, jnp.max(s, axis=0, keepdims=True))
            p = jnp.exp2((s - m_new) * SCORE_SCALE_LOG2)
            alpha = jnp.exp2((m_old - m_new) * SCORE_SCALE_LOG2)
            l_ref[...] = alpha * l_ref[...] + jnp.sum(p, axis=0, keepdims=True)
            m_ref[...] = m_new
            alpha_buf[slot] = alpha
            p_buf[slot] = p.astype(BF16)

        def values(b, slot):
            pv = lax.dot_general(v_ref[rows(b), :], p_buf[slot], _TN, preferred_element_type=F32)
            acc_ref[...] = alpha_buf[slot] * acc_ref[...] + pv

        def step(t, slot):
            values(t - 2, slot)
            softmax(1 - slot, False)
            scores(t, slot)

        scores(0, 0)

        @pl.when(i == 0)
        def _():
            softmax(0, True)
            values(0, 0)

        @pl.when(i > 0)
        def _():
            scores(1, 1)
            softmax(0, False)
            steady = i - 1

            def pair(u, carry):
                step(2 + 2 * u, 0)
                step(3 + 2 * u, 1)
                return carry

            lax.fori_loop(0, steady // 2, pair, 0)

            @pl.when(steady % 2 == 1)
            def _():
                step(i, 0)

            last = i % 2
            softmax(last, True)
            values(i - 1, 1 - last)
            values(i, last)

        l = l_ref[...]
        o_ref[...] = (acc_ref[...] / l).T
        lse_ref[...] = m_ref[...] * SCORE_SCALE + jnp.log(l)

    outs = pl.pallas_call(
        body, name=name, grid=(nh, nq),
        in_specs=[pl.BlockSpec((None, blk, QK), lambda h, i: (h, i, 0)), pl.BlockSpec((None, t, QK), lambda h, i: (h, 0, 0)),
                  pl.BlockSpec((None, t, VDIM), lambda h, i: (h, 0, 0))] + [ANY] * n,
        out_specs=[pl.BlockSpec((blk, VDIM), lambda h, i: (i, h)),
                   pl.BlockSpec((None, None, 1, blk), lambda h, i: (h, i, 0, 0))] + [ANY] * n,
        out_shape=[S((t, nh * VDIM), F32), S((nh, nq, 1, blk), F32)] + [S((N_CHIPS,) + s.shape, s.dtype) for s in shards],
        scratch_shapes=[pltpu.VMEM((1, blk), F32), pltpu.VMEM((1, blk), F32), pltpu.VMEM((VDIM, blk), F32),
                        pltpu.VMEM((2, blk, blk), F32), pltpu.VMEM((2, blk, blk), BF16), pltpu.VMEM((2, 1, blk), F32),
                        pltpu.VMEM((blk, blk), F32)]
        + ([pltpu.SemaphoreType.DMA((n, 3)), pltpu.SemaphoreType.DMA((n, 3))] if n else []),
        compiler_params=_params(2))(q, k, v, *shards)
    return outs[0], outs[1], list(outs[2:])


def attention_out_bwd(name, dh, w_o, o, swap=()):
    t, d = dh.shape
    n = w_o.shape[0]
    blk = ATT_BLOCK
    m = len(swap)

    def body(dh_ref, w_ref, o_ref, *refs):
        do_ref, d_ref = refs[m:m + 2]
        if m:
            _ride_along(swap_copies(refs[:m], refs[m + 2:2 * m + 2], *refs[2 * m + 2:]), (pl.program_id(0),), (t // blk,))
        do_ref[...] = lax.dot_general(dh_ref[...].astype(BF16), w_ref[...], _NT, preferred_element_type=F32)
        for h in range(N_HEADS):
            cols = slice(h * VDIM, (h + 1) * VDIM)
            d_ref[h] = jnp.sum((do_ref[:, cols] * o_ref[:, cols]).T, axis=0, keepdims=True)

    tile = pl.BlockSpec((blk, n), lambda i: (i, 0))
    outs = pl.pallas_call(
        body, name=name, grid=(t // blk,),
        in_specs=[pl.BlockSpec((blk, d), lambda i: (i, 0)), _resident((n, d), lambda i: (0, 0)), tile] + [ANY] * m,
        out_specs=[tile, pl.BlockSpec((N_HEADS, None, 1, blk), lambda i: (0, i, 0, 0))] + [ANY] * m,
        out_shape=[S((t, n), F32), S((N_HEADS, t // blk, 1, blk), F32)] + _swap_shapes(swap),
        scratch_shapes=[pltpu.SemaphoreType.DMA((m,)), pltpu.SemaphoreType.DMA((m,))] if m else [],
        compiler_params=_params(1))(dh, w_o, o, *swap)
    return outs[0], outs[1], list(outs[2:])


def attention_bwd(name, q, k, v, do, lse, delta, parts=()):
    nh, t, _ = q.shape
    blk = ATT_BLOCK
    nq = t // blk
    n_pairs = nq * (nq + 1) // 2
    n = len(parts)
    scale = SCORE_SCALE

    def body(q_ref, k_ref, v_ref, do_ref, lse_ref, dl_ref, *refs):
        src = refs[:n]
        dq_out, dk_out, dv_out = refs[n:n + 3]
        dst = refs[n + 3:2 * n + 3]
        s_buf, dp_buf, p_buf, ds_buf, bias_ref, dq_ref, dk_ref, dv_ref = refs[2 * n + 3:2 * n + 11]
        if n:
            send_sems, recv_sems = refs[2 * n + 11:]
            _ride_along(scatter_ici_copies(src, dst, send_sems, recv_sems), (pl.program_id(0),), (nh,))

        @pl.when(pl.program_id(0) == 0)
        def _():
            bias_ref[...] = jnp.where(_chunk_mask_t(0, 0, blk, blk), 0.0, MASK_VALUE)

        dq_ref[...] = jnp.zeros_like(dq_ref)
        dk_ref[...] = jnp.zeros_like(dk_ref)
        dv_ref[...] = jnp.zeros_like(dv_ref)

        def rows(x):
            return pl.ds(pl.multiple_of(x * blk, blk), blk)

        def after(jb):
            j, b = jb
            wrap = b == nq - 1 - j
            return jnp.where(wrap, j + 1, j), jnp.where(wrap, 0, b + 1)

        def products(jb, slot):
            j, b = jb
            s_buf[slot] = lax.dot_general(k_ref[rows(j), :], q_ref[rows(j + b), :], _NT, preferred_element_type=F32)
            dp_buf[slot] = lax.dot_general(v_ref[rows(j), :], do_ref[rows(j + b), :].astype(BF16), _NT, preferred_element_type=F32)

        def softmax_bwd(jb, slot):
            j, b = jb
            s = s_buf[slot] + bias_ref[...] * (b == 0).astype(F32)
            p = jnp.exp2(s * SCORE_SCALE_LOG2 - lse_ref[j + b] * LOG2_E)
            p_buf[slot] = p.astype(BF16)
            ds_buf[slot] = (p * (dp_buf[slot] - dl_ref[j + b]) * scale).astype(BF16)

        def gradients(jb, slot):
            j, b = jb
            dv_ref[rows(j), :] += jnp.dot(p_buf[slot], do_ref[rows(j + b), :].astype(BF16), preferred_element_type=F32)
            dk_ref[rows(j), :] += jnp.dot(ds_buf[slot], q_ref[rows(j + b), :], preferred_element_type=F32)
            dq_ref[rows(j + b), :] += lax.dot_general(ds_buf[slot], k_ref[rows(j), :], _TN, preferred_element_type=F32)

        def step(state, slot):
            third, second, first = state
            gradients(third, slot)
            softmax_bwd(second, 1 - slot)
            products(first, slot)
            return second, first, after(first)

        zero = jnp.int32(0)
        pair0 = (zero, zero)
        products(pair0, 0)
        if n_pairs == 1:
            softmax_bwd(pair0, 0)
            gradients(pair0, 0)
        else:
            pair1 = after(pair0)
            products(pair1, 1)
            softmax_bwd(pair0, 0)
            steady = n_pairs - 2
            state = lax.fori_loop(0, steady // 2, lambda u, st: step(step(st, 0), 1), (pair0, pair1, after(pair1)))
            if steady % 2:
                state = step(state, 0)
            before_last, last_pair, _ = state
            last = (n_pairs - 1) % 2
            softmax_bwd(last_pair, last)
            gradients(before_last, 1 - last)
            gradients(last_pair, last)
        dq_out[...] = dq_ref[...].astype(BF16)
        dk_out[...] = dk_ref[...].astype(BF16)
        dv_out[...] = dv_ref[...].astype(BF16)

    head = lambda w: pl.BlockSpec((None, t, w), lambda h: (h, 0, 0))
    stats = pl.BlockSpec((None, nq, 1, blk), lambda h: (h, 0, 0, 0))
    outs = pl.pallas_call(
        body, name=name, grid=(nh,),
        in_specs=[head(QK), head(QK), head(VDIM), pl.BlockSpec((t, VDIM), lambda h: (0, h)), stats, stats] + [ANY] * n,
        out_specs=[head(QK), head(QK), head(VDIM)] + [ANY] * n,
        out_shape=[S((nh, t, QK), BF16), S((nh, t, QK), BF16), S((nh, t, VDIM), BF16)] + [S(p.shape, p.dtype) for p in parts],
        scratch_shapes=[pltpu.VMEM((2, blk, blk), F32), pltpu.VMEM((2, blk, blk), F32), pltpu.VMEM((2, blk, blk), BF16),
                        pltpu.VMEM((2, blk, blk), BF16), pltpu.VMEM((blk, blk), F32),
                        pltpu.VMEM((t, QK), F32), pltpu.VMEM((t, QK), F32), pltpu.VMEM((t, VDIM), F32)]
        + ([pltpu.SemaphoreType.DMA((n, 3)), pltpu.SemaphoreType.DMA((n, 3))] if n else []),
        compiler_params=_params(1, VMEM_LIMIT_WHOLE_HEAD))(q, k, v, do, lse, delta, *parts)
    return outs[0], outs[1], outs[2], list(outs[3:])


def _shift_down(u, s):
    rows = lax.broadcasted_iota(jnp.int32, u.shape, 0)
    return jnp.where(rows >= s, pltpu.roll(u, s, 0), 0.0)


def _shift_up(u, s):
    n = u.shape[0]
    rows = lax.broadcasted_iota(jnp.int32, u.shape, 0)
    return jnp.where(rows < n - s, pltpu.roll(u, n - s, 0), 0.0)


def _conv_specs(t, d, lanes):
    slab = lambda part: pl.BlockSpec((None, t, lanes), lambda j, part=part: (part, 0, j))
    return slab, pl.BlockSpec((3, lanes), lambda j: (0, j)), pl.BlockSpec((t, lanes), lambda j: (0, j))


def conv_bwd(name, bcx, w, dy):
    _, t, d = bcx.shape
    lanes = _tile(d, 128, 128)
    slab, w_spec, col = _conv_specs(t, d, lanes)

    def body(b_ref, c_ref, x_ref, w_ref, dy_ref, d_ref, dw_ref):
        c = c_ref[...].astype(F32)
        x = x_ref[...].astype(F32)
        dyv = dy_ref[...]
        u = c * x
        u1 = _shift_down(u, 1)
        u2 = _shift_down(u, 2)
        w0, w1, w2 = w_ref[0:1, :], w_ref[1:2, :], w_ref[2:3, :]
        d_ref[0] = (dyv * (w0 * u2 + w1 * u1 + w2 * u)).astype(BF16)
        duc = dyv * b_ref[...].astype(F32)
        dw_ref[0:1, :] = jnp.sum(duc * u2, axis=0, keepdims=True)
        dw_ref[1:2, :] = jnp.sum(duc * u1, axis=0, keepdims=True)
        dw_ref[2:3, :] = jnp.sum(duc * u, axis=0, keepdims=True)
        du = w2 * duc + w1 * _shift_up(duc, 1) + w0 * _shift_up(duc, 2)
        d_ref[1] = (du * x).astype(BF16)
        d_ref[2] = (du * c).astype(BF16)

    return pl.pallas_call(
        body, name=name, grid=(d // lanes,), in_specs=[slab(0), slab(1), slab(2), w_spec, col],
        out_specs=[pl.BlockSpec((3, t, lanes), lambda j: (0, 0, j)), w_spec], out_shape=[S((3, t, d), BF16), S((3, d), F32)],
        compiler_params=_params(1))(bcx, bcx, bcx, w, dy)


def _adamw_update(w, g, m, v):
    m_new = ADAM_B1 * m + (1.0 - ADAM_B1) * g
    v_new = ADAM_B2 * v + (1.0 - ADAM_B2) * (g * g)
    m_hat = m_new / (1.0 - ADAM_B1 ** ADAM_STEP)
    v_hat = v_new / (1.0 - ADAM_B2 ** ADAM_STEP)
    return -ADAM_LR * (m_hat / (jnp.sqrt(v_hat) + ADAM_EPS) + ADAM_WD * w), m_new, v_new


def adamw(name, w, g, m, v):
    r, c = w.shape
    tr = _tile(r, 512)

    def body(w_ref, g_ref, m_ref, v_ref, d_ref, mo_ref, vo_ref):
        d_ref[...], mo_ref[...], vo_ref[...] = _adamw_update(w_ref[...], g_ref[...], m_ref[...], v_ref[...])

    blk = pl.BlockSpec((tr, c), lambda i: (i, 0))
    return pl.pallas_call(
        body, name=name, grid=(r // tr,), in_specs=[blk] * 4, out_specs=[blk] * 3, out_shape=[S((r, c), F32)] * 3,
        compiler_params=_params(1))(w, g, m, v)


def adamw_swapped(name, wt, g, mt, vt):
    nl, c, r = wt.shape
    tr = _tile(r, 512, 128)
    nr = r // tr

    def body(w_ref, g_ref, m_ref, v_ref, go_ref, d_ref, mo_ref, vo_ref):
        gt = g_ref[...].T
        go_ref[...] = gt
        d_ref[...], mo_ref[...], vo_ref[...] = _adamw_update(w_ref[...], gt, m_ref[...], v_ref[...])

    swapped = pl.BlockSpec((None, c, tr), lambda l, i: (l, 0, i))
    return pl.pallas_call(
        body, name=name, grid=(nl, nr),
        in_specs=[swapped, pl.BlockSpec((tr, c), lambda l, i: (l * nr + i, 0)), swapped, swapped],
        out_specs=[swapped] * 4, out_shape=[S((nl, c, r), F32)] * 4, compiler_params=_params(2))(wt, g, mt, vt)


def _place():
    x, y, c = lax.axis_index("x"), lax.axis_index("y"), lax.axis_index("c")
    other_chips = [(1 - x, y), (x, 1 - y), (1 - x, 1 - y)]
    return x, y, c, other_chips


def _half(c, rows):
    return pl.ds(pl.multiple_of(c * (rows // 2), 16), rows // 2)


def gather_weight_shards(shards, small, h, gain):
    n, ns = len(shards), len(small)
    t, d = h.shape
    tr = _tile(t, 512)
    steps = t // tr

    def body(h_ref, g_ref, *refs):
        src = refs[:n]
        small_src = refs[n:n + ns]
        a_ref = refs[n + ns]
        dst = refs[n + ns + 1:2 * n + ns + 1]
        small_dst = refs[2 * n + ns + 1:2 * (n + ns) + 1]
        send_sems, recv_sems, small_send, small_recv, local_sems = refs[2 * (n + ns) + 1:2 * (n + ns) + 6]
        stages = refs[2 * (n + ns) + 6:]
        x, y, c, chips = _place()
        me = 2 * x + y
        sibling = (x, y, 1 - c)
        own = [(s_ref, stages[i], d_ref.at[me], local_sems.at[i])
               for i, (s_ref, d_ref) in enumerate(zip(list(src) + list(small_src), list(dst) + list(small_dst)))]
        small_pairs = []
        for i in range(ns):
            for j, (px, py) in enumerate(chips):
                def whole(slot):
                    return pltpu.make_async_remote_copy(
                        src_ref=small_src[i], dst_ref=small_dst[i].at[slot], send_sem=small_send.at[i, j],
                        recv_sem=small_recv.at[i, j], device_id=(px, py, c), device_id_type=MESH)
                small_pairs.append((whole(me), whole(2 * px + py)))

        def copy(i, slot, half_of, sem, to, from_input=False):
            rows = _half(half_of, src[i].shape[0])
            return pltpu.make_async_remote_copy(
                src_ref=src[i].at[rows] if from_input else dst[i].at[slot, rows], dst_ref=dst[i].at[slot, rows],
                send_sem=send_sems.at[i, sem], recv_sem=recv_sems.at[i, sem], device_id=to, device_id_type=MESH)

        over_ici = [copy(i, me, c, j, (*chip, c), from_input=True) for i in range(n) for j, chip in enumerate(chips)]
        handed_on = [copy(i, 2 * px + py, c, 3 + j, sibling) for i in range(n) for j, (px, py) in enumerate(chips)]

        @pl.when(pl.program_id(0) == 0)
        def _():
            for s_ref, stage, _, sems in own:
                pltpu.make_async_copy(s_ref, stage, sems.at[0]).start()
            for outgoing, _ in small_pairs:
                outgoing.start()
            for cp in over_ici:
                cp.start()

        @pl.when(pl.program_id(0) == steps - 1)
        def _():
            k = 0
            for i in range(n):
                for j, (px, py) in enumerate(chips):
                    copy(i, 2 * px + py, c, j, sibling).wait_recv()
                    handed_on[k].start()
                    k += 1
            for i in range(n):
                for j, (px, py) in enumerate(chips):
                    copy(i, 2 * px + py, 1 - c, 3 + j, sibling).wait_recv()
            for cp in over_ici + handed_on:
                cp.wait_send()
            for _, incoming in small_pairs:
                incoming.wait_recv()
            for outgoing, _ in small_pairs:
                outgoing.wait_send()
            _place_locally(own)

        xv = h_ref[...]
        a_ref[...] = (xv * _rstd(xv) * g_ref[...]).astype(BF16)

    everything = list(shards) + list(small)
    outs = pl.pallas_call(
        body, name="gather_weight_shards", grid=(steps,),
        in_specs=[pl.BlockSpec((tr, d), lambda i: (i, 0)), pl.BlockSpec((1, d), lambda i: (0, 0))] + [ANY] * (n + ns),
        out_specs=[pl.BlockSpec((tr, d), lambda i: (i, 0))] + [ANY] * (n + ns),
        out_shape=[S((t, d), BF16)] + [S((N_CHIPS,) + s.shape, s.dtype) for s in everything],
        scratch_shapes=[pltpu.SemaphoreType.DMA((n, 6)), pltpu.SemaphoreType.DMA((n, 6)),
                        pltpu.SemaphoreType.DMA((max(ns, 1), 3)), pltpu.SemaphoreType.DMA((max(ns, 1), 3)),
                        pltpu.SemaphoreType.DMA((n + ns, 2))] + [pltpu.VMEM(s.shape, s.dtype) for s in everything],
        compiler_params=_params(1))(h, gain, *shards, *small)
    return outs[0], list(outs[1:n + 1]), list(outs[n + 1:])


def gather_ici_copies(src, dst, send_sems, recv_sems):
    x, y, c, chips = _place()
    me = 2 * x + y
    pairs = []
    for i in range(len(src)):
        rows = _half(c, src[i].shape[0])
        for j, (px, py) in enumerate(chips):
            def copy(slot):
                return pltpu.make_async_remote_copy(
                    src_ref=src[i].at[rows], dst_ref=dst[i].at[slot, rows], send_sem=send_sems.at[i, j],
                    recv_sem=recv_sems.at[i, j], device_id=(px, py, c), device_id_type=MESH)
            pairs.append((copy(me), copy(2 * px + py)))
    return pairs


def scatter_ici_copies(src, dst, send_sems, recv_sems):
    x, y, c, chips = _place()
    me = 2 * x + y
    pairs = []
    for i in range(len(src)):
        for j, (px, py) in enumerate(chips):
            def copy(from_slot, to_slot):
                return pltpu.make_async_remote_copy(
                    src_ref=src[i].at[from_slot], dst_ref=dst[i].at[to_slot], send_sem=send_sems.at[i, j],
                    recv_sem=recv_sems.at[i, j], device_id=(px, py, c), device_id_type=MESH)
            pairs.append((copy(2 * px + py, me), copy(me, 2 * px + py)))
    return pairs


def _ride_along(pairs, grid_ids, grid_sizes, local=()):
    first = grid_ids[0] == 0
    last = grid_ids[0] == grid_sizes[0] - 1
    for g, size in zip(grid_ids[1:], grid_sizes[1:]):
        first = first & (g == 0)
        last = last & (g == size - 1)

    @pl.when(first)
    def _():
        for outgoing, _ in pairs:
            outgoing.start()
        for src, stage, _, sems in local:
            pltpu.make_async_copy(src, stage, sems.at[0]).start()

    @pl.when(last)
    def _():
        placing = _start_placing(local)
        for _, incoming in pairs:
            incoming.wait_recv()
        for outgoing, _ in pairs:
            outgoing.wait_send()
        for cp in placing:
            cp.wait()


def _start_placing(local):
    for src, stage, _, sems in local:
        pltpu.make_async_copy(src, stage, sems.at[0]).wait()
    placing = [pltpu.make_async_copy(stage, dst, sems.at[1]) for _, stage, dst, sems in local]
    for cp in placing:
        cp.start()
    return placing


def _place_locally(local):
    for cp in _start_placing(local):
        cp.wait()


def forward_copies(src, dst, send_sems, recv_sems):
    x, y, c, chips = _place()
    pairs = []
    for i in range(len(src)):
        for j, (px, py) in enumerate(chips):
            def copy(half_of):
                rows = _half(half_of, src[i].shape[1])
                return pltpu.make_async_remote_copy(
                    src_ref=src[i].at[2 * px + py, rows], dst_ref=dst[i].at[2 * px + py, rows], send_sem=send_sems.at[i, j],
                    recv_sem=recv_sems.at[i, j], device_id=(x, y, 1 - c), device_id_type=MESH)
            pairs.append((copy(c), copy(1 - c)))
    return pairs


def attention_out_proj(name, attn, w_o, resid, next_gain, arriving, own):
    t, kdim = attn.shape
    n = w_o.shape[1]
    tm = _tile(t, 512)
    m = len(arriving)

    def body(x_ref, w_ref, r_ref, g_ref, *refs):
        src, own_refs = refs[:m], refs[m:2 * m]
        h_ref, a_ref = refs[2 * m:2 * m + 2]
        dst = refs[2 * m + 2:3 * m + 2]
        stages = refs[3 * m + 2:4 * m + 2]
        send_sems, recv_sems, local_sems = refs[4 * m + 2:]
        me = 2 * lax.axis_index("x") + lax.axis_index("y")
        placed = [(own_refs[i], stages[i], dst[i].at[me], local_sems.at[i]) for i in range(m)]
        _ride_along(forward_copies(src, dst, send_sems, recv_sems), (pl.program_id(0),), (t // tm,), placed)
        y = r_ref[...] + jnp.dot(x_ref[...].astype(BF16), w_ref[...], preferred_element_type=F32)
        h_ref[...] = y
        a_ref[...] = (y * _rstd(y) * g_ref[...]).astype(BF16)

    row = pl.BlockSpec((tm, n), lambda i: (i, 0))
    outs = pl.pallas_call(
        body, name=name, grid=(t // tm,),
        in_specs=[pl.BlockSpec((tm, kdim), lambda i: (i, 0)), _resident((kdim, n), lambda i: (0, 0)), row,
                  pl.BlockSpec((1, n), lambda i: (0, 0))] + [ANY] * (2 * m),
        out_specs=[row, row] + [ANY] * m, out_shape=[S((t, n), F32), S((t, n), BF16)] + [S(g.shape, g.dtype) for g in arriving],
        input_output_aliases={4 + i: 2 + i for i in range(m)},
        scratch_shapes=[pltpu.VMEM(o.shape, o.dtype) for o in own]
        + [pltpu.SemaphoreType.DMA((m, 3)), pltpu.SemaphoreType.DMA((m, 3)), pltpu.SemaphoreType.DMA((m, 2))],
        compiler_params=_params(1))(attn, w_o, resid, next_gain, *arriving, *own)
    return outs[0], outs[1], list(outs[2:])


def swap_copies(src, dst, send_sems, recv_sems):
    x, y, c, _ = _place()
    pairs = []
    for i in range(len(src)):
        cp = pltpu.make_async_remote_copy(
            src_ref=src[i].at[:, _half(1 - c, src[i].shape[1]), :], dst_ref=dst[i], send_sem=send_sems.at[i],
            recv_sem=recv_sems.at[i], device_id=(x, y, 1 - c), device_id_type=MESH)
        pairs.append((cp, cp))
    return pairs


def _swap_shapes(grads):
    return [S((g.shape[0], g.shape[1] // 2, g.shape[2]), g.dtype) for g in grads]


def sibling_swap_halves(name, grads):
    n = len(grads)

    def body(*refs):
        pairs = swap_copies(refs[:n], refs[n:2 * n], *refs[2 * n:])
        for outgoing, _ in pairs:
            outgoing.start()
        for _, incoming in pairs:
            incoming.wait_recv()
        for outgoing, _ in pairs:
            outgoing.wait_send()

    return pl.pallas_call(
        body, name=name, in_specs=[ANY] * n, out_specs=[ANY] * n, out_shape=_swap_shapes(grads),
        scratch_shapes=[pltpu.SemaphoreType.DMA((n,)), pltpu.SemaphoreType.DMA((n,))],
    )(*grads)


def add_halves(name, g, rx):
    _, r, cdim = g.shape
    r2 = r // 2
    tr = _tile(r2, 512, 16)
    nb = r2 // tr

    def body(lo_ref, hi_ref, rx_ref, o_ref):
        mine = jnp.where(lax.axis_index("c") == 0, lo_ref[...], hi_ref[...])
        o_ref[...] = (mine.astype(F32) + rx_ref[...].astype(F32)).astype(BF16)

    half = pl.BlockSpec((None, tr, cdim), lambda k, i: (k, i, 0))
    return pl.pallas_call(
        body, name=name, grid=(N_CHIPS, nb),
        in_specs=[half, pl.BlockSpec((None, tr, cdim), lambda k, i: (k, nb + i, 0)), half],
        out_specs=half, out_shape=S((N_CHIPS, r2, cdim), BF16), compiler_params=_params(2))(g, g, rx)


def sum_chips(name, arrived, mine):
    _, r2, cdim = arrived.shape
    tr = _tile(r2, 512, 16)

    def body(a_ref, m_ref, o_ref):
        me = 2 * lax.axis_index("x") + lax.axis_index("y")
        acc = jnp.zeros((tr, cdim), F32)
        for k in range(N_CHIPS):
            acc = acc + jnp.where(me == k, m_ref[k], a_ref[k]).astype(F32)
        o_ref[...] = acc

    slots = pl.BlockSpec((N_CHIPS, tr, cdim), lambda i: (0, i, 0))
    return pl.pallas_call(
        body, name=name, grid=(r2 // tr,), in_specs=[slots, slots],
        out_specs=pl.BlockSpec((tr, cdim), lambda i: (i, 0)), out_shape=S((r2, cdim), F32), compiler_params=_params(1))(arrived, mine)


def join_copies(src, dst, where, send_sems, recv_sems):
    x, y, c, _ = _place()
    pairs = []
    for i in range(len(src)):
        def copy(half_of):
            r2 = src[i].shape[0]
            rows = pl.ds(pl.multiple_of(where[i][1] + half_of * r2, 8), r2)
            return pltpu.make_async_remote_copy(
                src_ref=src[i], dst_ref=dst[where[i][0]].at[rows], send_sem=send_sems.at[i],
                recv_sem=recv_sems.at[i], device_id=(x, y, 1 - c), device_id_type=MESH)
        pairs.append((copy(c), copy(1 - c)))
    return pairs


def sibling_join_halves(name, halves, targets, where):
    n = len(halves)

    def body(*refs):
        src, dst = refs[:n], refs[n:n + len(targets)]
        send_sems, recv_sems, local_sems = refs[n + len(targets):n + len(targets) + 3]
        stages = refs[n + len(targets) + 3:]
        c = lax.axis_index("c")
        own = [(src[i], stages[i],
                dst[where[i][0]].at[pl.ds(pl.multiple_of(where[i][1] + c * src[i].shape[0], 8), src[i].shape[0])],
                local_sems.at[i]) for i in range(n)]
        for s_ref, stage, _, sems in own:
            pltpu.make_async_copy(s_ref, stage, sems.at[0]).start()
        pairs = join_copies(src, dst, where, send_sems, recv_sems)
        for outgoing, _ in pairs:
            outgoing.start()
        for _, incoming in pairs:
            incoming.wait_recv()
        for outgoing, _ in pairs:
            outgoing.wait_send()
        _place_locally(own)

    return list(pl.pallas_call(
        body, name=name, in_specs=[ANY] * n, out_specs=[ANY] * len(targets), out_shape=[S(tg, F32) for tg in targets],
        scratch_shapes=[pltpu.SemaphoreType.DMA((n,)), pltpu.SemaphoreType.DMA((n,)), pltpu.SemaphoreType.DMA((n, 2))]
        + [pltpu.VMEM(h.shape, h.dtype) for h in halves],
    )(*halves))


def all_reduce_small(name, packed):
    rows, width = packed.shape

    def body(x_ref, o_ref, gathered, send_sems, recv_sems):
        x, y, c, _ = _place()
        me = 4 * x + 2 * y + c
        gathered[me] = x_ref[...]
        flips = [(fx, fy, fc) for fx in (0, 1) for fy in (0, 1) for fc in (0, 1)][1:]

        def copy(r, slot, to):
            return pltpu.make_async_remote_copy(
                src_ref=x_ref, dst_ref=gathered.at[slot], send_sem=send_sems.at[r], recv_sem=recv_sems.at[r],
                device_id=to, device_id_type=MESH)

        def peer(f):
            return (x ^ f[0], y ^ f[1], c ^ f[2])

        sent = [copy(r, me, peer(f)) for r, f in enumerate(flips)]
        for cp in sent:
            cp.start()
        for r, f in enumerate(flips):
            px, py, pc = peer(f)
            copy(r, 4 * px + 2 * py + pc, peer(f)).wait_recv()
        for cp in sent:
            cp.wait_send()
        acc = gathered[0]
        for k in range(1, N_DEV):
            acc = acc + gathered[k]
        o_ref[...] = acc

    vmem = pl.BlockSpec(memory_space=pltpu.VMEM)
    return pl.pallas_call(
        body, name=name, in_specs=[vmem], out_specs=vmem, out_shape=S((rows, width), F32),
        scratch_shapes=[pltpu.VMEM((N_DEV, rows, width), F32), pltpu.SemaphoreType.DMA((N_DEV - 1,)),
                        pltpu.SemaphoreType.DMA((N_DEV - 1,))],
    )(packed)


def _rope_tables(positions):
    inv_freq = 1.0 / (ROPE_THETA ** (jnp.arange(0, ROPE, 2, dtype=F32) / ROPE))
    ang = positions.astype(F32)[:, None] * inv_freq
    return jnp.cos(ang), jnp.sin(ang)


def _unstack_cols(w):
    k4, k, n4 = w.shape
    return jnp.transpose(w, (1, 0, 2)).reshape(k, k4 * n4)


def _stack_cols(w):
    k, n = w.shape
    return jnp.transpose(w.reshape(k, N_CHIPS, n // N_CHIPS), (1, 0, 2))


def kernel(x, positions, mla_norm, mla_w_in, mla_g_cq, mla_g_ckv, mla_w_uq, mla_w_ukv, mla_w_o, conv_norm, conv_w_in, conv_w, conv_w_out, ffn_norm, ffn_w_gate, ffn_w_up, ffn_w_down, final_norm, loss_target, m_mla_norm, m_mla_w_in, m_mla_g_cq, m_mla_g_ckv, m_mla_w_uq, m_mla_w_ukv, m_mla_w_o, m_conv_norm, m_conv_w_in, m_conv_w, m_conv_w_out, m_ffn_norm, m_ffn_w_gate, m_ffn_w_up, m_ffn_w_down, m_final_norm, v_mla_norm, v_mla_w_in, v_mla_g_cq, v_mla_g_ckv, v_mla_w_uq, v_mla_w_ukv, v_mla_w_o, v_conv_norm, v_conv_w_in, v_conv_w, v_conv_w_out, v_ffn_norm, v_ffn_w_gate, v_ffn_w_up, v_ffn_w_down, v_final_norm):
    weights = dict(mla_norm=mla_norm, mla_w_in=mla_w_in, mla_g_cq=mla_g_cq, mla_g_ckv=mla_g_ckv, mla_w_uq=mla_w_uq,
                   mla_w_ukv=mla_w_ukv, mla_w_o=mla_w_o, conv_norm=conv_norm, conv_w_in=conv_w_in, conv_w=conv_w,
                   conv_w_out=conv_w_out, ffn_norm=ffn_norm, ffn_w_gate=ffn_w_gate, ffn_w_up=ffn_w_up,
                   ffn_w_down=ffn_w_down, final_norm=final_norm)
    m_in = dict(mla_norm=m_mla_norm, mla_w_in=m_mla_w_in, mla_g_cq=m_mla_g_cq, mla_g_ckv=m_mla_g_ckv, mla_w_uq=m_mla_w_uq,
                mla_w_ukv=m_mla_w_ukv, mla_w_o=m_mla_w_o, conv_norm=m_conv_norm, conv_w_in=m_conv_w_in, conv_w=m_conv_w,
                conv_w_out=m_conv_w_out, ffn_norm=m_ffn_norm, ffn_w_gate=m_ffn_w_gate, ffn_w_up=m_ffn_w_up,
                ffn_w_down=m_ffn_w_down, final_norm=m_final_norm)
    v_in = dict(mla_norm=v_mla_norm, mla_w_in=v_mla_w_in, mla_g_cq=v_mla_g_cq, mla_g_ckv=v_mla_g_ckv, mla_w_uq=v_mla_w_uq,
                mla_w_ukv=v_mla_w_ukv, mla_w_o=v_mla_w_o, conv_norm=v_conv_norm, conv_w_in=v_conv_w_in, conv_w=v_conv_w,
                conv_w_out=v_conv_w_out, ffn_norm=v_ffn_norm, ffn_w_gate=v_ffn_w_gate, ffn_w_up=v_ffn_w_up,
                ffn_w_down=v_ffn_w_down, final_norm=v_final_norm)
    big = ["mla_w_in", "mla_w_uq", "mla_w_ukv", "mla_w_o", "conv_w_in", "conv_w_out", "ffn_w_gate", "ffn_w_up", "ffn_w_down"]
    order = list(weights)

    t, d = x.shape[1], x.shape[2]
    h0 = x.reshape(t, d)
    target = loss_target.reshape(t, d)
    cos, sin = _rope_tables(positions.reshape(t))

    def rows2d(a):
        return a.reshape(-1, a.shape[-1])

    first, later = big[:4], big[4:]
    shards = {n: rows2d(weights[n]).astype(BF16) for n in big}
    d4 = d // N_CHIPS
    a0, first_gathered, (conv_norm_slots, conv_w_slots) = gather_weight_shards(
        [shards[n] for n in first], [conv_norm.reshape(1, d4), conv_w.reshape(3, d4)], h0, mla_norm)
    gathered = dict(zip(first, first_gathered))
    conv_norm_full = conv_norm_slots.reshape(1, d)
    conv_w_full = jnp.transpose(conv_w_slots, (1, 0, 2)).reshape(3, d)
    w_in = gathered["mla_w_in"].reshape(-1, gathered["mla_w_in"].shape[-1])
    w_uq = _unstack_cols(gathered["mla_w_uq"])
    w_ukv = _unstack_cols(gathered["mla_w_ukv"])
    w_o = gathered["mla_w_o"].reshape(-1, d)

    chip = 2 * lax.axis_index("x") + lax.axis_index("y")

    def pack_rows(rows):
        idx = lax.broadcasted_iota(jnp.int32, (SMALL_ROWS, d), 0)
        out = jnp.zeros((SMALL_ROWS, d), F32)
        for r, row in enumerate(rows):
            out = out + jnp.where(idx == r, row, 0.0)
        return out


    proj, cq, ckv, kr = mla_in_proj("mla_in_proj", a0, w_in, mla_g_cq, mla_g_ckv, cos, sin)
    qh, kh, vh, conv_arriving = qkv_heads("qkv_heads", cq, ckv, w_uq, w_ukv, kr, cos, sin, [shards[n] for n in later[:2]])
    attn, lse, ffn_arriving = attention_fwd("attention_fwd", qh, kh, vh, [shards[n] for n in later[2:]])
    h1, a1, handed = attention_out_proj("mla_out_proj", attn, w_o, h0, ffn_norm[0:1], conv_arriving + ffn_arriving,
                                        [shards[n] for n in later])
    gathered.update(zip(later, handed))
    cw_in = _unstack_cols(gathered["conv_w_in"])
    cw_out = gathered["conv_w_out"].reshape(-1, d)
    wg_all, wu_all, wd_all = gathered["ffn_w_gate"], gathered["ffn_w_up"], gathered["ffn_w_down"]

    def ffn_forward(tag, h, a, layer, next_gain):
        g, u, z = ffn_up(f"ffn{tag}_up", a, wg_all, wu_all, layer)
        return g, u, z, ffn_down(f"ffn{tag}_down", z, wd_all, layer, h, next_gain)

    g0, u0, z0, (h2, a2) = ffn_forward(0, h1, a1, 0, conv_norm_full)
    bcx, yc = conv_in_proj("conv_in_proj", a2, cw_in, conv_w_full)
    h3, a3 = linear("conv_out_proj", yc, cw_out, F32, resid=h2, next_gain=ffn_norm[1:2])
    g1, u1, z1 = ffn_up("ffn1_up", a3, wg_all, wu_all, 1)
    dh4, d_final_norm, loss_local = ffn_down_loss("ffn1_down_loss", z1, wd_all, 1, h3, final_norm.reshape(1, d), target)

    def ffn_backward(tag, dh, h, layer, a, g, u, z, swap=()):
        dg, du, swapped = ffn_bwd_hidden(f"ffn{tag}_bwd_hidden", dh, wd_all, layer, g, u, swap)
        d_wd = ffn_wgrad_down(f"ffn{tag}_wgrad_down", z, dh)
        dh_prev, d_norm = ffn_bwd_input(f"ffn{tag}_bwd_input", dg, du, wg_all, wu_all, layer, h, ffn_norm[layer:layer + 1], dh)
        d_wg = ffn_wgrad_up(f"ffn{tag}_wgrad_gate", a, dg)
        d_wu = ffn_wgrad_up(f"ffn{tag}_wgrad_up", a, du)
        return dh_prev, d_norm, [d_wg, d_wu, d_wd], swapped

    def pair_sums(tag, local, from_sibling):
        return [add_halves(f"pair_sum_{tag}{i}", g, r) for i, (g, r) in enumerate(zip(local, from_sibling))]

    def sum_from_chips(tag, pairs, arrived):
        return [sum_chips(f"chip_sum_{tag}{i}", a, p) for i, (a, p) in enumerate(zip(arrived, pairs))]

    def shard_shape(n):
        return rows2d(weights[n]).shape

    dh3, d_ffn_norm1, ffn1_grads, _ = ffn_backward(1, dh4, h3, 1, a3, g1, u1, z1)

    dyc = linear_nt("conv_out_bwd_input", dh3, cw_out, F32)
    d_cw_out = wgrad("conv_out_wgrad", yc, dh3)
    dbcx, d_conv_w = conv_bwd("conv_bwd", bcx, conv_w_full, dyc)
    dh2, d_conv_norm = conv_in_bwd_input("conv_in_bwd_input", dbcx, cw_in, h2, conv_norm_full, dh3)
    d_cw_in = conv_in_wgrad("conv_in_wgrad", a2, dbcx)

    second = [d_cw_in, d_cw_out.reshape(N_CHIPS, -1, d)] + ffn1_grads
    dh1, d_ffn_norm0, ffn0_grads, second_swapped = ffn_backward(0, dh2, h1, 0, a1, g0, u0, z0, second)
    d_w_o = wgrad("mla_out_wgrad", attn, dh1)
    first_part = ffn0_grads + [d_w_o.reshape(N_CHIPS, -1, d)]
    d_attn, delta, first_swapped = attention_out_bwd("mla_out_bwd_input", dh1, w_o, attn, first_part)
    rest_pairs = pair_sums("rest", second + first_part, second_swapped + first_swapped)
    dqh, dkh, dvh, rest_arrived = attention_bwd("attention_bwd", qh, kh, vh, d_attn, lse, delta, rest_pairs)
    rd, rf = ffn0_grads[0].shape[1], ffn0_grads[2].shape[1]
    rest_where = [(0, 0), (1, 0), (2, rd), (3, rd), (4, rf), (2, 0), (3, 0), (4, 0), (5, 0)]
    rest_names = later + ["mla_w_o"]
    dq, dkv, dproj, d_g_cq, d_g_ckv, rest_grads = qkv_heads_bwd(
        "qkv_heads_bwd", dqh, dkh, dvh, cos, sin, w_uq, w_ukv, proj, mla_g_cq, mla_g_ckv,
        sum_from_chips("rest", rest_pairs, rest_arrived), [shard_shape(n) for n in rest_names], rest_where)
    grads = dict(zip(rest_names, rest_grads))
    d_w_uq = wgrad("mla_q_up_wgrad", cq, dq)
    d_w_ukv = wgrad("mla_kv_up_wgrad", ckv, dkv)
    d_w_in = wgrad("mla_in_wgrad", a0, dproj)
    mla_local = [d_w_in.reshape(N_CHIPS, -1, d_w_in.shape[-1]), _stack_cols(d_w_uq), _stack_cols(d_w_ukv)]
    mla_pairs = pair_sums("mla", mla_local, sibling_swap_halves("sibling_swap_mla", mla_local))
    grad_x, d_mla_norm, mla_arrived = linear_nt_norm_bwd("mla_in_bwd_input", dproj, w_in, h0, mla_norm, dh1, mla_pairs)

    grads.update(zip(first[:3], sibling_join_halves("sibling_join_mla", sum_from_chips("mla", mla_pairs, mla_arrived),
                                                    [shard_shape(n) for n in first[:3]], [(i, 0) for i in range(3)])))

    def pad_row(v):
        return jnp.pad(v, ((0, 0), (0, d - v.shape[1])))

    small = all_reduce_small("all_reduce_small_grads", pack_rows([
        d_mla_norm, pad_row(d_g_cq), pad_row(d_g_ckv), d_ffn_norm0, d_ffn_norm1, d_final_norm, d_conv_norm,
        d_conv_w[0:1], d_conv_w[1:2], d_conv_w[2:3], jnp.broadcast_to(loss_local, (1, d))]))
    loss = small[10, 0]
    grads["mla_norm"] = small[0:1]
    grads["mla_g_cq"] = small[1:2, :mla_g_cq.shape[1]]
    grads["mla_g_ckv"] = small[2:3, :mla_g_ckv.shape[1]]
    grads["ffn_norm"] = small[3:5]
    grads["final_norm"] = small[5:6]
    grads["conv_norm"] = lax.dynamic_slice(small[6:7], (0, chip * d4), (1, d4))
    grads["conv_w"] = lax.dynamic_slice(small[7:10], (0, chip * d4), (3, d4))

    outs_g, outs_d, outs_m, outs_v = [], [], [], []
    for n in order:
        w = weights[n]
        if w.ndim == 3 and w.shape[2] % 128 and w.shape[1] % 128 == 0:
            results = adamw_swapped(f"adamw_{n}", jnp.swapaxes(w, 1, 2), grads[n].reshape(-1, w.shape[2]),
                                    jnp.swapaxes(m_in[n], 1, 2), jnp.swapaxes(v_in[n], 1, 2))
            grad_w, delta_w, new_m, new_v = [jnp.swapaxes(o, 1, 2) for o in results]
        else:
            delta_w, new_m, new_v = adamw(f"adamw_{n}", rows2d(w), grads[n].reshape(rows2d(w).shape), rows2d(m_in[n]), rows2d(v_in[n]))
            grad_w = grads[n]
        outs_g.append(grad_w.reshape(w.shape))
        outs_d.append(delta_w.reshape(w.shape))
        outs_m.append(new_m.reshape(w.shape))
        outs_v.append(new_v.reshape(w.shape))
    return (loss, grad_x.reshape(x.shape), *outs_g, *outs_d, *outs_m, *outs_v)
```

```python
import math

import jax
import jax.numpy as jnp
from jax import lax
from jax.experimental import pallas as pl
from jax.experimental.pallas import tpu as pltpu

F32 = jnp.float32
BF16 = jnp.bfloat16
S = jax.ShapeDtypeStruct

N_HEADS = 8
NOPE = 128
ROPE = 64
HALF = ROPE // 2
VDIM = 128
QK = NOPE + ROPE
CHUNK = 64
ROPE_THETA = 10000.0
RMS_EPS = 1e-6
ADAM_LR = 0.001
ADAM_B1 = 0.9
ADAM_B2 = 0.999
ADAM_EPS = 1e-08
ADAM_WD = 0.01
ADAM_STEP = 10

N_CHIPS = 4
N_DEV = 8
MASK_VALUE = -1e30
SCORE_SCALE = 1.0 / math.sqrt(QK)
LOG2_E = math.log2(math.e)
SCORE_SCALE_LOG2 = SCORE_SCALE * LOG2_E
VMEM_LIMIT = 48 * 1024 * 1024
VMEM_LIMIT_WHOLE_HEAD = 58 * 1024 * 1024
ATT_BLOCK = 512
CONV_SAVED_DTYPE = jnp.bfloat16
SMALL_ROWS = 16

_NN = (((1,), (0,)), ((), ()))
_NT = (((1,), (1,)), ((), ()))
_TN = (((0,), (0,)), ((), ()))
MESH = pl.DeviceIdType.MESH
ANY = pl.BlockSpec(memory_space=pl.ANY)


def _params(n_axes, vmem_limit=VMEM_LIMIT):
    return pltpu.CompilerParams(dimension_semantics=("arbitrary",) * n_axes, vmem_limit_bytes=vmem_limit)


def _tile(n, cap, mult=8):
    for t in range(min(cap, n), 0, -1):
        if n % t == 0 and t % mult == 0:
            return t
    return n


def _sigmoid(x):
    return 0.5 * jnp.tanh(0.5 * x) + 0.5


def _mm(name, a_ops, b_ops, products, dims, grid, k_axis, outs, acc_shape, epilogue, extra_ops=()):
    na, nb, ne, no = len(a_ops), len(b_ops), len(extra_ops), len(outs)
    n_acc = 1 + max(c for _, _, c in products)
    nk = 1 if k_axis is None else grid[k_axis]

    def body(*refs):
        a_refs = refs[:na]
        b_refs = refs[na:na + nb]
        e_refs = refs[na + nb:na + nb + ne]
        o_refs = refs[na + nb + ne:na + nb + ne + no]
        acc_refs = refs[na + nb + ne + no:]

        def partial_sums():
            vals = [None] * n_acc
            for ai, bi, ci in products:
                d = lax.dot_general(a_refs[ai][...].astype(BF16), b_refs[bi][...].astype(BF16), dims,
                                    preferred_element_type=F32)
                vals[ci] = d if vals[ci] is None else vals[ci] + d
            return vals

        if nk == 1:
            epilogue(partial_sums(), e_refs, o_refs)
        else:
            k = pl.program_id(k_axis)

            @pl.when(k == 0)
            def _():
                for acc in acc_refs:
                    acc[...] = jnp.zeros_like(acc)

            for acc, v in zip(acc_refs, partial_sums()):
                acc[...] += v

            @pl.when(k == nk - 1)
            def _():
                epilogue([acc[...] for acc in acc_refs], e_refs, o_refs)

    ops = list(a_ops) + list(b_ops) + list(extra_ops)
    return pl.pallas_call(
        body, name=name, grid=grid,
        in_specs=[s for _, s in ops], out_specs=[s for _, s in outs], out_shape=[o for o, _ in outs],
        scratch_shapes=[pltpu.VMEM(acc_shape, F32) for _ in range(n_acc if nk > 1 else 0)],
        compiler_params=_params(len(grid)),
    )(*[a for a, _ in ops])


def _store(accs, e_refs, o_refs):
    o_refs[0][...] = accs[0].astype(o_refs[0].dtype)


def linear(name, x, w, out_dtype, resid=None, next_gain=None):
    t, k = x.shape
    n = w.shape[1]
    tm = _tile(t, 512)
    tn = n if n <= 2048 else _tile(n, 1024, 128)
    tile = pl.BlockSpec((tm, tn), lambda j, i: (i, j))
    extra = [] if resid is None else [(resid, tile)]
    outs = [(S((t, n), out_dtype), tile)]
    if next_gain is not None:
        assert tn == n
        extra.append((next_gain, pl.BlockSpec((1, n), lambda j, i: (0, 0))))
        outs.append((S((t, n), BF16), tile))

    def epilogue(accs, e_refs, o_refs):
        y = accs[0] if resid is None else e_refs[0][...] + accs[0]
        o_refs[0][...] = y.astype(out_dtype)
        if next_gain is not None:
            o_refs[1][...] = (y * _rstd(y) * e_refs[-1][...]).astype(BF16)

    res = _mm(name, [(x, pl.BlockSpec((tm, k), lambda j, i: (i, 0)))], [(w, pl.BlockSpec((k, tn), lambda j, i: (0, j)))],
              [(0, 0, 0)], _NN, (n // tn, t // tm), None, outs, None, epilogue, extra)
    return res[0] if next_gain is None else res


def linear_nt(name, dy, w, out_dtype):
    t, n = dy.shape
    k = w.shape[0]
    tm = _tile(t, 512)
    tc = n if n <= 2048 else _tile(n, 1024, 128)
    return _mm(name, [(dy, pl.BlockSpec((tm, tc), lambda i, c: (i, c)))], [(w, pl.BlockSpec((k, tc), lambda i, c: (0, c)))],
               [(0, 0, 0)], _NT, (t // tm, n // tc), 1,
               [(S((t, k), out_dtype), pl.BlockSpec((tm, k), lambda i, c: (i, 0)))], (tm, k), _store)[0]


def wgrad(name, x, dy):
    t, k = x.shape
    n = dy.shape[1]
    tk = _tile(t, 512)
    tn = n if n <= 1024 else _tile(n, 1024, 128)
    return _mm(name, [(x, pl.BlockSpec((tk, k), lambda j, s: (s, 0)))], [(dy, pl.BlockSpec((tk, tn), lambda j, s: (s, j)))],
               [(0, 0, 0)], _TN, (n // tn, t // tk), 1,
               [(S((k, n), BF16), pl.BlockSpec((k, tn), lambda j, s: (0, j)))], (k, tn), _store)[0]


def _resident(shape, index_map):
    return pl.BlockSpec(shape, index_map, pipeline_mode=pl.Buffered(1))


def ffn_up(name, a, wg_all, wu_all, layer):
    t, d = a.shape
    f4 = wg_all.shape[2]
    tm = _tile(t, 512)
    w_spec = _resident((N_CHIPS, d, f4), lambda i: (0, layer, 0))
    h_spec = pl.BlockSpec((N_CHIPS, tm, f4), lambda i: (0, i, 0))

    def body(a_ref, wg_ref, wu_ref, zg_ref, zu_ref, z_ref):
        av = a_ref[...]
        for k in range(N_CHIPS):
            g = jnp.dot(av, wg_ref[k], preferred_element_type=F32)
            u = jnp.dot(av, wu_ref[k], preferred_element_type=F32)
            sg = _sigmoid(g)
            silu = g * sg
            zg_ref[k] = (u * (sg * (1.0 + g * (1.0 - sg)))).astype(BF16)
            zu_ref[k] = silu.astype(BF16)
            z_ref[k] = (silu * u).astype(BF16)

    return pl.pallas_call(
        body, name=name, grid=(t // tm,), in_specs=[pl.BlockSpec((tm, d), lambda i: (i, 0)), w_spec, w_spec],
        out_specs=[h_spec] * 3, out_shape=[S((N_CHIPS, t, f4), BF16)] * 3, compiler_params=_params(1))(a, wg_all, wu_all)


def ffn_down(name, z, wd_all, layer, resid, next_gain=None):
    _, t, f4 = z.shape
    d = wd_all.shape[2]
    tm = _tile(t, 512)
    row = pl.BlockSpec((tm, d), lambda i: (i, 0))
    normed = next_gain is not None

    def body(z_ref, wd_ref, r_ref, *refs):
        acc = r_ref[...]
        for k in range(N_CHIPS):
            acc = acc + jnp.dot(z_ref[k], wd_ref[k], preferred_element_type=F32)
        refs[-2 if normed else -1][...] = acc
        if normed:
            refs[-1][...] = (acc * _rstd(acc) * refs[0][...]).astype(BF16)

    res = pl.pallas_call(
        body, name=name, grid=(t // tm,),
        in_specs=[pl.BlockSpec((N_CHIPS, tm, f4), lambda i: (0, i, 0)), _resident((N_CHIPS, f4, d), lambda i: (0, layer, 0)), row]
        + ([pl.BlockSpec((1, d), lambda i: (0, 0))] if normed else []),
        out_specs=[row] * (2 if normed else 1), out_shape=[S((t, d), F32)] + ([S((t, d), BF16)] if normed else []),
        compiler_params=_params(1))(z, wd_all, resid, *([next_gain] if normed else []))
    return res if normed else res[0]


def ffn_bwd_hidden(name, dh, wd_all, layer, zg, zu, swap=()):
    t, d = dh.shape
    f4 = zg.shape[2]
    tm = _tile(t, 512)
    h_spec = pl.BlockSpec((N_CHIPS, tm, f4), lambda i: (0, i, 0))
    n = len(swap)

    def body(dh_ref, wd_ref, zg_ref, zu_ref, *refs):
        dg_ref, du_ref = refs[n:n + 2]
        if n:
            _ride_along(swap_copies(refs[:n], refs[n + 2:2 * n + 2], *refs[2 * n + 2:]), (pl.program_id(0),), (t // tm,))
        dhb = dh_ref[...].astype(BF16)
        for k in range(N_CHIPS):
            dz = lax.dot_general(dhb, wd_ref[k], _NT, preferred_element_type=F32)
            dg_ref[k] = (dz * zg_ref[k].astype(F32)).astype(BF16)
            du_ref[k] = (dz * zu_ref[k].astype(F32)).astype(BF16)

    outs = pl.pallas_call(
        body, name=name, grid=(t // tm,),
        in_specs=[pl.BlockSpec((tm, d), lambda i: (i, 0)), _resident((N_CHIPS, f4, d), lambda i: (0, layer, 0)), h_spec, h_spec]
        + [ANY] * n,
        out_specs=[h_spec] * 2 + [ANY] * n, out_shape=[S((N_CHIPS, t, f4), BF16)] * 2 + _swap_shapes(swap),
        scratch_shapes=[pltpu.SemaphoreType.DMA((n,)), pltpu.SemaphoreType.DMA((n,))] if n else [],
        compiler_params=_params(1))(dh, wd_all, zg, zu, *swap)
    return outs[0], outs[1], list(outs[2:])


def _norm_bwd_specs(tm, d):
    row = pl.BlockSpec((tm, d), lambda i: (i, 0))
    vec = pl.BlockSpec((1, d), lambda i: (0, 0))
    return [row, vec, row], [row, vec]


def _norm_bwd_tail(da, h_ref, g_ref, dhi_ref, dho_ref, dgain_ref):
    dx, dgain = _rms_bwd(h_ref[...], g_ref[...], da)
    dho_ref[...] = dhi_ref[...] + dx

    @pl.when(pl.program_id(0) == 0)
    def _():
        dgain_ref[...] = jnp.zeros_like(dgain_ref)

    dgain_ref[...] += dgain


def ffn_bwd_input(name, dg, du, wg_all, wu_all, layer, h, gain, dh_in):
    _, t, f4 = dg.shape
    d = h.shape[1]
    tm = _tile(t, 512)
    h_spec = pl.BlockSpec((N_CHIPS, tm, f4), lambda i: (0, i, 0))
    w_spec = _resident((N_CHIPS, d, f4), lambda i: (0, layer, 0))
    tail_in, tail_out = _norm_bwd_specs(tm, d)

    def body(dg_ref, du_ref, wg_ref, wu_ref, *tail):
        acc = jnp.zeros((tm, d), F32)
        for k in range(N_CHIPS):
            acc = acc + lax.dot_general(dg_ref[k], wg_ref[k], _NT, preferred_element_type=F32)
            acc = acc + lax.dot_general(du_ref[k], wu_ref[k], _NT, preferred_element_type=F32)
        _norm_bwd_tail(acc, *tail)

    return pl.pallas_call(
        body, name=name, grid=(t // tm,), in_specs=[h_spec, h_spec, w_spec, w_spec] + tail_in, out_specs=tail_out,
        out_shape=[S((t, d), F32), S((1, d), F32)], compiler_params=_params(1))(dg, du, wg_all, wu_all, h, gain, dh_in)


def ffn_wgrad_up(name, a, dy):
    t, d = a.shape
    f4 = dy.shape[2]
    tk = _tile(t, 512)
    nt = t // tk

    def body(a_ref, dy_ref, o_ref, acc):
        s = pl.program_id(0)

        @pl.when(s == 0)
        def _():
            acc[...] = jnp.zeros_like(acc)

        at = a_ref[...].T
        for k in range(N_CHIPS):
            acc[k] += jnp.dot(at, dy_ref[k], preferred_element_type=F32)

        @pl.when(s == nt - 1)
        def _():
            o_ref[...] = acc[...].astype(BF16)

    return pl.pallas_call(
        body, name=name, grid=(nt,),
        in_specs=[pl.BlockSpec((tk, d), lambda s: (s, 0)), pl.BlockSpec((N_CHIPS, tk, f4), lambda s: (0, s, 0))],
        out_specs=pl.BlockSpec((N_CHIPS, d, f4), lambda s: (0, 0, 0)), out_shape=S((N_CHIPS, d, f4), BF16),
        scratch_shapes=[pltpu.VMEM((N_CHIPS, d, f4), F32)], compiler_params=_params(1))(a, dy)


def ffn_wgrad_down(name, z, dh):
    _, t, f4 = z.shape
    d = dh.shape[1]
    tk = _tile(t, 512)
    nt = t // tk

    def body(z_ref, dh_ref, o_ref, acc):
        s = pl.program_id(0)

        @pl.when(s == 0)
        def _():
            acc[...] = jnp.zeros_like(acc)

        dhb = dh_ref[...].astype(BF16)
        for k in range(N_CHIPS):
            acc[k] += lax.dot_general(z_ref[k], dhb, _TN, preferred_element_type=F32)

        @pl.when(s == nt - 1)
        def _():
            o_ref[...] = acc[...].astype(BF16)

    return pl.pallas_call(
        body, name=name, grid=(nt,),
        in_specs=[pl.BlockSpec((N_CHIPS, tk, f4), lambda s: (0, s, 0)), pl.BlockSpec((tk, d), lambda s: (s, 0))],
        out_specs=pl.BlockSpec((N_CHIPS, f4, d), lambda s: (0, 0, 0)), out_shape=S((N_CHIPS, f4, d), BF16),
        scratch_shapes=[pltpu.VMEM((N_CHIPS, f4, d), F32)], compiler_params=_params(1))(z, dh)


def conv_in_proj(name, a, w, conv_w):
    t, d = a.shape
    tm = _tile(t, 512)
    keep = 8

    def body(a_ref, w_ref, cw_ref, bcx_ref, y_ref, u_ref):
        @pl.when(pl.program_id(0) == 0)
        def _():
            u_ref[0:keep, :] = jnp.zeros((keep, d), F32)

        av = a_ref[...]
        b, c, x = [jnp.dot(av, w_ref[:, j * d:(j + 1) * d], preferred_element_type=F32) for j in range(3)]
        for j, part in enumerate((b, c, x)):
            bcx_ref[j] = part.astype(bcx_ref.dtype)
        u_ref[keep:keep + tm, :] = c * x
        uc = (cw_ref[0:1, :] * u_ref[keep - 2:keep - 2 + tm, :] + cw_ref[1:2, :] * u_ref[keep - 1:keep - 1 + tm, :]
              + cw_ref[2:3, :] * u_ref[keep:keep + tm, :])
        y_ref[...] = (b * uc).astype(BF16)
        u_ref[0:keep, :] = u_ref[tm:tm + keep, :]

    return pl.pallas_call(
        body, name=name, grid=(t // tm,),
        in_specs=[pl.BlockSpec((tm, d), lambda i: (i, 0)), _resident((d, 3 * d), lambda i: (0, 0)), pl.BlockSpec((3, d), lambda i: (0, 0))],
        out_specs=[pl.BlockSpec((3, tm, d), lambda i: (0, i, 0)), pl.BlockSpec((tm, d), lambda i: (i, 0))],
        out_shape=[S((3, t, d), CONV_SAVED_DTYPE), S((t, d), BF16)], scratch_shapes=[pltpu.VMEM((tm + keep, d), F32)],
        compiler_params=_params(1))(a, w, conv_w)


def conv_in_bwd_input(name, dbcx, w, h, gain, dh_in):
    _, t, d = dbcx.shape
    tm = _tile(t, 512)
    tail_in, tail_out = _norm_bwd_specs(tm, d)

    def body(g_ref, w_ref, *tail):
        acc = jnp.zeros((tm, d), F32)
        for j in range(3):
            acc = acc + lax.dot_general(g_ref[j], w_ref[:, j * d:(j + 1) * d], _NT, preferred_element_type=F32)
        _norm_bwd_tail(acc, *tail)

    return pl.pallas_call(
        body, name=name, grid=(t // tm,),
        in_specs=[pl.BlockSpec((3, tm, d), lambda i: (0, i, 0)), _resident((d, 3 * d), lambda i: (0, 0))] + tail_in,
        out_specs=tail_out, out_shape=[S((t, d), F32), S((1, d), F32)], compiler_params=_params(1))(dbcx, w, h, gain, dh_in)


def linear_nt_norm_bwd(name, dy, w, h, gain, dh_in, parts=()):
    t, n = dy.shape
    k = w.shape[0]
    tm = _tile(t, 512)
    tail_in, tail_out = _norm_bwd_specs(tm, k)
    m = len(parts)

    def body(dy_ref, w_ref, h_ref, g_ref, dhi_ref, *refs):
        if m:
            _ride_along(scatter_ici_copies(refs[:m], refs[m + 2:2 * m + 2], *refs[2 * m + 2:]), (pl.program_id(0),), (t // tm,))
        da = lax.dot_general(dy_ref[...].astype(BF16), w_ref[...], _NT, preferred_element_type=F32)
        _norm_bwd_tail(da, h_ref, g_ref, dhi_ref, *refs[m:m + 2])

    outs = pl.pallas_call(
        body, name=name, grid=(t // tm,),
        in_specs=[pl.BlockSpec((tm, n), lambda i: (i, 0)), _resident((k, n), lambda i: (0, 0))] + tail_in + [ANY] * m,
        out_specs=tail_out + [ANY] * m, out_shape=[S((t, k), F32), S((1, k), F32)] + [S(p.shape, p.dtype) for p in parts],
        scratch_shapes=[pltpu.SemaphoreType.DMA((m, 3)), pltpu.SemaphoreType.DMA((m, 3))] if m else [],
        compiler_params=_params(1))(dy, w, h, gain, dh_in, *parts)
    return outs[0], outs[1], list(outs[2:])


def conv_in_wgrad(name, a, dbcx):
    t, d = a.shape
    tk = _tile(t, 512)
    nt = t // tk
    n4 = 3 * d // N_CHIPS

    def body(a_ref, g_ref, o_ref, acc):
        s = pl.program_id(0)

        @pl.when(s == 0)
        def _():
            acc[...] = jnp.zeros_like(acc)

        at = a_ref[...].T
        for j in range(3):
            acc[:, j * d:(j + 1) * d] += jnp.dot(at, g_ref[j], preferred_element_type=F32)

        @pl.when(s == nt - 1)
        def _():
            for k in range(N_CHIPS):
                o_ref[k] = acc[:, k * n4:(k + 1) * n4].astype(BF16)

    return pl.pallas_call(
        body, name=name, grid=(nt,),
        in_specs=[pl.BlockSpec((tk, d), lambda s: (s, 0)), pl.BlockSpec((3, tk, d), lambda s: (0, s, 0))],
        out_specs=pl.BlockSpec((N_CHIPS, d, n4), lambda s: (0, 0, 0)), out_shape=S((N_CHIPS, d, n4), BF16),
        scratch_shapes=[pltpu.VMEM((d, 3 * d), F32)], compiler_params=_params(1))(a, dbcx)


def _rstd(x):
    return lax.rsqrt(jnp.mean(x * x, axis=-1, keepdims=True) + RMS_EPS)


def _rms_bwd(x, g, dy):
    r = _rstd(x)
    xhat = x * r
    dgain = jnp.sum(dy * xhat, axis=0, keepdims=True)
    dxh = dy * g
    dx = r * (dxh - xhat * jnp.mean(dxh * xhat, axis=-1, keepdims=True))
    return dx, dgain


def ffn_down_loss(name, z, wd_all, layer, resid, gain, target):
    _, t, f4 = z.shape
    d = wd_all.shape[2]
    tm = _tile(t, 512)

    def body(z_ref, wd_ref, r_ref, g_ref, t_ref, dh_ref, dg_ref, loss_ref):
        x = r_ref[...]
        for k in range(N_CHIPS):
            x = x + jnp.dot(z_ref[k], wd_ref[k], preferred_element_type=F32)
        g = g_ref[...]
        r = _rstd(x)
        xhat = x * r
        err = xhat * g - t_ref[...]
        dy = err * (1.0 / d)
        dxh = dy * g
        dh_ref[...] = r * (dxh - xhat * jnp.mean(dxh * xhat, axis=-1, keepdims=True))

        @pl.when(pl.program_id(0) == 0)
        def _():
            dg_ref[...] = jnp.zeros_like(dg_ref)
            loss_ref[...] = jnp.zeros_like(loss_ref)

        dg_ref[...] += jnp.sum(dy * xhat, axis=0, keepdims=True)
        per_token = jnp.mean(err * err, axis=-1, keepdims=True)
        loss_ref[...] += 0.5 * jnp.sum(per_token, axis=0, keepdims=True)

    row = pl.BlockSpec((tm, d), lambda i: (i, 0))
    vec = pl.BlockSpec((1, d), lambda i: (0, 0))
    one = pl.BlockSpec((1, 1), lambda i: (0, 0))
    return pl.pallas_call(
        body, name=name, grid=(t // tm,),
        in_specs=[pl.BlockSpec((N_CHIPS, tm, f4), lambda i: (0, i, 0)), _resident((N_CHIPS, f4, d), lambda i: (0, layer, 0)), row, vec, row],
        out_specs=[row, vec, one], out_shape=[S((t, d), F32), S((1, d), F32), S((1, 1), F32)],
        compiler_params=_params(1))(z, wd_all, resid, gain, target)


def mla_in_proj(name, a, w, g_cq, g_ckv, cos, sin):
    t, d = a.shape
    n = w.shape[1]
    ql, kl = g_cq.shape[1], g_ckv.shape[1]
    tr = _tile(t, 512)

    def body(a_ref, w_ref, gq_ref, gk_ref, c_ref, s_ref, p_ref, cq_ref, ckv_ref, kr_ref):
        p_ref[...] = jnp.dot(a_ref[...], w_ref[...], preferred_element_type=F32)
        xq = p_ref[:, 0:ql]
        cq_ref[...] = (xq * _rstd(xq) * gq_ref[...]).astype(BF16)
        xk = p_ref[:, ql:ql + kl]
        ckv_ref[...] = (xk * _rstd(xk) * gk_ref[...]).astype(BF16)
        k1 = p_ref[:, ql + kl:ql + kl + HALF]
        k2 = p_ref[:, ql + kl + HALF:ql + kl + ROPE]
        c = c_ref[...]
        s = s_ref[...]
        kr_ref[:, 0:HALF] = k1 * c - k2 * s
        kr_ref[:, HALF:ROPE] = k1 * s + k2 * c

    def row(w):
        return pl.BlockSpec((tr, w), lambda i: (i, 0))

    def vec(w):
        return pl.BlockSpec((1, w), lambda i: (0, 0))

    return pl.pallas_call(
        body, name=name, grid=(t // tr,),
        in_specs=[row(d), _resident((d, n), lambda i: (0, 0)), vec(ql), vec(kl), row(HALF), row(HALF)],
        out_specs=[row(n), row(ql), row(kl), row(ROPE)],
        out_shape=[S((t, n), F32), S((t, ql), BF16), S((t, kl), BF16), S((t, ROPE), F32)],
        compiler_params=_params(1))(a, w, g_cq, g_ckv, cos, sin)


def qkv_heads(name, cq, ckv, w_uq, w_ukv, kr, cos, sin, shards=()):
    t = cq.shape[0]
    tr = _tile(t, 256)
    n = len(shards)

    def body(cq_ref, ckv_ref, wq_ref, wkv_ref, kr_ref, c_ref, s_ref, *refs):
        src = refs[:n]
        qo_ref, ko_ref, vo_ref = refs[n:n + 3]
        q_ref, kv_ref = refs[2 * n + 3:2 * n + 5]
        if n:
            _ride_along(gather_ici_copies(src, refs[n + 3:2 * n + 3], *refs[2 * n + 5:]), (pl.program_id(0),), (t // tr,))
        q_ref[...] = jnp.dot(cq_ref[...], wq_ref[...], preferred_element_type=F32)
        kv_ref[...] = jnp.dot(ckv_ref[...], wkv_ref[...], preferred_element_type=F32).astype(BF16)
        c = c_ref[...]
        s = s_ref[...]
        krb = kr_ref[...].astype(BF16)
        for h in range(N_HEADS):
            q0 = h * QK
            qo_ref[h, :, 0:NOPE] = q_ref[:, q0:q0 + NOPE].astype(BF16)
            q1 = q_ref[:, q0 + NOPE:q0 + NOPE + HALF]
            q2 = q_ref[:, q0 + NOPE + HALF:q0 + QK]
            qo_ref[h, :, NOPE:NOPE + HALF] = (q1 * c - q2 * s).astype(BF16)
            qo_ref[h, :, NOPE + HALF:QK] = (q1 * s + q2 * c).astype(BF16)
            k0 = h * (NOPE + VDIM)
            ko_ref[h, :, 0:NOPE] = kv_ref[:, k0:k0 + NOPE]
            ko_ref[h, :, NOPE:QK] = krb
            vo_ref[h] = kv_ref[:, k0 + NOPE:k0 + NOPE + VDIM]

    def row(w):
        return pl.BlockSpec((tr, w), lambda i: (i, 0))

    def heads(w):
        return pl.BlockSpec((N_HEADS, tr, w), lambda i: (0, i, 0))

    outs = pl.pallas_call(
        body, name=name, grid=(t // tr,),
        in_specs=[row(cq.shape[1]), row(ckv.shape[1]), _resident(w_uq.shape, lambda i: (0, 0)), _resident(w_ukv.shape, lambda i: (0, 0)),
                  row(ROPE), row(HALF), row(HALF)] + [ANY] * n,
        out_specs=[heads(QK), heads(QK), heads(VDIM)] + [ANY] * n,
        out_shape=[S((N_HEADS, t, QK), BF16), S((N_HEADS, t, QK), BF16), S((N_HEADS, t, VDIM), BF16)]
        + [S((N_CHIPS,) + s.shape, s.dtype) for s in shards],
        scratch_shapes=[pltpu.VMEM((tr, N_HEADS * QK), F32), pltpu.VMEM((tr, N_HEADS * (NOPE + VDIM)), BF16)]
        + ([pltpu.SemaphoreType.DMA((n, 3)), pltpu.SemaphoreType.DMA((n, 3))] if n else []),
        compiler_params=_params(1))(cq, ckv, w_uq, w_ukv, kr, cos, sin, *shards)
    return outs[0], outs[1], outs[2], list(outs[3:])


def qkv_heads_bwd(name, dq_h, dk_h, dv_h, cos, sin, w_uq, w_ukv, proj, g_cq, g_ckv, halves=(), targets=(), where=()):
    t = dq_h.shape[1]
    tr = _tile(t, 512)
    n, nt = len(halves), len(targets)
    ql, kl = g_cq.shape[1], g_ckv.shape[1]
    width = proj.shape[1]

    def body(dq_ref, dk_ref, dv_ref, c_ref, s_ref, wq_ref, wkv_ref, p_ref, gq_ref, gk_ref, *refs):
        q_ref, kv_ref, dp_ref, dgq_ref, dgk_ref = refs[n:n + 5]
        kr_ref = refs[n + 5 + nt]
        if n:
            src, dst = refs[:n], refs[n + 5:n + 5 + nt]
            stages = refs[n + 6 + nt:2 * n + 6 + nt]
            send_sems, recv_sems, local_sems = refs[2 * n + 6 + nt:]
            core = lax.axis_index("c")
            own = [(src[i], stages[i],
                    dst[where[i][0]].at[pl.ds(pl.multiple_of(where[i][1] + core * src[i].shape[0], 8), src[i].shape[0])],
                    local_sems.at[i]) for i in range(n)]
            _ride_along(join_copies(src, dst, where, send_sems, recv_sems), (pl.program_id(0),), (t // tr,), own)
        c = c_ref[...]
        s = s_ref[...]
        dkr = jnp.zeros((tr, ROPE), F32)
        for h in range(N_HEADS):
            q0 = h * QK
            q_ref[:, q0:q0 + NOPE] = dq_ref[h, :, 0:NOPE].astype(BF16)
            d1 = dq_ref[h, :, NOPE:NOPE + HALF]
            d2 = dq_ref[h, :, NOPE + HALF:QK]
            q_ref[:, q0 + NOPE:q0 + NOPE + HALF] = (d1 * c + d2 * s).astype(BF16)
            q_ref[:, q0 + NOPE + HALF:q0 + QK] = (d2 * c - d1 * s).astype(BF16)
            k0 = h * (NOPE + VDIM)
            kv_ref[:, k0:k0 + NOPE] = dk_ref[h, :, 0:NOPE].astype(BF16)
            kv_ref[:, k0 + NOPE:k0 + NOPE + VDIM] = dv_ref[h].astype(BF16)
            dkr = dkr + dk_ref[h, :, NOPE:QK]
        kr_ref[...] = dkr
        dcq = lax.dot_general(q_ref[...], wq_ref[...], _NT, preferred_element_type=F32)
        dckv = lax.dot_general(kv_ref[...], wkv_ref[...], _NT, preferred_element_type=F32)
        dxq, dgq = _rms_bwd(p_ref[:, 0:ql], gq_ref[...], dcq)
        dp_ref[:, 0:ql] = dxq.astype(BF16)
        dxk, dgk = _rms_bwd(p_ref[:, ql:ql + kl], gk_ref[...], dckv)
        dp_ref[:, ql:ql + kl] = dxk.astype(BF16)
        r1 = kr_ref[:, 0:HALF]
        r2 = kr_ref[:, HALF:ROPE]
        dp_ref[:, ql + kl:ql + kl + HALF] = (r1 * c + r2 * s).astype(BF16)
        dp_ref[:, ql + kl + HALF:ql + kl + ROPE] = (r2 * c - r1 * s).astype(BF16)

        @pl.when(pl.program_id(0) == 0)
        def _():
            dgq_ref[...] = jnp.zeros_like(dgq_ref)
            dgk_ref[...] = jnp.zeros_like(dgk_ref)

        dgq_ref[...] += dgq
        dgk_ref[...] += dgk

    def row(w):
        return pl.BlockSpec((tr, w), lambda i: (i, 0))

    def vec(w):
        return pl.BlockSpec((1, w), lambda i: (0, 0))

    def heads(w):
        return pl.BlockSpec((N_HEADS, tr, w), lambda i: (0, i, 0))

    outs = pl.pallas_call(
        body, name=name, grid=(t // tr,),
        in_specs=[heads(QK), heads(QK), heads(VDIM), row(HALF), row(HALF), _resident(w_uq.shape, lambda i: (0, 0)),
                  _resident(w_ukv.shape, lambda i: (0, 0)), row(width), vec(ql), vec(kl)] + [ANY] * n,
        out_specs=[row(N_HEADS * QK), row(N_HEADS * (NOPE + VDIM)), row(width), vec(ql), vec(kl)] + [ANY] * nt,
        out_shape=[S((t, N_HEADS * QK), BF16), S((t, N_HEADS * (NOPE + VDIM)), BF16), S((t, width), BF16),
                   S((1, ql), F32), S((1, kl), F32)] + [S(tg, F32) for tg in targets],
        scratch_shapes=[pltpu.VMEM((tr, ROPE), F32)] + [pltpu.VMEM(h.shape, h.dtype) for h in halves]
        + ([pltpu.SemaphoreType.DMA((n,)), pltpu.SemaphoreType.DMA((n,)), pltpu.SemaphoreType.DMA((n, 2))] if n else []),
        compiler_params=_params(1))(dq_h, dk_h, dv_h, cos, sin, w_uq, w_ukv, proj, g_cq, g_ckv, *halves)
    return outs[0], outs[1], outs[2], outs[3], outs[4], list(outs[5:])


def _chunk_mask_t(q_start, k_start, bq, bk):
    kc = (k_start + lax.broadcasted_iota(jnp.int32, (bk, bq), 0)) // CHUNK
    qc = (q_start + lax.broadcasted_iota(jnp.int32, (bk, bq), 1)) // CHUNK
    return kc <= qc


def attention_fwd(name, q, k, v, shards=()):
    nh, t, _ = q.shape
    blk = ATT_BLOCK
    nq = t // blk
    n = len(shards)

    def body(q_ref, k_ref, v_ref, *refs):
        src = refs[:n]
        o_ref, lse_ref = refs[n:n + 2]
        dst = refs[n + 2:2 * n + 2]
        m_ref, l_ref, acc_ref, s_buf, p_buf, alpha_buf, bias_ref = refs[2 * n + 2:2 * n + 9]
        i = pl.program_id(1)
        if n:
            send_sems, recv_sems = refs[2 * n + 9:]
            _ride_along(gather_ici_copies(src, dst, send_sems, recv_sems), (pl.program_id(0), i), (nh, nq))

        @pl.when((pl.program_id(0) == 0) & (i == 0))
        def _():
            bias_ref[...] = jnp.where(_chunk_mask_t(0, 0, blk, blk), 0.0, MASK_VALUE)

        m_ref[...] = jnp.full_like(m_ref, MASK_VALUE)
        l_ref[...] = jnp.zeros_like(l_ref)
        acc_ref[...] = jnp.zeros_like(acc_ref)

        def rows(b):
            return pl.ds(pl.multiple_of(b * blk, blk), blk)

        def scores(b, slot):
            s_buf[slot] = lax.dot_general(k_ref[rows(b), :], q_ref[...], _NT, preferred_element_type=F32)

        def softmax(slot, diagonal):
            s = s_buf[slot]
            if diagonal:
                s = s + bias_ref[...]
            m_old = m_ref[...]
            m_new = jnp.maximum(m_old, jnp.max(s, axis=0, keepdims=True))
            p = jnp.exp2((s - m_new) * SCORE_SCALE_LOG2)
            alpha = jnp.exp2((m_old - m_new) * SCORE_SCALE_LOG2)
            l_ref[...] = alpha * l_ref[...] + jnp.sum(p, axis=0, keepdims=True)
            m_ref[...] = m_new
            alpha_buf[slot] = alpha
            p_buf[slot] = p.astype(BF16)

        def values(b, slot):
            pv = lax.dot_general(v_ref[rows(b), :], p_buf[slot], _TN, preferred_element_type=F32)
            acc_ref[...] = alpha_buf[slot] * acc_ref[...] + pv

        def step(t, slot):
            values(t - 2, slot)
            softmax(1 - slot, False)
            scores(t, slot)

        scores(0, 0)

        @pl.when(i == 0)
        def _():
            softmax(0, True)
            values(0, 0)

        @pl.when(i > 0)
        def _():
            scores(1, 1)
            softmax(0, False)
            steady = i - 1

            def pair(u, carry):
                step(2 + 2 * u, 0)
                step(3 + 2 * u, 1)
                return carry

            lax.fori_loop(0, steady // 2, pair, 0)

            @pl.when(steady % 2 == 1)
            def _():
                step(i, 0)

            last = i % 2
            softmax(last, True)
            values(i - 1, 1 - last)
            values(i, last)

        l = l_ref[...]
        o_ref[...] = (acc_ref[...] / l).T
        lse_ref[...] = m_ref[...] * SCORE_SCALE + jnp.log(l)

    outs = pl.pallas_call(
        body, name=name, grid=(nh, nq),
        in_specs=[pl.BlockSpec((None, blk, QK), lambda h, i: (h, i, 0)), pl.BlockSpec((None, t, QK), lambda h, i: (h, 0, 0)),
                  pl.BlockSpec((None, t, VDIM), lambda h, i: (h, 0, 0))] + [ANY] * n,
        out_specs=[pl.BlockSpec((blk, VDIM), lambda h, i: (i, h)),
                   pl.BlockSpec((None, None, 1, blk), lambda h, i: (h, i, 0, 0))] + [ANY] * n,
        out_shape=[S((t, nh * VDIM), F32), S((nh, nq, 1, blk), F32)] + [S((N_CHIPS,) + s.shape, s.dtype) for s in shards],
        scratch_shapes=[pltpu.VMEM((1, blk), F32), pltpu.VMEM((1, blk), F32), pltpu.VMEM((VDIM, blk), F32),
                        pltpu.VMEM((2, blk, blk), F32), pltpu.VMEM((2, blk, blk), BF16), pltpu.VMEM((2, 1, blk), F32),
                        pltpu.VMEM((blk, blk), F32)]
        + ([pltpu.SemaphoreType.DMA((n, 3)), pltpu.SemaphoreType.DMA((n, 3))] if n else []),
        compiler_params=_params(2))(q, k, v, *shards)
    return outs[0], outs[1], list(outs[2:])


def attention_out_bwd(name, dh, w_o, o, swap=()):
    t, d = dh.shape
    n = w_o.shape[0]
    blk = ATT_BLOCK
    m = len(swap)

    def body(dh_ref, w_ref, o_ref, *refs):
        do_ref, d_ref = refs[m:m + 2]
        if m:
            _ride_along(swap_copies(refs[:m], refs[m + 2:2 * m + 2], *refs[2 * m + 2:]), (pl.program_id(0),), (t // blk,))
        do_ref[...] = lax.dot_general(dh_ref[...].astype(BF16), w_ref[...], _NT, preferred_element_type=F32)
        for h in range(N_HEADS):
            cols = slice(h * VDIM, (h + 1) * VDIM)
            d_ref[h] = jnp.sum((do_ref[:, cols] * o_ref[:, cols]).T, axis=0, keepdims=True)

    tile = pl.BlockSpec((blk, n), lambda i: (i, 0))
    outs = pl.pallas_call(
        body, name=name, grid=(t // blk,),
        in_specs=[pl.BlockSpec((blk, d), lambda i: (i, 0)), _resident((n, d), lambda i: (0, 0)), tile] + [ANY] * m,
        out_specs=[tile, pl.BlockSpec((N_HEADS, None, 1, blk), lambda i: (0, i, 0, 0))] + [ANY] * m,
        out_shape=[S((t, n), F32), S((N_HEADS, t // blk, 1, blk), F32)] + _swap_shapes(swap),
        scratch_shapes=[pltpu.SemaphoreType.DMA((m,)), pltpu.SemaphoreType.DMA((m,))] if m else [],
        compiler_params=_params(1))(dh, w_o, o, *swap)
    return outs[0], outs[1], list(outs[2:])


def attention_bwd(name, q, k, v, do, lse, delta, parts=()):
    nh, t, _ = q.shape
    blk = ATT_BLOCK
    nq = t // blk
    n_pairs = nq * (nq + 1) // 2
    n = len(parts)
    scale = SCORE_SCALE

    def body(q_ref, k_ref, v_ref, do_ref, lse_ref, dl_ref, *refs):
        src = refs[:n]
        dq_out, dk_out, dv_out = refs[n:n + 3]
        dst = refs[n + 3:2 * n + 3]
        s_buf, dp_buf, p_buf, ds_buf, bias_ref, dq_ref, dk_ref, dv_ref = refs[2 * n + 3:2 * n + 11]
        if n:
            send_sems, recv_sems = refs[2 * n + 11:]
            _ride_along(scatter_ici_copies(src, dst, send_sems, recv_sems), (pl.program_id(0),), (nh,))

        @pl.when(pl.program_id(0) == 0)
        def _():
            bias_ref[...] = jnp.where(_chunk_mask_t(0, 0, blk, blk), 0.0, MASK_VALUE)

        dq_ref[...] = jnp.zeros_like(dq_ref)
        dk_ref[...] = jnp.zeros_like(dk_ref)
        dv_ref[...] = jnp.zeros_like(dv_ref)

        def rows(x):
            return pl.ds(pl.multiple_of(x * blk, blk), blk)

        def after(jb):
            j, b = jb
            wrap = b == nq - 1 - j
            return jnp.where(wrap, j + 1, j), jnp.where(wrap, 0, b + 1)

        def products(jb, slot):
            j, b = jb
            s_buf[slot] = lax.dot_general(k_ref[rows(j), :], q_ref[rows(j + b), :], _NT, preferred_element_type=F32)
            dp_buf[slot] = lax.dot_general(v_ref[rows(j), :], do_ref[rows(j + b), :].astype(BF16), _NT, preferred_element_type=F32)

        def softmax_bwd(jb, slot):
            j, b = jb
            s = s_buf[slot] + bias_ref[...] * (b == 0).astype(F32)
            p = jnp.exp2(s * SCORE_SCALE_LOG2 - lse_ref[j + b] * LOG2_E)
            p_buf[slot] = p.astype(BF16)
            ds_buf[slot] = (p * (dp_buf[slot] - dl_ref[j + b]) * scale).astype(BF16)

        def gradients(jb, slot):
            j, b = jb
            dv_ref[rows(j), :] += jnp.dot(p_buf[slot], do_ref[rows(j + b), :].astype(BF16), preferred_element_type=F32)
            dk_ref[rows(j), :] += jnp.dot(ds_buf[slot], q_ref[rows(j + b), :], preferred_element_type=F32)
            dq_ref[rows(j + b), :] += lax.dot_general(ds_buf[slot], k_ref[rows(j), :], _TN, preferred_element_type=F32)

        def step(state, slot):
            third, second, first = state
            gradients(third, slot)
            softmax_bwd(second, 1 - slot)
            products(first, slot)
            return second, first, after(first)

        zero = jnp.int32(0)
        pair0 = (zero, zero)
        products(pair0, 0)
        if n_pairs == 1:
            softmax_bwd(pair0, 0)
            gradients(pair0, 0)
        else:
            pair1 = after(pair0)
            products(pair1, 1)
            softmax_bwd(pair0, 0)
            steady = n_pairs - 2
            state = lax.fori_loop(0, steady // 2, lambda u, st: step(step(st, 0), 1), (pair0, pair1, after(pair1)))
            if steady % 2:
                state = step(state, 0)
            before_last, last_pair, _ = state
            last = (n_pairs - 1) % 2
            softmax_bwd(last_pair, last)
            gradients(before_last, 1 - last)
            gradients(last_pair, last)
        dq_out[...] = dq_ref[...].astype(BF16)
        dk_out[...] = dk_ref[...].astype(BF16)
        dv_out[...] = dv_ref[...].astype(BF16)

    head = lambda w: pl.BlockSpec((None, t, w), lambda h: (h, 0, 0))
    stats = pl.BlockSpec((None, nq, 1, blk), lambda h: (h, 0, 0, 0))
    outs = pl.pallas_call(
        body, name=name, grid=(nh,),
        in_specs=[head(QK), head(QK), head(VDIM), pl.BlockSpec((t, VDIM), lambda h: (0, h)), stats, stats] + [ANY] * n,
        out_specs=[head(QK), head(QK), head(VDIM)] + [ANY] * n,
        out_shape=[S((nh, t, QK), BF16), S((nh, t, QK), BF16), S((nh, t, VDIM), BF16)] + [S(p.shape, p.dtype) for p in parts],
        scratch_shapes=[pltpu.VMEM((2, blk, blk), F32), pltpu.VMEM((2, blk, blk), F32), pltpu.VMEM((2, blk, blk), BF16),
                        pltpu.VMEM((2, blk, blk), BF16), pltpu.VMEM((blk, blk), F32),
                        pltpu.VMEM((t, QK), F32), pltpu.VMEM((t, QK), F32), pltpu.VMEM((t, VDIM), F32)]
        + ([pltpu.SemaphoreType.DMA((n, 3)), pltpu.SemaphoreType.DMA((n, 3))] if n else []),
        compiler_params=_params(1, VMEM_LIMIT_WHOLE_HEAD))(q, k, v, do, lse, delta, *parts)
    return outs[0], outs[1], outs[2], list(outs[3:])


def _shift_down(u, s):
    rows = lax.broadcasted_iota(jnp.int32, u.shape, 0)
    return jnp.where(rows >= s, pltpu.roll(u, s, 0), 0.0)


def _shift_up(u, s):
    n = u.shape[0]
    rows = lax.broadcasted_iota(jnp.int32, u.shape, 0)
    return jnp.where(rows < n - s, pltpu.roll(u, n - s, 0), 0.0)


def _conv_specs(t, d, lanes):
    slab = lambda part: pl.BlockSpec((None, t, lanes), lambda j, part=part: (part, 0, j))
    return slab, pl.BlockSpec((3, lanes), lambda j: (0, j)), pl.BlockSpec((t, lanes), lambda j: (0, j))


def conv_bwd(name, bcx, w, dy):
    _, t, d = bcx.shape
    lanes = _tile(d, 128, 128)
    slab, w_spec, col = _conv_specs(t, d, lanes)

    def body(b_ref, c_ref, x_ref, w_ref, dy_ref, d_ref, dw_ref):
        c = c_ref[...].astype(F32)
        x = x_ref[...].astype(F32)
        dyv = dy_ref[...]
        u = c * x
        u1 = _shift_down(u, 1)
        u2 = _shift_down(u, 2)
        w0, w1, w2 = w_ref[0:1, :], w_ref[1:2, :], w_ref[2:3, :]
        d_ref[0] = (dyv * (w0 * u2 + w1 * u1 + w2 * u)).astype(BF16)
        duc = dyv * b_ref[...].astype(F32)
        dw_ref[0:1, :] = jnp.sum(duc * u2, axis=0, keepdims=True)
        dw_ref[1:2, :] = jnp.sum(duc * u1, axis=0, keepdims=True)
        dw_ref[2:3, :] = jnp.sum(duc * u, axis=0, keepdims=True)
        du = w2 * duc + w1 * _shift_up(duc, 1) + w0 * _shift_up(duc, 2)
        d_ref[1] = (du * x).astype(BF16)
        d_ref[2] = (du * c).astype(BF16)

    return pl.pallas_call(
        body, name=name, grid=(d // lanes,), in_specs=[slab(0), slab(1), slab(2), w_spec, col],
        out_specs=[pl.BlockSpec((3, t, lanes), lambda j: (0, 0, j)), w_spec], out_shape=[S((3, t, d), BF16), S((3, d), F32)],
        compiler_params=_params(1))(bcx, bcx, bcx, w, dy)


def _adamw_update(w, g, m, v):
    m_new = ADAM_B1 * m + (1.0 - ADAM_B1) * g
    v_new = ADAM_B2 * v + (1.0 - ADAM_B2) * (g * g)
    m_hat = m_new / (1.0 - ADAM_B1 ** ADAM_STEP)
    v_hat = v_new / (1.0 - ADAM_B2 ** ADAM_STEP)
    return -ADAM_LR * (m_hat / (jnp.sqrt(v_hat) + ADAM_EPS) + ADAM_WD * w), m_new, v_new


def adamw(name, w, g, m, v):
    r, c = w.shape
    tr = _tile(r, 512)

    def body(w_ref, g_ref, m_ref, v_ref, d_ref, mo_ref, vo_ref):
        d_ref[...], mo_ref[...], vo_ref[...] = _adamw_update(w_ref[...], g_ref[...], m_ref[...], v_ref[...])

    blk = pl.BlockSpec((tr, c), lambda i: (i, 0))
    return pl.pallas_call(
        body, name=name, grid=(r // tr,), in_specs=[blk] * 4, out_specs=[blk] * 3, out_shape=[S((r, c), F32)] * 3,
        compiler_params=_params(1))(w, g, m, v)


def adamw_swapped(name, wt, g, mt, vt):
    nl, c, r = wt.shape
    tr = _tile(r, 512, 128)
    nr = r // tr

    def body(w_ref, g_ref, m_ref, v_ref, go_ref, d_ref, mo_ref, vo_ref):
        gt = g_ref[...].T
        go_ref[...] = gt
        d_ref[...], mo_ref[...], vo_ref[...] = _adamw_update(w_ref[...], gt, m_ref[...], v_ref[...])

    swapped = pl.BlockSpec((None, c, tr), lambda l, i: (l, 0, i))
    return pl.pallas_call(
        body, name=name, grid=(nl, nr),
        in_specs=[swapped, pl.BlockSpec((tr, c), lambda l, i: (l * nr + i, 0)), swapped, swapped],
        out_specs=[swapped] * 4, out_shape=[S((nl, c, r), F32)] * 4, compiler_params=_params(2))(wt, g, mt, vt)


def _place():
    x, y, c = lax.axis_index("x"), lax.axis_index("y"), lax.axis_index("c")
    other_chips = [(1 - x, y), (x, 1 - y), (1 - x, 1 - y)]
    return x, y, c, other_chips


def _half(c, rows):
    return pl.ds(pl.multiple_of(c * (rows // 2), 16), rows // 2)


def gather_weight_shards(shards, small, h, gain):
    n, ns = len(shards), len(small)
    t, d = h.shape
    tr = _tile(t, 512)
    steps = t // tr

    def body(h_ref, g_ref, *refs):
        src = refs[:n]
        small_src = refs[n:n + ns]
        a_ref = refs[n + ns]
        dst = refs[n + ns + 1:2 * n + ns + 1]
        small_dst = refs[2 * n + ns + 1:2 * (n + ns) + 1]
        send_sems, recv_sems, small_send, small_recv, local_sems = refs[2 * (n + ns) + 1:2 * (n + ns) + 6]
        stages = refs[2 * (n + ns) + 6:]
        x, y, c, chips = _place()
        me = 2 * x + y
        sibling = (x, y, 1 - c)
        own = [(s_ref, stages[i], d_ref.at[me], local_sems.at[i])
               for i, (s_ref, d_ref) in enumerate(zip(list(src) + list(small_src), list(dst) + list(small_dst)))]
        small_pairs = []
        for i in range(ns):
            for j, (px, py) in enumerate(chips):
                def whole(slot):
                    return pltpu.make_async_remote_copy(
                        src_ref=small_src[i], dst_ref=small_dst[i].at[slot], send_sem=small_send.at[i, j],
                        recv_sem=small_recv.at[i, j], device_id=(px, py, c), device_id_type=MESH)
                small_pairs.append((whole(me), whole(2 * px + py)))

        def copy(i, slot, half_of, sem, to, from_input=False):
            rows = _half(half_of, src[i].shape[0])
            return pltpu.make_async_remote_copy(
                src_ref=src[i].at[rows] if from_input else dst[i].at[slot, rows], dst_ref=dst[i].at[slot, rows],
                send_sem=send_sems.at[i, sem], recv_sem=recv_sems.at[i, sem], device_id=to, device_id_type=MESH)

        over_ici = [copy(i, me, c, j, (*chip, c), from_input=True) for i in range(n) for j, chip in enumerate(chips)]
        handed_on = [copy(i, 2 * px + py, c, 3 + j, sibling) for i in range(n) for j, (px, py) in enumerate(chips)]

        @pl.when(pl.program_id(0) == 0)
        def _():
            for s_ref, stage, _, sems in own:
                pltpu.make_async_copy(s_ref, stage, sems.at[0]).start()
            for outgoing, _ in small_pairs:
                outgoing.start()
            for cp in over_ici:
                cp.start()

        @pl.when(pl.program_id(0) == steps - 1)
        def _():
            k = 0
            for i in range(n):
                for j, (px, py) in enumerate(chips):
                    copy(i, 2 * px + py, c, j, sibling).wait_recv()
                    handed_on[k].start()
                    k += 1
            for i in range(n):
                for j, (px, py) in enumerate(chips):
                    copy(i, 2 * px + py, 1 - c, 3 + j, sibling).wait_recv()
            for cp in over_ici + handed_on:
                cp.wait_send()
            for _, incoming in small_pairs:
                incoming.wait_recv()
            for outgoing, _ in small_pairs:
                outgoing.wait_send()
            _place_locally(own)

        xv = h_ref[...]
        a_ref[...] = (xv * _rstd(xv) * g_ref[...]).astype(BF16)

    everything = list(shards) + list(small)
    outs = pl.pallas_call(
        body, name="gather_weight_shards", grid=(steps,),
        in_specs=[pl.BlockSpec((tr, d), lambda i: (i, 0)), pl.BlockSpec((1, d), lambda i: (0, 0))] + [ANY] * (n + ns),
        out_specs=[pl.BlockSpec((tr, d), lambda i: (i, 0))] + [ANY] * (n + ns),
        out_shape=[S((t, d), BF16)] + [S((N_CHIPS,) + s.shape, s.dtype) for s in everything],
        scratch_shapes=[pltpu.SemaphoreType.DMA((n, 6)), pltpu.SemaphoreType.DMA((n, 6)),
                        pltpu.SemaphoreType.DMA((max(ns, 1), 3)), pltpu.SemaphoreType.DMA((max(ns, 1), 3)),
                        pltpu.SemaphoreType.DMA((n + ns, 2))] + [pltpu.VMEM(s.shape, s.dtype) for s in everything],
        compiler_params=_params(1))(h, gain, *shards, *small)
    return outs[0], list(outs[1:n + 1]), list(outs[n + 1:])


def gather_ici_copies(src, dst, send_sems, recv_sems):
    x, y, c, chips = _place()
    me = 2 * x + y
    pairs = []
    for i in range(len(src)):
        rows = _half(c, src[i].shape[0])
        for j, (px, py) in enumerate(chips):
            def copy(slot):
                return pltpu.make_async_remote_copy(
                    src_ref=src[i].at[rows], dst_ref=dst[i].at[slot, rows], send_sem=send_sems.at[i, j],
                    recv_sem=recv_sems.at[i, j], device_id=(px, py, c), device_id_type=MESH)
            pairs.append((copy(me), copy(2 * px + py)))
    return pairs


def scatter_ici_copies(src, dst, send_sems, recv_sems):
    x, y, c, chips = _place()
    me = 2 * x + y
    pairs = []
    for i in range(len(src)):
        for j, (px, py) in enumerate(chips):
            def copy(from_slot, to_slot):
                return pltpu.make_async_remote_copy(
                    src_ref=src[i].at[from_slot], dst_ref=dst[i].at[to_slot], send_sem=send_sems.at[i, j],
                    recv_sem=recv_sems.at[i, j], device_id=(px, py, c), device_id_type=MESH)
            pairs.append((copy(2 * px + py, me), copy(me, 2 * px + py)))
    return pairs


def _ride_along(pairs, grid_ids, grid_sizes, local=()):
    first = grid_ids[0] == 0
    last = grid_ids[0] == grid_sizes[0] - 1
    for g, size in zip(grid_ids[1:], grid_sizes[1:]):
        first = first & (g == 0)
        last = last & (g == size - 1)

    @pl.when(first)
    def _():
        for outgoing, _ in pairs:
            outgoing.start()
        for src, stage, _, sems in local:
            pltpu.make_async_copy(src, stage, sems.at[0]).start()

    @pl.when(last)
    def _():
        for _, incoming in pairs:
            incoming.wait_recv()
        for outgoing, _ in pairs:
            outgoing.wait_send()
        _place_locally(local)


def _place_locally(local):
    for src, stage, _, sems in local:
        pltpu.make_async_copy(src, stage, sems.at[0]).wait()
    placed = [pltpu.make_async_copy(stage, dst, sems.at[1]) for _, stage, dst, sems in local]
    for cp in placed:
        cp.start()
    for cp in placed:
        cp.wait()


def forward_copies(src, dst, send_sems, recv_sems):
    x, y, c, chips = _place()
    pairs = []
    for i in range(len(src)):
        for j, (px, py) in enumerate(chips):
            def copy(half_of):
                rows = _half(half_of, src[i].shape[1])
                return pltpu.make_async_remote_copy(
                    src_ref=src[i].at[2 * px + py, rows], dst_ref=dst[i].at[2 * px + py, rows], send_sem=send_sems.at[i, j],
                    recv_sem=recv_sems.at[i, j], device_id=(x, y, 1 - c), device_id_type=MESH)
            pairs.append((copy(c), copy(1 - c)))
    return pairs


def attention_out_proj(name, attn, w_o, resid, next_gain, arriving, own):
    t, kdim = attn.shape
    n = w_o.shape[1]
    tm = _tile(t, 512)
    m = len(arriving)

    def body(x_ref, w_ref, r_ref, g_ref, *refs):
        src, own_refs = refs[:m], refs[m:2 * m]
        h_ref, a_ref = refs[2 * m:2 * m + 2]
        dst = refs[2 * m + 2:3 * m + 2]
        stages = refs[3 * m + 2:4 * m + 2]
        send_sems, recv_sems, local_sems = refs[4 * m + 2:]
        me = 2 * lax.axis_index("x") + lax.axis_index("y")
        placed = [(own_refs[i], stages[i], dst[i].at[me], local_sems.at[i]) for i in range(m)]
        _ride_along(forward_copies(src, dst, send_sems, recv_sems), (pl.program_id(0),), (t // tm,), placed)
        y = r_ref[...] + jnp.dot(x_ref[...].astype(BF16), w_ref[...], preferred_element_type=F32)
        h_ref[...] = y
        a_ref[...] = (y * _rstd(y) * g_ref[...]).astype(BF16)

    row = pl.BlockSpec((tm, n), lambda i: (i, 0))
    outs = pl.pallas_call(
        body, name=name, grid=(t // tm,),
        in_specs=[pl.BlockSpec((tm, kdim), lambda i: (i, 0)), _resident((kdim, n), lambda i: (0, 0)), row,
                  pl.BlockSpec((1, n), lambda i: (0, 0))] + [ANY] * (2 * m),
        out_specs=[row, row] + [ANY] * m, out_shape=[S((t, n), F32), S((t, n), BF16)] + [S(g.shape, g.dtype) for g in arriving],
        input_output_aliases={4 + i: 2 + i for i in range(m)},
        scratch_shapes=[pltpu.VMEM(o.shape, o.dtype) for o in own]
        + [pltpu.SemaphoreType.DMA((m, 3)), pltpu.SemaphoreType.DMA((m, 3)), pltpu.SemaphoreType.DMA((m, 2))],
        compiler_params=_params(1))(attn, w_o, resid, next_gain, *arriving, *own)
    return outs[0], outs[1], list(outs[2:])


def swap_copies(src, dst, send_sems, recv_sems):
    x, y, c, _ = _place()
    pairs = []
    for i in range(len(src)):
        cp = pltpu.make_async_remote_copy(
            src_ref=src[i].at[:, _half(1 - c, src[i].shape[1]), :], dst_ref=dst[i], send_sem=send_sems.at[i],
            recv_sem=recv_sems.at[i], device_id=(x, y, 1 - c), device_id_type=MESH)
        pairs.append((cp, cp))
    return pairs


def _swap_shapes(grads):
    return [S((g.shape[0], g.shape[1] // 2, g.shape[2]), g.dtype) for g in grads]


def sibling_swap_halves(name, grads):
    n = len(grads)

    def body(*refs):
        pairs = swap_copies(refs[:n], refs[n:2 * n], *refs[2 * n:])
        for outgoing, _ in pairs:
            outgoing.start()
        for _, incoming in pairs:
            incoming.wait_recv()
        for outgoing, _ in pairs:
            outgoing.wait_send()

    return pl.pallas_call(
        body, name=name, in_specs=[ANY] * n, out_specs=[ANY] * n, out_shape=_swap_shapes(grads),
        scratch_shapes=[pltpu.SemaphoreType.DMA((n,)), pltpu.SemaphoreType.DMA((n,))],
    )(*grads)


def add_halves(name, g, rx):
    _, r, cdim = g.shape
    r2 = r // 2
    tr = _tile(r2, 512, 16)
    nb = r2 // tr

    def body(lo_ref, hi_ref, rx_ref, o_ref):
        mine = jnp.where(lax.axis_index("c") == 0, lo_ref[...], hi_ref[...])
        o_ref[...] = (mine.astype(F32) + rx_ref[...].astype(F32)).astype(BF16)

    half = pl.BlockSpec((None, tr, cdim), lambda k, i: (k, i, 0))
    return pl.pallas_call(
        body, name=name, grid=(N_CHIPS, nb),
        in_specs=[half, pl.BlockSpec((None, tr, cdim), lambda k, i: (k, nb + i, 0)), half],
        out_specs=half, out_shape=S((N_CHIPS, r2, cdim), BF16), compiler_params=_params(2))(g, g, rx)


def sum_chips(name, arrived, mine):
    _, r2, cdim = arrived.shape
    tr = _tile(r2, 512, 16)

    def body(a_ref, m_ref, o_ref):
        me = 2 * lax.axis_index("x") + lax.axis_index("y")
        acc = jnp.zeros((tr, cdim), F32)
        for k in range(N_CHIPS):
            acc = acc + jnp.where(me == k, m_ref[k], a_ref[k]).astype(F32)
        o_ref[...] = acc

    slots = pl.BlockSpec((N_CHIPS, tr, cdim), lambda i: (0, i, 0))
    return pl.pallas_call(
        body, name=name, grid=(r2 // tr,), in_specs=[slots, slots],
        out_specs=pl.BlockSpec((tr, cdim), lambda i: (i, 0)), out_shape=S((r2, cdim), F32), compiler_params=_params(1))(arrived, mine)


def join_copies(src, dst, where, send_sems, recv_sems):
    x, y, c, _ = _place()
    pairs = []
    for i in range(len(src)):
        def copy(half_of):
            r2 = src[i].shape[0]
            rows = pl.ds(pl.multiple_of(where[i][1] + half_of * r2, 8), r2)
            return pltpu.make_async_remote_copy(
                src_ref=src[i], dst_ref=dst[where[i][0]].at[rows], send_sem=send_sems.at[i],
                recv_sem=recv_sems.at[i], device_id=(x, y, 1 - c), device_id_type=MESH)
        pairs.append((copy(c), copy(1 - c)))
    return pairs


def sibling_join_halves(name, halves, targets, where):
    n = len(halves)

    def body(*refs):
        src, dst = refs[:n], refs[n:n + len(targets)]
        send_sems, recv_sems, local_sems = refs[n + len(targets):n + len(targets) + 3]
        stages = refs[n + len(targets) + 3:]
        c = lax.axis_index("c")
        own = [(src[i], stages[i],
                dst[where[i][0]].at[pl.ds(pl.multiple_of(where[i][1] + c * src[i].shape[0], 8), src[i].shape[0])],
                local_sems.at[i]) for i in range(n)]
        for s_ref, stage, _, sems in own:
            pltpu.make_async_copy(s_ref, stage, sems.at[0]).start()
        pairs = join_copies(src, dst, where, send_sems, recv_sems)
        for outgoing, _ in pairs:
            outgoing.start()
        for _, incoming in pairs:
            incoming.wait_recv()
        for outgoing, _ in pairs:
            outgoing.wait_send()
        _place_locally(own)

    return list(pl.pallas_call(
        body, name=name, in_specs=[ANY] * n, out_specs=[ANY] * len(targets), out_shape=[S(tg, F32) for tg in targets],
        scratch_shapes=[pltpu.SemaphoreType.DMA((n,)), pltpu.SemaphoreType.DMA((n,)), pltpu.SemaphoreType.DMA((n, 2))]
        + [pltpu.VMEM(h.shape, h.dtype) for h in halves],
    )(*halves))


def all_reduce_small(name, packed):
    rows, width = packed.shape

    def body(x_ref, o_ref, gathered, send_sems, recv_sems):
        x, y, c, _ = _place()
        me = 4 * x + 2 * y + c
        gathered[me] = x_ref[...]
        flips = [(fx, fy, fc) for fx in (0, 1) for fy in (0, 1) for fc in (0, 1)][1:]

        def copy(r, slot, to):
            return pltpu.make_async_remote_copy(
                src_ref=x_ref, dst_ref=gathered.at[slot], send_sem=send_sems.at[r], recv_sem=recv_sems.at[r],
                device_id=to, device_id_type=MESH)

        def peer(f):
            return (x ^ f[0], y ^ f[1], c ^ f[2])

        sent = [copy(r, me, peer(f)) for r, f in enumerate(flips)]
        for cp in sent:
            cp.start()
        for r, f in enumerate(flips):
            px, py, pc = peer(f)
            copy(r, 4 * px + 2 * py + pc, peer(f)).wait_recv()
        for cp in sent:
            cp.wait_send()
        acc = gathered[0]
        for k in range(1, N_DEV):
            acc = acc + gathered[k]
        o_ref[...] = acc

    vmem = pl.BlockSpec(memory_space=pltpu.VMEM)
    return pl.pallas_call(
        body, name=name, in_specs=[vmem], out_specs=vmem, out_shape=S((rows, width), F32),
        scratch_shapes=[pltpu.VMEM((N_DEV, rows, width), F32), pltpu.SemaphoreType.DMA((N_DEV - 1,)),
                        pltpu.SemaphoreType.DMA((N_DEV - 1,))],
    )(packed)


def _rope_tables(positions):
    inv_freq = 1.0 / (ROPE_THETA ** (jnp.arange(0, ROPE, 2, dtype=F32) / ROPE))
    ang = positions.astype(F32)[:, None] * inv_freq
    return jnp.cos(ang), jnp.sin(ang)


def _unstack_cols(w):
    k4, k, n4 = w.shape
    return jnp.transpose(w, (1, 0, 2)).reshape(k, k4 * n4)


def _stack_cols(w):
    k, n = w.shape
    return jnp.transpose(w.reshape(k, N_CHIPS, n // N_CHIPS), (1, 0, 2))


def kernel(x, positions, mla_norm, mla_w_in, mla_g_cq, mla_g_ckv, mla_w_uq, mla_w_ukv, mla_w_o, conv_norm, conv_w_in, conv_w, conv_w_out, ffn_norm, ffn_w_gate, ffn_w_up, ffn_w_down, final_norm, loss_target, m_mla_norm, m_mla_w_in, m_mla_g_cq, m_mla_g_ckv, m_mla_w_uq, m_mla_w_ukv, m_mla_w_o, m_conv_norm, m_conv_w_in, m_conv_w, m_conv_w_out, m_ffn_norm, m_ffn_w_gate, m_ffn_w_up, m_ffn_w_down, m_final_norm, v_mla_norm, v_mla_w_in, v_mla_g_cq, v_mla_g_ckv, v_mla_w_uq, v_mla_w_ukv, v_mla_w_o, v_conv_norm, v_conv_w_in, v_conv_w, v_conv_w_out, v_ffn_norm, v_ffn_w_gate, v_ffn_w_up, v_ffn_w_down, v_final_norm):
    weights = dict(mla_norm=mla_norm, mla_w_in=mla_w_in, mla_g_cq=mla_g_cq, mla_g_ckv=mla_g_ckv, mla_w_uq=mla_w_uq,
                   mla_w_ukv=mla_w_ukv, mla_w_o=mla_w_o, conv_norm=conv_norm, conv_w_in=conv_w_in, conv_w=conv_w,
                   conv_w_out=conv_w_out, ffn_norm=ffn_norm, ffn_w_gate=ffn_w_gate, ffn_w_up=ffn_w_up,
                   ffn_w_down=ffn_w_down, final_norm=final_norm)
    m_in = dict(mla_norm=m_mla_norm, mla_w_in=m_mla_w_in, mla_g_cq=m_mla_g_cq, mla_g_ckv=m_mla_g_ckv, mla_w_uq=m_mla_w_uq,
                mla_w_ukv=m_mla_w_ukv, mla_w_o=m_mla_w_o, conv_norm=m_conv_norm, conv_w_in=m_conv_w_in, conv_w=m_conv_w,
                conv_w_out=m_conv_w_out, ffn_norm=m_ffn_norm, ffn_w_gate=m_ffn_w_gate, ffn_w_up=m_ffn_w_up,
                ffn_w_down=m_ffn_w_down, final_norm=m_final_norm)
    v_in = dict(mla_norm=v_mla_norm, mla_w_in=v_mla_w_in, mla_g_cq=v_mla_g_cq, mla_g_ckv=v_mla_g_ckv, mla_w_uq=v_mla_w_uq,
                mla_w_ukv=v_mla_w_ukv, mla_w_o=v_mla_w_o, conv_norm=v_conv_norm, conv_w_in=v_conv_w_in, conv_w=v_conv_w,
                conv_w_out=v_conv_w_out, ffn_norm=v_ffn_norm, ffn_w_gate=v_ffn_w_gate, ffn_w_up=v_ffn_w_up,
                ffn_w_down=v_ffn_w_down, final_norm=v_final_norm)
    big = ["mla_w_in", "mla_w_uq", "mla_w_ukv", "mla_w_o", "conv_w_in", "conv_w_out", "ffn_w_gate", "ffn_w_up", "ffn_w_down"]
    order = list(weights)

    t, d = x.shape[1], x.shape[2]
    h0 = x.reshape(t, d)
    target = loss_target.reshape(t, d)
    cos, sin = _rope_tables(positions.reshape(t))

    def rows2d(a):
        return a.reshape(-1, a.shape[-1])

    first, later = big[:4], big[4:]
    shards = {n: rows2d(weights[n]).astype(BF16) for n in big}
    d4 = d // N_CHIPS
    a0, first_gathered, (conv_norm_slots, conv_w_slots) = gather_weight_shards(
        [shards[n] for n in first], [conv_norm.reshape(1, d4), conv_w.reshape(3, d4)], h0, mla_norm)
    gathered = dict(zip(first, first_gathered))
    conv_norm_full = conv_norm_slots.reshape(1, d)
    conv_w_full = jnp.transpose(conv_w_slots, (1, 0, 2)).reshape(3, d)
    w_in = gathered["mla_w_in"].reshape(-1, gathered["mla_w_in"].shape[-1])
    w_uq = _unstack_cols(gathered["mla_w_uq"])
    w_ukv = _unstack_cols(gathered["mla_w_ukv"])
    w_o = gathered["mla_w_o"].reshape(-1, d)

    chip = 2 * lax.axis_index("x") + lax.axis_index("y")

    def pack_rows(rows):
        idx = lax.broadcasted_iota(jnp.int32, (SMALL_ROWS, d), 0)
        out = jnp.zeros((SMALL_ROWS, d), F32)
        for r, row in enumerate(rows):
            out = out + jnp.where(idx == r, row, 0.0)
        return out


    proj, cq, ckv, kr = mla_in_proj("mla_in_proj", a0, w_in, mla_g_cq, mla_g_ckv, cos, sin)
    qh, kh, vh, conv_arriving = qkv_heads("qkv_heads", cq, ckv, w_uq, w_ukv, kr, cos, sin, [shards[n] for n in later[:2]])
    attn, lse, ffn_arriving = attention_fwd("attention_fwd", qh, kh, vh, [shards[n] for n in later[2:]])
    h1, a1, handed = attention_out_proj("mla_out_proj", attn, w_o, h0, ffn_norm[0:1], conv_arriving + ffn_arriving,
                                        [shards[n] for n in later])
    gathered.update(zip(later, handed))
    cw_in = _unstack_cols(gathered["conv_w_in"])
    cw_out = gathered["conv_w_out"].reshape(-1, d)
    wg_all, wu_all, wd_all = gathered["ffn_w_gate"], gathered["ffn_w_up"], gathered["ffn_w_down"]

    def ffn_forward(tag, h, a, layer, next_gain):
        g, u, z = ffn_up(f"ffn{tag}_up", a, wg_all, wu_all, layer)
        return g, u, z, ffn_down(f"ffn{tag}_down", z, wd_all, layer, h, next_gain)

    g0, u0, z0, (h2, a2) = ffn_forward(0, h1, a1, 0, conv_norm_full)
    bcx, yc = conv_in_proj("conv_in_proj", a2, cw_in, conv_w_full)
    h3, a3 = linear("conv_out_proj", yc, cw_out, F32, resid=h2, next_gain=ffn_norm[1:2])
    g1, u1, z1 = ffn_up("ffn1_up", a3, wg_all, wu_all, 1)
    dh4, d_final_norm, loss_local = ffn_down_loss("ffn1_down_loss", z1, wd_all, 1, h3, final_norm.reshape(1, d), target)

    def ffn_backward(tag, dh, h, layer, a, g, u, z, swap=()):
        dg, du, swapped = ffn_bwd_hidden(f"ffn{tag}_bwd_hidden", dh, wd_all, layer, g, u, swap)
        d_wd = ffn_wgrad_down(f"ffn{tag}_wgrad_down", z, dh)
        dh_prev, d_norm = ffn_bwd_input(f"ffn{tag}_bwd_input", dg, du, wg_all, wu_all, layer, h, ffn_norm[layer:layer + 1], dh)
        d_wg = ffn_wgrad_up(f"ffn{tag}_wgrad_gate", a, dg)
        d_wu = ffn_wgrad_up(f"ffn{tag}_wgrad_up", a, du)
        return dh_prev, d_norm, [d_wg, d_wu, d_wd], swapped

    def pair_sums(tag, local, from_sibling):
        return [add_halves(f"pair_sum_{tag}{i}", g, r) for i, (g, r) in enumerate(zip(local, from_sibling))]

    def sum_from_chips(tag, pairs, arrived):
        return [sum_chips(f"chip_sum_{tag}{i}", a, p) for i, (a, p) in enumerate(zip(arrived, pairs))]

    def shard_shape(n):
        return rows2d(weights[n]).shape

    dh3, d_ffn_norm1, ffn1_grads, _ = ffn_backward(1, dh4, h3, 1, a3, g1, u1, z1)

    dyc = linear_nt("conv_out_bwd_input", dh3, cw_out, F32)
    d_cw_out = wgrad("conv_out_wgrad", yc, dh3)
    dbcx, d_conv_w = conv_bwd("conv_bwd", bcx, conv_w_full, dyc)
    dh2, d_conv_norm = conv_in_bwd_input("conv_in_bwd_input", dbcx, cw_in, h2, conv_norm_full, dh3)
    d_cw_in = conv_in_wgrad("conv_in_wgrad", a2, dbcx)

    second = [d_cw_in, d_cw_out.reshape(N_CHIPS, -1, d)] + ffn1_grads
    dh1, d_ffn_norm0, ffn0_grads, second_swapped = ffn_backward(0, dh2, h1, 0, a1, g0, u0, z0, second)
    d_w_o = wgrad("mla_out_wgrad", attn, dh1)
    first_part = ffn0_grads + [d_w_o.reshape(N_CHIPS, -1, d)]
    d_attn, delta, first_swapped = attention_out_bwd("mla_out_bwd_input", dh1, w_o, attn, first_part)
    rest_pairs = pair_sums("rest", second + first_part, second_swapped + first_swapped)
    dqh, dkh, dvh, rest_arrived = attention_bwd("attention_bwd", qh, kh, vh, d_attn, lse, delta, rest_pairs)
    rd, rf = ffn0_grads[0].shape[1], ffn0_grads[2].shape[1]
    rest_where = [(0, 0), (1, 0), (2, rd), (3, rd), (4, rf), (2, 0), (3, 0), (4, 0), (5, 0)]
    rest_names = later + ["mla_w_o"]
    dq, dkv, dproj, d_g_cq, d_g_ckv, rest_grads = qkv_heads_bwd(
        "qkv_heads_bwd", dqh, dkh, dvh, cos, sin, w_uq, w_ukv, proj, mla_g_cq, mla_g_ckv,
        sum_from_chips("rest", rest_pairs, rest_arrived), [shard_shape(n) for n in rest_names], rest_where)
    grads = dict(zip(rest_names, rest_grads))
    d_w_uq = wgrad("mla_q_up_wgrad", cq, dq)
    d_w_ukv = wgrad("mla_kv_up_wgrad", ckv, dkv)
    d_w_in = wgrad("mla_in_wgrad", a0, dproj)
    mla_local = [d_w_in.reshape(N_CHIPS, -1, d_w_in.shape[-1]), _stack_cols(d_w_uq), _stack_cols(d_w_ukv)]
    mla_pairs = pair_sums("mla", mla_local, sibling_swap_halves("sibling_swap_mla", mla_local))
    grad_x, d_mla_norm, mla_arrived = linear_nt_norm_bwd("mla_in_bwd_input", dproj, w_in, h0, mla_norm, dh1, mla_pairs)

    grads.update(zip(first[:3], sibling_join_halves("sibling_join_mla", sum_from_chips("mla", mla_pairs, mla_arrived),
                                                    [shard_shape(n) for n in first[:3]], [(i, 0) for i in range(3)])))

    def pad_row(v):
        return jnp.pad(v, ((0, 0), (0, d - v.shape[1])))

    small = all_reduce_small("all_reduce_small_grads", pack_rows([
        d_mla_norm, pad_row(d_g_cq), pad_row(d_g_ckv), d_ffn_norm0, d_ffn_norm1, d_final_norm, d_conv_norm,
        d_conv_w[0:1], d_conv_w[1:2], d_conv_w[2:3], jnp.broadcast_to(loss_local, (1, d))]))
    loss = small[10, 0]
    grads["mla_norm"] = small[0:1]
    grads["mla_g_cq"] = small[1:2, :mla_g_cq.shape[1]]
    grads["mla_g_ckv"] = small[2:3, :mla_g_ckv.shape[1]]
    grads["ffn_norm"] = small[3:5]
    grads["final_norm"] = small[5:6]
    grads["conv_norm"] = lax.dynamic_slice(small[6:7], (0, chip * d4), (1, d4))
    grads["conv_w"] = lax.dynamic_slice(small[7:10], (0, chip * d4), (3, d4))

    outs_g, outs_d, outs_m, outs_v = [], [], [], []
    for n in order:
        w = weights[n]
        if w.ndim == 3 and w.shape[2] % 128 and w.shape[1] % 128 == 0:
            results = adamw_swapped(f"adamw_{n}", jnp.swapaxes(w, 1, 2), grads[n].reshape(-1, w.shape[2]),
                                    jnp.swapaxes(m_in[n], 1, 2), jnp.swapaxes(v_in[n], 1, 2))
            grad_w, delta_w, new_m, new_v = [jnp.swapaxes(o, 1, 2) for o in results]
        else:
            delta_w, new_m, new_v = adamw(f"adamw_{n}", rows2d(w), grads[n].reshape(rows2d(w).shape), rows2d(m_in[n]), rows2d(v_in[n]))
            grad_w = grads[n]
        outs_g.append(grad_w.reshape(w.shape))
        outs_d.append(delta_w.reshape(w.shape))
        outs_m.append(new_m.reshape(w.shape))
        outs_v.append(new_v.reshape(w.shape))
    return (loss, grad_x.reshape(x.shape), *outs_g, *outs_d, *outs_m, *outs_v)
```

```python
import math

import jax
import jax.numpy as jnp
from jax import lax
from jax.experimental import pallas as pl
from jax.experimental.pallas import tpu as pltpu

F32 = jnp.float32
BF16 = jnp.bfloat16
S = jax.ShapeDtypeStruct

N_HEADS = 8
NOPE = 128
ROPE = 64
HALF = ROPE // 2
VDIM = 128
QK = NOPE + ROPE
CHUNK = 64
ROPE_THETA = 10000.0
RMS_EPS = 1e-6
ADAM_LR = 0.001
ADAM_B1 = 0.9
ADAM_B2 = 0.999
ADAM_EPS = 1e-08
ADAM_WD = 0.01
ADAM_STEP = 10

N_CHIPS = 4
N_DEV = 8
MASK_VALUE = -1e30
SCORE_SCALE = 1.0 / math.sqrt(QK)
LOG2_E = math.log2(math.e)
SCORE_SCALE_LOG2 = SCORE_SCALE * LOG2_E
VMEM_LIMIT = 48 * 1024 * 1024
VMEM_LIMIT_WHOLE_HEAD = 58 * 1024 * 1024
ATT_BLOCK = 512
CONV_SAVED_DTYPE = jnp.bfloat16
SMALL_ROWS = 16

_NN = (((1,), (0,)), ((), ()))
_NT = (((1,), (1,)), ((), ()))
_TN = (((0,), (0,)), ((), ()))
MESH = pl.DeviceIdType.MESH
ANY = pl.BlockSpec(memory_space=pl.ANY)


def _params(n_axes, vmem_limit=VMEM_LIMIT):
    return pltpu.CompilerParams(dimension_semantics=("arbitrary",) * n_axes, vmem_limit_bytes=vmem_limit)


def _tile(n, cap, mult=8):
    for t in range(min(cap, n), 0, -1):
        if n % t == 0 and t % mult == 0:
            return t
    return n


def _sigmoid(x):
    return 0.5 * jnp.tanh(0.5 * x) + 0.5


def _mm(name, a_ops, b_ops, products, dims, grid, k_axis, outs, acc_shape, epilogue, extra_ops=()):
    na, nb, ne, no = len(a_ops), len(b_ops), len(extra_ops), len(outs)
    n_acc = 1 + max(c for _, _, c in products)
    nk = 1 if k_axis is None else grid[k_axis]

    def body(*refs):
        a_refs = refs[:na]
        b_refs = refs[na:na + nb]
        e_refs = refs[na + nb:na + nb + ne]
        o_refs = refs[na + nb + ne:na + nb + ne + no]
        acc_refs = refs[na + nb + ne + no:]

        def partial_sums():
            vals = [None] * n_acc
            for ai, bi, ci in products:
                d = lax.dot_general(a_refs[ai][...].astype(BF16), b_refs[bi][...].astype(BF16), dims,
                                    preferred_element_type=F32)
                vals[ci] = d if vals[ci] is None else vals[ci] + d
            return vals

        if nk == 1:
            epilogue(partial_sums(), e_refs, o_refs)
        else:
            k = pl.program_id(k_axis)

            @pl.when(k == 0)
            def _():
                for acc in acc_refs:
                    acc[...] = jnp.zeros_like(acc)

            for acc, v in zip(acc_refs, partial_sums()):
                acc[...] += v

            @pl.when(k == nk - 1)
            def _():
                epilogue([acc[...] for acc in acc_refs], e_refs, o_refs)

    ops = list(a_ops) + list(b_ops) + list(extra_ops)
    return pl.pallas_call(
        body, name=name, grid=grid,
        in_specs=[s for _, s in ops], out_specs=[s for _, s in outs], out_shape=[o for o, _ in outs],
        scratch_shapes=[pltpu.VMEM(acc_shape, F32) for _ in range(n_acc if nk > 1 else 0)],
        compiler_params=_params(len(grid)),
    )(*[a for a, _ in ops])


def _store(accs, e_refs, o_refs):
    o_refs[0][...] = accs[0].astype(o_refs[0].dtype)


def linear(name, x, w, out_dtype, resid=None, next_gain=None):
    t, k = x.shape
    n = w.shape[1]
    tm = _tile(t, 512)
    tn = n if n <= 2048 else _tile(n, 1024, 128)
    tile = pl.BlockSpec((tm, tn), lambda j, i: (i, j))
    extra = [] if resid is None else [(resid, tile)]
    outs = [(S((t, n), out_dtype), tile)]
    if next_gain is not None:
        assert tn == n
        extra.append((next_gain, pl.BlockSpec((1, n), lambda j, i: (0, 0))))
        outs.append((S((t, n), BF16), tile))

    def epilogue(accs, e_refs, o_refs):
        y = accs[0] if resid is None else e_refs[0][...] + accs[0]
        o_refs[0][...] = y.astype(out_dtype)
        if next_gain is not None:
            o_refs[1][...] = (y * _rstd(y) * e_refs[-1][...]).astype(BF16)

    res = _mm(name, [(x, pl.BlockSpec((tm, k), lambda j, i: (i, 0)))], [(w, pl.BlockSpec((k, tn), lambda j, i: (0, j)))],
              [(0, 0, 0)], _NN, (n // tn, t // tm), None, outs, None, epilogue, extra)
    return res[0] if next_gain is None else res


def linear_nt(name, dy, w, out_dtype):
    t, n = dy.shape
    k = w.shape[0]
    tm = _tile(t, 512)
    tc = n if n <= 2048 else _tile(n, 1024, 128)
    return _mm(name, [(dy, pl.BlockSpec((tm, tc), lambda i, c: (i, c)))], [(w, pl.BlockSpec((k, tc), lambda i, c: (0, c)))],
               [(0, 0, 0)], _NT, (t // tm, n // tc), 1,
               [(S((t, k), out_dtype), pl.BlockSpec((tm, k), lambda i, c: (i, 0)))], (tm, k), _store)[0]


def wgrad(name, x, dy):
    t, k = x.shape
    n = dy.shape[1]
    tk = _tile(t, 512)
    tn = n if n <= 1024 else _tile(n, 1024, 128)
    return _mm(name, [(x, pl.BlockSpec((tk, k), lambda j, s: (s, 0)))], [(dy, pl.BlockSpec((tk, tn), lambda j, s: (s, j)))],
               [(0, 0, 0)], _TN, (n // tn, t // tk), 1,
               [(S((k, n), BF16), pl.BlockSpec((k, tn), lambda j, s: (0, j)))], (k, tn), _store)[0]


def _resident(shape, index_map):
    return pl.BlockSpec(shape, index_map, pipeline_mode=pl.Buffered(1))


def ffn_up(name, a, wg_all, wu_all, layer):
    t, d = a.shape
    f4 = wg_all.shape[2]
    tm = _tile(t, 512)
    w_spec = _resident((N_CHIPS, d, f4), lambda i: (0, layer, 0))
    h_spec = pl.BlockSpec((N_CHIPS, tm, f4), lambda i: (0, i, 0))

    def body(a_ref, wg_ref, wu_ref, zg_ref, zu_ref, z_ref):
        av = a_ref[...]
        for k in range(N_CHIPS):
            g = jnp.dot(av, wg_ref[k], preferred_element_type=F32)
            u = jnp.dot(av, wu_ref[k], preferred_element_type=F32)
            sg = _sigmoid(g)
            silu = g * sg
            zg_ref[k] = (u * (sg * (1.0 + g * (1.0 - sg)))).astype(BF16)
            zu_ref[k] = silu.astype(BF16)
            z_ref[k] = (silu * u).astype(BF16)

    return pl.pallas_call(
        body, name=name, grid=(t // tm,), in_specs=[pl.BlockSpec((tm, d), lambda i: (i, 0)), w_spec, w_spec],
        out_specs=[h_spec] * 3, out_shape=[S((N_CHIPS, t, f4), BF16)] * 3, compiler_params=_params(1))(a, wg_all, wu_all)


def ffn_down(name, z, wd_all, layer, resid, next_gain=None):
    _, t, f4 = z.shape
    d = wd_all.shape[2]
    tm = _tile(t, 512)
    row = pl.BlockSpec((tm, d), lambda i: (i, 0))
    normed = next_gain is not None

    def body(z_ref, wd_ref, r_ref, *refs):
        acc = r_ref[...]
        for k in range(N_CHIPS):
            acc = acc + jnp.dot(z_ref[k], wd_ref[k], preferred_element_type=F32)
        refs[-2 if normed else -1][...] = acc
        if normed:
            refs[-1][...] = (acc * _rstd(acc) * refs[0][...]).astype(BF16)

    res = pl.pallas_call(
        body, name=name, grid=(t // tm,),
        in_specs=[pl.BlockSpec((N_CHIPS, tm, f4), lambda i: (0, i, 0)), _resident((N_CHIPS, f4, d), lambda i: (0, layer, 0)), row]
        + ([pl.BlockSpec((1, d), lambda i: (0, 0))] if normed else []),
        out_specs=[row] * (2 if normed else 1), out_shape=[S((t, d), F32)] + ([S((t, d), BF16)] if normed else []),
        compiler_params=_params(1))(z, wd_all, resid, *([next_gain] if normed else []))
    return res if normed else res[0]


def ffn_bwd_hidden(name, dh, wd_all, layer, zg, zu, swap=()):
    t, d = dh.shape
    f4 = zg.shape[2]
    tm = _tile(t, 512)
    h_spec = pl.BlockSpec((N_CHIPS, tm, f4), lambda i: (0, i, 0))
    n = len(swap)

    def body(dh_ref, wd_ref, zg_ref, zu_ref, *refs):
        dg_ref, du_ref = refs[n:n + 2]
        if n:
            _ride_along(swap_copies(refs[:n], refs[n + 2:2 * n + 2], *refs[2 * n + 2:]), (pl.program_id(0),), (t // tm,))
        dhb = dh_ref[...].astype(BF16)
        for k in range(N_CHIPS):
            dz = lax.dot_general(dhb, wd_ref[k], _NT, preferred_element_type=F32)
            dg_ref[k] = (dz * zg_ref[k].astype(F32)).astype(BF16)
            du_ref[k] = (dz * zu_ref[k].astype(F32)).astype(BF16)

    outs = pl.pallas_call(
        body, name=name, grid=(t // tm,),
        in_specs=[pl.BlockSpec((tm, d), lambda i: (i, 0)), _resident((N_CHIPS, f4, d), lambda i: (0, layer, 0)), h_spec, h_spec]
        + [ANY] * n,
        out_specs=[h_spec] * 2 + [ANY] * n, out_shape=[S((N_CHIPS, t, f4), BF16)] * 2 + _swap_shapes(swap),
        scratch_shapes=[pltpu.SemaphoreType.DMA((n,)), pltpu.SemaphoreType.DMA((n,))] if n else [],
        compiler_params=_params(1))(dh, wd_all, zg, zu, *swap)
    return outs[0], outs[1], list(outs[2:])


def _norm_bwd_specs(tm, d):
    row = pl.BlockSpec((tm, d), lambda i: (i, 0))
    vec = pl.BlockSpec((1, d), lambda i: (0, 0))
    return [row, vec, row], [row, vec]


def _norm_bwd_tail(da, h_ref, g_ref, dhi_ref, dho_ref, dgain_ref):
    dx, dgain = _rms_bwd(h_ref[...], g_ref[...], da)
    dho_ref[...] = dhi_ref[...] + dx

    @pl.when(pl.program_id(0) == 0)
    def _():
        dgain_ref[...] = jnp.zeros_like(dgain_ref)

    dgain_ref[...] += dgain


def ffn_bwd_input(name, dg, du, wg_all, wu_all, layer, h, gain, dh_in):
    _, t, f4 = dg.shape
    d = h.shape[1]
    tm = _tile(t, 512)
    h_spec = pl.BlockSpec((N_CHIPS, tm, f4), lambda i: (0, i, 0))
    w_spec = _resident((N_CHIPS, d, f4), lambda i: (0, layer, 0))
    tail_in, tail_out = _norm_bwd_specs(tm, d)

    def body(dg_ref, du_ref, wg_ref, wu_ref, *tail):
        acc = jnp.zeros((tm, d), F32)
        for k in range(N_CHIPS):
            acc = acc + lax.dot_general(dg_ref[k], wg_ref[k], _NT, preferred_element_type=F32)
            acc = acc + lax.dot_general(du_ref[k], wu_ref[k], _NT, preferred_element_type=F32)
        _norm_bwd_tail(acc, *tail)

    return pl.pallas_call(
        body, name=name, grid=(t // tm,), in_specs=[h_spec, h_spec, w_spec, w_spec] + tail_in, out_specs=tail_out,
        out_shape=[S((t, d), F32), S((1, d), F32)], compiler_params=_params(1))(dg, du, wg_all, wu_all, h, gain, dh_in)


def ffn_wgrad_up(name, a, dy):
    t, d = a.shape
    f4 = dy.shape[2]
    tk = _tile(t, 512)
    nt = t // tk

    def body(a_ref, dy_ref, o_ref, acc):
        s = pl.program_id(0)

        @pl.when(s == 0)
        def _():
            acc[...] = jnp.zeros_like(acc)

        at = a_ref[...].T
        for k in range(N_CHIPS):
            acc[k] += jnp.dot(at, dy_ref[k], preferred_element_type=F32)

        @pl.when(s == nt - 1)
        def _():
            o_ref[...] = acc[...].astype(BF16)

    return pl.pallas_call(
        body, name=name, grid=(nt,),
        in_specs=[pl.BlockSpec((tk, d), lambda s: (s, 0)), pl.BlockSpec((N_CHIPS, tk, f4), lambda s: (0, s, 0))],
        out_specs=pl.BlockSpec((N_CHIPS, d, f4), lambda s: (0, 0, 0)), out_shape=S((N_CHIPS, d, f4), BF16),
        scratch_shapes=[pltpu.VMEM((N_CHIPS, d, f4), F32)], compiler_params=_params(1))(a, dy)


def ffn_wgrad_down(name, z, dh):
    _, t, f4 = z.shape
    d = dh.shape[1]
    tk = _tile(t, 512)
    nt = t // tk

    def body(z_ref, dh_ref, o_ref, acc):
        s = pl.program_id(0)

        @pl.when(s == 0)
        def _():
            acc[...] = jnp.zeros_like(acc)

        dhb = dh_ref[...].astype(BF16)
        for k in range(N_CHIPS):
            acc[k] += lax.dot_general(z_ref[k], dhb, _TN, preferred_element_type=F32)

        @pl.when(s == nt - 1)
        def _():
            o_ref[...] = acc[...].astype(BF16)

    return pl.pallas_call(
        body, name=name, grid=(nt,),
        in_specs=[pl.BlockSpec((N_CHIPS, tk, f4), lambda s: (0, s, 0)), pl.BlockSpec((tk, d), lambda s: (s, 0))],
        out_specs=pl.BlockSpec((N_CHIPS, f4, d), lambda s: (0, 0, 0)), out_shape=S((N_CHIPS, f4, d), BF16),
        scratch_shapes=[pltpu.VMEM((N_CHIPS, f4, d), F32)], compiler_params=_params(1))(z, dh)


def conv_in_proj(name, a, w, conv_w):
    t, d = a.shape
    tm = _tile(t, 256)
    keep = 8

    def body(a_ref, w_ref, cw_ref, bcx_ref, y_ref, u_ref):
        @pl.when(pl.program_id(0) == 0)
        def _():
            u_ref[0:keep, :] = jnp.zeros((keep, d), F32)

        av = a_ref[...]
        b, c, x = [jnp.dot(av, w_ref[:, j * d:(j + 1) * d], preferred_element_type=F32) for j in range(3)]
        for j, part in enumerate((b, c, x)):
            bcx_ref[j] = part.astype(bcx_ref.dtype)
        u_ref[keep:keep + tm, :] = c * x
        uc = (cw_ref[0:1, :] * u_ref[keep - 2:keep - 2 + tm, :] + cw_ref[1:2, :] * u_ref[keep - 1:keep - 1 + tm, :]
              + cw_ref[2:3, :] * u_ref[keep:keep + tm, :])
        y_ref[...] = (b * uc).astype(BF16)
        u_ref[0:keep, :] = u_ref[tm:tm + keep, :]

    return pl.pallas_call(
        body, name=name, grid=(t // tm,),
        in_specs=[pl.BlockSpec((tm, d), lambda i: (i, 0)), _resident((d, 3 * d), lambda i: (0, 0)), pl.BlockSpec((3, d), lambda i: (0, 0))],
        out_specs=[pl.BlockSpec((3, tm, d), lambda i: (0, i, 0)), pl.BlockSpec((tm, d), lambda i: (i, 0))],
        out_shape=[S((3, t, d), CONV_SAVED_DTYPE), S((t, d), BF16)], scratch_shapes=[pltpu.VMEM((tm + keep, d), F32)],
        compiler_params=_params(1))(a, w, conv_w)


def conv_in_bwd_input(name, dbcx, w, h, gain, dh_in):
    _, t, d = dbcx.shape
    tm = _tile(t, 512)
    tail_in, tail_out = _norm_bwd_specs(tm, d)

    def body(g_ref, w_ref, *tail):
        acc = jnp.zeros((tm, d), F32)
        for j in range(3):
            acc = acc + lax.dot_general(g_ref[j], w_ref[:, j * d:(j + 1) * d], _NT, preferred_element_type=F32)
        _norm_bwd_tail(acc, *tail)

    return pl.pallas_call(
        body, name=name, grid=(t // tm,),
        in_specs=[pl.BlockSpec((3, tm, d), lambda i: (0, i, 0)), _resident((d, 3 * d), lambda i: (0, 0))] + tail_in,
        out_specs=tail_out, out_shape=[S((t, d), F32), S((1, d), F32)], compiler_params=_params(1))(dbcx, w, h, gain, dh_in)


def linear_nt_norm_bwd(name, dy, w, h, gain, dh_in, parts=()):
    t, n = dy.shape
    k = w.shape[0]
    tm = _tile(t, 512)
    tail_in, tail_out = _norm_bwd_specs(tm, k)
    m = len(parts)

    def body(dy_ref, w_ref, h_ref, g_ref, dhi_ref, *refs):
        if m:
            _ride_along(scatter_ici_copies(refs[:m], refs[m + 2:2 * m + 2], *refs[2 * m + 2:]), (pl.program_id(0),), (t // tm,))
        da = lax.dot_general(dy_ref[...].astype(BF16), w_ref[...], _NT, preferred_element_type=F32)
        _norm_bwd_tail(da, h_ref, g_ref, dhi_ref, *refs[m:m + 2])

    outs = pl.pallas_call(
        body, name=name, grid=(t // tm,),
        in_specs=[pl.BlockSpec((tm, n), lambda i: (i, 0)), _resident((k, n), lambda i: (0, 0))] + tail_in + [ANY] * m,
        out_specs=tail_out + [ANY] * m, out_shape=[S((t, k), F32), S((1, k), F32)] + [S(p.shape, p.dtype) for p in parts],
        scratch_shapes=[pltpu.SemaphoreType.DMA((m, 3)), pltpu.SemaphoreType.DMA((m, 3))] if m else [],
        compiler_params=_params(1))(dy, w, h, gain, dh_in, *parts)
    return outs[0], outs[1], list(outs[2:])


def conv_in_wgrad(name, a, dbcx):
    t, d = a.shape
    tk = _tile(t, 512)
    nt = t // tk
    n4 = 3 * d // N_CHIPS

    def body(a_ref, g_ref, o_ref, acc):
        s = pl.program_id(0)

        @pl.when(s == 0)
        def _():
            acc[...] = jnp.zeros_like(acc)

        at = a_ref[...].T
        for j in range(3):
            acc[:, j * d:(j + 1) * d] += jnp.dot(at, g_ref[j], preferred_element_type=F32)

        @pl.when(s == nt - 1)
        def _():
            for k in range(N_CHIPS):
                o_ref[k] = acc[:, k * n4:(k + 1) * n4].astype(BF16)

    return pl.pallas_call(
        body, name=name, grid=(nt,),
        in_specs=[pl.BlockSpec((tk, d), lambda s: (s, 0)), pl.BlockSpec((3, tk, d), lambda s: (0, s, 0))],
        out_specs=pl.BlockSpec((N_CHIPS, d, n4), lambda s: (0, 0, 0)), out_shape=S((N_CHIPS, d, n4), BF16),
        scratch_shapes=[pltpu.VMEM((d, 3 * d), F32)], compiler_params=_params(1))(a, dbcx)


def _rstd(x):
    return lax.rsqrt(jnp.mean(x * x, axis=-1, keepdims=True) + RMS_EPS)


def _rms_bwd(x, g, dy):
    r = _rstd(x)
    xhat = x * r
    dgain = jnp.sum(dy * xhat, axis=0, keepdims=True)
    dxh = dy * g
    dx = r * (dxh - xhat * jnp.mean(dxh * xhat, axis=-1, keepdims=True))
    return dx, dgain


def ffn_down_loss(name, z, wd_all, layer, resid, gain, target):
    _, t, f4 = z.shape
    d = wd_all.shape[2]
    tm = _tile(t, 512)

    def body(z_ref, wd_ref, r_ref, g_ref, t_ref, dh_ref, dg_ref, loss_ref):
        x = r_ref[...]
        for k in range(N_CHIPS):
            x = x + jnp.dot(z_ref[k], wd_ref[k], preferred_element_type=F32)
        g = g_ref[...]
        r = _rstd(x)
        xhat = x * r
        err = xhat * g - t_ref[...]
        dy = err * (1.0 / d)
        dxh = dy * g
        dh_ref[...] = r * (dxh - xhat * jnp.mean(dxh * xhat, axis=-1, keepdims=True))

        @pl.when(pl.program_id(0) == 0)
        def _():
            dg_ref[...] = jnp.zeros_like(dg_ref)
            loss_ref[...] = jnp.zeros_like(loss_ref)

        dg_ref[...] += jnp.sum(dy * xhat, axis=0, keepdims=True)
        per_token = jnp.mean(err * err, axis=-1, keepdims=True)
        loss_ref[...] += 0.5 * jnp.sum(per_token, axis=0, keepdims=True)

    row = pl.BlockSpec((tm, d), lambda i: (i, 0))
    vec = pl.BlockSpec((1, d), lambda i: (0, 0))
    one = pl.BlockSpec((1, 1), lambda i: (0, 0))
    return pl.pallas_call(
        body, name=name, grid=(t // tm,),
        in_specs=[pl.BlockSpec((N_CHIPS, tm, f4), lambda i: (0, i, 0)), _resident((N_CHIPS, f4, d), lambda i: (0, layer, 0)), row, vec, row],
        out_specs=[row, vec, one], out_shape=[S((t, d), F32), S((1, d), F32), S((1, 1), F32)],
        compiler_params=_params(1))(z, wd_all, resid, gain, target)


def mla_in_proj(name, a, w, g_cq, g_ckv, cos, sin):
    t, d = a.shape
    n = w.shape[1]
    ql, kl = g_cq.shape[1], g_ckv.shape[1]
    tr = _tile(t, 512)

    def body(a_ref, w_ref, gq_ref, gk_ref, c_ref, s_ref, p_ref, cq_ref, ckv_ref, kr_ref):
        p_ref[...] = jnp.dot(a_ref[...], w_ref[...], preferred_element_type=F32)
        xq = p_ref[:, 0:ql]
        cq_ref[...] = (xq * _rstd(xq) * gq_ref[...]).astype(BF16)
        xk = p_ref[:, ql:ql + kl]
        ckv_ref[...] = (xk * _rstd(xk) * gk_ref[...]).astype(BF16)
        k1 = p_ref[:, ql + kl:ql + kl + HALF]
        k2 = p_ref[:, ql + kl + HALF:ql + kl + ROPE]
        c = c_ref[...]
        s = s_ref[...]
        kr_ref[:, 0:HALF] = k1 * c - k2 * s
        kr_ref[:, HALF:ROPE] = k1 * s + k2 * c

    def row(w):
        return pl.BlockSpec((tr, w), lambda i: (i, 0))

    def vec(w):
        return pl.BlockSpec((1, w), lambda i: (0, 0))

    return pl.pallas_call(
        body, name=name, grid=(t // tr,),
        in_specs=[row(d), _resident((d, n), lambda i: (0, 0)), vec(ql), vec(kl), row(HALF), row(HALF)],
        out_specs=[row(n), row(ql), row(kl), row(ROPE)],
        out_shape=[S((t, n), F32), S((t, ql), BF16), S((t, kl), BF16), S((t, ROPE), F32)],
        compiler_params=_params(1))(a, w, g_cq, g_ckv, cos, sin)


def qkv_heads(name, cq, ckv, w_uq, w_ukv, kr, cos, sin, shards=()):
    t = cq.shape[0]
    tr = _tile(t, 256)
    n = len(shards)

    def body(cq_ref, ckv_ref, wq_ref, wkv_ref, kr_ref, c_ref, s_ref, *refs):
        src = refs[:n]
        qo_ref, ko_ref, vo_ref = refs[n:n + 3]
        q_ref, kv_ref = refs[2 * n + 3:2 * n + 5]
        if n:
            _ride_along(gather_ici_copies(src, refs[n + 3:2 * n + 3], *refs[2 * n + 5:]), (pl.program_id(0),), (t // tr,))
        q_ref[...] = jnp.dot(cq_ref[...], wq_ref[...], preferred_element_type=F32)
        kv_ref[...] = jnp.dot(ckv_ref[...], wkv_ref[...], preferred_element_type=F32).astype(BF16)
        c = c_ref[...]
        s = s_ref[...]
        krb = kr_ref[...].astype(BF16)
        for h in range(N_HEADS):
            q0 = h * QK
            qo_ref[h, :, 0:NOPE] = q_ref[:, q0:q0 + NOPE].astype(BF16)
            q1 = q_ref[:, q0 + NOPE:q0 + NOPE + HALF]
            q2 = q_ref[:, q0 + NOPE + HALF:q0 + QK]
            qo_ref[h, :, NOPE:NOPE + HALF] = (q1 * c - q2 * s).astype(BF16)
            qo_ref[h, :, NOPE + HALF:QK] = (q1 * s + q2 * c).astype(BF16)
            k0 = h * (NOPE + VDIM)
            ko_ref[h, :, 0:NOPE] = kv_ref[:, k0:k0 + NOPE]
            ko_ref[h, :, NOPE:QK] = krb
            vo_ref[h] = kv_ref[:, k0 + NOPE:k0 + NOPE + VDIM]

    def row(w):
        return pl.BlockSpec((tr, w), lambda i: (i, 0))

    def heads(w):
        return pl.BlockSpec((N_HEADS, tr, w), lambda i: (0, i, 0))

    outs = pl.pallas_call(
        body, name=name, grid=(t // tr,),
        in_specs=[row(cq.shape[1]), row(ckv.shape[1]), _resident(w_uq.shape, lambda i: (0, 0)), _resident(w_ukv.shape, lambda i: (0, 0)),
                  row(ROPE), row(HALF), row(HALF)] + [ANY] * n,
        out_specs=[heads(QK), heads(QK), heads(VDIM)] + [ANY] * n,
        out_shape=[S((N_HEADS, t, QK), BF16), S((N_HEADS, t, QK), BF16), S((N_HEADS, t, VDIM), BF16)]
        + [S((N_CHIPS,) + s.shape, s.dtype) for s in shards],
        scratch_shapes=[pltpu.VMEM((tr, N_HEADS * QK), F32), pltpu.VMEM((tr, N_HEADS * (NOPE + VDIM)), BF16)]
        + ([pltpu.SemaphoreType.DMA((n, 3)), pltpu.SemaphoreType.DMA((n, 3))] if n else []),
        compiler_params=_params(1))(cq, ckv, w_uq, w_ukv, kr, cos, sin, *shards)
    return outs[0], outs[1], outs[2], list(outs[3:])


def qkv_heads_bwd(name, dq_h, dk_h, dv_h, cos, sin, w_uq, w_ukv, proj, g_cq, g_ckv, halves=(), targets=(), where=()):
    t = dq_h.shape[1]
    tr = _tile(t, 256)
    n, nt = len(halves), len(targets)
    ql, kl = g_cq.shape[1], g_ckv.shape[1]
    width = proj.shape[1]

    def body(dq_ref, dk_ref, dv_ref, c_ref, s_ref, wq_ref, wkv_ref, p_ref, gq_ref, gk_ref, *refs):
        q_ref, kv_ref, dp_ref, dgq_ref, dgk_ref = refs[n:n + 5]
        kr_ref = refs[n + 5 + nt]
        if n:
            src, dst = refs[:n], refs[n + 5:n + 5 + nt]
            stages = refs[n + 6 + nt:2 * n + 6 + nt]
            send_sems, recv_sems, local_sems = refs[2 * n + 6 + nt:]
            core = lax.axis_index("c")
            own = [(src[i], stages[i],
                    dst[where[i][0]].at[pl.ds(pl.multiple_of(where[i][1] + core * src[i].shape[0], 8), src[i].shape[0])],
                    local_sems.at[i]) for i in range(n)]
            _ride_along(join_copies(src, dst, where, send_sems, recv_sems), (pl.program_id(0),), (t // tr,), own)
        c = c_ref[...]
        s = s_ref[...]
        dkr = jnp.zeros((tr, ROPE), F32)
        for h in range(N_HEADS):
            q0 = h * QK
            q_ref[:, q0:q0 + NOPE] = dq_ref[h, :, 0:NOPE].astype(BF16)
            d1 = dq_ref[h, :, NOPE:NOPE + HALF]
            d2 = dq_ref[h, :, NOPE + HALF:QK]
            q_ref[:, q0 + NOPE:q0 + NOPE + HALF] = (d1 * c + d2 * s).astype(BF16)
            q_ref[:, q0 + NOPE + HALF:q0 + QK] = (d2 * c - d1 * s).astype(BF16)
            k0 = h * (NOPE + VDIM)
            kv_ref[:, k0:k0 + NOPE] = dk_ref[h, :, 0:NOPE].astype(BF16)
            kv_ref[:, k0 + NOPE:k0 + NOPE + VDIM] = dv_ref[h].astype(BF16)
            dkr = dkr + dk_ref[h, :, NOPE:QK]
        kr_ref[...] = dkr
        dcq = lax.dot_general(q_ref[...], wq_ref[...], _NT, preferred_element_type=F32)
        dckv = lax.dot_general(kv_ref[...], wkv_ref[...], _NT, preferred_element_type=F32)
        dxq, dgq = _rms_bwd(p_ref[:, 0:ql], gq_ref[...], dcq)
        dp_ref[:, 0:ql] = dxq.astype(BF16)
        dxk, dgk = _rms_bwd(p_ref[:, ql:ql + kl], gk_ref[...], dckv)
        dp_ref[:, ql:ql + kl] = dxk.astype(BF16)
        r1 = kr_ref[:, 0:HALF]
        r2 = kr_ref[:, HALF:ROPE]
        dp_ref[:, ql + kl:ql + kl + HALF] = (r1 * c + r2 * s).astype(BF16)
        dp_ref[:, ql + kl + HALF:ql + kl + ROPE] = (r2 * c - r1 * s).astype(BF16)

        @pl.when(pl.program_id(0) == 0)
        def _():
            dgq_ref[...] = jnp.zeros_like(dgq_ref)
            dgk_ref[...] = jnp.zeros_like(dgk_ref)

        dgq_ref[...] += dgq
        dgk_ref[...] += dgk

    def row(w):
        return pl.BlockSpec((tr, w), lambda i: (i, 0))

    def vec(w):
        return pl.BlockSpec((1, w), lambda i: (0, 0))

    def heads(w):
        return pl.BlockSpec((N_HEADS, tr, w), lambda i: (0, i, 0))

    outs = pl.pallas_call(
        body, name=name, grid=(t // tr,),
        in_specs=[heads(QK), heads(QK), heads(VDIM), row(HALF), row(HALF), _resident(w_uq.shape, lambda i: (0, 0)),
                  _resident(w_ukv.shape, lambda i: (0, 0)), row(width), vec(ql), vec(kl)] + [ANY] * n,
        out_specs=[row(N_HEADS * QK), row(N_HEADS * (NOPE + VDIM)), row(width), vec(ql), vec(kl)] + [ANY] * nt,
        out_shape=[S((t, N_HEADS * QK), BF16), S((t, N_HEADS * (NOPE + VDIM)), BF16), S((t, width), BF16),
                   S((1, ql), F32), S((1, kl), F32)] + [S(tg, F32) for tg in targets],
        scratch_shapes=[pltpu.VMEM((tr, ROPE), F32)] + [pltpu.VMEM(h.shape, h.dtype) for h in halves]
        + ([pltpu.SemaphoreType.DMA((n,)), pltpu.SemaphoreType.DMA((n,)), pltpu.SemaphoreType.DMA((n, 2))] if n else []),
        compiler_params=_params(1))(dq_h, dk_h, dv_h, cos, sin, w_uq, w_ukv, proj, g_cq, g_ckv, *halves)
    return outs[0], outs[1], outs[2], outs[3], outs[4], list(outs[5:])


def _chunk_mask_t(q_start, k_start, bq, bk):
    kc = (k_start + lax.broadcasted_iota(jnp.int32, (bk, bq), 0)) // CHUNK
    qc = (q_start + lax.broadcasted_iota(jnp.int32, (bk, bq), 1)) // CHUNK
    return kc <= qc


def attention_fwd(name, q, k, v, shards=()):
    nh, t, _ = q.shape
    blk = ATT_BLOCK
    nq = t // blk
    n = len(shards)

    def body(q_ref, k_ref, v_ref, *refs):
        src = refs[:n]
        o_ref, lse_ref = refs[n:n + 2]
        dst = refs[n + 2:2 * n + 2]
        m_ref, l_ref, acc_ref, s_buf, p_buf, alpha_buf, bias_ref = refs[2 * n + 2:2 * n + 9]
        i = pl.program_id(1)
        if n:
            send_sems, recv_sems = refs[2 * n + 9:]
            _ride_along(gather_ici_copies(src, dst, send_sems, recv_sems), (pl.program_id(0), i), (nh, nq))

        @pl.when((pl.program_id(0) == 0) & (i == 0))
        def _():
            bias_ref[...] = jnp.where(_chunk_mask_t(0, 0, blk, blk), 0.0, MASK_VALUE)

        m_ref[...] = jnp.full_like(m_ref, MASK_VALUE)
        l_ref[...] = jnp.zeros_like(l_ref)
        acc_ref[...] = jnp.zeros_like(acc_ref)

        def rows(b):
            return pl.ds(pl.multiple_of(b * blk, blk), blk)

        def scores(b, slot):
            s_buf[slot] = lax.dot_general(k_ref[rows(b), :], q_ref[...], _NT, preferred_element_type=F32)

        def softmax(slot, diagonal):
            s = s_buf[slot]
            if diagonal:
                s = s + bias_ref[...]
            m_old = m_ref[...]
            m_new = jnp.maximum(m_old, jnp.max(s, axis=0, keepdims=True))
            p = jnp.exp2((s - m_new) * SCORE_SCALE_LOG2)
            alpha = jnp.exp2((m_old - m_new) * SCORE_SCALE_LOG2)
            l_ref[...] = alpha * l_ref[...] + jnp.sum(p, axis=0, keepdims=True)
            m_ref[...] = m_new
            alpha_buf[slot] = alpha
            p_buf[slot] = p.astype(BF16)

        def values(b, slot):
            pv = lax.dot_general(v_ref[rows(b), :], p_buf[slot], _TN, preferred_element_type=F32)
            acc_ref[...] = alpha_buf[slot] * acc_ref[...] + pv

        def step(t, slot):
            values(t - 2, slot)
            softmax(1 - slot, False)
            scores(t, slot)

        scores(0, 0)

        @pl.when(i == 0)
        def _():
            softmax(0, True)
            values(0, 0)

        @pl.when(i > 0)
        def _():
            scores(1, 1)
            softmax(0, False)
            steady = i - 1

            def pair(u, carry):
                step(2 + 2 * u, 0)
                step(3 + 2 * u, 1)
                return carry

            lax.fori_loop(0, steady // 2, pair, 0)

            @pl.when(steady % 2 == 1)
            def _():
                step(i, 0)

            last = i % 2
            softmax(last, True)
            values(i - 1, 1 - last)
            values(i, last)

        l = l_ref[...]
        o_ref[...] = (acc_ref[...] / l).T
        lse_ref[...] = m_ref[...] * SCORE_SCALE + jnp.log(l)

    outs = pl.pallas_call(
        body, name=name, grid=(nh, nq),
        in_specs=[pl.BlockSpec((None, blk, QK), lambda h, i: (h, i, 0)), pl.BlockSpec((None, t, QK), lambda h, i: (h, 0, 0)),
                  pl.BlockSpec((None, t, VDIM), lambda h, i: (h, 0, 0))] + [ANY] * n,
        out_specs=[pl.BlockSpec((blk, VDIM), lambda h, i: (i, h)),
                   pl.BlockSpec((None, None, 1, blk), lambda h, i: (h, i, 0, 0))] + [ANY] * n,
        out_shape=[S((t, nh * VDIM), F32), S((nh, nq, 1, blk), F32)] + [S((N_CHIPS,) + s.shape, s.dtype) for s in shards],
        scratch_shapes=[pltpu.VMEM((1, blk), F32), pltpu.VMEM((1, blk), F32), pltpu.VMEM((VDIM, blk), F32),
                        pltpu.VMEM((2, blk, blk), F32), pltpu.VMEM((2, blk, blk), BF16), pltpu.VMEM((2, 1, blk), F32),
                        pltpu.VMEM((blk, blk), F32)]
        + ([pltpu.SemaphoreType.DMA((n, 3)), pltpu.SemaphoreType.DMA((n, 3))] if n else []),
        compiler_params=_params(2))(q, k, v, *shards)
    return outs[0], outs[1], list(outs[2:])


def attention_out_bwd(name, dh, w_o, o, swap=()):
    t, d = dh.shape
    n = w_o.shape[0]
    blk = ATT_BLOCK
    m = len(swap)

    def body(dh_ref, w_ref, o_ref, *refs):
        do_ref, d_ref = refs[m:m + 2]
        if m:
            _ride_along(swap_copies(refs[:m], refs[m + 2:2 * m + 2], *refs[2 * m + 2:]), (pl.program_id(0),), (t // blk,))
        do_ref[...] = lax.dot_general(dh_ref[...].astype(BF16), w_ref[...], _NT, preferred_element_type=F32)
        for h in range(N_HEADS):
            cols = slice(h * VDIM, (h + 1) * VDIM)
            d_ref[h] = jnp.sum((do_ref[:, cols] * o_ref[:, cols]).T, axis=0, keepdims=True)

    tile = pl.BlockSpec((blk, n), lambda i: (i, 0))
    outs = pl.pallas_call(
        body, name=name, grid=(t // blk,),
        in_specs=[pl.BlockSpec((blk, d), lambda i: (i, 0)), _resident((n, d), lambda i: (0, 0)), tile] + [ANY] * m,
        out_specs=[tile, pl.BlockSpec((N_HEADS, None, 1, blk), lambda i: (0, i, 0, 0))] + [ANY] * m,
        out_shape=[S((t, n), F32), S((N_HEADS, t // blk, 1, blk), F32)] + _swap_shapes(swap),
        scratch_shapes=[pltpu.SemaphoreType.DMA((m,)), pltpu.SemaphoreType.DMA((m,))] if m else [],
        compiler_params=_params(1))(dh, w_o, o, *swap)
    return outs[0], outs[1], list(outs[2:])


def attention_bwd(name, q, k, v, do, lse, delta, parts=()):
    nh, t, _ = q.shape
    blk = ATT_BLOCK
    nq = t // blk
    n_pairs = nq * (nq + 1) // 2
    n = len(parts)
    scale = SCORE_SCALE

    def body(q_ref, k_ref, v_ref, do_ref, lse_ref, dl_ref, *refs):
        src = refs[:n]
        dq_out, dk_out, dv_out = refs[n:n + 3]
        dst = refs[n + 3:2 * n + 3]
        s_buf, dp_buf, p_buf, ds_buf, bias_ref, dq_ref, dk_ref, dv_ref = refs[2 * n + 3:2 * n + 11]
        if n:
            send_sems, recv_sems = refs[2 * n + 11:]
            _ride_along(scatter_ici_copies(src, dst, send_sems, recv_sems), (pl.program_id(0),), (nh,))

        @pl.when(pl.program_id(0) == 0)
        def _():
            bias_ref[...] = jnp.where(_chunk_mask_t(0, 0, blk, blk), 0.0, MASK_VALUE)

        dq_ref[...] = jnp.zeros_like(dq_ref)
        dk_ref[...] = jnp.zeros_like(dk_ref)
        dv_ref[...] = jnp.zeros_like(dv_ref)

        def rows(x):
            return pl.ds(pl.multiple_of(x * blk, blk), blk)

        def after(jb):
            j, b = jb
            wrap = b == nq - 1 - j
            return jnp.where(wrap, j + 1, j), jnp.where(wrap, 0, b + 1)

        def products(jb, slot):
            j, b = jb
            s_buf[slot] = lax.dot_general(k_ref[rows(j), :], q_ref[rows(j + b), :], _NT, preferred_element_type=F32)
            dp_buf[slot] = lax.dot_general(v_ref[rows(j), :], do_ref[rows(j + b), :].astype(BF16), _NT, preferred_element_type=F32)

        def softmax_bwd(jb, slot):
            j, b = jb
            s = s_buf[slot] + bias_ref[...] * (b == 0).astype(F32)
            p = jnp.exp2(s * SCORE_SCALE_LOG2 - lse_ref[j + b] * LOG2_E)
            p_buf[slot] = p.astype(BF16)
            ds_buf[slot] = (p * (dp_buf[slot] - dl_ref[j + b]) * scale).astype(BF16)

        def gradients(jb, slot):
            j, b = jb
            dv_ref[rows(j), :] += jnp.dot(p_buf[slot], do_ref[rows(j + b), :].astype(BF16), preferred_element_type=F32)
            dk_ref[rows(j), :] += jnp.dot(ds_buf[slot], q_ref[rows(j + b), :], preferred_element_type=F32)
            dq_ref[rows(j + b), :] += lax.dot_general(ds_buf[slot], k_ref[rows(j), :], _TN, preferred_element_type=F32)

        def step(state, slot):
            third, second, first = state
            gradients(third, slot)
            softmax_bwd(second, 1 - slot)
            products(first, slot)
            return second, first, after(first)

        zero = jnp.int32(0)
        pair0 = (zero, zero)
        products(pair0, 0)
        if n_pairs == 1:
            softmax_bwd(pair0, 0)
            gradients(pair0, 0)
        else:
            pair1 = after(pair0)
            products(pair1, 1)
            softmax_bwd(pair0, 0)
            steady = n_pairs - 2
            state = lax.fori_loop(0, steady // 2, lambda u, st: step(step(st, 0), 1), (pair0, pair1, after(pair1)))
            if steady % 2:
                state = step(state, 0)
            before_last, last_pair, _ = state
            last = (n_pairs - 1) % 2
            softmax_bwd(last_pair, last)
            gradients(before_last, 1 - last)
            gradients(last_pair, last)
        dq_out[...] = dq_ref[...].astype(BF16)
        dk_out[...] = dk_ref[...].astype(BF16)
        dv_out[...] = dv_ref[...].astype(BF16)

    head = lambda w: pl.BlockSpec((None, t, w), lambda h: (h, 0, 0))
    stats = pl.BlockSpec((None, nq, 1, blk), lambda h: (h, 0, 0, 0))
    outs = pl.pallas_call(
        body, name=name, grid=(nh,),
        in_specs=[head(QK), head(QK), head(VDIM), pl.BlockSpec((t, VDIM), lambda h: (0, h)), stats, stats] + [ANY] * n,
        out_specs=[head(QK), head(QK), head(VDIM)] + [ANY] * n,
        out_shape=[S((nh, t, QK), BF16), S((nh, t, QK), BF16), S((nh, t, VDIM), BF16)] + [S(p.shape, p.dtype) for p in parts],
        scratch_shapes=[pltpu.VMEM((2, blk, blk), F32), pltpu.VMEM((2, blk, blk), F32), pltpu.VMEM((2, blk, blk), BF16),
                        pltpu.VMEM((2, blk, blk), BF16), pltpu.VMEM((blk, blk), F32),
                        pltpu.VMEM((t, QK), F32), pltpu.VMEM((t, QK), F32), pltpu.VMEM((t, VDIM), F32)]
        + ([pltpu.SemaphoreType.DMA((n, 3)), pltpu.SemaphoreType.DMA((n, 3))] if n else []),
        compiler_params=_params(1, VMEM_LIMIT_WHOLE_HEAD))(q, k, v, do, lse, delta, *parts)
    return outs[0], outs[1], outs[2], list(outs[3:])


def _shift_down(u, s):
    rows = lax.broadcasted_iota(jnp.int32, u.shape, 0)
    return jnp.where(rows >= s, pltpu.roll(u, s, 0), 0.0)


def _shift_up(u, s):
    n = u.shape[0]
    rows = lax.broadcasted_iota(jnp.int32, u.shape, 0)
    return jnp.where(rows < n - s, pltpu.roll(u, n - s, 0), 0.0)


def _conv_specs(t, d, lanes):
    slab = lambda part: pl.BlockSpec((None, t, lanes), lambda j, part=part: (part, 0, j))
    return slab, pl.BlockSpec((3, lanes), lambda j: (0, j)), pl.BlockSpec((t, lanes), lambda j: (0, j))


def conv_bwd(name, bcx, w, dy):
    _, t, d = bcx.shape
    lanes = _tile(d, 128, 128)
    slab, w_spec, col = _conv_specs(t, d, lanes)

    def body(b_ref, c_ref, x_ref, w_ref, dy_ref, d_ref, dw_ref):
        c = c_ref[...].astype(F32)
        x = x_ref[...].astype(F32)
        dyv = dy_ref[...]
        u = c * x
        u1 = _shift_down(u, 1)
        u2 = _shift_down(u, 2)
        w0, w1, w2 = w_ref[0:1, :], w_ref[1:2, :], w_ref[2:3, :]
        d_ref[0] = (dyv * (w0 * u2 + w1 * u1 + w2 * u)).astype(BF16)
        duc = dyv * b_ref[...].astype(F32)
        dw_ref[0:1, :] = jnp.sum(duc * u2, axis=0, keepdims=True)
        dw_ref[1:2, :] = jnp.sum(duc * u1, axis=0, keepdims=True)
        dw_ref[2:3, :] = jnp.sum(duc * u, axis=0, keepdims=True)
        du = w2 * duc + w1 * _shift_up(duc, 1) + w0 * _shift_up(duc, 2)
        d_ref[1] = (du * x).astype(BF16)
        d_ref[2] = (du * c).astype(BF16)

    return pl.pallas_call(
        body, name=name, grid=(d // lanes,), in_specs=[slab(0), slab(1), slab(2), w_spec, col],
        out_specs=[pl.BlockSpec((3, t, lanes), lambda j: (0, 0, j)), w_spec], out_shape=[S((3, t, d), BF16), S((3, d), F32)],
        compiler_params=_params(1))(bcx, bcx, bcx, w, dy)


def _adamw_update(w, g, m, v):
    m_new = ADAM_B1 * m + (1.0 - ADAM_B1) * g
    v_new = ADAM_B2 * v + (1.0 - ADAM_B2) * (g * g)
    m_hat = m_new / (1.0 - ADAM_B1 ** ADAM_STEP)
    v_hat = v_new / (1.0 - ADAM_B2 ** ADAM_STEP)
    return -ADAM_LR * (m_hat / (jnp.sqrt(v_hat) + ADAM_EPS) + ADAM_WD * w), m_new, v_new


def adamw(name, w, g, m, v):
    r, c = w.shape
    tr = _tile(r, 512)

    def body(w_ref, g_ref, m_ref, v_ref, d_ref, mo_ref, vo_ref):
        d_ref[...], mo_ref[...], vo_ref[...] = _adamw_update(w_ref[...], g_ref[...], m_ref[...], v_ref[...])

    blk = pl.BlockSpec((tr, c), lambda i: (i, 0))
    return pl.pallas_call(
        body, name=name, grid=(r // tr,), in_specs=[blk] * 4, out_specs=[blk] * 3, out_shape=[S((r, c), F32)] * 3,
        compiler_params=_params(1))(w, g, m, v)


def adamw_swapped(name, wt, g, mt, vt):
    nl, c, r = wt.shape
    tr = _tile(r, 512, 128)
    nr = r // tr

    def body(w_ref, g_ref, m_ref, v_ref, go_ref, d_ref, mo_ref, vo_ref):
        gt = g_ref[...].T
        go_ref[...] = gt
        d_ref[...], mo_ref[...], vo_ref[...] = _adamw_update(w_ref[...], gt, m_ref[...], v_ref[...])

    swapped = pl.BlockSpec((None, c, tr), lambda l, i: (l, 0, i))
    return pl.pallas_call(
        body, name=name, grid=(nl, nr),
        in_specs=[swapped, pl.BlockSpec((tr, c), lambda l, i: (l * nr + i, 0)), swapped, swapped],
        out_specs=[swapped] * 4, out_shape=[S((nl, c, r), F32)] * 4, compiler_params=_params(2))(wt, g, mt, vt)


def _place():
    x, y, c = lax.axis_index("x"), lax.axis_index("y"), lax.axis_index("c")
    other_chips = [(1 - x, y), (x, 1 - y), (1 - x, 1 - y)]
    return x, y, c, other_chips


def _half(c, rows):
    return pl.ds(pl.multiple_of(c * (rows // 2), 16), rows // 2)


def gather_weight_shards(shards, small, h, gain):
    n, ns = len(shards), len(small)
    t, d = h.shape
    tr = _tile(t, 512)
    steps = t // tr

    def body(h_ref, g_ref, *refs):
        src = refs[:n]
        small_src = refs[n:n + ns]
        a_ref = refs[n + ns]
        dst = refs[n + ns + 1:2 * n + ns + 1]
        small_dst = refs[2 * n + ns + 1:2 * (n + ns) + 1]
        send_sems, recv_sems, small_send, small_recv, local_sems = refs[2 * (n + ns) + 1:2 * (n + ns) + 6]
        stages = refs[2 * (n + ns) + 6:]
        x, y, c, chips = _place()
        me = 2 * x + y
        sibling = (x, y, 1 - c)
        own = [(s_ref, stages[i], d_ref.at[me], local_sems.at[i])
               for i, (s_ref, d_ref) in enumerate(zip(list(src) + list(small_src), list(dst) + list(small_dst)))]
        small_pairs = []
        for i in range(ns):
            for j, (px, py) in enumerate(chips):
                def whole(slot):
                    return pltpu.make_async_remote_copy(
                        src_ref=small_src[i], dst_ref=small_dst[i].at[slot], send_sem=small_send.at[i, j],
                        recv_sem=small_recv.at[i, j], device_id=(px, py, c), device_id_type=MESH)
                small_pairs.append((whole(me), whole(2 * px + py)))

        def copy(i, slot, half_of, sem, to, from_input=False):
            rows = _half(half_of, src[i].shape[0])
            return pltpu.make_async_remote_copy(
                src_ref=src[i].at[rows] if from_input else dst[i].at[slot, rows], dst_ref=dst[i].at[slot, rows],
                send_sem=send_sems.at[i, sem], recv_sem=recv_sems.at[i, sem], device_id=to, device_id_type=MESH)

        over_ici = [copy(i, me, c, j, (*chip, c), from_input=True) for i in range(n) for j, chip in enumerate(chips)]
        handed_on = [copy(i, 2 * px + py, c, 3 + j, sibling) for i in range(n) for j, (px, py) in enumerate(chips)]

        @pl.when(pl.program_id(0) == 0)
        def _():
            for s_ref, stage, _, sems in own:
                pltpu.make_async_copy(s_ref, stage, sems.at[0]).start()
            for outgoing, _ in small_pairs:
                outgoing.start()
            for cp in over_ici:
                cp.start()

        @pl.when(pl.program_id(0) == steps - 1)
        def _():
            k = 0
            for i in range(n):
                for j, (px, py) in enumerate(chips):
                    copy(i, 2 * px + py, c, j, sibling).wait_recv()
                    handed_on[k].start()
                    k += 1
            for i in range(n):
                for j, (px, py) in enumerate(chips):
                    copy(i, 2 * px + py, 1 - c, 3 + j, sibling).wait_recv()
            for cp in over_ici + handed_on:
                cp.wait_send()
            for _, incoming in small_pairs:
                incoming.wait_recv()
            for outgoing, _ in small_pairs:
                outgoing.wait_send()
            _place_locally(own)

        xv = h_ref[...]
        a_ref[...] = (xv * _rstd(xv) * g_ref[...]).astype(BF16)

    everything = list(shards) + list(small)
    outs = pl.pallas_call(
        body, name="gather_weight_shards", grid=(steps,),
        in_specs=[pl.BlockSpec((tr, d), lambda i: (i, 0)), pl.BlockSpec((1, d), lambda i: (0, 0))] + [ANY] * (n + ns),
        out_specs=[pl.BlockSpec((tr, d), lambda i: (i, 0))] + [ANY] * (n + ns),
        out_shape=[S((t, d), BF16)] + [S((N_CHIPS,) + s.shape, s.dtype) for s in everything],
        scratch_shapes=[pltpu.SemaphoreType.DMA((n, 6)), pltpu.SemaphoreType.DMA((n, 6)),
                        pltpu.SemaphoreType.DMA((max(ns, 1), 3)), pltpu.SemaphoreType.DMA((max(ns, 1), 3)),
                        pltpu.SemaphoreType.DMA((n + ns, 2))] + [pltpu.VMEM(s.shape, s.dtype) for s in everything],
        compiler_params=_params(1))(h, gain, *shards, *small)
    return outs[0], list(outs[1:n + 1]), list(outs[n + 1:])


def gather_ici_copies(src, dst, send_sems, recv_sems):
    x, y, c, chips = _place()
    me = 2 * x + y
    pairs = []
    for i in range(len(src)):
        rows = _half(c, src[i].shape[0])
        for j, (px, py) in enumerate(chips):
            def copy(slot):
                return pltpu.make_async_remote_copy(
                    src_ref=src[i].at[rows], dst_ref=dst[i].at[slot, rows], send_sem=send_sems.at[i, j],
                    recv_sem=recv_sems.at[i, j], device_id=(px, py, c), device_id_type=MESH)
            pairs.append((copy(me), copy(2 * px + py)))
    return pairs


def scatter_ici_copies(src, dst, send_sems, recv_sems):
    x, y, c, chips = _place()
    me = 2 * x + y
    pairs = []
    for i in range(len(src)):
        for j, (px, py) in enumerate(chips):
            def copy(from_slot, to_slot):
                return pltpu.make_async_remote_copy(
                    src_ref=src[i].at[from_slot], dst_ref=dst[i].at[to_slot], send_sem=send_sems.at[i, j],
                    recv_sem=recv_sems.at[i, j], device_id=(px, py, c), device_id_type=MESH)
            pairs.append((copy(2 * px + py, me), copy(me, 2 * px + py)))
    return pairs


def _ride_along(pairs, grid_ids, grid_sizes, local=()):
    first = grid_ids[0] == 0
    last = grid_ids[0] == grid_sizes[0] - 1
    for g, size in zip(grid_ids[1:], grid_sizes[1:]):
        first = first & (g == 0)
        last = last & (g == size - 1)

    @pl.when(first)
    def _():
        for outgoing, _ in pairs:
            outgoing.start()
        for src, stage, _, sems in local:
            pltpu.make_async_copy(src, stage, sems.at[0]).start()

    @pl.when(last)
    def _():
        for _, incoming in pairs:
            incoming.wait_recv()
        for outgoing, _ in pairs:
            outgoing.wait_send()
        _place_locally(local)


def _place_locally(local):
    for src, stage, _, sems in local:
        pltpu.make_async_copy(src, stage, sems.at[0]).wait()
    placed = [pltpu.make_async_copy(stage, dst, sems.at[1]) for _, stage, dst, sems in local]
    for cp in placed:
        cp.start()
    for cp in placed:
        cp.wait()


def forward_copies(src, dst, send_sems, recv_sems):
    x, y, c, chips = _place()
    pairs = []
    for i in range(len(src)):
        for j, (px, py) in enumerate(chips):
            def copy(half_of):
                rows = _half(half_of, src[i].shape[1])
                return pltpu.make_async_remote_copy(
                    src_ref=src[i].at[2 * px + py, rows], dst_ref=dst[i].at[2 * px + py, rows], send_sem=send_sems.at[i, j],
                    recv_sem=recv_sems.at[i, j], device_id=(x, y, 1 - c), device_id_type=MESH)
            pairs.append((copy(c), copy(1 - c)))
    return pairs


def attention_out_proj(name, attn, w_o, resid, next_gain, arriving, own):
    t, kdim = attn.shape
    n = w_o.shape[1]
    tm = _tile(t, 512)
    m = len(arriving)

    def body(x_ref, w_ref, r_ref, g_ref, *refs):
        src, own_refs = refs[:m], refs[m:2 * m]
        h_ref, a_ref = refs[2 * m:2 * m + 2]
        dst = refs[2 * m + 2:3 * m + 2]
        stages = refs[3 * m + 2:4 * m + 2]
        send_sems, recv_sems, local_sems = refs[4 * m + 2:]
        me = 2 * lax.axis_index("x") + lax.axis_index("y")
        placed = [(own_refs[i], stages[i], dst[i].at[me], local_sems.at[i]) for i in range(m)]
        _ride_along(forward_copies(src, dst, send_sems, recv_sems), (pl.program_id(0),), (t // tm,), placed)
        y = r_ref[...] + jnp.dot(x_ref[...].astype(BF16), w_ref[...], preferred_element_type=F32)
        h_ref[...] = y
        a_ref[...] = (y * _rstd(y) * g_ref[...]).astype(BF16)

    row = pl.BlockSpec((tm, n), lambda i: (i, 0))
    outs = pl.pallas_call(
        body, name=name, grid=(t // tm,),
        in_specs=[pl.BlockSpec((tm, kdim), lambda i: (i, 0)), _resident((kdim, n), lambda i: (0, 0)), row,
                  pl.BlockSpec((1, n), lambda i: (0, 0))] + [ANY] * (2 * m),
        out_specs=[row, row] + [ANY] * m, out_shape=[S((t, n), F32), S((t, n), BF16)] + [S(g.shape, g.dtype) for g in arriving],
        input_output_aliases={4 + i: 2 + i for i in range(m)},
        scratch_shapes=[pltpu.VMEM(o.shape, o.dtype) for o in own]
        + [pltpu.SemaphoreType.DMA((m, 3)), pltpu.SemaphoreType.DMA((m, 3)), pltpu.SemaphoreType.DMA((m, 2))],
        compiler_params=_params(1))(attn, w_o, resid, next_gain, *arriving, *own)
    return outs[0], outs[1], list(outs[2:])


def swap_copies(src, dst, send_sems, recv_sems):
    x, y, c, _ = _place()
    pairs = []
    for i in range(len(src)):
        cp = pltpu.make_async_remote_copy(
            src_ref=src[i].at[:, _half(1 - c, src[i].shape[1]), :], dst_ref=dst[i], send_sem=send_sems.at[i],
            recv_sem=recv_sems.at[i], device_id=(x, y, 1 - c), device_id_type=MESH)
        pairs.append((cp, cp))
    return pairs


def _swap_shapes(grads):
    return [S((g.shape[0], g.shape[1] // 2, g.shape[2]), g.dtype) for g in grads]


def sibling_swap_halves(name, grads):
    n = len(grads)

    def body(*refs):
        pairs = swap_copies(refs[:n], refs[n:2 * n], *refs[2 * n:])
        for outgoing, _ in pairs:
            outgoing.start()
        for _, incoming in pairs:
            incoming.wait_recv()
        for outgoing, _ in pairs:
            outgoing.wait_send()

    return pl.pallas_call(
        body, name=name, in_specs=[ANY] * n, out_specs=[ANY] * n, out_shape=_swap_shapes(grads),
        scratch_shapes=[pltpu.SemaphoreType.DMA((n,)), pltpu.SemaphoreType.DMA((n,))],
    )(*grads)


def add_halves(name, g, rx):
    _, r, cdim = g.shape
    r2 = r // 2
    tr = _tile(r2, 512, 16)
    nb = r2 // tr

    def body(core_ref, mine_ref, rx_ref, o_ref):
        o_ref[...] = (mine_ref[...].astype(F32) + rx_ref[...].astype(F32)).astype(BF16)

    core = lax.axis_index("c").astype(jnp.int32).reshape(1)
    half = pl.BlockSpec((None, tr, cdim), lambda k, i, core_ref: (k, i, 0))
    return pl.pallas_call(
        body, name=name, out_shape=S((N_CHIPS, r2, cdim), BF16),
        grid_spec=pltpu.PrefetchScalarGridSpec(
            num_scalar_prefetch=1, grid=(N_CHIPS, nb),
            in_specs=[pl.BlockSpec((None, tr, cdim), lambda k, i, core_ref: (k, core_ref[0] * nb + i, 0)), half],
            out_specs=half),
        compiler_params=_params(2))(core, g, rx)


def sum_chips(name, arrived, mine):
    _, r2, cdim = arrived.shape
    tr = _tile(r2, 512, 16)

    def body(a_ref, m_ref, o_ref):
        me = 2 * lax.axis_index("x") + lax.axis_index("y")
        acc = jnp.zeros((tr, cdim), F32)
        for k in range(N_CHIPS):
            acc = acc + jnp.where(me == k, m_ref[k], a_ref[k]).astype(F32)
        o_ref[...] = acc

    slots = pl.BlockSpec((N_CHIPS, tr, cdim), lambda i: (0, i, 0))
    return pl.pallas_call(
        body, name=name, grid=(r2 // tr,), in_specs=[slots, slots],
        out_specs=pl.BlockSpec((tr, cdim), lambda i: (i, 0)), out_shape=S((r2, cdim), F32), compiler_params=_params(1))(arrived, mine)


def join_copies(src, dst, where, send_sems, recv_sems):
    x, y, c, _ = _place()
    pairs = []
    for i in range(len(src)):
        def copy(half_of):
            r2 = src[i].shape[0]
            rows = pl.ds(pl.multiple_of(where[i][1] + half_of * r2, 8), r2)
            return pltpu.make_async_remote_copy(
                src_ref=src[i], dst_ref=dst[where[i][0]].at[rows], send_sem=send_sems.at[i],
                recv_sem=recv_sems.at[i], device_id=(x, y, 1 - c), device_id_type=MESH)
        pairs.append((copy(c), copy(1 - c)))
    return pairs


def sibling_join_halves(name, halves, targets, where):
    n = len(halves)

    def body(*refs):
        src, dst = refs[:n], refs[n:n + len(targets)]
        send_sems, recv_sems, local_sems = refs[n + len(targets):n + len(targets) + 3]
        stages = refs[n + len(targets) + 3:]
        c = lax.axis_index("c")
        own = [(src[i], stages[i],
                dst[where[i][0]].at[pl.ds(pl.multiple_of(where[i][1] + c * src[i].shape[0], 8), src[i].shape[0])],
                local_sems.at[i]) for i in range(n)]
        for s_ref, stage, _, sems in own:
            pltpu.make_async_copy(s_ref, stage, sems.at[0]).start()
        pairs = join_copies(src, dst, where, send_sems, recv_sems)
        for outgoing, _ in pairs:
            outgoing.start()
        for _, incoming in pairs:
            incoming.wait_recv()
        for outgoing, _ in pairs:
            outgoing.wait_send()
        _place_locally(own)

    return list(pl.pallas_call(
        body, name=name, in_specs=[ANY] * n, out_specs=[ANY] * len(targets), out_shape=[S(tg, F32) for tg in targets],
        scratch_shapes=[pltpu.SemaphoreType.DMA((n,)), pltpu.SemaphoreType.DMA((n,)), pltpu.SemaphoreType.DMA((n, 2))]
        + [pltpu.VMEM(h.shape, h.dtype) for h in halves],
    )(*halves))


def all_reduce_small(name, packed):
    rows, width = packed.shape

    def body(x_ref, o_ref, gathered, send_sems, recv_sems):
        x, y, c, _ = _place()
        me = 4 * x + 2 * y + c
        gathered[me] = x_ref[...]
        flips = [(fx, fy, fc) for fx in (0, 1) for fy in (0, 1) for fc in (0, 1)][1:]

        def copy(r, slot, to):
            return pltpu.make_async_remote_copy(
                src_ref=x_ref, dst_ref=gathered.at[slot], send_sem=send_sems.at[r], recv_sem=recv_sems.at[r],
                device_id=to, device_id_type=MESH)

        def peer(f):
            return (x ^ f[0], y ^ f[1], c ^ f[2])

        sent = [copy(r, me, peer(f)) for r, f in enumerate(flips)]
        for cp in sent:
            cp.start()
        for r, f in enumerate(flips):
            px, py, pc = peer(f)
            copy(r, 4 * px + 2 * py + pc, peer(f)).wait_recv()
        for cp in sent:
            cp.wait_send()
        acc = gathered[0]
        for k in range(1, N_DEV):
            acc = acc + gathered[k]
        o_ref[...] = acc

    vmem = pl.BlockSpec(memory_space=pltpu.VMEM)
    return pl.pallas_call(
        body, name=name, in_specs=[vmem], out_specs=vmem, out_shape=S((rows, width), F32),
        scratch_shapes=[pltpu.VMEM((N_DEV, rows, width), F32), pltpu.SemaphoreType.DMA((N_DEV - 1,)),
                        pltpu.SemaphoreType.DMA((N_DEV - 1,))],
    )(packed)


def _rope_tables(positions):
    inv_freq = 1.0 / (ROPE_THETA ** (jnp.arange(0, ROPE, 2, dtype=F32) / ROPE))
    ang = positions.astype(F32)[:, None] * inv_freq
    return jnp.cos(ang), jnp.sin(ang)


def _unstack_cols(w):
    k4, k, n4 = w.shape
    return jnp.transpose(w, (1, 0, 2)).reshape(k, k4 * n4)


def _stack_cols(w):
    k, n = w.shape
    return jnp.transpose(w.reshape(k, N_CHIPS, n // N_CHIPS), (1, 0, 2))


def kernel(x, positions, mla_norm, mla_w_in, mla_g_cq, mla_g_ckv, mla_w_uq, mla_w_ukv, mla_w_o, conv_norm, conv_w_in, conv_w, conv_w_out, ffn_norm, ffn_w_gate, ffn_w_up, ffn_w_down, final_norm, loss_target, m_mla_norm, m_mla_w_in, m_mla_g_cq, m_mla_g_ckv, m_mla_w_uq, m_mla_w_ukv, m_mla_w_o, m_conv_norm, m_conv_w_in, m_conv_w, m_conv_w_out, m_ffn_norm, m_ffn_w_gate, m_ffn_w_up, m_ffn_w_down, m_final_norm, v_mla_norm, v_mla_w_in, v_mla_g_cq, v_mla_g_ckv, v_mla_w_uq, v_mla_w_ukv, v_mla_w_o, v_conv_norm, v_conv_w_in, v_conv_w, v_conv_w_out, v_ffn_norm, v_ffn_w_gate, v_ffn_w_up, v_ffn_w_down, v_final_norm):
    weights = dict(mla_norm=mla_norm, mla_w_in=mla_w_in, mla_g_cq=mla_g_cq, mla_g_ckv=mla_g_ckv, mla_w_uq=mla_w_uq,
                   mla_w_ukv=mla_w_ukv, mla_w_o=mla_w_o, conv_norm=conv_norm, conv_w_in=conv_w_in, conv_w=conv_w,
                   conv_w_out=conv_w_out, ffn_norm=ffn_norm, ffn_w_gate=ffn_w_gate, ffn_w_up=ffn_w_up,
                   ffn_w_down=ffn_w_down, final_norm=final_norm)
    m_in = dict(mla_norm=m_mla_norm, mla_w_in=m_mla_w_in, mla_g_cq=m_mla_g_cq, mla_g_ckv=m_mla_g_ckv, mla_w_uq=m_mla_w_uq,
                mla_w_ukv=m_mla_w_ukv, mla_w_o=m_mla_w_o, conv_norm=m_conv_norm, conv_w_in=m_conv_w_in, conv_w=m_conv_w,
                conv_w_out=m_conv_w_out, ffn_norm=m_ffn_norm, ffn_w_gate=m_ffn_w_gate, ffn_w_up=m_ffn_w_up,
                ffn_w_down=m_ffn_w_down, final_norm=m_final_norm)
    v_in = dict(mla_norm=v_mla_norm, mla_w_in=v_mla_w_in, mla_g_cq=v_mla_g_cq, mla_g_ckv=v_mla_g_ckv, mla_w_uq=v_mla_w_uq,
                mla_w_ukv=v_mla_w_ukv, mla_w_o=v_mla_w_o, conv_norm=v_conv_norm, conv_w_in=v_conv_w_in, conv_w=v_conv_w,
                conv_w_out=v_conv_w_out, ffn_norm=v_ffn_norm, ffn_w_gate=v_ffn_w_gate, ffn_w_up=v_ffn_w_up,
                ffn_w_down=v_ffn_w_down, final_norm=v_final_norm)
    big = ["mla_w_in", "mla_w_uq", "mla_w_ukv", "mla_w_o", "conv_w_in", "conv_w_out", "ffn_w_gate", "ffn_w_up", "ffn_w_down"]
    order = list(weights)

    t, d = x.shape[1], x.shape[2]
    h0 = x.reshape(t, d)
    target = loss_target.reshape(t, d)
    cos, sin = _rope_tables(positions.reshape(t))

    def rows2d(a):
        return a.reshape(-1, a.shape[-1])

    first, later = big[:4], big[4:]
    shards = {n: rows2d(weights[n]).astype(BF16) for n in big}
    d4 = d // N_CHIPS
    a0, first_gathered, (conv_norm_slots, conv_w_slots) = gather_weight_shards(
        [shards[n] for n in first], [conv_norm.reshape(1, d4), conv_w.reshape(3, d4)], h0, mla_norm)
    gathered = dict(zip(first, first_gathered))
    conv_norm_full = conv_norm_slots.reshape(1, d)
    conv_w_full = jnp.transpose(conv_w_slots, (1, 0, 2)).reshape(3, d)
    w_in = gathered["mla_w_in"].reshape(-1, gathered["mla_w_in"].shape[-1])
    w_uq = _unstack_cols(gathered["mla_w_uq"])
    w_ukv = _unstack_cols(gathered["mla_w_ukv"])
    w_o = gathered["mla_w_o"].reshape(-1, d)

    chip = 2 * lax.axis_index("x") + lax.axis_index("y")

    def pack_rows(rows):
        idx = lax.broadcasted_iota(jnp.int32, (SMALL_ROWS, d), 0)
        out = jnp.zeros((SMALL_ROWS, d), F32)
        for r, row in enumerate(rows):
            out = out + jnp.where(idx == r, row, 0.0)
        return out


    proj, cq, ckv, kr = mla_in_proj("mla_in_proj", a0, w_in, mla_g_cq, mla_g_ckv, cos, sin)
    qh, kh, vh, conv_arriving = qkv_heads("qkv_heads", cq, ckv, w_uq, w_ukv, kr, cos, sin, [shards[n] for n in later[:2]])
    attn, lse, ffn_arriving = attention_fwd("attention_fwd", qh, kh, vh, [shards[n] for n in later[2:]])
    h1, a1, handed = attention_out_proj("mla_out_proj", attn, w_o, h0, ffn_norm[0:1], conv_arriving + ffn_arriving,
                                        [shards[n] for n in later])
    gathered.update(zip(later, handed))
    cw_in = _unstack_cols(gathered["conv_w_in"])
    cw_out = gathered["conv_w_out"].reshape(-1, d)
    wg_all, wu_all, wd_all = gathered["ffn_w_gate"], gathered["ffn_w_up"], gathered["ffn_w_down"]

    def ffn_forward(tag, h, a, layer, next_gain):
        g, u, z = ffn_up(f"ffn{tag}_up", a, wg_all, wu_all, layer)
        return g, u, z, ffn_down(f"ffn{tag}_down", z, wd_all, layer, h, next_gain)

    g0, u0, z0, (h2, a2) = ffn_forward(0, h1, a1, 0, conv_norm_full)
    bcx, yc = conv_in_proj("conv_in_proj", a2, cw_in, conv_w_full)
    h3, a3 = linear("conv_out_proj", yc, cw_out, F32, resid=h2, next_gain=ffn_norm[1:2])
    g1, u1, z1 = ffn_up("ffn1_up", a3, wg_all, wu_all, 1)
    dh4, d_final_norm, loss_local = ffn_down_loss("ffn1_down_loss", z1, wd_all, 1, h3, final_norm.reshape(1, d), target)

    def ffn_backward(tag, dh, h, layer, a, g, u, z, swap=()):
        dg, du, swapped = ffn_bwd_hidden(f"ffn{tag}_bwd_hidden", dh, wd_all, layer, g, u, swap)
        d_wd = ffn_wgrad_down(f"ffn{tag}_wgrad_down", z, dh)
        dh_prev, d_norm = ffn_bwd_input(f"ffn{tag}_bwd_input", dg, du, wg_all, wu_all, layer, h, ffn_norm[layer:layer + 1], dh)
        d_wg = ffn_wgrad_up(f"ffn{tag}_wgrad_gate", a, dg)
        d_wu = ffn_wgrad_up(f"ffn{tag}_wgrad_up", a, du)
        return dh_prev, d_norm, [d_wg, d_wu, d_wd], swapped

    def pair_sums(tag, local, from_sibling):
        return [add_halves(f"pair_sum_{tag}{i}", g, r) for i, (g, r) in enumerate(zip(local, from_sibling))]

    def sum_from_chips(tag, pairs, arrived):
        return [sum_chips(f"chip_sum_{tag}{i}", a, p) for i, (a, p) in enumerate(zip(arrived, pairs))]

    def shard_shape(n):
        return rows2d(weights[n]).shape

    dh3, d_ffn_norm1, ffn1_grads, _ = ffn_backward(1, dh4, h3, 1, a3, g1, u1, z1)

    dyc = linear_nt("conv_out_bwd_input", dh3, cw_out, F32)
    d_cw_out = wgrad("conv_out_wgrad", yc, dh3)
    dbcx, d_conv_w = conv_bwd("conv_bwd", bcx, conv_w_full, dyc)
    dh2, d_conv_norm = conv_in_bwd_input("conv_in_bwd_input", dbcx, cw_in, h2, conv_norm_full, dh3)
    d_cw_in = conv_in_wgrad("conv_in_wgrad", a2, dbcx)

    second = [d_cw_in, d_cw_out.reshape(N_CHIPS, -1, d)] + ffn1_grads
    dh1, d_ffn_norm0, ffn0_grads, second_swapped = ffn_backward(0, dh2, h1, 0, a1, g0, u0, z0, second)
    d_w_o = wgrad("mla_out_wgrad", attn, dh1)
    first_part = ffn0_grads + [d_w_o.reshape(N_CHIPS, -1, d)]
    d_attn, delta, first_swapped = attention_out_bwd("mla_out_bwd_input", dh1, w_o, attn, first_part)
    rest_pairs = pair_sums("rest", second + first_part, second_swapped + first_swapped)
    dqh, dkh, dvh, rest_arrived = attention_bwd("attention_bwd", qh, kh, vh, d_attn, lse, delta, rest_pairs)
    rd, rf = ffn0_grads[0].shape[1], ffn0_grads[2].shape[1]
    rest_where = [(0, 0), (1, 0), (2, rd), (3, rd), (4, rf), (2, 0), (3, 0), (4, 0), (5, 0)]
    rest_names = later + ["mla_w_o"]
    dq, dkv, dproj, d_g_cq, d_g_ckv, rest_grads = qkv_heads_bwd(
        "qkv_heads_bwd", dqh, dkh, dvh, cos, sin, w_uq, w_ukv, proj, mla_g_cq, mla_g_ckv,
        sum_from_chips("rest", rest_pairs, rest_arrived), [shard_shape(n) for n in rest_names], rest_where)
    grads = dict(zip(rest_names, rest_grads))
    d_w_uq = wgrad("mla_q_up_wgrad", cq, dq)
    d_w_ukv = wgrad("mla_kv_up_wgrad", ckv, dkv)
    d_w_in = wgrad("mla_in_wgrad", a0, dproj)
    mla_local = [d_w_in.reshape(N_CHIPS, -1, d_w_in.shape[-1]), _stack_cols(d_w_uq), _stack_cols(d_w_ukv)]
    mla_pairs = pair_sums("mla", mla_local, sibling_swap_halves("sibling_swap_mla", mla_local))
    grad_x, d_mla_norm, mla_arrived = linear_nt_norm_bwd("mla_in_bwd_input", dproj, w_in, h0, mla_norm, dh1, mla_pairs)

    grads.update(zip(first[:3], sibling_join_halves("sibling_join_mla", sum_from_chips("mla", mla_pairs, mla_arrived),
                                                    [shard_shape(n) for n in first[:3]], [(i, 0) for i in range(3)])))

    def pad_row(v):
        return jnp.pad(v, ((0, 0), (0, d - v.shape[1])))

    small = all_reduce_small("all_reduce_small_grads", pack_rows([
        d_mla_norm, pad_row(d_g_cq), pad_row(d_g_ckv), d_ffn_norm0, d_ffn_norm1, d_final_norm, d_conv_norm,
        d_conv_w[0:1], d_conv_w[1:2], d_conv_w[2:3], jnp.broadcast_to(loss_local, (1, d))]))
    loss = small[10, 0]
    grads["mla_norm"] = small[0:1]
    grads["mla_g_cq"] = small[1:2, :mla_g_cq.shape[1]]
    grads["mla_g_ckv"] = small[2:3, :mla_g_ckv.shape[1]]
    grads["ffn_norm"] = small[3:5]
    grads["final_norm"] = small[5:6]
    grads["conv_norm"] = lax.dynamic_slice(small[6:7], (0, chip * d4), (1, d4))
    grads["conv_w"] = lax.dynamic_slice(small[7:10], (0, chip * d4), (3, d4))

    outs_g, outs_d, outs_m, outs_v = [], [], [], []
    for n in order:
        w = weights[n]
        if w.ndim == 3 and w.shape[2] % 128 and w.shape[1] % 128 == 0:
            results = adamw_swapped(f"adamw_{n}", jnp.swapaxes(w, 1, 2), grads[n].reshape(-1, w.shape[2]),
                                    jnp.swapaxes(m_in[n], 1, 2), jnp.swapaxes(v_in[n], 1, 2))
            grad_w, delta_w, new_m, new_v = [jnp.swapaxes(o, 1, 2) for o in results]
        else:
            delta_w, new_m, new_v = adamw(f"adamw_{n}", rows2d(w), grads[n].reshape(rows2d(w).shape), rows2d(m_in[n]), rows2d(v_in[n]))
            grad_w = grads[n]
        outs_g.append(grad_w.reshape(w.shape))
        outs_d.append(delta_w.reshape(w.shape))
        outs_m.append(new_m.reshape(w.shape))
        outs_v.append(new_v.reshape(w.shape))
    return (loss, grad_x.reshape(x.shape), *outs_g, *outs_d, *outs_m, *outs_v)
```

```python
import math

import jax
import jax.numpy as jnp
from jax import lax
from jax.experimental import pallas as pl
from jax.experimental.pallas import tpu as pltpu

F32 = jnp.float32
BF16 = jnp.bfloat16
S = jax.ShapeDtypeStruct

N_HEADS = 8
NOPE = 128
ROPE = 64
HALF = ROPE // 2
VDIM = 128
QK = NOPE + ROPE
CHUNK = 64
ROPE_THETA = 10000.0
RMS_EPS = 1e-6
ADAM_LR = 0.001
ADAM_B1 = 0.9
ADAM_B2 = 0.999
ADAM_EPS = 1e-08
ADAM_WD = 0.01
ADAM_STEP = 10

N_CHIPS = 4
N_DEV = 8
MASK_VALUE = -1e30
SCORE_SCALE = 1.0 / math.sqrt(QK)
LOG2_E = math.log2(math.e)
SCORE_SCALE_LOG2 = SCORE_SCALE * LOG2_E
VMEM_LIMIT = 48 * 1024 * 1024
VMEM_LIMIT_WHOLE_HEAD = 58 * 1024 * 1024
ATT_BLOCK = 512
CONV_SAVED_DTYPE = jnp.bfloat16
SMALL_ROWS = 16

_NN = (((1,), (0,)), ((), ()))
_NT = (((1,), (1,)), ((), ()))
_TN = (((0,), (0,)), ((), ()))
MESH = pl.DeviceIdType.MESH
ANY = pl.BlockSpec(memory_space=pl.ANY)


def _params(n_axes, vmem_limit=VMEM_LIMIT):
    return pltpu.CompilerParams(dimension_semantics=("arbitrary",) * n_axes, vmem_limit_bytes=vmem_limit)


def _tile(n, cap, mult=8):
    for t in range(min(cap, n), 0, -1):
        if n % t == 0 and t % mult == 0:
            return t
    return n


def _sigmoid(x):
    return 0.5 * jnp.tanh(0.5 * x) + 0.5


def _mm(name, a_ops, b_ops, products, dims, grid, k_axis, outs, acc_shape, epilogue, extra_ops=()):
    na, nb, ne, no = len(a_ops), len(b_ops), len(extra_ops), len(outs)
    n_acc = 1 + max(c for _, _, c in products)
    nk = 1 if k_axis is None else grid[k_axis]

    def body(*refs):
        a_refs = refs[:na]
        b_refs = refs[na:na + nb]
        e_refs = refs[na + nb:na + nb + ne]
        o_refs = refs[na + nb + ne:na + nb + ne + no]
        acc_refs = refs[na + nb + ne + no:]

        def partial_sums():
            vals = [None] * n_acc
            for ai, bi, ci in products:
                d = lax.dot_general(a_refs[ai][...].astype(BF16), b_refs[bi][...].astype(BF16), dims,
                                    preferred_element_type=F32)
                vals[ci] = d if vals[ci] is None else vals[ci] + d
            return vals

        if nk == 1:
            epilogue(partial_sums(), e_refs, o_refs)
        else:
            k = pl.program_id(k_axis)

            @pl.when(k == 0)
            def _():
                for acc in acc_refs:
                    acc[...] = jnp.zeros_like(acc)

            for acc, v in zip(acc_refs, partial_sums()):
                acc[...] += v

            @pl.when(k == nk - 1)
            def _():
                epilogue([acc[...] for acc in acc_refs], e_refs, o_refs)

    ops = list(a_ops) + list(b_ops) + list(extra_ops)
    return pl.pallas_call(
        body, name=name, grid=grid,
        in_specs=[s for _, s in ops], out_specs=[s for _, s in outs], out_shape=[o for o, _ in outs],
        scratch_shapes=[pltpu.VMEM(acc_shape, F32) for _ in range(n_acc if nk > 1 else 0)],
        compiler_params=_params(len(grid)),
    )(*[a for a, _ in ops])


def _store(accs, e_refs, o_refs):
    o_refs[0][...] = accs[0].astype(o_refs[0].dtype)


def linear(name, x, w, out_dtype, resid=None, next_gain=None):
    t, k = x.shape
    n = w.shape[1]
    tm = _tile(t, 512)
    tn = n if n <= 2048 else _tile(n, 1024, 128)
    tile = pl.BlockSpec((tm, tn), lambda j, i: (i, j))
    extra = [] if resid is None else [(resid, tile)]
    outs = [(S((t, n), out_dtype), tile)]
    if next_gain is not None:
        assert tn == n
        extra.append((next_gain, pl.BlockSpec((1, n), lambda j, i: (0, 0))))
        outs.append((S((t, n), BF16), tile))

    def epilogue(accs, e_refs, o_refs):
        y = accs[0] if resid is None else e_refs[0][...] + accs[0]
        o_refs[0][...] = y.astype(out_dtype)
        if next_gain is not None:
            o_refs[1][...] = (y * _rstd(y) * e_refs[-1][...]).astype(BF16)

    res = _mm(name, [(x, pl.BlockSpec((tm, k), lambda j, i: (i, 0)))], [(w, pl.BlockSpec((k, tn), lambda j, i: (0, j)))],
              [(0, 0, 0)], _NN, (n // tn, t // tm), None, outs, None, epilogue, extra)
    return res[0] if next_gain is None else res


def linear_nt(name, dy, w, out_dtype):
    t, n = dy.shape
    k = w.shape[0]
    tm = _tile(t, 512)
    tc = n if n <= 2048 else _tile(n, 1024, 128)
    return _mm(name, [(dy, pl.BlockSpec((tm, tc), lambda i, c: (i, c)))], [(w, pl.BlockSpec((k, tc), lambda i, c: (0, c)))],
               [(0, 0, 0)], _NT, (t // tm, n // tc), 1,
               [(S((t, k), out_dtype), pl.BlockSpec((tm, k), lambda i, c: (i, 0)))], (tm, k), _store)[0]


def wgrad(name, x, dy):
    t, k = x.shape
    n = dy.shape[1]
    tk = _tile(t, 512)
    tn = n if n <= 1024 else _tile(n, 1024, 128)
    return _mm(name, [(x, pl.BlockSpec((tk, k), lambda j, s: (s, 0)))], [(dy, pl.BlockSpec((tk, tn), lambda j, s: (s, j)))],
               [(0, 0, 0)], _TN, (n // tn, t // tk), 1,
               [(S((k, n), BF16), pl.BlockSpec((k, tn), lambda j, s: (0, j)))], (k, tn), _store)[0]


def _resident(shape, index_map):
    return pl.BlockSpec(shape, index_map, pipeline_mode=pl.Buffered(1))


def ffn_up(name, a, wg_all, wu_all, layer):
    t, d = a.shape
    f4 = wg_all.shape[2]
    tm = _tile(t, 512)
    w_spec = _resident((N_CHIPS, d, f4), lambda i: (0, layer, 0))
    h_spec = pl.BlockSpec((N_CHIPS, tm, f4), lambda i: (0, i, 0))

    def body(a_ref, wg_ref, wu_ref, zg_ref, zu_ref, z_ref):
        av = a_ref[...]
        for k in range(N_CHIPS):
            g = jnp.dot(av, wg_ref[k], preferred_element_type=F32)
            u = jnp.dot(av, wu_ref[k], preferred_element_type=F32)
            sg = _sigmoid(g)
            silu = g * sg
            zg_ref[k] = (u * (sg * (1.0 + g * (1.0 - sg)))).astype(BF16)
            zu_ref[k] = silu.astype(BF16)
            z_ref[k] = (silu * u).astype(BF16)

    return pl.pallas_call(
        body, name=name, grid=(t // tm,), in_specs=[pl.BlockSpec((tm, d), lambda i: (i, 0)), w_spec, w_spec],
        out_specs=[h_spec] * 3, out_shape=[S((N_CHIPS, t, f4), BF16)] * 3, compiler_params=_params(1))(a, wg_all, wu_all)


def ffn_down(name, z, wd_all, layer, resid, next_gain=None):
    _, t, f4 = z.shape
    d = wd_all.shape[2]
    tm = _tile(t, 512)
    row = pl.BlockSpec((tm, d), lambda i: (i, 0))
    normed = next_gain is not None

    def body(z_ref, wd_ref, r_ref, *refs):
        acc = r_ref[...]
        for k in range(N_CHIPS):
            acc = acc + jnp.dot(z_ref[k], wd_ref[k], preferred_element_type=F32)
        refs[-2 if normed else -1][...] = acc
        if normed:
            refs[-1][...] = (acc * _rstd(acc) * refs[0][...]).astype(BF16)

    res = pl.pallas_call(
        body, name=name, grid=(t // tm,),
        in_specs=[pl.BlockSpec((N_CHIPS, tm, f4), lambda i: (0, i, 0)), _resident((N_CHIPS, f4, d), lambda i: (0, layer, 0)), row]
        + ([pl.BlockSpec((1, d), lambda i: (0, 0))] if normed else []),
        out_specs=[row] * (2 if normed else 1), out_shape=[S((t, d), F32)] + ([S((t, d), BF16)] if normed else []),
        compiler_params=_params(1))(z, wd_all, resid, *([next_gain] if normed else []))
    return res if normed else res[0]


def ffn_bwd_hidden(name, dh, wd_all, layer, zg, zu, swap=()):
    t, d = dh.shape
    f4 = zg.shape[2]
    tm = _tile(t, 512)
    h_spec = pl.BlockSpec((N_CHIPS, tm, f4), lambda i: (0, i, 0))
    n = len(swap)

    def body(dh_ref, wd_ref, zg_ref, zu_ref, *refs):
        dg_ref, du_ref = refs[n:n + 2]
        if n:
            _ride_along(swap_copies(refs[:n], refs[n + 2:2 * n + 2], *refs[2 * n + 2:]), (pl.program_id(0),), (t // tm,))
        dhb = dh_ref[...].astype(BF16)
        for k in range(N_CHIPS):
            dz = lax.dot_general(dhb, wd_ref[k], _NT, preferred_element_type=F32)
            dg_ref[k] = (dz * zg_ref[k].astype(F32)).astype(BF16)
            du_ref[k] = (dz * zu_ref[k].astype(F32)).astype(BF16)

    outs = pl.pallas_call(
        body, name=name, grid=(t // tm,),
        in_specs=[pl.BlockSpec((tm, d), lambda i: (i, 0)), _resident((N_CHIPS, f4, d), lambda i: (0, layer, 0)), h_spec, h_spec]
        + [ANY] * n,
        out_specs=[h_spec] * 2 + [ANY] * n, out_shape=[S((N_CHIPS, t, f4), BF16)] * 2 + _swap_shapes(swap),
        scratch_shapes=[pltpu.SemaphoreType.DMA((n,)), pltpu.SemaphoreType.DMA((n,))] if n else [],
        compiler_params=_params(1))(dh, wd_all, zg, zu, *swap)
    return outs[0], outs[1], list(outs[2:])


def _norm_bwd_specs(tm, d):
    row = pl.BlockSpec((tm, d), lambda i: (i, 0))
    vec = pl.BlockSpec((1, d), lambda i: (0, 0))
    return [row, vec, row], [row, vec]


def _norm_bwd_tail(da, h_ref, g_ref, dhi_ref, dho_ref, dgain_ref):
    dx, dgain = _rms_bwd(h_ref[...], g_ref[...], da)
    dho_ref[...] = dhi_ref[...] + dx

    @pl.when(pl.program_id(0) == 0)
    def _():
        dgain_ref[...] = jnp.zeros_like(dgain_ref)

    dgain_ref[...] += dgain


def ffn_bwd_input(name, dg, du, wg_all, wu_all, layer, h, gain, dh_in):
    _, t, f4 = dg.shape
    d = h.shape[1]
    tm = _tile(t, 512)
    h_spec = pl.BlockSpec((N_CHIPS, tm, f4), lambda i: (0, i, 0))
    w_spec = _resident((N_CHIPS, d, f4), lambda i: (0, layer, 0))
    tail_in, tail_out = _norm_bwd_specs(tm, d)

    def body(dg_ref, du_ref, wg_ref, wu_ref, *tail):
        acc = jnp.zeros((tm, d), F32)
        for k in range(N_CHIPS):
            acc = acc + lax.dot_general(dg_ref[k], wg_ref[k], _NT, preferred_element_type=F32)
            acc = acc + lax.dot_general(du_ref[k], wu_ref[k], _NT, preferred_element_type=F32)
        _norm_bwd_tail(acc, *tail)

    return pl.pallas_call(
        body, name=name, grid=(t // tm,), in_specs=[h_spec, h_spec, w_spec, w_spec] + tail_in, out_specs=tail_out,
        out_shape=[S((t, d), F32), S((1, d), F32)], compiler_params=_params(1))(dg, du, wg_all, wu_all, h, gain, dh_in)


def ffn_wgrad_up(name, a, dy):
    t, d = a.shape
    f4 = dy.shape[2]
    tk = _tile(t, 512)
    nt = t // tk

    def body(a_ref, dy_ref, o_ref, acc):
        s = pl.program_id(0)

        @pl.when(s == 0)
        def _():
            acc[...] = jnp.zeros_like(acc)

        at = a_ref[...].T
        for k in range(N_CHIPS):
            acc[k] += jnp.dot(at, dy_ref[k], preferred_element_type=F32)

        @pl.when(s == nt - 1)
        def _():
            o_ref[...] = acc[...].astype(BF16)

    return pl.pallas_call(
        body, name=name, grid=(nt,),
        in_specs=[pl.BlockSpec((tk, d), lambda s: (s, 0)), pl.BlockSpec((N_CHIPS, tk, f4), lambda s: (0, s, 0))],
        out_specs=pl.BlockSpec((N_CHIPS, d, f4), lambda s: (0, 0, 0)), out_shape=S((N_CHIPS, d, f4), BF16),
        scratch_shapes=[pltpu.VMEM((N_CHIPS, d, f4), F32)], compiler_params=_params(1))(a, dy)


def ffn_wgrad_down(name, z, dh):
    _, t, f4 = z.shape
    d = dh.shape[1]
    tk = _tile(t, 512)
    nt = t // tk

    def body(z_ref, dh_ref, o_ref, acc):
        s = pl.program_id(0)

        @pl.when(s == 0)
        def _():
            acc[...] = jnp.zeros_like(acc)

        dhb = dh_ref[...].astype(BF16)
        for k in range(N_CHIPS):
            acc[k] += lax.dot_general(z_ref[k], dhb, _TN, preferred_element_type=F32)

        @pl.when(s == nt - 1)
        def _():
            o_ref[...] = acc[...].astype(BF16)

    return pl.pallas_call(
        body, name=name, grid=(nt,),
        in_specs=[pl.BlockSpec((N_CHIPS, tk, f4), lambda s: (0, s, 0)), pl.BlockSpec((tk, d), lambda s: (s, 0))],
        out_specs=pl.BlockSpec((N_CHIPS, f4, d), lambda s: (0, 0, 0)), out_shape=S((N_CHIPS, f4, d), BF16),
        scratch_shapes=[pltpu.VMEM((N_CHIPS, f4, d), F32)], compiler_params=_params(1))(z, dh)


def conv_in_proj(name, a, w, conv_w):
    t, d = a.shape
    tm = _tile(t, 256)
    keep = 8

    def body(a_ref, w_ref, cw_ref, bcx_ref, y_ref, u_ref):
        @pl.when(pl.program_id(0) == 0)
        def _():
            u_ref[0:keep, :] = jnp.zeros((keep, d), F32)

        av = a_ref[...]
        b, c, x = [jnp.dot(av, w_ref[:, j * d:(j + 1) * d], preferred_element_type=F32) for j in range(3)]
        for j, part in enumerate((b, c, x)):
            bcx_ref[j] = part.astype(bcx_ref.dtype)
        u_ref[keep:keep + tm, :] = c * x
        uc = (cw_ref[0:1, :] * u_ref[keep - 2:keep - 2 + tm, :] + cw_ref[1:2, :] * u_ref[keep - 1:keep - 1 + tm, :]
              + cw_ref[2:3, :] * u_ref[keep:keep + tm, :])
        y_ref[...] = (b * uc).astype(BF16)
        u_ref[0:keep, :] = u_ref[tm:tm + keep, :]

    return pl.pallas_call(
        body, name=name, grid=(t // tm,),
        in_specs=[pl.BlockSpec((tm, d), lambda i: (i, 0)), _resident((d, 3 * d), lambda i: (0, 0)), pl.BlockSpec((3, d), lambda i: (0, 0))],
        out_specs=[pl.BlockSpec((3, tm, d), lambda i: (0, i, 0)), pl.BlockSpec((tm, d), lambda i: (i, 0))],
        out_shape=[S((3, t, d), CONV_SAVED_DTYPE), S((t, d), BF16)], scratch_shapes=[pltpu.VMEM((tm + keep, d), F32)],
        compiler_params=_params(1))(a, w, conv_w)


def conv_in_bwd_input(name, dbcx, w, h, gain, dh_in):
    _, t, d = dbcx.shape
    tm = _tile(t, 512)
    tail_in, tail_out = _norm_bwd_specs(tm, d)

    def body(g_ref, w_ref, *tail):
        acc = jnp.zeros((tm, d), F32)
        for j in range(3):
            acc = acc + lax.dot_general(g_ref[j], w_ref[:, j * d:(j + 1) * d], _NT, preferred_element_type=F32)
        _norm_bwd_tail(acc, *tail)

    return pl.pallas_call(
        body, name=name, grid=(t // tm,),
        in_specs=[pl.BlockSpec((3, tm, d), lambda i: (0, i, 0)), _resident((d, 3 * d), lambda i: (0, 0))] + tail_in,
        out_specs=tail_out, out_shape=[S((t, d), F32), S((1, d), F32)], compiler_params=_params(1))(dbcx, w, h, gain, dh_in)


def linear_nt_norm_bwd(name, dy, w, h, gain, dh_in, parts=()):
    t, n = dy.shape
    k = w.shape[0]
    tm = _tile(t, 512)
    tail_in, tail_out = _norm_bwd_specs(tm, k)
    m = len(parts)

    def body(dy_ref, w_ref, h_ref, g_ref, dhi_ref, *refs):
        if m:
            _ride_along(scatter_ici_copies(refs[:m], refs[m + 2:2 * m + 2], *refs[2 * m + 2:]), (pl.program_id(0),), (t // tm,))
        da = lax.dot_general(dy_ref[...].astype(BF16), w_ref[...], _NT, preferred_element_type=F32)
        _norm_bwd_tail(da, h_ref, g_ref, dhi_ref, *refs[m:m + 2])

    outs = pl.pallas_call(
        body, name=name, grid=(t // tm,),
        in_specs=[pl.BlockSpec((tm, n), lambda i: (i, 0)), _resident((k, n), lambda i: (0, 0))] + tail_in + [ANY] * m,
        out_specs=tail_out + [ANY] * m, out_shape=[S((t, k), F32), S((1, k), F32)] + [S(p.shape, p.dtype) for p in parts],
        scratch_shapes=[pltpu.SemaphoreType.DMA((m, 3)), pltpu.SemaphoreType.DMA((m, 3))] if m else [],
        compiler_params=_params(1))(dy, w, h, gain, dh_in, *parts)
    return outs[0], outs[1], list(outs[2:])


def conv_in_wgrad(name, a, dbcx):
    t, d = a.shape
    tk = _tile(t, 512)
    nt = t // tk
    n4 = 3 * d // N_CHIPS

    def body(a_ref, g_ref, o_ref, acc):
        s = pl.program_id(0)

        @pl.when(s == 0)
        def _():
            acc[...] = jnp.zeros_like(acc)

        at = a_ref[...].T
        for j in range(3):
            acc[:, j * d:(j + 1) * d] += jnp.dot(at, g_ref[j], preferred_element_type=F32)

        @pl.when(s == nt - 1)
        def _():
            for k in range(N_CHIPS):
                o_ref[k] = acc[:, k * n4:(k + 1) * n4].astype(BF16)

    return pl.pallas_call(
        body, name=name, grid=(nt,),
        in_specs=[pl.BlockSpec((tk, d), lambda s: (s, 0)), pl.BlockSpec((3, tk, d), lambda s: (0, s, 0))],
        out_specs=pl.BlockSpec((N_CHIPS, d, n4), lambda s: (0, 0, 0)), out_shape=S((N_CHIPS, d, n4), BF16),
        scratch_shapes=[pltpu.VMEM((d, 3 * d), F32)], compiler_params=_params(1))(a, dbcx)


def _rstd(x):
    return lax.rsqrt(jnp.mean(x * x, axis=-1, keepdims=True) + RMS_EPS)


def _rms_bwd(x, g, dy):
    r = _rstd(x)
    xhat = x * r
    dgain = jnp.sum(dy * xhat, axis=0, keepdims=True)
    dxh = dy * g
    dx = r * (dxh - xhat * jnp.mean(dxh * xhat, axis=-1, keepdims=True))
    return dx, dgain


def ffn_down_loss(name, z, wd_all, layer, resid, gain, target):
    _, t, f4 = z.shape
    d = wd_all.shape[2]
    tm = _tile(t, 512)

    def body(z_ref, wd_ref, r_ref, g_ref, t_ref, dh_ref, dg_ref, loss_ref):
        x = r_ref[...]
        for k in range(N_CHIPS):
            x = x + jnp.dot(z_ref[k], wd_ref[k], preferred_element_type=F32)
        g = g_ref[...]
        r = _rstd(x)
        xhat = x * r
        err = xhat * g - t_ref[...]
        dy = err * (1.0 / d)
        dxh = dy * g
        dh_ref[...] = r * (dxh - xhat * jnp.mean(dxh * xhat, axis=-1, keepdims=True))

        @pl.when(pl.program_id(0) == 0)
        def _():
            dg_ref[...] = jnp.zeros_like(dg_ref)
            loss_ref[...] = jnp.zeros_like(loss_ref)

        dg_ref[...] += jnp.sum(dy * xhat, axis=0, keepdims=True)
        per_token = jnp.mean(err * err, axis=-1, keepdims=True)
        loss_ref[...] += 0.5 * jnp.sum(per_token, axis=0, keepdims=True)

    row = pl.BlockSpec((tm, d), lambda i: (i, 0))
    vec = pl.BlockSpec((1, d), lambda i: (0, 0))
    one = pl.BlockSpec((1, 1), lambda i: (0, 0))
    return pl.pallas_call(
        body, name=name, grid=(t // tm,),
        in_specs=[pl.BlockSpec((N_CHIPS, tm, f4), lambda i: (0, i, 0)), _resident((N_CHIPS, f4, d), lambda i: (0, layer, 0)), row, vec, row],
        out_specs=[row, vec, one], out_shape=[S((t, d), F32), S((1, d), F32), S((1, 1), F32)],
        compiler_params=_params(1))(z, wd_all, resid, gain, target)


def mla_in_proj(name, a, w, g_cq, g_ckv, cos, sin):
    t, d = a.shape
    n = w.shape[1]
    ql, kl = g_cq.shape[1], g_ckv.shape[1]
    tr = _tile(t, 512)

    def body(a_ref, w_ref, gq_ref, gk_ref, c_ref, s_ref, p_ref, cq_ref, ckv_ref, kr_ref):
        p_ref[...] = jnp.dot(a_ref[...], w_ref[...], preferred_element_type=F32)
        xq = p_ref[:, 0:ql]
        cq_ref[...] = (xq * _rstd(xq) * gq_ref[...]).astype(BF16)
        xk = p_ref[:, ql:ql + kl]
        ckv_ref[...] = (xk * _rstd(xk) * gk_ref[...]).astype(BF16)
        k1 = p_ref[:, ql + kl:ql + kl + HALF]
        k2 = p_ref[:, ql + kl + HALF:ql + kl + ROPE]
        c = c_ref[...]
        s = s_ref[...]
        kr_ref[:, 0:HALF] = k1 * c - k2 * s
        kr_ref[:, HALF:ROPE] = k1 * s + k2 * c

    def row(w):
        return pl.BlockSpec((tr, w), lambda i: (i, 0))

    def vec(w):
        return pl.BlockSpec((1, w), lambda i: (0, 0))

    return pl.pallas_call(
        body, name=name, grid=(t // tr,),
        in_specs=[row(d), _resident((d, n), lambda i: (0, 0)), vec(ql), vec(kl), row(HALF), row(HALF)],
        out_specs=[row(n), row(ql), row(kl), row(ROPE)],
        out_shape=[S((t, n), F32), S((t, ql), BF16), S((t, kl), BF16), S((t, ROPE), F32)],
        compiler_params=_params(1))(a, w, g_cq, g_ckv, cos, sin)


def qkv_heads(name, cq, ckv, w_uq, w_ukv, kr, cos, sin, shards=()):
    t = cq.shape[0]
    tr = _tile(t, 256)
    n = len(shards)

    def body(cq_ref, ckv_ref, wq_ref, wkv_ref, kr_ref, c_ref, s_ref, *refs):
        src = refs[:n]
        qo_ref, ko_ref, vo_ref = refs[n:n + 3]
        q_ref, kv_ref = refs[2 * n + 3:2 * n + 5]
        if n:
            _ride_along(gather_ici_copies(src, refs[n + 3:2 * n + 3], *refs[2 * n + 5:]), (pl.program_id(0),), (t // tr,))
        q_ref[...] = jnp.dot(cq_ref[...], wq_ref[...], preferred_element_type=F32)
        kv_ref[...] = jnp.dot(ckv_ref[...], wkv_ref[...], preferred_element_type=F32).astype(BF16)
        c = c_ref[...]
        s = s_ref[...]
        krb = kr_ref[...].astype(BF16)
        for h in range(N_HEADS):
            q0 = h * QK
            qo_ref[h, :, 0:NOPE] = q_ref[:, q0:q0 + NOPE].astype(BF16)
            q1 = q_ref[:, q0 + NOPE:q0 + NOPE + HALF]
            q2 = q_ref[:, q0 + NOPE + HALF:q0 + QK]
            qo_ref[h, :, NOPE:NOPE + HALF] = (q1 * c - q2 * s).astype(BF16)
            qo_ref[h, :, NOPE + HALF:QK] = (q1 * s + q2 * c).astype(BF16)
            k0 = h * (NOPE + VDIM)
            ko_ref[h, :, 0:NOPE] = kv_ref[:, k0:k0 + NOPE]
            ko_ref[h, :, NOPE:QK] = krb
            vo_ref[h] = kv_ref[:, k0 + NOPE:k0 + NOPE + VDIM]

    def row(w):
        return pl.BlockSpec((tr, w), lambda i: (i, 0))

    def heads(w):
        return pl.BlockSpec((N_HEADS, tr, w), lambda i: (0, i, 0))

    outs = pl.pallas_call(
        body, name=name, grid=(t // tr,),
        in_specs=[row(cq.shape[1]), row(ckv.shape[1]), _resident(w_uq.shape, lambda i: (0, 0)), _resident(w_ukv.shape, lambda i: (0, 0)),
                  row(ROPE), row(HALF), row(HALF)] + [ANY] * n,
        out_specs=[heads(QK), heads(QK), heads(VDIM)] + [ANY] * n,
        out_shape=[S((N_HEADS, t, QK), BF16), S((N_HEADS, t, QK), BF16), S((N_HEADS, t, VDIM), BF16)]
        + [S((N_CHIPS,) + s.shape, s.dtype) for s in shards],
        scratch_shapes=[pltpu.VMEM((tr, N_HEADS * QK), F32), pltpu.VMEM((tr, N_HEADS * (NOPE + VDIM)), BF16)]
        + ([pltpu.SemaphoreType.DMA((n, 3)), pltpu.SemaphoreType.DMA((n, 3))] if n else []),
        compiler_params=_params(1))(cq, ckv, w_uq, w_ukv, kr, cos, sin, *shards)
    return outs[0], outs[1], outs[2], list(outs[3:])


def qkv_heads_bwd(name, dq_h, dk_h, dv_h, cos, sin, w_uq, w_ukv, proj, g_cq, g_ckv, halves=(), targets=(), where=()):
    t = dq_h.shape[1]
    tr = _tile(t, 256)
    n, nt = len(halves), len(targets)
    ql, kl = g_cq.shape[1], g_ckv.shape[1]
    width = proj.shape[1]

    def body(dq_ref, dk_ref, dv_ref, c_ref, s_ref, wq_ref, wkv_ref, p_ref, gq_ref, gk_ref, *refs):
        q_ref, kv_ref, dp_ref, dgq_ref, dgk_ref = refs[n:n + 5]
        kr_ref = refs[n + 5 + nt]
        if n:
            src, dst = refs[:n], refs[n + 5:n + 5 + nt]
            stages = refs[n + 6 + nt:2 * n + 6 + nt]
            send_sems, recv_sems, local_sems = refs[2 * n + 6 + nt:]
            core = lax.axis_index("c")
            own = [(src[i], stages[i],
                    dst[where[i][0]].at[pl.ds(pl.multiple_of(where[i][1] + core * src[i].shape[0], 8), src[i].shape[0])],
                    local_sems.at[i]) for i in range(n)]
            _ride_along(join_copies(src, dst, where, send_sems, recv_sems), (pl.program_id(0),), (t // tr,), own)
        c = c_ref[...]
        s = s_ref[...]
        dkr = jnp.zeros((tr, ROPE), F32)
        for h in range(N_HEADS):
            q0 = h * QK
            q_ref[:, q0:q0 + NOPE] = dq_ref[h, :, 0:NOPE].astype(BF16)
            d1 = dq_ref[h, :, NOPE:NOPE + HALF]
            d2 = dq_ref[h, :, NOPE + HALF:QK]
            q_ref[:, q0 + NOPE:q0 + NOPE + HALF] = (d1 * c + d2 * s).astype(BF16)
            q_ref[:, q0 + NOPE + HALF:q0 + QK] = (d2 * c - d1 * s).astype(BF16)
            k0 = h * (NOPE + VDIM)
            kv_ref[:, k0:k0 + NOPE] = dk_ref[h, :, 0:NOPE].astype(BF16)
            kv_ref[:, k0 + NOPE:k0 + NOPE + VDIM] = dv_ref[h].astype(BF16)
            dkr = dkr + dk_ref[h, :, NOPE:QK]
        kr_ref[...] = dkr
        dcq = lax.dot_general(q_ref[...], wq_ref[...], _NT, preferred_element_type=F32)
        dckv = lax.dot_general(kv_ref[...], wkv_ref[...], _NT, preferred_element_type=F32)
        dxq, dgq = _rms_bwd(p_ref[:, 0:ql], gq_ref[...], dcq)
        dp_ref[:, 0:ql] = dxq.astype(BF16)
        dxk, dgk = _rms_bwd(p_ref[:, ql:ql + kl], gk_ref[...], dckv)
        dp_ref[:, ql:ql + kl] = dxk.astype(BF16)
        r1 = kr_ref[:, 0:HALF]
        r2 = kr_ref[:, HALF:ROPE]
        dp_ref[:, ql + kl:ql + kl + HALF] = (r1 * c + r2 * s).astype(BF16)
        dp_ref[:, ql + kl + HALF:ql + kl + ROPE] = (r2 * c - r1 * s).astype(BF16)

        @pl.when(pl.program_id(0) == 0)
        def _():
            dgq_ref[...] = jnp.zeros_like(dgq_ref)
            dgk_ref[...] = jnp.zeros_like(dgk_ref)

        dgq_ref[...] += dgq
        dgk_ref[...] += dgk

    def row(w):
        return pl.BlockSpec((tr, w), lambda i: (i, 0))

    def vec(w):
        return pl.BlockSpec((1, w), lambda i: (0, 0))

    def heads(w):
        return pl.BlockSpec((N_HEADS, tr, w), lambda i: (0, i, 0))

    outs = pl.pallas_call(
        body, name=name, grid=(t // tr,),
        in_specs=[heads(QK), heads(QK), heads(VDIM), row(HALF), row(HALF), _resident(w_uq.shape, lambda i: (0, 0)),
                  _resident(w_ukv.shape, lambda i: (0, 0)), row(width), vec(ql), vec(kl)] + [ANY] * n,
        out_specs=[row(N_HEADS * QK), row(N_HEADS * (NOPE + VDIM)), row(width), vec(ql), vec(kl)] + [ANY] * nt,
        out_shape=[S((t, N_HEADS * QK), BF16), S((t, N_HEADS * (NOPE + VDIM)), BF16), S((t, width), BF16),
                   S((1, ql), F32), S((1, kl), F32)] + [S(tg, F32) for tg in targets],
        scratch_shapes=[pltpu.VMEM((tr, ROPE), F32)] + [pltpu.VMEM(h.shape, h.dtype) for h in halves]
        + ([pltpu.SemaphoreType.DMA((n,)), pltpu.SemaphoreType.DMA((n,)), pltpu.SemaphoreType.DMA((n, 2))] if n else []),
        compiler_params=_params(1))(dq_h, dk_h, dv_h, cos, sin, w_uq, w_ukv, proj, g_cq, g_ckv, *halves)
    return outs[0], outs[1], outs[2], outs[3], outs[4], list(outs[5:])


def _chunk_mask_t(q_start, k_start, bq, bk):
    kc = (k_start + lax.broadcasted_iota(jnp.int32, (bk, bq), 0)) // CHUNK
    qc = (q_start + lax.broadcasted_iota(jnp.int32, (bk, bq), 1)) // CHUNK
    return kc <= qc


def attention_fwd(name, q, k, v, shards=()):
    nh, t, _ = q.shape
    blk = ATT_BLOCK
    nq = t // blk
    n = len(shards)

    def body(q_ref, k_ref, v_ref, *refs):
        src = refs[:n]
        o_ref, lse_ref = refs[n:n + 2]
        dst = refs[n + 2:2 * n + 2]
        m_ref, l_ref, acc_ref, s_buf, p_buf, alpha_buf, bias_ref = refs[2 * n + 2:2 * n + 9]
        i = pl.program_id(1)
        if n:
            send_sems, recv_sems = refs[2 * n + 9:]
            _ride_along(gather_ici_copies(src, dst, send_sems, recv_sems), (pl.program_id(0), i), (nh, nq))

        @pl.when((pl.program_id(0) == 0) & (i == 0))
        def _():
            bias_ref[...] = jnp.where(_chunk_mask_t(0, 0, blk, blk), 0.0, MASK_VALUE)

        m_ref[...] = jnp.full_like(m_ref, MASK_VALUE)
        l_ref[...] = jnp.zeros_like(l_ref)
        acc_ref[...] = jnp.zeros_like(acc_ref)

        def rows(b):
            return pl.ds(pl.multiple_of(b * blk, blk), blk)

        def scores(b, slot):
            s_buf[slot] = lax.dot_general(k_ref[rows(b), :], q_ref[...], _NT, preferred_element_type=F32)

        def softmax(slot, diagonal):
            s = s_buf[slot]
            if diagonal:
                s = s + bias_ref[...]
            m_old = m_ref[...]
            m_new = jnp.maximum(m_old, jnp.max(s, axis=0, keepdims=True))
            p = jnp.exp2((s - m_new) * SCORE_SCALE_LOG2)
            alpha = jnp.exp2((m_old - m_new) * SCORE_SCALE_LOG2)
            l_ref[...] = alpha * l_ref[...] + jnp.sum(p, axis=0, keepdims=True)
            m_ref[...] = m_new
            alpha_buf[slot] = alpha
            p_buf[slot] = p.astype(BF16)

        def values(b, slot):
            pv = lax.dot_general(v_ref[rows(b), :], p_buf[slot], _TN, preferred_element_type=F32)
            acc_ref[...] = alpha_buf[slot] * acc_ref[...] + pv

        def step(t, slot):
            values(t - 2, slot)
            softmax(1 - slot, False)
            scores(t, slot)

        scores(0, 0)

        @pl.when(i == 0)
        def _():
            softmax(0, True)
            values(0, 0)

        @pl.when(i > 0)
        def _():
            scores(1, 1)
            softmax(0, False)
            steady = i - 1

            def pair(u, carry):
                step(2 + 2 * u, 0)
                step(3 + 2 * u, 1)
                return carry

            lax.fori_loop(0, steady // 2, pair, 0)

            @pl.when(steady % 2 == 1)
            def _():
                step(i, 0)

            last = i % 2
            softmax(last, True)
            values(i - 1, 1 - last)
            values(i, last)

        l = l_ref[...]
        o_ref[...] = (acc_ref[...] / l).T
        lse_ref[...] = m_ref[...] * SCORE_SCALE + jnp.log(l)

    outs = pl.pallas_call(
        body, name=name, grid=(nh, nq),
        in_specs=[pl.BlockSpec((None, blk, QK), lambda h, i: (h, i, 0)), pl.BlockSpec((None, t, QK), lambda h, i: (h, 0, 0)),
                  pl.BlockSpec((None, t, VDIM), lambda h, i: (h, 0, 0))] + [ANY] * n,
        out_specs=[pl.BlockSpec((blk, VDIM), lambda h, i: (i, h)),
                   pl.BlockSpec((None, None, 1, blk), lambda h, i: (h, i, 0, 0))] + [ANY] * n,
        out_shape=[S((t, nh * VDIM), F32), S((nh, nq, 1, blk), F32)] + [S((N_CHIPS,) + s.shape, s.dtype) for s in shards],
        scratch_shapes=[pltpu.VMEM((1, blk), F32), pltpu.VMEM((1, blk), F32), pltpu.VMEM((VDIM, blk), F32),
                        pltpu.VMEM((2, blk, blk), F32), pltpu.VMEM((2, blk, blk), BF16), pltpu.VMEM((2, 1, blk), F32),
                        pltpu.VMEM((blk, blk), F32)]
        + ([pltpu.SemaphoreType.DMA((n, 3)), pltpu.SemaphoreType.DMA((n, 3))] if n else []),
        compiler_params=_params(2))(q, k, v, *shards)
    return outs[0], outs[1], list(outs[2:])


def attention_out_bwd(name, dh, w_o, o, swap=()):
    t, d = dh.shape
    n = w_o.shape[0]
    blk = ATT_BLOCK
    m = len(swap)

    def body(dh_ref, w_ref, o_ref, *refs):
        do_ref, d_ref = refs[m:m + 2]
        if m:
            _ride_along(swap_copies(refs[:m], refs[m + 2:2 * m + 2], *refs[2 * m + 2:]), (pl.program_id(0),), (t // blk,))
        do_ref[...] = lax.dot_general(dh_ref[...].astype(BF16), w_ref[...], _NT, preferred_element_type=F32)
        for h in range(N_HEADS):
            cols = slice(h * VDIM, (h + 1) * VDIM)
            d_ref[h] = jnp.sum((do_ref[:, cols] * o_ref[:, cols]).T, axis=0, keepdims=True)

    tile = pl.BlockSpec((blk, n), lambda i: (i, 0))
    outs = pl.pallas_call(
        body, name=name, grid=(t // blk,),
        in_specs=[pl.BlockSpec((blk, d), lambda i: (i, 0)), _resident((n, d), lambda i: (0, 0)), tile] + [ANY] * m,
        out_specs=[tile, pl.BlockSpec((N_HEADS, None, 1, blk), lambda i: (0, i, 0, 0))] + [ANY] * m,
        out_shape=[S((t, n), F32), S((N_HEADS, t // blk, 1, blk), F32)] + _swap_shapes(swap),
        scratch_shapes=[pltpu.SemaphoreType.DMA((m,)), pltpu.SemaphoreType.DMA((m,))] if m else [],
        compiler_params=_params(1))(dh, w_o, o, *swap)
    return outs[0], outs[1], list(outs[2:])


def attention_bwd(name, q, k, v, do, lse, delta, parts=()):
    nh, t, _ = q.shape
    blk = ATT_BLOCK
    nq = t // blk
    n_pairs = nq * (nq + 1) // 2
    n = len(parts)
    scale = SCORE_SCALE

    def body(q_ref, k_ref, v_ref, do_ref, lse_ref, dl_ref, *refs):
        src = refs[:n]
        dq_out, dk_out, dv_out = refs[n:n + 3]
        dst = refs[n + 3:2 * n + 3]
        s_buf, dp_buf, p_buf, ds_buf, bias_ref, dq_ref, dk_ref, dv_ref = refs[2 * n + 3:2 * n + 11]
        if n:
            send_sems, recv_sems = refs[2 * n + 11:]
            _ride_along(scatter_ici_copies(src, dst, send_sems, recv_sems), (pl.program_id(0),), (nh,))

        @pl.when(pl.program_id(0) == 0)
        def _():
            bias_ref[...] = jnp.where(_chunk_mask_t(0, 0, blk, blk), 0.0, MASK_VALUE)

        dq_ref[...] = jnp.zeros_like(dq_ref)
        dk_ref[...] = jnp.zeros_like(dk_ref)
        dv_ref[...] = jnp.zeros_like(dv_ref)

        def rows(x):
            return pl.ds(pl.multiple_of(x * blk, blk), blk)

        def after(jb):
            j, b = jb
            wrap = b == nq - 1 - j
            return jnp.where(wrap, j + 1, j), jnp.where(wrap, 0, b + 1)

        def products(jb, slot):
            j, b = jb
            s_buf[slot] = lax.dot_general(k_ref[rows(j), :], q_ref[rows(j + b), :], _NT, preferred_element_type=F32)
            dp_buf[slot] = lax.dot_general(v_ref[rows(j), :], do_ref[rows(j + b), :].astype(BF16), _NT, preferred_element_type=F32)

        def softmax_bwd(jb, slot):
            j, b = jb
            s = s_buf[slot] + bias_ref[...] * (b == 0).astype(F32)
            p = jnp.exp2(s * SCORE_SCALE_LOG2 - lse_ref[j + b] * LOG2_E)
            p_buf[slot] = p.astype(BF16)
            ds_buf[slot] = (p * (dp_buf[slot] - dl_ref[j + b]) * scale).astype(BF16)

        def gradients(jb, slot):
            j, b = jb
            dv_ref[rows(j), :] += jnp.dot(p_buf[slot], do_ref[rows(j + b), :].astype(BF16), preferred_element_type=F32)
            dk_ref[rows(j), :] += jnp.dot(ds_buf[slot], q_ref[rows(j + b), :], preferred_element_type=F32)
            dq_ref[rows(j + b), :] += lax.dot_general(ds_buf[slot], k_ref[rows(j), :], _TN, preferred_element_type=F32)

        def step(state, slot):
            third, second, first = state
            gradients(third, slot)
            softmax_bwd(second, 1 - slot)
            products(first, slot)
            return second, first, after(first)

        zero = jnp.int32(0)
        pair0 = (zero, zero)
        products(pair0, 0)
        if n_pairs == 1:
            softmax_bwd(pair0, 0)
            gradients(pair0, 0)
        else:
            pair1 = after(pair0)
            products(pair1, 1)
            softmax_bwd(pair0, 0)
            steady = n_pairs - 2
            state = lax.fori_loop(0, steady // 2, lambda u, st: step(step(st, 0), 1), (pair0, pair1, after(pair1)))
            if steady % 2:
                state = step(state, 0)
            before_last, last_pair, _ = state
            last = (n_pairs - 1) % 2
            softmax_bwd(last_pair, last)
            gradients(before_last, 1 - last)
            gradients(last_pair, last)
        dq_out[...] = dq_ref[...].astype(BF16)
        dk_out[...] = dk_ref[...].astype(BF16)
        dv_out[...] = dv_ref[...].astype(BF16)

    head = lambda w: pl.BlockSpec((None, t, w), lambda h: (h, 0, 0))
    stats = pl.BlockSpec((None, nq, 1, blk), lambda h: (h, 0, 0, 0))
    outs = pl.pallas_call(
        body, name=name, grid=(nh,),
        in_specs=[head(QK), head(QK), head(VDIM), pl.BlockSpec((t, VDIM), lambda h: (0, h)), stats, stats] + [ANY] * n,
        out_specs=[head(QK), head(QK), head(VDIM)] + [ANY] * n,
        out_shape=[S((nh, t, QK), BF16), S((nh, t, QK), BF16), S((nh, t, VDIM), BF16)] + [S(p.shape, p.dtype) for p in parts],
        scratch_shapes=[pltpu.VMEM((2, blk, blk), F32), pltpu.VMEM((2, blk, blk), F32), pltpu.VMEM((2, blk, blk), BF16),
                        pltpu.VMEM((2, blk, blk), BF16), pltpu.VMEM((blk, blk), F32),
                        pltpu.VMEM((t, QK), F32), pltpu.VMEM((t, QK), F32), pltpu.VMEM((t, VDIM), F32)]
        + ([pltpu.SemaphoreType.DMA((n, 3)), pltpu.SemaphoreType.DMA((n, 3))] if n else []),
        compiler_params=_params(1, VMEM_LIMIT_WHOLE_HEAD))(q, k, v, do, lse, delta, *parts)
    return outs[0], outs[1], outs[2], list(outs[3:])


def _shift_down(u, s):
    rows = lax.broadcasted_iota(jnp.int32, u.shape, 0)
    return jnp.where(rows >= s, pltpu.roll(u, s, 0), 0.0)


def _shift_up(u, s):
    n = u.shape[0]
    rows = lax.broadcasted_iota(jnp.int32, u.shape, 0)
    return jnp.where(rows < n - s, pltpu.roll(u, n - s, 0), 0.0)


def _conv_specs(t, d, lanes):
    slab = lambda part: pl.BlockSpec((None, t, lanes), lambda j, part=part: (part, 0, j))
    return slab, pl.BlockSpec((3, lanes), lambda j: (0, j)), pl.BlockSpec((t, lanes), lambda j: (0, j))


def conv_bwd(name, bcx, w, dy):
    _, t, d = bcx.shape
    lanes = _tile(d, 128, 128)
    slab, w_spec, col = _conv_specs(t, d, lanes)

    def body(b_ref, c_ref, x_ref, w_ref, dy_ref, d_ref, dw_ref):
        c = c_ref[...].astype(F32)
        x = x_ref[...].astype(F32)
        dyv = dy_ref[...]
        u = c * x
        u1 = _shift_down(u, 1)
        u2 = _shift_down(u, 2)
        w0, w1, w2 = w_ref[0:1, :], w_ref[1:2, :], w_ref[2:3, :]
        d_ref[0] = (dyv * (w0 * u2 + w1 * u1 + w2 * u)).astype(BF16)
        duc = dyv * b_ref[...].astype(F32)
        dw_ref[0:1, :] = jnp.sum(duc * u2, axis=0, keepdims=True)
        dw_ref[1:2, :] = jnp.sum(duc * u1, axis=0, keepdims=True)
        dw_ref[2:3, :] = jnp.sum(duc * u, axis=0, keepdims=True)
        du = w2 * duc + w1 * _shift_up(duc, 1) + w0 * _shift_up(duc, 2)
        d_ref[1] = (du * x).astype(BF16)
        d_ref[2] = (du * c).astype(BF16)

    return pl.pallas_call(
        body, name=name, grid=(d // lanes,), in_specs=[slab(0), slab(1), slab(2), w_spec, col],
        out_specs=[pl.BlockSpec((3, t, lanes), lambda j: (0, 0, j)), w_spec], out_shape=[S((3, t, d), BF16), S((3, d), F32)],
        compiler_params=_params(1))(bcx, bcx, bcx, w, dy)


def _adamw_update(w, g, m, v):
    m_new = ADAM_B1 * m + (1.0 - ADAM_B1) * g
    v_new = ADAM_B2 * v + (1.0 - ADAM_B2) * (g * g)
    m_hat = m_new / (1.0 - ADAM_B1 ** ADAM_STEP)
    v_hat = v_new / (1.0 - ADAM_B2 ** ADAM_STEP)
    return -ADAM_LR * (m_hat / (jnp.sqrt(v_hat) + ADAM_EPS) + ADAM_WD * w), m_new, v_new


def adamw(name, w, g, m, v):
    r, c = w.shape
    tr = _tile(r, 512)

    def body(w_ref, g_ref, m_ref, v_ref, d_ref, mo_ref, vo_ref):
        d_ref[...], mo_ref[...], vo_ref[...] = _adamw_update(w_ref[...], g_ref[...], m_ref[...], v_ref[...])

    blk = pl.BlockSpec((tr, c), lambda i: (i, 0))
    return pl.pallas_call(
        body, name=name, grid=(r // tr,), in_specs=[blk] * 4, out_specs=[blk] * 3, out_shape=[S((r, c), F32)] * 3,
        compiler_params=_params(1))(w, g, m, v)


def adamw_swapped(name, wt, g, mt, vt):
    nl, c, r = wt.shape
    tr = _tile(r, 512, 128)
    nr = r // tr

    def body(w_ref, g_ref, m_ref, v_ref, go_ref, d_ref, mo_ref, vo_ref):
        gt = g_ref[...].T
        go_ref[...] = gt
        d_ref[...], mo_ref[...], vo_ref[...] = _adamw_update(w_ref[...], gt, m_ref[...], v_ref[...])

    swapped = pl.BlockSpec((None, c, tr), lambda l, i: (l, 0, i))
    return pl.pallas_call(
        body, name=name, grid=(nl, nr),
        in_specs=[swapped, pl.BlockSpec((tr, c), lambda l, i: (l * nr + i, 0)), swapped, swapped],
        out_specs=[swapped] * 4, out_shape=[S((nl, c, r), F32)] * 4, compiler_params=_params(2))(wt, g, mt, vt)


def _place():
    x, y, c = lax.axis_index("x"), lax.axis_index("y"), lax.axis_index("c")
    other_chips = [(1 - x, y), (x, 1 - y), (1 - x, 1 - y)]
    return x, y, c, other_chips


def _half(c, rows):
    return pl.ds(pl.multiple_of(c * (rows // 2), 16), rows // 2)


def gather_weight_shards(shards, small, h, gain):
    n, ns = len(shards), len(small)
    t, d = h.shape
    tr = _tile(t, 512)
    steps = t // tr

    def body(h_ref, g_ref, *refs):
        src = refs[:n]
        small_src = refs[n:n + ns]
        a_ref = refs[n + ns]
        dst = refs[n + ns + 1:2 * n + ns + 1]
        small_dst = refs[2 * n + ns + 1:2 * (n + ns) + 1]
        send_sems, recv_sems, small_send, small_recv, local_sems = refs[2 * (n + ns) + 1:2 * (n + ns) + 6]
        stages = refs[2 * (n + ns) + 6:]
        x, y, c, chips = _place()
        me = 2 * x + y
        sibling = (x, y, 1 - c)
        own = [(s_ref, stages[i], d_ref.at[me], local_sems.at[i])
               for i, (s_ref, d_ref) in enumerate(zip(list(src) + list(small_src), list(dst) + list(small_dst)))]
        small_pairs = []
        for i in range(ns):
            for j, (px, py) in enumerate(chips):
                def whole(slot):
                    return pltpu.make_async_remote_copy(
                        src_ref=small_src[i], dst_ref=small_dst[i].at[slot], send_sem=small_send.at[i, j],
                        recv_sem=small_recv.at[i, j], device_id=(px, py, c), device_id_type=MESH)
                small_pairs.append((whole(me), whole(2 * px + py)))

        def copy(i, slot, half_of, sem, to, from_input=False):
            rows = _half(half_of, src[i].shape[0])
            return pltpu.make_async_remote_copy(
                src_ref=src[i].at[rows] if from_input else dst[i].at[slot, rows], dst_ref=dst[i].at[slot, rows],
                send_sem=send_sems.at[i, sem], recv_sem=recv_sems.at[i, sem], device_id=to, device_id_type=MESH)

        over_ici = [copy(i, me, c, j, (*chip, c), from_input=True) for i in range(n) for j, chip in enumerate(chips)]
        handed_on = [copy(i, 2 * px + py, c, 3 + j, sibling) for i in range(n) for j, (px, py) in enumerate(chips)]

        @pl.when(pl.program_id(0) == 0)
        def _():
            for s_ref, stage, _, sems in own:
                pltpu.make_async_copy(s_ref, stage, sems.at[0]).start()
            for outgoing, _ in small_pairs:
                outgoing.start()
            for cp in over_ici:
                cp.start()

        @pl.when(pl.program_id(0) == steps - 1)
        def _():
            k = 0
            for i in range(n):
                for j, (px, py) in enumerate(chips):
                    copy(i, 2 * px + py, c, j, sibling).wait_recv()
                    handed_on[k].start()
                    k += 1
            for i in range(n):
                for j, (px, py) in enumerate(chips):
                    copy(i, 2 * px + py, 1 - c, 3 + j, sibling).wait_recv()
            for cp in over_ici + handed_on:
                cp.wait_send()
            for _, incoming in small_pairs:
                incoming.wait_recv()
            for outgoing, _ in small_pairs:
                outgoing.wait_send()
            _place_locally(own)

        xv = h_ref[...]
        a_ref[...] = (xv * _rstd(xv) * g_ref[...]).astype(BF16)

    everything = list(shards) + list(small)
    outs = pl.pallas_call(
        body, name="gather_weight_shards", grid=(steps,),
        in_specs=[pl.BlockSpec((tr, d), lambda i: (i, 0)), pl.BlockSpec((1, d), lambda i: (0, 0))] + [ANY] * (n + ns),
        out_specs=[pl.BlockSpec((tr, d), lambda i: (i, 0))] + [ANY] * (n + ns),
        out_shape=[S((t, d), BF16)] + [S((N_CHIPS,) + s.shape, s.dtype) for s in everything],
        scratch_shapes=[pltpu.SemaphoreType.DMA((n, 6)), pltpu.SemaphoreType.DMA((n, 6)),
                        pltpu.SemaphoreType.DMA((max(ns, 1), 3)), pltpu.SemaphoreType.DMA((max(ns, 1), 3)),
                        pltpu.SemaphoreType.DMA((n + ns, 2))] + [pltpu.VMEM(s.shape, s.dtype) for s in everything],
        compiler_params=_params(1))(h, gain, *shards, *small)
    return outs[0], list(outs[1:n + 1]), list(outs[n + 1:])


def gather_ici_copies(src, dst, send_sems, recv_sems):
    x, y, c, chips = _place()
    me = 2 * x + y
    pairs = []
    for i in range(len(src)):
        rows = _half(c, src[i].shape[0])
        for j, (px, py) in enumerate(chips):
            def copy(slot):
                return pltpu.make_async_remote_copy(
                    src_ref=src[i].at[rows], dst_ref=dst[i].at[slot, rows], send_sem=send_sems.at[i, j],
                    recv_sem=recv_sems.at[i, j], device_id=(px, py, c), device_id_type=MESH)
            pairs.append((copy(me), copy(2 * px + py)))
    return pairs


def scatter_ici_copies(src, dst, send_sems, recv_sems):
    x, y, c, chips = _place()
    me = 2 * x + y
    pairs = []
    for i in range(len(src)):
        for j, (px, py) in enumerate(chips):
            def copy(from_slot, to_slot):
                return pltpu.make_async_remote_copy(
                    src_ref=src[i].at[from_slot], dst_ref=dst[i].at[to_slot], send_sem=send_sems.at[i, j],
                    recv_sem=recv_sems.at[i, j], device_id=(px, py, c), device_id_type=MESH)
            pairs.append((copy(2 * px + py, me), copy(me, 2 * px + py)))
    return pairs


def _ride_along(pairs, grid_ids, grid_sizes, local=()):
    first = grid_ids[0] == 0
    last = grid_ids[0] == grid_sizes[0] - 1
    for g, size in zip(grid_ids[1:], grid_sizes[1:]):
        first = first & (g == 0)
        last = last & (g == size - 1)

    @pl.when(first)
    def _():
        for outgoing, _ in pairs:
            outgoing.start()
        for src, stage, _, sems in local:
            pltpu.make_async_copy(src, stage, sems.at[0]).start()

    @pl.when(last)
    def _():
        for _, incoming in pairs:
            incoming.wait_recv()
        for outgoing, _ in pairs:
            outgoing.wait_send()
        _place_locally(local)


def _place_locally(local):
    for src, stage, _, sems in local:
        pltpu.make_async_copy(src, stage, sems.at[0]).wait()
    placed = [pltpu.make_async_copy(stage, dst, sems.at[1]) for _, stage, dst, sems in local]
    for cp in placed:
        cp.start()
    for cp in placed:
        cp.wait()


def forward_copies(src, dst, send_sems, recv_sems):
    x, y, c, chips = _place()
    pairs = []
    for i in range(len(src)):
        for j, (px, py) in enumerate(chips):
            def copy(half_of):
                rows = _half(half_of, src[i].shape[1])
                return pltpu.make_async_remote_copy(
                    src_ref=src[i].at[2 * px + py, rows], dst_ref=dst[i].at[2 * px + py, rows], send_sem=send_sems.at[i, j],
                    recv_sem=recv_sems.at[i, j], device_id=(x, y, 1 - c), device_id_type=MESH)
            pairs.append((copy(c), copy(1 - c)))
    return pairs


def attention_out_proj(name, attn, w_o, resid, next_gain, arriving, own):
    t, kdim = attn.shape
    n = w_o.shape[1]
    tm = _tile(t, 512)
    m = len(arriving)

    def body(x_ref, w_ref, r_ref, g_ref, *refs):
        src, own_refs = refs[:m], refs[m:2 * m]
        h_ref, a_ref = refs[2 * m:2 * m + 2]
        dst = refs[2 * m + 2:3 * m + 2]
        stages = refs[3 * m + 2:4 * m + 2]
        send_sems, recv_sems, local_sems = refs[4 * m + 2:]
        me = 2 * lax.axis_index("x") + lax.axis_index("y")
        placed = [(own_refs[i], stages[i], dst[i].at[me], local_sems.at[i]) for i in range(m)]
        _ride_along(forward_copies(src, dst, send_sems, recv_sems), (pl.program_id(0),), (t // tm,), placed)
        y = r_ref[...] + jnp.dot(x_ref[...].astype(BF16), w_ref[...], preferred_element_type=F32)
        h_ref[...] = y
        a_ref[...] = (y * _rstd(y) * g_ref[...]).astype(BF16)

    row = pl.BlockSpec((tm, n), lambda i: (i, 0))
    outs = pl.pallas_call(
        body, name=name, grid=(t // tm,),
        in_specs=[pl.BlockSpec((tm, kdim), lambda i: (i, 0)), _resident((kdim, n), lambda i: (0, 0)), row,
                  pl.BlockSpec((1, n), lambda i: (0, 0))] + [ANY] * (2 * m),
        out_specs=[row, row] + [ANY] * m, out_shape=[S((t, n), F32), S((t, n), BF16)] + [S(g.shape, g.dtype) for g in arriving],
        input_output_aliases={4 + i: 2 + i for i in range(m)},
        scratch_shapes=[pltpu.VMEM(o.shape, o.dtype) for o in own]
        + [pltpu.SemaphoreType.DMA((m, 3)), pltpu.SemaphoreType.DMA((m, 3)), pltpu.SemaphoreType.DMA((m, 2))],
        compiler_params=_params(1))(attn, w_o, resid, next_gain, *arriving, *own)
    return outs[0], outs[1], list(outs[2:])


def swap_copies(src, dst, send_sems, recv_sems):
    x, y, c, _ = _place()
    pairs = []
    for i in range(len(src)):
        cp = pltpu.make_async_remote_copy(
            src_ref=src[i].at[:, _half(1 - c, src[i].shape[1]), :], dst_ref=dst[i], send_sem=send_sems.at[i],
            recv_sem=recv_sems.at[i], device_id=(x, y, 1 - c), device_id_type=MESH)
        pairs.append((cp, cp))
    return pairs


def _swap_shapes(grads):
    return [S((g.shape[0], g.shape[1] // 2, g.shape[2]), g.dtype) for g in grads]


def sibling_swap_halves(name, grads):
    n = len(grads)

    def body(*refs):
        pairs = swap_copies(refs[:n], refs[n:2 * n], *refs[2 * n:])
        for outgoing, _ in pairs:
            outgoing.start()
        for _, incoming in pairs:
            incoming.wait_recv()
        for outgoing, _ in pairs:
            outgoing.wait_send()

    return pl.pallas_call(
        body, name=name, in_specs=[ANY] * n, out_specs=[ANY] * n, out_shape=_swap_shapes(grads),
        scratch_shapes=[pltpu.SemaphoreType.DMA((n,)), pltpu.SemaphoreType.DMA((n,))],
    )(*grads)


def add_halves(name, g, rx):
    _, r, cdim = g.shape
    r2 = r // 2
    tr = _tile(r2, 512, 16)
    nb = r2 // tr

    def body(core_ref, mine_ref, rx_ref, o_ref):
        o_ref[...] = (mine_ref[...].astype(F32) + rx_ref[...].astype(F32)).astype(BF16)

    core = lax.axis_index("c").astype(jnp.int32).reshape(1)
    half = pl.BlockSpec((None, tr, cdim), lambda k, i, core_ref: (k, i, 0))
    return pl.pallas_call(
        body, name=name, out_shape=S((N_CHIPS, r2, cdim), BF16),
        grid_spec=pltpu.PrefetchScalarGridSpec(
            num_scalar_prefetch=1, grid=(N_CHIPS, nb),
            in_specs=[pl.BlockSpec((None, tr, cdim), lambda k, i, core_ref: (k, core_ref[0] * nb + i, 0)), half],
            out_specs=half),
        compiler_params=_params(2))(core, g, rx)


def sum_chips(name, arrived, mine):
    _, r2, cdim = arrived.shape
    tr = _tile(r2, 512, 16)

    def body(chip_ref, a_ref, m_ref, o_ref):
        me = chip_ref[0]
        own = m_ref[...]
        acc = jnp.zeros((tr, cdim), F32)
        for k in range(N_CHIPS):
            acc = acc + jnp.where(me == k, own, a_ref[k]).astype(F32)
        o_ref[...] = acc

    chip = (2 * lax.axis_index("x") + lax.axis_index("y")).astype(jnp.int32).reshape(1)
    return pl.pallas_call(
        body, name=name, out_shape=S((r2, cdim), F32),
        grid_spec=pltpu.PrefetchScalarGridSpec(
            num_scalar_prefetch=1, grid=(r2 // tr,),
            in_specs=[pl.BlockSpec((N_CHIPS, tr, cdim), lambda i, chip_ref: (0, i, 0)),
                      pl.BlockSpec((None, tr, cdim), lambda i, chip_ref: (chip_ref[0], i, 0))],
            out_specs=pl.BlockSpec((tr, cdim), lambda i, chip_ref: (i, 0))),
        compiler_params=_params(1))(chip, arrived, mine)


def join_copies(src, dst, where, send_sems, recv_sems):
    x, y, c, _ = _place()
    pairs = []
    for i in range(len(src)):
        def copy(half_of):
            r2 = src[i].shape[0]
            rows = pl.ds(pl.multiple_of(where[i][1] + half_of * r2, 8), r2)
            return pltpu.make_async_remote_copy(
                src_ref=src[i], dst_ref=dst[where[i][0]].at[rows], send_sem=send_sems.at[i],
                recv_sem=recv_sems.at[i], device_id=(x, y, 1 - c), device_id_type=MESH)
        pairs.append((copy(c), copy(1 - c)))
    return pairs


def sibling_join_halves(name, halves, targets, where):
    n = len(halves)

    def body(*refs):
        src, dst = refs[:n], refs[n:n + len(targets)]
        send_sems, recv_sems, local_sems = refs[n + len(targets):n + len(targets) + 3]
        stages = refs[n + len(targets) + 3:]
        c = lax.axis_index("c")
        own = [(src[i], stages[i],
                dst[where[i][0]].at[pl.ds(pl.multiple_of(where[i][1] + c * src[i].shape[0], 8), src[i].shape[0])],
                local_sems.at[i]) for i in range(n)]
        for s_ref, stage, _, sems in own:
            pltpu.make_async_copy(s_ref, stage, sems.at[0]).start()
        pairs = join_copies(src, dst, where, send_sems, recv_sems)
        for outgoing, _ in pairs:
            outgoing.start()
        for _, incoming in pairs:
            incoming.wait_recv()
        for outgoing, _ in pairs:
            outgoing.wait_send()
        _place_locally(own)

    return list(pl.pallas_call(
        body, name=name, in_specs=[ANY] * n, out_specs=[ANY] * len(targets), out_shape=[S(tg, F32) for tg in targets],
        scratch_shapes=[pltpu.SemaphoreType.DMA((n,)), pltpu.SemaphoreType.DMA((n,)), pltpu.SemaphoreType.DMA((n, 2))]
        + [pltpu.VMEM(h.shape, h.dtype) for h in halves],
    )(*halves))


def all_reduce_small(name, packed):
    rows, width = packed.shape

    def body(x_ref, o_ref, gathered, send_sems, recv_sems):
        x, y, c, _ = _place()
        me = 4 * x + 2 * y + c
        gathered[me] = x_ref[...]
        flips = [(fx, fy, fc) for fx in (0, 1) for fy in (0, 1) for fc in (0, 1)][1:]

        def copy(r, slot, to):
            return pltpu.make_async_remote_copy(
                src_ref=x_ref, dst_ref=gathered.at[slot], send_sem=send_sems.at[r], recv_sem=recv_sems.at[r],
                device_id=to, device_id_type=MESH)

        def peer(f):
            return (x ^ f[0], y ^ f[1], c ^ f[2])

        sent = [copy(r, me, peer(f)) for r, f in enumerate(flips)]
        for cp in sent:
            cp.start()
        for r, f in enumerate(flips):
            px, py, pc = peer(f)
            copy(r, 4 * px + 2 * py + pc, peer(f)).wait_recv()
        for cp in sent:
            cp.wait_send()
        acc = gathered[0]
        for k in range(1, N_DEV):
            acc = acc + gathered[k]
        o_ref[...] = acc

    vmem = pl.BlockSpec(memory_space=pltpu.VMEM)
    return pl.pallas_call(
        body, name=name, in_specs=[vmem], out_specs=vmem, out_shape=S((rows, width), F32),
        scratch_shapes=[pltpu.VMEM((N_DEV, rows, width), F32), pltpu.SemaphoreType.DMA((N_DEV - 1,)),
                        pltpu.SemaphoreType.DMA((N_DEV - 1,))],
    )(packed)


def _rope_tables(positions):
    inv_freq = 1.0 / (ROPE_THETA ** (jnp.arange(0, ROPE, 2, dtype=F32) / ROPE))
    ang = positions.astype(F32)[:, None] * inv_freq
    return jnp.cos(ang), jnp.sin(ang)


def _unstack_cols(w):
    k4, k, n4 = w.shape
    return jnp.transpose(w, (1, 0, 2)).reshape(k, k4 * n4)


def _stack_cols(w):
    k, n = w.shape
    return jnp.transpose(w.reshape(k, N_CHIPS, n // N_CHIPS), (1, 0, 2))


def kernel(x, positions, mla_norm, mla_w_in, mla_g_cq, mla_g_ckv, mla_w_uq, mla_w_ukv, mla_w_o, conv_norm, conv_w_in, conv_w, conv_w_out, ffn_norm, ffn_w_gate, ffn_w_up, ffn_w_down, final_norm, loss_target, m_mla_norm, m_mla_w_in, m_mla_g_cq, m_mla_g_ckv, m_mla_w_uq, m_mla_w_ukv, m_mla_w_o, m_conv_norm, m_conv_w_in, m_conv_w, m_conv_w_out, m_ffn_norm, m_ffn_w_gate, m_ffn_w_up, m_ffn_w_down, m_final_norm, v_mla_norm, v_mla_w_in, v_mla_g_cq, v_mla_g_ckv, v_mla_w_uq, v_mla_w_ukv, v_mla_w_o, v_conv_norm, v_conv_w_in, v_conv_w, v_conv_w_out, v_ffn_norm, v_ffn_w_gate, v_ffn_w_up, v_ffn_w_down, v_final_norm):
    weights = dict(mla_norm=mla_norm, mla_w_in=mla_w_in, mla_g_cq=mla_g_cq, mla_g_ckv=mla_g_ckv, mla_w_uq=mla_w_uq,
                   mla_w_ukv=mla_w_ukv, mla_w_o=mla_w_o, conv_norm=conv_norm, conv_w_in=conv_w_in, conv_w=conv_w,
                   conv_w_out=conv_w_out, ffn_norm=ffn_norm, ffn_w_gate=ffn_w_gate, ffn_w_up=ffn_w_up,
                   ffn_w_down=ffn_w_down, final_norm=final_norm)
    m_in = dict(mla_norm=m_mla_norm, mla_w_in=m_mla_w_in, mla_g_cq=m_mla_g_cq, mla_g_ckv=m_mla_g_ckv, mla_w_uq=m_mla_w_uq,
                mla_w_ukv=m_mla_w_ukv, mla_w_o=m_mla_w_o, conv_norm=m_conv_norm, conv_w_in=m_conv_w_in, conv_w=m_conv_w,
                conv_w_out=m_conv_w_out, ffn_norm=m_ffn_norm, ffn_w_gate=m_ffn_w_gate, ffn_w_up=m_ffn_w_up,
                ffn_w_down=m_ffn_w_down, final_norm=m_final_norm)
    v_in = dict(mla_norm=v_mla_norm, mla_w_in=v_mla_w_in, mla_g_cq=v_mla_g_cq, mla_g_ckv=v_mla_g_ckv, mla_w_uq=v_mla_w_uq,
                mla_w_ukv=v_mla_w_ukv, mla_w_o=v_mla_w_o, conv_norm=v_conv_norm, conv_w_in=v_conv_w_in, conv_w=v_conv_w,
                conv_w_out=v_conv_w_out, ffn_norm=v_ffn_norm, ffn_w_gate=v_ffn_w_gate, ffn_w_up=v_ffn_w_up,
                ffn_w_down=v_ffn_w_down, final_norm=v_final_norm)
    big = ["mla_w_in", "mla_w_uq", "mla_w_ukv", "mla_w_o", "conv_w_in", "conv_w_out", "ffn_w_gate", "ffn_w_up", "ffn_w_down"]
    order = list(weights)

    t, d = x.shape[1], x.shape[2]
    h0 = x.reshape(t, d)
    target = loss_target.reshape(t, d)
    cos, sin = _rope_tables(positions.reshape(t))

    def rows2d(a):
        return a.reshape(-1, a.shape[-1])

    first, later = big[:4], big[4:]
    shards = {n: rows2d(weights[n]).astype(BF16) for n in big}
    d4 = d // N_CHIPS
    a0, first_gathered, (conv_norm_slots, conv_w_slots) = gather_weight_shards(
        [shards[n] for n in first], [conv_norm.reshape(1, d4), conv_w.reshape(3, d4)], h0, mla_norm)
    gathered = dict(zip(first, first_gathered))
    conv_norm_full = conv_norm_slots.reshape(1, d)
    conv_w_full = jnp.transpose(conv_w_slots, (1, 0, 2)).reshape(3, d)
    w_in = gathered["mla_w_in"].reshape(-1, gathered["mla_w_in"].shape[-1])
    w_uq = _unstack_cols(gathered["mla_w_uq"])
    w_ukv = _unstack_cols(gathered["mla_w_ukv"])
    w_o = gathered["mla_w_o"].reshape(-1, d)

    chip = 2 * lax.axis_index("x") + lax.axis_index("y")

    def pack_rows(rows):
        idx = lax.broadcasted_iota(jnp.int32, (SMALL_ROWS, d), 0)
        out = jnp.zeros((SMALL_ROWS, d), F32)
        for r, row in enumerate(rows):
            out = out + jnp.where(idx == r, row, 0.0)
        return out


    proj, cq, ckv, kr = mla_in_proj("mla_in_proj", a0, w_in, mla_g_cq, mla_g_ckv, cos, sin)
    qh, kh, vh, conv_arriving = qkv_heads("qkv_heads", cq, ckv, w_uq, w_ukv, kr, cos, sin, [shards[n] for n in later[:2]])
    attn, lse, ffn_arriving = attention_fwd("attention_fwd", qh, kh, vh, [shards[n] for n in later[2:]])
    h1, a1, handed = attention_out_proj("mla_out_proj", attn, w_o, h0, ffn_norm[0:1], conv_arriving + ffn_arriving,
                                        [shards[n] for n in later])
    gathered.update(zip(later, handed))
    cw_in = _unstack_cols(gathered["conv_w_in"])
    cw_out = gathered["conv_w_out"].reshape(-1, d)
    wg_all, wu_all, wd_all = gathered["ffn_w_gate"], gathered["ffn_w_up"], gathered["ffn_w_down"]

    def ffn_forward(tag, h, a, layer, next_gain):
        g, u, z = ffn_up(f"ffn{tag}_up", a, wg_all, wu_all, layer)
        return g, u, z, ffn_down(f"ffn{tag}_down", z, wd_all, layer, h, next_gain)

    g0, u0, z0, (h2, a2) = ffn_forward(0, h1, a1, 0, conv_norm_full)
    bcx, yc = conv_in_proj("conv_in_proj", a2, cw_in, conv_w_full)
    h3, a3 = linear("conv_out_proj", yc, cw_out, F32, resid=h2, next_gain=ffn_norm[1:2])
    g1, u1, z1 = ffn_up("ffn1_up", a3, wg_all, wu_all, 1)
    dh4, d_final_norm, loss_local = ffn_down_loss("ffn1_down_loss", z1, wd_all, 1, h3, final_norm.reshape(1, d), target)

    def ffn_backward(tag, dh, h, layer, a, g, u, z, swap=()):
        dg, du, swapped = ffn_bwd_hidden(f"ffn{tag}_bwd_hidden", dh, wd_all, layer, g, u, swap)
        d_wd = ffn_wgrad_down(f"ffn{tag}_wgrad_down", z, dh)
        dh_prev, d_norm = ffn_bwd_input(f"ffn{tag}_bwd_input", dg, du, wg_all, wu_all, layer, h, ffn_norm[layer:layer + 1], dh)
        d_wg = ffn_wgrad_up(f"ffn{tag}_wgrad_gate", a, dg)
        d_wu = ffn_wgrad_up(f"ffn{tag}_wgrad_up", a, du)
        return dh_prev, d_norm, [d_wg, d_wu, d_wd], swapped

    def pair_sums(tag, local, from_sibling):
        return [add_halves(f"pair_sum_{tag}{i}", g, r) for i, (g, r) in enumerate(zip(local, from_sibling))]

    def sum_from_chips(tag, pairs, arrived):
        return [sum_chips(f"chip_sum_{tag}{i}", a, p) for i, (a, p) in enumerate(zip(arrived, pairs))]

    def shard_shape(n):
        return rows2d(weights[n]).shape

    dh3, d_ffn_norm1, ffn1_grads, _ = ffn_backward(1, dh4, h3, 1, a3, g1, u1, z1)

    dyc = linear_nt("conv_out_bwd_input", dh3, cw_out, F32)
    d_cw_out = wgrad("conv_out_wgrad", yc, dh3)
    dbcx, d_conv_w = conv_bwd("conv_bwd", bcx, conv_w_full, dyc)
    dh2, d_conv_norm = conv_in_bwd_input("conv_in_bwd_input", dbcx, cw_in, h2, conv_norm_full, dh3)
    d_cw_in = conv_in_wgrad("conv_in_wgrad", a2, dbcx)

    second = [d_cw_in, d_cw_out.reshape(N_CHIPS, -1, d)] + ffn1_grads
    dh1, d_ffn_norm0, ffn0_grads, second_swapped = ffn_backward(0, dh2, h1, 0, a1, g0, u0, z0, second)
    d_w_o = wgrad("mla_out_wgrad", attn, dh1)
    first_part = ffn0_grads + [d_w_o.reshape(N_CHIPS, -1, d)]
    d_attn, delta, first_swapped = attention_out_bwd("mla_out_bwd_input", dh1, w_o, attn, first_part)
    rest_pairs = pair_sums("rest", second + first_part, second_swapped + first_swapped)
    dqh, dkh, dvh, rest_arrived = attention_bwd("attention_bwd", qh, kh, vh, d_attn, lse, delta, rest_pairs)
    rd, rf = ffn0_grads[0].shape[1], ffn0_grads[2].shape[1]
    rest_where = [(0, 0), (1, 0), (2, rd), (3, rd), (4, rf), (2, 0), (3, 0), (4, 0), (5, 0)]
    rest_names = later + ["mla_w_o"]
    dq, dkv, dproj, d_g_cq, d_g_ckv, rest_grads = qkv_heads_bwd(
        "qkv_heads_bwd", dqh, dkh, dvh, cos, sin, w_uq, w_ukv, proj, mla_g_cq, mla_g_ckv,
        sum_from_chips("rest", rest_pairs, rest_arrived), [shard_shape(n) for n in rest_names], rest_where)
    grads = dict(zip(rest_names, rest_grads))
    d_w_uq = wgrad("mla_q_up_wgrad", cq, dq)
    d_w_ukv = wgrad("mla_kv_up_wgrad", ckv, dkv)
    d_w_in = wgrad("mla_in_wgrad", a0, dproj)
    mla_local = [d_w_in.reshape(N_CHIPS, -1, d_w_in.shape[-1]), _stack_cols(d_w_uq), _stack_cols(d_w_ukv)]
    mla_pairs = pair_sums("mla", mla_local, sibling_swap_halves("sibling_swap_mla", mla_local))
    grad_x, d_mla_norm, mla_arrived = linear_nt_norm_bwd("mla_in_bwd_input", dproj, w_in, h0, mla_norm, dh1, mla_pairs)

    grads.update(zip(first[:3], sibling_join_halves("sibling_join_mla", sum_from_chips("mla", mla_pairs, mla_arrived),
                                                    [shard_shape(n) for n in first[:3]], [(i, 0) for i in range(3)])))

    def pad_row(v):
        return jnp.pad(v, ((0, 0), (0, d - v.shape[1])))

    small = all_reduce_small("all_reduce_small_grads", pack_rows([
        d_mla_norm, pad_row(d_g_cq), pad_row(d_g_ckv), d_ffn_norm0, d_ffn_norm1, d_final_norm, d_conv_norm,
        d_conv_w[0:1], d_conv_w[1:2], d_conv_w[2:3], jnp.broadcast_to(loss_local, (1, d))]))
    loss = small[10, 0]
    grads["mla_norm"] = small[0:1]
    grads["mla_g_cq"] = small[1:2, :mla_g_cq.shape[1]]
    grads["mla_g_ckv"] = small[2:3, :mla_g_ckv.shape[1]]
    grads["ffn_norm"] = small[3:5]
    grads["final_norm"] = small[5:6]
    grads["conv_norm"] = lax.dynamic_slice(small[6:7], (0, chip * d4), (1, d4))
    grads["conv_w"] = lax.dynamic_slice(small[7:10], (0, chip * d4), (3, d4))

    outs_g, outs_d, outs_m, outs_v = [], [], [], []
    for n in order:
        w = weights[n]
        if w.ndim == 3 and w.shape[2] % 128 and w.shape[1] % 128 == 0:
            results = adamw_swapped(f"adamw_{n}", jnp.swapaxes(w, 1, 2), grads[n].reshape(-1, w.shape[2]),
                                    jnp.swapaxes(m_in[n], 1, 2), jnp.swapaxes(v_in[n], 1, 2))
            grad_w, delta_w, new_m, new_v = [jnp.swapaxes(o, 1, 2) for o in results]
        else:
            delta_w, new_m, new_v = adamw(f"adamw_{n}", rows2d(w), grads[n].reshape(rows2d(w).shape), rows2d(m_in[n]), rows2d(v_in[n]))
            grad_w = grads[n]
        outs_g.append(grad_w.reshape(w.shape))
        outs_d.append(delta_w.reshape(w.shape))
        outs_m.append(new_m.reshape(w.shape))
        outs_v.append(new_v.reshape(w.shape))
    return (loss, grad_x.reshape(x.shape), *outs_g, *outs_d, *outs_m, *outs_v)
```
